```python
import math
import jax, jax.numpy as jnp
from jax import lax
import numpy as np

D_MODEL = 1024
BATCH = 16
SEQ = 2048
DEPTH = 1

ATTN_Q_HEADS = 8
ATTN_KV_HEADS = 2
ATTN_HEAD_DIM = 64
WINDOW = 128
REL_BUCKETS = 32
REL_MAX_DIST = 128
DN_HEADS = 4
DN_HEAD_DIM = 128
DN_CONV = 4
DN_CHUNK = 64
D_FF = 2816
FFN_CONV = 3
RMS_EPS = 1e-6
L2_EPS = 1e-6
N_MOD = 6
NEG_INF = -1e30

ATTN_Q_DIM = ATTN_Q_HEADS * ATTN_HEAD_DIM
ATTN_KV_DIM = ATTN_KV_HEADS * ATTN_HEAD_DIM
DN_DIM = DN_HEADS * DN_HEAD_DIM
IN_SPLIT_SIZES = (ATTN_Q_DIM, ATTN_KV_DIM, ATTN_KV_DIM, 3 * DN_DIM, DN_DIM, DN_HEADS, DN_HEADS, D_MODEL, D_MODEL)
IN_DIM = sum(IN_SPLIT_SIZES)

kernel_name = 'hybrid_swa_gdn_convffn_block'


def rms_norm(x, w):
    xf = x.astype(jnp.float32)
    y = xf * lax.rsqrt(jnp.mean(xf * xf, axis=-1, keepdims=True) + RMS_EPS)
    return (y * w.astype(jnp.float32)).astype(x.dtype)


def l2_normalize(x):
    return x * lax.rsqrt(jnp.sum(x * x, axis=-1, keepdims=True) + L2_EPS)


def causal_depthwise_conv(x, w):
    k, ch = w.shape
    return lax.conv_general_dilated(x, w[:, None, :].astype(x.dtype), window_strides=(1,), padding=[(k - 1, 0)], dimension_numbers=('NWC', 'WIO', 'NWC'), feature_group_count=ch)


def t5_causal_bucket(dist):
    dist = jnp.maximum(dist, 0)
    max_exact = REL_BUCKETS // 2
    scaled = jnp.log(jnp.maximum(dist, 1).astype(jnp.float32) / max_exact) / math.log(REL_MAX_DIST / max_exact)
    large = max_exact + (scaled * (REL_BUCKETS - max_exact)).astype(jnp.int32)
    large = jnp.minimum(large, REL_BUCKETS - 1)
    return jnp.where(dist < max_exact, dist, large)


def sliding_window_gqa(q, k, v, sinks, rel_bias):
    b, s, _, hd = q.shape
    grp = ATTN_Q_HEADS // ATTN_KV_HEADS
    nb = s // WINDOW
    qb = q.reshape(b, nb, WINDOW, ATTN_KV_HEADS, grp, hd)

    def band(t):
        tb = t.reshape(b, nb, WINDOW, ATTN_KV_HEADS, hd)
        prev = jnp.concatenate([jnp.zeros_like(tb[:, :1]), tb[:, :-1]], axis=1)
        return jnp.concatenate([prev, tb], axis=2)

    kb, vb = band(k), band(v)
    scores = jnp.einsum('bnqhgd,bnkhd->bnhgqk', qb, kb, preferred_element_type=jnp.float32) * (hd ** -0.5)
    qi = jnp.arange(WINDOW)[:, None]
    kj = jnp.arange(2 * WINDOW)[None, :]
    dist = WINDOW + qi - kj
    bias = rel_bias[t5_causal_bucket(dist)]
    bias = jnp.transpose(bias, (2, 0, 1)).reshape(ATTN_KV_HEADS, grp, WINDOW, 2 * WINDOW).astype(jnp.float32)
    in_band = (dist >= 0) & (dist < WINDOW)
    key_pos = jnp.arange(nb)[:, None, None] * WINDOW - WINDOW + kj[None]
    mask = in_band[None] & (key_pos >= 0)
    scores = jnp.where(mask[None, :, None, None], scores + bias, NEG_INF)
    sink = sinks.astype(jnp.float32).reshape(1, 1, ATTN_KV_HEADS, grp, 1, 1)
    m = jnp.maximum(jnp.max(scores, axis=-1, keepdims=True), sink)
    p = jnp.exp(scores - m)
    probs = p / (jnp.sum(p, axis=-1, keepdims=True) + jnp.exp(sink - m))
    out = jnp.einsum('bnhgqk,bnkhd->bnqhgd', probs.astype(v.dtype), vb)
    return out.reshape(b, s, ATTN_Q_DIM)


def chunk_gated_delta_rule(q, k, v, g, beta):
    b, s, nh, dk = q.shape
    dv = v.shape[-1]
    n = s // DN_CHUNK

    def chunks(t):
        return jnp.swapaxes(t, 1, 2).reshape(b, nh, n, DN_CHUNK, t.shape[-1])

    q = chunks(q) * (dk ** -0.5)
    k = chunks(k)
    v = chunks(v)
    gc = jnp.cumsum(jnp.swapaxes(g, 1, 2).reshape(b, nh, n, DN_CHUNK), axis=-1)
    bt = jnp.swapaxes(beta, 1, 2).reshape(b, nh, n, DN_CHUNK)[..., None]
    k_beta = k * bt
    v_beta = v * bt
    incl = jnp.tril(jnp.ones((DN_CHUNK, DN_CHUNK), dtype=bool))
    strict = jnp.tril(jnp.ones((DN_CHUNK, DN_CHUNK), dtype=bool), k=-1)
    diff = gc[..., :, None] - gc[..., None, :]
    decay = jnp.where(incl, jnp.exp(jnp.where(incl, diff, 0.0)), 0.0)
    eg = jnp.exp(gc)[..., None]
    lower = jnp.where(strict, jnp.einsum('bhncd,bhnmd->bhncm', k_beta, k) * decay, 0.0)
    rhs = jnp.concatenate([v_beta, k_beta * eg], axis=-1)
    eye = jnp.eye(DN_CHUNK, dtype=jnp.float32)
    sol = lax.linalg.triangular_solve(eye + lower, rhs, left_side=True, lower=True)
    u, w = sol[..., :dv], sol[..., dv:]
    intra = jnp.where(incl, jnp.einsum('bhncd,bhnmd->bhncm', q, k) * decay, 0.0)
    q_dec = q * eg
    k_tail = k * jnp.exp(gc[..., -1:] - gc)[..., None]
    g_last = jnp.exp(gc[..., -1])

    def step(state, inp):
        q_i, k_i, u_i, w_i, a_i, gl_i = inp
        v_new = u_i - jnp.einsum('bhcd,bhde->bhce', w_i, state)
        o_i = jnp.einsum('bhcd,bhde->bhce', q_i, state) + jnp.einsum('bhcm,bhme->bhce', a_i, v_new)
        state = state * gl_i[..., None, None] + jnp.einsum('bhcd,bhce->bhde', k_i, v_new)
        return state, o_i

    xs = tuple(jnp.moveaxis(t, 2, 0) for t in (q_dec, k_tail, u, w, intra, g_last))
    state0 = jnp.zeros((b, nh, dk, dv), jnp.float32)
    _, o = lax.scan(step, state0, xs)
    o = jnp.moveaxis(o, 0, 2).reshape(b, nh, s, dv)
    return jnp.swapaxes(o, 1, 2)


def gated_deltanet(qkv, beta_logits, a_logits, z, conv_w, a_log, dt_bias, norm_w):
    b, s, _ = qkv.shape
    qkv = jax.nn.silu(causal_depthwise_conv(qkv, conv_w)).astype(jnp.float32)
    q, k, v = jnp.split(qkv, 3, axis=-1)
    q = l2_normalize(q.reshape(b, s, DN_HEADS, DN_HEAD_DIM))
    k = l2_normalize(k.reshape(b, s, DN_HEADS, DN_HEAD_DIM))
    v = v.reshape(b, s, DN_HEADS, DN_HEAD_DIM)
    beta = jax.nn.sigmoid(beta_logits.astype(jnp.float32))
    g = -jnp.exp(a_log.astype(jnp.float32)) * jax.nn.softplus(a_logits.astype(jnp.float32) + dt_bias.astype(jnp.float32))
    o = chunk_gated_delta_rule(q, k, v, g, beta)
    zf = z.astype(jnp.float32).reshape(b, s, DN_HEADS, DN_HEAD_DIM)
    o = rms_norm(o, norm_w) * jax.nn.silu(zf)
    return o.reshape(b, s, DN_DIM).astype(z.dtype)


def hybrid_mixer(h, w_in, dn_conv_w, dn_a_log, dn_dt_bias, dn_norm_w, attn_sinks, rel_bias, w_attn_branch, w_dn_branch, w_out):
    b, s, _ = h.shape
    proj = h @ w_in
    splits = np.cumsum(IN_SPLIT_SIZES)[:-1].tolist()
    aq, ak, av, dqkv, dz, dbeta, da, gate_a, gate_d = jnp.split(proj, splits, axis=-1)
    y_attn = sliding_window_gqa(aq.reshape(b, s, ATTN_Q_HEADS, ATTN_HEAD_DIM), ak.reshape(b, s, ATTN_KV_HEADS, ATTN_HEAD_DIM), av.reshape(b, s, ATTN_KV_HEADS, ATTN_HEAD_DIM), attn_sinks, rel_bias)
    y_dn = gated_deltanet(dqkv, dbeta, da, dz, dn_conv_w, dn_a_log, dn_dt_bias, dn_norm_w)
    merged = jax.nn.sigmoid(gate_a) * (y_attn @ w_attn_branch) + jax.nn.sigmoid(gate_d) * (y_dn @ w_dn_branch)
    return merged @ w_out


def conv_ffn(h, w_up, conv_w, w_down):
    u = causal_depthwise_conv(h @ w_up, conv_w)
    gate, val = jnp.split(u, 2, axis=-1)
    return (jax.nn.gelu(gate, approximate=True) * val) @ w_down


def _fwd_setup_inputs(seed: int = 0) -> dict:
    key = jax.random.key(seed)
    ks = jax.random.split(key, 22)
    f32 = jnp.float32
    L = DEPTH

    def normal(k, shape, scale):
        return jax.random.normal(k, shape, f32) * scale

    def gain(k, shape):
        return 1.0 + 0.05 * jax.random.normal(k, shape, f32)

    dt = jnp.exp(jax.random.uniform(ks[11], (L, DN_HEADS), f32, math.log(1e-3), math.log(1e-1)))
    return {
        'x': normal(ks[0], (BATCH, SEQ, D_MODEL), 1.0),
        'c': normal(ks[1], (BATCH, D_MODEL), 1.0),
        'ada_w': normal(ks[2], (L, D_MODEL, N_MOD * D_MODEL), 0.5 * D_MODEL ** -0.5),
        'ada_b': normal(ks[3], (L, N_MOD * D_MODEL), 0.02),
        'norm_mix_pre': gain(ks[4], (L, D_MODEL)),
        'norm_mix_post': gain(ks[5], (L, D_MODEL)),
        'norm_ffn_pre': gain(ks[6], (L, D_MODEL)),
        'norm_ffn_post': gain(ks[7], (L, D_MODEL)),
        'w_in': normal(ks[8], (L, D_MODEL, IN_DIM), D_MODEL ** -0.5),
        'dn_conv_w': normal(ks[9], (L, DN_CONV, 3 * DN_DIM), DN_CONV ** -0.5),
        'dn_a_log': jnp.log(jax.random.uniform(ks[10], (L, DN_HEADS), f32, 1.0, 16.0)),
        'dn_dt_bias': dt + jnp.log(-jnp.expm1(-dt)),
        'dn_norm_w': gain(ks[12], (L, DN_HEAD_DIM)),
        'attn_sinks': normal(ks[13], (L, ATTN_Q_HEADS), 1.0),
        'rel_bias': normal(ks[14], (REL_BUCKETS, ATTN_Q_HEADS), 0.5),
        'w_attn_branch': normal(ks[15], (L, ATTN_Q_DIM, D_MODEL), ATTN_Q_DIM ** -0.5),
        'w_dn_branch': normal(ks[16], (L, DN_DIM, D_MODEL), DN_DIM ** -0.5),
        'w_out': normal(ks[17], (L, D_MODEL, D_MODEL), D_MODEL ** -0.5),
        'ffn_w_up': normal(ks[18], (L, D_MODEL, 2 * D_FF), D_MODEL ** -0.5),
        'ffn_conv_w': normal(ks[19], (L, FFN_CONV, 2 * D_FF), FFN_CONV ** -0.5),
        'ffn_w_down': normal(ks[20], (L, D_FF, D_MODEL), D_FF ** -0.5),
    }


def _fwd_reference(x, c, ada_w, ada_b, norm_mix_pre, norm_mix_post, norm_ffn_pre, norm_ffn_post, w_in, dn_conv_w, dn_a_log, dn_dt_bias, dn_norm_w, attn_sinks, rel_bias, w_attn_branch, w_dn_branch, w_out, ffn_w_up, ffn_conv_w, ffn_w_down):
    h = x
    c_act = jax.nn.silu(c)
    for l in range(DEPTH):
        mod = c_act @ ada_w[l] + ada_b[l]
        sh1, sc1, g1, sh2, sc2, g2 = [m[:, None, :] for m in jnp.split(mod, N_MOD, axis=-1)]
        u = rms_norm(h, norm_mix_pre[l]) * (1.0 + sc1) + sh1
        y = hybrid_mixer(u, w_in[l], dn_conv_w[l], dn_a_log[l], dn_dt_bias[l], dn_norm_w[l], attn_sinks[l], rel_bias, w_attn_branch[l], w_dn_branch[l], w_out[l])
        h = h + g1 * rms_norm(y, norm_mix_post[l])
        u = rms_norm(h, norm_ffn_pre[l]) * (1.0 + sc2) + sh2
        y = conv_ffn(u, ffn_w_up[l], ffn_conv_w[l], ffn_w_down[l])
        h = h + g2 * rms_norm(y, norm_ffn_post[l])
    return h


import jax as _jax
import jax.numpy as _jnp

TWIN_FORMAT = 'train_step'
FWD_PARAMS = ['x', 'c', 'ada_w', 'ada_b', 'norm_mix_pre', 'norm_mix_post', 'norm_ffn_pre', 'norm_ffn_post', 'w_in', 'dn_conv_w', 'dn_a_log', 'dn_dt_bias', 'dn_norm_w', 'attn_sinks', 'rel_bias', 'w_attn_branch', 'w_dn_branch', 'w_out', 'ffn_w_up', 'ffn_conv_w', 'ffn_w_down']
TWIN_WEIGHTS = ['ada_w', 'ada_b', 'norm_mix_pre', 'norm_mix_post', 'norm_ffn_pre', 'norm_ffn_post', 'w_in', 'dn_conv_w', 'dn_a_log', 'dn_dt_bias', 'dn_norm_w', 'attn_sinks', 'rel_bias', 'w_attn_branch', 'w_dn_branch', 'w_out', 'ffn_w_up', 'ffn_conv_w', 'ffn_w_down']
TWIN_DIFF_INPUT = 'x'
TWIN_INPUTS = ['x', 'c', 'ada_w', 'ada_b', 'norm_mix_pre', 'norm_mix_post', 'norm_ffn_pre', 'norm_ffn_post', 'w_in', 'dn_conv_w', 'dn_a_log', 'dn_dt_bias', 'dn_norm_w', 'attn_sinks', 'rel_bias', 'w_attn_branch', 'w_dn_branch', 'w_out', 'ffn_w_up', 'ffn_conv_w', 'ffn_w_down', 'loss_target', 'm_ada_w', 'm_ada_b', 'm_norm_mix_pre', 'm_norm_mix_post', 'm_norm_ffn_pre', 'm_norm_ffn_post', 'm_w_in', 'm_dn_conv_w', 'm_dn_a_log', 'm_dn_dt_bias', 'm_dn_norm_w', 'm_attn_sinks', 'm_rel_bias', 'm_w_attn_branch', 'm_w_dn_branch', 'm_w_out', 'm_ffn_w_up', 'm_ffn_conv_w', 'm_ffn_w_down', 'v_ada_w', 'v_ada_b', 'v_norm_mix_pre', 'v_norm_mix_post', 'v_norm_ffn_pre', 'v_norm_ffn_post', 'v_w_in', 'v_dn_conv_w', 'v_dn_a_log', 'v_dn_dt_bias', 'v_dn_norm_w', 'v_attn_sinks', 'v_rel_bias', 'v_w_attn_branch', 'v_w_dn_branch', 'v_w_out', 'v_ffn_w_up', 'v_ffn_conv_w', 'v_ffn_w_down']
TWIN_OUTPUTS = ['loss', 'grad_x', 'grad_ada_w', 'grad_ada_b', 'grad_norm_mix_pre', 'grad_norm_mix_post', 'grad_norm_ffn_pre', 'grad_norm_ffn_post', 'grad_w_in', 'grad_dn_conv_w', 'grad_dn_a_log', 'grad_dn_dt_bias', 'grad_dn_norm_w', 'grad_attn_sinks', 'grad_rel_bias', 'grad_w_attn_branch', 'grad_w_dn_branch', 'grad_w_out', 'grad_ffn_w_up', 'grad_ffn_conv_w', 'grad_ffn_w_down', 'delta_ada_w', 'delta_ada_b', 'delta_norm_mix_pre', 'delta_norm_mix_post', 'delta_norm_ffn_pre', 'delta_norm_ffn_post', 'delta_w_in', 'delta_dn_conv_w', 'delta_dn_a_log', 'delta_dn_dt_bias', 'delta_dn_norm_w', 'delta_attn_sinks', 'delta_rel_bias', 'delta_w_attn_branch', 'delta_w_dn_branch', 'delta_w_out', 'delta_ffn_w_up', 'delta_ffn_conv_w', 'delta_ffn_w_down', 'new_m_ada_w', 'new_m_ada_b', 'new_m_norm_mix_pre', 'new_m_norm_mix_post', 'new_m_norm_ffn_pre', 'new_m_norm_ffn_post', 'new_m_w_in', 'new_m_dn_conv_w', 'new_m_dn_a_log', 'new_m_dn_dt_bias', 'new_m_dn_norm_w', 'new_m_attn_sinks', 'new_m_rel_bias', 'new_m_w_attn_branch', 'new_m_w_dn_branch', 'new_m_w_out', 'new_m_ffn_w_up', 'new_m_ffn_conv_w', 'new_m_ffn_w_down', 'new_v_ada_w', 'new_v_ada_b', 'new_v_norm_mix_pre', 'new_v_norm_mix_post', 'new_v_norm_ffn_pre', 'new_v_norm_ffn_post', 'new_v_w_in', 'new_v_dn_conv_w', 'new_v_dn_a_log', 'new_v_dn_dt_bias', 'new_v_dn_norm_w', 'new_v_attn_sinks', 'new_v_rel_bias', 'new_v_w_attn_branch', 'new_v_w_dn_branch', 'new_v_w_out', 'new_v_ffn_w_up', 'new_v_ffn_conv_w', 'new_v_ffn_w_down']
TWIN_LEAF_KINDS = {'loss': 'loss', 'grad_x': 'grad_x', 'grad_ada_w': 'grad_w', 'grad_ada_b': 'grad_w', 'grad_norm_mix_pre': 'grad_w', 'grad_norm_mix_post': 'grad_w', 'grad_norm_ffn_pre': 'grad_w', 'grad_norm_ffn_post': 'grad_w', 'grad_w_in': 'grad_w', 'grad_dn_conv_w': 'grad_w', 'grad_dn_a_log': 'grad_w', 'grad_dn_dt_bias': 'grad_w', 'grad_dn_norm_w': 'grad_w', 'grad_attn_sinks': 'grad_w', 'grad_rel_bias': 'grad_w', 'grad_w_attn_branch': 'grad_w', 'grad_w_dn_branch': 'grad_w', 'grad_w_out': 'grad_w', 'grad_ffn_w_up': 'grad_w', 'grad_ffn_conv_w': 'grad_w', 'grad_ffn_w_down': 'grad_w', 'delta_ada_w': 'delta_w', 'delta_ada_b': 'delta_w', 'delta_norm_mix_pre': 'delta_w', 'delta_norm_mix_post': 'delta_w', 'delta_norm_ffn_pre': 'delta_w', 'delta_norm_ffn_post': 'delta_w', 'delta_w_in': 'delta_w', 'delta_dn_conv_w': 'delta_w', 'delta_dn_a_log': 'delta_w', 'delta_dn_dt_bias': 'delta_w', 'delta_dn_norm_w': 'delta_w', 'delta_attn_sinks': 'delta_w', 'delta_rel_bias': 'delta_w', 'delta_w_attn_branch': 'delta_w', 'delta_w_dn_branch': 'delta_w', 'delta_w_out': 'delta_w', 'delta_ffn_w_up': 'delta_w', 'delta_ffn_conv_w': 'delta_w', 'delta_ffn_w_down': 'delta_w', 'new_m_ada_w': 'new_m', 'new_m_ada_b': 'new_m', 'new_m_norm_mix_pre': 'new_m', 'new_m_norm_mix_post': 'new_m', 'new_m_norm_ffn_pre': 'new_m', 'new_m_norm_ffn_post': 'new_m', 'new_m_w_in': 'new_m', 'new_m_dn_conv_w': 'new_m', 'new_m_dn_a_log': 'new_m', 'new_m_dn_dt_bias': 'new_m', 'new_m_dn_norm_w': 'new_m', 'new_m_attn_sinks': 'new_m', 'new_m_rel_bias': 'new_m', 'new_m_w_attn_branch': 'new_m', 'new_m_w_dn_branch': 'new_m', 'new_m_w_out': 'new_m', 'new_m_ffn_w_up': 'new_m', 'new_m_ffn_conv_w': 'new_m', 'new_m_ffn_w_down': 'new_m', 'new_v_ada_w': 'new_v', 'new_v_ada_b': 'new_v', 'new_v_norm_mix_pre': 'new_v', 'new_v_norm_mix_post': 'new_v', 'new_v_norm_ffn_pre': 'new_v', 'new_v_norm_ffn_post': 'new_v', 'new_v_w_in': 'new_v', 'new_v_dn_conv_w': 'new_v', 'new_v_dn_a_log': 'new_v', 'new_v_dn_dt_bias': 'new_v', 'new_v_dn_norm_w': 'new_v', 'new_v_attn_sinks': 'new_v', 'new_v_rel_bias': 'new_v', 'new_v_w_attn_branch': 'new_v', 'new_v_w_dn_branch': 'new_v', 'new_v_w_out': 'new_v', 'new_v_ffn_w_up': 'new_v', 'new_v_ffn_conv_w': 'new_v', 'new_v_ffn_w_down': 'new_v'}


def _forward(args):
    return _fwd_reference(*[args[k] for k in FWD_PARAMS])


def _output_shape():
    out = _jax.eval_shape(lambda: _forward(_fwd_setup_inputs(0)))
    return out.shape, out.dtype

N_MICROBATCH = 1
ADAM_LR = 0.001
ADAM_B1 = 0.9
ADAM_B2 = 0.999
ADAM_EPS = 1e-08
ADAM_WD = 0.01
ADAM_STEP = 10
PER_EXAMPLE_BATCH_AXIS = {'x': 0, 'c': 0, 'loss_target': 0}
SHARED_INPUTS = []
_WEIGHT_DTYPES = {'ada_w': _jnp.float32, 'ada_b': _jnp.float32, 'norm_mix_pre': _jnp.float32, 'norm_mix_post': _jnp.float32, 'norm_ffn_pre': _jnp.float32, 'norm_ffn_post': _jnp.float32, 'w_in': _jnp.float32, 'dn_conv_w': _jnp.float32, 'dn_a_log': _jnp.float32, 'dn_dt_bias': _jnp.float32, 'dn_norm_w': _jnp.float32, 'attn_sinks': _jnp.float32, 'rel_bias': _jnp.float32, 'w_attn_branch': _jnp.float32, 'w_dn_branch': _jnp.float32, 'w_out': _jnp.float32, 'ffn_w_up': _jnp.float32, 'ffn_conv_w': _jnp.float32, 'ffn_w_down': _jnp.float32}
MOMENT_SCALE = {'ada_w': 1.783465e+00, 'ada_b': 3.387329e+00, 'norm_mix_pre': 1.708033e-01, 'norm_mix_post': 3.782729e+00, 'norm_ffn_pre': 1.249307e-01, 'norm_ffn_post': 3.767581e+00, 'w_in': 1.399369e-01, 'dn_conv_w': 1.313807e-01, 'dn_a_log': 6.601497e-01, 'dn_dt_bias': 5.373962e-01, 'dn_norm_w': 5.803096e-01, 'attn_sinks': 4.145423e-02, 'rel_bias': 4.706788e-02, 'w_attn_branch': 2.942577e-01, 'w_dn_branch': 1.892448e-01, 'w_out': 3.472454e-01, 'ffn_w_up': 6.157185e-02, 'ffn_conv_w': 6.694005e-02, 'ffn_w_down': 1.233290e-01}


def _to_microbatches(a, axis):
    t = _jnp.moveaxis(a, axis, 0)
    t = t.reshape((N_MICROBATCH, t.shape[0] // N_MICROBATCH) + t.shape[1:])
    return _jnp.moveaxis(t, 1, axis + 1)


def setup_inputs(seed: int = 0) -> dict:
    inp = _fwd_setup_inputs(seed)
    key = _jax.random.fold_in(_jax.random.key(seed), 7919)
    shape, _ = _output_shape()
    out = dict(inp)
    out["loss_target"] = _jax.random.normal(_jax.random.fold_in(key, 0), shape, _jnp.float32)
    for i, name in enumerate(TWIN_WEIGHTS):
        w = inp[name].astype(_jnp.float32)
        if MOMENT_SCALE is None:
            s = _jnp.sqrt(_jnp.mean(_jnp.square(w)) + 1e-30)
        else:
            s = MOMENT_SCALE[name]
        km, kv = _jax.random.split(_jax.random.fold_in(key, i + 1))
        out[name] = w
        out["m_" + name] = s * _jax.random.normal(km, w.shape, _jnp.float32)
        out["v_" + name] = (s * s) * _jax.random.uniform(kv, w.shape, _jnp.float32, 0.5, 1.5)
    if N_MICROBATCH > 1:
        for name, axis in PER_EXAMPLE_BATCH_AXIS.items():
            out[name] = _to_microbatches(out[name], axis)
    return {'x': out['x'], 'c': out['c'], 'ada_w': out['ada_w'], 'ada_b': out['ada_b'], 'norm_mix_pre': out['norm_mix_pre'], 'norm_mix_post': out['norm_mix_post'], 'norm_ffn_pre': out['norm_ffn_pre'], 'norm_ffn_post': out['norm_ffn_post'], 'w_in': out['w_in'], 'dn_conv_w': out['dn_conv_w'], 'dn_a_log': out['dn_a_log'], 'dn_dt_bias': out['dn_dt_bias'], 'dn_norm_w': out['dn_norm_w'], 'attn_sinks': out['attn_sinks'], 'rel_bias': out['rel_bias'], 'w_attn_branch': out['w_attn_branch'], 'w_dn_branch': out['w_dn_branch'], 'w_out': out['w_out'], 'ffn_w_up': out['ffn_w_up'], 'ffn_conv_w': out['ffn_conv_w'], 'ffn_w_down': out['ffn_w_down'], 'loss_target': out['loss_target'], 'm_ada_w': out['m_ada_w'], 'm_ada_b': out['m_ada_b'], 'm_norm_mix_pre': out['m_norm_mix_pre'], 'm_norm_mix_post': out['m_norm_mix_post'], 'm_norm_ffn_pre': out['m_norm_ffn_pre'], 'm_norm_ffn_post': out['m_norm_ffn_post'], 'm_w_in': out['m_w_in'], 'm_dn_conv_w': out['m_dn_conv_w'], 'm_dn_a_log': out['m_dn_a_log'], 'm_dn_dt_bias': out['m_dn_dt_bias'], 'm_dn_norm_w': out['m_dn_norm_w'], 'm_attn_sinks': out['m_attn_sinks'], 'm_rel_bias': out['m_rel_bias'], 'm_w_attn_branch': out['m_w_attn_branch'], 'm_w_dn_branch': out['m_w_dn_branch'], 'm_w_out': out['m_w_out'], 'm_ffn_w_up': out['m_ffn_w_up'], 'm_ffn_conv_w': out['m_ffn_conv_w'], 'm_ffn_w_down': out['m_ffn_w_down'], 'v_ada_w': out['v_ada_w'], 'v_ada_b': out['v_ada_b'], 'v_norm_mix_pre': out['v_norm_mix_pre'], 'v_norm_mix_post': out['v_norm_mix_post'], 'v_norm_ffn_pre': out['v_norm_ffn_pre'], 'v_norm_ffn_post': out['v_norm_ffn_post'], 'v_w_in': out['v_w_in'], 'v_dn_conv_w': out['v_dn_conv_w'], 'v_dn_a_log': out['v_dn_a_log'], 'v_dn_dt_bias': out['v_dn_dt_bias'], 'v_dn_norm_w': out['v_dn_norm_w'], 'v_attn_sinks': out['v_attn_sinks'], 'v_rel_bias': out['v_rel_bias'], 'v_w_attn_branch': out['v_w_attn_branch'], 'v_w_dn_branch': out['v_w_dn_branch'], 'v_w_out': out['v_w_out'], 'v_ffn_w_up': out['v_ffn_w_up'], 'v_ffn_conv_w': out['v_ffn_conv_w'], 'v_ffn_w_down': out['v_ffn_w_down']}


def _loss(weights, diff, rest, loss_target):
    with _jax.named_scope("forward"):
        args = {**rest, TWIN_DIFF_INPUT: diff, **{k: w.astype(_WEIGHT_DTYPES[k]) for k, w in weights.items()}}
        y = _forward(args)
    with _jax.named_scope("loss_head"):
        err = _jnp.square(y.astype(_jnp.float32) - loss_target)
        return 0.5 * _jnp.sum(_jnp.mean(err, axis=-1)) if err.ndim else 0.5 * err


def _adamw(w, g, m, v):
    m = ADAM_B1 * m + (1.0 - ADAM_B1) * g
    v = ADAM_B2 * v + (1.0 - ADAM_B2) * _jnp.square(g)
    m_hat = m / (1.0 - ADAM_B1 ** ADAM_STEP)
    v_hat = v / (1.0 - ADAM_B2 ** ADAM_STEP)
    delta = -ADAM_LR * (m_hat / (_jnp.sqrt(v_hat) + ADAM_EPS) + ADAM_WD * w)
    return delta, m, v


def reference(x, c, ada_w, ada_b, norm_mix_pre, norm_mix_post, norm_ffn_pre, norm_ffn_post, w_in, dn_conv_w, dn_a_log, dn_dt_bias, dn_norm_w, attn_sinks, rel_bias, w_attn_branch, w_dn_branch, w_out, ffn_w_up, ffn_conv_w, ffn_w_down, loss_target, m_ada_w, m_ada_b, m_norm_mix_pre, m_norm_mix_post, m_norm_ffn_pre, m_norm_ffn_post, m_w_in, m_dn_conv_w, m_dn_a_log, m_dn_dt_bias, m_dn_norm_w, m_attn_sinks, m_rel_bias, m_w_attn_branch, m_w_dn_branch, m_w_out, m_ffn_w_up, m_ffn_conv_w, m_ffn_w_down, v_ada_w, v_ada_b, v_norm_mix_pre, v_norm_mix_post, v_norm_ffn_pre, v_norm_ffn_post, v_w_in, v_dn_conv_w, v_dn_a_log, v_dn_dt_bias, v_dn_norm_w, v_attn_sinks, v_rel_bias, v_w_attn_branch, v_w_dn_branch, v_w_out, v_ffn_w_up, v_ffn_conv_w, v_ffn_w_down):
    given = dict(x=x, c=c, ada_w=ada_w, ada_b=ada_b, norm_mix_pre=norm_mix_pre, norm_mix_post=norm_mix_post, norm_ffn_pre=norm_ffn_pre, norm_ffn_post=norm_ffn_post, w_in=w_in, dn_conv_w=dn_conv_w, dn_a_log=dn_a_log, dn_dt_bias=dn_dt_bias, dn_norm_w=dn_norm_w, attn_sinks=attn_sinks, rel_bias=rel_bias, w_attn_branch=w_attn_branch, w_dn_branch=w_dn_branch, w_out=w_out, ffn_w_up=ffn_w_up, ffn_conv_w=ffn_conv_w, ffn_w_down=ffn_w_down, loss_target=loss_target, m_ada_w=m_ada_w, m_ada_b=m_ada_b, m_norm_mix_pre=m_norm_mix_pre, m_norm_mix_post=m_norm_mix_post, m_norm_ffn_pre=m_norm_ffn_pre, m_norm_ffn_post=m_norm_ffn_post, m_w_in=m_w_in, m_dn_conv_w=m_dn_conv_w, m_dn_a_log=m_dn_a_log, m_dn_dt_bias=m_dn_dt_bias, m_dn_norm_w=m_dn_norm_w, m_attn_sinks=m_attn_sinks, m_rel_bias=m_rel_bias, m_w_attn_branch=m_w_attn_branch, m_w_dn_branch=m_w_dn_branch, m_w_out=m_w_out, m_ffn_w_up=m_ffn_w_up, m_ffn_conv_w=m_ffn_conv_w, m_ffn_w_down=m_ffn_w_down, v_ada_w=v_ada_w, v_ada_b=v_ada_b, v_norm_mix_pre=v_norm_mix_pre, v_norm_mix_post=v_norm_mix_post, v_norm_ffn_pre=v_norm_ffn_pre, v_norm_ffn_post=v_norm_ffn_post, v_w_in=v_w_in, v_dn_conv_w=v_dn_conv_w, v_dn_a_log=v_dn_a_log, v_dn_dt_bias=v_dn_dt_bias, v_dn_norm_w=v_dn_norm_w, v_attn_sinks=v_attn_sinks, v_rel_bias=v_rel_bias, v_w_attn_branch=v_w_attn_branch, v_w_dn_branch=v_w_dn_branch, v_w_out=v_w_out, v_ffn_w_up=v_ffn_w_up, v_ffn_conv_w=v_ffn_conv_w, v_ffn_w_down=v_ffn_w_down)
    weights = {n: given[n] for n in TWIN_WEIGHTS}
    shared = {n: given[n] for n in SHARED_INPUTS}
    per_example = {n: given[n] for n in ['x', 'c']}
    grad_fn = _jax.value_and_grad(_loss, argnums=(0, 1))

    def one_microbatch(ex, loss_target):
        ex = dict(ex)
        diff = ex.pop(TWIN_DIFF_INPUT)
        return grad_fn(weights, diff, {**shared, **ex}, loss_target)

    if N_MICROBATCH == 1:
        loss, (grad_w, grad_x) = one_microbatch(per_example, given["loss_target"])
    else:
        def body(carry, xs):
            loss_sum, grad_sum = carry
            l_k, (gw_k, gx_k) = one_microbatch(xs[0], xs[1])
            with _jax.named_scope("update"):
                return (loss_sum + l_k, _jax.tree.map(_jnp.add, grad_sum, gw_k)), gx_k

        init = (_jnp.zeros((), _jnp.float32), _jax.tree.map(_jnp.zeros_like, weights))
        (loss, grad_w), grad_x = _jax.lax.scan(body, init, (per_example, given["loss_target"]))
    with _jax.named_scope("update"):
        delta_w, new_m, new_v = {}, {}, {}
        for n in TWIN_WEIGHTS:
            delta_w[n], new_m[n], new_v[n] = _adamw(weights[n], grad_w[n], given["m_" + n], given["v_" + n])
    return (loss, grad_x, *[grad_w[n] for n in TWIN_WEIGHTS], *[delta_w[n] for n in TWIN_WEIGHTS],
            *[new_m[n] for n in TWIN_WEIGHTS], *[new_v[n] for n in TWIN_WEIGHTS])
```

```python
import functools
import math

import numpy as np
import jax
import jax.numpy as jnp
from jax import lax
from jax.experimental import pallas as pl
from jax.experimental.pallas import tpu as pltpu

f32 = jnp.float32
bf16 = jnp.bfloat16
HI = lax.Precision.HIGHEST
MESH = pl.DeviceIdType.MESH

NDEV = 8
D = 1024
HQ, HKV, HD, WIN, NBUCK, MAXDIST = 8, 2, 64, 128, 32, 128
DNH, DND, DNK, CH = 4, 128, 4, 64
DFF, FK = 2816, 3
NMOD = 6
RMS_EPS = 1e-6
L2_EPS = 1e-6
NEG_INF = -1e30
LR, B1, B2, EPS, WD, STEP = 0.001, 0.9, 0.999, 1e-08, 0.01, 10

LANE = 128
CB_GA, CB_GD, CB_AQ, CB_DQKV, CB_DZ, CB_AK, CB_AV, CB_BA, NPB = 0, 8, 16, 24, 36, 40, 42, 44, 45
NP = NPB * LANE
IN_SPLITS = (HQ * HD, HKV * HD, HKV * HD, 3 * DNH * DND, DNH * DND, DNH, DNH, D, D)
IN_DIM = sum(IN_SPLITS)
VMEM_LIMIT = 56 * 1024 * 1024

SMALL = (("ada_b", NMOD * D), ("norm_mix_pre", D), ("norm_mix_post", D), ("norm_ffn_pre", D), ("norm_ffn_post", D),
         ("dn_a_log", DNH), ("dn_dt_bias", DNH), ("dn_norm_w", DND), ("attn_sinks", HQ), ("rel_bias", NBUCK * HQ))
SMALL_N = sum(n for _, n in SMALL)
SMALL_PAD = 10752


def _cp(sem):
    return pltpu.CompilerParams(dimension_semantics=sem, vmem_limit_bytes=VMEM_LIMIT)


def _pick(dim, target):
    if dim <= target:
        return dim
    best = None
    for d in range(LANE, target + 1, LANE):
        if dim % d == 0:
            best = d
    assert best is not None, (dim, target)
    return best


def _me():
    x, y, c = lax.axis_index("x"), lax.axis_index("y"), lax.axis_index("c")
    return x, y, c, 4 * x + 2 * y + c


def _peer(x, y, c, k):
    px = 1 - x if k & 4 else x
    py = 1 - y if k & 2 else y
    pc = 1 - c if k & 1 else c
    return (px, py, pc), 4 * px + 2 * py + pc


def _exchange(arrs, name, scatter):
    n = len(arrs)
    if scatter:
        outs = [jax.ShapeDtypeStruct(a.shape, a.dtype) for a in arrs]
    else:
        outs = [jax.ShapeDtypeStruct((NDEV,) + a.shape, a.dtype) for a in arrs]

    def body(*refs):
        ins, out = refs[:n], refs[n:2 * n]
        send, recv, loc = refs[2 * n:]
        x, y, c, me = _me()
        copies = []
        for a in range(n):
            src_own = ins[a].at[me] if scatter else ins[a]
            lc = pltpu.make_async_copy(src_own, out[a].at[me], loc.at[a])
            lc.start()
            copies.append(lc)
        sends, recvs = [], []
        for a in range(n):
            for k in range(1, NDEV):
                peer, pid = _peer(x, y, c, k)
                s = a * (NDEV - 1) + k - 1
                src = ins[a].at[pid] if scatter else ins[a]
                cp = pltpu.make_async_remote_copy(src_ref=src, dst_ref=out[a].at[me], send_sem=send.at[s],
                                                  recv_sem=recv.at[s], device_id=peer, device_id_type=MESH)
                cp.start()
                sends.append(cp)
                recvs.append(pltpu.make_async_remote_copy(src_ref=src, dst_ref=out[a].at[pid], send_sem=send.at[s],
                                                          recv_sem=recv.at[s], device_id=peer, device_id_type=MESH))
        for cp in sends:
            cp.wait_send()
        for cp in recvs:
            cp.wait_recv()
        for lc in copies:
            lc.wait()

    any_spec = pl.BlockSpec(memory_space=pl.ANY)
    return pl.pallas_call(
        body, name=name, out_shape=outs, in_specs=[any_spec] * n, out_specs=[any_spec] * n,
        scratch_shapes=[pltpu.SemaphoreType.DMA((n * (NDEV - 1),)), pltpu.SemaphoreType.DMA((n * (NDEV - 1),)),
                        pltpu.SemaphoreType.DMA((n,))],
        compiler_params=pltpu.CompilerParams(has_side_effects=True),
    )(*arrs)


def mm(a, b, mode, out_dtype, name, tm=1024, tn=1024, tk=1024, precision=None):
    if mode == "nn":
        (M, K), (K2, N) = a.shape, b.shape
    elif mode == "nt":
        (M, K), (N, K2) = a.shape, b.shape
    else:
        (K, M), (K2, N) = a.shape, b.shape
    assert K == K2, (name, a.shape, b.shape)
    tm, tn, tk = _pick(M, tm), _pick(N, tn), _pick(K, tk)
    nk = K // tk
    if mode == "tn":
        a_spec = pl.BlockSpec((tk, tm), lambda i, j, k: (k, i))
    else:
        a_spec = pl.BlockSpec((tm, tk), lambda i, j, k: (i, k))
    if mode == "nt":
        b_spec = pl.BlockSpec((tn, tk), lambda i, j, k: (j, k))
    else:
        b_spec = pl.BlockSpec((tk, tn), lambda i, j, k: (k, j))
    dims = {"nn": ((1,), (0,)), "nt": ((1,), (1,)), "tn": ((0,), (0,))}[mode]

    def body(a_ref, b_ref, o_ref, *scr):
        p = lax.dot_general(a_ref[...], b_ref[...], (dims, ((), ())), preferred_element_type=f32, precision=precision)
        if nk == 1:
            o_ref[...] = p.astype(o_ref.dtype)
        else:
            acc = scr[0]
            k = pl.program_id(2)

            @pl.when(k == 0)
            def _():
                acc[...] = p

            @pl.when(k > 0)
            def _():
                acc[...] += p

            @pl.when(k == nk - 1)
            def _():
                o_ref[...] = acc[...].astype(o_ref.dtype)

    return pl.pallas_call(
        body, name=name, grid=(M // tm, N // tn, nk), in_specs=[a_spec, b_spec],
        out_specs=pl.BlockSpec((tm, tn), lambda i, j, k: (i, j)), out_shape=jax.ShapeDtypeStruct((M, N), out_dtype),
        scratch_shapes=[pltpu.VMEM((tm, tn), f32)] if nk > 1 else [],
        compiler_params=_cp(("parallel", "parallel", "arbitrary")),
    )(a, b)


def rowcall(name, fn, tok, bat, con, tok_out, acc_out, ts=256):
    B, S = tok[0][0].shape[:2]
    ts = min(ts, S)
    nt, nb, nc, no, na = len(tok), len(bat), len(con), len(tok_out), len(acc_out)

    def body(*refs):
        tr, br, cr = refs[:nt], refs[nt:nt + nb], refs[nt + nb:nt + nb + nc]
        orf, arf = refs[nt + nb + nc:nt + nb + nc + no], refs[nt + nb + nc + no:]
        touts, aouts = fn([r[0] for r in tr], [r[0] for r in br], [r[...] for r in cr])
        for r, v in zip(orf, touts):
            r[0] = v.astype(r.dtype)
        s = pl.program_id(1)
        for r, v in zip(arf, aouts):
            @pl.when(s == 0)
            def _(r=r):
                r[...] = jnp.zeros(r.shape, r.dtype)
            r[0] += v.astype(f32)

    in_specs = [pl.BlockSpec((1, ts, w), lambda b, s, cb=cb: (b, s, cb)) for (_, w, cb) in tok]
    in_specs += [pl.BlockSpec((1,) + a.shape[1:], lambda b, s: (b, 0, 0)) for a in bat]
    in_specs += [pl.BlockSpec(a.shape, lambda b, s, nd=a.ndim: (0,) * nd) for a in con]
    out_specs = [pl.BlockSpec((1, ts, w), lambda b, s: (b, s, 0)) for (w, _) in tok_out]
    out_specs += [pl.BlockSpec((1,) + shp, lambda b, s, nd=len(shp): (b,) + (0,) * nd) for shp in acc_out]
    out_shape = [jax.ShapeDtypeStruct((B, S, w), dt) for (w, dt) in tok_out]
    out_shape += [jax.ShapeDtypeStruct((B,) + shp, f32) for shp in acc_out]
    return pl.pallas_call(
        body, name=name, grid=(B, S // ts), in_specs=in_specs, out_specs=out_specs, out_shape=out_shape,
        compiler_params=_cp(("parallel", "arbitrary")),
    )(*[t[0] for t in tok], *bat, *con)


def rowcall_fwd(name, f, tok, bat, con, tok_out, ts=256):
    def fn(t, b, c):
        return f([v.astype(f32) for v in t], b, c), []
    return rowcall(name, fn, tok, bat, con, tok_out, [], ts)


def rowcall_bwd(name, f, tok, bat, con, cts, tok_grads, add=None, ts=256):
    nt, ncts = len(tok), len(cts)

    def fn(t, b, c):
        prim = [v.astype(f32) for v in t[:nt]]
        ct = [v.astype(f32) for v in t[nt:nt + ncts]]
        _, vjp = jax.vjp(lambda tt, bb, cc: f(tt, bb, cc), prim, b, c)
        dt, db, dc = vjp(ct)
        touts = [dt[i] for i, _ in tok_grads]
        if add is not None:
            touts[0] = touts[0] + t[nt + ncts].astype(f32)
        return touts, list(db) + list(dc)

    all_tok = list(tok) + list(cts) + ([add] if add is not None else [])
    tok_out = [(tok[i][1], dt) for i, dt in tok_grads]
    acc_out = [tuple(a.shape[1:]) for a in bat] + [tuple(a.shape) for a in con]
    return rowcall(name, fn, all_tok, bat, con, tok_out, acc_out, ts)


def _rms(y, w):
    return y * lax.rsqrt(jnp.mean(y * y, axis=-1, keepdims=True) + RMS_EPS) * w


def f_rms_mod(t, b, c):
    return [_rms(t[0], c[0]) * (1.0 + b[0]) + b[1]]


def f_resid(t, b, c):
    return [t[0] + b[0] * _rms(t[1], c[0])]


def f_merge(t, b, c):
    ga, gd, ya, yd = t
    return [jax.nn.sigmoid(ga) * ya + jax.nn.sigmoid(gd) * yd]


def f_dnout(t, b, c):
    o, z = t
    outs = []
    for h in range(DNH):
        sl = slice(h * DND, (h + 1) * DND)
        zh = z[:, sl]
        outs.append(_rms(o[:, sl], c[0]) * (zh * jax.nn.sigmoid(zh)))
    return [jnp.concatenate(outs, axis=1)]


def _softplus(x):
    return jnp.maximum(x, 0.0) + jnp.log(1.0 + jnp.exp(-jnp.abs(x)))


def f_gate(t, b, c):
    ba = t[0]
    a_log, dt_bias = c
    lane = lax.broadcasted_iota(jnp.int32, ba.shape, 1)
    beta = jax.nn.sigmoid(ba)
    g = -jnp.exp(a_log) * _softplus(ba + dt_bias)
    return [jnp.where(lane < DNH, beta, jnp.where(lane < 2 * DNH, g, 0.0))]


def _bucket_table():
    qi = np.arange(WIN)[:, None]
    kj = np.arange(2 * WIN)[None, :]
    dist = np.maximum(WIN + qi - kj, 0)
    max_exact = NBUCK // 2
    scaled = np.log(np.maximum(dist, 1).astype(np.float64) / max_exact) / math.log(MAXDIST / max_exact)
    large = np.minimum(max_exact + (scaled * (NBUCK - max_exact)).astype(np.int32), NBUCK - 1)
    return np.where(dist < max_exact, dist, large).astype(np.int32)


def _attn_mask(n):
    qi = lax.broadcasted_iota(jnp.int32, (WIN, 2 * WIN), 0)
    kj = lax.broadcasted_iota(jnp.int32, (WIN, 2 * WIN), 1)
    dist = WIN + qi - kj
    return (dist >= 0) & (dist < WIN) & ((kj >= WIN) | (n > 0))


def _attn_block(q, kp, kc, vp, vc, bias, sinks, mask):
    grp = HQ // HKV
    outs = []
    for j in range(HKV):
        sl = slice(j * LANE, (j + 1) * LANE)
        kb = jnp.concatenate([kp[:, sl], kc[:, sl]], axis=0).astype(bf16)
        vb = jnp.concatenate([vp[:, sl], vc[:, sl]], axis=0).astype(bf16)
        for g in range(grp):
            h = j * grp + g
            qh = q[:, h * LANE:(h + 1) * LANE].astype(bf16)
            s = lax.dot_general(qh, kb, (((1,), (1,)), ((), ())), preferred_element_type=f32) * (HD ** -0.5)
            s = jnp.where(mask, s + bias[h], NEG_INF)
            sink = sinks[h]
            m = jnp.maximum(jnp.max(s, axis=-1, keepdims=True), sink)
            p = jnp.exp(s - m)
            probs = p / (jnp.sum(p, axis=-1, keepdims=True) + jnp.exp(sink - m))
            outs.append(jnp.dot(probs.astype(bf16), vb, preferred_element_type=f32))
    return jnp.concatenate(outs, axis=1)


def _attn_specs(NB):
    last = NB - 1
    return [
        pl.BlockSpec((1, WIN, HQ * LANE), lambda b, n: (b, jnp.minimum(n, last), CB_AQ // 8)),
        pl.BlockSpec((1, WIN, HKV * LANE), lambda b, n: (b, jnp.clip(n - 1, 0, last), CB_AK // 2)),
        pl.BlockSpec((1, WIN, HKV * LANE), lambda b, n: (b, jnp.minimum(n, last), CB_AK // 2)),
        pl.BlockSpec((1, WIN, HKV * LANE), lambda b, n: (b, jnp.clip(n - 1, 0, last), CB_AV // 2)),
        pl.BlockSpec((1, WIN, HKV * LANE), lambda b, n: (b, jnp.minimum(n, last), CB_AV // 2)),
        pl.BlockSpec((HQ, WIN, 2 * WIN), lambda b, n: (0, 0, 0)),
        pl.BlockSpec((HQ, 1, 1), lambda b, n: (0, 0, 0)),
    ]


def attn_fwd(proj, bias, sinks):
    B, S, _ = proj.shape
    NB = S // WIN

    def body(q, kp, kc, vp, vc, bias_ref, sink_ref, o_ref):
        mask = _attn_mask(pl.program_id(1))
        o = _attn_block(q[0], kp[0], kc[0], vp[0], vc[0], bias_ref[...], sink_ref[...], mask)
        o_ref[0] = o.astype(o_ref.dtype)

    return pl.pallas_call(
        body, name="attn_fwd", grid=(B, NB), in_specs=_attn_specs(NB),
        out_specs=pl.BlockSpec((1, WIN, HQ * LANE), lambda b, n: (b, n, 0)),
        out_shape=jax.ShapeDtypeStruct((B, S, HQ * LANE), bf16), compiler_params=_cp(("parallel", "parallel")),
    )(proj, proj, proj, proj, proj, bias, sinks)


def attn_bwd(proj, bias, sinks, dy):
    B, S, _ = proj.shape
    NB = S // WIN
    last = NB - 1

    def body(q, kp, kc, vp, vc, bias_ref, sink_ref, dy_ref, dq_ref, dk_ref, dv_ref, dbias_ref, dsink_ref, kcar, vcar):
        b, n = pl.program_id(0), pl.program_id(1)

        @pl.when((b == 0) & (n == 0))
        def _():
            dbias_ref[...] = jnp.zeros(dbias_ref.shape, f32)
            dsink_ref[...] = jnp.zeros(dsink_ref.shape, f32)

        @pl.when(n == 0)
        def _():
            kcar[...] = jnp.zeros(kcar.shape, f32)
            vcar[...] = jnp.zeros(vcar.shape, f32)

        @pl.when(n < NB)
        def _():
            mask = _attn_mask(n)
            _, vjp = jax.vjp(lambda *a: _attn_block(*a, mask), q[0], kp[0], kc[0], vp[0], vc[0], bias_ref[...], sink_ref[...])
            dq, dkp, dkc, dvp, dvc, dbias, dsink = vjp(dy_ref[0].astype(f32))
            dq_ref[0] = dq.astype(dq_ref.dtype)
            dbias_ref[...] += dbias
            dsink_ref[...] += dsink
            dk_ref[0] = (kcar[...] + dkp).astype(dk_ref.dtype)
            dv_ref[0] = (vcar[...] + dvp).astype(dv_ref.dtype)
            kcar[...] = dkc
            vcar[...] = dvc

        @pl.when(n == NB)
        def _():
            dk_ref[0] = kcar[...].astype(dk_ref.dtype)
            dv_ref[0] = vcar[...].astype(dv_ref.dtype)

    in_specs = _attn_specs(NB) + [pl.BlockSpec((1, WIN, HQ * LANE), lambda b, n: (b, jnp.minimum(n, last), 0))]
    kv_out = pl.BlockSpec((1, WIN, HKV * LANE), lambda b, n: (b, jnp.maximum(n - 1, 0), 0))
    return pl.pallas_call(
        body, name="attn_bwd", grid=(B, NB + 1), in_specs=in_specs,
        out_specs=[pl.BlockSpec((1, WIN, HQ * LANE), lambda b, n: (b, jnp.minimum(n, last), 0)), kv_out, kv_out,
                   pl.BlockSpec((HQ, WIN, 2 * WIN), lambda b, n: (0, 0, 0)), pl.BlockSpec((HQ, 1, 1), lambda b, n: (0, 0, 0))],
        out_shape=[jax.ShapeDtypeStruct((B, S, HQ * LANE), bf16), jax.ShapeDtypeStruct((B, S, HKV * LANE), bf16),
                   jax.ShapeDtypeStruct((B, S, HKV * LANE), bf16), jax.ShapeDtypeStruct((HQ, WIN, 2 * WIN), f32),
                   jax.ShapeDtypeStruct((HQ, 1, 1), f32)],
        scratch_shapes=[pltpu.VMEM((WIN, HKV * LANE), f32), pltpu.VMEM((WIN, HKV * LANE), f32)],
        compiler_params=_cp(("arbitrary", "arbitrary")),
    )(proj, proj, proj, proj, proj, bias, sinks, dy)


def _causal_conv(x, w, width):
    S, C = x.shape
    xp = jnp.concatenate([jnp.zeros((8, C), f32), x], axis=0)
    out = None
    for j in range(width):
        off = 8 - (width - 1) + j
        term = w[j:j + 1, :] * xp[off:off + S, :]
        out = term if out is None else out + term
    return out


def _dnconv_f(x, w, isqk):
    y = _causal_conv(x, w, DNK)
    y = y * jax.nn.sigmoid(y)
    yn = y * lax.rsqrt(jnp.sum(y * y, axis=-1, keepdims=True) + L2_EPS)
    return jnp.where(isqk, yn, y)


def _dn_outblk(i):
    return (i % DNH) * 3 + i // DNH


def dnconv_fwd(proj, conv_w):
    B, S, _ = proj.shape

    def body(x_ref, w_ref, o_ref):
        o_ref[0] = _dnconv_f(x_ref[0], w_ref[...], pl.program_id(0) < 2 * DNH)

    return pl.pallas_call(
        body, name="dnconv_fwd", grid=(3 * DNH, B),
        in_specs=[pl.BlockSpec((1, S, LANE), lambda i, b: (b, 0, CB_DQKV + i)), pl.BlockSpec((DNK, LANE), lambda i, b: (0, i))],
        out_specs=pl.BlockSpec((1, S, LANE), lambda i, b: (b, 0, _dn_outblk(i))),
        out_shape=jax.ShapeDtypeStruct((B, S, 3 * DNH * DND), f32), compiler_params=_cp(("parallel", "parallel")),
    )(proj, conv_w)


def dnconv_bwd(proj, conv_w, dqkvn):
    B, S, _ = proj.shape

    def body(x_ref, w_ref, dy_ref, dx_ref, dw_ref):
        isqk = pl.program_id(0) < 2 * DNH
        _, vjp = jax.vjp(lambda x, w: _dnconv_f(x, w, isqk), x_ref[0], w_ref[...])
        dx, dw = vjp(dy_ref[0])
        dx_ref[0] = dx.astype(dx_ref.dtype)

        @pl.when(pl.program_id(1) == 0)
        def _():
            dw_ref[...] = jnp.zeros(dw_ref.shape, f32)
        dw_ref[...] += dw

    return pl.pallas_call(
        body, name="dnconv_bwd", grid=(3 * DNH, B),
        in_specs=[pl.BlockSpec((1, S, LANE), lambda i, b: (b, 0, CB_DQKV + i)), pl.BlockSpec((DNK, LANE), lambda i, b: (0, i)),
                  pl.BlockSpec((1, S, LANE), lambda i, b: (b, 0, _dn_outblk(i)))],
        out_specs=[pl.BlockSpec((1, S, LANE), lambda i, b: (b, 0, i)), pl.BlockSpec((DNK, LANE), lambda i, b: (0, i))],
        out_shape=[jax.ShapeDtypeStruct((B, S, 3 * DNH * DND), bf16), jax.ShapeDtypeStruct((DNK, 3 * DNH * DND), f32)],
        compiler_params=_cp(("parallel", "arbitrary")),
    )(proj, conv_w, dqkvn)


def _neumann_inverse(low):
    n = low.shape[0]
    eye = (lax.broadcasted_iota(jnp.int32, (n, n), 0) == lax.broadcasted_iota(jnp.int32, (n, n), 1)).astype(f32)
    p = -low
    x = eye + p
    for _ in range(5):
        p = jnp.dot(p, p, preferred_element_type=f32, precision=HI)
        x = x + jnp.dot(x, p, preferred_element_type=f32, precision=HI)
    return x


@jax.custom_vjp
def _unit_lower_inverse(low):
    return _neumann_inverse(low)


def _uli_fwd(low):
    t = _neumann_inverse(low)
    return t, t


def _uli_bwd(t, dt):
    a = lax.dot_general(t, dt, (((0,), (0,)), ((), ())), preferred_element_type=f32, precision=HI)
    return (-lax.dot_general(a, t, (((1,), (1,)), ((), ())), preferred_element_type=f32, precision=HI),)


_unit_lower_inverse.defvjp(_uli_fwd, _uli_bwd)


def _dot(a, b):
    return jnp.dot(a, b, preferred_element_type=f32, precision=HI)


def _dot_nt(a, b):
    return lax.dot_general(a, b, (((1,), (1,)), ((), ())), preferred_element_type=f32, precision=HI)


def _dot_tn(a, b):
    return lax.dot_general(a, b, (((0,), (0,)), ((), ())), preferred_element_type=f32, precision=HI)


def _delta_chunk(qkv, bg, state, h, inverse):
    q, k, v = qkv[:, :DND], qkv[:, DND:2 * DND], qkv[:, 2 * DND:]
    lane = lax.broadcasted_iota(jnp.int32, (CH, LANE), 1)
    beta = jnp.sum(jnp.where(lane == h, bg, 0.0), axis=1, keepdims=True)
    g = jnp.sum(jnp.where(lane == h + DNH, bg, 0.0), axis=1, keepdims=True)
    ri = lax.broadcasted_iota(jnp.int32, (CH, CH), 0)
    ci = lax.broadcasted_iota(jnp.int32, (CH, CH), 1)
    incl, strict = ri >= ci, ri > ci
    gc = _dot(incl.astype(f32), jnp.broadcast_to(g, (CH, LANE)))
    e0 = (lane == 0).astype(f32)
    gc_row = _dot_nt(e0, gc)
    diff = gc[:, :CH] - gc_row
    decay = jnp.where(incl, jnp.exp(jnp.where(incl, diff, 0.0)), 0.0)
    qs = q * (DND ** -0.5)
    kb, vb = k * beta, v * beta
    eg = jnp.exp(gc)
    low = jnp.where(strict, _dot_nt(kb, k) * decay, 0.0)
    tinv = inverse(low)
    u = _dot(tinv, vb)
    w = _dot(tinv, kb * eg)
    intra = jnp.where(incl, _dot_nt(qs, k) * decay, 0.0)
    gl = gc[CH - 1:CH, :]
    k_tail = k * jnp.exp(gl - gc)
    v_new = u - _dot(w, state)
    o = _dot(qs * eg, state) + _dot(intra, v_new)
    new_state = state * jnp.exp(gl) + _dot_tn(k_tail, v_new)
    return o, new_state


def delta_fwd(qkvn, bg):
    B, S, _ = qkvn.shape
    NC = S // CH

    def body(qkv_ref, bg_ref, o_ref, st_ref, state):
        @pl.when(pl.program_id(2) == 0)
        def _():
            state[...] = jnp.zeros(state.shape, f32)
        s0 = state[...]
        st_ref[0, 0, 0] = s0
        o, s1 = _delta_chunk(qkv_ref[0], bg_ref[0], s0, pl.program_id(1), _neumann_inverse)
        o_ref[0] = o
        state[...] = s1

    return pl.pallas_call(
        body, name="delta_fwd", grid=(B, DNH, NC),
        in_specs=[pl.BlockSpec((1, CH, 3 * DND), lambda b, h, c: (b, c, h)), pl.BlockSpec((1, CH, LANE), lambda b, h, c: (b, c, 0))],
        out_specs=[pl.BlockSpec((1, CH, DND), lambda b, h, c: (b, c, h)),
                   pl.BlockSpec((1, 1, 1, DND, DND), lambda b, h, c: (b, h, c, 0, 0))],
        out_shape=[jax.ShapeDtypeStruct((B, S, DNH * DND), f32), jax.ShapeDtypeStruct((B, DNH, NC, DND, DND), f32)],
        scratch_shapes=[pltpu.VMEM((DND, DND), f32)], compiler_params=_cp(("parallel", "parallel", "arbitrary")),
    )(qkvn, bg)


def delta_bwd(qkvn, bg, states, do):
    B, S, _ = qkvn.shape
    NC = S // CH

    def body(qkv_ref, bg_ref, st_ref, do_ref, dqkv_ref, dbg_ref, dstate):
        h = pl.program_id(1)

        @pl.when(pl.program_id(2) == 0)
        def _():
            dstate[...] = jnp.zeros(dstate.shape, f32)
        _, vjp = jax.vjp(lambda a, b, s: _delta_chunk(a, b, s, h, _unit_lower_inverse), qkv_ref[0], bg_ref[0], st_ref[0, 0, 0])
        dqkv, dbg, ds = vjp((do_ref[0], dstate[...]))
        dqkv_ref[0] = dqkv
        dbg_ref[0] = dbg
        dstate[...] = ds

    rev = lambda c: NC - 1 - c
    return pl.pallas_call(
        body, name="delta_bwd", grid=(B, DNH, NC),
        in_specs=[pl.BlockSpec((1, CH, 3 * DND), lambda b, h, c: (b, rev(c), h)), pl.BlockSpec((1, CH, LANE), lambda b, h, c: (b, rev(c), 0)),
                  pl.BlockSpec((1, 1, 1, DND, DND), lambda b, h, c: (b, h, rev(c), 0, 0)),
                  pl.BlockSpec((1, CH, DND), lambda b, h, c: (b, rev(c), h))],
        out_specs=[pl.BlockSpec((1, CH, 3 * DND), lambda b, h, c: (b, rev(c), h)), pl.BlockSpec((1, CH, LANE), lambda b, h, c: (b, rev(c), h))],
        out_shape=[jax.ShapeDtypeStruct((B, S, 3 * DNH * DND), f32), jax.ShapeDtypeStruct((B, S, DNH * LANE), f32)],
        scratch_shapes=[pltpu.VMEM((DND, DND), f32)], compiler_params=_cp(("parallel", "parallel", "arbitrary")),
    )(qkvn, bg, states, do)


def _ffn_f(blk, w):
    u = _causal_conv(blk, w, FK)
    gate, val = u[:, :LANE], u[:, LANE:]
    gl = 0.5 * gate * (1.0 + jnp.tanh(math.sqrt(2.0 / math.pi) * (gate + 0.044715 * gate * gate * gate)))
    return gl * val


def ffnconv_fwd(up, conv_w):
    B, S, _ = up.shape
    nblk = DFF // LANE

    def body(x_ref, w_ref, o_ref):
        o_ref[0] = _ffn_f(x_ref[0], w_ref[...]).astype(o_ref.dtype)

    return pl.pallas_call(
        body, name="ffnconv_fwd", grid=(nblk, B),
        in_specs=[pl.BlockSpec((1, S, 2 * LANE), lambda i, b: (b, 0, i)), pl.BlockSpec((FK, 2 * LANE), lambda i, b: (0, i))],
        out_specs=pl.BlockSpec((1, S, LANE), lambda i, b: (b, 0, i)),
        out_shape=jax.ShapeDtypeStruct((B, S, DFF), bf16), compiler_params=_cp(("parallel", "parallel")),
    )(up, conv_w)


def ffnconv_bwd(up, conv_w, dact):
    B, S, _ = up.shape
    nblk = DFF // LANE

    def body(x_ref, w_ref, dy_ref, dx_ref, dw_ref):
        _, vjp = jax.vjp(_ffn_f, x_ref[0], w_ref[...])
        dx, dw = vjp(dy_ref[0].astype(f32))
        dx_ref[0] = dx.astype(dx_ref.dtype)

        @pl.when(pl.program_id(1) == 0)
        def _():
            dw_ref[...] = jnp.zeros(dw_ref.shape, f32)
        dw_ref[...] += dw

    return pl.pallas_call(
        body, name="ffnconv_bwd", grid=(nblk, B),
        in_specs=[pl.BlockSpec((1, S, 2 * LANE), lambda i, b: (b, 0, i)), pl.BlockSpec((FK, 2 * LANE), lambda i, b: (0, i)),
                  pl.BlockSpec((1, S, LANE), lambda i, b: (b, 0, i))],
        out_specs=[pl.BlockSpec((1, S, 2 * LANE), lambda i, b: (b, 0, i)), pl.BlockSpec((FK, 2 * LANE), lambda i, b: (0, i))],
        out_shape=[jax.ShapeDtypeStruct((B, S, 2 * DFF), bf16), jax.ShapeDtypeStruct((FK, 2 * DFF), f32)],
        compiler_params=_cp(("parallel", "arbitrary")),
    )(up, conv_w, dact)


def ada_fwd(c_all, ada_w, ada_b):
    def body(c_ref, w_ref, b_ref, o_ref):
        c = c_ref[...]
        act = (c * jax.nn.sigmoid(c)).astype(bf16)
        o_ref[...] = jnp.dot(act, w_ref[...].astype(bf16), preferred_element_type=f32) + b_ref[...]

    return pl.pallas_call(body, name="ada_fwd", out_shape=jax.ShapeDtypeStruct((c_all.shape[0], ada_w.shape[1]), f32),
                          compiler_params=pltpu.CompilerParams(vmem_limit_bytes=VMEM_LIMIT))(c_all, ada_w, ada_b)


def ada_bwd(c_all, dmod):
    def body(c_ref, d_ref, o_ref):
        c = c_ref[...]
        act = (c * jax.nn.sigmoid(c)).astype(bf16)
        o_ref[...] = lax.dot_general(act, d_ref[...].astype(bf16), (((0,), (0,)), ((), ())), preferred_element_type=f32)

    return pl.pallas_call(body, name="ada_bwd", out_shape=jax.ShapeDtypeStruct((c_all.shape[1], dmod.shape[1]), f32),
                          compiler_params=pltpu.CompilerParams(vmem_limit_bytes=VMEM_LIMIT))(c_all, dmod)


def loss_head(h1, y2, target, g2, w):
    def fn(t, b, c):
        h, y, tg = [v.astype(f32) for v in t]

        def loss_fn(h, y, g, w):
            e = h + g * _rms(y, w) - tg
            return 0.5 * jnp.sum(jnp.mean(e * e, axis=-1))

        loss, grads = jax.value_and_grad(loss_fn, argnums=(0, 1, 2, 3))(h, y, b[0], c[0])
        return [grads[0], grads[1]], [grads[2], grads[3], jnp.full((1, LANE), loss, f32)]

    return rowcall("loss_head", fn, [(h1, D, 0), (y2, D, 0), (target, D, 0)], [g2], [w], [(D, f32), (D, bf16)],
                   [(1, D), (1, D), (1, LANE)])


def adamw(w, gparts, m, v, name):
    R, C = w.shape
    P = gparts.shape[0]
    tr = R
    if R * C * 4 > 2 * 1024 * 1024:
        for cand in (512, 256, 128, 64, 32, 16, 8):
            if R % cand == 0 and cand * C * 4 <= 2 * 1024 * 1024:
                tr = cand
                break

    def body(w_ref, g_ref, m_ref, v_ref, go, do, mo, vo):
        g = g_ref[0].astype(f32)
        for p in range(1, P):
            g = g + g_ref[p].astype(f32)
        m2 = B1 * m_ref[...] + (1.0 - B1) * g
        v2 = B2 * v_ref[...] + (1.0 - B2) * jnp.square(g)
        m_hat = m2 / (1.0 - B1 ** STEP)
        v_hat = v2 / (1.0 - B2 ** STEP)
        go[...] = g
        do[...] = -LR * (m_hat / (jnp.sqrt(v_hat) + EPS) + WD * w_ref[...])
        mo[...] = m2
        vo[...] = v2

    blk = pl.BlockSpec((tr, C), lambda i: (i, 0))
    return pl.pallas_call(
        body, name=name, grid=(R // tr,), in_specs=[blk, pl.BlockSpec((P, tr, C), lambda i: (0, i, 0)), blk, blk],
        out_specs=[blk] * 4, out_shape=[jax.ShapeDtypeStruct((R, C), f32)] * 4, compiler_params=_cp(("parallel",)),
    )(w, gparts, m, v)


def _pad_heads(w, nh):
    r = w.shape[0]
    return jnp.pad(w.reshape(r, nh, HD), ((0, 0), (0, 0), (0, LANE - HD))).reshape(r, nh * LANE)


def _unpad_heads(w, nh):
    return w.reshape(w.shape[0], nh, LANE)[:, :, :HD].reshape(w.shape[0], nh * HD)


def _pack_w_in(w):
    aq, ak, av, dqkv, dz, dbeta, da, ga, gd = jnp.split(w, np.cumsum(IN_SPLITS)[:-1].tolist(), axis=1)
    ba = jnp.pad(jnp.concatenate([dbeta, da], axis=1), ((0, 0), (0, LANE - 2 * DNH)))
    return jnp.concatenate([ga, gd, _pad_heads(aq, HQ), dqkv, dz, _pad_heads(ak, HKV), _pad_heads(av, HKV), ba], axis=1)


def _unpack_w_in(p):
    col = lambda cb, n: p[:, cb * LANE: cb * LANE + n]
    ba = col(CB_BA, 2 * DNH)
    return jnp.concatenate([_unpad_heads(col(CB_AQ, HQ * LANE), HQ), _unpad_heads(col(CB_AK, HKV * LANE), HKV),
                            _unpad_heads(col(CB_AV, HKV * LANE), HKV), col(CB_DQKV, 3 * DNH * DND), col(CB_DZ, DNH * DND),
                            ba[:, :DNH], ba[:, DNH:], col(CB_GA, D), col(CB_GD, D)], axis=1)


def _interleave(w):
    r = w.shape[0]
    return w.reshape(r, 2, DFF // LANE, LANE).transpose(0, 2, 1, 3).reshape(r, 2 * DFF)


def _deinterleave(w):
    r = w.shape[0]
    return w.reshape(r, DFF // LANE, 2, LANE).transpose(0, 2, 1, 3).reshape(r, 2 * DFF)


def _cols_gathered(g):
    return g.transpose(1, 0, 2).reshape(g.shape[1], NDEV * g.shape[2])


def _cols_split(w):
    r = w.shape[0]
    return w.reshape(r, NDEV, w.shape[1] // NDEV).transpose(1, 0, 2)


def kernel(x, c, ada_w, ada_b, norm_mix_pre, norm_mix_post, norm_ffn_pre, norm_ffn_post, w_in, dn_conv_w, dn_a_log, dn_dt_bias, dn_norm_w, attn_sinks, rel_bias, w_attn_branch, w_dn_branch, w_out, ffn_w_up, ffn_conv_w, ffn_w_down, loss_target, m_ada_w, m_ada_b, m_norm_mix_pre, m_norm_mix_post, m_norm_ffn_pre, m_norm_ffn_post, m_w_in, m_dn_conv_w, m_dn_a_log, m_dn_dt_bias, m_dn_norm_w, m_attn_sinks, m_rel_bias, m_w_attn_branch, m_w_dn_branch, m_w_out, m_ffn_w_up, m_ffn_conv_w, m_ffn_w_down, v_ada_w, v_ada_b, v_norm_mix_pre, v_norm_mix_post, v_norm_ffn_pre, v_norm_ffn_post, v_w_in, v_dn_conv_w, v_dn_a_log, v_dn_dt_bias, v_dn_norm_w, v_attn_sinks, v_rel_bias, v_w_attn_branch, v_w_dn_branch, v_w_out, v_ffn_w_up, v_ffn_conv_w, v_ffn_w_down):
    B, S, _ = x.shape
    T = B * S
    me = 4 * lax.axis_index("x") + 2 * lax.axis_index("y") + lax.axis_index("c")
    big = dict(w_in=w_in, dn_conv_w=dn_conv_w, w_attn_branch=w_attn_branch, w_dn_branch=w_dn_branch, w_out=w_out,
               ffn_w_up=ffn_w_up, ffn_conv_w=ffn_conv_w, ffn_w_down=ffn_w_down)
    big_names = list(big)

    gathered = _exchange([big[n][0].astype(bf16) for n in big_names], "gather_weights", scatter=False)
    gw = dict(zip(big_names, gathered))
    (c_all,) = _exchange([c], "gather_c", scatter=False)
    c_all = c_all.reshape(NDEV * B, D)

    wp = _pack_w_in(_cols_gathered(gw["w_in"]))
    conv_dn = _cols_gathered(gw["dn_conv_w"]).astype(f32)
    wa = _cols_gathered(gw["w_attn_branch"])
    wa = jnp.pad(wa.reshape(HQ, HD, D), ((0, 0), (0, LANE - HD), (0, 0))).reshape(HQ * LANE, D)
    wd = _cols_gathered(gw["w_dn_branch"])
    wo = gw["w_out"].reshape(D, D)
    wup = _interleave(_cols_gathered(gw["ffn_w_up"]))
    conv_ffn = _interleave(_cols_gathered(gw["ffn_conv_w"]).astype(f32))
    wdown = gw["ffn_w_down"].reshape(DFF, D)

    ncol = ada_w.shape[2]
    ada_b_mine = lax.dynamic_slice_in_dim(ada_b, me * ncol, ncol, axis=1)
    mod_cols = ada_fwd(c_all, ada_w[0], ada_b_mine)
    (mod_g,) = _exchange([mod_cols], "gather_mod", scatter=False)
    mod = lax.dynamic_slice_in_dim(mod_g, me * B, B, axis=1).transpose(1, 0, 2).reshape(B, NMOD * D)
    sh1, sc1, g1, sh2, sc2, g2 = [mod[:, i * D:(i + 1) * D].reshape(B, 1, D) for i in range(NMOD)]

    onehot = (jnp.asarray(_bucket_table()).reshape(1, -1) == jnp.arange(NBUCK, dtype=jnp.int32)[:, None]).astype(f32)
    bias = mm(rel_bias.T, onehot, "nn", f32, "bias_table", tn=8192, precision=HI).reshape(HQ, WIN, 2 * WIN)
    sinks = attn_sinks.reshape(HQ, 1, 1)
    a_log_pad = jnp.pad(dn_a_log, ((0, 0), (DNH, LANE - 2 * DNH)))
    dt_bias_pad = jnp.pad(dn_dt_bias, ((0, 0), (DNH, LANE - 2 * DNH)))

    (u1,) = rowcall_fwd("mix_pre", f_rms_mod, [(x, D, 0)], [sc1, sh1], [norm_mix_pre], [(D, bf16)])
    proj = mm(u1.reshape(T, D), wp, "nn", f32, "proj", tn=1152).reshape(B, S, NP)
    ya = attn_fwd(proj, bias, sinks)
    qkvn = dnconv_fwd(proj, conv_dn)
    (bg,) = rowcall_fwd("dn_gate", f_gate, [(proj, LANE, CB_BA)], [], [a_log_pad, dt_bias_pad], [(LANE, f32)])
    o_dn, states = delta_fwd(qkvn, bg)
    (yd,) = rowcall_fwd("dn_out", f_dnout, [(o_dn, DNH * DND, 0), (proj, DNH * DND, CB_DZ // 4)], [], [dn_norm_w], [(DNH * DND, bf16)])
    pa = mm(ya.reshape(T, HQ * LANE), wa, "nn", f32, "attn_branch").reshape(B, S, D)
    pd = mm(yd.reshape(T, DNH * DND), wd, "nn", f32, "dn_branch").reshape(B, S, D)
    merge_tok = [(proj, D, CB_GA // 8), (proj, D, CB_GD // 8), (pa, D, 0), (pd, D, 0)]
    (merged,) = rowcall_fwd("merge", f_merge, merge_tok, [], [], [(D, bf16)])
    y1 = mm(merged.reshape(T, D), wo, "nn", f32, "mix_out").reshape(B, S, D)
    (h1,) = rowcall_fwd("mix_post", f_resid, [(x, D, 0), (y1, D, 0)], [g1], [norm_mix_post], [(D, f32)])
    (u2,) = rowcall_fwd("ffn_pre", f_rms_mod, [(h1, D, 0)], [sc2, sh2], [norm_ffn_pre], [(D, bf16)])
    up = mm(u2.reshape(T, D), wup, "nn", f32, "ffn_up", tn=1408).reshape(B, S, 2 * DFF)
    act = ffnconv_fwd(up, conv_ffn)
    y2 = mm(act.reshape(T, DFF), wdown, "nn", f32, "ffn_down", tk=1408).reshape(B, S, D)

    dh1_a, dy2, dg2, dw_ffn_post, loss_b = loss_head(h1, y2, loss_target, g2, norm_ffn_post)
    dy2f = dy2.reshape(T, D)
    dact = mm(dy2f, wdown, "nt", bf16, "ffn_down_dx", tn=1408).reshape(B, S, DFF)
    g_wdown = mm(act.reshape(T, DFF), dy2f, "tn", f32, "ffn_down_dw", tm=1408, tk=512)
    dup, g_conv_ffn = ffnconv_bwd(up, conv_ffn, dact)
    dupf = dup.reshape(T, 2 * DFF)
    du2 = mm(dupf, wup, "nt", f32, "ffn_up_dx", tk=1408).reshape(B, S, D)
    g_wup = mm(u2.reshape(T, D), dupf, "tn", f32, "ffn_up_dw", tn=1408, tk=512)
    dh1, dsc2, dsh2, dw_ffn_pre = rowcall_bwd("ffn_pre_bwd", f_rms_mod, [(h1, D, 0)], [sc2, sh2], [norm_ffn_pre], [(du2, D, 0)],
                                              [(0, f32)], add=(dh1_a, D, 0))
    dy1, dg1, dw_mix_post = rowcall_bwd("mix_post_bwd", f_resid, [(x, D, 0), (y1, D, 0)], [g1], [norm_mix_post], [(dh1, D, 0)],
                                        [(1, bf16)])
    dy1f = dy1.reshape(T, D)
    dmerged = mm(dy1f, wo, "nt", f32, "mix_out_dx").reshape(B, S, D)
    g_wo = mm(merged.reshape(T, D), dy1f, "tn", f32, "mix_out_dw", tk=512)
    dga, dgd, dpa, dpd = rowcall_bwd("merge_bwd", f_merge, merge_tok, [], [], [(dmerged, D, 0)],
                                     [(0, bf16), (1, bf16), (2, bf16), (3, bf16)])
    dpaf, dpdf = dpa.reshape(T, D), dpd.reshape(T, D)
    dya = mm(dpaf, wa, "nt", bf16, "attn_branch_dx").reshape(B, S, HQ * LANE)
    g_wa = mm(ya.reshape(T, HQ * LANE), dpaf, "tn", f32, "attn_branch_dw", tk=512)
    dyd = mm(dpdf, wd, "nt", f32, "dn_branch_dx").reshape(B, S, DNH * DND)
    g_wd = mm(yd.reshape(T, DNH * DND), dpdf, "tn", f32, "dn_branch_dw", tk=512)
    do_dn, dz, dw_dn_norm = rowcall_bwd("dn_out_bwd", f_dnout, [(o_dn, DNH * DND, 0), (proj, DNH * DND, CB_DZ // 4)], [], [dn_norm_w],
                                        [(dyd, DNH * DND, 0)], [(0, f32), (1, bf16)])
    dqkvn, dbg4 = delta_bwd(qkvn, bg, states, do_dn)

    def f_gate4(t, b, c):
        return [jnp.concatenate(f_gate(t, b, c) * DNH, axis=1)]

    dba, da_log_pad, ddt_bias_pad = rowcall_bwd("dn_gate_bwd", f_gate4, [(proj, LANE, CB_BA)], [], [a_log_pad, dt_bias_pad],
                                                [(dbg4, DNH * LANE, 0)], [(0, bf16)])
    ddqkv, g_conv_dn = dnconv_bwd(proj, conv_dn, dqkvn)
    dq, dk, dv, dbias, dsinks = attn_bwd(proj, bias, sinks, dya)
    dproj = jnp.concatenate([dga, dgd, dq, ddqkv, dz, dk, dv, dba], axis=2).reshape(T, NP)
    du1 = mm(dproj, wp, "nt", f32, "proj_dx", tk=1152).reshape(B, S, D)
    g_wp = mm(u1.reshape(T, D), dproj, "tn", f32, "proj_dw", tn=1152, tk=512)
    grad_x, dsc1, dsh1, dw_mix_pre = rowcall_bwd("mix_pre_bwd", f_rms_mod, [(x, D, 0)], [sc1, sh1], [norm_mix_pre], [(du1, D, 0)],
                                                 [(0, f32)], add=(dh1, D, 0))
    g_rel = mm(dbias.reshape(HQ, WIN * 2 * WIN), onehot, "nt", f32, "rel_bias_dw", tk=8192, precision=HI)

    dmod = jnp.concatenate([dsh1, dsc1, dg1, dsh2, dsc2, dg2], axis=2).reshape(B, NMOD * D)
    (dmod_g,) = _exchange([dmod], "gather_dmod", scatter=False)
    dmod_cols = lax.dynamic_slice_in_dim(dmod_g.reshape(NDEV * B, NMOD * D), me * ncol, ncol, axis=1)
    g_ada_w = ada_bwd(c_all, dmod_cols)

    send = dict(
        w_in=_cols_split(_unpack_w_in(g_wp)), dn_conv_w=_cols_split(g_conv_dn),
        w_attn_branch=_cols_split(g_wa.reshape(HQ, LANE, D)[:, :HD].reshape(HQ * HD, D)), w_dn_branch=_cols_split(g_wd),
        w_out=g_wo.reshape(NDEV, D // NDEV, D), ffn_w_up=_cols_split(_deinterleave(g_wup)),
        ffn_conv_w=_cols_split(_deinterleave(g_conv_ffn)), ffn_w_down=g_wdown.reshape(NDEV, DFF // NDEV, D))
    parts = dict(zip(big_names, _exchange([send[n].astype(bf16) for n in big_names], "scatter_grads", scatter=True)))

    zrow = lambda a: jnp.concatenate([a.reshape(1, -1), jnp.zeros((B - 1, a.size), f32)], axis=0)
    small_g = jnp.concatenate([
        dmod, dw_mix_pre.reshape(B, D), dw_mix_post.reshape(B, D), dw_ffn_pre.reshape(B, D), dw_ffn_post.reshape(B, D),
        da_log_pad.reshape(B, LANE)[:, DNH:2 * DNH], ddt_bias_pad.reshape(B, LANE)[:, DNH:2 * DNH], dw_dn_norm.reshape(B, DND),
        zrow(dsinks), zrow(g_rel.T), loss_b.reshape(B, LANE)[:, :1], jnp.zeros((B, SMALL_PAD - SMALL_N - 1), f32)], axis=1)
    (small_all,) = _exchange([small_g], "gather_small", scatter=False)
    small_w = dict(ada_b=(ada_b, m_ada_b, v_ada_b), norm_mix_pre=(norm_mix_pre, m_norm_mix_pre, v_norm_mix_pre),
                   norm_mix_post=(norm_mix_post, m_norm_mix_post, v_norm_mix_post), norm_ffn_pre=(norm_ffn_pre, m_norm_ffn_pre, v_norm_ffn_pre),
                   norm_ffn_post=(norm_ffn_post, m_norm_ffn_post, v_norm_ffn_post), dn_a_log=(dn_a_log, m_dn_a_log, v_dn_a_log),
                   dn_dt_bias=(dn_dt_bias, m_dn_dt_bias, v_dn_dt_bias), dn_norm_w=(dn_norm_w, m_dn_norm_w, v_dn_norm_w),
                   attn_sinks=(attn_sinks, m_attn_sinks, v_attn_sinks), rel_bias=(rel_bias, m_rel_bias, v_rel_bias))

    def pack(i, fill):
        row = jnp.concatenate([small_w[n][i].reshape(1, -1) for n, _ in SMALL], axis=1)
        return jnp.pad(row, ((0, 0), (0, SMALL_PAD - SMALL_N)), constant_values=fill)

    small_out = adamw(pack(0, 0.0), small_all.reshape(NDEV * B, 1, SMALL_PAD), pack(1, 0.0), pack(2, 1.0), "adamw_small")
    loss = small_out[0][0, SMALL_N]

    res = {}
    off = 0
    for n, size in SMALL:
        shp = small_w[n][0].shape
        res[n] = [o[:, off:off + size].reshape(shp) for o in small_out]
        off += size
    res["ada_w"] = [o[None] for o in adamw(ada_w[0], g_ada_w[None], m_ada_w[0], v_ada_w[0], "adamw_ada_w")]
    moments = dict(w_in=(m_w_in, v_w_in), dn_conv_w=(m_dn_conv_w, v_dn_conv_w), w_attn_branch=(m_w_attn_branch, v_w_attn_branch),
                   w_dn_branch=(m_w_dn_branch, v_w_dn_branch), w_out=(m_w_out, v_w_out), ffn_w_up=(m_ffn_w_up, v_ffn_w_up),
                   ffn_conv_w=(m_ffn_conv_w, v_ffn_conv_w), ffn_w_down=(m_ffn_w_down, v_ffn_w_down))
    for n in big_names:
        res[n] = [o[None] for o in adamw(big[n][0], parts[n], moments[n][0][0], moments[n][1][0], "adamw_" + n)]

    order = ["ada_w", "ada_b", "norm_mix_pre", "norm_mix_post", "norm_ffn_pre", "norm_ffn_post", "w_in", "dn_conv_w", "dn_a_log",
             "dn_dt_bias", "dn_norm_w", "attn_sinks", "rel_bias", "w_attn_branch", "w_dn_branch", "w_out", "ffn_w_up", "ffn_conv_w",
             "ffn_w_down"]
    return (loss, grad_x, *[res[n][0] for n in order], *[res[n][1] for n in order], *[res[n][2] for n in order],
            *[res[n][3] for n in order])
```

```python
import functools
import math

import numpy as np
import jax
import jax.numpy as jnp
from jax import lax
from jax.experimental import pallas as pl
from jax.experimental.pallas import tpu as pltpu

f32 = jnp.float32
bf16 = jnp.bfloat16
HI = lax.Precision.HIGHEST
MID = lax.Precision.HIGH
MESH = pl.DeviceIdType.MESH

NDEV = 8
D = 1024
HQ, HKV, HD, WIN, NBUCK, MAXDIST = 8, 2, 64, 128, 32, 128
DNH, DND, DNK, CH = 4, 128, 4, 64
DFF, FK = 2816, 3
NMOD = 6
RMS_EPS = 1e-6
L2_EPS = 1e-6
NEG_INF = -1e30
LR, B1, B2, EPS, WD, STEP = 0.001, 0.9, 0.999, 1e-08, 0.01, 10

LANE = 128
CB_GA, CB_GD, CB_AQ, CB_DQKV, CB_DZ, CB_AK, CB_AV, CB_BA, NPB = 0, 8, 16, 24, 36, 40, 42, 44, 45
NP = NPB * LANE
IN_SPLITS = (HQ * HD, HKV * HD, HKV * HD, 3 * DNH * DND, DNH * DND, DNH, DNH, D, D)
IN_DIM = sum(IN_SPLITS)
VMEM_LIMIT = 56 * 1024 * 1024

SMALL = (("ada_b", NMOD * D), ("norm_mix_pre", D), ("norm_mix_post", D), ("norm_ffn_pre", D), ("norm_ffn_post", D),
         ("dn_a_log", DNH), ("dn_dt_bias", DNH), ("dn_norm_w", DND), ("attn_sinks", HQ), ("rel_bias", NBUCK * HQ))
SMALL_N = sum(n for _, n in SMALL)
SMALL_PAD = 10752


def _cp(sem):
    return pltpu.CompilerParams(dimension_semantics=sem, vmem_limit_bytes=VMEM_LIMIT)


def _pick(dim, target):
    if dim <= target:
        return dim
    best = None
    for d in range(LANE, target + 1, LANE):
        if dim % d == 0:
            best = d
    assert best is not None, (dim, target)
    return best


def _me():
    x, y, c = lax.axis_index("x"), lax.axis_index("y"), lax.axis_index("c")
    return x, y, c, 4 * x + 2 * y + c


def _peer(x, y, c, k):
    px = 1 - x if k & 4 else x
    py = 1 - y if k & 2 else y
    pc = 1 - c if k & 1 else c
    return (px, py, pc), 4 * px + 2 * py + pc


def _exchange(arrs, name, scatter):
    n = len(arrs)
    if scatter:
        outs = [jax.ShapeDtypeStruct(a.shape, a.dtype) for a in arrs]
    else:
        outs = [jax.ShapeDtypeStruct((NDEV,) + a.shape, a.dtype) for a in arrs]

    def body(*refs):
        ins, out = refs[:n], refs[n:2 * n]
        send, recv, loc = refs[2 * n:]
        x, y, c, me = _me()
        copies = []
        for a in range(n):
            src_own = ins[a].at[me] if scatter else ins[a]
            lc = pltpu.make_async_copy(src_own, out[a].at[me], loc.at[a])
            lc.start()
            copies.append(lc)
        sends, recvs = [], []
        for a in range(n):
            for k in range(1, NDEV):
                peer, pid = _peer(x, y, c, k)
                s = a * (NDEV - 1) + k - 1
                src = ins[a].at[pid] if scatter else ins[a]
                cp = pltpu.make_async_remote_copy(src_ref=src, dst_ref=out[a].at[me], send_sem=send.at[s],
                                                  recv_sem=recv.at[s], device_id=peer, device_id_type=MESH)
                cp.start()
                sends.append(cp)
                recvs.append(pltpu.make_async_remote_copy(src_ref=src, dst_ref=out[a].at[pid], send_sem=send.at[s],
                                                          recv_sem=recv.at[s], device_id=peer, device_id_type=MESH))
        for cp in sends:
            cp.wait_send()
        for cp in recvs:
            cp.wait_recv()
        for lc in copies:
            lc.wait()

    any_spec = pl.BlockSpec(memory_space=pl.ANY)
    return pl.pallas_call(
        body, name=name, out_shape=outs, in_specs=[any_spec] * n, out_specs=[any_spec] * n,
        scratch_shapes=[pltpu.SemaphoreType.DMA((n * (NDEV - 1),)), pltpu.SemaphoreType.DMA((n * (NDEV - 1),)),
                        pltpu.SemaphoreType.DMA((n,))],
        compiler_params=pltpu.CompilerParams(has_side_effects=True),
    )(*arrs)


def mm(a, b, mode, out_dtype, name, tm=1024, tn=1024, tk=1024, precision=None):
    if mode == "nn":
        (M, K), (K2, N) = a.shape, b.shape
    elif mode == "nt":
        (M, K), (N, K2) = a.shape, b.shape
    else:
        (K, M), (K2, N) = a.shape, b.shape
    assert K == K2, (name, a.shape, b.shape)
    tm, tn, tk = _pick(M, tm), _pick(N, tn), _pick(K, tk)
    nk = K // tk
    if mode == "tn":
        a_spec = pl.BlockSpec((tk, tm), lambda i, j, k: (k, i))
    else:
        a_spec = pl.BlockSpec((tm, tk), lambda i, j, k: (i, k))
    if mode == "nt":
        b_spec = pl.BlockSpec((tn, tk), lambda i, j, k: (j, k))
    else:
        b_spec = pl.BlockSpec((tk, tn), lambda i, j, k: (k, j))
    dims = {"nn": ((1,), (0,)), "nt": ((1,), (1,)), "tn": ((0,), (0,))}[mode]

    def body(a_ref, b_ref, o_ref, *scr):
        p = lax.dot_general(a_ref[...], b_ref[...], (dims, ((), ())), preferred_element_type=f32, precision=precision)
        if nk == 1:
            o_ref[...] = p.astype(o_ref.dtype)
        else:
            acc = scr[0]
            k = pl.program_id(2)

            @pl.when(k == 0)
            def _():
                acc[...] = p

            @pl.when(k > 0)
            def _():
                acc[...] += p

            @pl.when(k == nk - 1)
            def _():
                o_ref[...] = acc[...].astype(o_ref.dtype)

    return pl.pallas_call(
        body, name=name, grid=(M // tm, N // tn, nk), in_specs=[a_spec, b_spec],
        out_specs=pl.BlockSpec((tm, tn), lambda i, j, k: (i, j)), out_shape=jax.ShapeDtypeStruct((M, N), out_dtype),
        scratch_shapes=[pltpu.VMEM((tm, tn), f32)] if nk > 1 else [],
        compiler_params=_cp(("parallel", "parallel", "arbitrary")),
    )(a, b)


def rowcall(name, fn, tok, bat, con, tok_out, acc_out, ts=256):
    B, S = tok[0][0].shape[:2]
    ts = min(ts, S)
    nt, nb, nc, no, na = len(tok), len(bat), len(con), len(tok_out), len(acc_out)

    def body(*refs):
        tr, br, cr = refs[:nt], refs[nt:nt + nb], refs[nt + nb:nt + nb + nc]
        orf, arf = refs[nt + nb + nc:nt + nb + nc + no], refs[nt + nb + nc + no:]
        touts, aouts = fn([r[0] for r in tr], [r[0] for r in br], [r[...] for r in cr])
        for r, v in zip(orf, touts):
            r[0] = v.astype(r.dtype)
        s = pl.program_id(1)
        for r, v in zip(arf, aouts):
            @pl.when(s == 0)
            def _(r=r):
                r[...] = jnp.zeros(r.shape, r.dtype)
            r[0] += v.astype(f32)

    in_specs = [pl.BlockSpec((1, ts, w), lambda b, s, cb=cb: (b, s, cb)) for (_, w, cb) in tok]
    in_specs += [pl.BlockSpec((1,) + a.shape[1:], lambda b, s: (b, 0, 0)) for a in bat]
    in_specs += [pl.BlockSpec(a.shape, lambda b, s, nd=a.ndim: (0,) * nd) for a in con]
    out_specs = [pl.BlockSpec((1, ts, w), lambda b, s: (b, s, 0)) for (w, _) in tok_out]
    out_specs += [pl.BlockSpec((1,) + shp, lambda b, s, nd=len(shp): (b,) + (0,) * nd) for shp in acc_out]
    out_shape = [jax.ShapeDtypeStruct((B, S, w), dt) for (w, dt) in tok_out]
    out_shape += [jax.ShapeDtypeStruct((B,) + shp, f32) for shp in acc_out]
    return pl.pallas_call(
        body, name=name, grid=(B, S // ts), in_specs=in_specs, out_specs=out_specs, out_shape=out_shape,
        compiler_params=_cp(("parallel", "arbitrary")),
    )(*[t[0] for t in tok], *bat, *con)


def rowcall_fwd(name, f, tok, bat, con, tok_out, ts=256):
    def fn(t, b, c):
        return f([v.astype(f32) for v in t], b, c), []
    return rowcall(name, fn, tok, bat, con, tok_out, [], ts)


def rowcall_bwd(name, f, tok, bat, con, cts, tok_grads, add=None, ts=256):
    nt, ncts = len(tok), len(cts)

    def fn(t, b, c):
        prim = [v.astype(f32) for v in t[:nt]]
        ct = [v.astype(f32) for v in t[nt:nt + ncts]]
        _, vjp = jax.vjp(lambda tt, bb, cc: f(tt, bb, cc), prim, b, c)
        dt, db, dc = vjp(ct)
        touts = [dt[i] for i, _ in tok_grads]
        if add is not None:
            touts[0] = touts[0] + t[nt + ncts].astype(f32)
        return touts, list(db) + list(dc)

    all_tok = list(tok) + list(cts) + ([add] if add is not None else [])
    tok_out = [(tok[i][1], dt) for i, dt in tok_grads]
    acc_out = [tuple(a.shape[1:]) for a in bat] + [tuple(a.shape) for a in con]
    return rowcall(name, fn, all_tok, bat, con, tok_out, acc_out, ts)


def _rms(y, w):
    return y * lax.rsqrt(jnp.mean(y * y, axis=-1, keepdims=True) + RMS_EPS) * w


def f_rms_mod(t, b, c):
    return [_rms(t[0], c[0]) * (1.0 + b[0]) + b[1]]


def f_resid(t, b, c):
    return [t[0] + b[0] * _rms(t[1], c[0])]


def f_merge(t, b, c):
    ga, gd, ya, yd = t
    return [jax.nn.sigmoid(ga) * ya + jax.nn.sigmoid(gd) * yd]


def f_dnout(t, b, c):
    o, z = t
    outs = []
    for h in range(DNH):
        sl = slice(h * DND, (h + 1) * DND)
        zh = z[:, sl]
        outs.append(_rms(o[:, sl], c[0]) * (zh * jax.nn.sigmoid(zh)))
    return [jnp.concatenate(outs, axis=1)]


def _softplus(x):
    return jnp.maximum(x, 0.0) + jnp.log(1.0 + jnp.exp(-jnp.abs(x)))


def f_gate(t, b, c):
    ba = t[0]
    a_log, dt_bias = c
    lane = lax.broadcasted_iota(jnp.int32, ba.shape, 1)
    beta = jax.nn.sigmoid(ba)
    g = -jnp.exp(a_log) * _softplus(ba + dt_bias)
    return [jnp.where(lane < DNH, beta, jnp.where(lane < 2 * DNH, g, 0.0))]


def _bucket_table():
    qi = np.arange(WIN)[:, None]
    kj = np.arange(2 * WIN)[None, :]
    dist = np.maximum(WIN + qi - kj, 0)
    max_exact = NBUCK // 2
    scaled = np.log(np.maximum(dist, 1).astype(np.float64) / max_exact) / math.log(MAXDIST / max_exact)
    large = np.minimum(max_exact + (scaled * (NBUCK - max_exact)).astype(np.int32), NBUCK - 1)
    return np.where(dist < max_exact, dist, large).astype(np.int32)


def _attn_mask(n):
    qi = lax.broadcasted_iota(jnp.int32, (WIN, 2 * WIN), 0)
    kj = lax.broadcasted_iota(jnp.int32, (WIN, 2 * WIN), 1)
    dist = WIN + qi - kj
    return (dist >= 0) & (dist < WIN) & ((kj >= WIN) | (n > 0))


def _attn_block(q, kp, kc, vp, vc, bias, sinks, mask):
    grp = HQ // HKV
    outs = []
    for j in range(HKV):
        sl = slice(j * LANE, (j + 1) * LANE)
        kb = jnp.concatenate([kp[:, sl], kc[:, sl]], axis=0).astype(bf16)
        vb = jnp.concatenate([vp[:, sl], vc[:, sl]], axis=0).astype(bf16)
        for g in range(grp):
            h = j * grp + g
            qh = q[:, h * LANE:(h + 1) * LANE].astype(bf16)
            s = lax.dot_general(qh, kb, (((1,), (1,)), ((), ())), preferred_element_type=f32) * (HD ** -0.5)
            s = jnp.where(mask, s + bias[h], NEG_INF)
            sink = sinks[h]
            m = jnp.maximum(jnp.max(s, axis=-1, keepdims=True), sink)
            p = jnp.exp(s - m)
            probs = p / (jnp.sum(p, axis=-1, keepdims=True) + jnp.exp(sink - m))
            outs.append(jnp.dot(probs.astype(bf16), vb, preferred_element_type=f32))
    return jnp.concatenate(outs, axis=1)


def _attn_specs(NB):
    last = NB - 1
    return [
        pl.BlockSpec((1, WIN, HQ * LANE), lambda b, n: (b, jnp.minimum(n, last), CB_AQ // 8)),
        pl.BlockSpec((1, WIN, HKV * LANE), lambda b, n: (b, jnp.clip(n - 1, 0, last), CB_AK // 2)),
        pl.BlockSpec((1, WIN, HKV * LANE), lambda b, n: (b, jnp.minimum(n, last), CB_AK // 2)),
        pl.BlockSpec((1, WIN, HKV * LANE), lambda b, n: (b, jnp.clip(n - 1, 0, last), CB_AV // 2)),
        pl.BlockSpec((1, WIN, HKV * LANE), lambda b, n: (b, jnp.minimum(n, last), CB_AV // 2)),
        pl.BlockSpec((HQ, WIN, 2 * WIN), lambda b, n: (0, 0, 0)),
        pl.BlockSpec((HQ, 1, 1), lambda b, n: (0, 0, 0)),
    ]


def attn_fwd(proj, bias, sinks):
    B, S, _ = proj.shape
    NB = S // WIN

    def body(q, kp, kc, vp, vc, bias_ref, sink_ref, o_ref):
        mask = _attn_mask(pl.program_id(1))
        o = _attn_block(q[0], kp[0], kc[0], vp[0], vc[0], bias_ref[...], sink_ref[...], mask)
        o_ref[0] = o.astype(o_ref.dtype)

    return pl.pallas_call(
        body, name="attn_fwd", grid=(B, NB), in_specs=_attn_specs(NB),
        out_specs=pl.BlockSpec((1, WIN, HQ * LANE), lambda b, n: (b, n, 0)),
        out_shape=jax.ShapeDtypeStruct((B, S, HQ * LANE), bf16), compiler_params=_cp(("parallel", "parallel")),
    )(proj, proj, proj, proj, proj, bias, sinks)


def attn_bwd(proj, bias, sinks, dy):
    B, S, _ = proj.shape
    NB = S // WIN
    last = NB - 1

    def body(q, kp, kc, vp, vc, bias_ref, sink_ref, dy_ref, dq_ref, dk_ref, dv_ref, dbias_ref, dsink_ref, kcar, vcar):
        b, n = pl.program_id(0), pl.program_id(1)

        @pl.when((b == 0) & (n == 0))
        def _():
            dbias_ref[...] = jnp.zeros(dbias_ref.shape, f32)
            dsink_ref[...] = jnp.zeros(dsink_ref.shape, f32)

        @pl.when(n == 0)
        def _():
            kcar[...] = jnp.zeros(kcar.shape, f32)
            vcar[...] = jnp.zeros(vcar.shape, f32)

        @pl.when(n < NB)
        def _():
            mask = _attn_mask(n)
            _, vjp = jax.vjp(lambda *a: _attn_block(*a, mask), q[0], kp[0], kc[0], vp[0], vc[0], bias_ref[...], sink_ref[...])
            dq, dkp, dkc, dvp, dvc, dbias, dsink = vjp(dy_ref[0].astype(f32))
            dq_ref[0] = dq.astype(dq_ref.dtype)
            dbias_ref[...] += dbias
            dsink_ref[...] += dsink
            dk_ref[0] = (kcar[...] + dkp).astype(dk_ref.dtype)
            dv_ref[0] = (vcar[...] + dvp).astype(dv_ref.dtype)
            kcar[...] = dkc
            vcar[...] = dvc

        @pl.when(n == NB)
        def _():
            dk_ref[0] = kcar[...].astype(dk_ref.dtype)
            dv_ref[0] = vcar[...].astype(dv_ref.dtype)

    in_specs = _attn_specs(NB) + [pl.BlockSpec((1, WIN, HQ * LANE), lambda b, n: (b, jnp.minimum(n, last), 0))]
    kv_out = pl.BlockSpec((1, WIN, HKV * LANE), lambda b, n: (b, jnp.maximum(n - 1, 0), 0))
    return pl.pallas_call(
        body, name="attn_bwd", grid=(B, NB + 1), in_specs=in_specs,
        out_specs=[pl.BlockSpec((1, WIN, HQ * LANE), lambda b, n: (b, jnp.minimum(n, last), 0)), kv_out, kv_out,
                   pl.BlockSpec((HQ, WIN, 2 * WIN), lambda b, n: (0, 0, 0)), pl.BlockSpec((HQ, 1, 1), lambda b, n: (0, 0, 0))],
        out_shape=[jax.ShapeDtypeStruct((B, S, HQ * LANE), bf16), jax.ShapeDtypeStruct((B, S, HKV * LANE), bf16),
                   jax.ShapeDtypeStruct((B, S, HKV * LANE), bf16), jax.ShapeDtypeStruct((HQ, WIN, 2 * WIN), f32),
                   jax.ShapeDtypeStruct((HQ, 1, 1), f32)],
        scratch_shapes=[pltpu.VMEM((WIN, HKV * LANE), f32), pltpu.VMEM((WIN, HKV * LANE), f32)],
        compiler_params=_cp(("arbitrary", "arbitrary")),
    )(proj, proj, proj, proj, proj, bias, sinks, dy)


def _causal_conv(x, w, width):
    S, C = x.shape
    xp = jnp.concatenate([jnp.zeros((8, C), f32), x], axis=0)
    out = None
    for j in range(width):
        off = 8 - (width - 1) + j
        term = w[j:j + 1, :] * xp[off:off + S, :]
        out = term if out is None else out + term
    return out


def _dnconv_f(x, w, isqk):
    y = _causal_conv(x, w, DNK)
    y = y * jax.nn.sigmoid(y)
    yn = y * lax.rsqrt(jnp.sum(y * y, axis=-1, keepdims=True) + L2_EPS)
    return jnp.where(isqk, yn, y)


def _dn_outblk(i):
    return (i % DNH) * 3 + i // DNH


def dnconv_fwd(proj, conv_w):
    B, S, _ = proj.shape

    def body(x_ref, w_ref, o_ref):
        o_ref[0] = _dnconv_f(x_ref[0], w_ref[...], pl.program_id(0) < 2 * DNH)

    return pl.pallas_call(
        body, name="dnconv_fwd", grid=(3 * DNH, B),
        in_specs=[pl.BlockSpec((1, S, LANE), lambda i, b: (b, 0, CB_DQKV + i)), pl.BlockSpec((DNK, LANE), lambda i, b: (0, i))],
        out_specs=pl.BlockSpec((1, S, LANE), lambda i, b: (b, 0, _dn_outblk(i))),
        out_shape=jax.ShapeDtypeStruct((B, S, 3 * DNH * DND), f32), compiler_params=_cp(("parallel", "parallel")),
    )(proj, conv_w)


def dnconv_bwd(proj, conv_w, dqkvn):
    B, S, _ = proj.shape

    def body(x_ref, w_ref, dy_ref, dx_ref, dw_ref):
        isqk = pl.program_id(0) < 2 * DNH
        _, vjp = jax.vjp(lambda x, w: _dnconv_f(x, w, isqk), x_ref[0], w_ref[...])
        dx, dw = vjp(dy_ref[0])
        dx_ref[0] = dx.astype(dx_ref.dtype)

        @pl.when(pl.program_id(1) == 0)
        def _():
            dw_ref[...] = jnp.zeros(dw_ref.shape, f32)
        dw_ref[...] += dw

    return pl.pallas_call(
        body, name="dnconv_bwd", grid=(3 * DNH, B),
        in_specs=[pl.BlockSpec((1, S, LANE), lambda i, b: (b, 0, CB_DQKV + i)), pl.BlockSpec((DNK, LANE), lambda i, b: (0, i)),
                  pl.BlockSpec((1, S, LANE), lambda i, b: (b, 0, _dn_outblk(i)))],
        out_specs=[pl.BlockSpec((1, S, LANE), lambda i, b: (b, 0, i)), pl.BlockSpec((DNK, LANE), lambda i, b: (0, i))],
        out_shape=[jax.ShapeDtypeStruct((B, S, 3 * DNH * DND), bf16), jax.ShapeDtypeStruct((DNK, 3 * DNH * DND), f32)],
        compiler_params=_cp(("parallel", "arbitrary")),
    )(proj, conv_w, dqkvn)


def _bdot(a, b, ca, cb, precision=HI):
    return lax.dot_general(a, b, (((ca,), (cb,)), ((0,), (0,))), preferred_element_type=f32, precision=precision)


def _bdot_bf16(a, b, ca, cb):
    return _bdot(a.astype(bf16), b.astype(bf16), ca, cb, None)


@functools.partial(jax.custom_vjp, nondiff_argnums=(2, 3))
def _bdot_bf16_vjp(a, b, ca, cb):
    return _bdot_bf16(a, b, ca, cb)


def _bdot_bf16_fwd(a, b, ca, cb):
    return _bdot_bf16(a, b, ca, cb), (a, b)


def _bdot_bf16_bwd(ca, cb, res, g):
    a, b = res
    fa, fb = 3 - ca, 3 - cb
    da = _bdot_bf16(g, b, 2, fb) if ca == 2 else _bdot_bf16(b, g, fb, 2)
    db = _bdot_bf16(a, g, fa, 1) if cb == 1 else _bdot_bf16(g, a, 1, fa)
    return da, db


_bdot_bf16_vjp.defvjp(_bdot_bf16_fwd, _bdot_bf16_bwd)


def _neumann_inverse(low):
    n = low.shape[-1]
    eye = (lax.broadcasted_iota(jnp.int32, (n, n), 0) == lax.broadcasted_iota(jnp.int32, (n, n), 1)).astype(f32)
    p = -low
    x = eye[None] + p
    for _ in range(5):
        p = _bdot(p, p, 2, 1, MID)
        x = x + _bdot(x, p, 2, 1, MID)
    return x


@jax.custom_vjp
def _unit_lower_inverse(low):
    return _neumann_inverse(low)


def _uli_fwd(low):
    t = _neumann_inverse(low)
    return t, t


def _uli_bwd(t, dt):
    return (-_bdot(_bdot(t, dt, 1, 1, MID), t, 2, 2, MID),)


_unit_lower_inverse.defvjp(_uli_fwd, _uli_bwd)


def _stack(xs):
    return jnp.concatenate([x[None] for x in xs], axis=0)


def _delta_chunk(qkv, bg, state, differentiated):
    inverse = _unit_lower_inverse if differentiated else _neumann_inverse
    lo = _bdot_bf16_vjp if differentiated else _bdot_bf16
    B = qkv.shape[0]
    G = B * DNH
    pairs = [(b, h) for b in range(B) for h in range(DNH)]
    col = lambda b, h, kind: qkv[b, :, (3 * h + kind) * DND:(3 * h + kind + 1) * DND]
    q, k, v = [_stack([col(b, h, kind) for b, h in pairs]) for kind in range(3)]
    lane = lax.broadcasted_iota(jnp.int32, (CH, LANE), 1)
    pick = lambda b, l: jnp.sum(jnp.where(lane == l, bg[b], 0.0), axis=1, keepdims=True)
    beta = _stack([pick(b, h) for b, h in pairs])
    g = _stack([pick(b, h + DNH) for b, h in pairs])
    ri = lax.broadcasted_iota(jnp.int32, (CH, CH), 0)
    ci = lax.broadcasted_iota(jnp.int32, (CH, CH), 1)
    incl, strict = (ri >= ci)[None], (ri > ci)[None]
    gc = _bdot(jnp.broadcast_to(incl.astype(f32), (G, CH, CH)), jnp.broadcast_to(g, (G, CH, LANE)), 2, 1)
    e0 = jnp.broadcast_to((lane == 0).astype(f32)[None], (G, CH, LANE))
    gc_row = _bdot(e0, gc, 2, 2)
    diff = gc[:, :, :CH] - gc_row
    decay = jnp.where(incl, jnp.exp(jnp.where(incl, diff, 0.0)), 0.0)
    qs = q * (DND ** -0.5)
    kb, vb = k * beta, v * beta
    eg = jnp.exp(gc)
    low = jnp.where(strict, lo(kb, k, 2, 2) * decay, 0.0)
    tinv = inverse(low)
    u = _bdot(tinv, vb, 2, 1, MID)
    w = _bdot(tinv, kb * eg, 2, 1, MID)
    intra = jnp.where(incl, lo(qs, k, 2, 2) * decay, 0.0)
    gl = gc[:, CH - 1:CH, :]
    k_tail = k * jnp.exp(gl - gc)
    v_new = u - lo(w, state, 2, 1)
    o = lo(qs * eg, state, 2, 1) + lo(intra, v_new, 2, 1)
    new_state = state * jnp.exp(gl) + lo(k_tail, v_new, 1, 1)
    return o, new_state


def delta_fwd(qkvn, bg):
    B, S, _ = qkvn.shape
    NC, G = S // CH, B * DNH

    def body(qkv_ref, bg_ref, o_ref, st_ref, state):
        @pl.when(pl.program_id(0) == 0)
        def _():
            state[...] = jnp.zeros(state.shape, f32)
        s0 = state[...]
        st_ref[0] = s0
        o, s1 = _delta_chunk(qkv_ref[...], bg_ref[...], s0, False)
        for b in range(B):
            for h in range(DNH):
                o_ref[b, :, h * DND:(h + 1) * DND] = o[b * DNH + h]
        state[...] = s1

    return pl.pallas_call(
        body, name="delta_fwd", grid=(NC,),
        in_specs=[pl.BlockSpec((B, CH, 3 * DNH * DND), lambda c: (0, c, 0)), pl.BlockSpec((B, CH, LANE), lambda c: (0, c, 0))],
        out_specs=[pl.BlockSpec((B, CH, DNH * DND), lambda c: (0, c, 0)), pl.BlockSpec((1, G, DND, DND), lambda c: (c, 0, 0, 0))],
        out_shape=[jax.ShapeDtypeStruct((B, S, DNH * DND), f32), jax.ShapeDtypeStruct((NC, G, DND, DND), f32)],
        scratch_shapes=[pltpu.VMEM((G, DND, DND), f32)], compiler_params=_cp(("arbitrary",)),
    )(qkvn, bg)


def delta_bwd(qkvn, bg, states, do):
    B, S, _ = qkvn.shape
    NC, G = S // CH, B * DNH

    def body(qkv_ref, bg_ref, st_ref, do_ref, dqkv_ref, dbg_ref, dstate):
        @pl.when(pl.program_id(0) == 0)
        def _():
            dstate[...] = jnp.zeros(dstate.shape, f32)
        _, vjp = jax.vjp(lambda a, g, s: _delta_chunk(a, g, s, True), qkv_ref[...], bg_ref[...], st_ref[0])
        do = _stack([do_ref[b, :, h * DND:(h + 1) * DND] for b in range(B) for h in range(DNH)])
        dqkv, dbg, ds = vjp((do, dstate[...]))
        dqkv_ref[...] = dqkv
        dbg_ref[...] = dbg
        dstate[...] = ds

    rev = lambda c: NC - 1 - c
    return pl.pallas_call(
        body, name="delta_bwd", grid=(NC,),
        in_specs=[pl.BlockSpec((B, CH, 3 * DNH * DND), lambda c: (0, rev(c), 0)), pl.BlockSpec((B, CH, LANE), lambda c: (0, rev(c), 0)),
                  pl.BlockSpec((1, G, DND, DND), lambda c: (rev(c), 0, 0, 0)),
                  pl.BlockSpec((B, CH, DNH * DND), lambda c: (0, rev(c), 0))],
        out_specs=[pl.BlockSpec((B, CH, 3 * DNH * DND), lambda c: (0, rev(c), 0)), pl.BlockSpec((B, CH, LANE), lambda c: (0, rev(c), 0))],
        out_shape=[jax.ShapeDtypeStruct((B, S, 3 * DNH * DND), f32), jax.ShapeDtypeStruct((B, S, LANE), f32)],
        scratch_shapes=[pltpu.VMEM((G, DND, DND), f32)], compiler_params=_cp(("arbitrary",)),
    )(qkvn, bg, states, do)


def _ffn_f(blk, w):
    u = _causal_conv(blk, w, FK)
    gate, val = u[:, :LANE], u[:, LANE:]
    gl = 0.5 * gate * (1.0 + jnp.tanh(math.sqrt(2.0 / math.pi) * (gate + 0.044715 * gate * gate * gate)))
    return gl * val


def ffnconv_fwd(up, conv_w):
    B, S, _ = up.shape
    nblk = DFF // LANE

    def body(x_ref, w_ref, o_ref):
        o_ref[0] = _ffn_f(x_ref[0], w_ref[...]).astype(o_ref.dtype)

    return pl.pallas_call(
        body, name="ffnconv_fwd", grid=(nblk, B),
        in_specs=[pl.BlockSpec((1, S, 2 * LANE), lambda i, b: (b, 0, i)), pl.BlockSpec((FK, 2 * LANE), lambda i, b: (0, i))],
        out_specs=pl.BlockSpec((1, S, LANE), lambda i, b: (b, 0, i)),
        out_shape=jax.ShapeDtypeStruct((B, S, DFF), bf16), compiler_params=_cp(("parallel", "parallel")),
    )(up, conv_w)


def ffnconv_bwd(up, conv_w, dact):
    B, S, _ = up.shape
    nblk = DFF // LANE

    def body(x_ref, w_ref, dy_ref, dx_ref, dw_ref):
        _, vjp = jax.vjp(_ffn_f, x_ref[0], w_ref[...])
        dx, dw = vjp(dy_ref[0].astype(f32))
        dx_ref[0] = dx.astype(dx_ref.dtype)

        @pl.when(pl.program_id(1) == 0)
        def _():
            dw_ref[...] = jnp.zeros(dw_ref.shape, f32)
        dw_ref[...] += dw

    return pl.pallas_call(
        body, name="ffnconv_bwd", grid=(nblk, B),
        in_specs=[pl.BlockSpec((1, S, 2 * LANE), lambda i, b: (b, 0, i)), pl.BlockSpec((FK, 2 * LANE), lambda i, b: (0, i)),
                  pl.BlockSpec((1, S, LANE), lambda i, b: (b, 0, i))],
        out_specs=[pl.BlockSpec((1, S, 2 * LANE), lambda i, b: (b, 0, i)), pl.BlockSpec((FK, 2 * LANE), lambda i, b: (0, i))],
        out_shape=[jax.ShapeDtypeStruct((B, S, 2 * DFF), bf16), jax.ShapeDtypeStruct((FK, 2 * DFF), f32)],
        compiler_params=_cp(("parallel", "arbitrary")),
    )(up, conv_w, dact)


def ada_fwd(c_all, ada_w, ada_b):
    def body(c_ref, w_ref, b_ref, o_ref):
        c = c_ref[...]
        act = (c * jax.nn.sigmoid(c)).astype(bf16)
        o_ref[...] = jnp.dot(act, w_ref[...].astype(bf16), preferred_element_type=f32) + b_ref[...]

    return pl.pallas_call(body, name="ada_fwd", out_shape=jax.ShapeDtypeStruct((c_all.shape[0], ada_w.shape[1]), f32),
                          compiler_params=pltpu.CompilerParams(vmem_limit_bytes=VMEM_LIMIT))(c_all, ada_w, ada_b)


def ada_bwd(c_all, dmod):
    def body(c_ref, d_ref, o_ref):
        c = c_ref[...]
        act = (c * jax.nn.sigmoid(c)).astype(bf16)
        o_ref[...] = lax.dot_general(act, d_ref[...].astype(bf16), (((0,), (0,)), ((), ())), preferred_element_type=f32)

    return pl.pallas_call(body, name="ada_bwd", out_shape=jax.ShapeDtypeStruct((c_all.shape[1], dmod.shape[1]), f32),
                          compiler_params=pltpu.CompilerParams(vmem_limit_bytes=VMEM_LIMIT))(c_all, dmod)


def loss_head(h1, y2, target, g2, w):
    def fn(t, b, c):
        h, y, tg = [v.astype(f32) for v in t]

        def loss_fn(h, y, g, w):
            e = h + g * _rms(y, w) - tg
            return 0.5 * jnp.sum(jnp.mean(e * e, axis=-1))

        loss, grads = jax.value_and_grad(loss_fn, argnums=(0, 1, 2, 3))(h, y, b[0], c[0])
        return [grads[0], grads[1]], [grads[2], grads[3], jnp.full((1, LANE), loss, f32)]

    return rowcall("loss_head", fn, [(h1, D, 0), (y2, D, 0), (target, D, 0)], [g2], [w], [(D, f32), (D, bf16)],
                   [(1, D), (1, D), (1, LANE)])


def adamw(w, gparts, m, v, name):
    R, C = w.shape
    P = gparts.shape[0]
    tr = R
    if R * C * 4 > 2 * 1024 * 1024:
        for cand in (512, 256, 128, 64, 32, 16, 8):
            if R % cand == 0 and cand * C * 4 <= 2 * 1024 * 1024:
                tr = cand
                break

    def body(w_ref, g_ref, m_ref, v_ref, go, do, mo, vo):
        g = g_ref[0].astype(f32)
        for p in range(1, P):
            g = g + g_ref[p].astype(f32)
        m2 = B1 * m_ref[...] + (1.0 - B1) * g
        v2 = B2 * v_ref[...] + (1.0 - B2) * jnp.square(g)
        m_hat = m2 / (1.0 - B1 ** STEP)
        v_hat = v2 / (1.0 - B2 ** STEP)
        go[...] = g
        do[...] = -LR * (m_hat / (jnp.sqrt(v_hat) + EPS) + WD * w_ref[...])
        mo[...] = m2
        vo[...] = v2

    blk = pl.BlockSpec((tr, C), lambda i: (i, 0))
    return pl.pallas_call(
        body, name=name, grid=(R // tr,), in_specs=[blk, pl.BlockSpec((P, tr, C), lambda i: (0, i, 0)), blk, blk],
        out_specs=[blk] * 4, out_shape=[jax.ShapeDtypeStruct((R, C), f32)] * 4, compiler_params=_cp(("parallel",)),
    )(w, gparts, m, v)


def _pad_heads(w, nh):
    r = w.shape[0]
    return jnp.pad(w.reshape(r, nh, HD), ((0, 0), (0, 0), (0, LANE - HD))).reshape(r, nh * LANE)


def _unpad_heads(w, nh):
    return w.reshape(w.shape[0], nh, LANE)[:, :, :HD].reshape(w.shape[0], nh * HD)


def _pack_w_in(w):
    aq, ak, av, dqkv, dz, dbeta, da, ga, gd = jnp.split(w, np.cumsum(IN_SPLITS)[:-1].tolist(), axis=1)
    ba = jnp.pad(jnp.concatenate([dbeta, da], axis=1), ((0, 0), (0, LANE - 2 * DNH)))
    return jnp.concatenate([ga, gd, _pad_heads(aq, HQ), dqkv, dz, _pad_heads(ak, HKV), _pad_heads(av, HKV), ba], axis=1)


def _unpack_w_in(p):
    col = lambda cb, n: p[:, cb * LANE: cb * LANE + n]
    ba = col(CB_BA, 2 * DNH)
    return jnp.concatenate([_unpad_heads(col(CB_AQ, HQ * LANE), HQ), _unpad_heads(col(CB_AK, HKV * LANE), HKV),
                            _unpad_heads(col(CB_AV, HKV * LANE), HKV), col(CB_DQKV, 3 * DNH * DND), col(CB_DZ, DNH * DND),
                            ba[:, :DNH], ba[:, DNH:], col(CB_GA, D), col(CB_GD, D)], axis=1)


def _interleave(w):
    r = w.shape[0]
    return w.reshape(r, 2, DFF // LANE, LANE).transpose(0, 2, 1, 3).reshape(r, 2 * DFF)


def _deinterleave(w):
    r = w.shape[0]
    return w.reshape(r, DFF // LANE, 2, LANE).transpose(0, 2, 1, 3).reshape(r, 2 * DFF)


def _cols_gathered(g):
    return g.transpose(1, 0, 2).reshape(g.shape[1], NDEV * g.shape[2])


def _cols_split(w):
    r = w.shape[0]
    return w.reshape(r, NDEV, w.shape[1] // NDEV).transpose(1, 0, 2)


def kernel(x, c, ada_w, ada_b, norm_mix_pre, norm_mix_post, norm_ffn_pre, norm_ffn_post, w_in, dn_conv_w, dn_a_log, dn_dt_bias, dn_norm_w, attn_sinks, rel_bias, w_attn_branch, w_dn_branch, w_out, ffn_w_up, ffn_conv_w, ffn_w_down, loss_target, m_ada_w, m_ada_b, m_norm_mix_pre, m_norm_mix_post, m_norm_ffn_pre, m_norm_ffn_post, m_w_in, m_dn_conv_w, m_dn_a_log, m_dn_dt_bias, m_dn_norm_w, m_attn_sinks, m_rel_bias, m_w_attn_branch, m_w_dn_branch, m_w_out, m_ffn_w_up, m_ffn_conv_w, m_ffn_w_down, v_ada_w, v_ada_b, v_norm_mix_pre, v_norm_mix_post, v_norm_ffn_pre, v_norm_ffn_post, v_w_in, v_dn_conv_w, v_dn_a_log, v_dn_dt_bias, v_dn_norm_w, v_attn_sinks, v_rel_bias, v_w_attn_branch, v_w_dn_branch, v_w_out, v_ffn_w_up, v_ffn_conv_w, v_ffn_w_down):
    B, S, _ = x.shape
    T = B * S
    me = 4 * lax.axis_index("x") + 2 * lax.axis_index("y") + lax.axis_index("c")
    big = dict(w_in=w_in, dn_conv_w=dn_conv_w, w_attn_branch=w_attn_branch, w_dn_branch=w_dn_branch, w_out=w_out,
               ffn_w_up=ffn_w_up, ffn_conv_w=ffn_conv_w, ffn_w_down=ffn_w_down)
    big_names = list(big)

    gathered = _exchange([big[n][0].astype(bf16) for n in big_names], "gather_weights", scatter=False)
    gw = dict(zip(big_names, gathered))
    (c_all,) = _exchange([c], "gather_c", scatter=False)
    c_all = c_all.reshape(NDEV * B, D)

    wp = _pack_w_in(_cols_gathered(gw["w_in"]))
    conv_dn = _cols_gathered(gw["dn_conv_w"]).astype(f32)
    wa = _cols_gathered(gw["w_attn_branch"])
    wa = jnp.pad(wa.reshape(HQ, HD, D), ((0, 0), (0, LANE - HD), (0, 0))).reshape(HQ * LANE, D)
    wd = _cols_gathered(gw["w_dn_branch"])
    wo = gw["w_out"].reshape(D, D)
    wup = _interleave(_cols_gathered(gw["ffn_w_up"]))
    conv_ffn = _interleave(_cols_gathered(gw["ffn_conv_w"]).astype(f32))
    wdown = gw["ffn_w_down"].reshape(DFF, D)

    ncol = ada_w.shape[2]
    ada_b_mine = lax.dynamic_slice_in_dim(ada_b, me * ncol, ncol, axis=1)
    mod_cols = ada_fwd(c_all, ada_w[0], ada_b_mine)
    (mod_g,) = _exchange([mod_cols], "gather_mod", scatter=False)
    mod = lax.dynamic_slice_in_dim(mod_g, me * B, B, axis=1).transpose(1, 0, 2).reshape(B, NMOD * D)
    sh1, sc1, g1, sh2, sc2, g2 = [mod[:, i * D:(i + 1) * D].reshape(B, 1, D) for i in range(NMOD)]

    onehot = (jnp.asarray(_bucket_table()).reshape(1, -1) == jnp.arange(NBUCK, dtype=jnp.int32)[:, None]).astype(f32)
    bias = mm(rel_bias.T, onehot, "nn", f32, "bias_table", tn=8192, precision=HI).reshape(HQ, WIN, 2 * WIN)
    sinks = attn_sinks.reshape(HQ, 1, 1)
    a_log_pad = jnp.pad(dn_a_log, ((0, 0), (DNH, LANE - 2 * DNH)))
    dt_bias_pad = jnp.pad(dn_dt_bias, ((0, 0), (DNH, LANE - 2 * DNH)))

    (u1,) = rowcall_fwd("mix_pre", f_rms_mod, [(x, D, 0)], [sc1, sh1], [norm_mix_pre], [(D, bf16)])
    proj = mm(u1.reshape(T, D), wp, "nn", f32, "proj", tn=1152).reshape(B, S, NP)
    ya = attn_fwd(proj, bias, sinks)
    qkvn = dnconv_fwd(proj, conv_dn)
    (bg,) = rowcall_fwd("dn_gate", f_gate, [(proj, LANE, CB_BA)], [], [a_log_pad, dt_bias_pad], [(LANE, f32)])
    o_dn, states = delta_fwd(qkvn, bg)
    (yd,) = rowcall_fwd("dn_out", f_dnout, [(o_dn, DNH * DND, 0), (proj, DNH * DND, CB_DZ // 4)], [], [dn_norm_w], [(DNH * DND, bf16)])
    pa = mm(ya.reshape(T, HQ * LANE), wa, "nn", f32, "attn_branch").reshape(B, S, D)
    pd = mm(yd.reshape(T, DNH * DND), wd, "nn", f32, "dn_branch").reshape(B, S, D)
    merge_tok = [(proj, D, CB_GA // 8), (proj, D, CB_GD // 8), (pa, D, 0), (pd, D, 0)]
    (merged,) = rowcall_fwd("merge", f_merge, merge_tok, [], [], [(D, bf16)])
    y1 = mm(merged.reshape(T, D), wo, "nn", f32, "mix_out").reshape(B, S, D)
    (h1,) = rowcall_fwd("mix_post", f_resid, [(x, D, 0), (y1, D, 0)], [g1], [norm_mix_post], [(D, f32)])
    (u2,) = rowcall_fwd("ffn_pre", f_rms_mod, [(h1, D, 0)], [sc2, sh2], [norm_ffn_pre], [(D, bf16)])
    up = mm(u2.reshape(T, D), wup, "nn", f32, "ffn_up", tn=1408).reshape(B, S, 2 * DFF)
    act = ffnconv_fwd(up, conv_ffn)
    y2 = mm(act.reshape(T, DFF), wdown, "nn", f32, "ffn_down", tk=1408).reshape(B, S, D)

    dh1_a, dy2, dg2, dw_ffn_post, loss_b = loss_head(h1, y2, loss_target, g2, norm_ffn_post)
    dy2f = dy2.reshape(T, D)
    dact = mm(dy2f, wdown, "nt", bf16, "ffn_down_dx", tn=1408).reshape(B, S, DFF)
    g_wdown = mm(act.reshape(T, DFF), dy2f, "tn", f32, "ffn_down_dw", tm=1408, tk=2048)
    dup, g_conv_ffn = ffnconv_bwd(up, conv_ffn, dact)
    dupf = dup.reshape(T, 2 * DFF)
    du2 = mm(dupf, wup, "nt", f32, "ffn_up_dx", tk=1408).reshape(B, S, D)
    g_wup = mm(u2.reshape(T, D), dupf, "tn", f32, "ffn_up_dw", tn=1408, tk=2048)
    dh1, dsc2, dsh2, dw_ffn_pre = rowcall_bwd("ffn_pre_bwd", f_rms_mod, [(h1, D, 0)], [sc2, sh2], [norm_ffn_pre], [(du2, D, 0)],
                                              [(0, f32)], add=(dh1_a, D, 0))
    dy1, dg1, dw_mix_post = rowcall_bwd("mix_post_bwd", f_resid, [(x, D, 0), (y1, D, 0)], [g1], [norm_mix_post], [(dh1, D, 0)],
                                        [(1, bf16)])
    dy1f = dy1.reshape(T, D)
    dmerged = mm(dy1f, wo, "nt", f32, "mix_out_dx").reshape(B, S, D)
    g_wo = mm(merged.reshape(T, D), dy1f, "tn", f32, "mix_out_dw", tk=2048)
    dga, dgd, dpa, dpd = rowcall_bwd("merge_bwd", f_merge, merge_tok, [], [], [(dmerged, D, 0)],
                                     [(0, bf16), (1, bf16), (2, bf16), (3, bf16)])
    dpaf, dpdf = dpa.reshape(T, D), dpd.reshape(T, D)
    dya = mm(dpaf, wa, "nt", bf16, "attn_branch_dx").reshape(B, S, HQ * LANE)
    g_wa = mm(ya.reshape(T, HQ * LANE), dpaf, "tn", f32, "attn_branch_dw", tk=2048)
    dyd = mm(dpdf, wd, "nt", f32, "dn_branch_dx").reshape(B, S, DNH * DND)
    g_wd = mm(yd.reshape(T, DNH * DND), dpdf, "tn", f32, "dn_branch_dw", tk=2048)
    do_dn, dz, dw_dn_norm = rowcall_bwd("dn_out_bwd", f_dnout, [(o_dn, DNH * DND, 0), (proj, DNH * DND, CB_DZ // 4)], [], [dn_norm_w],
                                        [(dyd, DNH * DND, 0)], [(0, f32), (1, bf16)])
    dqkvn, dbg = delta_bwd(qkvn, bg, states, do_dn)
    dba, da_log_pad, ddt_bias_pad = rowcall_bwd("dn_gate_bwd", f_gate, [(proj, LANE, CB_BA)], [], [a_log_pad, dt_bias_pad],
                                                [(dbg, LANE, 0)], [(0, bf16)])
    ddqkv, g_conv_dn = dnconv_bwd(proj, conv_dn, dqkvn)
    dq, dk, dv, dbias, dsinks = attn_bwd(proj, bias, sinks, dya)
    dproj = jnp.concatenate([dga, dgd, dq, ddqkv, dz, dk, dv, dba], axis=2).reshape(T, NP)
    du1 = mm(dproj, wp, "nt", f32, "proj_dx", tk=1152).reshape(B, S, D)
    g_wp = mm(u1.reshape(T, D), dproj, "tn", f32, "proj_dw", tn=1152, tk=2048)
    grad_x, dsc1, dsh1, dw_mix_pre = rowcall_bwd("mix_pre_bwd", f_rms_mod, [(x, D, 0)], [sc1, sh1], [norm_mix_pre], [(du1, D, 0)],
                                                 [(0, f32)], add=(dh1, D, 0))
    g_rel = mm(dbias.reshape(HQ, WIN * 2 * WIN), onehot, "nt", f32, "rel_bias_dw", tk=8192, precision=HI)

    dmod = jnp.concatenate([dsh1, dsc1, dg1, dsh2, dsc2, dg2], axis=2).reshape(B, NMOD * D)
    (dmod_g,) = _exchange([dmod], "gather_dmod", scatter=False)
    dmod_cols = lax.dynamic_slice_in_dim(dmod_g.reshape(NDEV * B, NMOD * D), me * ncol, ncol, axis=1)
    g_ada_w = ada_bwd(c_all, dmod_cols)

    send = dict(
        w_in=_cols_split(_unpack_w_in(g_wp)), dn_conv_w=_cols_split(g_conv_dn),
        w_attn_branch=_cols_split(g_wa.reshape(HQ, LANE, D)[:, :HD].reshape(HQ * HD, D)), w_dn_branch=_cols_split(g_wd),
        w_out=g_wo.reshape(NDEV, D // NDEV, D), ffn_w_up=_cols_split(_deinterleave(g_wup)),
        ffn_conv_w=_cols_split(_deinterleave(g_conv_ffn)), ffn_w_down=g_wdown.reshape(NDEV, DFF // NDEV, D))
    parts = dict(zip(big_names, _exchange([send[n].astype(bf16) for n in big_names], "scatter_grads", scatter=True)))

    zrow = lambda a: jnp.concatenate([a.reshape(1, -1), jnp.zeros((B - 1, a.size), f32)], axis=0)
    small_g = jnp.concatenate([
        dmod, dw_mix_pre.reshape(B, D), dw_mix_post.reshape(B, D), dw_ffn_pre.reshape(B, D), dw_ffn_post.reshape(B, D),
        da_log_pad.reshape(B, LANE)[:, DNH:2 * DNH], ddt_bias_pad.reshape(B, LANE)[:, DNH:2 * DNH], dw_dn_norm.reshape(B, DND),
        zrow(dsinks), zrow(g_rel.T), loss_b.reshape(B, LANE)[:, :1], jnp.zeros((B, SMALL_PAD - SMALL_N - 1), f32)], axis=1)
    (small_all,) = _exchange([small_g], "gather_small", scatter=False)
    small_w = dict(ada_b=(ada_b, m_ada_b, v_ada_b), norm_mix_pre=(norm_mix_pre, m_norm_mix_pre, v_norm_mix_pre),
                   norm_mix_post=(norm_mix_post, m_norm_mix_post, v_norm_mix_post), norm_ffn_pre=(norm_ffn_pre, m_norm_ffn_pre, v_norm_ffn_pre),
                   norm_ffn_post=(norm_ffn_post, m_norm_ffn_post, v_norm_ffn_post), dn_a_log=(dn_a_log, m_dn_a_log, v_dn_a_log),
                   dn_dt_bias=(dn_dt_bias, m_dn_dt_bias, v_dn_dt_bias), dn_norm_w=(dn_norm_w, m_dn_norm_w, v_dn_norm_w),
                   attn_sinks=(attn_sinks, m_attn_sinks, v_attn_sinks), rel_bias=(rel_bias, m_rel_bias, v_rel_bias))

    def pack(i, fill):
        row = jnp.concatenate([small_w[n][i].reshape(1, -1) for n, _ in SMALL], axis=1)
        return jnp.pad(row, ((0, 0), (0, SMALL_PAD - SMALL_N)), constant_values=fill)

    small_out = adamw(pack(0, 0.0), small_all.reshape(NDEV * B, 1, SMALL_PAD), pack(1, 0.0), pack(2, 1.0), "adamw_small")
    loss = small_out[0][0, SMALL_N]

    res = {}
    off = 0
    for n, size in SMALL:
        shp = small_w[n][0].shape
        res[n] = [o[:, off:off + size].reshape(shp) for o in small_out]
        off += size
    res["ada_w"] = [o[None] for o in adamw(ada_w[0], g_ada_w[None], m_ada_w[0], v_ada_w[0], "adamw_ada_w")]
    moments = dict(w_in=(m_w_in, v_w_in), dn_conv_w=(m_dn_conv_w, v_dn_conv_w), w_attn_branch=(m_w_attn_branch, v_w_attn_branch),
                   w_dn_branch=(m_w_dn_branch, v_w_dn_branch), w_out=(m_w_out, v_w_out), ffn_w_up=(m_ffn_w_up, v_ffn_w_up),
                   ffn_conv_w=(m_ffn_conv_w, v_ffn_conv_w), ffn_w_down=(m_ffn_w_down, v_ffn_w_down))
    for n in big_names:
        res[n] = [o[None] for o in adamw(big[n][0], parts[n], moments[n][0][0], moments[n][1][0], "adamw_" + n)]

    order = ["ada_w", "ada_b", "norm_mix_pre", "norm_mix_post", "norm_ffn_pre", "norm_ffn_post", "w_in", "dn_conv_w", "dn_a_log",
             "dn_dt_bias", "dn_norm_w", "attn_sinks", "rel_bias", "w_attn_branch", "w_dn_branch", "w_out", "ffn_w_up", "ffn_conv_w",
             "ffn_w_down"]
    return (loss, grad_x, *[res[n][0] for n in order], *[res[n][1] for n in order], *[res[n][2] for n in order],
            *[res[n][3] for n in order])
```

```python
import functools
import math

import numpy as np
import jax
import jax.numpy as jnp
from jax import lax
from jax.experimental import pallas as pl
from jax.experimental.pallas import tpu as pltpu

f32 = jnp.float32
bf16 = jnp.bfloat16
HI = lax.Precision.HIGHEST
MID = lax.Precision.HIGH
MESH = pl.DeviceIdType.MESH

NDEV = 8
D = 1024
HQ, HKV, HD, WIN, NBUCK, MAXDIST = 8, 2, 64, 128, 32, 128
DNH, DND, DNK, CH = 4, 128, 4, 64
DFF, FK = 2816, 3
NMOD = 6
RMS_EPS = 1e-6
L2_EPS = 1e-6
NEG_INF = -1e30
LR, B1, B2, EPS, WD, STEP = 0.001, 0.9, 0.999, 1e-08, 0.01, 10

LANE = 128
CB_GA, CB_GD, CB_AQ, CB_DQKV, CB_DZ, CB_AK, CB_AV, CB_BA, NPB = 0, 8, 16, 24, 36, 40, 42, 44, 45
NP = NPB * LANE
IN_SPLITS = (HQ * HD, HKV * HD, HKV * HD, 3 * DNH * DND, DNH * DND, DNH, DNH, D, D)
IN_DIM = sum(IN_SPLITS)
VMEM_LIMIT = 56 * 1024 * 1024

SMALL = (("ada_b", NMOD * D), ("norm_mix_pre", D), ("norm_mix_post", D), ("norm_ffn_pre", D), ("norm_ffn_post", D),
         ("dn_a_log", DNH), ("dn_dt_bias", DNH), ("dn_norm_w", DND), ("attn_sinks", HQ), ("rel_bias", NBUCK * HQ))
SMALL_N = sum(n for _, n in SMALL)
SMALL_PAD = 10752


def _cp(sem):
    return pltpu.CompilerParams(dimension_semantics=sem, vmem_limit_bytes=VMEM_LIMIT)


def _pick(dim, target):
    if dim <= target:
        return dim
    best = None
    for d in range(LANE, target + 1, LANE):
        if dim % d == 0:
            best = d
    assert best is not None, (dim, target)
    return best


def _me():
    x, y, c = lax.axis_index("x"), lax.axis_index("y"), lax.axis_index("c")
    return x, y, c, 4 * x + 2 * y + c


def _peer(x, y, c, k):
    px = 1 - x if k & 4 else x
    py = 1 - y if k & 2 else y
    pc = 1 - c if k & 1 else c
    return (px, py, pc), 4 * px + 2 * py + pc


class _Comm:
    def __init__(self, arrs, scatter=False, two_level=False):
        assert not (scatter and two_level)
        self.arrs, self.n, self.scatter, self.two_level = list(arrs), len(arrs), scatter, two_level
        if scatter:
            self.out_shape = [jax.ShapeDtypeStruct(a.shape, a.dtype) for a in arrs]
        else:
            self.out_shape = [jax.ShapeDtypeStruct((NDEV,) + a.shape, a.dtype) for a in arrs]
        nsem = self.n * (NDEV - 1)
        self.scratch = [pltpu.SemaphoreType.DMA((nsem,)), pltpu.SemaphoreType.DMA((nsem,)), pltpu.SemaphoreType.DMA((self.n,))]
        self.specs = [pl.BlockSpec(memory_space=pl.ANY)] * self.n

    def phases(self, ins, out, send, recv, loc):
        x, y, c, me = _me()

        def remote(a, k, src, dst, to):
            s = a * (NDEV - 1) + k - 1
            return pltpu.make_async_remote_copy(src_ref=src, dst_ref=dst, send_sem=send.at[s], recv_sem=recv.at[s],
                                                device_id=to, device_id_type=MESH)

        def local(a):
            return pltpu.make_async_copy(ins[a].at[me] if self.scatter else ins[a], out[a].at[me], loc.at[a])

        if not self.two_level:
            def mine(a, k):
                peer, pid = _peer(x, y, c, k)
                return remote(a, k, ins[a].at[pid] if self.scatter else ins[a], out[a].at[me], peer)

            def theirs(a, k):
                peer, pid = _peer(x, y, c, k)
                return remote(a, k, ins[a].at[pid] if self.scatter else ins[a], out[a].at[pid], peer)

            def start():
                for a in range(self.n):
                    local(a).start()
                    for k in range(1, NDEV):
                        mine(a, k).start()

            def forward():
                pass

            def finish():
                for a in range(self.n):
                    for k in range(1, NDEV):
                        mine(a, k).wait_send()
                    for k in range(1, NDEV):
                        theirs(a, k).wait_recv()
                    local(a).wait()

            return start, forward, finish

        sibling = (x, y, 1 - c)
        chips = [(1 - x, y), (x, 1 - y), (1 - x, 1 - y)]
        slot = lambda px, py, pc: 4 * px + 2 * py + pc

        def own(a, k, to):
            return remote(a, k, ins[a], out[a].at[me], to)

        def landed(a, k, frm):
            return remote(a, k, ins[a], out[a].at[slot(*frm)], frm)

        def passed(a, j):
            rows = out[a].at[slot(*chips[j], c)]
            return remote(a, 5 + j, rows, rows, sibling)

        def start():
            for a in range(self.n):
                local(a).start()
                own(a, 1, sibling).start()
                for j, chip in enumerate(chips):
                    own(a, 2 + j, (*chip, c)).start()

        def forward():
            for a in range(self.n):
                for j, chip in enumerate(chips):
                    landed(a, 2 + j, (*chip, c)).wait_recv()
                    passed(a, j).start()

        def finish():
            for a in range(self.n):
                landed(a, 1, sibling).wait_recv()
                for j, chip in enumerate(chips):
                    remote(a, 5 + j, ins[a], out[a].at[slot(*chip, 1 - c)], sibling).wait_recv()
                own(a, 1, sibling).wait_send()
                for j, chip in enumerate(chips):
                    own(a, 2 + j, (*chip, c)).wait_send()
                    passed(a, j).wait_send()
                local(a).wait()

        return start, forward, finish


def _ride(body, n_in, n_out, n_scr, comm, first, mid, last):
    k = comm.n

    def wrapped(*refs):
        ins, cins = refs[:n_in], refs[n_in:n_in + k]
        o0 = n_in + k
        outs, couts = refs[o0:o0 + n_out], refs[o0 + n_out:o0 + n_out + k]
        s0 = o0 + n_out + k
        scr, sems = refs[s0:s0 + n_scr], refs[s0 + n_scr:]
        start, forward, finish = comm.phases(cins, couts, *sems)
        pl.when(first())(start)
        body(*ins, *outs, *scr)
        pl.when(mid())(forward)
        pl.when(last())(finish)

    return wrapped


def _exchange(arrs, name, scatter=False, two_level=False):
    comm = _Comm(arrs, scatter, two_level)

    def body(*refs):
        start, forward, finish = comm.phases(refs[:comm.n], refs[comm.n:2 * comm.n], *refs[2 * comm.n:])
        start()
        forward()
        finish()

    return pl.pallas_call(body, name=name, out_shape=comm.out_shape, in_specs=comm.specs, out_specs=comm.specs,
                          scratch_shapes=comm.scratch, compiler_params=pltpu.CompilerParams(has_side_effects=True))(*arrs)


def mm(a, b, mode, out_dtype, name, tm=1024, tn=1024, tk=1024, precision=None, comm=None):
    if mode == "nn":
        (M, K), (K2, N) = a.shape, b.shape
    elif mode == "nt":
        (M, K), (N, K2) = a.shape, b.shape
    else:
        (K, M), (K2, N) = a.shape, b.shape
    assert K == K2, (name, a.shape, b.shape)
    tm, tn, tk = _pick(M, tm), _pick(N, tn), _pick(K, tk)
    nk = K // tk
    if mode == "tn":
        a_spec = pl.BlockSpec((tk, tm), lambda i, j, k: (k, i))
    else:
        a_spec = pl.BlockSpec((tm, tk), lambda i, j, k: (i, k))
    if mode == "nt":
        b_spec = pl.BlockSpec((tn, tk), lambda i, j, k: (j, k))
    else:
        b_spec = pl.BlockSpec((tk, tn), lambda i, j, k: (k, j))
    dims = {"nn": ((1,), (0,)), "nt": ((1,), (1,)), "tn": ((0,), (0,))}[mode]

    def body(a_ref, b_ref, o_ref, *scr):
        p = lax.dot_general(a_ref[...], b_ref[...], (dims, ((), ())), preferred_element_type=f32, precision=precision)
        if nk == 1:
            o_ref[...] = p.astype(o_ref.dtype)
        else:
            acc = scr[0]
            k = pl.program_id(2)

            @pl.when(k == 0)
            def _():
                acc[...] = p

            @pl.when(k > 0)
            def _():
                acc[...] += p

            @pl.when(k == nk - 1)
            def _():
                o_ref[...] = acc[...].astype(o_ref.dtype)

    grid = (M // tm, N // tn, nk)
    scratch = [pltpu.VMEM((tm, tn), f32)] if nk > 1 else []
    out_spec = pl.BlockSpec((tm, tn), lambda i, j, k: (i, j))
    out_shape = jax.ShapeDtypeStruct((M, N), out_dtype)
    if comm is None:
        return pl.pallas_call(body, name=name, grid=grid, in_specs=[a_spec, b_spec], out_specs=out_spec, out_shape=out_shape,
                              scratch_shapes=scratch, compiler_params=_cp(("parallel", "parallel", "arbitrary")))(a, b)
    at = lambda pos: lambda: functools.reduce(jnp.logical_and, [pl.program_id(d) == p for d, p in enumerate(pos)])
    end = tuple(g - 1 for g in grid)
    return pl.pallas_call(
        _ride(body, 2, 1, len(scratch), comm, at((0, 0, 0)), at(end), at(end)), name=name, grid=grid,
        in_specs=[a_spec, b_spec] + comm.specs, out_specs=[out_spec] + comm.specs, out_shape=[out_shape] + comm.out_shape,
        scratch_shapes=scratch + comm.scratch, compiler_params=_cp(("arbitrary", "arbitrary", "arbitrary")),
    )(a, b, *comm.arrs)


def rowcall(name, fn, tok, bat, con, tok_out, acc_out, ts=256):
    B, S = tok[0][0].shape[:2]
    ts = min(ts, S)
    nt, nb, nc, no, na = len(tok), len(bat), len(con), len(tok_out), len(acc_out)

    def body(*refs):
        tr, br, cr = refs[:nt], refs[nt:nt + nb], refs[nt + nb:nt + nb + nc]
        orf, arf = refs[nt + nb + nc:nt + nb + nc + no], refs[nt + nb + nc + no:]
        touts, aouts = fn([r[0] for r in tr], [r[0] for r in br], [r[...] for r in cr])
        for r, v in zip(orf, touts):
            r[0] = v.astype(r.dtype)
        s = pl.program_id(1)
        for r, v in zip(arf, aouts):
            @pl.when(s == 0)
            def _(r=r):
                r[...] = jnp.zeros(r.shape, r.dtype)
            r[0] += v.astype(f32)

    in_specs = [pl.BlockSpec((1, ts, w), lambda b, s, cb=cb: (b, s, cb)) for (_, w, cb) in tok]
    in_specs += [pl.BlockSpec((1,) + a.shape[1:], lambda b, s: (b, 0, 0)) for a in bat]
    in_specs += [pl.BlockSpec(a.shape, lambda b, s, nd=a.ndim: (0,) * nd) for a in con]
    out_specs = [pl.BlockSpec((1, ts, w), lambda b, s: (b, s, 0)) for (w, _) in tok_out]
    out_specs += [pl.BlockSpec((1,) + shp, lambda b, s, nd=len(shp): (b,) + (0,) * nd) for shp in acc_out]
    out_shape = [jax.ShapeDtypeStruct((B, S, w), dt) for (w, dt) in tok_out]
    out_shape += [jax.ShapeDtypeStruct((B,) + shp, f32) for shp in acc_out]
    return pl.pallas_call(
        body, name=name, grid=(B, S // ts), in_specs=in_specs, out_specs=out_specs, out_shape=out_shape,
        compiler_params=_cp(("parallel", "arbitrary")),
    )(*[t[0] for t in tok], *bat, *con)


def rowcall_fwd(name, f, tok, bat, con, tok_out, ts=256):
    def fn(t, b, c):
        return f([v.astype(f32) for v in t], b, c), []
    return rowcall(name, fn, tok, bat, con, tok_out, [], ts)


def rowcall_bwd(name, f, tok, bat, con, cts, tok_grads, add=None, ts=256):
    nt, ncts = len(tok), len(cts)

    def fn(t, b, c):
        prim = [v.astype(f32) for v in t[:nt]]
        ct = [v.astype(f32) for v in t[nt:nt + ncts]]
        _, vjp = jax.vjp(lambda tt, bb, cc: f(tt, bb, cc), prim, b, c)
        dt, db, dc = vjp(ct)
        touts = [dt[i] for i, _ in tok_grads]
        if add is not None:
            touts[0] = touts[0] + t[nt + ncts].astype(f32)
        return touts, list(db) + list(dc)

    all_tok = list(tok) + list(cts) + ([add] if add is not None else [])
    tok_out = [(tok[i][1], dt) for i, dt in tok_grads]
    acc_out = [tuple(a.shape[1:]) for a in bat] + [tuple(a.shape) for a in con]
    return rowcall(name, fn, all_tok, bat, con, tok_out, acc_out, ts)


def _rms(y, w):
    return y * lax.rsqrt(jnp.mean(y * y, axis=-1, keepdims=True) + RMS_EPS) * w


def f_rms_mod(t, b, c):
    return [_rms(t[0], c[0]) * (1.0 + b[0]) + b[1]]


def f_resid(t, b, c):
    return [t[0] + b[0] * _rms(t[1], c[0])]


def f_merge(t, b, c):
    ga, gd, ya, yd = t
    return [jax.nn.sigmoid(ga) * ya + jax.nn.sigmoid(gd) * yd]


def f_dnout(t, b, c):
    o, z = t
    outs = []
    for h in range(DNH):
        sl = slice(h * DND, (h + 1) * DND)
        zh = z[:, sl]
        outs.append(_rms(o[:, sl], c[0]) * (zh * jax.nn.sigmoid(zh)))
    return [jnp.concatenate(outs, axis=1)]


def _softplus(x):
    return jnp.maximum(x, 0.0) + jnp.log(1.0 + jnp.exp(-jnp.abs(x)))


def f_gate(t, b, c):
    ba = t[0]
    a_log, dt_bias = c
    lane = lax.broadcasted_iota(jnp.int32, ba.shape, 1)
    beta = jax.nn.sigmoid(ba)
    g = -jnp.exp(a_log) * _softplus(ba + dt_bias)
    return [jnp.where(lane < DNH, beta, jnp.where(lane < 2 * DNH, g, 0.0))]


def _bucket_table():
    qi = np.arange(WIN)[:, None]
    kj = np.arange(2 * WIN)[None, :]
    dist = np.maximum(WIN + qi - kj, 0)
    max_exact = NBUCK // 2
    scaled = np.log(np.maximum(dist, 1).astype(np.float64) / max_exact) / math.log(MAXDIST / max_exact)
    large = np.minimum(max_exact + (scaled * (NBUCK - max_exact)).astype(np.int32), NBUCK - 1)
    return np.where(dist < max_exact, dist, large).astype(np.int32)


def _attn_mask(n):
    qi = lax.broadcasted_iota(jnp.int32, (WIN, 2 * WIN), 0)
    kj = lax.broadcasted_iota(jnp.int32, (WIN, 2 * WIN), 1)
    dist = WIN + qi - kj
    return (dist >= 0) & (dist < WIN) & ((kj >= WIN) | (n > 0))


def _attn_block(q, kp, kc, vp, vc, bias, sinks, mask):
    grp = HQ // HKV
    outs = []
    for j in range(HKV):
        sl = slice(j * LANE, (j + 1) * LANE)
        kb = jnp.concatenate([kp[:, sl], kc[:, sl]], axis=0).astype(bf16)
        vb = jnp.concatenate([vp[:, sl], vc[:, sl]], axis=0).astype(bf16)
        for g in range(grp):
            h = j * grp + g
            qh = q[:, h * LANE:(h + 1) * LANE].astype(bf16)
            s = lax.dot_general(qh, kb, (((1,), (1,)), ((), ())), preferred_element_type=f32) * (HD ** -0.5)
            s = jnp.where(mask, s + bias[h], NEG_INF)
            sink = sinks[h]
            m = jnp.maximum(jnp.max(s, axis=-1, keepdims=True), sink)
            p = jnp.exp(s - m)
            probs = p / (jnp.sum(p, axis=-1, keepdims=True) + jnp.exp(sink - m))
            outs.append(jnp.dot(probs.astype(bf16), vb, preferred_element_type=f32))
    return jnp.concatenate(outs, axis=1)


def _attn_specs(NB):
    last = NB - 1
    return [
        pl.BlockSpec((1, WIN, HQ * LANE), lambda b, n: (b, jnp.minimum(n, last), CB_AQ // 8)),
        pl.BlockSpec((1, WIN, HKV * LANE), lambda b, n: (b, jnp.clip(n - 1, 0, last), CB_AK // 2)),
        pl.BlockSpec((1, WIN, HKV * LANE), lambda b, n: (b, jnp.minimum(n, last), CB_AK // 2)),
        pl.BlockSpec((1, WIN, HKV * LANE), lambda b, n: (b, jnp.clip(n - 1, 0, last), CB_AV // 2)),
        pl.BlockSpec((1, WIN, HKV * LANE), lambda b, n: (b, jnp.minimum(n, last), CB_AV // 2)),
        pl.BlockSpec((HQ, WIN, 2 * WIN), lambda b, n: (0, 0, 0)),
        pl.BlockSpec((HQ, 1, 1), lambda b, n: (0, 0, 0)),
    ]


def attn_fwd(proj, bias, sinks, comm):
    B, S, _ = proj.shape
    NB = S // WIN

    def body(q, kp, kc, vp, vc, bias_ref, sink_ref, o_ref):
        mask = _attn_mask(pl.program_id(1))
        o = _attn_block(q[0], kp[0], kc[0], vp[0], vc[0], bias_ref[...], sink_ref[...], mask)
        o_ref[0] = o.astype(o_ref.dtype)

    at = lambda b, n: lambda: (pl.program_id(0) == b) & (pl.program_id(1) == n)
    return pl.pallas_call(
        _ride(body, 7, 1, 0, comm, at(0, 0), at(B - 1, (3 * NB) // 4), at(B - 1, NB - 1)), name="attn_fwd", grid=(B, NB),
        in_specs=_attn_specs(NB) + comm.specs,
        out_specs=[pl.BlockSpec((1, WIN, HQ * LANE), lambda b, n: (b, n, 0))] + comm.specs,
        out_shape=[jax.ShapeDtypeStruct((B, S, HQ * LANE), bf16)] + comm.out_shape, scratch_shapes=comm.scratch,
        compiler_params=_cp(("arbitrary", "arbitrary")),
    )(proj, proj, proj, proj, proj, bias, sinks, *comm.arrs)


def attn_bwd(proj, bias, sinks, dy, comm):
    B, S, _ = proj.shape
    NB = S // WIN
    last = NB - 1

    def body(q, kp, kc, vp, vc, bias_ref, sink_ref, dy_ref, dq_ref, dk_ref, dv_ref, dbias_ref, dsink_ref, kcar, vcar):
        b, n = pl.program_id(0), pl.program_id(1)

        @pl.when((b == 0) & (n == 0))
        def _():
            dbias_ref[...] = jnp.zeros(dbias_ref.shape, f32)
            dsink_ref[...] = jnp.zeros(dsink_ref.shape, f32)

        @pl.when(n == 0)
        def _():
            kcar[...] = jnp.zeros(kcar.shape, f32)
            vcar[...] = jnp.zeros(vcar.shape, f32)

        @pl.when(n < NB)
        def _():
            mask = _attn_mask(n)
            _, vjp = jax.vjp(lambda *a: _attn_block(*a, mask), q[0], kp[0], kc[0], vp[0], vc[0], bias_ref[...], sink_ref[...])
            dq, dkp, dkc, dvp, dvc, dbias, dsink = vjp(dy_ref[0].astype(f32))
            dq_ref[0] = dq.astype(dq_ref.dtype)
            dbias_ref[...] += dbias
            dsink_ref[...] += dsink
            dk_ref[0] = (kcar[...] + dkp).astype(dk_ref.dtype)
            dv_ref[0] = (vcar[...] + dvp).astype(dv_ref.dtype)
            kcar[...] = dkc
            vcar[...] = dvc

        @pl.when(n == NB)
        def _():
            dk_ref[0] = kcar[...].astype(dk_ref.dtype)
            dv_ref[0] = vcar[...].astype(dv_ref.dtype)

    in_specs = _attn_specs(NB) + [pl.BlockSpec((1, WIN, HQ * LANE), lambda b, n: (b, jnp.minimum(n, last), 0))]
    kv_out = pl.BlockSpec((1, WIN, HKV * LANE), lambda b, n: (b, jnp.maximum(n - 1, 0), 0))
    at = lambda b, n: lambda: (pl.program_id(0) == b) & (pl.program_id(1) == n)
    return pl.pallas_call(
        _ride(body, 8, 5, 2, comm, at(0, 0), at(B - 1, NB), at(B - 1, NB)), name="attn_bwd", grid=(B, NB + 1),
        in_specs=in_specs + comm.specs,
        out_specs=[pl.BlockSpec((1, WIN, HQ * LANE), lambda b, n: (b, jnp.minimum(n, last), 0)), kv_out, kv_out,
                   pl.BlockSpec((HQ, WIN, 2 * WIN), lambda b, n: (0, 0, 0)), pl.BlockSpec((HQ, 1, 1), lambda b, n: (0, 0, 0))] + comm.specs,
        out_shape=[jax.ShapeDtypeStruct((B, S, HQ * LANE), bf16), jax.ShapeDtypeStruct((B, S, HKV * LANE), bf16),
                   jax.ShapeDtypeStruct((B, S, HKV * LANE), bf16), jax.ShapeDtypeStruct((HQ, WIN, 2 * WIN), f32),
                   jax.ShapeDtypeStruct((HQ, 1, 1), f32)] + comm.out_shape,
        scratch_shapes=[pltpu.VMEM((WIN, HKV * LANE), f32), pltpu.VMEM((WIN, HKV * LANE), f32)] + comm.scratch,
        compiler_params=_cp(("arbitrary", "arbitrary")),
    )(proj, proj, proj, proj, proj, bias, sinks, dy, *comm.arrs)


def _causal_conv(x, w, width):
    S, C = x.shape
    xp = jnp.concatenate([jnp.zeros((8, C), f32), x], axis=0)
    out = None
    for j in range(width):
        off = 8 - (width - 1) + j
        term = w[j:j + 1, :] * xp[off:off + S, :]
        out = term if out is None else out + term
    return out


def _dnconv_f(x, w, isqk):
    y = _causal_conv(x, w, DNK)
    y = y * jax.nn.sigmoid(y)
    yn = y * lax.rsqrt(jnp.sum(y * y, axis=-1, keepdims=True) + L2_EPS)
    return jnp.where(isqk, yn, y)


def _dn_outblk(i):
    return (i % DNH) * 3 + i // DNH


def dnconv_fwd(proj, conv_w):
    B, S, _ = proj.shape

    def body(x_ref, w_ref, o_ref):
        o_ref[0] = _dnconv_f(x_ref[0], w_ref[...], pl.program_id(0) < 2 * DNH)

    return pl.pallas_call(
        body, name="dnconv_fwd", grid=(3 * DNH, B),
        in_specs=[pl.BlockSpec((1, S, LANE), lambda i, b: (b, 0, CB_DQKV + i)), pl.BlockSpec((DNK, LANE), lambda i, b: (0, i))],
        out_specs=pl.BlockSpec((1, S, LANE), lambda i, b: (b, 0, _dn_outblk(i))),
        out_shape=jax.ShapeDtypeStruct((B, S, 3 * DNH * DND), f32), compiler_params=_cp(("parallel", "parallel")),
    )(proj, conv_w)


def dnconv_bwd(proj, conv_w, dqkvn):
    B, S, _ = proj.shape

    def body(x_ref, w_ref, dy_ref, dx_ref, dw_ref):
        isqk = pl.program_id(0) < 2 * DNH
        _, vjp = jax.vjp(lambda x, w: _dnconv_f(x, w, isqk), x_ref[0], w_ref[...])
        dx, dw = vjp(dy_ref[0])
        dx_ref[0] = dx.astype(dx_ref.dtype)

        @pl.when(pl.program_id(1) == 0)
        def _():
            dw_ref[...] = jnp.zeros(dw_ref.shape, f32)
        dw_ref[...] += dw

    return pl.pallas_call(
        body, name="dnconv_bwd", grid=(3 * DNH, B),
        in_specs=[pl.BlockSpec((1, S, LANE), lambda i, b: (b, 0, CB_DQKV + i)), pl.BlockSpec((DNK, LANE), lambda i, b: (0, i)),
                  pl.BlockSpec((1, S, LANE), lambda i, b: (b, 0, _dn_outblk(i)))],
        out_specs=[pl.BlockSpec((1, S, LANE), lambda i, b: (b, 0, i)), pl.BlockSpec((DNK, LANE), lambda i, b: (0, i))],
        out_shape=[jax.ShapeDtypeStruct((B, S, 3 * DNH * DND), bf16), jax.ShapeDtypeStruct((DNK, 3 * DNH * DND), f32)],
        compiler_params=_cp(("parallel", "arbitrary")),
    )(proj, conv_w, dqkvn)


def _bdot(a, b, ca, cb, precision=HI):
    return lax.dot_general(a, b, (((ca,), (cb,)), ((0,), (0,))), preferred_element_type=f32, precision=precision)


def _bdot_bf16(a, b, ca, cb):
    return _bdot(a.astype(bf16), b.astype(bf16), ca, cb, None)


@functools.partial(jax.custom_vjp, nondiff_argnums=(2, 3))
def _bdot_bf16_vjp(a, b, ca, cb):
    return _bdot_bf16(a, b, ca, cb)


def _bdot_bf16_fwd(a, b, ca, cb):
    return _bdot_bf16(a, b, ca, cb), (a, b)


def _bdot_bf16_bwd(ca, cb, res, g):
    a, b = res
    fa, fb = 3 - ca, 3 - cb
    da = _bdot_bf16(g, b, 2, fb) if ca == 2 else _bdot_bf16(b, g, fb, 2)
    db = _bdot_bf16(a, g, fa, 1) if cb == 1 else _bdot_bf16(g, a, 1, fa)
    return da, db


_bdot_bf16_vjp.defvjp(_bdot_bf16_fwd, _bdot_bf16_bwd)


def _neumann_inverse(low):
    n = low.shape[-1]
    eye = (lax.broadcasted_iota(jnp.int32, (n, n), 0) == lax.broadcasted_iota(jnp.int32, (n, n), 1)).astype(f32)
    p = -low
    x = eye[None] + p
    for _ in range(5):
        p = _bdot(p, p, 2, 1, MID)
        x = x + _bdot(x, p, 2, 1, MID)
    return x


@jax.custom_vjp
def _unit_lower_inverse(low):
    return _neumann_inverse(low)


def _uli_fwd(low):
    t = _neumann_inverse(low)
    return t, t


def _uli_bwd(t, dt):
    return (-_bdot(_bdot(t, dt, 1, 1, MID), t, 2, 2, MID),)


_unit_lower_inverse.defvjp(_uli_fwd, _uli_bwd)


def _stack(xs):
    return jnp.concatenate([x[None] for x in xs], axis=0)


def _delta_chunk(qkv, bg, state, differentiated):
    inverse = _unit_lower_inverse if differentiated else _neumann_inverse
    lo = _bdot_bf16_vjp if differentiated else _bdot_bf16
    B = qkv.shape[0]
    G = B * DNH
    pairs = [(b, h) for b in range(B) for h in range(DNH)]
    col = lambda b, h, kind: qkv[b, :, (3 * h + kind) * DND:(3 * h + kind + 1) * DND]
    q, k, v = [_stack([col(b, h, kind) for b, h in pairs]) for kind in range(3)]
    lane = lax.broadcasted_iota(jnp.int32, (CH, LANE), 1)
    pick = lambda b, l: jnp.sum(jnp.where(lane == l, bg[b], 0.0), axis=1, keepdims=True)
    beta = _stack([pick(b, h) for b, h in pairs])
    g = _stack([pick(b, h + DNH) for b, h in pairs])
    ri = lax.broadcasted_iota(jnp.int32, (CH, CH), 0)
    ci = lax.broadcasted_iota(jnp.int32, (CH, CH), 1)
    incl, strict = (ri >= ci)[None], (ri > ci)[None]
    gc = _bdot(jnp.broadcast_to(incl.astype(f32), (G, CH, CH)), jnp.broadcast_to(g, (G, CH, LANE)), 2, 1)
    e0 = jnp.broadcast_to((lane == 0).astype(f32)[None], (G, CH, LANE))
    gc_row = _bdot(e0, gc, 2, 2)
    diff = gc[:, :, :CH] - gc_row
    decay = jnp.where(incl, jnp.exp(jnp.where(incl, diff, 0.0)), 0.0)
    qs = q * (DND ** -0.5)
    kb, vb = k * beta, v * beta
    eg = jnp.exp(gc)
    low = jnp.where(strict, lo(kb, k, 2, 2) * decay, 0.0)
    tinv = inverse(low)
    u = _bdot(tinv, vb, 2, 1, MID)
    w = _bdot(tinv, kb * eg, 2, 1, MID)
    intra = jnp.where(incl, lo(qs, k, 2, 2) * decay, 0.0)
    gl = gc[:, CH - 1:CH, :]
    k_tail = k * jnp.exp(gl - gc)
    v_new = u - lo(w, state, 2, 1)
    o = lo(qs * eg, state, 2, 1) + lo(intra, v_new, 2, 1)
    new_state = state * jnp.exp(gl) + lo(k_tail, v_new, 1, 1)
    return o, new_state


def delta_fwd(qkvn, bg, comm):
    B, S, _ = qkvn.shape
    NC, G = S // CH, B * DNH

    def body(qkv_ref, bg_ref, o_ref, st_ref, state):
        @pl.when(pl.program_id(0) == 0)
        def _():
            state[...] = jnp.zeros(state.shape, f32)
        s0 = state[...]
        st_ref[0] = s0
        o, s1 = _delta_chunk(qkv_ref[...], bg_ref[...], s0, False)
        for b in range(B):
            for h in range(DNH):
                o_ref[b, :, h * DND:(h + 1) * DND] = o[b * DNH + h]
        state[...] = s1

    at = lambda c: lambda: pl.program_id(0) == c
    return pl.pallas_call(
        _ride(body, 2, 2, 1, comm, at(0), at((7 * NC) // 8), at(NC - 1)), name="delta_fwd", grid=(NC,),
        in_specs=[pl.BlockSpec((B, CH, 3 * DNH * DND), lambda c: (0, c, 0)), pl.BlockSpec((B, CH, LANE), lambda c: (0, c, 0))] + comm.specs,
        out_specs=[pl.BlockSpec((B, CH, DNH * DND), lambda c: (0, c, 0)), pl.BlockSpec((1, G, DND, DND), lambda c: (c, 0, 0, 0))] + comm.specs,
        out_shape=[jax.ShapeDtypeStruct((B, S, DNH * DND), f32), jax.ShapeDtypeStruct((NC, G, DND, DND), f32)] + comm.out_shape,
        scratch_shapes=[pltpu.VMEM((G, DND, DND), f32)] + comm.scratch, compiler_params=_cp(("arbitrary",)),
    )(qkvn, bg, *comm.arrs)


def delta_bwd(qkvn, bg, states, do, comm):
    B, S, _ = qkvn.shape
    NC, G = S // CH, B * DNH

    def body(qkv_ref, bg_ref, st_ref, do_ref, dqkv_ref, dbg_ref, dstate):
        @pl.when(pl.program_id(0) == 0)
        def _():
            dstate[...] = jnp.zeros(dstate.shape, f32)
        _, vjp = jax.vjp(lambda a, g, s: _delta_chunk(a, g, s, True), qkv_ref[...], bg_ref[...], st_ref[0])
        do = _stack([do_ref[b, :, h * DND:(h + 1) * DND] for b in range(B) for h in range(DNH)])
        dqkv, dbg, ds = vjp((do, dstate[...]))
        dqkv_ref[...] = dqkv
        dbg_ref[...] = dbg
        dstate[...] = ds

    rev = lambda c: NC - 1 - c
    at = lambda c: lambda: pl.program_id(0) == c
    return pl.pallas_call(
        _ride(body, 4, 2, 1, comm, at(0), at(NC - 1), at(NC - 1)), name="delta_bwd", grid=(NC,),
        in_specs=[pl.BlockSpec((B, CH, 3 * DNH * DND), lambda c: (0, rev(c), 0)), pl.BlockSpec((B, CH, LANE), lambda c: (0, rev(c), 0)),
                  pl.BlockSpec((1, G, DND, DND), lambda c: (rev(c), 0, 0, 0)),
                  pl.BlockSpec((B, CH, DNH * DND), lambda c: (0, rev(c), 0))] + comm.specs,
        out_specs=[pl.BlockSpec((B, CH, 3 * DNH * DND), lambda c: (0, rev(c), 0)),
                   pl.BlockSpec((B, CH, LANE), lambda c: (0, rev(c), 0))] + comm.specs,
        out_shape=[jax.ShapeDtypeStruct((B, S, 3 * DNH * DND), f32), jax.ShapeDtypeStruct((B, S, LANE), f32)] + comm.out_shape,
        scratch_shapes=[pltpu.VMEM((G, DND, DND), f32)] + comm.scratch, compiler_params=_cp(("arbitrary",)),
    )(qkvn, bg, states, do, *comm.arrs)


def _ffn_f(blk, w):
    u = _causal_conv(blk, w, FK)
    gate, val = u[:, :LANE], u[:, LANE:]
    gl = 0.5 * gate * (1.0 + jnp.tanh(math.sqrt(2.0 / math.pi) * (gate + 0.044715 * gate * gate * gate)))
    return gl * val


def ffnconv_fwd(up, conv_w):
    B, S, _ = up.shape
    nblk = DFF // LANE

    def body(x_ref, w_ref, o_ref):
        o_ref[0] = _ffn_f(x_ref[0], w_ref[...]).astype(o_ref.dtype)

    return pl.pallas_call(
        body, name="ffnconv_fwd", grid=(nblk, B),
        in_specs=[pl.BlockSpec((1, S, 2 * LANE), lambda i, b: (b, 0, i)), pl.BlockSpec((FK, 2 * LANE), lambda i, b: (0, i))],
        out_specs=pl.BlockSpec((1, S, LANE), lambda i, b: (b, 0, i)),
        out_shape=jax.ShapeDtypeStruct((B, S, DFF), bf16), compiler_params=_cp(("parallel", "parallel")),
    )(up, conv_w)


def ffnconv_bwd(up, conv_w, dact):
    B, S, _ = up.shape
    nblk = DFF // LANE

    def body(x_ref, w_ref, dy_ref, dx_ref, dw_ref):
        _, vjp = jax.vjp(_ffn_f, x_ref[0], w_ref[...])
        dx, dw = vjp(dy_ref[0].astype(f32))
        dx_ref[0] = dx.astype(dx_ref.dtype)

        @pl.when(pl.program_id(1) == 0)
        def _():
            dw_ref[...] = jnp.zeros(dw_ref.shape, f32)
        dw_ref[...] += dw

    return pl.pallas_call(
        body, name="ffnconv_bwd", grid=(nblk, B),
        in_specs=[pl.BlockSpec((1, S, 2 * LANE), lambda i, b: (b, 0, i)), pl.BlockSpec((FK, 2 * LANE), lambda i, b: (0, i)),
                  pl.BlockSpec((1, S, LANE), lambda i, b: (b, 0, i))],
        out_specs=[pl.BlockSpec((1, S, 2 * LANE), lambda i, b: (b, 0, i)), pl.BlockSpec((FK, 2 * LANE), lambda i, b: (0, i))],
        out_shape=[jax.ShapeDtypeStruct((B, S, 2 * DFF), bf16), jax.ShapeDtypeStruct((FK, 2 * DFF), f32)],
        compiler_params=_cp(("parallel", "arbitrary")),
    )(up, conv_w, dact)


def ada_fwd(c_all, ada_w, ada_b):
    def body(c_ref, w_ref, b_ref, o_ref):
        c = c_ref[...]
        act = (c * jax.nn.sigmoid(c)).astype(bf16)
        o_ref[...] = jnp.dot(act, w_ref[...].astype(bf16), preferred_element_type=f32) + b_ref[...]

    return pl.pallas_call(body, name="ada_fwd", out_shape=jax.ShapeDtypeStruct((c_all.shape[0], ada_w.shape[1]), f32),
                          compiler_params=pltpu.CompilerParams(vmem_limit_bytes=VMEM_LIMIT))(c_all, ada_w, ada_b)


def ada_bwd(c_all, dmod):
    def body(c_ref, d_ref, o_ref):
        c = c_ref[...]
        act = (c * jax.nn.sigmoid(c)).astype(bf16)
        o_ref[...] = lax.dot_general(act, d_ref[...].astype(bf16), (((0,), (0,)), ((), ())), preferred_element_type=f32)

    return pl.pallas_call(body, name="ada_bwd", out_shape=jax.ShapeDtypeStruct((c_all.shape[1], dmod.shape[1]), f32),
                          compiler_params=pltpu.CompilerParams(vmem_limit_bytes=VMEM_LIMIT))(c_all, dmod)


def loss_head(h1, y2, target, g2, w):
    def fn(t, b, c):
        h, y, tg = [v.astype(f32) for v in t]

        def loss_fn(h, y, g, w):
            e = h + g * _rms(y, w) - tg
            return 0.5 * jnp.sum(jnp.mean(e * e, axis=-1))

        loss, grads = jax.value_and_grad(loss_fn, argnums=(0, 1, 2, 3))(h, y, b[0], c[0])
        return [grads[0], grads[1]], [grads[2], grads[3], jnp.full((1, LANE), loss, f32)]

    return rowcall("loss_head", fn, [(h1, D, 0), (y2, D, 0), (target, D, 0)], [g2], [w], [(D, f32), (D, bf16)],
                   [(1, D), (1, D), (1, LANE)])


def adamw(w, gparts, m, v, name):
    R, C = w.shape
    P = gparts.shape[0]
    tr = R
    if R * C * 4 > 2 * 1024 * 1024:
        for cand in (512, 256, 128, 64, 32, 16, 8):
            if R % cand == 0 and cand * C * 4 <= 2 * 1024 * 1024:
                tr = cand
                break

    def body(w_ref, g_ref, m_ref, v_ref, go, do, mo, vo):
        g = g_ref[0].astype(f32)
        for p in range(1, P):
            g = g + g_ref[p].astype(f32)
        m2 = B1 * m_ref[...] + (1.0 - B1) * g
        v2 = B2 * v_ref[...] + (1.0 - B2) * jnp.square(g)
        m_hat = m2 / (1.0 - B1 ** STEP)
        v_hat = v2 / (1.0 - B2 ** STEP)
        go[...] = g
        do[...] = -LR * (m_hat / (jnp.sqrt(v_hat) + EPS) + WD * w_ref[...])
        mo[...] = m2
        vo[...] = v2

    blk = pl.BlockSpec((tr, C), lambda i: (i, 0))
    return pl.pallas_call(
        body, name=name, grid=(R // tr,), in_specs=[blk, pl.BlockSpec((P, tr, C), lambda i: (0, i, 0)), blk, blk],
        out_specs=[blk] * 4, out_shape=[jax.ShapeDtypeStruct((R, C), f32)] * 4, compiler_params=_cp(("parallel",)),
    )(w, gparts, m, v)


def _pad_heads(w, nh):
    r = w.shape[0]
    return jnp.pad(w.reshape(r, nh, HD), ((0, 0), (0, 0), (0, LANE - HD))).reshape(r, nh * LANE)


def _unpad_heads(w, nh):
    return w.reshape(w.shape[0], nh, LANE)[:, :, :HD].reshape(w.shape[0], nh * HD)


def _pack_w_in(w):
    aq, ak, av, dqkv, dz, dbeta, da, ga, gd = jnp.split(w, np.cumsum(IN_SPLITS)[:-1].tolist(), axis=1)
    ba = jnp.pad(jnp.concatenate([dbeta, da], axis=1), ((0, 0), (0, LANE - 2 * DNH)))
    return jnp.concatenate([ga, gd, _pad_heads(aq, HQ), dqkv, dz, _pad_heads(ak, HKV), _pad_heads(av, HKV), ba], axis=1)


def _unpack_w_in(p):
    col = lambda cb, n: p[:, cb * LANE: cb * LANE + n]
    ba = col(CB_BA, 2 * DNH)
    return jnp.concatenate([_unpad_heads(col(CB_AQ, HQ * LANE), HQ), _unpad_heads(col(CB_AK, HKV * LANE), HKV),
                            _unpad_heads(col(CB_AV, HKV * LANE), HKV), col(CB_DQKV, 3 * DNH * DND), col(CB_DZ, DNH * DND),
                            ba[:, :DNH], ba[:, DNH:], col(CB_GA, D), col(CB_GD, D)], axis=1)


def _interleave(w):
    r = w.shape[0]
    return w.reshape(r, 2, DFF // LANE, LANE).transpose(0, 2, 1, 3).reshape(r, 2 * DFF)


def _deinterleave(w):
    r = w.shape[0]
    return w.reshape(r, DFF // LANE, 2, LANE).transpose(0, 2, 1, 3).reshape(r, 2 * DFF)


def _cols_gathered(g):
    return g.transpose(1, 0, 2).reshape(g.shape[1], NDEV * g.shape[2])


def _cols_split(w):
    r = w.shape[0]
    return w.reshape(r, NDEV, w.shape[1] // NDEV).transpose(1, 0, 2)


def kernel(x, c, ada_w, ada_b, norm_mix_pre, norm_mix_post, norm_ffn_pre, norm_ffn_post, w_in, dn_conv_w, dn_a_log, dn_dt_bias, dn_norm_w, attn_sinks, rel_bias, w_attn_branch, w_dn_branch, w_out, ffn_w_up, ffn_conv_w, ffn_w_down, loss_target, m_ada_w, m_ada_b, m_norm_mix_pre, m_norm_mix_post, m_norm_ffn_pre, m_norm_ffn_post, m_w_in, m_dn_conv_w, m_dn_a_log, m_dn_dt_bias, m_dn_norm_w, m_attn_sinks, m_rel_bias, m_w_attn_branch, m_w_dn_branch, m_w_out, m_ffn_w_up, m_ffn_conv_w, m_ffn_w_down, v_ada_w, v_ada_b, v_norm_mix_pre, v_norm_mix_post, v_norm_ffn_pre, v_norm_ffn_post, v_w_in, v_dn_conv_w, v_dn_a_log, v_dn_dt_bias, v_dn_norm_w, v_attn_sinks, v_rel_bias, v_w_attn_branch, v_w_dn_branch, v_w_out, v_ffn_w_up, v_ffn_conv_w, v_ffn_w_down):
    B, S, _ = x.shape
    T = B * S
    me = 4 * lax.axis_index("x") + 2 * lax.axis_index("y") + lax.axis_index("c")
    big = dict(w_in=w_in, dn_conv_w=dn_conv_w, w_attn_branch=w_attn_branch, w_dn_branch=w_dn_branch, w_out=w_out,
               ffn_w_up=ffn_w_up, ffn_conv_w=ffn_conv_w, ffn_w_down=ffn_w_down)
    big_names = list(big)

    first, mid, late = ["w_in", "dn_conv_w"], ["w_attn_branch", "w_dn_branch", "w_out"], ["ffn_w_up", "ffn_conv_w", "ffn_w_down"]
    shard = lambda names: [big[n][0].astype(bf16) for n in names]
    gw = dict(zip(first, _exchange(shard(first), "gather_w_in", two_level=True)))
    (c_all,) = _exchange([c], "gather_c")
    c_all = c_all.reshape(NDEV * B, D)

    wp = _pack_w_in(_cols_gathered(gw["w_in"]))
    conv_dn = _cols_gathered(gw["dn_conv_w"]).astype(f32)

    ncol = ada_w.shape[2]
    ada_b_mine = lax.dynamic_slice_in_dim(ada_b, me * ncol, ncol, axis=1)
    mod_cols = ada_fwd(c_all, ada_w[0], ada_b_mine)
    (mod_g,) = _exchange([mod_cols], "gather_mod")
    mod = lax.dynamic_slice_in_dim(mod_g, me * B, B, axis=1).transpose(1, 0, 2).reshape(B, NMOD * D)
    sh1, sc1, g1, sh2, sc2, g2 = [mod[:, i * D:(i + 1) * D].reshape(B, 1, D) for i in range(NMOD)]

    onehot = (jnp.asarray(_bucket_table()).reshape(1, -1) == jnp.arange(NBUCK, dtype=jnp.int32)[:, None]).astype(f32)
    bias = mm(rel_bias.T, onehot, "nn", f32, "bias_table", tn=8192, precision=HI).reshape(HQ, WIN, 2 * WIN)
    sinks = attn_sinks.reshape(HQ, 1, 1)
    a_log_pad = jnp.pad(dn_a_log, ((0, 0), (DNH, LANE - 2 * DNH)))
    dt_bias_pad = jnp.pad(dn_dt_bias, ((0, 0), (DNH, LANE - 2 * DNH)))

    (u1,) = rowcall_fwd("mix_pre", f_rms_mod, [(x, D, 0)], [sc1, sh1], [norm_mix_pre], [(D, bf16)])
    proj = mm(u1.reshape(T, D), wp, "nn", f32, "proj", tn=1152).reshape(B, S, NP)
    ya, *got = attn_fwd(proj, bias, sinks, _Comm(shard(mid), two_level=True))
    gw.update(zip(mid, got))
    wa = _cols_gathered(gw["w_attn_branch"])
    wa = jnp.pad(wa.reshape(HQ, HD, D), ((0, 0), (0, LANE - HD), (0, 0))).reshape(HQ * LANE, D)
    wd = _cols_gathered(gw["w_dn_branch"])
    wo = gw["w_out"].reshape(D, D)
    qkvn = dnconv_fwd(proj, conv_dn)
    (bg,) = rowcall_fwd("dn_gate", f_gate, [(proj, LANE, CB_BA)], [], [a_log_pad, dt_bias_pad], [(LANE, f32)])
    o_dn, states, *got = delta_fwd(qkvn, bg, _Comm(shard(late), two_level=True))
    gw.update(zip(late, got))
    wup = _interleave(_cols_gathered(gw["ffn_w_up"]))
    conv_ffn = _interleave(_cols_gathered(gw["ffn_conv_w"]).astype(f32))
    wdown = gw["ffn_w_down"].reshape(DFF, D)
    (yd,) = rowcall_fwd("dn_out", f_dnout, [(o_dn, DNH * DND, 0), (proj, DNH * DND, CB_DZ // 4)], [], [dn_norm_w], [(DNH * DND, bf16)])
    pa = mm(ya.reshape(T, HQ * LANE), wa, "nn", f32, "attn_branch").reshape(B, S, D)
    pd = mm(yd.reshape(T, DNH * DND), wd, "nn", f32, "dn_branch").reshape(B, S, D)
    merge_tok = [(proj, D, CB_GA // 8), (proj, D, CB_GD // 8), (pa, D, 0), (pd, D, 0)]
    (merged,) = rowcall_fwd("merge", f_merge, merge_tok, [], [], [(D, bf16)])
    y1 = mm(merged.reshape(T, D), wo, "nn", f32, "mix_out").reshape(B, S, D)
    (h1,) = rowcall_fwd("mix_post", f_resid, [(x, D, 0), (y1, D, 0)], [g1], [norm_mix_post], [(D, f32)])
    (u2,) = rowcall_fwd("ffn_pre", f_rms_mod, [(h1, D, 0)], [sc2, sh2], [norm_ffn_pre], [(D, bf16)])
    up = mm(u2.reshape(T, D), wup, "nn", f32, "ffn_up", tn=1408).reshape(B, S, 2 * DFF)
    act = ffnconv_fwd(up, conv_ffn)
    y2 = mm(act.reshape(T, DFF), wdown, "nn", f32, "ffn_down", tk=1408).reshape(B, S, D)

    dh1_a, dy2, dg2, dw_ffn_post, loss_b = loss_head(h1, y2, loss_target, g2, norm_ffn_post)
    dy2f = dy2.reshape(T, D)
    dact = mm(dy2f, wdown, "nt", bf16, "ffn_down_dx", tn=1408).reshape(B, S, DFF)
    g_wdown = mm(act.reshape(T, DFF), dy2f, "tn", f32, "ffn_down_dw", tm=1408, tk=2048)
    dup, g_conv_ffn = ffnconv_bwd(up, conv_ffn, dact)
    dupf = dup.reshape(T, 2 * DFF)
    du2 = mm(dupf, wup, "nt", f32, "ffn_up_dx", tk=1408).reshape(B, S, D)
    g_wup = mm(u2.reshape(T, D), dupf, "tn", f32, "ffn_up_dw", tn=1408, tk=2048)
    dh1, dsc2, dsh2, dw_ffn_pre = rowcall_bwd("ffn_pre_bwd", f_rms_mod, [(h1, D, 0)], [sc2, sh2], [norm_ffn_pre], [(du2, D, 0)],
                                              [(0, f32)], add=(dh1_a, D, 0))
    dy1, dg1, dw_mix_post = rowcall_bwd("mix_post_bwd", f_resid, [(x, D, 0), (y1, D, 0)], [g1], [norm_mix_post], [(dh1, D, 0)],
                                        [(1, bf16)])
    dy1f = dy1.reshape(T, D)
    dmerged = mm(dy1f, wo, "nt", f32, "mix_out_dx").reshape(B, S, D)
    g_wo = mm(merged.reshape(T, D), dy1f, "tn", f32, "mix_out_dw", tk=2048)
    dga, dgd, dpa, dpd = rowcall_bwd("merge_bwd", f_merge, merge_tok, [], [], [(dmerged, D, 0)],
                                     [(0, bf16), (1, bf16), (2, bf16), (3, bf16)])
    dpaf, dpdf = dpa.reshape(T, D), dpd.reshape(T, D)
    dya = mm(dpaf, wa, "nt", bf16, "attn_branch_dx").reshape(B, S, HQ * LANE)
    g_wa = mm(ya.reshape(T, HQ * LANE), dpaf, "tn", f32, "attn_branch_dw", tk=2048)
    dyd = mm(dpdf, wd, "nt", f32, "dn_branch_dx").reshape(B, S, DNH * DND)
    g_wd = mm(yd.reshape(T, DNH * DND), dpdf, "tn", f32, "dn_branch_dw", tk=2048)
    do_dn, dz, dw_dn_norm = rowcall_bwd("dn_out_bwd", f_dnout, [(o_dn, DNH * DND, 0), (proj, DNH * DND, CB_DZ // 4)], [], [dn_norm_w],
                                        [(dyd, DNH * DND, 0)], [(0, f32), (1, bf16)])
    parts = {}
    outbox = lambda d: _Comm([d[n].astype(bf16) for n in d], scatter=True)
    send = dict(ffn_w_up=_cols_split(_deinterleave(g_wup)), ffn_conv_w=_cols_split(_deinterleave(g_conv_ffn)),
                ffn_w_down=g_wdown.reshape(NDEV, DFF // NDEV, D))
    dqkvn, dbg, *got = delta_bwd(qkvn, bg, states, do_dn, outbox(send))
    parts.update(zip(send, got))
    dba, da_log_pad, ddt_bias_pad = rowcall_bwd("dn_gate_bwd", f_gate, [(proj, LANE, CB_BA)], [], [a_log_pad, dt_bias_pad],
                                                [(dbg, LANE, 0)], [(0, bf16)])
    ddqkv, g_conv_dn = dnconv_bwd(proj, conv_dn, dqkvn)
    send = dict(w_attn_branch=_cols_split(g_wa.reshape(HQ, LANE, D)[:, :HD].reshape(HQ * HD, D)), w_dn_branch=_cols_split(g_wd),
                w_out=g_wo.reshape(NDEV, D // NDEV, D))
    dq, dk, dv, dbias, dsinks, *got = attn_bwd(proj, bias, sinks, dya, outbox(send))
    parts.update(zip(send, got))
    dproj = jnp.concatenate([dga, dgd, dq, ddqkv, dz, dk, dv, dba], axis=2).reshape(T, NP)
    g_wp = mm(u1.reshape(T, D), dproj, "tn", f32, "proj_dw", tn=1152, tk=2048)
    send = dict(w_in=_cols_split(_unpack_w_in(g_wp)), dn_conv_w=_cols_split(g_conv_dn))
    du1, *got = mm(dproj, wp, "nt", f32, "proj_dx", tk=1152, comm=outbox(send))
    parts.update(zip(send, got))
    du1 = du1.reshape(B, S, D)
    grad_x, dsc1, dsh1, dw_mix_pre = rowcall_bwd("mix_pre_bwd", f_rms_mod, [(x, D, 0)], [sc1, sh1], [norm_mix_pre], [(du1, D, 0)],
                                                 [(0, f32)], add=(dh1, D, 0))
    g_rel = mm(dbias.reshape(HQ, WIN * 2 * WIN), onehot, "nt", f32, "rel_bias_dw", tk=8192, precision=HI)

    dmod = jnp.concatenate([dsh1, dsc1, dg1, dsh2, dsc2, dg2], axis=2).reshape(B, NMOD * D)
    (dmod_g,) = _exchange([dmod], "gather_dmod")
    dmod_cols = lax.dynamic_slice_in_dim(dmod_g.reshape(NDEV * B, NMOD * D), me * ncol, ncol, axis=1)
    g_ada_w = ada_bwd(c_all, dmod_cols)

    zrow = lambda a: jnp.concatenate([a.reshape(1, -1), jnp.zeros((B - 1, a.size), f32)], axis=0)
    small_g = jnp.concatenate([
        dmod, dw_mix_pre.reshape(B, D), dw_mix_post.reshape(B, D), dw_ffn_pre.reshape(B, D), dw_ffn_post.reshape(B, D),
        da_log_pad.reshape(B, LANE)[:, DNH:2 * DNH], ddt_bias_pad.reshape(B, LANE)[:, DNH:2 * DNH], dw_dn_norm.reshape(B, DND),
        zrow(dsinks), zrow(g_rel.T), loss_b.reshape(B, LANE)[:, :1], jnp.zeros((B, SMALL_PAD - SMALL_N - 1), f32)], axis=1)
    (small_all,) = _exchange([small_g], "gather_small")
    small_w = dict(ada_b=(ada_b, m_ada_b, v_ada_b), norm_mix_pre=(norm_mix_pre, m_norm_mix_pre, v_norm_mix_pre),
                   norm_mix_post=(norm_mix_post, m_norm_mix_post, v_norm_mix_post), norm_ffn_pre=(norm_ffn_pre, m_norm_ffn_pre, v_norm_ffn_pre),
                   norm_ffn_post=(norm_ffn_post, m_norm_ffn_post, v_norm_ffn_post), dn_a_log=(dn_a_log, m_dn_a_log, v_dn_a_log),
                   dn_dt_bias=(dn_dt_bias, m_dn_dt_bias, v_dn_dt_bias), dn_norm_w=(dn_norm_w, m_dn_norm_w, v_dn_norm_w),
                   attn_sinks=(attn_sinks, m_attn_sinks, v_attn_sinks), rel_bias=(rel_bias, m_rel_bias, v_rel_bias))

    def pack(i, fill):
        row = jnp.concatenate([small_w[n][i].reshape(1, -1) for n, _ in SMALL], axis=1)
        return jnp.pad(row, ((0, 0), (0, SMALL_PAD - SMALL_N)), constant_values=fill)

    small_out = adamw(pack(0, 0.0), small_all.reshape(NDEV * B, 1, SMALL_PAD), pack(1, 0.0), pack(2, 1.0), "adamw_small")
    loss = small_out[0][0, SMALL_N]

    res = {}
    off = 0
    for n, size in SMALL:
        shp = small_w[n][0].shape
        res[n] = [o[:, off:off + size].reshape(shp) for o in small_out]
        off += size
    res["ada_w"] = [o[None] for o in adamw(ada_w[0], g_ada_w[None], m_ada_w[0], v_ada_w[0], "adamw_ada_w")]
    moments = dict(w_in=(m_w_in, v_w_in), dn_conv_w=(m_dn_conv_w, v_dn_conv_w), w_attn_branch=(m_w_attn_branch, v_w_attn_branch),
                   w_dn_branch=(m_w_dn_branch, v_w_dn_branch), w_out=(m_w_out, v_w_out), ffn_w_up=(m_ffn_w_up, v_ffn_w_up),
                   ffn_conv_w=(m_ffn_conv_w, v_ffn_conv_w), ffn_w_down=(m_ffn_w_down, v_ffn_w_down))
    for n in big_names:
        res[n] = [o[None] for o in adamw(big[n][0], parts[n], moments[n][0][0], moments[n][1][0], "adamw_" + n)]

    order = ["ada_w", "ada_b", "norm_mix_pre", "norm_mix_post", "norm_ffn_pre", "norm_ffn_post", "w_in", "dn_conv_w", "dn_a_log",
             "dn_dt_bias", "dn_norm_w", "attn_sinks", "rel_bias", "w_attn_branch", "w_dn_branch", "w_out", "ffn_w_up", "ffn_conv_w",
             "ffn_w_down"]
    return (loss, grad_x, *[res[n][0] for n in order], *[res[n][1] for n in order], *[res[n][2] for n in order],
            *[res[n][3] for n in order])
```

```python
import functools
import math

import numpy as np
import jax
import jax.numpy as jnp
from jax import lax
from jax.experimental import pallas as pl
from jax.experimental.pallas import tpu as pltpu

f32 = jnp.float32
bf16 = jnp.bfloat16
HI = lax.Precision.HIGHEST
MID = lax.Precision.HIGH
MESH = pl.DeviceIdType.MESH

NDEV = 8
D = 1024
HQ, HKV, HD, WIN, NBUCK, MAXDIST = 8, 2, 64, 128, 32, 128
DNH, DND, DNK, CH = 4, 128, 4, 64
DFF, FK = 2816, 3
NMOD = 6
RMS_EPS = 1e-6
L2_EPS = 1e-6
NEG_INF = -1e30
LR, B1, B2, EPS, WD, STEP = 0.001, 0.9, 0.999, 1e-08, 0.01, 10

LANE = 128
CB_GA, CB_GD, CB_AQ, CB_DQKV, CB_DZ, CB_AK, CB_AV, CB_BA, NPB = 0, 8, 16, 24, 36, 40, 42, 44, 45
NP = NPB * LANE
IN_SPLITS = (HQ * HD, HKV * HD, HKV * HD, 3 * DNH * DND, DNH * DND, DNH, DNH, D, D)
IN_DIM = sum(IN_SPLITS)
VMEM_LIMIT = 56 * 1024 * 1024

SMALL = (("ada_b", NMOD * D), ("norm_mix_pre", D), ("norm_mix_post", D), ("norm_ffn_pre", D), ("norm_ffn_post", D),
         ("dn_a_log", DNH), ("dn_dt_bias", DNH), ("dn_norm_w", DND), ("attn_sinks", HQ), ("rel_bias", NBUCK * HQ))
SMALL_N = sum(n for _, n in SMALL)
SMALL_PAD = 10752


def _cp(sem):
    return pltpu.CompilerParams(dimension_semantics=sem, vmem_limit_bytes=VMEM_LIMIT)


def _pick(dim, target):
    if dim <= target:
        return dim
    best = None
    for d in range(LANE, target + 1, LANE):
        if dim % d == 0:
            best = d
    assert best is not None, (dim, target)
    return best


def _me():
    x, y, c = lax.axis_index("x"), lax.axis_index("y"), lax.axis_index("c")
    return x, y, c, 4 * x + 2 * y + c


def _peer(x, y, c, k):
    px = 1 - x if k & 4 else x
    py = 1 - y if k & 2 else y
    pc = 1 - c if k & 1 else c
    return (px, py, pc), 4 * px + 2 * py + pc


class _Comm:
    def __init__(self, arrs, scatter=False, two_level=False):
        assert not (scatter and two_level)
        self.arrs, self.n, self.scatter, self.two_level = list(arrs), len(arrs), scatter, two_level
        if scatter:
            self.out_shape = [jax.ShapeDtypeStruct(a.shape, a.dtype) for a in arrs]
        else:
            self.out_shape = [jax.ShapeDtypeStruct((NDEV,) + a.shape, a.dtype) for a in arrs]
        nsem = self.n * (NDEV - 1)
        self.scratch = [pltpu.SemaphoreType.DMA((nsem,)), pltpu.SemaphoreType.DMA((nsem,)), pltpu.SemaphoreType.DMA((self.n,))]
        self.specs = [pl.BlockSpec(memory_space=pl.ANY)] * self.n

    def phases(self, ins, out, send, recv, loc):
        x, y, c, me = _me()

        def remote(a, k, src, dst, to):
            s = a * (NDEV - 1) + k - 1
            return pltpu.make_async_remote_copy(src_ref=src, dst_ref=dst, send_sem=send.at[s], recv_sem=recv.at[s],
                                                device_id=to, device_id_type=MESH)

        def local(a):
            return pltpu.make_async_copy(ins[a].at[me] if self.scatter else ins[a], out[a].at[me], loc.at[a])

        if not self.two_level:
            def mine(a, k):
                peer, pid = _peer(x, y, c, k)
                return remote(a, k, ins[a].at[pid] if self.scatter else ins[a], out[a].at[me], peer)

            def theirs(a, k):
                peer, pid = _peer(x, y, c, k)
                return remote(a, k, ins[a].at[pid] if self.scatter else ins[a], out[a].at[pid], peer)

            def start():
                for a in range(self.n):
                    local(a).start()
                    for k in range(1, NDEV):
                        mine(a, k).start()

            def forward():
                pass

            def finish():
                for a in range(self.n):
                    for k in range(1, NDEV):
                        mine(a, k).wait_send()
                    for k in range(1, NDEV):
                        theirs(a, k).wait_recv()
                    local(a).wait()

            return start, forward, finish

        sibling = (x, y, 1 - c)
        chips = [(1 - x, y), (x, 1 - y), (1 - x, 1 - y)]
        slot = lambda px, py, pc: 4 * px + 2 * py + pc

        def own(a, k, to):
            return remote(a, k, ins[a], out[a].at[me], to)

        def landed(a, k, frm):
            return remote(a, k, ins[a], out[a].at[slot(*frm)], frm)

        def passed(a, j):
            rows = out[a].at[slot(*chips[j], c)]
            return remote(a, 5 + j, rows, rows, sibling)

        def start():
            for a in range(self.n):
                local(a).start()
                own(a, 1, sibling).start()
                for j, chip in enumerate(chips):
                    own(a, 2 + j, (*chip, c)).start()

        def forward():
            for a in range(self.n):
                for j, chip in enumerate(chips):
                    landed(a, 2 + j, (*chip, c)).wait_recv()
                    passed(a, j).start()

        def finish():
            for a in range(self.n):
                landed(a, 1, sibling).wait_recv()
                for j, chip in enumerate(chips):
                    remote(a, 5 + j, ins[a], out[a].at[slot(*chip, 1 - c)], sibling).wait_recv()
                own(a, 1, sibling).wait_send()
                for j, chip in enumerate(chips):
                    own(a, 2 + j, (*chip, c)).wait_send()
                    passed(a, j).wait_send()
                local(a).wait()

        return start, forward, finish


def _ride(body, n_in, n_out, n_scr, comm, first, mid, last):
    k = comm.n

    def wrapped(*refs):
        ins, cins = refs[:n_in], refs[n_in:n_in + k]
        o0 = n_in + k
        outs, couts = refs[o0:o0 + n_out], refs[o0 + n_out:o0 + n_out + k]
        s0 = o0 + n_out + k
        scr, sems = refs[s0:s0 + n_scr], refs[s0 + n_scr:]
        start, forward, finish = comm.phases(cins, couts, *sems)
        pl.when(first())(start)
        body(*ins, *outs, *scr)
        pl.when(mid())(forward)
        pl.when(last())(finish)

    return wrapped


def _exchange(arrs, name, scatter=False, two_level=False):
    comm = _Comm(arrs, scatter, two_level)

    def body(*refs):
        start, forward, finish = comm.phases(refs[:comm.n], refs[comm.n:2 * comm.n], *refs[2 * comm.n:])
        start()
        forward()
        finish()

    return pl.pallas_call(body, name=name, out_shape=comm.out_shape, in_specs=comm.specs, out_specs=comm.specs,
                          scratch_shapes=comm.scratch, compiler_params=pltpu.CompilerParams(has_side_effects=True))(*arrs)


def mm(a, b, mode, out_dtype, name, tm=1024, tn=1024, tk=1024, precision=None, comm=None):
    a_parts = a.shape[0] if a.ndim == 3 else 1
    b_parts = b.shape[0] if b.ndim == 3 else 1
    assert (a_parts == 1 or mode == "nt") and (b_parts == 1 or mode == "tn")
    ash, bsh = a.shape[-2:], b.shape[-2:]
    if mode == "nn":
        (M, K), (K2, N) = ash, bsh
    elif mode == "nt":
        (M, K), (N, K2) = (ash[0], ash[1] * a_parts), bsh
    else:
        (K, M), (K2, N) = ash, (bsh[0], bsh[1] * b_parts)
    assert K == K2, (name, a.shape, b.shape)
    tm, tn, tk = _pick(M, tm), _pick(N // b_parts, tn), _pick(K // a_parts, tk)
    nk = K // tk
    if mode == "tn":
        a_spec = pl.BlockSpec((tk, tm), lambda i, j, k: (k, i))
    elif a_parts > 1:
        per = nk // a_parts
        a_spec = pl.BlockSpec((None, tm, tk), lambda i, j, k: (k // per, i, k % per))
    else:
        a_spec = pl.BlockSpec((tm, tk), lambda i, j, k: (i, k))
    if mode == "nt":
        b_spec = pl.BlockSpec((tn, tk), lambda i, j, k: (j, k))
    elif b_parts > 1:
        per = N // tn // b_parts
        b_spec = pl.BlockSpec((None, tk, tn), lambda i, j, k: (j // per, k, j % per))
    else:
        b_spec = pl.BlockSpec((tk, tn), lambda i, j, k: (k, j))
    dims = {"nn": ((1,), (0,)), "nt": ((1,), (1,)), "tn": ((0,), (0,))}[mode]

    def body(a_ref, b_ref, o_ref, *scr):
        p = lax.dot_general(a_ref[...], b_ref[...], (dims, ((), ())), preferred_element_type=f32, precision=precision)
        if nk == 1:
            o_ref[...] = p.astype(o_ref.dtype)
        else:
            acc = scr[0]
            k = pl.program_id(2)

            @pl.when(k == 0)
            def _():
                acc[...] = p

            @pl.when(k > 0)
            def _():
                acc[...] += p

            @pl.when(k == nk - 1)
            def _():
                o_ref[...] = acc[...].astype(o_ref.dtype)

    grid = (M // tm, N // tn, nk)
    scratch = [pltpu.VMEM((tm, tn), f32)] if nk > 1 else []
    out_spec = pl.BlockSpec((tm, tn), lambda i, j, k: (i, j))
    out_shape = jax.ShapeDtypeStruct((M, N), out_dtype)
    if comm is None:
        return pl.pallas_call(body, name=name, grid=grid, in_specs=[a_spec, b_spec], out_specs=out_spec, out_shape=out_shape,
                              scratch_shapes=scratch, compiler_params=_cp(("parallel", "parallel", "arbitrary")))(a, b)
    at = lambda pos: lambda: functools.reduce(jnp.logical_and, [pl.program_id(d) == p for d, p in enumerate(pos)])
    end = tuple(g - 1 for g in grid)
    return pl.pallas_call(
        _ride(body, 2, 1, len(scratch), comm, at((0, 0, 0)), at(end), at(end)), name=name, grid=grid,
        in_specs=[a_spec, b_spec] + comm.specs, out_specs=[out_spec] + comm.specs, out_shape=[out_shape] + comm.out_shape,
        scratch_shapes=scratch + comm.scratch, compiler_params=_cp(("arbitrary", "arbitrary", "arbitrary")),
    )(a, b, *comm.arrs)


def rowcall(name, fn, tok, bat, con, tok_out, acc_out, ts=256):
    B, S = tok[0][0].shape[:2]
    ts = min(ts, S)
    nt, nb, nc, no, na = len(tok), len(bat), len(con), len(tok_out), len(acc_out)

    def body(*refs):
        tr, br, cr = refs[:nt], refs[nt:nt + nb], refs[nt + nb:nt + nb + nc]
        orf, arf = refs[nt + nb + nc:nt + nb + nc + no], refs[nt + nb + nc + no:]
        touts, aouts = fn([r[0] for r in tr], [r[0] for r in br], [r[...] for r in cr])
        for r, v in zip(orf, touts):
            r[0] = v.astype(r.dtype)
        s = pl.program_id(1)
        for r, v in zip(arf, aouts):
            @pl.when(s == 0)
            def _(r=r):
                r[...] = jnp.zeros(r.shape, r.dtype)
            r[0] += v.astype(f32)

    in_specs = [pl.BlockSpec((1, ts, w), lambda b, s, cb=cb: (b, s, cb)) for (_, w, cb) in tok]
    in_specs += [pl.BlockSpec((1,) + a.shape[1:], lambda b, s: (b, 0, 0)) for a in bat]
    in_specs += [pl.BlockSpec(a.shape, lambda b, s, nd=a.ndim: (0,) * nd) for a in con]
    out_specs = [pl.BlockSpec((1, ts, w), lambda b, s: (b, s, 0)) for (w, _) in tok_out]
    out_specs += [pl.BlockSpec((1,) + shp, lambda b, s, nd=len(shp): (b,) + (0,) * nd) for shp in acc_out]
    out_shape = [jax.ShapeDtypeStruct((B, S, w), dt) for (w, dt) in tok_out]
    out_shape += [jax.ShapeDtypeStruct((B,) + shp, f32) for shp in acc_out]
    return pl.pallas_call(
        body, name=name, grid=(B, S // ts), in_specs=in_specs, out_specs=out_specs, out_shape=out_shape,
        compiler_params=_cp(("parallel", "arbitrary")),
    )(*[t[0] for t in tok], *bat, *con)


def rowcall_fwd(name, f, tok, bat, con, tok_out, ts=256):
    def fn(t, b, c):
        return f([v.astype(f32) for v in t], b, c), []
    return rowcall(name, fn, tok, bat, con, tok_out, [], ts)


def rowcall_bwd(name, f, tok, bat, con, cts, tok_grads, add=None, ts=256):
    nt, ncts = len(tok), len(cts)

    def fn(t, b, c):
        prim = [v.astype(f32) for v in t[:nt]]
        ct = [v.astype(f32) for v in t[nt:nt + ncts]]
        _, vjp = jax.vjp(lambda tt, bb, cc: f(tt, bb, cc), prim, b, c)
        dt, db, dc = vjp(ct)
        touts = [dt[i] for i, _ in tok_grads]
        if add is not None:
            touts[0] = touts[0] + t[nt + ncts].astype(f32)
        return touts, list(db) + list(dc)

    all_tok = list(tok) + list(cts) + ([add] if add is not None else [])
    tok_out = [(tok[i][1], dt) for i, dt in tok_grads]
    acc_out = [tuple(a.shape[1:]) for a in bat] + [tuple(a.shape) for a in con]
    return rowcall(name, fn, all_tok, bat, con, tok_out, acc_out, ts)


def _rms(y, w):
    return y * lax.rsqrt(jnp.mean(y * y, axis=-1, keepdims=True) + RMS_EPS) * w


def f_rms_mod(t, b, c):
    return [_rms(t[0], c[0]) * (1.0 + b[0]) + b[1]]


def f_resid(t, b, c):
    return [t[0] + b[0] * _rms(t[1], c[0])]


def f_merge(t, b, c):
    ga, gd, ya, yd = t
    return [jax.nn.sigmoid(ga) * ya + jax.nn.sigmoid(gd) * yd]


def f_dnout(t, b, c):
    o, z = t
    outs = []
    for h in range(DNH):
        sl = slice(h * DND, (h + 1) * DND)
        zh = z[:, sl]
        outs.append(_rms(o[:, sl], c[0]) * (zh * jax.nn.sigmoid(zh)))
    return [jnp.concatenate(outs, axis=1)]


def _softplus(x):
    return jnp.maximum(x, 0.0) + jnp.log(1.0 + jnp.exp(-jnp.abs(x)))


def f_gate(t, b, c):
    ba = t[0]
    a_log, dt_bias = c
    lane = lax.broadcasted_iota(jnp.int32, ba.shape, 1)
    beta = jax.nn.sigmoid(ba)
    g = -jnp.exp(a_log) * _softplus(ba + dt_bias)
    return [jnp.where(lane < DNH, beta, jnp.where(lane < 2 * DNH, g, 0.0))]


def _bucket_table():
    qi = np.arange(WIN)[:, None]
    kj = np.arange(2 * WIN)[None, :]
    dist = np.maximum(WIN + qi - kj, 0)
    max_exact = NBUCK // 2
    scaled = np.log(np.maximum(dist, 1).astype(np.float64) / max_exact) / math.log(MAXDIST / max_exact)
    large = np.minimum(max_exact + (scaled * (NBUCK - max_exact)).astype(np.int32), NBUCK - 1)
    return np.where(dist < max_exact, dist, large).astype(np.int32)


def _attn_mask(n):
    qi = lax.broadcasted_iota(jnp.int32, (WIN, 2 * WIN), 0)
    kj = lax.broadcasted_iota(jnp.int32, (WIN, 2 * WIN), 1)
    dist = WIN + qi - kj
    return (dist >= 0) & (dist < WIN) & ((kj >= WIN) | (n > 0))


def _attn_block(q, kp, kc, vp, vc, bias, sinks, mask, differentiated):
    dot = _bdot_bf16_vjp if differentiated else _bdot_bf16
    grp = HQ // HKV
    band = lambda p, c, j: jnp.concatenate([p[:, j * LANE:(j + 1) * LANE], c[:, j * LANE:(j + 1) * LANE]], axis=0)
    qh = _stack([q[:, h * LANE:(h + 1) * LANE] for h in range(HQ)])
    kb = _stack([band(kp, kc, h // grp) for h in range(HQ)])
    vb = _stack([band(vp, vc, h // grp) for h in range(HQ)])
    s = dot(qh, kb, 2, 2) * (HD ** -0.5)
    s = jnp.where(mask[None], s + bias, NEG_INF)
    m = jnp.maximum(jnp.max(s, axis=-1, keepdims=True), sinks)
    p = jnp.exp(s - m)
    probs = p / (jnp.sum(p, axis=-1, keepdims=True) + jnp.exp(sinks - m))
    o = dot(probs, vb, 2, 1)
    return jnp.concatenate([o[h] for h in range(HQ)], axis=1)


def _attn_specs(NB):
    last = NB - 1
    return [
        pl.BlockSpec((1, WIN, HQ * LANE), lambda b, n: (b, jnp.minimum(n, last), CB_AQ // 8)),
        pl.BlockSpec((1, WIN, HKV * LANE), lambda b, n: (b, jnp.clip(n - 1, 0, last), CB_AK // 2)),
        pl.BlockSpec((1, WIN, HKV * LANE), lambda b, n: (b, jnp.minimum(n, last), CB_AK // 2)),
        pl.BlockSpec((1, WIN, HKV * LANE), lambda b, n: (b, jnp.clip(n - 1, 0, last), CB_AV // 2)),
        pl.BlockSpec((1, WIN, HKV * LANE), lambda b, n: (b, jnp.minimum(n, last), CB_AV // 2)),
        pl.BlockSpec((HQ, WIN, 2 * WIN), lambda b, n: (0, 0, 0)),
        pl.BlockSpec((HQ, 1, 1), lambda b, n: (0, 0, 0)),
    ]


def attn_fwd(proj, bias, sinks, comm):
    B, S, _ = proj.shape
    NB = S // WIN

    def body(q, kp, kc, vp, vc, bias_ref, sink_ref, o_ref):
        mask = _attn_mask(pl.program_id(1))
        o = _attn_block(q[0], kp[0], kc[0], vp[0], vc[0], bias_ref[...], sink_ref[...], mask, False)
        o_ref[0] = o.astype(o_ref.dtype)

    at = lambda b, n: lambda: (pl.program_id(0) == b) & (pl.program_id(1) == n)
    return pl.pallas_call(
        _ride(body, 7, 1, 0, comm, at(0, 0), at(B - 1, (3 * NB) // 4), at(B - 1, NB - 1)), name="attn_fwd", grid=(B, NB),
        in_specs=_attn_specs(NB) + comm.specs,
        out_specs=[pl.BlockSpec((1, WIN, HQ * LANE), lambda b, n: (b, n, 0))] + comm.specs,
        out_shape=[jax.ShapeDtypeStruct((B, S, HQ * LANE), bf16)] + comm.out_shape, scratch_shapes=comm.scratch,
        compiler_params=_cp(("arbitrary", "arbitrary")),
    )(proj, proj, proj, proj, proj, bias, sinks, *comm.arrs)


def attn_bwd(proj, bias, sinks, dy, comm):
    B, S, _ = proj.shape
    NB = S // WIN
    last = NB - 1

    def body(q, kp, kc, vp, vc, bias_ref, sink_ref, dy_ref, dq_ref, dk_ref, dv_ref, dbias_ref, dsink_ref, kcar, vcar):
        b, n = pl.program_id(0), pl.program_id(1)

        @pl.when((b == 0) & (n == 0))
        def _():
            dbias_ref[...] = jnp.zeros(dbias_ref.shape, f32)
            dsink_ref[...] = jnp.zeros(dsink_ref.shape, f32)

        @pl.when(n == 0)
        def _():
            kcar[...] = jnp.zeros(kcar.shape, f32)
            vcar[...] = jnp.zeros(vcar.shape, f32)

        @pl.when(n < NB)
        def _():
            mask = _attn_mask(n)
            _, vjp = jax.vjp(lambda *a: _attn_block(*a, mask, True), q[0], kp[0], kc[0], vp[0], vc[0], bias_ref[...], sink_ref[...])
            dq, dkp, dkc, dvp, dvc, dbias, dsink = vjp(dy_ref[0].astype(f32))
            dq_ref[0] = dq.astype(dq_ref.dtype)
            dbias_ref[...] += dbias
            dsink_ref[...] += dsink
            dk_ref[0] = (kcar[...] + dkp).astype(dk_ref.dtype)
            dv_ref[0] = (vcar[...] + dvp).astype(dv_ref.dtype)
            kcar[...] = dkc
            vcar[...] = dvc

        @pl.when(n == NB)
        def _():
            dk_ref[0] = kcar[...].astype(dk_ref.dtype)
            dv_ref[0] = vcar[...].astype(dv_ref.dtype)

    in_specs = _attn_specs(NB) + [pl.BlockSpec((1, WIN, HQ * LANE), lambda b, n: (b, jnp.minimum(n, last), 0))]
    kv_out = pl.BlockSpec((1, WIN, HKV * LANE), lambda b, n: (b, jnp.maximum(n - 1, 0), 0))
    at = lambda b, n: lambda: (pl.program_id(0) == b) & (pl.program_id(1) == n)
    return pl.pallas_call(
        _ride(body, 8, 5, 2, comm, at(0, 0), at(B - 1, NB), at(B - 1, NB)), name="attn_bwd", grid=(B, NB + 1),
        in_specs=in_specs + comm.specs,
        out_specs=[pl.BlockSpec((1, WIN, HQ * LANE), lambda b, n: (b, jnp.minimum(n, last), 0)), kv_out, kv_out,
                   pl.BlockSpec((HQ, WIN, 2 * WIN), lambda b, n: (0, 0, 0)), pl.BlockSpec((HQ, 1, 1), lambda b, n: (0, 0, 0))] + comm.specs,
        out_shape=[jax.ShapeDtypeStruct((B, S, HQ * LANE), bf16), jax.ShapeDtypeStruct((B, S, HKV * LANE), bf16),
                   jax.ShapeDtypeStruct((B, S, HKV * LANE), bf16), jax.ShapeDtypeStruct((HQ, WIN, 2 * WIN), f32),
                   jax.ShapeDtypeStruct((HQ, 1, 1), f32)] + comm.out_shape,
        scratch_shapes=[pltpu.VMEM((WIN, HKV * LANE), f32), pltpu.VMEM((WIN, HKV * LANE), f32)] + comm.scratch,
        compiler_params=_cp(("arbitrary", "arbitrary")),
    )(proj, proj, proj, proj, proj, bias, sinks, dy, *comm.arrs)


def _causal_conv(x, w, width):
    S, C = x.shape
    xp = jnp.concatenate([jnp.zeros((8, C), f32), x], axis=0)
    out = None
    for j in range(width):
        off = 8 - (width - 1) + j
        term = w[j:j + 1, :] * xp[off:off + S, :]
        out = term if out is None else out + term
    return out


def _dnconv_f(x, w, isqk):
    y = _causal_conv(x, w, DNK)
    y = y * jax.nn.sigmoid(y)
    yn = y * lax.rsqrt(jnp.sum(y * y, axis=-1, keepdims=True) + L2_EPS)
    return jnp.where(isqk, yn, y)


def _dn_outblk(i):
    return (i % DNH) * 3 + i // DNH


def dnconv_fwd(proj, conv_w):
    B, S, _ = proj.shape

    def body(x_ref, w_ref, o_ref):
        o_ref[0] = _dnconv_f(x_ref[0], w_ref[...], pl.program_id(0) < 2 * DNH)

    return pl.pallas_call(
        body, name="dnconv_fwd", grid=(3 * DNH, B),
        in_specs=[pl.BlockSpec((1, S, LANE), lambda i, b: (b, 0, CB_DQKV + i)), pl.BlockSpec((DNK, LANE), lambda i, b: (0, i))],
        out_specs=pl.BlockSpec((1, S, LANE), lambda i, b: (b, 0, _dn_outblk(i))),
        out_shape=jax.ShapeDtypeStruct((B, S, 3 * DNH * DND), f32), compiler_params=_cp(("parallel", "parallel")),
    )(proj, conv_w)


def dnconv_bwd(proj, conv_w, dqkvn):
    B, S, _ = proj.shape

    def body(x_ref, w_ref, dy_ref, dx_ref, dw_ref):
        isqk = pl.program_id(0) < 2 * DNH
        _, vjp = jax.vjp(lambda x, w: _dnconv_f(x, w, isqk), x_ref[0], w_ref[...])
        dx, dw = vjp(dy_ref[0])
        dx_ref[0] = dx.astype(dx_ref.dtype)

        @pl.when(pl.program_id(1) == 0)
        def _():
            dw_ref[...] = jnp.zeros(dw_ref.shape, f32)
        dw_ref[...] += dw

    return pl.pallas_call(
        body, name="dnconv_bwd", grid=(3 * DNH, B),
        in_specs=[pl.BlockSpec((1, S, LANE), lambda i, b: (b, 0, CB_DQKV + i)), pl.BlockSpec((DNK, LANE), lambda i, b: (0, i)),
                  pl.BlockSpec((1, S, LANE), lambda i, b: (b, 0, _dn_outblk(i)))],
        out_specs=[pl.BlockSpec((1, S, LANE), lambda i, b: (b, 0, i)), pl.BlockSpec((DNK, LANE), lambda i, b: (0, i))],
        out_shape=[jax.ShapeDtypeStruct((B, S, 3 * DNH * DND), bf16), jax.ShapeDtypeStruct((DNK, 3 * DNH * DND), f32)],
        compiler_params=_cp(("parallel", "arbitrary")),
    )(proj, conv_w, dqkvn)


def _bdot(a, b, ca, cb, precision=HI):
    return lax.dot_general(a, b, (((ca,), (cb,)), ((0,), (0,))), preferred_element_type=f32, precision=precision)


def _bdot_bf16(a, b, ca, cb):
    return _bdot(a.astype(bf16), b.astype(bf16), ca, cb, None)


@functools.partial(jax.custom_vjp, nondiff_argnums=(2, 3))
def _bdot_bf16_vjp(a, b, ca, cb):
    return _bdot_bf16(a, b, ca, cb)


def _bdot_bf16_fwd(a, b, ca, cb):
    return _bdot_bf16(a, b, ca, cb), (a, b)


def _bdot_bf16_bwd(ca, cb, res, g):
    a, b = res
    fa, fb = 3 - ca, 3 - cb
    da = _bdot_bf16(g, b, 2, fb) if ca == 2 else _bdot_bf16(b, g, fb, 2)
    db = _bdot_bf16(a, g, fa, 1) if cb == 1 else _bdot_bf16(g, a, 1, fa)
    return da, db


_bdot_bf16_vjp.defvjp(_bdot_bf16_fwd, _bdot_bf16_bwd)


def _neumann_inverse(low):
    n = low.shape[-1]
    eye = (lax.broadcasted_iota(jnp.int32, (n, n), 0) == lax.broadcasted_iota(jnp.int32, (n, n), 1)).astype(f32)
    p = -low
    x = eye[None] + p
    for _ in range(5):
        p = _bdot(p, p, 2, 1, MID)
        x = x + _bdot(x, p, 2, 1, MID)
    return x


@jax.custom_vjp
def _unit_lower_inverse(low):
    return _neumann_inverse(low)


def _uli_fwd(low):
    t = _neumann_inverse(low)
    return t, t


def _uli_bwd(t, dt):
    return (-_bdot(_bdot(t, dt, 1, 1, MID), t, 2, 2, MID),)


_unit_lower_inverse.defvjp(_uli_fwd, _uli_bwd)


def _stack(xs):
    return jnp.concatenate([x[None] for x in xs], axis=0)


def _delta_chunk(qkv, bg, state, differentiated):
    inverse = _unit_lower_inverse if differentiated else _neumann_inverse
    lo = _bdot_bf16_vjp if differentiated else _bdot_bf16
    B = qkv.shape[0]
    G = B * DNH
    pairs = [(b, h) for b in range(B) for h in range(DNH)]
    col = lambda b, h, kind: qkv[b, :, (3 * h + kind) * DND:(3 * h + kind + 1) * DND]
    q, k, v = [_stack([col(b, h, kind) for b, h in pairs]) for kind in range(3)]
    lane = lax.broadcasted_iota(jnp.int32, (CH, LANE), 1)
    pick = lambda b, l: jnp.sum(jnp.where(lane == l, bg[b], 0.0), axis=1, keepdims=True)
    beta = _stack([pick(b, h) for b, h in pairs])
    g = _stack([pick(b, h + DNH) for b, h in pairs])
    ri = lax.broadcasted_iota(jnp.int32, (CH, CH), 0)
    ci = lax.broadcasted_iota(jnp.int32, (CH, CH), 1)
    incl, strict = (ri >= ci)[None], (ri > ci)[None]
    gc = _bdot(jnp.broadcast_to(incl.astype(f32), (G, CH, CH)), jnp.broadcast_to(g, (G, CH, LANE)), 2, 1)
    e0 = jnp.broadcast_to((lane == 0).astype(f32)[None], (G, CH, LANE))
    gc_row = _bdot(e0, gc, 2, 2)
    diff = gc[:, :, :CH] - gc_row
    decay = jnp.where(incl, jnp.exp(jnp.where(incl, diff, 0.0)), 0.0)
    qs = q * (DND ** -0.5)
    kb, vb = k * beta, v * beta
    eg = jnp.exp(gc)
    low = jnp.where(strict, lo(kb, k, 2, 2) * decay, 0.0)
    tinv = inverse(low)
    u = _bdot(tinv, vb, 2, 1, MID)
    w = _bdot(tinv, kb * eg, 2, 1, MID)
    intra = jnp.where(incl, lo(qs, k, 2, 2) * decay, 0.0)
    gl = gc[:, CH - 1:CH, :]
    k_tail = k * jnp.exp(gl - gc)
    v_new = u - lo(w, state, 2, 1)
    o = lo(qs * eg, state, 2, 1) + lo(intra, v_new, 2, 1)
    new_state = state * jnp.exp(gl) + lo(k_tail, v_new, 1, 1)
    return o, new_state


def delta_fwd(qkvn, bg, comm):
    B, S, _ = qkvn.shape
    NC, G = S // CH, B * DNH

    def body(qkv_ref, bg_ref, o_ref, st_ref, state):
        @pl.when(pl.program_id(0) == 0)
        def _():
            state[...] = jnp.zeros(state.shape, f32)
        s0 = state[...]
        st_ref[0] = s0
        o, s1 = _delta_chunk(qkv_ref[...], bg_ref[...], s0, False)
        for b in range(B):
            for h in range(DNH):
                o_ref[b, :, h * DND:(h + 1) * DND] = o[b * DNH + h]
        state[...] = s1

    at = lambda c: lambda: pl.program_id(0) == c
    return pl.pallas_call(
        _ride(body, 2, 2, 1, comm, at(0), at((7 * NC) // 8), at(NC - 1)), name="delta_fwd", grid=(NC,),
        in_specs=[pl.BlockSpec((B, CH, 3 * DNH * DND), lambda c: (0, c, 0)), pl.BlockSpec((B, CH, LANE), lambda c: (0, c, 0))] + comm.specs,
        out_specs=[pl.BlockSpec((B, CH, DNH * DND), lambda c: (0, c, 0)), pl.BlockSpec((1, G, DND, DND), lambda c: (c, 0, 0, 0))] + comm.specs,
        out_shape=[jax.ShapeDtypeStruct((B, S, DNH * DND), f32), jax.ShapeDtypeStruct((NC, G, DND, DND), f32)] + comm.out_shape,
        scratch_shapes=[pltpu.VMEM((G, DND, DND), f32)] + comm.scratch, compiler_params=_cp(("arbitrary",)),
    )(qkvn, bg, *comm.arrs)


def delta_bwd(qkvn, bg, states, do, comm):
    B, S, _ = qkvn.shape
    NC, G = S // CH, B * DNH

    def body(qkv_ref, bg_ref, st_ref, do_ref, dqkv_ref, dbg_ref, dstate):
        @pl.when(pl.program_id(0) == 0)
        def _():
            dstate[...] = jnp.zeros(dstate.shape, f32)
        _, vjp = jax.vjp(lambda a, g, s: _delta_chunk(a, g, s, True), qkv_ref[...], bg_ref[...], st_ref[0])
        do = _stack([do_ref[b, :, h * DND:(h + 1) * DND] for b in range(B) for h in range(DNH)])
        dqkv, dbg, ds = vjp((do, dstate[...]))
        dqkv_ref[...] = dqkv
        dbg_ref[...] = dbg
        dstate[...] = ds

    rev = lambda c: NC - 1 - c
    at = lambda c: lambda: pl.program_id(0) == c
    return pl.pallas_call(
        _ride(body, 4, 2, 1, comm, at(0), at(NC - 1), at(NC - 1)), name="delta_bwd", grid=(NC,),
        in_specs=[pl.BlockSpec((B, CH, 3 * DNH * DND), lambda c: (0, rev(c), 0)), pl.BlockSpec((B, CH, LANE), lambda c: (0, rev(c), 0)),
                  pl.BlockSpec((1, G, DND, DND), lambda c: (rev(c), 0, 0, 0)),
                  pl.BlockSpec((B, CH, DNH * DND), lambda c: (0, rev(c), 0))] + comm.specs,
        out_specs=[pl.BlockSpec((B, CH, 3 * DNH * DND), lambda c: (0, rev(c), 0)),
                   pl.BlockSpec((B, CH, LANE), lambda c: (0, rev(c), 0))] + comm.specs,
        out_shape=[jax.ShapeDtypeStruct((B, S, 3 * DNH * DND), f32), jax.ShapeDtypeStruct((B, S, LANE), f32)] + comm.out_shape,
        scratch_shapes=[pltpu.VMEM((G, DND, DND), f32)] + comm.scratch, compiler_params=_cp(("arbitrary",)),
    )(qkvn, bg, states, do, *comm.arrs)


def _ffn_f(gate_x, val_x, gate_w, val_w):
    gate = _causal_conv(gate_x, gate_w, FK)
    val = _causal_conv(val_x, val_w, FK)
    gl = 0.5 * gate * (1.0 + jnp.tanh(math.sqrt(2.0 / math.pi) * (gate + 0.044715 * gate * gate * gate)))
    return gl * val


def _ffn_specs(S):
    nblk = DFF // LANE
    return [pl.BlockSpec((1, S, LANE), lambda i, b: (b, 0, i)), pl.BlockSpec((1, S, LANE), lambda i, b: (b, 0, nblk + i)),
            pl.BlockSpec((FK, LANE), lambda i, b: (0, i)), pl.BlockSpec((FK, LANE), lambda i, b: (0, nblk + i))]


def ffnconv_fwd(up, conv_w):
    B, S, _ = up.shape

    def body(g_ref, v_ref, gw_ref, vw_ref, o_ref):
        o_ref[0] = _ffn_f(g_ref[0], v_ref[0], gw_ref[...], vw_ref[...]).astype(o_ref.dtype)

    return pl.pallas_call(
        body, name="ffnconv_fwd", grid=(DFF // LANE, B), in_specs=_ffn_specs(S),
        out_specs=pl.BlockSpec((1, S, LANE), lambda i, b: (b, 0, i)),
        out_shape=jax.ShapeDtypeStruct((B, S, DFF), bf16), compiler_params=_cp(("parallel", "parallel")),
    )(up, up, conv_w, conv_w)


def ffnconv_bwd(up, conv_w, dact):
    B, S, _ = up.shape

    def body(g_ref, v_ref, gw_ref, vw_ref, dy_ref, dx_ref, dw_ref):
        _, vjp = jax.vjp(_ffn_f, g_ref[0], v_ref[0], gw_ref[...], vw_ref[...])
        dg, dv, dgw, dvw = vjp(dy_ref[0].astype(f32))
        dx_ref[0, 0] = dg.astype(dx_ref.dtype)
        dx_ref[1, 0] = dv.astype(dx_ref.dtype)

        @pl.when(pl.program_id(1) == 0)
        def _():
            dw_ref[...] = jnp.zeros(dw_ref.shape, f32)
        dw_ref[0] += dgw
        dw_ref[1] += dvw

    return pl.pallas_call(
        body, name="ffnconv_bwd", grid=(DFF // LANE, B),
        in_specs=_ffn_specs(S) + [pl.BlockSpec((1, S, LANE), lambda i, b: (b, 0, i))],
        out_specs=[pl.BlockSpec((2, 1, S, LANE), lambda i, b: (0, b, 0, i)), pl.BlockSpec((2, FK, LANE), lambda i, b: (0, 0, i))],
        out_shape=[jax.ShapeDtypeStruct((2, B, S, DFF), bf16), jax.ShapeDtypeStruct((2, FK, DFF), f32)],
        compiler_params=_cp(("parallel", "arbitrary")),
    )(up, up, conv_w, conv_w, dact)


def ada_fwd(c_all, ada_w, ada_b):
    def body(c_ref, w_ref, b_ref, o_ref):
        c = c_ref[...]
        act = (c * jax.nn.sigmoid(c)).astype(bf16)
        o_ref[...] = jnp.dot(act, w_ref[...].astype(bf16), preferred_element_type=f32) + b_ref[...]

    return pl.pallas_call(body, name="ada_fwd", out_shape=jax.ShapeDtypeStruct((c_all.shape[0], ada_w.shape[1]), f32),
                          compiler_params=pltpu.CompilerParams(vmem_limit_bytes=VMEM_LIMIT))(c_all, ada_w, ada_b)


def ada_bwd(c_all, dmod):
    def body(c_ref, d_ref, o_ref):
        c = c_ref[...]
        act = (c * jax.nn.sigmoid(c)).astype(bf16)
        o_ref[...] = lax.dot_general(act, d_ref[...].astype(bf16), (((0,), (0,)), ((), ())), preferred_element_type=f32)

    return pl.pallas_call(body, name="ada_bwd", out_shape=jax.ShapeDtypeStruct((c_all.shape[1], dmod.shape[1]), f32),
                          compiler_params=pltpu.CompilerParams(vmem_limit_bytes=VMEM_LIMIT))(c_all, dmod)


def loss_head(h1, y2, target, g2, w):
    def fn(t, b, c):
        h, y, tg = [v.astype(f32) for v in t]

        def loss_fn(h, y, g, w):
            e = h + g * _rms(y, w) - tg
            return 0.5 * jnp.sum(jnp.mean(e * e, axis=-1))

        loss, grads = jax.value_and_grad(loss_fn, argnums=(0, 1, 2, 3))(h, y, b[0], c[0])
        return [grads[0], grads[1]], [grads[2], grads[3], jnp.full((1, LANE), loss, f32)]

    return rowcall("loss_head", fn, [(h1, D, 0), (y2, D, 0), (target, D, 0)], [g2], [w], [(D, f32), (D, bf16)],
                   [(1, D), (1, D), (1, LANE)])


def adamw(w, gparts, m, v, name):
    R, C = w.shape
    P = gparts.shape[0]
    tr = R
    if R * C * 4 > 2 * 1024 * 1024:
        for cand in (512, 256, 128, 64, 32, 16, 8):
            if R % cand == 0 and cand * C * 4 <= 2 * 1024 * 1024:
                tr = cand
                break

    def body(w_ref, g_ref, m_ref, v_ref, go, do, mo, vo):
        g = g_ref[0].astype(f32)
        for p in range(1, P):
            g = g + g_ref[p].astype(f32)
        m2 = B1 * m_ref[...] + (1.0 - B1) * g
        v2 = B2 * v_ref[...] + (1.0 - B2) * jnp.square(g)
        m_hat = m2 / (1.0 - B1 ** STEP)
        v_hat = v2 / (1.0 - B2 ** STEP)
        go[...] = g
        do[...] = -LR * (m_hat / (jnp.sqrt(v_hat) + EPS) + WD * w_ref[...])
        mo[...] = m2
        vo[...] = v2

    blk = pl.BlockSpec((tr, C), lambda i: (i, 0))
    return pl.pallas_call(
        body, name=name, grid=(R // tr,), in_specs=[blk, pl.BlockSpec((P, tr, C), lambda i: (0, i, 0)), blk, blk],
        out_specs=[blk] * 4, out_shape=[jax.ShapeDtypeStruct((R, C), f32)] * 4, compiler_params=_cp(("parallel",)),
    )(w, gparts, m, v)


def _pad_heads(w, nh):
    r = w.shape[0]
    return jnp.pad(w.reshape(r, nh, HD), ((0, 0), (0, 0), (0, LANE - HD))).reshape(r, nh * LANE)


def _unpad_heads(w, nh):
    return w.reshape(w.shape[0], nh, LANE)[:, :, :HD].reshape(w.shape[0], nh * HD)


def _pack_w_in(w):
    aq, ak, av, dqkv, dz, dbeta, da, ga, gd = jnp.split(w, np.cumsum(IN_SPLITS)[:-1].tolist(), axis=1)
    ba = jnp.pad(jnp.concatenate([dbeta, da], axis=1), ((0, 0), (0, LANE - 2 * DNH)))
    return jnp.concatenate([ga, gd, _pad_heads(aq, HQ), dqkv, dz, _pad_heads(ak, HKV), _pad_heads(av, HKV), ba], axis=1)


def _unpack_w_in(p):
    col = lambda cb, n: p[:, cb * LANE: cb * LANE + n]
    ba = col(CB_BA, 2 * DNH)
    return jnp.concatenate([_unpad_heads(col(CB_AQ, HQ * LANE), HQ), _unpad_heads(col(CB_AK, HKV * LANE), HKV),
                            _unpad_heads(col(CB_AV, HKV * LANE), HKV), col(CB_DQKV, 3 * DNH * DND), col(CB_DZ, DNH * DND),
                            ba[:, :DNH], ba[:, DNH:], col(CB_GA, D), col(CB_GD, D)], axis=1)


def _cols_gathered(g):
    return g.transpose(1, 0, 2).reshape(g.shape[1], NDEV * g.shape[2])


def _cols_split(w):
    r = w.shape[0]
    return w.reshape(r, NDEV, w.shape[1] // NDEV).transpose(1, 0, 2)


def kernel(x, c, ada_w, ada_b, norm_mix_pre, norm_mix_post, norm_ffn_pre, norm_ffn_post, w_in, dn_conv_w, dn_a_log, dn_dt_bias, dn_norm_w, attn_sinks, rel_bias, w_attn_branch, w_dn_branch, w_out, ffn_w_up, ffn_conv_w, ffn_w_down, loss_target, m_ada_w, m_ada_b, m_norm_mix_pre, m_norm_mix_post, m_norm_ffn_pre, m_norm_ffn_post, m_w_in, m_dn_conv_w, m_dn_a_log, m_dn_dt_bias, m_dn_norm_w, m_attn_sinks, m_rel_bias, m_w_attn_branch, m_w_dn_branch, m_w_out, m_ffn_w_up, m_ffn_conv_w, m_ffn_w_down, v_ada_w, v_ada_b, v_norm_mix_pre, v_norm_mix_post, v_norm_ffn_pre, v_norm_ffn_post, v_w_in, v_dn_conv_w, v_dn_a_log, v_dn_dt_bias, v_dn_norm_w, v_attn_sinks, v_rel_bias, v_w_attn_branch, v_w_dn_branch, v_w_out, v_ffn_w_up, v_ffn_conv_w, v_ffn_w_down):
    B, S, _ = x.shape
    T = B * S
    me = 4 * lax.axis_index("x") + 2 * lax.axis_index("y") + lax.axis_index("c")
    big = dict(w_in=w_in, dn_conv_w=dn_conv_w, w_attn_branch=w_attn_branch, w_dn_branch=w_dn_branch, w_out=w_out,
               ffn_w_up=ffn_w_up, ffn_conv_w=ffn_conv_w, ffn_w_down=ffn_w_down)
    big_names = list(big)

    first, mid, late = ["w_in", "dn_conv_w"], ["w_attn_branch", "w_dn_branch", "w_out"], ["ffn_w_up", "ffn_conv_w", "ffn_w_down"]
    shard = lambda names: [big[n][0].astype(bf16) for n in names]
    gw = dict(zip(first, _exchange(shard(first), "gather_w_in", two_level=True)))
    (c_all,) = _exchange([c], "gather_c")
    c_all = c_all.reshape(NDEV * B, D)

    wp = _pack_w_in(_cols_gathered(gw["w_in"]))
    conv_dn = _cols_gathered(gw["dn_conv_w"]).astype(f32)

    ncol = ada_w.shape[2]
    ada_b_mine = lax.dynamic_slice_in_dim(ada_b, me * ncol, ncol, axis=1)
    mod_cols = ada_fwd(c_all, ada_w[0], ada_b_mine)
    (mod_g,) = _exchange([mod_cols], "gather_mod")
    mod = lax.dynamic_slice_in_dim(mod_g, me * B, B, axis=1).transpose(1, 0, 2).reshape(B, NMOD * D)
    sh1, sc1, g1, sh2, sc2, g2 = [mod[:, i * D:(i + 1) * D].reshape(B, 1, D) for i in range(NMOD)]

    onehot = (jnp.asarray(_bucket_table()).reshape(1, -1) == jnp.arange(NBUCK, dtype=jnp.int32)[:, None]).astype(f32)
    bias = mm(rel_bias.T, onehot, "nn", f32, "bias_table", tn=8192, precision=HI).reshape(HQ, WIN, 2 * WIN)
    sinks = attn_sinks.reshape(HQ, 1, 1)
    a_log_pad = jnp.pad(dn_a_log, ((0, 0), (DNH, LANE - 2 * DNH)))
    dt_bias_pad = jnp.pad(dn_dt_bias, ((0, 0), (DNH, LANE - 2 * DNH)))

    (u1,) = rowcall_fwd("mix_pre", f_rms_mod, [(x, D, 0)], [sc1, sh1], [norm_mix_pre], [(D, bf16)])
    proj = mm(u1.reshape(T, D), wp, "nn", f32, "proj", tn=1152).reshape(B, S, NP)
    ya, *got = attn_fwd(proj, bias, sinks, _Comm(shard(mid), two_level=True))
    gw.update(zip(mid, got))
    wa = _cols_gathered(gw["w_attn_branch"])
    wa = jnp.pad(wa.reshape(HQ, HD, D), ((0, 0), (0, LANE - HD), (0, 0))).reshape(HQ * LANE, D)
    wd = _cols_gathered(gw["w_dn_branch"])
    wo = gw["w_out"].reshape(D, D)
    qkvn = dnconv_fwd(proj, conv_dn)
    (bg,) = rowcall_fwd("dn_gate", f_gate, [(proj, LANE, CB_BA)], [], [a_log_pad, dt_bias_pad], [(LANE, f32)])
    o_dn, states, *got = delta_fwd(qkvn, bg, _Comm(shard(late), two_level=True))
    gw.update(zip(late, got))
    wup = _cols_gathered(gw["ffn_w_up"])
    conv_ffn = _cols_gathered(gw["ffn_conv_w"]).astype(f32)
    wdown = gw["ffn_w_down"].reshape(DFF, D)
    (yd,) = rowcall_fwd("dn_out", f_dnout, [(o_dn, DNH * DND, 0), (proj, DNH * DND, CB_DZ // 4)], [], [dn_norm_w], [(DNH * DND, bf16)])
    pa = mm(ya.reshape(T, HQ * LANE), wa, "nn", f32, "attn_branch").reshape(B, S, D)
    pd = mm(yd.reshape(T, DNH * DND), wd, "nn", f32, "dn_branch").reshape(B, S, D)
    merge_tok = [(proj, D, CB_GA // 8), (proj, D, CB_GD // 8), (pa, D, 0), (pd, D, 0)]
    (merged,) = rowcall_fwd("merge", f_merge, merge_tok, [], [], [(D, bf16)])
    y1 = mm(merged.reshape(T, D), wo, "nn", f32, "mix_out").reshape(B, S, D)
    (h1,) = rowcall_fwd("mix_post", f_resid, [(x, D, 0), (y1, D, 0)], [g1], [norm_mix_post], [(D, f32)])
    (u2,) = rowcall_fwd("ffn_pre", f_rms_mod, [(h1, D, 0)], [sc2, sh2], [norm_ffn_pre], [(D, bf16)])
    up = mm(u2.reshape(T, D), wup, "nn", f32, "ffn_up", tn=1408).reshape(B, S, 2 * DFF)
    act = ffnconv_fwd(up, conv_ffn)
    y2 = mm(act.reshape(T, DFF), wdown, "nn", f32, "ffn_down", tk=1408).reshape(B, S, D)

    dh1_a, dy2, dg2, dw_ffn_post, loss_b = loss_head(h1, y2, loss_target, g2, norm_ffn_post)
    dy2f = dy2.reshape(T, D)
    dact = mm(dy2f, wdown, "nt", bf16, "ffn_down_dx", tn=1408).reshape(B, S, DFF)
    g_wdown = mm(act.reshape(T, DFF), dy2f, "tn", f32, "ffn_down_dw", tm=1408, tk=2048)
    dup, g_conv_ffn = ffnconv_bwd(up, conv_ffn, dact)
    dupf = dup.reshape(2, T, DFF)
    g_conv_ffn = g_conv_ffn.transpose(1, 0, 2).reshape(FK, 2 * DFF)
    du2 = mm(dupf, wup, "nt", f32, "ffn_up_dx", tk=1408).reshape(B, S, D)
    g_wup = mm(u2.reshape(T, D), dupf, "tn", f32, "ffn_up_dw", tn=1408, tk=2048)
    dh1, dsc2, dsh2, dw_ffn_pre = rowcall_bwd("ffn_pre_bwd", f_rms_mod, [(h1, D, 0)], [sc2, sh2], [norm_ffn_pre], [(du2, D, 0)],
                                              [(0, f32)], add=(dh1_a, D, 0))
    dy1, dg1, dw_mix_post = rowcall_bwd("mix_post_bwd", f_resid, [(x, D, 0), (y1, D, 0)], [g1], [norm_mix_post], [(dh1, D, 0)],
                                        [(1, bf16)])
    dy1f = dy1.reshape(T, D)
    dmerged = mm(dy1f, wo, "nt", f32, "mix_out_dx").reshape(B, S, D)
    g_wo = mm(merged.reshape(T, D), dy1f, "tn", f32, "mix_out_dw", tk=2048)
    dga, dgd, dpa, dpd = rowcall_bwd("merge_bwd", f_merge, merge_tok, [], [], [(dmerged, D, 0)],
                                     [(0, bf16), (1, bf16), (2, bf16), (3, bf16)])
    dpaf, dpdf = dpa.reshape(T, D), dpd.reshape(T, D)
    dya = mm(dpaf, wa, "nt", bf16, "attn_branch_dx").reshape(B, S, HQ * LANE)
    g_wa = mm(ya.reshape(T, HQ * LANE), dpaf, "tn", f32, "attn_branch_dw", tk=2048)
    dyd = mm(dpdf, wd, "nt", f32, "dn_branch_dx").reshape(B, S, DNH * DND)
    g_wd = mm(yd.reshape(T, DNH * DND), dpdf, "tn", f32, "dn_branch_dw", tk=2048)
    do_dn, dz, dw_dn_norm = rowcall_bwd("dn_out_bwd", f_dnout, [(o_dn, DNH * DND, 0), (proj, DNH * DND, CB_DZ // 4)], [], [dn_norm_w],
                                        [(dyd, DNH * DND, 0)], [(0, f32), (1, bf16)])
    parts = {}
    outbox = lambda d: _Comm([d[n].astype(bf16) for n in d], scatter=True)
    send = dict(ffn_w_up=_cols_split(g_wup), ffn_conv_w=_cols_split(g_conv_ffn),
                ffn_w_down=g_wdown.reshape(NDEV, DFF // NDEV, D))
    dqkvn, dbg, *got = delta_bwd(qkvn, bg, states, do_dn, outbox(send))
    parts.update(zip(send, got))
    dba, da_log_pad, ddt_bias_pad = rowcall_bwd("dn_gate_bwd", f_gate, [(proj, LANE, CB_BA)], [], [a_log_pad, dt_bias_pad],
                                                [(dbg, LANE, 0)], [(0, bf16)])
    ddqkv, g_conv_dn = dnconv_bwd(proj, conv_dn, dqkvn)
    send = dict(w_attn_branch=_cols_split(g_wa.reshape(HQ, LANE, D)[:, :HD].reshape(HQ * HD, D)), w_dn_branch=_cols_split(g_wd),
                w_out=g_wo.reshape(NDEV, D // NDEV, D))
    dq, dk, dv, dbias, dsinks, *got = attn_bwd(proj, bias, sinks, dya, outbox(send))
    parts.update(zip(send, got))
    dproj = jnp.concatenate([dga, dgd, dq, ddqkv, dz, dk, dv, dba], axis=2).reshape(T, NP)
    g_wp = mm(u1.reshape(T, D), dproj, "tn", f32, "proj_dw", tn=1152, tk=2048)
    send = dict(w_in=_cols_split(_unpack_w_in(g_wp)), dn_conv_w=_cols_split(g_conv_dn))
    du1, *got = mm(dproj, wp, "nt", f32, "proj_dx", tk=1152, comm=outbox(send))
    parts.update(zip(send, got))
    du1 = du1.reshape(B, S, D)
    grad_x, dsc1, dsh1, dw_mix_pre = rowcall_bwd("mix_pre_bwd", f_rms_mod, [(x, D, 0)], [sc1, sh1], [norm_mix_pre], [(du1, D, 0)],
                                                 [(0, f32)], add=(dh1, D, 0))
    g_rel = mm(dbias.reshape(HQ, WIN * 2 * WIN), onehot, "nt", f32, "rel_bias_dw", tk=8192, precision=HI)

    dmod = jnp.concatenate([dsh1, dsc1, dg1, dsh2, dsc2, dg2], axis=2).reshape(B, NMOD * D)
    (dmod_g,) = _exchange([dmod], "gather_dmod")
    dmod_cols = lax.dynamic_slice_in_dim(dmod_g.reshape(NDEV * B, NMOD * D), me * ncol, ncol, axis=1)
    g_ada_w = ada_bwd(c_all, dmod_cols)

    zrow = lambda a: jnp.concatenate([a.reshape(1, -1), jnp.zeros((B - 1, a.size), f32)], axis=0)
    small_g = jnp.concatenate([
        dmod, dw_mix_pre.reshape(B, D), dw_mix_post.reshape(B, D), dw_ffn_pre.reshape(B, D), dw_ffn_post.reshape(B, D),
        da_log_pad.reshape(B, LANE)[:, DNH:2 * DNH], ddt_bias_pad.reshape(B, LANE)[:, DNH:2 * DNH], dw_dn_norm.reshape(B, DND),
        zrow(dsinks), zrow(g_rel.T), loss_b.reshape(B, LANE)[:, :1], jnp.zeros((B, SMALL_PAD - SMALL_N - 1), f32)], axis=1)
    (small_all,) = _exchange([small_g], "gather_small")
    small_w = dict(ada_b=(ada_b, m_ada_b, v_ada_b), norm_mix_pre=(norm_mix_pre, m_norm_mix_pre, v_norm_mix_pre),
                   norm_mix_post=(norm_mix_post, m_norm_mix_post, v_norm_mix_post), norm_ffn_pre=(norm_ffn_pre, m_norm_ffn_pre, v_norm_ffn_pre),
                   norm_ffn_post=(norm_ffn_post, m_norm_ffn_post, v_norm_ffn_post), dn_a_log=(dn_a_log, m_dn_a_log, v_dn_a_log),
                   dn_dt_bias=(dn_dt_bias, m_dn_dt_bias, v_dn_dt_bias), dn_norm_w=(dn_norm_w, m_dn_norm_w, v_dn_norm_w),
                   attn_sinks=(attn_sinks, m_attn_sinks, v_attn_sinks), rel_bias=(rel_bias, m_rel_bias, v_rel_bias))

    def pack(i, fill):
        row = jnp.concatenate([small_w[n][i].reshape(1, -1) for n, _ in SMALL], axis=1)
        return jnp.pad(row, ((0, 0), (0, SMALL_PAD - SMALL_N)), constant_values=fill)

    small_out = adamw(pack(0, 0.0), small_all.reshape(NDEV * B, 1, SMALL_PAD), pack(1, 0.0), pack(2, 1.0), "adamw_small")
    loss = small_out[0][0, SMALL_N]

    res = {}
    off = 0
    for n, size in SMALL:
        shp = small_w[n][0].shape
        res[n] = [o[:, off:off + size].reshape(shp) for o in small_out]
        off += size
    res["ada_w"] = [o[None] for o in adamw(ada_w[0], g_ada_w[None], m_ada_w[0], v_ada_w[0], "adamw_ada_w")]
    moments = dict(w_in=(m_w_in, v_w_in), dn_conv_w=(m_dn_conv_w, v_dn_conv_w), w_attn_branch=(m_w_attn_branch, v_w_attn_branch),
                   w_dn_branch=(m_w_dn_branch, v_w_dn_branch), w_out=(m_w_out, v_w_out), ffn_w_up=(m_ffn_w_up, v_ffn_w_up),
                   ffn_conv_w=(m_ffn_conv_w, v_ffn_conv_w), ffn_w_down=(m_ffn_w_down, v_ffn_w_down))
    for n in big_names:
        res[n] = [o[None] for o in adamw(big[n][0], parts[n], moments[n][0][0], moments[n][1][0], "adamw_" + n)]

    order = ["ada_w", "ada_b", "norm_mix_pre", "norm_mix_post", "norm_ffn_pre", "norm_ffn_post", "w_in", "dn_conv_w", "dn_a_log",
             "dn_dt_bias", "dn_norm_w", "attn_sinks", "rel_bias", "w_attn_branch", "w_dn_branch", "w_out", "ffn_w_up", "ffn_conv_w",
             "ffn_w_down"]
    return (loss, grad_x, *[res[n][0] for n in order], *[res[n][1] for n in order], *[res[n][2] for n in order],
            *[res[n][3] for n in order])
```

```python
import functools
import math

import numpy as np
import jax
import jax.numpy as jnp
from jax import lax
from jax.experimental import pallas as pl
from jax.experimental.pallas import tpu as pltpu

f32 = jnp.float32
bf16 = jnp.bfloat16
HI = lax.Precision.HIGHEST
MID = lax.Precision.HIGH
MESH = pl.DeviceIdType.MESH

NDEV = 8
D = 1024
HQ, HKV, HD, WIN, NBUCK, MAXDIST = 8, 2, 64, 128, 32, 128
DNH, DND, DNK, CH = 4, 128, 4, 64
DFF, FK = 2816, 3
NMOD = 6
RMS_EPS = 1e-6
L2_EPS = 1e-6
NEG_INF = -1e30
LR, B1, B2, EPS, WD, STEP = 0.001, 0.9, 0.999, 1e-08, 0.01, 10

LANE = 128
CB_GA, CB_GD, CB_AQ, CB_DQKV, CB_DZ, CB_AK, CB_AV, CB_BA, NPB = 0, 8, 16, 24, 36, 40, 42, 44, 45
NP = NPB * LANE
IN_SPLITS = (HQ * HD, HKV * HD, HKV * HD, 3 * DNH * DND, DNH * DND, DNH, DNH, D, D)
IN_DIM = sum(IN_SPLITS)
VMEM_LIMIT = 56 * 1024 * 1024

SMALL = (("ada_b", NMOD * D), ("norm_mix_pre", D), ("norm_mix_post", D), ("norm_ffn_pre", D), ("norm_ffn_post", D),
         ("dn_a_log", DNH), ("dn_dt_bias", DNH), ("dn_norm_w", DND), ("attn_sinks", HQ), ("rel_bias", NBUCK * HQ))
SMALL_N = sum(n for _, n in SMALL)
SMALL_PAD = 10752


def _cp(sem):
    return pltpu.CompilerParams(dimension_semantics=sem, vmem_limit_bytes=VMEM_LIMIT)


def _pick(dim, target):
    if dim <= target:
        return dim
    best = None
    for d in range(LANE, target + 1, LANE):
        if dim % d == 0:
            best = d
    assert best is not None, (dim, target)
    return best


def _me():
    x, y, c = lax.axis_index("x"), lax.axis_index("y"), lax.axis_index("c")
    return x, y, c, 4 * x + 2 * y + c


def _peer(x, y, c, k):
    px = 1 - x if k & 4 else x
    py = 1 - y if k & 2 else y
    pc = 1 - c if k & 1 else c
    return (px, py, pc), 4 * px + 2 * py + pc


class _Comm:
    def __init__(self, arrs, scatter=False, two_level=False):
        assert not (scatter and two_level)
        self.arrs, self.n, self.scatter, self.two_level = list(arrs), len(arrs), scatter, two_level
        if scatter:
            self.out_shape = [jax.ShapeDtypeStruct(a.shape, a.dtype) for a in arrs]
        else:
            self.out_shape = [jax.ShapeDtypeStruct((NDEV,) + a.shape, a.dtype) for a in arrs]
        nsem = self.n * (NDEV - 1)
        self.scratch = [pltpu.SemaphoreType.DMA((nsem,)), pltpu.SemaphoreType.DMA((nsem,)), pltpu.SemaphoreType.DMA((self.n,))]
        self.specs = [pl.BlockSpec(memory_space=pl.ANY)] * self.n

    def phases(self, ins, out, send, recv, loc):
        x, y, c, me = _me()

        def remote(a, k, src, dst, to):
            s = a * (NDEV - 1) + k - 1
            return pltpu.make_async_remote_copy(src_ref=src, dst_ref=dst, send_sem=send.at[s], recv_sem=recv.at[s],
                                                device_id=to, device_id_type=MESH)

        def local(a):
            return pltpu.make_async_copy(ins[a].at[me] if self.scatter else ins[a], out[a].at[me], loc.at[a])

        if not self.two_level:
            def mine(a, k):
                peer, pid = _peer(x, y, c, k)
                return remote(a, k, ins[a].at[pid] if self.scatter else ins[a], out[a].at[me], peer)

            def theirs(a, k):
                peer, pid = _peer(x, y, c, k)
                return remote(a, k, ins[a].at[pid] if self.scatter else ins[a], out[a].at[pid], peer)

            def start():
                for a in range(self.n):
                    local(a).start()
                    for k in range(1, NDEV):
                        mine(a, k).start()

            def forward():
                pass

            def finish():
                for a in range(self.n):
                    for k in range(1, NDEV):
                        mine(a, k).wait_send()
                    for k in range(1, NDEV):
                        theirs(a, k).wait_recv()
                    local(a).wait()

            return start, forward, finish

        sibling = (x, y, 1 - c)
        chips = [(1 - x, y), (x, 1 - y), (1 - x, 1 - y)]
        slot = lambda px, py, pc: 4 * px + 2 * py + pc

        def own(a, k, to):
            return remote(a, k, ins[a], out[a].at[me], to)

        def landed(a, k, frm):
            return remote(a, k, ins[a], out[a].at[slot(*frm)], frm)

        def passed(a, j):
            rows = out[a].at[slot(*chips[j], c)]
            return remote(a, 5 + j, rows, rows, sibling)

        def start():
            for a in range(self.n):
                local(a).start()
                own(a, 1, sibling).start()
                for j, chip in enumerate(chips):
                    own(a, 2 + j, (*chip, c)).start()

        def forward():
            for a in range(self.n):
                for j, chip in enumerate(chips):
                    landed(a, 2 + j, (*chip, c)).wait_recv()
                    passed(a, j).start()

        def finish():
            for a in range(self.n):
                landed(a, 1, sibling).wait_recv()
                for j, chip in enumerate(chips):
                    remote(a, 5 + j, ins[a], out[a].at[slot(*chip, 1 - c)], sibling).wait_recv()
                own(a, 1, sibling).wait_send()
                for j, chip in enumerate(chips):
                    own(a, 2 + j, (*chip, c)).wait_send()
                    passed(a, j).wait_send()
                local(a).wait()

        return start, forward, finish


def _ride(body, n_in, n_out, n_scr, comm, first, mid, last):
    k = comm.n

    def wrapped(*refs):
        ins, cins = refs[:n_in], refs[n_in:n_in + k]
        o0 = n_in + k
        outs, couts = refs[o0:o0 + n_out], refs[o0 + n_out:o0 + n_out + k]
        s0 = o0 + n_out + k
        scr, sems = refs[s0:s0 + n_scr], refs[s0 + n_scr:]
        start, forward, finish = comm.phases(cins, couts, *sems)
        pl.when(first())(start)
        body(*ins, *outs, *scr)
        pl.when(mid())(forward)
        pl.when(last())(finish)

    return wrapped


def _exchange(arrs, name, scatter=False, two_level=False):
    comm = _Comm(arrs, scatter, two_level)

    def body(*refs):
        start, forward, finish = comm.phases(refs[:comm.n], refs[comm.n:2 * comm.n], *refs[2 * comm.n:])
        start()
        forward()
        finish()

    return pl.pallas_call(body, name=name, out_shape=comm.out_shape, in_specs=comm.specs, out_specs=comm.specs,
                          scratch_shapes=comm.scratch, compiler_params=pltpu.CompilerParams(has_side_effects=True))(*arrs)


def mm(a, b, mode, out_dtype, name, tm=1024, tn=1024, tk=1024, precision=None, comm=None, b_cols=None):
    a_parts = a.shape[0] if a.ndim == 3 else 1
    b_parts = b.shape[0] if b.ndim == 3 else 1
    assert (a_parts == 1 or mode == "nt") and (b_parts == 1 or mode == "tn")
    ash, bsh = a.shape[-2:], b.shape[-2:]
    if mode == "nn":
        (M, K), (K2, N) = ash, bsh
    elif mode == "nt":
        (M, K), (N, K2) = (ash[0], ash[1] * a_parts), bsh
    else:
        (K, M), (K2, N) = ash, (bsh[0], bsh[1] * b_parts)
    assert K == K2, (name, a.shape, b.shape)
    col0 = 0
    if b_cols is not None:
        assert mode == "nn" and tn % LANE == 0
        col0, N = b_cols[0], b_cols[1] * tn
    tm, tn, tk = _pick(M, tm), _pick(N // b_parts, tn), _pick(K // a_parts, tk)
    nk = K // tk
    if mode == "tn":
        a_spec = pl.BlockSpec((tk, tm), lambda i, j, k: (k, i))
    elif a_parts > 1:
        per = nk // a_parts
        a_spec = pl.BlockSpec((None, tm, tk), lambda i, j, k: (k // per, i, k % per))
    else:
        a_spec = pl.BlockSpec((tm, tk), lambda i, j, k: (i, k))
    if mode == "nt":
        b_spec = pl.BlockSpec((tn, tk), lambda i, j, k: (j, k))
    elif b_parts > 1:
        per = N // tn // b_parts
        b_spec = pl.BlockSpec((None, tk, tn), lambda i, j, k: (j // per, k, j % per))
    else:
        b_spec = pl.BlockSpec((tk, tn), lambda i, j, k: (k, col0 + j))
    dims = {"nn": ((1,), (0,)), "nt": ((1,), (1,)), "tn": ((0,), (0,))}[mode]

    def body(a_ref, b_ref, o_ref, *scr):
        p = lax.dot_general(a_ref[...], b_ref[...], (dims, ((), ())), preferred_element_type=f32, precision=precision)
        if nk == 1:
            o_ref[...] = p.astype(o_ref.dtype)
        else:
            acc = scr[0]
            k = pl.program_id(2)

            @pl.when(k == 0)
            def _():
                acc[...] = p

            @pl.when(k > 0)
            def _():
                acc[...] += p

            @pl.when(k == nk - 1)
            def _():
                o_ref[...] = acc[...].astype(o_ref.dtype)

    grid = (M // tm, N // tn, nk)
    scratch = [pltpu.VMEM((tm, tn), f32)] if nk > 1 else []
    out_spec = pl.BlockSpec((tm, tn), lambda i, j, k: (i, j))
    out_shape = jax.ShapeDtypeStruct((M, N), out_dtype)
    if comm is None:
        return pl.pallas_call(body, name=name, grid=grid, in_specs=[a_spec, b_spec], out_specs=out_spec, out_shape=out_shape,
                              scratch_shapes=scratch, compiler_params=_cp(("parallel", "parallel", "arbitrary")))(a, b)
    at = lambda pos: lambda: functools.reduce(jnp.logical_and, [pl.program_id(d) == p for d, p in enumerate(pos)])
    end = tuple(g - 1 for g in grid)
    return pl.pallas_call(
        _ride(body, 2, 1, len(scratch), comm, at((0, 0, 0)), at(end), at(end)), name=name, grid=grid,
        in_specs=[a_spec, b_spec] + comm.specs, out_specs=[out_spec] + comm.specs, out_shape=[out_shape] + comm.out_shape,
        scratch_shapes=scratch + comm.scratch, compiler_params=_cp(("arbitrary", "arbitrary", "arbitrary")),
    )(a, b, *comm.arrs)


def rowcall(name, fn, tok, bat, con, tok_out, acc_out, ts=256):
    B, S = tok[0][0].shape[:2]
    ts = min(ts, S)
    nt, nb, nc, no, na = len(tok), len(bat), len(con), len(tok_out), len(acc_out)

    def body(*refs):
        tr, br, cr = refs[:nt], refs[nt:nt + nb], refs[nt + nb:nt + nb + nc]
        orf, arf = refs[nt + nb + nc:nt + nb + nc + no], refs[nt + nb + nc + no:]
        touts, aouts = fn([r[0] for r in tr], [r[0] for r in br], [r[...] for r in cr])
        for r, v in zip(orf, touts):
            r[0] = v.astype(r.dtype)
        s = pl.program_id(1)
        for r, v in zip(arf, aouts):
            @pl.when(s == 0)
            def _(r=r):
                r[...] = jnp.zeros(r.shape, r.dtype)
            r[0] += v.astype(f32)

    in_specs = [pl.BlockSpec((1, ts, w), lambda b, s, cb=cb: (b, s, cb)) for (_, w, cb) in tok]
    in_specs += [pl.BlockSpec((1,) + a.shape[1:], lambda b, s: (b, 0, 0)) for a in bat]
    in_specs += [pl.BlockSpec(a.shape, lambda b, s, nd=a.ndim: (0,) * nd) for a in con]
    out_specs = [pl.BlockSpec((1, ts, w), lambda b, s: (b, s, 0)) for (w, _) in tok_out]
    out_specs += [pl.BlockSpec((1,) + shp, lambda b, s, nd=len(shp): (b,) + (0,) * nd) for shp in acc_out]
    out_shape = [jax.ShapeDtypeStruct((B, S, w), dt) for (w, dt) in tok_out]
    out_shape += [jax.ShapeDtypeStruct((B,) + shp, f32) for shp in acc_out]
    return pl.pallas_call(
        body, name=name, grid=(B, S // ts), in_specs=in_specs, out_specs=out_specs, out_shape=out_shape,
        compiler_params=_cp(("parallel", "arbitrary")),
    )(*[t[0] for t in tok], *bat, *con)


def rowcall_fwd(name, f, tok, bat, con, tok_out, ts=256):
    def fn(t, b, c):
        return f([v.astype(f32) for v in t], b, c), []
    return rowcall(name, fn, tok, bat, con, tok_out, [], ts)


def rowcall_bwd(name, f, tok, bat, con, cts, tok_grads, add=None, ts=256):
    nt, ncts = len(tok), len(cts)

    def fn(t, b, c):
        prim = [v.astype(f32) for v in t[:nt]]
        ct = [v.astype(f32) for v in t[nt:nt + ncts]]
        _, vjp = jax.vjp(lambda tt, bb, cc: f(tt, bb, cc), prim, b, c)
        dt, db, dc = vjp(ct)
        touts = [dt[i] for i, _ in tok_grads]
        if add is not None:
            touts[0] = touts[0] + t[nt + ncts].astype(f32)
        return touts, list(db) + list(dc)

    all_tok = list(tok) + list(cts) + ([add] if add is not None else [])
    tok_out = [(tok[i][1], dt) for i, dt in tok_grads]
    acc_out = [tuple(a.shape[1:]) for a in bat] + [tuple(a.shape) for a in con]
    return rowcall(name, fn, all_tok, bat, con, tok_out, acc_out, ts)


def _rms(y, w):
    return y * lax.rsqrt(jnp.mean(y * y, axis=-1, keepdims=True) + RMS_EPS) * w


def f_rms_mod(t, b, c):
    return [_rms(t[0], c[0]) * (1.0 + b[0]) + b[1]]


def f_resid(t, b, c):
    return [t[0] + b[0] * _rms(t[1], c[0])]


def f_merge(t, b, c):
    ga, gd, ya, yd = t
    return [jax.nn.sigmoid(ga) * ya + jax.nn.sigmoid(gd) * yd]


def f_dnout(t, b, c):
    o, z = t
    outs = []
    for h in range(DNH):
        sl = slice(h * DND, (h + 1) * DND)
        zh = z[:, sl]
        outs.append(_rms(o[:, sl], c[0]) * (zh * jax.nn.sigmoid(zh)))
    return [jnp.concatenate(outs, axis=1)]


def _softplus(x):
    return jnp.maximum(x, 0.0) + jnp.log(1.0 + jnp.exp(-jnp.abs(x)))


def f_gate(t, b, c):
    ba = t[0]
    a_log, dt_bias = c
    lane = lax.broadcasted_iota(jnp.int32, ba.shape, 1)
    beta = jax.nn.sigmoid(ba)
    g = -jnp.exp(a_log) * _softplus(ba + dt_bias)
    return [jnp.where(lane < DNH, beta, jnp.where(lane < 2 * DNH, g, 0.0))]


def _bucket_table():
    qi = np.arange(WIN)[:, None]
    kj = np.arange(2 * WIN)[None, :]
    dist = np.maximum(WIN + qi - kj, 0)
    max_exact = NBUCK // 2
    scaled = np.log(np.maximum(dist, 1).astype(np.float64) / max_exact) / math.log(MAXDIST / max_exact)
    large = np.minimum(max_exact + (scaled * (NBUCK - max_exact)).astype(np.int32), NBUCK - 1)
    return np.where(dist < max_exact, dist, large).astype(np.int32)


def _attn_mask(n):
    qi = lax.broadcasted_iota(jnp.int32, (WIN, 2 * WIN), 0)
    kj = lax.broadcasted_iota(jnp.int32, (WIN, 2 * WIN), 1)
    dist = WIN + qi - kj
    return (dist >= 0) & (dist < WIN) & ((kj >= WIN) | (n > 0))


def _attn_block(q, kp, kc, vp, vc, bias, sinks, mask, differentiated):
    dot = _bdot_bf16_vjp if differentiated else _bdot_bf16
    grp = HQ // HKV
    band = lambda p, c, j: jnp.concatenate([p[:, j * LANE:(j + 1) * LANE], c[:, j * LANE:(j + 1) * LANE]], axis=0)
    qh = _stack([q[:, h * LANE:(h + 1) * LANE] for h in range(HQ)])
    kb = _stack([band(kp, kc, h // grp) for h in range(HQ)])
    vb = _stack([band(vp, vc, h // grp) for h in range(HQ)])
    s = dot(qh, kb, 2, 2) * (HD ** -0.5)
    s = jnp.where(mask[None], s + bias, NEG_INF)
    m = jnp.maximum(jnp.max(s, axis=-1, keepdims=True), sinks)
    p = jnp.exp(s - m)
    probs = p / (jnp.sum(p, axis=-1, keepdims=True) + jnp.exp(sinks - m))
    o = dot(probs, vb, 2, 1)
    return jnp.concatenate([o[h] for h in range(HQ)], axis=1)


def _attn_specs(NB):
    last = NB - 1
    return [
        pl.BlockSpec((1, WIN, HQ * LANE), lambda b, n: (b, jnp.minimum(n, last), CB_AQ // 8)),
        pl.BlockSpec((1, WIN, HKV * LANE), lambda b, n: (b, jnp.clip(n - 1, 0, last), CB_AK // 2)),
        pl.BlockSpec((1, WIN, HKV * LANE), lambda b, n: (b, jnp.minimum(n, last), CB_AK // 2)),
        pl.BlockSpec((1, WIN, HKV * LANE), lambda b, n: (b, jnp.clip(n - 1, 0, last), CB_AV // 2)),
        pl.BlockSpec((1, WIN, HKV * LANE), lambda b, n: (b, jnp.minimum(n, last), CB_AV // 2)),
        pl.BlockSpec((HQ, WIN, 2 * WIN), lambda b, n: (0, 0, 0)),
        pl.BlockSpec((HQ, 1, 1), lambda b, n: (0, 0, 0)),
    ]


def attn_fwd(proj, bias, sinks, comm):
    B, S, _ = proj.shape
    NB = S // WIN

    def body(q, kp, kc, vp, vc, bias_ref, sink_ref, o_ref):
        mask = _attn_mask(pl.program_id(1))
        o = _attn_block(*[r[0].astype(f32) for r in (q, kp, kc, vp, vc)], bias_ref[...], sink_ref[...], mask, False)
        o_ref[0] = o.astype(o_ref.dtype)

    at = lambda b, n: lambda: (pl.program_id(0) == b) & (pl.program_id(1) == n)
    return pl.pallas_call(
        _ride(body, 7, 1, 0, comm, at(0, 0), at(B - 1, (3 * NB) // 4), at(B - 1, NB - 1)), name="attn_fwd", grid=(B, NB),
        in_specs=_attn_specs(NB) + comm.specs,
        out_specs=[pl.BlockSpec((1, WIN, HQ * LANE), lambda b, n: (b, n, 0))] + comm.specs,
        out_shape=[jax.ShapeDtypeStruct((B, S, HQ * LANE), bf16)] + comm.out_shape, scratch_shapes=comm.scratch,
        compiler_params=_cp(("arbitrary", "arbitrary")),
    )(proj, proj, proj, proj, proj, bias, sinks, *comm.arrs)


def attn_bwd(proj, bias, sinks, dy, comm):
    B, S, _ = proj.shape
    NB = S // WIN
    last = NB - 1

    def body(q, kp, kc, vp, vc, bias_ref, sink_ref, dy_ref, dq_ref, dk_ref, dv_ref, dbias_ref, dsink_ref, kcar, vcar):
        b, n = pl.program_id(0), pl.program_id(1)

        @pl.when((b == 0) & (n == 0))
        def _():
            dbias_ref[...] = jnp.zeros(dbias_ref.shape, f32)
            dsink_ref[...] = jnp.zeros(dsink_ref.shape, f32)

        @pl.when(n == 0)
        def _():
            kcar[...] = jnp.zeros(kcar.shape, f32)
            vcar[...] = jnp.zeros(vcar.shape, f32)

        @pl.when(n < NB)
        def _():
            mask = _attn_mask(n)
            _, vjp = jax.vjp(lambda *a: _attn_block(*a, mask, True), *[r[0].astype(f32) for r in (q, kp, kc, vp, vc)],
                             bias_ref[...], sink_ref[...])
            dq, dkp, dkc, dvp, dvc, dbias, dsink = vjp(dy_ref[0].astype(f32))
            dq_ref[0] = dq.astype(dq_ref.dtype)
            dbias_ref[...] += dbias
            dsink_ref[...] += dsink
            dk_ref[0] = (kcar[...] + dkp).astype(dk_ref.dtype)
            dv_ref[0] = (vcar[...] + dvp).astype(dv_ref.dtype)
            kcar[...] = dkc
            vcar[...] = dvc

        @pl.when(n == NB)
        def _():
            dk_ref[0] = kcar[...].astype(dk_ref.dtype)
            dv_ref[0] = vcar[...].astype(dv_ref.dtype)

    in_specs = _attn_specs(NB) + [pl.BlockSpec((1, WIN, HQ * LANE), lambda b, n: (b, jnp.minimum(n, last), 0))]
    kv_out = pl.BlockSpec((1, WIN, HKV * LANE), lambda b, n: (b, jnp.maximum(n - 1, 0), 0))
    at = lambda b, n: lambda: (pl.program_id(0) == b) & (pl.program_id(1) == n)
    return pl.pallas_call(
        _ride(body, 8, 5, 2, comm, at(0, 0), at(B - 1, NB), at(B - 1, NB)), name="attn_bwd", grid=(B, NB + 1),
        in_specs=in_specs + comm.specs,
        out_specs=[pl.BlockSpec((1, WIN, HQ * LANE), lambda b, n: (b, jnp.minimum(n, last), 0)), kv_out, kv_out,
                   pl.BlockSpec((HQ, WIN, 2 * WIN), lambda b, n: (0, 0, 0)), pl.BlockSpec((HQ, 1, 1), lambda b, n: (0, 0, 0))] + comm.specs,
        out_shape=[jax.ShapeDtypeStruct((B, S, HQ * LANE), bf16), jax.ShapeDtypeStruct((B, S, HKV * LANE), bf16),
                   jax.ShapeDtypeStruct((B, S, HKV * LANE), bf16), jax.ShapeDtypeStruct((HQ, WIN, 2 * WIN), f32),
                   jax.ShapeDtypeStruct((HQ, 1, 1), f32)] + comm.out_shape,
        scratch_shapes=[pltpu.VMEM((WIN, HKV * LANE), f32), pltpu.VMEM((WIN, HKV * LANE), f32)] + comm.scratch,
        compiler_params=_cp(("arbitrary", "arbitrary")),
    )(proj, proj, proj, proj, proj, bias, sinks, dy, *comm.arrs)


def _causal_conv(x, w, width):
    S, C = x.shape
    xp = jnp.concatenate([jnp.zeros((8, C), f32), x], axis=0)
    out = None
    for j in range(width):
        off = 8 - (width - 1) + j
        term = w[j:j + 1, :] * xp[off:off + S, :]
        out = term if out is None else out + term
    return out


def _dnconv_f(x, w, isqk):
    y = _causal_conv(x, w, DNK)
    y = y * jax.nn.sigmoid(y)
    yn = y * lax.rsqrt(jnp.sum(y * y, axis=-1, keepdims=True) + L2_EPS)
    return jnp.where(isqk, yn, y)


def _dn_outblk(i):
    return (i % DNH) * 3 + i // DNH


def dnconv_fwd(proj, conv_w):
    B, S, _ = proj.shape

    def body(x_ref, w_ref, o_ref):
        o_ref[0] = _dnconv_f(x_ref[0].astype(f32), w_ref[...], pl.program_id(0) < 2 * DNH)

    return pl.pallas_call(
        body, name="dnconv_fwd", grid=(3 * DNH, B),
        in_specs=[pl.BlockSpec((1, S, LANE), lambda i, b: (b, 0, CB_DQKV + i)), pl.BlockSpec((DNK, LANE), lambda i, b: (0, i))],
        out_specs=pl.BlockSpec((1, S, LANE), lambda i, b: (b, 0, _dn_outblk(i))),
        out_shape=jax.ShapeDtypeStruct((B, S, 3 * DNH * DND), f32), compiler_params=_cp(("parallel", "parallel")),
    )(proj, conv_w)


def dnconv_bwd(proj, conv_w, dqkvn):
    B, S, _ = proj.shape

    def body(x_ref, w_ref, dy_ref, dx_ref, dw_ref):
        isqk = pl.program_id(0) < 2 * DNH
        _, vjp = jax.vjp(lambda x, w: _dnconv_f(x, w, isqk), x_ref[0].astype(f32), w_ref[...])
        dx, dw = vjp(dy_ref[0])
        dx_ref[0] = dx.astype(dx_ref.dtype)

        @pl.when(pl.program_id(1) == 0)
        def _():
            dw_ref[...] = jnp.zeros(dw_ref.shape, f32)
        dw_ref[...] += dw

    return pl.pallas_call(
        body, name="dnconv_bwd", grid=(3 * DNH, B),
        in_specs=[pl.BlockSpec((1, S, LANE), lambda i, b: (b, 0, CB_DQKV + i)), pl.BlockSpec((DNK, LANE), lambda i, b: (0, i)),
                  pl.BlockSpec((1, S, LANE), lambda i, b: (b, 0, _dn_outblk(i)))],
        out_specs=[pl.BlockSpec((1, S, LANE), lambda i, b: (b, 0, i)), pl.BlockSpec((DNK, LANE), lambda i, b: (0, i))],
        out_shape=[jax.ShapeDtypeStruct((B, S, 3 * DNH * DND), bf16), jax.ShapeDtypeStruct((DNK, 3 * DNH * DND), f32)],
        compiler_params=_cp(("parallel", "arbitrary")),
    )(proj, conv_w, dqkvn)


def _bdot(a, b, ca, cb, precision=HI):
    return lax.dot_general(a, b, (((ca,), (cb,)), ((0,), (0,))), preferred_element_type=f32, precision=precision)


def _bdot_bf16(a, b, ca, cb):
    return _bdot(a.astype(bf16), b.astype(bf16), ca, cb, None)


@functools.partial(jax.custom_vjp, nondiff_argnums=(2, 3))
def _bdot_bf16_vjp(a, b, ca, cb):
    return _bdot_bf16(a, b, ca, cb)


def _bdot_bf16_fwd(a, b, ca, cb):
    return _bdot_bf16(a, b, ca, cb), (a, b)


def _bdot_bf16_bwd(ca, cb, res, g):
    a, b = res
    fa, fb = 3 - ca, 3 - cb
    da = _bdot_bf16(g, b, 2, fb) if ca == 2 else _bdot_bf16(b, g, fb, 2)
    db = _bdot_bf16(a, g, fa, 1) if cb == 1 else _bdot_bf16(g, a, 1, fa)
    return da, db


_bdot_bf16_vjp.defvjp(_bdot_bf16_fwd, _bdot_bf16_bwd)


def _neumann_inverse(low):
    n = low.shape[-1]
    eye = (lax.broadcasted_iota(jnp.int32, (n, n), 0) == lax.broadcasted_iota(jnp.int32, (n, n), 1)).astype(f32)
    p = -low
    x = eye[None] + p
    for _ in range(5):
        p = _bdot(p, p, 2, 1, MID)
        x = x + _bdot(x, p, 2, 1, MID)
    return x


@jax.custom_vjp
def _unit_lower_inverse(low):
    return _neumann_inverse(low)


def _uli_fwd(low):
    t = _neumann_inverse(low)
    return t, t


def _uli_bwd(t, dt):
    return (-_bdot(_bdot(t, dt, 1, 1, MID), t, 2, 2, MID),)


_unit_lower_inverse.defvjp(_uli_fwd, _uli_bwd)


def _stack(xs):
    return jnp.concatenate([x[None] for x in xs], axis=0)


def _delta_chunk(qkv, bg, state, differentiated):
    inverse = _unit_lower_inverse if differentiated else _neumann_inverse
    lo = _bdot_bf16_vjp if differentiated else _bdot_bf16
    B = qkv.shape[0]
    G = B * DNH
    pairs = [(b, h) for b in range(B) for h in range(DNH)]
    col = lambda b, h, kind: qkv[b, :, (3 * h + kind) * DND:(3 * h + kind + 1) * DND]
    q, k, v = [_stack([col(b, h, kind) for b, h in pairs]) for kind in range(3)]
    lane = lax.broadcasted_iota(jnp.int32, (CH, LANE), 1)
    pick = lambda b, l: jnp.sum(jnp.where(lane == l, bg[b], 0.0), axis=1, keepdims=True)
    beta = _stack([pick(b, h) for b, h in pairs])
    g = _stack([pick(b, h + DNH) for b, h in pairs])
    ri = lax.broadcasted_iota(jnp.int32, (CH, CH), 0)
    ci = lax.broadcasted_iota(jnp.int32, (CH, CH), 1)
    incl, strict = (ri >= ci)[None], (ri > ci)[None]
    gc = _bdot(jnp.broadcast_to(incl.astype(f32), (G, CH, CH)), jnp.broadcast_to(g, (G, CH, LANE)), 2, 1)
    e0 = jnp.broadcast_to((lane == 0).astype(f32)[None], (G, CH, LANE))
    gc_row = _bdot(e0, gc, 2, 2)
    diff = gc[:, :, :CH] - gc_row
    decay = jnp.where(incl, jnp.exp(jnp.where(incl, diff, 0.0)), 0.0)
    qs = q * (DND ** -0.5)
    kb, vb = k * beta, v * beta
    eg = jnp.exp(gc)
    low = jnp.where(strict, lo(kb, k, 2, 2) * decay, 0.0)
    tinv = inverse(low)
    u = _bdot(tinv, vb, 2, 1, MID)
    w = _bdot(tinv, kb * eg, 2, 1, MID)
    intra = jnp.where(incl, lo(qs, k, 2, 2) * decay, 0.0)
    gl = gc[:, CH - 1:CH, :]
    k_tail = k * jnp.exp(gl - gc)
    v_new = u - lo(w, state, 2, 1)
    o = lo(qs * eg, state, 2, 1) + lo(intra, v_new, 2, 1)
    new_state = state * jnp.exp(gl) + lo(k_tail, v_new, 1, 1)
    return o, new_state


def delta_fwd(qkvn, bg, comm):
    B, S, _ = qkvn.shape
    NC, G = S // CH, B * DNH

    def body(qkv_ref, bg_ref, o_ref, st_ref, state):
        @pl.when(pl.program_id(0) == 0)
        def _():
            state[...] = jnp.zeros(state.shape, f32)
        s0 = state[...]
        st_ref[0] = s0
        o, s1 = _delta_chunk(qkv_ref[...], bg_ref[...], s0, False)
        for b in range(B):
            for h in range(DNH):
                o_ref[b, :, h * DND:(h + 1) * DND] = o[b * DNH + h]
        state[...] = s1

    at = lambda c: lambda: pl.program_id(0) == c
    return pl.pallas_call(
        _ride(body, 2, 2, 1, comm, at(0), at((7 * NC) // 8), at(NC - 1)), name="delta_fwd", grid=(NC,),
        in_specs=[pl.BlockSpec((B, CH, 3 * DNH * DND), lambda c: (0, c, 0)), pl.BlockSpec((B, CH, LANE), lambda c: (0, c, 0))] + comm.specs,
        out_specs=[pl.BlockSpec((B, CH, DNH * DND), lambda c: (0, c, 0)), pl.BlockSpec((1, G, DND, DND), lambda c: (c, 0, 0, 0))] + comm.specs,
        out_shape=[jax.ShapeDtypeStruct((B, S, DNH * DND), f32), jax.ShapeDtypeStruct((NC, G, DND, DND), f32)] + comm.out_shape,
        scratch_shapes=[pltpu.VMEM((G, DND, DND), f32)] + comm.scratch, compiler_params=_cp(("arbitrary",)),
    )(qkvn, bg, *comm.arrs)


def delta_bwd(qkvn, bg, states, do, comm):
    B, S, _ = qkvn.shape
    NC, G = S // CH, B * DNH

    def body(qkv_ref, bg_ref, st_ref, do_ref, dqkv_ref, dbg_ref, dstate):
        @pl.when(pl.program_id(0) == 0)
        def _():
            dstate[...] = jnp.zeros(dstate.shape, f32)
        _, vjp = jax.vjp(lambda a, g, s: _delta_chunk(a, g, s, True), qkv_ref[...], bg_ref[...], st_ref[0])
        do = _stack([do_ref[b, :, h * DND:(h + 1) * DND] for b in range(B) for h in range(DNH)])
        dqkv, dbg, ds = vjp((do, dstate[...]))
        dqkv_ref[...] = dqkv
        dbg_ref[...] = dbg
        dstate[...] = ds

    rev = lambda c: NC - 1 - c
    at = lambda c: lambda: pl.program_id(0) == c
    return pl.pallas_call(
        _ride(body, 4, 2, 1, comm, at(0), at(NC - 1), at(NC - 1)), name="delta_bwd", grid=(NC,),
        in_specs=[pl.BlockSpec((B, CH, 3 * DNH * DND), lambda c: (0, rev(c), 0)), pl.BlockSpec((B, CH, LANE), lambda c: (0, rev(c), 0)),
                  pl.BlockSpec((1, G, DND, DND), lambda c: (rev(c), 0, 0, 0)),
                  pl.BlockSpec((B, CH, DNH * DND), lambda c: (0, rev(c), 0))] + comm.specs,
        out_specs=[pl.BlockSpec((B, CH, 3 * DNH * DND), lambda c: (0, rev(c), 0)),
                   pl.BlockSpec((B, CH, LANE), lambda c: (0, rev(c), 0))] + comm.specs,
        out_shape=[jax.ShapeDtypeStruct((B, S, 3 * DNH * DND), f32), jax.ShapeDtypeStruct((B, S, LANE), f32)] + comm.out_shape,
        scratch_shapes=[pltpu.VMEM((G, DND, DND), f32)] + comm.scratch, compiler_params=_cp(("arbitrary",)),
    )(qkvn, bg, states, do, *comm.arrs)


def _ffn_f(gate_x, val_x, gate_w, val_w):
    gate = _causal_conv(gate_x, gate_w, FK)
    val = _causal_conv(val_x, val_w, FK)
    gl = 0.5 * gate * (1.0 + jnp.tanh(math.sqrt(2.0 / math.pi) * (gate + 0.044715 * gate * gate * gate)))
    return gl * val


def _ffn_specs(S):
    nblk = DFF // LANE
    return [pl.BlockSpec((1, S, LANE), lambda i, b: (b, 0, i)), pl.BlockSpec((1, S, LANE), lambda i, b: (b, 0, nblk + i)),
            pl.BlockSpec((FK, LANE), lambda i, b: (0, i)), pl.BlockSpec((FK, LANE), lambda i, b: (0, nblk + i))]


def ffnconv_fwd(up, conv_w):
    B, S, _ = up.shape

    def body(g_ref, v_ref, gw_ref, vw_ref, o_ref):
        o_ref[0] = _ffn_f(g_ref[0].astype(f32), v_ref[0].astype(f32), gw_ref[...], vw_ref[...]).astype(o_ref.dtype)

    return pl.pallas_call(
        body, name="ffnconv_fwd", grid=(DFF // LANE, B), in_specs=_ffn_specs(S),
        out_specs=pl.BlockSpec((1, S, LANE), lambda i, b: (b, 0, i)),
        out_shape=jax.ShapeDtypeStruct((B, S, DFF), bf16), compiler_params=_cp(("parallel", "parallel")),
    )(up, up, conv_w, conv_w)


def ffnconv_bwd(up, conv_w, dact):
    B, S, _ = up.shape

    def body(g_ref, v_ref, gw_ref, vw_ref, dy_ref, dx_ref, dw_ref):
        _, vjp = jax.vjp(_ffn_f, g_ref[0].astype(f32), v_ref[0].astype(f32), gw_ref[...], vw_ref[...])
        dg, dv, dgw, dvw = vjp(dy_ref[0].astype(f32))
        dx_ref[0, 0] = dg.astype(dx_ref.dtype)
        dx_ref[1, 0] = dv.astype(dx_ref.dtype)

        @pl.when(pl.program_id(1) == 0)
        def _():
            dw_ref[...] = jnp.zeros(dw_ref.shape, f32)
        dw_ref[0] += dgw
        dw_ref[1] += dvw

    return pl.pallas_call(
        body, name="ffnconv_bwd", grid=(DFF // LANE, B),
        in_specs=_ffn_specs(S) + [pl.BlockSpec((1, S, LANE), lambda i, b: (b, 0, i))],
        out_specs=[pl.BlockSpec((2, 1, S, LANE), lambda i, b: (0, b, 0, i)), pl.BlockSpec((2, FK, LANE), lambda i, b: (0, 0, i))],
        out_shape=[jax.ShapeDtypeStruct((2, B, S, DFF), bf16), jax.ShapeDtypeStruct((2, FK, DFF), f32)],
        compiler_params=_cp(("parallel", "arbitrary")),
    )(up, up, conv_w, conv_w, dact)


def ada_fwd(c_all, ada_w, ada_b):
    def body(c_ref, w_ref, b_ref, o_ref):
        c = c_ref[...]
        act = (c * jax.nn.sigmoid(c)).astype(bf16)
        o_ref[...] = jnp.dot(act, w_ref[...].astype(bf16), preferred_element_type=f32) + b_ref[...]

    return pl.pallas_call(body, name="ada_fwd", out_shape=jax.ShapeDtypeStruct((c_all.shape[0], ada_w.shape[1]), f32),
                          compiler_params=pltpu.CompilerParams(vmem_limit_bytes=VMEM_LIMIT))(c_all, ada_w, ada_b)


def ada_bwd(c_all, dmod):
    def body(c_ref, d_ref, o_ref):
        c = c_ref[...]
        act = (c * jax.nn.sigmoid(c)).astype(bf16)
        o_ref[...] = lax.dot_general(act, d_ref[...].astype(bf16), (((0,), (0,)), ((), ())), preferred_element_type=f32)

    return pl.pallas_call(body, name="ada_bwd", out_shape=jax.ShapeDtypeStruct((c_all.shape[1], dmod.shape[1]), f32),
                          compiler_params=pltpu.CompilerParams(vmem_limit_bytes=VMEM_LIMIT))(c_all, dmod)


def loss_head(h1, y2, target, g2, w):
    def fn(t, b, c):
        h, y, tg = [v.astype(f32) for v in t]

        def loss_fn(h, y, g, w):
            e = h + g * _rms(y, w) - tg
            return 0.5 * jnp.sum(jnp.mean(e * e, axis=-1))

        loss, grads = jax.value_and_grad(loss_fn, argnums=(0, 1, 2, 3))(h, y, b[0], c[0])
        return [grads[0], grads[1]], [grads[2], grads[3], jnp.full((1, LANE), loss, f32)]

    return rowcall("loss_head", fn, [(h1, D, 0), (y2, D, 0), (target, D, 0)], [g2], [w], [(D, f32), (D, bf16)],
                   [(1, D), (1, D), (1, LANE)])


def adamw(w, gparts, m, v, name):
    R, C = w.shape
    P = gparts.shape[0]
    tr = R
    if R * C * 4 > 2 * 1024 * 1024:
        for cand in (512, 256, 128, 64, 32, 16, 8):
            if R % cand == 0 and cand * C * 4 <= 2 * 1024 * 1024:
                tr = cand
                break

    def body(w_ref, g_ref, m_ref, v_ref, go, do, mo, vo):
        g = g_ref[0].astype(f32)
        for p in range(1, P):
            g = g + g_ref[p].astype(f32)
        m2 = B1 * m_ref[...] + (1.0 - B1) * g
        v2 = B2 * v_ref[...] + (1.0 - B2) * jnp.square(g)
        m_hat = m2 * (1.0 / (1.0 - B1 ** STEP))
        v_hat = v2 * (1.0 / (1.0 - B2 ** STEP))
        go[...] = g
        do[...] = -LR * (m_hat / (jnp.sqrt(v_hat) + EPS) + WD * w_ref[...])
        mo[...] = m2
        vo[...] = v2

    blk = pl.BlockSpec((tr, C), lambda i: (i, 0))
    return pl.pallas_call(
        body, name=name, grid=(R // tr,), in_specs=[blk, pl.BlockSpec((P, tr, C), lambda i: (0, i, 0)), blk, blk],
        out_specs=[blk] * 4, out_shape=[jax.ShapeDtypeStruct((R, C), f32)] * 4, compiler_params=_cp(("parallel",)),
    )(w, gparts, m, v)


def _pad_heads(w, nh):
    r = w.shape[0]
    return jnp.pad(w.reshape(r, nh, HD), ((0, 0), (0, 0), (0, LANE - HD))).reshape(r, nh * LANE)


def _unpad_heads(w, nh):
    return w.reshape(w.shape[0], nh, LANE)[:, :, :HD].reshape(w.shape[0], nh * HD)


def _pack_w_in(w):
    aq, ak, av, dqkv, dz, dbeta, da, ga, gd = jnp.split(w, np.cumsum(IN_SPLITS)[:-1].tolist(), axis=1)
    ba = jnp.pad(jnp.concatenate([dbeta, da], axis=1), ((0, 0), (0, LANE - 2 * DNH)))
    return jnp.concatenate([ga, gd, _pad_heads(aq, HQ), dqkv, dz, _pad_heads(ak, HKV), _pad_heads(av, HKV), ba], axis=1)


def _unpack_w_in(p):
    col = lambda cb, n: p[:, cb * LANE: cb * LANE + n]
    ba = col(CB_BA, 2 * DNH)
    return jnp.concatenate([_unpad_heads(col(CB_AQ, HQ * LANE), HQ), _unpad_heads(col(CB_AK, HKV * LANE), HKV),
                            _unpad_heads(col(CB_AV, HKV * LANE), HKV), col(CB_DQKV, 3 * DNH * DND), col(CB_DZ, DNH * DND),
                            ba[:, :DNH], ba[:, DNH:], col(CB_GA, D), col(CB_GD, D)], axis=1)


def _cols_gathered(g):
    return g.transpose(1, 0, 2).reshape(g.shape[1], NDEV * g.shape[2])


def _cols_split(w):
    r = w.shape[0]
    return w.reshape(r, NDEV, w.shape[1] // NDEV).transpose(1, 0, 2)


def kernel(x, c, ada_w, ada_b, norm_mix_pre, norm_mix_post, norm_ffn_pre, norm_ffn_post, w_in, dn_conv_w, dn_a_log, dn_dt_bias, dn_norm_w, attn_sinks, rel_bias, w_attn_branch, w_dn_branch, w_out, ffn_w_up, ffn_conv_w, ffn_w_down, loss_target, m_ada_w, m_ada_b, m_norm_mix_pre, m_norm_mix_post, m_norm_ffn_pre, m_norm_ffn_post, m_w_in, m_dn_conv_w, m_dn_a_log, m_dn_dt_bias, m_dn_norm_w, m_attn_sinks, m_rel_bias, m_w_attn_branch, m_w_dn_branch, m_w_out, m_ffn_w_up, m_ffn_conv_w, m_ffn_w_down, v_ada_w, v_ada_b, v_norm_mix_pre, v_norm_mix_post, v_norm_ffn_pre, v_norm_ffn_post, v_w_in, v_dn_conv_w, v_dn_a_log, v_dn_dt_bias, v_dn_norm_w, v_attn_sinks, v_rel_bias, v_w_attn_branch, v_w_dn_branch, v_w_out, v_ffn_w_up, v_ffn_conv_w, v_ffn_w_down):
    B, S, _ = x.shape
    T = B * S
    me = 4 * lax.axis_index("x") + 2 * lax.axis_index("y") + lax.axis_index("c")
    big = dict(w_in=w_in, dn_conv_w=dn_conv_w, w_attn_branch=w_attn_branch, w_dn_branch=w_dn_branch, w_out=w_out,
               ffn_w_up=ffn_w_up, ffn_conv_w=ffn_conv_w, ffn_w_down=ffn_w_down)
    big_names = list(big)

    first, mid, late = ["w_in", "dn_conv_w"], ["w_attn_branch", "w_dn_branch", "w_out"], ["ffn_w_up", "ffn_conv_w", "ffn_w_down"]
    shard = lambda names: [big[n][0].astype(bf16) for n in names]
    *got, c_all = _exchange(shard(first) + [c], "gather_w_in", two_level=True)
    gw = dict(zip(first, got))
    c_all = c_all.reshape(NDEV * B, D)

    wp = _pack_w_in(_cols_gathered(gw["w_in"]))
    conv_dn = _cols_gathered(gw["dn_conv_w"]).astype(f32)

    ncol = ada_w.shape[2]
    ada_b_mine = lax.dynamic_slice_in_dim(ada_b, me * ncol, ncol, axis=1)
    mod_cols = ada_fwd(c_all, ada_w[0], ada_b_mine)
    (mod_g,) = _exchange([mod_cols], "gather_mod")
    mod = lax.dynamic_slice_in_dim(mod_g, me * B, B, axis=1).transpose(1, 0, 2).reshape(B, NMOD * D)
    sh1, sc1, g1, sh2, sc2, g2 = [mod[:, i * D:(i + 1) * D].reshape(B, 1, D) for i in range(NMOD)]

    onehot = (jnp.asarray(_bucket_table()).reshape(1, -1) == jnp.arange(NBUCK, dtype=jnp.int32)[:, None]).astype(f32)
    bias = mm(rel_bias.T, onehot, "nn", f32, "bias_table", tn=8192, precision=HI).reshape(HQ, WIN, 2 * WIN)
    sinks = attn_sinks.reshape(HQ, 1, 1)
    a_log_pad = jnp.pad(dn_a_log, ((0, 0), (DNH, LANE - 2 * DNH)))
    dt_bias_pad = jnp.pad(dn_dt_bias, ((0, 0), (DNH, LANE - 2 * DNH)))

    (u1,) = rowcall_fwd("mix_pre", f_rms_mod, [(x, D, 0)], [sc1, sh1], [norm_mix_pre], [(D, bf16)])
    proj = mm(u1.reshape(T, D), wp, "nn", bf16, "proj", tn=1408, b_cols=(0, CB_BA * LANE // 1408)).reshape(B, S, CB_BA * LANE)
    ba = mm(u1.reshape(T, D), wp, "nn", f32, "proj_ba", tn=LANE, b_cols=(CB_BA, 1)).reshape(B, S, LANE)
    ya, *got = attn_fwd(proj, bias, sinks, _Comm(shard(mid), two_level=True))
    gw.update(zip(mid, got))
    wa = _cols_gathered(gw["w_attn_branch"])
    wa = jnp.pad(wa.reshape(HQ, HD, D), ((0, 0), (0, LANE - HD), (0, 0))).reshape(HQ * LANE, D)
    wd = _cols_gathered(gw["w_dn_branch"])
    wo = gw["w_out"].reshape(D, D)
    qkvn = dnconv_fwd(proj, conv_dn)
    (bg,) = rowcall_fwd("dn_gate", f_gate, [(ba, LANE, 0)], [], [a_log_pad, dt_bias_pad], [(LANE, f32)])
    o_dn, states, *got = delta_fwd(qkvn, bg, _Comm(shard(late), two_level=True))
    gw.update(zip(late, got))
    wup = _cols_gathered(gw["ffn_w_up"])
    conv_ffn = _cols_gathered(gw["ffn_conv_w"]).astype(f32)
    wdown = gw["ffn_w_down"].reshape(DFF, D)
    (yd,) = rowcall_fwd("dn_out", f_dnout, [(o_dn, DNH * DND, 0), (proj, DNH * DND, CB_DZ // 4)], [], [dn_norm_w], [(DNH * DND, bf16)])
    pa = mm(ya.reshape(T, HQ * LANE), wa, "nn", bf16, "attn_branch").reshape(B, S, D)
    pd = mm(yd.reshape(T, DNH * DND), wd, "nn", bf16, "dn_branch").reshape(B, S, D)
    merge_tok = [(proj, D, CB_GA // 8), (proj, D, CB_GD // 8), (pa, D, 0), (pd, D, 0)]
    (merged,) = rowcall_fwd("merge", f_merge, merge_tok, [], [], [(D, bf16)])
    y1 = mm(merged.reshape(T, D), wo, "nn", bf16, "mix_out").reshape(B, S, D)
    (h1,) = rowcall_fwd("mix_post", f_resid, [(x, D, 0), (y1, D, 0)], [g1], [norm_mix_post], [(D, f32)])
    (u2,) = rowcall_fwd("ffn_pre", f_rms_mod, [(h1, D, 0)], [sc2, sh2], [norm_ffn_pre], [(D, bf16)])
    up = mm(u2.reshape(T, D), wup, "nn", bf16, "ffn_up", tn=1408).reshape(B, S, 2 * DFF)
    act = ffnconv_fwd(up, conv_ffn)
    y2 = mm(act.reshape(T, DFF), wdown, "nn", bf16, "ffn_down", tk=1408).reshape(B, S, D)

    dh1_a, dy2, dg2, dw_ffn_post, loss_b = loss_head(h1, y2, loss_target, g2, norm_ffn_post)
    dy2f = dy2.reshape(T, D)
    dact = mm(dy2f, wdown, "nt", bf16, "ffn_down_dx", tn=1408).reshape(B, S, DFF)
    g_wdown = mm(act.reshape(T, DFF), dy2f, "tn", f32, "ffn_down_dw", tm=1408, tk=2048)
    dup, g_conv_ffn = ffnconv_bwd(up, conv_ffn, dact)
    dupf = dup.reshape(2, T, DFF)
    g_conv_ffn = g_conv_ffn.transpose(1, 0, 2).reshape(FK, 2 * DFF)
    du2 = mm(dupf, wup, "nt", bf16, "ffn_up_dx", tk=1408).reshape(B, S, D)
    g_wup = mm(u2.reshape(T, D), dupf, "tn", f32, "ffn_up_dw", tn=1408, tk=2048)
    dh1, dsc2, dsh2, dw_ffn_pre = rowcall_bwd("ffn_pre_bwd", f_rms_mod, [(h1, D, 0)], [sc2, sh2], [norm_ffn_pre], [(du2, D, 0)],
                                              [(0, f32)], add=(dh1_a, D, 0))
    dy1, dg1, dw_mix_post = rowcall_bwd("mix_post_bwd", f_resid, [(x, D, 0), (y1, D, 0)], [g1], [norm_mix_post], [(dh1, D, 0)],
                                        [(1, bf16)])
    dy1f = dy1.reshape(T, D)
    dmerged = mm(dy1f, wo, "nt", bf16, "mix_out_dx").reshape(B, S, D)
    g_wo = mm(merged.reshape(T, D), dy1f, "tn", f32, "mix_out_dw", tk=2048)
    dga, dgd, dpa, dpd = rowcall_bwd("merge_bwd", f_merge, merge_tok, [], [], [(dmerged, D, 0)],
                                     [(0, bf16), (1, bf16), (2, bf16), (3, bf16)])
    dpaf, dpdf = dpa.reshape(T, D), dpd.reshape(T, D)
    dya = mm(dpaf, wa, "nt", bf16, "attn_branch_dx").reshape(B, S, HQ * LANE)
    g_wa = mm(ya.reshape(T, HQ * LANE), dpaf, "tn", f32, "attn_branch_dw", tk=2048)
    dyd = mm(dpdf, wd, "nt", bf16, "dn_branch_dx").reshape(B, S, DNH * DND)
    g_wd = mm(yd.reshape(T, DNH * DND), dpdf, "tn", f32, "dn_branch_dw", tk=2048)
    do_dn, dz, dw_dn_norm = rowcall_bwd("dn_out_bwd", f_dnout, [(o_dn, DNH * DND, 0), (proj, DNH * DND, CB_DZ // 4)], [], [dn_norm_w],
                                        [(dyd, DNH * DND, 0)], [(0, f32), (1, bf16)])
    parts = {}
    outbox = lambda d: _Comm([d[n].astype(bf16) for n in d], scatter=True)
    send = dict(ffn_w_up=_cols_split(g_wup), ffn_conv_w=_cols_split(g_conv_ffn),
                ffn_w_down=g_wdown.reshape(NDEV, DFF // NDEV, D))
    dqkvn, dbg, *got = delta_bwd(qkvn, bg, states, do_dn, outbox(send))
    parts.update(zip(send, got))
    dba, da_log_pad, ddt_bias_pad = rowcall_bwd("dn_gate_bwd", f_gate, [(ba, LANE, 0)], [], [a_log_pad, dt_bias_pad],
                                                [(dbg, LANE, 0)], [(0, bf16)])
    ddqkv, g_conv_dn = dnconv_bwd(proj, conv_dn, dqkvn)
    send = dict(w_attn_branch=_cols_split(g_wa.reshape(HQ, LANE, D)[:, :HD].reshape(HQ * HD, D)), w_dn_branch=_cols_split(g_wd),
                w_out=g_wo.reshape(NDEV, D // NDEV, D))
    dq, dk, dv, dbias, dsinks, *got = attn_bwd(proj, bias, sinks, dya, outbox(send))
    parts.update(zip(send, got))
    dproj = jnp.concatenate([dga, dgd, dq, ddqkv, dz, dk, dv, dba], axis=2).reshape(T, NP)
    g_wp = mm(u1.reshape(T, D), dproj, "tn", f32, "proj_dw", tn=1152, tk=2048)
    send = dict(w_in=_cols_split(_unpack_w_in(g_wp)), dn_conv_w=_cols_split(g_conv_dn))
    du1, *got = mm(dproj, wp, "nt", bf16, "proj_dx", tk=1152, comm=outbox(send))
    parts.update(zip(send, got))
    du1 = du1.reshape(B, S, D)
    grad_x, dsc1, dsh1, dw_mix_pre = rowcall_bwd("mix_pre_bwd", f_rms_mod, [(x, D, 0)], [sc1, sh1], [norm_mix_pre], [(du1, D, 0)],
                                                 [(0, f32)], add=(dh1, D, 0))
    g_rel = mm(dbias.reshape(HQ, WIN * 2 * WIN), onehot, "nt", f32, "rel_bias_dw", tk=8192, precision=HI)

    dmod = jnp.concatenate([dsh1, dsc1, dg1, dsh2, dsc2, dg2], axis=2).reshape(B, NMOD * D)

    zrow = lambda a: jnp.concatenate([a.reshape(1, -1), jnp.zeros((B - 1, a.size), f32)], axis=0)
    small_g = jnp.concatenate([
        dmod, dw_mix_pre.reshape(B, D), dw_mix_post.reshape(B, D), dw_ffn_pre.reshape(B, D), dw_ffn_post.reshape(B, D),
        da_log_pad.reshape(B, LANE)[:, DNH:2 * DNH], ddt_bias_pad.reshape(B, LANE)[:, DNH:2 * DNH], dw_dn_norm.reshape(B, DND),
        zrow(dsinks), zrow(g_rel.T), loss_b.reshape(B, LANE)[:, :1], jnp.zeros((B, SMALL_PAD - SMALL_N - 1), f32)], axis=1)
    (small_all,) = _exchange([small_g], "gather_small")
    dmod_cols = lax.dynamic_slice_in_dim(small_all.reshape(NDEV * B, SMALL_PAD), me * ncol, ncol, axis=1)
    g_ada_w = ada_bwd(c_all, dmod_cols)
    small_w = dict(ada_b=(ada_b, m_ada_b, v_ada_b), norm_mix_pre=(norm_mix_pre, m_norm_mix_pre, v_norm_mix_pre),
                   norm_mix_post=(norm_mix_post, m_norm_mix_post, v_norm_mix_post), norm_ffn_pre=(norm_ffn_pre, m_norm_ffn_pre, v_norm_ffn_pre),
                   norm_ffn_post=(norm_ffn_post, m_norm_ffn_post, v_norm_ffn_post), dn_a_log=(dn_a_log, m_dn_a_log, v_dn_a_log),
                   dn_dt_bias=(dn_dt_bias, m_dn_dt_bias, v_dn_dt_bias), dn_norm_w=(dn_norm_w, m_dn_norm_w, v_dn_norm_w),
                   attn_sinks=(attn_sinks, m_attn_sinks, v_attn_sinks), rel_bias=(rel_bias, m_rel_bias, v_rel_bias))

    def pack(i, fill):
        row = jnp.concatenate([small_w[n][i].reshape(1, -1) for n, _ in SMALL], axis=1)
        return jnp.pad(row, ((0, 0), (0, SMALL_PAD - SMALL_N)), constant_values=fill)

    small_out = adamw(pack(0, 0.0), small_all.reshape(NDEV * B, 1, SMALL_PAD), pack(1, 0.0), pack(2, 1.0), "adamw_small")
    loss = small_out[0][0, SMALL_N]

    res = {}
    off = 0
    for n, size in SMALL:
        shp = small_w[n][0].shape
        res[n] = [o[:, off:off + size].reshape(shp) for o in small_out]
        off += size
    res["ada_w"] = [o[None] for o in adamw(ada_w[0], g_ada_w[None], m_ada_w[0], v_ada_w[0], "adamw_ada_w")]
    moments = dict(w_in=(m_w_in, v_w_in), dn_conv_w=(m_dn_conv_w, v_dn_conv_w), w_attn_branch=(m_w_attn_branch, v_w_attn_branch),
                   w_dn_branch=(m_w_dn_branch, v_w_dn_branch), w_out=(m_w_out, v_w_out), ffn_w_up=(m_ffn_w_up, v_ffn_w_up),
                   ffn_conv_w=(m_ffn_conv_w, v_ffn_conv_w), ffn_w_down=(m_ffn_w_down, v_ffn_w_down))
    for n in big_names:
        res[n] = [o[None] for o in adamw(big[n][0], parts[n], moments[n][0][0], moments[n][1][0], "adamw_" + n)]

    order = ["ada_w", "ada_b", "norm_mix_pre", "norm_mix_post", "norm_ffn_pre", "norm_ffn_post", "w_in", "dn_conv_w", "dn_a_log",
             "dn_dt_bias", "dn_norm_w", "attn_sinks", "rel_bias", "w_attn_branch", "w_dn_branch", "w_out", "ffn_w_up", "ffn_conv_w",
             "ffn_w_down"]
    return (loss, grad_x, *[res[n][0] for n in order], *[res[n][1] for n in order], *[res[n][2] for n in order],
            *[res[n][3] for n in order])
```

```python
import functools
import math

import numpy as np
import jax
import jax.numpy as jnp
from jax import lax
from jax.experimental import pallas as pl
from jax.experimental.pallas import tpu as pltpu

f32 = jnp.float32
bf16 = jnp.bfloat16
HI = lax.Precision.HIGHEST
MID = lax.Precision.HIGH
MESH = pl.DeviceIdType.MESH

NDEV = 8
D = 1024
HQ, HKV, HD, WIN, NBUCK, MAXDIST = 8, 2, 64, 128, 32, 128
DNH, DND, DNK, CH = 4, 128, 4, 64
DFF, FK = 2816, 3
NMOD = 6
RMS_EPS = 1e-6
L2_EPS = 1e-6
NEG_INF = -1e30
LR, B1, B2, EPS, WD, STEP = 0.001, 0.9, 0.999, 1e-08, 0.01, 10

LANE = 128
CB_GA, CB_GD, CB_AQ, CB_DQKV, CB_DZ, CB_AK, CB_AV, CB_BA, NPB = 0, 8, 16, 24, 36, 40, 42, 44, 45
NP = NPB * LANE
IN_SPLITS = (HQ * HD, HKV * HD, HKV * HD, 3 * DNH * DND, DNH * DND, DNH, DNH, D, D)
IN_DIM = sum(IN_SPLITS)
VMEM_LIMIT = 56 * 1024 * 1024

SMALL = (("ada_b", NMOD * D), ("norm_mix_pre", D), ("norm_mix_post", D), ("norm_ffn_pre", D), ("norm_ffn_post", D),
         ("dn_a_log", DNH), ("dn_dt_bias", DNH), ("dn_norm_w", DND), ("attn_sinks", HQ), ("rel_bias", NBUCK * HQ))
SMALL_N = sum(n for _, n in SMALL)
SMALL_PAD = 10752


def _cp(sem):
    return pltpu.CompilerParams(dimension_semantics=sem, vmem_limit_bytes=VMEM_LIMIT)


def _pick(dim, target):
    if dim <= target:
        return dim
    best = None
    for d in range(LANE, target + 1, LANE):
        if dim % d == 0:
            best = d
    assert best is not None, (dim, target)
    return best


def _me():
    x, y, c = lax.axis_index("x"), lax.axis_index("y"), lax.axis_index("c")
    return x, y, c, 4 * x + 2 * y + c


def _peer(x, y, c, k):
    px = 1 - x if k & 4 else x
    py = 1 - y if k & 2 else y
    pc = 1 - c if k & 1 else c
    return (px, py, pc), 4 * px + 2 * py + pc


class _Comm:
    def __init__(self, arrs, scatter=False, two_level=False):
        assert not (scatter and two_level)
        self.arrs, self.n, self.scatter, self.two_level = list(arrs), len(arrs), scatter, two_level
        if scatter:
            self.out_shape = [jax.ShapeDtypeStruct(a.shape, a.dtype) for a in arrs]
        else:
            self.out_shape = [jax.ShapeDtypeStruct((NDEV,) + a.shape, a.dtype) for a in arrs]
        nsem = self.n * (NDEV - 1)
        self.scratch = [pltpu.SemaphoreType.DMA((nsem,)), pltpu.SemaphoreType.DMA((nsem,)), pltpu.SemaphoreType.DMA((self.n,))]
        self.specs = [pl.BlockSpec(memory_space=pl.ANY)] * self.n

    def phases(self, ins, out, send, recv, loc):
        x, y, c, me = _me()

        def remote(a, k, src, dst, to):
            s = a * (NDEV - 1) + k - 1
            return pltpu.make_async_remote_copy(src_ref=src, dst_ref=dst, send_sem=send.at[s], recv_sem=recv.at[s],
                                                device_id=to, device_id_type=MESH)

        def local(a):
            return pltpu.make_async_copy(ins[a].at[me] if self.scatter else ins[a], out[a].at[me], loc.at[a])

        if not self.two_level:
            def mine(a, k):
                peer, pid = _peer(x, y, c, k)
                return remote(a, k, ins[a].at[pid] if self.scatter else ins[a], out[a].at[me], peer)

            def theirs(a, k):
                peer, pid = _peer(x, y, c, k)
                return remote(a, k, ins[a].at[pid] if self.scatter else ins[a], out[a].at[pid], peer)

            def start():
                for a in range(self.n):
                    local(a).start()
                    for k in range(1, NDEV):
                        mine(a, k).start()

            def forward():
                pass

            def finish():
                for a in range(self.n):
                    for k in range(1, NDEV):
                        mine(a, k).wait_send()
                    for k in range(1, NDEV):
                        theirs(a, k).wait_recv()
                    local(a).wait()

            return start, forward, finish

        sibling = (x, y, 1 - c)
        chips = [(1 - x, y), (x, 1 - y), (1 - x, 1 - y)]
        slot = lambda px, py, pc: 4 * px + 2 * py + pc

        def own(a, k, to):
            return remote(a, k, ins[a], out[a].at[me], to)

        def landed(a, k, frm):
            return remote(a, k, ins[a], out[a].at[slot(*frm)], frm)

        def passed(a, j):
            rows = out[a].at[slot(*chips[j], c)]
            return remote(a, 5 + j, rows, rows, sibling)

        def start():
            for a in range(self.n):
                local(a).start()
                own(a, 1, sibling).start()
                for j, chip in enumerate(chips):
                    own(a, 2 + j, (*chip, c)).start()

        def forward():
            for a in range(self.n):
                for j, chip in enumerate(chips):
                    landed(a, 2 + j, (*chip, c)).wait_recv()
                    passed(a, j).start()

        def finish():
            for a in range(self.n):
                landed(a, 1, sibling).wait_recv()
                for j, chip in enumerate(chips):
                    remote(a, 5 + j, ins[a], out[a].at[slot(*chip, 1 - c)], sibling).wait_recv()
                own(a, 1, sibling).wait_send()
                for j, chip in enumerate(chips):
                    own(a, 2 + j, (*chip, c)).wait_send()
                    passed(a, j).wait_send()
                local(a).wait()

        return start, forward, finish


def _ride(body, n_in, n_out, n_scr, comm, first, mid, last):
    k = comm.n

    def wrapped(*refs):
        ins, cins = refs[:n_in], refs[n_in:n_in + k]
        o0 = n_in + k
        outs, couts = refs[o0:o0 + n_out], refs[o0 + n_out:o0 + n_out + k]
        s0 = o0 + n_out + k
        scr, sems = refs[s0:s0 + n_scr], refs[s0 + n_scr:]
        start, forward, finish = comm.phases(cins, couts, *sems)
        pl.when(first())(start)
        body(*ins, *outs, *scr)
        pl.when(mid())(forward)
        pl.when(last())(finish)

    return wrapped


def _exchange(arrs, name, scatter=False, two_level=False):
    comm = _Comm(arrs, scatter, two_level)

    def body(*refs):
        start, forward, finish = comm.phases(refs[:comm.n], refs[comm.n:2 * comm.n], *refs[2 * comm.n:])
        start()
        forward()
        finish()

    return pl.pallas_call(body, name=name, out_shape=comm.out_shape, in_specs=comm.specs, out_specs=comm.specs,
                          scratch_shapes=comm.scratch, compiler_params=pltpu.CompilerParams(has_side_effects=True))(*arrs)


def mm(a, b, mode, out_dtype, name, tm=1024, tn=1024, tk=1024, precision=None, comm=None, b_cols=None):
    a_parts = a.shape[0] if a.ndim == 3 else 1
    b_parts = b.shape[0] if b.ndim == 3 else 1
    assert (a_parts == 1 or mode == "nt") and (b_parts == 1 or mode == "tn")
    ash, bsh = a.shape[-2:], b.shape[-2:]
    if mode == "nn":
        (M, K), (K2, N) = ash, bsh
    elif mode == "nt":
        (M, K), (N, K2) = (ash[0], ash[1] * a_parts), bsh
    else:
        (K, M), (K2, N) = ash, (bsh[0], bsh[1] * b_parts)
    assert K == K2, (name, a.shape, b.shape)
    col0 = 0
    if b_cols is not None:
        assert mode == "nn" and tn % LANE == 0
        col0, N = b_cols[0], b_cols[1] * tn
    tm, tn, tk = _pick(M, tm), _pick(N // b_parts, tn), _pick(K // a_parts, tk)
    nk = K // tk
    if mode == "tn":
        a_spec = pl.BlockSpec((tk, tm), lambda i, j, k: (k, i))
    elif a_parts > 1:
        per = nk // a_parts
        a_spec = pl.BlockSpec((None, tm, tk), lambda i, j, k: (k // per, i, k % per))
    else:
        a_spec = pl.BlockSpec((tm, tk), lambda i, j, k: (i, k))
    if mode == "nt":
        b_spec = pl.BlockSpec((tn, tk), lambda i, j, k: (j, k))
    elif b_parts > 1:
        per = N // tn // b_parts
        b_spec = pl.BlockSpec((None, tk, tn), lambda i, j, k: (j // per, k, j % per))
    else:
        b_spec = pl.BlockSpec((tk, tn), lambda i, j, k: (k, col0 + j))
    dims = {"nn": ((1,), (0,)), "nt": ((1,), (1,)), "tn": ((0,), (0,))}[mode]

    def body(a_ref, b_ref, o_ref, *scr):
        p = lax.dot_general(a_ref[...], b_ref[...], (dims, ((), ())), preferred_element_type=f32, precision=precision)
        if nk == 1:
            o_ref[...] = p.astype(o_ref.dtype)
        else:
            acc = scr[0]
            k = pl.program_id(2)

            @pl.when(k == 0)
            def _():
                acc[...] = p

            @pl.when(k > 0)
            def _():
                acc[...] += p

            @pl.when(k == nk - 1)
            def _():
                o_ref[...] = acc[...].astype(o_ref.dtype)

    grid = (M // tm, N // tn, nk)
    scratch = [pltpu.VMEM((tm, tn), f32)] if nk > 1 else []
    out_spec = pl.BlockSpec((tm, tn), lambda i, j, k: (i, j))
    out_shape = jax.ShapeDtypeStruct((M, N), out_dtype)
    if comm is None:
        return pl.pallas_call(body, name=name, grid=grid, in_specs=[a_spec, b_spec], out_specs=out_spec, out_shape=out_shape,
                              scratch_shapes=scratch, compiler_params=_cp(("parallel", "parallel", "arbitrary")))(a, b)
    at = lambda pos: lambda: functools.reduce(jnp.logical_and, [pl.program_id(d) == p for d, p in enumerate(pos)])
    end = tuple(g - 1 for g in grid)
    return pl.pallas_call(
        _ride(body, 2, 1, len(scratch), comm, at((0, 0, 0)), at(end), at(end)), name=name, grid=grid,
        in_specs=[a_spec, b_spec] + comm.specs, out_specs=[out_spec] + comm.specs, out_shape=[out_shape] + comm.out_shape,
        scratch_shapes=scratch + comm.scratch, compiler_params=_cp(("arbitrary", "arbitrary", "arbitrary")),
    )(a, b, *comm.arrs)


def rowcall(name, fn, tok, bat, con, tok_out, acc_out, ts=256):
    B, S = tok[0][0].shape[:2]
    ts = min(ts, S)
    nt, nb, nc, no, na = len(tok), len(bat), len(con), len(tok_out), len(acc_out)

    def body(*refs):
        tr, br, cr = refs[:nt], refs[nt:nt + nb], refs[nt + nb:nt + nb + nc]
        orf, arf = refs[nt + nb + nc:nt + nb + nc + no], refs[nt + nb + nc + no:]
        touts, aouts = fn([r[0] for r in tr], [r[0] for r in br], [r[...] for r in cr])
        for r, v in zip(orf, touts):
            r[0] = v.astype(r.dtype)
        s = pl.program_id(1)
        for r, v in zip(arf, aouts):
            @pl.when(s == 0)
            def _(r=r):
                r[...] = jnp.zeros(r.shape, r.dtype)
            r[0] += v.astype(f32)

    in_specs = [pl.BlockSpec((1, ts, w), lambda b, s, cb=cb: (b, s, cb)) for (_, w, cb) in tok]
    in_specs += [pl.BlockSpec((1,) + a.shape[1:], lambda b, s: (b, 0, 0)) for a in bat]
    in_specs += [pl.BlockSpec(a.shape, lambda b, s, nd=a.ndim: (0,) * nd) for a in con]
    out_specs = [pl.BlockSpec((1, ts, w), lambda b, s: (b, s, 0)) for (w, _) in tok_out]
    out_specs += [pl.BlockSpec((1,) + shp, lambda b, s, nd=len(shp): (b,) + (0,) * nd) for shp in acc_out]
    out_shape = [jax.ShapeDtypeStruct((B, S, w), dt) for (w, dt) in tok_out]
    out_shape += [jax.ShapeDtypeStruct((B,) + shp, f32) for shp in acc_out]
    return pl.pallas_call(
        body, name=name, grid=(B, S // ts), in_specs=in_specs, out_specs=out_specs, out_shape=out_shape,
        compiler_params=_cp(("parallel", "arbitrary")),
    )(*[t[0] for t in tok], *bat, *con)


def rowcall_fwd(name, f, tok, bat, con, tok_out, ts=256):
    def fn(t, b, c):
        return f([v.astype(f32) for v in t], b, c), []
    return rowcall(name, fn, tok, bat, con, tok_out, [], ts)


def rowcall_bwd(name, f, tok, bat, con, cts, tok_grads, add=None, ts=256):
    nt, ncts = len(tok), len(cts)

    def fn(t, b, c):
        prim = [v.astype(f32) for v in t[:nt]]
        ct = [v.astype(f32) for v in t[nt:nt + ncts]]
        _, vjp = jax.vjp(lambda tt, bb, cc: f(tt, bb, cc), prim, b, c)
        dt, db, dc = vjp(ct)
        touts = [dt[i] for i, _ in tok_grads]
        if add is not None:
            touts[0] = touts[0] + t[nt + ncts].astype(f32)
        return touts, list(db) + list(dc)

    all_tok = list(tok) + list(cts) + ([add] if add is not None else [])
    tok_out = [(tok[i][1], dt) for i, dt in tok_grads]
    acc_out = [tuple(a.shape[1:]) for a in bat] + [tuple(a.shape) for a in con]
    return rowcall(name, fn, all_tok, bat, con, tok_out, acc_out, ts)


def _rms(y, w):
    return y * lax.rsqrt(jnp.mean(y * y, axis=-1, keepdims=True) + RMS_EPS) * w


def f_rms_mod(t, b, c):
    return [_rms(t[0], c[0]) * (1.0 + b[0]) + b[1]]


def f_resid(t, b, c):
    return [t[0] + b[0] * _rms(t[1], c[0])]


def f_merge(t, b, c):
    ga, gd, ya, yd = t
    return [jax.nn.sigmoid(ga) * ya + jax.nn.sigmoid(gd) * yd]


def f_dnout(t, b, c):
    o, z = t
    outs = []
    for h in range(DNH):
        sl = slice(h * DND, (h + 1) * DND)
        zh = z[:, sl]
        outs.append(_rms(o[:, sl], c[0]) * (zh * jax.nn.sigmoid(zh)))
    return [jnp.concatenate(outs, axis=1)]


def _softplus(x):
    return jnp.maximum(x, 0.0) + jnp.log(1.0 + jnp.exp(-jnp.abs(x)))


def f_gate(t, b, c):
    ba = t[0]
    a_log, dt_bias = c
    lane = lax.broadcasted_iota(jnp.int32, ba.shape, 1)
    beta = jax.nn.sigmoid(ba)
    g = -jnp.exp(a_log) * _softplus(ba + dt_bias)
    return [jnp.where(lane < DNH, beta, jnp.where(lane < 2 * DNH, g, 0.0))]


def _bucket_table():
    qi = np.arange(WIN)[:, None]
    kj = np.arange(2 * WIN)[None, :]
    dist = np.maximum(WIN + qi - kj, 0)
    max_exact = NBUCK // 2
    scaled = np.log(np.maximum(dist, 1).astype(np.float64) / max_exact) / math.log(MAXDIST / max_exact)
    large = np.minimum(max_exact + (scaled * (NBUCK - max_exact)).astype(np.int32), NBUCK - 1)
    return np.where(dist < max_exact, dist, large).astype(np.int32)


def _attn_mask(n):
    qi = lax.broadcasted_iota(jnp.int32, (WIN, 2 * WIN), 0)
    kj = lax.broadcasted_iota(jnp.int32, (WIN, 2 * WIN), 1)
    dist = WIN + qi - kj
    return (dist >= 0) & (dist < WIN) & ((kj >= WIN) | (n > 0))


def _attn_block(q, kp, kc, vp, vc, bias, sinks, mask, differentiated):
    dot = _bdot_bf16_vjp if differentiated else _bdot_bf16
    grp = HQ // HKV
    band = lambda p, c, j: jnp.concatenate([p[:, j * LANE:(j + 1) * LANE], c[:, j * LANE:(j + 1) * LANE]], axis=0)
    qh = _stack([q[:, h * LANE:(h + 1) * LANE] for h in range(HQ)])
    kb = _stack([band(kp, kc, h // grp) for h in range(HQ)])
    vb = _stack([band(vp, vc, h // grp) for h in range(HQ)])
    s = dot(qh, kb, 2, 2) * (HD ** -0.5)
    s = jnp.where(mask[None], s + bias, NEG_INF)
    m = jnp.maximum(jnp.max(s, axis=-1, keepdims=True), sinks)
    p = jnp.exp(s - m)
    probs = p / (jnp.sum(p, axis=-1, keepdims=True) + jnp.exp(sinks - m))
    o = dot(probs, vb, 2, 1)
    return jnp.concatenate([o[h] for h in range(HQ)], axis=1)


def _attn_specs(NB):
    last = NB - 1
    return [
        pl.BlockSpec((1, WIN, HQ * LANE), lambda b, n: (b, jnp.minimum(n, last), CB_AQ // 8)),
        pl.BlockSpec((1, WIN, HKV * LANE), lambda b, n: (b, jnp.clip(n - 1, 0, last), CB_AK // 2)),
        pl.BlockSpec((1, WIN, HKV * LANE), lambda b, n: (b, jnp.minimum(n, last), CB_AK // 2)),
        pl.BlockSpec((1, WIN, HKV * LANE), lambda b, n: (b, jnp.clip(n - 1, 0, last), CB_AV // 2)),
        pl.BlockSpec((1, WIN, HKV * LANE), lambda b, n: (b, jnp.minimum(n, last), CB_AV // 2)),
        pl.BlockSpec((HQ, WIN, 2 * WIN), lambda b, n: (0, 0, 0)),
        pl.BlockSpec((HQ, 1, 1), lambda b, n: (0, 0, 0)),
    ]


def attn_fwd(proj, bias, sinks, comm):
    B, S, _ = proj.shape
    NB = S // WIN

    def body(q, kp, kc, vp, vc, bias_ref, sink_ref, o_ref):
        mask = _attn_mask(pl.program_id(1))
        o = _attn_block(*[r[0].astype(f32) for r in (q, kp, kc, vp, vc)], bias_ref[...], sink_ref[...], mask, False)
        o_ref[0] = o.astype(o_ref.dtype)

    at = lambda b, n: lambda: (pl.program_id(0) == b) & (pl.program_id(1) == n)
    return pl.pallas_call(
        _ride(body, 7, 1, 0, comm, at(0, 0), at(B - 1, (3 * NB) // 4), at(B - 1, NB - 1)), name="attn_fwd", grid=(B, NB),
        in_specs=_attn_specs(NB) + comm.specs,
        out_specs=[pl.BlockSpec((1, WIN, HQ * LANE), lambda b, n: (b, n, 0))] + comm.specs,
        out_shape=[jax.ShapeDtypeStruct((B, S, HQ * LANE), bf16)] + comm.out_shape, scratch_shapes=comm.scratch,
        compiler_params=_cp(("arbitrary", "arbitrary")),
    )(proj, proj, proj, proj, proj, bias, sinks, *comm.arrs)


def attn_bwd(proj, bias, sinks, dy, comm):
    B, S, _ = proj.shape
    NB = S // WIN
    last = NB - 1

    def body(q, kp, kc, vp, vc, bias_ref, sink_ref, dy_ref, dq_ref, dk_ref, dv_ref, dbias_ref, dsink_ref, kcar, vcar):
        b, n = pl.program_id(0), pl.program_id(1)

        @pl.when((b == 0) & (n == 0))
        def _():
            dbias_ref[...] = jnp.zeros(dbias_ref.shape, f32)
            dsink_ref[...] = jnp.zeros(dsink_ref.shape, f32)

        @pl.when(n == 0)
        def _():
            kcar[...] = jnp.zeros(kcar.shape, f32)
            vcar[...] = jnp.zeros(vcar.shape, f32)

        @pl.when(n < NB)
        def _():
            mask = _attn_mask(n)
            _, vjp = jax.vjp(lambda *a: _attn_block(*a, mask, True), *[r[0].astype(f32) for r in (q, kp, kc, vp, vc)],
                             bias_ref[...], sink_ref[...])
            dq, dkp, dkc, dvp, dvc, dbias, dsink = vjp(dy_ref[0].astype(f32))
            dq_ref[0] = dq.astype(dq_ref.dtype)
            dbias_ref[...] += dbias
            dsink_ref[...] += dsink
            dk_ref[0] = (kcar[...] + dkp).astype(dk_ref.dtype)
            dv_ref[0] = (vcar[...] + dvp).astype(dv_ref.dtype)
            kcar[...] = dkc
            vcar[...] = dvc

        @pl.when(n == NB)
        def _():
            dk_ref[0] = kcar[...].astype(dk_ref.dtype)
            dv_ref[0] = vcar[...].astype(dv_ref.dtype)

    in_specs = _attn_specs(NB) + [pl.BlockSpec((1, WIN, HQ * LANE), lambda b, n: (b, jnp.minimum(n, last), 0))]
    kv_out = pl.BlockSpec((1, WIN, HKV * LANE), lambda b, n: (b, jnp.maximum(n - 1, 0), 0))
    at = lambda b, n: lambda: (pl.program_id(0) == b) & (pl.program_id(1) == n)
    return pl.pallas_call(
        _ride(body, 8, 5, 2, comm, at(0, 0), at(B - 1, NB), at(B - 1, NB)), name="attn_bwd", grid=(B, NB + 1),
        in_specs=in_specs + comm.specs,
        out_specs=[pl.BlockSpec((1, WIN, HQ * LANE), lambda b, n: (b, jnp.minimum(n, last), 0)), kv_out, kv_out,
                   pl.BlockSpec((HQ, WIN, 2 * WIN), lambda b, n: (0, 0, 0)), pl.BlockSpec((HQ, 1, 1), lambda b, n: (0, 0, 0))] + comm.specs,
        out_shape=[jax.ShapeDtypeStruct((B, S, HQ * LANE), bf16), jax.ShapeDtypeStruct((B, S, HKV * LANE), bf16),
                   jax.ShapeDtypeStruct((B, S, HKV * LANE), bf16), jax.ShapeDtypeStruct((HQ, WIN, 2 * WIN), f32),
                   jax.ShapeDtypeStruct((HQ, 1, 1), f32)] + comm.out_shape,
        scratch_shapes=[pltpu.VMEM((WIN, HKV * LANE), f32), pltpu.VMEM((WIN, HKV * LANE), f32)] + comm.scratch,
        compiler_params=_cp(("arbitrary", "arbitrary")),
    )(proj, proj, proj, proj, proj, bias, sinks, dy, *comm.arrs)


DN_ROWS, FFN_ROWS = 256, 32


def _stage_rows(dst, value):
    dst[0:8] = jnp.zeros((8, LANE), f32)
    dst[8:8 + value.shape[0]] = value


def _conv_rows(xs, w, width, r, rows):
    wins = [xs[pl.ds(r + 8 - (width - 1) + j, rows), :] for j in range(width)]
    out = w[0:1] * wins[0]
    for j in range(1, width):
        out = out + w[j:j + 1] * wins[j]
    return out, wins


def _fold8(v):
    return jnp.sum(v.reshape(v.shape[0] // 8, 8, LANE), axis=0)


def _conv_rows_t(ds, w, width, r, rows):
    out = w[0:1] * ds[pl.ds(r + width - 1, rows), :]
    for j in range(1, width):
        out = out + w[j:j + 1] * ds[pl.ds(r + width - 1 - j, rows), :]
    return out


def _dn_outblk(i):
    return (i % DNH) * 3 + i // DNH


def _dn_act(c, isqk):
    sg = jax.nn.sigmoid(c)
    y = c * sg
    n = lax.rsqrt(jnp.sum(y * y, axis=-1, keepdims=True) + L2_EPS)
    return jnp.where(isqk, y * n, y), sg, n


def dnconv_fwd(proj, conv_w):
    B, S, _ = proj.shape
    rows = min(DN_ROWS, S)

    def body(x_ref, w_ref, o_ref, xs):
        isqk = pl.program_id(0) < 2 * DNH
        _stage_rows(xs, x_ref[0].astype(f32))
        w = w_ref[...]
        for r in range(0, S, rows):
            c, _ = _conv_rows(xs, w, DNK, r, rows)
            o_ref[0, pl.ds(r, rows), :] = _dn_act(c, isqk)[0]

    return pl.pallas_call(
        body, name="dnconv_fwd", grid=(3 * DNH, B),
        in_specs=[pl.BlockSpec((1, S, LANE), lambda i, b: (b, 0, CB_DQKV + i)), pl.BlockSpec((DNK, LANE), lambda i, b: (0, i))],
        out_specs=pl.BlockSpec((1, S, LANE), lambda i, b: (b, 0, _dn_outblk(i))),
        out_shape=jax.ShapeDtypeStruct((B, S, 3 * DNH * DND), f32), scratch_shapes=[pltpu.VMEM((S + 8, LANE), f32)],
        compiler_params=_cp(("parallel", "parallel")),
    )(proj, conv_w)


def dnconv_bwd(proj, conv_w, dqkvn):
    B, S, _ = proj.shape
    rows = min(DN_ROWS, S)

    def body(x_ref, w_ref, dy_ref, dx_ref, dw_ref, xs, ds):
        isqk = pl.program_id(0) < 2 * DNH
        _stage_rows(xs, x_ref[0].astype(f32))
        w = w_ref[...]
        dw = [jnp.zeros((8, LANE), f32) for _ in range(DNK)]
        for r in range(0, S, rows):
            c, wins = _conv_rows(xs, w, DNK, r, rows)
            out, sg, n = _dn_act(c, isqk)
            dout = dy_ref[0, pl.ds(r, rows), :]
            dy = jnp.where(isqk, n * (dout - out * jnp.sum(dout * out, axis=-1, keepdims=True)), dout)
            dc = dy * (sg * (1.0 + c * (1.0 - sg)))
            ds[pl.ds(r, rows), :] = dc
            for j in range(DNK):
                dw[j] = dw[j] + _fold8(dc * wins[j])
        ds[S:S + 8] = jnp.zeros((8, LANE), f32)
        for r in range(0, S, rows):
            dx_ref[0, pl.ds(r, rows), :] = _conv_rows_t(ds, w, DNK, r, rows).astype(dx_ref.dtype)

        @pl.when(pl.program_id(1) == 0)
        def _():
            dw_ref[...] = jnp.zeros(dw_ref.shape, f32)
        dw_ref[...] += jnp.concatenate([jnp.sum(d, axis=0, keepdims=True) for d in dw], axis=0)

    return pl.pallas_call(
        body, name="dnconv_bwd", grid=(3 * DNH, B),
        in_specs=[pl.BlockSpec((1, S, LANE), lambda i, b: (b, 0, CB_DQKV + i)), pl.BlockSpec((DNK, LANE), lambda i, b: (0, i)),
                  pl.BlockSpec((1, S, LANE), lambda i, b: (b, 0, _dn_outblk(i)))],
        out_specs=[pl.BlockSpec((1, S, LANE), lambda i, b: (b, 0, i)), pl.BlockSpec((DNK, LANE), lambda i, b: (0, i))],
        out_shape=[jax.ShapeDtypeStruct((B, S, 3 * DNH * DND), bf16), jax.ShapeDtypeStruct((DNK, 3 * DNH * DND), f32)],
        scratch_shapes=[pltpu.VMEM((S + 8, LANE), f32), pltpu.VMEM((S + 8, LANE), f32)],
        compiler_params=_cp(("parallel", "arbitrary")),
    )(proj, conv_w, dqkvn)


def _bdot(a, b, ca, cb, precision=HI):
    return lax.dot_general(a, b, (((ca,), (cb,)), ((0,), (0,))), preferred_element_type=f32, precision=precision)


def _bdot_bf16(a, b, ca, cb):
    return _bdot(a.astype(bf16), b.astype(bf16), ca, cb, None)


@functools.partial(jax.custom_vjp, nondiff_argnums=(2, 3))
def _bdot_bf16_vjp(a, b, ca, cb):
    return _bdot_bf16(a, b, ca, cb)


def _bdot_bf16_fwd(a, b, ca, cb):
    return _bdot_bf16(a, b, ca, cb), (a, b)


def _bdot_bf16_bwd(ca, cb, res, g):
    a, b = res
    fa, fb = 3 - ca, 3 - cb
    da = _bdot_bf16(g, b, 2, fb) if ca == 2 else _bdot_bf16(b, g, fb, 2)
    db = _bdot_bf16(a, g, fa, 1) if cb == 1 else _bdot_bf16(g, a, 1, fa)
    return da, db


_bdot_bf16_vjp.defvjp(_bdot_bf16_fwd, _bdot_bf16_bwd)


def _neumann_inverse(low):
    n = low.shape[-1]
    eye = (lax.broadcasted_iota(jnp.int32, (n, n), 0) == lax.broadcasted_iota(jnp.int32, (n, n), 1)).astype(f32)
    p = -low
    x = eye[None] + p
    for _ in range(5):
        p = _bdot(p, p, 2, 1, MID)
        x = x + _bdot(x, p, 2, 1, MID)
    return x


@jax.custom_vjp
def _unit_lower_inverse(low):
    return _neumann_inverse(low)


def _uli_fwd(low):
    t = _neumann_inverse(low)
    return t, t


def _uli_bwd(t, dt):
    return (-_bdot(_bdot(t, dt, 1, 1, MID), t, 2, 2, MID),)


_unit_lower_inverse.defvjp(_uli_fwd, _uli_bwd)


def _stack(xs):
    return jnp.concatenate([x[None] for x in xs], axis=0)


def _delta_chunk(qkv, bg, state, differentiated):
    inverse = _unit_lower_inverse if differentiated else _neumann_inverse
    lo = _bdot_bf16_vjp if differentiated else _bdot_bf16
    B = qkv.shape[0]
    G = B * DNH
    pairs = [(b, h) for b in range(B) for h in range(DNH)]
    col = lambda b, h, kind: qkv[b, :, (3 * h + kind) * DND:(3 * h + kind + 1) * DND]
    q, k, v = [_stack([col(b, h, kind) for b, h in pairs]) for kind in range(3)]
    lane = lax.broadcasted_iota(jnp.int32, (CH, LANE), 1)
    pick = lambda b, l: jnp.sum(jnp.where(lane == l, bg[b], 0.0), axis=1, keepdims=True)
    beta = _stack([pick(b, h) for b, h in pairs])
    g = _stack([pick(b, h + DNH) for b, h in pairs])
    ri = lax.broadcasted_iota(jnp.int32, (CH, CH), 0)
    ci = lax.broadcasted_iota(jnp.int32, (CH, CH), 1)
    incl, strict = (ri >= ci)[None], (ri > ci)[None]
    gc = _bdot(jnp.broadcast_to(incl.astype(f32), (G, CH, CH)), jnp.broadcast_to(g, (G, CH, LANE)), 2, 1)
    e0 = jnp.broadcast_to((lane == 0).astype(f32)[None], (G, CH, LANE))
    gc_row = _bdot(e0, gc, 2, 2)
    diff = gc[:, :, :CH] - gc_row
    decay = jnp.where(incl, jnp.exp(jnp.where(incl, diff, 0.0)), 0.0)
    qs = q * (DND ** -0.5)
    kb, vb = k * beta, v * beta
    eg = jnp.exp(gc)
    low = jnp.where(strict, lo(kb, k, 2, 2) * decay, 0.0)
    tinv = inverse(low)
    u = _bdot(tinv, vb, 2, 1, MID)
    w = _bdot(tinv, kb * eg, 2, 1, MID)
    intra = jnp.where(incl, lo(qs, k, 2, 2) * decay, 0.0)
    gl = gc[:, CH - 1:CH, :]
    k_tail = k * jnp.exp(gl - gc)
    v_new = u - lo(w, state, 2, 1)
    o = lo(qs * eg, state, 2, 1) + lo(intra, v_new, 2, 1)
    new_state = state * jnp.exp(gl) + lo(k_tail, v_new, 1, 1)
    return o, new_state


def delta_fwd(qkvn, bg, comm):
    B, S, _ = qkvn.shape
    NC, G = S // CH, B * DNH

    def body(qkv_ref, bg_ref, o_ref, st_ref, state):
        @pl.when(pl.program_id(0) == 0)
        def _():
            state[...] = jnp.zeros(state.shape, f32)
        s0 = state[...]
        st_ref[0] = s0
        o, s1 = _delta_chunk(qkv_ref[...], bg_ref[...], s0, False)
        for b in range(B):
            for h in range(DNH):
                o_ref[b, :, h * DND:(h + 1) * DND] = o[b * DNH + h]
        state[...] = s1

    at = lambda c: lambda: pl.program_id(0) == c
    return pl.pallas_call(
        _ride(body, 2, 2, 1, comm, at(0), at((7 * NC) // 8), at(NC - 1)), name="delta_fwd", grid=(NC,),
        in_specs=[pl.BlockSpec((B, CH, 3 * DNH * DND), lambda c: (0, c, 0)), pl.BlockSpec((B, CH, LANE), lambda c: (0, c, 0))] + comm.specs,
        out_specs=[pl.BlockSpec((B, CH, DNH * DND), lambda c: (0, c, 0)), pl.BlockSpec((1, G, DND, DND), lambda c: (c, 0, 0, 0))] + comm.specs,
        out_shape=[jax.ShapeDtypeStruct((B, S, DNH * DND), f32), jax.ShapeDtypeStruct((NC, G, DND, DND), f32)] + comm.out_shape,
        scratch_shapes=[pltpu.VMEM((G, DND, DND), f32)] + comm.scratch, compiler_params=_cp(("arbitrary",)),
    )(qkvn, bg, *comm.arrs)


def delta_bwd(qkvn, bg, states, do, comm):
    B, S, _ = qkvn.shape
    NC, G = S // CH, B * DNH

    def body(qkv_ref, bg_ref, st_ref, do_ref, dqkv_ref, dbg_ref, dstate):
        @pl.when(pl.program_id(0) == 0)
        def _():
            dstate[...] = jnp.zeros(dstate.shape, f32)
        _, vjp = jax.vjp(lambda a, g, s: _delta_chunk(a, g, s, True), qkv_ref[...], bg_ref[...], st_ref[0])
        do = _stack([do_ref[b, :, h * DND:(h + 1) * DND] for b in range(B) for h in range(DNH)])
        dqkv, dbg, ds = vjp((do, dstate[...]))
        dqkv_ref[...] = dqkv
        dbg_ref[...] = dbg
        dstate[...] = ds

    rev = lambda c: NC - 1 - c
    at = lambda c: lambda: pl.program_id(0) == c
    return pl.pallas_call(
        _ride(body, 4, 2, 1, comm, at(0), at(NC - 1), at(NC - 1)), name="delta_bwd", grid=(NC,),
        in_specs=[pl.BlockSpec((B, CH, 3 * DNH * DND), lambda c: (0, rev(c), 0)), pl.BlockSpec((B, CH, LANE), lambda c: (0, rev(c), 0)),
                  pl.BlockSpec((1, G, DND, DND), lambda c: (rev(c), 0, 0, 0)),
                  pl.BlockSpec((B, CH, DNH * DND), lambda c: (0, rev(c), 0))] + comm.specs,
        out_specs=[pl.BlockSpec((B, CH, 3 * DNH * DND), lambda c: (0, rev(c), 0)),
                   pl.BlockSpec((B, CH, LANE), lambda c: (0, rev(c), 0))] + comm.specs,
        out_shape=[jax.ShapeDtypeStruct((B, S, 3 * DNH * DND), f32), jax.ShapeDtypeStruct((B, S, LANE), f32)] + comm.out_shape,
        scratch_shapes=[pltpu.VMEM((G, DND, DND), f32)] + comm.scratch, compiler_params=_cp(("arbitrary",)),
    )(qkvn, bg, states, do, *comm.arrs)


GELU_C0, GELU_C1 = math.sqrt(2.0 / math.pi), 0.044715


def _ffn_specs(S):
    nblk = DFF // LANE
    return [pl.BlockSpec((1, S, LANE), lambda i, b: (b, 0, i)), pl.BlockSpec((1, S, LANE), lambda i, b: (b, 0, nblk + i)),
            pl.BlockSpec((FK, LANE), lambda i, b: (0, i)), pl.BlockSpec((FK, LANE), lambda i, b: (0, nblk + i))]


def ffnconv_fwd(up, conv_w):
    B, S, _ = up.shape
    rows = min(FFN_ROWS, S)

    def body(g_ref, v_ref, gw_ref, vw_ref, o_ref, xg, xv):
        _stage_rows(xg, g_ref[0].astype(f32))
        _stage_rows(xv, v_ref[0].astype(f32))
        gw, vw = gw_ref[...], vw_ref[...]
        for r in range(0, S, rows):
            g, _ = _conv_rows(xg, gw, FK, r, rows)
            v, _ = _conv_rows(xv, vw, FK, r, rows)
            t = jnp.tanh(GELU_C0 * (g * (1.0 + GELU_C1 * (g * g))))
            o_ref[0, pl.ds(r, rows), :] = (0.5 * g * (1.0 + t) * v).astype(o_ref.dtype)

    return pl.pallas_call(
        body, name="ffnconv_fwd", grid=(DFF // LANE, B), in_specs=_ffn_specs(S),
        out_specs=pl.BlockSpec((1, S, LANE), lambda i, b: (b, 0, i)), out_shape=jax.ShapeDtypeStruct((B, S, DFF), bf16),
        scratch_shapes=[pltpu.VMEM((S + 8, LANE), f32)] * 2, compiler_params=_cp(("parallel", "parallel")),
    )(up, up, conv_w, conv_w)


def ffnconv_bwd(up, conv_w, dact):
    B, S, _ = up.shape
    rows = min(FFN_ROWS, S)

    def body(g_ref, v_ref, gw_ref, vw_ref, dy_ref, dx_ref, dw_ref, xg, xv, dg, dv):
        _stage_rows(xg, g_ref[0].astype(f32))
        _stage_rows(xv, v_ref[0].astype(f32))
        gw, vw = gw_ref[...], vw_ref[...]
        dgw = [jnp.zeros((8, LANE), f32) for _ in range(FK)]
        dvw = [jnp.zeros((8, LANE), f32) for _ in range(FK)]
        for r in range(0, S, rows):
            g, gwins = _conv_rows(xg, gw, FK, r, rows)
            v, vwins = _conv_rows(xv, vw, FK, r, rows)
            g2 = g * g
            t = jnp.tanh(GELU_C0 * (g * (1.0 + GELU_C1 * g2)))
            half = 0.5 * (1.0 + t)
            dgelu = half + (0.5 * GELU_C0) * g * (1.0 - t * t) * (1.0 + (3.0 * GELU_C1) * g2)
            dy = dy_ref[0, pl.ds(r, rows), :].astype(f32)
            dvc = dy * (g * half)
            dgc = dy * v * dgelu
            dg[pl.ds(r, rows), :] = dgc
            dv[pl.ds(r, rows), :] = dvc
            for j in range(FK):
                dgw[j] = dgw[j] + _fold8(dgc * gwins[j])
                dvw[j] = dvw[j] + _fold8(dvc * vwins[j])
        dg[S:S + 8] = jnp.zeros((8, LANE), f32)
        dv[S:S + 8] = jnp.zeros((8, LANE), f32)
        for r in range(0, S, rows):
            dx_ref[0, 0, pl.ds(r, rows), :] = _conv_rows_t(dg, gw, FK, r, rows).astype(dx_ref.dtype)
            dx_ref[1, 0, pl.ds(r, rows), :] = _conv_rows_t(dv, vw, FK, r, rows).astype(dx_ref.dtype)

        @pl.when(pl.program_id(1) == 0)
        def _():
            dw_ref[...] = jnp.zeros(dw_ref.shape, f32)
        dw_ref[0] += jnp.concatenate([jnp.sum(d, axis=0, keepdims=True) for d in dgw], axis=0)
        dw_ref[1] += jnp.concatenate([jnp.sum(d, axis=0, keepdims=True) for d in dvw], axis=0)

    return pl.pallas_call(
        body, name="ffnconv_bwd", grid=(DFF // LANE, B),
        in_specs=_ffn_specs(S) + [pl.BlockSpec((1, S, LANE), lambda i, b: (b, 0, i))],
        out_specs=[pl.BlockSpec((2, 1, S, LANE), lambda i, b: (0, b, 0, i)), pl.BlockSpec((2, FK, LANE), lambda i, b: (0, 0, i))],
        out_shape=[jax.ShapeDtypeStruct((2, B, S, DFF), bf16), jax.ShapeDtypeStruct((2, FK, DFF), f32)],
        scratch_shapes=[pltpu.VMEM((S + 8, LANE), f32)] * 4, compiler_params=_cp(("parallel", "arbitrary")),
    )(up, up, conv_w, conv_w, dact)


def ada_fwd(c_all, ada_w, ada_b):
    def body(c_ref, w_ref, b_ref, o_ref):
        c = c_ref[...]
        act = (c * jax.nn.sigmoid(c)).astype(bf16)
        o_ref[...] = jnp.dot(act, w_ref[...].astype(bf16), preferred_element_type=f32) + b_ref[...]

    return pl.pallas_call(body, name="ada_fwd", out_shape=jax.ShapeDtypeStruct((c_all.shape[0], ada_w.shape[1]), f32),
                          compiler_params=pltpu.CompilerParams(vmem_limit_bytes=VMEM_LIMIT))(c_all, ada_w, ada_b)


def ada_bwd(c_all, dmod):
    def body(c_ref, d_ref, o_ref):
        c = c_ref[...]
        act = (c * jax.nn.sigmoid(c)).astype(bf16)
        o_ref[...] = lax.dot_general(act, d_ref[...].astype(bf16), (((0,), (0,)), ((), ())), preferred_element_type=f32)

    return pl.pallas_call(body, name="ada_bwd", out_shape=jax.ShapeDtypeStruct((c_all.shape[1], dmod.shape[1]), f32),
                          compiler_params=pltpu.CompilerParams(vmem_limit_bytes=VMEM_LIMIT))(c_all, dmod)


def loss_head(h1, y2, target, g2, w):
    def fn(t, b, c):
        h, y, tg = [v.astype(f32) for v in t]

        def loss_fn(h, y, g, w):
            e = h + g * _rms(y, w) - tg
            return 0.5 * jnp.sum(jnp.mean(e * e, axis=-1))

        loss, grads = jax.value_and_grad(loss_fn, argnums=(0, 1, 2, 3))(h, y, b[0], c[0])
        return [grads[0], grads[1]], [grads[2], grads[3], jnp.full((1, LANE), loss, f32)]

    return rowcall("loss_head", fn, [(h1, D, 0), (y2, D, 0), (target, D, 0)], [g2], [w], [(D, f32), (D, bf16)],
                   [(1, D), (1, D), (1, LANE)])


def adamw(w, gparts, m, v, name):
    R, C = w.shape
    P = gparts.shape[0]
    tr = R
    if R * C * 4 > 2 * 1024 * 1024:
        for cand in (512, 256, 128, 64, 32, 16, 8):
            if R % cand == 0 and cand * C * 4 <= 2 * 1024 * 1024:
                tr = cand
                break

    def body(w_ref, g_ref, m_ref, v_ref, go, do, mo, vo):
        g = g_ref[0].astype(f32)
        for p in range(1, P):
            g = g + g_ref[p].astype(f32)
        m2 = B1 * m_ref[...] + (1.0 - B1) * g
        v2 = B2 * v_ref[...] + (1.0 - B2) * jnp.square(g)
        m_hat = m2 * (1.0 / (1.0 - B1 ** STEP))
        v_hat = v2 * (1.0 / (1.0 - B2 ** STEP))
        go[...] = g
        do[...] = -LR * (m_hat / (jnp.sqrt(v_hat) + EPS) + WD * w_ref[...])
        mo[...] = m2
        vo[...] = v2

    blk = pl.BlockSpec((tr, C), lambda i: (i, 0))
    return pl.pallas_call(
        body, name=name, grid=(R // tr,), in_specs=[blk, pl.BlockSpec((P, tr, C), lambda i: (0, i, 0)), blk, blk],
        out_specs=[blk] * 4, out_shape=[jax.ShapeDtypeStruct((R, C), f32)] * 4, compiler_params=_cp(("parallel",)),
    )(w, gparts, m, v)


def _pad_heads(w, nh):
    r = w.shape[0]
    return jnp.pad(w.reshape(r, nh, HD), ((0, 0), (0, 0), (0, LANE - HD))).reshape(r, nh * LANE)


def _unpad_heads(w, nh):
    return w.reshape(w.shape[0], nh, LANE)[:, :, :HD].reshape(w.shape[0], nh * HD)


def _pack_w_in(w):
    aq, ak, av, dqkv, dz, dbeta, da, ga, gd = jnp.split(w, np.cumsum(IN_SPLITS)[:-1].tolist(), axis=1)
    ba = jnp.pad(jnp.concatenate([dbeta, da], axis=1), ((0, 0), (0, LANE - 2 * DNH)))
    return jnp.concatenate([ga, gd, _pad_heads(aq, HQ), dqkv, dz, _pad_heads(ak, HKV), _pad_heads(av, HKV), ba], axis=1)


def _unpack_w_in(p):
    col = lambda cb, n: p[:, cb * LANE: cb * LANE + n]
    ba = col(CB_BA, 2 * DNH)
    return jnp.concatenate([_unpad_heads(col(CB_AQ, HQ * LANE), HQ), _unpad_heads(col(CB_AK, HKV * LANE), HKV),
                            _unpad_heads(col(CB_AV, HKV * LANE), HKV), col(CB_DQKV, 3 * DNH * DND), col(CB_DZ, DNH * DND),
                            ba[:, :DNH], ba[:, DNH:], col(CB_GA, D), col(CB_GD, D)], axis=1)


def _cols_gathered(g):
    return g.transpose(1, 0, 2).reshape(g.shape[1], NDEV * g.shape[2])


def _cols_split(w):
    r = w.shape[0]
    return w.reshape(r, NDEV, w.shape[1] // NDEV).transpose(1, 0, 2)


def kernel(x, c, ada_w, ada_b, norm_mix_pre, norm_mix_post, norm_ffn_pre, norm_ffn_post, w_in, dn_conv_w, dn_a_log, dn_dt_bias, dn_norm_w, attn_sinks, rel_bias, w_attn_branch, w_dn_branch, w_out, ffn_w_up, ffn_conv_w, ffn_w_down, loss_target, m_ada_w, m_ada_b, m_norm_mix_pre, m_norm_mix_post, m_norm_ffn_pre, m_norm_ffn_post, m_w_in, m_dn_conv_w, m_dn_a_log, m_dn_dt_bias, m_dn_norm_w, m_attn_sinks, m_rel_bias, m_w_attn_branch, m_w_dn_branch, m_w_out, m_ffn_w_up, m_ffn_conv_w, m_ffn_w_down, v_ada_w, v_ada_b, v_norm_mix_pre, v_norm_mix_post, v_norm_ffn_pre, v_norm_ffn_post, v_w_in, v_dn_conv_w, v_dn_a_log, v_dn_dt_bias, v_dn_norm_w, v_attn_sinks, v_rel_bias, v_w_attn_branch, v_w_dn_branch, v_w_out, v_ffn_w_up, v_ffn_conv_w, v_ffn_w_down):
    B, S, _ = x.shape
    T = B * S
    me = 4 * lax.axis_index("x") + 2 * lax.axis_index("y") + lax.axis_index("c")
    big = dict(w_in=w_in, dn_conv_w=dn_conv_w, w_attn_branch=w_attn_branch, w_dn_branch=w_dn_branch, w_out=w_out,
               ffn_w_up=ffn_w_up, ffn_conv_w=ffn_conv_w, ffn_w_down=ffn_w_down)
    big_names = list(big)

    first, mid, late = ["w_in", "dn_conv_w"], ["w_attn_branch", "w_dn_branch", "w_out"], ["ffn_w_up", "ffn_conv_w", "ffn_w_down"]
    shard = lambda names: [big[n][0].astype(bf16) for n in names]
    *got, c_all = _exchange(shard(first) + [c], "gather_w_in", two_level=True)
    gw = dict(zip(first, got))
    c_all = c_all.reshape(NDEV * B, D)

    wp = _pack_w_in(_cols_gathered(gw["w_in"]))
    conv_dn = _cols_gathered(gw["dn_conv_w"]).astype(f32)

    ncol = ada_w.shape[2]
    ada_b_mine = lax.dynamic_slice_in_dim(ada_b, me * ncol, ncol, axis=1)
    mod_cols = ada_fwd(c_all, ada_w[0], ada_b_mine)
    (mod_g,) = _exchange([mod_cols], "gather_mod")
    mod = lax.dynamic_slice_in_dim(mod_g, me * B, B, axis=1).transpose(1, 0, 2).reshape(B, NMOD * D)
    sh1, sc1, g1, sh2, sc2, g2 = [mod[:, i * D:(i + 1) * D].reshape(B, 1, D) for i in range(NMOD)]

    onehot = (jnp.asarray(_bucket_table()).reshape(1, -1) == jnp.arange(NBUCK, dtype=jnp.int32)[:, None]).astype(f32)
    bias = mm(rel_bias.T, onehot, "nn", f32, "bias_table", tn=8192, precision=HI).reshape(HQ, WIN, 2 * WIN)
    sinks = attn_sinks.reshape(HQ, 1, 1)
    a_log_pad = jnp.pad(dn_a_log, ((0, 0), (DNH, LANE - 2 * DNH)))
    dt_bias_pad = jnp.pad(dn_dt_bias, ((0, 0), (DNH, LANE - 2 * DNH)))

    (u1,) = rowcall_fwd("mix_pre", f_rms_mod, [(x, D, 0)], [sc1, sh1], [norm_mix_pre], [(D, bf16)])
    proj = mm(u1.reshape(T, D), wp, "nn", bf16, "proj", tn=2816, b_cols=(0, CB_BA * LANE // 2816)).reshape(B, S, CB_BA * LANE)
    ba = mm(u1.reshape(T, D), wp, "nn", f32, "proj_ba", tn=LANE, b_cols=(CB_BA, 1)).reshape(B, S, LANE)
    ya, *got = attn_fwd(proj, bias, sinks, _Comm(shard(mid), two_level=True))
    gw.update(zip(mid, got))
    wa = _cols_gathered(gw["w_attn_branch"])
    wa = jnp.pad(wa.reshape(HQ, HD, D), ((0, 0), (0, LANE - HD), (0, 0))).reshape(HQ * LANE, D)
    wd = _cols_gathered(gw["w_dn_branch"])
    wo = gw["w_out"].reshape(D, D)
    qkvn = dnconv_fwd(proj, conv_dn)
    (bg,) = rowcall_fwd("dn_gate", f_gate, [(ba, LANE, 0)], [], [a_log_pad, dt_bias_pad], [(LANE, f32)])
    o_dn, states, *got = delta_fwd(qkvn, bg, _Comm(shard(late), two_level=True))
    gw.update(zip(late, got))
    wup = _cols_gathered(gw["ffn_w_up"])
    conv_ffn = _cols_gathered(gw["ffn_conv_w"]).astype(f32)
    wdown = gw["ffn_w_down"].reshape(DFF, D)
    (yd,) = rowcall_fwd("dn_out", f_dnout, [(o_dn, DNH * DND, 0), (proj, DNH * DND, CB_DZ // 4)], [], [dn_norm_w], [(DNH * DND, bf16)])
    pa = mm(ya.reshape(T, HQ * LANE), wa, "nn", bf16, "attn_branch").reshape(B, S, D)
    pd = mm(yd.reshape(T, DNH * DND), wd, "nn", bf16, "dn_branch").reshape(B, S, D)
    merge_tok = [(proj, D, CB_GA // 8), (proj, D, CB_GD // 8), (pa, D, 0), (pd, D, 0)]
    (merged,) = rowcall_fwd("merge", f_merge, merge_tok, [], [], [(D, bf16)])
    y1 = mm(merged.reshape(T, D), wo, "nn", bf16, "mix_out").reshape(B, S, D)
    (h1,) = rowcall_fwd("mix_post", f_resid, [(x, D, 0), (y1, D, 0)], [g1], [norm_mix_post], [(D, f32)])
    (u2,) = rowcall_fwd("ffn_pre", f_rms_mod, [(h1, D, 0)], [sc2, sh2], [norm_ffn_pre], [(D, bf16)])
    up = mm(u2.reshape(T, D), wup, "nn", bf16, "ffn_up", tn=2816).reshape(B, S, 2 * DFF)
    act = ffnconv_fwd(up, conv_ffn)
    y2 = mm(act.reshape(T, DFF), wdown, "nn", bf16, "ffn_down", tk=2816).reshape(B, S, D)

    dh1_a, dy2, dg2, dw_ffn_post, loss_b = loss_head(h1, y2, loss_target, g2, norm_ffn_post)
    dy2f = dy2.reshape(T, D)
    dact = mm(dy2f, wdown, "nt", bf16, "ffn_down_dx", tn=2816).reshape(B, S, DFF)
    g_wdown = mm(act.reshape(T, DFF), dy2f, "tn", f32, "ffn_down_dw", tm=2816, tn=512, tk=4096)
    dup, g_conv_ffn = ffnconv_bwd(up, conv_ffn, dact)
    dupf = dup.reshape(2, T, DFF)
    g_conv_ffn = g_conv_ffn.transpose(1, 0, 2).reshape(FK, 2 * DFF)
    du2 = mm(dupf, wup, "nt", bf16, "ffn_up_dx", tk=2816).reshape(B, S, D)
    g_wup = mm(u2.reshape(T, D), dupf, "tn", f32, "ffn_up_dw", tm=512, tn=2816, tk=2048)
    dh1, dsc2, dsh2, dw_ffn_pre = rowcall_bwd("ffn_pre_bwd", f_rms_mod, [(h1, D, 0)], [sc2, sh2], [norm_ffn_pre], [(du2, D, 0)],
                                              [(0, f32)], add=(dh1_a, D, 0))
    dy1, dg1, dw_mix_post = rowcall_bwd("mix_post_bwd", f_resid, [(x, D, 0), (y1, D, 0)], [g1], [norm_mix_post], [(dh1, D, 0)],
                                        [(1, bf16)])
    dy1f = dy1.reshape(T, D)
    dmerged = mm(dy1f, wo, "nt", bf16, "mix_out_dx").reshape(B, S, D)
    g_wo = mm(merged.reshape(T, D), dy1f, "tn", f32, "mix_out_dw", tk=2048)
    dga, dgd, dpa, dpd = rowcall_bwd("merge_bwd", f_merge, merge_tok, [], [], [(dmerged, D, 0)],
                                     [(0, bf16), (1, bf16), (2, bf16), (3, bf16)])
    dpaf, dpdf = dpa.reshape(T, D), dpd.reshape(T, D)
    dya = mm(dpaf, wa, "nt", bf16, "attn_branch_dx").reshape(B, S, HQ * LANE)
    g_wa = mm(ya.reshape(T, HQ * LANE), dpaf, "tn", f32, "attn_branch_dw", tk=2048)
    dyd = mm(dpdf, wd, "nt", bf16, "dn_branch_dx").reshape(B, S, DNH * DND)
    g_wd = mm(yd.reshape(T, DNH * DND), dpdf, "tn", f32, "dn_branch_dw", tk=2048)
    do_dn, dz, dw_dn_norm = rowcall_bwd("dn_out_bwd", f_dnout, [(o_dn, DNH * DND, 0), (proj, DNH * DND, CB_DZ // 4)], [], [dn_norm_w],
                                        [(dyd, DNH * DND, 0)], [(0, f32), (1, bf16)])
    parts = {}
    outbox = lambda d: _Comm([d[n].astype(bf16) for n in d], scatter=True)
    send = dict(ffn_w_up=_cols_split(g_wup), ffn_conv_w=_cols_split(g_conv_ffn),
                ffn_w_down=g_wdown.reshape(NDEV, DFF // NDEV, D))
    dqkvn, dbg, *got = delta_bwd(qkvn, bg, states, do_dn, outbox(send))
    parts.update(zip(send, got))
    dba, da_log_pad, ddt_bias_pad = rowcall_bwd("dn_gate_bwd", f_gate, [(ba, LANE, 0)], [], [a_log_pad, dt_bias_pad],
                                                [(dbg, LANE, 0)], [(0, bf16)])
    ddqkv, g_conv_dn = dnconv_bwd(proj, conv_dn, dqkvn)
    send = dict(w_attn_branch=_cols_split(g_wa.reshape(HQ, LANE, D)[:, :HD].reshape(HQ * HD, D)), w_dn_branch=_cols_split(g_wd),
                w_out=g_wo.reshape(NDEV, D // NDEV, D))
    dq, dk, dv, dbias, dsinks, *got = attn_bwd(proj, bias, sinks, dya, outbox(send))
    parts.update(zip(send, got))
    dproj = jnp.concatenate([dga, dgd, dq, ddqkv, dz, dk, dv, dba], axis=2).reshape(T, NP)
    g_wp = mm(u1.reshape(T, D), dproj, "tn", f32, "proj_dw", tn=1920, tk=2048)
    send = dict(w_in=_cols_split(_unpack_w_in(g_wp)), dn_conv_w=_cols_split(g_conv_dn))
    du1, *got = mm(dproj, wp, "nt", bf16, "proj_dx", tm=512, tk=5760, comm=outbox(send))
    parts.update(zip(send, got))
    du1 = du1.reshape(B, S, D)
    grad_x, dsc1, dsh1, dw_mix_pre = rowcall_bwd("mix_pre_bwd", f_rms_mod, [(x, D, 0)], [sc1, sh1], [norm_mix_pre], [(du1, D, 0)],
                                                 [(0, f32)], add=(dh1, D, 0))
    g_rel = mm(dbias.reshape(HQ, WIN * 2 * WIN), onehot, "nt", f32, "rel_bias_dw", tk=8192, precision=HI)

    dmod = jnp.concatenate([dsh1, dsc1, dg1, dsh2, dsc2, dg2], axis=2).reshape(B, NMOD * D)

    zrow = lambda a: jnp.concatenate([a.reshape(1, -1), jnp.zeros((B - 1, a.size), f32)], axis=0)
    small_g = jnp.concatenate([
        dmod, dw_mix_pre.reshape(B, D), dw_mix_post.reshape(B, D), dw_ffn_pre.reshape(B, D), dw_ffn_post.reshape(B, D),
        da_log_pad.reshape(B, LANE)[:, DNH:2 * DNH], ddt_bias_pad.reshape(B, LANE)[:, DNH:2 * DNH], dw_dn_norm.reshape(B, DND),
        zrow(dsinks), zrow(g_rel.T), loss_b.reshape(B, LANE)[:, :1], jnp.zeros((B, SMALL_PAD - SMALL_N - 1), f32)], axis=1)
    (small_all,) = _exchange([small_g], "gather_small")
    dmod_cols = lax.dynamic_slice_in_dim(small_all.reshape(NDEV * B, SMALL_PAD), me * ncol, ncol, axis=1)
    g_ada_w = ada_bwd(c_all, dmod_cols)
    small_w = dict(ada_b=(ada_b, m_ada_b, v_ada_b), norm_mix_pre=(norm_mix_pre, m_norm_mix_pre, v_norm_mix_pre),
                   norm_mix_post=(norm_mix_post, m_norm_mix_post, v_norm_mix_post), norm_ffn_pre=(norm_ffn_pre, m_norm_ffn_pre, v_norm_ffn_pre),
                   norm_ffn_post=(norm_ffn_post, m_norm_ffn_post, v_norm_ffn_post), dn_a_log=(dn_a_log, m_dn_a_log, v_dn_a_log),
                   dn_dt_bias=(dn_dt_bias, m_dn_dt_bias, v_dn_dt_bias), dn_norm_w=(dn_norm_w, m_dn_norm_w, v_dn_norm_w),
                   attn_sinks=(attn_sinks, m_attn_sinks, v_attn_sinks), rel_bias=(rel_bias, m_rel_bias, v_rel_bias))

    def pack(i, fill):
        row = jnp.concatenate([small_w[n][i].reshape(1, -1) for n, _ in SMALL], axis=1)
        return jnp.pad(row, ((0, 0), (0, SMALL_PAD - SMALL_N)), constant_values=fill)

    small_out = adamw(pack(0, 0.0), small_all.reshape(NDEV * B, 1, SMALL_PAD), pack(1, 0.0), pack(2, 1.0), "adamw_small")
    loss = small_out[0][0, SMALL_N]

    res = {}
    off = 0
    for n, size in SMALL:
        shp = small_w[n][0].shape
        res[n] = [o[:, off:off + size].reshape(shp) for o in small_out]
        off += size
    res["ada_w"] = [o[None] for o in adamw(ada_w[0], g_ada_w[None], m_ada_w[0], v_ada_w[0], "adamw_ada_w")]
    moments = dict(w_in=(m_w_in, v_w_in), dn_conv_w=(m_dn_conv_w, v_dn_conv_w), w_attn_branch=(m_w_attn_branch, v_w_attn_branch),
                   w_dn_branch=(m_w_dn_branch, v_w_dn_branch), w_out=(m_w_out, v_w_out), ffn_w_up=(m_ffn_w_up, v_ffn_w_up),
                   ffn_conv_w=(m_ffn_conv_w, v_ffn_conv_w), ffn_w_down=(m_ffn_w_down, v_ffn_w_down))
    for n in big_names:
        res[n] = [o[None] for o in adamw(big[n][0], parts[n], moments[n][0][0], moments[n][1][0], "adamw_" + n)]

    order = ["ada_w", "ada_b", "norm_mix_pre", "norm_mix_post", "norm_ffn_pre", "norm_ffn_post", "w_in", "dn_conv_w", "dn_a_log",
             "dn_dt_bias", "dn_norm_w", "attn_sinks", "rel_bias", "w_attn_branch", "w_dn_branch", "w_out", "ffn_w_up", "ffn_conv_w",
             "ffn_w_down"]
    return (loss, grad_x, *[res[n][0] for n in order], *[res[n][1] for n in order], *[res[n][2] for n in order],
            *[res[n][3] for n in order])
```

```python
import functools
import math

import numpy as np
import jax
import jax.numpy as jnp
from jax import lax
from jax.experimental import pallas as pl
from jax.experimental.pallas import tpu as pltpu

f32 = jnp.float32
bf16 = jnp.bfloat16
HI = lax.Precision.HIGHEST
MID = lax.Precision.HIGH
MESH = pl.DeviceIdType.MESH

NDEV = 8
D = 1024
HQ, HKV, HD, WIN, NBUCK, MAXDIST = 8, 2, 64, 128, 32, 128
DNH, DND, DNK, CH = 4, 128, 4, 64
DFF, FK = 2816, 3
NMOD = 6
RMS_EPS = 1e-6
L2_EPS = 1e-6
NEG_INF = -1e30
LR, B1, B2, EPS, WD, STEP = 0.001, 0.9, 0.999, 1e-08, 0.01, 10

LANE = 128
CB_GA, CB_GD, CB_AQ, CB_DQKV, CB_DZ, CB_AK, CB_AV, CB_BA, NPB = 0, 8, 16, 24, 36, 40, 42, 44, 45
NP = NPB * LANE
IN_SPLITS = (HQ * HD, HKV * HD, HKV * HD, 3 * DNH * DND, DNH * DND, DNH, DNH, D, D)
IN_DIM = sum(IN_SPLITS)
VMEM_LIMIT = 56 * 1024 * 1024

SMALL = (("ada_b", NMOD * D), ("norm_mix_pre", D), ("norm_mix_post", D), ("norm_ffn_pre", D), ("norm_ffn_post", D),
         ("dn_a_log", DNH), ("dn_dt_bias", DNH), ("dn_norm_w", DND), ("attn_sinks", HQ), ("rel_bias", NBUCK * HQ))
SMALL_N = sum(n for _, n in SMALL)
SMALL_PAD = 10752


def _cp(sem):
    return pltpu.CompilerParams(dimension_semantics=sem, vmem_limit_bytes=VMEM_LIMIT)


def _pick(dim, target):
    if dim <= target:
        return dim
    best = None
    for d in range(LANE, target + 1, LANE):
        if dim % d == 0:
            best = d
    assert best is not None, (dim, target)
    return best


def _me():
    x, y, c = lax.axis_index("x"), lax.axis_index("y"), lax.axis_index("c")
    return x, y, c, 4 * x + 2 * y + c


def _peer(x, y, c, k):
    px = 1 - x if k & 4 else x
    py = 1 - y if k & 2 else y
    pc = 1 - c if k & 1 else c
    return (px, py, pc), 4 * px + 2 * py + pc


class _Comm:
    def __init__(self, arrs, scatter=False, two_level=False):
        assert not (scatter and two_level)
        self.arrs, self.n, self.scatter, self.two_level = list(arrs), len(arrs), scatter, two_level
        if scatter:
            self.out_shape = [jax.ShapeDtypeStruct(a.shape, a.dtype) for a in arrs]
        else:
            self.out_shape = [jax.ShapeDtypeStruct((NDEV,) + a.shape, a.dtype) for a in arrs]
        nsem = self.n * (NDEV - 1)
        self.scratch = [pltpu.SemaphoreType.DMA((nsem,)), pltpu.SemaphoreType.DMA((nsem,)), pltpu.SemaphoreType.DMA((self.n,))]
        self.specs = [pl.BlockSpec(memory_space=pl.ANY)] * self.n

    def phases(self, ins, out, send, recv, loc):
        x, y, c, me = _me()

        def remote(a, k, src, dst, to):
            s = a * (NDEV - 1) + k - 1
            return pltpu.make_async_remote_copy(src_ref=src, dst_ref=dst, send_sem=send.at[s], recv_sem=recv.at[s],
                                                device_id=to, device_id_type=MESH)

        def local(a):
            return pltpu.make_async_copy(ins[a].at[me] if self.scatter else ins[a], out[a].at[me], loc.at[a])

        if not self.two_level:
            def mine(a, k):
                peer, pid = _peer(x, y, c, k)
                return remote(a, k, ins[a].at[pid] if self.scatter else ins[a], out[a].at[me], peer)

            def theirs(a, k):
                peer, pid = _peer(x, y, c, k)
                return remote(a, k, ins[a].at[pid] if self.scatter else ins[a], out[a].at[pid], peer)

            def start():
                for a in range(self.n):
                    local(a).start()
                    for k in range(1, NDEV):
                        mine(a, k).start()

            def forward():
                pass

            def finish():
                for a in range(self.n):
                    for k in range(1, NDEV):
                        mine(a, k).wait_send()
                    for k in range(1, NDEV):
                        theirs(a, k).wait_recv()
                    local(a).wait()

            return start, forward, finish

        sibling = (x, y, 1 - c)
        chips = [(1 - x, y), (x, 1 - y), (1 - x, 1 - y)]
        slot = lambda px, py, pc: 4 * px + 2 * py + pc

        def own(a, k, to):
            return remote(a, k, ins[a], out[a].at[me], to)

        def landed(a, k, frm):
            return remote(a, k, ins[a], out[a].at[slot(*frm)], frm)

        def passed(a, j):
            rows = out[a].at[slot(*chips[j], c)]
            return remote(a, 5 + j, rows, rows, sibling)

        def start():
            for a in range(self.n):
                local(a).start()
                own(a, 1, sibling).start()
                for j, chip in enumerate(chips):
                    own(a, 2 + j, (*chip, c)).start()

        def forward():
            for a in range(self.n):
                for j, chip in enumerate(chips):
                    landed(a, 2 + j, (*chip, c)).wait_recv()
                    passed(a, j).start()

        def finish():
            for a in range(self.n):
                landed(a, 1, sibling).wait_recv()
                for j, chip in enumerate(chips):
                    remote(a, 5 + j, ins[a], out[a].at[slot(*chip, 1 - c)], sibling).wait_recv()
                own(a, 1, sibling).wait_send()
                for j, chip in enumerate(chips):
                    own(a, 2 + j, (*chip, c)).wait_send()
                    passed(a, j).wait_send()
                local(a).wait()

        return start, forward, finish


def _ride(body, n_in, n_out, n_scr, comm, first, mid, last):
    k = comm.n

    def wrapped(*refs):
        ins, cins = refs[:n_in], refs[n_in:n_in + k]
        o0 = n_in + k
        outs, couts = refs[o0:o0 + n_out], refs[o0 + n_out:o0 + n_out + k]
        s0 = o0 + n_out + k
        scr, sems = refs[s0:s0 + n_scr], refs[s0 + n_scr:]
        start, forward, finish = comm.phases(cins, couts, *sems)
        pl.when(first())(start)
        body(*ins, *outs, *scr)
        pl.when(mid())(forward)
        pl.when(last())(finish)

    return wrapped


def _exchange(arrs, name, scatter=False, two_level=False):
    comm = _Comm(arrs, scatter, two_level)

    def body(*refs):
        start, forward, finish = comm.phases(refs[:comm.n], refs[comm.n:2 * comm.n], *refs[2 * comm.n:])
        start()
        forward()
        finish()

    return pl.pallas_call(body, name=name, out_shape=comm.out_shape, in_specs=comm.specs, out_specs=comm.specs,
                          scratch_shapes=comm.scratch, compiler_params=pltpu.CompilerParams(has_side_effects=True))(*arrs)


def mm(a, b, mode, out_dtype, name, tm=1024, tn=1024, tk=1024, precision=None, comm=None, b_cols=None):
    a_parts = a.shape[0] if a.ndim == 3 else 1
    b_parts = b.shape[0] if b.ndim == 3 else 1
    assert b_parts == 1 or mode == "tn"
    ash, bsh = (a.shape[-2], a.shape[-1] * a_parts), b.shape[-2:]
    if mode == "nn":
        (M, K), (K2, N) = ash, bsh
    elif mode == "nt":
        (M, K), (N, K2) = ash, bsh
    else:
        (K, M), (K2, N) = ash, (bsh[0], bsh[1] * b_parts)
    assert K == K2, (name, a.shape, b.shape)
    col0 = 0
    if b_cols is not None:
        assert mode in ("nn", "nt") and tn % LANE == 0
        col0, N = b_cols[0], b_cols[1] * tn
    if mode == "tn":
        tm, tn, tk = _pick(M // a_parts, tm), _pick(N // b_parts, tn), _pick(K, tk)
    else:
        tm, tn, tk = _pick(M, tm), _pick(N // b_parts, tn), _pick(K // a_parts, tk)
    nk = K // tk
    if mode == "tn" and a_parts > 1:
        per = M // tm // a_parts
        a_spec = pl.BlockSpec((None, tk, tm), lambda i, j, k: (i // per, k, i % per))
    elif mode == "tn":
        a_spec = pl.BlockSpec((tk, tm), lambda i, j, k: (k, i))
    elif a_parts > 1:
        per = nk // a_parts
        a_spec = pl.BlockSpec((None, tm, tk), lambda i, j, k: (k // per, i, k % per))
    else:
        a_spec = pl.BlockSpec((tm, tk), lambda i, j, k: (i, k))
    if mode == "nt":
        b_spec = pl.BlockSpec((tn, tk), lambda i, j, k: (col0 + j, k))
    elif b_parts > 1:
        per = N // tn // b_parts
        b_spec = pl.BlockSpec((None, tk, tn), lambda i, j, k: (j // per, k, j % per))
    else:
        b_spec = pl.BlockSpec((tk, tn), lambda i, j, k: (k, col0 + j))
    dims = {"nn": ((1,), (0,)), "nt": ((1,), (1,)), "tn": ((0,), (0,))}[mode]

    def body(a_ref, b_ref, o_ref, *scr):
        p = lax.dot_general(a_ref[...], b_ref[...], (dims, ((), ())), preferred_element_type=f32, precision=precision)
        if nk == 1:
            o_ref[...] = p.astype(o_ref.dtype)
        else:
            acc = scr[0]
            k = pl.program_id(2)

            @pl.when(k == 0)
            def _():
                acc[...] = p

            @pl.when(k > 0)
            def _():
                acc[...] += p

            @pl.when(k == nk - 1)
            def _():
                o_ref[...] = acc[...].astype(o_ref.dtype)

    grid = (M // tm, N // tn, nk)
    scratch = [pltpu.VMEM((tm, tn), f32)] if nk > 1 else []
    out_spec = pl.BlockSpec((tm, tn), lambda i, j, k: (i, j))
    out_shape = jax.ShapeDtypeStruct((M, N), out_dtype)
    if comm is None:
        return pl.pallas_call(body, name=name, grid=grid, in_specs=[a_spec, b_spec], out_specs=out_spec, out_shape=out_shape,
                              scratch_shapes=scratch, compiler_params=_cp(("parallel", "parallel", "arbitrary")))(a, b)
    at = lambda pos: lambda: functools.reduce(jnp.logical_and, [pl.program_id(d) == p for d, p in enumerate(pos)])
    end = tuple(g - 1 for g in grid)
    return pl.pallas_call(
        _ride(body, 2, 1, len(scratch), comm, at((0, 0, 0)), at(end), at(end)), name=name, grid=grid,
        in_specs=[a_spec, b_spec] + comm.specs, out_specs=[out_spec] + comm.specs, out_shape=[out_shape] + comm.out_shape,
        scratch_shapes=scratch + comm.scratch, compiler_params=_cp(("arbitrary", "arbitrary", "arbitrary")),
    )(a, b, *comm.arrs)


def rowcall(name, fn, tok, bat, con, tok_out, acc_out, ts=256):
    B, S = tok[0][0].shape[:2]
    ts = min(ts, S)
    nt, nb, nc, no, na = len(tok), len(bat), len(con), len(tok_out), len(acc_out)

    def body(*refs):
        tr, br, cr = refs[:nt], refs[nt:nt + nb], refs[nt + nb:nt + nb + nc]
        orf, arf = refs[nt + nb + nc:nt + nb + nc + no], refs[nt + nb + nc + no:]
        touts, aouts = fn([r[0] for r in tr], [r[0] for r in br], [r[...] for r in cr])
        for r, v in zip(orf, touts):
            r[0] = v.astype(r.dtype)
        s = pl.program_id(1)
        for r, v in zip(arf, aouts):
            @pl.when(s == 0)
            def _(r=r):
                r[...] = jnp.zeros(r.shape, r.dtype)
            r[0] += v.astype(f32)

    in_specs = [pl.BlockSpec((1, ts, w), lambda b, s, cb=cb: (b, s, cb)) for (_, w, cb) in tok]
    in_specs += [pl.BlockSpec((1,) + a.shape[1:], lambda b, s: (b, 0, 0)) for a in bat]
    in_specs += [pl.BlockSpec(a.shape, lambda b, s, nd=a.ndim: (0,) * nd) for a in con]
    out_specs = [pl.BlockSpec((1, ts, w), lambda b, s: (b, s, 0)) for (w, _) in tok_out]
    out_specs += [pl.BlockSpec((1,) + shp, lambda b, s, nd=len(shp): (b,) + (0,) * nd) for shp in acc_out]
    out_shape = [jax.ShapeDtypeStruct((B, S, w), dt) for (w, dt) in tok_out]
    out_shape += [jax.ShapeDtypeStruct((B,) + shp, f32) for shp in acc_out]
    return pl.pallas_call(
        body, name=name, grid=(B, S // ts), in_specs=in_specs, out_specs=out_specs, out_shape=out_shape,
        compiler_params=_cp(("parallel", "arbitrary")),
    )(*[t[0] for t in tok], *bat, *con)


def rowcall_fwd(name, f, tok, bat, con, tok_out, ts=256):
    def fn(t, b, c):
        return f([v.astype(f32) for v in t], b, c), []
    return rowcall(name, fn, tok, bat, con, tok_out, [], ts)


def rowcall_bwd(name, f, tok, bat, con, cts, tok_grads, add=None, ts=256):
    nt, ncts = len(tok), len(cts)

    def fn(t, b, c):
        prim = [v.astype(f32) for v in t[:nt]]
        ct = [v.astype(f32) for v in t[nt:nt + ncts]]
        _, vjp = jax.vjp(lambda tt, bb, cc: f(tt, bb, cc), prim, b, c)
        dt, db, dc = vjp(ct)
        touts = [dt[i] for i, _ in tok_grads]
        if add is not None:
            touts[0] = touts[0] + t[nt + ncts].astype(f32)
        return touts, list(db) + list(dc)

    all_tok = list(tok) + list(cts) + ([add] if add is not None else [])
    tok_out = [(tok[i][1], dt) for i, dt in tok_grads]
    acc_out = [tuple(a.shape[1:]) for a in bat] + [tuple(a.shape) for a in con]
    return rowcall(name, fn, all_tok, bat, con, tok_out, acc_out, ts)


def _rms(y, w):
    return y * lax.rsqrt(jnp.mean(y * y, axis=-1, keepdims=True) + RMS_EPS) * w


def f_rms_mod(t, b, c):
    return [_rms(t[0], c[0]) * (1.0 + b[0]) + b[1]]


def f_resid(t, b, c):
    return [t[0] + b[0] * _rms(t[1], c[0])]


def f_merge(t, b, c):
    ga, gd, ya, yd = t
    return [jax.nn.sigmoid(ga) * ya + jax.nn.sigmoid(gd) * yd]


def f_dnout(t, b, c):
    o, z = t
    outs = []
    for h in range(DNH):
        sl = slice(h * DND, (h + 1) * DND)
        zh = z[:, sl]
        outs.append(_rms(o[:, sl], c[0]) * (zh * jax.nn.sigmoid(zh)))
    return [jnp.concatenate(outs, axis=1)]


def _softplus(x):
    return jnp.maximum(x, 0.0) + jnp.log(1.0 + jnp.exp(-jnp.abs(x)))


def f_gate(t, b, c):
    ba = t[0]
    a_log, dt_bias = c
    lane = lax.broadcasted_iota(jnp.int32, ba.shape, 1)
    beta = jax.nn.sigmoid(ba)
    g = -jnp.exp(a_log) * _softplus(ba + dt_bias)
    return [jnp.where(lane < DNH, beta, jnp.where(lane < 2 * DNH, g, 0.0))]


def _bucket_table():
    qi = np.arange(WIN)[:, None]
    kj = np.arange(2 * WIN)[None, :]
    dist = np.maximum(WIN + qi - kj, 0)
    max_exact = NBUCK // 2
    scaled = np.log(np.maximum(dist, 1).astype(np.float64) / max_exact) / math.log(MAXDIST / max_exact)
    large = np.minimum(max_exact + (scaled * (NBUCK - max_exact)).astype(np.int32), NBUCK - 1)
    return np.where(dist < max_exact, dist, large).astype(np.int32)


def _attn_mask(n):
    qi = lax.broadcasted_iota(jnp.int32, (WIN, 2 * WIN), 0)
    kj = lax.broadcasted_iota(jnp.int32, (WIN, 2 * WIN), 1)
    dist = WIN + qi - kj
    return (dist >= 0) & (dist < WIN) & ((kj >= WIN) | (n > 0))


def _attn_block(q, kp, kc, vp, vc, bias, sinks, mask, differentiated):
    dot = _bdot_bf16_vjp if differentiated else _bdot_bf16
    grp = HQ // HKV
    band = lambda p, c, j: jnp.concatenate([p[:, j * LANE:(j + 1) * LANE], c[:, j * LANE:(j + 1) * LANE]], axis=0)
    qh = _stack([q[:, h * LANE:(h + 1) * LANE] for h in range(HQ)])
    kb = _stack([band(kp, kc, h // grp) for h in range(HQ)])
    vb = _stack([band(vp, vc, h // grp) for h in range(HQ)])
    s = dot(qh, kb, 2, 2) * (HD ** -0.5)
    s = jnp.where(mask[None], s + bias, NEG_INF)
    m = jnp.maximum(jnp.max(s, axis=-1, keepdims=True), sinks)
    p = jnp.exp(s - m)
    probs = p / (jnp.sum(p, axis=-1, keepdims=True) + jnp.exp(sinks - m))
    o = dot(probs, vb, 2, 1)
    return jnp.concatenate([o[h] for h in range(HQ)], axis=1)


def _attn_specs(NB):
    last = NB - 1
    return [
        pl.BlockSpec((1, WIN, HQ * LANE), lambda b, n: (b, jnp.minimum(n, last), CB_AQ // 8)),
        pl.BlockSpec((1, WIN, HKV * LANE), lambda b, n: (b, jnp.clip(n - 1, 0, last), CB_AK // 2)),
        pl.BlockSpec((1, WIN, HKV * LANE), lambda b, n: (b, jnp.minimum(n, last), CB_AK // 2)),
        pl.BlockSpec((1, WIN, HKV * LANE), lambda b, n: (b, jnp.clip(n - 1, 0, last), CB_AV // 2)),
        pl.BlockSpec((1, WIN, HKV * LANE), lambda b, n: (b, jnp.minimum(n, last), CB_AV // 2)),
        pl.BlockSpec((HQ, WIN, 2 * WIN), lambda b, n: (0, 0, 0)),
        pl.BlockSpec((HQ, 1, 1), lambda b, n: (0, 0, 0)),
    ]


def attn_fwd(proj, bias, sinks, comm):
    B, S, _ = proj.shape
    NB = S // WIN

    def body(q, kp, kc, vp, vc, bias_ref, sink_ref, o_ref):
        mask = _attn_mask(pl.program_id(1))
        o = _attn_block(*[r[0].astype(f32) for r in (q, kp, kc, vp, vc)], bias_ref[...], sink_ref[...], mask, False)
        o_ref[0] = o.astype(o_ref.dtype)

    at = lambda b, n: lambda: (pl.program_id(0) == b) & (pl.program_id(1) == n)
    return pl.pallas_call(
        _ride(body, 7, 1, 0, comm, at(0, 0), at(B - 1, (3 * NB) // 4), at(B - 1, NB - 1)), name="attn_fwd", grid=(B, NB),
        in_specs=_attn_specs(NB) + comm.specs,
        out_specs=[pl.BlockSpec((1, WIN, HQ * LANE), lambda b, n: (b, n, 0))] + comm.specs,
        out_shape=[jax.ShapeDtypeStruct((B, S, HQ * LANE), bf16)] + comm.out_shape, scratch_shapes=comm.scratch,
        compiler_params=_cp(("arbitrary", "arbitrary")),
    )(proj, proj, proj, proj, proj, bias, sinks, *comm.arrs)


def attn_bwd(proj, bias, sinks, dy, comm):
    B, S, _ = proj.shape
    NB = S // WIN
    last = NB - 1

    def body(q, kp, kc, vp, vc, bias_ref, sink_ref, dy_ref, dq_ref, dk_ref, dv_ref, dbias_ref, dsink_ref, kcar, vcar):
        b, n = pl.program_id(0), pl.program_id(1)

        @pl.when((b == 0) & (n == 0))
        def _():
            dbias_ref[...] = jnp.zeros(dbias_ref.shape, f32)
            dsink_ref[...] = jnp.zeros(dsink_ref.shape, f32)

        @pl.when(n == 0)
        def _():
            kcar[...] = jnp.zeros(kcar.shape, f32)
            vcar[...] = jnp.zeros(vcar.shape, f32)

        @pl.when(n < NB)
        def _():
            mask = _attn_mask(n)
            _, vjp = jax.vjp(lambda *a: _attn_block(*a, mask, True), *[r[0].astype(f32) for r in (q, kp, kc, vp, vc)],
                             bias_ref[...], sink_ref[...])
            dq, dkp, dkc, dvp, dvc, dbias, dsink = vjp(dy_ref[0].astype(f32))
            dq_ref[0] = dq.astype(dq_ref.dtype)
            dbias_ref[...] += dbias
            dsink_ref[...] += dsink
            dk_ref[0] = (kcar[...] + dkp).astype(dk_ref.dtype)
            dv_ref[0] = (vcar[...] + dvp).astype(dv_ref.dtype)
            kcar[...] = dkc
            vcar[...] = dvc

        @pl.when(n == NB)
        def _():
            dk_ref[0] = kcar[...].astype(dk_ref.dtype)
            dv_ref[0] = vcar[...].astype(dv_ref.dtype)

    in_specs = _attn_specs(NB) + [pl.BlockSpec((1, WIN, HQ * LANE), lambda b, n: (b, jnp.minimum(n, last), 0))]
    kv_out = pl.BlockSpec((1, WIN, HKV * LANE), lambda b, n: (b, jnp.maximum(n - 1, 0), 0))
    at = lambda b, n: lambda: (pl.program_id(0) == b) & (pl.program_id(1) == n)
    return pl.pallas_call(
        _ride(body, 8, 5, 2, comm, at(0, 0), at(B - 1, NB), at(B - 1, NB)), name="attn_bwd", grid=(B, NB + 1),
        in_specs=in_specs + comm.specs,
        out_specs=[pl.BlockSpec((1, WIN, HQ * LANE), lambda b, n: (b, jnp.minimum(n, last), 0)), kv_out, kv_out,
                   pl.BlockSpec((HQ, WIN, 2 * WIN), lambda b, n: (0, 0, 0)), pl.BlockSpec((HQ, 1, 1), lambda b, n: (0, 0, 0))] + comm.specs,
        out_shape=[jax.ShapeDtypeStruct((B, S, HQ * LANE), bf16), jax.ShapeDtypeStruct((B, S, HKV * LANE), bf16),
                   jax.ShapeDtypeStruct((B, S, HKV * LANE), bf16), jax.ShapeDtypeStruct((HQ, WIN, 2 * WIN), f32),
                   jax.ShapeDtypeStruct((HQ, 1, 1), f32)] + comm.out_shape,
        scratch_shapes=[pltpu.VMEM((WIN, HKV * LANE), f32), pltpu.VMEM((WIN, HKV * LANE), f32)] + comm.scratch,
        compiler_params=_cp(("arbitrary", "arbitrary")),
    )(proj, proj, proj, proj, proj, bias, sinks, dy, *comm.arrs)


DN_ROWS, FFN_ROWS = 256, 32


def _stage_rows(dst, value):
    dst[0:8] = jnp.zeros((8, LANE), f32)
    dst[8:8 + value.shape[0]] = value


def _conv_rows(xs, w, width, r, rows):
    wins = [xs[pl.ds(r + 8 - (width - 1) + j, rows), :] for j in range(width)]
    out = w[0:1] * wins[0]
    for j in range(1, width):
        out = out + w[j:j + 1] * wins[j]
    return out, wins


def _fold8(v):
    return jnp.sum(v.reshape(v.shape[0] // 8, 8, LANE), axis=0)


def _conv_rows_t(ds, w, width, r, rows):
    out = w[0:1] * ds[pl.ds(r + width - 1, rows), :]
    for j in range(1, width):
        out = out + w[j:j + 1] * ds[pl.ds(r + width - 1 - j, rows), :]
    return out


def _dn_outblk(i):
    return (i % DNH) * 3 + i // DNH


def _dn_act(c, isqk):
    sg = jax.nn.sigmoid(c)
    y = c * sg
    n = lax.rsqrt(jnp.sum(y * y, axis=-1, keepdims=True) + L2_EPS)
    return jnp.where(isqk, y * n, y), sg, n


def dnconv_fwd(proj, conv_w):
    B, S, _ = proj.shape
    rows = min(DN_ROWS, S)

    def body(x_ref, w_ref, o_ref, xs):
        isqk = pl.program_id(0) < 2 * DNH
        _stage_rows(xs, x_ref[0].astype(f32))
        w = w_ref[...]
        for r in range(0, S, rows):
            c, _ = _conv_rows(xs, w, DNK, r, rows)
            o_ref[0, pl.ds(r, rows), :] = _dn_act(c, isqk)[0]

    return pl.pallas_call(
        body, name="dnconv_fwd", grid=(3 * DNH, B),
        in_specs=[pl.BlockSpec((1, S, LANE), lambda i, b: (b, 0, CB_DQKV + i)), pl.BlockSpec((DNK, LANE), lambda i, b: (0, i))],
        out_specs=pl.BlockSpec((1, S, LANE), lambda i, b: (b, 0, _dn_outblk(i))),
        out_shape=jax.ShapeDtypeStruct((B, S, 3 * DNH * DND), f32), scratch_shapes=[pltpu.VMEM((S + 8, LANE), f32)],
        compiler_params=_cp(("parallel", "parallel")),
    )(proj, conv_w)


def dnconv_bwd(proj, conv_w, dqkvn):
    B, S, _ = proj.shape
    rows = min(DN_ROWS, S)

    def body(x_ref, w_ref, dy_ref, dx_ref, dw_ref, xs, ds):
        isqk = pl.program_id(0) < 2 * DNH
        _stage_rows(xs, x_ref[0].astype(f32))
        w = w_ref[...]
        dw = [jnp.zeros((8, LANE), f32) for _ in range(DNK)]
        for r in range(0, S, rows):
            c, wins = _conv_rows(xs, w, DNK, r, rows)
            out, sg, n = _dn_act(c, isqk)
            dout = dy_ref[0, pl.ds(r, rows), :]
            dy = jnp.where(isqk, n * (dout - out * jnp.sum(dout * out, axis=-1, keepdims=True)), dout)
            dc = dy * (sg * (1.0 + c * (1.0 - sg)))
            ds[pl.ds(r, rows), :] = dc
            for j in range(DNK):
                dw[j] = dw[j] + _fold8(dc * wins[j])
        ds[S:S + 8] = jnp.zeros((8, LANE), f32)
        for r in range(0, S, rows):
            dx_ref[0, pl.ds(r, rows), :] = _conv_rows_t(ds, w, DNK, r, rows).astype(dx_ref.dtype)

        @pl.when(pl.program_id(1) == 0)
        def _():
            dw_ref[...] = jnp.zeros(dw_ref.shape, f32)
        dw_ref[...] += jnp.concatenate([jnp.sum(d, axis=0, keepdims=True) for d in dw], axis=0)

    return pl.pallas_call(
        body, name="dnconv_bwd", grid=(3 * DNH, B),
        in_specs=[pl.BlockSpec((1, S, LANE), lambda i, b: (b, 0, CB_DQKV + i)), pl.BlockSpec((DNK, LANE), lambda i, b: (0, i)),
                  pl.BlockSpec((1, S, LANE), lambda i, b: (b, 0, _dn_outblk(i)))],
        out_specs=[pl.BlockSpec((1, S, LANE), lambda i, b: (b, 0, i)), pl.BlockSpec((DNK, LANE), lambda i, b: (0, i))],
        out_shape=[jax.ShapeDtypeStruct((B, S, 3 * DNH * DND), bf16), jax.ShapeDtypeStruct((DNK, 3 * DNH * DND), f32)],
        scratch_shapes=[pltpu.VMEM((S + 8, LANE), f32), pltpu.VMEM((S + 8, LANE), f32)],
        compiler_params=_cp(("parallel", "arbitrary")),
    )(proj, conv_w, dqkvn)


def _bdot(a, b, ca, cb, precision=HI):
    return lax.dot_general(a, b, (((ca,), (cb,)), ((0,), (0,))), preferred_element_type=f32, precision=precision)


def _bdot_bf16(a, b, ca, cb):
    return _bdot(a.astype(bf16), b.astype(bf16), ca, cb, None)


@functools.partial(jax.custom_vjp, nondiff_argnums=(2, 3))
def _bdot_bf16_vjp(a, b, ca, cb):
    return _bdot_bf16(a, b, ca, cb)


def _bdot_bf16_fwd(a, b, ca, cb):
    return _bdot_bf16(a, b, ca, cb), (a, b)


def _bdot_bf16_bwd(ca, cb, res, g):
    a, b = res
    fa, fb = 3 - ca, 3 - cb
    da = _bdot_bf16(g, b, 2, fb) if ca == 2 else _bdot_bf16(b, g, fb, 2)
    db = _bdot_bf16(a, g, fa, 1) if cb == 1 else _bdot_bf16(g, a, 1, fa)
    return da, db


_bdot_bf16_vjp.defvjp(_bdot_bf16_fwd, _bdot_bf16_bwd)


def _neumann_inverse(low):
    n = low.shape[-1]
    eye = (lax.broadcasted_iota(jnp.int32, (n, n), 0) == lax.broadcasted_iota(jnp.int32, (n, n), 1)).astype(f32)
    p = -low
    x = eye[None] + p
    for _ in range(5):
        p = _bdot(p, p, 2, 1, MID)
        x = x + _bdot(x, p, 2, 1, MID)
    return x


@jax.custom_vjp
def _unit_lower_inverse(low):
    return _neumann_inverse(low)


def _uli_fwd(low):
    t = _neumann_inverse(low)
    return t, t


def _uli_bwd(t, dt):
    return (-_bdot(_bdot(t, dt, 1, 1, MID), t, 2, 2, MID),)


_unit_lower_inverse.defvjp(_uli_fwd, _uli_bwd)


def _stack(xs):
    return jnp.concatenate([x[None] for x in xs], axis=0)


def _delta_chunk(qkv, bg, state, differentiated):
    inverse = _unit_lower_inverse if differentiated else _neumann_inverse
    lo = _bdot_bf16_vjp if differentiated else _bdot_bf16
    B = qkv.shape[0]
    G = B * DNH
    pairs = [(b, h) for b in range(B) for h in range(DNH)]
    col = lambda b, h, kind: qkv[b, :, (3 * h + kind) * DND:(3 * h + kind + 1) * DND]
    q, k, v = [_stack([col(b, h, kind) for b, h in pairs]) for kind in range(3)]
    lane = lax.broadcasted_iota(jnp.int32, (CH, LANE), 1)
    pick = lambda b, l: jnp.sum(jnp.where(lane == l, bg[b], 0.0), axis=1, keepdims=True)
    beta = _stack([pick(b, h) for b, h in pairs])
    g = _stack([pick(b, h + DNH) for b, h in pairs])
    ri = lax.broadcasted_iota(jnp.int32, (CH, CH), 0)
    ci = lax.broadcasted_iota(jnp.int32, (CH, CH), 1)
    incl, strict = (ri >= ci)[None], (ri > ci)[None]
    gc = _bdot(jnp.broadcast_to(incl.astype(f32), (G, CH, CH)), jnp.broadcast_to(g, (G, CH, LANE)), 2, 1)
    e0 = jnp.broadcast_to((lane == 0).astype(f32)[None], (G, CH, LANE))
    gc_row = _bdot(e0, gc, 2, 2)
    diff = gc[:, :, :CH] - gc_row
    decay = jnp.where(incl, jnp.exp(jnp.where(incl, diff, 0.0)), 0.0)
    qs = q * (DND ** -0.5)
    kb, vb = k * beta, v * beta
    eg = jnp.exp(gc)
    low = jnp.where(strict, lo(kb, k, 2, 2) * decay, 0.0)
    tinv = inverse(low)
    u = _bdot(tinv, vb, 2, 1, MID)
    w = _bdot(tinv, kb * eg, 2, 1, MID)
    intra = jnp.where(incl, lo(qs, k, 2, 2) * decay, 0.0)
    gl = gc[:, CH - 1:CH, :]
    k_tail = k * jnp.exp(gl - gc)
    v_new = u - lo(w, state, 2, 1)
    o = lo(qs * eg, state, 2, 1) + lo(intra, v_new, 2, 1)
    new_state = state * jnp.exp(gl) + lo(k_tail, v_new, 1, 1)
    return o, new_state


def delta_fwd(qkvn, bg, comm):
    B, S, _ = qkvn.shape
    NC, G = S // CH, B * DNH

    def body(qkv_ref, bg_ref, o_ref, st_ref, state):
        @pl.when(pl.program_id(0) == 0)
        def _():
            state[...] = jnp.zeros(state.shape, f32)
        s0 = state[...]
        st_ref[0] = s0
        o, s1 = _delta_chunk(qkv_ref[...], bg_ref[...], s0, False)
        for b in range(B):
            for h in range(DNH):
                o_ref[b, :, h * DND:(h + 1) * DND] = o[b * DNH + h]
        state[...] = s1

    at = lambda c: lambda: pl.program_id(0) == c
    return pl.pallas_call(
        _ride(body, 2, 2, 1, comm, at(0), at((7 * NC) // 8), at(NC - 1)), name="delta_fwd", grid=(NC,),
        in_specs=[pl.BlockSpec((B, CH, 3 * DNH * DND), lambda c: (0, c, 0)), pl.BlockSpec((B, CH, LANE), lambda c: (0, c, 0))] + comm.specs,
        out_specs=[pl.BlockSpec((B, CH, DNH * DND), lambda c: (0, c, 0)), pl.BlockSpec((1, G, DND, DND), lambda c: (c, 0, 0, 0))] + comm.specs,
        out_shape=[jax.ShapeDtypeStruct((B, S, DNH * DND), f32), jax.ShapeDtypeStruct((NC, G, DND, DND), f32)] + comm.out_shape,
        scratch_shapes=[pltpu.VMEM((G, DND, DND), f32)] + comm.scratch, compiler_params=_cp(("arbitrary",)),
    )(qkvn, bg, *comm.arrs)


def delta_bwd(qkvn, bg, states, do, comm):
    B, S, _ = qkvn.shape
    NC, G = S // CH, B * DNH

    def body(qkv_ref, bg_ref, st_ref, do_ref, dqkv_ref, dbg_ref, dstate):
        @pl.when(pl.program_id(0) == 0)
        def _():
            dstate[...] = jnp.zeros(dstate.shape, f32)
        _, vjp = jax.vjp(lambda a, g, s: _delta_chunk(a, g, s, True), qkv_ref[...], bg_ref[...], st_ref[0])
        do = _stack([do_ref[b, :, h * DND:(h + 1) * DND] for b in range(B) for h in range(DNH)])
        dqkv, dbg, ds = vjp((do, dstate[...]))
        dqkv_ref[...] = dqkv
        dbg_ref[...] = dbg
        dstate[...] = ds

    rev = lambda c: NC - 1 - c
    at = lambda c: lambda: pl.program_id(0) == c
    return pl.pallas_call(
        _ride(body, 4, 2, 1, comm, at(0), at(NC - 1), at(NC - 1)), name="delta_bwd", grid=(NC,),
        in_specs=[pl.BlockSpec((B, CH, 3 * DNH * DND), lambda c: (0, rev(c), 0)), pl.BlockSpec((B, CH, LANE), lambda c: (0, rev(c), 0)),
                  pl.BlockSpec((1, G, DND, DND), lambda c: (rev(c), 0, 0, 0)),
                  pl.BlockSpec((B, CH, DNH * DND), lambda c: (0, rev(c), 0))] + comm.specs,
        out_specs=[pl.BlockSpec((B, CH, 3 * DNH * DND), lambda c: (0, rev(c), 0)),
                   pl.BlockSpec((B, CH, LANE), lambda c: (0, rev(c), 0))] + comm.specs,
        out_shape=[jax.ShapeDtypeStruct((B, S, 3 * DNH * DND), f32), jax.ShapeDtypeStruct((B, S, LANE), f32)] + comm.out_shape,
        scratch_shapes=[pltpu.VMEM((G, DND, DND), f32)] + comm.scratch, compiler_params=_cp(("arbitrary",)),
    )(qkvn, bg, states, do, *comm.arrs)


GELU_C0, GELU_C1 = math.sqrt(2.0 / math.pi), 0.044715


def _ffn_specs(S):
    nblk = DFF // LANE
    return [pl.BlockSpec((1, S, LANE), lambda i, b: (b, 0, i)), pl.BlockSpec((1, S, LANE), lambda i, b: (b, 0, nblk + i)),
            pl.BlockSpec((FK, LANE), lambda i, b: (0, i)), pl.BlockSpec((FK, LANE), lambda i, b: (0, nblk + i))]


def ffnconv_fwd(up, conv_w):
    B, S, _ = up.shape
    rows = min(FFN_ROWS, S)

    def body(g_ref, v_ref, gw_ref, vw_ref, o_ref, xg, xv):
        _stage_rows(xg, g_ref[0].astype(f32))
        _stage_rows(xv, v_ref[0].astype(f32))
        gw, vw = gw_ref[...], vw_ref[...]
        for r in range(0, S, rows):
            g, _ = _conv_rows(xg, gw, FK, r, rows)
            v, _ = _conv_rows(xv, vw, FK, r, rows)
            t = jnp.tanh(GELU_C0 * (g * (1.0 + GELU_C1 * (g * g))))
            o_ref[0, pl.ds(r, rows), :] = (0.5 * g * (1.0 + t) * v).astype(o_ref.dtype)

    return pl.pallas_call(
        body, name="ffnconv_fwd", grid=(DFF // LANE, B), in_specs=_ffn_specs(S),
        out_specs=pl.BlockSpec((1, S, LANE), lambda i, b: (b, 0, i)), out_shape=jax.ShapeDtypeStruct((B, S, DFF), bf16),
        scratch_shapes=[pltpu.VMEM((S + 8, LANE), f32)] * 2, compiler_params=_cp(("parallel", "parallel")),
    )(up, up, conv_w, conv_w)


def ffnconv_bwd(up, conv_w, dact):
    B, S, _ = up.shape
    rows = min(FFN_ROWS, S)

    def body(g_ref, v_ref, gw_ref, vw_ref, dy_ref, dx_ref, dw_ref, xg, xv, dg, dv):
        _stage_rows(xg, g_ref[0].astype(f32))
        _stage_rows(xv, v_ref[0].astype(f32))
        gw, vw = gw_ref[...], vw_ref[...]
        dgw = [jnp.zeros((8, LANE), f32) for _ in range(FK)]
        dvw = [jnp.zeros((8, LANE), f32) for _ in range(FK)]
        for r in range(0, S, rows):
            g, gwins = _conv_rows(xg, gw, FK, r, rows)
            v, vwins = _conv_rows(xv, vw, FK, r, rows)
            g2 = g * g
            t = jnp.tanh(GELU_C0 * (g * (1.0 + GELU_C1 * g2)))
            half = 0.5 * (1.0 + t)
            dgelu = half + (0.5 * GELU_C0) * g * (1.0 - t * t) * (1.0 + (3.0 * GELU_C1) * g2)
            dy = dy_ref[0, pl.ds(r, rows), :].astype(f32)
            dvc = dy * (g * half)
            dgc = dy * v * dgelu
            dg[pl.ds(r, rows), :] = dgc
            dv[pl.ds(r, rows), :] = dvc
            for j in range(FK):
                dgw[j] = dgw[j] + _fold8(dgc * gwins[j])
                dvw[j] = dvw[j] + _fold8(dvc * vwins[j])
        dg[S:S + 8] = jnp.zeros((8, LANE), f32)
        dv[S:S + 8] = jnp.zeros((8, LANE), f32)
        for r in range(0, S, rows):
            dx_ref[0, 0, pl.ds(r, rows), :] = _conv_rows_t(dg, gw, FK, r, rows).astype(dx_ref.dtype)
            dx_ref[1, 0, pl.ds(r, rows), :] = _conv_rows_t(dv, vw, FK, r, rows).astype(dx_ref.dtype)

        @pl.when(pl.program_id(1) == 0)
        def _():
            dw_ref[...] = jnp.zeros(dw_ref.shape, f32)
        dw_ref[0] += jnp.concatenate([jnp.sum(d, axis=0, keepdims=True) for d in dgw], axis=0)
        dw_ref[1] += jnp.concatenate([jnp.sum(d, axis=0, keepdims=True) for d in dvw], axis=0)

    return pl.pallas_call(
        body, name="ffnconv_bwd", grid=(DFF // LANE, B),
        in_specs=_ffn_specs(S) + [pl.BlockSpec((1, S, LANE), lambda i, b: (b, 0, i))],
        out_specs=[pl.BlockSpec((2, 1, S, LANE), lambda i, b: (0, b, 0, i)), pl.BlockSpec((2, FK, LANE), lambda i, b: (0, 0, i))],
        out_shape=[jax.ShapeDtypeStruct((2, B, S, DFF), bf16), jax.ShapeDtypeStruct((2, FK, DFF), f32)],
        scratch_shapes=[pltpu.VMEM((S + 8, LANE), f32)] * 4, compiler_params=_cp(("parallel", "arbitrary")),
    )(up, up, conv_w, conv_w, dact)


def ada_fwd(c_all, ada_w, ada_b):
    def body(c_ref, w_ref, b_ref, o_ref):
        c = c_ref[...]
        act = (c * jax.nn.sigmoid(c)).astype(bf16)
        o_ref[...] = jnp.dot(act, w_ref[...].astype(bf16), preferred_element_type=f32) + b_ref[...]

    return pl.pallas_call(body, name="ada_fwd", out_shape=jax.ShapeDtypeStruct((c_all.shape[0], ada_w.shape[1]), f32),
                          compiler_params=pltpu.CompilerParams(vmem_limit_bytes=VMEM_LIMIT))(c_all, ada_w, ada_b)


def ada_bwd(c_all, dmod):
    def body(c_ref, d_ref, o_ref):
        c = c_ref[...]
        act = (c * jax.nn.sigmoid(c)).astype(bf16)
        o_ref[...] = lax.dot_general(act, d_ref[...].astype(bf16), (((0,), (0,)), ((), ())), preferred_element_type=f32)

    return pl.pallas_call(body, name="ada_bwd", out_shape=jax.ShapeDtypeStruct((c_all.shape[1], dmod.shape[1]), f32),
                          compiler_params=pltpu.CompilerParams(vmem_limit_bytes=VMEM_LIMIT))(c_all, dmod)


def loss_head(h1, y2, target, g2, w):
    def fn(t, b, c):
        h, y, tg = [v.astype(f32) for v in t]

        def loss_fn(h, y, g, w):
            e = h + g * _rms(y, w) - tg
            return 0.5 * jnp.sum(jnp.mean(e * e, axis=-1))

        loss, grads = jax.value_and_grad(loss_fn, argnums=(0, 1, 2, 3))(h, y, b[0], c[0])
        return [grads[0], grads[1]], [grads[2], grads[3], jnp.full((1, LANE), loss, f32)]

    return rowcall("loss_head", fn, [(h1, D, 0), (y2, D, 0), (target, D, 0)], [g2], [w], [(D, f32), (D, bf16)],
                   [(1, D), (1, D), (1, LANE)])


def adamw(w, gparts, m, v, name):
    R, C = w.shape
    P = gparts.shape[0]
    budget = 2 * 1024 * 1024
    tr, tc = R, C
    if R * C * 4 > budget and R % 8 == 0:
        tr = max(t for t in range(8, R + 1, 8) if R % t == 0 and t * C * 4 <= budget)
    elif R * C * 4 > budget:
        tc = max(t for t in range(LANE, C + 1, LANE) if C % t == 0 and R * t * 4 <= budget)

    def body(w_ref, g_ref, m_ref, v_ref, go, do, mo, vo):
        g = g_ref[0].astype(f32)
        for p in range(1, P):
            g = g + g_ref[p].astype(f32)
        m2 = B1 * m_ref[...] + (1.0 - B1) * g
        v2 = B2 * v_ref[...] + (1.0 - B2) * jnp.square(g)
        m_hat = m2 * (1.0 / (1.0 - B1 ** STEP))
        v_hat = v2 * (1.0 / (1.0 - B2 ** STEP))
        go[...] = g
        do[...] = -LR * (m_hat / (jnp.sqrt(v_hat) + EPS) + WD * w_ref[...])
        mo[...] = m2
        vo[...] = v2

    blk = pl.BlockSpec((tr, tc), lambda i, j: (i, j))
    return pl.pallas_call(
        body, name=name, grid=(R // tr, C // tc), in_specs=[blk, pl.BlockSpec((P, tr, tc), lambda i, j: (0, i, j)), blk, blk],
        out_specs=[blk] * 4, out_shape=[jax.ShapeDtypeStruct((R, C), f32)] * 4, compiler_params=_cp(("parallel", "parallel")),
    )(w, gparts, m, v)


def _pad_heads(w, nh):
    c = w.shape[1]
    return jnp.pad(w.reshape(nh, HD, c), ((0, 0), (0, LANE - HD), (0, 0))).reshape(nh * LANE, c)


def _unpad_heads(w, nh):
    return w.reshape(nh, LANE, w.shape[1])[:, :HD].reshape(nh * HD, w.shape[1])


def _pack_w_in(wt):
    aq, ak, av, dqkv, dz, dbeta, da, ga, gd = jnp.split(wt, np.cumsum(IN_SPLITS)[:-1].tolist(), axis=0)
    ba = jnp.pad(jnp.concatenate([dbeta, da], axis=0), ((0, LANE - 2 * DNH), (0, 0)))
    return jnp.concatenate([ga, gd, _pad_heads(aq, HQ), dqkv, dz, _pad_heads(ak, HKV), _pad_heads(av, HKV), ba], axis=0)


def _unpack_w_in(p):
    row = lambda cb, n: p[cb * LANE: cb * LANE + n]
    ba = row(CB_BA, 2 * DNH)
    return jnp.concatenate([_unpad_heads(row(CB_AQ, HQ * LANE), HQ), _unpad_heads(row(CB_AK, HKV * LANE), HKV),
                            _unpad_heads(row(CB_AV, HKV * LANE), HKV), row(CB_DQKV, 3 * DNH * DND), row(CB_DZ, DNH * DND),
                            ba[:DNH], ba[DNH:], row(CB_GA, D), row(CB_GD, D)], axis=0)


def _cols_gathered(g):
    return g.transpose(1, 0, 2).reshape(g.shape[1], NDEV * g.shape[2])


def _cols_split(w):
    r = w.shape[0]
    return w.reshape(r, NDEV, w.shape[1] // NDEV).transpose(1, 0, 2)


def kernel(x, c, ada_w, ada_b, norm_mix_pre, norm_mix_post, norm_ffn_pre, norm_ffn_post, w_in, dn_conv_w, dn_a_log, dn_dt_bias, dn_norm_w, attn_sinks, rel_bias, w_attn_branch, w_dn_branch, w_out, ffn_w_up, ffn_conv_w, ffn_w_down, loss_target, m_ada_w, m_ada_b, m_norm_mix_pre, m_norm_mix_post, m_norm_ffn_pre, m_norm_ffn_post, m_w_in, m_dn_conv_w, m_dn_a_log, m_dn_dt_bias, m_dn_norm_w, m_attn_sinks, m_rel_bias, m_w_attn_branch, m_w_dn_branch, m_w_out, m_ffn_w_up, m_ffn_conv_w, m_ffn_w_down, v_ada_w, v_ada_b, v_norm_mix_pre, v_norm_mix_post, v_norm_ffn_pre, v_norm_ffn_post, v_w_in, v_dn_conv_w, v_dn_a_log, v_dn_dt_bias, v_dn_norm_w, v_attn_sinks, v_rel_bias, v_w_attn_branch, v_w_dn_branch, v_w_out, v_ffn_w_up, v_ffn_conv_w, v_ffn_w_down):
    B, S, _ = x.shape
    T = B * S
    me = 4 * lax.axis_index("x") + 2 * lax.axis_index("y") + lax.axis_index("c")
    big = dict(w_in=w_in, dn_conv_w=dn_conv_w, w_attn_branch=w_attn_branch, w_dn_branch=w_dn_branch, w_out=w_out,
               ffn_w_up=ffn_w_up, ffn_conv_w=ffn_conv_w, ffn_w_down=ffn_w_down)
    big_names = list(big)

    first, mid, late = ["w_in", "dn_conv_w"], ["w_attn_branch", "w_dn_branch", "w_out"], ["ffn_w_up", "ffn_conv_w", "ffn_w_down"]
    transposed = ("w_in", "ffn_w_up")
    local = lambda n, a: a[0].T if n in transposed else a[0]
    shard = lambda names: [local(n, big[n]).astype(bf16) for n in names]
    *got, c_all = _exchange(shard(first) + [c], "gather_w_in", two_level=True)
    gw = dict(zip(first, got))
    c_all = c_all.reshape(NDEV * B, D)

    wp = _pack_w_in(gw["w_in"].reshape(IN_DIM, D))
    conv_dn = _cols_gathered(gw["dn_conv_w"]).astype(f32)

    ncol = ada_w.shape[2]
    ada_b_mine = lax.dynamic_slice_in_dim(ada_b, me * ncol, ncol, axis=1)
    mod_cols = ada_fwd(c_all, ada_w[0], ada_b_mine)
    (mod_g,) = _exchange([mod_cols], "gather_mod")
    mod = lax.dynamic_slice_in_dim(mod_g, me * B, B, axis=1).transpose(1, 0, 2).reshape(B, NMOD * D)
    sh1, sc1, g1, sh2, sc2, g2 = [mod[:, i * D:(i + 1) * D].reshape(B, 1, D) for i in range(NMOD)]

    onehot = (jnp.asarray(_bucket_table()).reshape(1, -1) == jnp.arange(NBUCK, dtype=jnp.int32)[:, None]).astype(f32)
    bias = mm(rel_bias.T, onehot, "nn", f32, "bias_table", tn=8192, precision=HI).reshape(HQ, WIN, 2 * WIN)
    sinks = attn_sinks.reshape(HQ, 1, 1)
    a_log_pad = jnp.pad(dn_a_log, ((0, 0), (DNH, LANE - 2 * DNH)))
    dt_bias_pad = jnp.pad(dn_dt_bias, ((0, 0), (DNH, LANE - 2 * DNH)))

    (u1,) = rowcall_fwd("mix_pre", f_rms_mod, [(x, D, 0)], [sc1, sh1], [norm_mix_pre], [(D, bf16)])
    proj = mm(u1.reshape(T, D), wp, "nt", bf16, "proj", tn=2816, b_cols=(0, CB_BA * LANE // 2816)).reshape(B, S, CB_BA * LANE)
    ba = mm(u1.reshape(T, D), wp, "nt", f32, "proj_ba", tn=LANE, b_cols=(CB_BA, 1)).reshape(B, S, LANE)
    ya, *got = attn_fwd(proj, bias, sinks, _Comm(shard(mid), two_level=True))
    gw.update(zip(mid, got))
    wa = _cols_gathered(gw["w_attn_branch"])
    wa = jnp.pad(wa.reshape(HQ, HD, D), ((0, 0), (0, LANE - HD), (0, 0))).reshape(HQ * LANE, D)
    wd = _cols_gathered(gw["w_dn_branch"])
    wo = gw["w_out"].reshape(D, D)
    qkvn = dnconv_fwd(proj, conv_dn)
    (bg,) = rowcall_fwd("dn_gate", f_gate, [(ba, LANE, 0)], [], [a_log_pad, dt_bias_pad], [(LANE, f32)])
    o_dn, states, *got = delta_fwd(qkvn, bg, _Comm(shard(late), two_level=True))
    gw.update(zip(late, got))
    wup = gw["ffn_w_up"].reshape(2 * DFF, D)
    conv_ffn = _cols_gathered(gw["ffn_conv_w"]).astype(f32)
    wdown = gw["ffn_w_down"].reshape(DFF, D)
    (yd,) = rowcall_fwd("dn_out", f_dnout, [(o_dn, DNH * DND, 0), (proj, DNH * DND, CB_DZ // 4)], [], [dn_norm_w], [(DNH * DND, bf16)])
    pa = mm(ya.reshape(T, HQ * LANE), wa, "nn", bf16, "attn_branch").reshape(B, S, D)
    pd = mm(yd.reshape(T, DNH * DND), wd, "nn", bf16, "dn_branch").reshape(B, S, D)
    merge_tok = [(proj, D, CB_GA // 8), (proj, D, CB_GD // 8), (pa, D, 0), (pd, D, 0)]
    (merged,) = rowcall_fwd("merge", f_merge, merge_tok, [], [], [(D, bf16)])
    y1 = mm(merged.reshape(T, D), wo, "nn", bf16, "mix_out").reshape(B, S, D)
    (h1,) = rowcall_fwd("mix_post", f_resid, [(x, D, 0), (y1, D, 0)], [g1], [norm_mix_post], [(D, f32)])
    (u2,) = rowcall_fwd("ffn_pre", f_rms_mod, [(h1, D, 0)], [sc2, sh2], [norm_ffn_pre], [(D, bf16)])
    up = mm(u2.reshape(T, D), wup, "nt", bf16, "ffn_up", tn=2816).reshape(B, S, 2 * DFF)
    act = ffnconv_fwd(up, conv_ffn)
    y2 = mm(act.reshape(T, DFF), wdown, "nn", bf16, "ffn_down", tk=2816).reshape(B, S, D)

    dh1_a, dy2, dg2, dw_ffn_post, loss_b = loss_head(h1, y2, loss_target, g2, norm_ffn_post)
    dy2f = dy2.reshape(T, D)
    dact = mm(dy2f, wdown, "nt", bf16, "ffn_down_dx", tn=2816).reshape(B, S, DFF)
    g_wdown = mm(act.reshape(T, DFF), dy2f, "tn", f32, "ffn_down_dw", tm=2816, tn=512, tk=4096)
    dup, g_conv_ffn = ffnconv_bwd(up, conv_ffn, dact)
    dupf = dup.reshape(2, T, DFF)
    g_conv_ffn = g_conv_ffn.transpose(1, 0, 2).reshape(FK, 2 * DFF)
    du2 = mm(dupf, wup, "nn", bf16, "ffn_up_dx", tk=2816).reshape(B, S, D)
    g_wup = mm(dupf, u2.reshape(T, D), "tn", f32, "ffn_up_dw", tm=1408, tk=2048)
    dh1, dsc2, dsh2, dw_ffn_pre = rowcall_bwd("ffn_pre_bwd", f_rms_mod, [(h1, D, 0)], [sc2, sh2], [norm_ffn_pre], [(du2, D, 0)],
                                              [(0, f32)], add=(dh1_a, D, 0))
    dy1, dg1, dw_mix_post = rowcall_bwd("mix_post_bwd", f_resid, [(x, D, 0), (y1, D, 0)], [g1], [norm_mix_post], [(dh1, D, 0)],
                                        [(1, bf16)])
    dy1f = dy1.reshape(T, D)
    dmerged = mm(dy1f, wo, "nt", bf16, "mix_out_dx").reshape(B, S, D)
    g_wo = mm(merged.reshape(T, D), dy1f, "tn", f32, "mix_out_dw", tk=2048)
    dga, dgd, dpa, dpd = rowcall_bwd("merge_bwd", f_merge, merge_tok, [], [], [(dmerged, D, 0)],
                                     [(0, bf16), (1, bf16), (2, bf16), (3, bf16)])
    dpaf, dpdf = dpa.reshape(T, D), dpd.reshape(T, D)
    dya = mm(dpaf, wa, "nt", bf16, "attn_branch_dx").reshape(B, S, HQ * LANE)
    g_wa = mm(ya.reshape(T, HQ * LANE), dpaf, "tn", f32, "attn_branch_dw", tk=2048)
    dyd = mm(dpdf, wd, "nt", bf16, "dn_branch_dx").reshape(B, S, DNH * DND)
    g_wd = mm(yd.reshape(T, DNH * DND), dpdf, "tn", f32, "dn_branch_dw", tk=2048)
    do_dn, dz, dw_dn_norm = rowcall_bwd("dn_out_bwd", f_dnout, [(o_dn, DNH * DND, 0), (proj, DNH * DND, CB_DZ // 4)], [], [dn_norm_w],
                                        [(dyd, DNH * DND, 0)], [(0, f32), (1, bf16)])
    parts = {}
    outbox = lambda d: _Comm([d[n].astype(bf16) for n in d], scatter=True)
    send = dict(ffn_w_up=g_wup.reshape(NDEV, 2 * DFF // NDEV, D), ffn_conv_w=_cols_split(g_conv_ffn),
                ffn_w_down=g_wdown.reshape(NDEV, DFF // NDEV, D))
    dqkvn, dbg, *got = delta_bwd(qkvn, bg, states, do_dn, outbox(send))
    parts.update(zip(send, got))
    dba, da_log_pad, ddt_bias_pad = rowcall_bwd("dn_gate_bwd", f_gate, [(ba, LANE, 0)], [], [a_log_pad, dt_bias_pad],
                                                [(dbg, LANE, 0)], [(0, bf16)])
    ddqkv, g_conv_dn = dnconv_bwd(proj, conv_dn, dqkvn)
    send = dict(w_attn_branch=_cols_split(g_wa.reshape(HQ, LANE, D)[:, :HD].reshape(HQ * HD, D)), w_dn_branch=_cols_split(g_wd),
                w_out=g_wo.reshape(NDEV, D // NDEV, D))
    dq, dk, dv, dbias, dsinks, *got = attn_bwd(proj, bias, sinks, dya, outbox(send))
    parts.update(zip(send, got))
    dproj = jnp.concatenate([dga, dgd, dq, ddqkv, dz, dk, dv, dba], axis=2).reshape(T, NP)
    g_wp = mm(dproj, u1.reshape(T, D), "tn", f32, "proj_dw", tm=1152, tk=2048)
    send = dict(w_in=_unpack_w_in(g_wp).reshape(NDEV, IN_DIM // NDEV, D), dn_conv_w=_cols_split(g_conv_dn))
    du1, *got = mm(dproj, wp, "nn", bf16, "proj_dx", tm=512, tk=5760, comm=outbox(send))
    parts.update(zip(send, got))
    du1 = du1.reshape(B, S, D)
    grad_x, dsc1, dsh1, dw_mix_pre = rowcall_bwd("mix_pre_bwd", f_rms_mod, [(x, D, 0)], [sc1, sh1], [norm_mix_pre], [(du1, D, 0)],
                                                 [(0, f32)], add=(dh1, D, 0))
    g_rel = mm(dbias.reshape(HQ, WIN * 2 * WIN), onehot, "nt", f32, "rel_bias_dw", tk=8192, precision=HI)

    dmod = jnp.concatenate([dsh1, dsc1, dg1, dsh2, dsc2, dg2], axis=2).reshape(B, NMOD * D)

    zrow = lambda a: jnp.concatenate([a.reshape(1, -1), jnp.zeros((B - 1, a.size), f32)], axis=0)
    small_g = jnp.concatenate([
        dmod, dw_mix_pre.reshape(B, D), dw_mix_post.reshape(B, D), dw_ffn_pre.reshape(B, D), dw_ffn_post.reshape(B, D),
        da_log_pad.reshape(B, LANE)[:, DNH:2 * DNH], ddt_bias_pad.reshape(B, LANE)[:, DNH:2 * DNH], dw_dn_norm.reshape(B, DND),
        zrow(dsinks), zrow(g_rel.T), loss_b.reshape(B, LANE)[:, :1], jnp.zeros((B, SMALL_PAD - SMALL_N - 1), f32)], axis=1)
    (small_all,) = _exchange([small_g], "gather_small")
    dmod_cols = lax.dynamic_slice_in_dim(small_all.reshape(NDEV * B, SMALL_PAD), me * ncol, ncol, axis=1)
    g_ada_w = ada_bwd(c_all, dmod_cols)
    small_w = dict(ada_b=(ada_b, m_ada_b, v_ada_b), norm_mix_pre=(norm_mix_pre, m_norm_mix_pre, v_norm_mix_pre),
                   norm_mix_post=(norm_mix_post, m_norm_mix_post, v_norm_mix_post), norm_ffn_pre=(norm_ffn_pre, m_norm_ffn_pre, v_norm_ffn_pre),
                   norm_ffn_post=(norm_ffn_post, m_norm_ffn_post, v_norm_ffn_post), dn_a_log=(dn_a_log, m_dn_a_log, v_dn_a_log),
                   dn_dt_bias=(dn_dt_bias, m_dn_dt_bias, v_dn_dt_bias), dn_norm_w=(dn_norm_w, m_dn_norm_w, v_dn_norm_w),
                   attn_sinks=(attn_sinks, m_attn_sinks, v_attn_sinks), rel_bias=(rel_bias, m_rel_bias, v_rel_bias))

    def pack(i, fill):
        row = jnp.concatenate([small_w[n][i].reshape(1, -1) for n, _ in SMALL], axis=1)
        return jnp.pad(row, ((0, 0), (0, SMALL_PAD - SMALL_N)), constant_values=fill)

    small_out = adamw(pack(0, 0.0), small_all.reshape(NDEV * B, 1, SMALL_PAD), pack(1, 0.0), pack(2, 1.0), "adamw_small")
    loss = small_out[0][0, SMALL_N]

    res = {}
    off = 0
    for n, size in SMALL:
        shp = small_w[n][0].shape
        res[n] = [o[:, off:off + size].reshape(shp) for o in small_out]
        off += size
    res["ada_w"] = [o[None] for o in adamw(ada_w[0], g_ada_w[None], m_ada_w[0], v_ada_w[0], "adamw_ada_w")]
    moments = dict(w_in=(m_w_in, v_w_in), dn_conv_w=(m_dn_conv_w, v_dn_conv_w), w_attn_branch=(m_w_attn_branch, v_w_attn_branch),
                   w_dn_branch=(m_w_dn_branch, v_w_dn_branch), w_out=(m_w_out, v_w_out), ffn_w_up=(m_ffn_w_up, v_ffn_w_up),
                   ffn_conv_w=(m_ffn_conv_w, v_ffn_conv_w), ffn_w_down=(m_ffn_w_down, v_ffn_w_down))
    for n in big_names:
        outs = adamw(local(n, big[n]), parts[n], local(n, moments[n][0]), local(n, moments[n][1]), "adamw_" + n)
        res[n] = [(o.T if n in transposed else o)[None] for o in outs]

    order = ["ada_w", "ada_b", "norm_mix_pre", "norm_mix_post", "norm_ffn_pre", "norm_ffn_post", "w_in", "dn_conv_w", "dn_a_log",
             "dn_dt_bias", "dn_norm_w", "attn_sinks", "rel_bias", "w_attn_branch", "w_dn_branch", "w_out", "ffn_w_up", "ffn_conv_w",
             "ffn_w_down"]
    return (loss, grad_x, *[res[n][0] for n in order], *[res[n][1] for n in order], *[res[n][2] for n in order],
            *[res[n][3] for n in order])
```

```python
import functools
import math

import numpy as np
import jax
import jax.numpy as jnp
from jax import lax
from jax.experimental import pallas as pl
from jax.experimental.pallas import tpu as pltpu

f32 = jnp.float32
bf16 = jnp.bfloat16
HI = lax.Precision.HIGHEST
MID = lax.Precision.HIGH
MESH = pl.DeviceIdType.MESH

NDEV = 8
D = 1024
HQ, HKV, HD, WIN, NBUCK, MAXDIST = 8, 2, 64, 128, 32, 128
DNH, DND, DNK, CH = 4, 128, 4, 64
DFF, FK = 2816, 3
NMOD = 6
RMS_EPS = 1e-6
L2_EPS = 1e-6
NEG_INF = -1e30
LR, B1, B2, EPS, WD, STEP = 0.001, 0.9, 0.999, 1e-08, 0.01, 10

LANE = 128
CB_GA, CB_GD, CB_AQ, CB_DQKV, CB_DZ, CB_AK, CB_AV, CB_BA, NPB = 0, 8, 16, 20, 32, 36, 37, 38, 39
NP = NPB * LANE
IN_SPLITS = (HQ * HD, HKV * HD, HKV * HD, 3 * DNH * DND, DNH * DND, DNH, DNH, D, D)
IN_DIM = sum(IN_SPLITS)
VMEM_LIMIT = 56 * 1024 * 1024

SMALL = (("ada_b", NMOD * D), ("norm_mix_pre", D), ("norm_mix_post", D), ("norm_ffn_pre", D), ("norm_ffn_post", D),
         ("dn_a_log", DNH), ("dn_dt_bias", DNH), ("dn_norm_w", DND), ("attn_sinks", HQ), ("rel_bias", NBUCK * HQ))
SMALL_N = sum(n for _, n in SMALL)
SMALL_PAD = 10752


def _cp(sem):
    return pltpu.CompilerParams(dimension_semantics=sem, vmem_limit_bytes=VMEM_LIMIT)


def _pick(dim, target):
    if dim <= target:
        return dim
    best = None
    for d in range(LANE, target + 1, LANE):
        if dim % d == 0:
            best = d
    assert best is not None, (dim, target)
    return best


def _me():
    x, y, c = lax.axis_index("x"), lax.axis_index("y"), lax.axis_index("c")
    return x, y, c, 4 * x + 2 * y + c


def _peer(x, y, c, k):
    px = 1 - x if k & 4 else x
    py = 1 - y if k & 2 else y
    pc = 1 - c if k & 1 else c
    return (px, py, pc), 4 * px + 2 * py + pc


class _Comm:
    def __init__(self, arrs, scatter=False, two_level=False):
        assert not (scatter and two_level)
        self.arrs, self.n, self.scatter, self.two_level = list(arrs), len(arrs), scatter, two_level
        if scatter:
            self.out_shape = [jax.ShapeDtypeStruct(a.shape, a.dtype) for a in arrs]
        else:
            self.out_shape = [jax.ShapeDtypeStruct((NDEV,) + a.shape, a.dtype) for a in arrs]
        nsem = self.n * (NDEV - 1)
        self.scratch = [pltpu.SemaphoreType.DMA((nsem,)), pltpu.SemaphoreType.DMA((nsem,)), pltpu.SemaphoreType.DMA((self.n,))]
        self.specs = [pl.BlockSpec(memory_space=pl.ANY)] * self.n

    def phases(self, ins, out, send, recv, loc):
        x, y, c, me = _me()

        def remote(a, k, src, dst, to):
            s = a * (NDEV - 1) + k - 1
            return pltpu.make_async_remote_copy(src_ref=src, dst_ref=dst, send_sem=send.at[s], recv_sem=recv.at[s],
                                                device_id=to, device_id_type=MESH)

        def local(a):
            return pltpu.make_async_copy(ins[a].at[me] if self.scatter else ins[a], out[a].at[me], loc.at[a])

        if not self.two_level:
            def mine(a, k):
                peer, pid = _peer(x, y, c, k)
                return remote(a, k, ins[a].at[pid] if self.scatter else ins[a], out[a].at[me], peer)

            def theirs(a, k):
                peer, pid = _peer(x, y, c, k)
                return remote(a, k, ins[a].at[pid] if self.scatter else ins[a], out[a].at[pid], peer)

            def start():
                for a in range(self.n):
                    local(a).start()
                    for k in range(1, NDEV):
                        mine(a, k).start()

            def forward():
                pass

            def finish():
                for a in range(self.n):
                    for k in range(1, NDEV):
                        mine(a, k).wait_send()
                    for k in range(1, NDEV):
                        theirs(a, k).wait_recv()
                    local(a).wait()

            return start, forward, finish

        sibling = (x, y, 1 - c)
        chips = [(1 - x, y), (x, 1 - y), (1 - x, 1 - y)]
        slot = lambda px, py, pc: 4 * px + 2 * py + pc

        def own(a, k, to):
            return remote(a, k, ins[a], out[a].at[me], to)

        def landed(a, k, frm):
            return remote(a, k, ins[a], out[a].at[slot(*frm)], frm)

        def passed(a, j):
            rows = out[a].at[slot(*chips[j], c)]
            return remote(a, 5 + j, rows, rows, sibling)

        def start():
            for a in range(self.n):
                local(a).start()
                own(a, 1, sibling).start()
                for j, chip in enumerate(chips):
                    own(a, 2 + j, (*chip, c)).start()

        def forward():
            for a in range(self.n):
                for j, chip in enumerate(chips):
                    landed(a, 2 + j, (*chip, c)).wait_recv()
                    passed(a, j).start()

        def finish():
            for a in range(self.n):
                landed(a, 1, sibling).wait_recv()
                for j, chip in enumerate(chips):
                    remote(a, 5 + j, ins[a], out[a].at[slot(*chip, 1 - c)], sibling).wait_recv()
                own(a, 1, sibling).wait_send()
                for j, chip in enumerate(chips):
                    own(a, 2 + j, (*chip, c)).wait_send()
                    passed(a, j).wait_send()
                local(a).wait()

        return start, forward, finish


def _ride(body, n_in, n_out, n_scr, comm, first, mid, last):
    k = comm.n

    def wrapped(*refs):
        ins, cins = refs[:n_in], refs[n_in:n_in + k]
        o0 = n_in + k
        outs, couts = refs[o0:o0 + n_out], refs[o0 + n_out:o0 + n_out + k]
        s0 = o0 + n_out + k
        scr, sems = refs[s0:s0 + n_scr], refs[s0 + n_scr:]
        start, forward, finish = comm.phases(cins, couts, *sems)
        pl.when(first())(start)
        body(*ins, *outs, *scr)
        pl.when(mid())(forward)
        pl.when(last())(finish)

    return wrapped


def _exchange(arrs, name, scatter=False, two_level=False):
    comm = _Comm(arrs, scatter, two_level)

    def body(*refs):
        start, forward, finish = comm.phases(refs[:comm.n], refs[comm.n:2 * comm.n], *refs[2 * comm.n:])
        start()
        forward()
        finish()

    return pl.pallas_call(body, name=name, out_shape=comm.out_shape, in_specs=comm.specs, out_specs=comm.specs,
                          scratch_shapes=comm.scratch, compiler_params=pltpu.CompilerParams(has_side_effects=True))(*arrs)


def mm(a, b, mode, out_dtype, name, tm=1024, tn=1024, tk=1024, precision=None, comm=None, b_cols=None):
    a_parts = a.shape[0] if a.ndim == 3 else 1
    b_parts = b.shape[0] if b.ndim == 3 else 1
    assert b_parts == 1 or mode == "tn"
    ash, bsh = (a.shape[-2], a.shape[-1] * a_parts), b.shape[-2:]
    if mode == "nn":
        (M, K), (K2, N) = ash, bsh
    elif mode == "nt":
        (M, K), (N, K2) = ash, bsh
    else:
        (K, M), (K2, N) = ash, (bsh[0], bsh[1] * b_parts)
    assert K == K2, (name, a.shape, b.shape)
    col0 = 0
    if b_cols is not None:
        assert mode in ("nn", "nt") and tn % LANE == 0
        col0, N = b_cols[0], b_cols[1] * tn
    if mode == "tn":
        tm, tn, tk = _pick(M // a_parts, tm), _pick(N // b_parts, tn), _pick(K, tk)
    else:
        tm, tn, tk = _pick(M, tm), _pick(N // b_parts, tn), _pick(K // a_parts, tk)
    nk = K // tk
    if mode == "tn" and a_parts > 1:
        per = M // tm // a_parts
        a_spec = pl.BlockSpec((None, tk, tm), lambda i, j, k: (i // per, k, i % per))
    elif mode == "tn":
        a_spec = pl.BlockSpec((tk, tm), lambda i, j, k: (k, i))
    elif a_parts > 1:
        per = nk // a_parts
        a_spec = pl.BlockSpec((None, tm, tk), lambda i, j, k: (k // per, i, k % per))
    else:
        a_spec = pl.BlockSpec((tm, tk), lambda i, j, k: (i, k))
    if mode == "nt":
        b_spec = pl.BlockSpec((tn, tk), lambda i, j, k: (col0 + j, k))
    elif b_parts > 1:
        per = N // tn // b_parts
        b_spec = pl.BlockSpec((None, tk, tn), lambda i, j, k: (j // per, k, j % per))
    else:
        b_spec = pl.BlockSpec((tk, tn), lambda i, j, k: (k, col0 + j))
    dims = {"nn": ((1,), (0,)), "nt": ((1,), (1,)), "tn": ((0,), (0,))}[mode]

    def body(a_ref, b_ref, o_ref, *scr):
        p = lax.dot_general(a_ref[...], b_ref[...], (dims, ((), ())), preferred_element_type=f32, precision=precision)
        if nk == 1:
            o_ref[...] = p.astype(o_ref.dtype)
        else:
            acc = scr[0]
            k = pl.program_id(2)

            @pl.when(k == 0)
            def _():
                acc[...] = p

            @pl.when(k > 0)
            def _():
                acc[...] += p

            @pl.when(k == nk - 1)
            def _():
                o_ref[...] = acc[...].astype(o_ref.dtype)

    grid = (M // tm, N // tn, nk)
    scratch = [pltpu.VMEM((tm, tn), f32)] if nk > 1 else []
    out_spec = pl.BlockSpec((tm, tn), lambda i, j, k: (i, j))
    out_shape = jax.ShapeDtypeStruct((M, N), out_dtype)
    if comm is None:
        return pl.pallas_call(body, name=name, grid=grid, in_specs=[a_spec, b_spec], out_specs=out_spec, out_shape=out_shape,
                              scratch_shapes=scratch, compiler_params=_cp(("parallel", "parallel", "arbitrary")))(a, b)
    at = lambda pos: lambda: functools.reduce(jnp.logical_and, [pl.program_id(d) == p for d, p in enumerate(pos)])
    end = tuple(g - 1 for g in grid)
    return pl.pallas_call(
        _ride(body, 2, 1, len(scratch), comm, at((0, 0, 0)), at(end), at(end)), name=name, grid=grid,
        in_specs=[a_spec, b_spec] + comm.specs, out_specs=[out_spec] + comm.specs, out_shape=[out_shape] + comm.out_shape,
        scratch_shapes=scratch + comm.scratch, compiler_params=_cp(("arbitrary", "arbitrary", "arbitrary")),
    )(a, b, *comm.arrs)


def rowcall(name, fn, tok, bat, con, tok_out, acc_out, ts=256):
    B, S = tok[0][0].shape[:2]
    ts = min(ts, S)
    nt, nb, nc, no, na = len(tok), len(bat), len(con), len(tok_out), len(acc_out)

    def body(*refs):
        tr, br, cr = refs[:nt], refs[nt:nt + nb], refs[nt + nb:nt + nb + nc]
        orf, arf = refs[nt + nb + nc:nt + nb + nc + no], refs[nt + nb + nc + no:]
        touts, aouts = fn([r[0] for r in tr], [r[0] for r in br], [r[...] for r in cr])
        for r, v in zip(orf, touts):
            r[0] = v.astype(r.dtype)
        s = pl.program_id(1)
        for r, v in zip(arf, aouts):
            @pl.when(s == 0)
            def _(r=r):
                r[...] = jnp.zeros(r.shape, r.dtype)
            r[0] += v.astype(f32)

    in_specs = [pl.BlockSpec((1, ts, w), lambda b, s, cb=cb: (b, s, cb)) for (_, w, cb) in tok]
    in_specs += [pl.BlockSpec((1,) + a.shape[1:], lambda b, s: (b, 0, 0)) for a in bat]
    in_specs += [pl.BlockSpec(a.shape, lambda b, s, nd=a.ndim: (0,) * nd) for a in con]
    out_specs = [pl.BlockSpec((1, ts, w), lambda b, s: (b, s, 0)) for (w, _) in tok_out]
    out_specs += [pl.BlockSpec((1,) + shp, lambda b, s, nd=len(shp): (b,) + (0,) * nd) for shp in acc_out]
    out_shape = [jax.ShapeDtypeStruct((B, S, w), dt) for (w, dt) in tok_out]
    out_shape += [jax.ShapeDtypeStruct((B,) + shp, f32) for shp in acc_out]
    return pl.pallas_call(
        body, name=name, grid=(B, S // ts), in_specs=in_specs, out_specs=out_specs, out_shape=out_shape,
        compiler_params=_cp(("parallel", "arbitrary")),
    )(*[t[0] for t in tok], *bat, *con)


def rowcall_fwd(name, f, tok, bat, con, tok_out, ts=256):
    def fn(t, b, c):
        return f([v.astype(f32) for v in t], b, c), []
    return rowcall(name, fn, tok, bat, con, tok_out, [], ts)


def rowcall_bwd(name, f, tok, bat, con, cts, tok_grads, add=None, ts=256):
    nt, ncts = len(tok), len(cts)

    def fn(t, b, c):
        prim = [v.astype(f32) for v in t[:nt]]
        ct = [v.astype(f32) for v in t[nt:nt + ncts]]
        _, vjp = jax.vjp(lambda tt, bb, cc: f(tt, bb, cc), prim, b, c)
        dt, db, dc = vjp(ct)
        touts = [dt[i] for i, _ in tok_grads]
        if add is not None:
            touts[0] = touts[0] + t[nt + ncts].astype(f32)
        return touts, list(db) + list(dc)

    all_tok = list(tok) + list(cts) + ([add] if add is not None else [])
    tok_out = [(tok[i][1], dt) for i, dt in tok_grads]
    acc_out = [tuple(a.shape[1:]) for a in bat] + [tuple(a.shape) for a in con]
    return rowcall(name, fn, all_tok, bat, con, tok_out, acc_out, ts)


def _rms(y, w):
    return y * lax.rsqrt(jnp.mean(y * y, axis=-1, keepdims=True) + RMS_EPS) * w


def f_rms_mod(t, b, c):
    return [_rms(t[0], c[0]) * (1.0 + b[0]) + b[1]]


def f_post_pre(t, b, c):
    h1 = t[0] + b[0] * _rms(t[1], c[0])
    return [h1, _rms(h1, c[1]) * (1.0 + b[1]) + b[2]]


def f_merge(t, b, c):
    ga, gd, ya, yd = t
    return [jax.nn.sigmoid(ga) * ya + jax.nn.sigmoid(gd) * yd]


def f_dnout(t, b, c):
    o, z = t
    outs = []
    for h in range(DNH):
        sl = slice(h * DND, (h + 1) * DND)
        zh = z[:, sl]
        outs.append(_rms(o[:, sl], c[0]) * (zh * jax.nn.sigmoid(zh)))
    return [jnp.concatenate(outs, axis=1)]


def _softplus(x):
    return jnp.maximum(x, 0.0) + jnp.log(1.0 + jnp.exp(-jnp.abs(x)))


def f_gate(t, b, c):
    ba = t[0]
    a_log, dt_bias = c
    lane = lax.broadcasted_iota(jnp.int32, ba.shape, 1)
    beta = jax.nn.sigmoid(ba)
    g = -jnp.exp(a_log) * _softplus(ba + dt_bias)
    return [jnp.where(lane < DNH, beta, jnp.where(lane < 2 * DNH, g, 0.0))]


def _bucket_table():
    qi = np.arange(WIN)[:, None]
    kj = np.arange(2 * WIN)[None, :]
    dist = np.maximum(WIN + qi - kj, 0)
    max_exact = NBUCK // 2
    scaled = np.log(np.maximum(dist, 1).astype(np.float64) / max_exact) / math.log(MAXDIST / max_exact)
    large = np.minimum(max_exact + (scaled * (NBUCK - max_exact)).astype(np.int32), NBUCK - 1)
    return np.where(dist < max_exact, dist, large).astype(np.int32)


def _attn_mask(n):
    qi = lax.broadcasted_iota(jnp.int32, (WIN, 2 * WIN), 0)
    kj = lax.broadcasted_iota(jnp.int32, (WIN, 2 * WIN), 1)
    dist = WIN + qi - kj
    return (dist >= 0) & (dist < WIN) & ((kj >= WIN) | (n > 0))


def _swap_halves(x):
    return pltpu.roll(x, HD, axis=x.ndim - 1)


@jax.custom_vjp
def _swap_halves_vjp(x):
    return _swap_halves(x)


_swap_halves_vjp.defvjp(lambda x: (_swap_halves(x), None), lambda _, g: (_swap_halves(g),))


def _attn_block(q, kp, kc, vp, vc, bias, sinks, mask, differentiated):
    dot = _bdot_bf16_vjp if differentiated else _bdot_bf16
    swap = _swap_halves_vjp if differentiated else _swap_halves
    grp = HQ // HKV
    upper = lax.broadcasted_iota(jnp.int32, (2 * WIN, LANE), 1) >= HD
    kb, vb = jnp.concatenate([kp, kc], axis=0), jnp.concatenate([vp, vc], axis=0)
    kb_sw, vb_sw = swap(kb), swap(vb)

    def placed(natural, swapped, j, half):
        keep = upper if half == 1 else ~upper
        return jnp.where(keep, natural if j == half else swapped, 0.0)

    qh = _stack([q[:, (h // 2) * LANE:(h // 2 + 1) * LANE] for h in range(HQ)])
    ks = _stack([placed(kb, kb_sw, h // grp, h % 2) for h in range(HQ)])
    vs = _stack([placed(vb, vb_sw, h // grp, h % 2) for h in range(HQ)])
    s = dot(qh, ks, 2, 2) * (HD ** -0.5)
    s = jnp.where(mask[None], s + bias, NEG_INF)
    m = jnp.maximum(jnp.max(s, axis=-1, keepdims=True), sinks)
    p = jnp.exp(s - m)
    probs = p / (jnp.sum(p, axis=-1, keepdims=True) + jnp.exp(sinks - m))
    o = dot(probs, vs, 2, 1)
    return jnp.concatenate([o[2 * i] + o[2 * i + 1] for i in range(HQ // 2)], axis=1)


def _attn_specs(NB):
    last = NB - 1
    return [
        pl.BlockSpec((1, WIN, HQ * HD), lambda b, n: (b, jnp.minimum(n, last), CB_AQ // 4)),
        pl.BlockSpec((1, WIN, LANE), lambda b, n: (b, jnp.clip(n - 1, 0, last), CB_AK)),
        pl.BlockSpec((1, WIN, LANE), lambda b, n: (b, jnp.minimum(n, last), CB_AK)),
        pl.BlockSpec((1, WIN, LANE), lambda b, n: (b, jnp.clip(n - 1, 0, last), CB_AV)),
        pl.BlockSpec((1, WIN, LANE), lambda b, n: (b, jnp.minimum(n, last), CB_AV)),
        pl.BlockSpec((HQ, WIN, 2 * WIN), lambda b, n: (0, 0, 0)),
        pl.BlockSpec((HQ, 1, 1), lambda b, n: (0, 0, 0)),
    ]


def attn_fwd(proj, bias, sinks, comm):
    B, S, _ = proj.shape
    NB = S // WIN

    def body(q, kp, kc, vp, vc, bias_ref, sink_ref, o_ref):
        mask = _attn_mask(pl.program_id(1))
        o = _attn_block(*[r[0].astype(f32) for r in (q, kp, kc, vp, vc)], bias_ref[...], sink_ref[...], mask, False)
        o_ref[0] = o.astype(o_ref.dtype)

    at = lambda b, n: lambda: (pl.program_id(0) == b) & (pl.program_id(1) == n)
    return pl.pallas_call(
        _ride(body, 7, 1, 0, comm, at(0, 0), at(B - 1, (3 * NB) // 4), at(B - 1, NB - 1)), name="attn_fwd", grid=(B, NB),
        in_specs=_attn_specs(NB) + comm.specs,
        out_specs=[pl.BlockSpec((1, WIN, HQ * HD), lambda b, n: (b, n, 0))] + comm.specs,
        out_shape=[jax.ShapeDtypeStruct((B, S, HQ * HD), bf16)] + comm.out_shape, scratch_shapes=comm.scratch,
        compiler_params=_cp(("arbitrary", "arbitrary")),
    )(proj, proj, proj, proj, proj, bias, sinks, *comm.arrs)


def attn_bwd(proj, bias, sinks, dy, comm):
    B, S, _ = proj.shape
    NB = S // WIN
    last = NB - 1

    def body(q, kp, kc, vp, vc, bias_ref, sink_ref, dy_ref, dq_ref, dk_ref, dv_ref, dbias_ref, dsink_ref, kcar, vcar):
        b, n = pl.program_id(0), pl.program_id(1)

        @pl.when((b == 0) & (n == 0))
        def _():
            dbias_ref[...] = jnp.zeros(dbias_ref.shape, f32)
            dsink_ref[...] = jnp.zeros(dsink_ref.shape, f32)

        @pl.when(n == 0)
        def _():
            kcar[...] = jnp.zeros(kcar.shape, f32)
            vcar[...] = jnp.zeros(vcar.shape, f32)

        @pl.when(n < NB)
        def _():
            mask = _attn_mask(n)
            _, vjp = jax.vjp(lambda *a: _attn_block(*a, mask, True), *[r[0].astype(f32) for r in (q, kp, kc, vp, vc)],
                             bias_ref[...], sink_ref[...])
            dq, dkp, dkc, dvp, dvc, dbias, dsink = vjp(dy_ref[0].astype(f32))
            dq_ref[0] = dq.astype(dq_ref.dtype)
            dbias_ref[...] += dbias
            dsink_ref[...] += dsink
            dk_ref[0] = (kcar[...] + dkp).astype(dk_ref.dtype)
            dv_ref[0] = (vcar[...] + dvp).astype(dv_ref.dtype)
            kcar[...] = dkc
            vcar[...] = dvc

        @pl.when(n == NB)
        def _():
            dk_ref[0] = kcar[...].astype(dk_ref.dtype)
            dv_ref[0] = vcar[...].astype(dv_ref.dtype)

    in_specs = _attn_specs(NB) + [pl.BlockSpec((1, WIN, HQ * HD), lambda b, n: (b, jnp.minimum(n, last), 0))]
    kv_out = pl.BlockSpec((1, WIN, LANE), lambda b, n: (b, jnp.maximum(n - 1, 0), 0))
    at = lambda b, n: lambda: (pl.program_id(0) == b) & (pl.program_id(1) == n)
    return pl.pallas_call(
        _ride(body, 8, 5, 2, comm, at(0, 0), at(B - 1, NB), at(B - 1, NB)), name="attn_bwd", grid=(B, NB + 1),
        in_specs=in_specs + comm.specs,
        out_specs=[pl.BlockSpec((1, WIN, HQ * HD), lambda b, n: (b, jnp.minimum(n, last), 0)), kv_out, kv_out,
                   pl.BlockSpec((HQ, WIN, 2 * WIN), lambda b, n: (0, 0, 0)), pl.BlockSpec((HQ, 1, 1), lambda b, n: (0, 0, 0))] + comm.specs,
        out_shape=[jax.ShapeDtypeStruct((B, S, HQ * HD), bf16), jax.ShapeDtypeStruct((B, S, LANE), bf16),
                   jax.ShapeDtypeStruct((B, S, LANE), bf16), jax.ShapeDtypeStruct((HQ, WIN, 2 * WIN), f32),
                   jax.ShapeDtypeStruct((HQ, 1, 1), f32)] + comm.out_shape,
        scratch_shapes=[pltpu.VMEM((WIN, LANE), f32), pltpu.VMEM((WIN, LANE), f32)] + comm.scratch,
        compiler_params=_cp(("arbitrary", "arbitrary")),
    )(proj, proj, proj, proj, proj, bias, sinks, dy, *comm.arrs)


DN_ROWS, FFN_ROWS = 256, 32


def _stage_rows(dst, value):
    dst[0:8] = jnp.zeros((8, LANE), f32)
    dst[8:8 + value.shape[0]] = value


def _conv_rows(xs, w, width, r, rows):
    wins = [xs[pl.ds(r + 8 - (width - 1) + j, rows), :] for j in range(width)]
    out = w[0:1] * wins[0]
    for j in range(1, width):
        out = out + w[j:j + 1] * wins[j]
    return out, wins


def _fold8(v):
    return jnp.sum(v.reshape(v.shape[0] // 8, 8, LANE), axis=0)


def _conv_rows_t(ds, w, width, r, rows):
    out = w[0:1] * ds[pl.ds(r + width - 1, rows), :]
    for j in range(1, width):
        out = out + w[j:j + 1] * ds[pl.ds(r + width - 1 - j, rows), :]
    return out


def _dn_outblk(i):
    return (i % DNH) * 3 + i // DNH


def _dn_act(c, isqk):
    sg = jax.nn.sigmoid(c)
    y = c * sg
    n = lax.rsqrt(jnp.sum(y * y, axis=-1, keepdims=True) + L2_EPS)
    return jnp.where(isqk, y * n, y), sg, n


def dnconv_fwd(proj, conv_w):
    B, S, _ = proj.shape
    rows = min(DN_ROWS, S)

    def body(x_ref, w_ref, o_ref, xs):
        isqk = pl.program_id(0) < 2 * DNH
        _stage_rows(xs, x_ref[0].astype(f32))
        w = w_ref[...]
        for r in range(0, S, rows):
            c, _ = _conv_rows(xs, w, DNK, r, rows)
            o_ref[0, pl.ds(r, rows), :] = _dn_act(c, isqk)[0]

    return pl.pallas_call(
        body, name="dnconv_fwd", grid=(3 * DNH, B),
        in_specs=[pl.BlockSpec((1, S, LANE), lambda i, b: (b, 0, CB_DQKV + i)), pl.BlockSpec((DNK, LANE), lambda i, b: (0, i))],
        out_specs=pl.BlockSpec((1, S, LANE), lambda i, b: (b, 0, _dn_outblk(i))),
        out_shape=jax.ShapeDtypeStruct((B, S, 3 * DNH * DND), f32), scratch_shapes=[pltpu.VMEM((S + 8, LANE), f32)],
        compiler_params=_cp(("parallel", "parallel")),
    )(proj, conv_w)


def dnconv_bwd(proj, conv_w, dqkvn):
    B, S, _ = proj.shape
    rows = min(DN_ROWS, S)

    def body(x_ref, w_ref, dy_ref, dx_ref, dw_ref, xs, ds):
        isqk = pl.program_id(0) < 2 * DNH
        _stage_rows(xs, x_ref[0].astype(f32))
        w = w_ref[...]
        dw = [jnp.zeros((8, LANE), f32) for _ in range(DNK)]
        for r in range(0, S, rows):
            c, wins = _conv_rows(xs, w, DNK, r, rows)
            out, sg, n = _dn_act(c, isqk)
            dout = dy_ref[0, pl.ds(r, rows), :]
            dy = jnp.where(isqk, n * (dout - out * jnp.sum(dout * out, axis=-1, keepdims=True)), dout)
            dc = dy * (sg * (1.0 + c * (1.0 - sg)))
            ds[pl.ds(r, rows), :] = dc
            for j in range(DNK):
                dw[j] = dw[j] + _fold8(dc * wins[j])
        ds[S:S + 8] = jnp.zeros((8, LANE), f32)
        for r in range(0, S, rows):
            dx_ref[0, pl.ds(r, rows), :] = _conv_rows_t(ds, w, DNK, r, rows).astype(dx_ref.dtype)

        @pl.when(pl.program_id(1) == 0)
        def _():
            dw_ref[...] = jnp.zeros(dw_ref.shape, f32)
        dw_ref[...] += jnp.concatenate([jnp.sum(d, axis=0, keepdims=True) for d in dw], axis=0)

    return pl.pallas_call(
        body, name="dnconv_bwd", grid=(3 * DNH, B),
        in_specs=[pl.BlockSpec((1, S, LANE), lambda i, b: (b, 0, CB_DQKV + i)), pl.BlockSpec((DNK, LANE), lambda i, b: (0, i)),
                  pl.BlockSpec((1, S, LANE), lambda i, b: (b, 0, _dn_outblk(i)))],
        out_specs=[pl.BlockSpec((1, S, LANE), lambda i, b: (b, 0, i)), pl.BlockSpec((DNK, LANE), lambda i, b: (0, i))],
        out_shape=[jax.ShapeDtypeStruct((B, S, 3 * DNH * DND), bf16), jax.ShapeDtypeStruct((DNK, 3 * DNH * DND), f32)],
        scratch_shapes=[pltpu.VMEM((S + 8, LANE), f32), pltpu.VMEM((S + 8, LANE), f32)],
        compiler_params=_cp(("parallel", "arbitrary")),
    )(proj, conv_w, dqkvn)


def _bdot(a, b, ca, cb, precision=HI):
    return lax.dot_general(a, b, (((ca,), (cb,)), ((0,), (0,))), preferred_element_type=f32, precision=precision)


def _bdot_bf16(a, b, ca, cb):
    return _bdot(a.astype(bf16), b.astype(bf16), ca, cb, None)


@functools.partial(jax.custom_vjp, nondiff_argnums=(2, 3))
def _bdot_bf16_vjp(a, b, ca, cb):
    return _bdot_bf16(a, b, ca, cb)


def _bdot_bf16_fwd(a, b, ca, cb):
    return _bdot_bf16(a, b, ca, cb), (a, b)


def _bdot_bf16_bwd(ca, cb, res, g):
    a, b = res
    fa, fb = 3 - ca, 3 - cb
    da = _bdot_bf16(g, b, 2, fb) if ca == 2 else _bdot_bf16(b, g, fb, 2)
    db = _bdot_bf16(a, g, fa, 1) if cb == 1 else _bdot_bf16(g, a, 1, fa)
    return da, db


_bdot_bf16_vjp.defvjp(_bdot_bf16_fwd, _bdot_bf16_bwd)


def _neumann_inverse(low):
    n = low.shape[-1]
    eye = (lax.broadcasted_iota(jnp.int32, (n, n), 0) == lax.broadcasted_iota(jnp.int32, (n, n), 1)).astype(f32)
    p = -low
    x = eye[None] + p
    for _ in range(5):
        p = _bdot(p, p, 2, 1, MID)
        x = x + _bdot(x, p, 2, 1, MID)
    return x


@jax.custom_vjp
def _unit_lower_inverse(low):
    return _neumann_inverse(low)


def _uli_fwd(low):
    t = _neumann_inverse(low)
    return t, t


def _uli_bwd(t, dt):
    return (-_bdot(_bdot(t, dt, 1, 1, MID), t, 2, 2, MID),)


_unit_lower_inverse.defvjp(_uli_fwd, _uli_bwd)


def _stack(xs):
    return jnp.concatenate([x[None] for x in xs], axis=0)


def _delta_chunk(qkv, bg, state, differentiated):
    inverse = _unit_lower_inverse if differentiated else _neumann_inverse
    lo = _bdot_bf16_vjp if differentiated else _bdot_bf16
    B = qkv.shape[0]
    G = B * DNH
    pairs = [(b, h) for b in range(B) for h in range(DNH)]
    col = lambda b, h, kind: qkv[b, :, (3 * h + kind) * DND:(3 * h + kind + 1) * DND]
    q, k, v = [_stack([col(b, h, kind) for b, h in pairs]) for kind in range(3)]
    lane = lax.broadcasted_iota(jnp.int32, (CH, LANE), 1)
    pick = lambda b, l: jnp.sum(jnp.where(lane == l, bg[b], 0.0), axis=1, keepdims=True)
    beta = _stack([pick(b, h) for b, h in pairs])
    g = _stack([pick(b, h + DNH) for b, h in pairs])
    ri = lax.broadcasted_iota(jnp.int32, (CH, CH), 0)
    ci = lax.broadcasted_iota(jnp.int32, (CH, CH), 1)
    incl, strict = (ri >= ci)[None], (ri > ci)[None]
    gc = _bdot(jnp.broadcast_to(incl.astype(f32), (G, CH, CH)), jnp.broadcast_to(g, (G, CH, LANE)), 2, 1, MID)
    e0 = jnp.broadcast_to((lane == 0).astype(f32)[None], (G, CH, LANE))
    gc_row = _bdot(e0, gc, 2, 2, MID)
    diff = gc[:, :, :CH] - gc_row
    decay = jnp.where(incl, jnp.exp(jnp.where(incl, diff, 0.0)), 0.0)
    qs = q * (DND ** -0.5)
    kb, vb = k * beta, v * beta
    eg = jnp.exp(gc)
    with_k = lo(jnp.concatenate([kb, qs], axis=1), k, 2, 2)
    low = jnp.where(strict, with_k[:, :CH] * decay, 0.0)
    intra = jnp.where(incl, with_k[:, CH:] * decay, 0.0)
    tinv = inverse(low)
    solved = _bdot(tinv, jnp.concatenate([vb, kb * eg], axis=2), 2, 1, MID)
    u, w = solved[:, :, :DND], solved[:, :, DND:]
    gl = gc[:, CH - 1:CH, :]
    k_tail = k * jnp.exp(gl - gc)
    with_state = lo(jnp.concatenate([w, qs * eg], axis=1), state, 2, 1)
    v_new = u - with_state[:, :CH]
    o = with_state[:, CH:] + lo(intra, v_new, 2, 1)
    new_state = state * jnp.exp(gl) + lo(k_tail, v_new, 1, 1)
    return o, new_state


def delta_fwd(qkvn, bg, comm):
    B, S, _ = qkvn.shape
    NC, G = S // CH, B * DNH

    def body(qkv_ref, bg_ref, o_ref, st_ref, state):
        @pl.when(pl.program_id(0) == 0)
        def _():
            state[...] = jnp.zeros(state.shape, f32)
        s0 = state[...]
        st_ref[0] = s0
        o, s1 = _delta_chunk(qkv_ref[...], bg_ref[...], s0, False)
        for b in range(B):
            for h in range(DNH):
                o_ref[b, :, h * DND:(h + 1) * DND] = o[b * DNH + h]
        state[...] = s1

    at = lambda c: lambda: pl.program_id(0) == c
    return pl.pallas_call(
        _ride(body, 2, 2, 1, comm, at(0), at((7 * NC) // 8), at(NC - 1)), name="delta_fwd", grid=(NC,),
        in_specs=[pl.BlockSpec((B, CH, 3 * DNH * DND), lambda c: (0, c, 0)), pl.BlockSpec((B, CH, LANE), lambda c: (0, c, 0))] + comm.specs,
        out_specs=[pl.BlockSpec((B, CH, DNH * DND), lambda c: (0, c, 0)), pl.BlockSpec((1, G, DND, DND), lambda c: (c, 0, 0, 0))] + comm.specs,
        out_shape=[jax.ShapeDtypeStruct((B, S, DNH * DND), f32), jax.ShapeDtypeStruct((NC, G, DND, DND), f32)] + comm.out_shape,
        scratch_shapes=[pltpu.VMEM((G, DND, DND), f32)] + comm.scratch, compiler_params=_cp(("arbitrary",)),
    )(qkvn, bg, *comm.arrs)


def delta_bwd(qkvn, bg, states, do, comm):
    B, S, _ = qkvn.shape
    NC, G = S // CH, B * DNH

    def body(qkv_ref, bg_ref, st_ref, do_ref, dqkv_ref, dbg_ref, dstate):
        @pl.when(pl.program_id(0) == 0)
        def _():
            dstate[...] = jnp.zeros(dstate.shape, f32)
        _, vjp = jax.vjp(lambda a, g, s: _delta_chunk(a, g, s, True), qkv_ref[...], bg_ref[...], st_ref[0])
        do = _stack([do_ref[b, :, h * DND:(h + 1) * DND] for b in range(B) for h in range(DNH)])
        dqkv, dbg, ds = vjp((do, dstate[...]))
        dqkv_ref[...] = dqkv
        dbg_ref[...] = dbg
        dstate[...] = ds

    rev = lambda c: NC - 1 - c
    at = lambda c: lambda: pl.program_id(0) == c
    return pl.pallas_call(
        _ride(body, 4, 2, 1, comm, at(0), at(NC - 1), at(NC - 1)), name="delta_bwd", grid=(NC,),
        in_specs=[pl.BlockSpec((B, CH, 3 * DNH * DND), lambda c: (0, rev(c), 0)), pl.BlockSpec((B, CH, LANE), lambda c: (0, rev(c), 0)),
                  pl.BlockSpec((1, G, DND, DND), lambda c: (rev(c), 0, 0, 0)),
                  pl.BlockSpec((B, CH, DNH * DND), lambda c: (0, rev(c), 0))] + comm.specs,
        out_specs=[pl.BlockSpec((B, CH, 3 * DNH * DND), lambda c: (0, rev(c), 0)),
                   pl.BlockSpec((B, CH, LANE), lambda c: (0, rev(c), 0))] + comm.specs,
        out_shape=[jax.ShapeDtypeStruct((B, S, 3 * DNH * DND), f32), jax.ShapeDtypeStruct((B, S, LANE), f32)] + comm.out_shape,
        scratch_shapes=[pltpu.VMEM((G, DND, DND), f32)] + comm.scratch, compiler_params=_cp(("arbitrary",)),
    )(qkvn, bg, states, do, *comm.arrs)


GELU_C0, GELU_C1 = math.sqrt(2.0 / math.pi), 0.044715


def _ffn_specs(S):
    nblk = DFF // LANE
    return [pl.BlockSpec((1, S, LANE), lambda i, b: (b, 0, i)), pl.BlockSpec((1, S, LANE), lambda i, b: (b, 0, nblk + i)),
            pl.BlockSpec((FK, LANE), lambda i, b: (0, i)), pl.BlockSpec((FK, LANE), lambda i, b: (0, nblk + i))]


def ffnconv_fwd(up, conv_w):
    B, S, _ = up.shape
    rows = min(FFN_ROWS, S)

    def body(g_ref, v_ref, gw_ref, vw_ref, o_ref, xg, xv):
        _stage_rows(xg, g_ref[0].astype(f32))
        _stage_rows(xv, v_ref[0].astype(f32))
        gw, vw = gw_ref[...], vw_ref[...]
        for r in range(0, S, rows):
            g, _ = _conv_rows(xg, gw, FK, r, rows)
            v, _ = _conv_rows(xv, vw, FK, r, rows)
            t = jnp.tanh(GELU_C0 * (g * (1.0 + GELU_C1 * (g * g))))
            o_ref[0, pl.ds(r, rows), :] = (0.5 * g * (1.0 + t) * v).astype(o_ref.dtype)

    return pl.pallas_call(
        body, name="ffnconv_fwd", grid=(DFF // LANE, B), in_specs=_ffn_specs(S),
        out_specs=pl.BlockSpec((1, S, LANE), lambda i, b: (b, 0, i)), out_shape=jax.ShapeDtypeStruct((B, S, DFF), bf16),
        scratch_shapes=[pltpu.VMEM((S + 8, LANE), f32)] * 2, compiler_params=_cp(("parallel", "parallel")),
    )(up, up, conv_w, conv_w)


def ffnconv_bwd(up, conv_w, dact):
    B, S, _ = up.shape
    rows = min(FFN_ROWS, S)

    def body(g_ref, v_ref, gw_ref, vw_ref, dy_ref, dx_ref, dw_ref, xg, xv, dg, dv):
        _stage_rows(xg, g_ref[0].astype(f32))
        _stage_rows(xv, v_ref[0].astype(f32))
        gw, vw = gw_ref[...], vw_ref[...]
        dgw = [jnp.zeros((8, LANE), f32) for _ in range(FK)]
        dvw = [jnp.zeros((8, LANE), f32) for _ in range(FK)]
        for r in range(0, S, rows):
            g, gwins = _conv_rows(xg, gw, FK, r, rows)
            v, vwins = _conv_rows(xv, vw, FK, r, rows)
            g2 = g * g
            t = jnp.tanh(GELU_C0 * (g * (1.0 + GELU_C1 * g2)))
            half = 0.5 * (1.0 + t)
            dgelu = half + (0.5 * GELU_C0) * g * (1.0 - t * t) * (1.0 + (3.0 * GELU_C1) * g2)
            dy = dy_ref[0, pl.ds(r, rows), :].astype(f32)
            dvc = dy * (g * half)
            dgc = dy * v * dgelu
            dg[pl.ds(r, rows), :] = dgc
            dv[pl.ds(r, rows), :] = dvc
            for j in range(FK):
                dgw[j] = dgw[j] + _fold8(dgc * gwins[j])
                dvw[j] = dvw[j] + _fold8(dvc * vwins[j])
        dg[S:S + 8] = jnp.zeros((8, LANE), f32)
        dv[S:S + 8] = jnp.zeros((8, LANE), f32)
        for r in range(0, S, rows):
            dx_ref[0, 0, pl.ds(r, rows), :] = _conv_rows_t(dg, gw, FK, r, rows).astype(dx_ref.dtype)
            dx_ref[1, 0, pl.ds(r, rows), :] = _conv_rows_t(dv, vw, FK, r, rows).astype(dx_ref.dtype)

        @pl.when(pl.program_id(1) == 0)
        def _():
            dw_ref[...] = jnp.zeros(dw_ref.shape, f32)
        dw_ref[0] += jnp.concatenate([jnp.sum(d, axis=0, keepdims=True) for d in dgw], axis=0)
        dw_ref[1] += jnp.concatenate([jnp.sum(d, axis=0, keepdims=True) for d in dvw], axis=0)

    return pl.pallas_call(
        body, name="ffnconv_bwd", grid=(DFF // LANE, B),
        in_specs=_ffn_specs(S) + [pl.BlockSpec((1, S, LANE), lambda i, b: (b, 0, i))],
        out_specs=[pl.BlockSpec((2, 1, S, LANE), lambda i, b: (0, b, 0, i)), pl.BlockSpec((2, FK, LANE), lambda i, b: (0, 0, i))],
        out_shape=[jax.ShapeDtypeStruct((2, B, S, DFF), bf16), jax.ShapeDtypeStruct((2, FK, DFF), f32)],
        scratch_shapes=[pltpu.VMEM((S + 8, LANE), f32)] * 4, compiler_params=_cp(("parallel", "arbitrary")),
    )(up, up, conv_w, conv_w, dact)


def ada_fwd(c_all, ada_w, ada_b):
    def body(c_ref, w_ref, b_ref, o_ref):
        c = c_ref[...]
        act = (c * jax.nn.sigmoid(c)).astype(bf16)
        o_ref[...] = jnp.dot(act, w_ref[...].astype(bf16), preferred_element_type=f32) + b_ref[...]

    return pl.pallas_call(body, name="ada_fwd", out_shape=jax.ShapeDtypeStruct((c_all.shape[0], ada_w.shape[1]), f32),
                          compiler_params=pltpu.CompilerParams(vmem_limit_bytes=VMEM_LIMIT))(c_all, ada_w, ada_b)


def ada_bwd(c_all, dmod):
    def body(c_ref, d_ref, o_ref):
        c = c_ref[...]
        act = (c * jax.nn.sigmoid(c)).astype(bf16)
        o_ref[...] = lax.dot_general(act, d_ref[...].astype(bf16), (((0,), (0,)), ((), ())), preferred_element_type=f32)

    return pl.pallas_call(body, name="ada_bwd", out_shape=jax.ShapeDtypeStruct((c_all.shape[1], dmod.shape[1]), f32),
                          compiler_params=pltpu.CompilerParams(vmem_limit_bytes=VMEM_LIMIT))(c_all, dmod)


def loss_head(h1, y2, target, g2, w):
    def fn(t, b, c):
        h, y, tg = [v.astype(f32) for v in t]

        def loss_fn(h, y, g, w):
            e = h + g * _rms(y, w) - tg
            return 0.5 * jnp.sum(jnp.mean(e * e, axis=-1))

        loss, grads = jax.value_and_grad(loss_fn, argnums=(0, 1, 2, 3))(h, y, b[0], c[0])
        return [grads[0], grads[1]], [grads[2], grads[3], jnp.full((1, LANE), loss, f32)]

    return rowcall("loss_head", fn, [(h1, D, 0), (y2, D, 0), (target, D, 0)], [g2], [w], [(D, f32), (D, bf16)],
                   [(1, D), (1, D), (1, LANE)])


def adamw(w, gparts, m, v, name):
    R, C = w.shape
    P = gparts.shape[0]
    budget = 2 * 1024 * 1024
    tr, tc = R, C
    if R * C * 4 > budget and R % 8 == 0:
        tr = max(t for t in range(8, R + 1, 8) if R % t == 0 and t * C * 4 <= budget)
    elif R * C * 4 > budget:
        tc = max(t for t in range(LANE, C + 1, LANE) if C % t == 0 and R * t * 4 <= budget)

    def body(w_ref, g_ref, m_ref, v_ref, go, do, mo, vo):
        g = g_ref[0].astype(f32)
        for p in range(1, P):
            g = g + g_ref[p].astype(f32)
        m2 = B1 * m_ref[...] + (1.0 - B1) * g
        v2 = B2 * v_ref[...] + (1.0 - B2) * jnp.square(g)
        m_hat = m2 * (1.0 / (1.0 - B1 ** STEP))
        v_hat = v2 * (1.0 / (1.0 - B2 ** STEP))
        go[...] = g
        do[...] = -LR * (m_hat / (jnp.sqrt(v_hat) + EPS) + WD * w_ref[...])
        mo[...] = m2
        vo[...] = v2

    blk = pl.BlockSpec((tr, tc), lambda i, j: (i, j))
    return pl.pallas_call(
        body, name=name, grid=(R // tr, C // tc), in_specs=[blk, pl.BlockSpec((P, tr, tc), lambda i, j: (0, i, j)), blk, blk],
        out_specs=[blk] * 4, out_shape=[jax.ShapeDtypeStruct((R, C), f32)] * 4, compiler_params=_cp(("parallel", "parallel")),
    )(w, gparts, m, v)


def _pack_w_in(wt):
    aq, ak, av, dqkv, dz, dbeta, da, ga, gd = jnp.split(wt, np.cumsum(IN_SPLITS)[:-1].tolist(), axis=0)
    ba = jnp.pad(jnp.concatenate([dbeta, da], axis=0), ((0, LANE - 2 * DNH), (0, 0)))
    return jnp.concatenate([ga, gd, aq, dqkv, dz, ak, av, ba], axis=0)


def _unpack_w_in(p):
    row = lambda cb, n: p[cb * LANE: cb * LANE + n]
    ba = row(CB_BA, 2 * DNH)
    return jnp.concatenate([row(CB_AQ, HQ * HD), row(CB_AK, HKV * HD), row(CB_AV, HKV * HD), row(CB_DQKV, 3 * DNH * DND),
                            row(CB_DZ, DNH * DND), ba[:DNH], ba[DNH:], row(CB_GA, D), row(CB_GD, D)], axis=0)


def _cols_gathered(g):
    return g.transpose(1, 0, 2).reshape(g.shape[1], NDEV * g.shape[2])


def _cols_split(w):
    r = w.shape[0]
    return w.reshape(r, NDEV, w.shape[1] // NDEV).transpose(1, 0, 2)


def kernel(x, c, ada_w, ada_b, norm_mix_pre, norm_mix_post, norm_ffn_pre, norm_ffn_post, w_in, dn_conv_w, dn_a_log, dn_dt_bias, dn_norm_w, attn_sinks, rel_bias, w_attn_branch, w_dn_branch, w_out, ffn_w_up, ffn_conv_w, ffn_w_down, loss_target, m_ada_w, m_ada_b, m_norm_mix_pre, m_norm_mix_post, m_norm_ffn_pre, m_norm_ffn_post, m_w_in, m_dn_conv_w, m_dn_a_log, m_dn_dt_bias, m_dn_norm_w, m_attn_sinks, m_rel_bias, m_w_attn_branch, m_w_dn_branch, m_w_out, m_ffn_w_up, m_ffn_conv_w, m_ffn_w_down, v_ada_w, v_ada_b, v_norm_mix_pre, v_norm_mix_post, v_norm_ffn_pre, v_norm_ffn_post, v_w_in, v_dn_conv_w, v_dn_a_log, v_dn_dt_bias, v_dn_norm_w, v_attn_sinks, v_rel_bias, v_w_attn_branch, v_w_dn_branch, v_w_out, v_ffn_w_up, v_ffn_conv_w, v_ffn_w_down):
    B, S, _ = x.shape
    T = B * S
    me = 4 * lax.axis_index("x") + 2 * lax.axis_index("y") + lax.axis_index("c")
    big = dict(w_in=w_in, dn_conv_w=dn_conv_w, w_attn_branch=w_attn_branch, w_dn_branch=w_dn_branch, w_out=w_out,
               ffn_w_up=ffn_w_up, ffn_conv_w=ffn_conv_w, ffn_w_down=ffn_w_down)
    big_names = list(big)

    first, mid, late = ["w_in", "dn_conv_w"], ["w_attn_branch", "w_dn_branch", "w_out"], ["ffn_w_up", "ffn_conv_w", "ffn_w_down"]
    transposed = ("w_in", "ffn_w_up")
    local = lambda n, a: a[0].T if n in transposed else a[0]
    shard = lambda names: [local(n, big[n]).astype(bf16) for n in names]
    *got, c_all = _exchange(shard(first) + [c], "gather_w_in", two_level=True)
    gw = dict(zip(first, got))
    c_all = c_all.reshape(NDEV * B, D)

    wp = _pack_w_in(gw["w_in"].reshape(IN_DIM, D))
    conv_dn = _cols_gathered(gw["dn_conv_w"]).astype(f32)

    ncol = ada_w.shape[2]
    ada_b_mine = lax.dynamic_slice_in_dim(ada_b, me * ncol, ncol, axis=1)
    mod_cols = ada_fwd(c_all, ada_w[0], ada_b_mine)
    (mod_g,) = _exchange([mod_cols], "gather_mod")
    mod = lax.dynamic_slice_in_dim(mod_g, me * B, B, axis=1).transpose(1, 0, 2).reshape(B, NMOD * D)
    sh1, sc1, g1, sh2, sc2, g2 = [mod[:, i * D:(i + 1) * D].reshape(B, 1, D) for i in range(NMOD)]

    onehot = (jnp.asarray(_bucket_table()).reshape(1, -1) == jnp.arange(NBUCK, dtype=jnp.int32)[:, None]).astype(f32)
    bias = mm(rel_bias.T, onehot, "nn", f32, "bias_table", tn=8192, precision=HI).reshape(HQ, WIN, 2 * WIN)
    sinks = attn_sinks.reshape(HQ, 1, 1)
    a_log_pad = jnp.pad(dn_a_log, ((0, 0), (DNH, LANE - 2 * DNH)))
    dt_bias_pad = jnp.pad(dn_dt_bias, ((0, 0), (DNH, LANE - 2 * DNH)))

    (u1,) = rowcall_fwd("mix_pre", f_rms_mod, [(x, D, 0)], [sc1, sh1], [norm_mix_pre], [(D, bf16)])
    proj = mm(u1.reshape(T, D), wp, "nt", bf16, "proj", tm=512, tn=CB_BA * LANE, b_cols=(0, 1)).reshape(B, S, CB_BA * LANE)
    ba = mm(u1.reshape(T, D), wp, "nt", f32, "proj_ba", tn=LANE, b_cols=(CB_BA, 1)).reshape(B, S, LANE)
    ya, *got = attn_fwd(proj, bias, sinks, _Comm(shard(mid), two_level=True))
    gw.update(zip(mid, got))
    wa = _cols_gathered(gw["w_attn_branch"])
    wd = _cols_gathered(gw["w_dn_branch"])
    wo = gw["w_out"].reshape(D, D)
    qkvn = dnconv_fwd(proj, conv_dn)
    (bg,) = rowcall_fwd("dn_gate", f_gate, [(ba, LANE, 0)], [], [a_log_pad, dt_bias_pad], [(LANE, f32)])
    o_dn, states, *got = delta_fwd(qkvn, bg, _Comm(shard(late), two_level=True))
    gw.update(zip(late, got))
    wup = gw["ffn_w_up"].reshape(2 * DFF, D)
    conv_ffn = _cols_gathered(gw["ffn_conv_w"]).astype(f32)
    wdown = gw["ffn_w_down"].reshape(DFF, D)
    (yd,) = rowcall_fwd("dn_out", f_dnout, [(o_dn, DNH * DND, 0), (proj, DNH * DND, CB_DZ // 4)], [], [dn_norm_w], [(DNH * DND, bf16)])
    pa = mm(ya.reshape(T, HQ * HD), wa, "nn", bf16, "attn_branch").reshape(B, S, D)
    pd = mm(yd.reshape(T, DNH * DND), wd, "nn", bf16, "dn_branch").reshape(B, S, D)
    merge_tok = [(proj, D, CB_GA // 8), (proj, D, CB_GD // 8), (pa, D, 0), (pd, D, 0)]
    (merged,) = rowcall_fwd("merge", f_merge, merge_tok, [], [], [(D, bf16)])
    y1 = mm(merged.reshape(T, D), wo, "nn", bf16, "mix_out").reshape(B, S, D)
    post_pre = ([(x, D, 0), (y1, D, 0)], [g1, sc2, sh2], [norm_mix_post, norm_ffn_pre])
    h1, u2 = rowcall_fwd("mix_post_ffn_pre", f_post_pre, *post_pre, [(D, f32), (D, bf16)])
    up = mm(u2.reshape(T, D), wup, "nt", bf16, "ffn_up", tn=2816).reshape(B, S, 2 * DFF)
    act = ffnconv_fwd(up, conv_ffn)
    y2 = mm(act.reshape(T, DFF), wdown, "nn", bf16, "ffn_down", tk=2816).reshape(B, S, D)

    dh1_a, dy2, dg2, dw_ffn_post, loss_b = loss_head(h1, y2, loss_target, g2, norm_ffn_post)
    dy2f = dy2.reshape(T, D)
    dact = mm(dy2f, wdown, "nt", bf16, "ffn_down_dx", tn=2816).reshape(B, S, DFF)
    g_wdown = mm(act.reshape(T, DFF), dy2f, "tn", f32, "ffn_down_dw", tm=2816, tn=512, tk=4096)
    dup, g_conv_ffn = ffnconv_bwd(up, conv_ffn, dact)
    dupf = dup.reshape(2, T, DFF)
    g_conv_ffn = g_conv_ffn.transpose(1, 0, 2).reshape(FK, 2 * DFF)
    du2 = mm(dupf, wup, "nn", bf16, "ffn_up_dx", tk=2816).reshape(B, S, D)
    g_wup = mm(dupf, u2.reshape(T, D), "tn", f32, "ffn_up_dw", tm=1408, tk=2048)
    dh1, dy1, dg1, dsc2, dsh2, dw_mix_post, dw_ffn_pre = rowcall_bwd(
        "mix_post_ffn_pre_bwd", f_post_pre, *post_pre, [(dh1_a, D, 0), (du2, D, 0)], [(0, f32), (1, bf16)])
    dy1f = dy1.reshape(T, D)
    dmerged = mm(dy1f, wo, "nt", bf16, "mix_out_dx").reshape(B, S, D)
    g_wo = mm(merged.reshape(T, D), dy1f, "tn", f32, "mix_out_dw", tk=2048)
    dga, dgd, dpa, dpd = rowcall_bwd("merge_bwd", f_merge, merge_tok, [], [], [(dmerged, D, 0)],
                                     [(0, bf16), (1, bf16), (2, bf16), (3, bf16)])
    dpaf, dpdf = dpa.reshape(T, D), dpd.reshape(T, D)
    dya = mm(dpaf, wa, "nt", bf16, "attn_branch_dx").reshape(B, S, HQ * HD)
    g_wa = mm(ya.reshape(T, HQ * HD), dpaf, "tn", f32, "attn_branch_dw", tk=2048)
    dyd = mm(dpdf, wd, "nt", bf16, "dn_branch_dx").reshape(B, S, DNH * DND)
    g_wd = mm(yd.reshape(T, DNH * DND), dpdf, "tn", f32, "dn_branch_dw", tk=2048)
    do_dn, dz, dw_dn_norm = rowcall_bwd("dn_out_bwd", f_dnout, [(o_dn, DNH * DND, 0), (proj, DNH * DND, CB_DZ // 4)], [], [dn_norm_w],
                                        [(dyd, DNH * DND, 0)], [(0, f32), (1, bf16)])
    parts = {}
    outbox = lambda d: _Comm([d[n].astype(bf16) for n in d], scatter=True)
    send = dict(ffn_w_up=g_wup.reshape(NDEV, 2 * DFF // NDEV, D), ffn_conv_w=_cols_split(g_conv_ffn),
                ffn_w_down=g_wdown.reshape(NDEV, DFF // NDEV, D))
    dqkvn, dbg, *got = delta_bwd(qkvn, bg, states, do_dn, outbox(send))
    parts.update(zip(send, got))
    dba, da_log_pad, ddt_bias_pad = rowcall_bwd("dn_gate_bwd", f_gate, [(ba, LANE, 0)], [], [a_log_pad, dt_bias_pad],
                                                [(dbg, LANE, 0)], [(0, bf16)])
    ddqkv, g_conv_dn = dnconv_bwd(proj, conv_dn, dqkvn)
    send = dict(w_attn_branch=_cols_split(g_wa), w_dn_branch=_cols_split(g_wd),
                w_out=g_wo.reshape(NDEV, D // NDEV, D))
    dq, dk, dv, dbias, dsinks, *got = attn_bwd(proj, bias, sinks, dya, outbox(send))
    parts.update(zip(send, got))
    dproj = jnp.concatenate([dga, dgd, dq, ddqkv, dz, dk, dv, dba], axis=2).reshape(T, NP)
    g_wp = mm(dproj, u1.reshape(T, D), "tn", f32, "proj_dw", tm=1664, tk=1024)
    send = dict(w_in=_unpack_w_in(g_wp).reshape(NDEV, IN_DIM // NDEV, D), dn_conv_w=_cols_split(g_conv_dn))
    du1, *got = mm(dproj, wp, "nn", bf16, "proj_dx", tm=512, tk=NP, comm=outbox(send))
    parts.update(zip(send, got))
    du1 = du1.reshape(B, S, D)
    grad_x, dsc1, dsh1, dw_mix_pre = rowcall_bwd("mix_pre_bwd", f_rms_mod, [(x, D, 0)], [sc1, sh1], [norm_mix_pre], [(du1, D, 0)],
                                                 [(0, f32)], add=(dh1, D, 0))
    g_rel = mm(dbias.reshape(HQ, WIN * 2 * WIN), onehot, "nt", f32, "rel_bias_dw", tk=8192, precision=HI)

    dmod = jnp.concatenate([dsh1, dsc1, dg1, dsh2, dsc2, dg2], axis=2).reshape(B, NMOD * D)

    zrow = lambda a: jnp.concatenate([a.reshape(1, -1), jnp.zeros((B - 1, a.size), f32)], axis=0)
    small_g = jnp.concatenate([
        dmod, dw_mix_pre.reshape(B, D), dw_mix_post.reshape(B, D), dw_ffn_pre.reshape(B, D), dw_ffn_post.reshape(B, D),
        da_log_pad.reshape(B, LANE)[:, DNH:2 * DNH], ddt_bias_pad.reshape(B, LANE)[:, DNH:2 * DNH], dw_dn_norm.reshape(B, DND),
        zrow(dsinks), zrow(g_rel.T), loss_b.reshape(B, LANE)[:, :1], jnp.zeros((B, SMALL_PAD - SMALL_N - 1), f32)], axis=1)
    (small_all,) = _exchange([small_g], "gather_small")
    dmod_cols = lax.dynamic_slice_in_dim(small_all.reshape(NDEV * B, SMALL_PAD), me * ncol, ncol, axis=1)
    g_ada_w = ada_bwd(c_all, dmod_cols)
    small_w = dict(ada_b=(ada_b, m_ada_b, v_ada_b), norm_mix_pre=(norm_mix_pre, m_norm_mix_pre, v_norm_mix_pre),
                   norm_mix_post=(norm_mix_post, m_norm_mix_post, v_norm_mix_post), norm_ffn_pre=(norm_ffn_pre, m_norm_ffn_pre, v_norm_ffn_pre),
                   norm_ffn_post=(norm_ffn_post, m_norm_ffn_post, v_norm_ffn_post), dn_a_log=(dn_a_log, m_dn_a_log, v_dn_a_log),
                   dn_dt_bias=(dn_dt_bias, m_dn_dt_bias, v_dn_dt_bias), dn_norm_w=(dn_norm_w, m_dn_norm_w, v_dn_norm_w),
                   attn_sinks=(attn_sinks, m_attn_sinks, v_attn_sinks), rel_bias=(rel_bias, m_rel_bias, v_rel_bias))

    def pack(i, fill):
        row = jnp.concatenate([small_w[n][i].reshape(1, -1) for n, _ in SMALL], axis=1)
        return jnp.pad(row, ((0, 0), (0, SMALL_PAD - SMALL_N)), constant_values=fill)

    small_out = adamw(pack(0, 0.0), small_all.reshape(NDEV * B, 1, SMALL_PAD), pack(1, 0.0), pack(2, 1.0), "adamw_small")
    loss = small_out[0][0, SMALL_N]

    res = {}
    off = 0
    for n, size in SMALL:
        shp = small_w[n][0].shape
        res[n] = [o[:, off:off + size].reshape(shp) for o in small_out]
        off += size
    res["ada_w"] = [o[None] for o in adamw(ada_w[0], g_ada_w[None], m_ada_w[0], v_ada_w[0], "adamw_ada_w")]
    moments = dict(w_in=(m_w_in, v_w_in), dn_conv_w=(m_dn_conv_w, v_dn_conv_w), w_attn_branch=(m_w_attn_branch, v_w_attn_branch),
                   w_dn_branch=(m_w_dn_branch, v_w_dn_branch), w_out=(m_w_out, v_w_out), ffn_w_up=(m_ffn_w_up, v_ffn_w_up),
                   ffn_conv_w=(m_ffn_conv_w, v_ffn_conv_w), ffn_w_down=(m_ffn_w_down, v_ffn_w_down))
    for n in big_names:
        outs = adamw(local(n, big[n]), parts[n], local(n, moments[n][0]), local(n, moments[n][1]), "adamw_" + n)
        res[n] = [(o.T if n in transposed else o)[None] for o in outs]

    order = ["ada_w", "ada_b", "norm_mix_pre", "norm_mix_post", "norm_ffn_pre", "norm_ffn_post", "w_in", "dn_conv_w", "dn_a_log",
             "dn_dt_bias", "dn_norm_w", "attn_sinks", "rel_bias", "w_attn_branch", "w_dn_branch", "w_out", "ffn_w_up", "ffn_conv_w",
             "ffn_w_down"]
    return (loss, grad_x, *[res[n][0] for n in order], *[res[n][1] for n in order], *[res[n][2] for n in order],
            *[res[n][3] for n in order])
```

```python
import functools
import math

import numpy as np
import jax
import jax.numpy as jnp
from jax import lax
from jax.experimental import pallas as pl
from jax.experimental.pallas import tpu as pltpu

f32 = jnp.float32
bf16 = jnp.bfloat16
HI = lax.Precision.HIGHEST
MID = lax.Precision.HIGH
MESH = pl.DeviceIdType.MESH

NDEV = 8
D = 1024
HQ, HKV, HD, WIN, NBUCK, MAXDIST = 8, 2, 64, 128, 32, 128
DNH, DND, DNK, CH = 4, 128, 4, 64
DFF, FK = 2816, 3
NMOD = 6
RMS_EPS = 1e-6
L2_EPS = 1e-6
NEG_INF = -1e30
LR, B1, B2, EPS, WD, STEP = 0.001, 0.9, 0.999, 1e-08, 0.01, 10

LANE = 128
CB_GA, CB_GD, CB_AQ, CB_DQKV, CB_DZ, CB_AK, CB_AV, CB_BA, NPB = 0, 8, 16, 20, 32, 36, 37, 38, 39
NP = NPB * LANE
IN_SPLITS = (HQ * HD, HKV * HD, HKV * HD, 3 * DNH * DND, DNH * DND, DNH, DNH, D, D)
IN_DIM = sum(IN_SPLITS)
VMEM_LIMIT = 56 * 1024 * 1024

SMALL = (("ada_b", NMOD * D), ("norm_mix_pre", D), ("norm_mix_post", D), ("norm_ffn_pre", D), ("norm_ffn_post", D),
         ("dn_a_log", DNH), ("dn_dt_bias", DNH), ("dn_norm_w", DND), ("attn_sinks", HQ), ("rel_bias", NBUCK * HQ))
SMALL_N = sum(n for _, n in SMALL)
SMALL_PAD = 10752


def _cp(sem):
    return pltpu.CompilerParams(dimension_semantics=sem, vmem_limit_bytes=VMEM_LIMIT)


def _pick(dim, target):
    if dim <= target:
        return dim
    best = None
    for d in range(LANE, target + 1, LANE):
        if dim % d == 0:
            best = d
    assert best is not None, (dim, target)
    return best


def _me():
    x, y, c = lax.axis_index("x"), lax.axis_index("y"), lax.axis_index("c")
    return x, y, c, 4 * x + 2 * y + c


def _peer(x, y, c, k):
    px = 1 - x if k & 4 else x
    py = 1 - y if k & 2 else y
    pc = 1 - c if k & 1 else c
    return (px, py, pc), 4 * px + 2 * py + pc


class _Comm:
    def __init__(self, arrs, scatter=False, two_level=False):
        assert not (scatter and two_level)
        self.arrs, self.n, self.scatter, self.two_level = list(arrs), len(arrs), scatter, two_level
        if scatter:
            self.out_shape = [jax.ShapeDtypeStruct(a.shape, a.dtype) for a in arrs]
        else:
            self.out_shape = [jax.ShapeDtypeStruct((NDEV,) + a.shape, a.dtype) for a in arrs]
        nsem = self.n * (NDEV - 1)
        self.scratch = [pltpu.SemaphoreType.DMA((nsem,)), pltpu.SemaphoreType.DMA((nsem,)), pltpu.SemaphoreType.DMA((self.n,))]
        self.specs = [pl.BlockSpec(memory_space=pl.ANY)] * self.n

    def phases(self, ins, out, send, recv, loc):
        x, y, c, me = _me()

        def remote(a, k, src, dst, to):
            s = a * (NDEV - 1) + k - 1
            return pltpu.make_async_remote_copy(src_ref=src, dst_ref=dst, send_sem=send.at[s], recv_sem=recv.at[s],
                                                device_id=to, device_id_type=MESH)

        def local(a):
            return pltpu.make_async_copy(ins[a].at[me] if self.scatter else ins[a], out[a].at[me], loc.at[a])

        if not self.two_level:
            def mine(a, k):
                peer, pid = _peer(x, y, c, k)
                return remote(a, k, ins[a].at[pid] if self.scatter else ins[a], out[a].at[me], peer)

            def theirs(a, k):
                peer, pid = _peer(x, y, c, k)
                return remote(a, k, ins[a].at[pid] if self.scatter else ins[a], out[a].at[pid], peer)

            def start():
                for a in range(self.n):
                    local(a).start()
                    for k in range(1, NDEV):
                        mine(a, k).start()

            def forward():
                pass

            def finish():
                for a in range(self.n):
                    for k in range(1, NDEV):
                        mine(a, k).wait_send()
                    for k in range(1, NDEV):
                        theirs(a, k).wait_recv()
                    local(a).wait()

            return start, forward, finish

        sibling = (x, y, 1 - c)
        chips = [(1 - x, y), (x, 1 - y), (1 - x, 1 - y)]
        slot = lambda px, py, pc: 4 * px + 2 * py + pc

        def own(a, k, to):
            return remote(a, k, ins[a], out[a].at[me], to)

        def landed(a, k, frm):
            return remote(a, k, ins[a], out[a].at[slot(*frm)], frm)

        def passed(a, j):
            rows = out[a].at[slot(*chips[j], c)]
            return remote(a, 5 + j, rows, rows, sibling)

        def start():
            for a in range(self.n):
                local(a).start()
                own(a, 1, sibling).start()
                for j, chip in enumerate(chips):
                    own(a, 2 + j, (*chip, c)).start()

        def forward():
            for a in range(self.n):
                for j, chip in enumerate(chips):
                    landed(a, 2 + j, (*chip, c)).wait_recv()
                    passed(a, j).start()

        def finish():
            for a in range(self.n):
                landed(a, 1, sibling).wait_recv()
                for j, chip in enumerate(chips):
                    remote(a, 5 + j, ins[a], out[a].at[slot(*chip, 1 - c)], sibling).wait_recv()
                own(a, 1, sibling).wait_send()
                for j, chip in enumerate(chips):
                    own(a, 2 + j, (*chip, c)).wait_send()
                    passed(a, j).wait_send()
                local(a).wait()

        return start, forward, finish


def _ride(body, n_in, n_out, n_scr, comm, first, mid, last):
    k = comm.n

    def wrapped(*refs):
        ins, cins = refs[:n_in], refs[n_in:n_in + k]
        o0 = n_in + k
        outs, couts = refs[o0:o0 + n_out], refs[o0 + n_out:o0 + n_out + k]
        s0 = o0 + n_out + k
        scr, sems = refs[s0:s0 + n_scr], refs[s0 + n_scr:]
        start, forward, finish = comm.phases(cins, couts, *sems)
        pl.when(first())(start)
        body(*ins, *outs, *scr)
        pl.when(mid())(forward)
        pl.when(last())(finish)

    return wrapped


def _exchange(arrs, name, scatter=False, two_level=False):
    comm = _Comm(arrs, scatter, two_level)

    def body(*refs):
        start, forward, finish = comm.phases(refs[:comm.n], refs[comm.n:2 * comm.n], *refs[2 * comm.n:])
        start()
        forward()
        finish()

    return pl.pallas_call(body, name=name, out_shape=comm.out_shape, in_specs=comm.specs, out_specs=comm.specs,
                          scratch_shapes=comm.scratch, compiler_params=pltpu.CompilerParams(has_side_effects=True))(*arrs)


def mm(a, b, mode, out_dtype, name, tm=1024, tn=1024, tk=1024, precision=None, comm=None, b_cols=None):
    a_parts = a.shape[0] if a.ndim == 3 else 1
    b_parts = b.shape[0] if b.ndim == 3 else 1
    assert b_parts == 1 or mode == "tn"
    ash, bsh = (a.shape[-2], a.shape[-1] * a_parts), b.shape[-2:]
    if mode == "nn":
        (M, K), (K2, N) = ash, bsh
    elif mode == "nt":
        (M, K), (N, K2) = ash, bsh
    else:
        (K, M), (K2, N) = ash, (bsh[0], bsh[1] * b_parts)
    assert K == K2, (name, a.shape, b.shape)
    col0 = 0
    if b_cols is not None:
        assert mode in ("nn", "nt") and tn % LANE == 0
        col0, N = b_cols[0], b_cols[1] * tn
    if mode == "tn":
        tm, tn, tk = _pick(M // a_parts, tm), _pick(N // b_parts, tn), _pick(K, tk)
    else:
        tm, tn, tk = _pick(M, tm), _pick(N // b_parts, tn), _pick(K // a_parts, tk)
    nk = K // tk
    if mode == "tn" and a_parts > 1:
        per = M // tm // a_parts
        a_spec = pl.BlockSpec((None, tk, tm), lambda i, j, k: (i // per, k, i % per))
    elif mode == "tn":
        a_spec = pl.BlockSpec((tk, tm), lambda i, j, k: (k, i))
    elif a_parts > 1:
        per = nk // a_parts
        a_spec = pl.BlockSpec((None, tm, tk), lambda i, j, k: (k // per, i, k % per))
    else:
        a_spec = pl.BlockSpec((tm, tk), lambda i, j, k: (i, k))
    if mode == "nt":
        b_spec = pl.BlockSpec((tn, tk), lambda i, j, k: (col0 + j, k))
    elif b_parts > 1:
        per = N // tn // b_parts
        b_spec = pl.BlockSpec((None, tk, tn), lambda i, j, k: (j // per, k, j % per))
    else:
        b_spec = pl.BlockSpec((tk, tn), lambda i, j, k: (k, col0 + j))
    dims = {"nn": ((1,), (0,)), "nt": ((1,), (1,)), "tn": ((0,), (0,))}[mode]

    def body(a_ref, b_ref, o_ref, *scr):
        p = lax.dot_general(a_ref[...], b_ref[...], (dims, ((), ())), preferred_element_type=f32, precision=precision)
        if nk == 1:
            o_ref[...] = p.astype(o_ref.dtype)
        else:
            acc = scr[0]
            k = pl.program_id(2)

            @pl.when(k == 0)
            def _():
                acc[...] = p

            @pl.when(k > 0)
            def _():
                acc[...] += p

            @pl.when(k == nk - 1)
            def _():
                o_ref[...] = acc[...].astype(o_ref.dtype)

    grid = (M // tm, N // tn, nk)
    scratch = [pltpu.VMEM((tm, tn), f32)] if nk > 1 else []
    out_spec = pl.BlockSpec((tm, tn), lambda i, j, k: (i, j))
    out_shape = jax.ShapeDtypeStruct((M, N), out_dtype)
    if comm is None:
        return pl.pallas_call(body, name=name, grid=grid, in_specs=[a_spec, b_spec], out_specs=out_spec, out_shape=out_shape,
                              scratch_shapes=scratch, compiler_params=_cp(("parallel", "parallel", "arbitrary")))(a, b)
    at = lambda pos: lambda: functools.reduce(jnp.logical_and, [pl.program_id(d) == p for d, p in enumerate(pos)])
    end = tuple(g - 1 for g in grid)
    return pl.pallas_call(
        _ride(body, 2, 1, len(scratch), comm, at((0, 0, 0)), at(end), at(end)), name=name, grid=grid,
        in_specs=[a_spec, b_spec] + comm.specs, out_specs=[out_spec] + comm.specs, out_shape=[out_shape] + comm.out_shape,
        scratch_shapes=scratch + comm.scratch, compiler_params=_cp(("arbitrary", "arbitrary", "arbitrary")),
    )(a, b, *comm.arrs)


def rowcall(name, fn, tok, bat, con, tok_out, acc_out, ts=256):
    B, S = tok[0][0].shape[:2]
    ts = min(ts, S)
    nt, nb, nc, no, na = len(tok), len(bat), len(con), len(tok_out), len(acc_out)

    def body(*refs):
        tr, br, cr = refs[:nt], refs[nt:nt + nb], refs[nt + nb:nt + nb + nc]
        orf, arf = refs[nt + nb + nc:nt + nb + nc + no], refs[nt + nb + nc + no:]
        touts, aouts = fn([r[0] for r in tr], [r[0] for r in br], [r[...] for r in cr])
        for r, v in zip(orf, touts):
            r[0] = v.astype(r.dtype)
        s = pl.program_id(1)
        for r, v in zip(arf, aouts):
            @pl.when(s == 0)
            def _(r=r):
                r[...] = jnp.zeros(r.shape, r.dtype)
            r[0] += v.astype(f32)

    in_specs = [pl.BlockSpec((1, ts, w), lambda b, s, cb=cb: (b, s, cb)) for (_, w, cb) in tok]
    in_specs += [pl.BlockSpec((1,) + a.shape[1:], lambda b, s: (b, 0, 0)) for a in bat]
    in_specs += [pl.BlockSpec(a.shape, lambda b, s, nd=a.ndim: (0,) * nd) for a in con]
    out_specs = [pl.BlockSpec((1, ts, w), lambda b, s: (b, s, 0)) for (w, _) in tok_out]
    out_specs += [pl.BlockSpec((1,) + shp, lambda b, s, nd=len(shp): (b,) + (0,) * nd) for shp in acc_out]
    out_shape = [jax.ShapeDtypeStruct((B, S, w), dt) for (w, dt) in tok_out]
    out_shape += [jax.ShapeDtypeStruct((B,) + shp, f32) for shp in acc_out]
    return pl.pallas_call(
        body, name=name, grid=(B, S // ts), in_specs=in_specs, out_specs=out_specs, out_shape=out_shape,
        compiler_params=_cp(("parallel", "arbitrary")),
    )(*[t[0] for t in tok], *bat, *con)


def rowcall_fwd(name, f, tok, bat, con, tok_out, ts=256):
    def fn(t, b, c):
        return f([v.astype(f32) for v in t], b, c), []
    return rowcall(name, fn, tok, bat, con, tok_out, [], ts)


def rowcall_bwd(name, f, tok, bat, con, cts, tok_grads, add=None, ts=256):
    nt, ncts = len(tok), len(cts)

    def fn(t, b, c):
        prim = [v.astype(f32) for v in t[:nt]]
        ct = [v.astype(f32) for v in t[nt:nt + ncts]]
        _, vjp = jax.vjp(lambda tt, bb, cc: f(tt, bb, cc), prim, b, c)
        dt, db, dc = vjp(ct)
        touts = [dt[i] for i, _ in tok_grads]
        if add is not None:
            touts[0] = touts[0] + t[nt + ncts].astype(f32)
        return touts, list(db) + list(dc)

    all_tok = list(tok) + list(cts) + ([add] if add is not None else [])
    tok_out = [(tok[i][1], dt) for i, dt in tok_grads]
    acc_out = [tuple(a.shape[1:]) for a in bat] + [tuple(a.shape) for a in con]
    return rowcall(name, fn, all_tok, bat, con, tok_out, acc_out, ts)


def _rms(y, w):
    return y * lax.rsqrt(jnp.mean(y * y, axis=-1, keepdims=True) + RMS_EPS) * w


def f_rms_mod(t, b, c):
    return [_rms(t[0], c[0]) * (1.0 + b[0]) + b[1]]


def f_post_pre(t, b, c):
    h1 = t[0] + b[0] * _rms(t[1], c[0])
    return [h1, _rms(h1, c[1]) * (1.0 + b[1]) + b[2]]


def f_merge(t, b, c):
    ga, gd, ya, yd = t
    return [jax.nn.sigmoid(ga) * ya + jax.nn.sigmoid(gd) * yd]


def f_dnout(t, b, c):
    o, z = t
    outs = []
    for h in range(DNH):
        sl = slice(h * DND, (h + 1) * DND)
        zh = z[:, sl]
        outs.append(_rms(o[:, sl], c[0]) * (zh * jax.nn.sigmoid(zh)))
    return [jnp.concatenate(outs, axis=1)]


def _softplus(x):
    return jnp.maximum(x, 0.0) + jnp.log(1.0 + jnp.exp(-jnp.abs(x)))


def f_gate(t, b, c):
    ba = t[0]
    a_log, dt_bias = c
    lane = lax.broadcasted_iota(jnp.int32, ba.shape, 1)
    beta = jax.nn.sigmoid(ba)
    g = -jnp.exp(a_log) * _softplus(ba + dt_bias)
    return [jnp.where(lane < DNH, beta, jnp.where(lane < 2 * DNH, g, 0.0))]


def _bucket_table():
    qi = np.arange(WIN)[:, None]
    kj = np.arange(2 * WIN)[None, :]
    dist = np.maximum(WIN + qi - kj, 0)
    max_exact = NBUCK // 2
    scaled = np.log(np.maximum(dist, 1).astype(np.float64) / max_exact) / math.log(MAXDIST / max_exact)
    large = np.minimum(max_exact + (scaled * (NBUCK - max_exact)).astype(np.int32), NBUCK - 1)
    return np.where(dist < max_exact, dist, large).astype(np.int32)


def _attn_mask(n):
    qi = lax.broadcasted_iota(jnp.int32, (WIN, 2 * WIN), 0)
    kj = lax.broadcasted_iota(jnp.int32, (WIN, 2 * WIN), 1)
    dist = WIN + qi - kj
    return (dist >= 0) & (dist < WIN) & ((kj >= WIN) | (n > 0))


def _swap_halves(x):
    return pltpu.roll(x, HD, axis=x.ndim - 1)


@jax.custom_vjp
def _swap_halves_vjp(x):
    return _swap_halves(x)


_swap_halves_vjp.defvjp(lambda x: (_swap_halves(x), None), lambda _, g: (_swap_halves(g),))


def _attn_block(q, kp, kc, vp, vc, bias, sinks, mask, differentiated):
    dot = _bdot_bf16_vjp if differentiated else _bdot_bf16
    swap = _swap_halves_vjp if differentiated else _swap_halves
    B, grp = q.shape[0], HQ // HKV
    upper = lax.broadcasted_iota(jnp.int32, (2 * WIN, LANE), 1) >= HD

    def placed(natural, swapped, j, half):
        keep = upper if half == 1 else ~upper
        return jnp.where(keep, natural if j == half else swapped, 0.0)

    qh, ks, vs = [], [], []
    for b in range(B):
        kb, vb = jnp.concatenate([kp[b], kc[b]], axis=0), jnp.concatenate([vp[b], vc[b]], axis=0)
        kb_sw, vb_sw = swap(kb), swap(vb)
        for h in range(HQ):
            qh.append(q[b, :, (h // 2) * LANE:(h // 2 + 1) * LANE])
            ks.append(placed(kb, kb_sw, h // grp, h % 2))
            vs.append(placed(vb, vb_sw, h // grp, h % 2))
    s = dot(_stack(qh), _stack(ks), 2, 2).reshape(B, HQ, WIN, 2 * WIN) * (HD ** -0.5)
    s = jnp.where(mask, s + bias, NEG_INF)
    m = jnp.maximum(jnp.max(s, axis=-1, keepdims=True), sinks)
    p = jnp.exp(s - m)
    probs = p / (jnp.sum(p, axis=-1, keepdims=True) + jnp.exp(sinks - m))
    o = dot(probs.reshape(B * HQ, WIN, 2 * WIN), _stack(vs), 2, 1)
    return _stack([jnp.concatenate([o[b * HQ + 2 * i] + o[b * HQ + 2 * i + 1] for i in range(HQ // 2)], axis=1) for b in range(B)])


def _attn_specs(B, NB):
    last = NB - 1
    return [
        pl.BlockSpec((B, WIN, HQ * HD), lambda n: (0, jnp.minimum(n, last), CB_AQ // 4)),
        pl.BlockSpec((B, WIN, LANE), lambda n: (0, jnp.clip(n - 1, 0, last), CB_AK)),
        pl.BlockSpec((B, WIN, LANE), lambda n: (0, jnp.minimum(n, last), CB_AK)),
        pl.BlockSpec((B, WIN, LANE), lambda n: (0, jnp.clip(n - 1, 0, last), CB_AV)),
        pl.BlockSpec((B, WIN, LANE), lambda n: (0, jnp.minimum(n, last), CB_AV)),
        pl.BlockSpec((HQ, WIN, 2 * WIN), lambda n: (0, 0, 0)),
        pl.BlockSpec((HQ, 1, 1), lambda n: (0, 0, 0)),
    ]


def attn_fwd(proj, bias, sinks, comm):
    B, S, _ = proj.shape
    NB = S // WIN

    def body(q, kp, kc, vp, vc, bias_ref, sink_ref, o_ref):
        mask = _attn_mask(pl.program_id(0))
        o = _attn_block(*[r[...].astype(f32) for r in (q, kp, kc, vp, vc)], bias_ref[...], sink_ref[...], mask, False)
        o_ref[...] = o.astype(o_ref.dtype)

    at = lambda n: lambda: pl.program_id(0) == n
    return pl.pallas_call(
        _ride(body, 7, 1, 0, comm, at(0), at((3 * NB) // 4), at(NB - 1)), name="attn_fwd", grid=(NB,),
        in_specs=_attn_specs(B, NB) + comm.specs,
        out_specs=[pl.BlockSpec((B, WIN, HQ * HD), lambda n: (0, n, 0))] + comm.specs,
        out_shape=[jax.ShapeDtypeStruct((B, S, HQ * HD), bf16)] + comm.out_shape, scratch_shapes=comm.scratch,
        compiler_params=_cp(("arbitrary",)),
    )(proj, proj, proj, proj, proj, bias, sinks, *comm.arrs)


def attn_bwd(proj, bias, sinks, dy, comm):
    B, S, _ = proj.shape
    NB = S // WIN
    last = NB - 1

    def body(q, kp, kc, vp, vc, bias_ref, sink_ref, dy_ref, dq_ref, dk_ref, dv_ref, dbias_ref, dsink_ref, kcar, vcar):
        n = pl.program_id(0)

        @pl.when(n == 0)
        def _():
            dbias_ref[...] = jnp.zeros(dbias_ref.shape, f32)
            dsink_ref[...] = jnp.zeros(dsink_ref.shape, f32)
            kcar[...] = jnp.zeros(kcar.shape, f32)
            vcar[...] = jnp.zeros(vcar.shape, f32)

        @pl.when(n < NB)
        def _():
            mask = _attn_mask(n)
            _, vjp = jax.vjp(lambda *a: _attn_block(*a, mask, True), *[r[...].astype(f32) for r in (q, kp, kc, vp, vc)],
                             bias_ref[...], sink_ref[...])
            dq, dkp, dkc, dvp, dvc, dbias, dsink = vjp(dy_ref[...].astype(f32))
            dq_ref[...] = dq.astype(dq_ref.dtype)
            dbias_ref[...] += dbias
            dsink_ref[...] += dsink
            dk_ref[...] = (kcar[...] + dkp).astype(dk_ref.dtype)
            dv_ref[...] = (vcar[...] + dvp).astype(dv_ref.dtype)
            kcar[...] = dkc
            vcar[...] = dvc

        @pl.when(n == NB)
        def _():
            dk_ref[...] = kcar[...].astype(dk_ref.dtype)
            dv_ref[...] = vcar[...].astype(dv_ref.dtype)

    in_specs = _attn_specs(B, NB) + [pl.BlockSpec((B, WIN, HQ * HD), lambda n: (0, jnp.minimum(n, last), 0))]
    kv_out = pl.BlockSpec((B, WIN, LANE), lambda n: (0, jnp.maximum(n - 1, 0), 0))
    at = lambda n: lambda: pl.program_id(0) == n
    return pl.pallas_call(
        _ride(body, 8, 5, 2, comm, at(0), at(NB), at(NB)), name="attn_bwd", grid=(NB + 1,),
        in_specs=in_specs + comm.specs,
        out_specs=[pl.BlockSpec((B, WIN, HQ * HD), lambda n: (0, jnp.minimum(n, last), 0)), kv_out, kv_out,
                   pl.BlockSpec((HQ, WIN, 2 * WIN), lambda n: (0, 0, 0)), pl.BlockSpec((HQ, 1, 1), lambda n: (0, 0, 0))] + comm.specs,
        out_shape=[jax.ShapeDtypeStruct((B, S, HQ * HD), bf16), jax.ShapeDtypeStruct((B, S, LANE), bf16),
                   jax.ShapeDtypeStruct((B, S, LANE), bf16), jax.ShapeDtypeStruct((HQ, WIN, 2 * WIN), f32),
                   jax.ShapeDtypeStruct((HQ, 1, 1), f32)] + comm.out_shape,
        scratch_shapes=[pltpu.VMEM((B, WIN, LANE), f32), pltpu.VMEM((B, WIN, LANE), f32)] + comm.scratch,
        compiler_params=_cp(("arbitrary",)),
    )(proj, proj, proj, proj, proj, bias, sinks, dy, *comm.arrs)


DN_ROWS, FFN_ROWS = 256, 32


def _stage_rows(dst, value):
    dst[0:8] = jnp.zeros((8, LANE), f32)
    dst[8:8 + value.shape[0]] = value


def _conv_rows(xs, w, width, r, rows):
    wins = [xs[pl.ds(r + 8 - (width - 1) + j, rows), :] for j in range(width)]
    out = w[0:1] * wins[0]
    for j in range(1, width):
        out = out + w[j:j + 1] * wins[j]
    return out, wins


def _fold8(v):
    return jnp.sum(v.reshape(v.shape[0] // 8, 8, LANE), axis=0)


def _conv_rows_t(ds, w, width, r, rows):
    out = w[0:1] * ds[pl.ds(r + width - 1, rows), :]
    for j in range(1, width):
        out = out + w[j:j + 1] * ds[pl.ds(r + width - 1 - j, rows), :]
    return out


def _dn_outblk(i):
    return (i % DNH) * 3 + i // DNH


def _dn_act(c, isqk):
    sg = jax.nn.sigmoid(c)
    y = c * sg
    n = lax.rsqrt(jnp.sum(y * y, axis=-1, keepdims=True) + L2_EPS)
    return jnp.where(isqk, y * n, y), sg, n


def dnconv_fwd(proj, conv_w):
    B, S, _ = proj.shape
    rows = min(DN_ROWS, S)

    def body(x_ref, w_ref, o_ref, xs):
        isqk = pl.program_id(0) < 2 * DNH
        _stage_rows(xs, x_ref[0].astype(f32))
        w = w_ref[...]
        for r in range(0, S, rows):
            c, _ = _conv_rows(xs, w, DNK, r, rows)
            o_ref[0, pl.ds(r, rows), :] = _dn_act(c, isqk)[0]

    return pl.pallas_call(
        body, name="dnconv_fwd", grid=(3 * DNH, B),
        in_specs=[pl.BlockSpec((1, S, LANE), lambda i, b: (b, 0, CB_DQKV + i)), pl.BlockSpec((DNK, LANE), lambda i, b: (0, i))],
        out_specs=pl.BlockSpec((1, S, LANE), lambda i, b: (b, 0, _dn_outblk(i))),
        out_shape=jax.ShapeDtypeStruct((B, S, 3 * DNH * DND), f32), scratch_shapes=[pltpu.VMEM((S + 8, LANE), f32)],
        compiler_params=_cp(("parallel", "parallel")),
    )(proj, conv_w)


def dnconv_bwd(proj, conv_w, dqkvn):
    B, S, _ = proj.shape
    rows = min(DN_ROWS, S)

    def body(x_ref, w_ref, dy_ref, dx_ref, dw_ref, xs, ds):
        isqk = pl.program_id(0) < 2 * DNH
        _stage_rows(xs, x_ref[0].astype(f32))
        w = w_ref[...]
        dw = [jnp.zeros((8, LANE), f32) for _ in range(DNK)]
        for r in range(0, S, rows):
            c, wins = _conv_rows(xs, w, DNK, r, rows)
            out, sg, n = _dn_act(c, isqk)
            dout = dy_ref[0, pl.ds(r, rows), :]
            dy = jnp.where(isqk, n * (dout - out * jnp.sum(dout * out, axis=-1, keepdims=True)), dout)
            dc = dy * (sg * (1.0 + c * (1.0 - sg)))
            ds[pl.ds(r, rows), :] = dc
            for j in range(DNK):
                dw[j] = dw[j] + _fold8(dc * wins[j])
        ds[S:S + 8] = jnp.zeros((8, LANE), f32)
        for r in range(0, S, rows):
            dx_ref[0, pl.ds(r, rows), :] = _conv_rows_t(ds, w, DNK, r, rows).astype(dx_ref.dtype)

        @pl.when(pl.program_id(1) == 0)
        def _():
            dw_ref[...] = jnp.zeros(dw_ref.shape, f32)
        dw_ref[...] += jnp.concatenate([jnp.sum(d, axis=0, keepdims=True) for d in dw], axis=0)

    return pl.pallas_call(
        body, name="dnconv_bwd", grid=(3 * DNH, B),
        in_specs=[pl.BlockSpec((1, S, LANE), lambda i, b: (b, 0, CB_DQKV + i)), pl.BlockSpec((DNK, LANE), lambda i, b: (0, i)),
                  pl.BlockSpec((1, S, LANE), lambda i, b: (b, 0, _dn_outblk(i)))],
        out_specs=[pl.BlockSpec((1, S, LANE), lambda i, b: (b, 0, i)), pl.BlockSpec((DNK, LANE), lambda i, b: (0, i))],
        out_shape=[jax.ShapeDtypeStruct((B, S, 3 * DNH * DND), bf16), jax.ShapeDtypeStruct((DNK, 3 * DNH * DND), f32)],
        scratch_shapes=[pltpu.VMEM((S + 8, LANE), f32), pltpu.VMEM((S + 8, LANE), f32)],
        compiler_params=_cp(("parallel", "arbitrary")),
    )(proj, conv_w, dqkvn)


def _bdot(a, b, ca, cb, precision=HI):
    return lax.dot_general(a, b, (((ca,), (cb,)), ((0,), (0,))), preferred_element_type=f32, precision=precision)


def _bdot_bf16(a, b, ca, cb):
    return _bdot(a.astype(bf16), b.astype(bf16), ca, cb, None)


@functools.partial(jax.custom_vjp, nondiff_argnums=(2, 3))
def _bdot_bf16_vjp(a, b, ca, cb):
    return _bdot_bf16(a, b, ca, cb)


def _bdot_bf16_fwd(a, b, ca, cb):
    return _bdot_bf16(a, b, ca, cb), (a, b)


def _bdot_bf16_bwd(ca, cb, res, g):
    a, b = res
    fa, fb = 3 - ca, 3 - cb
    da = _bdot_bf16(g, b, 2, fb) if ca == 2 else _bdot_bf16(b, g, fb, 2)
    db = _bdot_bf16(a, g, fa, 1) if cb == 1 else _bdot_bf16(g, a, 1, fa)
    return da, db


_bdot_bf16_vjp.defvjp(_bdot_bf16_fwd, _bdot_bf16_bwd)


def _neumann_inverse(low):
    n = low.shape[-1]
    eye = (lax.broadcasted_iota(jnp.int32, (n, n), 0) == lax.broadcasted_iota(jnp.int32, (n, n), 1)).astype(f32)
    p = -low
    x = eye[None] + p
    for _ in range(5):
        p = _bdot(p, p, 2, 1, MID)
        x = x + _bdot(x, p, 2, 1, MID)
    return x


@jax.custom_vjp
def _unit_lower_inverse(low):
    return _neumann_inverse(low)


def _uli_fwd(low):
    t = _neumann_inverse(low)
    return t, t


def _uli_bwd(t, dt):
    return (-_bdot(_bdot(t, dt, 1, 1, MID), t, 2, 2, MID),)


_unit_lower_inverse.defvjp(_uli_fwd, _uli_bwd)


def _stack(xs):
    return jnp.concatenate([x[None] for x in xs], axis=0)


DELTA_CHUNKS = 2


def _delta_chunks(qkv, bg, state, differentiated):
    inverse = _unit_lower_inverse if differentiated else _neumann_inverse
    lo = _bdot_bf16_vjp if differentiated else _bdot_bf16
    B, n = qkv.shape[0], qkv.shape[1] // CH
    G = B * DNH
    N = n * G
    triples = [(i, b, h) for i in range(n) for b in range(B) for h in range(DNH)]
    col = lambda i, b, h, kind: qkv[b, i * CH:(i + 1) * CH, (3 * h + kind) * DND:(3 * h + kind + 1) * DND]
    q, k, v = [_stack([col(i, b, h, kind) for i, b, h in triples]) for kind in range(3)]
    lane = lax.broadcasted_iota(jnp.int32, (CH, LANE), 1)
    pick = lambda i, b, l: jnp.sum(jnp.where(lane == l, bg[b, i * CH:(i + 1) * CH], 0.0), axis=1, keepdims=True)
    beta = _stack([pick(i, b, h) for i, b, h in triples])
    g = _stack([pick(i, b, h + DNH) for i, b, h in triples])
    ri = lax.broadcasted_iota(jnp.int32, (CH, CH), 0)
    ci = lax.broadcasted_iota(jnp.int32, (CH, CH), 1)
    incl, strict = (ri >= ci)[None], (ri > ci)[None]
    gc = _bdot(jnp.broadcast_to(incl.astype(f32), (N, CH, CH)), jnp.broadcast_to(g, (N, CH, LANE)), 2, 1, MID)
    e0 = jnp.broadcast_to((lane == 0).astype(f32)[None], (N, CH, LANE))
    gc_row = _bdot(e0, gc, 2, 2, MID)
    diff = gc[:, :, :CH] - gc_row
    decay = jnp.where(incl, jnp.exp(jnp.where(incl, diff, 0.0)), 0.0)
    qs = q * (DND ** -0.5)
    kb, vb = k * beta, v * beta
    eg = jnp.exp(gc)
    with_k = lo(jnp.concatenate([kb, qs], axis=1), k, 2, 2)
    low = jnp.where(strict, with_k[:, :CH] * decay, 0.0)
    intra = jnp.where(incl, with_k[:, CH:] * decay, 0.0)
    tinv = inverse(low)
    solved = _bdot(tinv, jnp.concatenate([vb, kb * eg], axis=2), 2, 1, MID)
    gl = gc[:, CH - 1:CH, :]
    k_tail = k * jnp.exp(gl - gc)
    to_state = jnp.concatenate([solved[:, :, DND:], qs * eg], axis=1)
    decay_all = jnp.exp(gl)
    outs = []
    for i in range(n):
        sl = slice(i * G, (i + 1) * G)
        with_state = lo(to_state[sl], state, 2, 1)
        v_new = solved[sl, :, :DND] - with_state[:, :CH]
        outs.append(with_state[:, CH:] + lo(intra[sl], v_new, 2, 1))
        state = state * decay_all[sl] + lo(k_tail[sl], v_new, 1, 1)
    return outs, state


def delta_fwd(qkvn, bg, comm):
    B, S, _ = qkvn.shape
    n = DELTA_CHUNKS if (S // CH) % DELTA_CHUNKS == 0 else 1
    steps, G, rows = S // (n * CH), B * DNH, n * CH

    def body(qkv_ref, bg_ref, o_ref, st_ref, state):
        @pl.when(pl.program_id(0) == 0)
        def _():
            state[...] = jnp.zeros(state.shape, f32)
        s0 = state[...]
        st_ref[0] = s0
        outs, s1 = _delta_chunks(qkv_ref[...], bg_ref[...], s0, False)
        for i, o in enumerate(outs):
            for b in range(B):
                for h in range(DNH):
                    o_ref[b, i * CH:(i + 1) * CH, h * DND:(h + 1) * DND] = o[b * DNH + h]
        state[...] = s1

    at = lambda c: lambda: pl.program_id(0) == c
    return pl.pallas_call(
        _ride(body, 2, 2, 1, comm, at(0), at((7 * steps) // 8), at(steps - 1)), name="delta_fwd", grid=(steps,),
        in_specs=[pl.BlockSpec((B, rows, 3 * DNH * DND), lambda c: (0, c, 0)), pl.BlockSpec((B, rows, LANE), lambda c: (0, c, 0))] + comm.specs,
        out_specs=[pl.BlockSpec((B, rows, DNH * DND), lambda c: (0, c, 0)), pl.BlockSpec((1, G, DND, DND), lambda c: (c, 0, 0, 0))] + comm.specs,
        out_shape=[jax.ShapeDtypeStruct((B, S, DNH * DND), f32), jax.ShapeDtypeStruct((steps, G, DND, DND), f32)] + comm.out_shape,
        scratch_shapes=[pltpu.VMEM((G, DND, DND), f32)] + comm.scratch, compiler_params=_cp(("arbitrary",)),
    )(qkvn, bg, *comm.arrs)


def delta_bwd(qkvn, bg, states, do, comm):
    B, S, _ = qkvn.shape
    steps, G = states.shape[0], B * DNH
    rows = S // steps
    n = rows // CH

    def body(qkv_ref, bg_ref, st_ref, do_ref, dqkv_ref, dbg_ref, dstate):
        @pl.when(pl.program_id(0) == 0)
        def _():
            dstate[...] = jnp.zeros(dstate.shape, f32)
        _, vjp = jax.vjp(lambda a, g, s: _delta_chunks(a, g, s, True), qkv_ref[...], bg_ref[...], st_ref[0])
        do = [_stack([do_ref[b, i * CH:(i + 1) * CH, h * DND:(h + 1) * DND] for b in range(B) for h in range(DNH)]) for i in range(n)]
        dqkv, dbg, ds = vjp((do, dstate[...]))
        dqkv_ref[...] = dqkv
        dbg_ref[...] = dbg
        dstate[...] = ds

    rev = lambda c: steps - 1 - c
    at = lambda c: lambda: pl.program_id(0) == c
    return pl.pallas_call(
        _ride(body, 4, 2, 1, comm, at(0), at(steps - 1), at(steps - 1)), name="delta_bwd", grid=(steps,),
        in_specs=[pl.BlockSpec((B, rows, 3 * DNH * DND), lambda c: (0, rev(c), 0)), pl.BlockSpec((B, rows, LANE), lambda c: (0, rev(c), 0)),
                  pl.BlockSpec((1, G, DND, DND), lambda c: (rev(c), 0, 0, 0)),
                  pl.BlockSpec((B, rows, DNH * DND), lambda c: (0, rev(c), 0))] + comm.specs,
        out_specs=[pl.BlockSpec((B, rows, 3 * DNH * DND), lambda c: (0, rev(c), 0)),
                   pl.BlockSpec((B, rows, LANE), lambda c: (0, rev(c), 0))] + comm.specs,
        out_shape=[jax.ShapeDtypeStruct((B, S, 3 * DNH * DND), f32), jax.ShapeDtypeStruct((B, S, LANE), f32)] + comm.out_shape,
        scratch_shapes=[pltpu.VMEM((G, DND, DND), f32)] + comm.scratch, compiler_params=_cp(("arbitrary",)),
    )(qkvn, bg, states, do, *comm.arrs)


GELU_C0, GELU_C1 = math.sqrt(2.0 / math.pi), 0.044715


def _ffn_specs(S):
    nblk = DFF // LANE
    return [pl.BlockSpec((1, S, LANE), lambda i, b: (b, 0, i)), pl.BlockSpec((1, S, LANE), lambda i, b: (b, 0, nblk + i)),
            pl.BlockSpec((FK, LANE), lambda i, b: (0, i)), pl.BlockSpec((FK, LANE), lambda i, b: (0, nblk + i))]


def ffnconv_fwd(up, conv_w):
    B, S, _ = up.shape
    rows = min(FFN_ROWS, S)

    def body(g_ref, v_ref, gw_ref, vw_ref, o_ref, xg, xv):
        _stage_rows(xg, g_ref[0].astype(f32))
        _stage_rows(xv, v_ref[0].astype(f32))
        gw, vw = gw_ref[...], vw_ref[...]
        for r in range(0, S, rows):
            g, _ = _conv_rows(xg, gw, FK, r, rows)
            v, _ = _conv_rows(xv, vw, FK, r, rows)
            t = jnp.tanh(GELU_C0 * (g * (1.0 + GELU_C1 * (g * g))))
            o_ref[0, pl.ds(r, rows), :] = (0.5 * g * (1.0 + t) * v).astype(o_ref.dtype)

    return pl.pallas_call(
        body, name="ffnconv_fwd", grid=(DFF // LANE, B), in_specs=_ffn_specs(S),
        out_specs=pl.BlockSpec((1, S, LANE), lambda i, b: (b, 0, i)), out_shape=jax.ShapeDtypeStruct((B, S, DFF), bf16),
        scratch_shapes=[pltpu.VMEM((S + 8, LANE), f32)] * 2, compiler_params=_cp(("parallel", "parallel")),
    )(up, up, conv_w, conv_w)


def ffnconv_bwd(up, conv_w, dact):
    B, S, _ = up.shape
    rows = min(FFN_ROWS, S)

    def body(g_ref, v_ref, gw_ref, vw_ref, dy_ref, dx_ref, dw_ref, xg, xv, dg, dv):
        _stage_rows(xg, g_ref[0].astype(f32))
        _stage_rows(xv, v_ref[0].astype(f32))
        gw, vw = gw_ref[...], vw_ref[...]
        dgw = [jnp.zeros((8, LANE), f32) for _ in range(FK)]
        dvw = [jnp.zeros((8, LANE), f32) for _ in range(FK)]
        for r in range(0, S, rows):
            g, gwins = _conv_rows(xg, gw, FK, r, rows)
            v, vwins = _conv_rows(xv, vw, FK, r, rows)
            g2 = g * g
            t = jnp.tanh(GELU_C0 * (g * (1.0 + GELU_C1 * g2)))
            half = 0.5 * (1.0 + t)
            dgelu = half + (0.5 * GELU_C0) * g * (1.0 - t * t) * (1.0 + (3.0 * GELU_C1) * g2)
            dy = dy_ref[0, pl.ds(r, rows), :].astype(f32)
            dvc = dy * (g * half)
            dgc = dy * v * dgelu
            dg[pl.ds(r, rows), :] = dgc
            dv[pl.ds(r, rows), :] = dvc
            for j in range(FK):
                dgw[j] = dgw[j] + _fold8(dgc * gwins[j])
                dvw[j] = dvw[j] + _fold8(dvc * vwins[j])
        dg[S:S + 8] = jnp.zeros((8, LANE), f32)
        dv[S:S + 8] = jnp.zeros((8, LANE), f32)
        for r in range(0, S, rows):
            dx_ref[0, 0, pl.ds(r, rows), :] = _conv_rows_t(dg, gw, FK, r, rows).astype(dx_ref.dtype)
            dx_ref[1, 0, pl.ds(r, rows), :] = _conv_rows_t(dv, vw, FK, r, rows).astype(dx_ref.dtype)

        @pl.when(pl.program_id(1) == 0)
        def _():
            dw_ref[...] = jnp.zeros(dw_ref.shape, f32)
        dw_ref[0] += jnp.concatenate([jnp.sum(d, axis=0, keepdims=True) for d in dgw], axis=0)
        dw_ref[1] += jnp.concatenate([jnp.sum(d, axis=0, keepdims=True) for d in dvw], axis=0)

    return pl.pallas_call(
        body, name="ffnconv_bwd", grid=(DFF // LANE, B),
        in_specs=_ffn_specs(S) + [pl.BlockSpec((1, S, LANE), lambda i, b: (b, 0, i))],
        out_specs=[pl.BlockSpec((2, 1, S, LANE), lambda i, b: (0, b, 0, i)), pl.BlockSpec((2, FK, LANE), lambda i, b: (0, 0, i))],
        out_shape=[jax.ShapeDtypeStruct((2, B, S, DFF), bf16), jax.ShapeDtypeStruct((2, FK, DFF), f32)],
        scratch_shapes=[pltpu.VMEM((S + 8, LANE), f32)] * 4, compiler_params=_cp(("parallel", "arbitrary")),
    )(up, up, conv_w, conv_w, dact)


def ada_fwd(c_all, ada_w, ada_b):
    def body(c_ref, w_ref, b_ref, o_ref):
        c = c_ref[...]
        act = (c * jax.nn.sigmoid(c)).astype(bf16)
        o_ref[...] = jnp.dot(act, w_ref[...].astype(bf16), preferred_element_type=f32) + b_ref[...]

    return pl.pallas_call(body, name="ada_fwd", out_shape=jax.ShapeDtypeStruct((c_all.shape[0], ada_w.shape[1]), f32),
                          compiler_params=pltpu.CompilerParams(vmem_limit_bytes=VMEM_LIMIT))(c_all, ada_w, ada_b)


def ada_bwd(c_all, dmod):
    def body(c_ref, d_ref, o_ref):
        c = c_ref[...]
        act = (c * jax.nn.sigmoid(c)).astype(bf16)
        o_ref[...] = lax.dot_general(act, d_ref[...].astype(bf16), (((0,), (0,)), ((), ())), preferred_element_type=f32)

    return pl.pallas_call(body, name="ada_bwd", out_shape=jax.ShapeDtypeStruct((c_all.shape[1], dmod.shape[1]), f32),
                          compiler_params=pltpu.CompilerParams(vmem_limit_bytes=VMEM_LIMIT))(c_all, dmod)


def loss_head(h1, y2, target, g2, w):
    def fn(t, b, c):
        h, y, tg = [v.astype(f32) for v in t]

        def loss_fn(h, y, g, w):
            e = h + g * _rms(y, w) - tg
            return 0.5 * jnp.sum(jnp.mean(e * e, axis=-1))

        loss, grads = jax.value_and_grad(loss_fn, argnums=(0, 1, 2, 3))(h, y, b[0], c[0])
        return [grads[0], grads[1]], [grads[2], grads[3], jnp.full((1, LANE), loss, f32)]

    return rowcall("loss_head", fn, [(h1, D, 0), (y2, D, 0), (target, D, 0)], [g2], [w], [(D, f32), (D, bf16)],
                   [(1, D), (1, D), (1, LANE)])


def adamw(w, gparts, m, v, name):
    R, C = w.shape
    P = gparts.shape[0]
    budget = 2 * 1024 * 1024
    tr, tc = R, C
    if R * C * 4 > budget and R % 8 == 0:
        tr = max(t for t in range(8, R + 1, 8) if R % t == 0 and t * C * 4 <= budget)
    elif R * C * 4 > budget:
        tc = max(t for t in range(LANE, C + 1, LANE) if C % t == 0 and R * t * 4 <= budget)

    def body(w_ref, g_ref, m_ref, v_ref, go, do, mo, vo):
        g = g_ref[0].astype(f32)
        for p in range(1, P):
            g = g + g_ref[p].astype(f32)
        m2 = B1 * m_ref[...] + (1.0 - B1) * g
        v2 = B2 * v_ref[...] + (1.0 - B2) * jnp.square(g)
        m_hat = m2 * (1.0 / (1.0 - B1 ** STEP))
        v_hat = v2 * (1.0 / (1.0 - B2 ** STEP))
        go[...] = g
        do[...] = -LR * (m_hat / (jnp.sqrt(v_hat) + EPS) + WD * w_ref[...])
        mo[...] = m2
        vo[...] = v2

    blk = pl.BlockSpec((tr, tc), lambda i, j: (i, j))
    return pl.pallas_call(
        body, name=name, grid=(R // tr, C // tc), in_specs=[blk, pl.BlockSpec((P, tr, tc), lambda i, j: (0, i, j)), blk, blk],
        out_specs=[blk] * 4, out_shape=[jax.ShapeDtypeStruct((R, C), f32)] * 4, compiler_params=_cp(("parallel", "parallel")),
    )(w, gparts, m, v)


def _pack_w_in(wt):
    aq, ak, av, dqkv, dz, dbeta, da, ga, gd = jnp.split(wt, np.cumsum(IN_SPLITS)[:-1].tolist(), axis=0)
    ba = jnp.pad(jnp.concatenate([dbeta, da], axis=0), ((0, LANE - 2 * DNH), (0, 0)))
    return jnp.concatenate([ga, gd, aq, dqkv, dz, ak, av, ba], axis=0)


def _unpack_w_in(p):
    row = lambda cb, n: p[cb * LANE: cb * LANE + n]
    ba = row(CB_BA, 2 * DNH)
    return jnp.concatenate([row(CB_AQ, HQ * HD), row(CB_AK, HKV * HD), row(CB_AV, HKV * HD), row(CB_DQKV, 3 * DNH * DND),
                            row(CB_DZ, DNH * DND), ba[:DNH], ba[DNH:], row(CB_GA, D), row(CB_GD, D)], axis=0)


def _cols_gathered(g):
    return g.transpose(1, 0, 2).reshape(g.shape[1], NDEV * g.shape[2])


def _cols_split(w):
    r = w.shape[0]
    return w.reshape(r, NDEV, w.shape[1] // NDEV).transpose(1, 0, 2)


def kernel(x, c, ada_w, ada_b, norm_mix_pre, norm_mix_post, norm_ffn_pre, norm_ffn_post, w_in, dn_conv_w, dn_a_log, dn_dt_bias, dn_norm_w, attn_sinks, rel_bias, w_attn_branch, w_dn_branch, w_out, ffn_w_up, ffn_conv_w, ffn_w_down, loss_target, m_ada_w, m_ada_b, m_norm_mix_pre, m_norm_mix_post, m_norm_ffn_pre, m_norm_ffn_post, m_w_in, m_dn_conv_w, m_dn_a_log, m_dn_dt_bias, m_dn_norm_w, m_attn_sinks, m_rel_bias, m_w_attn_branch, m_w_dn_branch, m_w_out, m_ffn_w_up, m_ffn_conv_w, m_ffn_w_down, v_ada_w, v_ada_b, v_norm_mix_pre, v_norm_mix_post, v_norm_ffn_pre, v_norm_ffn_post, v_w_in, v_dn_conv_w, v_dn_a_log, v_dn_dt_bias, v_dn_norm_w, v_attn_sinks, v_rel_bias, v_w_attn_branch, v_w_dn_branch, v_w_out, v_ffn_w_up, v_ffn_conv_w, v_ffn_w_down):
    B, S, _ = x.shape
    T = B * S
    me = 4 * lax.axis_index("x") + 2 * lax.axis_index("y") + lax.axis_index("c")
    big = dict(w_in=w_in, dn_conv_w=dn_conv_w, w_attn_branch=w_attn_branch, w_dn_branch=w_dn_branch, w_out=w_out,
               ffn_w_up=ffn_w_up, ffn_conv_w=ffn_conv_w, ffn_w_down=ffn_w_down)
    big_names = list(big)

    first, mid, late = ["w_in", "dn_conv_w"], ["w_attn_branch", "w_dn_branch", "w_out"], ["ffn_w_up", "ffn_conv_w", "ffn_w_down"]
    transposed = ("w_in", "ffn_w_up")
    local = lambda n, a: a[0].T if n in transposed else a[0]
    shard = lambda names: [local(n, big[n]).astype(bf16) for n in names]
    *got, c_all = _exchange(shard(first) + [c], "gather_w_in", two_level=True)
    gw = dict(zip(first, got))
    c_all = c_all.reshape(NDEV * B, D)

    wp = _pack_w_in(gw["w_in"].reshape(IN_DIM, D))
    conv_dn = _cols_gathered(gw["dn_conv_w"]).astype(f32)

    ncol = ada_w.shape[2]
    ada_b_mine = lax.dynamic_slice_in_dim(ada_b, me * ncol, ncol, axis=1)
    mod_cols = ada_fwd(c_all, ada_w[0], ada_b_mine)
    (mod_g,) = _exchange([mod_cols], "gather_mod")
    mod = lax.dynamic_slice_in_dim(mod_g, me * B, B, axis=1).transpose(1, 0, 2).reshape(B, NMOD * D)
    sh1, sc1, g1, sh2, sc2, g2 = [mod[:, i * D:(i + 1) * D].reshape(B, 1, D) for i in range(NMOD)]

    onehot = (jnp.asarray(_bucket_table()).reshape(1, -1) == jnp.arange(NBUCK, dtype=jnp.int32)[:, None]).astype(f32)
    bias = mm(rel_bias.T, onehot, "nn", f32, "bias_table", tn=8192, precision=HI).reshape(HQ, WIN, 2 * WIN)
    sinks = attn_sinks.reshape(HQ, 1, 1)
    a_log_pad = jnp.pad(dn_a_log, ((0, 0), (DNH, LANE - 2 * DNH)))
    dt_bias_pad = jnp.pad(dn_dt_bias, ((0, 0), (DNH, LANE - 2 * DNH)))

    (u1,) = rowcall_fwd("mix_pre", f_rms_mod, [(x, D, 0)], [sc1, sh1], [norm_mix_pre], [(D, bf16)])
    proj = mm(u1.reshape(T, D), wp, "nt", bf16, "proj", tm=512, tn=CB_BA * LANE, b_cols=(0, 1)).reshape(B, S, CB_BA * LANE)
    ba = mm(u1.reshape(T, D), wp, "nt", f32, "proj_ba", tn=LANE, b_cols=(CB_BA, 1)).reshape(B, S, LANE)
    ya, *got = attn_fwd(proj, bias, sinks, _Comm(shard(mid), two_level=True))
    gw.update(zip(mid, got))
    wa = _cols_gathered(gw["w_attn_branch"])
    wd = _cols_gathered(gw["w_dn_branch"])
    wo = gw["w_out"].reshape(D, D)
    qkvn = dnconv_fwd(proj, conv_dn)
    (bg,) = rowcall_fwd("dn_gate", f_gate, [(ba, LANE, 0)], [], [a_log_pad, dt_bias_pad], [(LANE, f32)])
    o_dn, states, *got = delta_fwd(qkvn, bg, _Comm(shard(late), two_level=True))
    gw.update(zip(late, got))
    wup = gw["ffn_w_up"].reshape(2 * DFF, D)
    conv_ffn = _cols_gathered(gw["ffn_conv_w"]).astype(f32)
    wdown = gw["ffn_w_down"].reshape(DFF, D)
    (yd,) = rowcall_fwd("dn_out", f_dnout, [(o_dn, DNH * DND, 0), (proj, DNH * DND, CB_DZ // 4)], [], [dn_norm_w], [(DNH * DND, bf16)])
    pa = mm(ya.reshape(T, HQ * HD), wa, "nn", bf16, "attn_branch").reshape(B, S, D)
    pd = mm(yd.reshape(T, DNH * DND), wd, "nn", bf16, "dn_branch").reshape(B, S, D)
    merge_tok = [(proj, D, CB_GA // 8), (proj, D, CB_GD // 8), (pa, D, 0), (pd, D, 0)]
    (merged,) = rowcall_fwd("merge", f_merge, merge_tok, [], [], [(D, bf16)])
    y1 = mm(merged.reshape(T, D), wo, "nn", bf16, "mix_out").reshape(B, S, D)
    post_pre = ([(x, D, 0), (y1, D, 0)], [g1, sc2, sh2], [norm_mix_post, norm_ffn_pre])
    h1, u2 = rowcall_fwd("mix_post_ffn_pre", f_post_pre, *post_pre, [(D, f32), (D, bf16)])
    up = mm(u2.reshape(T, D), wup, "nt", bf16, "ffn_up", tn=2816).reshape(B, S, 2 * DFF)
    act = ffnconv_fwd(up, conv_ffn)
    y2 = mm(act.reshape(T, DFF), wdown, "nn", bf16, "ffn_down", tk=2816).reshape(B, S, D)

    dh1_a, dy2, dg2, dw_ffn_post, loss_b = loss_head(h1, y2, loss_target, g2, norm_ffn_post)
    dy2f = dy2.reshape(T, D)
    dact = mm(dy2f, wdown, "nt", bf16, "ffn_down_dx", tn=2816).reshape(B, S, DFF)
    g_wdown = mm(act.reshape(T, DFF), dy2f, "tn", f32, "ffn_down_dw", tm=2816, tn=512, tk=4096)
    dup, g_conv_ffn = ffnconv_bwd(up, conv_ffn, dact)
    dupf = dup.reshape(2, T, DFF)
    g_conv_ffn = g_conv_ffn.transpose(1, 0, 2).reshape(FK, 2 * DFF)
    du2 = mm(dupf, wup, "nn", bf16, "ffn_up_dx", tk=2816).reshape(B, S, D)
    g_wup = mm(dupf, u2.reshape(T, D), "tn", f32, "ffn_up_dw", tm=1408, tk=2048)
    dh1, dy1, dg1, dsc2, dsh2, dw_mix_post, dw_ffn_pre = rowcall_bwd(
        "mix_post_ffn_pre_bwd", f_post_pre, *post_pre, [(dh1_a, D, 0), (du2, D, 0)], [(0, f32), (1, bf16)])
    dy1f = dy1.reshape(T, D)
    dmerged = mm(dy1f, wo, "nt", bf16, "mix_out_dx").reshape(B, S, D)
    g_wo = mm(merged.reshape(T, D), dy1f, "tn", f32, "mix_out_dw", tk=2048)
    dga, dgd, dpa, dpd = rowcall_bwd("merge_bwd", f_merge, merge_tok, [], [], [(dmerged, D, 0)],
                                     [(0, bf16), (1, bf16), (2, bf16), (3, bf16)])
    dpaf, dpdf = dpa.reshape(T, D), dpd.reshape(T, D)
    dya = mm(dpaf, wa, "nt", bf16, "attn_branch_dx").reshape(B, S, HQ * HD)
    g_wa = mm(ya.reshape(T, HQ * HD), dpaf, "tn", f32, "attn_branch_dw", tk=2048)
    dyd = mm(dpdf, wd, "nt", bf16, "dn_branch_dx").reshape(B, S, DNH * DND)
    g_wd = mm(yd.reshape(T, DNH * DND), dpdf, "tn", f32, "dn_branch_dw", tk=2048)
    do_dn, dz, dw_dn_norm = rowcall_bwd("dn_out_bwd", f_dnout, [(o_dn, DNH * DND, 0), (proj, DNH * DND, CB_DZ // 4)], [], [dn_norm_w],
                                        [(dyd, DNH * DND, 0)], [(0, f32), (1, bf16)])
    parts = {}
    outbox = lambda d: _Comm([d[n].astype(bf16) for n in d], scatter=True)
    send = dict(ffn_w_up=g_wup.reshape(NDEV, 2 * DFF // NDEV, D), ffn_conv_w=_cols_split(g_conv_ffn),
                ffn_w_down=g_wdown.reshape(NDEV, DFF // NDEV, D))
    dqkvn, dbg, *got = delta_bwd(qkvn, bg, states, do_dn, outbox(send))
    parts.update(zip(send, got))
    dba, da_log_pad, ddt_bias_pad = rowcall_bwd("dn_gate_bwd", f_gate, [(ba, LANE, 0)], [], [a_log_pad, dt_bias_pad],
                                                [(dbg, LANE, 0)], [(0, bf16)])
    ddqkv, g_conv_dn = dnconv_bwd(proj, conv_dn, dqkvn)
    send = dict(w_attn_branch=_cols_split(g_wa), w_dn_branch=_cols_split(g_wd),
                w_out=g_wo.reshape(NDEV, D // NDEV, D))
    dq, dk, dv, dbias, dsinks, *got = attn_bwd(proj, bias, sinks, dya, outbox(send))
    parts.update(zip(send, got))
    dproj = jnp.concatenate([dga, dgd, dq, ddqkv, dz, dk, dv, dba], axis=2).reshape(T, NP)
    g_wp = mm(dproj, u1.reshape(T, D), "tn", f32, "proj_dw", tm=1664, tk=1024)
    send = dict(w_in=_unpack_w_in(g_wp).reshape(NDEV, IN_DIM // NDEV, D), dn_conv_w=_cols_split(g_conv_dn))
    du1, *got = mm(dproj, wp, "nn", bf16, "proj_dx", tm=512, tk=NP, comm=outbox(send))
    parts.update(zip(send, got))
    du1 = du1.reshape(B, S, D)
    grad_x, dsc1, dsh1, dw_mix_pre = rowcall_bwd("mix_pre_bwd", f_rms_mod, [(x, D, 0)], [sc1, sh1], [norm_mix_pre], [(du1, D, 0)],
                                                 [(0, f32)], add=(dh1, D, 0))
    g_rel = mm(dbias.reshape(HQ, WIN * 2 * WIN), onehot, "nt", f32, "rel_bias_dw", tk=8192, precision=HI)

    dmod = jnp.concatenate([dsh1, dsc1, dg1, dsh2, dsc2, dg2], axis=2).reshape(B, NMOD * D)

    zrow = lambda a: jnp.concatenate([a.reshape(1, -1), jnp.zeros((B - 1, a.size), f32)], axis=0)
    small_g = jnp.concatenate([
        dmod, dw_mix_pre.reshape(B, D), dw_mix_post.reshape(B, D), dw_ffn_pre.reshape(B, D), dw_ffn_post.reshape(B, D),
        da_log_pad.reshape(B, LANE)[:, DNH:2 * DNH], ddt_bias_pad.reshape(B, LANE)[:, DNH:2 * DNH], dw_dn_norm.reshape(B, DND),
        zrow(dsinks), zrow(g_rel.T), loss_b.reshape(B, LANE)[:, :1], jnp.zeros((B, SMALL_PAD - SMALL_N - 1), f32)], axis=1)
    (small_all,) = _exchange([small_g], "gather_small")
    dmod_cols = lax.dynamic_slice_in_dim(small_all.reshape(NDEV * B, SMALL_PAD), me * ncol, ncol, axis=1)
    g_ada_w = ada_bwd(c_all, dmod_cols)
    small_w = dict(ada_b=(ada_b, m_ada_b, v_ada_b), norm_mix_pre=(norm_mix_pre, m_norm_mix_pre, v_norm_mix_pre),
                   norm_mix_post=(norm_mix_post, m_norm_mix_post, v_norm_mix_post), norm_ffn_pre=(norm_ffn_pre, m_norm_ffn_pre, v_norm_ffn_pre),
                   norm_ffn_post=(norm_ffn_post, m_norm_ffn_post, v_norm_ffn_post), dn_a_log=(dn_a_log, m_dn_a_log, v_dn_a_log),
                   dn_dt_bias=(dn_dt_bias, m_dn_dt_bias, v_dn_dt_bias), dn_norm_w=(dn_norm_w, m_dn_norm_w, v_dn_norm_w),
                   attn_sinks=(attn_sinks, m_attn_sinks, v_attn_sinks), rel_bias=(rel_bias, m_rel_bias, v_rel_bias))

    def pack(i, fill):
        row = jnp.concatenate([small_w[n][i].reshape(1, -1) for n, _ in SMALL], axis=1)
        return jnp.pad(row, ((0, 0), (0, SMALL_PAD - SMALL_N)), constant_values=fill)

    small_out = adamw(pack(0, 0.0), small_all.reshape(NDEV * B, 1, SMALL_PAD), pack(1, 0.0), pack(2, 1.0), "adamw_small")
    loss = small_out[0][0, SMALL_N]

    res = {}
    off = 0
    for n, size in SMALL:
        shp = small_w[n][0].shape
        res[n] = [o[:, off:off + size].reshape(shp) for o in small_out]
        off += size
    res["ada_w"] = [o[None] for o in adamw(ada_w[0], g_ada_w[None], m_ada_w[0], v_ada_w[0], "adamw_ada_w")]
    moments = dict(w_in=(m_w_in, v_w_in), dn_conv_w=(m_dn_conv_w, v_dn_conv_w), w_attn_branch=(m_w_attn_branch, v_w_attn_branch),
                   w_dn_branch=(m_w_dn_branch, v_w_dn_branch), w_out=(m_w_out, v_w_out), ffn_w_up=(m_ffn_w_up, v_ffn_w_up),
                   ffn_conv_w=(m_ffn_conv_w, v_ffn_conv_w), ffn_w_down=(m_ffn_w_down, v_ffn_w_down))
    for n in big_names:
        outs = adamw(local(n, big[n]), parts[n], local(n, moments[n][0]), local(n, moments[n][1]), "adamw_" + n)
        res[n] = [(o.T if n in transposed else o)[None] for o in outs]

    order = ["ada_w", "ada_b", "norm_mix_pre", "norm_mix_post", "norm_ffn_pre", "norm_ffn_post", "w_in", "dn_conv_w", "dn_a_log",
             "dn_dt_bias", "dn_norm_w", "attn_sinks", "rel_bias", "w_attn_branch", "w_dn_branch", "w_out", "ffn_w_up", "ffn_conv_w",
             "ffn_w_down"]
    return (loss, grad_x, *[res[n][0] for n in order], *[res[n][1] for n in order], *[res[n][2] for n in order],
            *[res[n][3] for n in order])
```

```python
import functools
import math

import numpy as np
import jax
import jax.numpy as jnp
from jax import lax
from jax.experimental import pallas as pl
from jax.experimental.pallas import tpu as pltpu

f32 = jnp.float32
bf16 = jnp.bfloat16
HI = lax.Precision.HIGHEST
MID = lax.Precision.HIGH
MESH = pl.DeviceIdType.MESH

NDEV = 8
D = 1024
HQ, HKV, HD, WIN, NBUCK, MAXDIST = 8, 2, 64, 128, 32, 128
DNH, DND, DNK, CH = 4, 128, 4, 64
DFF, FK = 2816, 3
NMOD = 6
RMS_EPS = 1e-6
L2_EPS = 1e-6
NEG_INF = -1e30
LR, B1, B2, EPS, WD, STEP = 0.001, 0.9, 0.999, 1e-08, 0.01, 10

LANE = 128
CB_GA, CB_GD, CB_AQ, CB_DQKV, CB_DZ, CB_AK, CB_AV, CB_BA, NPB = 0, 8, 16, 20, 32, 36, 37, 38, 39
NP = NPB * LANE
IN_SPLITS = (HQ * HD, HKV * HD, HKV * HD, 3 * DNH * DND, DNH * DND, DNH, DNH, D, D)
IN_DIM = sum(IN_SPLITS)
VMEM_LIMIT = 56 * 1024 * 1024

SMALL = (("ada_b", NMOD * D), ("norm_mix_pre", D), ("norm_mix_post", D), ("norm_ffn_pre", D), ("norm_ffn_post", D),
         ("dn_a_log", DNH), ("dn_dt_bias", DNH), ("dn_norm_w", DND), ("attn_sinks", HQ), ("rel_bias", NBUCK * HQ))
SMALL_N = sum(n for _, n in SMALL)
SMALL_PAD = 10752


def _cp(sem):
    return pltpu.CompilerParams(dimension_semantics=sem, vmem_limit_bytes=VMEM_LIMIT)


def _pick(dim, target):
    if dim <= target:
        return dim
    best = None
    for d in range(LANE, target + 1, LANE):
        if dim % d == 0:
            best = d
    assert best is not None, (dim, target)
    return best


def _me():
    x, y, c = lax.axis_index("x"), lax.axis_index("y"), lax.axis_index("c")
    return x, y, c, 4 * x + 2 * y + c


def _peer(x, y, c, k):
    px = 1 - x if k & 4 else x
    py = 1 - y if k & 2 else y
    pc = 1 - c if k & 1 else c
    return (px, py, pc), 4 * px + 2 * py + pc


class _Comm:
    def __init__(self, arrs, scatter=False, two_level=False):
        assert not (scatter and two_level)
        self.arrs, self.n, self.scatter, self.two_level = list(arrs), len(arrs), scatter, two_level
        if scatter:
            self.out_shape = [jax.ShapeDtypeStruct(a.shape, a.dtype) for a in arrs]
        else:
            self.out_shape = [jax.ShapeDtypeStruct((NDEV,) + a.shape, a.dtype) for a in arrs]
        nsem = self.n * (NDEV - 1)
        self.scratch = [pltpu.SemaphoreType.DMA((nsem,)), pltpu.SemaphoreType.DMA((nsem,)), pltpu.SemaphoreType.DMA((self.n,))]
        self.specs = [pl.BlockSpec(memory_space=pl.ANY)] * self.n

    def phases(self, ins, out, send, recv, loc):
        x, y, c, me = _me()

        def remote(a, k, src, dst, to):
            s = a * (NDEV - 1) + k - 1
            return pltpu.make_async_remote_copy(src_ref=src, dst_ref=dst, send_sem=send.at[s], recv_sem=recv.at[s],
                                                device_id=to, device_id_type=MESH)

        def local(a):
            return pltpu.make_async_copy(ins[a].at[me] if self.scatter else ins[a], out[a].at[me], loc.at[a])

        if not self.two_level:
            def mine(a, k):
                peer, pid = _peer(x, y, c, k)
                return remote(a, k, ins[a].at[pid] if self.scatter else ins[a], out[a].at[me], peer)

            def theirs(a, k):
                peer, pid = _peer(x, y, c, k)
                return remote(a, k, ins[a].at[pid] if self.scatter else ins[a], out[a].at[pid], peer)

            def start():
                for a in range(self.n):
                    local(a).start()
                    for k in range(1, NDEV):
                        mine(a, k).start()

            def forward():
                pass

            def finish():
                for a in range(self.n):
                    for k in range(1, NDEV):
                        mine(a, k).wait_send()
                    for k in range(1, NDEV):
                        theirs(a, k).wait_recv()
                    local(a).wait()

            return start, forward, finish

        sibling = (x, y, 1 - c)
        chips = [(1 - x, y), (x, 1 - y), (1 - x, 1 - y)]
        slot = lambda px, py, pc: 4 * px + 2 * py + pc

        def own(a, k, to):
            return remote(a, k, ins[a], out[a].at[me], to)

        def landed(a, k, frm):
            return remote(a, k, ins[a], out[a].at[slot(*frm)], frm)

        def passed(a, j):
            rows = out[a].at[slot(*chips[j], c)]
            return remote(a, 5 + j, rows, rows, sibling)

        def start():
            for a in range(self.n):
                local(a).start()
                own(a, 1, sibling).start()
                for j, chip in enumerate(chips):
                    own(a, 2 + j, (*chip, c)).start()

        def forward():
            for a in range(self.n):
                for j, chip in enumerate(chips):
                    landed(a, 2 + j, (*chip, c)).wait_recv()
                    passed(a, j).start()

        def finish():
            for a in range(self.n):
                landed(a, 1, sibling).wait_recv()
                for j, chip in enumerate(chips):
                    remote(a, 5 + j, ins[a], out[a].at[slot(*chip, 1 - c)], sibling).wait_recv()
                own(a, 1, sibling).wait_send()
                for j, chip in enumerate(chips):
                    own(a, 2 + j, (*chip, c)).wait_send()
                    passed(a, j).wait_send()
                local(a).wait()

        return start, forward, finish


def _ride(body, n_in, n_out, n_scr, comm, first, mid, last):
    k = comm.n

    def wrapped(*refs):
        ins, cins = refs[:n_in], refs[n_in:n_in + k]
        o0 = n_in + k
        outs, couts = refs[o0:o0 + n_out], refs[o0 + n_out:o0 + n_out + k]
        s0 = o0 + n_out + k
        scr, sems = refs[s0:s0 + n_scr], refs[s0 + n_scr:]
        start, forward, finish = comm.phases(cins, couts, *sems)
        pl.when(first())(start)
        body(*ins, *outs, *scr)
        pl.when(mid())(forward)
        pl.when(last())(finish)

    return wrapped


def _exchange(arrs, name, scatter=False, two_level=False):
    comm = _Comm(arrs, scatter, two_level)

    def body(*refs):
        start, forward, finish = comm.phases(refs[:comm.n], refs[comm.n:2 * comm.n], *refs[2 * comm.n:])
        start()
        forward()
        finish()

    return pl.pallas_call(body, name=name, out_shape=comm.out_shape, in_specs=comm.specs, out_specs=comm.specs,
                          scratch_shapes=comm.scratch, compiler_params=pltpu.CompilerParams(has_side_effects=True))(*arrs)


def mm(a, b, mode, out_dtype, name, tm=1024, tn=1024, tk=1024, precision=None, comm=None, b_cols=None):
    a_parts = a.shape[0] if a.ndim == 3 else 1
    b_parts = b.shape[0] if b.ndim == 3 else 1
    assert b_parts == 1 or mode == "tn"
    ash, bsh = (a.shape[-2], a.shape[-1] * a_parts), b.shape[-2:]
    if mode == "nn":
        (M, K), (K2, N) = ash, bsh
    elif mode == "nt":
        (M, K), (N, K2) = ash, bsh
    else:
        (K, M), (K2, N) = ash, (bsh[0], bsh[1] * b_parts)
    assert K == K2, (name, a.shape, b.shape)
    col0 = 0
    if b_cols is not None:
        assert mode in ("nn", "nt") and tn % LANE == 0
        col0, N = b_cols[0], b_cols[1] * tn
    if mode == "tn":
        tm, tn, tk = _pick(M // a_parts, tm), _pick(N // b_parts, tn), _pick(K, tk)
    else:
        tm, tn, tk = _pick(M, tm), _pick(N // b_parts, tn), _pick(K // a_parts, tk)
    nk = K // tk
    if mode == "tn" and a_parts > 1:
        per = M // tm // a_parts
        a_spec = pl.BlockSpec((None, tk, tm), lambda i, j, k: (i // per, k, i % per))
    elif mode == "tn":
        a_spec = pl.BlockSpec((tk, tm), lambda i, j, k: (k, i))
    elif a_parts > 1:
        per = nk // a_parts
        a_spec = pl.BlockSpec((None, tm, tk), lambda i, j, k: (k // per, i, k % per))
    else:
        a_spec = pl.BlockSpec((tm, tk), lambda i, j, k: (i, k))
    if mode == "nt":
        b_spec = pl.BlockSpec((tn, tk), lambda i, j, k: (col0 + j, k))
    elif b_parts > 1:
        per = N // tn // b_parts
        b_spec = pl.BlockSpec((None, tk, tn), lambda i, j, k: (j // per, k, j % per))
    else:
        b_spec = pl.BlockSpec((tk, tn), lambda i, j, k: (k, col0 + j))
    dims = {"nn": ((1,), (0,)), "nt": ((1,), (1,)), "tn": ((0,), (0,))}[mode]

    def body(a_ref, b_ref, o_ref, *scr):
        p = lax.dot_general(a_ref[...], b_ref[...], (dims, ((), ())), preferred_element_type=f32, precision=precision)
        if nk == 1:
            o_ref[...] = p.astype(o_ref.dtype)
        else:
            acc = scr[0]
            k = pl.program_id(2)

            @pl.when(k == 0)
            def _():
                acc[...] = p

            @pl.when(k > 0)
            def _():
                acc[...] += p

            @pl.when(k == nk - 1)
            def _():
                o_ref[...] = acc[...].astype(o_ref.dtype)

    grid = (M // tm, N // tn, nk)
    scratch = [pltpu.VMEM((tm, tn), f32)] if nk > 1 else []
    out_spec = pl.BlockSpec((tm, tn), lambda i, j, k: (i, j))
    out_shape = jax.ShapeDtypeStruct((M, N), out_dtype)
    if comm is None:
        return pl.pallas_call(body, name=name, grid=grid, in_specs=[a_spec, b_spec], out_specs=out_spec, out_shape=out_shape,
                              scratch_shapes=scratch, compiler_params=_cp(("parallel", "parallel", "arbitrary")))(a, b)
    at = lambda pos: lambda: functools.reduce(jnp.logical_and, [pl.program_id(d) == p for d, p in enumerate(pos)])
    end = tuple(g - 1 for g in grid)
    return pl.pallas_call(
        _ride(body, 2, 1, len(scratch), comm, at((0, 0, 0)), at(end), at(end)), name=name, grid=grid,
        in_specs=[a_spec, b_spec] + comm.specs, out_specs=[out_spec] + comm.specs, out_shape=[out_shape] + comm.out_shape,
        scratch_shapes=scratch + comm.scratch, compiler_params=_cp(("arbitrary", "arbitrary", "arbitrary")),
    )(a, b, *comm.arrs)


def rowcall(name, fn, tok, bat, con, tok_out, acc_out, ts=256):
    B, S = tok[0][0].shape[:2]
    ts = min(ts, S)
    nt, nb, nc, no, na = len(tok), len(bat), len(con), len(tok_out), len(acc_out)

    def body(*refs):
        tr, br, cr = refs[:nt], refs[nt:nt + nb], refs[nt + nb:nt + nb + nc]
        orf, arf = refs[nt + nb + nc:nt + nb + nc + no], refs[nt + nb + nc + no:]
        touts, aouts = fn([r[0] for r in tr], [r[0] for r in br], [r[...] for r in cr])
        for r, v in zip(orf, touts):
            r[0] = v.astype(r.dtype)
        s = pl.program_id(1)
        for r, v in zip(arf, aouts):
            @pl.when(s == 0)
            def _(r=r):
                r[...] = jnp.zeros(r.shape, r.dtype)
            r[0] += v.astype(f32)

    in_specs = [pl.BlockSpec((1, ts, w), lambda b, s, cb=cb: (b, s, cb)) for (_, w, cb) in tok]
    in_specs += [pl.BlockSpec((1,) + a.shape[1:], lambda b, s: (b, 0, 0)) for a in bat]
    in_specs += [pl.BlockSpec(a.shape, lambda b, s, nd=a.ndim: (0,) * nd) for a in con]
    out_specs = [pl.BlockSpec((1, ts, w), lambda b, s: (b, s, 0)) for (w, _) in tok_out]
    out_specs += [pl.BlockSpec((1,) + shp, lambda b, s, nd=len(shp): (b,) + (0,) * nd) for shp in acc_out]
    out_shape = [jax.ShapeDtypeStruct((B, S, w), dt) for (w, dt) in tok_out]
    out_shape += [jax.ShapeDtypeStruct((B,) + shp, f32) for shp in acc_out]
    return pl.pallas_call(
        body, name=name, grid=(B, S // ts), in_specs=in_specs, out_specs=out_specs, out_shape=out_shape,
        compiler_params=_cp(("parallel", "arbitrary")),
    )(*[t[0] for t in tok], *bat, *con)


def rowcall_fwd(name, f, tok, bat, con, tok_out, ts=256):
    def fn(t, b, c):
        return f([v.astype(f32) for v in t], b, c), []
    return rowcall(name, fn, tok, bat, con, tok_out, [], ts)


def rowcall_bwd(name, f, tok, bat, con, cts, tok_grads, add=None, ts=256):
    nt, ncts = len(tok), len(cts)

    def fn(t, b, c):
        prim = [v.astype(f32) for v in t[:nt]]
        ct = [v.astype(f32) for v in t[nt:nt + ncts]]
        _, vjp = jax.vjp(lambda tt, bb, cc: f(tt, bb, cc), prim, b, c)
        dt, db, dc = vjp(ct)
        touts = [dt[i] for i, _ in tok_grads]
        if add is not None:
            touts[0] = touts[0] + t[nt + ncts].astype(f32)
        return touts, list(db) + list(dc)

    all_tok = list(tok) + list(cts) + ([add] if add is not None else [])
    tok_out = [(tok[i][1], dt) for i, dt in tok_grads]
    acc_out = [tuple(a.shape[1:]) for a in bat] + [tuple(a.shape) for a in con]
    return rowcall(name, fn, all_tok, bat, con, tok_out, acc_out, ts)


def _rms(y, w):
    return y * lax.rsqrt(jnp.mean(y * y, axis=-1, keepdims=True) + RMS_EPS) * w


def f_rms_mod(t, b, c):
    return [_rms(t[0], c[0]) * (1.0 + b[0]) + b[1]]


def f_post_pre(t, b, c):
    h1 = t[0] + b[0] * _rms(t[1], c[0])
    return [h1, _rms(h1, c[1]) * (1.0 + b[1]) + b[2]]


def f_merge(t, b, c):
    ga, gd, ya, yd = t
    return [jax.nn.sigmoid(ga) * ya + jax.nn.sigmoid(gd) * yd]


def f_dnout(t, b, c):
    o, z = t
    outs = []
    for h in range(DNH):
        sl = slice(h * DND, (h + 1) * DND)
        zh = z[:, sl]
        outs.append(_rms(o[:, sl], c[0]) * (zh * jax.nn.sigmoid(zh)))
    return [jnp.concatenate(outs, axis=1)]


def _softplus(x):
    return jnp.maximum(x, 0.0) + jnp.log(1.0 + jnp.exp(-jnp.abs(x)))


def f_gate(t, b, c):
    ba = t[0]
    a_log, dt_bias = c
    lane = lax.broadcasted_iota(jnp.int32, ba.shape, 1)
    beta = jax.nn.sigmoid(ba)
    g = -jnp.exp(a_log) * _softplus(ba + dt_bias)
    return [jnp.where(lane < DNH, beta, jnp.where(lane < 2 * DNH, g, 0.0))]


def _bucket_table():
    qi = np.arange(WIN)[:, None]
    kj = np.arange(2 * WIN)[None, :]
    dist = np.maximum(WIN + qi - kj, 0)
    max_exact = NBUCK // 2
    scaled = np.log(np.maximum(dist, 1).astype(np.float64) / max_exact) / math.log(MAXDIST / max_exact)
    large = np.minimum(max_exact + (scaled * (NBUCK - max_exact)).astype(np.int32), NBUCK - 1)
    return np.where(dist < max_exact, dist, large).astype(np.int32)


def _attn_mask(n):
    qi = lax.broadcasted_iota(jnp.int32, (WIN, 2 * WIN), 0)
    kj = lax.broadcasted_iota(jnp.int32, (WIN, 2 * WIN), 1)
    dist = WIN + qi - kj
    return (dist >= 0) & (dist < WIN) & ((kj >= WIN) | (n > 0))


def _swap_halves(x):
    return pltpu.roll(x, HD, axis=x.ndim - 1)


@jax.custom_vjp
def _swap_halves_vjp(x):
    return _swap_halves(x)


_swap_halves_vjp.defvjp(lambda x: (_swap_halves(x), None), lambda _, g: (_swap_halves(g),))


def _attn_block(q, kp, kc, vp, vc, bias, sinks, mask, differentiated):
    dot = _bdot_bf16_vjp if differentiated else _bdot_bf16
    swap = _swap_halves_vjp if differentiated else _swap_halves
    B, grp = q.shape[0], HQ // HKV
    upper = lax.broadcasted_iota(jnp.int32, (2 * WIN, LANE), 1) >= HD

    def placed(natural, swapped, j, half):
        keep = upper if half == 1 else ~upper
        return jnp.where(keep, natural if j == half else swapped, 0.0)

    qh, ks, vs = [], [], []
    for b in range(B):
        kb, vb = jnp.concatenate([kp[b], kc[b]], axis=0), jnp.concatenate([vp[b], vc[b]], axis=0)
        kb_sw, vb_sw = swap(kb), swap(vb)
        for h in range(HQ):
            qh.append(q[b, :, (h // 2) * LANE:(h // 2 + 1) * LANE])
            ks.append(placed(kb, kb_sw, h // grp, h % 2))
            vs.append(placed(vb, vb_sw, h // grp, h % 2))
    s = dot(_stack(qh), _stack(ks), 2, 2).reshape(B, HQ, WIN, 2 * WIN) * (HD ** -0.5)
    s = jnp.where(mask, s + bias, NEG_INF)
    m = jnp.maximum(jnp.max(s, axis=-1, keepdims=True), sinks)
    p = jnp.exp(s - m)
    probs = p / (jnp.sum(p, axis=-1, keepdims=True) + jnp.exp(sinks - m))
    o = dot(probs.reshape(B * HQ, WIN, 2 * WIN), _stack(vs), 2, 1)
    return _stack([jnp.concatenate([o[b * HQ + 2 * i] + o[b * HQ + 2 * i + 1] for i in range(HQ // 2)], axis=1) for b in range(B)])


def _attn_specs(B, NB):
    last = NB - 1
    return [
        pl.BlockSpec((B, WIN, HQ * HD), lambda n: (0, jnp.minimum(n, last), CB_AQ // 4)),
        pl.BlockSpec((B, WIN, LANE), lambda n: (0, jnp.clip(n - 1, 0, last), CB_AK)),
        pl.BlockSpec((B, WIN, LANE), lambda n: (0, jnp.minimum(n, last), CB_AK)),
        pl.BlockSpec((B, WIN, LANE), lambda n: (0, jnp.clip(n - 1, 0, last), CB_AV)),
        pl.BlockSpec((B, WIN, LANE), lambda n: (0, jnp.minimum(n, last), CB_AV)),
        pl.BlockSpec((HQ, WIN, 2 * WIN), lambda n: (0, 0, 0)),
        pl.BlockSpec((HQ, 1, 1), lambda n: (0, 0, 0)),
    ]


def attn_fwd(proj, bias, sinks, comm):
    B, S, _ = proj.shape
    NB = S // WIN

    def body(q, kp, kc, vp, vc, bias_ref, sink_ref, o_ref):
        mask = _attn_mask(pl.program_id(0))
        o = _attn_block(*[r[...].astype(f32) for r in (q, kp, kc, vp, vc)], bias_ref[...], sink_ref[...], mask, False)
        o_ref[...] = o.astype(o_ref.dtype)

    at = lambda n: lambda: pl.program_id(0) == n
    return pl.pallas_call(
        _ride(body, 7, 1, 0, comm, at(0), at((3 * NB) // 4), at(NB - 1)), name="attn_fwd", grid=(NB,),
        in_specs=_attn_specs(B, NB) + comm.specs,
        out_specs=[pl.BlockSpec((B, WIN, HQ * HD), lambda n: (0, n, 0))] + comm.specs,
        out_shape=[jax.ShapeDtypeStruct((B, S, HQ * HD), bf16)] + comm.out_shape, scratch_shapes=comm.scratch,
        compiler_params=_cp(("arbitrary",)),
    )(proj, proj, proj, proj, proj, bias, sinks, *comm.arrs)


def attn_bwd(proj, bias, sinks, dy, comm):
    B, S, _ = proj.shape
    NB = S // WIN
    last = NB - 1

    def body(q, kp, kc, vp, vc, bias_ref, sink_ref, dy_ref, dq_ref, dk_ref, dv_ref, dbias_ref, dsink_ref, kcar, vcar):
        n = pl.program_id(0)

        @pl.when(n == 0)
        def _():
            dbias_ref[...] = jnp.zeros(dbias_ref.shape, f32)
            dsink_ref[...] = jnp.zeros(dsink_ref.shape, f32)
            kcar[...] = jnp.zeros(kcar.shape, f32)
            vcar[...] = jnp.zeros(vcar.shape, f32)

        @pl.when(n < NB)
        def _():
            mask = _attn_mask(n)
            _, vjp = jax.vjp(lambda *a: _attn_block(*a, mask, True), *[r[...].astype(f32) for r in (q, kp, kc, vp, vc)],
                             bias_ref[...], sink_ref[...])
            dq, dkp, dkc, dvp, dvc, dbias, dsink = vjp(dy_ref[...].astype(f32))
            dq_ref[...] = dq.astype(dq_ref.dtype)
            dbias_ref[...] += dbias
            dsink_ref[...] += dsink
            dk_ref[...] = (kcar[...] + dkp).astype(dk_ref.dtype)
            dv_ref[...] = (vcar[...] + dvp).astype(dv_ref.dtype)
            kcar[...] = dkc
            vcar[...] = dvc

        @pl.when(n == NB)
        def _():
            dk_ref[...] = kcar[...].astype(dk_ref.dtype)
            dv_ref[...] = vcar[...].astype(dv_ref.dtype)

    in_specs = _attn_specs(B, NB) + [pl.BlockSpec((B, WIN, HQ * HD), lambda n: (0, jnp.minimum(n, last), 0))]
    kv_out = pl.BlockSpec((B, WIN, LANE), lambda n: (0, jnp.maximum(n - 1, 0), 0))
    at = lambda n: lambda: pl.program_id(0) == n
    return pl.pallas_call(
        _ride(body, 8, 5, 2, comm, at(0), at(NB), at(NB)), name="attn_bwd", grid=(NB + 1,),
        in_specs=in_specs + comm.specs,
        out_specs=[pl.BlockSpec((B, WIN, HQ * HD), lambda n: (0, jnp.minimum(n, last), 0)), kv_out, kv_out,
                   pl.BlockSpec((HQ, WIN, 2 * WIN), lambda n: (0, 0, 0)), pl.BlockSpec((HQ, 1, 1), lambda n: (0, 0, 0))] + comm.specs,
        out_shape=[jax.ShapeDtypeStruct((B, S, HQ * HD), bf16), jax.ShapeDtypeStruct((B, S, LANE), bf16),
                   jax.ShapeDtypeStruct((B, S, LANE), bf16), jax.ShapeDtypeStruct((HQ, WIN, 2 * WIN), f32),
                   jax.ShapeDtypeStruct((HQ, 1, 1), f32)] + comm.out_shape,
        scratch_shapes=[pltpu.VMEM((B, WIN, LANE), f32), pltpu.VMEM((B, WIN, LANE), f32)] + comm.scratch,
        compiler_params=_cp(("arbitrary",)),
    )(proj, proj, proj, proj, proj, bias, sinks, dy, *comm.arrs)


DN_ROWS, FFN_ROWS = 256, 32


def _stage_rows(dst, value):
    dst[0:8] = jnp.zeros((8, LANE), f32)
    dst[8:8 + value.shape[0]] = value


def _conv_rows(xs, w, width, r, rows):
    wins = [xs[pl.ds(r + 8 - (width - 1) + j, rows), :] for j in range(width)]
    out = w[0:1] * wins[0]
    for j in range(1, width):
        out = out + w[j:j + 1] * wins[j]
    return out, wins


def _fold8(v):
    return jnp.sum(v.reshape(v.shape[0] // 8, 8, LANE), axis=0)


def _conv_rows_t(ds, w, width, r, rows):
    out = w[0:1] * ds[pl.ds(r + width - 1, rows), :]
    for j in range(1, width):
        out = out + w[j:j + 1] * ds[pl.ds(r + width - 1 - j, rows), :]
    return out


def _dn_outblk(i):
    return (i % DNH) * 3 + i // DNH


def _dn_act(c, isqk):
    sg = jax.nn.sigmoid(c)
    y = c * sg
    n = lax.rsqrt(jnp.sum(y * y, axis=-1, keepdims=True) + L2_EPS)
    return jnp.where(isqk, y * n, y), sg, n


def dnconv_fwd(proj, conv_w):
    B, S, _ = proj.shape
    rows = min(DN_ROWS, S)

    def body(x_ref, w_ref, o_ref, xs):
        isqk = pl.program_id(0) < 2 * DNH
        _stage_rows(xs, x_ref[0].astype(f32))
        w = w_ref[...]
        for r in range(0, S, rows):
            c, _ = _conv_rows(xs, w, DNK, r, rows)
            o_ref[0, pl.ds(r, rows), :] = _dn_act(c, isqk)[0]

    return pl.pallas_call(
        body, name="dnconv_fwd", grid=(3 * DNH, B),
        in_specs=[pl.BlockSpec((1, S, LANE), lambda i, b: (b, 0, CB_DQKV + i)), pl.BlockSpec((DNK, LANE), lambda i, b: (0, i))],
        out_specs=pl.BlockSpec((1, S, LANE), lambda i, b: (b, 0, _dn_outblk(i))),
        out_shape=jax.ShapeDtypeStruct((B, S, 3 * DNH * DND), f32), scratch_shapes=[pltpu.VMEM((S + 8, LANE), f32)],
        compiler_params=_cp(("parallel", "parallel")),
    )(proj, conv_w)


def dnconv_bwd(proj, conv_w, dqkvn):
    B, S, _ = proj.shape
    rows = min(DN_ROWS, S)

    def body(x_ref, w_ref, dy_ref, dx_ref, dw_ref, xs, ds):
        isqk = pl.program_id(0) < 2 * DNH
        _stage_rows(xs, x_ref[0].astype(f32))
        w = w_ref[...]
        dw = [jnp.zeros((8, LANE), f32) for _ in range(DNK)]
        for r in range(0, S, rows):
            c, wins = _conv_rows(xs, w, DNK, r, rows)
            out, sg, n = _dn_act(c, isqk)
            dout = dy_ref[0, pl.ds(r, rows), :]
            dy = jnp.where(isqk, n * (dout - out * jnp.sum(dout * out, axis=-1, keepdims=True)), dout)
            dc = dy * (sg * (1.0 + c * (1.0 - sg)))
            ds[pl.ds(r, rows), :] = dc
            for j in range(DNK):
                dw[j] = dw[j] + _fold8(dc * wins[j])
        ds[S:S + 8] = jnp.zeros((8, LANE), f32)
        for r in range(0, S, rows):
            dx_ref[0, pl.ds(r, rows), :] = _conv_rows_t(ds, w, DNK, r, rows).astype(dx_ref.dtype)

        @pl.when(pl.program_id(1) == 0)
        def _():
            dw_ref[...] = jnp.zeros(dw_ref.shape, f32)
        dw_ref[...] += jnp.concatenate([jnp.sum(d, axis=0, keepdims=True) for d in dw], axis=0)

    return pl.pallas_call(
        body, name="dnconv_bwd", grid=(3 * DNH, B),
        in_specs=[pl.BlockSpec((1, S, LANE), lambda i, b: (b, 0, CB_DQKV + i)), pl.BlockSpec((DNK, LANE), lambda i, b: (0, i)),
                  pl.BlockSpec((1, S, LANE), lambda i, b: (b, 0, _dn_outblk(i)))],
        out_specs=[pl.BlockSpec((1, S, LANE), lambda i, b: (b, 0, i)), pl.BlockSpec((DNK, LANE), lambda i, b: (0, i))],
        out_shape=[jax.ShapeDtypeStruct((B, S, 3 * DNH * DND), bf16), jax.ShapeDtypeStruct((DNK, 3 * DNH * DND), f32)],
        scratch_shapes=[pltpu.VMEM((S + 8, LANE), f32), pltpu.VMEM((S + 8, LANE), f32)],
        compiler_params=_cp(("parallel", "arbitrary")),
    )(proj, conv_w, dqkvn)


def _bdot(a, b, ca, cb, precision=HI):
    return lax.dot_general(a, b, (((ca,), (cb,)), ((0,), (0,))), preferred_element_type=f32, precision=precision)


def _bdot_bf16(a, b, ca, cb):
    return _bdot(a.astype(bf16), b.astype(bf16), ca, cb, None)


@functools.partial(jax.custom_vjp, nondiff_argnums=(2, 3))
def _bdot_bf16_vjp(a, b, ca, cb):
    return _bdot_bf16(a, b, ca, cb)


def _bdot_bf16_fwd(a, b, ca, cb):
    return _bdot_bf16(a, b, ca, cb), (a, b)


def _bdot_bf16_bwd(ca, cb, res, g):
    a, b = res
    fa, fb = 3 - ca, 3 - cb
    da = _bdot_bf16(g, b, 2, fb) if ca == 2 else _bdot_bf16(b, g, fb, 2)
    db = _bdot_bf16(a, g, fa, 1) if cb == 1 else _bdot_bf16(g, a, 1, fa)
    return da, db


_bdot_bf16_vjp.defvjp(_bdot_bf16_fwd, _bdot_bf16_bwd)


def _neumann_inverse(low):
    n = low.shape[-1]
    eye = (lax.broadcasted_iota(jnp.int32, (n, n), 0) == lax.broadcasted_iota(jnp.int32, (n, n), 1)).astype(f32)
    p = -low
    x = eye[None] + p
    for _ in range(5):
        p = _bdot(p, p, 2, 1, MID)
        x = x + _bdot(x, p, 2, 1, MID)
    return x


@jax.custom_vjp
def _unit_lower_inverse(low):
    return _neumann_inverse(low)


def _uli_fwd(low):
    t = _neumann_inverse(low)
    return t, t


def _uli_bwd(t, dt):
    return (-_bdot(_bdot(t, dt, 1, 1, MID), t, 2, 2, MID),)


_unit_lower_inverse.defvjp(_uli_fwd, _uli_bwd)


def _stack(xs):
    return jnp.concatenate([x[None] for x in xs], axis=0)


DELTA_CHUNKS = 2


def _delta_chunks(qkv, bg, state, differentiated):
    inverse = _unit_lower_inverse if differentiated else _neumann_inverse
    lo = _bdot_bf16_vjp if differentiated else _bdot_bf16
    B, n = qkv.shape[0], qkv.shape[1] // CH
    G = B * DNH
    N = n * G
    triples = [(i, b, h) for i in range(n) for b in range(B) for h in range(DNH)]
    col = lambda i, b, h, kind: qkv[b, i * CH:(i + 1) * CH, (3 * h + kind) * DND:(3 * h + kind + 1) * DND]
    q, k, v = [_stack([col(i, b, h, kind) for i, b, h in triples]) for kind in range(3)]
    lane = lax.broadcasted_iota(jnp.int32, (CH, LANE), 1)
    pick = lambda i, b, l: jnp.sum(jnp.where(lane == l, bg[b, i * CH:(i + 1) * CH], 0.0), axis=1, keepdims=True)
    beta = _stack([pick(i, b, h) for i, b, h in triples])
    g = _stack([pick(i, b, h + DNH) for i, b, h in triples])
    ri = lax.broadcasted_iota(jnp.int32, (CH, CH), 0)
    ci = lax.broadcasted_iota(jnp.int32, (CH, CH), 1)
    incl, strict = (ri >= ci)[None], (ri > ci)[None]
    gc = _bdot(jnp.broadcast_to(incl.astype(f32), (N, CH, CH)), jnp.broadcast_to(g, (N, CH, LANE)), 2, 1, MID)
    e0 = jnp.broadcast_to((lane == 0).astype(f32)[None], (N, CH, LANE))
    gc_row = _bdot(e0, gc, 2, 2, MID)
    diff = gc[:, :, :CH] - gc_row
    decay = jnp.where(incl, jnp.exp(jnp.where(incl, diff, 0.0)), 0.0)
    qs = q * (DND ** -0.5)
    kb, vb = k * beta, v * beta
    eg = jnp.exp(gc)
    with_k = lo(jnp.concatenate([kb, qs], axis=1), k, 2, 2)
    low = jnp.where(strict, with_k[:, :CH] * decay, 0.0)
    intra = jnp.where(incl, with_k[:, CH:] * decay, 0.0)
    tinv = inverse(low)
    solved = _bdot(tinv, jnp.concatenate([vb, kb * eg], axis=2), 2, 1, MID)
    gl = gc[:, CH - 1:CH, :]
    k_tail = k * jnp.exp(gl - gc)
    to_state = jnp.concatenate([solved[:, :, DND:], qs * eg], axis=1)
    decay_all = jnp.exp(gl)
    outs = []
    for i in range(n):
        sl = slice(i * G, (i + 1) * G)
        with_state = lo(to_state[sl], state, 2, 1)
        v_new = solved[sl, :, :DND] - with_state[:, :CH]
        outs.append(with_state[:, CH:] + lo(intra[sl], v_new, 2, 1))
        state = state * decay_all[sl] + lo(k_tail[sl], v_new, 1, 1)
    return outs, state


def delta_fwd(qkvn, bg, comm):
    B, S, _ = qkvn.shape
    n = DELTA_CHUNKS if (S // CH) % DELTA_CHUNKS == 0 else 1
    steps, G, rows = S // (n * CH), B * DNH, n * CH

    def body(qkv_ref, bg_ref, o_ref, st_ref, state):
        @pl.when(pl.program_id(0) == 0)
        def _():
            state[...] = jnp.zeros(state.shape, f32)
        s0 = state[...]
        st_ref[0] = s0
        outs, s1 = _delta_chunks(qkv_ref[...], bg_ref[...], s0, False)
        for i, o in enumerate(outs):
            for b in range(B):
                for h in range(DNH):
                    o_ref[b, i * CH:(i + 1) * CH, h * DND:(h + 1) * DND] = o[b * DNH + h]
        state[...] = s1

    at = lambda c: lambda: pl.program_id(0) == c
    return pl.pallas_call(
        _ride(body, 2, 2, 1, comm, at(0), at((7 * steps) // 8), at(steps - 1)), name="delta_fwd", grid=(steps,),
        in_specs=[pl.BlockSpec((B, rows, 3 * DNH * DND), lambda c: (0, c, 0)), pl.BlockSpec((B, rows, LANE), lambda c: (0, c, 0))] + comm.specs,
        out_specs=[pl.BlockSpec((B, rows, DNH * DND), lambda c: (0, c, 0)), pl.BlockSpec((1, G, DND, DND), lambda c: (c, 0, 0, 0))] + comm.specs,
        out_shape=[jax.ShapeDtypeStruct((B, S, DNH * DND), f32), jax.ShapeDtypeStruct((steps, G, DND, DND), f32)] + comm.out_shape,
        scratch_shapes=[pltpu.VMEM((G, DND, DND), f32)] + comm.scratch, compiler_params=_cp(("arbitrary",)),
    )(qkvn, bg, *comm.arrs)


def delta_bwd(qkvn, bg, states, do, comm):
    B, S, _ = qkvn.shape
    steps, G = states.shape[0], B * DNH
    rows = S // steps
    n = rows // CH

    def body(qkv_ref, bg_ref, st_ref, do_ref, dqkv_ref, dbg_ref, dstate):
        @pl.when(pl.program_id(0) == 0)
        def _():
            dstate[...] = jnp.zeros(dstate.shape, f32)
        _, vjp = jax.vjp(lambda a, g, s: _delta_chunks(a, g, s, True), qkv_ref[...], bg_ref[...], st_ref[0])
        do = [_stack([do_ref[b, i * CH:(i + 1) * CH, h * DND:(h + 1) * DND] for b in range(B) for h in range(DNH)]) for i in range(n)]
        dqkv, dbg, ds = vjp((do, dstate[...]))
        dqkv_ref[...] = dqkv
        dbg_ref[...] = dbg
        dstate[...] = ds

    rev = lambda c: steps - 1 - c
    at = lambda c: lambda: pl.program_id(0) == c
    return pl.pallas_call(
        _ride(body, 4, 2, 1, comm, at(0), at(steps - 1), at(steps - 1)), name="delta_bwd", grid=(steps,),
        in_specs=[pl.BlockSpec((B, rows, 3 * DNH * DND), lambda c: (0, rev(c), 0)), pl.BlockSpec((B, rows, LANE), lambda c: (0, rev(c), 0)),
                  pl.BlockSpec((1, G, DND, DND), lambda c: (rev(c), 0, 0, 0)),
                  pl.BlockSpec((B, rows, DNH * DND), lambda c: (0, rev(c), 0))] + comm.specs,
        out_specs=[pl.BlockSpec((B, rows, 3 * DNH * DND), lambda c: (0, rev(c), 0)),
                   pl.BlockSpec((B, rows, LANE), lambda c: (0, rev(c), 0))] + comm.specs,
        out_shape=[jax.ShapeDtypeStruct((B, S, 3 * DNH * DND), f32), jax.ShapeDtypeStruct((B, S, LANE), f32)] + comm.out_shape,
        scratch_shapes=[pltpu.VMEM((G, DND, DND), f32)] + comm.scratch, compiler_params=_cp(("arbitrary",)),
    )(qkvn, bg, states, do, *comm.arrs)


GELU_C0, GELU_C1 = math.sqrt(2.0 / math.pi), 0.044715


def _ffn_specs(S):
    nblk = DFF // LANE
    return [pl.BlockSpec((1, S, LANE), lambda i, b: (b, 0, i)), pl.BlockSpec((1, S, LANE), lambda i, b: (b, 0, nblk + i)),
            pl.BlockSpec((FK, LANE), lambda i, b: (0, i)), pl.BlockSpec((FK, LANE), lambda i, b: (0, nblk + i))]


def ffnconv_fwd(up, conv_w):
    B, S, _ = up.shape
    rows = min(FFN_ROWS, S)

    def body(g_ref, v_ref, gw_ref, vw_ref, o_ref, xg, xv):
        _stage_rows(xg, g_ref[0].astype(f32))
        _stage_rows(xv, v_ref[0].astype(f32))
        gw, vw = gw_ref[...], vw_ref[...]
        for r in range(0, S, rows):
            g, _ = _conv_rows(xg, gw, FK, r, rows)
            v, _ = _conv_rows(xv, vw, FK, r, rows)
            t = jnp.tanh(GELU_C0 * (g * (1.0 + GELU_C1 * (g * g))))
            o_ref[0, pl.ds(r, rows), :] = (0.5 * g * (1.0 + t) * v).astype(o_ref.dtype)

    return pl.pallas_call(
        body, name="ffnconv_fwd", grid=(DFF // LANE, B), in_specs=_ffn_specs(S),
        out_specs=pl.BlockSpec((1, S, LANE), lambda i, b: (b, 0, i)), out_shape=jax.ShapeDtypeStruct((B, S, DFF), bf16),
        scratch_shapes=[pltpu.VMEM((S + 8, LANE), f32)] * 2, compiler_params=_cp(("parallel", "parallel")),
    )(up, up, conv_w, conv_w)


def ffnconv_bwd(up, conv_w, dact, comm):
    B, S, _ = up.shape
    rows = min(FFN_ROWS, S)

    def body(g_ref, v_ref, gw_ref, vw_ref, dy_ref, dx_ref, dw_ref, xg, xv, dg, dv):
        _stage_rows(xg, g_ref[0].astype(f32))
        _stage_rows(xv, v_ref[0].astype(f32))
        gw, vw = gw_ref[...], vw_ref[...]
        dgw = [jnp.zeros((8, LANE), f32) for _ in range(FK)]
        dvw = [jnp.zeros((8, LANE), f32) for _ in range(FK)]
        for r in range(0, S, rows):
            g, gwins = _conv_rows(xg, gw, FK, r, rows)
            v, vwins = _conv_rows(xv, vw, FK, r, rows)
            g2 = g * g
            t = jnp.tanh(GELU_C0 * (g * (1.0 + GELU_C1 * g2)))
            half = 0.5 * (1.0 + t)
            dgelu = half + (0.5 * GELU_C0) * g * (1.0 - t * t) * (1.0 + (3.0 * GELU_C1) * g2)
            dy = dy_ref[0, pl.ds(r, rows), :].astype(f32)
            dvc = dy * (g * half)
            dgc = dy * v * dgelu
            dg[pl.ds(r, rows), :] = dgc
            dv[pl.ds(r, rows), :] = dvc
            for j in range(FK):
                dgw[j] = dgw[j] + _fold8(dgc * gwins[j])
                dvw[j] = dvw[j] + _fold8(dvc * vwins[j])
        dg[S:S + 8] = jnp.zeros((8, LANE), f32)
        dv[S:S + 8] = jnp.zeros((8, LANE), f32)
        for r in range(0, S, rows):
            dx_ref[0, 0, pl.ds(r, rows), :] = _conv_rows_t(dg, gw, FK, r, rows).astype(dx_ref.dtype)
            dx_ref[1, 0, pl.ds(r, rows), :] = _conv_rows_t(dv, vw, FK, r, rows).astype(dx_ref.dtype)

        @pl.when(pl.program_id(1) == 0)
        def _():
            dw_ref[...] = jnp.zeros(dw_ref.shape, f32)
        dw_ref[0] += jnp.concatenate([jnp.sum(d, axis=0, keepdims=True) for d in dgw], axis=0)
        dw_ref[1] += jnp.concatenate([jnp.sum(d, axis=0, keepdims=True) for d in dvw], axis=0)

    nblk = DFF // LANE
    at = lambda i, b: lambda: (pl.program_id(0) == i) & (pl.program_id(1) == b)
    return pl.pallas_call(
        _ride(body, 5, 2, 4, comm, at(0, 0), at(nblk - 1, B - 1), at(nblk - 1, B - 1)), name="ffnconv_bwd", grid=(nblk, B),
        in_specs=_ffn_specs(S) + [pl.BlockSpec((1, S, LANE), lambda i, b: (b, 0, i))] + comm.specs,
        out_specs=[pl.BlockSpec((2, 1, S, LANE), lambda i, b: (0, b, 0, i)),
                   pl.BlockSpec((2, FK, LANE), lambda i, b: (0, 0, i))] + comm.specs,
        out_shape=[jax.ShapeDtypeStruct((2, B, S, DFF), bf16), jax.ShapeDtypeStruct((2, FK, DFF), f32)] + comm.out_shape,
        scratch_shapes=[pltpu.VMEM((S + 8, LANE), f32)] * 4 + comm.scratch, compiler_params=_cp(("arbitrary", "arbitrary")),
    )(up, up, conv_w, conv_w, dact, *comm.arrs)


def ada_fwd(c_all, ada_w, ada_b):
    def body(c_ref, w_ref, b_ref, o_ref):
        c = c_ref[...]
        act = (c * jax.nn.sigmoid(c)).astype(bf16)
        o_ref[...] = jnp.dot(act, w_ref[...].astype(bf16), preferred_element_type=f32) + b_ref[...]

    return pl.pallas_call(body, name="ada_fwd", out_shape=jax.ShapeDtypeStruct((c_all.shape[0], ada_w.shape[1]), f32),
                          compiler_params=pltpu.CompilerParams(vmem_limit_bytes=VMEM_LIMIT))(c_all, ada_w, ada_b)


def ada_bwd(c_all, dmod):
    def body(c_ref, d_ref, o_ref):
        c = c_ref[...]
        act = (c * jax.nn.sigmoid(c)).astype(bf16)
        o_ref[...] = lax.dot_general(act, d_ref[...].astype(bf16), (((0,), (0,)), ((), ())), preferred_element_type=f32)

    return pl.pallas_call(body, name="ada_bwd", out_shape=jax.ShapeDtypeStruct((c_all.shape[1], dmod.shape[1]), f32),
                          compiler_params=pltpu.CompilerParams(vmem_limit_bytes=VMEM_LIMIT))(c_all, dmod)


def loss_head(h1, y2, target, g2, w):
    def fn(t, b, c):
        h, y, tg = [v.astype(f32) for v in t]

        def loss_fn(h, y, g, w):
            e = h + g * _rms(y, w) - tg
            return 0.5 * jnp.sum(jnp.mean(e * e, axis=-1))

        loss, grads = jax.value_and_grad(loss_fn, argnums=(0, 1, 2, 3))(h, y, b[0], c[0])
        return [grads[0], grads[1]], [grads[2], grads[3], jnp.full((1, LANE), loss, f32)]

    return rowcall("loss_head", fn, [(h1, D, 0), (y2, D, 0), (target, D, 0)], [g2], [w], [(D, f32), (D, bf16)],
                   [(1, D), (1, D), (1, LANE)])


def adamw(w, gparts, m, v, name):
    R, C = w.shape
    P = gparts.shape[0]
    budget = 2 * 1024 * 1024
    tr, tc = R, C
    if R * C * 4 > budget and R % 8 == 0:
        tr = max(t for t in range(8, R + 1, 8) if R % t == 0 and t * C * 4 <= budget)
    elif R * C * 4 > budget:
        tc = max(t for t in range(LANE, C + 1, LANE) if C % t == 0 and R * t * 4 <= budget)

    def body(w_ref, g_ref, m_ref, v_ref, go, do, mo, vo):
        g = g_ref[0].astype(f32)
        for p in range(1, P):
            g = g + g_ref[p].astype(f32)
        m2 = B1 * m_ref[...] + (1.0 - B1) * g
        v2 = B2 * v_ref[...] + (1.0 - B2) * jnp.square(g)
        m_hat = m2 * (1.0 / (1.0 - B1 ** STEP))
        v_hat = v2 * (1.0 / (1.0 - B2 ** STEP))
        go[...] = g
        do[...] = -LR * (m_hat / (jnp.sqrt(v_hat) + EPS) + WD * w_ref[...])
        mo[...] = m2
        vo[...] = v2

    blk = pl.BlockSpec((tr, tc), lambda i, j: (i, j))
    return pl.pallas_call(
        body, name=name, grid=(R // tr, C // tc), in_specs=[blk, pl.BlockSpec((P, tr, tc), lambda i, j: (0, i, j)), blk, blk],
        out_specs=[blk] * 4, out_shape=[jax.ShapeDtypeStruct((R, C), f32)] * 4, compiler_params=_cp(("parallel", "parallel")),
    )(w, gparts, m, v)


def _pack_w_in(wt):
    aq, ak, av, dqkv, dz, dbeta, da, ga, gd = jnp.split(wt, np.cumsum(IN_SPLITS)[:-1].tolist(), axis=0)
    ba = jnp.pad(jnp.concatenate([dbeta, da], axis=0), ((0, LANE - 2 * DNH), (0, 0)))
    return jnp.concatenate([ga, gd, aq, dqkv, dz, ak, av, ba], axis=0)


def _unpack_w_in(p):
    row = lambda cb, n: p[cb * LANE: cb * LANE + n]
    ba = row(CB_BA, 2 * DNH)
    return jnp.concatenate([row(CB_AQ, HQ * HD), row(CB_AK, HKV * HD), row(CB_AV, HKV * HD), row(CB_DQKV, 3 * DNH * DND),
                            row(CB_DZ, DNH * DND), ba[:DNH], ba[DNH:], row(CB_GA, D), row(CB_GD, D)], axis=0)


def _cols_gathered(g):
    return g.transpose(1, 0, 2).reshape(g.shape[1], NDEV * g.shape[2])


def _cols_split(w):
    r = w.shape[0]
    return w.reshape(r, NDEV, w.shape[1] // NDEV).transpose(1, 0, 2)


def kernel(x, c, ada_w, ada_b, norm_mix_pre, norm_mix_post, norm_ffn_pre, norm_ffn_post, w_in, dn_conv_w, dn_a_log, dn_dt_bias, dn_norm_w, attn_sinks, rel_bias, w_attn_branch, w_dn_branch, w_out, ffn_w_up, ffn_conv_w, ffn_w_down, loss_target, m_ada_w, m_ada_b, m_norm_mix_pre, m_norm_mix_post, m_norm_ffn_pre, m_norm_ffn_post, m_w_in, m_dn_conv_w, m_dn_a_log, m_dn_dt_bias, m_dn_norm_w, m_attn_sinks, m_rel_bias, m_w_attn_branch, m_w_dn_branch, m_w_out, m_ffn_w_up, m_ffn_conv_w, m_ffn_w_down, v_ada_w, v_ada_b, v_norm_mix_pre, v_norm_mix_post, v_norm_ffn_pre, v_norm_ffn_post, v_w_in, v_dn_conv_w, v_dn_a_log, v_dn_dt_bias, v_dn_norm_w, v_attn_sinks, v_rel_bias, v_w_attn_branch, v_w_dn_branch, v_w_out, v_ffn_w_up, v_ffn_conv_w, v_ffn_w_down):
    B, S, _ = x.shape
    T = B * S
    me = 4 * lax.axis_index("x") + 2 * lax.axis_index("y") + lax.axis_index("c")
    big = dict(w_in=w_in, dn_conv_w=dn_conv_w, w_attn_branch=w_attn_branch, w_dn_branch=w_dn_branch, w_out=w_out,
               ffn_w_up=ffn_w_up, ffn_conv_w=ffn_conv_w, ffn_w_down=ffn_w_down)
    big_names = list(big)

    first, mid, late = ["w_in", "dn_conv_w"], ["w_attn_branch", "w_dn_branch", "w_out"], ["ffn_w_up", "ffn_conv_w"]
    transposed = ("w_in", "ffn_w_up")
    local = lambda n, a: a[0].T if n in transposed else a[0]
    shard = lambda names: [local(n, big[n]).astype(bf16) for n in names]
    *got, c_all = _exchange(shard(first) + [c], "gather_w_in", two_level=True)
    gw = dict(zip(first, got))
    c_all = c_all.reshape(NDEV * B, D)

    wp = _pack_w_in(gw["w_in"].reshape(IN_DIM, D))
    conv_dn = _cols_gathered(gw["dn_conv_w"]).astype(f32)

    ncol = ada_w.shape[2]
    ada_b_mine = lax.dynamic_slice_in_dim(ada_b, me * ncol, ncol, axis=1)
    mod_cols = ada_fwd(c_all, ada_w[0], ada_b_mine)
    (mod_g,) = _exchange([mod_cols], "gather_mod")
    mod = lax.dynamic_slice_in_dim(mod_g, me * B, B, axis=1).transpose(1, 0, 2).reshape(B, NMOD * D)
    sh1, sc1, g1, sh2, sc2, g2 = [mod[:, i * D:(i + 1) * D].reshape(B, 1, D) for i in range(NMOD)]

    onehot = (jnp.asarray(_bucket_table()).reshape(1, -1) == jnp.arange(NBUCK, dtype=jnp.int32)[:, None]).astype(f32)
    bias = mm(rel_bias.T, onehot, "nn", f32, "bias_table", tn=8192, precision=HI).reshape(HQ, WIN, 2 * WIN)
    sinks = attn_sinks.reshape(HQ, 1, 1)
    a_log_pad = jnp.pad(dn_a_log, ((0, 0), (DNH, LANE - 2 * DNH)))
    dt_bias_pad = jnp.pad(dn_dt_bias, ((0, 0), (DNH, LANE - 2 * DNH)))

    (u1,) = rowcall_fwd("mix_pre", f_rms_mod, [(x, D, 0)], [sc1, sh1], [norm_mix_pre], [(D, bf16)])
    proj, gw["ffn_w_down"] = mm(u1.reshape(T, D), wp, "nt", bf16, "proj", tm=512, tn=CB_BA * LANE, b_cols=(0, 1),
                                comm=_Comm(shard(["ffn_w_down"]), two_level=True))
    proj = proj.reshape(B, S, CB_BA * LANE)
    ba = mm(u1.reshape(T, D), wp, "nt", f32, "proj_ba", tn=LANE, b_cols=(CB_BA, 1)).reshape(B, S, LANE)
    ya, *got = attn_fwd(proj, bias, sinks, _Comm(shard(mid), two_level=True))
    gw.update(zip(mid, got))
    wa = _cols_gathered(gw["w_attn_branch"])
    wd = _cols_gathered(gw["w_dn_branch"])
    wo = gw["w_out"].reshape(D, D)
    qkvn = dnconv_fwd(proj, conv_dn)
    (bg,) = rowcall_fwd("dn_gate", f_gate, [(ba, LANE, 0)], [], [a_log_pad, dt_bias_pad], [(LANE, f32)])
    o_dn, states, *got = delta_fwd(qkvn, bg, _Comm(shard(late), two_level=True))
    gw.update(zip(late, got))
    wup = gw["ffn_w_up"].reshape(2 * DFF, D)
    conv_ffn = _cols_gathered(gw["ffn_conv_w"]).astype(f32)
    wdown = gw["ffn_w_down"].reshape(DFF, D)
    (yd,) = rowcall_fwd("dn_out", f_dnout, [(o_dn, DNH * DND, 0), (proj, DNH * DND, CB_DZ // 4)], [], [dn_norm_w], [(DNH * DND, bf16)])
    pa = mm(ya.reshape(T, HQ * HD), wa, "nn", bf16, "attn_branch").reshape(B, S, D)
    pd = mm(yd.reshape(T, DNH * DND), wd, "nn", bf16, "dn_branch").reshape(B, S, D)
    merge_tok = [(proj, D, CB_GA // 8), (proj, D, CB_GD // 8), (pa, D, 0), (pd, D, 0)]
    (merged,) = rowcall_fwd("merge", f_merge, merge_tok, [], [], [(D, bf16)])
    y1 = mm(merged.reshape(T, D), wo, "nn", bf16, "mix_out").reshape(B, S, D)
    post_pre = ([(x, D, 0), (y1, D, 0)], [g1, sc2, sh2], [norm_mix_post, norm_ffn_pre])
    h1, u2 = rowcall_fwd("mix_post_ffn_pre", f_post_pre, *post_pre, [(D, f32), (D, bf16)])
    up = mm(u2.reshape(T, D), wup, "nt", bf16, "ffn_up", tn=2816).reshape(B, S, 2 * DFF)
    act = ffnconv_fwd(up, conv_ffn)
    y2 = mm(act.reshape(T, DFF), wdown, "nn", bf16, "ffn_down", tk=2816).reshape(B, S, D)

    dh1_a, dy2, dg2, dw_ffn_post, loss_b = loss_head(h1, y2, loss_target, g2, norm_ffn_post)
    dy2f = dy2.reshape(T, D)
    dact = mm(dy2f, wdown, "nt", bf16, "ffn_down_dx", tn=2816).reshape(B, S, DFF)
    g_wdown = mm(act.reshape(T, DFF), dy2f, "tn", f32, "ffn_down_dw", tm=2816, tn=512, tk=4096)
    parts = {}
    outbox = lambda d: _Comm([d[n].astype(bf16) for n in d], scatter=True)
    dup, g_conv_ffn, parts["ffn_w_down"] = ffnconv_bwd(up, conv_ffn, dact, outbox(dict(ffn_w_down=g_wdown.reshape(NDEV, DFF // NDEV, D))))
    dupf = dup.reshape(2, T, DFF)
    g_conv_ffn = g_conv_ffn.transpose(1, 0, 2).reshape(FK, 2 * DFF)
    du2 = mm(dupf, wup, "nn", bf16, "ffn_up_dx", tk=2816).reshape(B, S, D)
    g_wup = mm(dupf, u2.reshape(T, D), "tn", f32, "ffn_up_dw", tm=1408, tk=2048)
    dh1, dy1, dg1, dsc2, dsh2, dw_mix_post, dw_ffn_pre = rowcall_bwd(
        "mix_post_ffn_pre_bwd", f_post_pre, *post_pre, [(dh1_a, D, 0), (du2, D, 0)], [(0, f32), (1, bf16)])
    dy1f = dy1.reshape(T, D)
    dmerged = mm(dy1f, wo, "nt", bf16, "mix_out_dx").reshape(B, S, D)
    g_wo = mm(merged.reshape(T, D), dy1f, "tn", f32, "mix_out_dw", tk=2048)
    dga, dgd, dpa, dpd = rowcall_bwd("merge_bwd", f_merge, merge_tok, [], [], [(dmerged, D, 0)],
                                     [(0, bf16), (1, bf16), (2, bf16), (3, bf16)])
    dpaf, dpdf = dpa.reshape(T, D), dpd.reshape(T, D)
    dya = mm(dpaf, wa, "nt", bf16, "attn_branch_dx").reshape(B, S, HQ * HD)
    g_wa = mm(ya.reshape(T, HQ * HD), dpaf, "tn", f32, "attn_branch_dw", tk=2048)
    dyd = mm(dpdf, wd, "nt", bf16, "dn_branch_dx").reshape(B, S, DNH * DND)
    g_wd = mm(yd.reshape(T, DNH * DND), dpdf, "tn", f32, "dn_branch_dw", tk=2048)
    do_dn, dz, dw_dn_norm = rowcall_bwd("dn_out_bwd", f_dnout, [(o_dn, DNH * DND, 0), (proj, DNH * DND, CB_DZ // 4)], [], [dn_norm_w],
                                        [(dyd, DNH * DND, 0)], [(0, f32), (1, bf16)])
    send = dict(ffn_w_up=g_wup.reshape(NDEV, 2 * DFF // NDEV, D), ffn_conv_w=_cols_split(g_conv_ffn))
    dqkvn, dbg, *got = delta_bwd(qkvn, bg, states, do_dn, outbox(send))
    parts.update(zip(send, got))
    dba, da_log_pad, ddt_bias_pad = rowcall_bwd("dn_gate_bwd", f_gate, [(ba, LANE, 0)], [], [a_log_pad, dt_bias_pad],
                                                [(dbg, LANE, 0)], [(0, bf16)])
    ddqkv, g_conv_dn = dnconv_bwd(proj, conv_dn, dqkvn)
    send = dict(w_attn_branch=_cols_split(g_wa), w_dn_branch=_cols_split(g_wd),
                w_out=g_wo.reshape(NDEV, D // NDEV, D))
    dq, dk, dv, dbias, dsinks, *got = attn_bwd(proj, bias, sinks, dya, outbox(send))
    parts.update(zip(send, got))
    dproj = jnp.concatenate([dga, dgd, dq, ddqkv, dz, dk, dv, dba], axis=2).reshape(T, NP)
    g_wp = mm(dproj, u1.reshape(T, D), "tn", f32, "proj_dw", tm=1664, tk=1024)
    send = dict(w_in=_unpack_w_in(g_wp).reshape(NDEV, IN_DIM // NDEV, D), dn_conv_w=_cols_split(g_conv_dn))
    du1, *got = mm(dproj, wp, "nn", bf16, "proj_dx", tm=512, tk=NP, comm=outbox(send))
    parts.update(zip(send, got))
    du1 = du1.reshape(B, S, D)
    grad_x, dsc1, dsh1, dw_mix_pre = rowcall_bwd("mix_pre_bwd", f_rms_mod, [(x, D, 0)], [sc1, sh1], [norm_mix_pre], [(du1, D, 0)],
                                                 [(0, f32)], add=(dh1, D, 0))
    g_rel = mm(dbias.reshape(HQ, WIN * 2 * WIN), onehot, "nt", f32, "rel_bias_dw", tk=8192, precision=HI)

    dmod = jnp.concatenate([dsh1, dsc1, dg1, dsh2, dsc2, dg2], axis=2).reshape(B, NMOD * D)

    zrow = lambda a: jnp.concatenate([a.reshape(1, -1), jnp.zeros((B - 1, a.size), f32)], axis=0)
    small_g = jnp.concatenate([
        dmod, dw_mix_pre.reshape(B, D), dw_mix_post.reshape(B, D), dw_ffn_pre.reshape(B, D), dw_ffn_post.reshape(B, D),
        da_log_pad.reshape(B, LANE)[:, DNH:2 * DNH], ddt_bias_pad.reshape(B, LANE)[:, DNH:2 * DNH], dw_dn_norm.reshape(B, DND),
        zrow(dsinks), zrow(g_rel.T), loss_b.reshape(B, LANE)[:, :1], jnp.zeros((B, SMALL_PAD - SMALL_N - 1), f32)], axis=1)
    (small_all,) = _exchange([small_g], "gather_small")
    dmod_cols = lax.dynamic_slice_in_dim(small_all.reshape(NDEV * B, SMALL_PAD), me * ncol, ncol, axis=1)
    g_ada_w = ada_bwd(c_all, dmod_cols)
    small_w = dict(ada_b=(ada_b, m_ada_b, v_ada_b), norm_mix_pre=(norm_mix_pre, m_norm_mix_pre, v_norm_mix_pre),
                   norm_mix_post=(norm_mix_post, m_norm_mix_post, v_norm_mix_post), norm_ffn_pre=(norm_ffn_pre, m_norm_ffn_pre, v_norm_ffn_pre),
                   norm_ffn_post=(norm_ffn_post, m_norm_ffn_post, v_norm_ffn_post), dn_a_log=(dn_a_log, m_dn_a_log, v_dn_a_log),
                   dn_dt_bias=(dn_dt_bias, m_dn_dt_bias, v_dn_dt_bias), dn_norm_w=(dn_norm_w, m_dn_norm_w, v_dn_norm_w),
                   attn_sinks=(attn_sinks, m_attn_sinks, v_attn_sinks), rel_bias=(rel_bias, m_rel_bias, v_rel_bias))

    def pack(i, fill):
        row = jnp.concatenate([small_w[n][i].reshape(1, -1) for n, _ in SMALL], axis=1)
        return jnp.pad(row, ((0, 0), (0, SMALL_PAD - SMALL_N)), constant_values=fill)

    small_out = adamw(pack(0, 0.0), small_all.reshape(NDEV * B, 1, SMALL_PAD), pack(1, 0.0), pack(2, 1.0), "adamw_small")
    loss = small_out[0][0, SMALL_N]

    res = {}
    off = 0
    for n, size in SMALL:
        shp = small_w[n][0].shape
        res[n] = [o[:, off:off + size].reshape(shp) for o in small_out]
        off += size
    res["ada_w"] = [o[None] for o in adamw(ada_w[0], g_ada_w[None], m_ada_w[0], v_ada_w[0], "adamw_ada_w")]
    moments = dict(w_in=(m_w_in, v_w_in), dn_conv_w=(m_dn_conv_w, v_dn_conv_w), w_attn_branch=(m_w_attn_branch, v_w_attn_branch),
                   w_dn_branch=(m_w_dn_branch, v_w_dn_branch), w_out=(m_w_out, v_w_out), ffn_w_up=(m_ffn_w_up, v_ffn_w_up),
                   ffn_conv_w=(m_ffn_conv_w, v_ffn_conv_w), ffn_w_down=(m_ffn_w_down, v_ffn_w_down))
    for n in big_names:
        outs = adamw(local(n, big[n]), parts[n], local(n, moments[n][0]), local(n, moments[n][1]), "adamw_" + n)
        res[n] = [(o.T if n in transposed else o)[None] for o in outs]

    order = ["ada_w", "ada_b", "norm_mix_pre", "norm_mix_post", "norm_ffn_pre", "norm_ffn_post", "w_in", "dn_conv_w", "dn_a_log",
             "dn_dt_bias", "dn_norm_w", "attn_sinks", "rel_bias", "w_attn_branch", "w_dn_branch", "w_out", "ffn_w_up", "ffn_conv_w",
             "ffn_w_down"]
    return (loss, grad_x, *[res[n][0] for n in order], *[res[n][1] for n in order], *[res[n][2] for n in order],
            *[res[n][3] for n in order])
```

```python
import functools
import math

import numpy as np
import jax
import jax.numpy as jnp
from jax import lax
from jax.experimental import pallas as pl
from jax.experimental.pallas import tpu as pltpu

f32 = jnp.float32
bf16 = jnp.bfloat16
HI = lax.Precision.HIGHEST
MID = lax.Precision.HIGH
MESH = pl.DeviceIdType.MESH

NDEV = 8
D = 1024
HQ, HKV, HD, WIN, NBUCK, MAXDIST = 8, 2, 64, 128, 32, 128
DNH, DND, DNK, CH = 4, 128, 4, 64
DFF, FK = 2816, 3
NMOD = 6
RMS_EPS = 1e-6
L2_EPS = 1e-6
NEG_INF = -1e30
LR, B1, B2, EPS, WD, STEP = 0.001, 0.9, 0.999, 1e-08, 0.01, 10

LANE = 128
CB_GA, CB_GD, CB_AQ, CB_DQKV, CB_DZ, CB_AK, CB_AV, CB_BA, NPB = 0, 8, 16, 20, 32, 36, 37, 38, 39
NP = NPB * LANE
IN_SPLITS = (HQ * HD, HKV * HD, HKV * HD, 3 * DNH * DND, DNH * DND, DNH, DNH, D, D)
IN_DIM = sum(IN_SPLITS)
VMEM_LIMIT = 56 * 1024 * 1024

SMALL = (("ada_b", NMOD * D), ("norm_mix_pre", D), ("norm_mix_post", D), ("norm_ffn_pre", D), ("norm_ffn_post", D),
         ("dn_a_log", DNH), ("dn_dt_bias", DNH), ("dn_norm_w", DND), ("attn_sinks", HQ), ("rel_bias", NBUCK * HQ))
SMALL_N = sum(n for _, n in SMALL)
SMALL_PAD = 10752


def _cp(sem):
    return pltpu.CompilerParams(dimension_semantics=sem, vmem_limit_bytes=VMEM_LIMIT)


def _pick(dim, target):
    if dim <= target:
        return dim
    best = None
    for d in range(LANE, target + 1, LANE):
        if dim % d == 0:
            best = d
    assert best is not None, (dim, target)
    return best


def _me():
    x, y, c = lax.axis_index("x"), lax.axis_index("y"), lax.axis_index("c")
    return x, y, c, 4 * x + 2 * y + c


def _peer(x, y, c, k):
    px = 1 - x if k & 4 else x
    py = 1 - y if k & 2 else y
    pc = 1 - c if k & 1 else c
    return (px, py, pc), 4 * px + 2 * py + pc


class _Comm:
    def __init__(self, arrs, scatter=False, two_level=False):
        assert not (scatter and two_level)
        self.arrs, self.n, self.scatter, self.two_level = list(arrs), len(arrs), scatter, two_level
        if scatter:
            self.out_shape = [jax.ShapeDtypeStruct(a.shape, a.dtype) for a in arrs]
        else:
            self.out_shape = [jax.ShapeDtypeStruct((NDEV,) + a.shape, a.dtype) for a in arrs]
        nsem = self.n * (NDEV - 1)
        self.scratch = [pltpu.SemaphoreType.DMA((nsem,)), pltpu.SemaphoreType.DMA((nsem,)), pltpu.SemaphoreType.DMA((self.n,))]
        self.specs = [pl.BlockSpec(memory_space=pl.ANY)] * self.n

    def phases(self, ins, out, send, recv, loc):
        x, y, c, me = _me()

        def remote(a, k, src, dst, to):
            s = a * (NDEV - 1) + k - 1
            return pltpu.make_async_remote_copy(src_ref=src, dst_ref=dst, send_sem=send.at[s], recv_sem=recv.at[s],
                                                device_id=to, device_id_type=MESH)

        def local(a):
            return pltpu.make_async_copy(ins[a].at[me] if self.scatter else ins[a], out[a].at[me], loc.at[a])

        if not self.two_level:
            def mine(a, k):
                peer, pid = _peer(x, y, c, k)
                return remote(a, k, ins[a].at[pid] if self.scatter else ins[a], out[a].at[me], peer)

            def theirs(a, k):
                peer, pid = _peer(x, y, c, k)
                return remote(a, k, ins[a].at[pid] if self.scatter else ins[a], out[a].at[pid], peer)

            def start():
                for a in range(self.n):
                    local(a).start()
                    for k in range(1, NDEV):
                        mine(a, k).start()

            def forward():
                pass

            def finish():
                for a in range(self.n):
                    for k in range(1, NDEV):
                        mine(a, k).wait_send()
                    for k in range(1, NDEV):
                        theirs(a, k).wait_recv()
                    local(a).wait()

            return start, forward, finish

        sibling = (x, y, 1 - c)
        chips = [(1 - x, y), (x, 1 - y), (1 - x, 1 - y)]
        slot = lambda px, py, pc: 4 * px + 2 * py + pc

        def own(a, k, to):
            return remote(a, k, ins[a], out[a].at[me], to)

        def landed(a, k, frm):
            return remote(a, k, ins[a], out[a].at[slot(*frm)], frm)

        def passed(a, j):
            rows = out[a].at[slot(*chips[j], c)]
            return remote(a, 5 + j, rows, rows, sibling)

        def start():
            for a in range(self.n):
                local(a).start()
                own(a, 1, sibling).start()
                for j, chip in enumerate(chips):
                    own(a, 2 + j, (*chip, c)).start()

        def forward():
            for a in range(self.n):
                for j, chip in enumerate(chips):
                    landed(a, 2 + j, (*chip, c)).wait_recv()
                    passed(a, j).start()

        def finish():
            for a in range(self.n):
                landed(a, 1, sibling).wait_recv()
                for j, chip in enumerate(chips):
                    remote(a, 5 + j, ins[a], out[a].at[slot(*chip, 1 - c)], sibling).wait_recv()
                own(a, 1, sibling).wait_send()
                for j, chip in enumerate(chips):
                    own(a, 2 + j, (*chip, c)).wait_send()
                    passed(a, j).wait_send()
                local(a).wait()

        return start, forward, finish


def _ride(body, n_in, n_out, n_scr, comm, first, mid, last):
    k = comm.n

    def wrapped(*refs):
        ins, cins = refs[:n_in], refs[n_in:n_in + k]
        o0 = n_in + k
        outs, couts = refs[o0:o0 + n_out], refs[o0 + n_out:o0 + n_out + k]
        s0 = o0 + n_out + k
        scr, sems = refs[s0:s0 + n_scr], refs[s0 + n_scr:]
        start, forward, finish = comm.phases(cins, couts, *sems)
        pl.when(first())(start)
        body(*ins, *outs, *scr)
        pl.when(mid())(forward)
        pl.when(last())(finish)

    return wrapped


def _exchange(arrs, name, scatter=False, two_level=False):
    comm = _Comm(arrs, scatter, two_level)

    def body(*refs):
        start, forward, finish = comm.phases(refs[:comm.n], refs[comm.n:2 * comm.n], *refs[2 * comm.n:])
        start()
        forward()
        finish()

    return pl.pallas_call(body, name=name, out_shape=comm.out_shape, in_specs=comm.specs, out_specs=comm.specs,
                          scratch_shapes=comm.scratch, compiler_params=pltpu.CompilerParams(has_side_effects=True))(*arrs)


def mm(a, b, mode, out_dtype, name, tm=1024, tn=1024, tk=1024, precision=None, comm=None, b_cols=None):
    a_parts = a.shape[0] if a.ndim == 3 else 1
    b_parts = b.shape[0] if b.ndim == 3 else 1
    assert b_parts == 1 or mode == "tn"
    ash, bsh = (a.shape[-2], a.shape[-1] * a_parts), b.shape[-2:]
    if mode == "nn":
        (M, K), (K2, N) = ash, bsh
    elif mode == "nt":
        (M, K), (N, K2) = ash, bsh
    else:
        (K, M), (K2, N) = ash, (bsh[0], bsh[1] * b_parts)
    assert K == K2, (name, a.shape, b.shape)
    col0 = 0
    if b_cols is not None:
        assert mode in ("nn", "nt") and tn % LANE == 0
        col0, N = b_cols[0], b_cols[1] * tn
    if mode == "tn":
        tm, tn, tk = _pick(M // a_parts, tm), _pick(N // b_parts, tn), _pick(K, tk)
    else:
        tm, tn, tk = _pick(M, tm), _pick(N // b_parts, tn), _pick(K // a_parts, tk)
    nk = K // tk
    if mode == "tn" and a_parts > 1:
        per = M // tm // a_parts
        a_spec = pl.BlockSpec((None, tk, tm), lambda i, j, k: (i // per, k, i % per))
    elif mode == "tn":
        a_spec = pl.BlockSpec((tk, tm), lambda i, j, k: (k, i))
    elif a_parts > 1:
        per = nk // a_parts
        a_spec = pl.BlockSpec((None, tm, tk), lambda i, j, k: (k // per, i, k % per))
    else:
        a_spec = pl.BlockSpec((tm, tk), lambda i, j, k: (i, k))
    if mode == "nt":
        b_spec = pl.BlockSpec((tn, tk), lambda i, j, k: (col0 + j, k))
    elif b_parts > 1:
        per = N // tn // b_parts
        b_spec = pl.BlockSpec((None, tk, tn), lambda i, j, k: (j // per, k, j % per))
    else:
        b_spec = pl.BlockSpec((tk, tn), lambda i, j, k: (k, col0 + j))
    dims = {"nn": ((1,), (0,)), "nt": ((1,), (1,)), "tn": ((0,), (0,))}[mode]

    def body(a_ref, b_ref, o_ref, *scr):
        p = lax.dot_general(a_ref[...], b_ref[...], (dims, ((), ())), preferred_element_type=f32, precision=precision)
        if nk == 1:
            o_ref[...] = p.astype(o_ref.dtype)
        else:
            acc = scr[0]
            k = pl.program_id(2)

            @pl.when(k == 0)
            def _():
                acc[...] = p

            @pl.when(k > 0)
            def _():
                acc[...] += p

            @pl.when(k == nk - 1)
            def _():
                o_ref[...] = acc[...].astype(o_ref.dtype)

    grid = (M // tm, N // tn, nk)
    scratch = [pltpu.VMEM((tm, tn), f32)] if nk > 1 else []
    out_spec = pl.BlockSpec((tm, tn), lambda i, j, k: (i, j))
    out_shape = jax.ShapeDtypeStruct((M, N), out_dtype)
    if comm is None:
        return pl.pallas_call(body, name=name, grid=grid, in_specs=[a_spec, b_spec], out_specs=out_spec, out_shape=out_shape,
                              scratch_shapes=scratch, compiler_params=_cp(("parallel", "parallel", "arbitrary")))(a, b)
    at = lambda pos: lambda: functools.reduce(jnp.logical_and, [pl.program_id(d) == p for d, p in enumerate(pos)])
    end = tuple(g - 1 for g in grid)
    return pl.pallas_call(
        _ride(body, 2, 1, len(scratch), comm, at((0, 0, 0)), at(end), at(end)), name=name, grid=grid,
        in_specs=[a_spec, b_spec] + comm.specs, out_specs=[out_spec] + comm.specs, out_shape=[out_shape] + comm.out_shape,
        scratch_shapes=scratch + comm.scratch, compiler_params=_cp(("arbitrary", "arbitrary", "arbitrary")),
    )(a, b, *comm.arrs)


def rowcall(name, fn, tok, bat, con, tok_out, acc_out, ts=256, into=None):
    B, S = tok[0][0].shape[:2]
    ts = min(ts, S)
    nt, nb, nc, no, na = len(tok), len(bat), len(con), len(tok_out), len(acc_out)
    nin = nt + nb + nc + (1 if into is not None else 0)

    def body(*refs):
        tr, br, cr = refs[:nt], refs[nt:nt + nb], refs[nt + nb:nt + nb + nc]
        orf, arf = refs[nin:nin + no], refs[nin + no:]
        touts, aouts = fn([r[0] for r in tr], [r[0] for r in br], [r[...] for r in cr])
        for r, v in zip(orf, touts):
            r[0] = v.astype(r.dtype)
        s = pl.program_id(1)
        for r, v in zip(arf, aouts):
            @pl.when(s == 0)
            def _(r=r):
                r[...] = jnp.zeros(r.shape, r.dtype)
            r[0] += v.astype(f32)

    in_specs = [pl.BlockSpec((1, ts, w), lambda b, s, cb=cb: (b, s, cb)) for (_, w, cb) in tok]
    in_specs += [pl.BlockSpec((1,) + a.shape[1:], lambda b, s: (b, 0, 0)) for a in bat]
    in_specs += [pl.BlockSpec(a.shape, lambda b, s, nd=a.ndim: (0,) * nd) for a in con]
    out_specs = [pl.BlockSpec((1, ts, w), lambda b, s: (b, s, 0)) for (w, _) in tok_out]
    out_specs += [pl.BlockSpec((1,) + shp, lambda b, s, nd=len(shp): (b,) + (0,) * nd) for shp in acc_out]
    out_shape = [jax.ShapeDtypeStruct((B, S, w), dt) for (w, dt) in tok_out]
    out_shape += [jax.ShapeDtypeStruct((B,) + shp, f32) for shp in acc_out]
    extra, aliases = [], {}
    if into is not None:
        buf, cb = into
        assert buf.dtype == tok_out[0][1]
        in_specs.append(pl.BlockSpec(memory_space=pl.ANY))
        out_specs[0] = pl.BlockSpec((1, ts, tok_out[0][0]), lambda b, s: (b, s, cb))
        out_shape[0] = jax.ShapeDtypeStruct(buf.shape, buf.dtype)
        extra, aliases = [buf], {nin - 1: 0}
    return pl.pallas_call(
        body, name=name, grid=(B, S // ts), in_specs=in_specs, out_specs=out_specs, out_shape=out_shape,
        input_output_aliases=aliases, compiler_params=_cp(("parallel", "arbitrary")),
    )(*[t[0] for t in tok], *bat, *con, *extra)


def rowcall_fwd(name, f, tok, bat, con, tok_out, ts=256):
    def fn(t, b, c):
        return f([v.astype(f32) for v in t], b, c), []
    return rowcall(name, fn, tok, bat, con, tok_out, [], ts)


def rowcall_bwd(name, f, tok, bat, con, cts, tok_grads, add=None, ts=256, join_first=1, into=None):
    nt, ncts = len(tok), len(cts)

    def fn(t, b, c):
        prim = [v.astype(f32) for v in t[:nt]]
        ct = [v.astype(f32) for v in t[nt:nt + ncts]]
        _, vjp = jax.vjp(lambda tt, bb, cc: f(tt, bb, cc), prim, b, c)
        dt, db, dc = vjp(ct)
        touts = [dt[i] for i, _ in tok_grads]
        if add is not None:
            touts[0] = touts[0] + t[nt + ncts].astype(f32)
        if join_first > 1:
            touts = [jnp.concatenate(touts[:join_first], axis=1)] + touts[join_first:]
        return touts, list(db) + list(dc)

    all_tok = list(tok) + list(cts) + ([add] if add is not None else [])
    tok_out = [(tok[i][1], dt) for i, dt in tok_grads]
    if join_first > 1:
        tok_out = [(sum(w for w, _ in tok_out[:join_first]), tok_out[0][1])] + tok_out[join_first:]
    acc_out = [tuple(a.shape[1:]) for a in bat] + [tuple(a.shape) for a in con]
    return rowcall(name, fn, all_tok, bat, con, tok_out, acc_out, ts, into)


def _rms(y, w):
    return y * lax.rsqrt(jnp.mean(y * y, axis=-1, keepdims=True) + RMS_EPS) * w


def f_rms_mod(t, b, c):
    return [_rms(t[0], c[0]) * (1.0 + b[0]) + b[1]]


def f_post_pre(t, b, c):
    h1 = t[0] + b[0] * _rms(t[1], c[0])
    return [h1, _rms(h1, c[1]) * (1.0 + b[1]) + b[2]]


def f_merge(t, b, c):
    ga, gd, ya, yd = t
    return [jax.nn.sigmoid(ga) * ya + jax.nn.sigmoid(gd) * yd]


def f_dnout(t, b, c):
    o, z = t
    outs = []
    for h in range(DNH):
        sl = slice(h * DND, (h + 1) * DND)
        zh = z[:, sl]
        outs.append(_rms(o[:, sl], c[0]) * (zh * jax.nn.sigmoid(zh)))
    return [jnp.concatenate(outs, axis=1)]


def _softplus(x):
    return jnp.maximum(x, 0.0) + jnp.log(1.0 + jnp.exp(-jnp.abs(x)))


def f_gate(t, b, c):
    ba = t[0]
    a_log, dt_bias = c
    lane = lax.broadcasted_iota(jnp.int32, ba.shape, 1)
    beta = jax.nn.sigmoid(ba)
    g = -jnp.exp(a_log) * _softplus(ba + dt_bias)
    return [jnp.where(lane < DNH, beta, jnp.where(lane < 2 * DNH, g, 0.0))]


def _bucket_table():
    qi = np.arange(WIN)[:, None]
    kj = np.arange(2 * WIN)[None, :]
    dist = np.maximum(WIN + qi - kj, 0)
    max_exact = NBUCK // 2
    scaled = np.log(np.maximum(dist, 1).astype(np.float64) / max_exact) / math.log(MAXDIST / max_exact)
    large = np.minimum(max_exact + (scaled * (NBUCK - max_exact)).astype(np.int32), NBUCK - 1)
    return np.where(dist < max_exact, dist, large).astype(np.int32)


def _attn_mask(n):
    qi = lax.broadcasted_iota(jnp.int32, (WIN, 2 * WIN), 0)
    kj = lax.broadcasted_iota(jnp.int32, (WIN, 2 * WIN), 1)
    dist = WIN + qi - kj
    return (dist >= 0) & (dist < WIN) & ((kj >= WIN) | (n > 0))


def _swap_halves(x):
    return pltpu.roll(x, HD, axis=x.ndim - 1)


@jax.custom_vjp
def _swap_halves_vjp(x):
    return _swap_halves(x)


_swap_halves_vjp.defvjp(lambda x: (_swap_halves(x), None), lambda _, g: (_swap_halves(g),))


def _attn_block(q, kp, kc, vp, vc, bias, sinks, mask, differentiated):
    dot = _bdot_bf16_vjp if differentiated else _bdot_bf16
    swap = _swap_halves_vjp if differentiated else _swap_halves
    B, grp = q.shape[0], HQ // HKV
    upper = lax.broadcasted_iota(jnp.int32, (2 * WIN, LANE), 1) >= HD

    def placed(natural, swapped, j, half):
        keep = upper if half == 1 else ~upper
        return jnp.where(keep, natural if j == half else swapped, 0.0)

    qh, ks, vs = [], [], []
    for b in range(B):
        kb, vb = jnp.concatenate([kp[b], kc[b]], axis=0), jnp.concatenate([vp[b], vc[b]], axis=0)
        kb_sw, vb_sw = swap(kb), swap(vb)
        for h in range(HQ):
            qh.append(q[b, :, (h // 2) * LANE:(h // 2 + 1) * LANE])
            ks.append(placed(kb, kb_sw, h // grp, h % 2))
            vs.append(placed(vb, vb_sw, h // grp, h % 2))
    s = dot(_stack(qh), _stack(ks), 2, 2).reshape(B, HQ, WIN, 2 * WIN) * (HD ** -0.5)
    s = jnp.where(mask, s + bias, NEG_INF)
    m = jnp.maximum(jnp.max(s, axis=-1, keepdims=True), sinks)
    p = jnp.exp(s - m)
    probs = p / (jnp.sum(p, axis=-1, keepdims=True) + jnp.exp(sinks - m))
    o = dot(probs.reshape(B * HQ, WIN, 2 * WIN), _stack(vs), 2, 1)
    return _stack([jnp.concatenate([o[b * HQ + 2 * i] + o[b * HQ + 2 * i + 1] for i in range(HQ // 2)], axis=1) for b in range(B)])


def _attn_specs(B, NB):
    last = NB - 1
    return [
        pl.BlockSpec((B, WIN, HQ * HD), lambda n: (0, jnp.minimum(n, last), CB_AQ // 4)),
        pl.BlockSpec((B, WIN, LANE), lambda n: (0, jnp.clip(n - 1, 0, last), CB_AK)),
        pl.BlockSpec((B, WIN, LANE), lambda n: (0, jnp.minimum(n, last), CB_AK)),
        pl.BlockSpec((B, WIN, LANE), lambda n: (0, jnp.clip(n - 1, 0, last), CB_AV)),
        pl.BlockSpec((B, WIN, LANE), lambda n: (0, jnp.minimum(n, last), CB_AV)),
        pl.BlockSpec((HQ, WIN, 2 * WIN), lambda n: (0, 0, 0)),
        pl.BlockSpec((HQ, 1, 1), lambda n: (0, 0, 0)),
    ]


def attn_fwd(proj, bias, sinks, comm):
    B, S, _ = proj.shape
    NB = S // WIN

    def body(q, kp, kc, vp, vc, bias_ref, sink_ref, o_ref):
        mask = _attn_mask(pl.program_id(0))
        o = _attn_block(*[r[...].astype(f32) for r in (q, kp, kc, vp, vc)], bias_ref[...], sink_ref[...], mask, False)
        o_ref[...] = o.astype(o_ref.dtype)

    at = lambda n: lambda: pl.program_id(0) == n
    return pl.pallas_call(
        _ride(body, 7, 1, 0, comm, at(0), at((3 * NB) // 4), at(NB - 1)), name="attn_fwd", grid=(NB,),
        in_specs=_attn_specs(B, NB) + comm.specs,
        out_specs=[pl.BlockSpec((B, WIN, HQ * HD), lambda n: (0, n, 0))] + comm.specs,
        out_shape=[jax.ShapeDtypeStruct((B, S, HQ * HD), bf16)] + comm.out_shape, scratch_shapes=comm.scratch,
        compiler_params=_cp(("arbitrary",)),
    )(proj, proj, proj, proj, proj, bias, sinks, *comm.arrs)


def attn_bwd(proj, bias, sinks, dy, dproj, comm):
    B, S, _ = proj.shape
    NB = S // WIN
    last = NB - 1

    def body(q, kp, kc, vp, vc, bias_ref, sink_ref, dy_ref, _, dq_ref, dk_ref, dv_ref, dbias_ref, dsink_ref, kcar, vcar):
        n = pl.program_id(0)

        @pl.when(n == 0)
        def _():
            dbias_ref[...] = jnp.zeros(dbias_ref.shape, f32)
            dsink_ref[...] = jnp.zeros(dsink_ref.shape, f32)
            kcar[...] = jnp.zeros(kcar.shape, f32)
            vcar[...] = jnp.zeros(vcar.shape, f32)

        @pl.when(n < NB)
        def _():
            mask = _attn_mask(n)
            _, vjp = jax.vjp(lambda *a: _attn_block(*a, mask, True), *[r[...].astype(f32) for r in (q, kp, kc, vp, vc)],
                             bias_ref[...], sink_ref[...])
            dq, dkp, dkc, dvp, dvc, dbias, dsink = vjp(dy_ref[...].astype(f32))
            dq_ref[...] = dq.astype(dq_ref.dtype)
            dbias_ref[...] += dbias
            dsink_ref[...] += dsink
            dk_ref[...] = (kcar[...] + dkp).astype(dk_ref.dtype)
            dv_ref[...] = (vcar[...] + dvp).astype(dv_ref.dtype)
            kcar[...] = dkc
            vcar[...] = dvc

        @pl.when(n == NB)
        def _():
            dk_ref[...] = kcar[...].astype(dk_ref.dtype)
            dv_ref[...] = vcar[...].astype(dv_ref.dtype)

    in_specs = _attn_specs(B, NB) + [pl.BlockSpec((B, WIN, HQ * HD), lambda n: (0, jnp.minimum(n, last), 0)),
                                     pl.BlockSpec(memory_space=pl.ANY)]
    kv_out = pl.BlockSpec((B, WIN, LANE), lambda n: (0, jnp.maximum(n - 1, 0), 0))
    at = lambda n: lambda: pl.program_id(0) == n
    return pl.pallas_call(
        _ride(body, 9, 5, 2, comm, at(0), at(NB), at(NB)), name="attn_bwd", grid=(NB + 1,),
        in_specs=in_specs + comm.specs, input_output_aliases={8: 0},
        out_specs=[pl.BlockSpec((B, WIN, HQ * HD), lambda n: (0, jnp.minimum(n, last), CB_AQ // 4)), kv_out, kv_out,
                   pl.BlockSpec((HQ, WIN, 2 * WIN), lambda n: (0, 0, 0)), pl.BlockSpec((HQ, 1, 1), lambda n: (0, 0, 0))] + comm.specs,
        out_shape=[jax.ShapeDtypeStruct(dproj.shape, dproj.dtype), jax.ShapeDtypeStruct((B, S, LANE), bf16),
                   jax.ShapeDtypeStruct((B, S, LANE), bf16), jax.ShapeDtypeStruct((HQ, WIN, 2 * WIN), f32),
                   jax.ShapeDtypeStruct((HQ, 1, 1), f32)] + comm.out_shape,
        scratch_shapes=[pltpu.VMEM((B, WIN, LANE), f32), pltpu.VMEM((B, WIN, LANE), f32)] + comm.scratch,
        compiler_params=_cp(("arbitrary",)),
    )(proj, proj, proj, proj, proj, bias, sinks, dy, dproj, *comm.arrs)


DN_ROWS, FFN_ROWS = 256, 32


def _stage_rows(dst, value):
    dst[0:8] = jnp.zeros((8, LANE), f32)
    dst[8:8 + value.shape[0]] = value


def _conv_rows(xs, w, width, r, rows):
    wins = [xs[pl.ds(r + 8 - (width - 1) + j, rows), :] for j in range(width)]
    out = w[0:1] * wins[0]
    for j in range(1, width):
        out = out + w[j:j + 1] * wins[j]
    return out, wins


def _fold8(v):
    return jnp.sum(v.reshape(v.shape[0] // 8, 8, LANE), axis=0)


def _conv_rows_t(ds, w, width, r, rows):
    out = w[0:1] * ds[pl.ds(r + width - 1, rows), :]
    for j in range(1, width):
        out = out + w[j:j + 1] * ds[pl.ds(r + width - 1 - j, rows), :]
    return out


def _dn_outblk(i):
    return (i % DNH) * 3 + i // DNH


def _dn_act(c, isqk):
    sg = jax.nn.sigmoid(c)
    y = c * sg
    n = lax.rsqrt(jnp.sum(y * y, axis=-1, keepdims=True) + L2_EPS)
    return jnp.where(isqk, y * n, y), sg, n


def dnconv_fwd(proj, conv_w):
    B, S, _ = proj.shape
    rows = min(DN_ROWS, S)

    def body(x_ref, w_ref, o_ref, xs):
        isqk = pl.program_id(0) < 2 * DNH
        _stage_rows(xs, x_ref[0].astype(f32))
        w = w_ref[...]
        for r in range(0, S, rows):
            c, _ = _conv_rows(xs, w, DNK, r, rows)
            o_ref[0, pl.ds(r, rows), :] = _dn_act(c, isqk)[0]

    return pl.pallas_call(
        body, name="dnconv_fwd", grid=(3 * DNH, B),
        in_specs=[pl.BlockSpec((1, S, LANE), lambda i, b: (b, 0, CB_DQKV + i)), pl.BlockSpec((DNK, LANE), lambda i, b: (0, i))],
        out_specs=pl.BlockSpec((1, S, LANE), lambda i, b: (b, 0, _dn_outblk(i))),
        out_shape=jax.ShapeDtypeStruct((B, S, 3 * DNH * DND), f32), scratch_shapes=[pltpu.VMEM((S + 8, LANE), f32)],
        compiler_params=_cp(("parallel", "parallel")),
    )(proj, conv_w)


def dnconv_bwd(proj, conv_w, dqkvn, dproj):
    B, S, _ = proj.shape
    rows = min(DN_ROWS, S)

    def body(x_ref, w_ref, dy_ref, _, dx_ref, dw_ref, xs, ds):
        isqk = pl.program_id(0) < 2 * DNH
        _stage_rows(xs, x_ref[0].astype(f32))
        w = w_ref[...]
        dw = [jnp.zeros((8, LANE), f32) for _ in range(DNK)]
        for r in range(0, S, rows):
            c, wins = _conv_rows(xs, w, DNK, r, rows)
            out, sg, n = _dn_act(c, isqk)
            dout = dy_ref[0, pl.ds(r, rows), :]
            dy = jnp.where(isqk, n * (dout - out * jnp.sum(dout * out, axis=-1, keepdims=True)), dout)
            dc = dy * (sg * (1.0 + c * (1.0 - sg)))
            ds[pl.ds(r, rows), :] = dc
            for j in range(DNK):
                dw[j] = dw[j] + _fold8(dc * wins[j])
        ds[S:S + 8] = jnp.zeros((8, LANE), f32)
        for r in range(0, S, rows):
            dx_ref[0, pl.ds(r, rows), :] = _conv_rows_t(ds, w, DNK, r, rows).astype(dx_ref.dtype)

        @pl.when(pl.program_id(1) == 0)
        def _():
            dw_ref[...] = jnp.zeros(dw_ref.shape, f32)
        dw_ref[...] += jnp.concatenate([jnp.sum(d, axis=0, keepdims=True) for d in dw], axis=0)

    return pl.pallas_call(
        body, name="dnconv_bwd", grid=(3 * DNH, B),
        in_specs=[pl.BlockSpec((1, S, LANE), lambda i, b: (b, 0, CB_DQKV + i)), pl.BlockSpec((DNK, LANE), lambda i, b: (0, i)),
                  pl.BlockSpec((1, S, LANE), lambda i, b: (b, 0, _dn_outblk(i))), pl.BlockSpec(memory_space=pl.ANY)],
        out_specs=[pl.BlockSpec((1, S, LANE), lambda i, b: (b, 0, CB_DQKV + i)), pl.BlockSpec((DNK, LANE), lambda i, b: (0, i))],
        out_shape=[jax.ShapeDtypeStruct(dproj.shape, dproj.dtype), jax.ShapeDtypeStruct((DNK, 3 * DNH * DND), f32)],
        scratch_shapes=[pltpu.VMEM((S + 8, LANE), f32), pltpu.VMEM((S + 8, LANE), f32)],
        input_output_aliases={3: 0}, compiler_params=_cp(("parallel", "arbitrary")),
    )(proj, conv_w, dqkvn, dproj)


def _bdot(a, b, ca, cb, precision=HI):
    return lax.dot_general(a, b, (((ca,), (cb,)), ((0,), (0,))), preferred_element_type=f32, precision=precision)


def _bdot_bf16(a, b, ca, cb):
    return _bdot(a.astype(bf16), b.astype(bf16), ca, cb, None)


@functools.partial(jax.custom_vjp, nondiff_argnums=(2, 3))
def _bdot_bf16_vjp(a, b, ca, cb):
    return _bdot_bf16(a, b, ca, cb)


def _bdot_bf16_fwd(a, b, ca, cb):
    return _bdot_bf16(a, b, ca, cb), (a, b)


def _bdot_bf16_bwd(ca, cb, res, g):
    a, b = res
    fa, fb = 3 - ca, 3 - cb
    da = _bdot_bf16(g, b, 2, fb) if ca == 2 else _bdot_bf16(b, g, fb, 2)
    db = _bdot_bf16(a, g, fa, 1) if cb == 1 else _bdot_bf16(g, a, 1, fa)
    return da, db


_bdot_bf16_vjp.defvjp(_bdot_bf16_fwd, _bdot_bf16_bwd)


def _neumann_inverse(low):
    n = low.shape[-1]
    eye = (lax.broadcasted_iota(jnp.int32, (n, n), 0) == lax.broadcasted_iota(jnp.int32, (n, n), 1)).astype(f32)
    p = -low
    x = eye[None] + p
    for _ in range(5):
        p = _bdot(p, p, 2, 1, MID)
        x = x + _bdot(x, p, 2, 1, MID)
    return x


@jax.custom_vjp
def _unit_lower_inverse(low):
    return _neumann_inverse(low)


def _uli_fwd(low):
    t = _neumann_inverse(low)
    return t, t


def _uli_bwd(t, dt):
    return (-_bdot(_bdot(t, dt, 1, 1, MID), t, 2, 2, MID),)


_unit_lower_inverse.defvjp(_uli_fwd, _uli_bwd)


def _stack(xs):
    return jnp.concatenate([x[None] for x in xs], axis=0)


DELTA_CHUNKS = 2


def _delta_chunks(qkv, bg, state, differentiated):
    inverse = _unit_lower_inverse if differentiated else _neumann_inverse
    lo = _bdot_bf16_vjp if differentiated else _bdot_bf16
    B, n = qkv.shape[0], qkv.shape[1] // CH
    G = B * DNH
    N = n * G
    triples = [(i, b, h) for i in range(n) for b in range(B) for h in range(DNH)]
    col = lambda i, b, h, kind: qkv[b, i * CH:(i + 1) * CH, (3 * h + kind) * DND:(3 * h + kind + 1) * DND]
    q, k, v = [_stack([col(i, b, h, kind) for i, b, h in triples]) for kind in range(3)]
    lane = lax.broadcasted_iota(jnp.int32, (CH, LANE), 1)
    pick = lambda i, b, l: jnp.sum(jnp.where(lane == l, bg[b, i * CH:(i + 1) * CH], 0.0), axis=1, keepdims=True)
    beta = _stack([pick(i, b, h) for i, b, h in triples])
    g = _stack([pick(i, b, h + DNH) for i, b, h in triples])
    ri = lax.broadcasted_iota(jnp.int32, (CH, CH), 0)
    ci = lax.broadcasted_iota(jnp.int32, (CH, CH), 1)
    incl, strict = (ri >= ci)[None], (ri > ci)[None]
    gc = _bdot(jnp.broadcast_to(incl.astype(f32), (N, CH, CH)), jnp.broadcast_to(g, (N, CH, LANE)), 2, 1, MID)
    e0 = jnp.broadcast_to((lane == 0).astype(f32)[None], (N, CH, LANE))
    gc_row = _bdot(e0, gc, 2, 2, MID)
    diff = gc[:, :, :CH] - gc_row
    decay = jnp.where(incl, jnp.exp(jnp.where(incl, diff, 0.0)), 0.0)
    qs = q * (DND ** -0.5)
    kb, vb = k * beta, v * beta
    eg = jnp.exp(gc)
    with_k = lo(jnp.concatenate([kb, qs], axis=1), k, 2, 2)
    low = jnp.where(strict, with_k[:, :CH] * decay, 0.0)
    intra = jnp.where(incl, with_k[:, CH:] * decay, 0.0)
    tinv = inverse(low)
    solved = _bdot(tinv, jnp.concatenate([vb, kb * eg], axis=2), 2, 1, MID)
    gl = gc[:, CH - 1:CH, :]
    k_tail = k * jnp.exp(gl - gc)
    to_state = jnp.concatenate([solved[:, :, DND:], qs * eg], axis=1)
    decay_all = jnp.exp(gl)
    outs = []
    for i in range(n):
        sl = slice(i * G, (i + 1) * G)
        with_state = lo(to_state[sl], state, 2, 1)
        v_new = solved[sl, :, :DND] - with_state[:, :CH]
        outs.append(with_state[:, CH:] + lo(intra[sl], v_new, 2, 1))
        state = state * decay_all[sl] + lo(k_tail[sl], v_new, 1, 1)
    return outs, state


def delta_fwd(qkvn, bg, comm):
    B, S, _ = qkvn.shape
    n = DELTA_CHUNKS if (S // CH) % DELTA_CHUNKS == 0 else 1
    steps, G, rows = S // (n * CH), B * DNH, n * CH

    def body(qkv_ref, bg_ref, o_ref, st_ref, state):
        @pl.when(pl.program_id(0) == 0)
        def _():
            state[...] = jnp.zeros(state.shape, f32)
        s0 = state[...]
        st_ref[0] = s0
        outs, s1 = _delta_chunks(qkv_ref[...], bg_ref[...], s0, False)
        for i, o in enumerate(outs):
            for b in range(B):
                for h in range(DNH):
                    o_ref[b, i * CH:(i + 1) * CH, h * DND:(h + 1) * DND] = o[b * DNH + h]
        state[...] = s1

    at = lambda c: lambda: pl.program_id(0) == c
    return pl.pallas_call(
        _ride(body, 2, 2, 1, comm, at(0), at((7 * steps) // 8), at(steps - 1)), name="delta_fwd", grid=(steps,),
        in_specs=[pl.BlockSpec((B, rows, 3 * DNH * DND), lambda c: (0, c, 0)), pl.BlockSpec((B, rows, LANE), lambda c: (0, c, 0))] + comm.specs,
        out_specs=[pl.BlockSpec((B, rows, DNH * DND), lambda c: (0, c, 0)), pl.BlockSpec((1, G, DND, DND), lambda c: (c, 0, 0, 0))] + comm.specs,
        out_shape=[jax.ShapeDtypeStruct((B, S, DNH * DND), f32), jax.ShapeDtypeStruct((steps, G, DND, DND), f32)] + comm.out_shape,
        scratch_shapes=[pltpu.VMEM((G, DND, DND), f32)] + comm.scratch, compiler_params=_cp(("arbitrary",)),
    )(qkvn, bg, *comm.arrs)


def delta_bwd(qkvn, bg, states, do, comm):
    B, S, _ = qkvn.shape
    steps, G = states.shape[0], B * DNH
    rows = S // steps
    n = rows // CH

    def body(qkv_ref, bg_ref, st_ref, do_ref, dqkv_ref, dbg_ref, dstate):
        @pl.when(pl.program_id(0) == 0)
        def _():
            dstate[...] = jnp.zeros(dstate.shape, f32)
        _, vjp = jax.vjp(lambda a, g, s: _delta_chunks(a, g, s, True), qkv_ref[...], bg_ref[...], st_ref[0])
        do = [_stack([do_ref[b, i * CH:(i + 1) * CH, h * DND:(h + 1) * DND] for b in range(B) for h in range(DNH)]) for i in range(n)]
        dqkv, dbg, ds = vjp((do, dstate[...]))
        dqkv_ref[...] = dqkv
        dbg_ref[...] = dbg
        dstate[...] = ds

    rev = lambda c: steps - 1 - c
    at = lambda c: lambda: pl.program_id(0) == c
    return pl.pallas_call(
        _ride(body, 4, 2, 1, comm, at(0), at(steps - 1), at(steps - 1)), name="delta_bwd", grid=(steps,),
        in_specs=[pl.BlockSpec((B, rows, 3 * DNH * DND), lambda c: (0, rev(c), 0)), pl.BlockSpec((B, rows, LANE), lambda c: (0, rev(c), 0)),
                  pl.BlockSpec((1, G, DND, DND), lambda c: (rev(c), 0, 0, 0)),
                  pl.BlockSpec((B, rows, DNH * DND), lambda c: (0, rev(c), 0))] + comm.specs,
        out_specs=[pl.BlockSpec((B, rows, 3 * DNH * DND), lambda c: (0, rev(c), 0)),
                   pl.BlockSpec((B, rows, LANE), lambda c: (0, rev(c), 0))] + comm.specs,
        out_shape=[jax.ShapeDtypeStruct((B, S, 3 * DNH * DND), f32), jax.ShapeDtypeStruct((B, S, LANE), f32)] + comm.out_shape,
        scratch_shapes=[pltpu.VMEM((G, DND, DND), f32)] + comm.scratch, compiler_params=_cp(("arbitrary",)),
    )(qkvn, bg, states, do, *comm.arrs)


GELU_C0, GELU_C1 = math.sqrt(2.0 / math.pi), 0.044715


def _ffn_specs(S):
    nblk = DFF // LANE
    return [pl.BlockSpec((1, S, LANE), lambda i, b: (b, 0, i)), pl.BlockSpec((1, S, LANE), lambda i, b: (b, 0, nblk + i)),
            pl.BlockSpec((FK, LANE), lambda i, b: (0, i)), pl.BlockSpec((FK, LANE), lambda i, b: (0, nblk + i))]


def ffnconv_fwd(up, conv_w):
    B, S, _ = up.shape
    rows = min(FFN_ROWS, S)

    def body(g_ref, v_ref, gw_ref, vw_ref, o_ref, xg, xv):
        _stage_rows(xg, g_ref[0].astype(f32))
        _stage_rows(xv, v_ref[0].astype(f32))
        gw, vw = gw_ref[...], vw_ref[...]
        for r in range(0, S, rows):
            g, _ = _conv_rows(xg, gw, FK, r, rows)
            v, _ = _conv_rows(xv, vw, FK, r, rows)
            t = jnp.tanh(GELU_C0 * (g * (1.0 + GELU_C1 * (g * g))))
            o_ref[0, pl.ds(r, rows), :] = (0.5 * g * (1.0 + t) * v).astype(o_ref.dtype)

    return pl.pallas_call(
        body, name="ffnconv_fwd", grid=(DFF // LANE, B), in_specs=_ffn_specs(S),
        out_specs=pl.BlockSpec((1, S, LANE), lambda i, b: (b, 0, i)), out_shape=jax.ShapeDtypeStruct((B, S, DFF), bf16),
        scratch_shapes=[pltpu.VMEM((S + 8, LANE), f32)] * 2, compiler_params=_cp(("parallel", "parallel")),
    )(up, up, conv_w, conv_w)


def ffnconv_bwd(up, conv_w, dact, comm):
    B, S, _ = up.shape
    rows = min(FFN_ROWS, S)

    def body(g_ref, v_ref, gw_ref, vw_ref, dy_ref, dx_ref, dw_ref, xg, xv, dg, dv):
        _stage_rows(xg, g_ref[0].astype(f32))
        _stage_rows(xv, v_ref[0].astype(f32))
        gw, vw = gw_ref[...], vw_ref[...]
        dgw = [jnp.zeros((8, LANE), f32) for _ in range(FK)]
        dvw = [jnp.zeros((8, LANE), f32) for _ in range(FK)]
        for r in range(0, S, rows):
            g, gwins = _conv_rows(xg, gw, FK, r, rows)
            v, vwins = _conv_rows(xv, vw, FK, r, rows)
            g2 = g * g
            t = jnp.tanh(GELU_C0 * (g * (1.0 + GELU_C1 * g2)))
            half = 0.5 * (1.0 + t)
            dgelu = half + (0.5 * GELU_C0) * g * (1.0 - t * t) * (1.0 + (3.0 * GELU_C1) * g2)
            dy = dy_ref[0, pl.ds(r, rows), :].astype(f32)
            dvc = dy * (g * half)
            dgc = dy * v * dgelu
            dg[pl.ds(r, rows), :] = dgc
            dv[pl.ds(r, rows), :] = dvc
            for j in range(FK):
                dgw[j] = dgw[j] + _fold8(dgc * gwins[j])
                dvw[j] = dvw[j] + _fold8(dvc * vwins[j])
        dg[S:S + 8] = jnp.zeros((8, LANE), f32)
        dv[S:S + 8] = jnp.zeros((8, LANE), f32)
        for r in range(0, S, rows):
            dx_ref[0, 0, pl.ds(r, rows), :] = _conv_rows_t(dg, gw, FK, r, rows).astype(dx_ref.dtype)
            dx_ref[1, 0, pl.ds(r, rows), :] = _conv_rows_t(dv, vw, FK, r, rows).astype(dx_ref.dtype)

        @pl.when(pl.program_id(1) == 0)
        def _():
            dw_ref[...] = jnp.zeros(dw_ref.shape, f32)
        dw_ref[0] += jnp.concatenate([jnp.sum(d, axis=0, keepdims=True) for d in dgw], axis=0)
        dw_ref[1] += jnp.concatenate([jnp.sum(d, axis=0, keepdims=True) for d in dvw], axis=0)

    nblk = DFF // LANE
    at = lambda i, b: lambda: (pl.program_id(0) == i) & (pl.program_id(1) == b)
    return pl.pallas_call(
        _ride(body, 5, 2, 4, comm, at(0, 0), at(nblk - 1, B - 1), at(nblk - 1, B - 1)), name="ffnconv_bwd", grid=(nblk, B),
        in_specs=_ffn_specs(S) + [pl.BlockSpec((1, S, LANE), lambda i, b: (b, 0, i))] + comm.specs,
        out_specs=[pl.BlockSpec((2, 1, S, LANE), lambda i, b: (0, b, 0, i)),
                   pl.BlockSpec((2, FK, LANE), lambda i, b: (0, 0, i))] + comm.specs,
        out_shape=[jax.ShapeDtypeStruct((2, B, S, DFF), bf16), jax.ShapeDtypeStruct((2, FK, DFF), f32)] + comm.out_shape,
        scratch_shapes=[pltpu.VMEM((S + 8, LANE), f32)] * 4 + comm.scratch, compiler_params=_cp(("arbitrary", "arbitrary")),
    )(up, up, conv_w, conv_w, dact, *comm.arrs)


def ada_fwd(c_all, ada_w, ada_b):
    def body(c_ref, w_ref, b_ref, o_ref):
        c = c_ref[...]
        act = (c * jax.nn.sigmoid(c)).astype(bf16)
        o_ref[...] = jnp.dot(act, w_ref[...].astype(bf16), preferred_element_type=f32) + b_ref[...]

    return pl.pallas_call(body, name="ada_fwd", out_shape=jax.ShapeDtypeStruct((c_all.shape[0], ada_w.shape[1]), f32),
                          compiler_params=pltpu.CompilerParams(vmem_limit_bytes=VMEM_LIMIT))(c_all, ada_w, ada_b)


def ada_bwd(c_all, dmod):
    def body(c_ref, d_ref, o_ref):
        c = c_ref[...]
        act = (c * jax.nn.sigmoid(c)).astype(bf16)
        o_ref[...] = lax.dot_general(act, d_ref[...].astype(bf16), (((0,), (0,)), ((), ())), preferred_element_type=f32)

    return pl.pallas_call(body, name="ada_bwd", out_shape=jax.ShapeDtypeStruct((c_all.shape[1], dmod.shape[1]), f32),
                          compiler_params=pltpu.CompilerParams(vmem_limit_bytes=VMEM_LIMIT))(c_all, dmod)


def loss_head(h1, y2, target, g2, w):
    def fn(t, b, c):
        h, y, tg = [v.astype(f32) for v in t]

        def loss_fn(h, y, g, w):
            e = h + g * _rms(y, w) - tg
            return 0.5 * jnp.sum(jnp.mean(e * e, axis=-1))

        loss, grads = jax.value_and_grad(loss_fn, argnums=(0, 1, 2, 3))(h, y, b[0], c[0])
        return [grads[0], grads[1]], [grads[2], grads[3], jnp.full((1, LANE), loss, f32)]

    return rowcall("loss_head", fn, [(h1, D, 0), (y2, D, 0), (target, D, 0)], [g2], [w], [(D, f32), (D, bf16)],
                   [(1, D), (1, D), (1, LANE)])


def adamw(w, gparts, m, v, name):
    R, C = w.shape
    P = gparts.shape[0]
    budget = 2 * 1024 * 1024
    tr, tc = R, C
    if R * C * 4 > budget and R % 8 == 0:
        tr = max(t for t in range(8, R + 1, 8) if R % t == 0 and t * C * 4 <= budget)
    elif R * C * 4 > budget:
        tc = max(t for t in range(LANE, C + 1, LANE) if C % t == 0 and R * t * 4 <= budget)

    def body(w_ref, g_ref, m_ref, v_ref, go, do, mo, vo):
        g = g_ref[0].astype(f32)
        for p in range(1, P):
            g = g + g_ref[p].astype(f32)
        m2 = B1 * m_ref[...] + (1.0 - B1) * g
        v2 = B2 * v_ref[...] + (1.0 - B2) * jnp.square(g)
        m_hat = m2 * (1.0 / (1.0 - B1 ** STEP))
        v_hat = v2 * (1.0 / (1.0 - B2 ** STEP))
        go[...] = g
        do[...] = -LR * (m_hat / (jnp.sqrt(v_hat) + EPS) + WD * w_ref[...])
        mo[...] = m2
        vo[...] = v2

    blk = pl.BlockSpec((tr, tc), lambda i, j: (i, j))
    return pl.pallas_call(
        body, name=name, grid=(R // tr, C // tc), in_specs=[blk, pl.BlockSpec((P, tr, tc), lambda i, j: (0, i, j)), blk, blk],
        out_specs=[blk] * 4, out_shape=[jax.ShapeDtypeStruct((R, C), f32)] * 4, compiler_params=_cp(("parallel", "parallel")),
    )(w, gparts, m, v)


def _pack_w_in(wt):
    aq, ak, av, dqkv, dz, dbeta, da, ga, gd = jnp.split(wt, np.cumsum(IN_SPLITS)[:-1].tolist(), axis=0)
    ba = jnp.pad(jnp.concatenate([dbeta, da], axis=0), ((0, LANE - 2 * DNH), (0, 0)))
    return jnp.concatenate([ga, gd, aq, dqkv, dz, ak, av, ba], axis=0)


def _unpack_w_in(p):
    row = lambda cb, n: p[cb * LANE: cb * LANE + n]
    ba = row(CB_BA, 2 * DNH)
    return jnp.concatenate([row(CB_AQ, HQ * HD), row(CB_AK, HKV * HD), row(CB_AV, HKV * HD), row(CB_DQKV, 3 * DNH * DND),
                            row(CB_DZ, DNH * DND), ba[:DNH], ba[DNH:], row(CB_GA, D), row(CB_GD, D)], axis=0)


def _cols_gathered(g):
    return g.transpose(1, 0, 2).reshape(g.shape[1], NDEV * g.shape[2])


def _cols_split(w):
    r = w.shape[0]
    return w.reshape(r, NDEV, w.shape[1] // NDEV).transpose(1, 0, 2)


def kernel(x, c, ada_w, ada_b, norm_mix_pre, norm_mix_post, norm_ffn_pre, norm_ffn_post, w_in, dn_conv_w, dn_a_log, dn_dt_bias, dn_norm_w, attn_sinks, rel_bias, w_attn_branch, w_dn_branch, w_out, ffn_w_up, ffn_conv_w, ffn_w_down, loss_target, m_ada_w, m_ada_b, m_norm_mix_pre, m_norm_mix_post, m_norm_ffn_pre, m_norm_ffn_post, m_w_in, m_dn_conv_w, m_dn_a_log, m_dn_dt_bias, m_dn_norm_w, m_attn_sinks, m_rel_bias, m_w_attn_branch, m_w_dn_branch, m_w_out, m_ffn_w_up, m_ffn_conv_w, m_ffn_w_down, v_ada_w, v_ada_b, v_norm_mix_pre, v_norm_mix_post, v_norm_ffn_pre, v_norm_ffn_post, v_w_in, v_dn_conv_w, v_dn_a_log, v_dn_dt_bias, v_dn_norm_w, v_attn_sinks, v_rel_bias, v_w_attn_branch, v_w_dn_branch, v_w_out, v_ffn_w_up, v_ffn_conv_w, v_ffn_w_down):
    B, S, _ = x.shape
    T = B * S
    me = 4 * lax.axis_index("x") + 2 * lax.axis_index("y") + lax.axis_index("c")
    big = dict(w_in=w_in, dn_conv_w=dn_conv_w, w_attn_branch=w_attn_branch, w_dn_branch=w_dn_branch, w_out=w_out,
               ffn_w_up=ffn_w_up, ffn_conv_w=ffn_conv_w, ffn_w_down=ffn_w_down)
    big_names = list(big)

    first, mid, late = ["w_in", "dn_conv_w"], ["w_attn_branch", "w_dn_branch", "w_out"], ["ffn_w_up", "ffn_conv_w"]
    transposed = ("w_in", "ffn_w_up")
    local = lambda n, a: a[0].T if n in transposed else a[0]
    shard = lambda names: [local(n, big[n]).astype(bf16) for n in names]
    *got, c_all = _exchange(shard(first) + [c], "gather_w_in", two_level=True)
    gw = dict(zip(first, got))
    c_all = c_all.reshape(NDEV * B, D)

    wp = _pack_w_in(gw["w_in"].reshape(IN_DIM, D))
    conv_dn = _cols_gathered(gw["dn_conv_w"]).astype(f32)

    ncol = ada_w.shape[2]
    ada_b_mine = lax.dynamic_slice_in_dim(ada_b, me * ncol, ncol, axis=1)
    mod_cols = ada_fwd(c_all, ada_w[0], ada_b_mine)
    (mod_g,) = _exchange([mod_cols], "gather_mod")
    mod = lax.dynamic_slice_in_dim(mod_g, me * B, B, axis=1).transpose(1, 0, 2).reshape(B, NMOD * D)
    sh1, sc1, g1, sh2, sc2, g2 = [mod[:, i * D:(i + 1) * D].reshape(B, 1, D) for i in range(NMOD)]

    onehot = (jnp.asarray(_bucket_table()).reshape(1, -1) == jnp.arange(NBUCK, dtype=jnp.int32)[:, None]).astype(f32)
    bias = mm(rel_bias.T, onehot, "nn", f32, "bias_table", tn=8192, precision=HI).reshape(HQ, WIN, 2 * WIN)
    sinks = attn_sinks.reshape(HQ, 1, 1)
    a_log_pad = jnp.pad(dn_a_log, ((0, 0), (DNH, LANE - 2 * DNH)))
    dt_bias_pad = jnp.pad(dn_dt_bias, ((0, 0), (DNH, LANE - 2 * DNH)))

    (u1,) = rowcall_fwd("mix_pre", f_rms_mod, [(x, D, 0)], [sc1, sh1], [norm_mix_pre], [(D, bf16)])
    proj, gw["ffn_w_down"] = mm(u1.reshape(T, D), wp, "nt", bf16, "proj", tm=512, tn=CB_BA * LANE, b_cols=(0, 1),
                                comm=_Comm(shard(["ffn_w_down"]), two_level=True))
    proj = proj.reshape(B, S, CB_BA * LANE)
    ba = mm(u1.reshape(T, D), wp, "nt", f32, "proj_ba", tn=LANE, b_cols=(CB_BA, 1)).reshape(B, S, LANE)
    ya, *got = attn_fwd(proj, bias, sinks, _Comm(shard(mid), two_level=True))
    gw.update(zip(mid, got))
    wa = _cols_gathered(gw["w_attn_branch"])
    wd = _cols_gathered(gw["w_dn_branch"])
    wo = gw["w_out"].reshape(D, D)
    qkvn = dnconv_fwd(proj, conv_dn)
    (bg,) = rowcall_fwd("dn_gate", f_gate, [(ba, LANE, 0)], [], [a_log_pad, dt_bias_pad], [(LANE, f32)])
    o_dn, states, *got = delta_fwd(qkvn, bg, _Comm(shard(late), two_level=True))
    gw.update(zip(late, got))
    wup = gw["ffn_w_up"].reshape(2 * DFF, D)
    conv_ffn = _cols_gathered(gw["ffn_conv_w"]).astype(f32)
    wdown = gw["ffn_w_down"].reshape(DFF, D)
    (yd,) = rowcall_fwd("dn_out", f_dnout, [(o_dn, DNH * DND, 0), (proj, DNH * DND, CB_DZ // 4)], [], [dn_norm_w], [(DNH * DND, bf16)])
    pa = mm(ya.reshape(T, HQ * HD), wa, "nn", bf16, "attn_branch").reshape(B, S, D)
    pd = mm(yd.reshape(T, DNH * DND), wd, "nn", bf16, "dn_branch").reshape(B, S, D)
    merge_tok = [(proj, D, CB_GA // 8), (proj, D, CB_GD // 8), (pa, D, 0), (pd, D, 0)]
    (merged,) = rowcall_fwd("merge", f_merge, merge_tok, [], [], [(D, bf16)])
    y1 = mm(merged.reshape(T, D), wo, "nn", bf16, "mix_out").reshape(B, S, D)
    post_pre = ([(x, D, 0), (y1, D, 0)], [g1, sc2, sh2], [norm_mix_post, norm_ffn_pre])
    h1, u2 = rowcall_fwd("mix_post_ffn_pre", f_post_pre, *post_pre, [(D, f32), (D, bf16)])
    up = mm(u2.reshape(T, D), wup, "nt", bf16, "ffn_up", tn=2816).reshape(B, S, 2 * DFF)
    act = ffnconv_fwd(up, conv_ffn)
    y2 = mm(act.reshape(T, DFF), wdown, "nn", bf16, "ffn_down", tk=2816).reshape(B, S, D)

    dh1_a, dy2, dg2, dw_ffn_post, loss_b = loss_head(h1, y2, loss_target, g2, norm_ffn_post)
    dy2f = dy2.reshape(T, D)
    dact = mm(dy2f, wdown, "nt", bf16, "ffn_down_dx", tn=2816).reshape(B, S, DFF)
    g_wdown = mm(act.reshape(T, DFF), dy2f, "tn", bf16, "ffn_down_dw", tm=2816, tn=512, tk=4096)
    parts = {}
    outbox = lambda d: _Comm([d[n].astype(bf16) for n in d], scatter=True)
    dup, g_conv_ffn, parts["ffn_w_down"] = ffnconv_bwd(up, conv_ffn, dact, outbox(dict(ffn_w_down=g_wdown.reshape(NDEV, DFF // NDEV, D))))
    dupf = dup.reshape(2, T, DFF)
    g_conv_ffn = g_conv_ffn.transpose(1, 0, 2).reshape(FK, 2 * DFF)
    du2 = mm(dupf, wup, "nn", bf16, "ffn_up_dx", tk=2816).reshape(B, S, D)
    g_wup = mm(dupf, u2.reshape(T, D), "tn", bf16, "ffn_up_dw", tm=1408, tk=2048)
    dh1, dy1, dg1, dsc2, dsh2, dw_mix_post, dw_ffn_pre = rowcall_bwd(
        "mix_post_ffn_pre_bwd", f_post_pre, *post_pre, [(dh1_a, D, 0), (du2, D, 0)], [(0, f32), (1, bf16)])
    dy1f = dy1.reshape(T, D)
    dmerged = mm(dy1f, wo, "nt", bf16, "mix_out_dx").reshape(B, S, D)
    g_wo = mm(merged.reshape(T, D), dy1f, "tn", bf16, "mix_out_dw", tk=2048)
    dproj = lax.empty((B, S, NP), bf16)
    dproj, dpa, dpd = rowcall_bwd("merge_bwd", f_merge, merge_tok, [], [], [(dmerged, D, 0)],
                                  [(0, bf16), (1, bf16), (2, bf16), (3, bf16)], join_first=2, into=(dproj, CB_GA // 16))
    dpaf, dpdf = dpa.reshape(T, D), dpd.reshape(T, D)
    dya = mm(dpaf, wa, "nt", bf16, "attn_branch_dx").reshape(B, S, HQ * HD)
    g_wa = mm(ya.reshape(T, HQ * HD), dpaf, "tn", bf16, "attn_branch_dw", tk=2048)
    dyd = mm(dpdf, wd, "nt", bf16, "dn_branch_dx").reshape(B, S, DNH * DND)
    g_wd = mm(yd.reshape(T, DNH * DND), dpdf, "tn", bf16, "dn_branch_dw", tk=2048)
    dproj, do_dn, dw_dn_norm = rowcall_bwd("dn_out_bwd", f_dnout, [(o_dn, DNH * DND, 0), (proj, DNH * DND, CB_DZ // 4)], [], [dn_norm_w],
                                           [(dyd, DNH * DND, 0)], [(1, bf16), (0, f32)], into=(dproj, CB_DZ // 4))
    send = dict(ffn_w_up=g_wup.reshape(NDEV, 2 * DFF // NDEV, D), ffn_conv_w=_cols_split(g_conv_ffn))
    dqkvn, dbg, *got = delta_bwd(qkvn, bg, states, do_dn, outbox(send))
    parts.update(zip(send, got))
    dproj, da_log_pad, ddt_bias_pad = rowcall_bwd("dn_gate_bwd", f_gate, [(ba, LANE, 0)], [], [a_log_pad, dt_bias_pad],
                                                  [(dbg, LANE, 0)], [(0, bf16)], into=(dproj, CB_BA))
    dproj, g_conv_dn = dnconv_bwd(proj, conv_dn, dqkvn, dproj)
    send = dict(w_attn_branch=_cols_split(g_wa), w_dn_branch=_cols_split(g_wd),
                w_out=g_wo.reshape(NDEV, D // NDEV, D))
    dproj, dk, dv, dbias, dsinks, *got = attn_bwd(proj, bias, sinks, dya, dproj, outbox(send))
    parts.update(zip(send, got))
    dproj = lax.dynamic_update_slice(dproj, jnp.concatenate([dk, dv], axis=2), (0, 0, CB_AK * LANE)).reshape(T, NP)
    g_wp = mm(dproj, u1.reshape(T, D), "tn", bf16, "proj_dw", tm=1664, tk=1024)
    send = dict(w_in=_unpack_w_in(g_wp).reshape(NDEV, IN_DIM // NDEV, D), dn_conv_w=_cols_split(g_conv_dn))
    du1, *got = mm(dproj, wp, "nn", bf16, "proj_dx", tm=512, tk=NP, comm=outbox(send))
    parts.update(zip(send, got))
    du1 = du1.reshape(B, S, D)
    grad_x, dsc1, dsh1, dw_mix_pre = rowcall_bwd("mix_pre_bwd", f_rms_mod, [(x, D, 0)], [sc1, sh1], [norm_mix_pre], [(du1, D, 0)],
                                                 [(0, f32)], add=(dh1, D, 0))
    g_rel = mm(dbias.reshape(HQ, WIN * 2 * WIN), onehot, "nt", f32, "rel_bias_dw", tk=8192, precision=HI)

    dmod = jnp.concatenate([dsh1, dsc1, dg1, dsh2, dsc2, dg2], axis=2).reshape(B, NMOD * D)

    zrow = lambda a: jnp.concatenate([a.reshape(1, -1), jnp.zeros((B - 1, a.size), f32)], axis=0)
    small_g = jnp.concatenate([
        dmod, dw_mix_pre.reshape(B, D), dw_mix_post.reshape(B, D), dw_ffn_pre.reshape(B, D), dw_ffn_post.reshape(B, D),
        da_log_pad.reshape(B, LANE)[:, DNH:2 * DNH], ddt_bias_pad.reshape(B, LANE)[:, DNH:2 * DNH], dw_dn_norm.reshape(B, DND),
        zrow(dsinks), zrow(g_rel.T), loss_b.reshape(B, LANE)[:, :1], jnp.zeros((B, SMALL_PAD - SMALL_N - 1), f32)], axis=1)
    (small_all,) = _exchange([small_g], "gather_small")
    dmod_cols = lax.dynamic_slice_in_dim(small_all.reshape(NDEV * B, SMALL_PAD), me * ncol, ncol, axis=1)
    g_ada_w = ada_bwd(c_all, dmod_cols)
    small_w = dict(ada_b=(ada_b, m_ada_b, v_ada_b), norm_mix_pre=(norm_mix_pre, m_norm_mix_pre, v_norm_mix_pre),
                   norm_mix_post=(norm_mix_post, m_norm_mix_post, v_norm_mix_post), norm_ffn_pre=(norm_ffn_pre, m_norm_ffn_pre, v_norm_ffn_pre),
                   norm_ffn_post=(norm_ffn_post, m_norm_ffn_post, v_norm_ffn_post), dn_a_log=(dn_a_log, m_dn_a_log, v_dn_a_log),
                   dn_dt_bias=(dn_dt_bias, m_dn_dt_bias, v_dn_dt_bias), dn_norm_w=(dn_norm_w, m_dn_norm_w, v_dn_norm_w),
                   attn_sinks=(attn_sinks, m_attn_sinks, v_attn_sinks), rel_bias=(rel_bias, m_rel_bias, v_rel_bias))

    def pack(i, fill):
        row = jnp.concatenate([small_w[n][i].reshape(1, -1) for n, _ in SMALL], axis=1)
        return jnp.pad(row, ((0, 0), (0, SMALL_PAD - SMALL_N)), constant_values=fill)

    small_out = adamw(pack(0, 0.0), small_all.reshape(NDEV * B, 1, SMALL_PAD), pack(1, 0.0), pack(2, 1.0), "adamw_small")
    loss = small_out[0][0, SMALL_N]

    res = {}
    off = 0
    for n, size in SMALL:
        shp = small_w[n][0].shape
        res[n] = [o[:, off:off + size].reshape(shp) for o in small_out]
        off += size
    res["ada_w"] = [o[None] for o in adamw(ada_w[0], g_ada_w[None], m_ada_w[0], v_ada_w[0], "adamw_ada_w")]
    moments = dict(w_in=(m_w_in, v_w_in), dn_conv_w=(m_dn_conv_w, v_dn_conv_w), w_attn_branch=(m_w_attn_branch, v_w_attn_branch),
                   w_dn_branch=(m_w_dn_branch, v_w_dn_branch), w_out=(m_w_out, v_w_out), ffn_w_up=(m_ffn_w_up, v_ffn_w_up),
                   ffn_conv_w=(m_ffn_conv_w, v_ffn_conv_w), ffn_w_down=(m_ffn_w_down, v_ffn_w_down))
    for n in big_names:
        outs = adamw(local(n, big[n]), parts[n], local(n, moments[n][0]), local(n, moments[n][1]), "adamw_" + n)
        res[n] = [(o.T if n in transposed else o)[None] for o in outs]

    order = ["ada_w", "ada_b", "norm_mix_pre", "norm_mix_post", "norm_ffn_pre", "norm_ffn_post", "w_in", "dn_conv_w", "dn_a_log",
             "dn_dt_bias", "dn_norm_w", "attn_sinks", "rel_bias", "w_attn_branch", "w_dn_branch", "w_out", "ffn_w_up", "ffn_conv_w",
             "ffn_w_down"]
    return (loss, grad_x, *[res[n][0] for n in order], *[res[n][1] for n in order], *[res[n][2] for n in order],
            *[res[n][3] for n in order])
```

```python
import functools
import math

import numpy as np
import jax
import jax.numpy as jnp
from jax import lax
from jax.experimental import pallas as pl
from jax.experimental.pallas import tpu as pltpu

f32 = jnp.float32
bf16 = jnp.bfloat16
HI = lax.Precision.HIGHEST
MID = lax.Precision.HIGH
MESH = pl.DeviceIdType.MESH

NDEV = 8
D = 1024
HQ, HKV, HD, WIN, NBUCK, MAXDIST = 8, 2, 64, 128, 32, 128
DNH, DND, DNK, CH = 4, 128, 4, 64
DFF, FK = 2816, 3
NMOD = 6
RMS_EPS = 1e-6
L2_EPS = 1e-6
NEG_INF = -1e30
LR, B1, B2, EPS, WD, STEP = 0.001, 0.9, 0.999, 1e-08, 0.01, 10

LANE = 128
CB_GA, CB_GD, CB_AQ, CB_DQKV, CB_DZ, CB_AK, CB_AV, CB_BA, NPB = 0, 8, 16, 20, 32, 36, 37, 38, 39
NP = NPB * LANE
IN_SPLITS = (HQ * HD, HKV * HD, HKV * HD, 3 * DNH * DND, DNH * DND, DNH, DNH, D, D)
IN_DIM = sum(IN_SPLITS)
VMEM_LIMIT = 56 * 1024 * 1024

SMALL = (("ada_b", NMOD * D), ("norm_mix_pre", D), ("norm_mix_post", D), ("norm_ffn_pre", D), ("norm_ffn_post", D),
         ("dn_a_log", DNH), ("dn_dt_bias", DNH), ("dn_norm_w", DND), ("attn_sinks", HQ), ("rel_bias", NBUCK * HQ))
SMALL_N = sum(n for _, n in SMALL)
SMALL_PAD = 10752


def _cp(sem):
    return pltpu.CompilerParams(dimension_semantics=sem, vmem_limit_bytes=VMEM_LIMIT)


def _pick(dim, target):
    if dim <= target:
        return dim
    best = None
    for d in range(LANE, target + 1, LANE):
        if dim % d == 0:
            best = d
    assert best is not None, (dim, target)
    return best


def _me():
    x, y, c = lax.axis_index("x"), lax.axis_index("y"), lax.axis_index("c")
    return x, y, c, 4 * x + 2 * y + c


def _peer(x, y, c, k):
    px = 1 - x if k & 4 else x
    py = 1 - y if k & 2 else y
    pc = 1 - c if k & 1 else c
    return (px, py, pc), 4 * px + 2 * py + pc


class _Comm:
    def __init__(self, arrs, scatter=False, two_level=False):
        assert not (scatter and two_level)
        self.arrs, self.n, self.scatter, self.two_level = list(arrs), len(arrs), scatter, two_level
        if scatter:
            self.out_shape = [jax.ShapeDtypeStruct(a.shape, a.dtype) for a in arrs]
        else:
            self.out_shape = [jax.ShapeDtypeStruct((NDEV,) + a.shape, a.dtype) for a in arrs]
        nsem = self.n * (NDEV - 1)
        self.scratch = [pltpu.SemaphoreType.DMA((nsem,)), pltpu.SemaphoreType.DMA((nsem,)), pltpu.SemaphoreType.DMA((self.n,))]
        self.specs = [pl.BlockSpec(memory_space=pl.ANY)] * self.n

    def phases(self, ins, out, send, recv, loc):
        x, y, c, me = _me()

        def remote(a, k, src, dst, to):
            s = a * (NDEV - 1) + k - 1
            return pltpu.make_async_remote_copy(src_ref=src, dst_ref=dst, send_sem=send.at[s], recv_sem=recv.at[s],
                                                device_id=to, device_id_type=MESH)

        def local(a):
            return pltpu.make_async_copy(ins[a].at[me] if self.scatter else ins[a], out[a].at[me], loc.at[a])

        if not self.two_level:
            def mine(a, k):
                peer, pid = _peer(x, y, c, k)
                return remote(a, k, ins[a].at[pid] if self.scatter else ins[a], out[a].at[me], peer)

            def theirs(a, k):
                peer, pid = _peer(x, y, c, k)
                return remote(a, k, ins[a].at[pid] if self.scatter else ins[a], out[a].at[pid], peer)

            def start():
                for a in range(self.n):
                    local(a).start()
                    for k in range(1, NDEV):
                        mine(a, k).start()

            def forward():
                pass

            def finish():
                for a in range(self.n):
                    for k in range(1, NDEV):
                        mine(a, k).wait_send()
                    for k in range(1, NDEV):
                        theirs(a, k).wait_recv()
                    local(a).wait()

            return start, forward, finish

        sibling = (x, y, 1 - c)
        chips = [(1 - x, y), (x, 1 - y), (1 - x, 1 - y)]
        slot = lambda px, py, pc: 4 * px + 2 * py + pc

        def own(a, k, to):
            return remote(a, k, ins[a], out[a].at[me], to)

        def landed(a, k, frm):
            return remote(a, k, ins[a], out[a].at[slot(*frm)], frm)

        def passed(a, j):
            rows = out[a].at[slot(*chips[j], c)]
            return remote(a, 5 + j, rows, rows, sibling)

        def start():
            for a in range(self.n):
                local(a).start()
                own(a, 1, sibling).start()
                for j, chip in enumerate(chips):
                    own(a, 2 + j, (*chip, c)).start()

        def forward():
            for a in range(self.n):
                for j, chip in enumerate(chips):
                    landed(a, 2 + j, (*chip, c)).wait_recv()
                    passed(a, j).start()

        def finish():
            for a in range(self.n):
                landed(a, 1, sibling).wait_recv()
                for j, chip in enumerate(chips):
                    remote(a, 5 + j, ins[a], out[a].at[slot(*chip, 1 - c)], sibling).wait_recv()
                own(a, 1, sibling).wait_send()
                for j, chip in enumerate(chips):
                    own(a, 2 + j, (*chip, c)).wait_send()
                    passed(a, j).wait_send()
                local(a).wait()

        return start, forward, finish


def _ride(body, n_in, n_out, n_scr, comm, first, mid, last):
    k = comm.n

    def wrapped(*refs):
        ins, cins = refs[:n_in], refs[n_in:n_in + k]
        o0 = n_in + k
        outs, couts = refs[o0:o0 + n_out], refs[o0 + n_out:o0 + n_out + k]
        s0 = o0 + n_out + k
        scr, sems = refs[s0:s0 + n_scr], refs[s0 + n_scr:]
        start, forward, finish = comm.phases(cins, couts, *sems)
        pl.when(first())(start)
        body(*ins, *outs, *scr)
        pl.when(mid())(forward)
        pl.when(last())(finish)

    return wrapped


def _exchange(arrs, name, scatter=False, two_level=False):
    comm = _Comm(arrs, scatter, two_level)

    def body(*refs):
        start, forward, finish = comm.phases(refs[:comm.n], refs[comm.n:2 * comm.n], *refs[2 * comm.n:])
        start()
        forward()
        finish()

    return pl.pallas_call(body, name=name, out_shape=comm.out_shape, in_specs=comm.specs, out_specs=comm.specs,
                          scratch_shapes=comm.scratch, compiler_params=pltpu.CompilerParams(has_side_effects=True))(*arrs)


def mm(a, b, mode, out_dtype, name, tm=1024, tn=1024, tk=1024, precision=None, comm=None, b_cols=None):
    a_parts = a.shape[0] if a.ndim == 3 else 1
    b_parts = b.shape[0] if b.ndim == 3 else 1
    assert b_parts == 1 or mode == "tn"
    ash, bsh = (a.shape[-2], a.shape[-1] * a_parts), b.shape[-2:]
    if mode == "nn":
        (M, K), (K2, N) = ash, bsh
    elif mode == "nt":
        (M, K), (N, K2) = ash, bsh
    else:
        (K, M), (K2, N) = ash, (bsh[0], bsh[1] * b_parts)
    assert K == K2, (name, a.shape, b.shape)
    col0 = 0
    if b_cols is not None:
        assert mode in ("nn", "nt") and tn % LANE == 0
        col0, N = b_cols[0], b_cols[1] * tn
    if mode == "tn":
        tm, tn, tk = _pick(M // a_parts, tm), _pick(N // b_parts, tn), _pick(K, tk)
    else:
        tm, tn, tk = _pick(M, tm), _pick(N // b_parts, tn), _pick(K // a_parts, tk)
    nk = K // tk
    if mode == "tn" and a_parts > 1:
        per = M // tm // a_parts
        a_spec = pl.BlockSpec((None, tk, tm), lambda i, j, k: (i // per, k, i % per))
    elif mode == "tn":
        a_spec = pl.BlockSpec((tk, tm), lambda i, j, k: (k, i))
    elif a_parts > 1:
        per = nk // a_parts
        a_spec = pl.BlockSpec((None, tm, tk), lambda i, j, k: (k // per, i, k % per))
    else:
        a_spec = pl.BlockSpec((tm, tk), lambda i, j, k: (i, k))
    if mode == "nt":
        b_spec = pl.BlockSpec((tn, tk), lambda i, j, k: (col0 + j, k))
    elif b_parts > 1:
        per = N // tn // b_parts
        b_spec = pl.BlockSpec((None, tk, tn), lambda i, j, k: (j // per, k, j % per))
    else:
        b_spec = pl.BlockSpec((tk, tn), lambda i, j, k: (k, col0 + j))
    dims = {"nn": ((1,), (0,)), "nt": ((1,), (1,)), "tn": ((0,), (0,))}[mode]

    def body(a_ref, b_ref, o_ref, *scr):
        p = lax.dot_general(a_ref[...], b_ref[...], (dims, ((), ())), preferred_element_type=f32, precision=precision)
        if nk == 1:
            o_ref[...] = p.astype(o_ref.dtype)
        else:
            acc = scr[0]
            k = pl.program_id(2)

            @pl.when(k == 0)
            def _():
                acc[...] = p

            @pl.when(k > 0)
            def _():
                acc[...] += p

            @pl.when(k == nk - 1)
            def _():
                o_ref[...] = acc[...].astype(o_ref.dtype)

    grid = (M // tm, N // tn, nk)
    scratch = [pltpu.VMEM((tm, tn), f32)] if nk > 1 else []
    out_spec = pl.BlockSpec((tm, tn), lambda i, j, k: (i, j))
    out_shape = jax.ShapeDtypeStruct((M, N), out_dtype)
    if comm is None:
        return pl.pallas_call(body, name=name, grid=grid, in_specs=[a_spec, b_spec], out_specs=out_spec, out_shape=out_shape,
                              scratch_shapes=scratch, compiler_params=_cp(("parallel", "parallel", "arbitrary")))(a, b)
    at = lambda pos: lambda: functools.reduce(jnp.logical_and, [pl.program_id(d) == p for d, p in enumerate(pos)])
    end = tuple(g - 1 for g in grid)
    return pl.pallas_call(
        _ride(body, 2, 1, len(scratch), comm, at((0, 0, 0)), at(end), at(end)), name=name, grid=grid,
        in_specs=[a_spec, b_spec] + comm.specs, out_specs=[out_spec] + comm.specs, out_shape=[out_shape] + comm.out_shape,
        scratch_shapes=scratch + comm.scratch, compiler_params=_cp(("arbitrary", "arbitrary", "arbitrary")),
    )(a, b, *comm.arrs)


def rowcall(name, fn, tok, bat, con, tok_out, acc_out, ts=256, into=None):
    B, S = tok[0][0].shape[:2]
    ts = min(ts, S)
    nt, nb, nc, no, na = len(tok), len(bat), len(con), len(tok_out), len(acc_out)
    nin = nt + nb + nc + (1 if into is not None else 0)

    def body(*refs):
        tr, br, cr = refs[:nt], refs[nt:nt + nb], refs[nt + nb:nt + nb + nc]
        orf, arf = refs[nin:nin + no], refs[nin + no:]
        touts, aouts = fn([r[0] for r in tr], [r[0] for r in br], [r[...] for r in cr])
        for r, v in zip(orf, touts):
            r[0] = v.astype(r.dtype)
        s = pl.program_id(1)
        for r, v in zip(arf, aouts):
            @pl.when(s == 0)
            def _(r=r):
                r[...] = jnp.zeros(r.shape, r.dtype)
            r[0] += v.astype(f32)

    in_specs = [pl.BlockSpec((1, ts, w), lambda b, s, cb=cb: (b, s, cb)) for (_, w, cb) in tok]
    in_specs += [pl.BlockSpec((1,) + a.shape[1:], lambda b, s: (b, 0, 0)) for a in bat]
    in_specs += [pl.BlockSpec(a.shape, lambda b, s, nd=a.ndim: (0,) * nd) for a in con]
    out_specs = [pl.BlockSpec((1, ts, w), lambda b, s: (b, s, 0)) for (w, _) in tok_out]
    out_specs += [pl.BlockSpec((1,) + shp, lambda b, s, nd=len(shp): (b,) + (0,) * nd) for shp in acc_out]
    out_shape = [jax.ShapeDtypeStruct((B, S, w), dt) for (w, dt) in tok_out]
    out_shape += [jax.ShapeDtypeStruct((B,) + shp, f32) for shp in acc_out]
    extra, aliases = [], {}
    if into is not None:
        buf, cb = into
        assert buf.dtype == tok_out[0][1]
        in_specs.append(pl.BlockSpec(memory_space=pl.ANY))
        out_specs[0] = pl.BlockSpec((1, ts, tok_out[0][0]), lambda b, s: (b, s, cb))
        out_shape[0] = jax.ShapeDtypeStruct(buf.shape, buf.dtype)
        extra, aliases = [buf], {nin - 1: 0}
    return pl.pallas_call(
        body, name=name, grid=(B, S // ts), in_specs=in_specs, out_specs=out_specs, out_shape=out_shape,
        input_output_aliases=aliases, compiler_params=_cp(("parallel", "arbitrary")),
    )(*[t[0] for t in tok], *bat, *con, *extra)


def rowcall_fwd(name, f, tok, bat, con, tok_out, ts=256):
    def fn(t, b, c):
        return f([v.astype(f32) for v in t], b, c), []
    return rowcall(name, fn, tok, bat, con, tok_out, [], ts)


def rowcall_bwd(name, f, tok, bat, con, cts, tok_grads, add=None, ts=256, join_first=1, into=None):
    nt, ncts = len(tok), len(cts)

    def fn(t, b, c):
        prim = [v.astype(f32) for v in t[:nt]]
        ct = [v.astype(f32) for v in t[nt:nt + ncts]]
        _, vjp = jax.vjp(lambda tt, bb, cc: f(tt, bb, cc), prim, b, c)
        dt, db, dc = vjp(ct)
        touts = [dt[i] for i, _ in tok_grads]
        if add is not None:
            touts[0] = touts[0] + t[nt + ncts].astype(f32)
        if join_first > 1:
            touts = [jnp.concatenate(touts[:join_first], axis=1)] + touts[join_first:]
        return touts, list(db) + list(dc)

    all_tok = list(tok) + list(cts) + ([add] if add is not None else [])
    tok_out = [(tok[i][1], dt) for i, dt in tok_grads]
    if join_first > 1:
        tok_out = [(sum(w for w, _ in tok_out[:join_first]), tok_out[0][1])] + tok_out[join_first:]
    acc_out = [tuple(a.shape[1:]) for a in bat] + [tuple(a.shape) for a in con]
    return rowcall(name, fn, all_tok, bat, con, tok_out, acc_out, ts, into)


def _rms(y, w):
    return y * lax.rsqrt(jnp.mean(y * y, axis=-1, keepdims=True) + RMS_EPS) * w


def f_rms_mod(t, b, c):
    return [_rms(t[0], c[0]) * (1.0 + b[0]) + b[1]]


def f_post_pre(t, b, c):
    h1 = t[0] + b[0] * _rms(t[1], c[0])
    return [h1, _rms(h1, c[1]) * (1.0 + b[1]) + b[2]]


def f_merge(t, b, c):
    ga, gd, ya, yd = t
    return [jax.nn.sigmoid(ga) * ya + jax.nn.sigmoid(gd) * yd]


def f_dnout(t, b, c):
    o, z = t
    outs = []
    for h in range(DNH):
        sl = slice(h * DND, (h + 1) * DND)
        zh = z[:, sl]
        outs.append(_rms(o[:, sl], c[0]) * (zh * jax.nn.sigmoid(zh)))
    return [jnp.concatenate(outs, axis=1)]


def _softplus(x):
    return jnp.maximum(x, 0.0) + jnp.log(1.0 + jnp.exp(-jnp.abs(x)))


def f_gate(t, b, c):
    ba = t[0]
    a_log, dt_bias = c
    lane = lax.broadcasted_iota(jnp.int32, ba.shape, 1)
    beta = jax.nn.sigmoid(ba)
    g = -jnp.exp(a_log) * _softplus(ba + dt_bias)
    return [jnp.where(lane < DNH, beta, jnp.where(lane < 2 * DNH, g, 0.0))]


def _bucket_table():
    qi = np.arange(WIN)[:, None]
    kj = np.arange(2 * WIN)[None, :]
    dist = np.maximum(WIN + qi - kj, 0)
    max_exact = NBUCK // 2
    scaled = np.log(np.maximum(dist, 1).astype(np.float64) / max_exact) / math.log(MAXDIST / max_exact)
    large = np.minimum(max_exact + (scaled * (NBUCK - max_exact)).astype(np.int32), NBUCK - 1)
    return np.where(dist < max_exact, dist, large).astype(np.int32)


def _attn_mask(n):
    qi = lax.broadcasted_iota(jnp.int32, (WIN, 2 * WIN), 0)
    kj = lax.broadcasted_iota(jnp.int32, (WIN, 2 * WIN), 1)
    dist = WIN + qi - kj
    return (dist >= 0) & (dist < WIN) & ((kj >= WIN) | (n > 0))


def _swap_halves(x):
    return pltpu.roll(x, HD, axis=x.ndim - 1)


@jax.custom_vjp
def _swap_halves_vjp(x):
    return _swap_halves(x)


_swap_halves_vjp.defvjp(lambda x: (_swap_halves(x), None), lambda _, g: (_swap_halves(g),))


def _attn_block(q, kp, kc, vp, vc, bias, sinks, mask, differentiated):
    dot = _bdot_bf16_vjp if differentiated else _bdot_bf16
    swap = _swap_halves_vjp if differentiated else _swap_halves
    B, grp = q.shape[0], HQ // HKV
    upper = lax.broadcasted_iota(jnp.int32, (2 * WIN, LANE), 1) >= HD

    def placed(natural, swapped, j, half):
        keep = upper if half == 1 else ~upper
        return jnp.where(keep, natural if j == half else swapped, 0.0)

    qh, ks, vs = [], [], []
    for b in range(B):
        kb, vb = jnp.concatenate([kp[b], kc[b]], axis=0), jnp.concatenate([vp[b], vc[b]], axis=0)
        kb_sw, vb_sw = swap(kb), swap(vb)
        for h in range(HQ):
            qh.append(q[b, :, (h // 2) * LANE:(h // 2 + 1) * LANE])
            ks.append(placed(kb, kb_sw, h // grp, h % 2))
            vs.append(placed(vb, vb_sw, h // grp, h % 2))
    s = dot(_stack(qh), _stack(ks), 2, 2).reshape(B, HQ, WIN, 2 * WIN) * (HD ** -0.5)
    s = jnp.where(mask, s + bias, NEG_INF)
    m = jnp.maximum(jnp.max(s, axis=-1, keepdims=True), sinks)
    p = jnp.exp(s - m)
    probs = p / (jnp.sum(p, axis=-1, keepdims=True) + jnp.exp(sinks - m))
    o = dot(probs.reshape(B * HQ, WIN, 2 * WIN), _stack(vs), 2, 1)
    return _stack([jnp.concatenate([o[b * HQ + 2 * i] + o[b * HQ + 2 * i + 1] for i in range(HQ // 2)], axis=1) for b in range(B)])


def _attn_specs(B, NB):
    last = NB - 1
    return [
        pl.BlockSpec((B, WIN, HQ * HD), lambda n: (0, jnp.minimum(n, last), CB_AQ // 4)),
        pl.BlockSpec((B, WIN, LANE), lambda n: (0, jnp.clip(n - 1, 0, last), CB_AK)),
        pl.BlockSpec((B, WIN, LANE), lambda n: (0, jnp.minimum(n, last), CB_AK)),
        pl.BlockSpec((B, WIN, LANE), lambda n: (0, jnp.clip(n - 1, 0, last), CB_AV)),
        pl.BlockSpec((B, WIN, LANE), lambda n: (0, jnp.minimum(n, last), CB_AV)),
        pl.BlockSpec((HQ, WIN, 2 * WIN), lambda n: (0, 0, 0)),
        pl.BlockSpec((HQ, 1, 1), lambda n: (0, 0, 0)),
    ]


def attn_fwd(proj, bias, sinks, comm):
    B, S, _ = proj.shape
    NB = S // WIN

    def body(q, kp, kc, vp, vc, bias_ref, sink_ref, o_ref):
        mask = _attn_mask(pl.program_id(0))
        o = _attn_block(*[r[...].astype(f32) for r in (q, kp, kc, vp, vc)], bias_ref[...], sink_ref[...], mask, False)
        o_ref[...] = o.astype(o_ref.dtype)

    at = lambda n: lambda: pl.program_id(0) == n
    return pl.pallas_call(
        _ride(body, 7, 1, 0, comm, at(0), at((3 * NB) // 4), at(NB - 1)), name="attn_fwd", grid=(NB,),
        in_specs=_attn_specs(B, NB) + comm.specs,
        out_specs=[pl.BlockSpec((B, WIN, HQ * HD), lambda n: (0, n, 0))] + comm.specs,
        out_shape=[jax.ShapeDtypeStruct((B, S, HQ * HD), bf16)] + comm.out_shape, scratch_shapes=comm.scratch,
        compiler_params=_cp(("arbitrary",)),
    )(proj, proj, proj, proj, proj, bias, sinks, *comm.arrs)


def attn_bwd(proj, bias, sinks, dy, dproj, comm):
    B, S, _ = proj.shape
    NB = S // WIN
    last = NB - 1

    def body(q, kp, kc, vp, vc, bias_ref, sink_ref, dy_ref, _, dq_ref, dk_ref, dv_ref, dbias_ref, dsink_ref, kcar, vcar):
        n = pl.program_id(0)

        @pl.when(n == 0)
        def _():
            dbias_ref[...] = jnp.zeros(dbias_ref.shape, f32)
            dsink_ref[...] = jnp.zeros(dsink_ref.shape, f32)
            kcar[...] = jnp.zeros(kcar.shape, f32)
            vcar[...] = jnp.zeros(vcar.shape, f32)

        @pl.when(n < NB)
        def _():
            mask = _attn_mask(n)
            _, vjp = jax.vjp(lambda *a: _attn_block(*a, mask, True), *[r[...].astype(f32) for r in (q, kp, kc, vp, vc)],
                             bias_ref[...], sink_ref[...])
            dq, dkp, dkc, dvp, dvc, dbias, dsink = vjp(dy_ref[...].astype(f32))
            dq_ref[...] = dq.astype(dq_ref.dtype)
            dbias_ref[...] += dbias
            dsink_ref[...] += dsink
            dk_ref[...] = (kcar[...] + dkp).astype(dk_ref.dtype)
            dv_ref[...] = (vcar[...] + dvp).astype(dv_ref.dtype)
            kcar[...] = dkc
            vcar[...] = dvc

        @pl.when(n == NB)
        def _():
            dk_ref[...] = kcar[...].astype(dk_ref.dtype)
            dv_ref[...] = vcar[...].astype(dv_ref.dtype)

    in_specs = _attn_specs(B, NB) + [pl.BlockSpec((B, WIN, HQ * HD), lambda n: (0, jnp.minimum(n, last), 0)),
                                     pl.BlockSpec(memory_space=pl.ANY)]
    kv_out = pl.BlockSpec((B, WIN, LANE), lambda n: (0, jnp.maximum(n - 1, 0), 0))
    at = lambda n: lambda: pl.program_id(0) == n
    return pl.pallas_call(
        _ride(body, 9, 5, 2, comm, at(0), at(NB), at(NB)), name="attn_bwd", grid=(NB + 1,),
        in_specs=in_specs + comm.specs, input_output_aliases={8: 0},
        out_specs=[pl.BlockSpec((B, WIN, HQ * HD), lambda n: (0, jnp.minimum(n, last), CB_AQ // 4)), kv_out, kv_out,
                   pl.BlockSpec((HQ, WIN, 2 * WIN), lambda n: (0, 0, 0)), pl.BlockSpec((HQ, 1, 1), lambda n: (0, 0, 0))] + comm.specs,
        out_shape=[jax.ShapeDtypeStruct(dproj.shape, dproj.dtype), jax.ShapeDtypeStruct((B, S, LANE), bf16),
                   jax.ShapeDtypeStruct((B, S, LANE), bf16), jax.ShapeDtypeStruct((HQ, WIN, 2 * WIN), f32),
                   jax.ShapeDtypeStruct((HQ, 1, 1), f32)] + comm.out_shape,
        scratch_shapes=[pltpu.VMEM((B, WIN, LANE), f32), pltpu.VMEM((B, WIN, LANE), f32)] + comm.scratch,
        compiler_params=_cp(("arbitrary",)),
    )(proj, proj, proj, proj, proj, bias, sinks, dy, dproj, *comm.arrs)


DN_ROWS, FFN_ROWS = 256, 32


def _stage_rows(dst, value):
    dst[0:8] = jnp.zeros((8, LANE), f32)
    dst[8:8 + value.shape[0]] = value


def _conv_rows(xs, w, width, r, rows):
    wins = [xs[pl.ds(r + 8 - (width - 1) + j, rows), :] for j in range(width)]
    out = w[0:1] * wins[0]
    for j in range(1, width):
        out = out + w[j:j + 1] * wins[j]
    return out, wins


def _fold8(v):
    return jnp.sum(v.reshape(v.shape[0] // 8, 8, LANE), axis=0)


def _conv_rows_t(ds, w, width, r, rows):
    out = w[0:1] * ds[pl.ds(r + width - 1, rows), :]
    for j in range(1, width):
        out = out + w[j:j + 1] * ds[pl.ds(r + width - 1 - j, rows), :]
    return out


def _dn_outblk(i):
    return (i % DNH) * 3 + i // DNH


def _dn_act(c, isqk):
    sg = jax.nn.sigmoid(c)
    y = c * sg
    n = lax.rsqrt(jnp.sum(y * y, axis=-1, keepdims=True) + L2_EPS)
    return jnp.where(isqk, y * n, y), sg, n


def dnconv_fwd(proj, conv_w):
    B, S, _ = proj.shape
    rows = min(DN_ROWS, S)

    def body(x_ref, w_ref, o_ref, xs):
        isqk = pl.program_id(0) < 2 * DNH
        _stage_rows(xs, x_ref[0].astype(f32))
        w = w_ref[...]
        for r in range(0, S, rows):
            c, _ = _conv_rows(xs, w, DNK, r, rows)
            o_ref[0, pl.ds(r, rows), :] = _dn_act(c, isqk)[0]

    return pl.pallas_call(
        body, name="dnconv_fwd", grid=(3 * DNH, B),
        in_specs=[pl.BlockSpec((1, S, LANE), lambda i, b: (b, 0, CB_DQKV + i)), pl.BlockSpec((DNK, LANE), lambda i, b: (0, i))],
        out_specs=pl.BlockSpec((1, S, LANE), lambda i, b: (b, 0, _dn_outblk(i))),
        out_shape=jax.ShapeDtypeStruct((B, S, 3 * DNH * DND), f32), scratch_shapes=[pltpu.VMEM((S + 8, LANE), f32)],
        compiler_params=_cp(("parallel", "parallel")),
    )(proj, conv_w)


def dnconv_bwd(proj, conv_w, dqkvn, dproj):
    B, S, _ = proj.shape
    rows = min(DN_ROWS, S)

    def body(x_ref, w_ref, dy_ref, _, dx_ref, dw_ref, xs, ds):
        isqk = pl.program_id(0) < 2 * DNH
        _stage_rows(xs, x_ref[0].astype(f32))
        w = w_ref[...]
        dw = [jnp.zeros((8, LANE), f32) for _ in range(DNK)]
        for r in range(0, S, rows):
            c, wins = _conv_rows(xs, w, DNK, r, rows)
            out, sg, n = _dn_act(c, isqk)
            dout = dy_ref[0, pl.ds(r, rows), :]
            dy = jnp.where(isqk, n * (dout - out * jnp.sum(dout * out, axis=-1, keepdims=True)), dout)
            dc = dy * (sg * (1.0 + c * (1.0 - sg)))
            ds[pl.ds(r, rows), :] = dc
            for j in range(DNK):
                dw[j] = dw[j] + _fold8(dc * wins[j])
        ds[S:S + 8] = jnp.zeros((8, LANE), f32)
        for r in range(0, S, rows):
            dx_ref[0, pl.ds(r, rows), :] = _conv_rows_t(ds, w, DNK, r, rows).astype(dx_ref.dtype)

        @pl.when(pl.program_id(1) == 0)
        def _():
            dw_ref[...] = jnp.zeros(dw_ref.shape, f32)
        dw_ref[...] += jnp.concatenate([jnp.sum(d, axis=0, keepdims=True) for d in dw], axis=0)

    return pl.pallas_call(
        body, name="dnconv_bwd", grid=(3 * DNH, B),
        in_specs=[pl.BlockSpec((1, S, LANE), lambda i, b: (b, 0, CB_DQKV + i)), pl.BlockSpec((DNK, LANE), lambda i, b: (0, i)),
                  pl.BlockSpec((1, S, LANE), lambda i, b: (b, 0, _dn_outblk(i))), pl.BlockSpec(memory_space=pl.ANY)],
        out_specs=[pl.BlockSpec((1, S, LANE), lambda i, b: (b, 0, CB_DQKV + i)), pl.BlockSpec((DNK, LANE), lambda i, b: (0, i))],
        out_shape=[jax.ShapeDtypeStruct(dproj.shape, dproj.dtype), jax.ShapeDtypeStruct((DNK, 3 * DNH * DND), f32)],
        scratch_shapes=[pltpu.VMEM((S + 8, LANE), f32), pltpu.VMEM((S + 8, LANE), f32)],
        input_output_aliases={3: 0}, compiler_params=_cp(("parallel", "arbitrary")),
    )(proj, conv_w, dqkvn, dproj)


def _bdot(a, b, ca, cb, precision=HI):
    return lax.dot_general(a, b, (((ca,), (cb,)), ((0,), (0,))), preferred_element_type=f32, precision=precision)


def _bdot_bf16(a, b, ca, cb):
    return _bdot(a.astype(bf16), b.astype(bf16), ca, cb, None)


@functools.partial(jax.custom_vjp, nondiff_argnums=(2, 3))
def _bdot_bf16_vjp(a, b, ca, cb):
    return _bdot_bf16(a, b, ca, cb)


def _bdot_bf16_fwd(a, b, ca, cb):
    return _bdot_bf16(a, b, ca, cb), (a, b)


def _bdot_bf16_bwd(ca, cb, res, g):
    a, b = res
    fa, fb = 3 - ca, 3 - cb
    da = _bdot_bf16(g, b, 2, fb) if ca == 2 else _bdot_bf16(b, g, fb, 2)
    db = _bdot_bf16(a, g, fa, 1) if cb == 1 else _bdot_bf16(g, a, 1, fa)
    return da, db


_bdot_bf16_vjp.defvjp(_bdot_bf16_fwd, _bdot_bf16_bwd)


def _neumann_inverse(low):
    n = low.shape[-1]
    eye = (lax.broadcasted_iota(jnp.int32, (n, n), 0) == lax.broadcasted_iota(jnp.int32, (n, n), 1)).astype(f32)
    p = -low
    x = eye[None] + p
    for _ in range(5):
        p = _bdot_bf16(p, p, 2, 1)
        x = x + _bdot_bf16(x, p, 2, 1)
    return x


@jax.custom_vjp
def _unit_lower_inverse(low):
    return _neumann_inverse(low)


def _uli_fwd(low):
    t = _neumann_inverse(low)
    return t, t


def _uli_bwd(t, dt):
    return (-_bdot_bf16(_bdot_bf16(t, dt, 1, 1), t, 2, 2),)


_unit_lower_inverse.defvjp(_uli_fwd, _uli_bwd)


def _stack(xs):
    return jnp.concatenate([x[None] for x in xs], axis=0)


DELTA_CHUNKS = 2


def _delta_chunks(qkv, bg, state, differentiated):
    inverse = _unit_lower_inverse if differentiated else _neumann_inverse
    lo = _bdot_bf16_vjp if differentiated else _bdot_bf16
    B, n = qkv.shape[0], qkv.shape[1] // CH
    G = B * DNH
    N = n * G
    triples = [(i, b, h) for i in range(n) for b in range(B) for h in range(DNH)]
    col = lambda i, b, h, kind: qkv[b, i * CH:(i + 1) * CH, (3 * h + kind) * DND:(3 * h + kind + 1) * DND]
    q, k, v = [_stack([col(i, b, h, kind) for i, b, h in triples]) for kind in range(3)]
    lane = lax.broadcasted_iota(jnp.int32, (CH, LANE), 1)
    pick = lambda i, b, l: jnp.sum(jnp.where(lane == l, bg[b, i * CH:(i + 1) * CH], 0.0), axis=1, keepdims=True)
    beta = _stack([pick(i, b, h) for i, b, h in triples])
    g = _stack([pick(i, b, h + DNH) for i, b, h in triples])
    ri = lax.broadcasted_iota(jnp.int32, (CH, CH), 0)
    ci = lax.broadcasted_iota(jnp.int32, (CH, CH), 1)
    incl, strict = (ri >= ci)[None], (ri > ci)[None]
    gc = _bdot(jnp.broadcast_to(incl.astype(f32), (N, CH, CH)), jnp.broadcast_to(g, (N, CH, LANE)), 2, 1, MID)
    e0 = jnp.broadcast_to((lane == 0).astype(f32)[None], (N, CH, LANE))
    gc_row = _bdot(e0, gc, 2, 2, MID)
    diff = gc[:, :, :CH] - gc_row
    decay = jnp.where(incl, jnp.exp(jnp.where(incl, diff, 0.0)), 0.0)
    qs = q * (DND ** -0.5)
    kb, vb = k * beta, v * beta
    eg = jnp.exp(gc)
    with_k = lo(jnp.concatenate([kb, qs], axis=1), k, 2, 2)
    low = jnp.where(strict, with_k[:, :CH] * decay, 0.0)
    intra = jnp.where(incl, with_k[:, CH:] * decay, 0.0)
    tinv = inverse(low)
    solved = lo(tinv, jnp.concatenate([vb, kb * eg], axis=2), 2, 1)
    gl = gc[:, CH - 1:CH, :]
    k_tail = k * jnp.exp(gl - gc)
    to_state = jnp.concatenate([solved[:, :, DND:], qs * eg], axis=1)
    decay_all = jnp.exp(gl)
    outs = []
    for i in range(n):
        sl = slice(i * G, (i + 1) * G)
        with_state = lo(to_state[sl], state, 2, 1)
        v_new = solved[sl, :, :DND] - with_state[:, :CH]
        outs.append(with_state[:, CH:] + lo(intra[sl], v_new, 2, 1))
        state = state * decay_all[sl] + lo(k_tail[sl], v_new, 1, 1)
    return outs, state


def delta_fwd(qkvn, bg, comm):
    B, S, _ = qkvn.shape
    n = DELTA_CHUNKS if (S // CH) % DELTA_CHUNKS == 0 else 1
    steps, G, rows = S // (n * CH), B * DNH, n * CH

    def body(qkv_ref, bg_ref, o_ref, st_ref, state):
        @pl.when(pl.program_id(0) == 0)
        def _():
            state[...] = jnp.zeros(state.shape, f32)
        s0 = state[...]
        st_ref[0] = s0
        outs, s1 = _delta_chunks(qkv_ref[...], bg_ref[...], s0, False)
        for i, o in enumerate(outs):
            for b in range(B):
                for h in range(DNH):
                    o_ref[b, i * CH:(i + 1) * CH, h * DND:(h + 1) * DND] = o[b * DNH + h]
        state[...] = s1

    at = lambda c: lambda: pl.program_id(0) == c
    return pl.pallas_call(
        _ride(body, 2, 2, 1, comm, at(0), at((7 * steps) // 8), at(steps - 1)), name="delta_fwd", grid=(steps,),
        in_specs=[pl.BlockSpec((B, rows, 3 * DNH * DND), lambda c: (0, c, 0)), pl.BlockSpec((B, rows, LANE), lambda c: (0, c, 0))] + comm.specs,
        out_specs=[pl.BlockSpec((B, rows, DNH * DND), lambda c: (0, c, 0)), pl.BlockSpec((1, G, DND, DND), lambda c: (c, 0, 0, 0))] + comm.specs,
        out_shape=[jax.ShapeDtypeStruct((B, S, DNH * DND), f32), jax.ShapeDtypeStruct((steps, G, DND, DND), f32)] + comm.out_shape,
        scratch_shapes=[pltpu.VMEM((G, DND, DND), f32)] + comm.scratch, compiler_params=_cp(("arbitrary",)),
    )(qkvn, bg, *comm.arrs)


def delta_bwd(qkvn, bg, states, do, comm):
    B, S, _ = qkvn.shape
    steps, G = states.shape[0], B * DNH
    rows = S // steps
    n = rows // CH

    def body(qkv_ref, bg_ref, st_ref, do_ref, dqkv_ref, dbg_ref, dstate):
        @pl.when(pl.program_id(0) == 0)
        def _():
            dstate[...] = jnp.zeros(dstate.shape, f32)
        _, vjp = jax.vjp(lambda a, g, s: _delta_chunks(a, g, s, True), qkv_ref[...], bg_ref[...], st_ref[0])
        do = [_stack([do_ref[b, i * CH:(i + 1) * CH, h * DND:(h + 1) * DND] for b in range(B) for h in range(DNH)]) for i in range(n)]
        dqkv, dbg, ds = vjp((do, dstate[...]))
        dqkv_ref[...] = dqkv
        dbg_ref[...] = dbg
        dstate[...] = ds

    rev = lambda c: steps - 1 - c
    at = lambda c: lambda: pl.program_id(0) == c
    return pl.pallas_call(
        _ride(body, 4, 2, 1, comm, at(0), at(steps - 1), at(steps - 1)), name="delta_bwd", grid=(steps,),
        in_specs=[pl.BlockSpec((B, rows, 3 * DNH * DND), lambda c: (0, rev(c), 0)), pl.BlockSpec((B, rows, LANE), lambda c: (0, rev(c), 0)),
                  pl.BlockSpec((1, G, DND, DND), lambda c: (rev(c), 0, 0, 0)),
                  pl.BlockSpec((B, rows, DNH * DND), lambda c: (0, rev(c), 0))] + comm.specs,
        out_specs=[pl.BlockSpec((B, rows, 3 * DNH * DND), lambda c: (0, rev(c), 0)),
                   pl.BlockSpec((B, rows, LANE), lambda c: (0, rev(c), 0))] + comm.specs,
        out_shape=[jax.ShapeDtypeStruct((B, S, 3 * DNH * DND), f32), jax.ShapeDtypeStruct((B, S, LANE), f32)] + comm.out_shape,
        scratch_shapes=[pltpu.VMEM((G, DND, DND), f32)] + comm.scratch, compiler_params=_cp(("arbitrary",)),
    )(qkvn, bg, states, do, *comm.arrs)


GELU_C0, GELU_C1 = math.sqrt(2.0 / math.pi), 0.044715


def _ffn_specs(S):
    nblk = DFF // LANE
    return [pl.BlockSpec((1, S, LANE), lambda i, b: (b, 0, i)), pl.BlockSpec((1, S, LANE), lambda i, b: (b, 0, nblk + i)),
            pl.BlockSpec((FK, LANE), lambda i, b: (0, i)), pl.BlockSpec((FK, LANE), lambda i, b: (0, nblk + i))]


def ffnconv_fwd(up, conv_w):
    B, S, _ = up.shape
    rows = min(FFN_ROWS, S)

    def body(g_ref, v_ref, gw_ref, vw_ref, o_ref, xg, xv):
        _stage_rows(xg, g_ref[0].astype(f32))
        _stage_rows(xv, v_ref[0].astype(f32))
        gw, vw = gw_ref[...], vw_ref[...]
        for r in range(0, S, rows):
            g, _ = _conv_rows(xg, gw, FK, r, rows)
            v, _ = _conv_rows(xv, vw, FK, r, rows)
            t = jnp.tanh(GELU_C0 * (g * (1.0 + GELU_C1 * (g * g))))
            o_ref[0, pl.ds(r, rows), :] = (0.5 * g * (1.0 + t) * v).astype(o_ref.dtype)

    return pl.pallas_call(
        body, name="ffnconv_fwd", grid=(DFF // LANE, B), in_specs=_ffn_specs(S),
        out_specs=pl.BlockSpec((1, S, LANE), lambda i, b: (b, 0, i)), out_shape=jax.ShapeDtypeStruct((B, S, DFF), bf16),
        scratch_shapes=[pltpu.VMEM((S + 8, LANE), f32)] * 2, compiler_params=_cp(("parallel", "parallel")),
    )(up, up, conv_w, conv_w)


def ffnconv_bwd(up, conv_w, dact, comm):
    B, S, _ = up.shape
    rows = min(FFN_ROWS, S)

    def body(g_ref, v_ref, gw_ref, vw_ref, dy_ref, dx_ref, dw_ref, xg, xv, dg, dv):
        _stage_rows(xg, g_ref[0].astype(f32))
        _stage_rows(xv, v_ref[0].astype(f32))
        gw, vw = gw_ref[...], vw_ref[...]
        dgw = [jnp.zeros((8, LANE), f32) for _ in range(FK)]
        dvw = [jnp.zeros((8, LANE), f32) for _ in range(FK)]
        for r in range(0, S, rows):
            g, gwins = _conv_rows(xg, gw, FK, r, rows)
            v, vwins = _conv_rows(xv, vw, FK, r, rows)
            g2 = g * g
            t = jnp.tanh(GELU_C0 * (g * (1.0 + GELU_C1 * g2)))
            half = 0.5 * (1.0 + t)
            dgelu = half + (0.5 * GELU_C0) * g * (1.0 - t * t) * (1.0 + (3.0 * GELU_C1) * g2)
            dy = dy_ref[0, pl.ds(r, rows), :].astype(f32)
            dvc = dy * (g * half)
            dgc = dy * v * dgelu
            dg[pl.ds(r, rows), :] = dgc
            dv[pl.ds(r, rows), :] = dvc
            for j in range(FK):
                dgw[j] = dgw[j] + _fold8(dgc * gwins[j])
                dvw[j] = dvw[j] + _fold8(dvc * vwins[j])
        dg[S:S + 8] = jnp.zeros((8, LANE), f32)
        dv[S:S + 8] = jnp.zeros((8, LANE), f32)
        for r in range(0, S, rows):
            dx_ref[0, 0, pl.ds(r, rows), :] = _conv_rows_t(dg, gw, FK, r, rows).astype(dx_ref.dtype)
            dx_ref[1, 0, pl.ds(r, rows), :] = _conv_rows_t(dv, vw, FK, r, rows).astype(dx_ref.dtype)

        @pl.when(pl.program_id(1) == 0)
        def _():
            dw_ref[...] = jnp.zeros(dw_ref.shape, f32)
        dw_ref[0] += jnp.concatenate([jnp.sum(d, axis=0, keepdims=True) for d in dgw], axis=0)
        dw_ref[1] += jnp.concatenate([jnp.sum(d, axis=0, keepdims=True) for d in dvw], axis=0)

    nblk = DFF // LANE
    at = lambda i, b: lambda: (pl.program_id(0) == i) & (pl.program_id(1) == b)
    return pl.pallas_call(
        _ride(body, 5, 2, 4, comm, at(0, 0), at(nblk - 1, B - 1), at(nblk - 1, B - 1)), name="ffnconv_bwd", grid=(nblk, B),
        in_specs=_ffn_specs(S) + [pl.BlockSpec((1, S, LANE), lambda i, b: (b, 0, i))] + comm.specs,
        out_specs=[pl.BlockSpec((2, 1, S, LANE), lambda i, b: (0, b, 0, i)),
                   pl.BlockSpec((2, FK, LANE), lambda i, b: (0, 0, i))] + comm.specs,
        out_shape=[jax.ShapeDtypeStruct((2, B, S, DFF), bf16), jax.ShapeDtypeStruct((2, FK, DFF), f32)] + comm.out_shape,
        scratch_shapes=[pltpu.VMEM((S + 8, LANE), f32)] * 4 + comm.scratch, compiler_params=_cp(("arbitrary", "arbitrary")),
    )(up, up, conv_w, conv_w, dact, *comm.arrs)


def ada_fwd(c_all, ada_w, ada_b):
    def body(c_ref, w_ref, b_ref, o_ref):
        c = c_ref[...]
        act = (c * jax.nn.sigmoid(c)).astype(bf16)
        o_ref[...] = jnp.dot(act, w_ref[...].astype(bf16), preferred_element_type=f32) + b_ref[...]

    return pl.pallas_call(body, name="ada_fwd", out_shape=jax.ShapeDtypeStruct((c_all.shape[0], ada_w.shape[1]), f32),
                          compiler_params=pltpu.CompilerParams(vmem_limit_bytes=VMEM_LIMIT))(c_all, ada_w, ada_b)


def ada_bwd(c_all, dmod):
    def body(c_ref, d_ref, o_ref):
        c = c_ref[...]
        act = (c * jax.nn.sigmoid(c)).astype(bf16)
        o_ref[...] = lax.dot_general(act, d_ref[...].astype(bf16), (((0,), (0,)), ((), ())), preferred_element_type=f32)

    return pl.pallas_call(body, name="ada_bwd", out_shape=jax.ShapeDtypeStruct((c_all.shape[1], dmod.shape[1]), f32),
                          compiler_params=pltpu.CompilerParams(vmem_limit_bytes=VMEM_LIMIT))(c_all, dmod)


def loss_head(h1, y2, target, g2, w):
    def fn(t, b, c):
        h, y, tg = [v.astype(f32) for v in t]

        def loss_fn(h, y, g, w):
            e = h + g * _rms(y, w) - tg
            return 0.5 * jnp.sum(jnp.mean(e * e, axis=-1))

        loss, grads = jax.value_and_grad(loss_fn, argnums=(0, 1, 2, 3))(h, y, b[0], c[0])
        return [grads[0], grads[1]], [grads[2], grads[3], jnp.full((1, LANE), loss, f32)]

    return rowcall("loss_head", fn, [(h1, D, 0), (y2, D, 0), (target, D, 0)], [g2], [w], [(D, f32), (D, bf16)],
                   [(1, D), (1, D), (1, LANE)])


def adamw(w, gparts, m, v, name):
    R, C = w.shape
    P = gparts.shape[0]
    budget = 2 * 1024 * 1024
    tr, tc = R, C
    if R * C * 4 > budget and R % 8 == 0:
        tr = max(t for t in range(8, R + 1, 8) if R % t == 0 and t * C * 4 <= budget)
    elif R * C * 4 > budget:
        tc = max(t for t in range(LANE, C + 1, LANE) if C % t == 0 and R * t * 4 <= budget)

    def body(w_ref, g_ref, m_ref, v_ref, go, do, mo, vo):
        g = g_ref[0].astype(f32)
        for p in range(1, P):
            g = g + g_ref[p].astype(f32)
        m2 = B1 * m_ref[...] + (1.0 - B1) * g
        v2 = B2 * v_ref[...] + (1.0 - B2) * jnp.square(g)
        m_hat = m2 * (1.0 / (1.0 - B1 ** STEP))
        v_hat = v2 * (1.0 / (1.0 - B2 ** STEP))
        go[...] = g
        do[...] = -LR * (m_hat / (jnp.sqrt(v_hat) + EPS) + WD * w_ref[...])
        mo[...] = m2
        vo[...] = v2

    blk = pl.BlockSpec((tr, tc), lambda i, j: (i, j))
    return pl.pallas_call(
        body, name=name, grid=(R // tr, C // tc), in_specs=[blk, pl.BlockSpec((P, tr, tc), lambda i, j: (0, i, j)), blk, blk],
        out_specs=[blk] * 4, out_shape=[jax.ShapeDtypeStruct((R, C), f32)] * 4, compiler_params=_cp(("parallel", "parallel")),
    )(w, gparts, m, v)


def _pack_w_in(wt):
    aq, ak, av, dqkv, dz, dbeta, da, ga, gd = jnp.split(wt, np.cumsum(IN_SPLITS)[:-1].tolist(), axis=0)
    ba = jnp.pad(jnp.concatenate([dbeta, da], axis=0), ((0, LANE - 2 * DNH), (0, 0)))
    return jnp.concatenate([ga, gd, aq, dqkv, dz, ak, av, ba], axis=0)


def _unpack_w_in(p):
    row = lambda cb, n: p[cb * LANE: cb * LANE + n]
    ba = row(CB_BA, 2 * DNH)
    return jnp.concatenate([row(CB_AQ, HQ * HD), row(CB_AK, HKV * HD), row(CB_AV, HKV * HD), row(CB_DQKV, 3 * DNH * DND),
                            row(CB_DZ, DNH * DND), ba[:DNH], ba[DNH:], row(CB_GA, D), row(CB_GD, D)], axis=0)


def _cols_gathered(g):
    return g.transpose(1, 0, 2).reshape(g.shape[1], NDEV * g.shape[2])


def _cols_split(w):
    r = w.shape[0]
    return w.reshape(r, NDEV, w.shape[1] // NDEV).transpose(1, 0, 2)


def kernel(x, c, ada_w, ada_b, norm_mix_pre, norm_mix_post, norm_ffn_pre, norm_ffn_post, w_in, dn_conv_w, dn_a_log, dn_dt_bias, dn_norm_w, attn_sinks, rel_bias, w_attn_branch, w_dn_branch, w_out, ffn_w_up, ffn_conv_w, ffn_w_down, loss_target, m_ada_w, m_ada_b, m_norm_mix_pre, m_norm_mix_post, m_norm_ffn_pre, m_norm_ffn_post, m_w_in, m_dn_conv_w, m_dn_a_log, m_dn_dt_bias, m_dn_norm_w, m_attn_sinks, m_rel_bias, m_w_attn_branch, m_w_dn_branch, m_w_out, m_ffn_w_up, m_ffn_conv_w, m_ffn_w_down, v_ada_w, v_ada_b, v_norm_mix_pre, v_norm_mix_post, v_norm_ffn_pre, v_norm_ffn_post, v_w_in, v_dn_conv_w, v_dn_a_log, v_dn_dt_bias, v_dn_norm_w, v_attn_sinks, v_rel_bias, v_w_attn_branch, v_w_dn_branch, v_w_out, v_ffn_w_up, v_ffn_conv_w, v_ffn_w_down):
    B, S, _ = x.shape
    T = B * S
    me = 4 * lax.axis_index("x") + 2 * lax.axis_index("y") + lax.axis_index("c")
    big = dict(w_in=w_in, dn_conv_w=dn_conv_w, w_attn_branch=w_attn_branch, w_dn_branch=w_dn_branch, w_out=w_out,
               ffn_w_up=ffn_w_up, ffn_conv_w=ffn_conv_w, ffn_w_down=ffn_w_down)
    big_names = list(big)

    first, mid, late = ["w_in", "dn_conv_w"], ["w_attn_branch", "w_dn_branch", "w_out"], ["ffn_w_up", "ffn_conv_w"]
    transposed = ("w_in", "ffn_w_up")
    local = lambda n, a: a[0].T if n in transposed else a[0]
    shard = lambda names: [local(n, big[n]).astype(bf16) for n in names]
    *got, c_all = _exchange(shard(first) + [c], "gather_w_in", two_level=True)
    gw = dict(zip(first, got))
    c_all = c_all.reshape(NDEV * B, D)

    wp = _pack_w_in(gw["w_in"].reshape(IN_DIM, D))
    conv_dn = _cols_gathered(gw["dn_conv_w"]).astype(f32)

    ncol = ada_w.shape[2]
    ada_b_mine = lax.dynamic_slice_in_dim(ada_b, me * ncol, ncol, axis=1)
    mod_cols = ada_fwd(c_all, ada_w[0], ada_b_mine)
    (mod_g,) = _exchange([mod_cols], "gather_mod")
    mod = lax.dynamic_slice_in_dim(mod_g, me * B, B, axis=1).transpose(1, 0, 2).reshape(B, NMOD * D)
    sh1, sc1, g1, sh2, sc2, g2 = [mod[:, i * D:(i + 1) * D].reshape(B, 1, D) for i in range(NMOD)]

    onehot = (jnp.asarray(_bucket_table()).reshape(1, -1) == jnp.arange(NBUCK, dtype=jnp.int32)[:, None]).astype(f32)
    bias = mm(rel_bias.T, onehot, "nn", f32, "bias_table", tn=8192, precision=HI).reshape(HQ, WIN, 2 * WIN)
    sinks = attn_sinks.reshape(HQ, 1, 1)
    a_log_pad = jnp.pad(dn_a_log, ((0, 0), (DNH, LANE - 2 * DNH)))
    dt_bias_pad = jnp.pad(dn_dt_bias, ((0, 0), (DNH, LANE - 2 * DNH)))

    (u1,) = rowcall_fwd("mix_pre", f_rms_mod, [(x, D, 0)], [sc1, sh1], [norm_mix_pre], [(D, bf16)])
    proj, gw["ffn_w_down"] = mm(u1.reshape(T, D), wp, "nt", bf16, "proj", tm=512, tn=CB_BA * LANE, b_cols=(0, 1),
                                comm=_Comm(shard(["ffn_w_down"]), two_level=True))
    proj = proj.reshape(B, S, CB_BA * LANE)
    ba = mm(u1.reshape(T, D), wp, "nt", f32, "proj_ba", tn=LANE, b_cols=(CB_BA, 1)).reshape(B, S, LANE)
    ya, *got = attn_fwd(proj, bias, sinks, _Comm(shard(mid), two_level=True))
    gw.update(zip(mid, got))
    wa = _cols_gathered(gw["w_attn_branch"])
    wd = _cols_gathered(gw["w_dn_branch"])
    wo = gw["w_out"].reshape(D, D)
    qkvn = dnconv_fwd(proj, conv_dn)
    (bg,) = rowcall_fwd("dn_gate", f_gate, [(ba, LANE, 0)], [], [a_log_pad, dt_bias_pad], [(LANE, f32)])
    o_dn, states, *got = delta_fwd(qkvn, bg, _Comm(shard(late), two_level=True))
    gw.update(zip(late, got))
    wup = gw["ffn_w_up"].reshape(2 * DFF, D)
    conv_ffn = _cols_gathered(gw["ffn_conv_w"]).astype(f32)
    wdown = gw["ffn_w_down"].reshape(DFF, D)
    (yd,) = rowcall_fwd("dn_out", f_dnout, [(o_dn, DNH * DND, 0), (proj, DNH * DND, CB_DZ // 4)], [], [dn_norm_w], [(DNH * DND, bf16)])
    pa = mm(ya.reshape(T, HQ * HD), wa, "nn", bf16, "attn_branch").reshape(B, S, D)
    pd = mm(yd.reshape(T, DNH * DND), wd, "nn", bf16, "dn_branch").reshape(B, S, D)
    merge_tok = [(proj, D, CB_GA // 8), (proj, D, CB_GD // 8), (pa, D, 0), (pd, D, 0)]
    (merged,) = rowcall_fwd("merge", f_merge, merge_tok, [], [], [(D, bf16)])
    y1 = mm(merged.reshape(T, D), wo, "nn", bf16, "mix_out").reshape(B, S, D)
    post_pre = ([(x, D, 0), (y1, D, 0)], [g1, sc2, sh2], [norm_mix_post, norm_ffn_pre])
    h1, u2 = rowcall_fwd("mix_post_ffn_pre", f_post_pre, *post_pre, [(D, f32), (D, bf16)])
    up = mm(u2.reshape(T, D), wup, "nt", bf16, "ffn_up", tn=2816).reshape(B, S, 2 * DFF)
    act = ffnconv_fwd(up, conv_ffn)
    y2 = mm(act.reshape(T, DFF), wdown, "nn", bf16, "ffn_down", tk=2816).reshape(B, S, D)

    dh1_a, dy2, dg2, dw_ffn_post, loss_b = loss_head(h1, y2, loss_target, g2, norm_ffn_post)
    dy2f = dy2.reshape(T, D)
    dact = mm(dy2f, wdown, "nt", bf16, "ffn_down_dx", tn=2816).reshape(B, S, DFF)
    g_wdown = mm(act.reshape(T, DFF), dy2f, "tn", bf16, "ffn_down_dw", tm=2816, tn=512, tk=4096)
    parts = {}
    outbox = lambda d: _Comm([d[n].astype(bf16) for n in d], scatter=True)
    dup, g_conv_ffn, parts["ffn_w_down"] = ffnconv_bwd(up, conv_ffn, dact, outbox(dict(ffn_w_down=g_wdown.reshape(NDEV, DFF // NDEV, D))))
    dupf = dup.reshape(2, T, DFF)
    g_conv_ffn = g_conv_ffn.transpose(1, 0, 2).reshape(FK, 2 * DFF)
    du2 = mm(dupf, wup, "nn", bf16, "ffn_up_dx", tk=2816).reshape(B, S, D)
    g_wup = mm(dupf, u2.reshape(T, D), "tn", bf16, "ffn_up_dw", tm=1408, tk=2048)
    dh1, dy1, dg1, dsc2, dsh2, dw_mix_post, dw_ffn_pre = rowcall_bwd(
        "mix_post_ffn_pre_bwd", f_post_pre, *post_pre, [(dh1_a, D, 0), (du2, D, 0)], [(0, f32), (1, bf16)])
    dy1f = dy1.reshape(T, D)
    dmerged = mm(dy1f, wo, "nt", bf16, "mix_out_dx").reshape(B, S, D)
    g_wo = mm(merged.reshape(T, D), dy1f, "tn", bf16, "mix_out_dw", tk=2048)
    dproj = lax.empty((B, S, NP), bf16)
    dproj, dpa, dpd = rowcall_bwd("merge_bwd", f_merge, merge_tok, [], [], [(dmerged, D, 0)],
                                  [(0, bf16), (1, bf16), (2, bf16), (3, bf16)], join_first=2, into=(dproj, CB_GA // 16))
    dpaf, dpdf = dpa.reshape(T, D), dpd.reshape(T, D)
    dya = mm(dpaf, wa, "nt", bf16, "attn_branch_dx").reshape(B, S, HQ * HD)
    g_wa = mm(ya.reshape(T, HQ * HD), dpaf, "tn", bf16, "attn_branch_dw", tk=2048)
    dyd = mm(dpdf, wd, "nt", bf16, "dn_branch_dx").reshape(B, S, DNH * DND)
    g_wd = mm(yd.reshape(T, DNH * DND), dpdf, "tn", bf16, "dn_branch_dw", tk=2048)
    dproj, do_dn, dw_dn_norm = rowcall_bwd("dn_out_bwd", f_dnout, [(o_dn, DNH * DND, 0), (proj, DNH * DND, CB_DZ // 4)], [], [dn_norm_w],
                                           [(dyd, DNH * DND, 0)], [(1, bf16), (0, f32)], into=(dproj, CB_DZ // 4))
    send = dict(ffn_w_up=g_wup.reshape(NDEV, 2 * DFF // NDEV, D), ffn_conv_w=_cols_split(g_conv_ffn))
    dqkvn, dbg, *got = delta_bwd(qkvn, bg, states, do_dn, outbox(send))
    parts.update(zip(send, got))
    dproj, da_log_pad, ddt_bias_pad = rowcall_bwd("dn_gate_bwd", f_gate, [(ba, LANE, 0)], [], [a_log_pad, dt_bias_pad],
                                                  [(dbg, LANE, 0)], [(0, bf16)], into=(dproj, CB_BA))
    dproj, g_conv_dn = dnconv_bwd(proj, conv_dn, dqkvn, dproj)
    send = dict(w_attn_branch=_cols_split(g_wa), w_dn_branch=_cols_split(g_wd),
                w_out=g_wo.reshape(NDEV, D // NDEV, D))
    dproj, dk, dv, dbias, dsinks, *got = attn_bwd(proj, bias, sinks, dya, dproj, outbox(send))
    parts.update(zip(send, got))
    dproj = lax.dynamic_update_slice(dproj, jnp.concatenate([dk, dv], axis=2), (0, 0, CB_AK * LANE)).reshape(T, NP)
    g_wp = mm(dproj, u1.reshape(T, D), "tn", bf16, "proj_dw", tm=1664, tk=1024)
    send = dict(w_in=_unpack_w_in(g_wp).reshape(NDEV, IN_DIM // NDEV, D), dn_conv_w=_cols_split(g_conv_dn))
    du1, *got = mm(dproj, wp, "nn", bf16, "proj_dx", tm=512, tk=NP, comm=outbox(send))
    parts.update(zip(send, got))
    du1 = du1.reshape(B, S, D)
    grad_x, dsc1, dsh1, dw_mix_pre = rowcall_bwd("mix_pre_bwd", f_rms_mod, [(x, D, 0)], [sc1, sh1], [norm_mix_pre], [(du1, D, 0)],
                                                 [(0, f32)], add=(dh1, D, 0))
    g_rel = mm(dbias.reshape(HQ, WIN * 2 * WIN), onehot, "nt", f32, "rel_bias_dw", tk=8192, precision=HI)

    dmod = jnp.concatenate([dsh1, dsc1, dg1, dsh2, dsc2, dg2], axis=2).reshape(B, NMOD * D)

    zrow = lambda a: jnp.concatenate([a.reshape(1, -1), jnp.zeros((B - 1, a.size), f32)], axis=0)
    small_g = jnp.concatenate([
        dmod, dw_mix_pre.reshape(B, D), dw_mix_post.reshape(B, D), dw_ffn_pre.reshape(B, D), dw_ffn_post.reshape(B, D),
        da_log_pad.reshape(B, LANE)[:, DNH:2 * DNH], ddt_bias_pad.reshape(B, LANE)[:, DNH:2 * DNH], dw_dn_norm.reshape(B, DND),
        zrow(dsinks), zrow(g_rel.T), loss_b.reshape(B, LANE)[:, :1], jnp.zeros((B, SMALL_PAD - SMALL_N - 1), f32)], axis=1)
    (small_all,) = _exchange([small_g], "gather_small")
    dmod_cols = lax.dynamic_slice_in_dim(small_all.reshape(NDEV * B, SMALL_PAD), me * ncol, ncol, axis=1)
    g_ada_w = ada_bwd(c_all, dmod_cols)
    small_w = dict(ada_b=(ada_b, m_ada_b, v_ada_b), norm_mix_pre=(norm_mix_pre, m_norm_mix_pre, v_norm_mix_pre),
                   norm_mix_post=(norm_mix_post, m_norm_mix_post, v_norm_mix_post), norm_ffn_pre=(norm_ffn_pre, m_norm_ffn_pre, v_norm_ffn_pre),
                   norm_ffn_post=(norm_ffn_post, m_norm_ffn_post, v_norm_ffn_post), dn_a_log=(dn_a_log, m_dn_a_log, v_dn_a_log),
                   dn_dt_bias=(dn_dt_bias, m_dn_dt_bias, v_dn_dt_bias), dn_norm_w=(dn_norm_w, m_dn_norm_w, v_dn_norm_w),
                   attn_sinks=(attn_sinks, m_attn_sinks, v_attn_sinks), rel_bias=(rel_bias, m_rel_bias, v_rel_bias))

    def pack(i, fill):
        row = jnp.concatenate([small_w[n][i].reshape(1, -1) for n, _ in SMALL], axis=1)
        return jnp.pad(row, ((0, 0), (0, SMALL_PAD - SMALL_N)), constant_values=fill)

    small_out = adamw(pack(0, 0.0), small_all.reshape(NDEV * B, 1, SMALL_PAD), pack(1, 0.0), pack(2, 1.0), "adamw_small")
    loss = small_out[0][0, SMALL_N]

    res = {}
    off = 0
    for n, size in SMALL:
        shp = small_w[n][0].shape
        res[n] = [o[:, off:off + size].reshape(shp) for o in small_out]
        off += size
    res["ada_w"] = [o[None] for o in adamw(ada_w[0], g_ada_w[None], m_ada_w[0], v_ada_w[0], "adamw_ada_w")]
    moments = dict(w_in=(m_w_in, v_w_in), dn_conv_w=(m_dn_conv_w, v_dn_conv_w), w_attn_branch=(m_w_attn_branch, v_w_attn_branch),
                   w_dn_branch=(m_w_dn_branch, v_w_dn_branch), w_out=(m_w_out, v_w_out), ffn_w_up=(m_ffn_w_up, v_ffn_w_up),
                   ffn_conv_w=(m_ffn_conv_w, v_ffn_conv_w), ffn_w_down=(m_ffn_w_down, v_ffn_w_down))
    for n in big_names:
        outs = adamw(local(n, big[n]), parts[n], local(n, moments[n][0]), local(n, moments[n][1]), "adamw_" + n)
        res[n] = [(o.T if n in transposed else o)[None] for o in outs]

    order = ["ada_w", "ada_b", "norm_mix_pre", "norm_mix_post", "norm_ffn_pre", "norm_ffn_post", "w_in", "dn_conv_w", "dn_a_log",
             "dn_dt_bias", "dn_norm_w", "attn_sinks", "rel_bias", "w_attn_branch", "w_dn_branch", "w_out", "ffn_w_up", "ffn_conv_w",
             "ffn_w_down"]
    return (loss, grad_x, *[res[n][0] for n in order], *[res[n][1] for n in order], *[res[n][2] for n in order],
            *[res[n][3] for n in order])
```

```python
import functools
import math

import numpy as np
import jax
import jax.numpy as jnp
from jax import lax
from jax.experimental import pallas as pl
from jax.experimental.pallas import tpu as pltpu

f32 = jnp.float32
bf16 = jnp.bfloat16
HI = lax.Precision.HIGHEST
MID = lax.Precision.HIGH
MESH = pl.DeviceIdType.MESH

NDEV = 8
D = 1024
HQ, HKV, HD, WIN, NBUCK, MAXDIST = 8, 2, 64, 128, 32, 128
DNH, DND, DNK, CH = 4, 128, 4, 64
DFF, FK = 2816, 3
NMOD = 6
RMS_EPS = 1e-6
L2_EPS = 1e-6
NEG_INF = -1e30
LR, B1, B2, EPS, WD, STEP = 0.001, 0.9, 0.999, 1e-08, 0.01, 10

LANE = 128
CB_GA, CB_GD, CB_AQ, CB_DQKV, CB_DZ, CB_AK, CB_AV, CB_BA, NPB = 0, 8, 16, 20, 32, 36, 37, 38, 39
NP = NPB * LANE
IN_SPLITS = (HQ * HD, HKV * HD, HKV * HD, 3 * DNH * DND, DNH * DND, DNH, DNH, D, D)
IN_DIM = sum(IN_SPLITS)
VMEM_LIMIT = 56 * 1024 * 1024

SMALL = (("ada_b", NMOD * D), ("norm_mix_pre", D), ("norm_mix_post", D), ("norm_ffn_pre", D), ("norm_ffn_post", D),
         ("dn_a_log", DNH), ("dn_dt_bias", DNH), ("dn_norm_w", DND), ("attn_sinks", HQ), ("rel_bias", NBUCK * HQ))
SMALL_N = sum(n for _, n in SMALL)
SMALL_PAD = 10752


def _cp(sem):
    return pltpu.CompilerParams(dimension_semantics=sem, vmem_limit_bytes=VMEM_LIMIT)


def _pick(dim, target):
    if dim <= target:
        return dim
    best = None
    for d in range(LANE, target + 1, LANE):
        if dim % d == 0:
            best = d
    assert best is not None, (dim, target)
    return best


def _me():
    x, y, c = lax.axis_index("x"), lax.axis_index("y"), lax.axis_index("c")
    return x, y, c, 4 * x + 2 * y + c


def _peer(x, y, c, k):
    px = 1 - x if k & 4 else x
    py = 1 - y if k & 2 else y
    pc = 1 - c if k & 1 else c
    return (px, py, pc), 4 * px + 2 * py + pc


class _Comm:
    def __init__(self, arrs, scatter=False, two_level=False):
        assert not (scatter and two_level)
        self.arrs, self.n, self.scatter, self.two_level = list(arrs), len(arrs), scatter, two_level
        if scatter:
            self.out_shape = [jax.ShapeDtypeStruct(a.shape, a.dtype) for a in arrs]
        else:
            self.out_shape = [jax.ShapeDtypeStruct((NDEV,) + a.shape, a.dtype) for a in arrs]
        nsem = self.n * (NDEV - 1)
        self.scratch = [pltpu.SemaphoreType.DMA((nsem,)), pltpu.SemaphoreType.DMA((nsem,)), pltpu.SemaphoreType.DMA((self.n,))]
        self.specs = [pl.BlockSpec(memory_space=pl.ANY)] * self.n

    def phases(self, ins, out, send, recv, loc):
        x, y, c, me = _me()

        def remote(a, k, src, dst, to):
            s = a * (NDEV - 1) + k - 1
            return pltpu.make_async_remote_copy(src_ref=src, dst_ref=dst, send_sem=send.at[s], recv_sem=recv.at[s],
                                                device_id=to, device_id_type=MESH)

        def local(a):
            return pltpu.make_async_copy(ins[a].at[me] if self.scatter else ins[a], out[a].at[me], loc.at[a])

        if not self.two_level:
            def mine(a, k):
                peer, pid = _peer(x, y, c, k)
                return remote(a, k, ins[a].at[pid] if self.scatter else ins[a], out[a].at[me], peer)

            def theirs(a, k):
                peer, pid = _peer(x, y, c, k)
                return remote(a, k, ins[a].at[pid] if self.scatter else ins[a], out[a].at[pid], peer)

            def start():
                for a in range(self.n):
                    local(a).start()
                    for k in range(1, NDEV):
                        mine(a, k).start()

            def forward():
                pass

            def finish():
                for a in range(self.n):
                    for k in range(1, NDEV):
                        mine(a, k).wait_send()
                    for k in range(1, NDEV):
                        theirs(a, k).wait_recv()
                    local(a).wait()

            return start, forward, finish

        sibling = (x, y, 1 - c)
        chips = [(1 - x, y), (x, 1 - y), (1 - x, 1 - y)]
        slot = lambda px, py, pc: 4 * px + 2 * py + pc

        def own(a, k, to):
            return remote(a, k, ins[a], out[a].at[me], to)

        def landed(a, k, frm):
            return remote(a, k, ins[a], out[a].at[slot(*frm)], frm)

        def passed(a, j):
            rows = out[a].at[slot(*chips[j], c)]
            return remote(a, 5 + j, rows, rows, sibling)

        def start():
            for a in range(self.n):
                local(a).start()
                own(a, 1, sibling).start()
                for j, chip in enumerate(chips):
                    own(a, 2 + j, (*chip, c)).start()

        def forward():
            for a in range(self.n):
                for j, chip in enumerate(chips):
                    landed(a, 2 + j, (*chip, c)).wait_recv()
                    passed(a, j).start()

        def finish():
            for a in range(self.n):
                landed(a, 1, sibling).wait_recv()
                for j, chip in enumerate(chips):
                    remote(a, 5 + j, ins[a], out[a].at[slot(*chip, 1 - c)], sibling).wait_recv()
                own(a, 1, sibling).wait_send()
                for j, chip in enumerate(chips):
                    own(a, 2 + j, (*chip, c)).wait_send()
                    passed(a, j).wait_send()
                local(a).wait()

        return start, forward, finish


def _ride(body, n_in, n_out, n_scr, comm, first, mid, last):
    k = comm.n

    def wrapped(*refs):
        ins, cins = refs[:n_in], refs[n_in:n_in + k]
        o0 = n_in + k
        outs, couts = refs[o0:o0 + n_out], refs[o0 + n_out:o0 + n_out + k]
        s0 = o0 + n_out + k
        scr, sems = refs[s0:s0 + n_scr], refs[s0 + n_scr:]
        start, forward, finish = comm.phases(cins, couts, *sems)
        pl.when(first())(start)
        body(*ins, *outs, *scr)
        pl.when(mid())(forward)
        pl.when(last())(finish)

    return wrapped


def _scatter_start(arrs, name):
    n = len(arrs)

    def body(*refs):
        ins, lands, send, recv, token = refs[:n], refs[n:2 * n], refs[2 * n], refs[2 * n + 1], refs[-1]
        x, y, c, me = _me()
        for a in range(n):
            for k in range(1, NDEV):
                peer, pid = _peer(x, y, c, k)
                s = a * (NDEV - 1) + k - 1
                pltpu.make_async_remote_copy(src_ref=ins[a].at[pid], dst_ref=lands[a].at[me], send_sem=send.at[s], recv_sem=recv.at[s],
                                             device_id=peer, device_id_type=MESH).start()
        token[...] = jnp.zeros(token.shape, token.dtype)

    hbm, sem = pl.BlockSpec(memory_space=pltpu.HBM), pl.BlockSpec(memory_space=pltpu.SEMAPHORE)
    nsem = n * (NDEV - 1)
    thru = [pltpu.HBM(a.shape, a.dtype) for a in arrs]
    return pl.pallas_call(
        body, name=name, in_specs=[hbm] * (2 * n),
        out_shape=(pltpu.SemaphoreType.DMA((nsem,)), pltpu.SemaphoreType.DMA((nsem,)), *thru, *thru, jax.ShapeDtypeStruct((8, LANE), f32)),
        out_specs=(sem, sem, *[hbm] * (2 * n), pl.BlockSpec(memory_space=pltpu.VMEM)),
        input_output_aliases={i: 2 + i for i in range(2 * n)},
        compiler_params=pltpu.CompilerParams(has_side_effects=pltpu.SideEffectType.DATAFLOW_SIDE_EFFECTING),
    )(*[pltpu.with_memory_space_constraint(a, pltpu.HBM) for a in arrs],
      *[pltpu.with_memory_space_constraint(lax.empty(a.shape, a.dtype), pltpu.HBM) for a in arrs])


def _scatter_finish(started, n, after, name):
    send, recv, *rest = started
    srcs, lands = rest[:n], rest[n:2 * n]

    def body(*refs):
        ins, lnd, send_ref, recv_ref = refs[:n], refs[n:2 * n], refs[2 * n], refs[2 * n + 1]
        x, y, c, me = _me()
        for a in range(n):
            for k in range(1, NDEV):
                peer, pid = _peer(x, y, c, k)
                s = a * (NDEV - 1) + k - 1
                cp = pltpu.make_async_remote_copy(src_ref=ins[a].at[pid], dst_ref=lnd[a].at[pid], send_sem=send_ref.at[s],
                                                  recv_sem=recv_ref.at[s], device_id=peer, device_id_type=MESH)
                cp.wait_send()
                cp.wait_recv()

    hbm, sem = pl.BlockSpec(memory_space=pltpu.HBM), pl.BlockSpec(memory_space=pltpu.SEMAPHORE)
    thru = [pltpu.HBM(a.shape, a.dtype) for a in srcs]
    out = pl.pallas_call(
        body, name=name, in_specs=[hbm] * (2 * n) + [sem, sem, pl.BlockSpec(memory_space=pl.ANY)],
        out_shape=(*thru, *thru), out_specs=tuple([hbm] * (2 * n)), input_output_aliases={i: i for i in range(2 * n)},
        compiler_params=pltpu.CompilerParams(has_side_effects=pltpu.SideEffectType.DATAFLOW_SIDE_EFFECTING),
    )(*srcs, *lands, send, recv, after)
    return list(out[n:])


def _exchange(arrs, name, scatter=False, two_level=False):
    comm = _Comm(arrs, scatter, two_level)

    def body(*refs):
        start, forward, finish = comm.phases(refs[:comm.n], refs[comm.n:2 * comm.n], *refs[2 * comm.n:])
        start()
        forward()
        finish()

    return pl.pallas_call(body, name=name, out_shape=comm.out_shape, in_specs=comm.specs, out_specs=comm.specs,
                          scratch_shapes=comm.scratch, compiler_params=pltpu.CompilerParams(has_side_effects=True))(*arrs)


def mm(a, b, mode, out_dtype, name, tm=1024, tn=1024, tk=1024, precision=None, comm=None, b_cols=None):
    a_parts = a.shape[0] if a.ndim == 3 else 1
    b_parts = b.shape[0] if b.ndim == 3 else 1
    assert b_parts == 1 or mode == "tn"
    ash, bsh = (a.shape[-2], a.shape[-1] * a_parts), b.shape[-2:]
    if mode == "nn":
        (M, K), (K2, N) = ash, bsh
    elif mode == "nt":
        (M, K), (N, K2) = ash, bsh
    else:
        (K, M), (K2, N) = ash, (bsh[0], bsh[1] * b_parts)
    assert K == K2, (name, a.shape, b.shape)
    col0 = 0
    if b_cols is not None:
        assert mode in ("nn", "nt") and tn % LANE == 0
        col0, N = b_cols[0], b_cols[1] * tn
    if mode == "tn":
        tm, tn, tk = _pick(M // a_parts, tm), _pick(N // b_parts, tn), _pick(K, tk)
    else:
        tm, tn, tk = _pick(M, tm), _pick(N // b_parts, tn), _pick(K // a_parts, tk)
    nk = K // tk
    if mode == "tn" and a_parts > 1:
        per = M // tm // a_parts
        a_spec = pl.BlockSpec((None, tk, tm), lambda i, j, k: (i // per, k, i % per))
    elif mode == "tn":
        a_spec = pl.BlockSpec((tk, tm), lambda i, j, k: (k, i))
    elif a_parts > 1:
        per = nk // a_parts
        a_spec = pl.BlockSpec((None, tm, tk), lambda i, j, k: (k // per, i, k % per))
    else:
        a_spec = pl.BlockSpec((tm, tk), lambda i, j, k: (i, k))
    if mode == "nt":
        b_spec = pl.BlockSpec((tn, tk), lambda i, j, k: (col0 + j, k))
    elif b_parts > 1:
        per = N // tn // b_parts
        b_spec = pl.BlockSpec((None, tk, tn), lambda i, j, k: (j // per, k, j % per))
    else:
        b_spec = pl.BlockSpec((tk, tn), lambda i, j, k: (k, col0 + j))
    dims = {"nn": ((1,), (0,)), "nt": ((1,), (1,)), "tn": ((0,), (0,))}[mode]

    def body(a_ref, b_ref, o_ref, *scr):
        p = lax.dot_general(a_ref[...], b_ref[...], (dims, ((), ())), preferred_element_type=f32, precision=precision)
        if nk == 1:
            o_ref[...] = p.astype(o_ref.dtype)
        else:
            acc = scr[0]
            k = pl.program_id(2)

            @pl.when(k == 0)
            def _():
                acc[...] = p

            @pl.when(k > 0)
            def _():
                acc[...] += p

            @pl.when(k == nk - 1)
            def _():
                o_ref[...] = acc[...].astype(o_ref.dtype)

    grid = (M // tm, N // tn, nk)
    scratch = [pltpu.VMEM((tm, tn), f32)] if nk > 1 else []
    out_spec = pl.BlockSpec((tm, tn), lambda i, j, k: (i, j))
    out_shape = jax.ShapeDtypeStruct((M, N), out_dtype)
    if comm is None:
        return pl.pallas_call(body, name=name, grid=grid, in_specs=[a_spec, b_spec], out_specs=out_spec, out_shape=out_shape,
                              scratch_shapes=scratch, compiler_params=_cp(("parallel", "parallel", "arbitrary")))(a, b)
    at = lambda pos: lambda: functools.reduce(jnp.logical_and, [pl.program_id(d) == p for d, p in enumerate(pos)])
    end = tuple(g - 1 for g in grid)
    return pl.pallas_call(
        _ride(body, 2, 1, len(scratch), comm, at((0, 0, 0)), at(end), at(end)), name=name, grid=grid,
        in_specs=[a_spec, b_spec] + comm.specs, out_specs=[out_spec] + comm.specs, out_shape=[out_shape] + comm.out_shape,
        scratch_shapes=scratch + comm.scratch, compiler_params=_cp(("arbitrary", "arbitrary", "arbitrary")),
    )(a, b, *comm.arrs)


def rowcall(name, fn, tok, bat, con, tok_out, acc_out, ts=256, into=None):
    B, S = tok[0][0].shape[:2]
    ts = min(ts, S)
    nt, nb, nc, no, na = len(tok), len(bat), len(con), len(tok_out), len(acc_out)
    nin = nt + nb + nc + (1 if into is not None else 0)

    def body(*refs):
        tr, br, cr = refs[:nt], refs[nt:nt + nb], refs[nt + nb:nt + nb + nc]
        orf, arf = refs[nin:nin + no], refs[nin + no:]
        touts, aouts = fn([r[0] for r in tr], [r[0] for r in br], [r[...] for r in cr])
        for r, v in zip(orf, touts):
            r[0] = v.astype(r.dtype)
        s = pl.program_id(1)
        for r, v in zip(arf, aouts):
            @pl.when(s == 0)
            def _(r=r):
                r[...] = jnp.zeros(r.shape, r.dtype)
            r[0] += v.astype(f32)

    in_specs = [pl.BlockSpec((1, ts, w), lambda b, s, cb=cb: (b, s, cb)) for (_, w, cb) in tok]
    in_specs += [pl.BlockSpec((1,) + a.shape[1:], lambda b, s: (b, 0, 0)) for a in bat]
    in_specs += [pl.BlockSpec(a.shape, lambda b, s, nd=a.ndim: (0,) * nd) for a in con]
    out_specs = [pl.BlockSpec((1, ts, w), lambda b, s: (b, s, 0)) for (w, _) in tok_out]
    out_specs += [pl.BlockSpec((1,) + shp, lambda b, s, nd=len(shp): (b,) + (0,) * nd) for shp in acc_out]
    out_shape = [jax.ShapeDtypeStruct((B, S, w), dt) for (w, dt) in tok_out]
    out_shape += [jax.ShapeDtypeStruct((B,) + shp, f32) for shp in acc_out]
    extra, aliases = [], {}
    if into is not None:
        buf, cb = into
        assert buf.dtype == tok_out[0][1]
        in_specs.append(pl.BlockSpec(memory_space=pl.ANY))
        out_specs[0] = pl.BlockSpec((1, ts, tok_out[0][0]), lambda b, s: (b, s, cb))
        out_shape[0] = jax.ShapeDtypeStruct(buf.shape, buf.dtype)
        extra, aliases = [buf], {nin - 1: 0}
    return pl.pallas_call(
        body, name=name, grid=(B, S // ts), in_specs=in_specs, out_specs=out_specs, out_shape=out_shape,
        input_output_aliases=aliases, compiler_params=_cp(("parallel", "arbitrary")),
    )(*[t[0] for t in tok], *bat, *con, *extra)


def rowcall_fwd(name, f, tok, bat, con, tok_out, ts=256):
    def fn(t, b, c):
        return f([v.astype(f32) for v in t], b, c), []
    return rowcall(name, fn, tok, bat, con, tok_out, [], ts)


def rowcall_bwd(name, f, tok, bat, con, cts, tok_grads, add=None, ts=256, join_first=1, into=None):
    nt, ncts = len(tok), len(cts)

    def fn(t, b, c):
        prim = [v.astype(f32) for v in t[:nt]]
        ct = [v.astype(f32) for v in t[nt:nt + ncts]]
        _, vjp = jax.vjp(lambda tt, bb, cc: f(tt, bb, cc), prim, b, c)
        dt, db, dc = vjp(ct)
        touts = [dt[i] for i, _ in tok_grads]
        if add is not None:
            touts[0] = touts[0] + t[nt + ncts].astype(f32)
        if join_first > 1:
            touts = [jnp.concatenate(touts[:join_first], axis=1)] + touts[join_first:]
        return touts, list(db) + list(dc)

    all_tok = list(tok) + list(cts) + ([add] if add is not None else [])
    tok_out = [(tok[i][1], dt) for i, dt in tok_grads]
    if join_first > 1:
        tok_out = [(sum(w for w, _ in tok_out[:join_first]), tok_out[0][1])] + tok_out[join_first:]
    acc_out = [tuple(a.shape[1:]) for a in bat] + [tuple(a.shape) for a in con]
    return rowcall(name, fn, all_tok, bat, con, tok_out, acc_out, ts, into)


def _rms(y, w):
    return y * lax.rsqrt(jnp.mean(y * y, axis=-1, keepdims=True) + RMS_EPS) * w


def f_rms_mod(t, b, c):
    return [_rms(t[0], c[0]) * (1.0 + b[0]) + b[1]]


def f_post_pre(t, b, c):
    h1 = t[0] + b[0] * _rms(t[1], c[0])
    return [h1, _rms(h1, c[1]) * (1.0 + b[1]) + b[2]]


def f_merge(t, b, c):
    ga, gd, ya, yd = t
    return [jax.nn.sigmoid(ga) * ya + jax.nn.sigmoid(gd) * yd]


def f_dnout(t, b, c):
    o, z = t
    outs = []
    for h in range(DNH):
        sl = slice(h * DND, (h + 1) * DND)
        zh = z[:, sl]
        outs.append(_rms(o[:, sl], c[0]) * (zh * jax.nn.sigmoid(zh)))
    return [jnp.concatenate(outs, axis=1)]


def _softplus(x):
    return jnp.maximum(x, 0.0) + jnp.log(1.0 + jnp.exp(-jnp.abs(x)))


def f_gate(t, b, c):
    ba = t[0]
    a_log, dt_bias = c
    lane = lax.broadcasted_iota(jnp.int32, ba.shape, 1)
    beta = jax.nn.sigmoid(ba)
    g = -jnp.exp(a_log) * _softplus(ba + dt_bias)
    return [jnp.where(lane < DNH, beta, jnp.where(lane < 2 * DNH, g, 0.0))]


def _bucket_table():
    qi = np.arange(WIN)[:, None]
    kj = np.arange(2 * WIN)[None, :]
    dist = np.maximum(WIN + qi - kj, 0)
    max_exact = NBUCK // 2
    scaled = np.log(np.maximum(dist, 1).astype(np.float64) / max_exact) / math.log(MAXDIST / max_exact)
    large = np.minimum(max_exact + (scaled * (NBUCK - max_exact)).astype(np.int32), NBUCK - 1)
    return np.where(dist < max_exact, dist, large).astype(np.int32)


def _attn_mask(n):
    qi = lax.broadcasted_iota(jnp.int32, (WIN, 2 * WIN), 0)
    kj = lax.broadcasted_iota(jnp.int32, (WIN, 2 * WIN), 1)
    dist = WIN + qi - kj
    return (dist >= 0) & (dist < WIN) & ((kj >= WIN) | (n > 0))


def _swap_halves(x):
    return pltpu.roll(x, HD, axis=x.ndim - 1)


@jax.custom_vjp
def _swap_halves_vjp(x):
    return _swap_halves(x)


_swap_halves_vjp.defvjp(lambda x: (_swap_halves(x), None), lambda _, g: (_swap_halves(g),))


def _attn_block(q, kp, kc, vp, vc, bias, sinks, mask, differentiated):
    dot = _bdot_bf16_vjp if differentiated else _bdot_bf16
    swap = _swap_halves_vjp if differentiated else _swap_halves
    B, grp = q.shape[0], HQ // HKV
    upper = lax.broadcasted_iota(jnp.int32, (2 * WIN, LANE), 1) >= HD

    def placed(natural, swapped, j, half):
        keep = upper if half == 1 else ~upper
        return jnp.where(keep, natural if j == half else swapped, 0.0)

    qh, ks, vs = [], [], []
    for b in range(B):
        kb, vb = jnp.concatenate([kp[b], kc[b]], axis=0), jnp.concatenate([vp[b], vc[b]], axis=0)
        kb_sw, vb_sw = swap(kb), swap(vb)
        for h in range(HQ):
            qh.append(q[b, :, (h // 2) * LANE:(h // 2 + 1) * LANE])
            ks.append(placed(kb, kb_sw, h // grp, h % 2))
            vs.append(placed(vb, vb_sw, h // grp, h % 2))
    s = dot(_stack(qh), _stack(ks), 2, 2).reshape(B, HQ, WIN, 2 * WIN) * (HD ** -0.5)
    s = jnp.where(mask, s + bias, NEG_INF)
    m = jnp.maximum(jnp.max(s, axis=-1, keepdims=True), sinks)
    p = jnp.exp(s - m)
    probs = p / (jnp.sum(p, axis=-1, keepdims=True) + jnp.exp(sinks - m))
    o = dot(probs.reshape(B * HQ, WIN, 2 * WIN), _stack(vs), 2, 1)
    return _stack([jnp.concatenate([o[b * HQ + 2 * i] + o[b * HQ + 2 * i + 1] for i in range(HQ // 2)], axis=1) for b in range(B)])


def _attn_specs(B, NB):
    last = NB - 1
    return [
        pl.BlockSpec((B, WIN, HQ * HD), lambda n: (0, jnp.minimum(n, last), CB_AQ // 4)),
        pl.BlockSpec((B, WIN, LANE), lambda n: (0, jnp.clip(n - 1, 0, last), CB_AK)),
        pl.BlockSpec((B, WIN, LANE), lambda n: (0, jnp.minimum(n, last), CB_AK)),
        pl.BlockSpec((B, WIN, LANE), lambda n: (0, jnp.clip(n - 1, 0, last), CB_AV)),
        pl.BlockSpec((B, WIN, LANE), lambda n: (0, jnp.minimum(n, last), CB_AV)),
        pl.BlockSpec((HQ, WIN, 2 * WIN), lambda n: (0, 0, 0)),
        pl.BlockSpec((HQ, 1, 1), lambda n: (0, 0, 0)),
    ]


def attn_fwd(proj, bias, sinks, comm):
    B, S, _ = proj.shape
    NB = S // WIN

    def body(q, kp, kc, vp, vc, bias_ref, sink_ref, o_ref):
        mask = _attn_mask(pl.program_id(0))
        o = _attn_block(*[r[...].astype(f32) for r in (q, kp, kc, vp, vc)], bias_ref[...], sink_ref[...], mask, False)
        o_ref[...] = o.astype(o_ref.dtype)

    at = lambda n: lambda: pl.program_id(0) == n
    return pl.pallas_call(
        _ride(body, 7, 1, 0, comm, at(0), at((3 * NB) // 4), at(NB - 1)), name="attn_fwd", grid=(NB,),
        in_specs=_attn_specs(B, NB) + comm.specs,
        out_specs=[pl.BlockSpec((B, WIN, HQ * HD), lambda n: (0, n, 0))] + comm.specs,
        out_shape=[jax.ShapeDtypeStruct((B, S, HQ * HD), bf16)] + comm.out_shape, scratch_shapes=comm.scratch,
        compiler_params=_cp(("arbitrary",)),
    )(proj, proj, proj, proj, proj, bias, sinks, *comm.arrs)


def attn_bwd(proj, bias, sinks, dy, dproj, comm):
    B, S, _ = proj.shape
    NB = S // WIN
    last = NB - 1

    def body(q, kp, kc, vp, vc, bias_ref, sink_ref, dy_ref, _, dq_ref, dk_ref, dv_ref, dbias_ref, dsink_ref, kcar, vcar):
        n = pl.program_id(0)

        @pl.when(n == 0)
        def _():
            dbias_ref[...] = jnp.zeros(dbias_ref.shape, f32)
            dsink_ref[...] = jnp.zeros(dsink_ref.shape, f32)
            kcar[...] = jnp.zeros(kcar.shape, f32)
            vcar[...] = jnp.zeros(vcar.shape, f32)

        @pl.when(n < NB)
        def _():
            mask = _attn_mask(n)
            _, vjp = jax.vjp(lambda *a: _attn_block(*a, mask, True), *[r[...].astype(f32) for r in (q, kp, kc, vp, vc)],
                             bias_ref[...], sink_ref[...])
            dq, dkp, dkc, dvp, dvc, dbias, dsink = vjp(dy_ref[...].astype(f32))
            dq_ref[...] = dq.astype(dq_ref.dtype)
            dbias_ref[...] += dbias
            dsink_ref[...] += dsink
            dk_ref[...] = (kcar[...] + dkp).astype(dk_ref.dtype)
            dv_ref[...] = (vcar[...] + dvp).astype(dv_ref.dtype)
            kcar[...] = dkc
            vcar[...] = dvc

        @pl.when(n == NB)
        def _():
            dk_ref[...] = kcar[...].astype(dk_ref.dtype)
            dv_ref[...] = vcar[...].astype(dv_ref.dtype)

    in_specs = _attn_specs(B, NB) + [pl.BlockSpec((B, WIN, HQ * HD), lambda n: (0, jnp.minimum(n, last), 0)),
                                     pl.BlockSpec(memory_space=pl.ANY)]
    kv_out = pl.BlockSpec((B, WIN, LANE), lambda n: (0, jnp.maximum(n - 1, 0), 0))
    at = lambda n: lambda: pl.program_id(0) == n
    return pl.pallas_call(
        _ride(body, 9, 5, 2, comm, at(0), at(NB), at(NB)), name="attn_bwd", grid=(NB + 1,),
        in_specs=in_specs + comm.specs, input_output_aliases={8: 0},
        out_specs=[pl.BlockSpec((B, WIN, HQ * HD), lambda n: (0, jnp.minimum(n, last), CB_AQ // 4)), kv_out, kv_out,
                   pl.BlockSpec((HQ, WIN, 2 * WIN), lambda n: (0, 0, 0)), pl.BlockSpec((HQ, 1, 1), lambda n: (0, 0, 0))] + comm.specs,
        out_shape=[jax.ShapeDtypeStruct(dproj.shape, dproj.dtype), jax.ShapeDtypeStruct((B, S, LANE), bf16),
                   jax.ShapeDtypeStruct((B, S, LANE), bf16), jax.ShapeDtypeStruct((HQ, WIN, 2 * WIN), f32),
                   jax.ShapeDtypeStruct((HQ, 1, 1), f32)] + comm.out_shape,
        scratch_shapes=[pltpu.VMEM((B, WIN, LANE), f32), pltpu.VMEM((B, WIN, LANE), f32)] + comm.scratch,
        compiler_params=_cp(("arbitrary",)),
    )(proj, proj, proj, proj, proj, bias, sinks, dy, dproj, *comm.arrs)


DN_ROWS, FFN_ROWS = 256, 32


def _stage_rows(dst, value):
    dst[0:8] = jnp.zeros((8, LANE), f32)
    dst[8:8 + value.shape[0]] = value


def _conv_rows(xs, w, width, r, rows):
    wins = [xs[pl.ds(r + 8 - (width - 1) + j, rows), :] for j in range(width)]
    out = w[0:1] * wins[0]
    for j in range(1, width):
        out = out + w[j:j + 1] * wins[j]
    return out, wins


def _fold8(v):
    return jnp.sum(v.reshape(v.shape[0] // 8, 8, LANE), axis=0)


def _conv_rows_t(ds, w, width, r, rows):
    out = w[0:1] * ds[pl.ds(r + width - 1, rows), :]
    for j in range(1, width):
        out = out + w[j:j + 1] * ds[pl.ds(r + width - 1 - j, rows), :]
    return out


def _dn_outblk(i):
    return (i % DNH) * 3 + i // DNH


def _dn_act(c, isqk):
    sg = jax.nn.sigmoid(c)
    y = c * sg
    n = lax.rsqrt(jnp.sum(y * y, axis=-1, keepdims=True) + L2_EPS)
    return jnp.where(isqk, y * n, y), sg, n


def dnconv_fwd(proj, conv_w):
    B, S, _ = proj.shape
    rows = min(DN_ROWS, S)

    def body(x_ref, w_ref, o_ref, xs):
        isqk = pl.program_id(0) < 2 * DNH
        _stage_rows(xs, x_ref[0].astype(f32))
        w = w_ref[...]
        for r in range(0, S, rows):
            c, _ = _conv_rows(xs, w, DNK, r, rows)
            o_ref[0, pl.ds(r, rows), :] = _dn_act(c, isqk)[0]

    return pl.pallas_call(
        body, name="dnconv_fwd", grid=(3 * DNH, B),
        in_specs=[pl.BlockSpec((1, S, LANE), lambda i, b: (b, 0, CB_DQKV + i)), pl.BlockSpec((DNK, LANE), lambda i, b: (0, i))],
        out_specs=pl.BlockSpec((1, S, LANE), lambda i, b: (b, 0, _dn_outblk(i))),
        out_shape=jax.ShapeDtypeStruct((B, S, 3 * DNH * DND), f32), scratch_shapes=[pltpu.VMEM((S + 8, LANE), f32)],
        compiler_params=_cp(("parallel", "parallel")),
    )(proj, conv_w)


def dnconv_bwd(proj, conv_w, dqkvn, dproj):
    B, S, _ = proj.shape
    rows = min(DN_ROWS, S)

    def body(x_ref, w_ref, dy_ref, _, dx_ref, dw_ref, xs, ds):
        isqk = pl.program_id(0) < 2 * DNH
        _stage_rows(xs, x_ref[0].astype(f32))
        w = w_ref[...]
        dw = [jnp.zeros((8, LANE), f32) for _ in range(DNK)]
        for r in range(0, S, rows):
            c, wins = _conv_rows(xs, w, DNK, r, rows)
            out, sg, n = _dn_act(c, isqk)
            dout = dy_ref[0, pl.ds(r, rows), :]
            dy = jnp.where(isqk, n * (dout - out * jnp.sum(dout * out, axis=-1, keepdims=True)), dout)
            dc = dy * (sg * (1.0 + c * (1.0 - sg)))
            ds[pl.ds(r, rows), :] = dc
            for j in range(DNK):
                dw[j] = dw[j] + _fold8(dc * wins[j])
        ds[S:S + 8] = jnp.zeros((8, LANE), f32)
        for r in range(0, S, rows):
            dx_ref[0, pl.ds(r, rows), :] = _conv_rows_t(ds, w, DNK, r, rows).astype(dx_ref.dtype)

        @pl.when(pl.program_id(1) == 0)
        def _():
            dw_ref[...] = jnp.zeros(dw_ref.shape, f32)
        dw_ref[...] += jnp.concatenate([jnp.sum(d, axis=0, keepdims=True) for d in dw], axis=0)

    return pl.pallas_call(
        body, name="dnconv_bwd", grid=(3 * DNH, B),
        in_specs=[pl.BlockSpec((1, S, LANE), lambda i, b: (b, 0, CB_DQKV + i)), pl.BlockSpec((DNK, LANE), lambda i, b: (0, i)),
                  pl.BlockSpec((1, S, LANE), lambda i, b: (b, 0, _dn_outblk(i))), pl.BlockSpec(memory_space=pl.ANY)],
        out_specs=[pl.BlockSpec((1, S, LANE), lambda i, b: (b, 0, CB_DQKV + i)), pl.BlockSpec((DNK, LANE), lambda i, b: (0, i))],
        out_shape=[jax.ShapeDtypeStruct(dproj.shape, dproj.dtype), jax.ShapeDtypeStruct((DNK, 3 * DNH * DND), f32)],
        scratch_shapes=[pltpu.VMEM((S + 8, LANE), f32), pltpu.VMEM((S + 8, LANE), f32)],
        input_output_aliases={3: 0}, compiler_params=_cp(("parallel", "arbitrary")),
    )(proj, conv_w, dqkvn, dproj)


def _bdot(a, b, ca, cb, precision=HI):
    return lax.dot_general(a, b, (((ca,), (cb,)), ((0,), (0,))), preferred_element_type=f32, precision=precision)


def _bdot_bf16(a, b, ca, cb):
    return _bdot(a.astype(bf16), b.astype(bf16), ca, cb, None)


@functools.partial(jax.custom_vjp, nondiff_argnums=(2, 3))
def _bdot_bf16_vjp(a, b, ca, cb):
    return _bdot_bf16(a, b, ca, cb)


def _bdot_bf16_fwd(a, b, ca, cb):
    return _bdot_bf16(a, b, ca, cb), (a, b)


def _bdot_bf16_bwd(ca, cb, res, g):
    a, b = res
    fa, fb = 3 - ca, 3 - cb
    da = _bdot_bf16(g, b, 2, fb) if ca == 2 else _bdot_bf16(b, g, fb, 2)
    db = _bdot_bf16(a, g, fa, 1) if cb == 1 else _bdot_bf16(g, a, 1, fa)
    return da, db


_bdot_bf16_vjp.defvjp(_bdot_bf16_fwd, _bdot_bf16_bwd)


def _neumann_inverse(low):
    n = low.shape[-1]
    eye = (lax.broadcasted_iota(jnp.int32, (n, n), 0) == lax.broadcasted_iota(jnp.int32, (n, n), 1)).astype(f32)
    p = -low
    x = eye[None] + p
    for _ in range(5):
        p = _bdot_bf16(p, p, 2, 1)
        x = x + _bdot_bf16(x, p, 2, 1)
    return x


@jax.custom_vjp
def _unit_lower_inverse(low):
    return _neumann_inverse(low)


def _uli_fwd(low):
    t = _neumann_inverse(low)
    return t, t


def _uli_bwd(t, dt):
    return (-_bdot_bf16(_bdot_bf16(t, dt, 1, 1), t, 2, 2),)


_unit_lower_inverse.defvjp(_uli_fwd, _uli_bwd)


def _stack(xs):
    return jnp.concatenate([x[None] for x in xs], axis=0)


DELTA_CHUNKS = 2


def _delta_chunks(qkv, bg, state, differentiated):
    inverse = _unit_lower_inverse if differentiated else _neumann_inverse
    lo = _bdot_bf16_vjp if differentiated else _bdot_bf16
    B, n = qkv.shape[0], qkv.shape[1] // CH
    G = B * DNH
    N = n * G
    triples = [(i, b, h) for i in range(n) for b in range(B) for h in range(DNH)]
    col = lambda i, b, h, kind: qkv[b, i * CH:(i + 1) * CH, (3 * h + kind) * DND:(3 * h + kind + 1) * DND]
    q, k, v = [_stack([col(i, b, h, kind) for i, b, h in triples]) for kind in range(3)]
    lane = lax.broadcasted_iota(jnp.int32, (CH, LANE), 1)
    pick = lambda i, b, l: jnp.sum(jnp.where(lane == l, bg[b, i * CH:(i + 1) * CH], 0.0), axis=1, keepdims=True)
    beta = _stack([pick(i, b, h) for i, b, h in triples])
    g = _stack([pick(i, b, h + DNH) for i, b, h in triples])
    ri = lax.broadcasted_iota(jnp.int32, (CH, CH), 0)
    ci = lax.broadcasted_iota(jnp.int32, (CH, CH), 1)
    incl, strict = (ri >= ci)[None], (ri > ci)[None]
    gc = _bdot(jnp.broadcast_to(incl.astype(f32), (N, CH, CH)), jnp.broadcast_to(g, (N, CH, LANE)), 2, 1, MID)
    e0 = jnp.broadcast_to((lane == 0).astype(f32)[None], (N, CH, LANE))
    gc_row = _bdot(e0, gc, 2, 2, MID)
    diff = gc[:, :, :CH] - gc_row
    decay = jnp.where(incl, jnp.exp(jnp.where(incl, diff, 0.0)), 0.0)
    qs = q * (DND ** -0.5)
    kb, vb = k * beta, v * beta
    eg = jnp.exp(gc)
    with_k = lo(jnp.concatenate([kb, qs], axis=1), k, 2, 2)
    low = jnp.where(strict, with_k[:, :CH] * decay, 0.0)
    intra = jnp.where(incl, with_k[:, CH:] * decay, 0.0)
    tinv = inverse(low)
    solved = lo(tinv, jnp.concatenate([vb, kb * eg], axis=2), 2, 1)
    gl = gc[:, CH - 1:CH, :]
    k_tail = k * jnp.exp(gl - gc)
    to_state = jnp.concatenate([solved[:, :, DND:], qs * eg], axis=1)
    decay_all = jnp.exp(gl)
    outs = []
    for i in range(n):
        sl = slice(i * G, (i + 1) * G)
        with_state = lo(to_state[sl], state, 2, 1)
        v_new = solved[sl, :, :DND] - with_state[:, :CH]
        outs.append(with_state[:, CH:] + lo(intra[sl], v_new, 2, 1))
        state = state * decay_all[sl] + lo(k_tail[sl], v_new, 1, 1)
    return outs, state


def delta_fwd(qkvn, bg, comm):
    B, S, _ = qkvn.shape
    n = DELTA_CHUNKS if (S // CH) % DELTA_CHUNKS == 0 else 1
    steps, G, rows = S // (n * CH), B * DNH, n * CH

    def body(qkv_ref, bg_ref, o_ref, st_ref, state):
        @pl.when(pl.program_id(0) == 0)
        def _():
            state[...] = jnp.zeros(state.shape, f32)
        s0 = state[...]
        st_ref[0] = s0
        outs, s1 = _delta_chunks(qkv_ref[...], bg_ref[...], s0, False)
        for i, o in enumerate(outs):
            for b in range(B):
                for h in range(DNH):
                    o_ref[b, i * CH:(i + 1) * CH, h * DND:(h + 1) * DND] = o[b * DNH + h]
        state[...] = s1

    at = lambda c: lambda: pl.program_id(0) == c
    return pl.pallas_call(
        _ride(body, 2, 2, 1, comm, at(0), at((7 * steps) // 8), at(steps - 1)), name="delta_fwd", grid=(steps,),
        in_specs=[pl.BlockSpec((B, rows, 3 * DNH * DND), lambda c: (0, c, 0)), pl.BlockSpec((B, rows, LANE), lambda c: (0, c, 0))] + comm.specs,
        out_specs=[pl.BlockSpec((B, rows, DNH * DND), lambda c: (0, c, 0)), pl.BlockSpec((1, G, DND, DND), lambda c: (c, 0, 0, 0))] + comm.specs,
        out_shape=[jax.ShapeDtypeStruct((B, S, DNH * DND), f32), jax.ShapeDtypeStruct((steps, G, DND, DND), f32)] + comm.out_shape,
        scratch_shapes=[pltpu.VMEM((G, DND, DND), f32)] + comm.scratch, compiler_params=_cp(("arbitrary",)),
    )(qkvn, bg, *comm.arrs)


def delta_bwd(qkvn, bg, states, do, comm):
    B, S, _ = qkvn.shape
    steps, G = states.shape[0], B * DNH
    rows = S // steps
    n = rows // CH

    def body(qkv_ref, bg_ref, st_ref, do_ref, dqkv_ref, dbg_ref, dstate):
        @pl.when(pl.program_id(0) == 0)
        def _():
            dstate[...] = jnp.zeros(dstate.shape, f32)
        _, vjp = jax.vjp(lambda a, g, s: _delta_chunks(a, g, s, True), qkv_ref[...], bg_ref[...], st_ref[0])
        do = [_stack([do_ref[b, i * CH:(i + 1) * CH, h * DND:(h + 1) * DND] for b in range(B) for h in range(DNH)]) for i in range(n)]
        dqkv, dbg, ds = vjp((do, dstate[...]))
        dqkv_ref[...] = dqkv
        dbg_ref[...] = dbg
        dstate[...] = ds

    rev = lambda c: steps - 1 - c
    at = lambda c: lambda: pl.program_id(0) == c
    return pl.pallas_call(
        _ride(body, 4, 2, 1, comm, at(0), at(steps - 1), at(steps - 1)), name="delta_bwd", grid=(steps,),
        in_specs=[pl.BlockSpec((B, rows, 3 * DNH * DND), lambda c: (0, rev(c), 0)), pl.BlockSpec((B, rows, LANE), lambda c: (0, rev(c), 0)),
                  pl.BlockSpec((1, G, DND, DND), lambda c: (rev(c), 0, 0, 0)),
                  pl.BlockSpec((B, rows, DNH * DND), lambda c: (0, rev(c), 0))] + comm.specs,
        out_specs=[pl.BlockSpec((B, rows, 3 * DNH * DND), lambda c: (0, rev(c), 0)),
                   pl.BlockSpec((B, rows, LANE), lambda c: (0, rev(c), 0))] + comm.specs,
        out_shape=[jax.ShapeDtypeStruct((B, S, 3 * DNH * DND), f32), jax.ShapeDtypeStruct((B, S, LANE), f32)] + comm.out_shape,
        scratch_shapes=[pltpu.VMEM((G, DND, DND), f32)] + comm.scratch, compiler_params=_cp(("arbitrary",)),
    )(qkvn, bg, states, do, *comm.arrs)


GELU_C0, GELU_C1 = math.sqrt(2.0 / math.pi), 0.044715


def _ffn_specs(S):
    nblk = DFF // LANE
    return [pl.BlockSpec((1, S, LANE), lambda i, b: (b, 0, i)), pl.BlockSpec((1, S, LANE), lambda i, b: (b, 0, nblk + i)),
            pl.BlockSpec((FK, LANE), lambda i, b: (0, i)), pl.BlockSpec((FK, LANE), lambda i, b: (0, nblk + i))]


def ffnconv_fwd(up, conv_w):
    B, S, _ = up.shape
    rows = min(FFN_ROWS, S)

    def body(g_ref, v_ref, gw_ref, vw_ref, o_ref, xg, xv):
        _stage_rows(xg, g_ref[0].astype(f32))
        _stage_rows(xv, v_ref[0].astype(f32))
        gw, vw = gw_ref[...], vw_ref[...]
        for r in range(0, S, rows):
            g, _ = _conv_rows(xg, gw, FK, r, rows)
            v, _ = _conv_rows(xv, vw, FK, r, rows)
            t = jnp.tanh(GELU_C0 * (g * (1.0 + GELU_C1 * (g * g))))
            o_ref[0, pl.ds(r, rows), :] = (0.5 * g * (1.0 + t) * v).astype(o_ref.dtype)

    return pl.pallas_call(
        body, name="ffnconv_fwd", grid=(DFF // LANE, B), in_specs=_ffn_specs(S),
        out_specs=pl.BlockSpec((1, S, LANE), lambda i, b: (b, 0, i)), out_shape=jax.ShapeDtypeStruct((B, S, DFF), bf16),
        scratch_shapes=[pltpu.VMEM((S + 8, LANE), f32)] * 2, compiler_params=_cp(("parallel", "parallel")),
    )(up, up, conv_w, conv_w)


def ffnconv_bwd(up, conv_w, dact, comm):
    B, S, _ = up.shape
    rows = min(FFN_ROWS, S)

    def body(g_ref, v_ref, gw_ref, vw_ref, dy_ref, dx_ref, dw_ref, xg, xv, dg, dv):
        _stage_rows(xg, g_ref[0].astype(f32))
        _stage_rows(xv, v_ref[0].astype(f32))
        gw, vw = gw_ref[...], vw_ref[...]
        dgw = [jnp.zeros((8, LANE), f32) for _ in range(FK)]
        dvw = [jnp.zeros((8, LANE), f32) for _ in range(FK)]
        for r in range(0, S, rows):
            g, gwins = _conv_rows(xg, gw, FK, r, rows)
            v, vwins = _conv_rows(xv, vw, FK, r, rows)
            g2 = g * g
            t = jnp.tanh(GELU_C0 * (g * (1.0 + GELU_C1 * g2)))
            half = 0.5 * (1.0 + t)
            dgelu = half + (0.5 * GELU_C0) * g * (1.0 - t * t) * (1.0 + (3.0 * GELU_C1) * g2)
            dy = dy_ref[0, pl.ds(r, rows), :].astype(f32)
            dvc = dy * (g * half)
            dgc = dy * v * dgelu
            dg[pl.ds(r, rows), :] = dgc
            dv[pl.ds(r, rows), :] = dvc
            for j in range(FK):
                dgw[j] = dgw[j] + _fold8(dgc * gwins[j])
                dvw[j] = dvw[j] + _fold8(dvc * vwins[j])
        dg[S:S + 8] = jnp.zeros((8, LANE), f32)
        dv[S:S + 8] = jnp.zeros((8, LANE), f32)
        for r in range(0, S, rows):
            dx_ref[0, 0, pl.ds(r, rows), :] = _conv_rows_t(dg, gw, FK, r, rows).astype(dx_ref.dtype)
            dx_ref[1, 0, pl.ds(r, rows), :] = _conv_rows_t(dv, vw, FK, r, rows).astype(dx_ref.dtype)

        @pl.when(pl.program_id(1) == 0)
        def _():
            dw_ref[...] = jnp.zeros(dw_ref.shape, f32)
        dw_ref[0] += jnp.concatenate([jnp.sum(d, axis=0, keepdims=True) for d in dgw], axis=0)
        dw_ref[1] += jnp.concatenate([jnp.sum(d, axis=0, keepdims=True) for d in dvw], axis=0)

    nblk = DFF // LANE
    at = lambda i, b: lambda: (pl.program_id(0) == i) & (pl.program_id(1) == b)
    return pl.pallas_call(
        _ride(body, 5, 2, 4, comm, at(0, 0), at(nblk - 1, B - 1), at(nblk - 1, B - 1)), name="ffnconv_bwd", grid=(nblk, B),
        in_specs=_ffn_specs(S) + [pl.BlockSpec((1, S, LANE), lambda i, b: (b, 0, i))] + comm.specs,
        out_specs=[pl.BlockSpec((2, 1, S, LANE), lambda i, b: (0, b, 0, i)),
                   pl.BlockSpec((2, FK, LANE), lambda i, b: (0, 0, i))] + comm.specs,
        out_shape=[jax.ShapeDtypeStruct((2, B, S, DFF), bf16), jax.ShapeDtypeStruct((2, FK, DFF), f32)] + comm.out_shape,
        scratch_shapes=[pltpu.VMEM((S + 8, LANE), f32)] * 4 + comm.scratch, compiler_params=_cp(("arbitrary", "arbitrary")),
    )(up, up, conv_w, conv_w, dact, *comm.arrs)


def ada_fwd(c_all, ada_w, ada_b):
    def body(c_ref, w_ref, b_ref, o_ref):
        c = c_ref[...]
        act = (c * jax.nn.sigmoid(c)).astype(bf16)
        o_ref[...] = jnp.dot(act, w_ref[...].astype(bf16), preferred_element_type=f32) + b_ref[...]

    return pl.pallas_call(body, name="ada_fwd", out_shape=jax.ShapeDtypeStruct((c_all.shape[0], ada_w.shape[1]), f32),
                          compiler_params=pltpu.CompilerParams(vmem_limit_bytes=VMEM_LIMIT))(c_all, ada_w, ada_b)


def ada_bwd(c_all, dmod):
    def body(c_ref, d_ref, o_ref):
        c = c_ref[...]
        act = (c * jax.nn.sigmoid(c)).astype(bf16)
        o_ref[...] = lax.dot_general(act, d_ref[...].astype(bf16), (((0,), (0,)), ((), ())), preferred_element_type=f32)

    return pl.pallas_call(body, name="ada_bwd", out_shape=jax.ShapeDtypeStruct((c_all.shape[1], dmod.shape[1]), f32),
                          compiler_params=pltpu.CompilerParams(vmem_limit_bytes=VMEM_LIMIT))(c_all, dmod)


def loss_head(h1, y2, target, g2, w):
    def fn(t, b, c):
        h, y, tg = [v.astype(f32) for v in t]

        def loss_fn(h, y, g, w):
            e = h + g * _rms(y, w) - tg
            return 0.5 * jnp.sum(jnp.mean(e * e, axis=-1))

        loss, grads = jax.value_and_grad(loss_fn, argnums=(0, 1, 2, 3))(h, y, b[0], c[0])
        return [grads[0], grads[1]], [grads[2], grads[3], jnp.full((1, LANE), loss, f32)]

    return rowcall("loss_head", fn, [(h1, D, 0), (y2, D, 0), (target, D, 0)], [g2], [w], [(D, f32), (D, bf16)],
                   [(1, D), (1, D), (1, LANE)])


def adamw(w, gparts, m, v, name):
    R, C = w.shape
    P = gparts.shape[0]
    budget = 2 * 1024 * 1024
    tr, tc = R, C
    if R * C * 4 > budget and R % 8 == 0:
        tr = max(t for t in range(8, R + 1, 8) if R % t == 0 and t * C * 4 <= budget)
    elif R * C * 4 > budget:
        tc = max(t for t in range(LANE, C + 1, LANE) if C % t == 0 and R * t * 4 <= budget)

    def body(w_ref, g_ref, m_ref, v_ref, go, do, mo, vo):
        g = g_ref[0].astype(f32)
        for p in range(1, P):
            g = g + g_ref[p].astype(f32)
        m2 = B1 * m_ref[...] + (1.0 - B1) * g
        v2 = B2 * v_ref[...] + (1.0 - B2) * jnp.square(g)
        m_hat = m2 * (1.0 / (1.0 - B1 ** STEP))
        v_hat = v2 * (1.0 / (1.0 - B2 ** STEP))
        go[...] = g
        do[...] = -LR * (m_hat / (jnp.sqrt(v_hat) + EPS) + WD * w_ref[...])
        mo[...] = m2
        vo[...] = v2

    blk = pl.BlockSpec((tr, tc), lambda i, j: (i, j))
    return pl.pallas_call(
        body, name=name, grid=(R // tr, C // tc), in_specs=[blk, pl.BlockSpec((P, tr, tc), lambda i, j: (0, i, j)), blk, blk],
        out_specs=[blk] * 4, out_shape=[jax.ShapeDtypeStruct((R, C), f32)] * 4, compiler_params=_cp(("parallel", "parallel")),
    )(w, gparts, m, v)


def _pack_w_in(wt):
    aq, ak, av, dqkv, dz, dbeta, da, ga, gd = jnp.split(wt, np.cumsum(IN_SPLITS)[:-1].tolist(), axis=0)
    ba = jnp.pad(jnp.concatenate([dbeta, da], axis=0), ((0, LANE - 2 * DNH), (0, 0)))
    return jnp.concatenate([ga, gd, aq, dqkv, dz, ak, av, ba], axis=0)


def _unpack_w_in(p):
    row = lambda cb, n: p[cb * LANE: cb * LANE + n]
    ba = row(CB_BA, 2 * DNH)
    return jnp.concatenate([row(CB_AQ, HQ * HD), row(CB_AK, HKV * HD), row(CB_AV, HKV * HD), row(CB_DQKV, 3 * DNH * DND),
                            row(CB_DZ, DNH * DND), ba[:DNH], ba[DNH:], row(CB_GA, D), row(CB_GD, D)], axis=0)


def _cols_gathered(g):
    return g.transpose(1, 0, 2).reshape(g.shape[1], NDEV * g.shape[2])


def _cols_split(w):
    r = w.shape[0]
    return w.reshape(r, NDEV, w.shape[1] // NDEV).transpose(1, 0, 2)


def kernel(x, c, ada_w, ada_b, norm_mix_pre, norm_mix_post, norm_ffn_pre, norm_ffn_post, w_in, dn_conv_w, dn_a_log, dn_dt_bias, dn_norm_w, attn_sinks, rel_bias, w_attn_branch, w_dn_branch, w_out, ffn_w_up, ffn_conv_w, ffn_w_down, loss_target, m_ada_w, m_ada_b, m_norm_mix_pre, m_norm_mix_post, m_norm_ffn_pre, m_norm_ffn_post, m_w_in, m_dn_conv_w, m_dn_a_log, m_dn_dt_bias, m_dn_norm_w, m_attn_sinks, m_rel_bias, m_w_attn_branch, m_w_dn_branch, m_w_out, m_ffn_w_up, m_ffn_conv_w, m_ffn_w_down, v_ada_w, v_ada_b, v_norm_mix_pre, v_norm_mix_post, v_norm_ffn_pre, v_norm_ffn_post, v_w_in, v_dn_conv_w, v_dn_a_log, v_dn_dt_bias, v_dn_norm_w, v_attn_sinks, v_rel_bias, v_w_attn_branch, v_w_dn_branch, v_w_out, v_ffn_w_up, v_ffn_conv_w, v_ffn_w_down):
    B, S, _ = x.shape
    T = B * S
    me = 4 * lax.axis_index("x") + 2 * lax.axis_index("y") + lax.axis_index("c")
    big = dict(w_in=w_in, dn_conv_w=dn_conv_w, w_attn_branch=w_attn_branch, w_dn_branch=w_dn_branch, w_out=w_out,
               ffn_w_up=ffn_w_up, ffn_conv_w=ffn_conv_w, ffn_w_down=ffn_w_down)
    big_names = list(big)

    first, mid, late = ["w_in", "dn_conv_w"], ["w_attn_branch", "w_dn_branch", "w_out"], ["ffn_w_up", "ffn_conv_w"]
    transposed = ("w_in", "ffn_w_up")
    local = lambda n, a: a[0].T if n in transposed else a[0]
    shard = lambda names: [local(n, big[n]).astype(bf16) for n in names]
    *got, c_all = _exchange(shard(first) + [c], "gather_w_in", two_level=True)
    gw = dict(zip(first, got))
    c_all = c_all.reshape(NDEV * B, D)

    wp = _pack_w_in(gw["w_in"].reshape(IN_DIM, D))
    conv_dn = _cols_gathered(gw["dn_conv_w"]).astype(f32)

    ncol = ada_w.shape[2]
    ada_b_mine = lax.dynamic_slice_in_dim(ada_b, me * ncol, ncol, axis=1)
    mod_cols = ada_fwd(c_all, ada_w[0], ada_b_mine)
    (mod_g,) = _exchange([mod_cols], "gather_mod")
    mod = lax.dynamic_slice_in_dim(mod_g, me * B, B, axis=1).transpose(1, 0, 2).reshape(B, NMOD * D)
    sh1, sc1, g1, sh2, sc2, g2 = [mod[:, i * D:(i + 1) * D].reshape(B, 1, D) for i in range(NMOD)]

    onehot = (jnp.asarray(_bucket_table()).reshape(1, -1) == jnp.arange(NBUCK, dtype=jnp.int32)[:, None]).astype(f32)
    bias = mm(rel_bias.T, onehot, "nn", f32, "bias_table", tn=8192, precision=HI).reshape(HQ, WIN, 2 * WIN)
    sinks = attn_sinks.reshape(HQ, 1, 1)
    a_log_pad = jnp.pad(dn_a_log, ((0, 0), (DNH, LANE - 2 * DNH)))
    dt_bias_pad = jnp.pad(dn_dt_bias, ((0, 0), (DNH, LANE - 2 * DNH)))

    (u1,) = rowcall_fwd("mix_pre", f_rms_mod, [(x, D, 0)], [sc1, sh1], [norm_mix_pre], [(D, bf16)])
    proj, gw["ffn_w_down"] = mm(u1.reshape(T, D), wp, "nt", bf16, "proj", tm=512, tn=CB_BA * LANE, b_cols=(0, 1),
                                comm=_Comm(shard(["ffn_w_down"]), two_level=True))
    proj = proj.reshape(B, S, CB_BA * LANE)
    ba = mm(u1.reshape(T, D), wp, "nt", f32, "proj_ba", tn=LANE, b_cols=(CB_BA, 1)).reshape(B, S, LANE)
    ya, *got = attn_fwd(proj, bias, sinks, _Comm(shard(mid), two_level=True))
    gw.update(zip(mid, got))
    wa = _cols_gathered(gw["w_attn_branch"])
    wd = _cols_gathered(gw["w_dn_branch"])
    wo = gw["w_out"].reshape(D, D)
    qkvn = dnconv_fwd(proj, conv_dn)
    (bg,) = rowcall_fwd("dn_gate", f_gate, [(ba, LANE, 0)], [], [a_log_pad, dt_bias_pad], [(LANE, f32)])
    o_dn, states, *got = delta_fwd(qkvn, bg, _Comm(shard(late), two_level=True))
    gw.update(zip(late, got))
    wup = gw["ffn_w_up"].reshape(2 * DFF, D)
    conv_ffn = _cols_gathered(gw["ffn_conv_w"]).astype(f32)
    wdown = gw["ffn_w_down"].reshape(DFF, D)
    (yd,) = rowcall_fwd("dn_out", f_dnout, [(o_dn, DNH * DND, 0), (proj, DNH * DND, CB_DZ // 4)], [], [dn_norm_w], [(DNH * DND, bf16)])
    pa = mm(ya.reshape(T, HQ * HD), wa, "nn", bf16, "attn_branch").reshape(B, S, D)
    pd = mm(yd.reshape(T, DNH * DND), wd, "nn", bf16, "dn_branch").reshape(B, S, D)
    merge_tok = [(proj, D, CB_GA // 8), (proj, D, CB_GD // 8), (pa, D, 0), (pd, D, 0)]
    (merged,) = rowcall_fwd("merge", f_merge, merge_tok, [], [], [(D, bf16)])
    y1 = mm(merged.reshape(T, D), wo, "nn", bf16, "mix_out").reshape(B, S, D)
    post_pre = ([(x, D, 0), (y1, D, 0)], [g1, sc2, sh2], [norm_mix_post, norm_ffn_pre])
    h1, u2 = rowcall_fwd("mix_post_ffn_pre", f_post_pre, *post_pre, [(D, f32), (D, bf16)])
    up = mm(u2.reshape(T, D), wup, "nt", bf16, "ffn_up", tn=2816).reshape(B, S, 2 * DFF)
    act = ffnconv_fwd(up, conv_ffn)
    y2 = mm(act.reshape(T, DFF), wdown, "nn", bf16, "ffn_down", tk=2816).reshape(B, S, D)

    dh1_a, dy2, dg2, dw_ffn_post, loss_b = loss_head(h1, y2, loss_target, g2, norm_ffn_post)
    dy2f = dy2.reshape(T, D)
    dact = mm(dy2f, wdown, "nt", bf16, "ffn_down_dx", tn=2816).reshape(B, S, DFF)
    g_wdown = mm(act.reshape(T, DFF), dy2f, "tn", bf16, "ffn_down_dw", tm=2816, tn=512, tk=4096)
    parts = {}
    outbox = lambda d: _Comm([d[n].astype(bf16) for n in d], scatter=True)
    dup, g_conv_ffn, parts["ffn_w_down"] = ffnconv_bwd(up, conv_ffn, dact, outbox(dict(ffn_w_down=g_wdown.reshape(NDEV, DFF // NDEV, D))))
    dupf = dup.reshape(2, T, DFF)
    g_conv_ffn = g_conv_ffn.transpose(1, 0, 2).reshape(FK, 2 * DFF)
    du2 = mm(dupf, wup, "nn", bf16, "ffn_up_dx", tk=2816).reshape(B, S, D)
    g_wup = mm(dupf, u2.reshape(T, D), "tn", bf16, "ffn_up_dw", tm=1408, tk=2048)
    dh1, dy1, dg1, dsc2, dsh2, dw_mix_post, dw_ffn_pre = rowcall_bwd(
        "mix_post_ffn_pre_bwd", f_post_pre, *post_pre, [(dh1_a, D, 0), (du2, D, 0)], [(0, f32), (1, bf16)])
    dy1f = dy1.reshape(T, D)
    dmerged = mm(dy1f, wo, "nt", bf16, "mix_out_dx").reshape(B, S, D)
    g_wo = mm(merged.reshape(T, D), dy1f, "tn", bf16, "mix_out_dw", tk=2048)
    dproj = lax.empty((B, S, NP), bf16)
    dproj, dpa, dpd = rowcall_bwd("merge_bwd", f_merge, merge_tok, [], [], [(dmerged, D, 0)],
                                  [(0, bf16), (1, bf16), (2, bf16), (3, bf16)], join_first=2, into=(dproj, CB_GA // 16))
    dpaf, dpdf = dpa.reshape(T, D), dpd.reshape(T, D)
    dya = mm(dpaf, wa, "nt", bf16, "attn_branch_dx").reshape(B, S, HQ * HD)
    g_wa = mm(ya.reshape(T, HQ * HD), dpaf, "tn", bf16, "attn_branch_dw", tk=2048)
    dyd = mm(dpdf, wd, "nt", bf16, "dn_branch_dx").reshape(B, S, DNH * DND)
    g_wd = mm(yd.reshape(T, DNH * DND), dpdf, "tn", bf16, "dn_branch_dw", tk=2048)
    dproj, do_dn, dw_dn_norm = rowcall_bwd("dn_out_bwd", f_dnout, [(o_dn, DNH * DND, 0), (proj, DNH * DND, CB_DZ // 4)], [], [dn_norm_w],
                                           [(dyd, DNH * DND, 0)], [(1, bf16), (0, f32)], into=(dproj, CB_DZ // 4))
    send = dict(ffn_w_up=g_wup.reshape(NDEV, 2 * DFF // NDEV, D), ffn_conv_w=_cols_split(g_conv_ffn))
    dqkvn, dbg, *got = delta_bwd(qkvn, bg, states, do_dn, outbox(send))
    parts.update(zip(send, got))
    dproj, da_log_pad, ddt_bias_pad = rowcall_bwd("dn_gate_bwd", f_gate, [(ba, LANE, 0)], [], [a_log_pad, dt_bias_pad],
                                                  [(dbg, LANE, 0)], [(0, bf16)], into=(dproj, CB_BA))
    dproj, g_conv_dn = dnconv_bwd(proj, conv_dn, dqkvn, dproj)
    send = dict(w_attn_branch=_cols_split(g_wa), w_dn_branch=_cols_split(g_wd),
                w_out=g_wo.reshape(NDEV, D // NDEV, D))
    dproj, dk, dv, dbias, dsinks, *got = attn_bwd(proj, bias, sinks, dya, dproj, outbox(send))
    parts.update(zip(send, got))
    dproj = lax.dynamic_update_slice(dproj, jnp.concatenate([dk, dv], axis=2), (0, 0, CB_AK * LANE)).reshape(T, NP)
    g_wp = mm(dproj, u1.reshape(T, D), "tn", bf16, "proj_dw", tm=1664, tk=1024)
    send = dict(w_in=_unpack_w_in(g_wp).reshape(NDEV, IN_DIM // NDEV, D), dn_conv_w=_cols_split(g_conv_dn).astype(bf16))
    started = _scatter_start(list(send.values()), "scatter_w_in_start")
    du1 = mm(dproj, wp, "nn", bf16, "proj_dx", tm=512, tk=NP).reshape(B, S, D)
    grad_x, dsc1, dsh1, dw_mix_pre = rowcall_bwd("mix_pre_bwd", f_rms_mod, [(x, D, 0)], [sc1 + started[-1][0, 0], sh1], [norm_mix_pre],
                                                 [(du1, D, 0)], [(0, f32)], add=(dh1, D, 0))
    g_rel = mm(dbias.reshape(HQ, WIN * 2 * WIN), onehot, "nt", f32, "rel_bias_dw", tk=8192, precision=HI)

    dmod = jnp.concatenate([dsh1, dsc1, dg1, dsh2, dsc2, dg2], axis=2).reshape(B, NMOD * D)

    zrow = lambda a: jnp.concatenate([a.reshape(1, -1), jnp.zeros((B - 1, a.size), f32)], axis=0)
    small_g = jnp.concatenate([
        dmod, dw_mix_pre.reshape(B, D), dw_mix_post.reshape(B, D), dw_ffn_pre.reshape(B, D), dw_ffn_post.reshape(B, D),
        da_log_pad.reshape(B, LANE)[:, DNH:2 * DNH], ddt_bias_pad.reshape(B, LANE)[:, DNH:2 * DNH], dw_dn_norm.reshape(B, DND),
        zrow(dsinks), zrow(g_rel.T), loss_b.reshape(B, LANE)[:, :1], jnp.zeros((B, SMALL_PAD - SMALL_N - 1), f32)], axis=1)
    (small_all,) = _exchange([small_g], "gather_small")
    dmod_cols = lax.dynamic_slice_in_dim(small_all.reshape(NDEV * B, SMALL_PAD), me * ncol, ncol, axis=1)
    g_ada_w = ada_bwd(c_all, dmod_cols)
    landed = _scatter_finish(started, len(send), g_ada_w, "scatter_w_in_finish")
    for nme, src, got in zip(send, send.values(), landed):
        parts[nme] = lax.dynamic_update_slice_in_dim(got, lax.dynamic_slice_in_dim(src, me, 1, axis=0), me, axis=0)
    small_w = dict(ada_b=(ada_b, m_ada_b, v_ada_b), norm_mix_pre=(norm_mix_pre, m_norm_mix_pre, v_norm_mix_pre),
                   norm_mix_post=(norm_mix_post, m_norm_mix_post, v_norm_mix_post), norm_ffn_pre=(norm_ffn_pre, m_norm_ffn_pre, v_norm_ffn_pre),
                   norm_ffn_post=(norm_ffn_post, m_norm_ffn_post, v_norm_ffn_post), dn_a_log=(dn_a_log, m_dn_a_log, v_dn_a_log),
                   dn_dt_bias=(dn_dt_bias, m_dn_dt_bias, v_dn_dt_bias), dn_norm_w=(dn_norm_w, m_dn_norm_w, v_dn_norm_w),
                   attn_sinks=(attn_sinks, m_attn_sinks, v_attn_sinks), rel_bias=(rel_bias, m_rel_bias, v_rel_bias))

    def pack(i, fill):
        row = jnp.concatenate([small_w[n][i].reshape(1, -1) for n, _ in SMALL], axis=1)
        return jnp.pad(row, ((0, 0), (0, SMALL_PAD - SMALL_N)), constant_values=fill)

    small_out = adamw(pack(0, 0.0), small_all.reshape(NDEV * B, 1, SMALL_PAD), pack(1, 0.0), pack(2, 1.0), "adamw_small")
    loss = small_out[0][0, SMALL_N]

    res = {}
    off = 0
    for n, size in SMALL:
        shp = small_w[n][0].shape
        res[n] = [o[:, off:off + size].reshape(shp) for o in small_out]
        off += size
    res["ada_w"] = [o[None] for o in adamw(ada_w[0], g_ada_w[None], m_ada_w[0], v_ada_w[0], "adamw_ada_w")]
    moments = dict(w_in=(m_w_in, v_w_in), dn_conv_w=(m_dn_conv_w, v_dn_conv_w), w_attn_branch=(m_w_attn_branch, v_w_attn_branch),
                   w_dn_branch=(m_w_dn_branch, v_w_dn_branch), w_out=(m_w_out, v_w_out), ffn_w_up=(m_ffn_w_up, v_ffn_w_up),
                   ffn_conv_w=(m_ffn_conv_w, v_ffn_conv_w), ffn_w_down=(m_ffn_w_down, v_ffn_w_down))
    for n in big_names:
        outs = adamw(local(n, big[n]), parts[n], local(n, moments[n][0]), local(n, moments[n][1]), "adamw_" + n)
        res[n] = [(o.T if n in transposed else o)[None] for o in outs]

    order = ["ada_w", "ada_b", "norm_mix_pre", "norm_mix_post", "norm_ffn_pre", "norm_ffn_post", "w_in", "dn_conv_w", "dn_a_log",
             "dn_dt_bias", "dn_norm_w", "attn_sinks", "rel_bias", "w_attn_branch", "w_dn_branch", "w_out", "ffn_w_up", "ffn_conv_w",
             "ffn_w_down"]
    return (loss, grad_x, *[res[n][0] for n in order], *[res[n][1] for n in order], *[res[n][2] for n in order],
            *[res[n][3] for n in order])
```

```python
import functools
import math

import numpy as np
import jax
import jax.numpy as jnp
from jax import lax
from jax.experimental import pallas as pl
from jax.experimental.pallas import tpu as pltpu

f32 = jnp.float32
bf16 = jnp.bfloat16
HI = lax.Precision.HIGHEST
MID = lax.Precision.HIGH
MESH = pl.DeviceIdType.MESH

NDEV = 8
D = 1024
HQ, HKV, HD, WIN, NBUCK, MAXDIST = 8, 2, 64, 128, 32, 128
DNH, DND, DNK, CH = 4, 128, 4, 64
DFF, FK = 2816, 3
NMOD = 6
RMS_EPS = 1e-6
L2_EPS = 1e-6
NEG_INF = -1e30
LR, B1, B2, EPS, WD, STEP = 0.001, 0.9, 0.999, 1e-08, 0.01, 10

LANE = 128
CB_GA, CB_GD, CB_AQ, CB_DQKV, CB_DZ, CB_AK, CB_AV, CB_BA, NPB = 0, 8, 16, 20, 32, 36, 37, 38, 39
NP = NPB * LANE
IN_SPLITS = (HQ * HD, HKV * HD, HKV * HD, 3 * DNH * DND, DNH * DND, DNH, DNH, D, D)
IN_DIM = sum(IN_SPLITS)
VMEM_LIMIT = 56 * 1024 * 1024

SMALL = (("ada_b", NMOD * D), ("norm_mix_pre", D), ("norm_mix_post", D), ("norm_ffn_pre", D), ("norm_ffn_post", D),
         ("dn_a_log", DNH), ("dn_dt_bias", DNH), ("dn_norm_w", DND), ("attn_sinks", HQ), ("rel_bias", NBUCK * HQ))
SMALL_N = sum(n for _, n in SMALL)
SMALL_PAD = 10752


def _cp(sem):
    return pltpu.CompilerParams(dimension_semantics=sem, vmem_limit_bytes=VMEM_LIMIT)


def _pick(dim, target):
    if dim <= target:
        return dim
    best = None
    for d in range(LANE, target + 1, LANE):
        if dim % d == 0:
            best = d
    assert best is not None, (dim, target)
    return best


def _me():
    x, y, c = lax.axis_index("x"), lax.axis_index("y"), lax.axis_index("c")
    return x, y, c, 4 * x + 2 * y + c


def _peer(x, y, c, k):
    px = 1 - x if k & 4 else x
    py = 1 - y if k & 2 else y
    pc = 1 - c if k & 1 else c
    return (px, py, pc), 4 * px + 2 * py + pc


class _Comm:
    def __init__(self, arrs, scatter=False, two_level=False):
        assert not (scatter and two_level)
        self.arrs, self.n, self.scatter, self.two_level = list(arrs), len(arrs), scatter, two_level
        if scatter:
            self.out_shape = [jax.ShapeDtypeStruct(a.shape, a.dtype) for a in arrs]
        else:
            self.out_shape = [jax.ShapeDtypeStruct((NDEV,) + a.shape, a.dtype) for a in arrs]
        nsem = self.n * (NDEV - 1)
        self.scratch = [pltpu.SemaphoreType.DMA((nsem,)), pltpu.SemaphoreType.DMA((nsem,)), pltpu.SemaphoreType.DMA((self.n,))]
        self.specs = [pl.BlockSpec(memory_space=pl.ANY)] * self.n

    def phases(self, ins, out, send, recv, loc):
        x, y, c, me = _me()

        def remote(a, k, src, dst, to):
            s = a * (NDEV - 1) + k - 1
            return pltpu.make_async_remote_copy(src_ref=src, dst_ref=dst, send_sem=send.at[s], recv_sem=recv.at[s],
                                                device_id=to, device_id_type=MESH)

        def local(a):
            return pltpu.make_async_copy(ins[a].at[me] if self.scatter else ins[a], out[a].at[me], loc.at[a])

        if not self.two_level:
            def mine(a, k):
                peer, pid = _peer(x, y, c, k)
                return remote(a, k, ins[a].at[pid] if self.scatter else ins[a], out[a].at[me], peer)

            def theirs(a, k):
                peer, pid = _peer(x, y, c, k)
                return remote(a, k, ins[a].at[pid] if self.scatter else ins[a], out[a].at[pid], peer)

            def start():
                for a in range(self.n):
                    local(a).start()
                    for k in range(1, NDEV):
                        mine(a, k).start()

            def forward():
                pass

            def finish():
                for a in range(self.n):
                    for k in range(1, NDEV):
                        mine(a, k).wait_send()
                    for k in range(1, NDEV):
                        theirs(a, k).wait_recv()
                    local(a).wait()

            return start, forward, finish

        sibling = (x, y, 1 - c)
        chips = [(1 - x, y), (x, 1 - y), (1 - x, 1 - y)]
        slot = lambda px, py, pc: 4 * px + 2 * py + pc

        def own(a, k, to):
            return remote(a, k, ins[a], out[a].at[me], to)

        def landed(a, k, frm):
            return remote(a, k, ins[a], out[a].at[slot(*frm)], frm)

        def passed(a, j):
            rows = out[a].at[slot(*chips[j], c)]
            return remote(a, 5 + j, rows, rows, sibling)

        def start():
            for a in range(self.n):
                local(a).start()
                own(a, 1, sibling).start()
                for j, chip in enumerate(chips):
                    own(a, 2 + j, (*chip, c)).start()

        def forward():
            for a in range(self.n):
                for j, chip in enumerate(chips):
                    landed(a, 2 + j, (*chip, c)).wait_recv()
                    passed(a, j).start()

        def finish():
            for a in range(self.n):
                landed(a, 1, sibling).wait_recv()
                for j, chip in enumerate(chips):
                    remote(a, 5 + j, ins[a], out[a].at[slot(*chip, 1 - c)], sibling).wait_recv()
                own(a, 1, sibling).wait_send()
                for j, chip in enumerate(chips):
                    own(a, 2 + j, (*chip, c)).wait_send()
                    passed(a, j).wait_send()
                local(a).wait()

        return start, forward, finish


class _NoComm:
    n, arrs, out_shape, specs, scratch = 0, [], [], [], []

    def phases(self, *_):
        return (lambda: None,) * 3


def _ride(body, n_in, n_out, n_scr, comm, first, mid, last):
    k = comm.n

    def wrapped(*refs):
        ins, cins = refs[:n_in], refs[n_in:n_in + k]
        o0 = n_in + k
        outs, couts = refs[o0:o0 + n_out], refs[o0 + n_out:o0 + n_out + k]
        s0 = o0 + n_out + k
        scr, sems = refs[s0:s0 + n_scr], refs[s0 + n_scr:]
        start, forward, finish = comm.phases(cins, couts, *sems)
        pl.when(first())(start)
        body(*ins, *outs, *scr)
        pl.when(mid())(forward)
        pl.when(last())(finish)

    return wrapped


def _scatter_start(arrs, name):
    n = len(arrs)

    def body(*refs):
        ins, lands, send, recv, token = refs[:n], refs[n:2 * n], refs[2 * n], refs[2 * n + 1], refs[-1]
        x, y, c, me = _me()
        for a in range(n):
            for k in range(1, NDEV):
                peer, pid = _peer(x, y, c, k)
                s = a * (NDEV - 1) + k - 1
                pltpu.make_async_remote_copy(src_ref=ins[a].at[pid], dst_ref=lands[a].at[me], send_sem=send.at[s], recv_sem=recv.at[s],
                                             device_id=peer, device_id_type=MESH).start()
        token[...] = jnp.zeros(token.shape, token.dtype)

    hbm, sem = pl.BlockSpec(memory_space=pltpu.HBM), pl.BlockSpec(memory_space=pltpu.SEMAPHORE)
    nsem = n * (NDEV - 1)
    thru = [pltpu.HBM(a.shape, a.dtype) for a in arrs]
    return pl.pallas_call(
        body, name=name, in_specs=[hbm] * (2 * n),
        out_shape=(pltpu.SemaphoreType.DMA((nsem,)), pltpu.SemaphoreType.DMA((nsem,)), *thru, *thru, jax.ShapeDtypeStruct((8, LANE), f32)),
        out_specs=(sem, sem, *[hbm] * (2 * n), pl.BlockSpec(memory_space=pltpu.VMEM)),
        input_output_aliases={i: 2 + i for i in range(2 * n)},
        compiler_params=pltpu.CompilerParams(has_side_effects=pltpu.SideEffectType.DATAFLOW_SIDE_EFFECTING),
    )(*[pltpu.with_memory_space_constraint(a, pltpu.HBM) for a in arrs],
      *[pltpu.with_memory_space_constraint(lax.empty(a.shape, a.dtype), pltpu.HBM) for a in arrs])


def _scatter_finish(started, n, after, name):
    send, recv, *rest = started
    srcs, lands = rest[:n], rest[n:2 * n]

    def body(*refs):
        ins, lnd, send_ref, recv_ref = refs[:n], refs[n:2 * n], refs[2 * n], refs[2 * n + 1]
        x, y, c, me = _me()
        for a in range(n):
            for k in range(1, NDEV):
                peer, pid = _peer(x, y, c, k)
                s = a * (NDEV - 1) + k - 1
                cp = pltpu.make_async_remote_copy(src_ref=ins[a].at[pid], dst_ref=lnd[a].at[pid], send_sem=send_ref.at[s],
                                                  recv_sem=recv_ref.at[s], device_id=peer, device_id_type=MESH)
                cp.wait_send()
                cp.wait_recv()

    hbm, sem = pl.BlockSpec(memory_space=pltpu.HBM), pl.BlockSpec(memory_space=pltpu.SEMAPHORE)
    thru = [pltpu.HBM(a.shape, a.dtype) for a in srcs]
    out = pl.pallas_call(
        body, name=name, in_specs=[hbm] * (2 * n) + [sem, sem, pl.BlockSpec(memory_space=pl.ANY)],
        out_shape=(*thru, *thru), out_specs=tuple([hbm] * (2 * n)), input_output_aliases={i: i for i in range(2 * n)},
        compiler_params=pltpu.CompilerParams(has_side_effects=pltpu.SideEffectType.DATAFLOW_SIDE_EFFECTING),
    )(*srcs, *lands, send, recv, after)
    return list(out[n:])


def _exchange(arrs, name, scatter=False, two_level=False):
    comm = _Comm(arrs, scatter, two_level)

    def body(*refs):
        start, forward, finish = comm.phases(refs[:comm.n], refs[comm.n:2 * comm.n], *refs[2 * comm.n:])
        start()
        forward()
        finish()

    return pl.pallas_call(body, name=name, out_shape=comm.out_shape, in_specs=comm.specs, out_specs=comm.specs,
                          scratch_shapes=comm.scratch, compiler_params=pltpu.CompilerParams(has_side_effects=True))(*arrs)


def mm(a, b, mode, out_dtype, name, tm=1024, tn=1024, tk=1024, precision=None, comm=None, b_cols=None):
    a_parts = a.shape[0] if a.ndim == 3 else 1
    b_parts = b.shape[0] if b.ndim == 3 else 1
    assert b_parts == 1 or mode == "tn"
    ash, bsh = (a.shape[-2], a.shape[-1] * a_parts), b.shape[-2:]
    if mode == "nn":
        (M, K), (K2, N) = ash, bsh
    elif mode == "nt":
        (M, K), (N, K2) = ash, bsh
    else:
        (K, M), (K2, N) = ash, (bsh[0], bsh[1] * b_parts)
    assert K == K2, (name, a.shape, b.shape)
    col0 = 0
    if b_cols is not None:
        assert mode in ("nn", "nt") and tn % LANE == 0
        col0, N = b_cols[0], b_cols[1] * tn
    if mode == "tn":
        tm, tn, tk = _pick(M // a_parts, tm), _pick(N // b_parts, tn), _pick(K, tk)
    else:
        tm, tn, tk = _pick(M, tm), _pick(N // b_parts, tn), _pick(K // a_parts, tk)
    nk = K // tk
    if mode == "tn" and a_parts > 1:
        per = M // tm // a_parts
        a_spec = pl.BlockSpec((None, tk, tm), lambda i, j, k: (i // per, k, i % per))
    elif mode == "tn":
        a_spec = pl.BlockSpec((tk, tm), lambda i, j, k: (k, i))
    elif a_parts > 1:
        per = nk // a_parts
        a_spec = pl.BlockSpec((None, tm, tk), lambda i, j, k: (k // per, i, k % per))
    else:
        a_spec = pl.BlockSpec((tm, tk), lambda i, j, k: (i, k))
    if mode == "nt":
        b_spec = pl.BlockSpec((tn, tk), lambda i, j, k: (col0 + j, k))
    elif b_parts > 1:
        per = N // tn // b_parts
        b_spec = pl.BlockSpec((None, tk, tn), lambda i, j, k: (j // per, k, j % per))
    else:
        b_spec = pl.BlockSpec((tk, tn), lambda i, j, k: (k, col0 + j))
    dims = {"nn": ((1,), (0,)), "nt": ((1,), (1,)), "tn": ((0,), (0,))}[mode]

    def body(a_ref, b_ref, o_ref, *scr):
        p = lax.dot_general(a_ref[...], b_ref[...], (dims, ((), ())), preferred_element_type=f32, precision=precision)
        if nk == 1:
            o_ref[...] = p.astype(o_ref.dtype)
        else:
            acc = scr[0]
            k = pl.program_id(2)

            @pl.when(k == 0)
            def _():
                acc[...] = p

            @pl.when(k > 0)
            def _():
                acc[...] += p

            @pl.when(k == nk - 1)
            def _():
                o_ref[...] = acc[...].astype(o_ref.dtype)

    grid = (M // tm, N // tn, nk)
    scratch = [pltpu.VMEM((tm, tn), f32)] if nk > 1 else []
    out_spec = pl.BlockSpec((tm, tn), lambda i, j, k: (i, j))
    out_shape = jax.ShapeDtypeStruct((M, N), out_dtype)
    if comm is None:
        return pl.pallas_call(body, name=name, grid=grid, in_specs=[a_spec, b_spec], out_specs=out_spec, out_shape=out_shape,
                              scratch_shapes=scratch, compiler_params=_cp(("parallel", "parallel", "arbitrary")))(a, b)
    at = lambda pos: lambda: functools.reduce(jnp.logical_and, [pl.program_id(d) == p for d, p in enumerate(pos)])
    end = tuple(g - 1 for g in grid)
    return pl.pallas_call(
        _ride(body, 2, 1, len(scratch), comm, at((0, 0, 0)), at(end), at(end)), name=name, grid=grid,
        in_specs=[a_spec, b_spec] + comm.specs, out_specs=[out_spec] + comm.specs, out_shape=[out_shape] + comm.out_shape,
        scratch_shapes=scratch + comm.scratch, compiler_params=_cp(("arbitrary", "arbitrary", "arbitrary")),
    )(a, b, *comm.arrs)


def rowcall(name, fn, tok, bat, con, tok_out, acc_out, ts=256, into=None):
    B, S = tok[0][0].shape[:2]
    ts = min(ts, S)
    nt, nb, nc, no, na = len(tok), len(bat), len(con), len(tok_out), len(acc_out)
    nin = nt + nb + nc + (1 if into is not None else 0)

    def body(*refs):
        tr, br, cr = refs[:nt], refs[nt:nt + nb], refs[nt + nb:nt + nb + nc]
        orf, arf = refs[nin:nin + no], refs[nin + no:]
        touts, aouts = fn([r[0] for r in tr], [r[0] for r in br], [r[...] for r in cr])
        for r, v in zip(orf, touts):
            r[0] = v.astype(r.dtype)
        s = pl.program_id(1)
        for r, v in zip(arf, aouts):
            @pl.when(s == 0)
            def _(r=r):
                r[...] = jnp.zeros(r.shape, r.dtype)
            r[0] += v.astype(f32)

    in_specs = [pl.BlockSpec((1, ts, w), lambda b, s, cb=cb: (b, s, cb)) for (_, w, cb) in tok]
    in_specs += [pl.BlockSpec((1,) + a.shape[1:], lambda b, s: (b, 0, 0)) for a in bat]
    in_specs += [pl.BlockSpec(a.shape, lambda b, s, nd=a.ndim: (0,) * nd) for a in con]
    out_specs = [pl.BlockSpec((1, ts, w), lambda b, s: (b, s, 0)) for (w, _) in tok_out]
    out_specs += [pl.BlockSpec((1,) + shp, lambda b, s, nd=len(shp): (b,) + (0,) * nd) for shp in acc_out]
    out_shape = [jax.ShapeDtypeStruct((B, S, w), dt) for (w, dt) in tok_out]
    out_shape += [jax.ShapeDtypeStruct((B,) + shp, f32) for shp in acc_out]
    extra, aliases = [], {}
    if into is not None:
        buf, cb = into
        assert buf.dtype == tok_out[0][1]
        in_specs.append(pl.BlockSpec(memory_space=pl.ANY))
        out_specs[0] = pl.BlockSpec((1, ts, tok_out[0][0]), lambda b, s: (b, s, cb))
        out_shape[0] = jax.ShapeDtypeStruct(buf.shape, buf.dtype)
        extra, aliases = [buf], {nin - 1: 0}
    return pl.pallas_call(
        body, name=name, grid=(B, S // ts), in_specs=in_specs, out_specs=out_specs, out_shape=out_shape,
        input_output_aliases=aliases, compiler_params=_cp(("parallel", "arbitrary")),
    )(*[t[0] for t in tok], *bat, *con, *extra)


def rowcall_fwd(name, f, tok, bat, con, tok_out, ts=256):
    def fn(t, b, c):
        return f([v.astype(f32) for v in t], b, c), []
    return rowcall(name, fn, tok, bat, con, tok_out, [], ts)


def rowcall_bwd(name, f, tok, bat, con, cts, tok_grads, add=None, ts=256, join_first=1, into=None):
    nt, ncts = len(tok), len(cts)

    def fn(t, b, c):
        prim = [v.astype(f32) for v in t[:nt]]
        ct = [v.astype(f32) for v in t[nt:nt + ncts]]
        _, vjp = jax.vjp(lambda tt, bb, cc: f(tt, bb, cc), prim, b, c)
        dt, db, dc = vjp(ct)
        touts = [dt[i] for i, _ in tok_grads]
        if add is not None:
            touts[0] = touts[0] + t[nt + ncts].astype(f32)
        if join_first > 1:
            touts = [jnp.concatenate(touts[:join_first], axis=1)] + touts[join_first:]
        return touts, list(db) + list(dc)

    all_tok = list(tok) + list(cts) + ([add] if add is not None else [])
    tok_out = [(tok[i][1], dt) for i, dt in tok_grads]
    if join_first > 1:
        tok_out = [(sum(w for w, _ in tok_out[:join_first]), tok_out[0][1])] + tok_out[join_first:]
    acc_out = [tuple(a.shape[1:]) for a in bat] + [tuple(a.shape) for a in con]
    return rowcall(name, fn, all_tok, bat, con, tok_out, acc_out, ts, into)


def _rms(y, w):
    return y * lax.rsqrt(jnp.mean(y * y, axis=-1, keepdims=True) + RMS_EPS) * w


def f_rms_mod(t, b, c):
    return [_rms(t[0], c[0]) * (1.0 + b[0]) + b[1]]


def f_post_pre(t, b, c):
    h1 = t[0] + b[0] * _rms(t[1], c[0])
    return [h1, _rms(h1, c[1]) * (1.0 + b[1]) + b[2]]


def f_merge(t, b, c):
    ga, gd, ya, yd = t
    return [jax.nn.sigmoid(ga) * ya + jax.nn.sigmoid(gd) * yd]


def f_dnout(t, b, c):
    o, z = t
    outs = []
    for h in range(DNH):
        sl = slice(h * DND, (h + 1) * DND)
        zh = z[:, sl]
        outs.append(_rms(o[:, sl], c[0]) * (zh * jax.nn.sigmoid(zh)))
    return [jnp.concatenate(outs, axis=1)]


def _softplus(x):
    return jnp.maximum(x, 0.0) + jnp.log(1.0 + jnp.exp(-jnp.abs(x)))


def f_gate(t, b, c):
    ba = t[0]
    a_log, dt_bias = c
    lane = lax.broadcasted_iota(jnp.int32, ba.shape, 1)
    beta = jax.nn.sigmoid(ba)
    g = -jnp.exp(a_log) * _softplus(ba + dt_bias)
    return [jnp.where(lane < DNH, beta, jnp.where(lane < 2 * DNH, g, 0.0))]


def _bucket_table():
    qi = np.arange(WIN)[:, None]
    kj = np.arange(2 * WIN)[None, :]
    dist = np.maximum(WIN + qi - kj, 0)
    max_exact = NBUCK // 2
    scaled = np.log(np.maximum(dist, 1).astype(np.float64) / max_exact) / math.log(MAXDIST / max_exact)
    large = np.minimum(max_exact + (scaled * (NBUCK - max_exact)).astype(np.int32), NBUCK - 1)
    return np.where(dist < max_exact, dist, large).astype(np.int32)


def _attn_mask(n):
    qi = lax.broadcasted_iota(jnp.int32, (WIN, 2 * WIN), 0)
    kj = lax.broadcasted_iota(jnp.int32, (WIN, 2 * WIN), 1)
    dist = WIN + qi - kj
    return (dist >= 0) & (dist < WIN) & ((kj >= WIN) | (n > 0))


def _swap_halves(x):
    return pltpu.roll(x, HD, axis=x.ndim - 1)


@jax.custom_vjp
def _swap_halves_vjp(x):
    return _swap_halves(x)


_swap_halves_vjp.defvjp(lambda x: (_swap_halves(x), None), lambda _, g: (_swap_halves(g),))


def _attn_block(q, kp, kc, vp, vc, bias, sinks, mask, differentiated):
    dot = _bdot_bf16_vjp if differentiated else _bdot_bf16
    swap = _swap_halves_vjp if differentiated else _swap_halves
    B, grp = q.shape[0], HQ // HKV
    upper = lax.broadcasted_iota(jnp.int32, (2 * WIN, LANE), 1) >= HD

    def placed(natural, swapped, j, half):
        keep = upper if half == 1 else ~upper
        return jnp.where(keep, natural if j == half else swapped, 0.0)

    qh, ks, vs = [], [], []
    for b in range(B):
        kb, vb = jnp.concatenate([kp[b], kc[b]], axis=0), jnp.concatenate([vp[b], vc[b]], axis=0)
        kb_sw, vb_sw = swap(kb), swap(vb)
        for h in range(HQ):
            qh.append(q[b, :, (h // 2) * LANE:(h // 2 + 1) * LANE])
            ks.append(placed(kb, kb_sw, h // grp, h % 2))
            vs.append(placed(vb, vb_sw, h // grp, h % 2))
    s = dot(_stack(qh), _stack(ks), 2, 2).reshape(B, HQ, WIN, 2 * WIN) * (HD ** -0.5)
    s = jnp.where(mask, s + bias, NEG_INF)
    m = jnp.maximum(jnp.max(s, axis=-1, keepdims=True), sinks)
    p = jnp.exp(s - m)
    probs = p / (jnp.sum(p, axis=-1, keepdims=True) + jnp.exp(sinks - m))
    o = dot(probs.reshape(B * HQ, WIN, 2 * WIN), _stack(vs), 2, 1)
    return _stack([jnp.concatenate([o[b * HQ + 2 * i] + o[b * HQ + 2 * i + 1] for i in range(HQ // 2)], axis=1) for b in range(B)])


def _attn_specs(B, NB):
    last = NB - 1
    return [
        pl.BlockSpec((B, WIN, HQ * HD), lambda n: (0, jnp.minimum(n, last), CB_AQ // 4)),
        pl.BlockSpec((B, WIN, LANE), lambda n: (0, jnp.clip(n - 1, 0, last), CB_AK)),
        pl.BlockSpec((B, WIN, LANE), lambda n: (0, jnp.minimum(n, last), CB_AK)),
        pl.BlockSpec((B, WIN, LANE), lambda n: (0, jnp.clip(n - 1, 0, last), CB_AV)),
        pl.BlockSpec((B, WIN, LANE), lambda n: (0, jnp.minimum(n, last), CB_AV)),
        pl.BlockSpec((HQ, WIN, 2 * WIN), lambda n: (0, 0, 0)),
        pl.BlockSpec((HQ, 1, 1), lambda n: (0, 0, 0)),
    ]


def attn_fwd(proj, bias, sinks, comm):
    B, S, _ = proj.shape
    NB = S // WIN

    def body(q, kp, kc, vp, vc, bias_ref, sink_ref, o_ref):
        mask = _attn_mask(pl.program_id(0))
        o = _attn_block(*[r[...].astype(f32) for r in (q, kp, kc, vp, vc)], bias_ref[...], sink_ref[...], mask, False)
        o_ref[...] = o.astype(o_ref.dtype)

    at = lambda n: lambda: pl.program_id(0) == n
    return pl.pallas_call(
        _ride(body, 7, 1, 0, comm, at(0), at((3 * NB) // 4), at(NB - 1)), name="attn_fwd", grid=(NB,),
        in_specs=_attn_specs(B, NB) + comm.specs,
        out_specs=[pl.BlockSpec((B, WIN, HQ * HD), lambda n: (0, n, 0))] + comm.specs,
        out_shape=[jax.ShapeDtypeStruct((B, S, HQ * HD), bf16)] + comm.out_shape, scratch_shapes=comm.scratch,
        compiler_params=_cp(("arbitrary",)),
    )(proj, proj, proj, proj, proj, bias, sinks, *comm.arrs)


def attn_bwd(proj, bias, sinks, dy, dproj, comm):
    B, S, _ = proj.shape
    NB = S // WIN
    last = NB - 1

    def body(q, kp, kc, vp, vc, bias_ref, sink_ref, dy_ref, _, dq_ref, dk_ref, dv_ref, dbias_ref, dsink_ref, kcar, vcar):
        n = pl.program_id(0)

        @pl.when(n == 0)
        def _():
            dbias_ref[...] = jnp.zeros(dbias_ref.shape, f32)
            dsink_ref[...] = jnp.zeros(dsink_ref.shape, f32)
            kcar[...] = jnp.zeros(kcar.shape, f32)
            vcar[...] = jnp.zeros(vcar.shape, f32)

        @pl.when(n < NB)
        def _():
            mask = _attn_mask(n)
            _, vjp = jax.vjp(lambda *a: _attn_block(*a, mask, True), *[r[...].astype(f32) for r in (q, kp, kc, vp, vc)],
                             bias_ref[...], sink_ref[...])
            dq, dkp, dkc, dvp, dvc, dbias, dsink = vjp(dy_ref[...].astype(f32))
            dq_ref[...] = dq.astype(dq_ref.dtype)
            dbias_ref[...] += dbias
            dsink_ref[...] += dsink
            dk_ref[...] = (kcar[...] + dkp).astype(dk_ref.dtype)
            dv_ref[...] = (vcar[...] + dvp).astype(dv_ref.dtype)
            kcar[...] = dkc
            vcar[...] = dvc

        @pl.when(n == NB)
        def _():
            dk_ref[...] = kcar[...].astype(dk_ref.dtype)
            dv_ref[...] = vcar[...].astype(dv_ref.dtype)

    in_specs = _attn_specs(B, NB) + [pl.BlockSpec((B, WIN, HQ * HD), lambda n: (0, jnp.minimum(n, last), 0)),
                                     pl.BlockSpec(memory_space=pl.ANY)]
    kv_out = pl.BlockSpec((B, WIN, LANE), lambda n: (0, jnp.maximum(n - 1, 0), 0))
    at = lambda n: lambda: pl.program_id(0) == n
    return pl.pallas_call(
        _ride(body, 9, 5, 2, comm, at(0), at(NB), at(NB)), name="attn_bwd", grid=(NB + 1,),
        in_specs=in_specs + comm.specs, input_output_aliases={8: 0},
        out_specs=[pl.BlockSpec((B, WIN, HQ * HD), lambda n: (0, jnp.minimum(n, last), CB_AQ // 4)), kv_out, kv_out,
                   pl.BlockSpec((HQ, WIN, 2 * WIN), lambda n: (0, 0, 0)), pl.BlockSpec((HQ, 1, 1), lambda n: (0, 0, 0))] + comm.specs,
        out_shape=[jax.ShapeDtypeStruct(dproj.shape, dproj.dtype), jax.ShapeDtypeStruct((B, S, LANE), bf16),
                   jax.ShapeDtypeStruct((B, S, LANE), bf16), jax.ShapeDtypeStruct((HQ, WIN, 2 * WIN), f32),
                   jax.ShapeDtypeStruct((HQ, 1, 1), f32)] + comm.out_shape,
        scratch_shapes=[pltpu.VMEM((B, WIN, LANE), f32), pltpu.VMEM((B, WIN, LANE), f32)] + comm.scratch,
        compiler_params=_cp(("arbitrary",)),
    )(proj, proj, proj, proj, proj, bias, sinks, dy, dproj, *comm.arrs)


DN_ROWS, FFN_ROWS = 256, 32


def _stage_rows(dst, value):
    dst[0:8] = jnp.zeros((8, LANE), f32)
    dst[8:8 + value.shape[0]] = value


def _conv_rows(xs, w, width, r, rows):
    wins = [xs[pl.ds(r + 8 - (width - 1) + j, rows), :] for j in range(width)]
    out = w[0:1] * wins[0]
    for j in range(1, width):
        out = out + w[j:j + 1] * wins[j]
    return out, wins


def _fold8(v):
    return jnp.sum(v.reshape(v.shape[0] // 8, 8, LANE), axis=0)


def _conv_rows_t(ds, w, width, r, rows):
    out = w[0:1] * ds[pl.ds(r + width - 1, rows), :]
    for j in range(1, width):
        out = out + w[j:j + 1] * ds[pl.ds(r + width - 1 - j, rows), :]
    return out


def _dn_outblk(i):
    return (i % DNH) * 3 + i // DNH


def _dn_act(c, isqk):
    sg = jax.nn.sigmoid(c)
    y = c * sg
    n = lax.rsqrt(jnp.sum(y * y, axis=-1, keepdims=True) + L2_EPS)
    return jnp.where(isqk, y * n, y), sg, n


def dnconv_fwd(proj, conv_w):
    B, S, _ = proj.shape
    rows = min(DN_ROWS, S)

    def body(x_ref, w_ref, o_ref, xs):
        isqk = pl.program_id(0) < 2 * DNH
        _stage_rows(xs, x_ref[0].astype(f32))
        w = w_ref[...]
        for r in range(0, S, rows):
            c, _ = _conv_rows(xs, w, DNK, r, rows)
            o_ref[0, pl.ds(r, rows), :] = _dn_act(c, isqk)[0]

    return pl.pallas_call(
        body, name="dnconv_fwd", grid=(3 * DNH, B),
        in_specs=[pl.BlockSpec((1, S, LANE), lambda i, b: (b, 0, CB_DQKV + i)), pl.BlockSpec((DNK, LANE), lambda i, b: (0, i))],
        out_specs=pl.BlockSpec((1, S, LANE), lambda i, b: (b, 0, _dn_outblk(i))),
        out_shape=jax.ShapeDtypeStruct((B, S, 3 * DNH * DND), f32), scratch_shapes=[pltpu.VMEM((S + 8, LANE), f32)],
        compiler_params=_cp(("parallel", "parallel")),
    )(proj, conv_w)


def dnconv_bwd(proj, conv_w, dqkvn, dproj):
    B, S, _ = proj.shape
    rows = min(DN_ROWS, S)

    def body(x_ref, w_ref, dy_ref, _, dx_ref, dw_ref, xs, ds):
        isqk = pl.program_id(0) < 2 * DNH
        _stage_rows(xs, x_ref[0].astype(f32))
        w = w_ref[...]
        dw = [jnp.zeros((8, LANE), f32) for _ in range(DNK)]
        for r in range(0, S, rows):
            c, wins = _conv_rows(xs, w, DNK, r, rows)
            out, sg, n = _dn_act(c, isqk)
            dout = dy_ref[0, pl.ds(r, rows), :]
            dy = jnp.where(isqk, n * (dout - out * jnp.sum(dout * out, axis=-1, keepdims=True)), dout)
            dc = dy * (sg * (1.0 + c * (1.0 - sg)))
            ds[pl.ds(r, rows), :] = dc
            for j in range(DNK):
                dw[j] = dw[j] + _fold8(dc * wins[j])
        ds[S:S + 8] = jnp.zeros((8, LANE), f32)
        for r in range(0, S, rows):
            dx_ref[0, pl.ds(r, rows), :] = _conv_rows_t(ds, w, DNK, r, rows).astype(dx_ref.dtype)

        @pl.when(pl.program_id(1) == 0)
        def _():
            dw_ref[...] = jnp.zeros(dw_ref.shape, f32)
        dw_ref[...] += jnp.concatenate([jnp.sum(d, axis=0, keepdims=True) for d in dw], axis=0)

    return pl.pallas_call(
        body, name="dnconv_bwd", grid=(3 * DNH, B),
        in_specs=[pl.BlockSpec((1, S, LANE), lambda i, b: (b, 0, CB_DQKV + i)), pl.BlockSpec((DNK, LANE), lambda i, b: (0, i)),
                  pl.BlockSpec((1, S, LANE), lambda i, b: (b, 0, _dn_outblk(i))), pl.BlockSpec(memory_space=pl.ANY)],
        out_specs=[pl.BlockSpec((1, S, LANE), lambda i, b: (b, 0, CB_DQKV + i)), pl.BlockSpec((DNK, LANE), lambda i, b: (0, i))],
        out_shape=[jax.ShapeDtypeStruct(dproj.shape, dproj.dtype), jax.ShapeDtypeStruct((DNK, 3 * DNH * DND), f32)],
        scratch_shapes=[pltpu.VMEM((S + 8, LANE), f32), pltpu.VMEM((S + 8, LANE), f32)],
        input_output_aliases={3: 0}, compiler_params=_cp(("parallel", "arbitrary")),
    )(proj, conv_w, dqkvn, dproj)


def _bdot(a, b, ca, cb, precision=HI):
    return lax.dot_general(a, b, (((ca,), (cb,)), ((0,), (0,))), preferred_element_type=f32, precision=precision)


def _bdot_bf16(a, b, ca, cb):
    return _bdot(a.astype(bf16), b.astype(bf16), ca, cb, None)


@functools.partial(jax.custom_vjp, nondiff_argnums=(2, 3))
def _bdot_bf16_vjp(a, b, ca, cb):
    return _bdot_bf16(a, b, ca, cb)


def _bdot_bf16_fwd(a, b, ca, cb):
    return _bdot_bf16(a, b, ca, cb), (a, b)


def _bdot_bf16_bwd(ca, cb, res, g):
    a, b = res
    fa, fb = 3 - ca, 3 - cb
    da = _bdot_bf16(g, b, 2, fb) if ca == 2 else _bdot_bf16(b, g, fb, 2)
    db = _bdot_bf16(a, g, fa, 1) if cb == 1 else _bdot_bf16(g, a, 1, fa)
    return da, db


_bdot_bf16_vjp.defvjp(_bdot_bf16_fwd, _bdot_bf16_bwd)


def _neumann_inverse(low):
    n = low.shape[-1]
    eye = (lax.broadcasted_iota(jnp.int32, (n, n), 0) == lax.broadcasted_iota(jnp.int32, (n, n), 1)).astype(f32)
    p = -low
    x = eye[None] + p
    for _ in range(5):
        p = _bdot_bf16(p, p, 2, 1)
        x = x + _bdot_bf16(x, p, 2, 1)
    return x


@jax.custom_vjp
def _unit_lower_inverse(low):
    return _neumann_inverse(low)


def _uli_fwd(low):
    t = _neumann_inverse(low)
    return t, t


def _uli_bwd(t, dt):
    return (-_bdot_bf16(_bdot_bf16(t, dt, 1, 1), t, 2, 2),)


_unit_lower_inverse.defvjp(_uli_fwd, _uli_bwd)


def _stack(xs):
    return jnp.concatenate([x[None] for x in xs], axis=0)


DELTA_CHUNKS = 2


def _delta_chunks(qkv, bg, state, differentiated):
    inverse = _unit_lower_inverse if differentiated else _neumann_inverse
    lo = _bdot_bf16_vjp if differentiated else _bdot_bf16
    B, n = qkv.shape[0], qkv.shape[1] // CH
    G = B * DNH
    N = n * G
    triples = [(i, b, h) for i in range(n) for b in range(B) for h in range(DNH)]
    col = lambda i, b, h, kind: qkv[b, i * CH:(i + 1) * CH, (3 * h + kind) * DND:(3 * h + kind + 1) * DND]
    q, k, v = [_stack([col(i, b, h, kind) for i, b, h in triples]) for kind in range(3)]
    lane = lax.broadcasted_iota(jnp.int32, (CH, LANE), 1)
    pick = lambda i, b, l: jnp.sum(jnp.where(lane == l, bg[b, i * CH:(i + 1) * CH], 0.0), axis=1, keepdims=True)
    beta = _stack([pick(i, b, h) for i, b, h in triples])
    g = _stack([pick(i, b, h + DNH) for i, b, h in triples])
    ri = lax.broadcasted_iota(jnp.int32, (CH, CH), 0)
    ci = lax.broadcasted_iota(jnp.int32, (CH, CH), 1)
    incl, strict = (ri >= ci)[None], (ri > ci)[None]
    gc = _bdot(jnp.broadcast_to(incl.astype(f32), (N, CH, CH)), jnp.broadcast_to(g, (N, CH, LANE)), 2, 1, MID)
    e0 = jnp.broadcast_to((lane == 0).astype(f32)[None], (N, CH, LANE))
    gc_row = _bdot(e0, gc, 2, 2, MID)
    diff = gc[:, :, :CH] - gc_row
    decay = jnp.where(incl, jnp.exp(jnp.where(incl, diff, 0.0)), 0.0)
    qs = q * (DND ** -0.5)
    kb, vb = k * beta, v * beta
    eg = jnp.exp(gc)
    with_k = lo(jnp.concatenate([kb, qs], axis=1), k, 2, 2)
    low = jnp.where(strict, with_k[:, :CH] * decay, 0.0)
    intra = jnp.where(incl, with_k[:, CH:] * decay, 0.0)
    tinv = inverse(low)
    solved = lo(tinv, jnp.concatenate([vb, kb * eg], axis=2), 2, 1)
    gl = gc[:, CH - 1:CH, :]
    k_tail = k * jnp.exp(gl - gc)
    to_state = jnp.concatenate([solved[:, :, DND:], qs * eg], axis=1)
    decay_all = jnp.exp(gl)
    outs = []
    for i in range(n):
        sl = slice(i * G, (i + 1) * G)
        with_state = lo(to_state[sl], state, 2, 1)
        v_new = solved[sl, :, :DND] - with_state[:, :CH]
        outs.append(with_state[:, CH:] + lo(intra[sl], v_new, 2, 1))
        state = state * decay_all[sl] + lo(k_tail[sl], v_new, 1, 1)
    return outs, state


def delta_fwd(qkvn, bg, comm):
    B, S, _ = qkvn.shape
    n = DELTA_CHUNKS if (S // CH) % DELTA_CHUNKS == 0 else 1
    steps, G, rows = S // (n * CH), B * DNH, n * CH

    def body(qkv_ref, bg_ref, o_ref, st_ref, state):
        @pl.when(pl.program_id(0) == 0)
        def _():
            state[...] = jnp.zeros(state.shape, f32)
        s0 = state[...]
        st_ref[0] = s0
        outs, s1 = _delta_chunks(qkv_ref[...], bg_ref[...], s0, False)
        for i, o in enumerate(outs):
            for b in range(B):
                for h in range(DNH):
                    o_ref[b, i * CH:(i + 1) * CH, h * DND:(h + 1) * DND] = o[b * DNH + h]
        state[...] = s1

    at = lambda c: lambda: pl.program_id(0) == c
    return pl.pallas_call(
        _ride(body, 2, 2, 1, comm, at(0), at((7 * steps) // 8), at(steps - 1)), name="delta_fwd", grid=(steps,),
        in_specs=[pl.BlockSpec((B, rows, 3 * DNH * DND), lambda c: (0, c, 0)), pl.BlockSpec((B, rows, LANE), lambda c: (0, c, 0))] + comm.specs,
        out_specs=[pl.BlockSpec((B, rows, DNH * DND), lambda c: (0, c, 0)), pl.BlockSpec((1, G, DND, DND), lambda c: (c, 0, 0, 0))] + comm.specs,
        out_shape=[jax.ShapeDtypeStruct((B, S, DNH * DND), f32), jax.ShapeDtypeStruct((steps, G, DND, DND), f32)] + comm.out_shape,
        scratch_shapes=[pltpu.VMEM((G, DND, DND), f32)] + comm.scratch, compiler_params=_cp(("arbitrary",)),
    )(qkvn, bg, *comm.arrs)


def delta_bwd(qkvn, bg, states, do, comm):
    B, S, _ = qkvn.shape
    steps, G = states.shape[0], B * DNH
    rows = S // steps
    n = rows // CH

    def body(qkv_ref, bg_ref, st_ref, do_ref, dqkv_ref, dbg_ref, dstate):
        @pl.when(pl.program_id(0) == 0)
        def _():
            dstate[...] = jnp.zeros(dstate.shape, f32)
        _, vjp = jax.vjp(lambda a, g, s: _delta_chunks(a, g, s, True), qkv_ref[...], bg_ref[...], st_ref[0])
        do = [_stack([do_ref[b, i * CH:(i + 1) * CH, h * DND:(h + 1) * DND] for b in range(B) for h in range(DNH)]) for i in range(n)]
        dqkv, dbg, ds = vjp((do, dstate[...]))
        dqkv_ref[...] = dqkv
        dbg_ref[...] = dbg
        dstate[...] = ds

    rev = lambda c: steps - 1 - c
    at = lambda c: lambda: pl.program_id(0) == c
    return pl.pallas_call(
        _ride(body, 4, 2, 1, comm, at(0), at(steps - 1), at(steps - 1)), name="delta_bwd", grid=(steps,),
        in_specs=[pl.BlockSpec((B, rows, 3 * DNH * DND), lambda c: (0, rev(c), 0)), pl.BlockSpec((B, rows, LANE), lambda c: (0, rev(c), 0)),
                  pl.BlockSpec((1, G, DND, DND), lambda c: (rev(c), 0, 0, 0)),
                  pl.BlockSpec((B, rows, DNH * DND), lambda c: (0, rev(c), 0))] + comm.specs,
        out_specs=[pl.BlockSpec((B, rows, 3 * DNH * DND), lambda c: (0, rev(c), 0)),
                   pl.BlockSpec((B, rows, LANE), lambda c: (0, rev(c), 0))] + comm.specs,
        out_shape=[jax.ShapeDtypeStruct((B, S, 3 * DNH * DND), f32), jax.ShapeDtypeStruct((B, S, LANE), f32)] + comm.out_shape,
        scratch_shapes=[pltpu.VMEM((G, DND, DND), f32)] + comm.scratch, compiler_params=_cp(("arbitrary",)),
    )(qkvn, bg, states, do, *comm.arrs)


GELU_C0, GELU_C1 = math.sqrt(2.0 / math.pi), 0.044715


def _ffn_specs(S):
    nblk = DFF // LANE
    return [pl.BlockSpec((1, S, LANE), lambda i, b: (b, 0, i)), pl.BlockSpec((1, S, LANE), lambda i, b: (b, 0, nblk + i)),
            pl.BlockSpec((FK, LANE), lambda i, b: (0, i)), pl.BlockSpec((FK, LANE), lambda i, b: (0, nblk + i))]


def ffnconv_fwd(up, conv_w):
    B, S, _ = up.shape
    rows = min(FFN_ROWS, S)

    def body(g_ref, v_ref, gw_ref, vw_ref, o_ref, xg, xv):
        _stage_rows(xg, g_ref[0].astype(f32))
        _stage_rows(xv, v_ref[0].astype(f32))
        gw, vw = gw_ref[...], vw_ref[...]
        for r in range(0, S, rows):
            g, _ = _conv_rows(xg, gw, FK, r, rows)
            v, _ = _conv_rows(xv, vw, FK, r, rows)
            t = jnp.tanh(GELU_C0 * (g * (1.0 + GELU_C1 * (g * g))))
            o_ref[0, pl.ds(r, rows), :] = (0.5 * g * (1.0 + t) * v).astype(o_ref.dtype)

    return pl.pallas_call(
        body, name="ffnconv_fwd", grid=(DFF // LANE, B), in_specs=_ffn_specs(S),
        out_specs=pl.BlockSpec((1, S, LANE), lambda i, b: (b, 0, i)), out_shape=jax.ShapeDtypeStruct((B, S, DFF), bf16),
        scratch_shapes=[pltpu.VMEM((S + 8, LANE), f32)] * 2, compiler_params=_cp(("parallel", "parallel")),
    )(up, up, conv_w, conv_w)


def ffnconv_bwd(up, conv_w, dact, comm):
    B, S, _ = up.shape
    rows = min(FFN_ROWS, S)

    def body(g_ref, v_ref, gw_ref, vw_ref, dy_ref, dx_ref, dw_ref, xg, xv, dg, dv):
        _stage_rows(xg, g_ref[0].astype(f32))
        _stage_rows(xv, v_ref[0].astype(f32))
        gw, vw = gw_ref[...], vw_ref[...]
        dgw = [jnp.zeros((8, LANE), f32) for _ in range(FK)]
        dvw = [jnp.zeros((8, LANE), f32) for _ in range(FK)]
        for r in range(0, S, rows):
            g, gwins = _conv_rows(xg, gw, FK, r, rows)
            v, vwins = _conv_rows(xv, vw, FK, r, rows)
            g2 = g * g
            t = jnp.tanh(GELU_C0 * (g * (1.0 + GELU_C1 * g2)))
            half = 0.5 * (1.0 + t)
            dgelu = half + (0.5 * GELU_C0) * g * (1.0 - t * t) * (1.0 + (3.0 * GELU_C1) * g2)
            dy = dy_ref[0, pl.ds(r, rows), :].astype(f32)
            dvc = dy * (g * half)
            dgc = dy * v * dgelu
            dg[pl.ds(r, rows), :] = dgc
            dv[pl.ds(r, rows), :] = dvc
            for j in range(FK):
                dgw[j] = dgw[j] + _fold8(dgc * gwins[j])
                dvw[j] = dvw[j] + _fold8(dvc * vwins[j])
        dg[S:S + 8] = jnp.zeros((8, LANE), f32)
        dv[S:S + 8] = jnp.zeros((8, LANE), f32)
        for r in range(0, S, rows):
            dx_ref[0, 0, pl.ds(r, rows), :] = _conv_rows_t(dg, gw, FK, r, rows).astype(dx_ref.dtype)
            dx_ref[1, 0, pl.ds(r, rows), :] = _conv_rows_t(dv, vw, FK, r, rows).astype(dx_ref.dtype)

        @pl.when(pl.program_id(1) == 0)
        def _():
            dw_ref[...] = jnp.zeros(dw_ref.shape, f32)
        dw_ref[0] += jnp.concatenate([jnp.sum(d, axis=0, keepdims=True) for d in dgw], axis=0)
        dw_ref[1] += jnp.concatenate([jnp.sum(d, axis=0, keepdims=True) for d in dvw], axis=0)

    nblk = DFF // LANE
    at = lambda i, b: lambda: (pl.program_id(0) == i) & (pl.program_id(1) == b)
    return pl.pallas_call(
        _ride(body, 5, 2, 4, comm, at(0, 0), at(nblk - 1, B - 1), at(nblk - 1, B - 1)), name="ffnconv_bwd", grid=(nblk, B),
        in_specs=_ffn_specs(S) + [pl.BlockSpec((1, S, LANE), lambda i, b: (b, 0, i))] + comm.specs,
        out_specs=[pl.BlockSpec((2, 1, S, LANE), lambda i, b: (0, b, 0, i)),
                   pl.BlockSpec((2, FK, LANE), lambda i, b: (0, 0, i))] + comm.specs,
        out_shape=[jax.ShapeDtypeStruct((2, B, S, DFF), bf16), jax.ShapeDtypeStruct((2, FK, DFF), f32)] + comm.out_shape,
        scratch_shapes=[pltpu.VMEM((S + 8, LANE), f32)] * 4 + comm.scratch, compiler_params=_cp(("arbitrary", "arbitrary")),
    )(up, up, conv_w, conv_w, dact, *comm.arrs)


def ada_fwd(c_all, ada_w, ada_b):
    def body(c_ref, w_ref, b_ref, o_ref):
        c = c_ref[...]
        act = (c * jax.nn.sigmoid(c)).astype(bf16)
        o_ref[...] = jnp.dot(act, w_ref[...].astype(bf16), preferred_element_type=f32) + b_ref[...]

    return pl.pallas_call(body, name="ada_fwd", out_shape=jax.ShapeDtypeStruct((c_all.shape[0], ada_w.shape[1]), f32),
                          compiler_params=pltpu.CompilerParams(vmem_limit_bytes=VMEM_LIMIT))(c_all, ada_w, ada_b)


def ada_bwd(c_all, dmod):
    def body(c_ref, d_ref, o_ref):
        c = c_ref[...]
        act = (c * jax.nn.sigmoid(c)).astype(bf16)
        o_ref[...] = lax.dot_general(act, d_ref[...].astype(bf16), (((0,), (0,)), ((), ())), preferred_element_type=f32)

    return pl.pallas_call(body, name="ada_bwd", out_shape=jax.ShapeDtypeStruct((c_all.shape[1], dmod.shape[1]), f32),
                          compiler_params=pltpu.CompilerParams(vmem_limit_bytes=VMEM_LIMIT))(c_all, dmod)


def loss_head(h1, y2, target, g2, w):
    def fn(t, b, c):
        h, y, tg = [v.astype(f32) for v in t]

        def loss_fn(h, y, g, w):
            e = h + g * _rms(y, w) - tg
            return 0.5 * jnp.sum(jnp.mean(e * e, axis=-1))

        loss, grads = jax.value_and_grad(loss_fn, argnums=(0, 1, 2, 3))(h, y, b[0], c[0])
        return [grads[0], grads[1]], [grads[2], grads[3], jnp.full((1, LANE), loss, f32)]

    return rowcall("loss_head", fn, [(h1, D, 0), (y2, D, 0), (target, D, 0)], [g2], [w], [(D, f32), (D, bf16)],
                   [(1, D), (1, D), (1, LANE)])


def adamw(w, gparts, m, v, name):
    R, C = w.shape
    P = gparts.shape[0]
    budget = 2 * 1024 * 1024
    tr, tc = R, C
    if R * C * 4 > budget and R % 8 == 0:
        tr = max(t for t in range(8, R + 1, 8) if R % t == 0 and t * C * 4 <= budget)
    elif R * C * 4 > budget:
        tc = max(t for t in range(LANE, C + 1, LANE) if C % t == 0 and R * t * 4 <= budget)

    def body(w_ref, g_ref, m_ref, v_ref, go, do, mo, vo):
        g = g_ref[0].astype(f32)
        for p in range(1, P):
            g = g + g_ref[p].astype(f32)
        m2 = B1 * m_ref[...] + (1.0 - B1) * g
        v2 = B2 * v_ref[...] + (1.0 - B2) * jnp.square(g)
        m_hat = m2 * (1.0 / (1.0 - B1 ** STEP))
        v_hat = v2 * (1.0 / (1.0 - B2 ** STEP))
        go[...] = g
        do[...] = -LR * (m_hat / (jnp.sqrt(v_hat) + EPS) + WD * w_ref[...])
        mo[...] = m2
        vo[...] = v2

    blk = pl.BlockSpec((tr, tc), lambda i, j: (i, j))
    return pl.pallas_call(
        body, name=name, grid=(R // tr, C // tc), in_specs=[blk, pl.BlockSpec((P, tr, tc), lambda i, j: (0, i, j)), blk, blk],
        out_specs=[blk] * 4, out_shape=[jax.ShapeDtypeStruct((R, C), f32)] * 4, compiler_params=_cp(("parallel", "parallel")),
    )(w, gparts, m, v)


def _pack_w_in(wt):
    aq, ak, av, dqkv, dz, dbeta, da, ga, gd = jnp.split(wt, np.cumsum(IN_SPLITS)[:-1].tolist(), axis=0)
    ba = jnp.pad(jnp.concatenate([dbeta, da], axis=0), ((0, LANE - 2 * DNH), (0, 0)))
    return jnp.concatenate([ga, gd, aq, dqkv, dz, ak, av, ba], axis=0)


def _unpack_w_in(p):
    row = lambda cb, n: p[cb * LANE: cb * LANE + n]
    ba = row(CB_BA, 2 * DNH)
    return jnp.concatenate([row(CB_AQ, HQ * HD), row(CB_AK, HKV * HD), row(CB_AV, HKV * HD), row(CB_DQKV, 3 * DNH * DND),
                            row(CB_DZ, DNH * DND), ba[:DNH], ba[DNH:], row(CB_GA, D), row(CB_GD, D)], axis=0)


def _cols_gathered(g):
    return g.transpose(1, 0, 2).reshape(g.shape[1], NDEV * g.shape[2])


def _cols_split(w):
    r = w.shape[0]
    return w.reshape(r, NDEV, w.shape[1] // NDEV).transpose(1, 0, 2)


def kernel(x, c, ada_w, ada_b, norm_mix_pre, norm_mix_post, norm_ffn_pre, norm_ffn_post, w_in, dn_conv_w, dn_a_log, dn_dt_bias, dn_norm_w, attn_sinks, rel_bias, w_attn_branch, w_dn_branch, w_out, ffn_w_up, ffn_conv_w, ffn_w_down, loss_target, m_ada_w, m_ada_b, m_norm_mix_pre, m_norm_mix_post, m_norm_ffn_pre, m_norm_ffn_post, m_w_in, m_dn_conv_w, m_dn_a_log, m_dn_dt_bias, m_dn_norm_w, m_attn_sinks, m_rel_bias, m_w_attn_branch, m_w_dn_branch, m_w_out, m_ffn_w_up, m_ffn_conv_w, m_ffn_w_down, v_ada_w, v_ada_b, v_norm_mix_pre, v_norm_mix_post, v_norm_ffn_pre, v_norm_ffn_post, v_w_in, v_dn_conv_w, v_dn_a_log, v_dn_dt_bias, v_dn_norm_w, v_attn_sinks, v_rel_bias, v_w_attn_branch, v_w_dn_branch, v_w_out, v_ffn_w_up, v_ffn_conv_w, v_ffn_w_down):
    B, S, _ = x.shape
    T = B * S
    me = 4 * lax.axis_index("x") + 2 * lax.axis_index("y") + lax.axis_index("c")
    big = dict(w_in=w_in, dn_conv_w=dn_conv_w, w_attn_branch=w_attn_branch, w_dn_branch=w_dn_branch, w_out=w_out,
               ffn_w_up=ffn_w_up, ffn_conv_w=ffn_conv_w, ffn_w_down=ffn_w_down)
    big_names = list(big)

    first, mid, late = ["w_in", "dn_conv_w"], ["w_attn_branch", "w_dn_branch", "w_out"], ["ffn_w_up", "ffn_conv_w"]
    transposed = ("w_in", "ffn_w_up")
    local = lambda n, a: a[0].T if n in transposed else a[0]
    shard = lambda names: [local(n, big[n]).astype(bf16) for n in names]
    *got, c_all = _exchange(shard(first) + [c], "gather_w_in", two_level=True)
    gw = dict(zip(first, got))
    c_all = c_all.reshape(NDEV * B, D)

    wp = _pack_w_in(gw["w_in"].reshape(IN_DIM, D))
    conv_dn = _cols_gathered(gw["dn_conv_w"]).astype(f32)

    ncol = ada_w.shape[2]
    ada_b_mine = lax.dynamic_slice_in_dim(ada_b, me * ncol, ncol, axis=1)
    mod_cols = ada_fwd(c_all, ada_w[0], ada_b_mine)
    (mod_g,) = _exchange([mod_cols], "gather_mod")
    mod = lax.dynamic_slice_in_dim(mod_g, me * B, B, axis=1).transpose(1, 0, 2).reshape(B, NMOD * D)
    sh1, sc1, g1, sh2, sc2, g2 = [mod[:, i * D:(i + 1) * D].reshape(B, 1, D) for i in range(NMOD)]

    onehot = (jnp.asarray(_bucket_table()).reshape(1, -1) == jnp.arange(NBUCK, dtype=jnp.int32)[:, None]).astype(f32)
    bias = mm(rel_bias.T, onehot, "nn", f32, "bias_table", tn=8192, precision=HI).reshape(HQ, WIN, 2 * WIN)
    sinks = attn_sinks.reshape(HQ, 1, 1)
    a_log_pad = jnp.pad(dn_a_log, ((0, 0), (DNH, LANE - 2 * DNH)))
    dt_bias_pad = jnp.pad(dn_dt_bias, ((0, 0), (DNH, LANE - 2 * DNH)))

    (u1,) = rowcall_fwd("mix_pre", f_rms_mod, [(x, D, 0)], [sc1, sh1], [norm_mix_pre], [(D, bf16)])
    proj, gw["ffn_w_down"] = mm(u1.reshape(T, D), wp, "nt", bf16, "proj", tm=512, tn=CB_BA * LANE, b_cols=(0, 1),
                                comm=_Comm(shard(["ffn_w_down"]), two_level=True))
    proj = proj.reshape(B, S, CB_BA * LANE)
    ba = mm(u1.reshape(T, D), wp, "nt", f32, "proj_ba", tn=LANE, b_cols=(CB_BA, 1)).reshape(B, S, LANE)
    ya, *got = attn_fwd(proj, bias, sinks, _Comm(shard(mid), two_level=True))
    gw.update(zip(mid, got))
    wa = _cols_gathered(gw["w_attn_branch"])
    wd = _cols_gathered(gw["w_dn_branch"])
    wo = gw["w_out"].reshape(D, D)
    qkvn = dnconv_fwd(proj, conv_dn)
    (bg,) = rowcall_fwd("dn_gate", f_gate, [(ba, LANE, 0)], [], [a_log_pad, dt_bias_pad], [(LANE, f32)])
    o_dn, states, *got = delta_fwd(qkvn, bg, _Comm(shard(late), two_level=True))
    gw.update(zip(late, got))
    wup = gw["ffn_w_up"].reshape(2 * DFF, D)
    conv_ffn = _cols_gathered(gw["ffn_conv_w"]).astype(f32)
    wdown = gw["ffn_w_down"].reshape(DFF, D)
    (yd,) = rowcall_fwd("dn_out", f_dnout, [(o_dn, DNH * DND, 0), (proj, DNH * DND, CB_DZ // 4)], [], [dn_norm_w], [(DNH * DND, bf16)])
    pa = mm(ya.reshape(T, HQ * HD), wa, "nn", bf16, "attn_branch").reshape(B, S, D)
    pd = mm(yd.reshape(T, DNH * DND), wd, "nn", bf16, "dn_branch").reshape(B, S, D)
    merge_tok = [(proj, D, CB_GA // 8), (proj, D, CB_GD // 8), (pa, D, 0), (pd, D, 0)]
    (merged,) = rowcall_fwd("merge", f_merge, merge_tok, [], [], [(D, bf16)])
    y1 = mm(merged.reshape(T, D), wo, "nn", bf16, "mix_out").reshape(B, S, D)
    post_pre = ([(x, D, 0), (y1, D, 0)], [g1, sc2, sh2], [norm_mix_post, norm_ffn_pre])
    h1, u2 = rowcall_fwd("mix_post_ffn_pre", f_post_pre, *post_pre, [(D, f32), (D, bf16)])
    up = mm(u2.reshape(T, D), wup, "nt", bf16, "ffn_up", tn=2816).reshape(B, S, 2 * DFF)
    act = ffnconv_fwd(up, conv_ffn)
    y2 = mm(act.reshape(T, DFF), wdown, "nn", bf16, "ffn_down", tk=2816).reshape(B, S, D)

    dh1_a, dy2, dg2, dw_ffn_post, loss_b = loss_head(h1, y2, loss_target, g2, norm_ffn_post)
    dy2f = dy2.reshape(T, D)
    dact = mm(dy2f, wdown, "nt", bf16, "ffn_down_dx", tn=2816).reshape(B, S, DFF)
    g_wdown = mm(act.reshape(T, DFF), dy2f, "tn", bf16, "ffn_down_dw", tm=2816, tn=512, tk=4096)
    in_flight = []

    def send_off(d, tag):
        in_flight.append((d, _scatter_start([a.astype(bf16) for a in d.values()], "scatter_" + tag + "_start")))
        return in_flight[-1][1][-1][0, 0]

    send_off(dict(ffn_w_down=g_wdown.reshape(NDEV, DFF // NDEV, D)), "ffn_down")
    dup, g_conv_ffn = ffnconv_bwd(up, conv_ffn, dact, _NoComm())
    dupf = dup.reshape(2, T, DFF)
    g_conv_ffn = g_conv_ffn.transpose(1, 0, 2).reshape(FK, 2 * DFF)
    du2 = mm(dupf, wup, "nn", bf16, "ffn_up_dx", tk=2816).reshape(B, S, D)
    g_wup = mm(dupf, u2.reshape(T, D), "tn", bf16, "ffn_up_dw", tm=1408, tk=2048)
    send_off(dict(ffn_w_up=g_wup.reshape(NDEV, 2 * DFF // NDEV, D), ffn_conv_w=_cols_split(g_conv_ffn)), "ffn_up")
    dh1, dy1, dg1, dsc2, dsh2, dw_mix_post, dw_ffn_pre = rowcall_bwd(
        "mix_post_ffn_pre_bwd", f_post_pre, *post_pre, [(dh1_a, D, 0), (du2, D, 0)], [(0, f32), (1, bf16)])
    dy1f = dy1.reshape(T, D)
    dmerged = mm(dy1f, wo, "nt", bf16, "mix_out_dx").reshape(B, S, D)
    g_wo = mm(merged.reshape(T, D), dy1f, "tn", bf16, "mix_out_dw", tk=2048)
    dproj = lax.empty((B, S, NP), bf16)
    dproj, dpa, dpd = rowcall_bwd("merge_bwd", f_merge, merge_tok, [], [], [(dmerged, D, 0)],
                                  [(0, bf16), (1, bf16), (2, bf16), (3, bf16)], join_first=2, into=(dproj, CB_GA // 16))
    dpaf, dpdf = dpa.reshape(T, D), dpd.reshape(T, D)
    dya = mm(dpaf, wa, "nt", bf16, "attn_branch_dx").reshape(B, S, HQ * HD)
    g_wa = mm(ya.reshape(T, HQ * HD), dpaf, "tn", bf16, "attn_branch_dw", tk=2048)
    dyd = mm(dpdf, wd, "nt", bf16, "dn_branch_dx").reshape(B, S, DNH * DND)
    g_wd = mm(yd.reshape(T, DNH * DND), dpdf, "tn", bf16, "dn_branch_dw", tk=2048)
    dproj, do_dn, dw_dn_norm = rowcall_bwd("dn_out_bwd", f_dnout, [(o_dn, DNH * DND, 0), (proj, DNH * DND, CB_DZ // 4)], [], [dn_norm_w],
                                           [(dyd, DNH * DND, 0)], [(1, bf16), (0, f32)], into=(dproj, CB_DZ // 4))
    send_off(dict(w_attn_branch=_cols_split(g_wa), w_dn_branch=_cols_split(g_wd), w_out=g_wo.reshape(NDEV, D // NDEV, D)), "branches")
    dqkvn, dbg = delta_bwd(qkvn, bg, states, do_dn, _NoComm())
    dproj, da_log_pad, ddt_bias_pad = rowcall_bwd("dn_gate_bwd", f_gate, [(ba, LANE, 0)], [], [a_log_pad, dt_bias_pad],
                                                  [(dbg, LANE, 0)], [(0, bf16)], into=(dproj, CB_BA))
    dproj, g_conv_dn = dnconv_bwd(proj, conv_dn, dqkvn, dproj)
    dproj, dk, dv, dbias, dsinks = attn_bwd(proj, bias, sinks, dya, dproj, _NoComm())
    dproj = lax.dynamic_update_slice(dproj, jnp.concatenate([dk, dv], axis=2), (0, 0, CB_AK * LANE)).reshape(T, NP)
    g_wp = mm(dproj, u1.reshape(T, D), "tn", bf16, "proj_dw", tm=1664, tk=1024)
    started = send_off(dict(w_in=_unpack_w_in(g_wp).reshape(NDEV, IN_DIM // NDEV, D), dn_conv_w=_cols_split(g_conv_dn)), "w_in")
    du1 = mm(dproj, wp, "nn", bf16, "proj_dx", tm=512, tk=NP).reshape(B, S, D)
    grad_x, dsc1, dsh1, dw_mix_pre = rowcall_bwd("mix_pre_bwd", f_rms_mod, [(x, D, 0)], [sc1 + started, sh1], [norm_mix_pre],
                                                 [(du1, D, 0)], [(0, f32)], add=(dh1, D, 0))
    g_rel = mm(dbias.reshape(HQ, WIN * 2 * WIN), onehot, "nt", f32, "rel_bias_dw", tk=8192, precision=HI)

    dmod = jnp.concatenate([dsh1, dsc1, dg1, dsh2, dsc2, dg2], axis=2).reshape(B, NMOD * D)

    zrow = lambda a: jnp.concatenate([a.reshape(1, -1), jnp.zeros((B - 1, a.size), f32)], axis=0)
    small_g = jnp.concatenate([
        dmod, dw_mix_pre.reshape(B, D), dw_mix_post.reshape(B, D), dw_ffn_pre.reshape(B, D), dw_ffn_post.reshape(B, D),
        da_log_pad.reshape(B, LANE)[:, DNH:2 * DNH], ddt_bias_pad.reshape(B, LANE)[:, DNH:2 * DNH], dw_dn_norm.reshape(B, DND),
        zrow(dsinks), zrow(g_rel.T), loss_b.reshape(B, LANE)[:, :1], jnp.zeros((B, SMALL_PAD - SMALL_N - 1), f32)], axis=1)
    (small_all,) = _exchange([small_g], "gather_small")
    dmod_cols = lax.dynamic_slice_in_dim(small_all.reshape(NDEV * B, SMALL_PAD), me * ncol, ncol, axis=1)
    g_ada_w = ada_bwd(c_all, dmod_cols)
    parts = {}
    for i, (d, started) in enumerate(in_flight):
        landed = _scatter_finish(started, len(d), g_ada_w, "scatter_finish_%d" % i)
        for nme, src, got in zip(d, started[2:2 + len(d)], landed):
            parts[nme] = lax.dynamic_update_slice_in_dim(got, lax.dynamic_slice_in_dim(src, me, 1, axis=0), me, axis=0)
    small_w = dict(ada_b=(ada_b, m_ada_b, v_ada_b), norm_mix_pre=(norm_mix_pre, m_norm_mix_pre, v_norm_mix_pre),
                   norm_mix_post=(norm_mix_post, m_norm_mix_post, v_norm_mix_post), norm_ffn_pre=(norm_ffn_pre, m_norm_ffn_pre, v_norm_ffn_pre),
                   norm_ffn_post=(norm_ffn_post, m_norm_ffn_post, v_norm_ffn_post), dn_a_log=(dn_a_log, m_dn_a_log, v_dn_a_log),
                   dn_dt_bias=(dn_dt_bias, m_dn_dt_bias, v_dn_dt_bias), dn_norm_w=(dn_norm_w, m_dn_norm_w, v_dn_norm_w),
                   attn_sinks=(attn_sinks, m_attn_sinks, v_attn_sinks), rel_bias=(rel_bias, m_rel_bias, v_rel_bias))

    def pack(i, fill):
        row = jnp.concatenate([small_w[n][i].reshape(1, -1) for n, _ in SMALL], axis=1)
        return jnp.pad(row, ((0, 0), (0, SMALL_PAD - SMALL_N)), constant_values=fill)

    small_out = adamw(pack(0, 0.0), small_all.reshape(NDEV * B, 1, SMALL_PAD), pack(1, 0.0), pack(2, 1.0), "adamw_small")
    loss = small_out[0][0, SMALL_N]

    res = {}
    off = 0
    for n, size in SMALL:
        shp = small_w[n][0].shape
        res[n] = [o[:, off:off + size].reshape(shp) for o in small_out]
        off += size
    res["ada_w"] = [o[None] for o in adamw(ada_w[0], g_ada_w[None], m_ada_w[0], v_ada_w[0], "adamw_ada_w")]
    moments = dict(w_in=(m_w_in, v_w_in), dn_conv_w=(m_dn_conv_w, v_dn_conv_w), w_attn_branch=(m_w_attn_branch, v_w_attn_branch),
                   w_dn_branch=(m_w_dn_branch, v_w_dn_branch), w_out=(m_w_out, v_w_out), ffn_w_up=(m_ffn_w_up, v_ffn_w_up),
                   ffn_conv_w=(m_ffn_conv_w, v_ffn_conv_w), ffn_w_down=(m_ffn_w_down, v_ffn_w_down))
    for n in big_names:
        outs = adamw(local(n, big[n]), parts[n], local(n, moments[n][0]), local(n, moments[n][1]), "adamw_" + n)
        res[n] = [(o.T if n in transposed else o)[None] for o in outs]

    order = ["ada_w", "ada_b", "norm_mix_pre", "norm_mix_post", "norm_ffn_pre", "norm_ffn_post", "w_in", "dn_conv_w", "dn_a_log",
             "dn_dt_bias", "dn_norm_w", "attn_sinks", "rel_bias", "w_attn_branch", "w_dn_branch", "w_out", "ffn_w_up", "ffn_conv_w",
             "ffn_w_down"]
    return (loss, grad_x, *[res[n][0] for n in order], *[res[n][1] for n in order], *[res[n][2] for n in order],
            *[res[n][3] for n in order])
```

```python
import functools
import math

import numpy as np
import jax
import jax.numpy as jnp
from jax import lax
from jax.experimental import pallas as pl
from jax.experimental.pallas import tpu as pltpu

f32 = jnp.float32
bf16 = jnp.bfloat16
HI = lax.Precision.HIGHEST
MID = lax.Precision.HIGH
MESH = pl.DeviceIdType.MESH

NDEV = 8
D = 1024
HQ, HKV, HD, WIN, NBUCK, MAXDIST = 8, 2, 64, 128, 32, 128
DNH, DND, DNK, CH = 4, 128, 4, 64
DFF, FK = 2816, 3
NMOD = 6
RMS_EPS = 1e-6
L2_EPS = 1e-6
NEG_INF = -1e30
LR, B1, B2, EPS, WD, STEP = 0.001, 0.9, 0.999, 1e-08, 0.01, 10

LANE = 128
CB_GA, CB_GD, CB_AQ, CB_DQKV, CB_DZ, CB_AK, CB_AV, CB_BA, NPB = 0, 8, 16, 20, 32, 36, 37, 38, 39
NP = NPB * LANE
IN_SPLITS = (HQ * HD, HKV * HD, HKV * HD, 3 * DNH * DND, DNH * DND, DNH, DNH, D, D)
IN_DIM = sum(IN_SPLITS)
VMEM_LIMIT = 56 * 1024 * 1024

SMALL = (("ada_b", NMOD * D), ("norm_mix_pre", D), ("norm_mix_post", D), ("norm_ffn_pre", D), ("norm_ffn_post", D),
         ("dn_a_log", DNH), ("dn_dt_bias", DNH), ("dn_norm_w", DND), ("attn_sinks", HQ), ("rel_bias", NBUCK * HQ))
SMALL_N = sum(n for _, n in SMALL)
SMALL_PAD = 10752


def _cp(sem):
    return pltpu.CompilerParams(dimension_semantics=sem, vmem_limit_bytes=VMEM_LIMIT)


def _pick(dim, target):
    if dim <= target:
        return dim
    best = None
    for d in range(LANE, target + 1, LANE):
        if dim % d == 0:
            best = d
    assert best is not None, (dim, target)
    return best


def _me():
    x, y, c = lax.axis_index("x"), lax.axis_index("y"), lax.axis_index("c")
    return x, y, c, 4 * x + 2 * y + c


def _peer(x, y, c, k):
    px = 1 - x if k & 4 else x
    py = 1 - y if k & 2 else y
    pc = 1 - c if k & 1 else c
    return (px, py, pc), 4 * px + 2 * py + pc


class _Comm:
    def __init__(self, arrs, scatter=False, two_level=False):
        assert not (scatter and two_level)
        self.arrs, self.n, self.scatter, self.two_level = list(arrs), len(arrs), scatter, two_level
        if scatter:
            self.out_shape = [jax.ShapeDtypeStruct(a.shape, a.dtype) for a in arrs]
        else:
            self.out_shape = [jax.ShapeDtypeStruct((NDEV,) + a.shape, a.dtype) for a in arrs]
        nsem = self.n * (NDEV - 1)
        self.scratch = [pltpu.SemaphoreType.DMA((nsem,)), pltpu.SemaphoreType.DMA((nsem,)), pltpu.SemaphoreType.DMA((self.n,))]
        self.specs = [pl.BlockSpec(memory_space=pl.ANY)] * self.n

    def phases(self, ins, out, send, recv, loc):
        x, y, c, me = _me()

        def remote(a, k, src, dst, to):
            s = a * (NDEV - 1) + k - 1
            return pltpu.make_async_remote_copy(src_ref=src, dst_ref=dst, send_sem=send.at[s], recv_sem=recv.at[s],
                                                device_id=to, device_id_type=MESH)

        def local(a):
            return pltpu.make_async_copy(ins[a].at[me] if self.scatter else ins[a], out[a].at[me], loc.at[a])

        if not self.two_level:
            def mine(a, k):
                peer, pid = _peer(x, y, c, k)
                return remote(a, k, ins[a].at[pid] if self.scatter else ins[a], out[a].at[me], peer)

            def theirs(a, k):
                peer, pid = _peer(x, y, c, k)
                return remote(a, k, ins[a].at[pid] if self.scatter else ins[a], out[a].at[pid], peer)

            def start():
                for a in range(self.n):
                    local(a).start()
                    for k in range(1, NDEV):
                        mine(a, k).start()

            def forward():
                pass

            def finish():
                for a in range(self.n):
                    for k in range(1, NDEV):
                        mine(a, k).wait_send()
                    for k in range(1, NDEV):
                        theirs(a, k).wait_recv()
                    local(a).wait()

            return start, forward, finish

        sibling = (x, y, 1 - c)
        chips = [(1 - x, y), (x, 1 - y), (1 - x, 1 - y)]
        slot = lambda px, py, pc: 4 * px + 2 * py + pc

        def own(a, k, to):
            return remote(a, k, ins[a], out[a].at[me], to)

        def landed(a, k, frm):
            return remote(a, k, ins[a], out[a].at[slot(*frm)], frm)

        def passed(a, j):
            rows = out[a].at[slot(*chips[j], c)]
            return remote(a, 5 + j, rows, rows, sibling)

        def start():
            for a in range(self.n):
                local(a).start()
                own(a, 1, sibling).start()
                for j, chip in enumerate(chips):
                    own(a, 2 + j, (*chip, c)).start()

        def forward():
            for a in range(self.n):
                for j, chip in enumerate(chips):
                    landed(a, 2 + j, (*chip, c)).wait_recv()
                    passed(a, j).start()

        def finish():
            for a in range(self.n):
                landed(a, 1, sibling).wait_recv()
                for j, chip in enumerate(chips):
                    remote(a, 5 + j, ins[a], out[a].at[slot(*chip, 1 - c)], sibling).wait_recv()
                own(a, 1, sibling).wait_send()
                for j, chip in enumerate(chips):
                    own(a, 2 + j, (*chip, c)).wait_send()
                    passed(a, j).wait_send()
                local(a).wait()

        return start, forward, finish


class _NoComm:
    n, arrs, out_shape, specs, scratch = 0, [], [], [], []

    def phases(self, *_):
        return (lambda: None,) * 3


def _ride(body, n_in, n_out, n_scr, comm, first, mid, last):
    k = comm.n

    def wrapped(*refs):
        ins, cins = refs[:n_in], refs[n_in:n_in + k]
        o0 = n_in + k
        outs, couts = refs[o0:o0 + n_out], refs[o0 + n_out:o0 + n_out + k]
        s0 = o0 + n_out + k
        scr, sems = refs[s0:s0 + n_scr], refs[s0 + n_scr:]
        start, forward, finish = comm.phases(cins, couts, *sems)
        pl.when(first())(start)
        body(*ins, *outs, *scr)
        pl.when(mid())(forward)
        pl.when(last())(finish)

    return wrapped


def _scatter_start(arrs, name):
    n = len(arrs)

    def body(*refs):
        ins, lands, send, recv, token = refs[:n], refs[n:2 * n], refs[2 * n], refs[2 * n + 1], refs[-1]
        x, y, c, me = _me()
        for a in range(n):
            for k in range(1, NDEV):
                peer, pid = _peer(x, y, c, k)
                s = a * (NDEV - 1) + k - 1
                pltpu.make_async_remote_copy(src_ref=ins[a].at[pid], dst_ref=lands[a].at[me], send_sem=send.at[s], recv_sem=recv.at[s],
                                             device_id=peer, device_id_type=MESH).start()
        token[...] = jnp.zeros(token.shape, token.dtype)

    hbm, sem = pl.BlockSpec(memory_space=pltpu.HBM), pl.BlockSpec(memory_space=pltpu.SEMAPHORE)
    nsem = n * (NDEV - 1)
    thru = [pltpu.HBM(a.shape, a.dtype) for a in arrs]
    return pl.pallas_call(
        body, name=name, in_specs=[hbm] * (2 * n),
        out_shape=(pltpu.SemaphoreType.DMA((nsem,)), pltpu.SemaphoreType.DMA((nsem,)), *thru, *thru, jax.ShapeDtypeStruct((8, LANE), f32)),
        out_specs=(sem, sem, *[hbm] * (2 * n), pl.BlockSpec(memory_space=pltpu.VMEM)),
        input_output_aliases={i: 2 + i for i in range(2 * n)},
        compiler_params=pltpu.CompilerParams(has_side_effects=pltpu.SideEffectType.DATAFLOW_SIDE_EFFECTING),
    )(*[pltpu.with_memory_space_constraint(a, pltpu.HBM) for a in arrs],
      *[pltpu.with_memory_space_constraint(lax.empty(a.shape, a.dtype), pltpu.HBM) for a in arrs])


def _scatter_finish(started, n, after, name):
    send, recv, *rest = started
    srcs, lands = rest[:n], rest[n:2 * n]

    def body(*refs):
        ins, lnd, send_ref, recv_ref = refs[:n], refs[n:2 * n], refs[2 * n], refs[2 * n + 1]
        x, y, c, me = _me()
        for a in range(n):
            for k in range(1, NDEV):
                peer, pid = _peer(x, y, c, k)
                s = a * (NDEV - 1) + k - 1
                cp = pltpu.make_async_remote_copy(src_ref=ins[a].at[pid], dst_ref=lnd[a].at[pid], send_sem=send_ref.at[s],
                                                  recv_sem=recv_ref.at[s], device_id=peer, device_id_type=MESH)
                cp.wait_send()
                cp.wait_recv()

    hbm, sem = pl.BlockSpec(memory_space=pltpu.HBM), pl.BlockSpec(memory_space=pltpu.SEMAPHORE)
    thru = [pltpu.HBM(a.shape, a.dtype) for a in srcs]
    out = pl.pallas_call(
        body, name=name, in_specs=[hbm] * (2 * n) + [sem, sem, pl.BlockSpec(memory_space=pl.ANY)],
        out_shape=(*thru, *thru), out_specs=tuple([hbm] * (2 * n)), input_output_aliases={i: i for i in range(2 * n)},
        compiler_params=pltpu.CompilerParams(has_side_effects=pltpu.SideEffectType.DATAFLOW_SIDE_EFFECTING),
    )(*srcs, *lands, send, recv, after)
    return list(out[n:])


def _exchange(arrs, name, scatter=False, two_level=False):
    comm = _Comm(arrs, scatter, two_level)

    def body(*refs):
        start, forward, finish = comm.phases(refs[:comm.n], refs[comm.n:2 * comm.n], *refs[2 * comm.n:])
        start()
        forward()
        finish()

    return pl.pallas_call(body, name=name, out_shape=comm.out_shape, in_specs=comm.specs, out_specs=comm.specs,
                          scratch_shapes=comm.scratch, compiler_params=pltpu.CompilerParams(has_side_effects=True))(*arrs)


def mm(a, b, mode, out_dtype, name, tm=1024, tn=1024, tk=1024, precision=None, comm=None, b_cols=None):
    a_parts = a.shape[0] if a.ndim == 3 else 1
    b_parts = b.shape[0] if b.ndim == 3 else 1
    assert b_parts == 1 or mode == "tn"
    ash, bsh = (a.shape[-2], a.shape[-1] * a_parts), b.shape[-2:]
    if mode == "nn":
        (M, K), (K2, N) = ash, bsh
    elif mode == "nt":
        (M, K), (N, K2) = ash, bsh
    else:
        (K, M), (K2, N) = ash, (bsh[0], bsh[1] * b_parts)
    assert K == K2, (name, a.shape, b.shape)
    col0 = 0
    if b_cols is not None:
        assert mode in ("nn", "nt") and tn % LANE == 0
        col0, N = b_cols[0], b_cols[1] * tn
    if mode == "tn":
        tm, tn, tk = _pick(M // a_parts, tm), _pick(N // b_parts, tn), _pick(K, tk)
    else:
        tm, tn, tk = _pick(M, tm), _pick(N // b_parts, tn), _pick(K // a_parts, tk)
    nk = K // tk
    if mode == "tn" and a_parts > 1:
        per = M // tm // a_parts
        a_spec = pl.BlockSpec((None, tk, tm), lambda i, j, k: (i // per, k, i % per))
    elif mode == "tn":
        a_spec = pl.BlockSpec((tk, tm), lambda i, j, k: (k, i))
    elif a_parts > 1:
        per = nk // a_parts
        a_spec = pl.BlockSpec((None, tm, tk), lambda i, j, k: (k // per, i, k % per))
    else:
        a_spec = pl.BlockSpec((tm, tk), lambda i, j, k: (i, k))
    if mode == "nt":
        b_spec = pl.BlockSpec((tn, tk), lambda i, j, k: (col0 + j, k))
    elif b_parts > 1:
        per = N // tn // b_parts
        b_spec = pl.BlockSpec((None, tk, tn), lambda i, j, k: (j // per, k, j % per))
    else:
        b_spec = pl.BlockSpec((tk, tn), lambda i, j, k: (k, col0 + j))
    dims = {"nn": ((1,), (0,)), "nt": ((1,), (1,)), "tn": ((0,), (0,))}[mode]

    def body(a_ref, b_ref, o_ref, *scr):
        p = lax.dot_general(a_ref[...], b_ref[...], (dims, ((), ())), preferred_element_type=f32, precision=precision)
        if nk == 1:
            o_ref[...] = p.astype(o_ref.dtype)
        else:
            acc = scr[0]
            k = pl.program_id(2)

            @pl.when(k == 0)
            def _():
                acc[...] = p

            @pl.when(k > 0)
            def _():
                acc[...] += p

            @pl.when(k == nk - 1)
            def _():
                o_ref[...] = acc[...].astype(o_ref.dtype)

    grid = (M // tm, N // tn, nk)
    scratch = [pltpu.VMEM((tm, tn), f32)] if nk > 1 else []
    out_spec = pl.BlockSpec((tm, tn), lambda i, j, k: (i, j))
    out_shape = jax.ShapeDtypeStruct((M, N), out_dtype)
    if comm is None:
        return pl.pallas_call(body, name=name, grid=grid, in_specs=[a_spec, b_spec], out_specs=out_spec, out_shape=out_shape,
                              scratch_shapes=scratch, compiler_params=_cp(("parallel", "parallel", "arbitrary")))(a, b)
    at = lambda pos: lambda: functools.reduce(jnp.logical_and, [pl.program_id(d) == p for d, p in enumerate(pos)])
    end = tuple(g - 1 for g in grid)
    return pl.pallas_call(
        _ride(body, 2, 1, len(scratch), comm, at((0, 0, 0)), at(end), at(end)), name=name, grid=grid,
        in_specs=[a_spec, b_spec] + comm.specs, out_specs=[out_spec] + comm.specs, out_shape=[out_shape] + comm.out_shape,
        scratch_shapes=scratch + comm.scratch, compiler_params=_cp(("arbitrary", "arbitrary", "arbitrary")),
    )(a, b, *comm.arrs)


def rowcall(name, fn, tok, bat, con, tok_out, acc_out, ts=256, into=None):
    B, S = tok[0][0].shape[:2]
    ts = min(ts, S)
    nt, nb, nc, no, na = len(tok), len(bat), len(con), len(tok_out), len(acc_out)
    nin = nt + nb + nc + (1 if into is not None else 0)

    def body(*refs):
        tr, br, cr = refs[:nt], refs[nt:nt + nb], refs[nt + nb:nt + nb + nc]
        orf, arf = refs[nin:nin + no], refs[nin + no:]
        touts, aouts = fn([r[0] for r in tr], [r[0] for r in br], [r[...] for r in cr])
        for r, v in zip(orf, touts):
            r[0] = v.astype(r.dtype)
        s = pl.program_id(1)
        for r, v in zip(arf, aouts):
            @pl.when(s == 0)
            def _(r=r):
                r[...] = jnp.zeros(r.shape, r.dtype)
            r[0] += v.astype(f32)

    in_specs = [pl.BlockSpec((1, ts, w), lambda b, s, cb=cb: (b, s, cb)) for (_, w, cb) in tok]
    in_specs += [pl.BlockSpec((1,) + a.shape[1:], lambda b, s: (b, 0, 0)) for a in bat]
    in_specs += [pl.BlockSpec(a.shape, lambda b, s, nd=a.ndim: (0,) * nd) for a in con]
    out_specs = [pl.BlockSpec((1, ts, w), lambda b, s: (b, s, 0)) for (w, _) in tok_out]
    out_specs += [pl.BlockSpec((1,) + shp, lambda b, s, nd=len(shp): (b,) + (0,) * nd) for shp in acc_out]
    out_shape = [jax.ShapeDtypeStruct((B, S, w), dt) for (w, dt) in tok_out]
    out_shape += [jax.ShapeDtypeStruct((B,) + shp, f32) for shp in acc_out]
    extra, aliases = [], {}
    if into is not None:
        buf, cb = into
        assert buf.dtype == tok_out[0][1]
        in_specs.append(pl.BlockSpec(memory_space=pl.ANY))
        out_specs[0] = pl.BlockSpec((1, ts, tok_out[0][0]), lambda b, s: (b, s, cb))
        out_shape[0] = jax.ShapeDtypeStruct(buf.shape, buf.dtype)
        extra, aliases = [buf], {nin - 1: 0}
    return pl.pallas_call(
        body, name=name, grid=(B, S // ts), in_specs=in_specs, out_specs=out_specs, out_shape=out_shape,
        input_output_aliases=aliases, compiler_params=_cp(("parallel", "arbitrary")),
    )(*[t[0] for t in tok], *bat, *con, *extra)


def rowcall_fwd(name, f, tok, bat, con, tok_out, ts=256):
    def fn(t, b, c):
        return f([v.astype(f32) for v in t], b, c), []
    return rowcall(name, fn, tok, bat, con, tok_out, [], ts)


def rowcall_bwd(name, f, tok, bat, con, cts, tok_grads, add=None, ts=256, join_first=1, into=None):
    nt, ncts = len(tok), len(cts)

    def fn(t, b, c):
        prim = [v.astype(f32) for v in t[:nt]]
        ct = [v.astype(f32) for v in t[nt:nt + ncts]]
        _, vjp = jax.vjp(lambda tt, bb, cc: f(tt, bb, cc), prim, b, c)
        dt, db, dc = vjp(ct)
        touts = [dt[i] for i, _ in tok_grads]
        if add is not None:
            touts[0] = touts[0] + t[nt + ncts].astype(f32)
        if join_first > 1:
            touts = [jnp.concatenate(touts[:join_first], axis=1)] + touts[join_first:]
        return touts, list(db) + list(dc)

    all_tok = list(tok) + list(cts) + ([add] if add is not None else [])
    tok_out = [(tok[i][1], dt) for i, dt in tok_grads]
    if join_first > 1:
        tok_out = [(sum(w for w, _ in tok_out[:join_first]), tok_out[0][1])] + tok_out[join_first:]
    acc_out = [tuple(a.shape[1:]) for a in bat] + [tuple(a.shape) for a in con]
    return rowcall(name, fn, all_tok, bat, con, tok_out, acc_out, ts, into)


def _rms(y, w):
    return y * lax.rsqrt(jnp.mean(y * y, axis=-1, keepdims=True) + RMS_EPS) * w


def f_rms_mod(t, b, c):
    return [_rms(t[0], c[0]) * (1.0 + b[0]) + b[1]]


def f_post_pre(t, b, c):
    h1 = t[0] + b[0] * _rms(t[1], c[0])
    return [h1, _rms(h1, c[1]) * (1.0 + b[1]) + b[2]]


def f_merge(t, b, c):
    ga, gd, ya, yd = t
    return [jax.nn.sigmoid(ga) * ya + jax.nn.sigmoid(gd) * yd]


def f_dnout(t, b, c):
    o, z = t
    outs = []
    for h in range(DNH):
        sl = slice(h * DND, (h + 1) * DND)
        zh = z[:, sl]
        outs.append(_rms(o[:, sl], c[0]) * (zh * jax.nn.sigmoid(zh)))
    return [jnp.concatenate(outs, axis=1)]


def _softplus(x):
    return jnp.maximum(x, 0.0) + jnp.log(1.0 + jnp.exp(-jnp.abs(x)))


def f_gate(t, b, c):
    ba = t[0]
    a_log, dt_bias = c
    lane = lax.broadcasted_iota(jnp.int32, ba.shape, 1)
    beta = jax.nn.sigmoid(ba)
    g = -jnp.exp(a_log) * _softplus(ba + dt_bias)
    return [jnp.where(lane < DNH, beta, jnp.where(lane < 2 * DNH, g, 0.0))]


def _bucket_table():
    qi = np.arange(WIN)[:, None]
    kj = np.arange(2 * WIN)[None, :]
    dist = np.maximum(WIN + qi - kj, 0)
    max_exact = NBUCK // 2
    scaled = np.log(np.maximum(dist, 1).astype(np.float64) / max_exact) / math.log(MAXDIST / max_exact)
    large = np.minimum(max_exact + (scaled * (NBUCK - max_exact)).astype(np.int32), NBUCK - 1)
    return np.where(dist < max_exact, dist, large).astype(np.int32)


def _attn_mask(n):
    qi = lax.broadcasted_iota(jnp.int32, (WIN, 2 * WIN), 0)
    kj = lax.broadcasted_iota(jnp.int32, (WIN, 2 * WIN), 1)
    dist = WIN + qi - kj
    return (dist >= 0) & (dist < WIN) & ((kj >= WIN) | (n > 0))


def _swap_halves(x):
    return pltpu.roll(x, HD, axis=x.ndim - 1)


@jax.custom_vjp
def _swap_halves_vjp(x):
    return _swap_halves(x)


_swap_halves_vjp.defvjp(lambda x: (_swap_halves(x), None), lambda _, g: (_swap_halves(g),))


def _attn_block(q, kp, kc, vp, vc, bias, sinks, mask, differentiated):
    dot = _bdot_bf16_vjp if differentiated else _bdot_bf16
    swap = _swap_halves_vjp if differentiated else _swap_halves
    B, grp = q.shape[0], HQ // HKV
    upper = lax.broadcasted_iota(jnp.int32, (2 * WIN, LANE), 1) >= HD

    def placed(natural, swapped, j, half):
        keep = upper if half == 1 else ~upper
        return jnp.where(keep, natural if j == half else swapped, 0.0)

    qh, ks, vs = [], [], []
    for b in range(B):
        kb, vb = jnp.concatenate([kp[b], kc[b]], axis=0), jnp.concatenate([vp[b], vc[b]], axis=0)
        kb_sw, vb_sw = swap(kb), swap(vb)
        for h in range(HQ):
            qh.append(q[b, :, (h // 2) * LANE:(h // 2 + 1) * LANE])
            ks.append(placed(kb, kb_sw, h // grp, h % 2))
            vs.append(placed(vb, vb_sw, h // grp, h % 2))
    s = dot(_stack(qh), _stack(ks), 2, 2).reshape(B, HQ, WIN, 2 * WIN) * (HD ** -0.5)
    s = jnp.where(mask, s + bias, NEG_INF)
    m = jnp.maximum(jnp.max(s, axis=-1, keepdims=True), sinks)
    p = jnp.exp(s - m)
    probs = p / (jnp.sum(p, axis=-1, keepdims=True) + jnp.exp(sinks - m))
    o = dot(probs.reshape(B * HQ, WIN, 2 * WIN), _stack(vs), 2, 1)
    return _stack([jnp.concatenate([o[b * HQ + 2 * i] + o[b * HQ + 2 * i + 1] for i in range(HQ // 2)], axis=1) for b in range(B)])


def _attn_specs(B, NB):
    last = NB - 1
    return [
        pl.BlockSpec((B, WIN, HQ * HD), lambda n: (0, jnp.minimum(n, last), CB_AQ // 4)),
        pl.BlockSpec((B, WIN, LANE), lambda n: (0, jnp.clip(n - 1, 0, last), CB_AK)),
        pl.BlockSpec((B, WIN, LANE), lambda n: (0, jnp.minimum(n, last), CB_AK)),
        pl.BlockSpec((B, WIN, LANE), lambda n: (0, jnp.clip(n - 1, 0, last), CB_AV)),
        pl.BlockSpec((B, WIN, LANE), lambda n: (0, jnp.minimum(n, last), CB_AV)),
        pl.BlockSpec((HQ, WIN, 2 * WIN), lambda n: (0, 0, 0)),
        pl.BlockSpec((HQ, 1, 1), lambda n: (0, 0, 0)),
    ]


def attn_fwd(proj, bias, sinks, comm):
    B, S, _ = proj.shape
    NB = S // WIN

    def body(q, kp, kc, vp, vc, bias_ref, sink_ref, o_ref):
        mask = _attn_mask(pl.program_id(0))
        o = _attn_block(*[r[...].astype(f32) for r in (q, kp, kc, vp, vc)], bias_ref[...], sink_ref[...], mask, False)
        o_ref[...] = o.astype(o_ref.dtype)

    at = lambda n: lambda: pl.program_id(0) == n
    return pl.pallas_call(
        _ride(body, 7, 1, 0, comm, at(0), at((3 * NB) // 4), at(NB - 1)), name="attn_fwd", grid=(NB,),
        in_specs=_attn_specs(B, NB) + comm.specs,
        out_specs=[pl.BlockSpec((B, WIN, HQ * HD), lambda n: (0, n, 0))] + comm.specs,
        out_shape=[jax.ShapeDtypeStruct((B, S, HQ * HD), bf16)] + comm.out_shape, scratch_shapes=comm.scratch,
        compiler_params=_cp(("arbitrary",)),
    )(proj, proj, proj, proj, proj, bias, sinks, *comm.arrs)


def attn_bwd(proj, bias, sinks, dy, dproj, comm):
    B, S, _ = proj.shape
    NB = S // WIN
    last = NB - 1

    def body(q, kp, kc, vp, vc, bias_ref, sink_ref, dy_ref, _, dq_ref, dk_ref, dv_ref, dbias_ref, dsink_ref, kcar, vcar):
        n = pl.program_id(0)

        @pl.when(n == 0)
        def _():
            dbias_ref[...] = jnp.zeros(dbias_ref.shape, f32)
            dsink_ref[...] = jnp.zeros(dsink_ref.shape, f32)
            kcar[...] = jnp.zeros(kcar.shape, f32)
            vcar[...] = jnp.zeros(vcar.shape, f32)

        @pl.when(n < NB)
        def _():
            mask = _attn_mask(n)
            _, vjp = jax.vjp(lambda *a: _attn_block(*a, mask, True), *[r[...].astype(f32) for r in (q, kp, kc, vp, vc)],
                             bias_ref[...], sink_ref[...])
            dq, dkp, dkc, dvp, dvc, dbias, dsink = vjp(dy_ref[...].astype(f32))
            dq_ref[...] = dq.astype(dq_ref.dtype)
            dbias_ref[...] += dbias
            dsink_ref[...] += dsink
            dk_ref[...] = (kcar[...] + dkp).astype(dk_ref.dtype)
            dv_ref[...] = (vcar[...] + dvp).astype(dv_ref.dtype)
            kcar[...] = dkc
            vcar[...] = dvc

        @pl.when(n == NB)
        def _():
            dk_ref[...] = kcar[...].astype(dk_ref.dtype)
            dv_ref[...] = vcar[...].astype(dv_ref.dtype)

    in_specs = _attn_specs(B, NB) + [pl.BlockSpec((B, WIN, HQ * HD), lambda n: (0, jnp.minimum(n, last), 0)),
                                     pl.BlockSpec(memory_space=pl.ANY)]
    kv_out = pl.BlockSpec((B, WIN, LANE), lambda n: (0, jnp.maximum(n - 1, 0), 0))
    at = lambda n: lambda: pl.program_id(0) == n
    return pl.pallas_call(
        _ride(body, 9, 5, 2, comm, at(0), at(NB), at(NB)), name="attn_bwd", grid=(NB + 1,),
        in_specs=in_specs + comm.specs, input_output_aliases={8: 0},
        out_specs=[pl.BlockSpec((B, WIN, HQ * HD), lambda n: (0, jnp.minimum(n, last), CB_AQ // 4)), kv_out, kv_out,
                   pl.BlockSpec((HQ, WIN, 2 * WIN), lambda n: (0, 0, 0)), pl.BlockSpec((HQ, 1, 1), lambda n: (0, 0, 0))] + comm.specs,
        out_shape=[jax.ShapeDtypeStruct(dproj.shape, dproj.dtype), jax.ShapeDtypeStruct((B, S, LANE), bf16),
                   jax.ShapeDtypeStruct((B, S, LANE), bf16), jax.ShapeDtypeStruct((HQ, WIN, 2 * WIN), f32),
                   jax.ShapeDtypeStruct((HQ, 1, 1), f32)] + comm.out_shape,
        scratch_shapes=[pltpu.VMEM((B, WIN, LANE), f32), pltpu.VMEM((B, WIN, LANE), f32)] + comm.scratch,
        compiler_params=_cp(("arbitrary",)),
    )(proj, proj, proj, proj, proj, bias, sinks, dy, dproj, *comm.arrs)


DN_ROWS, FFN_ROWS = 256, 32


def _stage_rows(dst, value):
    dst[0:8] = jnp.zeros((8, LANE), f32)
    dst[8:8 + value.shape[0]] = value


def _conv_rows(xs, w, width, r, rows):
    wins = [xs[pl.ds(r + 8 - (width - 1) + j, rows), :] for j in range(width)]
    out = w[0:1] * wins[0]
    for j in range(1, width):
        out = out + w[j:j + 1] * wins[j]
    return out, wins


def _fold8(v):
    return jnp.sum(v.reshape(v.shape[0] // 8, 8, LANE), axis=0)


def _conv_rows_t(ds, w, width, r, rows):
    out = w[0:1] * ds[pl.ds(r + width - 1, rows), :]
    for j in range(1, width):
        out = out + w[j:j + 1] * ds[pl.ds(r + width - 1 - j, rows), :]
    return out


def _dn_outblk(i):
    return (i % DNH) * 3 + i // DNH


def _dn_act(c, isqk):
    sg = jax.nn.sigmoid(c)
    y = c * sg
    n = lax.rsqrt(jnp.sum(y * y, axis=-1, keepdims=True) + L2_EPS)
    return jnp.where(isqk, y * n, y), sg, n


def dnconv_fwd(proj, conv_w):
    B, S, _ = proj.shape
    rows = min(DN_ROWS, S)

    def body(x_ref, w_ref, o_ref, xs):
        isqk = pl.program_id(0) < 2 * DNH
        _stage_rows(xs, x_ref[0].astype(f32))
        w = w_ref[...]
        for r in range(0, S, rows):
            c, _ = _conv_rows(xs, w, DNK, r, rows)
            o_ref[0, pl.ds(r, rows), :] = _dn_act(c, isqk)[0]

    return pl.pallas_call(
        body, name="dnconv_fwd", grid=(3 * DNH, B),
        in_specs=[pl.BlockSpec((1, S, LANE), lambda i, b: (b, 0, CB_DQKV + i)), pl.BlockSpec((DNK, LANE), lambda i, b: (0, i))],
        out_specs=pl.BlockSpec((1, S, LANE), lambda i, b: (b, 0, _dn_outblk(i))),
        out_shape=jax.ShapeDtypeStruct((B, S, 3 * DNH * DND), f32), scratch_shapes=[pltpu.VMEM((S + 8, LANE), f32)],
        compiler_params=_cp(("parallel", "parallel")),
    )(proj, conv_w)


def dnconv_bwd(proj, conv_w, dqkvn, dproj):
    B, S, _ = proj.shape
    rows = min(DN_ROWS, S)

    def body(x_ref, w_ref, dy_ref, _, dx_ref, dw_ref, xs, ds):
        isqk = pl.program_id(0) < 2 * DNH
        _stage_rows(xs, x_ref[0].astype(f32))
        w = w_ref[...]
        dw = [jnp.zeros((8, LANE), f32) for _ in range(DNK)]
        for r in range(0, S, rows):
            c, wins = _conv_rows(xs, w, DNK, r, rows)
            out, sg, n = _dn_act(c, isqk)
            dout = dy_ref[0, pl.ds(r, rows), :]
            dy = jnp.where(isqk, n * (dout - out * jnp.sum(dout * out, axis=-1, keepdims=True)), dout)
            dc = dy * (sg * (1.0 + c * (1.0 - sg)))
            ds[pl.ds(r, rows), :] = dc
            for j in range(DNK):
                dw[j] = dw[j] + _fold8(dc * wins[j])
        ds[S:S + 8] = jnp.zeros((8, LANE), f32)
        for r in range(0, S, rows):
            dx_ref[0, pl.ds(r, rows), :] = _conv_rows_t(ds, w, DNK, r, rows).astype(dx_ref.dtype)

        @pl.when(pl.program_id(1) == 0)
        def _():
            dw_ref[...] = jnp.zeros(dw_ref.shape, f32)
        dw_ref[...] += jnp.concatenate([jnp.sum(d, axis=0, keepdims=True) for d in dw], axis=0)

    return pl.pallas_call(
        body, name="dnconv_bwd", grid=(3 * DNH, B),
        in_specs=[pl.BlockSpec((1, S, LANE), lambda i, b: (b, 0, CB_DQKV + i)), pl.BlockSpec((DNK, LANE), lambda i, b: (0, i)),
                  pl.BlockSpec((1, S, LANE), lambda i, b: (b, 0, _dn_outblk(i))), pl.BlockSpec(memory_space=pl.ANY)],
        out_specs=[pl.BlockSpec((1, S, LANE), lambda i, b: (b, 0, CB_DQKV + i)), pl.BlockSpec((DNK, LANE), lambda i, b: (0, i))],
        out_shape=[jax.ShapeDtypeStruct(dproj.shape, dproj.dtype), jax.ShapeDtypeStruct((DNK, 3 * DNH * DND), f32)],
        scratch_shapes=[pltpu.VMEM((S + 8, LANE), f32), pltpu.VMEM((S + 8, LANE), f32)],
        input_output_aliases={3: 0}, compiler_params=_cp(("parallel", "arbitrary")),
    )(proj, conv_w, dqkvn, dproj)


def _bdot(a, b, ca, cb, precision=HI):
    return lax.dot_general(a, b, (((ca,), (cb,)), ((0,), (0,))), preferred_element_type=f32, precision=precision)


def _bdot_bf16(a, b, ca, cb):
    return _bdot(a.astype(bf16), b.astype(bf16), ca, cb, None)


@functools.partial(jax.custom_vjp, nondiff_argnums=(2, 3))
def _bdot_bf16_vjp(a, b, ca, cb):
    return _bdot_bf16(a, b, ca, cb)


def _bdot_bf16_fwd(a, b, ca, cb):
    return _bdot_bf16(a, b, ca, cb), (a, b)


def _bdot_bf16_bwd(ca, cb, res, g):
    a, b = res
    fa, fb = 3 - ca, 3 - cb
    da = _bdot_bf16(g, b, 2, fb) if ca == 2 else _bdot_bf16(b, g, fb, 2)
    db = _bdot_bf16(a, g, fa, 1) if cb == 1 else _bdot_bf16(g, a, 1, fa)
    return da, db


_bdot_bf16_vjp.defvjp(_bdot_bf16_fwd, _bdot_bf16_bwd)


def _neumann_inverse(low):
    n = low.shape[-1]
    eye = (lax.broadcasted_iota(jnp.int32, (n, n), 0) == lax.broadcasted_iota(jnp.int32, (n, n), 1)).astype(f32)
    p = -low
    x = eye[None] + p
    for _ in range(5):
        p = _bdot_bf16(p, p, 2, 1)
        x = x + _bdot_bf16(x, p, 2, 1)
    return x


@jax.custom_vjp
def _unit_lower_inverse(low):
    return _neumann_inverse(low)


def _uli_fwd(low):
    t = _neumann_inverse(low)
    return t, t


def _uli_bwd(t, dt):
    return (-_bdot_bf16(_bdot_bf16(t, dt, 1, 1), t, 2, 2),)


_unit_lower_inverse.defvjp(_uli_fwd, _uli_bwd)


def _stack(xs):
    return jnp.concatenate([x[None] for x in xs], axis=0)


DELTA_CHUNKS = 2


def _delta_chunks(qkv, bg, state, differentiated):
    inverse = _unit_lower_inverse if differentiated else _neumann_inverse
    lo = _bdot_bf16_vjp if differentiated else _bdot_bf16
    B, n = qkv.shape[0], qkv.shape[1] // CH
    G = B * DNH
    N = n * G
    triples = [(i, b, h) for i in range(n) for b in range(B) for h in range(DNH)]
    col = lambda i, b, h, kind: qkv[b, i * CH:(i + 1) * CH, (3 * h + kind) * DND:(3 * h + kind + 1) * DND]
    q, k, v = [_stack([col(i, b, h, kind) for i, b, h in triples]) for kind in range(3)]
    lane = lax.broadcasted_iota(jnp.int32, (CH, LANE), 1)
    pick = lambda i, b, l: jnp.sum(jnp.where(lane == l, bg[b, i * CH:(i + 1) * CH], 0.0), axis=1, keepdims=True)
    beta = _stack([pick(i, b, h) for i, b, h in triples])
    g = _stack([pick(i, b, h + DNH) for i, b, h in triples])
    ri = lax.broadcasted_iota(jnp.int32, (CH, CH), 0)
    ci = lax.broadcasted_iota(jnp.int32, (CH, CH), 1)
    incl, strict = (ri >= ci)[None], (ri > ci)[None]
    gc = _bdot(jnp.broadcast_to(incl.astype(f32), (N, CH, CH)), jnp.broadcast_to(g, (N, CH, LANE)), 2, 1, MID)
    e0 = jnp.broadcast_to((lane == 0).astype(f32)[None], (N, CH, LANE))
    gc_row = _bdot(e0, gc, 2, 2, MID)
    diff = gc[:, :, :CH] - gc_row
    decay = jnp.where(incl, jnp.exp(jnp.where(incl, diff, 0.0)), 0.0)
    qs = q * (DND ** -0.5)
    kb, vb = k * beta, v * beta
    eg = jnp.exp(gc)
    with_k = lo(jnp.concatenate([kb, qs], axis=1), k, 2, 2)
    low = jnp.where(strict, with_k[:, :CH] * decay, 0.0)
    intra = jnp.where(incl, with_k[:, CH:] * decay, 0.0)
    tinv = inverse(low)
    solved = lo(tinv, jnp.concatenate([vb, kb * eg], axis=2), 2, 1)
    gl = gc[:, CH - 1:CH, :]
    k_tail = k * jnp.exp(gl - gc)
    to_state = jnp.concatenate([solved[:, :, DND:], qs * eg], axis=1)
    decay_all = jnp.exp(gl)
    outs = []
    for i in range(n):
        sl = slice(i * G, (i + 1) * G)
        with_state = lo(to_state[sl], state, 2, 1)
        v_new = solved[sl, :, :DND] - with_state[:, :CH]
        outs.append(with_state[:, CH:] + lo(intra[sl], v_new, 2, 1))
        state = state * decay_all[sl] + lo(k_tail[sl], v_new, 1, 1)
    return outs, state


def delta_fwd(qkvn, bg, comm):
    B, S, _ = qkvn.shape
    n = DELTA_CHUNKS if (S // CH) % DELTA_CHUNKS == 0 else 1
    steps, G, rows = S // (n * CH), B * DNH, n * CH

    def body(qkv_ref, bg_ref, o_ref, st_ref, state):
        @pl.when(pl.program_id(0) == 0)
        def _():
            state[...] = jnp.zeros(state.shape, f32)
        s0 = state[...]
        st_ref[0] = s0
        outs, s1 = _delta_chunks(qkv_ref[...], bg_ref[...], s0, False)
        for i, o in enumerate(outs):
            for b in range(B):
                for h in range(DNH):
                    o_ref[b, i * CH:(i + 1) * CH, h * DND:(h + 1) * DND] = o[b * DNH + h]
        state[...] = s1

    at = lambda c: lambda: pl.program_id(0) == c
    return pl.pallas_call(
        _ride(body, 2, 2, 1, comm, at(0), at((7 * steps) // 8), at(steps - 1)), name="delta_fwd", grid=(steps,),
        in_specs=[pl.BlockSpec((B, rows, 3 * DNH * DND), lambda c: (0, c, 0)), pl.BlockSpec((B, rows, LANE), lambda c: (0, c, 0))] + comm.specs,
        out_specs=[pl.BlockSpec((B, rows, DNH * DND), lambda c: (0, c, 0)), pl.BlockSpec((1, G, DND, DND), lambda c: (c, 0, 0, 0))] + comm.specs,
        out_shape=[jax.ShapeDtypeStruct((B, S, DNH * DND), f32), jax.ShapeDtypeStruct((steps, G, DND, DND), f32)] + comm.out_shape,
        scratch_shapes=[pltpu.VMEM((G, DND, DND), f32)] + comm.scratch, compiler_params=_cp(("arbitrary",)),
    )(qkvn, bg, *comm.arrs)


def delta_bwd(qkvn, bg, states, do, comm):
    B, S, _ = qkvn.shape
    steps, G = states.shape[0], B * DNH
    rows = S // steps
    n = rows // CH

    def body(qkv_ref, bg_ref, st_ref, do_ref, dqkv_ref, dbg_ref, dstate):
        @pl.when(pl.program_id(0) == 0)
        def _():
            dstate[...] = jnp.zeros(dstate.shape, f32)
        _, vjp = jax.vjp(lambda a, g, s: _delta_chunks(a, g, s, True), qkv_ref[...], bg_ref[...], st_ref[0])
        do = [_stack([do_ref[b, i * CH:(i + 1) * CH, h * DND:(h + 1) * DND] for b in range(B) for h in range(DNH)]) for i in range(n)]
        dqkv, dbg, ds = vjp((do, dstate[...]))
        dqkv_ref[...] = dqkv
        dbg_ref[...] = dbg
        dstate[...] = ds

    rev = lambda c: steps - 1 - c
    at = lambda c: lambda: pl.program_id(0) == c
    return pl.pallas_call(
        _ride(body, 4, 2, 1, comm, at(0), at(steps - 1), at(steps - 1)), name="delta_bwd", grid=(steps,),
        in_specs=[pl.BlockSpec((B, rows, 3 * DNH * DND), lambda c: (0, rev(c), 0)), pl.BlockSpec((B, rows, LANE), lambda c: (0, rev(c), 0)),
                  pl.BlockSpec((1, G, DND, DND), lambda c: (rev(c), 0, 0, 0)),
                  pl.BlockSpec((B, rows, DNH * DND), lambda c: (0, rev(c), 0))] + comm.specs,
        out_specs=[pl.BlockSpec((B, rows, 3 * DNH * DND), lambda c: (0, rev(c), 0)),
                   pl.BlockSpec((B, rows, LANE), lambda c: (0, rev(c), 0))] + comm.specs,
        out_shape=[jax.ShapeDtypeStruct((B, S, 3 * DNH * DND), f32), jax.ShapeDtypeStruct((B, S, LANE), f32)] + comm.out_shape,
        scratch_shapes=[pltpu.VMEM((G, DND, DND), f32)] + comm.scratch, compiler_params=_cp(("arbitrary",)),
    )(qkvn, bg, states, do, *comm.arrs)


GELU_C0, GELU_C1 = math.sqrt(2.0 / math.pi), 0.044715


def _ffn_specs(S):
    nblk = DFF // LANE
    return [pl.BlockSpec((1, S, LANE), lambda i, b: (b, 0, i)), pl.BlockSpec((1, S, LANE), lambda i, b: (b, 0, nblk + i)),
            pl.BlockSpec((FK, LANE), lambda i, b: (0, i)), pl.BlockSpec((FK, LANE), lambda i, b: (0, nblk + i))]


def ffnconv_fwd(up, conv_w):
    B, S, _ = up.shape
    rows = min(FFN_ROWS, S)

    def body(g_ref, v_ref, gw_ref, vw_ref, o_ref, xg, xv):
        _stage_rows(xg, g_ref[0].astype(f32))
        _stage_rows(xv, v_ref[0].astype(f32))
        gw, vw = gw_ref[...], vw_ref[...]
        for r in range(0, S, rows):
            g, _ = _conv_rows(xg, gw, FK, r, rows)
            v, _ = _conv_rows(xv, vw, FK, r, rows)
            t = jnp.tanh(GELU_C0 * (g * (1.0 + GELU_C1 * (g * g))))
            o_ref[0, pl.ds(r, rows), :] = (0.5 * g * (1.0 + t) * v).astype(o_ref.dtype)

    return pl.pallas_call(
        body, name="ffnconv_fwd", grid=(DFF // LANE, B), in_specs=_ffn_specs(S),
        out_specs=pl.BlockSpec((1, S, LANE), lambda i, b: (b, 0, i)), out_shape=jax.ShapeDtypeStruct((B, S, DFF), bf16),
        scratch_shapes=[pltpu.VMEM((S + 8, LANE), f32)] * 2, compiler_params=_cp(("parallel", "parallel")),
    )(up, up, conv_w, conv_w)


def ffnconv_bwd(up, conv_w, dact, comm):
    B, S, _ = up.shape
    rows = min(FFN_ROWS, S)

    def body(g_ref, v_ref, gw_ref, vw_ref, dy_ref, dx_ref, dw_ref, xg, xv, dg, dv):
        _stage_rows(xg, g_ref[0].astype(f32))
        _stage_rows(xv, v_ref[0].astype(f32))
        gw, vw = gw_ref[...], vw_ref[...]
        dgw = [jnp.zeros((8, LANE), f32) for _ in range(FK)]
        dvw = [jnp.zeros((8, LANE), f32) for _ in range(FK)]
        for r in range(0, S, rows):
            g, gwins = _conv_rows(xg, gw, FK, r, rows)
            v, vwins = _conv_rows(xv, vw, FK, r, rows)
            g2 = g * g
            t = jnp.tanh(GELU_C0 * (g * (1.0 + GELU_C1 * g2)))
            half = 0.5 * (1.0 + t)
            dgelu = half + (0.5 * GELU_C0) * g * (1.0 - t * t) * (1.0 + (3.0 * GELU_C1) * g2)
            dy = dy_ref[0, pl.ds(r, rows), :].astype(f32)
            dvc = dy * (g * half)
            dgc = dy * v * dgelu
            dg[pl.ds(r, rows), :] = dgc
            dv[pl.ds(r, rows), :] = dvc
            for j in range(FK):
                dgw[j] = dgw[j] + _fold8(dgc * gwins[j])
                dvw[j] = dvw[j] + _fold8(dvc * vwins[j])
        dg[S:S + 8] = jnp.zeros((8, LANE), f32)
        dv[S:S + 8] = jnp.zeros((8, LANE), f32)
        for r in range(0, S, rows):
            dx_ref[0, 0, pl.ds(r, rows), :] = _conv_rows_t(dg, gw, FK, r, rows).astype(dx_ref.dtype)
            dx_ref[1, 0, pl.ds(r, rows), :] = _conv_rows_t(dv, vw, FK, r, rows).astype(dx_ref.dtype)

        @pl.when(pl.program_id(1) == 0)
        def _():
            dw_ref[...] = jnp.zeros(dw_ref.shape, f32)
        dw_ref[0] += jnp.concatenate([jnp.sum(d, axis=0, keepdims=True) for d in dgw], axis=0)
        dw_ref[1] += jnp.concatenate([jnp.sum(d, axis=0, keepdims=True) for d in dvw], axis=0)

    nblk = DFF // LANE
    at = lambda i, b: lambda: (pl.program_id(0) == i) & (pl.program_id(1) == b)
    return pl.pallas_call(
        _ride(body, 5, 2, 4, comm, at(0, 0), at(nblk - 1, B - 1), at(nblk - 1, B - 1)), name="ffnconv_bwd", grid=(nblk, B),
        in_specs=_ffn_specs(S) + [pl.BlockSpec((1, S, LANE), lambda i, b: (b, 0, i))] + comm.specs,
        out_specs=[pl.BlockSpec((2, 1, S, LANE), lambda i, b: (0, b, 0, i)),
                   pl.BlockSpec((2, FK, LANE), lambda i, b: (0, 0, i))] + comm.specs,
        out_shape=[jax.ShapeDtypeStruct((2, B, S, DFF), bf16), jax.ShapeDtypeStruct((2, FK, DFF), f32)] + comm.out_shape,
        scratch_shapes=[pltpu.VMEM((S + 8, LANE), f32)] * 4 + comm.scratch, compiler_params=_cp(("arbitrary", "arbitrary")),
    )(up, up, conv_w, conv_w, dact, *comm.arrs)


def ada_fwd(c_all, ada_w, ada_b):
    def body(c_ref, w_ref, b_ref, o_ref):
        c = c_ref[...]
        act = (c * jax.nn.sigmoid(c)).astype(bf16)
        o_ref[...] = jnp.dot(act, w_ref[...].astype(bf16), preferred_element_type=f32) + b_ref[...]

    return pl.pallas_call(body, name="ada_fwd", out_shape=jax.ShapeDtypeStruct((c_all.shape[0], ada_w.shape[1]), f32),
                          compiler_params=pltpu.CompilerParams(vmem_limit_bytes=VMEM_LIMIT))(c_all, ada_w, ada_b)


def ada_bwd(c_all, dmod):
    def body(c_ref, d_ref, o_ref):
        c = c_ref[...]
        act = (c * jax.nn.sigmoid(c)).astype(bf16)
        o_ref[...] = lax.dot_general(act, d_ref[...].astype(bf16), (((0,), (0,)), ((), ())), preferred_element_type=f32)

    return pl.pallas_call(body, name="ada_bwd", out_shape=jax.ShapeDtypeStruct((c_all.shape[1], dmod.shape[1]), f32),
                          compiler_params=pltpu.CompilerParams(vmem_limit_bytes=VMEM_LIMIT))(c_all, dmod)


def loss_head(h1, y2, target, g2, w):
    def fn(t, b, c):
        h, y, tg = [v.astype(f32) for v in t]

        def loss_fn(h, y, g, w):
            e = h + g * _rms(y, w) - tg
            return 0.5 * jnp.sum(jnp.mean(e * e, axis=-1))

        loss, grads = jax.value_and_grad(loss_fn, argnums=(0, 1, 2, 3))(h, y, b[0], c[0])
        return [grads[0], grads[1]], [grads[2], grads[3], jnp.full((1, LANE), loss, f32)]

    return rowcall("loss_head", fn, [(h1, D, 0), (y2, D, 0), (target, D, 0)], [g2], [w], [(D, f32), (D, bf16)],
                   [(1, D), (1, D), (1, LANE)])


def adamw(w, gparts, m, v, name):
    R, C = w.shape
    P = gparts.shape[0]
    budget = 2 * 1024 * 1024
    tr, tc = R, C
    if R * C * 4 > budget and R % 8 == 0:
        tr = max(t for t in range(8, R + 1, 8) if R % t == 0 and t * C * 4 <= budget)
    elif R * C * 4 > budget:
        tc = max(t for t in range(LANE, C + 1, LANE) if C % t == 0 and R * t * 4 <= budget)

    def body(w_ref, g_ref, m_ref, v_ref, go, do, mo, vo):
        g = g_ref[0].astype(f32)
        for p in range(1, P):
            g = g + g_ref[p].astype(f32)
        m2 = B1 * m_ref[...] + (1.0 - B1) * g
        v2 = B2 * v_ref[...] + (1.0 - B2) * jnp.square(g)
        m_hat = m2 * (1.0 / (1.0 - B1 ** STEP))
        v_hat = v2 * (1.0 / (1.0 - B2 ** STEP))
        go[...] = g
        do[...] = -LR * (m_hat / (jnp.sqrt(v_hat) + EPS) + WD * w_ref[...])
        mo[...] = m2
        vo[...] = v2

    blk = pl.BlockSpec((tr, tc), lambda i, j: (i, j))
    return pl.pallas_call(
        body, name=name, grid=(R // tr, C // tc), in_specs=[blk, pl.BlockSpec((P, tr, tc), lambda i, j: (0, i, j)), blk, blk],
        out_specs=[blk] * 4, out_shape=[jax.ShapeDtypeStruct((R, C), f32)] * 4, compiler_params=_cp(("parallel", "parallel")),
    )(w, gparts, m, v)


def _pack_w_in(wt):
    aq, ak, av, dqkv, dz, dbeta, da, ga, gd = jnp.split(wt, np.cumsum(IN_SPLITS)[:-1].tolist(), axis=0)
    ba = jnp.pad(jnp.concatenate([dbeta, da], axis=0), ((0, LANE - 2 * DNH), (0, 0)))
    return jnp.concatenate([ga, gd, aq, dqkv, dz, ak, av, ba], axis=0)


def _unpack_w_in(p):
    row = lambda cb, n: p[cb * LANE: cb * LANE + n]
    ba = row(CB_BA, 2 * DNH)
    return jnp.concatenate([row(CB_AQ, HQ * HD), row(CB_AK, HKV * HD), row(CB_AV, HKV * HD), row(CB_DQKV, 3 * DNH * DND),
                            row(CB_DZ, DNH * DND), ba[:DNH], ba[DNH:], row(CB_GA, D), row(CB_GD, D)], axis=0)


def _cols_gathered(g):
    return g.transpose(1, 0, 2).reshape(g.shape[1], NDEV * g.shape[2])


def _cols_split(w):
    r = w.shape[0]
    return w.reshape(r, NDEV, w.shape[1] // NDEV).transpose(1, 0, 2)


def kernel(x, c, ada_w, ada_b, norm_mix_pre, norm_mix_post, norm_ffn_pre, norm_ffn_post, w_in, dn_conv_w, dn_a_log, dn_dt_bias, dn_norm_w, attn_sinks, rel_bias, w_attn_branch, w_dn_branch, w_out, ffn_w_up, ffn_conv_w, ffn_w_down, loss_target, m_ada_w, m_ada_b, m_norm_mix_pre, m_norm_mix_post, m_norm_ffn_pre, m_norm_ffn_post, m_w_in, m_dn_conv_w, m_dn_a_log, m_dn_dt_bias, m_dn_norm_w, m_attn_sinks, m_rel_bias, m_w_attn_branch, m_w_dn_branch, m_w_out, m_ffn_w_up, m_ffn_conv_w, m_ffn_w_down, v_ada_w, v_ada_b, v_norm_mix_pre, v_norm_mix_post, v_norm_ffn_pre, v_norm_ffn_post, v_w_in, v_dn_conv_w, v_dn_a_log, v_dn_dt_bias, v_dn_norm_w, v_attn_sinks, v_rel_bias, v_w_attn_branch, v_w_dn_branch, v_w_out, v_ffn_w_up, v_ffn_conv_w, v_ffn_w_down):
    B, S, _ = x.shape
    T = B * S
    me = 4 * lax.axis_index("x") + 2 * lax.axis_index("y") + lax.axis_index("c")
    big = dict(w_in=w_in, dn_conv_w=dn_conv_w, w_attn_branch=w_attn_branch, w_dn_branch=w_dn_branch, w_out=w_out,
               ffn_w_up=ffn_w_up, ffn_conv_w=ffn_conv_w, ffn_w_down=ffn_w_down)
    big_names = list(big)

    first, mid, late = ["w_in", "dn_conv_w"], ["w_attn_branch", "w_dn_branch", "w_out"], ["ffn_w_up", "ffn_conv_w"]
    transposed = ("w_in", "ffn_w_up")
    local = lambda n, a: a[0].T if n in transposed else a[0]
    shard = lambda names: [local(n, big[n]).astype(bf16) for n in names]
    *got, c_all = _exchange(shard(first) + [c], "gather_w_in", two_level=True)
    gw = dict(zip(first, got))
    c_all = c_all.reshape(NDEV * B, D)

    wp = _pack_w_in(gw["w_in"].reshape(IN_DIM, D))
    conv_dn = _cols_gathered(gw["dn_conv_w"]).astype(f32)

    ncol = ada_w.shape[2]
    ada_b_mine = lax.dynamic_slice_in_dim(ada_b, me * ncol, ncol, axis=1)
    mod_cols = ada_fwd(c_all, ada_w[0], ada_b_mine)
    (mod_g,) = _exchange([mod_cols], "gather_mod")
    mod = lax.dynamic_slice_in_dim(mod_g, me * B, B, axis=1).transpose(1, 0, 2).reshape(B, NMOD * D)
    sh1, sc1, g1, sh2, sc2, g2 = [mod[:, i * D:(i + 1) * D].reshape(B, 1, D) for i in range(NMOD)]

    onehot = (jnp.asarray(_bucket_table()).reshape(1, -1) == jnp.arange(NBUCK, dtype=jnp.int32)[:, None]).astype(f32)
    bias = mm(rel_bias.T, onehot, "nn", f32, "bias_table", tn=8192, precision=HI).reshape(HQ, WIN, 2 * WIN)
    sinks = attn_sinks.reshape(HQ, 1, 1)
    a_log_pad = jnp.pad(dn_a_log, ((0, 0), (DNH, LANE - 2 * DNH)))
    dt_bias_pad = jnp.pad(dn_dt_bias, ((0, 0), (DNH, LANE - 2 * DNH)))

    (u1,) = rowcall_fwd("mix_pre", f_rms_mod, [(x, D, 0)], [sc1, sh1], [norm_mix_pre], [(D, bf16)])
    proj, gw["ffn_w_down"] = mm(u1.reshape(T, D), wp, "nt", bf16, "proj", tm=512, tn=CB_BA * LANE, b_cols=(0, 1),
                                comm=_Comm(shard(["ffn_w_down"]), two_level=True))
    proj = proj.reshape(B, S, CB_BA * LANE)
    ba = mm(u1.reshape(T, D), wp, "nt", f32, "proj_ba", tn=LANE, b_cols=(CB_BA, 1)).reshape(B, S, LANE)
    ya, *got = attn_fwd(proj, bias, sinks, _Comm(shard(mid), two_level=True))
    gw.update(zip(mid, got))
    wa = _cols_gathered(gw["w_attn_branch"])
    wd = _cols_gathered(gw["w_dn_branch"])
    wo = gw["w_out"].reshape(D, D)
    qkvn = dnconv_fwd(proj, conv_dn)
    (bg,) = rowcall_fwd("dn_gate", f_gate, [(ba, LANE, 0)], [], [a_log_pad, dt_bias_pad], [(LANE, f32)])
    o_dn, states, *got = delta_fwd(qkvn, bg, _Comm(shard(late), two_level=True))
    gw.update(zip(late, got))
    wup = gw["ffn_w_up"].reshape(2 * DFF, D)
    conv_ffn = _cols_gathered(gw["ffn_conv_w"]).astype(f32)
    wdown = gw["ffn_w_down"].reshape(DFF, D)
    (yd,) = rowcall_fwd("dn_out", f_dnout, [(o_dn, DNH * DND, 0), (proj, DNH * DND, CB_DZ // 4)], [], [dn_norm_w], [(DNH * DND, bf16)])
    pa = mm(ya.reshape(T, HQ * HD), wa, "nn", bf16, "attn_branch").reshape(B, S, D)
    pd = mm(yd.reshape(T, DNH * DND), wd, "nn", bf16, "dn_branch").reshape(B, S, D)
    merge_tok = [(proj, D, CB_GA // 8), (proj, D, CB_GD // 8), (pa, D, 0), (pd, D, 0)]
    (merged,) = rowcall_fwd("merge", f_merge, merge_tok, [], [], [(D, bf16)])
    y1 = mm(merged.reshape(T, D), wo, "nn", bf16, "mix_out").reshape(B, S, D)
    post_pre = ([(x, D, 0), (y1, D, 0)], [g1, sc2, sh2], [norm_mix_post, norm_ffn_pre])
    h1, u2 = rowcall_fwd("mix_post_ffn_pre", f_post_pre, *post_pre, [(D, f32), (D, bf16)])
    up = mm(u2.reshape(T, D), wup, "nt", bf16, "ffn_up", tn=2816).reshape(B, S, 2 * DFF)
    act = ffnconv_fwd(up, conv_ffn)
    y2 = mm(act.reshape(T, DFF), wdown, "nn", bf16, "ffn_down", tk=2816).reshape(B, S, D)

    dh1_a, dy2, dg2, dw_ffn_post, loss_b = loss_head(h1, y2, loss_target, g2, norm_ffn_post)
    dy2f = dy2.reshape(T, D)
    dact = mm(dy2f, wdown, "nt", bf16, "ffn_down_dx", tn=2816).reshape(B, S, DFF)
    g_wdown = mm(act.reshape(T, DFF), dy2f, "tn", bf16, "ffn_down_dw", tm=2816, tn=512, tk=4096)
    in_flight = []

    def send_off(d, tag):
        in_flight.append((d, _scatter_start([a.astype(bf16) for a in d.values()], "scatter_" + tag + "_start")))
        return in_flight[-1][1][-1][0, 0]

    started = send_off(dict(ffn_w_down=g_wdown.reshape(NDEV, DFF // NDEV, D)), "ffn_down")
    dup, g_conv_ffn = ffnconv_bwd(up, conv_ffn + started, dact, _NoComm())
    dupf = dup.reshape(2, T, DFF)
    g_conv_ffn = g_conv_ffn.transpose(1, 0, 2).reshape(FK, 2 * DFF)
    du2 = mm(dupf, wup, "nn", bf16, "ffn_up_dx", tk=2816).reshape(B, S, D)
    g_wup = mm(dupf, u2.reshape(T, D), "tn", bf16, "ffn_up_dw", tm=1408, tk=2048)
    started = send_off(dict(ffn_w_up=g_wup.reshape(NDEV, 2 * DFF // NDEV, D), ffn_conv_w=_cols_split(g_conv_ffn)), "ffn_up")
    post_pre = (post_pre[0], [g1 + started, sc2, sh2], post_pre[2])
    dh1, dy1, dg1, dsc2, dsh2, dw_mix_post, dw_ffn_pre = rowcall_bwd(
        "mix_post_ffn_pre_bwd", f_post_pre, *post_pre, [(dh1_a, D, 0), (du2, D, 0)], [(0, f32), (1, bf16)])
    dy1f = dy1.reshape(T, D)
    dmerged = mm(dy1f, wo, "nt", bf16, "mix_out_dx").reshape(B, S, D)
    g_wo = mm(merged.reshape(T, D), dy1f, "tn", bf16, "mix_out_dw", tk=2048)
    dproj = lax.empty((B, S, NP), bf16)
    dproj, dpa, dpd = rowcall_bwd("merge_bwd", f_merge, merge_tok, [], [], [(dmerged, D, 0)],
                                  [(0, bf16), (1, bf16), (2, bf16), (3, bf16)], join_first=2, into=(dproj, CB_GA // 16))
    dpaf, dpdf = dpa.reshape(T, D), dpd.reshape(T, D)
    dya = mm(dpaf, wa, "nt", bf16, "attn_branch_dx").reshape(B, S, HQ * HD)
    g_wa = mm(ya.reshape(T, HQ * HD), dpaf, "tn", bf16, "attn_branch_dw", tk=2048)
    dyd = mm(dpdf, wd, "nt", bf16, "dn_branch_dx").reshape(B, S, DNH * DND)
    g_wd = mm(yd.reshape(T, DNH * DND), dpdf, "tn", bf16, "dn_branch_dw", tk=2048)
    dproj, do_dn, dw_dn_norm = rowcall_bwd("dn_out_bwd", f_dnout, [(o_dn, DNH * DND, 0), (proj, DNH * DND, CB_DZ // 4)], [], [dn_norm_w],
                                           [(dyd, DNH * DND, 0)], [(1, bf16), (0, f32)], into=(dproj, CB_DZ // 4))
    started = send_off(dict(w_attn_branch=_cols_split(g_wa), w_dn_branch=_cols_split(g_wd), w_out=g_wo.reshape(NDEV, D // NDEV, D)), "branches")
    dqkvn, dbg = delta_bwd(qkvn, bg + started, states, do_dn, _NoComm())
    dproj, da_log_pad, ddt_bias_pad = rowcall_bwd("dn_gate_bwd", f_gate, [(ba, LANE, 0)], [], [a_log_pad, dt_bias_pad],
                                                  [(dbg, LANE, 0)], [(0, bf16)], into=(dproj, CB_BA))
    dproj, g_conv_dn = dnconv_bwd(proj, conv_dn, dqkvn, dproj)
    dproj, dk, dv, dbias, dsinks = attn_bwd(proj, bias, sinks, dya, dproj, _NoComm())
    dproj = lax.dynamic_update_slice(dproj, jnp.concatenate([dk, dv], axis=2), (0, 0, CB_AK * LANE)).reshape(T, NP)
    g_wp = mm(dproj, u1.reshape(T, D), "tn", bf16, "proj_dw", tm=1664, tk=1024)
    started = send_off(dict(w_in=_unpack_w_in(g_wp).reshape(NDEV, IN_DIM // NDEV, D), dn_conv_w=_cols_split(g_conv_dn)), "w_in")
    du1 = mm(dproj, wp, "nn", bf16, "proj_dx", tm=512, tk=NP).reshape(B, S, D)
    grad_x, dsc1, dsh1, dw_mix_pre = rowcall_bwd("mix_pre_bwd", f_rms_mod, [(x, D, 0)], [sc1 + started, sh1], [norm_mix_pre],
                                                 [(du1, D, 0)], [(0, f32)], add=(dh1, D, 0))
    g_rel = mm(dbias.reshape(HQ, WIN * 2 * WIN), onehot, "nt", f32, "rel_bias_dw", tk=8192, precision=HI)

    dmod = jnp.concatenate([dsh1, dsc1, dg1, dsh2, dsc2, dg2], axis=2).reshape(B, NMOD * D)

    zrow = lambda a: jnp.concatenate([a.reshape(1, -1), jnp.zeros((B - 1, a.size), f32)], axis=0)
    small_g = jnp.concatenate([
        dmod, dw_mix_pre.reshape(B, D), dw_mix_post.reshape(B, D), dw_ffn_pre.reshape(B, D), dw_ffn_post.reshape(B, D),
        da_log_pad.reshape(B, LANE)[:, DNH:2 * DNH], ddt_bias_pad.reshape(B, LANE)[:, DNH:2 * DNH], dw_dn_norm.reshape(B, DND),
        zrow(dsinks), zrow(g_rel.T), loss_b.reshape(B, LANE)[:, :1], jnp.zeros((B, SMALL_PAD - SMALL_N - 1), f32)], axis=1)
    (small_all,) = _exchange([small_g], "gather_small")
    dmod_cols = lax.dynamic_slice_in_dim(small_all.reshape(NDEV * B, SMALL_PAD), me * ncol, ncol, axis=1)
    g_ada_w = ada_bwd(c_all, dmod_cols)
    parts = {}
    for i, (d, started) in enumerate(in_flight):
        landed = _scatter_finish(started, len(d), g_ada_w, "scatter_finish_%d" % i)
        for nme, src, got in zip(d, started[2:2 + len(d)], landed):
            parts[nme] = lax.dynamic_update_slice_in_dim(got, lax.dynamic_slice_in_dim(src, me, 1, axis=0), me, axis=0)
    small_w = dict(ada_b=(ada_b, m_ada_b, v_ada_b), norm_mix_pre=(norm_mix_pre, m_norm_mix_pre, v_norm_mix_pre),
                   norm_mix_post=(norm_mix_post, m_norm_mix_post, v_norm_mix_post), norm_ffn_pre=(norm_ffn_pre, m_norm_ffn_pre, v_norm_ffn_pre),
                   norm_ffn_post=(norm_ffn_post, m_norm_ffn_post, v_norm_ffn_post), dn_a_log=(dn_a_log, m_dn_a_log, v_dn_a_log),
                   dn_dt_bias=(dn_dt_bias, m_dn_dt_bias, v_dn_dt_bias), dn_norm_w=(dn_norm_w, m_dn_norm_w, v_dn_norm_w),
                   attn_sinks=(attn_sinks, m_attn_sinks, v_attn_sinks), rel_bias=(rel_bias, m_rel_bias, v_rel_bias))

    def pack(i, fill):
        row = jnp.concatenate([small_w[n][i].reshape(1, -1) for n, _ in SMALL], axis=1)
        return jnp.pad(row, ((0, 0), (0, SMALL_PAD - SMALL_N)), constant_values=fill)

    small_out = adamw(pack(0, 0.0), small_all.reshape(NDEV * B, 1, SMALL_PAD), pack(1, 0.0), pack(2, 1.0), "adamw_small")
    loss = small_out[0][0, SMALL_N]

    res = {}
    off = 0
    for n, size in SMALL:
        shp = small_w[n][0].shape
        res[n] = [o[:, off:off + size].reshape(shp) for o in small_out]
        off += size
    res["ada_w"] = [o[None] for o in adamw(ada_w[0], g_ada_w[None], m_ada_w[0], v_ada_w[0], "adamw_ada_w")]
    moments = dict(w_in=(m_w_in, v_w_in), dn_conv_w=(m_dn_conv_w, v_dn_conv_w), w_attn_branch=(m_w_attn_branch, v_w_attn_branch),
                   w_dn_branch=(m_w_dn_branch, v_w_dn_branch), w_out=(m_w_out, v_w_out), ffn_w_up=(m_ffn_w_up, v_ffn_w_up),
                   ffn_conv_w=(m_ffn_conv_w, v_ffn_conv_w), ffn_w_down=(m_ffn_w_down, v_ffn_w_down))
    for n in big_names:
        outs = adamw(local(n, big[n]), parts[n], local(n, moments[n][0]), local(n, moments[n][1]), "adamw_" + n)
        res[n] = [(o.T if n in transposed else o)[None] for o in outs]

    order = ["ada_w", "ada_b", "norm_mix_pre", "norm_mix_post", "norm_ffn_pre", "norm_ffn_post", "w_in", "dn_conv_w", "dn_a_log",
             "dn_dt_bias", "dn_norm_w", "attn_sinks", "rel_bias", "w_attn_branch", "w_dn_branch", "w_out", "ffn_w_up", "ffn_conv_w",
             "ffn_w_down"]
    return (loss, grad_x, *[res[n][0] for n in order], *[res[n][1] for n in order], *[res[n][2] for n in order],
            *[res[n][3] for n in order])
```

```python
import functools
import math

import numpy as np
import jax
import jax.numpy as jnp
from jax import lax
from jax.experimental import pallas as pl
from jax.experimental.pallas import tpu as pltpu

f32 = jnp.float32
bf16 = jnp.bfloat16
HI = lax.Precision.HIGHEST
MID = lax.Precision.HIGH
MESH = pl.DeviceIdType.MESH

NDEV = 8
D = 1024
HQ, HKV, HD, WIN, NBUCK, MAXDIST = 8, 2, 64, 128, 32, 128
DNH, DND, DNK, CH = 4, 128, 4, 64
DFF, FK = 2816, 3
NMOD = 6
RMS_EPS = 1e-6
L2_EPS = 1e-6
NEG_INF = -1e30
LR, B1, B2, EPS, WD, STEP = 0.001, 0.9, 0.999, 1e-08, 0.01, 10

LANE = 128
CB_GA, CB_GD, CB_AQ, CB_DQKV, CB_DZ, CB_AK, CB_AV, CB_BA, NPB = 0, 8, 16, 20, 32, 36, 37, 38, 39
NP = NPB * LANE
IN_SPLITS = (HQ * HD, HKV * HD, HKV * HD, 3 * DNH * DND, DNH * DND, DNH, DNH, D, D)
IN_DIM = sum(IN_SPLITS)
VMEM_LIMIT = 56 * 1024 * 1024

SMALL = (("ada_b", NMOD * D), ("norm_mix_pre", D), ("norm_mix_post", D), ("norm_ffn_pre", D), ("norm_ffn_post", D),
         ("dn_a_log", DNH), ("dn_dt_bias", DNH), ("dn_norm_w", DND), ("attn_sinks", HQ), ("rel_bias", NBUCK * HQ))
SMALL_N = sum(n for _, n in SMALL)
SMALL_PAD = 10752


def _cp(sem):
    return pltpu.CompilerParams(dimension_semantics=sem, vmem_limit_bytes=VMEM_LIMIT)


def _pick(dim, target):
    if dim <= target:
        return dim
    best = None
    for d in range(LANE, target + 1, LANE):
        if dim % d == 0:
            best = d
    assert best is not None, (dim, target)
    return best


def _me():
    x, y, c = lax.axis_index("x"), lax.axis_index("y"), lax.axis_index("c")
    return x, y, c, 4 * x + 2 * y + c


def _peer(x, y, c, k):
    px = 1 - x if k & 4 else x
    py = 1 - y if k & 2 else y
    pc = 1 - c if k & 1 else c
    return (px, py, pc), 4 * px + 2 * py + pc


class _Comm:
    def __init__(self, arrs, scatter=False, two_level=False):
        assert not (scatter and two_level)
        self.arrs, self.n, self.scatter, self.two_level = list(arrs), len(arrs), scatter, two_level
        if scatter:
            self.out_shape = [jax.ShapeDtypeStruct(a.shape, a.dtype) for a in arrs]
        else:
            self.out_shape = [jax.ShapeDtypeStruct((NDEV,) + a.shape, a.dtype) for a in arrs]
        nsem = self.n * (NDEV - 1)
        self.scratch = [pltpu.SemaphoreType.DMA((nsem,)), pltpu.SemaphoreType.DMA((nsem,)), pltpu.SemaphoreType.DMA((self.n,))]
        self.specs = [pl.BlockSpec(memory_space=pl.ANY)] * self.n

    def phases(self, ins, out, send, recv, loc):
        x, y, c, me = _me()

        def remote(a, k, src, dst, to):
            s = a * (NDEV - 1) + k - 1
            return pltpu.make_async_remote_copy(src_ref=src, dst_ref=dst, send_sem=send.at[s], recv_sem=recv.at[s],
                                                device_id=to, device_id_type=MESH)

        def local(a):
            return pltpu.make_async_copy(ins[a].at[me] if self.scatter else ins[a], out[a].at[me], loc.at[a])

        if not self.two_level:
            def mine(a, k):
                peer, pid = _peer(x, y, c, k)
                return remote(a, k, ins[a].at[pid] if self.scatter else ins[a], out[a].at[me], peer)

            def theirs(a, k):
                peer, pid = _peer(x, y, c, k)
                return remote(a, k, ins[a].at[pid] if self.scatter else ins[a], out[a].at[pid], peer)

            def start():
                for a in range(self.n):
                    local(a).start()
                    for k in range(1, NDEV):
                        mine(a, k).start()

            def forward():
                pass

            def finish():
                for a in range(self.n):
                    for k in range(1, NDEV):
                        mine(a, k).wait_send()
                    for k in range(1, NDEV):
                        theirs(a, k).wait_recv()
                    local(a).wait()

            return start, forward, finish

        sibling = (x, y, 1 - c)
        chips = [(1 - x, y), (x, 1 - y), (1 - x, 1 - y)]
        slot = lambda px, py, pc: 4 * px + 2 * py + pc

        def own(a, k, to):
            return remote(a, k, ins[a], out[a].at[me], to)

        def landed(a, k, frm):
            return remote(a, k, ins[a], out[a].at[slot(*frm)], frm)

        def passed(a, j):
            rows = out[a].at[slot(*chips[j], c)]
            return remote(a, 5 + j, rows, rows, sibling)

        def start():
            for a in range(self.n):
                local(a).start()
                own(a, 1, sibling).start()
                for j, chip in enumerate(chips):
                    own(a, 2 + j, (*chip, c)).start()

        def forward():
            for a in range(self.n):
                for j, chip in enumerate(chips):
                    landed(a, 2 + j, (*chip, c)).wait_recv()
                    passed(a, j).start()

        def finish():
            for a in range(self.n):
                landed(a, 1, sibling).wait_recv()
                for j, chip in enumerate(chips):
                    remote(a, 5 + j, ins[a], out[a].at[slot(*chip, 1 - c)], sibling).wait_recv()
                own(a, 1, sibling).wait_send()
                for j, chip in enumerate(chips):
                    own(a, 2 + j, (*chip, c)).wait_send()
                    passed(a, j).wait_send()
                local(a).wait()

        return start, forward, finish


class _NoComm:
    n, arrs, out_shape, specs, scratch = 0, [], [], [], []

    def phases(self, *_):
        return (lambda: None,) * 3


def _ride(body, n_in, n_out, n_scr, comm, first, mid, last):
    k = comm.n

    def wrapped(*refs):
        ins, cins = refs[:n_in], refs[n_in:n_in + k]
        o0 = n_in + k
        outs, couts = refs[o0:o0 + n_out], refs[o0 + n_out:o0 + n_out + k]
        s0 = o0 + n_out + k
        scr, sems = refs[s0:s0 + n_scr], refs[s0 + n_scr:]
        start, forward, finish = comm.phases(cins, couts, *sems)
        pl.when(first())(start)
        body(*ins, *outs, *scr)
        pl.when(mid())(forward)
        pl.when(last())(finish)

    return wrapped


def _scatter_start(arrs, name):
    n = len(arrs)

    def body(*refs):
        ins, lands, send, recv, own, token = refs[:n], refs[n:2 * n], refs[2 * n], refs[2 * n + 1], refs[2 * n + 2], refs[-1]
        x, y, c, me = _me()
        for a in range(n):
            pltpu.make_async_copy(ins[a].at[me], lands[a].at[me], own.at[a]).start()
            for k in range(1, NDEV):
                peer, pid = _peer(x, y, c, k)
                s = a * (NDEV - 1) + k - 1
                pltpu.make_async_remote_copy(src_ref=ins[a].at[pid], dst_ref=lands[a].at[me], send_sem=send.at[s], recv_sem=recv.at[s],
                                             device_id=peer, device_id_type=MESH).start()
        token[...] = jnp.zeros(token.shape, token.dtype)

    hbm, sem = pl.BlockSpec(memory_space=pltpu.HBM), pl.BlockSpec(memory_space=pltpu.SEMAPHORE)
    nsem = n * (NDEV - 1)
    thru = [pltpu.HBM(a.shape, a.dtype) for a in arrs]
    return pl.pallas_call(
        body, name=name, in_specs=[hbm] * (2 * n),
        out_shape=(pltpu.SemaphoreType.DMA((nsem,)), pltpu.SemaphoreType.DMA((nsem,)), pltpu.SemaphoreType.DMA((n,)), *thru, *thru,
                   jax.ShapeDtypeStruct((8, LANE), f32)),
        out_specs=(sem, sem, sem, *[hbm] * (2 * n), pl.BlockSpec(memory_space=pltpu.VMEM)),
        input_output_aliases={i: 3 + i for i in range(2 * n)},
        compiler_params=pltpu.CompilerParams(has_side_effects=pltpu.SideEffectType.DATAFLOW_SIDE_EFFECTING),
    )(*[pltpu.with_memory_space_constraint(a, pltpu.HBM) for a in arrs],
      *[pltpu.with_memory_space_constraint(lax.empty(a.shape, a.dtype), pltpu.HBM) for a in arrs])


def _scatter_finish(started, n, after, name):
    send, recv, own, *rest = started
    srcs, lands = rest[:n], rest[n:2 * n]

    def body(*refs):
        ins, lnd, send_ref, recv_ref, own_ref = refs[:n], refs[n:2 * n], refs[2 * n], refs[2 * n + 1], refs[2 * n + 2]
        x, y, c, me = _me()
        for a in range(n):
            pltpu.make_async_copy(ins[a].at[me], lnd[a].at[me], own_ref.at[a]).wait()
            for k in range(1, NDEV):
                peer, pid = _peer(x, y, c, k)
                s = a * (NDEV - 1) + k - 1
                cp = pltpu.make_async_remote_copy(src_ref=ins[a].at[pid], dst_ref=lnd[a].at[pid], send_sem=send_ref.at[s],
                                                  recv_sem=recv_ref.at[s], device_id=peer, device_id_type=MESH)
                cp.wait_send()
                cp.wait_recv()

    hbm, sem = pl.BlockSpec(memory_space=pltpu.HBM), pl.BlockSpec(memory_space=pltpu.SEMAPHORE)
    thru = [pltpu.HBM(a.shape, a.dtype) for a in srcs]
    out = pl.pallas_call(
        body, name=name, in_specs=[hbm] * (2 * n) + [sem, sem, sem, pl.BlockSpec(memory_space=pl.ANY)],
        out_shape=(*thru, *thru), out_specs=tuple([hbm] * (2 * n)), input_output_aliases={i: i for i in range(2 * n)},
        compiler_params=pltpu.CompilerParams(has_side_effects=pltpu.SideEffectType.DATAFLOW_SIDE_EFFECTING),
    )(*srcs, *lands, send, recv, own, after)
    return list(out[n:])


def _exchange(arrs, name, scatter=False, two_level=False):
    comm = _Comm(arrs, scatter, two_level)

    def body(*refs):
        start, forward, finish = comm.phases(refs[:comm.n], refs[comm.n:2 * comm.n], *refs[2 * comm.n:])
        start()
        forward()
        finish()

    return pl.pallas_call(body, name=name, out_shape=comm.out_shape, in_specs=comm.specs, out_specs=comm.specs,
                          scratch_shapes=comm.scratch, compiler_params=pltpu.CompilerParams(has_side_effects=True))(*arrs)


def mm(a, b, mode, out_dtype, name, tm=1024, tn=1024, tk=1024, precision=None, comm=None, b_cols=None):
    a_parts = a.shape[0] if a.ndim == 3 else 1
    b_parts = b.shape[0] if b.ndim == 3 else 1
    assert b_parts == 1 or mode == "tn"
    ash, bsh = (a.shape[-2], a.shape[-1] * a_parts), b.shape[-2:]
    if mode == "nn":
        (M, K), (K2, N) = ash, bsh
    elif mode == "nt":
        (M, K), (N, K2) = ash, bsh
    else:
        (K, M), (K2, N) = ash, (bsh[0], bsh[1] * b_parts)
    assert K == K2, (name, a.shape, b.shape)
    col0 = 0
    if b_cols is not None:
        assert mode in ("nn", "nt") and tn % LANE == 0
        col0, N = b_cols[0], b_cols[1] * tn
    if mode == "tn":
        tm, tn, tk = _pick(M // a_parts, tm), _pick(N // b_parts, tn), _pick(K, tk)
    else:
        tm, tn, tk = _pick(M, tm), _pick(N // b_parts, tn), _pick(K // a_parts, tk)
    nk = K // tk
    if mode == "tn" and a_parts > 1:
        per = M // tm // a_parts
        a_spec = pl.BlockSpec((None, tk, tm), lambda i, j, k: (i // per, k, i % per))
    elif mode == "tn":
        a_spec = pl.BlockSpec((tk, tm), lambda i, j, k: (k, i))
    elif a_parts > 1:
        per = nk // a_parts
        a_spec = pl.BlockSpec((None, tm, tk), lambda i, j, k: (k // per, i, k % per))
    else:
        a_spec = pl.BlockSpec((tm, tk), lambda i, j, k: (i, k))
    if mode == "nt":
        b_spec = pl.BlockSpec((tn, tk), lambda i, j, k: (col0 + j, k))
    elif b_parts > 1:
        per = N // tn // b_parts
        b_spec = pl.BlockSpec((None, tk, tn), lambda i, j, k: (j // per, k, j % per))
    else:
        b_spec = pl.BlockSpec((tk, tn), lambda i, j, k: (k, col0 + j))
    dims = {"nn": ((1,), (0,)), "nt": ((1,), (1,)), "tn": ((0,), (0,))}[mode]

    def body(a_ref, b_ref, o_ref, *scr):
        p = lax.dot_general(a_ref[...], b_ref[...], (dims, ((), ())), preferred_element_type=f32, precision=precision)
        if nk == 1:
            o_ref[...] = p.astype(o_ref.dtype)
        else:
            acc = scr[0]
            k = pl.program_id(2)

            @pl.when(k == 0)
            def _():
                acc[...] = p

            @pl.when(k > 0)
            def _():
                acc[...] += p

            @pl.when(k == nk - 1)
            def _():
                o_ref[...] = acc[...].astype(o_ref.dtype)

    grid = (M // tm, N // tn, nk)
    scratch = [pltpu.VMEM((tm, tn), f32)] if nk > 1 else []
    out_spec = pl.BlockSpec((tm, tn), lambda i, j, k: (i, j))
    out_shape = jax.ShapeDtypeStruct((M, N), out_dtype)
    if comm is None:
        return pl.pallas_call(body, name=name, grid=grid, in_specs=[a_spec, b_spec], out_specs=out_spec, out_shape=out_shape,
                              scratch_shapes=scratch, compiler_params=_cp(("parallel", "parallel", "arbitrary")))(a, b)
    at = lambda pos: lambda: functools.reduce(jnp.logical_and, [pl.program_id(d) == p for d, p in enumerate(pos)])
    end = tuple(g - 1 for g in grid)
    return pl.pallas_call(
        _ride(body, 2, 1, len(scratch), comm, at((0, 0, 0)), at(end), at(end)), name=name, grid=grid,
        in_specs=[a_spec, b_spec] + comm.specs, out_specs=[out_spec] + comm.specs, out_shape=[out_shape] + comm.out_shape,
        scratch_shapes=scratch + comm.scratch, compiler_params=_cp(("arbitrary", "arbitrary", "arbitrary")),
    )(a, b, *comm.arrs)


def rowcall(name, fn, tok, bat, con, tok_out, acc_out, ts=256, into=None):
    B, S = tok[0][0].shape[:2]
    ts = min(ts, S)
    nt, nb, nc, no, na = len(tok), len(bat), len(con), len(tok_out), len(acc_out)
    nin = nt + nb + nc + (1 if into is not None else 0)

    def body(*refs):
        tr, br, cr = refs[:nt], refs[nt:nt + nb], refs[nt + nb:nt + nb + nc]
        orf, arf = refs[nin:nin + no], refs[nin + no:]
        touts, aouts = fn([r[0] for r in tr], [r[0] for r in br], [r[...] for r in cr])
        for r, v in zip(orf, touts):
            r[0] = v.astype(r.dtype)
        s = pl.program_id(1)
        for r, v in zip(arf, aouts):
            @pl.when(s == 0)
            def _(r=r):
                r[...] = jnp.zeros(r.shape, r.dtype)
            r[0] += v.astype(f32)

    in_specs = [pl.BlockSpec((1, ts, w), lambda b, s, cb=cb: (b, s, cb)) for (_, w, cb) in tok]
    in_specs += [pl.BlockSpec((1,) + a.shape[1:], lambda b, s: (b, 0, 0)) for a in bat]
    in_specs += [pl.BlockSpec(a.shape, lambda b, s, nd=a.ndim: (0,) * nd) for a in con]
    out_specs = [pl.BlockSpec((1, ts, w), lambda b, s: (b, s, 0)) for (w, _) in tok_out]
    out_specs += [pl.BlockSpec((1,) + shp, lambda b, s, nd=len(shp): (b,) + (0,) * nd) for shp in acc_out]
    out_shape = [jax.ShapeDtypeStruct((B, S, w), dt) for (w, dt) in tok_out]
    out_shape += [jax.ShapeDtypeStruct((B,) + shp, f32) for shp in acc_out]
    extra, aliases = [], {}
    if into is not None:
        buf, cb = into
        assert buf.dtype == tok_out[0][1]
        in_specs.append(pl.BlockSpec(memory_space=pl.ANY))
        out_specs[0] = pl.BlockSpec((1, ts, tok_out[0][0]), lambda b, s: (b, s, cb))
        out_shape[0] = jax.ShapeDtypeStruct(buf.shape, buf.dtype)
        extra, aliases = [buf], {nin - 1: 0}
    return pl.pallas_call(
        body, name=name, grid=(B, S // ts), in_specs=in_specs, out_specs=out_specs, out_shape=out_shape,
        input_output_aliases=aliases, compiler_params=_cp(("parallel", "arbitrary")),
    )(*[t[0] for t in tok], *bat, *con, *extra)


def rowcall_fwd(name, f, tok, bat, con, tok_out, ts=256):
    def fn(t, b, c):
        return f([v.astype(f32) for v in t], b, c), []
    return rowcall(name, fn, tok, bat, con, tok_out, [], ts)


def rowcall_bwd(name, f, tok, bat, con, cts, tok_grads, add=None, ts=256, join_first=1, into=None):
    nt, ncts = len(tok), len(cts)

    def fn(t, b, c):
        prim = [v.astype(f32) for v in t[:nt]]
        ct = [v.astype(f32) for v in t[nt:nt + ncts]]
        _, vjp = jax.vjp(lambda tt, bb, cc: f(tt, bb, cc), prim, b, c)
        dt, db, dc = vjp(ct)
        touts = [dt[i] for i, _ in tok_grads]
        if add is not None:
            touts[0] = touts[0] + t[nt + ncts].astype(f32)
        if join_first > 1:
            touts = [jnp.concatenate(touts[:join_first], axis=1)] + touts[join_first:]
        return touts, list(db) + list(dc)

    all_tok = list(tok) + list(cts) + ([add] if add is not None else [])
    tok_out = [(tok[i][1], dt) for i, dt in tok_grads]
    if join_first > 1:
        tok_out = [(sum(w for w, _ in tok_out[:join_first]), tok_out[0][1])] + tok_out[join_first:]
    acc_out = [tuple(a.shape[1:]) for a in bat] + [tuple(a.shape) for a in con]
    return rowcall(name, fn, all_tok, bat, con, tok_out, acc_out, ts, into)


def _rms(y, w):
    return y * lax.rsqrt(jnp.mean(y * y, axis=-1, keepdims=True) + RMS_EPS) * w


def f_rms_mod(t, b, c):
    return [_rms(t[0], c[0]) * (1.0 + b[0]) + b[1]]


def f_post_pre(t, b, c):
    h1 = t[0] + b[0] * _rms(t[1], c[0])
    return [h1, _rms(h1, c[1]) * (1.0 + b[1]) + b[2]]


def f_merge(t, b, c):
    ga, gd, ya, yd = t
    return [jax.nn.sigmoid(ga) * ya + jax.nn.sigmoid(gd) * yd]


def f_dnout(t, b, c):
    o, z = t
    outs = []
    for h in range(DNH):
        sl = slice(h * DND, (h + 1) * DND)
        zh = z[:, sl]
        outs.append(_rms(o[:, sl], c[0]) * (zh * jax.nn.sigmoid(zh)))
    return [jnp.concatenate(outs, axis=1)]


def _softplus(x):
    return jnp.maximum(x, 0.0) + jnp.log(1.0 + jnp.exp(-jnp.abs(x)))


def f_gate(t, b, c):
    ba = t[0]
    a_log, dt_bias = c
    lane = lax.broadcasted_iota(jnp.int32, ba.shape, 1)
    beta = jax.nn.sigmoid(ba)
    g = -jnp.exp(a_log) * _softplus(ba + dt_bias)
    return [jnp.where(lane < DNH, beta, jnp.where(lane < 2 * DNH, g, 0.0))]


def _bucket_table():
    qi = np.arange(WIN)[:, None]
    kj = np.arange(2 * WIN)[None, :]
    dist = np.maximum(WIN + qi - kj, 0)
    max_exact = NBUCK // 2
    scaled = np.log(np.maximum(dist, 1).astype(np.float64) / max_exact) / math.log(MAXDIST / max_exact)
    large = np.minimum(max_exact + (scaled * (NBUCK - max_exact)).astype(np.int32), NBUCK - 1)
    return np.where(dist < max_exact, dist, large).astype(np.int32)


def _attn_mask(n):
    qi = lax.broadcasted_iota(jnp.int32, (WIN, 2 * WIN), 0)
    kj = lax.broadcasted_iota(jnp.int32, (WIN, 2 * WIN), 1)
    dist = WIN + qi - kj
    return (dist >= 0) & (dist < WIN) & ((kj >= WIN) | (n > 0))


def _swap_halves(x):
    return pltpu.roll(x, HD, axis=x.ndim - 1)


@jax.custom_vjp
def _swap_halves_vjp(x):
    return _swap_halves(x)


_swap_halves_vjp.defvjp(lambda x: (_swap_halves(x), None), lambda _, g: (_swap_halves(g),))


def _attn_block(q, kp, kc, vp, vc, bias, sinks, mask, differentiated):
    dot = _bdot_bf16_vjp if differentiated else _bdot_bf16
    swap = _swap_halves_vjp if differentiated else _swap_halves
    B, grp = q.shape[0], HQ // HKV
    upper = lax.broadcasted_iota(jnp.int32, (2 * WIN, LANE), 1) >= HD

    def placed(natural, swapped, j, half):
        keep = upper if half == 1 else ~upper
        return jnp.where(keep, natural if j == half else swapped, 0.0)

    qh, ks, vs = [], [], []
    for b in range(B):
        kb, vb = jnp.concatenate([kp[b], kc[b]], axis=0), jnp.concatenate([vp[b], vc[b]], axis=0)
        kb_sw, vb_sw = swap(kb), swap(vb)
        for h in range(HQ):
            qh.append(q[b, :, (h // 2) * LANE:(h // 2 + 1) * LANE])
            ks.append(placed(kb, kb_sw, h // grp, h % 2))
            vs.append(placed(vb, vb_sw, h // grp, h % 2))
    s = dot(_stack(qh), _stack(ks), 2, 2).reshape(B, HQ, WIN, 2 * WIN) * (HD ** -0.5)
    s = jnp.where(mask, s + bias, NEG_INF)
    m = jnp.maximum(jnp.max(s, axis=-1, keepdims=True), sinks)
    p = jnp.exp(s - m)
    probs = p / (jnp.sum(p, axis=-1, keepdims=True) + jnp.exp(sinks - m))
    o = dot(probs.reshape(B * HQ, WIN, 2 * WIN), _stack(vs), 2, 1)
    return _stack([jnp.concatenate([o[b * HQ + 2 * i] + o[b * HQ + 2 * i + 1] for i in range(HQ // 2)], axis=1) for b in range(B)])


def _attn_specs(B, NB):
    last = NB - 1
    return [
        pl.BlockSpec((B, WIN, HQ * HD), lambda n: (0, jnp.minimum(n, last), CB_AQ // 4)),
        pl.BlockSpec((B, WIN, LANE), lambda n: (0, jnp.clip(n - 1, 0, last), CB_AK)),
        pl.BlockSpec((B, WIN, LANE), lambda n: (0, jnp.minimum(n, last), CB_AK)),
        pl.BlockSpec((B, WIN, LANE), lambda n: (0, jnp.clip(n - 1, 0, last), CB_AV)),
        pl.BlockSpec((B, WIN, LANE), lambda n: (0, jnp.minimum(n, last), CB_AV)),
        pl.BlockSpec((HQ, WIN, 2 * WIN), lambda n: (0, 0, 0)),
        pl.BlockSpec((HQ, 1, 1), lambda n: (0, 0, 0)),
    ]


def attn_fwd(proj, bias, sinks, comm):
    B, S, _ = proj.shape
    NB = S // WIN

    def body(q, kp, kc, vp, vc, bias_ref, sink_ref, o_ref):
        mask = _attn_mask(pl.program_id(0))
        o = _attn_block(*[r[...].astype(f32) for r in (q, kp, kc, vp, vc)], bias_ref[...], sink_ref[...], mask, False)
        o_ref[...] = o.astype(o_ref.dtype)

    at = lambda n: lambda: pl.program_id(0) == n
    return pl.pallas_call(
        _ride(body, 7, 1, 0, comm, at(0), at((3 * NB) // 4), at(NB - 1)), name="attn_fwd", grid=(NB,),
        in_specs=_attn_specs(B, NB) + comm.specs,
        out_specs=[pl.BlockSpec((B, WIN, HQ * HD), lambda n: (0, n, 0))] + comm.specs,
        out_shape=[jax.ShapeDtypeStruct((B, S, HQ * HD), bf16)] + comm.out_shape, scratch_shapes=comm.scratch,
        compiler_params=_cp(("arbitrary",)),
    )(proj, proj, proj, proj, proj, bias, sinks, *comm.arrs)


def attn_bwd(proj, bias, sinks, dy, dproj, comm):
    B, S, _ = proj.shape
    NB = S // WIN
    last = NB - 1

    def body(q, kp, kc, vp, vc, bias_ref, sink_ref, dy_ref, _, dq_ref, dk_ref, dv_ref, dbias_ref, dsink_ref, kcar, vcar):
        n = pl.program_id(0)

        @pl.when(n == 0)
        def _():
            dbias_ref[...] = jnp.zeros(dbias_ref.shape, f32)
            dsink_ref[...] = jnp.zeros(dsink_ref.shape, f32)
            kcar[...] = jnp.zeros(kcar.shape, f32)
            vcar[...] = jnp.zeros(vcar.shape, f32)

        @pl.when(n < NB)
        def _():
            mask = _attn_mask(n)
            _, vjp = jax.vjp(lambda *a: _attn_block(*a, mask, True), *[r[...].astype(f32) for r in (q, kp, kc, vp, vc)],
                             bias_ref[...], sink_ref[...])
            dq, dkp, dkc, dvp, dvc, dbias, dsink = vjp(dy_ref[...].astype(f32))
            dq_ref[...] = dq.astype(dq_ref.dtype)
            dbias_ref[...] += dbias
            dsink_ref[...] += dsink
            dk_ref[...] = (kcar[...] + dkp).astype(dk_ref.dtype)
            dv_ref[...] = (vcar[...] + dvp).astype(dv_ref.dtype)
            kcar[...] = dkc
            vcar[...] = dvc

        @pl.when(n == NB)
        def _():
            dk_ref[...] = kcar[...].astype(dk_ref.dtype)
            dv_ref[...] = vcar[...].astype(dv_ref.dtype)

    in_specs = _attn_specs(B, NB) + [pl.BlockSpec((B, WIN, HQ * HD), lambda n: (0, jnp.minimum(n, last), 0)),
                                     pl.BlockSpec(memory_space=pl.ANY)]
    kv_out = pl.BlockSpec((B, WIN, LANE), lambda n: (0, jnp.maximum(n - 1, 0), 0))
    at = lambda n: lambda: pl.program_id(0) == n
    return pl.pallas_call(
        _ride(body, 9, 5, 2, comm, at(0), at(NB), at(NB)), name="attn_bwd", grid=(NB + 1,),
        in_specs=in_specs + comm.specs, input_output_aliases={8: 0},
        out_specs=[pl.BlockSpec((B, WIN, HQ * HD), lambda n: (0, jnp.minimum(n, last), CB_AQ // 4)), kv_out, kv_out,
                   pl.BlockSpec((HQ, WIN, 2 * WIN), lambda n: (0, 0, 0)), pl.BlockSpec((HQ, 1, 1), lambda n: (0, 0, 0))] + comm.specs,
        out_shape=[jax.ShapeDtypeStruct(dproj.shape, dproj.dtype), jax.ShapeDtypeStruct((B, S, LANE), bf16),
                   jax.ShapeDtypeStruct((B, S, LANE), bf16), jax.ShapeDtypeStruct((HQ, WIN, 2 * WIN), f32),
                   jax.ShapeDtypeStruct((HQ, 1, 1), f32)] + comm.out_shape,
        scratch_shapes=[pltpu.VMEM((B, WIN, LANE), f32), pltpu.VMEM((B, WIN, LANE), f32)] + comm.scratch,
        compiler_params=_cp(("arbitrary",)),
    )(proj, proj, proj, proj, proj, bias, sinks, dy, dproj, *comm.arrs)


DN_ROWS, FFN_ROWS = 256, 32


def _stage_rows(dst, value):
    dst[0:8] = jnp.zeros((8, LANE), f32)
    dst[8:8 + value.shape[0]] = value


def _conv_rows(xs, w, width, r, rows):
    wins = [xs[pl.ds(r + 8 - (width - 1) + j, rows), :] for j in range(width)]
    out = w[0:1] * wins[0]
    for j in range(1, width):
        out = out + w[j:j + 1] * wins[j]
    return out, wins


def _fold8(v):
    return jnp.sum(v.reshape(v.shape[0] // 8, 8, LANE), axis=0)


def _conv_rows_t(ds, w, width, r, rows):
    out = w[0:1] * ds[pl.ds(r + width - 1, rows), :]
    for j in range(1, width):
        out = out + w[j:j + 1] * ds[pl.ds(r + width - 1 - j, rows), :]
    return out


def _dn_outblk(i):
    return (i % DNH) * 3 + i // DNH


def _dn_act(c, isqk):
    sg = jax.nn.sigmoid(c)
    y = c * sg
    n = lax.rsqrt(jnp.sum(y * y, axis=-1, keepdims=True) + L2_EPS)
    return jnp.where(isqk, y * n, y), sg, n


def dnconv_fwd(proj, conv_w):
    B, S, _ = proj.shape
    rows = min(DN_ROWS, S)

    def body(x_ref, w_ref, o_ref, xs):
        isqk = pl.program_id(0) < 2 * DNH
        _stage_rows(xs, x_ref[0].astype(f32))
        w = w_ref[...]
        for r in range(0, S, rows):
            c, _ = _conv_rows(xs, w, DNK, r, rows)
            o_ref[0, pl.ds(r, rows), :] = _dn_act(c, isqk)[0]

    return pl.pallas_call(
        body, name="dnconv_fwd", grid=(3 * DNH, B),
        in_specs=[pl.BlockSpec((1, S, LANE), lambda i, b: (b, 0, CB_DQKV + i)), pl.BlockSpec((DNK, LANE), lambda i, b: (0, i))],
        out_specs=pl.BlockSpec((1, S, LANE), lambda i, b: (b, 0, _dn_outblk(i))),
        out_shape=jax.ShapeDtypeStruct((B, S, 3 * DNH * DND), f32), scratch_shapes=[pltpu.VMEM((S + 8, LANE), f32)],
        compiler_params=_cp(("parallel", "parallel")),
    )(proj, conv_w)


def dnconv_bwd(proj, conv_w, dqkvn, dproj):
    B, S, _ = proj.shape
    rows = min(DN_ROWS, S)

    def body(x_ref, w_ref, dy_ref, _, dx_ref, dw_ref, xs, ds):
        isqk = pl.program_id(0) < 2 * DNH
        _stage_rows(xs, x_ref[0].astype(f32))
        w = w_ref[...]
        dw = [jnp.zeros((8, LANE), f32) for _ in range(DNK)]
        for r in range(0, S, rows):
            c, wins = _conv_rows(xs, w, DNK, r, rows)
            out, sg, n = _dn_act(c, isqk)
            dout = dy_ref[0, pl.ds(r, rows), :]
            dy = jnp.where(isqk, n * (dout - out * jnp.sum(dout * out, axis=-1, keepdims=True)), dout)
            dc = dy * (sg * (1.0 + c * (1.0 - sg)))
            ds[pl.ds(r, rows), :] = dc
            for j in range(DNK):
                dw[j] = dw[j] + _fold8(dc * wins[j])
        ds[S:S + 8] = jnp.zeros((8, LANE), f32)
        for r in range(0, S, rows):
            dx_ref[0, pl.ds(r, rows), :] = _conv_rows_t(ds, w, DNK, r, rows).astype(dx_ref.dtype)

        @pl.when(pl.program_id(1) == 0)
        def _():
            dw_ref[...] = jnp.zeros(dw_ref.shape, f32)
        dw_ref[...] += jnp.concatenate([jnp.sum(d, axis=0, keepdims=True) for d in dw], axis=0)

    return pl.pallas_call(
        body, name="dnconv_bwd", grid=(3 * DNH, B),
        in_specs=[pl.BlockSpec((1, S, LANE), lambda i, b: (b, 0, CB_DQKV + i)), pl.BlockSpec((DNK, LANE), lambda i, b: (0, i)),
                  pl.BlockSpec((1, S, LANE), lambda i, b: (b, 0, _dn_outblk(i))), pl.BlockSpec(memory_space=pl.ANY)],
        out_specs=[pl.BlockSpec((1, S, LANE), lambda i, b: (b, 0, CB_DQKV + i)), pl.BlockSpec((DNK, LANE), lambda i, b: (0, i))],
        out_shape=[jax.ShapeDtypeStruct(dproj.shape, dproj.dtype), jax.ShapeDtypeStruct((DNK, 3 * DNH * DND), f32)],
        scratch_shapes=[pltpu.VMEM((S + 8, LANE), f32), pltpu.VMEM((S + 8, LANE), f32)],
        input_output_aliases={3: 0}, compiler_params=_cp(("parallel", "arbitrary")),
    )(proj, conv_w, dqkvn, dproj)


def _bdot(a, b, ca, cb, precision=HI):
    return lax.dot_general(a, b, (((ca,), (cb,)), ((0,), (0,))), preferred_element_type=f32, precision=precision)


def _bdot_bf16(a, b, ca, cb):
    return _bdot(a.astype(bf16), b.astype(bf16), ca, cb, None)


@functools.partial(jax.custom_vjp, nondiff_argnums=(2, 3))
def _bdot_bf16_vjp(a, b, ca, cb):
    return _bdot_bf16(a, b, ca, cb)


def _bdot_bf16_fwd(a, b, ca, cb):
    return _bdot_bf16(a, b, ca, cb), (a, b)


def _bdot_bf16_bwd(ca, cb, res, g):
    a, b = res
    fa, fb = 3 - ca, 3 - cb
    da = _bdot_bf16(g, b, 2, fb) if ca == 2 else _bdot_bf16(b, g, fb, 2)
    db = _bdot_bf16(a, g, fa, 1) if cb == 1 else _bdot_bf16(g, a, 1, fa)
    return da, db


_bdot_bf16_vjp.defvjp(_bdot_bf16_fwd, _bdot_bf16_bwd)


def _neumann_inverse(low):
    n = low.shape[-1]
    eye = (lax.broadcasted_iota(jnp.int32, (n, n), 0) == lax.broadcasted_iota(jnp.int32, (n, n), 1)).astype(f32)
    p = -low
    x = eye[None] + p
    for _ in range(5):
        p = _bdot_bf16(p, p, 2, 1)
        x = x + _bdot_bf16(x, p, 2, 1)
    return x


@jax.custom_vjp
def _unit_lower_inverse(low):
    return _neumann_inverse(low)


def _uli_fwd(low):
    t = _neumann_inverse(low)
    return t, t


def _uli_bwd(t, dt):
    return (-_bdot_bf16(_bdot_bf16(t, dt, 1, 1), t, 2, 2),)


_unit_lower_inverse.defvjp(_uli_fwd, _uli_bwd)


def _stack(xs):
    return jnp.concatenate([x[None] for x in xs], axis=0)


DELTA_CHUNKS = 2


def _delta_chunks(qkv, bg, state, differentiated):
    inverse = _unit_lower_inverse if differentiated else _neumann_inverse
    lo = _bdot_bf16_vjp if differentiated else _bdot_bf16
    B, n = qkv.shape[0], qkv.shape[1] // CH
    G = B * DNH
    N = n * G
    triples = [(i, b, h) for i in range(n) for b in range(B) for h in range(DNH)]
    col = lambda i, b, h, kind: qkv[b, i * CH:(i + 1) * CH, (3 * h + kind) * DND:(3 * h + kind + 1) * DND]
    q, k, v = [_stack([col(i, b, h, kind) for i, b, h in triples]) for kind in range(3)]
    lane = lax.broadcasted_iota(jnp.int32, (CH, LANE), 1)
    pick = lambda i, b, l: jnp.sum(jnp.where(lane == l, bg[b, i * CH:(i + 1) * CH], 0.0), axis=1, keepdims=True)
    beta = _stack([pick(i, b, h) for i, b, h in triples])
    g = _stack([pick(i, b, h + DNH) for i, b, h in triples])
    ri = lax.broadcasted_iota(jnp.int32, (CH, CH), 0)
    ci = lax.broadcasted_iota(jnp.int32, (CH, CH), 1)
    incl, strict = (ri >= ci)[None], (ri > ci)[None]
    gc = _bdot(jnp.broadcast_to(incl.astype(f32), (N, CH, CH)), jnp.broadcast_to(g, (N, CH, LANE)), 2, 1, MID)
    e0 = jnp.broadcast_to((lane == 0).astype(f32)[None], (N, CH, LANE))
    gc_row = _bdot(e0, gc, 2, 2, MID)
    diff = gc[:, :, :CH] - gc_row
    decay = jnp.where(incl, jnp.exp(jnp.where(incl, diff, 0.0)), 0.0)
    qs = q * (DND ** -0.5)
    kb, vb = k * beta, v * beta
    eg = jnp.exp(gc)
    with_k = lo(jnp.concatenate([kb, qs], axis=1), k, 2, 2)
    low = jnp.where(strict, with_k[:, :CH] * decay, 0.0)
    intra = jnp.where(incl, with_k[:, CH:] * decay, 0.0)
    tinv = inverse(low)
    solved = lo(tinv, jnp.concatenate([vb, kb * eg], axis=2), 2, 1)
    gl = gc[:, CH - 1:CH, :]
    k_tail = k * jnp.exp(gl - gc)
    to_state = jnp.concatenate([solved[:, :, DND:], qs * eg], axis=1)
    decay_all = jnp.exp(gl)
    outs = []
    for i in range(n):
        sl = slice(i * G, (i + 1) * G)
        with_state = lo(to_state[sl], state, 2, 1)
        v_new = solved[sl, :, :DND] - with_state[:, :CH]
        outs.append(with_state[:, CH:] + lo(intra[sl], v_new, 2, 1))
        state = state * decay_all[sl] + lo(k_tail[sl], v_new, 1, 1)
    return outs, state


def delta_fwd(qkvn, bg, comm):
    B, S, _ = qkvn.shape
    n = DELTA_CHUNKS if (S // CH) % DELTA_CHUNKS == 0 else 1
    steps, G, rows = S // (n * CH), B * DNH, n * CH

    def body(qkv_ref, bg_ref, o_ref, st_ref, state):
        @pl.when(pl.program_id(0) == 0)
        def _():
            state[...] = jnp.zeros(state.shape, f32)
        s0 = state[...]
        st_ref[0] = s0
        outs, s1 = _delta_chunks(qkv_ref[...], bg_ref[...], s0, False)
        for i, o in enumerate(outs):
            for b in range(B):
                for h in range(DNH):
                    o_ref[b, i * CH:(i + 1) * CH, h * DND:(h + 1) * DND] = o[b * DNH + h]
        state[...] = s1

    at = lambda c: lambda: pl.program_id(0) == c
    return pl.pallas_call(
        _ride(body, 2, 2, 1, comm, at(0), at((7 * steps) // 8), at(steps - 1)), name="delta_fwd", grid=(steps,),
        in_specs=[pl.BlockSpec((B, rows, 3 * DNH * DND), lambda c: (0, c, 0)), pl.BlockSpec((B, rows, LANE), lambda c: (0, c, 0))] + comm.specs,
        out_specs=[pl.BlockSpec((B, rows, DNH * DND), lambda c: (0, c, 0)), pl.BlockSpec((1, G, DND, DND), lambda c: (c, 0, 0, 0))] + comm.specs,
        out_shape=[jax.ShapeDtypeStruct((B, S, DNH * DND), f32), jax.ShapeDtypeStruct((steps, G, DND, DND), f32)] + comm.out_shape,
        scratch_shapes=[pltpu.VMEM((G, DND, DND), f32)] + comm.scratch, compiler_params=_cp(("arbitrary",)),
    )(qkvn, bg, *comm.arrs)


def delta_bwd(qkvn, bg, states, do, comm):
    B, S, _ = qkvn.shape
    steps, G = states.shape[0], B * DNH
    rows = S // steps
    n = rows // CH

    def body(qkv_ref, bg_ref, st_ref, do_ref, dqkv_ref, dbg_ref, dstate):
        @pl.when(pl.program_id(0) == 0)
        def _():
            dstate[...] = jnp.zeros(dstate.shape, f32)
        _, vjp = jax.vjp(lambda a, g, s: _delta_chunks(a, g, s, True), qkv_ref[...], bg_ref[...], st_ref[0])
        do = [_stack([do_ref[b, i * CH:(i + 1) * CH, h * DND:(h + 1) * DND] for b in range(B) for h in range(DNH)]) for i in range(n)]
        dqkv, dbg, ds = vjp((do, dstate[...]))
        dqkv_ref[...] = dqkv
        dbg_ref[...] = dbg
        dstate[...] = ds

    rev = lambda c: steps - 1 - c
    at = lambda c: lambda: pl.program_id(0) == c
    return pl.pallas_call(
        _ride(body, 4, 2, 1, comm, at(0), at(steps - 1), at(steps - 1)), name="delta_bwd", grid=(steps,),
        in_specs=[pl.BlockSpec((B, rows, 3 * DNH * DND), lambda c: (0, rev(c), 0)), pl.BlockSpec((B, rows, LANE), lambda c: (0, rev(c), 0)),
                  pl.BlockSpec((1, G, DND, DND), lambda c: (rev(c), 0, 0, 0)),
                  pl.BlockSpec((B, rows, DNH * DND), lambda c: (0, rev(c), 0))] + comm.specs,
        out_specs=[pl.BlockSpec((B, rows, 3 * DNH * DND), lambda c: (0, rev(c), 0)),
                   pl.BlockSpec((B, rows, LANE), lambda c: (0, rev(c), 0))] + comm.specs,
        out_shape=[jax.ShapeDtypeStruct((B, S, 3 * DNH * DND), f32), jax.ShapeDtypeStruct((B, S, LANE), f32)] + comm.out_shape,
        scratch_shapes=[pltpu.VMEM((G, DND, DND), f32)] + comm.scratch, compiler_params=_cp(("arbitrary",)),
    )(qkvn, bg, states, do, *comm.arrs)


GELU_C0, GELU_C1 = math.sqrt(2.0 / math.pi), 0.044715


def _ffn_specs(S):
    nblk = DFF // LANE
    return [pl.BlockSpec((1, S, LANE), lambda i, b: (b, 0, i)), pl.BlockSpec((1, S, LANE), lambda i, b: (b, 0, nblk + i)),
            pl.BlockSpec((FK, LANE), lambda i, b: (0, i)), pl.BlockSpec((FK, LANE), lambda i, b: (0, nblk + i))]


def ffnconv_fwd(up, conv_w):
    B, S, _ = up.shape
    rows = min(FFN_ROWS, S)

    def body(g_ref, v_ref, gw_ref, vw_ref, o_ref, xg, xv):
        _stage_rows(xg, g_ref[0].astype(f32))
        _stage_rows(xv, v_ref[0].astype(f32))
        gw, vw = gw_ref[...], vw_ref[...]
        for r in range(0, S, rows):
            g, _ = _conv_rows(xg, gw, FK, r, rows)
            v, _ = _conv_rows(xv, vw, FK, r, rows)
            t = jnp.tanh(GELU_C0 * (g * (1.0 + GELU_C1 * (g * g))))
            o_ref[0, pl.ds(r, rows), :] = (0.5 * g * (1.0 + t) * v).astype(o_ref.dtype)

    return pl.pallas_call(
        body, name="ffnconv_fwd", grid=(DFF // LANE, B), in_specs=_ffn_specs(S),
        out_specs=pl.BlockSpec((1, S, LANE), lambda i, b: (b, 0, i)), out_shape=jax.ShapeDtypeStruct((B, S, DFF), bf16),
        scratch_shapes=[pltpu.VMEM((S + 8, LANE), f32)] * 2, compiler_params=_cp(("parallel", "parallel")),
    )(up, up, conv_w, conv_w)


def ffnconv_bwd(up, conv_w, dact, comm):
    B, S, _ = up.shape
    rows = min(FFN_ROWS, S)

    def body(g_ref, v_ref, gw_ref, vw_ref, dy_ref, dx_ref, dw_ref, xg, xv, dg, dv):
        _stage_rows(xg, g_ref[0].astype(f32))
        _stage_rows(xv, v_ref[0].astype(f32))
        gw, vw = gw_ref[...], vw_ref[...]
        dgw = [jnp.zeros((8, LANE), f32) for _ in range(FK)]
        dvw = [jnp.zeros((8, LANE), f32) for _ in range(FK)]
        for r in range(0, S, rows):
            g, gwins = _conv_rows(xg, gw, FK, r, rows)
            v, vwins = _conv_rows(xv, vw, FK, r, rows)
            g2 = g * g
            t = jnp.tanh(GELU_C0 * (g * (1.0 + GELU_C1 * g2)))
            half = 0.5 * (1.0 + t)
            dgelu = half + (0.5 * GELU_C0) * g * (1.0 - t * t) * (1.0 + (3.0 * GELU_C1) * g2)
            dy = dy_ref[0, pl.ds(r, rows), :].astype(f32)
            dvc = dy * (g * half)
            dgc = dy * v * dgelu
            dg[pl.ds(r, rows), :] = dgc
            dv[pl.ds(r, rows), :] = dvc
            for j in range(FK):
                dgw[j] = dgw[j] + _fold8(dgc * gwins[j])
                dvw[j] = dvw[j] + _fold8(dvc * vwins[j])
        dg[S:S + 8] = jnp.zeros((8, LANE), f32)
        dv[S:S + 8] = jnp.zeros((8, LANE), f32)
        for r in range(0, S, rows):
            dx_ref[0, 0, pl.ds(r, rows), :] = _conv_rows_t(dg, gw, FK, r, rows).astype(dx_ref.dtype)
            dx_ref[1, 0, pl.ds(r, rows), :] = _conv_rows_t(dv, vw, FK, r, rows).astype(dx_ref.dtype)

        @pl.when(pl.program_id(1) == 0)
        def _():
            dw_ref[...] = jnp.zeros(dw_ref.shape, f32)
        dw_ref[0] += jnp.concatenate([jnp.sum(d, axis=0, keepdims=True) for d in dgw], axis=0)
        dw_ref[1] += jnp.concatenate([jnp.sum(d, axis=0, keepdims=True) for d in dvw], axis=0)

    nblk = DFF // LANE
    at = lambda i, b: lambda: (pl.program_id(0) == i) & (pl.program_id(1) == b)
    return pl.pallas_call(
        _ride(body, 5, 2, 4, comm, at(0, 0), at(nblk - 1, B - 1), at(nblk - 1, B - 1)), name="ffnconv_bwd", grid=(nblk, B),
        in_specs=_ffn_specs(S) + [pl.BlockSpec((1, S, LANE), lambda i, b: (b, 0, i))] + comm.specs,
        out_specs=[pl.BlockSpec((2, 1, S, LANE), lambda i, b: (0, b, 0, i)),
                   pl.BlockSpec((2, FK, LANE), lambda i, b: (0, 0, i))] + comm.specs,
        out_shape=[jax.ShapeDtypeStruct((2, B, S, DFF), bf16), jax.ShapeDtypeStruct((2, FK, DFF), f32)] + comm.out_shape,
        scratch_shapes=[pltpu.VMEM((S + 8, LANE), f32)] * 4 + comm.scratch, compiler_params=_cp(("arbitrary", "arbitrary")),
    )(up, up, conv_w, conv_w, dact, *comm.arrs)


def ada_fwd(c_all, ada_w, ada_b):
    def body(c_ref, w_ref, b_ref, o_ref):
        c = c_ref[...]
        act = (c * jax.nn.sigmoid(c)).astype(bf16)
        o_ref[...] = jnp.dot(act, w_ref[...].astype(bf16), preferred_element_type=f32) + b_ref[...]

    return pl.pallas_call(body, name="ada_fwd", out_shape=jax.ShapeDtypeStruct((c_all.shape[0], ada_w.shape[1]), f32),
                          compiler_params=pltpu.CompilerParams(vmem_limit_bytes=VMEM_LIMIT))(c_all, ada_w, ada_b)


def ada_bwd(c_all, dmod):
    def body(c_ref, d_ref, o_ref):
        c = c_ref[...]
        act = (c * jax.nn.sigmoid(c)).astype(bf16)
        o_ref[...] = lax.dot_general(act, d_ref[...].astype(bf16), (((0,), (0,)), ((), ())), preferred_element_type=f32)

    return pl.pallas_call(body, name="ada_bwd", out_shape=jax.ShapeDtypeStruct((c_all.shape[1], dmod.shape[1]), f32),
                          compiler_params=pltpu.CompilerParams(vmem_limit_bytes=VMEM_LIMIT))(c_all, dmod)


def loss_head(h1, y2, target, g2, w):
    def fn(t, b, c):
        h, y, tg = [v.astype(f32) for v in t]

        def loss_fn(h, y, g, w):
            e = h + g * _rms(y, w) - tg
            return 0.5 * jnp.sum(jnp.mean(e * e, axis=-1))

        loss, grads = jax.value_and_grad(loss_fn, argnums=(0, 1, 2, 3))(h, y, b[0], c[0])
        return [grads[0], grads[1]], [grads[2], grads[3], jnp.full((1, LANE), loss, f32)]

    return rowcall("loss_head", fn, [(h1, D, 0), (y2, D, 0), (target, D, 0)], [g2], [w], [(D, f32), (D, bf16)],
                   [(1, D), (1, D), (1, LANE)])


def adamw(w, gparts, m, v, name):
    R, C = w.shape
    P = gparts.shape[0]
    budget = 2 * 1024 * 1024
    tr, tc = R, C
    if R * C * 4 > budget and R % 8 == 0:
        tr = max(t for t in range(8, R + 1, 8) if R % t == 0 and t * C * 4 <= budget)
    elif R * C * 4 > budget:
        tc = max(t for t in range(LANE, C + 1, LANE) if C % t == 0 and R * t * 4 <= budget)

    def body(w_ref, g_ref, m_ref, v_ref, go, do, mo, vo):
        g = g_ref[0].astype(f32)
        for p in range(1, P):
            g = g + g_ref[p].astype(f32)
        m2 = B1 * m_ref[...] + (1.0 - B1) * g
        v2 = B2 * v_ref[...] + (1.0 - B2) * jnp.square(g)
        m_hat = m2 * (1.0 / (1.0 - B1 ** STEP))
        v_hat = v2 * (1.0 / (1.0 - B2 ** STEP))
        go[...] = g
        do[...] = -LR * (m_hat / (jnp.sqrt(v_hat) + EPS) + WD * w_ref[...])
        mo[...] = m2
        vo[...] = v2

    blk = pl.BlockSpec((tr, tc), lambda i, j: (i, j))
    return pl.pallas_call(
        body, name=name, grid=(R // tr, C // tc), in_specs=[blk, pl.BlockSpec((P, tr, tc), lambda i, j: (0, i, j)), blk, blk],
        out_specs=[blk] * 4, out_shape=[jax.ShapeDtypeStruct((R, C), f32)] * 4, compiler_params=_cp(("parallel", "parallel")),
    )(w, gparts, m, v)


def _pack_w_in(wt):
    aq, ak, av, dqkv, dz, dbeta, da, ga, gd = jnp.split(wt, np.cumsum(IN_SPLITS)[:-1].tolist(), axis=0)
    ba = jnp.pad(jnp.concatenate([dbeta, da], axis=0), ((0, LANE - 2 * DNH), (0, 0)))
    return jnp.concatenate([ga, gd, aq, dqkv, dz, ak, av, ba], axis=0)


def _unpack_w_in(p):
    row = lambda cb, n: p[cb * LANE: cb * LANE + n]
    ba = row(CB_BA, 2 * DNH)
    return jnp.concatenate([row(CB_AQ, HQ * HD), row(CB_AK, HKV * HD), row(CB_AV, HKV * HD), row(CB_DQKV, 3 * DNH * DND),
                            row(CB_DZ, DNH * DND), ba[:DNH], ba[DNH:], row(CB_GA, D), row(CB_GD, D)], axis=0)


def _cols_gathered(g):
    return g.transpose(1, 0, 2).reshape(g.shape[1], NDEV * g.shape[2])


def _cols_split(w):
    r = w.shape[0]
    return w.reshape(r, NDEV, w.shape[1] // NDEV).transpose(1, 0, 2)


def kernel(x, c, ada_w, ada_b, norm_mix_pre, norm_mix_post, norm_ffn_pre, norm_ffn_post, w_in, dn_conv_w, dn_a_log, dn_dt_bias, dn_norm_w, attn_sinks, rel_bias, w_attn_branch, w_dn_branch, w_out, ffn_w_up, ffn_conv_w, ffn_w_down, loss_target, m_ada_w, m_ada_b, m_norm_mix_pre, m_norm_mix_post, m_norm_ffn_pre, m_norm_ffn_post, m_w_in, m_dn_conv_w, m_dn_a_log, m_dn_dt_bias, m_dn_norm_w, m_attn_sinks, m_rel_bias, m_w_attn_branch, m_w_dn_branch, m_w_out, m_ffn_w_up, m_ffn_conv_w, m_ffn_w_down, v_ada_w, v_ada_b, v_norm_mix_pre, v_norm_mix_post, v_norm_ffn_pre, v_norm_ffn_post, v_w_in, v_dn_conv_w, v_dn_a_log, v_dn_dt_bias, v_dn_norm_w, v_attn_sinks, v_rel_bias, v_w_attn_branch, v_w_dn_branch, v_w_out, v_ffn_w_up, v_ffn_conv_w, v_ffn_w_down):
    B, S, _ = x.shape
    T = B * S
    me = 4 * lax.axis_index("x") + 2 * lax.axis_index("y") + lax.axis_index("c")
    big = dict(w_in=w_in, dn_conv_w=dn_conv_w, w_attn_branch=w_attn_branch, w_dn_branch=w_dn_branch, w_out=w_out,
               ffn_w_up=ffn_w_up, ffn_conv_w=ffn_conv_w, ffn_w_down=ffn_w_down)
    big_names = list(big)

    first, mid, late = ["w_in", "dn_conv_w"], ["w_attn_branch", "w_dn_branch", "w_out"], ["ffn_w_up", "ffn_conv_w"]
    transposed = ("w_in", "ffn_w_up")
    local = lambda n, a: a[0].T if n in transposed else a[0]
    shard = lambda names: [local(n, big[n]).astype(bf16) for n in names]
    *got, c_all = _exchange(shard(first) + [c], "gather_w_in", two_level=True)
    gw = dict(zip(first, got))
    c_all = c_all.reshape(NDEV * B, D)

    wp = _pack_w_in(gw["w_in"].reshape(IN_DIM, D))
    conv_dn = _cols_gathered(gw["dn_conv_w"]).astype(f32)

    ncol = ada_w.shape[2]
    ada_b_mine = lax.dynamic_slice_in_dim(ada_b, me * ncol, ncol, axis=1)
    mod_cols = ada_fwd(c_all, ada_w[0], ada_b_mine)
    (mod_g,) = _exchange([mod_cols], "gather_mod")
    mod = lax.dynamic_slice_in_dim(mod_g, me * B, B, axis=1).transpose(1, 0, 2).reshape(B, NMOD * D)
    sh1, sc1, g1, sh2, sc2, g2 = [mod[:, i * D:(i + 1) * D].reshape(B, 1, D) for i in range(NMOD)]

    onehot = (jnp.asarray(_bucket_table()).reshape(1, -1) == jnp.arange(NBUCK, dtype=jnp.int32)[:, None]).astype(f32)
    bias = mm(rel_bias.T, onehot, "nn", f32, "bias_table", tn=8192, precision=HI).reshape(HQ, WIN, 2 * WIN)
    sinks = attn_sinks.reshape(HQ, 1, 1)
    a_log_pad = jnp.pad(dn_a_log, ((0, 0), (DNH, LANE - 2 * DNH)))
    dt_bias_pad = jnp.pad(dn_dt_bias, ((0, 0), (DNH, LANE - 2 * DNH)))

    (u1,) = rowcall_fwd("mix_pre", f_rms_mod, [(x, D, 0)], [sc1, sh1], [norm_mix_pre], [(D, bf16)])
    proj, gw["ffn_w_down"] = mm(u1.reshape(T, D), wp, "nt", bf16, "proj", tm=512, tn=CB_BA * LANE, b_cols=(0, 1),
                                comm=_Comm(shard(["ffn_w_down"]), two_level=True))
    proj = proj.reshape(B, S, CB_BA * LANE)
    ba = mm(u1.reshape(T, D), wp, "nt", f32, "proj_ba", tn=LANE, b_cols=(CB_BA, 1)).reshape(B, S, LANE)
    ya, *got = attn_fwd(proj, bias, sinks, _Comm(shard(mid), two_level=True))
    gw.update(zip(mid, got))
    wa = _cols_gathered(gw["w_attn_branch"])
    wd = _cols_gathered(gw["w_dn_branch"])
    wo = gw["w_out"].reshape(D, D)
    qkvn = dnconv_fwd(proj, conv_dn)
    (bg,) = rowcall_fwd("dn_gate", f_gate, [(ba, LANE, 0)], [], [a_log_pad, dt_bias_pad], [(LANE, f32)])
    o_dn, states, *got = delta_fwd(qkvn, bg, _Comm(shard(late), two_level=True))
    gw.update(zip(late, got))
    wup = gw["ffn_w_up"].reshape(2 * DFF, D)
    conv_ffn = _cols_gathered(gw["ffn_conv_w"]).astype(f32)
    wdown = gw["ffn_w_down"].reshape(DFF, D)
    (yd,) = rowcall_fwd("dn_out", f_dnout, [(o_dn, DNH * DND, 0), (proj, DNH * DND, CB_DZ // 4)], [], [dn_norm_w], [(DNH * DND, bf16)])
    pa = mm(ya.reshape(T, HQ * HD), wa, "nn", bf16, "attn_branch").reshape(B, S, D)
    pd = mm(yd.reshape(T, DNH * DND), wd, "nn", bf16, "dn_branch").reshape(B, S, D)
    merge_tok = [(proj, D, CB_GA // 8), (proj, D, CB_GD // 8), (pa, D, 0), (pd, D, 0)]
    (merged,) = rowcall_fwd("merge", f_merge, merge_tok, [], [], [(D, bf16)])
    y1 = mm(merged.reshape(T, D), wo, "nn", bf16, "mix_out").reshape(B, S, D)
    post_pre = ([(x, D, 0), (y1, D, 0)], [g1, sc2, sh2], [norm_mix_post, norm_ffn_pre])
    h1, u2 = rowcall_fwd("mix_post_ffn_pre", f_post_pre, *post_pre, [(D, f32), (D, bf16)])
    up = mm(u2.reshape(T, D), wup, "nt", bf16, "ffn_up", tn=2816).reshape(B, S, 2 * DFF)
    act = ffnconv_fwd(up, conv_ffn)
    y2 = mm(act.reshape(T, DFF), wdown, "nn", bf16, "ffn_down", tk=2816).reshape(B, S, D)

    dh1_a, dy2, dg2, dw_ffn_post, loss_b = loss_head(h1, y2, loss_target, g2, norm_ffn_post)
    dy2f = dy2.reshape(T, D)
    dact = mm(dy2f, wdown, "nt", bf16, "ffn_down_dx", tn=2816).reshape(B, S, DFF)
    g_wdown = mm(act.reshape(T, DFF), dy2f, "tn", bf16, "ffn_down_dw", tm=2816, tn=512, tk=4096)
    in_flight = []

    def send_off(d, tag):
        in_flight.append((d, _scatter_start([a.astype(bf16) for a in d.values()], "scatter_" + tag + "_start")))
        return in_flight[-1][1][-1][0, 0]

    started = send_off(dict(ffn_w_down=g_wdown.reshape(NDEV, DFF // NDEV, D)), "ffn_down")
    dup, g_conv_ffn = ffnconv_bwd(up, conv_ffn + started, dact, _NoComm())
    dupf = dup.reshape(2, T, DFF)
    g_conv_ffn = g_conv_ffn.transpose(1, 0, 2).reshape(FK, 2 * DFF)
    du2 = mm(dupf, wup, "nn", bf16, "ffn_up_dx", tk=2816).reshape(B, S, D)
    g_wup = mm(dupf, u2.reshape(T, D), "tn", bf16, "ffn_up_dw", tm=1408, tk=2048)
    started = send_off(dict(ffn_w_up=g_wup.reshape(NDEV, 2 * DFF // NDEV, D), ffn_conv_w=_cols_split(g_conv_ffn)), "ffn_up")
    post_pre = (post_pre[0], [g1 + started, sc2, sh2], post_pre[2])
    dh1, dy1, dg1, dsc2, dsh2, dw_mix_post, dw_ffn_pre = rowcall_bwd(
        "mix_post_ffn_pre_bwd", f_post_pre, *post_pre, [(dh1_a, D, 0), (du2, D, 0)], [(0, f32), (1, bf16)])
    dy1f = dy1.reshape(T, D)
    dmerged = mm(dy1f, wo, "nt", bf16, "mix_out_dx").reshape(B, S, D)
    g_wo = mm(merged.reshape(T, D), dy1f, "tn", bf16, "mix_out_dw", tk=2048)
    dproj = lax.empty((B, S, NP), bf16)
    dproj, dpa, dpd = rowcall_bwd("merge_bwd", f_merge, merge_tok, [], [], [(dmerged, D, 0)],
                                  [(0, bf16), (1, bf16), (2, bf16), (3, bf16)], join_first=2, into=(dproj, CB_GA // 16))
    dpaf, dpdf = dpa.reshape(T, D), dpd.reshape(T, D)
    dya = mm(dpaf, wa, "nt", bf16, "attn_branch_dx").reshape(B, S, HQ * HD)
    g_wa = mm(ya.reshape(T, HQ * HD), dpaf, "tn", bf16, "attn_branch_dw", tk=2048)
    dyd = mm(dpdf, wd, "nt", bf16, "dn_branch_dx").reshape(B, S, DNH * DND)
    g_wd = mm(yd.reshape(T, DNH * DND), dpdf, "tn", bf16, "dn_branch_dw", tk=2048)
    dproj, do_dn, dw_dn_norm = rowcall_bwd("dn_out_bwd", f_dnout, [(o_dn, DNH * DND, 0), (proj, DNH * DND, CB_DZ // 4)], [], [dn_norm_w],
                                           [(dyd, DNH * DND, 0)], [(1, bf16), (0, f32)], into=(dproj, CB_DZ // 4))
    started = send_off(dict(w_attn_branch=_cols_split(g_wa), w_dn_branch=_cols_split(g_wd), w_out=g_wo.reshape(NDEV, D // NDEV, D)), "branches")
    dqkvn, dbg = delta_bwd(qkvn, bg + started, states, do_dn, _NoComm())
    dproj, da_log_pad, ddt_bias_pad = rowcall_bwd("dn_gate_bwd", f_gate, [(ba, LANE, 0)], [], [a_log_pad, dt_bias_pad],
                                                  [(dbg, LANE, 0)], [(0, bf16)], into=(dproj, CB_BA))
    dproj, g_conv_dn = dnconv_bwd(proj, conv_dn, dqkvn, dproj)
    dproj, dk, dv, dbias, dsinks = attn_bwd(proj, bias, sinks, dya, dproj, _NoComm())
    dproj = lax.dynamic_update_slice(dproj, jnp.concatenate([dk, dv], axis=2), (0, 0, CB_AK * LANE)).reshape(T, NP)
    g_wp = mm(dproj, u1.reshape(T, D), "tn", bf16, "proj_dw", tm=1664, tk=1024)
    started = send_off(dict(w_in=_unpack_w_in(g_wp).reshape(NDEV, IN_DIM // NDEV, D), dn_conv_w=_cols_split(g_conv_dn)), "w_in")
    du1 = mm(dproj, wp, "nn", bf16, "proj_dx", tm=512, tk=NP).reshape(B, S, D)
    grad_x, dsc1, dsh1, dw_mix_pre = rowcall_bwd("mix_pre_bwd", f_rms_mod, [(x, D, 0)], [sc1 + started, sh1], [norm_mix_pre],
                                                 [(du1, D, 0)], [(0, f32)], add=(dh1, D, 0))
    g_rel = mm(dbias.reshape(HQ, WIN * 2 * WIN), onehot, "nt", f32, "rel_bias_dw", tk=8192, precision=HI)

    dmod = jnp.concatenate([dsh1, dsc1, dg1, dsh2, dsc2, dg2], axis=2).reshape(B, NMOD * D)

    zrow = lambda a: jnp.concatenate([a.reshape(1, -1), jnp.zeros((B - 1, a.size), f32)], axis=0)
    small_g = jnp.concatenate([
        dmod, dw_mix_pre.reshape(B, D), dw_mix_post.reshape(B, D), dw_ffn_pre.reshape(B, D), dw_ffn_post.reshape(B, D),
        da_log_pad.reshape(B, LANE)[:, DNH:2 * DNH], ddt_bias_pad.reshape(B, LANE)[:, DNH:2 * DNH], dw_dn_norm.reshape(B, DND),
        zrow(dsinks), zrow(g_rel.T), loss_b.reshape(B, LANE)[:, :1], jnp.zeros((B, SMALL_PAD - SMALL_N - 1), f32)], axis=1)
    (small_all,) = _exchange([small_g], "gather_small")
    dmod_cols = lax.dynamic_slice_in_dim(small_all.reshape(NDEV * B, SMALL_PAD), me * ncol, ncol, axis=1)
    g_ada_w = ada_bwd(c_all, dmod_cols)
    parts = {}
    for i, (d, started) in enumerate(in_flight):
        parts.update(zip(d, _scatter_finish(started, len(d), g_ada_w, "scatter_finish_%d" % i)))
    small_w = dict(ada_b=(ada_b, m_ada_b, v_ada_b), norm_mix_pre=(norm_mix_pre, m_norm_mix_pre, v_norm_mix_pre),
                   norm_mix_post=(norm_mix_post, m_norm_mix_post, v_norm_mix_post), norm_ffn_pre=(norm_ffn_pre, m_norm_ffn_pre, v_norm_ffn_pre),
                   norm_ffn_post=(norm_ffn_post, m_norm_ffn_post, v_norm_ffn_post), dn_a_log=(dn_a_log, m_dn_a_log, v_dn_a_log),
                   dn_dt_bias=(dn_dt_bias, m_dn_dt_bias, v_dn_dt_bias), dn_norm_w=(dn_norm_w, m_dn_norm_w, v_dn_norm_w),
                   attn_sinks=(attn_sinks, m_attn_sinks, v_attn_sinks), rel_bias=(rel_bias, m_rel_bias, v_rel_bias))

    def pack(i, fill):
        row = jnp.concatenate([small_w[n][i].reshape(1, -1) for n, _ in SMALL], axis=1)
        return jnp.pad(row, ((0, 0), (0, SMALL_PAD - SMALL_N)), constant_values=fill)

    small_out = adamw(pack(0, 0.0), small_all.reshape(NDEV * B, 1, SMALL_PAD), pack(1, 0.0), pack(2, 1.0), "adamw_small")
    loss = small_out[0][0, SMALL_N]

    res = {}
    off = 0
    for n, size in SMALL:
        shp = small_w[n][0].shape
        res[n] = [o[:, off:off + size].reshape(shp) for o in small_out]
        off += size
    res["ada_w"] = [o[None] for o in adamw(ada_w[0], g_ada_w[None], m_ada_w[0], v_ada_w[0], "adamw_ada_w")]
    moments = dict(w_in=(m_w_in, v_w_in), dn_conv_w=(m_dn_conv_w, v_dn_conv_w), w_attn_branch=(m_w_attn_branch, v_w_attn_branch),
                   w_dn_branch=(m_w_dn_branch, v_w_dn_branch), w_out=(m_w_out, v_w_out), ffn_w_up=(m_ffn_w_up, v_ffn_w_up),
                   ffn_conv_w=(m_ffn_conv_w, v_ffn_conv_w), ffn_w_down=(m_ffn_w_down, v_ffn_w_down))
    for n in big_names:
        outs = adamw(local(n, big[n]), parts[n], local(n, moments[n][0]), local(n, moments[n][1]), "adamw_" + n)
        res[n] = [(o.T if n in transposed else o)[None] for o in outs]

    order = ["ada_w", "ada_b", "norm_mix_pre", "norm_mix_post", "norm_ffn_pre", "norm_ffn_post", "w_in", "dn_conv_w", "dn_a_log",
             "dn_dt_bias", "dn_norm_w", "attn_sinks", "rel_bias", "w_attn_branch", "w_dn_branch", "w_out", "ffn_w_up", "ffn_conv_w",
             "ffn_w_down"]
    return (loss, grad_x, *[res[n][0] for n in order], *[res[n][1] for n in order], *[res[n][2] for n in order],
            *[res[n][3] for n in order])
```

```python
import functools
import math

import numpy as np
import jax
import jax.numpy as jnp
from jax import lax
from jax.experimental import pallas as pl
from jax.experimental.pallas import tpu as pltpu

f32 = jnp.float32
bf16 = jnp.bfloat16
HI = lax.Precision.HIGHEST
MID = lax.Precision.HIGH
MESH = pl.DeviceIdType.MESH

NDEV = 8
D = 1024
HQ, HKV, HD, WIN, NBUCK, MAXDIST = 8, 2, 64, 128, 32, 128
DNH, DND, DNK, CH = 4, 128, 4, 64
DFF, FK = 2816, 3
NMOD = 6
RMS_EPS = 1e-6
L2_EPS = 1e-6
NEG_INF = -1e30
LR, B1, B2, EPS, WD, STEP = 0.001, 0.9, 0.999, 1e-08, 0.01, 10

LANE = 128
CB_GA, CB_GD, CB_AQ, CB_DQKV, CB_DZ, CB_AK, CB_AV, CB_BA, NPB = 0, 8, 16, 20, 32, 36, 37, 38, 39
NP = NPB * LANE
IN_SPLITS = (HQ * HD, HKV * HD, HKV * HD, 3 * DNH * DND, DNH * DND, DNH, DNH, D, D)
IN_DIM = sum(IN_SPLITS)
VMEM_LIMIT = 56 * 1024 * 1024

SMALL = (("ada_b", NMOD * D), ("norm_mix_pre", D), ("norm_mix_post", D), ("norm_ffn_pre", D), ("norm_ffn_post", D),
         ("dn_a_log", DNH), ("dn_dt_bias", DNH), ("dn_norm_w", DND), ("attn_sinks", HQ), ("rel_bias", NBUCK * HQ))
SMALL_N = sum(n for _, n in SMALL)
SMALL_PAD = 10752


def _cp(sem):
    return pltpu.CompilerParams(dimension_semantics=sem, vmem_limit_bytes=VMEM_LIMIT)


def _pick(dim, target):
    if dim <= target:
        return dim
    best = None
    for d in range(LANE, target + 1, LANE):
        if dim % d == 0:
            best = d
    assert best is not None, (dim, target)
    return best


def _me():
    x, y, c = lax.axis_index("x"), lax.axis_index("y"), lax.axis_index("c")
    return x, y, c, 4 * x + 2 * y + c


def _peer(x, y, c, k):
    px = 1 - x if k & 4 else x
    py = 1 - y if k & 2 else y
    pc = 1 - c if k & 1 else c
    return (px, py, pc), 4 * px + 2 * py + pc


class _Comm:
    def __init__(self, arrs, scatter=False, two_level=False):
        assert not (scatter and two_level)
        self.arrs, self.n, self.scatter, self.two_level = list(arrs), len(arrs), scatter, two_level
        if scatter:
            self.out_shape = [jax.ShapeDtypeStruct(a.shape, a.dtype) for a in arrs]
        else:
            self.out_shape = [jax.ShapeDtypeStruct((NDEV,) + a.shape, a.dtype) for a in arrs]
        nsem = self.n * (NDEV - 1)
        self.scratch = [pltpu.SemaphoreType.DMA((nsem,)), pltpu.SemaphoreType.DMA((nsem,)), pltpu.SemaphoreType.DMA((self.n,))]
        self.specs = [pl.BlockSpec(memory_space=pl.ANY)] * self.n

    def phases(self, ins, out, send, recv, loc):
        x, y, c, me = _me()

        def remote(a, k, src, dst, to):
            s = a * (NDEV - 1) + k - 1
            return pltpu.make_async_remote_copy(src_ref=src, dst_ref=dst, send_sem=send.at[s], recv_sem=recv.at[s],
                                                device_id=to, device_id_type=MESH)

        def local(a):
            return pltpu.make_async_copy(ins[a].at[me] if self.scatter else ins[a], out[a].at[me], loc.at[a])

        if not self.two_level:
            def mine(a, k):
                peer, pid = _peer(x, y, c, k)
                return remote(a, k, ins[a].at[pid] if self.scatter else ins[a], out[a].at[me], peer)

            def theirs(a, k):
                peer, pid = _peer(x, y, c, k)
                return remote(a, k, ins[a].at[pid] if self.scatter else ins[a], out[a].at[pid], peer)

            def start():
                for a in range(self.n):
                    local(a).start()
                    for k in range(1, NDEV):
                        mine(a, k).start()

            def forward():
                pass

            def finish():
                for a in range(self.n):
                    for k in range(1, NDEV):
                        mine(a, k).wait_send()
                    for k in range(1, NDEV):
                        theirs(a, k).wait_recv()
                    local(a).wait()

            return start, forward, finish

        sibling = (x, y, 1 - c)
        chips = [(1 - x, y), (x, 1 - y), (1 - x, 1 - y)]
        slot = lambda px, py, pc: 4 * px + 2 * py + pc

        def own(a, k, to):
            return remote(a, k, ins[a], out[a].at[me], to)

        def landed(a, k, frm):
            return remote(a, k, ins[a], out[a].at[slot(*frm)], frm)

        def passed(a, j):
            rows = out[a].at[slot(*chips[j], c)]
            return remote(a, 5 + j, rows, rows, sibling)

        def start():
            for a in range(self.n):
                local(a).start()
                own(a, 1, sibling).start()
                for j, chip in enumerate(chips):
                    own(a, 2 + j, (*chip, c)).start()

        def forward():
            for a in range(self.n):
                for j, chip in enumerate(chips):
                    landed(a, 2 + j, (*chip, c)).wait_recv()
                    passed(a, j).start()

        def finish():
            for a in range(self.n):
                landed(a, 1, sibling).wait_recv()
                for j, chip in enumerate(chips):
                    remote(a, 5 + j, ins[a], out[a].at[slot(*chip, 1 - c)], sibling).wait_recv()
                own(a, 1, sibling).wait_send()
                for j, chip in enumerate(chips):
                    own(a, 2 + j, (*chip, c)).wait_send()
                    passed(a, j).wait_send()
                local(a).wait()

        return start, forward, finish


class _NoComm:
    n, arrs, out_shape, specs, scratch = 0, [], [], [], []

    def phases(self, *_):
        return (lambda: None,) * 3


def _ride(body, n_in, n_out, n_scr, comm, first, mid, last):
    k = comm.n

    def wrapped(*refs):
        ins, cins = refs[:n_in], refs[n_in:n_in + k]
        o0 = n_in + k
        outs, couts = refs[o0:o0 + n_out], refs[o0 + n_out:o0 + n_out + k]
        s0 = o0 + n_out + k
        scr, sems = refs[s0:s0 + n_scr], refs[s0 + n_scr:]
        start, forward, finish = comm.phases(cins, couts, *sems)
        pl.when(first())(start)
        body(*ins, *outs, *scr)
        pl.when(mid())(forward)
        pl.when(last())(finish)

    return wrapped


def _scatter_start(arrs, name):
    n = len(arrs)

    def body(*refs):
        ins, lands, send, recv, own, token = refs[:n], refs[n:2 * n], refs[2 * n], refs[2 * n + 1], refs[2 * n + 2], refs[-1]
        x, y, c, me = _me()
        for a in range(n):
            pltpu.make_async_copy(ins[a].at[me], lands[a].at[me], own.at[a]).start()
            for k in range(1, NDEV):
                peer, pid = _peer(x, y, c, k)
                s = a * (NDEV - 1) + k - 1
                pltpu.make_async_remote_copy(src_ref=ins[a].at[pid], dst_ref=lands[a].at[me], send_sem=send.at[s], recv_sem=recv.at[s],
                                             device_id=peer, device_id_type=MESH).start()
        token[...] = jnp.zeros(token.shape, token.dtype)

    hbm, sem = pl.BlockSpec(memory_space=pltpu.HBM), pl.BlockSpec(memory_space=pltpu.SEMAPHORE)
    nsem = n * (NDEV - 1)
    thru = [pltpu.HBM(a.shape, a.dtype) for a in arrs]
    return pl.pallas_call(
        body, name=name, in_specs=[hbm] * (2 * n),
        out_shape=(pltpu.SemaphoreType.DMA((nsem,)), pltpu.SemaphoreType.DMA((nsem,)), pltpu.SemaphoreType.DMA((n,)), *thru, *thru,
                   jax.ShapeDtypeStruct((8, LANE), f32)),
        out_specs=(sem, sem, sem, *[hbm] * (2 * n), pl.BlockSpec(memory_space=pltpu.VMEM)),
        input_output_aliases={i: 3 + i for i in range(2 * n)},
        compiler_params=pltpu.CompilerParams(has_side_effects=pltpu.SideEffectType.DATAFLOW_SIDE_EFFECTING),
    )(*[pltpu.with_memory_space_constraint(a, pltpu.HBM) for a in arrs],
      *[pltpu.with_memory_space_constraint(lax.empty(a.shape, a.dtype), pltpu.HBM) for a in arrs])


def _scatter_finish(started, n, after, name):
    send, recv, own, *rest = started
    srcs, lands = rest[:n], rest[n:2 * n]

    def body(*refs):
        ins, lnd, send_ref, recv_ref, own_ref = refs[:n], refs[n:2 * n], refs[2 * n], refs[2 * n + 1], refs[2 * n + 2]
        x, y, c, me = _me()
        for a in range(n):
            pltpu.make_async_copy(ins[a].at[me], lnd[a].at[me], own_ref.at[a]).wait()
            for k in range(1, NDEV):
                peer, pid = _peer(x, y, c, k)
                s = a * (NDEV - 1) + k - 1
                cp = pltpu.make_async_remote_copy(src_ref=ins[a].at[pid], dst_ref=lnd[a].at[pid], send_sem=send_ref.at[s],
                                                  recv_sem=recv_ref.at[s], device_id=peer, device_id_type=MESH)
                cp.wait_send()
                cp.wait_recv()

    hbm, sem = pl.BlockSpec(memory_space=pltpu.HBM), pl.BlockSpec(memory_space=pltpu.SEMAPHORE)
    thru = [pltpu.HBM(a.shape, a.dtype) for a in srcs]
    out = pl.pallas_call(
        body, name=name, in_specs=[hbm] * (2 * n) + [sem, sem, sem, pl.BlockSpec(memory_space=pl.ANY)],
        out_shape=(*thru, *thru), out_specs=tuple([hbm] * (2 * n)), input_output_aliases={i: i for i in range(2 * n)},
        compiler_params=pltpu.CompilerParams(has_side_effects=pltpu.SideEffectType.DATAFLOW_SIDE_EFFECTING),
    )(*srcs, *lands, send, recv, own, after)
    return list(out[n:])


def _exchange(arrs, name, scatter=False, two_level=False):
    comm = _Comm(arrs, scatter, two_level)

    def body(*refs):
        start, forward, finish = comm.phases(refs[:comm.n], refs[comm.n:2 * comm.n], *refs[2 * comm.n:])
        start()
        forward()
        finish()

    return pl.pallas_call(body, name=name, out_shape=comm.out_shape, in_specs=comm.specs, out_specs=comm.specs,
                          scratch_shapes=comm.scratch, compiler_params=pltpu.CompilerParams(has_side_effects=True))(*arrs)


def mm(a, b, mode, out_dtype, name, tm=1024, tn=1024, tk=1024, precision=None, comm=None, b_cols=None):
    a_parts = a.shape[0] if a.ndim == 3 else 1
    b_parts = b.shape[0] if b.ndim == 3 else 1
    assert b_parts == 1 or mode == "tn"
    ash, bsh = (a.shape[-2], a.shape[-1] * a_parts), b.shape[-2:]
    if mode == "nn":
        (M, K), (K2, N) = ash, bsh
    elif mode == "nt":
        (M, K), (N, K2) = ash, bsh
    else:
        (K, M), (K2, N) = ash, (bsh[0], bsh[1] * b_parts)
    assert K == K2, (name, a.shape, b.shape)
    col0 = 0
    if b_cols is not None:
        assert mode in ("nn", "nt") and tn % LANE == 0
        col0, N = b_cols[0], b_cols[1] * tn
    if mode == "tn":
        tm, tn, tk = _pick(M // a_parts, tm), _pick(N // b_parts, tn), _pick(K, tk)
    else:
        tm, tn, tk = _pick(M, tm), _pick(N // b_parts, tn), _pick(K // a_parts, tk)
    nk = K // tk
    if mode == "tn" and a_parts > 1:
        per = M // tm // a_parts
        a_spec = pl.BlockSpec((None, tk, tm), lambda i, j, k: (i // per, k, i % per))
    elif mode == "tn":
        a_spec = pl.BlockSpec((tk, tm), lambda i, j, k: (k, i))
    elif a_parts > 1:
        per = nk // a_parts
        a_spec = pl.BlockSpec((None, tm, tk), lambda i, j, k: (k // per, i, k % per))
    else:
        a_spec = pl.BlockSpec((tm, tk), lambda i, j, k: (i, k))
    if mode == "nt":
        b_spec = pl.BlockSpec((tn, tk), lambda i, j, k: (col0 + j, k))
    elif b_parts > 1:
        per = N // tn // b_parts
        b_spec = pl.BlockSpec((None, tk, tn), lambda i, j, k: (j // per, k, j % per))
    else:
        b_spec = pl.BlockSpec((tk, tn), lambda i, j, k: (k, col0 + j))
    dims = {"nn": ((1,), (0,)), "nt": ((1,), (1,)), "tn": ((0,), (0,))}[mode]

    def body(a_ref, b_ref, o_ref, *scr):
        p = lax.dot_general(a_ref[...], b_ref[...], (dims, ((), ())), preferred_element_type=f32, precision=precision)
        if nk == 1:
            o_ref[...] = p.astype(o_ref.dtype)
        else:
            acc = scr[0]
            k = pl.program_id(2)

            @pl.when(k == 0)
            def _():
                acc[...] = p

            @pl.when(k > 0)
            def _():
                acc[...] += p

            @pl.when(k == nk - 1)
            def _():
                o_ref[...] = acc[...].astype(o_ref.dtype)

    grid = (M // tm, N // tn, nk)
    scratch = [pltpu.VMEM((tm, tn), f32)] if nk > 1 else []
    out_spec = pl.BlockSpec((tm, tn), lambda i, j, k: (i, j))
    out_shape = jax.ShapeDtypeStruct((M, N), out_dtype)
    if comm is None:
        return pl.pallas_call(body, name=name, grid=grid, in_specs=[a_spec, b_spec], out_specs=out_spec, out_shape=out_shape,
                              scratch_shapes=scratch, compiler_params=_cp(("parallel", "parallel", "arbitrary")))(a, b)
    at = lambda pos: lambda: functools.reduce(jnp.logical_and, [pl.program_id(d) == p for d, p in enumerate(pos)])
    end = tuple(g - 1 for g in grid)
    return pl.pallas_call(
        _ride(body, 2, 1, len(scratch), comm, at((0, 0, 0)), at(end), at(end)), name=name, grid=grid,
        in_specs=[a_spec, b_spec] + comm.specs, out_specs=[out_spec] + comm.specs, out_shape=[out_shape] + comm.out_shape,
        scratch_shapes=scratch + comm.scratch, compiler_params=_cp(("arbitrary", "arbitrary", "arbitrary")),
    )(a, b, *comm.arrs)


ROW_TILE = 512


def rowcall(name, fn, tok, bat, con, tok_out, acc_out, ts=ROW_TILE, into=None):
    B, S = tok[0][0].shape[:2]
    ts = min(ts, S)
    nt, nb, nc, no, na = len(tok), len(bat), len(con), len(tok_out), len(acc_out)
    nin = nt + nb + nc + (1 if into is not None else 0)

    def body(*refs):
        tr, br, cr = refs[:nt], refs[nt:nt + nb], refs[nt + nb:nt + nb + nc]
        orf, arf = refs[nin:nin + no], refs[nin + no:]
        touts, aouts = fn([r[0] for r in tr], [r[0] for r in br], [r[...] for r in cr])
        for r, v in zip(orf, touts):
            r[0] = v.astype(r.dtype)
        s = pl.program_id(1)
        for r, v in zip(arf, aouts):
            @pl.when(s == 0)
            def _(r=r):
                r[...] = jnp.zeros(r.shape, r.dtype)
            r[0] += v.astype(f32)

    in_specs = [pl.BlockSpec((1, ts, w), lambda b, s, cb=cb: (b, s, cb)) for (_, w, cb) in tok]
    in_specs += [pl.BlockSpec((1,) + a.shape[1:], lambda b, s: (b, 0, 0)) for a in bat]
    in_specs += [pl.BlockSpec(a.shape, lambda b, s, nd=a.ndim: (0,) * nd) for a in con]
    out_specs = [pl.BlockSpec((1, ts, w), lambda b, s: (b, s, 0)) for (w, _) in tok_out]
    out_specs += [pl.BlockSpec((1,) + shp, lambda b, s, nd=len(shp): (b,) + (0,) * nd) for shp in acc_out]
    out_shape = [jax.ShapeDtypeStruct((B, S, w), dt) for (w, dt) in tok_out]
    out_shape += [jax.ShapeDtypeStruct((B,) + shp, f32) for shp in acc_out]
    extra, aliases = [], {}
    if into is not None:
        buf, cb = into
        assert buf.dtype == tok_out[0][1]
        in_specs.append(pl.BlockSpec(memory_space=pl.ANY))
        out_specs[0] = pl.BlockSpec((1, ts, tok_out[0][0]), lambda b, s: (b, s, cb))
        out_shape[0] = jax.ShapeDtypeStruct(buf.shape, buf.dtype)
        extra, aliases = [buf], {nin - 1: 0}
    return pl.pallas_call(
        body, name=name, grid=(B, S // ts), in_specs=in_specs, out_specs=out_specs, out_shape=out_shape,
        input_output_aliases=aliases, compiler_params=_cp(("parallel", "arbitrary")),
    )(*[t[0] for t in tok], *bat, *con, *extra)


def rowcall_fwd(name, f, tok, bat, con, tok_out, ts=ROW_TILE):
    def fn(t, b, c):
        return f([v.astype(f32) for v in t], b, c), []
    return rowcall(name, fn, tok, bat, con, tok_out, [], ts)


def rowcall_bwd(name, f, tok, bat, con, cts, tok_grads, add=None, ts=ROW_TILE, join_first=1, into=None):
    nt, ncts = len(tok), len(cts)

    def fn(t, b, c):
        prim = [v.astype(f32) for v in t[:nt]]
        ct = [v.astype(f32) for v in t[nt:nt + ncts]]
        _, vjp = jax.vjp(lambda tt, bb, cc: f(tt, bb, cc), prim, b, c)
        dt, db, dc = vjp(ct)
        touts = [dt[i] for i, _ in tok_grads]
        if add is not None:
            touts[0] = touts[0] + t[nt + ncts].astype(f32)
        if join_first > 1:
            touts = [jnp.concatenate(touts[:join_first], axis=1)] + touts[join_first:]
        return touts, list(db) + list(dc)

    all_tok = list(tok) + list(cts) + ([add] if add is not None else [])
    tok_out = [(tok[i][1], dt) for i, dt in tok_grads]
    if join_first > 1:
        tok_out = [(sum(w for w, _ in tok_out[:join_first]), tok_out[0][1])] + tok_out[join_first:]
    acc_out = [tuple(a.shape[1:]) for a in bat] + [tuple(a.shape) for a in con]
    return rowcall(name, fn, all_tok, bat, con, tok_out, acc_out, ts, into)


def _rms(y, w):
    return y * lax.rsqrt(jnp.mean(y * y, axis=-1, keepdims=True) + RMS_EPS) * w


def f_rms_mod(t, b, c):
    return [_rms(t[0], c[0]) * (1.0 + b[0]) + b[1]]


def f_post_pre(t, b, c):
    h1 = t[0] + b[0] * _rms(t[1], c[0])
    return [h1, _rms(h1, c[1]) * (1.0 + b[1]) + b[2]]


def f_merge(t, b, c):
    ga, gd, ya, yd = t
    return [jax.nn.sigmoid(ga) * ya + jax.nn.sigmoid(gd) * yd]


def f_dnout(t, b, c):
    o, z = t
    outs = []
    for h in range(DNH):
        sl = slice(h * DND, (h + 1) * DND)
        zh = z[:, sl]
        outs.append(_rms(o[:, sl], c[0]) * (zh * jax.nn.sigmoid(zh)))
    return [jnp.concatenate(outs, axis=1)]


def _softplus(x):
    return jnp.maximum(x, 0.0) + jnp.log(1.0 + jnp.exp(-jnp.abs(x)))


def f_gate(t, b, c):
    ba = t[0]
    a_log, dt_bias = c
    lane = lax.broadcasted_iota(jnp.int32, ba.shape, 1)
    beta = jax.nn.sigmoid(ba)
    g = -jnp.exp(a_log) * _softplus(ba + dt_bias)
    return [jnp.where(lane < DNH, beta, jnp.where(lane < 2 * DNH, g, 0.0))]


def _bucket_table():
    qi = np.arange(WIN)[:, None]
    kj = np.arange(2 * WIN)[None, :]
    dist = np.maximum(WIN + qi - kj, 0)
    max_exact = NBUCK // 2
    scaled = np.log(np.maximum(dist, 1).astype(np.float64) / max_exact) / math.log(MAXDIST / max_exact)
    large = np.minimum(max_exact + (scaled * (NBUCK - max_exact)).astype(np.int32), NBUCK - 1)
    return np.where(dist < max_exact, dist, large).astype(np.int32)


def _attn_mask(n):
    qi = lax.broadcasted_iota(jnp.int32, (WIN, 2 * WIN), 0)
    kj = lax.broadcasted_iota(jnp.int32, (WIN, 2 * WIN), 1)
    dist = WIN + qi - kj
    return (dist >= 0) & (dist < WIN) & ((kj >= WIN) | (n > 0))


def _swap_halves(x):
    return pltpu.roll(x, HD, axis=x.ndim - 1)


@jax.custom_vjp
def _swap_halves_vjp(x):
    return _swap_halves(x)


_swap_halves_vjp.defvjp(lambda x: (_swap_halves(x), None), lambda _, g: (_swap_halves(g),))


def _attn_block(q, kp, kc, vp, vc, bias, sinks, mask, differentiated):
    dot = _bdot_bf16_vjp if differentiated else _bdot_bf16
    swap = _swap_halves_vjp if differentiated else _swap_halves
    B, grp = q.shape[0], HQ // HKV
    upper = lax.broadcasted_iota(jnp.int32, (2 * WIN, LANE), 1) >= HD

    def placed(natural, swapped, j, half):
        keep = upper if half == 1 else ~upper
        return jnp.where(keep, natural if j == half else swapped, 0.0)

    qh, ks, vs = [], [], []
    for b in range(B):
        kb, vb = jnp.concatenate([kp[b], kc[b]], axis=0), jnp.concatenate([vp[b], vc[b]], axis=0)
        kb_sw, vb_sw = swap(kb), swap(vb)
        for h in range(HQ):
            qh.append(q[b, :, (h // 2) * LANE:(h // 2 + 1) * LANE])
            ks.append(placed(kb, kb_sw, h // grp, h % 2))
            vs.append(placed(vb, vb_sw, h // grp, h % 2))
    s = dot(_stack(qh), _stack(ks), 2, 2).reshape(B, HQ, WIN, 2 * WIN) * (HD ** -0.5)
    s = jnp.where(mask, s + bias, NEG_INF)
    m = jnp.maximum(jnp.max(s, axis=-1, keepdims=True), sinks)
    p = jnp.exp(s - m)
    probs = p / (jnp.sum(p, axis=-1, keepdims=True) + jnp.exp(sinks - m))
    o = dot(probs.reshape(B * HQ, WIN, 2 * WIN), _stack(vs), 2, 1)
    return _stack([jnp.concatenate([o[b * HQ + 2 * i] + o[b * HQ + 2 * i + 1] for i in range(HQ // 2)], axis=1) for b in range(B)])


def _attn_specs(B, NB):
    last = NB - 1
    return [
        pl.BlockSpec((B, WIN, HQ * HD), lambda n: (0, jnp.minimum(n, last), CB_AQ // 4)),
        pl.BlockSpec((B, WIN, LANE), lambda n: (0, jnp.clip(n - 1, 0, last), CB_AK)),
        pl.BlockSpec((B, WIN, LANE), lambda n: (0, jnp.minimum(n, last), CB_AK)),
        pl.BlockSpec((B, WIN, LANE), lambda n: (0, jnp.clip(n - 1, 0, last), CB_AV)),
        pl.BlockSpec((B, WIN, LANE), lambda n: (0, jnp.minimum(n, last), CB_AV)),
        pl.BlockSpec((HQ, WIN, 2 * WIN), lambda n: (0, 0, 0)),
        pl.BlockSpec((HQ, 1, 1), lambda n: (0, 0, 0)),
    ]


def attn_fwd(proj, bias, sinks, comm):
    B, S, _ = proj.shape
    NB = S // WIN

    def body(q, kp, kc, vp, vc, bias_ref, sink_ref, o_ref):
        mask = _attn_mask(pl.program_id(0))
        o = _attn_block(*[r[...].astype(f32) for r in (q, kp, kc, vp, vc)], bias_ref[...], sink_ref[...], mask, False)
        o_ref[...] = o.astype(o_ref.dtype)

    at = lambda n: lambda: pl.program_id(0) == n
    return pl.pallas_call(
        _ride(body, 7, 1, 0, comm, at(0), at((3 * NB) // 4), at(NB - 1)), name="attn_fwd", grid=(NB,),
        in_specs=_attn_specs(B, NB) + comm.specs,
        out_specs=[pl.BlockSpec((B, WIN, HQ * HD), lambda n: (0, n, 0))] + comm.specs,
        out_shape=[jax.ShapeDtypeStruct((B, S, HQ * HD), bf16)] + comm.out_shape, scratch_shapes=comm.scratch,
        compiler_params=_cp(("arbitrary",)),
    )(proj, proj, proj, proj, proj, bias, sinks, *comm.arrs)


def attn_bwd(proj, bias, sinks, dy, dproj, comm):
    B, S, _ = proj.shape
    NB = S // WIN
    last = NB - 1

    def body(q, kp, kc, vp, vc, bias_ref, sink_ref, dy_ref, _, dq_ref, dk_ref, dv_ref, dbias_ref, dsink_ref, kcar, vcar):
        n = pl.program_id(0)

        @pl.when(n == 0)
        def _():
            dbias_ref[...] = jnp.zeros(dbias_ref.shape, f32)
            dsink_ref[...] = jnp.zeros(dsink_ref.shape, f32)
            kcar[...] = jnp.zeros(kcar.shape, f32)
            vcar[...] = jnp.zeros(vcar.shape, f32)

        @pl.when(n < NB)
        def _():
            mask = _attn_mask(n)
            _, vjp = jax.vjp(lambda *a: _attn_block(*a, mask, True), *[r[...].astype(f32) for r in (q, kp, kc, vp, vc)],
                             bias_ref[...], sink_ref[...])
            dq, dkp, dkc, dvp, dvc, dbias, dsink = vjp(dy_ref[...].astype(f32))
            dq_ref[...] = dq.astype(dq_ref.dtype)
            dbias_ref[...] += dbias
            dsink_ref[...] += dsink
            dk_ref[...] = (kcar[...] + dkp).astype(dk_ref.dtype)
            dv_ref[...] = (vcar[...] + dvp).astype(dv_ref.dtype)
            kcar[...] = dkc
            vcar[...] = dvc

        @pl.when(n == NB)
        def _():
            dk_ref[...] = kcar[...].astype(dk_ref.dtype)
            dv_ref[...] = vcar[...].astype(dv_ref.dtype)

    in_specs = _attn_specs(B, NB) + [pl.BlockSpec((B, WIN, HQ * HD), lambda n: (0, jnp.minimum(n, last), 0)),
                                     pl.BlockSpec(memory_space=pl.ANY)]
    kv_out = pl.BlockSpec((B, WIN, LANE), lambda n: (0, jnp.maximum(n - 1, 0), 0))
    at = lambda n: lambda: pl.program_id(0) == n
    return pl.pallas_call(
        _ride(body, 9, 5, 2, comm, at(0), at(NB), at(NB)), name="attn_bwd", grid=(NB + 1,),
        in_specs=in_specs + comm.specs, input_output_aliases={8: 0},
        out_specs=[pl.BlockSpec((B, WIN, HQ * HD), lambda n: (0, jnp.minimum(n, last), CB_AQ // 4)), kv_out, kv_out,
                   pl.BlockSpec((HQ, WIN, 2 * WIN), lambda n: (0, 0, 0)), pl.BlockSpec((HQ, 1, 1), lambda n: (0, 0, 0))] + comm.specs,
        out_shape=[jax.ShapeDtypeStruct(dproj.shape, dproj.dtype), jax.ShapeDtypeStruct((B, S, LANE), bf16),
                   jax.ShapeDtypeStruct((B, S, LANE), bf16), jax.ShapeDtypeStruct((HQ, WIN, 2 * WIN), f32),
                   jax.ShapeDtypeStruct((HQ, 1, 1), f32)] + comm.out_shape,
        scratch_shapes=[pltpu.VMEM((B, WIN, LANE), f32), pltpu.VMEM((B, WIN, LANE), f32)] + comm.scratch,
        compiler_params=_cp(("arbitrary",)),
    )(proj, proj, proj, proj, proj, bias, sinks, dy, dproj, *comm.arrs)


DN_ROWS, FFN_ROWS = 256, 32


def _stage_rows(dst, value):
    dst[0:8] = jnp.zeros((8, LANE), f32)
    dst[8:8 + value.shape[0]] = value


def _conv_rows(xs, w, width, r, rows):
    wins = [xs[pl.ds(r + 8 - (width - 1) + j, rows), :] for j in range(width)]
    out = w[0:1] * wins[0]
    for j in range(1, width):
        out = out + w[j:j + 1] * wins[j]
    return out, wins


def _fold8(v):
    return jnp.sum(v.reshape(v.shape[0] // 8, 8, LANE), axis=0)


def _conv_rows_t(ds, w, width, r, rows):
    out = w[0:1] * ds[pl.ds(r + width - 1, rows), :]
    for j in range(1, width):
        out = out + w[j:j + 1] * ds[pl.ds(r + width - 1 - j, rows), :]
    return out


def _dn_outblk(i):
    return (i % DNH) * 3 + i // DNH


def _dn_act(c, isqk):
    sg = jax.nn.sigmoid(c)
    y = c * sg
    n = lax.rsqrt(jnp.sum(y * y, axis=-1, keepdims=True) + L2_EPS)
    return jnp.where(isqk, y * n, y), sg, n


def dnconv_fwd(proj, conv_w):
    B, S, _ = proj.shape
    rows = min(DN_ROWS, S)

    def body(x_ref, w_ref, o_ref, xs):
        isqk = pl.program_id(0) < 2 * DNH
        _stage_rows(xs, x_ref[0].astype(f32))
        w = w_ref[...]
        for r in range(0, S, rows):
            c, _ = _conv_rows(xs, w, DNK, r, rows)
            o_ref[0, pl.ds(r, rows), :] = _dn_act(c, isqk)[0]

    return pl.pallas_call(
        body, name="dnconv_fwd", grid=(3 * DNH, B),
        in_specs=[pl.BlockSpec((1, S, LANE), lambda i, b: (b, 0, CB_DQKV + i)), pl.BlockSpec((DNK, LANE), lambda i, b: (0, i))],
        out_specs=pl.BlockSpec((1, S, LANE), lambda i, b: (b, 0, _dn_outblk(i))),
        out_shape=jax.ShapeDtypeStruct((B, S, 3 * DNH * DND), f32), scratch_shapes=[pltpu.VMEM((S + 8, LANE), f32)],
        compiler_params=_cp(("parallel", "parallel")),
    )(proj, conv_w)


def dnconv_bwd(proj, conv_w, dqkvn, dproj):
    B, S, _ = proj.shape
    rows = min(DN_ROWS, S)

    def body(x_ref, w_ref, dy_ref, _, dx_ref, dw_ref, xs, ds):
        isqk = pl.program_id(0) < 2 * DNH
        _stage_rows(xs, x_ref[0].astype(f32))
        w = w_ref[...]
        dw = [jnp.zeros((8, LANE), f32) for _ in range(DNK)]
        for r in range(0, S, rows):
            c, wins = _conv_rows(xs, w, DNK, r, rows)
            out, sg, n = _dn_act(c, isqk)
            dout = dy_ref[0, pl.ds(r, rows), :]
            dy = jnp.where(isqk, n * (dout - out * jnp.sum(dout * out, axis=-1, keepdims=True)), dout)
            dc = dy * (sg * (1.0 + c * (1.0 - sg)))
            ds[pl.ds(r, rows), :] = dc
            for j in range(DNK):
                dw[j] = dw[j] + _fold8(dc * wins[j])
        ds[S:S + 8] = jnp.zeros((8, LANE), f32)
        for r in range(0, S, rows):
            dx_ref[0, pl.ds(r, rows), :] = _conv_rows_t(ds, w, DNK, r, rows).astype(dx_ref.dtype)

        @pl.when(pl.program_id(1) == 0)
        def _():
            dw_ref[...] = jnp.zeros(dw_ref.shape, f32)
        dw_ref[...] += jnp.concatenate([jnp.sum(d, axis=0, keepdims=True) for d in dw], axis=0)

    return pl.pallas_call(
        body, name="dnconv_bwd", grid=(3 * DNH, B),
        in_specs=[pl.BlockSpec((1, S, LANE), lambda i, b: (b, 0, CB_DQKV + i)), pl.BlockSpec((DNK, LANE), lambda i, b: (0, i)),
                  pl.BlockSpec((1, S, LANE), lambda i, b: (b, 0, _dn_outblk(i))), pl.BlockSpec(memory_space=pl.ANY)],
        out_specs=[pl.BlockSpec((1, S, LANE), lambda i, b: (b, 0, CB_DQKV + i)), pl.BlockSpec((DNK, LANE), lambda i, b: (0, i))],
        out_shape=[jax.ShapeDtypeStruct(dproj.shape, dproj.dtype), jax.ShapeDtypeStruct((DNK, 3 * DNH * DND), f32)],
        scratch_shapes=[pltpu.VMEM((S + 8, LANE), f32), pltpu.VMEM((S + 8, LANE), f32)],
        input_output_aliases={3: 0}, compiler_params=_cp(("parallel", "arbitrary")),
    )(proj, conv_w, dqkvn, dproj)


def _bdot(a, b, ca, cb, precision=HI):
    return lax.dot_general(a, b, (((ca,), (cb,)), ((0,), (0,))), preferred_element_type=f32, precision=precision)


def _bdot_bf16(a, b, ca, cb):
    return _bdot(a.astype(bf16), b.astype(bf16), ca, cb, None)


@functools.partial(jax.custom_vjp, nondiff_argnums=(2, 3))
def _bdot_bf16_vjp(a, b, ca, cb):
    return _bdot_bf16(a, b, ca, cb)


def _bdot_bf16_fwd(a, b, ca, cb):
    return _bdot_bf16(a, b, ca, cb), (a, b)


def _bdot_bf16_bwd(ca, cb, res, g):
    a, b = res
    fa, fb = 3 - ca, 3 - cb
    da = _bdot_bf16(g, b, 2, fb) if ca == 2 else _bdot_bf16(b, g, fb, 2)
    db = _bdot_bf16(a, g, fa, 1) if cb == 1 else _bdot_bf16(g, a, 1, fa)
    return da, db


_bdot_bf16_vjp.defvjp(_bdot_bf16_fwd, _bdot_bf16_bwd)


def _neumann_inverse(low):
    n = low.shape[-1]
    eye = (lax.broadcasted_iota(jnp.int32, (n, n), 0) == lax.broadcasted_iota(jnp.int32, (n, n), 1)).astype(f32)
    p = -low
    x = eye[None] + p
    for _ in range(5):
        p = _bdot_bf16(p, p, 2, 1)
        x = x + _bdot_bf16(x, p, 2, 1)
    return x


@jax.custom_vjp
def _unit_lower_inverse(low):
    return _neumann_inverse(low)


def _uli_fwd(low):
    t = _neumann_inverse(low)
    return t, t


def _uli_bwd(t, dt):
    return (-_bdot_bf16(_bdot_bf16(t, dt, 1, 1), t, 2, 2),)


_unit_lower_inverse.defvjp(_uli_fwd, _uli_bwd)


def _stack(xs):
    return jnp.concatenate([x[None] for x in xs], axis=0)


DELTA_CHUNKS = 2


def _delta_chunks(qkv, bg, state, differentiated):
    inverse = _unit_lower_inverse if differentiated else _neumann_inverse
    lo = _bdot_bf16_vjp if differentiated else _bdot_bf16
    B, n = qkv.shape[0], qkv.shape[1] // CH
    G = B * DNH
    N = n * G
    triples = [(i, b, h) for i in range(n) for b in range(B) for h in range(DNH)]
    col = lambda i, b, h, kind: qkv[b, i * CH:(i + 1) * CH, (3 * h + kind) * DND:(3 * h + kind + 1) * DND]
    q, k, v = [_stack([col(i, b, h, kind) for i, b, h in triples]) for kind in range(3)]
    lane = lax.broadcasted_iota(jnp.int32, (CH, LANE), 1)
    pick = lambda i, b, l: jnp.sum(jnp.where(lane == l, bg[b, i * CH:(i + 1) * CH], 0.0), axis=1, keepdims=True)
    beta = _stack([pick(i, b, h) for i, b, h in triples])
    g = _stack([pick(i, b, h + DNH) for i, b, h in triples])
    ri = lax.broadcasted_iota(jnp.int32, (CH, CH), 0)
    ci = lax.broadcasted_iota(jnp.int32, (CH, CH), 1)
    incl, strict = (ri >= ci)[None], (ri > ci)[None]
    gc = _bdot(jnp.broadcast_to(incl.astype(f32), (N, CH, CH)), jnp.broadcast_to(g, (N, CH, LANE)), 2, 1, MID)
    e0 = jnp.broadcast_to((lane == 0).astype(f32)[None], (N, CH, LANE))
    gc_row = _bdot(e0, gc, 2, 2, MID)
    diff = gc[:, :, :CH] - gc_row
    decay = jnp.where(incl, jnp.exp(jnp.where(incl, diff, 0.0)), 0.0)
    qs = q * (DND ** -0.5)
    kb, vb = k * beta, v * beta
    eg = jnp.exp(gc)
    with_k = lo(jnp.concatenate([kb, qs], axis=1), k, 2, 2)
    low = jnp.where(strict, with_k[:, :CH] * decay, 0.0)
    intra = jnp.where(incl, with_k[:, CH:] * decay, 0.0)
    tinv = inverse(low)
    solved = lo(tinv, jnp.concatenate([vb, kb * eg], axis=2), 2, 1)
    gl = gc[:, CH - 1:CH, :]
    k_tail = k * jnp.exp(gl - gc)
    to_state = jnp.concatenate([solved[:, :, DND:], qs * eg], axis=1)
    decay_all = jnp.exp(gl)
    outs = []
    for i in range(n):
        sl = slice(i * G, (i + 1) * G)
        with_state = lo(to_state[sl], state, 2, 1)
        v_new = solved[sl, :, :DND] - with_state[:, :CH]
        outs.append(with_state[:, CH:] + lo(intra[sl], v_new, 2, 1))
        state = state * decay_all[sl] + lo(k_tail[sl], v_new, 1, 1)
    return outs, state


def delta_fwd(qkvn, bg, comm):
    B, S, _ = qkvn.shape
    n = DELTA_CHUNKS if (S // CH) % DELTA_CHUNKS == 0 else 1
    steps, G, rows = S // (n * CH), B * DNH, n * CH

    def body(qkv_ref, bg_ref, o_ref, st_ref, state):
        @pl.when(pl.program_id(0) == 0)
        def _():
            state[...] = jnp.zeros(state.shape, f32)
        s0 = state[...]
        st_ref[0] = s0
        outs, s1 = _delta_chunks(qkv_ref[...], bg_ref[...], s0, False)
        for i, o in enumerate(outs):
            for b in range(B):
                for h in range(DNH):
                    o_ref[b, i * CH:(i + 1) * CH, h * DND:(h + 1) * DND] = o[b * DNH + h]
        state[...] = s1

    at = lambda c: lambda: pl.program_id(0) == c
    return pl.pallas_call(
        _ride(body, 2, 2, 1, comm, at(0), at((7 * steps) // 8), at(steps - 1)), name="delta_fwd", grid=(steps,),
        in_specs=[pl.BlockSpec((B, rows, 3 * DNH * DND), lambda c: (0, c, 0)), pl.BlockSpec((B, rows, LANE), lambda c: (0, c, 0))] + comm.specs,
        out_specs=[pl.BlockSpec((B, rows, DNH * DND), lambda c: (0, c, 0)), pl.BlockSpec((1, G, DND, DND), lambda c: (c, 0, 0, 0))] + comm.specs,
        out_shape=[jax.ShapeDtypeStruct((B, S, DNH * DND), f32), jax.ShapeDtypeStruct((steps, G, DND, DND), f32)] + comm.out_shape,
        scratch_shapes=[pltpu.VMEM((G, DND, DND), f32)] + comm.scratch, compiler_params=_cp(("arbitrary",)),
    )(qkvn, bg, *comm.arrs)


def delta_bwd(qkvn, bg, states, do, comm):
    B, S, _ = qkvn.shape
    steps, G = states.shape[0], B * DNH
    rows = S // steps
    n = rows // CH

    def body(qkv_ref, bg_ref, st_ref, do_ref, dqkv_ref, dbg_ref, dstate):
        @pl.when(pl.program_id(0) == 0)
        def _():
            dstate[...] = jnp.zeros(dstate.shape, f32)
        _, vjp = jax.vjp(lambda a, g, s: _delta_chunks(a, g, s, True), qkv_ref[...], bg_ref[...], st_ref[0])
        do = [_stack([do_ref[b, i * CH:(i + 1) * CH, h * DND:(h + 1) * DND] for b in range(B) for h in range(DNH)]) for i in range(n)]
        dqkv, dbg, ds = vjp((do, dstate[...]))
        dqkv_ref[...] = dqkv
        dbg_ref[...] = dbg
        dstate[...] = ds

    rev = lambda c: steps - 1 - c
    at = lambda c: lambda: pl.program_id(0) == c
    return pl.pallas_call(
        _ride(body, 4, 2, 1, comm, at(0), at(steps - 1), at(steps - 1)), name="delta_bwd", grid=(steps,),
        in_specs=[pl.BlockSpec((B, rows, 3 * DNH * DND), lambda c: (0, rev(c), 0)), pl.BlockSpec((B, rows, LANE), lambda c: (0, rev(c), 0)),
                  pl.BlockSpec((1, G, DND, DND), lambda c: (rev(c), 0, 0, 0)),
                  pl.BlockSpec((B, rows, DNH * DND), lambda c: (0, rev(c), 0))] + comm.specs,
        out_specs=[pl.BlockSpec((B, rows, 3 * DNH * DND), lambda c: (0, rev(c), 0)),
                   pl.BlockSpec((B, rows, LANE), lambda c: (0, rev(c), 0))] + comm.specs,
        out_shape=[jax.ShapeDtypeStruct((B, S, 3 * DNH * DND), f32), jax.ShapeDtypeStruct((B, S, LANE), f32)] + comm.out_shape,
        scratch_shapes=[pltpu.VMEM((G, DND, DND), f32)] + comm.scratch, compiler_params=_cp(("arbitrary",)),
    )(qkvn, bg, states, do, *comm.arrs)


GELU_C0, GELU_C1 = math.sqrt(2.0 / math.pi), 0.044715


def _ffn_specs(S):
    nblk = DFF // LANE
    return [pl.BlockSpec((1, S, LANE), lambda i, b: (b, 0, i)), pl.BlockSpec((1, S, LANE), lambda i, b: (b, 0, nblk + i)),
            pl.BlockSpec((FK, LANE), lambda i, b: (0, i)), pl.BlockSpec((FK, LANE), lambda i, b: (0, nblk + i))]


def ffnconv_fwd(up, conv_w):
    B, S, _ = up.shape
    rows = min(FFN_ROWS, S)

    def body(g_ref, v_ref, gw_ref, vw_ref, o_ref, xg, xv):
        _stage_rows(xg, g_ref[0].astype(f32))
        _stage_rows(xv, v_ref[0].astype(f32))
        gw, vw = gw_ref[...], vw_ref[...]
        for r in range(0, S, rows):
            g, _ = _conv_rows(xg, gw, FK, r, rows)
            v, _ = _conv_rows(xv, vw, FK, r, rows)
            t = jnp.tanh(GELU_C0 * (g * (1.0 + GELU_C1 * (g * g))))
            o_ref[0, pl.ds(r, rows), :] = (0.5 * g * (1.0 + t) * v).astype(o_ref.dtype)

    return pl.pallas_call(
        body, name="ffnconv_fwd", grid=(DFF // LANE, B), in_specs=_ffn_specs(S),
        out_specs=pl.BlockSpec((1, S, LANE), lambda i, b: (b, 0, i)), out_shape=jax.ShapeDtypeStruct((B, S, DFF), bf16),
        scratch_shapes=[pltpu.VMEM((S + 8, LANE), f32)] * 2, compiler_params=_cp(("parallel", "parallel")),
    )(up, up, conv_w, conv_w)


def ffnconv_bwd(up, conv_w, dact, comm):
    B, S, _ = up.shape
    rows = min(FFN_ROWS, S)

    def body(g_ref, v_ref, gw_ref, vw_ref, dy_ref, dx_ref, dw_ref, xg, xv, dg, dv):
        _stage_rows(xg, g_ref[0].astype(f32))
        _stage_rows(xv, v_ref[0].astype(f32))
        gw, vw = gw_ref[...], vw_ref[...]
        dgw = [jnp.zeros((8, LANE), f32) for _ in range(FK)]
        dvw = [jnp.zeros((8, LANE), f32) for _ in range(FK)]
        for r in range(0, S, rows):
            g, gwins = _conv_rows(xg, gw, FK, r, rows)
            v, vwins = _conv_rows(xv, vw, FK, r, rows)
            g2 = g * g
            t = jnp.tanh(GELU_C0 * (g * (1.0 + GELU_C1 * g2)))
            half = 0.5 * (1.0 + t)
            dgelu = half + (0.5 * GELU_C0) * g * (1.0 - t * t) * (1.0 + (3.0 * GELU_C1) * g2)
            dy = dy_ref[0, pl.ds(r, rows), :].astype(f32)
            dvc = dy * (g * half)
            dgc = dy * v * dgelu
            dg[pl.ds(r, rows), :] = dgc
            dv[pl.ds(r, rows), :] = dvc
            for j in range(FK):
                dgw[j] = dgw[j] + _fold8(dgc * gwins[j])
                dvw[j] = dvw[j] + _fold8(dvc * vwins[j])
        dg[S:S + 8] = jnp.zeros((8, LANE), f32)
        dv[S:S + 8] = jnp.zeros((8, LANE), f32)
        for r in range(0, S, rows):
            dx_ref[0, 0, pl.ds(r, rows), :] = _conv_rows_t(dg, gw, FK, r, rows).astype(dx_ref.dtype)
            dx_ref[1, 0, pl.ds(r, rows), :] = _conv_rows_t(dv, vw, FK, r, rows).astype(dx_ref.dtype)

        @pl.when(pl.program_id(1) == 0)
        def _():
            dw_ref[...] = jnp.zeros(dw_ref.shape, f32)
        dw_ref[0] += jnp.concatenate([jnp.sum(d, axis=0, keepdims=True) for d in dgw], axis=0)
        dw_ref[1] += jnp.concatenate([jnp.sum(d, axis=0, keepdims=True) for d in dvw], axis=0)

    nblk = DFF // LANE
    at = lambda i, b: lambda: (pl.program_id(0) == i) & (pl.program_id(1) == b)
    return pl.pallas_call(
        _ride(body, 5, 2, 4, comm, at(0, 0), at(nblk - 1, B - 1), at(nblk - 1, B - 1)), name="ffnconv_bwd", grid=(nblk, B),
        in_specs=_ffn_specs(S) + [pl.BlockSpec((1, S, LANE), lambda i, b: (b, 0, i))] + comm.specs,
        out_specs=[pl.BlockSpec((2, 1, S, LANE), lambda i, b: (0, b, 0, i)),
                   pl.BlockSpec((2, FK, LANE), lambda i, b: (0, 0, i))] + comm.specs,
        out_shape=[jax.ShapeDtypeStruct((2, B, S, DFF), bf16), jax.ShapeDtypeStruct((2, FK, DFF), f32)] + comm.out_shape,
        scratch_shapes=[pltpu.VMEM((S + 8, LANE), f32)] * 4 + comm.scratch, compiler_params=_cp(("arbitrary", "arbitrary")),
    )(up, up, conv_w, conv_w, dact, *comm.arrs)


def ada_fwd(c_all, ada_w, ada_b):
    def body(c_ref, w_ref, b_ref, o_ref):
        c = c_ref[...]
        act = (c * jax.nn.sigmoid(c)).astype(bf16)
        o_ref[...] = jnp.dot(act, w_ref[...].astype(bf16), preferred_element_type=f32) + b_ref[...]

    return pl.pallas_call(body, name="ada_fwd", out_shape=jax.ShapeDtypeStruct((c_all.shape[0], ada_w.shape[1]), f32),
                          compiler_params=pltpu.CompilerParams(vmem_limit_bytes=VMEM_LIMIT))(c_all, ada_w, ada_b)


def ada_bwd(c_all, dmod):
    def body(c_ref, d_ref, o_ref):
        c = c_ref[...]
        act = (c * jax.nn.sigmoid(c)).astype(bf16)
        o_ref[...] = lax.dot_general(act, d_ref[...].astype(bf16), (((0,), (0,)), ((), ())), preferred_element_type=f32)

    return pl.pallas_call(body, name="ada_bwd", out_shape=jax.ShapeDtypeStruct((c_all.shape[1], dmod.shape[1]), f32),
                          compiler_params=pltpu.CompilerParams(vmem_limit_bytes=VMEM_LIMIT))(c_all, dmod)


def loss_head(h1, y2, target, g2, w):
    def fn(t, b, c):
        h, y, tg = [v.astype(f32) for v in t]

        def loss_fn(h, y, g, w):
            e = h + g * _rms(y, w) - tg
            return 0.5 * jnp.sum(jnp.mean(e * e, axis=-1))

        loss, grads = jax.value_and_grad(loss_fn, argnums=(0, 1, 2, 3))(h, y, b[0], c[0])
        return [grads[0], grads[1]], [grads[2], grads[3], jnp.full((1, LANE), loss, f32)]

    return rowcall("loss_head", fn, [(h1, D, 0), (y2, D, 0), (target, D, 0)], [g2], [w], [(D, f32), (D, bf16)],
                   [(1, D), (1, D), (1, LANE)])


def adamw(w, gparts, m, v, name):
    R, C = w.shape
    P = gparts.shape[0]
    budget = 2 * 1024 * 1024
    tr, tc = R, C
    if R * C * 4 > budget and R % 8 == 0:
        tr = max(t for t in range(8, R + 1, 8) if R % t == 0 and t * C * 4 <= budget)
    elif R * C * 4 > budget:
        tc = max(t for t in range(LANE, C + 1, LANE) if C % t == 0 and R * t * 4 <= budget)

    def body(w_ref, g_ref, m_ref, v_ref, go, do, mo, vo):
        g = g_ref[0].astype(f32)
        for p in range(1, P):
            g = g + g_ref[p].astype(f32)
        m2 = B1 * m_ref[...] + (1.0 - B1) * g
        v2 = B2 * v_ref[...] + (1.0 - B2) * jnp.square(g)
        m_hat = m2 * (1.0 / (1.0 - B1 ** STEP))
        v_hat = v2 * (1.0 / (1.0 - B2 ** STEP))
        go[...] = g
        do[...] = -LR * (m_hat / (jnp.sqrt(v_hat) + EPS) + WD * w_ref[...])
        mo[...] = m2
        vo[...] = v2

    blk = pl.BlockSpec((tr, tc), lambda i, j: (i, j))
    return pl.pallas_call(
        body, name=name, grid=(R // tr, C // tc), in_specs=[blk, pl.BlockSpec((P, tr, tc), lambda i, j: (0, i, j)), blk, blk],
        out_specs=[blk] * 4, out_shape=[jax.ShapeDtypeStruct((R, C), f32)] * 4, compiler_params=_cp(("parallel", "parallel")),
    )(w, gparts, m, v)


def _pack_w_in(wt):
    aq, ak, av, dqkv, dz, dbeta, da, ga, gd = jnp.split(wt, np.cumsum(IN_SPLITS)[:-1].tolist(), axis=0)
    ba = jnp.pad(jnp.concatenate([dbeta, da], axis=0), ((0, LANE - 2 * DNH), (0, 0)))
    return jnp.concatenate([ga, gd, aq, dqkv, dz, ak, av, ba], axis=0)


def _unpack_w_in(p):
    row = lambda cb, n: p[cb * LANE: cb * LANE + n]
    ba = row(CB_BA, 2 * DNH)
    return jnp.concatenate([row(CB_AQ, HQ * HD), row(CB_AK, HKV * HD), row(CB_AV, HKV * HD), row(CB_DQKV, 3 * DNH * DND),
                            row(CB_DZ, DNH * DND), ba[:DNH], ba[DNH:], row(CB_GA, D), row(CB_GD, D)], axis=0)


def _cols_gathered(g):
    return g.transpose(1, 0, 2).reshape(g.shape[1], NDEV * g.shape[2])


def _cols_split(w):
    r = w.shape[0]
    return w.reshape(r, NDEV, w.shape[1] // NDEV).transpose(1, 0, 2)


def kernel(x, c, ada_w, ada_b, norm_mix_pre, norm_mix_post, norm_ffn_pre, norm_ffn_post, w_in, dn_conv_w, dn_a_log, dn_dt_bias, dn_norm_w, attn_sinks, rel_bias, w_attn_branch, w_dn_branch, w_out, ffn_w_up, ffn_conv_w, ffn_w_down, loss_target, m_ada_w, m_ada_b, m_norm_mix_pre, m_norm_mix_post, m_norm_ffn_pre, m_norm_ffn_post, m_w_in, m_dn_conv_w, m_dn_a_log, m_dn_dt_bias, m_dn_norm_w, m_attn_sinks, m_rel_bias, m_w_attn_branch, m_w_dn_branch, m_w_out, m_ffn_w_up, m_ffn_conv_w, m_ffn_w_down, v_ada_w, v_ada_b, v_norm_mix_pre, v_norm_mix_post, v_norm_ffn_pre, v_norm_ffn_post, v_w_in, v_dn_conv_w, v_dn_a_log, v_dn_dt_bias, v_dn_norm_w, v_attn_sinks, v_rel_bias, v_w_attn_branch, v_w_dn_branch, v_w_out, v_ffn_w_up, v_ffn_conv_w, v_ffn_w_down):
    B, S, _ = x.shape
    T = B * S
    me = 4 * lax.axis_index("x") + 2 * lax.axis_index("y") + lax.axis_index("c")
    big = dict(w_in=w_in, dn_conv_w=dn_conv_w, w_attn_branch=w_attn_branch, w_dn_branch=w_dn_branch, w_out=w_out,
               ffn_w_up=ffn_w_up, ffn_conv_w=ffn_conv_w, ffn_w_down=ffn_w_down)
    big_names = list(big)

    first, mid, late = ["w_in", "dn_conv_w"], ["w_attn_branch", "w_dn_branch", "w_out"], ["ffn_w_up", "ffn_conv_w"]
    transposed = ("w_in", "ffn_w_up")
    local = lambda n, a: a[0].T if n in transposed else a[0]
    shard = lambda names: [local(n, big[n]).astype(bf16) for n in names]
    *got, c_all = _exchange(shard(first) + [c], "gather_w_in", two_level=True)
    gw = dict(zip(first, got))
    c_all = c_all.reshape(NDEV * B, D)

    wp = _pack_w_in(gw["w_in"].reshape(IN_DIM, D))
    conv_dn = _cols_gathered(gw["dn_conv_w"]).astype(f32)

    ncol = ada_w.shape[2]
    ada_b_mine = lax.dynamic_slice_in_dim(ada_b, me * ncol, ncol, axis=1)
    mod_cols = ada_fwd(c_all, ada_w[0], ada_b_mine)
    (mod_g,) = _exchange([mod_cols], "gather_mod")
    mod = lax.dynamic_slice_in_dim(mod_g, me * B, B, axis=1).transpose(1, 0, 2).reshape(B, NMOD * D)
    sh1, sc1, g1, sh2, sc2, g2 = [mod[:, i * D:(i + 1) * D].reshape(B, 1, D) for i in range(NMOD)]

    onehot = (jnp.asarray(_bucket_table()).reshape(1, -1) == jnp.arange(NBUCK, dtype=jnp.int32)[:, None]).astype(f32)
    bias = mm(rel_bias.T, onehot, "nn", f32, "bias_table", tn=8192, precision=HI).reshape(HQ, WIN, 2 * WIN)
    sinks = attn_sinks.reshape(HQ, 1, 1)
    a_log_pad = jnp.pad(dn_a_log, ((0, 0), (DNH, LANE - 2 * DNH)))
    dt_bias_pad = jnp.pad(dn_dt_bias, ((0, 0), (DNH, LANE - 2 * DNH)))

    (u1,) = rowcall_fwd("mix_pre", f_rms_mod, [(x, D, 0)], [sc1, sh1], [norm_mix_pre], [(D, bf16)])
    proj, gw["ffn_w_down"] = mm(u1.reshape(T, D), wp, "nt", bf16, "proj", tm=512, tn=CB_BA * LANE, b_cols=(0, 1),
                                comm=_Comm(shard(["ffn_w_down"]), two_level=True))
    proj = proj.reshape(B, S, CB_BA * LANE)
    ba = mm(u1.reshape(T, D), wp, "nt", f32, "proj_ba", tn=LANE, b_cols=(CB_BA, 1)).reshape(B, S, LANE)
    ya, *got = attn_fwd(proj, bias, sinks, _Comm(shard(mid), two_level=True))
    gw.update(zip(mid, got))
    wa = _cols_gathered(gw["w_attn_branch"])
    wd = _cols_gathered(gw["w_dn_branch"])
    wo = gw["w_out"].reshape(D, D)
    qkvn = dnconv_fwd(proj, conv_dn)
    (bg,) = rowcall_fwd("dn_gate", f_gate, [(ba, LANE, 0)], [], [a_log_pad, dt_bias_pad], [(LANE, f32)])
    o_dn, states, *got = delta_fwd(qkvn, bg, _Comm(shard(late), two_level=True))
    gw.update(zip(late, got))
    wup = gw["ffn_w_up"].reshape(2 * DFF, D)
    conv_ffn = _cols_gathered(gw["ffn_conv_w"]).astype(f32)
    wdown = gw["ffn_w_down"].reshape(DFF, D)
    (yd,) = rowcall_fwd("dn_out", f_dnout, [(o_dn, DNH * DND, 0), (proj, DNH * DND, CB_DZ // 4)], [], [dn_norm_w], [(DNH * DND, bf16)])
    pa = mm(ya.reshape(T, HQ * HD), wa, "nn", bf16, "attn_branch").reshape(B, S, D)
    pd = mm(yd.reshape(T, DNH * DND), wd, "nn", bf16, "dn_branch").reshape(B, S, D)
    merge_tok = [(proj, D, CB_GA // 8), (proj, D, CB_GD // 8), (pa, D, 0), (pd, D, 0)]
    (merged,) = rowcall_fwd("merge", f_merge, merge_tok, [], [], [(D, bf16)])
    y1 = mm(merged.reshape(T, D), wo, "nn", bf16, "mix_out").reshape(B, S, D)
    post_pre = ([(x, D, 0), (y1, D, 0)], [g1, sc2, sh2], [norm_mix_post, norm_ffn_pre])
    h1, u2 = rowcall_fwd("mix_post_ffn_pre", f_post_pre, *post_pre, [(D, f32), (D, bf16)])
    up = mm(u2.reshape(T, D), wup, "nt", bf16, "ffn_up", tn=2816).reshape(B, S, 2 * DFF)
    act = ffnconv_fwd(up, conv_ffn)
    y2 = mm(act.reshape(T, DFF), wdown, "nn", bf16, "ffn_down", tk=2816).reshape(B, S, D)

    dh1_a, dy2, dg2, dw_ffn_post, loss_b = loss_head(h1, y2, loss_target, g2, norm_ffn_post)
    dy2f = dy2.reshape(T, D)
    dact = mm(dy2f, wdown, "nt", bf16, "ffn_down_dx", tn=2816).reshape(B, S, DFF)
    g_wdown = mm(act.reshape(T, DFF), dy2f, "tn", bf16, "ffn_down_dw", tm=2816, tn=512, tk=4096)
    in_flight = []

    def send_off(d, tag):
        in_flight.append((d, _scatter_start([a.astype(bf16) for a in d.values()], "scatter_" + tag + "_start")))
        return in_flight[-1][1][-1][0, 0]

    started = send_off(dict(ffn_w_down=g_wdown.reshape(NDEV, DFF // NDEV, D)), "ffn_down")
    dup, g_conv_ffn = ffnconv_bwd(up, conv_ffn + started, dact, _NoComm())
    dupf = dup.reshape(2, T, DFF)
    g_conv_ffn = g_conv_ffn.transpose(1, 0, 2).reshape(FK, 2 * DFF)
    du2 = mm(dupf, wup, "nn", bf16, "ffn_up_dx", tk=2816).reshape(B, S, D)
    g_wup = mm(dupf, u2.reshape(T, D), "tn", bf16, "ffn_up_dw", tm=1408, tk=2048)
    started = send_off(dict(ffn_w_up=g_wup.reshape(NDEV, 2 * DFF // NDEV, D), ffn_conv_w=_cols_split(g_conv_ffn)), "ffn_up")
    post_pre = (post_pre[0], [g1 + started, sc2, sh2], post_pre[2])
    dh1, dy1, dg1, dsc2, dsh2, dw_mix_post, dw_ffn_pre = rowcall_bwd(
        "mix_post_ffn_pre_bwd", f_post_pre, *post_pre, [(dh1_a, D, 0), (du2, D, 0)], [(0, f32), (1, bf16)])
    dy1f = dy1.reshape(T, D)
    dmerged = mm(dy1f, wo, "nt", bf16, "mix_out_dx").reshape(B, S, D)
    g_wo = mm(merged.reshape(T, D), dy1f, "tn", bf16, "mix_out_dw", tk=2048)
    dproj = lax.empty((B, S, NP), bf16)
    dproj, dpa, dpd = rowcall_bwd("merge_bwd", f_merge, merge_tok, [], [], [(dmerged, D, 0)],
                                  [(0, bf16), (1, bf16), (2, bf16), (3, bf16)], join_first=2, into=(dproj, CB_GA // 16))
    dpaf, dpdf = dpa.reshape(T, D), dpd.reshape(T, D)
    dya = mm(dpaf, wa, "nt", bf16, "attn_branch_dx").reshape(B, S, HQ * HD)
    g_wa = mm(ya.reshape(T, HQ * HD), dpaf, "tn", bf16, "attn_branch_dw", tk=2048)
    dyd = mm(dpdf, wd, "nt", bf16, "dn_branch_dx").reshape(B, S, DNH * DND)
    g_wd = mm(yd.reshape(T, DNH * DND), dpdf, "tn", bf16, "dn_branch_dw", tk=2048)
    dproj, do_dn, dw_dn_norm = rowcall_bwd("dn_out_bwd", f_dnout, [(o_dn, DNH * DND, 0), (proj, DNH * DND, CB_DZ // 4)], [], [dn_norm_w],
                                           [(dyd, DNH * DND, 0)], [(1, bf16), (0, f32)], into=(dproj, CB_DZ // 4))
    started = send_off(dict(w_attn_branch=_cols_split(g_wa), w_dn_branch=_cols_split(g_wd), w_out=g_wo.reshape(NDEV, D // NDEV, D)), "branches")
    dqkvn, dbg = delta_bwd(qkvn, bg + started, states, do_dn, _NoComm())
    dproj, da_log_pad, ddt_bias_pad = rowcall_bwd("dn_gate_bwd", f_gate, [(ba, LANE, 0)], [], [a_log_pad, dt_bias_pad],
                                                  [(dbg, LANE, 0)], [(0, bf16)], into=(dproj, CB_BA))
    dproj, g_conv_dn = dnconv_bwd(proj, conv_dn, dqkvn, dproj)
    dproj, dk, dv, dbias, dsinks = attn_bwd(proj, bias, sinks, dya, dproj, _NoComm())
    dproj = lax.dynamic_update_slice(dproj, jnp.concatenate([dk, dv], axis=2), (0, 0, CB_AK * LANE)).reshape(T, NP)
    g_wp = mm(dproj, u1.reshape(T, D), "tn", bf16, "proj_dw", tm=1664, tk=1024)
    started = send_off(dict(w_in=_unpack_w_in(g_wp).reshape(NDEV, IN_DIM // NDEV, D), dn_conv_w=_cols_split(g_conv_dn)), "w_in")
    du1 = mm(dproj, wp, "nn", bf16, "proj_dx", tm=512, tk=NP).reshape(B, S, D)
    grad_x, dsc1, dsh1, dw_mix_pre = rowcall_bwd("mix_pre_bwd", f_rms_mod, [(x, D, 0)], [sc1 + started, sh1], [norm_mix_pre],
                                                 [(du1, D, 0)], [(0, f32)], add=(dh1, D, 0))
    g_rel = mm(dbias.reshape(HQ, WIN * 2 * WIN), onehot, "nt", f32, "rel_bias_dw", tk=8192, precision=HI)

    dmod = jnp.concatenate([dsh1, dsc1, dg1, dsh2, dsc2, dg2], axis=2).reshape(B, NMOD * D)

    zrow = lambda a: jnp.concatenate([a.reshape(1, -1), jnp.zeros((B - 1, a.size), f32)], axis=0)
    small_g = jnp.concatenate([
        dmod, dw_mix_pre.reshape(B, D), dw_mix_post.reshape(B, D), dw_ffn_pre.reshape(B, D), dw_ffn_post.reshape(B, D),
        da_log_pad.reshape(B, LANE)[:, DNH:2 * DNH], ddt_bias_pad.reshape(B, LANE)[:, DNH:2 * DNH], dw_dn_norm.reshape(B, DND),
        zrow(dsinks), zrow(g_rel.T), loss_b.reshape(B, LANE)[:, :1], jnp.zeros((B, SMALL_PAD - SMALL_N - 1), f32)], axis=1)
    (small_all,) = _exchange([small_g], "gather_small")
    dmod_cols = lax.dynamic_slice_in_dim(small_all.reshape(NDEV * B, SMALL_PAD), me * ncol, ncol, axis=1)
    g_ada_w = ada_bwd(c_all, dmod_cols)
    parts = {}
    for i, (d, started) in enumerate(in_flight):
        parts.update(zip(d, _scatter_finish(started, len(d), g_ada_w, "scatter_finish_%d" % i)))
    small_w = dict(ada_b=(ada_b, m_ada_b, v_ada_b), norm_mix_pre=(norm_mix_pre, m_norm_mix_pre, v_norm_mix_pre),
                   norm_mix_post=(norm_mix_post, m_norm_mix_post, v_norm_mix_post), norm_ffn_pre=(norm_ffn_pre, m_norm_ffn_pre, v_norm_ffn_pre),
                   norm_ffn_post=(norm_ffn_post, m_norm_ffn_post, v_norm_ffn_post), dn_a_log=(dn_a_log, m_dn_a_log, v_dn_a_log),
                   dn_dt_bias=(dn_dt_bias, m_dn_dt_bias, v_dn_dt_bias), dn_norm_w=(dn_norm_w, m_dn_norm_w, v_dn_norm_w),
                   attn_sinks=(attn_sinks, m_attn_sinks, v_attn_sinks), rel_bias=(rel_bias, m_rel_bias, v_rel_bias))

    def pack(i, fill):
        row = jnp.concatenate([small_w[n][i].reshape(1, -1) for n, _ in SMALL], axis=1)
        return jnp.pad(row, ((0, 0), (0, SMALL_PAD - SMALL_N)), constant_values=fill)

    small_out = adamw(pack(0, 0.0), small_all.reshape(NDEV * B, 1, SMALL_PAD), pack(1, 0.0), pack(2, 1.0), "adamw_small")
    loss = small_out[0][0, SMALL_N]

    res = {}
    off = 0
    for n, size in SMALL:
        shp = small_w[n][0].shape
        res[n] = [o[:, off:off + size].reshape(shp) for o in small_out]
        off += size
    res["ada_w"] = [o[None] for o in adamw(ada_w[0], g_ada_w[None], m_ada_w[0], v_ada_w[0], "adamw_ada_w")]
    moments = dict(w_in=(m_w_in, v_w_in), dn_conv_w=(m_dn_conv_w, v_dn_conv_w), w_attn_branch=(m_w_attn_branch, v_w_attn_branch),
                   w_dn_branch=(m_w_dn_branch, v_w_dn_branch), w_out=(m_w_out, v_w_out), ffn_w_up=(m_ffn_w_up, v_ffn_w_up),
                   ffn_conv_w=(m_ffn_conv_w, v_ffn_conv_w), ffn_w_down=(m_ffn_w_down, v_ffn_w_down))
    for n in big_names:
        outs = adamw(local(n, big[n]), parts[n], local(n, moments[n][0]), local(n, moments[n][1]), "adamw_" + n)
        res[n] = [(o.T if n in transposed else o)[None] for o in outs]

    order = ["ada_w", "ada_b", "norm_mix_pre", "norm_mix_post", "norm_ffn_pre", "norm_ffn_post", "w_in", "dn_conv_w", "dn_a_log",
             "dn_dt_bias", "dn_norm_w", "attn_sinks", "rel_bias", "w_attn_branch", "w_dn_branch", "w_out", "ffn_w_up", "ffn_conv_w",
             "ffn_w_down"]
    return (loss, grad_x, *[res[n][0] for n in order], *[res[n][1] for n in order], *[res[n][2] for n in order],
            *[res[n][3] for n in order])
```

```python
import functools
import math

import numpy as np
import jax
import jax.numpy as jnp
from jax import lax
from jax.experimental import pallas as pl
from jax.experimental.pallas import tpu as pltpu

f32 = jnp.float32
bf16 = jnp.bfloat16
HI = lax.Precision.HIGHEST
MID = lax.Precision.HIGH
MESH = pl.DeviceIdType.MESH

NDEV = 8
D = 1024
HQ, HKV, HD, WIN, NBUCK, MAXDIST = 8, 2, 64, 128, 32, 128
DNH, DND, DNK, CH = 4, 128, 4, 64
DFF, FK = 2816, 3
NMOD = 6
RMS_EPS = 1e-6
L2_EPS = 1e-6
NEG_INF = -1e30
LR, B1, B2, EPS, WD, STEP = 0.001, 0.9, 0.999, 1e-08, 0.01, 10

LANE = 128
CB_GA, CB_GD, CB_AQ, CB_DQKV, CB_DZ, CB_AK, CB_AV, CB_BA, NPB = 0, 8, 16, 20, 32, 36, 37, 38, 39
NP = NPB * LANE
IN_SPLITS = (HQ * HD, HKV * HD, HKV * HD, 3 * DNH * DND, DNH * DND, DNH, DNH, D, D)
IN_DIM = sum(IN_SPLITS)
VMEM_LIMIT = 56 * 1024 * 1024

SMALL = (("ada_b", NMOD * D), ("norm_mix_pre", D), ("norm_mix_post", D), ("norm_ffn_pre", D), ("norm_ffn_post", D),
         ("dn_a_log", DNH), ("dn_dt_bias", DNH), ("dn_norm_w", DND), ("attn_sinks", HQ), ("rel_bias", NBUCK * HQ))
SMALL_N = sum(n for _, n in SMALL)
SMALL_PAD = 10752


def _cp(sem):
    return pltpu.CompilerParams(dimension_semantics=sem, vmem_limit_bytes=VMEM_LIMIT)


def _pick(dim, target):
    if dim <= target:
        return dim
    best = None
    for d in range(LANE, target + 1, LANE):
        if dim % d == 0:
            best = d
    assert best is not None, (dim, target)
    return best


def _me():
    x, y, c = lax.axis_index("x"), lax.axis_index("y"), lax.axis_index("c")
    return x, y, c, 4 * x + 2 * y + c


def _peer(x, y, c, k):
    px = 1 - x if k & 4 else x
    py = 1 - y if k & 2 else y
    pc = 1 - c if k & 1 else c
    return (px, py, pc), 4 * px + 2 * py + pc


class _Comm:
    def __init__(self, arrs, two_level=False):
        self.arrs, self.n, self.two_level = list(arrs), len(arrs), two_level
        self.out_shape = [jax.ShapeDtypeStruct((NDEV,) + a.shape, a.dtype) for a in arrs]
        nsem = self.n * (NDEV - 1)
        self.scratch = [pltpu.SemaphoreType.DMA((nsem,)), pltpu.SemaphoreType.DMA((nsem,)), pltpu.SemaphoreType.DMA((self.n,))]
        self.specs = [pl.BlockSpec(memory_space=pl.ANY)] * self.n

    def phases(self, ins, out, send, recv, loc):
        x, y, c, me = _me()

        def remote(a, k, src, dst, to):
            s = a * (NDEV - 1) + k - 1
            return pltpu.make_async_remote_copy(src_ref=src, dst_ref=dst, send_sem=send.at[s], recv_sem=recv.at[s],
                                                device_id=to, device_id_type=MESH)

        def local(a):
            return pltpu.make_async_copy(ins[a], out[a].at[me], loc.at[a])

        if not self.two_level:
            def mine(a, k):
                peer, pid = _peer(x, y, c, k)
                return remote(a, k, ins[a], out[a].at[me], peer)

            def theirs(a, k):
                peer, pid = _peer(x, y, c, k)
                return remote(a, k, ins[a], out[a].at[pid], peer)

            def start():
                for a in range(self.n):
                    local(a).start()
                    for k in range(1, NDEV):
                        mine(a, k).start()

            def forward():
                pass

            def finish():
                for a in range(self.n):
                    for k in range(1, NDEV):
                        mine(a, k).wait_send()
                    for k in range(1, NDEV):
                        theirs(a, k).wait_recv()
                    local(a).wait()

            return start, forward, finish

        sibling = (x, y, 1 - c)
        chips = [(1 - x, y), (x, 1 - y), (1 - x, 1 - y)]
        slot = lambda px, py, pc: 4 * px + 2 * py + pc

        def own(a, k, to):
            return remote(a, k, ins[a], out[a].at[me], to)

        def landed(a, k, frm):
            return remote(a, k, ins[a], out[a].at[slot(*frm)], frm)

        def passed(a, j):
            rows = out[a].at[slot(*chips[j], c)]
            return remote(a, 5 + j, rows, rows, sibling)

        def start():
            for a in range(self.n):
                local(a).start()
                own(a, 1, sibling).start()
                for j, chip in enumerate(chips):
                    own(a, 2 + j, (*chip, c)).start()

        def forward():
            for a in range(self.n):
                for j, chip in enumerate(chips):
                    landed(a, 2 + j, (*chip, c)).wait_recv()
                    passed(a, j).start()

        def finish():
            for a in range(self.n):
                landed(a, 1, sibling).wait_recv()
                for j, chip in enumerate(chips):
                    remote(a, 5 + j, ins[a], out[a].at[slot(*chip, 1 - c)], sibling).wait_recv()
                own(a, 1, sibling).wait_send()
                for j, chip in enumerate(chips):
                    own(a, 2 + j, (*chip, c)).wait_send()
                    passed(a, j).wait_send()
                local(a).wait()

        return start, forward, finish


def _ride(body, n_in, n_out, n_scr, comm, first, mid, last):
    k = comm.n

    def wrapped(*refs):
        ins, cins = refs[:n_in], refs[n_in:n_in + k]
        o0 = n_in + k
        outs, couts = refs[o0:o0 + n_out], refs[o0 + n_out:o0 + n_out + k]
        s0 = o0 + n_out + k
        scr, sems = refs[s0:s0 + n_scr], refs[s0 + n_scr:]
        start, forward, finish = comm.phases(cins, couts, *sems)
        pl.when(first())(start)
        body(*ins, *outs, *scr)
        pl.when(mid())(forward)
        pl.when(last())(finish)

    return wrapped


def _scatter_start(arrs, name):
    n = len(arrs)

    def body(*refs):
        ins, lands, send, recv, own, token = refs[:n], refs[n:2 * n], refs[2 * n], refs[2 * n + 1], refs[2 * n + 2], refs[-1]
        x, y, c, me = _me()
        for a in range(n):
            pltpu.make_async_copy(ins[a].at[me], lands[a].at[me], own.at[a]).start()
            for k in range(1, NDEV):
                peer, pid = _peer(x, y, c, k)
                s = a * (NDEV - 1) + k - 1
                pltpu.make_async_remote_copy(src_ref=ins[a].at[pid], dst_ref=lands[a].at[me], send_sem=send.at[s], recv_sem=recv.at[s],
                                             device_id=peer, device_id_type=MESH).start()
        token[...] = jnp.zeros(token.shape, token.dtype)

    hbm, sem = pl.BlockSpec(memory_space=pltpu.HBM), pl.BlockSpec(memory_space=pltpu.SEMAPHORE)
    nsem = n * (NDEV - 1)
    thru = [pltpu.HBM(a.shape, a.dtype) for a in arrs]
    return pl.pallas_call(
        body, name=name, in_specs=[hbm] * (2 * n),
        out_shape=(pltpu.SemaphoreType.DMA((nsem,)), pltpu.SemaphoreType.DMA((nsem,)), pltpu.SemaphoreType.DMA((n,)), *thru, *thru,
                   jax.ShapeDtypeStruct((8, LANE), f32)),
        out_specs=(sem, sem, sem, *[hbm] * (2 * n), pl.BlockSpec(memory_space=pltpu.VMEM)),
        input_output_aliases={i: 3 + i for i in range(2 * n)},
        compiler_params=pltpu.CompilerParams(has_side_effects=pltpu.SideEffectType.DATAFLOW_SIDE_EFFECTING),
    )(*[pltpu.with_memory_space_constraint(a, pltpu.HBM) for a in arrs],
      *[pltpu.with_memory_space_constraint(lax.empty(a.shape, a.dtype), pltpu.HBM) for a in arrs])


def _scatter_finish(started, n, after, name):
    send, recv, own, *rest = started
    srcs, lands = rest[:n], rest[n:2 * n]

    def body(*refs):
        ins, lnd, send_ref, recv_ref, own_ref = refs[:n], refs[n:2 * n], refs[2 * n], refs[2 * n + 1], refs[2 * n + 2]
        x, y, c, me = _me()
        for a in range(n):
            pltpu.make_async_copy(ins[a].at[me], lnd[a].at[me], own_ref.at[a]).wait()
            for k in range(1, NDEV):
                peer, pid = _peer(x, y, c, k)
                s = a * (NDEV - 1) + k - 1
                cp = pltpu.make_async_remote_copy(src_ref=ins[a].at[pid], dst_ref=lnd[a].at[pid], send_sem=send_ref.at[s],
                                                  recv_sem=recv_ref.at[s], device_id=peer, device_id_type=MESH)
                cp.wait_send()
                cp.wait_recv()

    hbm, sem = pl.BlockSpec(memory_space=pltpu.HBM), pl.BlockSpec(memory_space=pltpu.SEMAPHORE)
    thru = [pltpu.HBM(a.shape, a.dtype) for a in srcs]
    out = pl.pallas_call(
        body, name=name, in_specs=[hbm] * (2 * n) + [sem, sem, sem, pl.BlockSpec(memory_space=pl.ANY)],
        out_shape=(*thru, *thru), out_specs=tuple([hbm] * (2 * n)), input_output_aliases={i: i for i in range(2 * n)},
        compiler_params=pltpu.CompilerParams(has_side_effects=pltpu.SideEffectType.DATAFLOW_SIDE_EFFECTING),
    )(*srcs, *lands, send, recv, own, after)
    return list(out[n:])


def _exchange(arrs, name, two_level=False):
    comm = _Comm(arrs, two_level)

    def body(*refs):
        start, forward, finish = comm.phases(refs[:comm.n], refs[comm.n:2 * comm.n], *refs[2 * comm.n:])
        start()
        forward()
        finish()

    return pl.pallas_call(body, name=name, out_shape=comm.out_shape, in_specs=comm.specs, out_specs=comm.specs,
                          scratch_shapes=comm.scratch, compiler_params=pltpu.CompilerParams(has_side_effects=True))(*arrs)


def mm(a, b, mode, out_dtype, name, tm=1024, tn=1024, tk=1024, precision=None, comm=None, b_cols=None):
    a_parts = a.shape[0] if a.ndim == 3 else 1
    b_parts = b.shape[0] if b.ndim == 3 else 1
    assert b_parts == 1 or mode == "tn"
    ash, bsh = (a.shape[-2], a.shape[-1] * a_parts), b.shape[-2:]
    if mode == "nn":
        (M, K), (K2, N) = ash, bsh
    elif mode == "nt":
        (M, K), (N, K2) = ash, bsh
    else:
        (K, M), (K2, N) = ash, (bsh[0], bsh[1] * b_parts)
    assert K == K2, (name, a.shape, b.shape)
    col0 = 0
    if b_cols is not None:
        assert mode in ("nn", "nt") and tn % LANE == 0
        col0, N = b_cols[0], b_cols[1] * tn
    if mode == "tn":
        tm, tn, tk = _pick(M // a_parts, tm), _pick(N // b_parts, tn), _pick(K, tk)
    else:
        tm, tn, tk = _pick(M, tm), _pick(N // b_parts, tn), _pick(K // a_parts, tk)
    nk = K // tk
    if mode == "tn" and a_parts > 1:
        per = M // tm // a_parts
        a_spec = pl.BlockSpec((None, tk, tm), lambda i, j, k: (i // per, k, i % per))
    elif mode == "tn":
        a_spec = pl.BlockSpec((tk, tm), lambda i, j, k: (k, i))
    elif a_parts > 1:
        per = nk // a_parts
        a_spec = pl.BlockSpec((None, tm, tk), lambda i, j, k: (k // per, i, k % per))
    else:
        a_spec = pl.BlockSpec((tm, tk), lambda i, j, k: (i, k))
    if mode == "nt":
        b_spec = pl.BlockSpec((tn, tk), lambda i, j, k: (col0 + j, k))
    elif b_parts > 1:
        per = N // tn // b_parts
        b_spec = pl.BlockSpec((None, tk, tn), lambda i, j, k: (j // per, k, j % per))
    else:
        b_spec = pl.BlockSpec((tk, tn), lambda i, j, k: (k, col0 + j))
    dims = {"nn": ((1,), (0,)), "nt": ((1,), (1,)), "tn": ((0,), (0,))}[mode]

    def body(a_ref, b_ref, o_ref, *scr):
        p = lax.dot_general(a_ref[...], b_ref[...], (dims, ((), ())), preferred_element_type=f32, precision=precision)
        if nk == 1:
            o_ref[...] = p.astype(o_ref.dtype)
        else:
            acc = scr[0]
            k = pl.program_id(2)

            @pl.when(k == 0)
            def _():
                acc[...] = p

            @pl.when(k > 0)
            def _():
                acc[...] += p

            @pl.when(k == nk - 1)
            def _():
                o_ref[...] = acc[...].astype(o_ref.dtype)

    grid = (M // tm, N // tn, nk)
    scratch = [pltpu.VMEM((tm, tn), f32)] if nk > 1 else []
    out_spec = pl.BlockSpec((tm, tn), lambda i, j, k: (i, j))
    out_shape = jax.ShapeDtypeStruct((M, N), out_dtype)
    if comm is None:
        return pl.pallas_call(body, name=name, grid=grid, in_specs=[a_spec, b_spec], out_specs=out_spec, out_shape=out_shape,
                              scratch_shapes=scratch, compiler_params=_cp(("parallel", "parallel", "arbitrary")))(a, b)
    at = lambda pos: lambda: functools.reduce(jnp.logical_and, [pl.program_id(d) == p for d, p in enumerate(pos)])
    end = tuple(g - 1 for g in grid)
    return pl.pallas_call(
        _ride(body, 2, 1, len(scratch), comm, at((0, 0, 0)), at(end), at(end)), name=name, grid=grid,
        in_specs=[a_spec, b_spec] + comm.specs, out_specs=[out_spec] + comm.specs, out_shape=[out_shape] + comm.out_shape,
        scratch_shapes=scratch + comm.scratch, compiler_params=_cp(("arbitrary", "arbitrary", "arbitrary")),
    )(a, b, *comm.arrs)


ROW_TILE = 512


def rowcall(name, fn, tok, bat, con, tok_out, acc_out, ts=ROW_TILE, into=None):
    B, S = tok[0][0].shape[:2]
    ts = min(ts, S)
    nt, nb, nc, no, na = len(tok), len(bat), len(con), len(tok_out), len(acc_out)
    nin = nt + nb + nc + (1 if into is not None else 0)

    def body(*refs):
        tr, br, cr = refs[:nt], refs[nt:nt + nb], refs[nt + nb:nt + nb + nc]
        orf, arf = refs[nin:nin + no], refs[nin + no:]
        touts, aouts = fn([r[0] for r in tr], [r[0] for r in br], [r[...] for r in cr])
        for r, v in zip(orf, touts):
            r[0] = v.astype(r.dtype)
        s = pl.program_id(1)
        for r, v in zip(arf, aouts):
            @pl.when(s == 0)
            def _(r=r):
                r[...] = jnp.zeros(r.shape, r.dtype)
            r[0] += v.astype(f32)

    in_specs = [pl.BlockSpec((1, ts, w), lambda b, s, cb=cb: (b, s, cb)) for (_, w, cb) in tok]
    in_specs += [pl.BlockSpec((1,) + a.shape[1:], lambda b, s: (b, 0, 0)) for a in bat]
    in_specs += [pl.BlockSpec(a.shape, lambda b, s, nd=a.ndim: (0,) * nd) for a in con]
    out_specs = [pl.BlockSpec((1, ts, w), lambda b, s: (b, s, 0)) for (w, _) in tok_out]
    out_specs += [pl.BlockSpec((1,) + shp, lambda b, s, nd=len(shp): (b,) + (0,) * nd) for shp in acc_out]
    out_shape = [jax.ShapeDtypeStruct((B, S, w), dt) for (w, dt) in tok_out]
    out_shape += [jax.ShapeDtypeStruct((B,) + shp, f32) for shp in acc_out]
    extra, aliases = [], {}
    if into is not None:
        buf, cb = into
        assert buf.dtype == tok_out[0][1]
        in_specs.append(pl.BlockSpec(memory_space=pl.ANY))
        out_specs[0] = pl.BlockSpec((1, ts, tok_out[0][0]), lambda b, s: (b, s, cb))
        out_shape[0] = jax.ShapeDtypeStruct(buf.shape, buf.dtype)
        extra, aliases = [buf], {nin - 1: 0}
    return pl.pallas_call(
        body, name=name, grid=(B, S // ts), in_specs=in_specs, out_specs=out_specs, out_shape=out_shape,
        input_output_aliases=aliases, compiler_params=_cp(("parallel", "arbitrary")),
    )(*[t[0] for t in tok], *bat, *con, *extra)


def rowcall_fwd(name, f, tok, bat, con, tok_out, ts=ROW_TILE):
    def fn(t, b, c):
        return f([v.astype(f32) for v in t], b, c), []
    return rowcall(name, fn, tok, bat, con, tok_out, [], ts)


def rowcall_bwd(name, f, tok, bat, con, cts, tok_grads, add=None, ts=ROW_TILE, join_first=1, into=None):
    nt, ncts = len(tok), len(cts)

    def fn(t, b, c):
        prim = [v.astype(f32) for v in t[:nt]]
        ct = [v.astype(f32) for v in t[nt:nt + ncts]]
        _, vjp = jax.vjp(lambda tt, bb, cc: f(tt, bb, cc), prim, b, c)
        dt, db, dc = vjp(ct)
        touts = [dt[i] for i, _ in tok_grads]
        if add is not None:
            touts[0] = touts[0] + t[nt + ncts].astype(f32)
        if join_first > 1:
            touts = [jnp.concatenate(touts[:join_first], axis=1)] + touts[join_first:]
        return touts, list(db) + list(dc)

    all_tok = list(tok) + list(cts) + ([add] if add is not None else [])
    tok_out = [(tok[i][1], dt) for i, dt in tok_grads]
    if join_first > 1:
        tok_out = [(sum(w for w, _ in tok_out[:join_first]), tok_out[0][1])] + tok_out[join_first:]
    acc_out = [tuple(a.shape[1:]) for a in bat] + [tuple(a.shape) for a in con]
    return rowcall(name, fn, all_tok, bat, con, tok_out, acc_out, ts, into)


def _rms(y, w):
    return y * lax.rsqrt(jnp.mean(y * y, axis=-1, keepdims=True) + RMS_EPS) * w


def f_rms_mod(t, b, c):
    return [_rms(t[0], c[0]) * (1.0 + b[0]) + b[1]]


def f_post_pre(t, b, c):
    h1 = t[0] + b[0] * _rms(t[1], c[0])
    return [h1, _rms(h1, c[1]) * (1.0 + b[1]) + b[2]]


def f_merge(t, b, c):
    ga, gd, ya, yd = t
    return [jax.nn.sigmoid(ga) * ya + jax.nn.sigmoid(gd) * yd]


def f_dnout(t, b, c):
    o, z = t
    outs = []
    for h in range(DNH):
        sl = slice(h * DND, (h + 1) * DND)
        zh = z[:, sl]
        outs.append(_rms(o[:, sl], c[0]) * (zh * jax.nn.sigmoid(zh)))
    return [jnp.concatenate(outs, axis=1)]


def _softplus(x):
    return jnp.maximum(x, 0.0) + jnp.log(1.0 + jnp.exp(-jnp.abs(x)))


def f_gate(t, b, c):
    ba = t[0]
    a_log, dt_bias = c
    lane = lax.broadcasted_iota(jnp.int32, ba.shape, 1)
    beta = jax.nn.sigmoid(ba)
    g = -jnp.exp(a_log) * _softplus(ba + dt_bias)
    return [jnp.where(lane < DNH, beta, jnp.where(lane < 2 * DNH, g, 0.0))]


def _bucket_table():
    qi = np.arange(WIN)[:, None]
    kj = np.arange(2 * WIN)[None, :]
    dist = np.maximum(WIN + qi - kj, 0)
    max_exact = NBUCK // 2
    scaled = np.log(np.maximum(dist, 1).astype(np.float64) / max_exact) / math.log(MAXDIST / max_exact)
    large = np.minimum(max_exact + (scaled * (NBUCK - max_exact)).astype(np.int32), NBUCK - 1)
    return np.where(dist < max_exact, dist, large).astype(np.int32)


def _attn_mask(n):
    qi = lax.broadcasted_iota(jnp.int32, (WIN, 2 * WIN), 0)
    kj = lax.broadcasted_iota(jnp.int32, (WIN, 2 * WIN), 1)
    dist = WIN + qi - kj
    return (dist >= 0) & (dist < WIN) & ((kj >= WIN) | (n > 0))


def _swap_halves(x):
    return pltpu.roll(x, HD, axis=x.ndim - 1)


@jax.custom_vjp
def _swap_halves_vjp(x):
    return _swap_halves(x)


_swap_halves_vjp.defvjp(lambda x: (_swap_halves(x), None), lambda _, g: (_swap_halves(g),))


def _attn_block(q, kp, kc, vp, vc, bias, sinks, mask, differentiated):
    dot = _bdot_bf16_vjp if differentiated else _bdot_bf16
    swap = _swap_halves_vjp if differentiated else _swap_halves
    B, grp = q.shape[0], HQ // HKV
    upper = lax.broadcasted_iota(jnp.int32, (2 * WIN, LANE), 1) >= HD

    def placed(natural, swapped, j, half):
        keep = upper if half == 1 else ~upper
        return jnp.where(keep, natural if j == half else swapped, 0.0)

    qh, ks, vs = [], [], []
    for b in range(B):
        kb, vb = jnp.concatenate([kp[b], kc[b]], axis=0), jnp.concatenate([vp[b], vc[b]], axis=0)
        kb_sw, vb_sw = swap(kb), swap(vb)
        for h in range(HQ):
            qh.append(q[b, :, (h // 2) * LANE:(h // 2 + 1) * LANE])
            ks.append(placed(kb, kb_sw, h // grp, h % 2))
            vs.append(placed(vb, vb_sw, h // grp, h % 2))
    s = dot(_stack(qh), _stack(ks), 2, 2).reshape(B, HQ, WIN, 2 * WIN) * (HD ** -0.5)
    s = jnp.where(mask, s + bias, NEG_INF)
    m = jnp.maximum(jnp.max(s, axis=-1, keepdims=True), sinks)
    p = jnp.exp(s - m)
    probs = p / (jnp.sum(p, axis=-1, keepdims=True) + jnp.exp(sinks - m))
    o = dot(probs.reshape(B * HQ, WIN, 2 * WIN), _stack(vs), 2, 1)
    return _stack([jnp.concatenate([o[b * HQ + 2 * i] + o[b * HQ + 2 * i + 1] for i in range(HQ // 2)], axis=1) for b in range(B)])


def _attn_specs(B, NB):
    last = NB - 1
    return [
        pl.BlockSpec((B, WIN, HQ * HD), lambda n: (0, jnp.minimum(n, last), CB_AQ // 4)),
        pl.BlockSpec((B, WIN, LANE), lambda n: (0, jnp.clip(n - 1, 0, last), CB_AK)),
        pl.BlockSpec((B, WIN, LANE), lambda n: (0, jnp.minimum(n, last), CB_AK)),
        pl.BlockSpec((B, WIN, LANE), lambda n: (0, jnp.clip(n - 1, 0, last), CB_AV)),
        pl.BlockSpec((B, WIN, LANE), lambda n: (0, jnp.minimum(n, last), CB_AV)),
        pl.BlockSpec((HQ, WIN, 2 * WIN), lambda n: (0, 0, 0)),
        pl.BlockSpec((HQ, 1, 1), lambda n: (0, 0, 0)),
    ]


def attn_fwd(proj, bias, sinks, comm):
    B, S, _ = proj.shape
    NB = S // WIN

    def body(q, kp, kc, vp, vc, bias_ref, sink_ref, o_ref):
        mask = _attn_mask(pl.program_id(0))
        o = _attn_block(*[r[...].astype(f32) for r in (q, kp, kc, vp, vc)], bias_ref[...], sink_ref[...], mask, False)
        o_ref[...] = o.astype(o_ref.dtype)

    at = lambda n: lambda: pl.program_id(0) == n
    return pl.pallas_call(
        _ride(body, 7, 1, 0, comm, at(0), at((3 * NB) // 4), at(NB - 1)), name="attn_fwd", grid=(NB,),
        in_specs=_attn_specs(B, NB) + comm.specs,
        out_specs=[pl.BlockSpec((B, WIN, HQ * HD), lambda n: (0, n, 0))] + comm.specs,
        out_shape=[jax.ShapeDtypeStruct((B, S, HQ * HD), bf16)] + comm.out_shape, scratch_shapes=comm.scratch,
        compiler_params=_cp(("arbitrary",)),
    )(proj, proj, proj, proj, proj, bias, sinks, *comm.arrs)


def attn_bwd(proj, bias, sinks, dy, dproj):
    B, S, _ = proj.shape
    NB = S // WIN
    last = NB - 1

    def body(q, kp, kc, vp, vc, bias_ref, sink_ref, dy_ref, _, dq_ref, dk_ref, dv_ref, dbias_ref, dsink_ref, kcar, vcar):
        n = pl.program_id(0)

        @pl.when(n == 0)
        def _():
            dbias_ref[...] = jnp.zeros(dbias_ref.shape, f32)
            dsink_ref[...] = jnp.zeros(dsink_ref.shape, f32)
            kcar[...] = jnp.zeros(kcar.shape, f32)
            vcar[...] = jnp.zeros(vcar.shape, f32)

        @pl.when(n < NB)
        def _():
            mask = _attn_mask(n)
            _, vjp = jax.vjp(lambda *a: _attn_block(*a, mask, True), *[r[...].astype(f32) for r in (q, kp, kc, vp, vc)],
                             bias_ref[...], sink_ref[...])
            dq, dkp, dkc, dvp, dvc, dbias, dsink = vjp(dy_ref[...].astype(f32))
            dq_ref[...] = dq.astype(dq_ref.dtype)
            dbias_ref[...] += dbias
            dsink_ref[...] += dsink
            dk_ref[...] = (kcar[...] + dkp).astype(dk_ref.dtype)
            dv_ref[...] = (vcar[...] + dvp).astype(dv_ref.dtype)
            kcar[...] = dkc
            vcar[...] = dvc

        @pl.when(n == NB)
        def _():
            dk_ref[...] = kcar[...].astype(dk_ref.dtype)
            dv_ref[...] = vcar[...].astype(dv_ref.dtype)

    in_specs = _attn_specs(B, NB) + [pl.BlockSpec((B, WIN, HQ * HD), lambda n: (0, jnp.minimum(n, last), 0)),
                                     pl.BlockSpec(memory_space=pl.ANY)]
    kv_out = pl.BlockSpec((B, WIN, LANE), lambda n: (0, jnp.maximum(n - 1, 0), 0))
    return pl.pallas_call(
        body, name="attn_bwd", grid=(NB + 1,), in_specs=in_specs, input_output_aliases={8: 0},
        out_specs=[pl.BlockSpec((B, WIN, HQ * HD), lambda n: (0, jnp.minimum(n, last), CB_AQ // 4)), kv_out, kv_out,
                   pl.BlockSpec((HQ, WIN, 2 * WIN), lambda n: (0, 0, 0)), pl.BlockSpec((HQ, 1, 1), lambda n: (0, 0, 0))],
        out_shape=[jax.ShapeDtypeStruct(dproj.shape, dproj.dtype), jax.ShapeDtypeStruct((B, S, LANE), bf16),
                   jax.ShapeDtypeStruct((B, S, LANE), bf16), jax.ShapeDtypeStruct((HQ, WIN, 2 * WIN), f32),
                   jax.ShapeDtypeStruct((HQ, 1, 1), f32)],
        scratch_shapes=[pltpu.VMEM((B, WIN, LANE), f32), pltpu.VMEM((B, WIN, LANE), f32)],
        compiler_params=_cp(("arbitrary",)),
    )(proj, proj, proj, proj, proj, bias, sinks, dy, dproj)


DN_ROWS, FFN_ROWS = 256, 32


def _stage_rows(dst, value):
    dst[0:8] = jnp.zeros((8, LANE), f32)
    dst[8:8 + value.shape[0]] = value


def _conv_rows(xs, w, width, r, rows):
    wins = [xs[pl.ds(r + 8 - (width - 1) + j, rows), :] for j in range(width)]
    out = w[0:1] * wins[0]
    for j in range(1, width):
        out = out + w[j:j + 1] * wins[j]
    return out, wins


def _fold8(v):
    return jnp.sum(v.reshape(v.shape[0] // 8, 8, LANE), axis=0)


def _conv_rows_t(ds, w, width, r, rows):
    out = w[0:1] * ds[pl.ds(r + width - 1, rows), :]
    for j in range(1, width):
        out = out + w[j:j + 1] * ds[pl.ds(r + width - 1 - j, rows), :]
    return out


def _dn_outblk(i):
    return (i % DNH) * 3 + i // DNH


def _dn_act(c, isqk):
    sg = jax.nn.sigmoid(c)
    y = c * sg
    n = lax.rsqrt(jnp.sum(y * y, axis=-1, keepdims=True) + L2_EPS)
    return jnp.where(isqk, y * n, y), sg, n


def dnconv_fwd(proj, conv_w):
    B, S, _ = proj.shape
    rows = min(DN_ROWS, S)

    def body(x_ref, w_ref, o_ref, xs):
        isqk = pl.program_id(0) < 2 * DNH
        _stage_rows(xs, x_ref[0].astype(f32))
        w = w_ref[...]
        for r in range(0, S, rows):
            c, _ = _conv_rows(xs, w, DNK, r, rows)
            o_ref[0, pl.ds(r, rows), :] = _dn_act(c, isqk)[0]

    return pl.pallas_call(
        body, name="dnconv_fwd", grid=(3 * DNH, B),
        in_specs=[pl.BlockSpec((1, S, LANE), lambda i, b: (b, 0, CB_DQKV + i)), pl.BlockSpec((DNK, LANE), lambda i, b: (0, i))],
        out_specs=pl.BlockSpec((1, S, LANE), lambda i, b: (b, 0, _dn_outblk(i))),
        out_shape=jax.ShapeDtypeStruct((B, S, 3 * DNH * DND), f32), scratch_shapes=[pltpu.VMEM((S + 8, LANE), f32)],
        compiler_params=_cp(("parallel", "parallel")),
    )(proj, conv_w)


def dnconv_bwd(proj, conv_w, dqkvn, dproj):
    B, S, _ = proj.shape
    rows = min(DN_ROWS, S)

    def body(x_ref, w_ref, dy_ref, _, dx_ref, dw_ref, xs, ds):
        isqk = pl.program_id(0) < 2 * DNH
        _stage_rows(xs, x_ref[0].astype(f32))
        w = w_ref[...]
        dw = [jnp.zeros((8, LANE), f32) for _ in range(DNK)]
        for r in range(0, S, rows):
            c, wins = _conv_rows(xs, w, DNK, r, rows)
            out, sg, n = _dn_act(c, isqk)
            dout = dy_ref[0, pl.ds(r, rows), :]
            dy = jnp.where(isqk, n * (dout - out * jnp.sum(dout * out, axis=-1, keepdims=True)), dout)
            dc = dy * (sg * (1.0 + c * (1.0 - sg)))
            ds[pl.ds(r, rows), :] = dc
            for j in range(DNK):
                dw[j] = dw[j] + _fold8(dc * wins[j])
        ds[S:S + 8] = jnp.zeros((8, LANE), f32)
        for r in range(0, S, rows):
            dx_ref[0, pl.ds(r, rows), :] = _conv_rows_t(ds, w, DNK, r, rows).astype(dx_ref.dtype)

        @pl.when(pl.program_id(1) == 0)
        def _():
            dw_ref[...] = jnp.zeros(dw_ref.shape, f32)
        dw_ref[...] += jnp.concatenate([jnp.sum(d, axis=0, keepdims=True) for d in dw], axis=0)

    return pl.pallas_call(
        body, name="dnconv_bwd", grid=(3 * DNH, B),
        in_specs=[pl.BlockSpec((1, S, LANE), lambda i, b: (b, 0, CB_DQKV + i)), pl.BlockSpec((DNK, LANE), lambda i, b: (0, i)),
                  pl.BlockSpec((1, S, LANE), lambda i, b: (b, 0, _dn_outblk(i))), pl.BlockSpec(memory_space=pl.ANY)],
        out_specs=[pl.BlockSpec((1, S, LANE), lambda i, b: (b, 0, CB_DQKV + i)), pl.BlockSpec((DNK, LANE), lambda i, b: (0, i))],
        out_shape=[jax.ShapeDtypeStruct(dproj.shape, dproj.dtype), jax.ShapeDtypeStruct((DNK, 3 * DNH * DND), f32)],
        scratch_shapes=[pltpu.VMEM((S + 8, LANE), f32), pltpu.VMEM((S + 8, LANE), f32)],
        input_output_aliases={3: 0}, compiler_params=_cp(("parallel", "arbitrary")),
    )(proj, conv_w, dqkvn, dproj)


def _bdot(a, b, ca, cb, precision=HI):
    return lax.dot_general(a, b, (((ca,), (cb,)), ((0,), (0,))), preferred_element_type=f32, precision=precision)


def _bdot_bf16(a, b, ca, cb):
    return _bdot(a.astype(bf16), b.astype(bf16), ca, cb, None)


@functools.partial(jax.custom_vjp, nondiff_argnums=(2, 3))
def _bdot_bf16_vjp(a, b, ca, cb):
    return _bdot_bf16(a, b, ca, cb)


def _bdot_bf16_fwd(a, b, ca, cb):
    return _bdot_bf16(a, b, ca, cb), (a, b)


def _bdot_bf16_bwd(ca, cb, res, g):
    a, b = res
    fa, fb = 3 - ca, 3 - cb
    da = _bdot_bf16(g, b, 2, fb) if ca == 2 else _bdot_bf16(b, g, fb, 2)
    db = _bdot_bf16(a, g, fa, 1) if cb == 1 else _bdot_bf16(g, a, 1, fa)
    return da, db


_bdot_bf16_vjp.defvjp(_bdot_bf16_fwd, _bdot_bf16_bwd)


def _neumann_inverse(low):
    n = low.shape[-1]
    eye = (lax.broadcasted_iota(jnp.int32, (n, n), 0) == lax.broadcasted_iota(jnp.int32, (n, n), 1)).astype(f32)
    p = -low
    x = eye[None] + p
    for _ in range(5):
        p = _bdot_bf16(p, p, 2, 1)
        x = x + _bdot_bf16(x, p, 2, 1)
    return x


@jax.custom_vjp
def _unit_lower_inverse(low):
    return _neumann_inverse(low)


def _uli_fwd(low):
    t = _neumann_inverse(low)
    return t, t


def _uli_bwd(t, dt):
    return (-_bdot_bf16(_bdot_bf16(t, dt, 1, 1), t, 2, 2),)


_unit_lower_inverse.defvjp(_uli_fwd, _uli_bwd)


def _stack(xs):
    return jnp.concatenate([x[None] for x in xs], axis=0)


DELTA_CHUNKS = 2


def _delta_chunks(qkv, bg, state, differentiated):
    inverse = _unit_lower_inverse if differentiated else _neumann_inverse
    lo = _bdot_bf16_vjp if differentiated else _bdot_bf16
    B, n = qkv.shape[0], qkv.shape[1] // CH
    G = B * DNH
    N = n * G
    triples = [(i, b, h) for i in range(n) for b in range(B) for h in range(DNH)]
    col = lambda i, b, h, kind: qkv[b, i * CH:(i + 1) * CH, (3 * h + kind) * DND:(3 * h + kind + 1) * DND]
    q, k, v = [_stack([col(i, b, h, kind) for i, b, h in triples]) for kind in range(3)]
    lane = lax.broadcasted_iota(jnp.int32, (CH, LANE), 1)
    pick = lambda i, b, l: jnp.sum(jnp.where(lane == l, bg[b, i * CH:(i + 1) * CH], 0.0), axis=1, keepdims=True)
    beta = _stack([pick(i, b, h) for i, b, h in triples])
    g = _stack([pick(i, b, h + DNH) for i, b, h in triples])
    ri = lax.broadcasted_iota(jnp.int32, (CH, CH), 0)
    ci = lax.broadcasted_iota(jnp.int32, (CH, CH), 1)
    incl, strict = (ri >= ci)[None], (ri > ci)[None]
    gc = _bdot(jnp.broadcast_to(incl.astype(f32), (N, CH, CH)), jnp.broadcast_to(g, (N, CH, LANE)), 2, 1, MID)
    e0 = jnp.broadcast_to((lane == 0).astype(f32)[None], (N, CH, LANE))
    gc_row = _bdot(e0, gc, 2, 2, MID)
    diff = gc[:, :, :CH] - gc_row
    decay = jnp.where(incl, jnp.exp(jnp.where(incl, diff, 0.0)), 0.0)
    qs = q * (DND ** -0.5)
    kb, vb = k * beta, v * beta
    eg = jnp.exp(gc)
    with_k = lo(jnp.concatenate([kb, qs], axis=1), k, 2, 2)
    low = jnp.where(strict, with_k[:, :CH] * decay, 0.0)
    intra = jnp.where(incl, with_k[:, CH:] * decay, 0.0)
    tinv = inverse(low)
    solved = lo(tinv, jnp.concatenate([vb, kb * eg], axis=2), 2, 1)
    gl = gc[:, CH - 1:CH, :]
    k_tail = k * jnp.exp(gl - gc)
    to_state = jnp.concatenate([solved[:, :, DND:], qs * eg], axis=1)
    decay_all = jnp.exp(gl)
    outs = []
    for i in range(n):
        sl = slice(i * G, (i + 1) * G)
        with_state = lo(to_state[sl], state, 2, 1)
        v_new = solved[sl, :, :DND] - with_state[:, :CH]
        outs.append(with_state[:, CH:] + lo(intra[sl], v_new, 2, 1))
        state = state * decay_all[sl] + lo(k_tail[sl], v_new, 1, 1)
    return outs, state


def delta_fwd(qkvn, bg, comm):
    B, S, _ = qkvn.shape
    n = DELTA_CHUNKS if (S // CH) % DELTA_CHUNKS == 0 else 1
    steps, G, rows = S // (n * CH), B * DNH, n * CH

    def body(qkv_ref, bg_ref, o_ref, st_ref, state):
        @pl.when(pl.program_id(0) == 0)
        def _():
            state[...] = jnp.zeros(state.shape, f32)
        s0 = state[...]
        st_ref[0] = s0
        outs, s1 = _delta_chunks(qkv_ref[...], bg_ref[...], s0, False)
        for i, o in enumerate(outs):
            for b in range(B):
                for h in range(DNH):
                    o_ref[b, i * CH:(i + 1) * CH, h * DND:(h + 1) * DND] = o[b * DNH + h]
        state[...] = s1

    at = lambda c: lambda: pl.program_id(0) == c
    return pl.pallas_call(
        _ride(body, 2, 2, 1, comm, at(0), at((7 * steps) // 8), at(steps - 1)), name="delta_fwd", grid=(steps,),
        in_specs=[pl.BlockSpec((B, rows, 3 * DNH * DND), lambda c: (0, c, 0)), pl.BlockSpec((B, rows, LANE), lambda c: (0, c, 0))] + comm.specs,
        out_specs=[pl.BlockSpec((B, rows, DNH * DND), lambda c: (0, c, 0)), pl.BlockSpec((1, G, DND, DND), lambda c: (c, 0, 0, 0))] + comm.specs,
        out_shape=[jax.ShapeDtypeStruct((B, S, DNH * DND), f32), jax.ShapeDtypeStruct((steps, G, DND, DND), f32)] + comm.out_shape,
        scratch_shapes=[pltpu.VMEM((G, DND, DND), f32)] + comm.scratch, compiler_params=_cp(("arbitrary",)),
    )(qkvn, bg, *comm.arrs)


def delta_bwd(qkvn, bg, states, do):
    B, S, _ = qkvn.shape
    steps, G = states.shape[0], B * DNH
    rows = S // steps
    n = rows // CH

    def body(qkv_ref, bg_ref, st_ref, do_ref, dqkv_ref, dbg_ref, dstate):
        @pl.when(pl.program_id(0) == 0)
        def _():
            dstate[...] = jnp.zeros(dstate.shape, f32)
        _, vjp = jax.vjp(lambda a, g, s: _delta_chunks(a, g, s, True), qkv_ref[...], bg_ref[...], st_ref[0])
        do = [_stack([do_ref[b, i * CH:(i + 1) * CH, h * DND:(h + 1) * DND] for b in range(B) for h in range(DNH)]) for i in range(n)]
        dqkv, dbg, ds = vjp((do, dstate[...]))
        dqkv_ref[...] = dqkv
        dbg_ref[...] = dbg
        dstate[...] = ds

    rev = lambda c: steps - 1 - c
    return pl.pallas_call(
        body, name="delta_bwd", grid=(steps,),
        in_specs=[pl.BlockSpec((B, rows, 3 * DNH * DND), lambda c: (0, rev(c), 0)), pl.BlockSpec((B, rows, LANE), lambda c: (0, rev(c), 0)),
                  pl.BlockSpec((1, G, DND, DND), lambda c: (rev(c), 0, 0, 0)),
                  pl.BlockSpec((B, rows, DNH * DND), lambda c: (0, rev(c), 0))],
        out_specs=[pl.BlockSpec((B, rows, 3 * DNH * DND), lambda c: (0, rev(c), 0)), pl.BlockSpec((B, rows, LANE), lambda c: (0, rev(c), 0))],
        out_shape=[jax.ShapeDtypeStruct((B, S, 3 * DNH * DND), f32), jax.ShapeDtypeStruct((B, S, LANE), f32)],
        scratch_shapes=[pltpu.VMEM((G, DND, DND), f32)], compiler_params=_cp(("arbitrary",)),
    )(qkvn, bg, states, do)


GELU_C0, GELU_C1 = math.sqrt(2.0 / math.pi), 0.044715


def _ffn_specs(S):
    nblk = DFF // LANE
    return [pl.BlockSpec((1, S, LANE), lambda i, b: (b, 0, i)), pl.BlockSpec((1, S, LANE), lambda i, b: (b, 0, nblk + i)),
            pl.BlockSpec((FK, LANE), lambda i, b: (0, i)), pl.BlockSpec((FK, LANE), lambda i, b: (0, nblk + i))]


def ffnconv_fwd(up, conv_w):
    B, S, _ = up.shape
    rows = min(FFN_ROWS, S)

    def body(g_ref, v_ref, gw_ref, vw_ref, o_ref, xg, xv):
        _stage_rows(xg, g_ref[0].astype(f32))
        _stage_rows(xv, v_ref[0].astype(f32))
        gw, vw = gw_ref[...], vw_ref[...]
        for r in range(0, S, rows):
            g, _ = _conv_rows(xg, gw, FK, r, rows)
            v, _ = _conv_rows(xv, vw, FK, r, rows)
            t = jnp.tanh(GELU_C0 * (g * (1.0 + GELU_C1 * (g * g))))
            o_ref[0, pl.ds(r, rows), :] = (0.5 * g * (1.0 + t) * v).astype(o_ref.dtype)

    return pl.pallas_call(
        body, name="ffnconv_fwd", grid=(DFF // LANE, B), in_specs=_ffn_specs(S),
        out_specs=pl.BlockSpec((1, S, LANE), lambda i, b: (b, 0, i)), out_shape=jax.ShapeDtypeStruct((B, S, DFF), bf16),
        scratch_shapes=[pltpu.VMEM((S + 8, LANE), f32)] * 2, compiler_params=_cp(("parallel", "parallel")),
    )(up, up, conv_w, conv_w)


def ffnconv_bwd(up, conv_w, dact):
    B, S, _ = up.shape
    rows = min(FFN_ROWS, S)

    def body(g_ref, v_ref, gw_ref, vw_ref, dy_ref, dx_ref, dw_ref, xg, xv, dg, dv):
        _stage_rows(xg, g_ref[0].astype(f32))
        _stage_rows(xv, v_ref[0].astype(f32))
        gw, vw = gw_ref[...], vw_ref[...]
        dgw = [jnp.zeros((8, LANE), f32) for _ in range(FK)]
        dvw = [jnp.zeros((8, LANE), f32) for _ in range(FK)]
        for r in range(0, S, rows):
            g, gwins = _conv_rows(xg, gw, FK, r, rows)
            v, vwins = _conv_rows(xv, vw, FK, r, rows)
            g2 = g * g
            t = jnp.tanh(GELU_C0 * (g * (1.0 + GELU_C1 * g2)))
            half = 0.5 * (1.0 + t)
            dgelu = half + (0.5 * GELU_C0) * g * (1.0 - t * t) * (1.0 + (3.0 * GELU_C1) * g2)
            dy = dy_ref[0, pl.ds(r, rows), :].astype(f32)
            dvc = dy * (g * half)
            dgc = dy * v * dgelu
            dg[pl.ds(r, rows), :] = dgc
            dv[pl.ds(r, rows), :] = dvc
            for j in range(FK):
                dgw[j] = dgw[j] + _fold8(dgc * gwins[j])
                dvw[j] = dvw[j] + _fold8(dvc * vwins[j])
        dg[S:S + 8] = jnp.zeros((8, LANE), f32)
        dv[S:S + 8] = jnp.zeros((8, LANE), f32)
        for r in range(0, S, rows):
            dx_ref[0, 0, pl.ds(r, rows), :] = _conv_rows_t(dg, gw, FK, r, rows).astype(dx_ref.dtype)
            dx_ref[1, 0, pl.ds(r, rows), :] = _conv_rows_t(dv, vw, FK, r, rows).astype(dx_ref.dtype)

        @pl.when(pl.program_id(1) == 0)
        def _():
            dw_ref[...] = jnp.zeros(dw_ref.shape, f32)
        dw_ref[0] += jnp.concatenate([jnp.sum(d, axis=0, keepdims=True) for d in dgw], axis=0)
        dw_ref[1] += jnp.concatenate([jnp.sum(d, axis=0, keepdims=True) for d in dvw], axis=0)

    return pl.pallas_call(
        body, name="ffnconv_bwd", grid=(DFF // LANE, B),
        in_specs=_ffn_specs(S) + [pl.BlockSpec((1, S, LANE), lambda i, b: (b, 0, i))],
        out_specs=[pl.BlockSpec((2, 1, S, LANE), lambda i, b: (0, b, 0, i)), pl.BlockSpec((2, FK, LANE), lambda i, b: (0, 0, i))],
        out_shape=[jax.ShapeDtypeStruct((2, B, S, DFF), bf16), jax.ShapeDtypeStruct((2, FK, DFF), f32)],
        scratch_shapes=[pltpu.VMEM((S + 8, LANE), f32)] * 4, compiler_params=_cp(("parallel", "arbitrary")),
    )(up, up, conv_w, conv_w, dact)


def ada_fwd(c_all, ada_w, ada_b):
    def body(c_ref, w_ref, b_ref, o_ref):
        c = c_ref[...]
        act = (c * jax.nn.sigmoid(c)).astype(bf16)
        o_ref[...] = jnp.dot(act, w_ref[...].astype(bf16), preferred_element_type=f32) + b_ref[...]

    return pl.pallas_call(body, name="ada_fwd", out_shape=jax.ShapeDtypeStruct((c_all.shape[0], ada_w.shape[1]), f32),
                          compiler_params=pltpu.CompilerParams(vmem_limit_bytes=VMEM_LIMIT))(c_all, ada_w, ada_b)


def ada_bwd(c_all, dmod):
    def body(c_ref, d_ref, o_ref):
        c = c_ref[...]
        act = (c * jax.nn.sigmoid(c)).astype(bf16)
        o_ref[...] = lax.dot_general(act, d_ref[...].astype(bf16), (((0,), (0,)), ((), ())), preferred_element_type=f32)

    return pl.pallas_call(body, name="ada_bwd", out_shape=jax.ShapeDtypeStruct((c_all.shape[1], dmod.shape[1]), f32),
                          compiler_params=pltpu.CompilerParams(vmem_limit_bytes=VMEM_LIMIT))(c_all, dmod)


def loss_head(h1, y2, target, g2, w):
    def fn(t, b, c):
        h, y, tg = [v.astype(f32) for v in t]

        def loss_fn(h, y, g, w):
            e = h + g * _rms(y, w) - tg
            return 0.5 * jnp.sum(jnp.mean(e * e, axis=-1))

        loss, grads = jax.value_and_grad(loss_fn, argnums=(0, 1, 2, 3))(h, y, b[0], c[0])
        return [grads[0], grads[1]], [grads[2], grads[3], jnp.full((1, LANE), loss, f32)]

    return rowcall("loss_head", fn, [(h1, D, 0), (y2, D, 0), (target, D, 0)], [g2], [w], [(D, f32), (D, bf16)],
                   [(1, D), (1, D), (1, LANE)])


def adamw(w, gparts, m, v, name):
    R, C = w.shape
    P = gparts.shape[0]
    budget = 2 * 1024 * 1024
    tr, tc = R, C
    if R * C * 4 > budget and R % 8 == 0:
        tr = max(t for t in range(8, R + 1, 8) if R % t == 0 and t * C * 4 <= budget)
    elif R * C * 4 > budget:
        tc = max(t for t in range(LANE, C + 1, LANE) if C % t == 0 and R * t * 4 <= budget)

    def body(w_ref, g_ref, m_ref, v_ref, go, do, mo, vo):
        g = g_ref[0].astype(f32)
        for p in range(1, P):
            g = g + g_ref[p].astype(f32)
        m2 = B1 * m_ref[...] + (1.0 - B1) * g
        v2 = B2 * v_ref[...] + (1.0 - B2) * jnp.square(g)
        m_hat = m2 * (1.0 / (1.0 - B1 ** STEP))
        v_hat = v2 * (1.0 / (1.0 - B2 ** STEP))
        go[...] = g
        do[...] = -LR * (m_hat / (jnp.sqrt(v_hat) + EPS) + WD * w_ref[...])
        mo[...] = m2
        vo[...] = v2

    blk = pl.BlockSpec((tr, tc), lambda i, j: (i, j))
    return pl.pallas_call(
        body, name=name, grid=(R // tr, C // tc), in_specs=[blk, pl.BlockSpec((P, tr, tc), lambda i, j: (0, i, j)), blk, blk],
        out_specs=[blk] * 4, out_shape=[jax.ShapeDtypeStruct((R, C), f32)] * 4, compiler_params=_cp(("parallel", "parallel")),
    )(w, gparts, m, v)


def _pack_w_in(wt):
    aq, ak, av, dqkv, dz, dbeta, da, ga, gd = jnp.split(wt, np.cumsum(IN_SPLITS)[:-1].tolist(), axis=0)
    ba = jnp.pad(jnp.concatenate([dbeta, da], axis=0), ((0, LANE - 2 * DNH), (0, 0)))
    return jnp.concatenate([ga, gd, aq, dqkv, dz, ak, av, ba], axis=0)


def _unpack_w_in(p):
    row = lambda cb, n: p[cb * LANE: cb * LANE + n]
    ba = row(CB_BA, 2 * DNH)
    return jnp.concatenate([row(CB_AQ, HQ * HD), row(CB_AK, HKV * HD), row(CB_AV, HKV * HD), row(CB_DQKV, 3 * DNH * DND),
                            row(CB_DZ, DNH * DND), ba[:DNH], ba[DNH:], row(CB_GA, D), row(CB_GD, D)], axis=0)


def _cols_gathered(g):
    return g.transpose(1, 0, 2).reshape(g.shape[1], NDEV * g.shape[2])


def _cols_split(w):
    r = w.shape[0]
    return w.reshape(r, NDEV, w.shape[1] // NDEV).transpose(1, 0, 2)


def kernel(x, c, ada_w, ada_b, norm_mix_pre, norm_mix_post, norm_ffn_pre, norm_ffn_post, w_in, dn_conv_w, dn_a_log, dn_dt_bias, dn_norm_w, attn_sinks, rel_bias, w_attn_branch, w_dn_branch, w_out, ffn_w_up, ffn_conv_w, ffn_w_down, loss_target, m_ada_w, m_ada_b, m_norm_mix_pre, m_norm_mix_post, m_norm_ffn_pre, m_norm_ffn_post, m_w_in, m_dn_conv_w, m_dn_a_log, m_dn_dt_bias, m_dn_norm_w, m_attn_sinks, m_rel_bias, m_w_attn_branch, m_w_dn_branch, m_w_out, m_ffn_w_up, m_ffn_conv_w, m_ffn_w_down, v_ada_w, v_ada_b, v_norm_mix_pre, v_norm_mix_post, v_norm_ffn_pre, v_norm_ffn_post, v_w_in, v_dn_conv_w, v_dn_a_log, v_dn_dt_bias, v_dn_norm_w, v_attn_sinks, v_rel_bias, v_w_attn_branch, v_w_dn_branch, v_w_out, v_ffn_w_up, v_ffn_conv_w, v_ffn_w_down):
    B, S, _ = x.shape
    T = B * S
    me = 4 * lax.axis_index("x") + 2 * lax.axis_index("y") + lax.axis_index("c")
    big = dict(w_in=w_in, dn_conv_w=dn_conv_w, w_attn_branch=w_attn_branch, w_dn_branch=w_dn_branch, w_out=w_out,
               ffn_w_up=ffn_w_up, ffn_conv_w=ffn_conv_w, ffn_w_down=ffn_w_down)
    big_names = list(big)

    first, mid, late = ["w_in", "dn_conv_w"], ["w_attn_branch", "w_dn_branch", "w_out"], ["ffn_w_up", "ffn_conv_w"]
    transposed = ("w_in", "ffn_w_up")
    local = lambda n, a: a[0].T if n in transposed else a[0]
    shard = lambda names: [local(n, big[n]).astype(bf16) for n in names]
    *got, c_all = _exchange(shard(first) + [c], "gather_w_in", two_level=True)
    gw = dict(zip(first, got))
    c_all = c_all.reshape(NDEV * B, D)

    wp = _pack_w_in(gw["w_in"].reshape(IN_DIM, D))
    conv_dn = _cols_gathered(gw["dn_conv_w"]).astype(f32)

    ncol = ada_w.shape[2]
    ada_b_mine = lax.dynamic_slice_in_dim(ada_b, me * ncol, ncol, axis=1)
    mod_cols = ada_fwd(c_all, ada_w[0], ada_b_mine)
    (mod_g,) = _exchange([mod_cols], "gather_mod")
    mod = lax.dynamic_slice_in_dim(mod_g, me * B, B, axis=1).transpose(1, 0, 2).reshape(B, NMOD * D)
    sh1, sc1, g1, sh2, sc2, g2 = [mod[:, i * D:(i + 1) * D].reshape(B, 1, D) for i in range(NMOD)]

    onehot = (jnp.asarray(_bucket_table()).reshape(1, -1) == jnp.arange(NBUCK, dtype=jnp.int32)[:, None]).astype(f32)
    bias = mm(rel_bias.T, onehot, "nn", f32, "bias_table", tn=8192, precision=HI).reshape(HQ, WIN, 2 * WIN)
    sinks = attn_sinks.reshape(HQ, 1, 1)
    a_log_pad = jnp.pad(dn_a_log, ((0, 0), (DNH, LANE - 2 * DNH)))
    dt_bias_pad = jnp.pad(dn_dt_bias, ((0, 0), (DNH, LANE - 2 * DNH)))

    (u1,) = rowcall_fwd("mix_pre", f_rms_mod, [(x, D, 0)], [sc1, sh1], [norm_mix_pre], [(D, bf16)])
    proj, gw["ffn_w_down"] = mm(u1.reshape(T, D), wp, "nt", bf16, "proj", tm=512, tn=CB_BA * LANE, b_cols=(0, 1),
                                comm=_Comm(shard(["ffn_w_down"]), two_level=True))
    proj = proj.reshape(B, S, CB_BA * LANE)
    ba = mm(u1.reshape(T, D), wp, "nt", f32, "proj_ba", tn=LANE, b_cols=(CB_BA, 1)).reshape(B, S, LANE)
    ya, *got = attn_fwd(proj, bias, sinks, _Comm(shard(mid), two_level=True))
    gw.update(zip(mid, got))
    wa = _cols_gathered(gw["w_attn_branch"])
    wd = _cols_gathered(gw["w_dn_branch"])
    wo = gw["w_out"].reshape(D, D)
    qkvn = dnconv_fwd(proj, conv_dn)
    (bg,) = rowcall_fwd("dn_gate", f_gate, [(ba, LANE, 0)], [], [a_log_pad, dt_bias_pad], [(LANE, f32)])
    o_dn, states, *got = delta_fwd(qkvn, bg, _Comm(shard(late), two_level=True))
    gw.update(zip(late, got))
    wup = gw["ffn_w_up"].reshape(2 * DFF, D)
    conv_ffn = _cols_gathered(gw["ffn_conv_w"]).astype(f32)
    wdown = gw["ffn_w_down"].reshape(DFF, D)
    (yd,) = rowcall_fwd("dn_out", f_dnout, [(o_dn, DNH * DND, 0), (proj, DNH * DND, CB_DZ // 4)], [], [dn_norm_w], [(DNH * DND, bf16)])
    pa = mm(ya.reshape(T, HQ * HD), wa, "nn", bf16, "attn_branch").reshape(B, S, D)
    pd = mm(yd.reshape(T, DNH * DND), wd, "nn", bf16, "dn_branch").reshape(B, S, D)
    merge_tok = [(proj, D, CB_GA // 8), (proj, D, CB_GD // 8), (pa, D, 0), (pd, D, 0)]
    (merged,) = rowcall_fwd("merge", f_merge, merge_tok, [], [], [(D, bf16)])
    y1 = mm(merged.reshape(T, D), wo, "nn", bf16, "mix_out").reshape(B, S, D)
    post_pre = ([(x, D, 0), (y1, D, 0)], [g1, sc2, sh2], [norm_mix_post, norm_ffn_pre])
    h1, u2 = rowcall_fwd("mix_post_ffn_pre", f_post_pre, *post_pre, [(D, f32), (D, bf16)])
    up = mm(u2.reshape(T, D), wup, "nt", bf16, "ffn_up", tn=2816).reshape(B, S, 2 * DFF)
    act = ffnconv_fwd(up, conv_ffn)
    y2 = mm(act.reshape(T, DFF), wdown, "nn", bf16, "ffn_down", tk=2816).reshape(B, S, D)

    dh1_a, dy2, dg2, dw_ffn_post, loss_b = loss_head(h1, y2, loss_target, g2, norm_ffn_post)
    dy2f = dy2.reshape(T, D)
    dact = mm(dy2f, wdown, "nt", bf16, "ffn_down_dx", tn=2816).reshape(B, S, DFF)
    g_wdown = mm(act.reshape(T, DFF), dy2f, "tn", bf16, "ffn_down_dw", tm=1408, tk=2048)
    in_flight = []

    def send_off(d, tag):
        in_flight.append((d, _scatter_start([a.astype(bf16) for a in d.values()], "scatter_" + tag + "_start")))
        return in_flight[-1][1][-1][0, 0]

    started = send_off(dict(ffn_w_down=g_wdown.reshape(NDEV, DFF // NDEV, D)), "ffn_down")
    dup, g_conv_ffn = ffnconv_bwd(up, conv_ffn + started, dact)
    dupf = dup.reshape(2, T, DFF)
    g_conv_ffn = g_conv_ffn.transpose(1, 0, 2).reshape(FK, 2 * DFF)
    du2 = mm(dupf, wup, "nn", bf16, "ffn_up_dx", tk=2816).reshape(B, S, D)
    g_wup = mm(dupf, u2.reshape(T, D), "tn", bf16, "ffn_up_dw", tm=1408, tk=2048)
    started = send_off(dict(ffn_w_up=g_wup.reshape(NDEV, 2 * DFF // NDEV, D), ffn_conv_w=_cols_split(g_conv_ffn)), "ffn_up")
    post_pre = (post_pre[0], [g1 + started, sc2, sh2], post_pre[2])
    dh1, dy1, dg1, dsc2, dsh2, dw_mix_post, dw_ffn_pre = rowcall_bwd(
        "mix_post_ffn_pre_bwd", f_post_pre, *post_pre, [(dh1_a, D, 0), (du2, D, 0)], [(0, f32), (1, bf16)])
    dy1f = dy1.reshape(T, D)
    dmerged = mm(dy1f, wo, "nt", bf16, "mix_out_dx").reshape(B, S, D)
    g_wo = mm(merged.reshape(T, D), dy1f, "tn", bf16, "mix_out_dw", tk=2048)
    dproj = lax.empty((B, S, NP), bf16)
    dproj, dpa, dpd = rowcall_bwd("merge_bwd", f_merge, merge_tok, [], [], [(dmerged, D, 0)],
                                  [(0, bf16), (1, bf16), (2, bf16), (3, bf16)], join_first=2, into=(dproj, CB_GA // 16))
    dpaf, dpdf = dpa.reshape(T, D), dpd.reshape(T, D)
    dya = mm(dpaf, wa, "nt", bf16, "attn_branch_dx").reshape(B, S, HQ * HD)
    g_wa = mm(ya.reshape(T, HQ * HD), dpaf, "tn", bf16, "attn_branch_dw", tk=2048)
    dyd = mm(dpdf, wd, "nt", bf16, "dn_branch_dx").reshape(B, S, DNH * DND)
    g_wd = mm(yd.reshape(T, DNH * DND), dpdf, "tn", bf16, "dn_branch_dw", tk=2048)
    dproj, do_dn, dw_dn_norm = rowcall_bwd("dn_out_bwd", f_dnout, [(o_dn, DNH * DND, 0), (proj, DNH * DND, CB_DZ // 4)], [], [dn_norm_w],
                                           [(dyd, DNH * DND, 0)], [(1, bf16), (0, f32)], into=(dproj, CB_DZ // 4))
    started = send_off(dict(w_attn_branch=_cols_split(g_wa), w_dn_branch=_cols_split(g_wd), w_out=g_wo.reshape(NDEV, D // NDEV, D)), "branches")
    dqkvn, dbg = delta_bwd(qkvn, bg + started, states, do_dn)
    dproj, da_log_pad, ddt_bias_pad = rowcall_bwd("dn_gate_bwd", f_gate, [(ba, LANE, 0)], [], [a_log_pad, dt_bias_pad],
                                                  [(dbg, LANE, 0)], [(0, bf16)], into=(dproj, CB_BA))
    dproj, g_conv_dn = dnconv_bwd(proj, conv_dn, dqkvn, dproj)
    dproj, dk, dv, dbias, dsinks = attn_bwd(proj, bias, sinks, dya, dproj)
    dproj = lax.dynamic_update_slice(dproj, jnp.concatenate([dk, dv], axis=2), (0, 0, CB_AK * LANE)).reshape(T, NP)
    g_wp = mm(dproj, u1.reshape(T, D), "tn", bf16, "proj_dw", tm=1664, tk=1024)
    started = send_off(dict(w_in=_unpack_w_in(g_wp).reshape(NDEV, IN_DIM // NDEV, D), dn_conv_w=_cols_split(g_conv_dn)), "w_in")
    du1 = mm(dproj, wp, "nn", bf16, "proj_dx", tm=512, tk=NP).reshape(B, S, D)
    grad_x, dsc1, dsh1, dw_mix_pre = rowcall_bwd("mix_pre_bwd", f_rms_mod, [(x, D, 0)], [sc1 + started, sh1], [norm_mix_pre],
                                                 [(du1, D, 0)], [(0, f32)], add=(dh1, D, 0))
    g_rel = mm(dbias.reshape(HQ, WIN * 2 * WIN), onehot, "nt", f32, "rel_bias_dw", tk=8192, precision=HI)

    dmod = jnp.concatenate([dsh1, dsc1, dg1, dsh2, dsc2, dg2], axis=2).reshape(B, NMOD * D)

    zrow = lambda a: jnp.concatenate([a.reshape(1, -1), jnp.zeros((B - 1, a.size), f32)], axis=0)
    small_g = jnp.concatenate([
        dmod, dw_mix_pre.reshape(B, D), dw_mix_post.reshape(B, D), dw_ffn_pre.reshape(B, D), dw_ffn_post.reshape(B, D),
        da_log_pad.reshape(B, LANE)[:, DNH:2 * DNH], ddt_bias_pad.reshape(B, LANE)[:, DNH:2 * DNH], dw_dn_norm.reshape(B, DND),
        zrow(dsinks), zrow(g_rel.T), loss_b.reshape(B, LANE)[:, :1], jnp.zeros((B, SMALL_PAD - SMALL_N - 1), f32)], axis=1)
    (small_all,) = _exchange([small_g], "gather_small")
    dmod_cols = lax.dynamic_slice_in_dim(small_all.reshape(NDEV * B, SMALL_PAD), me * ncol, ncol, axis=1)
    g_ada_w = ada_bwd(c_all, dmod_cols)
    parts = {}
    for i, (d, started) in enumerate(in_flight):
        parts.update(zip(d, _scatter_finish(started, len(d), g_ada_w, "scatter_finish_%d" % i)))
    small_w = dict(ada_b=(ada_b, m_ada_b, v_ada_b), norm_mix_pre=(norm_mix_pre, m_norm_mix_pre, v_norm_mix_pre),
                   norm_mix_post=(norm_mix_post, m_norm_mix_post, v_norm_mix_post), norm_ffn_pre=(norm_ffn_pre, m_norm_ffn_pre, v_norm_ffn_pre),
                   norm_ffn_post=(norm_ffn_post, m_norm_ffn_post, v_norm_ffn_post), dn_a_log=(dn_a_log, m_dn_a_log, v_dn_a_log),
                   dn_dt_bias=(dn_dt_bias, m_dn_dt_bias, v_dn_dt_bias), dn_norm_w=(dn_norm_w, m_dn_norm_w, v_dn_norm_w),
                   attn_sinks=(attn_sinks, m_attn_sinks, v_attn_sinks), rel_bias=(rel_bias, m_rel_bias, v_rel_bias))

    def pack(i, fill):
        row = jnp.concatenate([small_w[n][i].reshape(1, -1) for n, _ in SMALL], axis=1)
        return jnp.pad(row, ((0, 0), (0, SMALL_PAD - SMALL_N)), constant_values=fill)

    small_out = adamw(pack(0, 0.0), small_all.reshape(NDEV * B, 1, SMALL_PAD), pack(1, 0.0), pack(2, 1.0), "adamw_small")
    loss = small_out[0][0, SMALL_N]

    res = {}
    off = 0
    for n, size in SMALL:
        shp = small_w[n][0].shape
        res[n] = [o[:, off:off + size].reshape(shp) for o in small_out]
        off += size
    res["ada_w"] = [o[None] for o in adamw(ada_w[0], g_ada_w[None], m_ada_w[0], v_ada_w[0], "adamw_ada_w")]
    moments = dict(w_in=(m_w_in, v_w_in), dn_conv_w=(m_dn_conv_w, v_dn_conv_w), w_attn_branch=(m_w_attn_branch, v_w_attn_branch),
                   w_dn_branch=(m_w_dn_branch, v_w_dn_branch), w_out=(m_w_out, v_w_out), ffn_w_up=(m_ffn_w_up, v_ffn_w_up),
                   ffn_conv_w=(m_ffn_conv_w, v_ffn_conv_w), ffn_w_down=(m_ffn_w_down, v_ffn_w_down))
    for n in big_names:
        outs = adamw(local(n, big[n]), parts[n], local(n, moments[n][0]), local(n, moments[n][1]), "adamw_" + n)
        res[n] = [(o.T if n in transposed else o)[None] for o in outs]

    order = ["ada_w", "ada_b", "norm_mix_pre", "norm_mix_post", "norm_ffn_pre", "norm_ffn_post", "w_in", "dn_conv_w", "dn_a_log",
             "dn_dt_bias", "dn_norm_w", "attn_sinks", "rel_bias", "w_attn_branch", "w_dn_branch", "w_out", "ffn_w_up", "ffn_conv_w",
             "ffn_w_down"]
    return (loss, grad_x, *[res[n][0] for n in order], *[res[n][1] for n in order], *[res[n][2] for n in order],
            *[res[n][3] for n in order])
```

```python
import functools
import math

import numpy as np
import jax
import jax.numpy as jnp
from jax import lax
from jax.experimental import pallas as pl
from jax.experimental.pallas import tpu as pltpu

f32 = jnp.float32
bf16 = jnp.bfloat16
HI = lax.Precision.HIGHEST
MID = lax.Precision.HIGH
MESH = pl.DeviceIdType.MESH

NDEV = 8
D = 1024
HQ, HKV, HD, WIN, NBUCK, MAXDIST = 8, 2, 64, 128, 32, 128
DNH, DND, DNK, CH = 4, 128, 4, 64
DFF, FK = 2816, 3
NMOD = 6
RMS_EPS = 1e-6
L2_EPS = 1e-6
NEG_INF = -1e30
LR, B1, B2, EPS, WD, STEP = 0.001, 0.9, 0.999, 1e-08, 0.01, 10

LANE = 128
CB_GA, CB_GD, CB_AQ, CB_DQKV, CB_DZ, CB_AK, CB_AV, CB_BA, NPB = 0, 8, 16, 20, 32, 36, 37, 38, 39
NP = NPB * LANE
IN_SPLITS = (HQ * HD, HKV * HD, HKV * HD, 3 * DNH * DND, DNH * DND, DNH, DNH, D, D)
IN_DIM = sum(IN_SPLITS)
VMEM_LIMIT = 56 * 1024 * 1024

SMALL = (("ada_b", NMOD * D), ("norm_mix_pre", D), ("norm_mix_post", D), ("norm_ffn_pre", D), ("norm_ffn_post", D),
         ("dn_a_log", DNH), ("dn_dt_bias", DNH), ("dn_norm_w", DND), ("attn_sinks", HQ), ("rel_bias", NBUCK * HQ))
SMALL_N = sum(n for _, n in SMALL)
SMALL_PAD = 10752


def _cp(sem):
    return pltpu.CompilerParams(dimension_semantics=sem, vmem_limit_bytes=VMEM_LIMIT)


def _pick(dim, target):
    if dim <= target:
        return dim
    best = None
    for d in range(LANE, target + 1, LANE):
        if dim % d == 0:
            best = d
    assert best is not None, (dim, target)
    return best


def _me():
    x, y, c = lax.axis_index("x"), lax.axis_index("y"), lax.axis_index("c")
    return x, y, c, 4 * x + 2 * y + c


def _peer(x, y, c, k):
    px = 1 - x if k & 4 else x
    py = 1 - y if k & 2 else y
    pc = 1 - c if k & 1 else c
    return (px, py, pc), 4 * px + 2 * py + pc


class _Comm:
    def __init__(self, arrs, two_level=False):
        self.arrs, self.n, self.two_level = list(arrs), len(arrs), two_level
        self.out_shape = [jax.ShapeDtypeStruct((NDEV,) + a.shape, a.dtype) for a in arrs]
        nsem = self.n * (NDEV - 1)
        self.scratch = [pltpu.SemaphoreType.DMA((nsem,)), pltpu.SemaphoreType.DMA((nsem,)), pltpu.SemaphoreType.DMA((self.n,))]
        self.specs = [pl.BlockSpec(memory_space=pl.ANY)] * self.n

    def phases(self, ins, out, send, recv, loc):
        x, y, c, me = _me()

        def remote(a, k, src, dst, to):
            s = a * (NDEV - 1) + k - 1
            return pltpu.make_async_remote_copy(src_ref=src, dst_ref=dst, send_sem=send.at[s], recv_sem=recv.at[s],
                                                device_id=to, device_id_type=MESH)

        def local(a):
            return pltpu.make_async_copy(ins[a], out[a].at[me], loc.at[a])

        if not self.two_level:
            def mine(a, k):
                peer, pid = _peer(x, y, c, k)
                return remote(a, k, ins[a], out[a].at[me], peer)

            def theirs(a, k):
                peer, pid = _peer(x, y, c, k)
                return remote(a, k, ins[a], out[a].at[pid], peer)

            def start():
                for a in range(self.n):
                    local(a).start()
                    for k in range(1, NDEV):
                        mine(a, k).start()

            def forward():
                pass

            def finish():
                for a in range(self.n):
                    for k in range(1, NDEV):
                        mine(a, k).wait_send()
                    for k in range(1, NDEV):
                        theirs(a, k).wait_recv()
                    local(a).wait()

            return start, forward, finish

        sibling = (x, y, 1 - c)
        chips = [(1 - x, y), (x, 1 - y), (1 - x, 1 - y)]
        slot = lambda px, py, pc: 4 * px + 2 * py + pc

        def own(a, k, to):
            return remote(a, k, ins[a], out[a].at[me], to)

        def landed(a, k, frm):
            return remote(a, k, ins[a], out[a].at[slot(*frm)], frm)

        def passed(a, j):
            rows = out[a].at[slot(*chips[j], c)]
            return remote(a, 5 + j, rows, rows, sibling)

        def start():
            for a in range(self.n):
                local(a).start()
                own(a, 1, sibling).start()
                for j, chip in enumerate(chips):
                    own(a, 2 + j, (*chip, c)).start()

        def forward():
            for a in range(self.n):
                for j, chip in enumerate(chips):
                    landed(a, 2 + j, (*chip, c)).wait_recv()
                    passed(a, j).start()

        def finish():
            for a in range(self.n):
                landed(a, 1, sibling).wait_recv()
                for j, chip in enumerate(chips):
                    remote(a, 5 + j, ins[a], out[a].at[slot(*chip, 1 - c)], sibling).wait_recv()
                own(a, 1, sibling).wait_send()
                for j, chip in enumerate(chips):
                    own(a, 2 + j, (*chip, c)).wait_send()
                    passed(a, j).wait_send()
                local(a).wait()

        return start, forward, finish


def _copy_start(arrs, name, gather=False, after=None):
    n = len(arrs)
    order = [] if after is None else [after]
    n_in = 2 * n + len(order)
    block = (lambda ref, j: ref) if gather else (lambda ref, j: ref.at[j])

    def body(*refs):
        ins, lands, send, recv, own, token = refs[:n], refs[n:2 * n], refs[n_in], refs[n_in + 1], refs[n_in + 2], refs[-1]
        x, y, c, me = _me()
        for a in range(n):
            pltpu.make_async_copy(block(ins[a], me), lands[a].at[me], own.at[a]).start()
            for k in range(1, NDEV):
                peer, pid = _peer(x, y, c, k)
                s = a * (NDEV - 1) + k - 1
                pltpu.make_async_remote_copy(src_ref=block(ins[a], pid), dst_ref=lands[a].at[me], send_sem=send.at[s],
                                             recv_sem=recv.at[s], device_id=peer, device_id_type=MESH).start()
        token[...] = jnp.zeros(token.shape, token.dtype)

    hbm, sem = pl.BlockSpec(memory_space=pltpu.HBM), pl.BlockSpec(memory_space=pltpu.SEMAPHORE)
    nsem = n * (NDEV - 1)
    land_shapes = [((NDEV,) + a.shape if gather else a.shape) for a in arrs]
    thru = [pltpu.HBM(a.shape, a.dtype) for a in arrs] + [pltpu.HBM(shp, a.dtype) for shp, a in zip(land_shapes, arrs)]
    return pl.pallas_call(
        body, name=name, in_specs=[hbm] * (2 * n) + [pl.BlockSpec(memory_space=pl.ANY)] * len(order),
        out_shape=(pltpu.SemaphoreType.DMA((nsem,)), pltpu.SemaphoreType.DMA((nsem,)), pltpu.SemaphoreType.DMA((n,)), *thru,
                   jax.ShapeDtypeStruct((8, LANE), f32)),
        out_specs=(sem, sem, sem, *[hbm] * (2 * n), pl.BlockSpec(memory_space=pltpu.VMEM)),
        input_output_aliases={i: 3 + i for i in range(2 * n)},
        compiler_params=pltpu.CompilerParams(has_side_effects=pltpu.SideEffectType.DATAFLOW_SIDE_EFFECTING),
    )(*[pltpu.with_memory_space_constraint(a, pltpu.HBM) for a in arrs],
      *[pltpu.with_memory_space_constraint(lax.empty(shp, a.dtype), pltpu.HBM) for shp, a in zip(land_shapes, arrs)], *order)


def _copy_finish(started, n, after, name, gather=False):
    send, recv, own, *rest = started
    srcs, lands = rest[:n], rest[n:2 * n]
    block = (lambda ref, j: ref) if gather else (lambda ref, j: ref.at[j])

    def body(*refs):
        ins, lnd, send_ref, recv_ref, own_ref = refs[:n], refs[n:2 * n], refs[2 * n], refs[2 * n + 1], refs[2 * n + 2]
        x, y, c, me = _me()
        for a in range(n):
            pltpu.make_async_copy(block(ins[a], me), lnd[a].at[me], own_ref.at[a]).wait()
            for k in range(1, NDEV):
                peer, pid = _peer(x, y, c, k)
                s = a * (NDEV - 1) + k - 1
                cp = pltpu.make_async_remote_copy(src_ref=block(ins[a], pid), dst_ref=lnd[a].at[pid], send_sem=send_ref.at[s],
                                                  recv_sem=recv_ref.at[s], device_id=peer, device_id_type=MESH)
                cp.wait_send()
                cp.wait_recv()

    hbm, sem = pl.BlockSpec(memory_space=pltpu.HBM), pl.BlockSpec(memory_space=pltpu.SEMAPHORE)
    thru = [pltpu.HBM(a.shape, a.dtype) for a in srcs] + [pltpu.HBM(a.shape, a.dtype) for a in lands]
    out = pl.pallas_call(
        body, name=name, in_specs=[hbm] * (2 * n) + [sem, sem, sem, pl.BlockSpec(memory_space=pl.ANY)],
        out_shape=tuple(thru), out_specs=tuple([hbm] * (2 * n)), input_output_aliases={i: i for i in range(2 * n)},
        compiler_params=pltpu.CompilerParams(has_side_effects=pltpu.SideEffectType.DATAFLOW_SIDE_EFFECTING),
    )(*srcs, *lands, send, recv, own, after)
    return list(out[n:])


def _exchange(arrs, name, two_level=False):
    comm = _Comm(arrs, two_level)

    def body(*refs):
        start, forward, finish = comm.phases(refs[:comm.n], refs[comm.n:2 * comm.n], *refs[2 * comm.n:])
        start()
        forward()
        finish()

    return pl.pallas_call(body, name=name, out_shape=comm.out_shape, in_specs=comm.specs, out_specs=comm.specs,
                          scratch_shapes=comm.scratch, compiler_params=pltpu.CompilerParams(has_side_effects=True))(*arrs)


def mm(a, b, mode, out_dtype, name, tm=1024, tn=1024, tk=1024, precision=None, b_cols=None):
    a_parts = a.shape[0] if a.ndim == 3 else 1
    b_parts = b.shape[0] if b.ndim == 3 else 1
    assert b_parts == 1 or mode == "tn"
    ash, bsh = (a.shape[-2], a.shape[-1] * a_parts), b.shape[-2:]
    if mode == "nn":
        (M, K), (K2, N) = ash, bsh
    elif mode == "nt":
        (M, K), (N, K2) = ash, bsh
    else:
        (K, M), (K2, N) = ash, (bsh[0], bsh[1] * b_parts)
    assert K == K2, (name, a.shape, b.shape)
    col0 = 0
    if b_cols is not None:
        assert mode in ("nn", "nt") and tn % LANE == 0
        col0, N = b_cols[0], b_cols[1] * tn
    if mode == "tn":
        tm, tn, tk = _pick(M // a_parts, tm), _pick(N // b_parts, tn), _pick(K, tk)
    else:
        tm, tn, tk = _pick(M, tm), _pick(N // b_parts, tn), _pick(K // a_parts, tk)
    nk = K // tk
    if mode == "tn" and a_parts > 1:
        per = M // tm // a_parts
        a_spec = pl.BlockSpec((None, tk, tm), lambda i, j, k: (i // per, k, i % per))
    elif mode == "tn":
        a_spec = pl.BlockSpec((tk, tm), lambda i, j, k: (k, i))
    elif a_parts > 1:
        per = nk // a_parts
        a_spec = pl.BlockSpec((None, tm, tk), lambda i, j, k: (k // per, i, k % per))
    else:
        a_spec = pl.BlockSpec((tm, tk), lambda i, j, k: (i, k))
    if mode == "nt":
        b_spec = pl.BlockSpec((tn, tk), lambda i, j, k: (col0 + j, k))
    elif b_parts > 1:
        per = N // tn // b_parts
        b_spec = pl.BlockSpec((None, tk, tn), lambda i, j, k: (j // per, k, j % per))
    else:
        b_spec = pl.BlockSpec((tk, tn), lambda i, j, k: (k, col0 + j))
    dims = {"nn": ((1,), (0,)), "nt": ((1,), (1,)), "tn": ((0,), (0,))}[mode]

    def body(a_ref, b_ref, o_ref, *scr):
        p = lax.dot_general(a_ref[...], b_ref[...], (dims, ((), ())), preferred_element_type=f32, precision=precision)
        if nk == 1:
            o_ref[...] = p.astype(o_ref.dtype)
        else:
            acc = scr[0]
            k = pl.program_id(2)

            @pl.when(k == 0)
            def _():
                acc[...] = p

            @pl.when(k > 0)
            def _():
                acc[...] += p

            @pl.when(k == nk - 1)
            def _():
                o_ref[...] = acc[...].astype(o_ref.dtype)

    return pl.pallas_call(
        body, name=name, grid=(M // tm, N // tn, nk), in_specs=[a_spec, b_spec],
        out_specs=pl.BlockSpec((tm, tn), lambda i, j, k: (i, j)), out_shape=jax.ShapeDtypeStruct((M, N), out_dtype),
        scratch_shapes=[pltpu.VMEM((tm, tn), f32)] if nk > 1 else [],
        compiler_params=_cp(("parallel", "parallel", "arbitrary")),
    )(a, b)


ROW_TILE = 512


def rowcall(name, fn, tok, bat, con, tok_out, acc_out, ts=ROW_TILE, into=None):
    B, S = tok[0][0].shape[:2]
    ts = min(ts, S)
    nt, nb, nc, no, na = len(tok), len(bat), len(con), len(tok_out), len(acc_out)
    nin = nt + nb + nc + (1 if into is not None else 0)

    def body(*refs):
        tr, br, cr = refs[:nt], refs[nt:nt + nb], refs[nt + nb:nt + nb + nc]
        orf, arf = refs[nin:nin + no], refs[nin + no:]
        touts, aouts = fn([r[0] for r in tr], [r[0] for r in br], [r[...] for r in cr])
        for r, v in zip(orf, touts):
            r[0] = v.astype(r.dtype)
        s = pl.program_id(1)
        for r, v in zip(arf, aouts):
            @pl.when(s == 0)
            def _(r=r):
                r[...] = jnp.zeros(r.shape, r.dtype)
            r[0] += v.astype(f32)

    in_specs = [pl.BlockSpec((1, ts, w), lambda b, s, cb=cb: (b, s, cb)) for (_, w, cb) in tok]
    in_specs += [pl.BlockSpec((1,) + a.shape[1:], lambda b, s: (b, 0, 0)) for a in bat]
    in_specs += [pl.BlockSpec(a.shape, lambda b, s, nd=a.ndim: (0,) * nd) for a in con]
    out_specs = [pl.BlockSpec((1, ts, w), lambda b, s: (b, s, 0)) for (w, _) in tok_out]
    out_specs += [pl.BlockSpec((1,) + shp, lambda b, s, nd=len(shp): (b,) + (0,) * nd) for shp in acc_out]
    out_shape = [jax.ShapeDtypeStruct((B, S, w), dt) for (w, dt) in tok_out]
    out_shape += [jax.ShapeDtypeStruct((B,) + shp, f32) for shp in acc_out]
    extra, aliases = [], {}
    if into is not None:
        buf, cb = into
        assert buf.dtype == tok_out[0][1]
        in_specs.append(pl.BlockSpec(memory_space=pl.ANY))
        out_specs[0] = pl.BlockSpec((1, ts, tok_out[0][0]), lambda b, s: (b, s, cb))
        out_shape[0] = jax.ShapeDtypeStruct(buf.shape, buf.dtype)
        extra, aliases = [buf], {nin - 1: 0}
    return pl.pallas_call(
        body, name=name, grid=(B, S // ts), in_specs=in_specs, out_specs=out_specs, out_shape=out_shape,
        input_output_aliases=aliases, compiler_params=_cp(("parallel", "arbitrary")),
    )(*[t[0] for t in tok], *bat, *con, *extra)


def rowcall_fwd(name, f, tok, bat, con, tok_out, ts=ROW_TILE):
    def fn(t, b, c):
        return f([v.astype(f32) for v in t], b, c), []
    return rowcall(name, fn, tok, bat, con, tok_out, [], ts)


def rowcall_bwd(name, f, tok, bat, con, cts, tok_grads, add=None, ts=ROW_TILE, join_first=1, into=None):
    nt, ncts = len(tok), len(cts)

    def fn(t, b, c):
        prim = [v.astype(f32) for v in t[:nt]]
        ct = [v.astype(f32) for v in t[nt:nt + ncts]]
        _, vjp = jax.vjp(lambda tt, bb, cc: f(tt, bb, cc), prim, b, c)
        dt, db, dc = vjp(ct)
        touts = [dt[i] for i, _ in tok_grads]
        if add is not None:
            touts[0] = touts[0] + t[nt + ncts].astype(f32)
        if join_first > 1:
            touts = [jnp.concatenate(touts[:join_first], axis=1)] + touts[join_first:]
        return touts, list(db) + list(dc)

    all_tok = list(tok) + list(cts) + ([add] if add is not None else [])
    tok_out = [(tok[i][1], dt) for i, dt in tok_grads]
    if join_first > 1:
        tok_out = [(sum(w for w, _ in tok_out[:join_first]), tok_out[0][1])] + tok_out[join_first:]
    acc_out = [tuple(a.shape[1:]) for a in bat] + [tuple(a.shape) for a in con]
    return rowcall(name, fn, all_tok, bat, con, tok_out, acc_out, ts, into)


def _rms(y, w):
    return y * lax.rsqrt(jnp.mean(y * y, axis=-1, keepdims=True) + RMS_EPS) * w


def f_rms_mod(t, b, c):
    return [_rms(t[0], c[0]) * (1.0 + b[0]) + b[1]]


def f_post_pre(t, b, c):
    h1 = t[0] + b[0] * _rms(t[1], c[0])
    return [h1, _rms(h1, c[1]) * (1.0 + b[1]) + b[2]]


def f_merge(t, b, c):
    ga, gd, ya, yd = t
    return [jax.nn.sigmoid(ga) * ya + jax.nn.sigmoid(gd) * yd]


def f_dnout(t, b, c):
    o, z = t
    outs = []
    for h in range(DNH):
        sl = slice(h * DND, (h + 1) * DND)
        zh = z[:, sl]
        outs.append(_rms(o[:, sl], c[0]) * (zh * jax.nn.sigmoid(zh)))
    return [jnp.concatenate(outs, axis=1)]


def _softplus(x):
    return jnp.maximum(x, 0.0) + jnp.log(1.0 + jnp.exp(-jnp.abs(x)))


def f_gate(t, b, c):
    ba = t[0]
    a_log, dt_bias = c
    lane = lax.broadcasted_iota(jnp.int32, ba.shape, 1)
    beta = jax.nn.sigmoid(ba)
    g = -jnp.exp(a_log) * _softplus(ba + dt_bias)
    return [jnp.where(lane < DNH, beta, jnp.where(lane < 2 * DNH, g, 0.0))]


def _bucket_table():
    qi = np.arange(WIN)[:, None]
    kj = np.arange(2 * WIN)[None, :]
    dist = np.maximum(WIN + qi - kj, 0)
    max_exact = NBUCK // 2
    scaled = np.log(np.maximum(dist, 1).astype(np.float64) / max_exact) / math.log(MAXDIST / max_exact)
    large = np.minimum(max_exact + (scaled * (NBUCK - max_exact)).astype(np.int32), NBUCK - 1)
    return np.where(dist < max_exact, dist, large).astype(np.int32)


def _attn_mask(n):
    qi = lax.broadcasted_iota(jnp.int32, (WIN, 2 * WIN), 0)
    kj = lax.broadcasted_iota(jnp.int32, (WIN, 2 * WIN), 1)
    dist = WIN + qi - kj
    return (dist >= 0) & (dist < WIN) & ((kj >= WIN) | (n > 0))


def _swap_halves(x):
    return pltpu.roll(x, HD, axis=x.ndim - 1)


@jax.custom_vjp
def _swap_halves_vjp(x):
    return _swap_halves(x)


_swap_halves_vjp.defvjp(lambda x: (_swap_halves(x), None), lambda _, g: (_swap_halves(g),))


def _attn_block(q, kp, kc, vp, vc, bias, sinks, mask, differentiated):
    dot = _bdot_bf16_vjp if differentiated else _bdot_bf16
    swap = _swap_halves_vjp if differentiated else _swap_halves
    B, grp = q.shape[0], HQ // HKV
    upper = lax.broadcasted_iota(jnp.int32, (2 * WIN, LANE), 1) >= HD

    def placed(natural, swapped, j, half):
        keep = upper if half == 1 else ~upper
        return jnp.where(keep, natural if j == half else swapped, 0.0)

    qh, ks, vs = [], [], []
    for b in range(B):
        kb, vb = jnp.concatenate([kp[b], kc[b]], axis=0), jnp.concatenate([vp[b], vc[b]], axis=0)
        kb_sw, vb_sw = swap(kb), swap(vb)
        for h in range(HQ):
            qh.append(q[b, :, (h // 2) * LANE:(h // 2 + 1) * LANE])
            ks.append(placed(kb, kb_sw, h // grp, h % 2))
            vs.append(placed(vb, vb_sw, h // grp, h % 2))
    s = dot(_stack(qh), _stack(ks), 2, 2).reshape(B, HQ, WIN, 2 * WIN) * (HD ** -0.5)
    s = jnp.where(mask, s + bias, NEG_INF)
    m = jnp.maximum(jnp.max(s, axis=-1, keepdims=True), sinks)
    p = jnp.exp(s - m)
    probs = p / (jnp.sum(p, axis=-1, keepdims=True) + jnp.exp(sinks - m))
    o = dot(probs.reshape(B * HQ, WIN, 2 * WIN), _stack(vs), 2, 1)
    return _stack([jnp.concatenate([o[b * HQ + 2 * i] + o[b * HQ + 2 * i + 1] for i in range(HQ // 2)], axis=1) for b in range(B)])


def _attn_specs(B, NB):
    last = NB - 1
    return [
        pl.BlockSpec((B, WIN, HQ * HD), lambda n: (0, jnp.minimum(n, last), CB_AQ // 4)),
        pl.BlockSpec((B, WIN, LANE), lambda n: (0, jnp.clip(n - 1, 0, last), CB_AK)),
        pl.BlockSpec((B, WIN, LANE), lambda n: (0, jnp.minimum(n, last), CB_AK)),
        pl.BlockSpec((B, WIN, LANE), lambda n: (0, jnp.clip(n - 1, 0, last), CB_AV)),
        pl.BlockSpec((B, WIN, LANE), lambda n: (0, jnp.minimum(n, last), CB_AV)),
        pl.BlockSpec((HQ, WIN, 2 * WIN), lambda n: (0, 0, 0)),
        pl.BlockSpec((HQ, 1, 1), lambda n: (0, 0, 0)),
    ]


def attn_fwd(proj, bias, sinks):
    B, S, _ = proj.shape
    NB = S // WIN

    def body(q, kp, kc, vp, vc, bias_ref, sink_ref, o_ref):
        mask = _attn_mask(pl.program_id(0))
        o = _attn_block(*[r[...].astype(f32) for r in (q, kp, kc, vp, vc)], bias_ref[...], sink_ref[...], mask, False)
        o_ref[...] = o.astype(o_ref.dtype)

    return pl.pallas_call(
        body, name="attn_fwd", grid=(NB,), in_specs=_attn_specs(B, NB),
        out_specs=pl.BlockSpec((B, WIN, HQ * HD), lambda n: (0, n, 0)), out_shape=jax.ShapeDtypeStruct((B, S, HQ * HD), bf16),
        compiler_params=_cp(("parallel",)),
    )(proj, proj, proj, proj, proj, bias, sinks)


def attn_bwd(proj, bias, sinks, dy, dproj):
    B, S, _ = proj.shape
    NB = S // WIN
    last = NB - 1

    def body(q, kp, kc, vp, vc, bias_ref, sink_ref, dy_ref, _, dq_ref, dk_ref, dv_ref, dbias_ref, dsink_ref, kcar, vcar):
        n = pl.program_id(0)

        @pl.when(n == 0)
        def _():
            dbias_ref[...] = jnp.zeros(dbias_ref.shape, f32)
            dsink_ref[...] = jnp.zeros(dsink_ref.shape, f32)
            kcar[...] = jnp.zeros(kcar.shape, f32)
            vcar[...] = jnp.zeros(vcar.shape, f32)

        @pl.when(n < NB)
        def _():
            mask = _attn_mask(n)
            _, vjp = jax.vjp(lambda *a: _attn_block(*a, mask, True), *[r[...].astype(f32) for r in (q, kp, kc, vp, vc)],
                             bias_ref[...], sink_ref[...])
            dq, dkp, dkc, dvp, dvc, dbias, dsink = vjp(dy_ref[...].astype(f32))
            dq_ref[...] = dq.astype(dq_ref.dtype)
            dbias_ref[...] += dbias
            dsink_ref[...] += dsink
            dk_ref[...] = (kcar[...] + dkp).astype(dk_ref.dtype)
            dv_ref[...] = (vcar[...] + dvp).astype(dv_ref.dtype)
            kcar[...] = dkc
            vcar[...] = dvc

        @pl.when(n == NB)
        def _():
            dk_ref[...] = kcar[...].astype(dk_ref.dtype)
            dv_ref[...] = vcar[...].astype(dv_ref.dtype)

    in_specs = _attn_specs(B, NB) + [pl.BlockSpec((B, WIN, HQ * HD), lambda n: (0, jnp.minimum(n, last), 0)),
                                     pl.BlockSpec(memory_space=pl.ANY)]
    kv_out = pl.BlockSpec((B, WIN, LANE), lambda n: (0, jnp.maximum(n - 1, 0), 0))
    return pl.pallas_call(
        body, name="attn_bwd", grid=(NB + 1,), in_specs=in_specs, input_output_aliases={8: 0},
        out_specs=[pl.BlockSpec((B, WIN, HQ * HD), lambda n: (0, jnp.minimum(n, last), CB_AQ // 4)), kv_out, kv_out,
                   pl.BlockSpec((HQ, WIN, 2 * WIN), lambda n: (0, 0, 0)), pl.BlockSpec((HQ, 1, 1), lambda n: (0, 0, 0))],
        out_shape=[jax.ShapeDtypeStruct(dproj.shape, dproj.dtype), jax.ShapeDtypeStruct((B, S, LANE), bf16),
                   jax.ShapeDtypeStruct((B, S, LANE), bf16), jax.ShapeDtypeStruct((HQ, WIN, 2 * WIN), f32),
                   jax.ShapeDtypeStruct((HQ, 1, 1), f32)],
        scratch_shapes=[pltpu.VMEM((B, WIN, LANE), f32), pltpu.VMEM((B, WIN, LANE), f32)],
        compiler_params=_cp(("arbitrary",)),
    )(proj, proj, proj, proj, proj, bias, sinks, dy, dproj)


DN_ROWS, FFN_ROWS = 256, 32


def _stage_rows(dst, value):
    dst[0:8] = jnp.zeros((8, LANE), f32)
    dst[8:8 + value.shape[0]] = value


def _conv_rows(xs, w, width, r, rows):
    wins = [xs[pl.ds(r + 8 - (width - 1) + j, rows), :] for j in range(width)]
    out = w[0:1] * wins[0]
    for j in range(1, width):
        out = out + w[j:j + 1] * wins[j]
    return out, wins


def _fold8(v):
    return jnp.sum(v.reshape(v.shape[0] // 8, 8, LANE), axis=0)


def _conv_rows_t(ds, w, width, r, rows):
    out = w[0:1] * ds[pl.ds(r + width - 1, rows), :]
    for j in range(1, width):
        out = out + w[j:j + 1] * ds[pl.ds(r + width - 1 - j, rows), :]
    return out


def _dn_outblk(i):
    return (i % DNH) * 3 + i // DNH


def _dn_act(c, isqk):
    sg = jax.nn.sigmoid(c)
    y = c * sg
    n = lax.rsqrt(jnp.sum(y * y, axis=-1, keepdims=True) + L2_EPS)
    return jnp.where(isqk, y * n, y), sg, n


def dnconv_fwd(proj, conv_w):
    B, S, _ = proj.shape
    rows = min(DN_ROWS, S)

    def body(x_ref, w_ref, o_ref, xs):
        isqk = pl.program_id(0) < 2 * DNH
        _stage_rows(xs, x_ref[0].astype(f32))
        w = w_ref[...]
        for r in range(0, S, rows):
            c, _ = _conv_rows(xs, w, DNK, r, rows)
            o_ref[0, pl.ds(r, rows), :] = _dn_act(c, isqk)[0]

    return pl.pallas_call(
        body, name="dnconv_fwd", grid=(3 * DNH, B),
        in_specs=[pl.BlockSpec((1, S, LANE), lambda i, b: (b, 0, CB_DQKV + i)), pl.BlockSpec((DNK, LANE), lambda i, b: (0, i))],
        out_specs=pl.BlockSpec((1, S, LANE), lambda i, b: (b, 0, _dn_outblk(i))),
        out_shape=jax.ShapeDtypeStruct((B, S, 3 * DNH * DND), f32), scratch_shapes=[pltpu.VMEM((S + 8, LANE), f32)],
        compiler_params=_cp(("parallel", "parallel")),
    )(proj, conv_w)


def dnconv_bwd(proj, conv_w, dqkvn, dproj):
    B, S, _ = proj.shape
    rows = min(DN_ROWS, S)

    def body(x_ref, w_ref, dy_ref, _, dx_ref, dw_ref, xs, ds):
        isqk = pl.program_id(0) < 2 * DNH
        _stage_rows(xs, x_ref[0].astype(f32))
        w = w_ref[...]
        dw = [jnp.zeros((8, LANE), f32) for _ in range(DNK)]
        for r in range(0, S, rows):
            c, wins = _conv_rows(xs, w, DNK, r, rows)
            out, sg, n = _dn_act(c, isqk)
            dout = dy_ref[0, pl.ds(r, rows), :]
            dy = jnp.where(isqk, n * (dout - out * jnp.sum(dout * out, axis=-1, keepdims=True)), dout)
            dc = dy * (sg * (1.0 + c * (1.0 - sg)))
            ds[pl.ds(r, rows), :] = dc
            for j in range(DNK):
                dw[j] = dw[j] + _fold8(dc * wins[j])
        ds[S:S + 8] = jnp.zeros((8, LANE), f32)
        for r in range(0, S, rows):
            dx_ref[0, pl.ds(r, rows), :] = _conv_rows_t(ds, w, DNK, r, rows).astype(dx_ref.dtype)

        @pl.when(pl.program_id(1) == 0)
        def _():
            dw_ref[...] = jnp.zeros(dw_ref.shape, f32)
        dw_ref[...] += jnp.concatenate([jnp.sum(d, axis=0, keepdims=True) for d in dw], axis=0)

    return pl.pallas_call(
        body, name="dnconv_bwd", grid=(3 * DNH, B),
        in_specs=[pl.BlockSpec((1, S, LANE), lambda i, b: (b, 0, CB_DQKV + i)), pl.BlockSpec((DNK, LANE), lambda i, b: (0, i)),
                  pl.BlockSpec((1, S, LANE), lambda i, b: (b, 0, _dn_outblk(i))), pl.BlockSpec(memory_space=pl.ANY)],
        out_specs=[pl.BlockSpec((1, S, LANE), lambda i, b: (b, 0, CB_DQKV + i)), pl.BlockSpec((DNK, LANE), lambda i, b: (0, i))],
        out_shape=[jax.ShapeDtypeStruct(dproj.shape, dproj.dtype), jax.ShapeDtypeStruct((DNK, 3 * DNH * DND), f32)],
        scratch_shapes=[pltpu.VMEM((S + 8, LANE), f32), pltpu.VMEM((S + 8, LANE), f32)],
        input_output_aliases={3: 0}, compiler_params=_cp(("parallel", "arbitrary")),
    )(proj, conv_w, dqkvn, dproj)


def _bdot(a, b, ca, cb, precision=HI):
    return lax.dot_general(a, b, (((ca,), (cb,)), ((0,), (0,))), preferred_element_type=f32, precision=precision)


def _bdot_bf16(a, b, ca, cb):
    return _bdot(a.astype(bf16), b.astype(bf16), ca, cb, None)


@functools.partial(jax.custom_vjp, nondiff_argnums=(2, 3))
def _bdot_bf16_vjp(a, b, ca, cb):
    return _bdot_bf16(a, b, ca, cb)


def _bdot_bf16_fwd(a, b, ca, cb):
    return _bdot_bf16(a, b, ca, cb), (a, b)


def _bdot_bf16_bwd(ca, cb, res, g):
    a, b = res
    fa, fb = 3 - ca, 3 - cb
    da = _bdot_bf16(g, b, 2, fb) if ca == 2 else _bdot_bf16(b, g, fb, 2)
    db = _bdot_bf16(a, g, fa, 1) if cb == 1 else _bdot_bf16(g, a, 1, fa)
    return da, db


_bdot_bf16_vjp.defvjp(_bdot_bf16_fwd, _bdot_bf16_bwd)


def _neumann_inverse(low):
    n = low.shape[-1]
    eye = (lax.broadcasted_iota(jnp.int32, (n, n), 0) == lax.broadcasted_iota(jnp.int32, (n, n), 1)).astype(f32)
    p = -low
    x = eye[None] + p
    for _ in range(5):
        p = _bdot_bf16(p, p, 2, 1)
        x = x + _bdot_bf16(x, p, 2, 1)
    return x


@jax.custom_vjp
def _unit_lower_inverse(low):
    return _neumann_inverse(low)


def _uli_fwd(low):
    t = _neumann_inverse(low)
    return t, t


def _uli_bwd(t, dt):
    return (-_bdot_bf16(_bdot_bf16(t, dt, 1, 1), t, 2, 2),)


_unit_lower_inverse.defvjp(_uli_fwd, _uli_bwd)


def _stack(xs):
    return jnp.concatenate([x[None] for x in xs], axis=0)


DELTA_CHUNKS = 2


def _delta_chunks(qkv, bg, state, differentiated):
    inverse = _unit_lower_inverse if differentiated else _neumann_inverse
    lo = _bdot_bf16_vjp if differentiated else _bdot_bf16
    B, n = qkv.shape[0], qkv.shape[1] // CH
    G = B * DNH
    N = n * G
    triples = [(i, b, h) for i in range(n) for b in range(B) for h in range(DNH)]
    col = lambda i, b, h, kind: qkv[b, i * CH:(i + 1) * CH, (3 * h + kind) * DND:(3 * h + kind + 1) * DND]
    q, k, v = [_stack([col(i, b, h, kind) for i, b, h in triples]) for kind in range(3)]
    lane = lax.broadcasted_iota(jnp.int32, (CH, LANE), 1)
    pick = lambda i, b, l: jnp.sum(jnp.where(lane == l, bg[b, i * CH:(i + 1) * CH], 0.0), axis=1, keepdims=True)
    beta = _stack([pick(i, b, h) for i, b, h in triples])
    g = _stack([pick(i, b, h + DNH) for i, b, h in triples])
    ri = lax.broadcasted_iota(jnp.int32, (CH, CH), 0)
    ci = lax.broadcasted_iota(jnp.int32, (CH, CH), 1)
    incl, strict = (ri >= ci)[None], (ri > ci)[None]
    gc = _bdot(jnp.broadcast_to(incl.astype(f32), (N, CH, CH)), jnp.broadcast_to(g, (N, CH, LANE)), 2, 1, MID)
    e0 = jnp.broadcast_to((lane == 0).astype(f32)[None], (N, CH, LANE))
    gc_row = _bdot(e0, gc, 2, 2, MID)
    diff = gc[:, :, :CH] - gc_row
    decay = jnp.where(incl, jnp.exp(jnp.where(incl, diff, 0.0)), 0.0)
    qs = q * (DND ** -0.5)
    kb, vb = k * beta, v * beta
    eg = jnp.exp(gc)
    with_k = lo(jnp.concatenate([kb, qs], axis=1), k, 2, 2)
    low = jnp.where(strict, with_k[:, :CH] * decay, 0.0)
    intra = jnp.where(incl, with_k[:, CH:] * decay, 0.0)
    tinv = inverse(low)
    solved = lo(tinv, jnp.concatenate([vb, kb * eg], axis=2), 2, 1)
    gl = gc[:, CH - 1:CH, :]
    k_tail = k * jnp.exp(gl - gc)
    to_state = jnp.concatenate([solved[:, :, DND:], qs * eg], axis=1)
    decay_all = jnp.exp(gl)
    outs = []
    for i in range(n):
        sl = slice(i * G, (i + 1) * G)
        with_state = lo(to_state[sl], state, 2, 1)
        v_new = solved[sl, :, :DND] - with_state[:, :CH]
        outs.append(with_state[:, CH:] + lo(intra[sl], v_new, 2, 1))
        state = state * decay_all[sl] + lo(k_tail[sl], v_new, 1, 1)
    return outs, state


def delta_fwd(qkvn, bg):
    B, S, _ = qkvn.shape
    n = DELTA_CHUNKS if (S // CH) % DELTA_CHUNKS == 0 else 1
    steps, G, rows = S // (n * CH), B * DNH, n * CH

    def body(qkv_ref, bg_ref, o_ref, st_ref, state):
        @pl.when(pl.program_id(0) == 0)
        def _():
            state[...] = jnp.zeros(state.shape, f32)
        s0 = state[...]
        st_ref[0] = s0
        outs, s1 = _delta_chunks(qkv_ref[...], bg_ref[...], s0, False)
        for i, o in enumerate(outs):
            for b in range(B):
                for h in range(DNH):
                    o_ref[b, i * CH:(i + 1) * CH, h * DND:(h + 1) * DND] = o[b * DNH + h]
        state[...] = s1

    return pl.pallas_call(
        body, name="delta_fwd", grid=(steps,),
        in_specs=[pl.BlockSpec((B, rows, 3 * DNH * DND), lambda c: (0, c, 0)), pl.BlockSpec((B, rows, LANE), lambda c: (0, c, 0))],
        out_specs=[pl.BlockSpec((B, rows, DNH * DND), lambda c: (0, c, 0)), pl.BlockSpec((1, G, DND, DND), lambda c: (c, 0, 0, 0))],
        out_shape=[jax.ShapeDtypeStruct((B, S, DNH * DND), f32), jax.ShapeDtypeStruct((steps, G, DND, DND), f32)],
        scratch_shapes=[pltpu.VMEM((G, DND, DND), f32)], compiler_params=_cp(("arbitrary",)),
    )(qkvn, bg)


def delta_bwd(qkvn, bg, states, do):
    B, S, _ = qkvn.shape
    steps, G = states.shape[0], B * DNH
    rows = S // steps
    n = rows // CH

    def body(qkv_ref, bg_ref, st_ref, do_ref, dqkv_ref, dbg_ref, dstate):
        @pl.when(pl.program_id(0) == 0)
        def _():
            dstate[...] = jnp.zeros(dstate.shape, f32)
        _, vjp = jax.vjp(lambda a, g, s: _delta_chunks(a, g, s, True), qkv_ref[...], bg_ref[...], st_ref[0])
        do = [_stack([do_ref[b, i * CH:(i + 1) * CH, h * DND:(h + 1) * DND] for b in range(B) for h in range(DNH)]) for i in range(n)]
        dqkv, dbg, ds = vjp((do, dstate[...]))
        dqkv_ref[...] = dqkv
        dbg_ref[...] = dbg
        dstate[...] = ds

    rev = lambda c: steps - 1 - c
    return pl.pallas_call(
        body, name="delta_bwd", grid=(steps,),
        in_specs=[pl.BlockSpec((B, rows, 3 * DNH * DND), lambda c: (0, rev(c), 0)), pl.BlockSpec((B, rows, LANE), lambda c: (0, rev(c), 0)),
                  pl.BlockSpec((1, G, DND, DND), lambda c: (rev(c), 0, 0, 0)),
                  pl.BlockSpec((B, rows, DNH * DND), lambda c: (0, rev(c), 0))],
        out_specs=[pl.BlockSpec((B, rows, 3 * DNH * DND), lambda c: (0, rev(c), 0)), pl.BlockSpec((B, rows, LANE), lambda c: (0, rev(c), 0))],
        out_shape=[jax.ShapeDtypeStruct((B, S, 3 * DNH * DND), f32), jax.ShapeDtypeStruct((B, S, LANE), f32)],
        scratch_shapes=[pltpu.VMEM((G, DND, DND), f32)], compiler_params=_cp(("arbitrary",)),
    )(qkvn, bg, states, do)


GELU_C0, GELU_C1 = math.sqrt(2.0 / math.pi), 0.044715


def _ffn_specs(S):
    nblk = DFF // LANE
    return [pl.BlockSpec((1, S, LANE), lambda i, b: (b, 0, i)), pl.BlockSpec((1, S, LANE), lambda i, b: (b, 0, nblk + i)),
            pl.BlockSpec((FK, LANE), lambda i, b: (0, i)), pl.BlockSpec((FK, LANE), lambda i, b: (0, nblk + i))]


def ffnconv_fwd(up, conv_w):
    B, S, _ = up.shape
    rows = min(FFN_ROWS, S)

    def body(g_ref, v_ref, gw_ref, vw_ref, o_ref, xg, xv):
        _stage_rows(xg, g_ref[0].astype(f32))
        _stage_rows(xv, v_ref[0].astype(f32))
        gw, vw = gw_ref[...], vw_ref[...]
        for r in range(0, S, rows):
            g, _ = _conv_rows(xg, gw, FK, r, rows)
            v, _ = _conv_rows(xv, vw, FK, r, rows)
            t = jnp.tanh(GELU_C0 * (g * (1.0 + GELU_C1 * (g * g))))
            o_ref[0, pl.ds(r, rows), :] = (0.5 * g * (1.0 + t) * v).astype(o_ref.dtype)

    return pl.pallas_call(
        body, name="ffnconv_fwd", grid=(DFF // LANE, B), in_specs=_ffn_specs(S),
        out_specs=pl.BlockSpec((1, S, LANE), lambda i, b: (b, 0, i)), out_shape=jax.ShapeDtypeStruct((B, S, DFF), bf16),
        scratch_shapes=[pltpu.VMEM((S + 8, LANE), f32)] * 2, compiler_params=_cp(("parallel", "parallel")),
    )(up, up, conv_w, conv_w)


def ffnconv_bwd(up, conv_w, dact):
    B, S, _ = up.shape
    rows = min(FFN_ROWS, S)

    def body(g_ref, v_ref, gw_ref, vw_ref, dy_ref, dx_ref, dw_ref, xg, xv, dg, dv):
        _stage_rows(xg, g_ref[0].astype(f32))
        _stage_rows(xv, v_ref[0].astype(f32))
        gw, vw = gw_ref[...], vw_ref[...]
        dgw = [jnp.zeros((8, LANE), f32) for _ in range(FK)]
        dvw = [jnp.zeros((8, LANE), f32) for _ in range(FK)]
        for r in range(0, S, rows):
            g, gwins = _conv_rows(xg, gw, FK, r, rows)
            v, vwins = _conv_rows(xv, vw, FK, r, rows)
            g2 = g * g
            t = jnp.tanh(GELU_C0 * (g * (1.0 + GELU_C1 * g2)))
            half = 0.5 * (1.0 + t)
            dgelu = half + (0.5 * GELU_C0) * g * (1.0 - t * t) * (1.0 + (3.0 * GELU_C1) * g2)
            dy = dy_ref[0, pl.ds(r, rows), :].astype(f32)
            dvc = dy * (g * half)
            dgc = dy * v * dgelu
            dg[pl.ds(r, rows), :] = dgc
            dv[pl.ds(r, rows), :] = dvc
            for j in range(FK):
                dgw[j] = dgw[j] + _fold8(dgc * gwins[j])
                dvw[j] = dvw[j] + _fold8(dvc * vwins[j])
        dg[S:S + 8] = jnp.zeros((8, LANE), f32)
        dv[S:S + 8] = jnp.zeros((8, LANE), f32)
        for r in range(0, S, rows):
            dx_ref[0, 0, pl.ds(r, rows), :] = _conv_rows_t(dg, gw, FK, r, rows).astype(dx_ref.dtype)
            dx_ref[1, 0, pl.ds(r, rows), :] = _conv_rows_t(dv, vw, FK, r, rows).astype(dx_ref.dtype)

        @pl.when(pl.program_id(1) == 0)
        def _():
            dw_ref[...] = jnp.zeros(dw_ref.shape, f32)
        dw_ref[0] += jnp.concatenate([jnp.sum(d, axis=0, keepdims=True) for d in dgw], axis=0)
        dw_ref[1] += jnp.concatenate([jnp.sum(d, axis=0, keepdims=True) for d in dvw], axis=0)

    return pl.pallas_call(
        body, name="ffnconv_bwd", grid=(DFF // LANE, B),
        in_specs=_ffn_specs(S) + [pl.BlockSpec((1, S, LANE), lambda i, b: (b, 0, i))],
        out_specs=[pl.BlockSpec((2, 1, S, LANE), lambda i, b: (0, b, 0, i)), pl.BlockSpec((2, FK, LANE), lambda i, b: (0, 0, i))],
        out_shape=[jax.ShapeDtypeStruct((2, B, S, DFF), bf16), jax.ShapeDtypeStruct((2, FK, DFF), f32)],
        scratch_shapes=[pltpu.VMEM((S + 8, LANE), f32)] * 4, compiler_params=_cp(("parallel", "arbitrary")),
    )(up, up, conv_w, conv_w, dact)


def ada_fwd(c_all, ada_w, ada_b):
    def body(c_ref, w_ref, b_ref, o_ref):
        c = c_ref[...]
        act = (c * jax.nn.sigmoid(c)).astype(bf16)
        o_ref[...] = jnp.dot(act, w_ref[...].astype(bf16), preferred_element_type=f32) + b_ref[...]

    return pl.pallas_call(body, name="ada_fwd", out_shape=jax.ShapeDtypeStruct((c_all.shape[0], ada_w.shape[1]), f32),
                          compiler_params=pltpu.CompilerParams(vmem_limit_bytes=VMEM_LIMIT))(c_all, ada_w, ada_b)


def ada_bwd(c_all, dmod):
    def body(c_ref, d_ref, o_ref):
        c = c_ref[...]
        act = (c * jax.nn.sigmoid(c)).astype(bf16)
        o_ref[...] = lax.dot_general(act, d_ref[...].astype(bf16), (((0,), (0,)), ((), ())), preferred_element_type=f32)

    return pl.pallas_call(body, name="ada_bwd", out_shape=jax.ShapeDtypeStruct((c_all.shape[1], dmod.shape[1]), f32),
                          compiler_params=pltpu.CompilerParams(vmem_limit_bytes=VMEM_LIMIT))(c_all, dmod)


def loss_head(h1, y2, target, g2, w):
    def fn(t, b, c):
        h, y, tg = [v.astype(f32) for v in t]

        def loss_fn(h, y, g, w):
            e = h + g * _rms(y, w) - tg
            return 0.5 * jnp.sum(jnp.mean(e * e, axis=-1))

        loss, grads = jax.value_and_grad(loss_fn, argnums=(0, 1, 2, 3))(h, y, b[0], c[0])
        return [grads[0], grads[1]], [grads[2], grads[3], jnp.full((1, LANE), loss, f32)]

    return rowcall("loss_head", fn, [(h1, D, 0), (y2, D, 0), (target, D, 0)], [g2], [w], [(D, f32), (D, bf16)],
                   [(1, D), (1, D), (1, LANE)])


def adamw(w, gparts, m, v, name):
    R, C = w.shape
    P = gparts.shape[0]
    budget = 2 * 1024 * 1024
    tr, tc = R, C
    if R * C * 4 > budget and R % 8 == 0:
        tr = max(t for t in range(8, R + 1, 8) if R % t == 0 and t * C * 4 <= budget)
    elif R * C * 4 > budget:
        tc = max(t for t in range(LANE, C + 1, LANE) if C % t == 0 and R * t * 4 <= budget)

    def body(w_ref, g_ref, m_ref, v_ref, go, do, mo, vo):
        g = g_ref[0].astype(f32)
        for p in range(1, P):
            g = g + g_ref[p].astype(f32)
        m2 = B1 * m_ref[...] + (1.0 - B1) * g
        v2 = B2 * v_ref[...] + (1.0 - B2) * jnp.square(g)
        m_hat = m2 * (1.0 / (1.0 - B1 ** STEP))
        v_hat = v2 * (1.0 / (1.0 - B2 ** STEP))
        go[...] = g
        do[...] = -LR * (m_hat / (jnp.sqrt(v_hat) + EPS) + WD * w_ref[...])
        mo[...] = m2
        vo[...] = v2

    blk = pl.BlockSpec((tr, tc), lambda i, j: (i, j))
    return pl.pallas_call(
        body, name=name, grid=(R // tr, C // tc), in_specs=[blk, pl.BlockSpec((P, tr, tc), lambda i, j: (0, i, j)), blk, blk],
        out_specs=[blk] * 4, out_shape=[jax.ShapeDtypeStruct((R, C), f32)] * 4, compiler_params=_cp(("parallel", "parallel")),
    )(w, gparts, m, v)


def _pack_w_in(wt):
    aq, ak, av, dqkv, dz, dbeta, da, ga, gd = jnp.split(wt, np.cumsum(IN_SPLITS)[:-1].tolist(), axis=0)
    ba = jnp.pad(jnp.concatenate([dbeta, da], axis=0), ((0, LANE - 2 * DNH), (0, 0)))
    return jnp.concatenate([ga, gd, aq, dqkv, dz, ak, av, ba], axis=0)


def _unpack_w_in(p):
    row = lambda cb, n: p[cb * LANE: cb * LANE + n]
    ba = row(CB_BA, 2 * DNH)
    return jnp.concatenate([row(CB_AQ, HQ * HD), row(CB_AK, HKV * HD), row(CB_AV, HKV * HD), row(CB_DQKV, 3 * DNH * DND),
                            row(CB_DZ, DNH * DND), ba[:DNH], ba[DNH:], row(CB_GA, D), row(CB_GD, D)], axis=0)


def _cols_gathered(g):
    return g.transpose(1, 0, 2).reshape(g.shape[1], NDEV * g.shape[2])


def _cols_split(w):
    r = w.shape[0]
    return w.reshape(r, NDEV, w.shape[1] // NDEV).transpose(1, 0, 2)


def kernel(x, c, ada_w, ada_b, norm_mix_pre, norm_mix_post, norm_ffn_pre, norm_ffn_post, w_in, dn_conv_w, dn_a_log, dn_dt_bias, dn_norm_w, attn_sinks, rel_bias, w_attn_branch, w_dn_branch, w_out, ffn_w_up, ffn_conv_w, ffn_w_down, loss_target, m_ada_w, m_ada_b, m_norm_mix_pre, m_norm_mix_post, m_norm_ffn_pre, m_norm_ffn_post, m_w_in, m_dn_conv_w, m_dn_a_log, m_dn_dt_bias, m_dn_norm_w, m_attn_sinks, m_rel_bias, m_w_attn_branch, m_w_dn_branch, m_w_out, m_ffn_w_up, m_ffn_conv_w, m_ffn_w_down, v_ada_w, v_ada_b, v_norm_mix_pre, v_norm_mix_post, v_norm_ffn_pre, v_norm_ffn_post, v_w_in, v_dn_conv_w, v_dn_a_log, v_dn_dt_bias, v_dn_norm_w, v_attn_sinks, v_rel_bias, v_w_attn_branch, v_w_dn_branch, v_w_out, v_ffn_w_up, v_ffn_conv_w, v_ffn_w_down):
    B, S, _ = x.shape
    T = B * S
    me = 4 * lax.axis_index("x") + 2 * lax.axis_index("y") + lax.axis_index("c")
    big = dict(w_in=w_in, dn_conv_w=dn_conv_w, w_attn_branch=w_attn_branch, w_dn_branch=w_dn_branch, w_out=w_out,
               ffn_w_up=ffn_w_up, ffn_conv_w=ffn_conv_w, ffn_w_down=ffn_w_down)
    big_names = list(big)

    first, mid, late = ["w_in", "dn_conv_w"], ["w_attn_branch", "w_dn_branch", "w_out"], ["ffn_w_up", "ffn_conv_w", "ffn_w_down"]
    transposed = ("w_in", "ffn_w_up")
    local = lambda n, a: a[0].T if n in transposed else a[0]
    shard = lambda names: [local(n, big[n]).astype(bf16) for n in names]
    *got, c_all = _exchange(shard(first) + [c], "gather_w_in", two_level=True)
    gw = dict(zip(first, got))
    gathering_mid = _copy_start(shard(mid), "gather_branches_start", gather=True, after=c_all)
    gathering_ffn = _copy_start(shard(late), "gather_ffn_start", gather=True, after=gathering_mid[-1])
    c_all = c_all.reshape(NDEV * B, D) + gathering_ffn[-1][0, 0]

    wp = _pack_w_in(gw["w_in"].reshape(IN_DIM, D))
    conv_dn = _cols_gathered(gw["dn_conv_w"]).astype(f32)

    ncol = ada_w.shape[2]
    ada_b_mine = lax.dynamic_slice_in_dim(ada_b, me * ncol, ncol, axis=1)
    mod_cols = ada_fwd(c_all, ada_w[0], ada_b_mine)
    (mod_g,) = _exchange([mod_cols], "gather_mod")
    mod = lax.dynamic_slice_in_dim(mod_g, me * B, B, axis=1).transpose(1, 0, 2).reshape(B, NMOD * D)
    sh1, sc1, g1, sh2, sc2, g2 = [mod[:, i * D:(i + 1) * D].reshape(B, 1, D) for i in range(NMOD)]

    onehot = (jnp.asarray(_bucket_table()).reshape(1, -1) == jnp.arange(NBUCK, dtype=jnp.int32)[:, None]).astype(f32)
    bias = mm(rel_bias.T, onehot, "nn", f32, "bias_table", tn=8192, precision=HI).reshape(HQ, WIN, 2 * WIN)
    sinks = attn_sinks.reshape(HQ, 1, 1)
    a_log_pad = jnp.pad(dn_a_log, ((0, 0), (DNH, LANE - 2 * DNH)))
    dt_bias_pad = jnp.pad(dn_dt_bias, ((0, 0), (DNH, LANE - 2 * DNH)))

    (u1,) = rowcall_fwd("mix_pre", f_rms_mod, [(x, D, 0)], [sc1, sh1], [norm_mix_pre], [(D, bf16)])
    proj = mm(u1.reshape(T, D), wp, "nt", bf16, "proj", tm=512, tn=CB_BA * LANE, b_cols=(0, 1)).reshape(B, S, CB_BA * LANE)
    ba = mm(u1.reshape(T, D), wp, "nt", f32, "proj_ba", tn=LANE, b_cols=(CB_BA, 1)).reshape(B, S, LANE)
    ya = attn_fwd(proj, bias, sinks)
    qkvn = dnconv_fwd(proj, conv_dn)
    (bg,) = rowcall_fwd("dn_gate", f_gate, [(ba, LANE, 0)], [], [a_log_pad, dt_bias_pad], [(LANE, f32)])
    o_dn, states = delta_fwd(qkvn, bg)
    gw.update(zip(mid, _copy_finish(gathering_mid, len(mid), o_dn, "gather_branches_finish", gather=True)))
    wa = _cols_gathered(gw["w_attn_branch"])
    wd = _cols_gathered(gw["w_dn_branch"])
    wo = gw["w_out"].reshape(D, D)
    (yd,) = rowcall_fwd("dn_out", f_dnout, [(o_dn, DNH * DND, 0), (proj, DNH * DND, CB_DZ // 4)], [], [dn_norm_w], [(DNH * DND, bf16)])
    pa = mm(ya.reshape(T, HQ * HD), wa, "nn", bf16, "attn_branch").reshape(B, S, D)
    pd = mm(yd.reshape(T, DNH * DND), wd, "nn", bf16, "dn_branch").reshape(B, S, D)
    merge_tok = [(proj, D, CB_GA // 8), (proj, D, CB_GD // 8), (pa, D, 0), (pd, D, 0)]
    (merged,) = rowcall_fwd("merge", f_merge, merge_tok, [], [], [(D, bf16)])
    y1 = mm(merged.reshape(T, D), wo, "nn", bf16, "mix_out").reshape(B, S, D)
    post_pre = ([(x, D, 0), (y1, D, 0)], [g1, sc2, sh2], [norm_mix_post, norm_ffn_pre])
    h1, u2 = rowcall_fwd("mix_post_ffn_pre", f_post_pre, *post_pre, [(D, f32), (D, bf16)])
    gw.update(zip(late, _copy_finish(gathering_ffn, len(late), h1, "gather_ffn_finish", gather=True)))
    wup = gw["ffn_w_up"].reshape(2 * DFF, D)
    conv_ffn = _cols_gathered(gw["ffn_conv_w"]).astype(f32)
    wdown = gw["ffn_w_down"].reshape(DFF, D)
    up = mm(u2.reshape(T, D), wup, "nt", bf16, "ffn_up", tn=2816).reshape(B, S, 2 * DFF)
    act = ffnconv_fwd(up, conv_ffn)
    y2 = mm(act.reshape(T, DFF), wdown, "nn", bf16, "ffn_down", tk=2816).reshape(B, S, D)

    dh1_a, dy2, dg2, dw_ffn_post, loss_b = loss_head(h1, y2, loss_target, g2, norm_ffn_post)
    dy2f = dy2.reshape(T, D)
    dact = mm(dy2f, wdown, "nt", bf16, "ffn_down_dx", tn=2816).reshape(B, S, DFF)
    g_wdown = mm(act.reshape(T, DFF), dy2f, "tn", bf16, "ffn_down_dw", tm=1408, tk=2048)
    in_flight = []

    def send_off(d, tag):
        in_flight.append((d, _copy_start([a.astype(bf16) for a in d.values()], "scatter_" + tag + "_start")))
        return in_flight[-1][1][-1][0, 0]

    started = send_off(dict(ffn_w_down=g_wdown.reshape(NDEV, DFF // NDEV, D)), "ffn_down")
    dup, g_conv_ffn = ffnconv_bwd(up, conv_ffn + started, dact)
    dupf = dup.reshape(2, T, DFF)
    g_conv_ffn = g_conv_ffn.transpose(1, 0, 2).reshape(FK, 2 * DFF)
    du2 = mm(dupf, wup, "nn", bf16, "ffn_up_dx", tk=2816).reshape(B, S, D)
    g_wup = mm(dupf, u2.reshape(T, D), "tn", bf16, "ffn_up_dw", tm=1408, tk=2048)
    started = send_off(dict(ffn_w_up=g_wup.reshape(NDEV, 2 * DFF // NDEV, D), ffn_conv_w=_cols_split(g_conv_ffn)), "ffn_up")
    post_pre = (post_pre[0], [g1 + started, sc2, sh2], post_pre[2])
    dh1, dy1, dg1, dsc2, dsh2, dw_mix_post, dw_ffn_pre = rowcall_bwd(
        "mix_post_ffn_pre_bwd", f_post_pre, *post_pre, [(dh1_a, D, 0), (du2, D, 0)], [(0, f32), (1, bf16)])
    dy1f = dy1.reshape(T, D)
    dmerged = mm(dy1f, wo, "nt", bf16, "mix_out_dx").reshape(B, S, D)
    g_wo = mm(merged.reshape(T, D), dy1f, "tn", bf16, "mix_out_dw", tk=2048)
    dproj = lax.empty((B, S, NP), bf16)
    dproj, dpa, dpd = rowcall_bwd("merge_bwd", f_merge, merge_tok, [], [], [(dmerged, D, 0)],
                                  [(0, bf16), (1, bf16), (2, bf16), (3, bf16)], join_first=2, into=(dproj, CB_GA // 16))
    dpaf, dpdf = dpa.reshape(T, D), dpd.reshape(T, D)
    dya = mm(dpaf, wa, "nt", bf16, "attn_branch_dx").reshape(B, S, HQ * HD)
    g_wa = mm(ya.reshape(T, HQ * HD), dpaf, "tn", bf16, "attn_branch_dw", tk=2048)
    dyd = mm(dpdf, wd, "nt", bf16, "dn_branch_dx").reshape(B, S, DNH * DND)
    g_wd = mm(yd.reshape(T, DNH * DND), dpdf, "tn", bf16, "dn_branch_dw", tk=2048)
    dproj, do_dn, dw_dn_norm = rowcall_bwd("dn_out_bwd", f_dnout, [(o_dn, DNH * DND, 0), (proj, DNH * DND, CB_DZ // 4)], [], [dn_norm_w],
                                           [(dyd, DNH * DND, 0)], [(1, bf16), (0, f32)], into=(dproj, CB_DZ // 4))
    started = send_off(dict(w_attn_branch=_cols_split(g_wa), w_dn_branch=_cols_split(g_wd), w_out=g_wo.reshape(NDEV, D // NDEV, D)), "branches")
    dqkvn, dbg = delta_bwd(qkvn, bg + started, states, do_dn)
    dproj, da_log_pad, ddt_bias_pad = rowcall_bwd("dn_gate_bwd", f_gate, [(ba, LANE, 0)], [], [a_log_pad, dt_bias_pad],
                                                  [(dbg, LANE, 0)], [(0, bf16)], into=(dproj, CB_BA))
    dproj, g_conv_dn = dnconv_bwd(proj, conv_dn, dqkvn, dproj)
    dproj, dk, dv, dbias, dsinks = attn_bwd(proj, bias, sinks, dya, dproj)
    dproj = lax.dynamic_update_slice(dproj, jnp.concatenate([dk, dv], axis=2), (0, 0, CB_AK * LANE)).reshape(T, NP)
    g_wp = mm(dproj, u1.reshape(T, D), "tn", bf16, "proj_dw", tm=1664, tk=1024)
    started = send_off(dict(w_in=_unpack_w_in(g_wp).reshape(NDEV, IN_DIM // NDEV, D), dn_conv_w=_cols_split(g_conv_dn)), "w_in")
    du1 = mm(dproj, wp, "nn", bf16, "proj_dx", tm=512, tk=NP).reshape(B, S, D)
    grad_x, dsc1, dsh1, dw_mix_pre = rowcall_bwd("mix_pre_bwd", f_rms_mod, [(x, D, 0)], [sc1 + started, sh1], [norm_mix_pre],
                                                 [(du1, D, 0)], [(0, f32)], add=(dh1, D, 0))
    g_rel = mm(dbias.reshape(HQ, WIN * 2 * WIN), onehot, "nt", f32, "rel_bias_dw", tk=8192, precision=HI)

    dmod = jnp.concatenate([dsh1, dsc1, dg1, dsh2, dsc2, dg2], axis=2).reshape(B, NMOD * D)

    zrow = lambda a: jnp.concatenate([a.reshape(1, -1), jnp.zeros((B - 1, a.size), f32)], axis=0)
    small_g = jnp.concatenate([
        dmod, dw_mix_pre.reshape(B, D), dw_mix_post.reshape(B, D), dw_ffn_pre.reshape(B, D), dw_ffn_post.reshape(B, D),
        da_log_pad.reshape(B, LANE)[:, DNH:2 * DNH], ddt_bias_pad.reshape(B, LANE)[:, DNH:2 * DNH], dw_dn_norm.reshape(B, DND),
        zrow(dsinks), zrow(g_rel.T), loss_b.reshape(B, LANE)[:, :1], jnp.zeros((B, SMALL_PAD - SMALL_N - 1), f32)], axis=1)
    (small_all,) = _exchange([small_g], "gather_small")
    dmod_cols = lax.dynamic_slice_in_dim(small_all.reshape(NDEV * B, SMALL_PAD), me * ncol, ncol, axis=1)
    g_ada_w = ada_bwd(c_all, dmod_cols)
    parts = {}
    for i, (d, started) in enumerate(in_flight):
        parts.update(zip(d, _copy_finish(started, len(d), g_ada_w, "scatter_finish_%d" % i)))
    small_w = dict(ada_b=(ada_b, m_ada_b, v_ada_b), norm_mix_pre=(norm_mix_pre, m_norm_mix_pre, v_norm_mix_pre),
                   norm_mix_post=(norm_mix_post, m_norm_mix_post, v_norm_mix_post), norm_ffn_pre=(norm_ffn_pre, m_norm_ffn_pre, v_norm_ffn_pre),
                   norm_ffn_post=(norm_ffn_post, m_norm_ffn_post, v_norm_ffn_post), dn_a_log=(dn_a_log, m_dn_a_log, v_dn_a_log),
                   dn_dt_bias=(dn_dt_bias, m_dn_dt_bias, v_dn_dt_bias), dn_norm_w=(dn_norm_w, m_dn_norm_w, v_dn_norm_w),
                   attn_sinks=(attn_sinks, m_attn_sinks, v_attn_sinks), rel_bias=(rel_bias, m_rel_bias, v_rel_bias))

    def pack(i, fill):
        row = jnp.concatenate([small_w[n][i].reshape(1, -1) for n, _ in SMALL], axis=1)
        return jnp.pad(row, ((0, 0), (0, SMALL_PAD - SMALL_N)), constant_values=fill)

    small_out = adamw(pack(0, 0.0), small_all.reshape(NDEV * B, 1, SMALL_PAD), pack(1, 0.0), pack(2, 1.0), "adamw_small")
    loss = small_out[0][0, SMALL_N]

    res = {}
    off = 0
    for n, size in SMALL:
        shp = small_w[n][0].shape
        res[n] = [o[:, off:off + size].reshape(shp) for o in small_out]
        off += size
    res["ada_w"] = [o[None] for o in adamw(ada_w[0], g_ada_w[None], m_ada_w[0], v_ada_w[0], "adamw_ada_w")]
    moments = dict(w_in=(m_w_in, v_w_in), dn_conv_w=(m_dn_conv_w, v_dn_conv_w), w_attn_branch=(m_w_attn_branch, v_w_attn_branch),
                   w_dn_branch=(m_w_dn_branch, v_w_dn_branch), w_out=(m_w_out, v_w_out), ffn_w_up=(m_ffn_w_up, v_ffn_w_up),
                   ffn_conv_w=(m_ffn_conv_w, v_ffn_conv_w), ffn_w_down=(m_ffn_w_down, v_ffn_w_down))
    for n in big_names:
        outs = adamw(local(n, big[n]), parts[n], local(n, moments[n][0]), local(n, moments[n][1]), "adamw_" + n)
        res[n] = [(o.T if n in transposed else o)[None] for o in outs]

    order = ["ada_w", "ada_b", "norm_mix_pre", "norm_mix_post", "norm_ffn_pre", "norm_ffn_post", "w_in", "dn_conv_w", "dn_a_log",
             "dn_dt_bias", "dn_norm_w", "attn_sinks", "rel_bias", "w_attn_branch", "w_dn_branch", "w_out", "ffn_w_up", "ffn_conv_w",
             "ffn_w_down"]
    return (loss, grad_x, *[res[n][0] for n in order], *[res[n][1] for n in order], *[res[n][2] for n in order],
            *[res[n][3] for n in order])
```

```python
import functools
import math

import numpy as np
import jax
import jax.numpy as jnp
from jax import lax
from jax.experimental import pallas as pl
from jax.experimental.pallas import tpu as pltpu

f32 = jnp.float32
bf16 = jnp.bfloat16
HI = lax.Precision.HIGHEST
MID = lax.Precision.HIGH
MESH = pl.DeviceIdType.MESH

NDEV = 8
D = 1024
HQ, HKV, HD, WIN, NBUCK, MAXDIST = 8, 2, 64, 128, 32, 128
DNH, DND, DNK, CH = 4, 128, 4, 64
DFF, FK = 2816, 3
NMOD = 6
RMS_EPS = 1e-6
L2_EPS = 1e-6
NEG_INF = -1e30
LR, B1, B2, EPS, WD, STEP = 0.001, 0.9, 0.999, 1e-08, 0.01, 10

LANE = 128
CB_GA, CB_GD, CB_AQ, CB_DQKV, CB_DZ, CB_AK, CB_AV, CB_BA, NPB = 0, 8, 16, 20, 32, 36, 37, 38, 39
NP = NPB * LANE
IN_SPLITS = (HQ * HD, HKV * HD, HKV * HD, 3 * DNH * DND, DNH * DND, DNH, DNH, D, D)
IN_DIM = sum(IN_SPLITS)
VMEM_LIMIT = 56 * 1024 * 1024

SMALL = (("ada_b", NMOD * D), ("norm_mix_pre", D), ("norm_mix_post", D), ("norm_ffn_pre", D), ("norm_ffn_post", D),
         ("dn_a_log", DNH), ("dn_dt_bias", DNH), ("dn_norm_w", DND), ("attn_sinks", HQ), ("rel_bias", NBUCK * HQ))
SMALL_N = sum(n for _, n in SMALL)
SMALL_PAD = 10752


def _cp(sem):
    return pltpu.CompilerParams(dimension_semantics=sem, vmem_limit_bytes=VMEM_LIMIT)


def _pick(dim, target):
    if dim <= target:
        return dim
    best = None
    for d in range(LANE, target + 1, LANE):
        if dim % d == 0:
            best = d
    assert best is not None, (dim, target)
    return best


def _me():
    x, y, c = lax.axis_index("x"), lax.axis_index("y"), lax.axis_index("c")
    return x, y, c, 4 * x + 2 * y + c


def _peer(x, y, c, k):
    px = 1 - x if k & 4 else x
    py = 1 - y if k & 2 else y
    pc = 1 - c if k & 1 else c
    return (px, py, pc), 4 * px + 2 * py + pc


class _Comm:
    def __init__(self, arrs, two_level=False):
        self.arrs, self.n, self.two_level = list(arrs), len(arrs), two_level
        self.out_shape = [jax.ShapeDtypeStruct((NDEV,) + a.shape, a.dtype) for a in arrs]
        nsem = self.n * (NDEV - 1)
        self.scratch = [pltpu.SemaphoreType.DMA((nsem,)), pltpu.SemaphoreType.DMA((nsem,)), pltpu.SemaphoreType.DMA((self.n,))]
        self.specs = [pl.BlockSpec(memory_space=pl.ANY)] * self.n

    def phases(self, ins, out, send, recv, loc):
        x, y, c, me = _me()

        def remote(a, k, src, dst, to):
            s = a * (NDEV - 1) + k - 1
            return pltpu.make_async_remote_copy(src_ref=src, dst_ref=dst, send_sem=send.at[s], recv_sem=recv.at[s],
                                                device_id=to, device_id_type=MESH)

        def local(a):
            return pltpu.make_async_copy(ins[a], out[a].at[me], loc.at[a])

        if not self.two_level:
            def mine(a, k):
                peer, pid = _peer(x, y, c, k)
                return remote(a, k, ins[a], out[a].at[me], peer)

            def theirs(a, k):
                peer, pid = _peer(x, y, c, k)
                return remote(a, k, ins[a], out[a].at[pid], peer)

            def start():
                for a in range(self.n):
                    local(a).start()
                    for k in range(1, NDEV):
                        mine(a, k).start()

            def forward():
                pass

            def finish():
                for a in range(self.n):
                    for k in range(1, NDEV):
                        mine(a, k).wait_send()
                    for k in range(1, NDEV):
                        theirs(a, k).wait_recv()
                    local(a).wait()

            return start, forward, finish

        sibling = (x, y, 1 - c)
        chips = [(1 - x, y), (x, 1 - y), (1 - x, 1 - y)]
        slot = lambda px, py, pc: 4 * px + 2 * py + pc

        def own(a, k, to):
            return remote(a, k, ins[a], out[a].at[me], to)

        def landed(a, k, frm):
            return remote(a, k, ins[a], out[a].at[slot(*frm)], frm)

        def passed(a, j):
            rows = out[a].at[slot(*chips[j], c)]
            return remote(a, 5 + j, rows, rows, sibling)

        def start():
            for a in range(self.n):
                local(a).start()
                own(a, 1, sibling).start()
                for j, chip in enumerate(chips):
                    own(a, 2 + j, (*chip, c)).start()

        def forward():
            for a in range(self.n):
                for j, chip in enumerate(chips):
                    landed(a, 2 + j, (*chip, c)).wait_recv()
                    passed(a, j).start()

        def finish():
            for a in range(self.n):
                landed(a, 1, sibling).wait_recv()
                for j, chip in enumerate(chips):
                    remote(a, 5 + j, ins[a], out[a].at[slot(*chip, 1 - c)], sibling).wait_recv()
                own(a, 1, sibling).wait_send()
                for j, chip in enumerate(chips):
                    own(a, 2 + j, (*chip, c)).wait_send()
                    passed(a, j).wait_send()
                local(a).wait()

        return start, forward, finish


def _copy_start(arrs, name, gather=False, after=None):
    n = len(arrs)
    order = [] if after is None else [after]
    n_in = 2 * n + len(order)
    block = (lambda ref, j: ref) if gather else (lambda ref, j: ref.at[j])

    def body(*refs):
        ins, lands, send, recv, own, token = refs[:n], refs[n:2 * n], refs[n_in], refs[n_in + 1], refs[n_in + 2], refs[-1]
        x, y, c, me = _me()
        for a in range(n):
            pltpu.make_async_copy(block(ins[a], me), lands[a].at[me], own.at[a]).start()
            for k in range(1, NDEV):
                peer, pid = _peer(x, y, c, k)
                s = a * (NDEV - 1) + k - 1
                pltpu.make_async_remote_copy(src_ref=block(ins[a], pid), dst_ref=lands[a].at[me], send_sem=send.at[s],
                                             recv_sem=recv.at[s], device_id=peer, device_id_type=MESH).start()
        token[...] = jnp.zeros(token.shape, token.dtype)

    hbm, sem = pl.BlockSpec(memory_space=pltpu.HBM), pl.BlockSpec(memory_space=pltpu.SEMAPHORE)
    nsem = n * (NDEV - 1)
    land_shapes = [((NDEV,) + a.shape if gather else a.shape) for a in arrs]
    thru = [pltpu.HBM(a.shape, a.dtype) for a in arrs] + [pltpu.HBM(shp, a.dtype) for shp, a in zip(land_shapes, arrs)]
    return pl.pallas_call(
        body, name=name, in_specs=[hbm] * (2 * n) + [pl.BlockSpec(memory_space=pl.ANY)] * len(order),
        out_shape=(pltpu.SemaphoreType.DMA((nsem,)), pltpu.SemaphoreType.DMA((nsem,)), pltpu.SemaphoreType.DMA((n,)), *thru,
                   jax.ShapeDtypeStruct((8, LANE), f32)),
        out_specs=(sem, sem, sem, *[hbm] * (2 * n), pl.BlockSpec(memory_space=pltpu.VMEM)),
        input_output_aliases={i: 3 + i for i in range(2 * n)},
        compiler_params=pltpu.CompilerParams(has_side_effects=pltpu.SideEffectType.DATAFLOW_SIDE_EFFECTING),
    )(*[pltpu.with_memory_space_constraint(a, pltpu.HBM) for a in arrs],
      *[pltpu.with_memory_space_constraint(lax.empty(shp, a.dtype), pltpu.HBM) for shp, a in zip(land_shapes, arrs)], *order)


def _copy_finish(started, n, after, name, gather=False):
    send, recv, own, *rest = started
    srcs, lands = rest[:n], rest[n:2 * n]
    block = (lambda ref, j: ref) if gather else (lambda ref, j: ref.at[j])

    def body(*refs):
        ins, lnd, send_ref, recv_ref, own_ref = refs[:n], refs[n:2 * n], refs[2 * n], refs[2 * n + 1], refs[2 * n + 2]
        x, y, c, me = _me()
        for a in range(n):
            pltpu.make_async_copy(block(ins[a], me), lnd[a].at[me], own_ref.at[a]).wait()
            for k in range(1, NDEV):
                peer, pid = _peer(x, y, c, k)
                s = a * (NDEV - 1) + k - 1
                cp = pltpu.make_async_remote_copy(src_ref=block(ins[a], pid), dst_ref=lnd[a].at[pid], send_sem=send_ref.at[s],
                                                  recv_sem=recv_ref.at[s], device_id=peer, device_id_type=MESH)
                cp.wait_send()
                cp.wait_recv()

    hbm, sem = pl.BlockSpec(memory_space=pltpu.HBM), pl.BlockSpec(memory_space=pltpu.SEMAPHORE)
    thru = [pltpu.HBM(a.shape, a.dtype) for a in srcs] + [pltpu.HBM(a.shape, a.dtype) for a in lands]
    out = pl.pallas_call(
        body, name=name, in_specs=[hbm] * (2 * n) + [sem, sem, sem, pl.BlockSpec(memory_space=pl.ANY)],
        out_shape=tuple(thru), out_specs=tuple([hbm] * (2 * n)), input_output_aliases={i: i for i in range(2 * n)},
        compiler_params=pltpu.CompilerParams(has_side_effects=pltpu.SideEffectType.DATAFLOW_SIDE_EFFECTING),
    )(*srcs, *lands, send, recv, own, after)
    return list(out[n:])


def _exchange(arrs, name, two_level=False):
    comm = _Comm(arrs, two_level)

    def body(*refs):
        start, forward, finish = comm.phases(refs[:comm.n], refs[comm.n:2 * comm.n], *refs[2 * comm.n:])
        start()
        forward()
        finish()

    return pl.pallas_call(body, name=name, out_shape=comm.out_shape, in_specs=comm.specs, out_specs=comm.specs,
                          scratch_shapes=comm.scratch, compiler_params=pltpu.CompilerParams(has_side_effects=True))(*arrs)


def mm(a, b, mode, out_dtype, name, tm=1024, tn=1024, tk=1024, precision=None, b_cols=None):
    a_parts = a.shape[0] if a.ndim == 3 else 1
    b_parts = b.shape[0] if b.ndim == 3 else 1
    assert b_parts == 1 or mode == "tn"
    ash, bsh = (a.shape[-2], a.shape[-1] * a_parts), b.shape[-2:]
    if mode == "nn":
        (M, K), (K2, N) = ash, bsh
    elif mode == "nt":
        (M, K), (N, K2) = ash, bsh
    else:
        (K, M), (K2, N) = ash, (bsh[0], bsh[1] * b_parts)
    assert K == K2, (name, a.shape, b.shape)
    col0 = 0
    if b_cols is not None:
        assert mode in ("nn", "nt") and tn % LANE == 0
        col0, N = b_cols[0], b_cols[1] * tn
    if mode == "tn":
        tm, tn, tk = _pick(M // a_parts, tm), _pick(N // b_parts, tn), _pick(K, tk)
    else:
        tm, tn, tk = _pick(M, tm), _pick(N // b_parts, tn), _pick(K // a_parts, tk)
    nk = K // tk
    if mode == "tn" and a_parts > 1:
        per = M // tm // a_parts
        a_spec = pl.BlockSpec((None, tk, tm), lambda i, j, k: (i // per, k, i % per))
    elif mode == "tn":
        a_spec = pl.BlockSpec((tk, tm), lambda i, j, k: (k, i))
    elif a_parts > 1:
        per = nk // a_parts
        a_spec = pl.BlockSpec((None, tm, tk), lambda i, j, k: (k // per, i, k % per))
    else:
        a_spec = pl.BlockSpec((tm, tk), lambda i, j, k: (i, k))
    if mode == "nt":
        b_spec = pl.BlockSpec((tn, tk), lambda i, j, k: (col0 + j, k))
    elif b_parts > 1:
        per = N // tn // b_parts
        b_spec = pl.BlockSpec((None, tk, tn), lambda i, j, k: (j // per, k, j % per))
    else:
        b_spec = pl.BlockSpec((tk, tn), lambda i, j, k: (k, col0 + j))
    dims = {"nn": ((1,), (0,)), "nt": ((1,), (1,)), "tn": ((0,), (0,))}[mode]

    def body(a_ref, b_ref, o_ref, *scr):
        p = lax.dot_general(a_ref[...], b_ref[...], (dims, ((), ())), preferred_element_type=f32, precision=precision)
        if nk == 1:
            o_ref[...] = p.astype(o_ref.dtype)
        else:
            acc = scr[0]
            k = pl.program_id(2)

            @pl.when(k == 0)
            def _():
                acc[...] = p

            @pl.when(k > 0)
            def _():
                acc[...] += p

            @pl.when(k == nk - 1)
            def _():
                o_ref[...] = acc[...].astype(o_ref.dtype)

    return pl.pallas_call(
        body, name=name, grid=(M // tm, N // tn, nk), in_specs=[a_spec, b_spec],
        out_specs=pl.BlockSpec((tm, tn), lambda i, j, k: (i, j)), out_shape=jax.ShapeDtypeStruct((M, N), out_dtype),
        scratch_shapes=[pltpu.VMEM((tm, tn), f32)] if nk > 1 else [],
        compiler_params=_cp(("parallel", "parallel", "arbitrary")),
    )(a, b)


ROW_TILE = 512


def rowcall(name, fn, tok, bat, con, tok_out, acc_out, ts=ROW_TILE, into=None):
    B, S = tok[0][0].shape[:2]
    ts = min(ts, S)
    nt, nb, nc, no, na = len(tok), len(bat), len(con), len(tok_out), len(acc_out)
    nin = nt + nb + nc + (1 if into is not None else 0)

    def body(*refs):
        tr, br, cr = refs[:nt], refs[nt:nt + nb], refs[nt + nb:nt + nb + nc]
        orf, arf = refs[nin:nin + no], refs[nin + no:]
        touts, aouts = fn([r[0] for r in tr], [r[0] for r in br], [r[...] for r in cr])
        for r, v in zip(orf, touts):
            r[0] = v.astype(r.dtype)
        s = pl.program_id(1)
        for r, v in zip(arf, aouts):
            @pl.when(s == 0)
            def _(r=r):
                r[...] = jnp.zeros(r.shape, r.dtype)
            r[0] += v.astype(f32)

    in_specs = [pl.BlockSpec((1, ts, w), lambda b, s, cb=cb: (b, s, cb)) for (_, w, cb) in tok]
    in_specs += [pl.BlockSpec((1,) + a.shape[1:], lambda b, s: (b, 0, 0)) for a in bat]
    in_specs += [pl.BlockSpec(a.shape, lambda b, s, nd=a.ndim: (0,) * nd) for a in con]
    out_specs = [pl.BlockSpec((1, ts, w), lambda b, s: (b, s, 0)) for (w, _) in tok_out]
    out_specs += [pl.BlockSpec((1,) + shp, lambda b, s, nd=len(shp): (b,) + (0,) * nd) for shp in acc_out]
    out_shape = [jax.ShapeDtypeStruct((B, S, w), dt) for (w, dt) in tok_out]
    out_shape += [jax.ShapeDtypeStruct((B,) + shp, f32) for shp in acc_out]
    extra, aliases = [], {}
    if into is not None:
        buf, cb = into
        assert buf.dtype == tok_out[0][1]
        in_specs.append(pl.BlockSpec(memory_space=pl.ANY))
        out_specs[0] = pl.BlockSpec((1, ts, tok_out[0][0]), lambda b, s: (b, s, cb))
        out_shape[0] = jax.ShapeDtypeStruct(buf.shape, buf.dtype)
        extra, aliases = [buf], {nin - 1: 0}
    return pl.pallas_call(
        body, name=name, grid=(B, S // ts), in_specs=in_specs, out_specs=out_specs, out_shape=out_shape,
        input_output_aliases=aliases, compiler_params=_cp(("parallel", "arbitrary")),
    )(*[t[0] for t in tok], *bat, *con, *extra)


def rowcall_fwd(name, f, tok, bat, con, tok_out, ts=ROW_TILE):
    def fn(t, b, c):
        return f([v.astype(f32) for v in t], b, c), []
    return rowcall(name, fn, tok, bat, con, tok_out, [], ts)


def rowcall_bwd(name, f, tok, bat, con, cts, tok_grads, add=None, ts=ROW_TILE, join_first=1, into=None):
    nt, ncts = len(tok), len(cts)

    def fn(t, b, c):
        prim = [v.astype(f32) for v in t[:nt]]
        ct = [v.astype(f32) for v in t[nt:nt + ncts]]
        _, vjp = jax.vjp(lambda tt, bb, cc: f(tt, bb, cc), prim, b, c)
        dt, db, dc = vjp(ct)
        touts = [dt[i] for i, _ in tok_grads]
        if add is not None:
            touts[0] = touts[0] + t[nt + ncts].astype(f32)
        if join_first > 1:
            touts = [jnp.concatenate(touts[:join_first], axis=1)] + touts[join_first:]
        return touts, list(db) + list(dc)

    all_tok = list(tok) + list(cts) + ([add] if add is not None else [])
    tok_out = [(tok[i][1], dt) for i, dt in tok_grads]
    if join_first > 1:
        tok_out = [(sum(w for w, _ in tok_out[:join_first]), tok_out[0][1])] + tok_out[join_first:]
    acc_out = [tuple(a.shape[1:]) for a in bat] + [tuple(a.shape) for a in con]
    return rowcall(name, fn, all_tok, bat, con, tok_out, acc_out, ts, into)


def _rms(y, w):
    return y * lax.rsqrt(jnp.mean(y * y, axis=-1, keepdims=True) + RMS_EPS) * w


def f_rms_mod(t, b, c):
    return [_rms(t[0], c[0]) * (1.0 + b[0]) + b[1]]


def f_post_pre(t, b, c):
    h1 = t[0] + b[0] * _rms(t[1], c[0])
    return [h1, _rms(h1, c[1]) * (1.0 + b[1]) + b[2]]


def f_merge(t, b, c):
    ga, gd, ya, yd = t
    return [jax.nn.sigmoid(ga) * ya + jax.nn.sigmoid(gd) * yd]


def f_dnout(t, b, c):
    o, z = t
    outs = []
    for h in range(DNH):
        sl = slice(h * DND, (h + 1) * DND)
        zh = z[:, sl]
        outs.append(_rms(o[:, sl], c[0]) * (zh * jax.nn.sigmoid(zh)))
    return [jnp.concatenate(outs, axis=1)]


def _softplus(x):
    return jnp.maximum(x, 0.0) + jnp.log(1.0 + jnp.exp(-jnp.abs(x)))


def f_gate(t, b, c):
    ba = t[0]
    a_log, dt_bias = c
    lane = lax.broadcasted_iota(jnp.int32, ba.shape, 1)
    beta = jax.nn.sigmoid(ba)
    g = -jnp.exp(a_log) * _softplus(ba + dt_bias)
    return [jnp.where(lane < DNH, beta, jnp.where(lane < 2 * DNH, g, 0.0))]


def _bucket_table():
    qi = np.arange(WIN)[:, None]
    kj = np.arange(2 * WIN)[None, :]
    dist = np.maximum(WIN + qi - kj, 0)
    max_exact = NBUCK // 2
    scaled = np.log(np.maximum(dist, 1).astype(np.float64) / max_exact) / math.log(MAXDIST / max_exact)
    large = np.minimum(max_exact + (scaled * (NBUCK - max_exact)).astype(np.int32), NBUCK - 1)
    return np.where(dist < max_exact, dist, large).astype(np.int32)


def _attn_mask(n):
    qi = lax.broadcasted_iota(jnp.int32, (WIN, 2 * WIN), 0)
    kj = lax.broadcasted_iota(jnp.int32, (WIN, 2 * WIN), 1)
    dist = WIN + qi - kj
    return (dist >= 0) & (dist < WIN) & ((kj >= WIN) | (n > 0))


def _swap_halves(x):
    return pltpu.roll(x, HD, axis=x.ndim - 1)


@jax.custom_vjp
def _swap_halves_vjp(x):
    return _swap_halves(x)


_swap_halves_vjp.defvjp(lambda x: (_swap_halves(x), None), lambda _, g: (_swap_halves(g),))


def _attn_block(q, kp, kc, vp, vc, bias, sinks, mask, differentiated):
    dot = _bdot_bf16_vjp if differentiated else _bdot_bf16
    swap = _swap_halves_vjp if differentiated else _swap_halves
    B, grp = q.shape[0], HQ // HKV
    upper = lax.broadcasted_iota(jnp.int32, (2 * WIN, LANE), 1) >= HD

    def placed(natural, swapped, j, half):
        keep = upper if half == 1 else ~upper
        return jnp.where(keep, natural if j == half else swapped, 0.0)

    qh, ks, vs = [], [], []
    for b in range(B):
        kb, vb = jnp.concatenate([kp[b], kc[b]], axis=0), jnp.concatenate([vp[b], vc[b]], axis=0)
        kb_sw, vb_sw = swap(kb), swap(vb)
        for h in range(HQ):
            qh.append(q[b, :, (h // 2) * LANE:(h // 2 + 1) * LANE])
            ks.append(placed(kb, kb_sw, h // grp, h % 2))
            vs.append(placed(vb, vb_sw, h // grp, h % 2))
    s = dot(_stack(qh), _stack(ks), 2, 2).reshape(B, HQ, WIN, 2 * WIN) * (HD ** -0.5)
    s = jnp.where(mask, s + bias, NEG_INF)
    m = jnp.maximum(jnp.max(s, axis=-1, keepdims=True), sinks)
    p = jnp.exp(s - m)
    probs = p / (jnp.sum(p, axis=-1, keepdims=True) + jnp.exp(sinks - m))
    o = dot(probs.reshape(B * HQ, WIN, 2 * WIN), _stack(vs), 2, 1)
    return _stack([jnp.concatenate([o[b * HQ + 2 * i] + o[b * HQ + 2 * i + 1] for i in range(HQ // 2)], axis=1) for b in range(B)])


def _attn_specs(B, NB):
    last = NB - 1
    return [
        pl.BlockSpec((B, WIN, HQ * HD), lambda n: (0, jnp.minimum(n, last), CB_AQ // 4)),
        pl.BlockSpec((B, WIN, LANE), lambda n: (0, jnp.clip(n - 1, 0, last), CB_AK)),
        pl.BlockSpec((B, WIN, LANE), lambda n: (0, jnp.minimum(n, last), CB_AK)),
        pl.BlockSpec((B, WIN, LANE), lambda n: (0, jnp.clip(n - 1, 0, last), CB_AV)),
        pl.BlockSpec((B, WIN, LANE), lambda n: (0, jnp.minimum(n, last), CB_AV)),
        pl.BlockSpec((HQ, WIN, 2 * WIN), lambda n: (0, 0, 0)),
        pl.BlockSpec((HQ, 1, 1), lambda n: (0, 0, 0)),
    ]


def attn_fwd(proj, bias, sinks):
    B, S, _ = proj.shape
    NB = S // WIN

    def body(q, kp, kc, vp, vc, bias_ref, sink_ref, o_ref):
        mask = _attn_mask(pl.program_id(0))
        o = _attn_block(*[r[...].astype(f32) for r in (q, kp, kc, vp, vc)], bias_ref[...], sink_ref[...], mask, False)
        o_ref[...] = o.astype(o_ref.dtype)

    return pl.pallas_call(
        body, name="attn_fwd", grid=(NB,), in_specs=_attn_specs(B, NB),
        out_specs=pl.BlockSpec((B, WIN, HQ * HD), lambda n: (0, n, 0)), out_shape=jax.ShapeDtypeStruct((B, S, HQ * HD), bf16),
        compiler_params=_cp(("parallel",)),
    )(proj, proj, proj, proj, proj, bias, sinks)


def attn_bwd(proj, bias, sinks, dy, dproj):
    B, S, _ = proj.shape
    NB = S // WIN
    last = NB - 1

    def body(q, kp, kc, vp, vc, bias_ref, sink_ref, dy_ref, _, dq_ref, dk_ref, dv_ref, dbias_ref, dsink_ref, kcar, vcar):
        n = pl.program_id(0)

        @pl.when(n == 0)
        def _():
            dbias_ref[...] = jnp.zeros(dbias_ref.shape, f32)
            dsink_ref[...] = jnp.zeros(dsink_ref.shape, f32)
            kcar[...] = jnp.zeros(kcar.shape, f32)
            vcar[...] = jnp.zeros(vcar.shape, f32)

        @pl.when(n < NB)
        def _():
            mask = _attn_mask(n)
            _, vjp = jax.vjp(lambda *a: _attn_block(*a, mask, True), *[r[...].astype(f32) for r in (q, kp, kc, vp, vc)],
                             bias_ref[...], sink_ref[...])
            dq, dkp, dkc, dvp, dvc, dbias, dsink = vjp(dy_ref[...].astype(f32))
            dq_ref[...] = dq.astype(dq_ref.dtype)
            dbias_ref[...] += dbias
            dsink_ref[...] += dsink
            dk_ref[...] = (kcar[...] + dkp).astype(dk_ref.dtype)
            dv_ref[...] = (vcar[...] + dvp).astype(dv_ref.dtype)
            kcar[...] = dkc
            vcar[...] = dvc

        @pl.when(n == NB)
        def _():
            dk_ref[...] = kcar[...].astype(dk_ref.dtype)
            dv_ref[...] = vcar[...].astype(dv_ref.dtype)

    in_specs = _attn_specs(B, NB) + [pl.BlockSpec((B, WIN, HQ * HD), lambda n: (0, jnp.minimum(n, last), 0)),
                                     pl.BlockSpec(memory_space=pl.ANY)]
    kv_out = pl.BlockSpec((B, WIN, LANE), lambda n: (0, jnp.maximum(n - 1, 0), 0))
    return pl.pallas_call(
        body, name="attn_bwd", grid=(NB + 1,), in_specs=in_specs, input_output_aliases={8: 0},
        out_specs=[pl.BlockSpec((B, WIN, HQ * HD), lambda n: (0, jnp.minimum(n, last), CB_AQ // 4)), kv_out, kv_out,
                   pl.BlockSpec((HQ, WIN, 2 * WIN), lambda n: (0, 0, 0)), pl.BlockSpec((HQ, 1, 1), lambda n: (0, 0, 0))],
        out_shape=[jax.ShapeDtypeStruct(dproj.shape, dproj.dtype), jax.ShapeDtypeStruct((B, S, LANE), bf16),
                   jax.ShapeDtypeStruct((B, S, LANE), bf16), jax.ShapeDtypeStruct((HQ, WIN, 2 * WIN), f32),
                   jax.ShapeDtypeStruct((HQ, 1, 1), f32)],
        scratch_shapes=[pltpu.VMEM((B, WIN, LANE), f32), pltpu.VMEM((B, WIN, LANE), f32)],
        compiler_params=_cp(("arbitrary",)),
    )(proj, proj, proj, proj, proj, bias, sinks, dy, dproj)


DN_ROWS, FFN_ROWS = 256, 32


def _stage_rows(dst, value):
    dst[0:8] = jnp.zeros((8, LANE), f32)
    dst[8:8 + value.shape[0]] = value


def _conv_rows(xs, w, width, r, rows):
    wins = [xs[pl.ds(r + 8 - (width - 1) + j, rows), :] for j in range(width)]
    out = w[0:1] * wins[0]
    for j in range(1, width):
        out = out + w[j:j + 1] * wins[j]
    return out, wins


def _fold8(v):
    return jnp.sum(v.reshape(v.shape[0] // 8, 8, LANE), axis=0)


def _conv_rows_t(ds, w, width, r, rows):
    out = w[0:1] * ds[pl.ds(r + width - 1, rows), :]
    for j in range(1, width):
        out = out + w[j:j + 1] * ds[pl.ds(r + width - 1 - j, rows), :]
    return out


def _dn_outblk(i):
    return (i % DNH) * 3 + i // DNH


def _dn_act(c, isqk):
    sg = jax.nn.sigmoid(c)
    y = c * sg
    n = lax.rsqrt(jnp.sum(y * y, axis=-1, keepdims=True) + L2_EPS)
    return jnp.where(isqk, y * n, y), sg, n


def dnconv_fwd(proj, conv_w):
    B, S, _ = proj.shape
    rows = min(DN_ROWS, S)

    def body(x_ref, w_ref, o_ref, xs):
        isqk = pl.program_id(0) < 2 * DNH
        _stage_rows(xs, x_ref[0].astype(f32))
        w = w_ref[...]
        for r in range(0, S, rows):
            c, _ = _conv_rows(xs, w, DNK, r, rows)
            o_ref[0, pl.ds(r, rows), :] = _dn_act(c, isqk)[0]

    return pl.pallas_call(
        body, name="dnconv_fwd", grid=(3 * DNH, B),
        in_specs=[pl.BlockSpec((1, S, LANE), lambda i, b: (b, 0, CB_DQKV + i)), pl.BlockSpec((DNK, LANE), lambda i, b: (0, i))],
        out_specs=pl.BlockSpec((1, S, LANE), lambda i, b: (b, 0, _dn_outblk(i))),
        out_shape=jax.ShapeDtypeStruct((B, S, 3 * DNH * DND), f32), scratch_shapes=[pltpu.VMEM((S + 8, LANE), f32)],
        compiler_params=_cp(("parallel", "parallel")),
    )(proj, conv_w)


def dnconv_bwd(proj, conv_w, dqkvn, dproj):
    B, S, _ = proj.shape
    rows = min(DN_ROWS, S)

    def body(x_ref, w_ref, dy_ref, _, dx_ref, dw_ref, xs, ds):
        isqk = pl.program_id(0) < 2 * DNH
        _stage_rows(xs, x_ref[0].astype(f32))
        w = w_ref[...]
        dw = [jnp.zeros((8, LANE), f32) for _ in range(DNK)]
        for r in range(0, S, rows):
            c, wins = _conv_rows(xs, w, DNK, r, rows)
            out, sg, n = _dn_act(c, isqk)
            dout = dy_ref[0, pl.ds(r, rows), :]
            dy = jnp.where(isqk, n * (dout - out * jnp.sum(dout * out, axis=-1, keepdims=True)), dout)
            dc = dy * (sg * (1.0 + c * (1.0 - sg)))
            ds[pl.ds(r, rows), :] = dc
            for j in range(DNK):
                dw[j] = dw[j] + _fold8(dc * wins[j])
        ds[S:S + 8] = jnp.zeros((8, LANE), f32)
        for r in range(0, S, rows):
            dx_ref[0, pl.ds(r, rows), :] = _conv_rows_t(ds, w, DNK, r, rows).astype(dx_ref.dtype)

        @pl.when(pl.program_id(1) == 0)
        def _():
            dw_ref[...] = jnp.zeros(dw_ref.shape, f32)
        dw_ref[...] += jnp.concatenate([jnp.sum(d, axis=0, keepdims=True) for d in dw], axis=0)

    return pl.pallas_call(
        body, name="dnconv_bwd", grid=(3 * DNH, B),
        in_specs=[pl.BlockSpec((1, S, LANE), lambda i, b: (b, 0, CB_DQKV + i)), pl.BlockSpec((DNK, LANE), lambda i, b: (0, i)),
                  pl.BlockSpec((1, S, LANE), lambda i, b: (b, 0, _dn_outblk(i))), pl.BlockSpec(memory_space=pl.ANY)],
        out_specs=[pl.BlockSpec((1, S, LANE), lambda i, b: (b, 0, CB_DQKV + i)), pl.BlockSpec((DNK, LANE), lambda i, b: (0, i))],
        out_shape=[jax.ShapeDtypeStruct(dproj.shape, dproj.dtype), jax.ShapeDtypeStruct((DNK, 3 * DNH * DND), f32)],
        scratch_shapes=[pltpu.VMEM((S + 8, LANE), f32), pltpu.VMEM((S + 8, LANE), f32)],
        input_output_aliases={3: 0}, compiler_params=_cp(("parallel", "arbitrary")),
    )(proj, conv_w, dqkvn, dproj)


def _bdot(a, b, ca, cb, precision=HI):
    return lax.dot_general(a, b, (((ca,), (cb,)), ((0,), (0,))), preferred_element_type=f32, precision=precision)


def _bdot_bf16(a, b, ca, cb):
    return _bdot(a.astype(bf16), b.astype(bf16), ca, cb, None)


@functools.partial(jax.custom_vjp, nondiff_argnums=(2, 3))
def _bdot_bf16_vjp(a, b, ca, cb):
    return _bdot_bf16(a, b, ca, cb)


def _bdot_bf16_fwd(a, b, ca, cb):
    return _bdot_bf16(a, b, ca, cb), (a, b)


def _bdot_bf16_bwd(ca, cb, res, g):
    a, b = res
    fa, fb = 3 - ca, 3 - cb
    da = _bdot_bf16(g, b, 2, fb) if ca == 2 else _bdot_bf16(b, g, fb, 2)
    db = _bdot_bf16(a, g, fa, 1) if cb == 1 else _bdot_bf16(g, a, 1, fa)
    return da, db


_bdot_bf16_vjp.defvjp(_bdot_bf16_fwd, _bdot_bf16_bwd)


def _neumann_inverse(low):
    n = low.shape[-1]
    eye = (lax.broadcasted_iota(jnp.int32, (n, n), 0) == lax.broadcasted_iota(jnp.int32, (n, n), 1)).astype(f32)
    p = -low
    x = eye[None] + p
    for _ in range(5):
        p = _bdot_bf16(p, p, 2, 1)
        x = x + _bdot_bf16(x, p, 2, 1)
    return x


@jax.custom_vjp
def _unit_lower_inverse(low):
    return _neumann_inverse(low)


def _uli_fwd(low):
    t = _neumann_inverse(low)
    return t, t


def _uli_bwd(t, dt):
    return (-_bdot_bf16(_bdot_bf16(t, dt, 1, 1), t, 2, 2),)


_unit_lower_inverse.defvjp(_uli_fwd, _uli_bwd)


def _stack(xs):
    return jnp.concatenate([x[None] for x in xs], axis=0)


DELTA_CHUNKS = 2


def _delta_chunks(qkv, bg, state, differentiated):
    inverse = _unit_lower_inverse if differentiated else _neumann_inverse
    lo = _bdot_bf16_vjp if differentiated else _bdot_bf16
    B, n = qkv.shape[0], qkv.shape[1] // CH
    G = B * DNH
    N = n * G
    triples = [(i, b, h) for i in range(n) for b in range(B) for h in range(DNH)]
    col = lambda i, b, h, kind: qkv[b, i * CH:(i + 1) * CH, (3 * h + kind) * DND:(3 * h + kind + 1) * DND]
    q, k, v = [_stack([col(i, b, h, kind) for i, b, h in triples]) for kind in range(3)]
    lane = lax.broadcasted_iota(jnp.int32, (CH, LANE), 1)
    pick = lambda i, b, l: jnp.sum(jnp.where(lane == l, bg[b, i * CH:(i + 1) * CH], 0.0), axis=1, keepdims=True)
    beta = _stack([pick(i, b, h) for i, b, h in triples])
    g = _stack([pick(i, b, h + DNH) for i, b, h in triples])
    ri = lax.broadcasted_iota(jnp.int32, (CH, CH), 0)
    ci = lax.broadcasted_iota(jnp.int32, (CH, CH), 1)
    incl, strict = (ri >= ci)[None], (ri > ci)[None]
    gc = _bdot(jnp.broadcast_to(incl.astype(f32), (N, CH, CH)), jnp.broadcast_to(g, (N, CH, LANE)), 2, 1, MID)
    e0 = jnp.broadcast_to((lane == 0).astype(f32)[None], (N, CH, LANE))
    gc_row = _bdot(e0, gc, 2, 2, MID)
    diff = gc[:, :, :CH] - gc_row
    decay = jnp.where(incl, jnp.exp(jnp.where(incl, diff, 0.0)), 0.0)
    qs = q * (DND ** -0.5)
    kb, vb = k * beta, v * beta
    eg = jnp.exp(gc)
    with_k = lo(jnp.concatenate([kb, qs], axis=1), k, 2, 2)
    low = jnp.where(strict, with_k[:, :CH] * decay, 0.0)
    intra = jnp.where(incl, with_k[:, CH:] * decay, 0.0)
    tinv = inverse(low)
    solved = lo(tinv, jnp.concatenate([vb, kb * eg], axis=2), 2, 1)
    gl = gc[:, CH - 1:CH, :]
    k_tail = k * jnp.exp(gl - gc)
    to_state = jnp.concatenate([solved[:, :, DND:], qs * eg], axis=1)
    decay_all = jnp.exp(gl)
    outs = []
    for i in range(n):
        sl = slice(i * G, (i + 1) * G)
        with_state = lo(to_state[sl], state, 2, 1)
        v_new = solved[sl, :, :DND] - with_state[:, :CH]
        outs.append(with_state[:, CH:] + lo(intra[sl], v_new, 2, 1))
        state = state * decay_all[sl] + lo(k_tail[sl], v_new, 1, 1)
    return outs, state


def delta_fwd(qkvn, bg):
    B, S, _ = qkvn.shape
    n = DELTA_CHUNKS if (S // CH) % DELTA_CHUNKS == 0 else 1
    steps, G, rows = S // (n * CH), B * DNH, n * CH

    def body(qkv_ref, bg_ref, o_ref, st_ref, state):
        @pl.when(pl.program_id(0) == 0)
        def _():
            state[...] = jnp.zeros(state.shape, f32)
        s0 = state[...]
        st_ref[0] = s0
        outs, s1 = _delta_chunks(qkv_ref[...], bg_ref[...], s0, False)
        for i, o in enumerate(outs):
            for b in range(B):
                for h in range(DNH):
                    o_ref[b, i * CH:(i + 1) * CH, h * DND:(h + 1) * DND] = o[b * DNH + h]
        state[...] = s1

    return pl.pallas_call(
        body, name="delta_fwd", grid=(steps,),
        in_specs=[pl.BlockSpec((B, rows, 3 * DNH * DND), lambda c: (0, c, 0)), pl.BlockSpec((B, rows, LANE), lambda c: (0, c, 0))],
        out_specs=[pl.BlockSpec((B, rows, DNH * DND), lambda c: (0, c, 0)), pl.BlockSpec((1, G, DND, DND), lambda c: (c, 0, 0, 0))],
        out_shape=[jax.ShapeDtypeStruct((B, S, DNH * DND), f32), jax.ShapeDtypeStruct((steps, G, DND, DND), f32)],
        scratch_shapes=[pltpu.VMEM((G, DND, DND), f32)], compiler_params=_cp(("arbitrary",)),
    )(qkvn, bg)


def delta_bwd(qkvn, bg, states, do):
    B, S, _ = qkvn.shape
    steps, G = states.shape[0], B * DNH
    rows = S // steps
    n = rows // CH

    def body(qkv_ref, bg_ref, st_ref, do_ref, dqkv_ref, dbg_ref, dstate):
        @pl.when(pl.program_id(0) == 0)
        def _():
            dstate[...] = jnp.zeros(dstate.shape, f32)
        _, vjp = jax.vjp(lambda a, g, s: _delta_chunks(a, g, s, True), qkv_ref[...], bg_ref[...], st_ref[0])
        do = [_stack([do_ref[b, i * CH:(i + 1) * CH, h * DND:(h + 1) * DND] for b in range(B) for h in range(DNH)]) for i in range(n)]
        dqkv, dbg, ds = vjp((do, dstate[...]))
        dqkv_ref[...] = dqkv
        dbg_ref[...] = dbg
        dstate[...] = ds

    rev = lambda c: steps - 1 - c
    return pl.pallas_call(
        body, name="delta_bwd", grid=(steps,),
        in_specs=[pl.BlockSpec((B, rows, 3 * DNH * DND), lambda c: (0, rev(c), 0)), pl.BlockSpec((B, rows, LANE), lambda c: (0, rev(c), 0)),
                  pl.BlockSpec((1, G, DND, DND), lambda c: (rev(c), 0, 0, 0)),
                  pl.BlockSpec((B, rows, DNH * DND), lambda c: (0, rev(c), 0))],
        out_specs=[pl.BlockSpec((B, rows, 3 * DNH * DND), lambda c: (0, rev(c), 0)), pl.BlockSpec((B, rows, LANE), lambda c: (0, rev(c), 0))],
        out_shape=[jax.ShapeDtypeStruct((B, S, 3 * DNH * DND), f32), jax.ShapeDtypeStruct((B, S, LANE), f32)],
        scratch_shapes=[pltpu.VMEM((G, DND, DND), f32)], compiler_params=_cp(("arbitrary",)),
    )(qkvn, bg, states, do)


GELU_C0, GELU_C1 = math.sqrt(2.0 / math.pi), 0.044715


def _ffn_specs(S):
    nblk = DFF // LANE
    return [pl.BlockSpec((1, S, LANE), lambda i, b: (b, 0, i)), pl.BlockSpec((1, S, LANE), lambda i, b: (b, 0, nblk + i)),
            pl.BlockSpec((FK, LANE), lambda i, b: (0, i)), pl.BlockSpec((FK, LANE), lambda i, b: (0, nblk + i))]


def ffnconv_fwd(up, conv_w):
    B, S, _ = up.shape
    rows = min(FFN_ROWS, S)

    def body(g_ref, v_ref, gw_ref, vw_ref, o_ref, xg, xv):
        _stage_rows(xg, g_ref[0].astype(f32))
        _stage_rows(xv, v_ref[0].astype(f32))
        gw, vw = gw_ref[...], vw_ref[...]
        for r in range(0, S, rows):
            g, _ = _conv_rows(xg, gw, FK, r, rows)
            v, _ = _conv_rows(xv, vw, FK, r, rows)
            t = jnp.tanh(GELU_C0 * (g * (1.0 + GELU_C1 * (g * g))))
            o_ref[0, pl.ds(r, rows), :] = (0.5 * g * (1.0 + t) * v).astype(o_ref.dtype)

    return pl.pallas_call(
        body, name="ffnconv_fwd", grid=(DFF // LANE, B), in_specs=_ffn_specs(S),
        out_specs=pl.BlockSpec((1, S, LANE), lambda i, b: (b, 0, i)), out_shape=jax.ShapeDtypeStruct((B, S, DFF), bf16),
        scratch_shapes=[pltpu.VMEM((S + 8, LANE), f32)] * 2, compiler_params=_cp(("parallel", "parallel")),
    )(up, up, conv_w, conv_w)


def ffnconv_bwd(up, conv_w, dact):
    B, S, _ = up.shape
    rows = min(FFN_ROWS, S)

    def body(g_ref, v_ref, gw_ref, vw_ref, dy_ref, dx_ref, dw_ref, xg, xv, dg, dv):
        _stage_rows(xg, g_ref[0].astype(f32))
        _stage_rows(xv, v_ref[0].astype(f32))
        gw, vw = gw_ref[...], vw_ref[...]
        dgw = [jnp.zeros((8, LANE), f32) for _ in range(FK)]
        dvw = [jnp.zeros((8, LANE), f32) for _ in range(FK)]
        for r in range(0, S, rows):
            g, gwins = _conv_rows(xg, gw, FK, r, rows)
            v, vwins = _conv_rows(xv, vw, FK, r, rows)
            g2 = g * g
            t = jnp.tanh(GELU_C0 * (g * (1.0 + GELU_C1 * g2)))
            half = 0.5 * (1.0 + t)
            dgelu = half + (0.5 * GELU_C0) * g * (1.0 - t * t) * (1.0 + (3.0 * GELU_C1) * g2)
            dy = dy_ref[0, pl.ds(r, rows), :].astype(f32)
            dvc = dy * (g * half)
            dgc = dy * v * dgelu
            dg[pl.ds(r, rows), :] = dgc
            dv[pl.ds(r, rows), :] = dvc
            for j in range(FK):
                dgw[j] = dgw[j] + _fold8(dgc * gwins[j])
                dvw[j] = dvw[j] + _fold8(dvc * vwins[j])
        dg[S:S + 8] = jnp.zeros((8, LANE), f32)
        dv[S:S + 8] = jnp.zeros((8, LANE), f32)
        for r in range(0, S, rows):
            dx_ref[0, 0, pl.ds(r, rows), :] = _conv_rows_t(dg, gw, FK, r, rows).astype(dx_ref.dtype)
            dx_ref[1, 0, pl.ds(r, rows), :] = _conv_rows_t(dv, vw, FK, r, rows).astype(dx_ref.dtype)

        @pl.when(pl.program_id(1) == 0)
        def _():
            dw_ref[...] = jnp.zeros(dw_ref.shape, f32)
        dw_ref[0] += jnp.concatenate([jnp.sum(d, axis=0, keepdims=True) for d in dgw], axis=0)
        dw_ref[1] += jnp.concatenate([jnp.sum(d, axis=0, keepdims=True) for d in dvw], axis=0)

    return pl.pallas_call(
        body, name="ffnconv_bwd", grid=(DFF // LANE, B),
        in_specs=_ffn_specs(S) + [pl.BlockSpec((1, S, LANE), lambda i, b: (b, 0, i))],
        out_specs=[pl.BlockSpec((2, 1, S, LANE), lambda i, b: (0, b, 0, i)), pl.BlockSpec((2, FK, LANE), lambda i, b: (0, 0, i))],
        out_shape=[jax.ShapeDtypeStruct((2, B, S, DFF), bf16), jax.ShapeDtypeStruct((2, FK, DFF), f32)],
        scratch_shapes=[pltpu.VMEM((S + 8, LANE), f32)] * 4, compiler_params=_cp(("parallel", "arbitrary")),
    )(up, up, conv_w, conv_w, dact)


def ada_fwd(c_all, ada_w, ada_b):
    def body(c_ref, w_ref, b_ref, o_ref):
        c = c_ref[...]
        act = (c * jax.nn.sigmoid(c)).astype(bf16)
        o_ref[...] = jnp.dot(act, w_ref[...].astype(bf16), preferred_element_type=f32) + b_ref[...]

    return pl.pallas_call(body, name="ada_fwd", out_shape=jax.ShapeDtypeStruct((c_all.shape[0], ada_w.shape[1]), f32),
                          compiler_params=pltpu.CompilerParams(vmem_limit_bytes=VMEM_LIMIT))(c_all, ada_w, ada_b)


def ada_bwd(c_all, dmod):
    def body(c_ref, d_ref, o_ref):
        c = c_ref[...]
        act = (c * jax.nn.sigmoid(c)).astype(bf16)
        o_ref[...] = lax.dot_general(act, d_ref[...].astype(bf16), (((0,), (0,)), ((), ())), preferred_element_type=f32)

    return pl.pallas_call(body, name="ada_bwd", out_shape=jax.ShapeDtypeStruct((c_all.shape[1], dmod.shape[1]), f32),
                          compiler_params=pltpu.CompilerParams(vmem_limit_bytes=VMEM_LIMIT))(c_all, dmod)


def loss_head(h1, y2, target, g2, w):
    def fn(t, b, c):
        h, y, tg = [v.astype(f32) for v in t]

        def loss_fn(h, y, g, w):
            e = h + g * _rms(y, w) - tg
            return 0.5 * jnp.sum(jnp.mean(e * e, axis=-1))

        loss, grads = jax.value_and_grad(loss_fn, argnums=(0, 1, 2, 3))(h, y, b[0], c[0])
        return [grads[0], grads[1]], [grads[2], grads[3], jnp.full((1, LANE), loss, f32)]

    return rowcall("loss_head", fn, [(h1, D, 0), (y2, D, 0), (target, D, 0)], [g2], [w], [(D, f32), (D, bf16)],
                   [(1, D), (1, D), (1, LANE)])


def adamw(w, gparts, m, v, name):
    R, C = w.shape
    P = gparts.shape[0]
    budget = 2 * 1024 * 1024
    tr, tc = R, C
    if R * C * 4 > budget and R % 8 == 0:
        tr = max(t for t in range(8, R + 1, 8) if R % t == 0 and t * C * 4 <= budget)
    elif R * C * 4 > budget:
        tc = max(t for t in range(LANE, C + 1, LANE) if C % t == 0 and R * t * 4 <= budget)

    def body(w_ref, g_ref, m_ref, v_ref, go, do, mo, vo):
        g = g_ref[0].astype(f32)
        for p in range(1, P):
            g = g + g_ref[p].astype(f32)
        m2 = B1 * m_ref[...] + (1.0 - B1) * g
        v2 = B2 * v_ref[...] + (1.0 - B2) * jnp.square(g)
        m_hat = m2 * (1.0 / (1.0 - B1 ** STEP))
        v_hat = v2 * (1.0 / (1.0 - B2 ** STEP))
        go[...] = g
        do[...] = -LR * (m_hat / (jnp.sqrt(v_hat) + EPS) + WD * w_ref[...])
        mo[...] = m2
        vo[...] = v2

    blk = pl.BlockSpec((tr, tc), lambda i, j: (i, j))
    return pl.pallas_call(
        body, name=name, grid=(R // tr, C // tc), in_specs=[blk, pl.BlockSpec((P, tr, tc), lambda i, j: (0, i, j)), blk, blk],
        out_specs=[blk] * 4, out_shape=[jax.ShapeDtypeStruct((R, C), f32)] * 4, compiler_params=_cp(("parallel", "parallel")),
    )(w, gparts, m, v)


def _pack_w_in(wt):
    aq, ak, av, dqkv, dz, dbeta, da, ga, gd = jnp.split(wt, np.cumsum(IN_SPLITS)[:-1].tolist(), axis=0)
    ba = jnp.pad(jnp.concatenate([dbeta, da], axis=0), ((0, LANE - 2 * DNH), (0, 0)))
    return jnp.concatenate([ga, gd, aq, dqkv, dz, ak, av, ba], axis=0)


def _unpack_w_in(p):
    row = lambda cb, n: p[cb * LANE: cb * LANE + n]
    ba = row(CB_BA, 2 * DNH)
    return jnp.concatenate([row(CB_AQ, HQ * HD), row(CB_AK, HKV * HD), row(CB_AV, HKV * HD), row(CB_DQKV, 3 * DNH * DND),
                            row(CB_DZ, DNH * DND), ba[:DNH], ba[DNH:], row(CB_GA, D), row(CB_GD, D)], axis=0)


def _cols_gathered(g):
    return g.transpose(1, 0, 2).reshape(g.shape[1], NDEV * g.shape[2])


def _cols_split(w):
    r = w.shape[0]
    return w.reshape(r, NDEV, w.shape[1] // NDEV).transpose(1, 0, 2)


def kernel(x, c, ada_w, ada_b, norm_mix_pre, norm_mix_post, norm_ffn_pre, norm_ffn_post, w_in, dn_conv_w, dn_a_log, dn_dt_bias, dn_norm_w, attn_sinks, rel_bias, w_attn_branch, w_dn_branch, w_out, ffn_w_up, ffn_conv_w, ffn_w_down, loss_target, m_ada_w, m_ada_b, m_norm_mix_pre, m_norm_mix_post, m_norm_ffn_pre, m_norm_ffn_post, m_w_in, m_dn_conv_w, m_dn_a_log, m_dn_dt_bias, m_dn_norm_w, m_attn_sinks, m_rel_bias, m_w_attn_branch, m_w_dn_branch, m_w_out, m_ffn_w_up, m_ffn_conv_w, m_ffn_w_down, v_ada_w, v_ada_b, v_norm_mix_pre, v_norm_mix_post, v_norm_ffn_pre, v_norm_ffn_post, v_w_in, v_dn_conv_w, v_dn_a_log, v_dn_dt_bias, v_dn_norm_w, v_attn_sinks, v_rel_bias, v_w_attn_branch, v_w_dn_branch, v_w_out, v_ffn_w_up, v_ffn_conv_w, v_ffn_w_down):
    B, S, _ = x.shape
    T = B * S
    me = 4 * lax.axis_index("x") + 2 * lax.axis_index("y") + lax.axis_index("c")
    big = dict(w_in=w_in, dn_conv_w=dn_conv_w, w_attn_branch=w_attn_branch, w_dn_branch=w_dn_branch, w_out=w_out,
               ffn_w_up=ffn_w_up, ffn_conv_w=ffn_conv_w, ffn_w_down=ffn_w_down)
    big_names = list(big)

    first, mid, late = ["w_in", "dn_conv_w"], ["w_attn_branch", "w_dn_branch", "w_out"], ["ffn_w_up", "ffn_conv_w", "ffn_w_down"]
    transposed = ("w_in", "ffn_w_up")
    local = lambda n, a: a[0].T if n in transposed else a[0]
    shard = lambda names: [local(n, big[n]).astype(bf16) for n in names]
    *got, c_all = _exchange(shard(first) + [c], "gather_w_in", two_level=True)
    gw = dict(zip(first, got))
    c_all = c_all.reshape(NDEV * B, D)

    wp = _pack_w_in(gw["w_in"].reshape(IN_DIM, D))
    conv_dn = _cols_gathered(gw["dn_conv_w"]).astype(f32)

    ncol = ada_w.shape[2]
    ada_b_mine = lax.dynamic_slice_in_dim(ada_b, me * ncol, ncol, axis=1)
    mod_cols = ada_fwd(c_all, ada_w[0], ada_b_mine)
    (mod_g,) = _exchange([mod_cols], "gather_mod")
    gathering_mid = _copy_start(shard(mid), "gather_branches_start", gather=True, after=mod_g)
    gathering_ffn = _copy_start(shard(late), "gather_ffn_start", gather=True, after=gathering_mid[-1])
    mod_g = mod_g + gathering_ffn[-1][0, 0]
    mod = lax.dynamic_slice_in_dim(mod_g, me * B, B, axis=1).transpose(1, 0, 2).reshape(B, NMOD * D)
    sh1, sc1, g1, sh2, sc2, g2 = [mod[:, i * D:(i + 1) * D].reshape(B, 1, D) for i in range(NMOD)]

    onehot = (jnp.asarray(_bucket_table()).reshape(1, -1) == jnp.arange(NBUCK, dtype=jnp.int32)[:, None]).astype(f32)
    bias = mm(rel_bias.T, onehot, "nn", f32, "bias_table", tn=8192, precision=HI).reshape(HQ, WIN, 2 * WIN)
    sinks = attn_sinks.reshape(HQ, 1, 1)
    a_log_pad = jnp.pad(dn_a_log, ((0, 0), (DNH, LANE - 2 * DNH)))
    dt_bias_pad = jnp.pad(dn_dt_bias, ((0, 0), (DNH, LANE - 2 * DNH)))

    (u1,) = rowcall_fwd("mix_pre", f_rms_mod, [(x, D, 0)], [sc1, sh1], [norm_mix_pre], [(D, bf16)])
    proj = mm(u1.reshape(T, D), wp, "nt", bf16, "proj", tm=512, tn=CB_BA * LANE, b_cols=(0, 1)).reshape(B, S, CB_BA * LANE)
    ba = mm(u1.reshape(T, D), wp, "nt", f32, "proj_ba", tn=LANE, b_cols=(CB_BA, 1)).reshape(B, S, LANE)
    ya = attn_fwd(proj, bias, sinks)
    qkvn = dnconv_fwd(proj, conv_dn)
    (bg,) = rowcall_fwd("dn_gate", f_gate, [(ba, LANE, 0)], [], [a_log_pad, dt_bias_pad], [(LANE, f32)])
    o_dn, states = delta_fwd(qkvn, bg)
    gw.update(zip(mid, _copy_finish(gathering_mid, len(mid), o_dn, "gather_branches_finish", gather=True)))
    wa = _cols_gathered(gw["w_attn_branch"])
    wd = _cols_gathered(gw["w_dn_branch"])
    wo = gw["w_out"].reshape(D, D)
    (yd,) = rowcall_fwd("dn_out", f_dnout, [(o_dn, DNH * DND, 0), (proj, DNH * DND, CB_DZ // 4)], [], [dn_norm_w], [(DNH * DND, bf16)])
    pa = mm(ya.reshape(T, HQ * HD), wa, "nn", bf16, "attn_branch").reshape(B, S, D)
    pd = mm(yd.reshape(T, DNH * DND), wd, "nn", bf16, "dn_branch").reshape(B, S, D)
    merge_tok = [(proj, D, CB_GA // 8), (proj, D, CB_GD // 8), (pa, D, 0), (pd, D, 0)]
    (merged,) = rowcall_fwd("merge", f_merge, merge_tok, [], [], [(D, bf16)])
    y1 = mm(merged.reshape(T, D), wo, "nn", bf16, "mix_out").reshape(B, S, D)
    post_pre = ([(x, D, 0), (y1, D, 0)], [g1, sc2, sh2], [norm_mix_post, norm_ffn_pre])
    h1, u2 = rowcall_fwd("mix_post_ffn_pre", f_post_pre, *post_pre, [(D, f32), (D, bf16)])
    gw.update(zip(late, _copy_finish(gathering_ffn, len(late), h1, "gather_ffn_finish", gather=True)))
    wup = gw["ffn_w_up"].reshape(2 * DFF, D)
    conv_ffn = _cols_gathered(gw["ffn_conv_w"]).astype(f32)
    wdown = gw["ffn_w_down"].reshape(DFF, D)
    up = mm(u2.reshape(T, D), wup, "nt", bf16, "ffn_up", tn=2816).reshape(B, S, 2 * DFF)
    act = ffnconv_fwd(up, conv_ffn)
    y2 = mm(act.reshape(T, DFF), wdown, "nn", bf16, "ffn_down", tk=2816).reshape(B, S, D)

    dh1_a, dy2, dg2, dw_ffn_post, loss_b = loss_head(h1, y2, loss_target, g2, norm_ffn_post)
    dy2f = dy2.reshape(T, D)
    dact = mm(dy2f, wdown, "nt", bf16, "ffn_down_dx", tn=2816).reshape(B, S, DFF)
    g_wdown = mm(act.reshape(T, DFF), dy2f, "tn", bf16, "ffn_down_dw", tm=1408, tk=2048)
    in_flight = []

    def send_off(d, tag):
        in_flight.append((d, _copy_start([a.astype(bf16) for a in d.values()], "scatter_" + tag + "_start")))
        return in_flight[-1][1][-1][0, 0]

    started = send_off(dict(ffn_w_down=g_wdown.reshape(NDEV, DFF // NDEV, D)), "ffn_down")
    dup, g_conv_ffn = ffnconv_bwd(up, conv_ffn + started, dact)
    dupf = dup.reshape(2, T, DFF)
    g_conv_ffn = g_conv_ffn.transpose(1, 0, 2).reshape(FK, 2 * DFF)
    du2 = mm(dupf, wup, "nn", bf16, "ffn_up_dx", tk=2816).reshape(B, S, D)
    g_wup = mm(dupf, u2.reshape(T, D), "tn", bf16, "ffn_up_dw", tm=1408, tk=2048)
    started = send_off(dict(ffn_w_up=g_wup.reshape(NDEV, 2 * DFF // NDEV, D), ffn_conv_w=_cols_split(g_conv_ffn)), "ffn_up")
    post_pre = (post_pre[0], [g1 + started, sc2, sh2], post_pre[2])
    dh1, dy1, dg1, dsc2, dsh2, dw_mix_post, dw_ffn_pre = rowcall_bwd(
        "mix_post_ffn_pre_bwd", f_post_pre, *post_pre, [(dh1_a, D, 0), (du2, D, 0)], [(0, f32), (1, bf16)])
    dy1f = dy1.reshape(T, D)
    dmerged = mm(dy1f, wo, "nt", bf16, "mix_out_dx").reshape(B, S, D)
    g_wo = mm(merged.reshape(T, D), dy1f, "tn", bf16, "mix_out_dw", tk=2048)
    dproj = lax.empty((B, S, NP), bf16)
    dproj, dpa, dpd = rowcall_bwd("merge_bwd", f_merge, merge_tok, [], [], [(dmerged, D, 0)],
                                  [(0, bf16), (1, bf16), (2, bf16), (3, bf16)], join_first=2, into=(dproj, CB_GA // 16))
    dpaf, dpdf = dpa.reshape(T, D), dpd.reshape(T, D)
    dya = mm(dpaf, wa, "nt", bf16, "attn_branch_dx").reshape(B, S, HQ * HD)
    g_wa = mm(ya.reshape(T, HQ * HD), dpaf, "tn", bf16, "attn_branch_dw", tk=2048)
    dyd = mm(dpdf, wd, "nt", bf16, "dn_branch_dx").reshape(B, S, DNH * DND)
    g_wd = mm(yd.reshape(T, DNH * DND), dpdf, "tn", bf16, "dn_branch_dw", tk=2048)
    dproj, do_dn, dw_dn_norm = rowcall_bwd("dn_out_bwd", f_dnout, [(o_dn, DNH * DND, 0), (proj, DNH * DND, CB_DZ // 4)], [], [dn_norm_w],
                                           [(dyd, DNH * DND, 0)], [(1, bf16), (0, f32)], into=(dproj, CB_DZ // 4))
    started = send_off(dict(w_attn_branch=_cols_split(g_wa), w_dn_branch=_cols_split(g_wd), w_out=g_wo.reshape(NDEV, D // NDEV, D)), "branches")
    dqkvn, dbg = delta_bwd(qkvn, bg + started, states, do_dn)
    dproj, da_log_pad, ddt_bias_pad = rowcall_bwd("dn_gate_bwd", f_gate, [(ba, LANE, 0)], [], [a_log_pad, dt_bias_pad],
                                                  [(dbg, LANE, 0)], [(0, bf16)], into=(dproj, CB_BA))
    dproj, g_conv_dn = dnconv_bwd(proj, conv_dn, dqkvn, dproj)
    dproj, dk, dv, dbias, dsinks = attn_bwd(proj, bias, sinks, dya, dproj)
    dproj = lax.dynamic_update_slice(dproj, jnp.concatenate([dk, dv], axis=2), (0, 0, CB_AK * LANE)).reshape(T, NP)
    g_wp = mm(dproj, u1.reshape(T, D), "tn", bf16, "proj_dw", tm=1664, tk=1024)
    started = send_off(dict(w_in=_unpack_w_in(g_wp).reshape(NDEV, IN_DIM // NDEV, D), dn_conv_w=_cols_split(g_conv_dn)), "w_in")
    du1 = mm(dproj, wp, "nn", bf16, "proj_dx", tm=512, tk=NP).reshape(B, S, D)
    grad_x, dsc1, dsh1, dw_mix_pre = rowcall_bwd("mix_pre_bwd", f_rms_mod, [(x, D, 0)], [sc1 + started, sh1], [norm_mix_pre],
                                                 [(du1, D, 0)], [(0, f32)], add=(dh1, D, 0))
    g_rel = mm(dbias.reshape(HQ, WIN * 2 * WIN), onehot, "nt", f32, "rel_bias_dw", tk=8192, precision=HI)

    dmod = jnp.concatenate([dsh1, dsc1, dg1, dsh2, dsc2, dg2], axis=2).reshape(B, NMOD * D)

    zrow = lambda a: jnp.concatenate([a.reshape(1, -1), jnp.zeros((B - 1, a.size), f32)], axis=0)
    small_g = jnp.concatenate([
        dmod, dw_mix_pre.reshape(B, D), dw_mix_post.reshape(B, D), dw_ffn_pre.reshape(B, D), dw_ffn_post.reshape(B, D),
        da_log_pad.reshape(B, LANE)[:, DNH:2 * DNH], ddt_bias_pad.reshape(B, LANE)[:, DNH:2 * DNH], dw_dn_norm.reshape(B, DND),
        zrow(dsinks), zrow(g_rel.T), loss_b.reshape(B, LANE)[:, :1], jnp.zeros((B, SMALL_PAD - SMALL_N - 1), f32)], axis=1)
    (small_all,) = _exchange([small_g], "gather_small")
    dmod_cols = lax.dynamic_slice_in_dim(small_all.reshape(NDEV * B, SMALL_PAD), me * ncol, ncol, axis=1)
    g_ada_w = ada_bwd(c_all, dmod_cols)
    parts = {}
    for i, (d, started) in enumerate(in_flight):
        parts.update(zip(d, _copy_finish(started, len(d), g_ada_w, "scatter_finish_%d" % i)))
    small_w = dict(ada_b=(ada_b, m_ada_b, v_ada_b), norm_mix_pre=(norm_mix_pre, m_norm_mix_pre, v_norm_mix_pre),
                   norm_mix_post=(norm_mix_post, m_norm_mix_post, v_norm_mix_post), norm_ffn_pre=(norm_ffn_pre, m_norm_ffn_pre, v_norm_ffn_pre),
                   norm_ffn_post=(norm_ffn_post, m_norm_ffn_post, v_norm_ffn_post), dn_a_log=(dn_a_log, m_dn_a_log, v_dn_a_log),
                   dn_dt_bias=(dn_dt_bias, m_dn_dt_bias, v_dn_dt_bias), dn_norm_w=(dn_norm_w, m_dn_norm_w, v_dn_norm_w),
                   attn_sinks=(attn_sinks, m_attn_sinks, v_attn_sinks), rel_bias=(rel_bias, m_rel_bias, v_rel_bias))

    def pack(i, fill):
        row = jnp.concatenate([small_w[n][i].reshape(1, -1) for n, _ in SMALL], axis=1)
        return jnp.pad(row, ((0, 0), (0, SMALL_PAD - SMALL_N)), constant_values=fill)

    small_out = adamw(pack(0, 0.0), small_all.reshape(NDEV * B, 1, SMALL_PAD), pack(1, 0.0), pack(2, 1.0), "adamw_small")
    loss = small_out[0][0, SMALL_N]

    res = {}
    off = 0
    for n, size in SMALL:
        shp = small_w[n][0].shape
        res[n] = [o[:, off:off + size].reshape(shp) for o in small_out]
        off += size
    res["ada_w"] = [o[None] for o in adamw(ada_w[0], g_ada_w[None], m_ada_w[0], v_ada_w[0], "adamw_ada_w")]
    moments = dict(w_in=(m_w_in, v_w_in), dn_conv_w=(m_dn_conv_w, v_dn_conv_w), w_attn_branch=(m_w_attn_branch, v_w_attn_branch),
                   w_dn_branch=(m_w_dn_branch, v_w_dn_branch), w_out=(m_w_out, v_w_out), ffn_w_up=(m_ffn_w_up, v_ffn_w_up),
                   ffn_conv_w=(m_ffn_conv_w, v_ffn_conv_w), ffn_w_down=(m_ffn_w_down, v_ffn_w_down))
    for n in big_names:
        outs = adamw(local(n, big[n]), parts[n], local(n, moments[n][0]), local(n, moments[n][1]), "adamw_" + n)
        res[n] = [(o.T if n in transposed else o)[None] for o in outs]

    order = ["ada_w", "ada_b", "norm_mix_pre", "norm_mix_post", "norm_ffn_pre", "norm_ffn_post", "w_in", "dn_conv_w", "dn_a_log",
             "dn_dt_bias", "dn_norm_w", "attn_sinks", "rel_bias", "w_attn_branch", "w_dn_branch", "w_out", "ffn_w_up", "ffn_conv_w",
             "ffn_w_down"]
    return (loss, grad_x, *[res[n][0] for n in order], *[res[n][1] for n in order], *[res[n][2] for n in order],
            *[res[n][3] for n in order])
```

```python
import functools
import math

import numpy as np
import jax
import jax.numpy as jnp
from jax import lax
from jax.experimental import pallas as pl
from jax.experimental.pallas import tpu as pltpu

f32 = jnp.float32
bf16 = jnp.bfloat16
HI = lax.Precision.HIGHEST
MID = lax.Precision.HIGH
MESH = pl.DeviceIdType.MESH

NDEV = 8
D = 1024
HQ, HKV, HD, WIN, NBUCK, MAXDIST = 8, 2, 64, 128, 32, 128
DNH, DND, DNK, CH = 4, 128, 4, 64
DFF, FK = 2816, 3
NMOD = 6
RMS_EPS = 1e-6
L2_EPS = 1e-6
NEG_INF = -1e30
LR, B1, B2, EPS, WD, STEP = 0.001, 0.9, 0.999, 1e-08, 0.01, 10

LANE = 128
CB_GA, CB_GD, CB_AQ, CB_DQKV, CB_DZ, CB_AK, CB_AV, CB_BA, NPB = 0, 8, 16, 20, 32, 36, 37, 38, 39
NP = NPB * LANE
IN_SPLITS = (HQ * HD, HKV * HD, HKV * HD, 3 * DNH * DND, DNH * DND, DNH, DNH, D, D)
IN_DIM = sum(IN_SPLITS)
VMEM_LIMIT = 56 * 1024 * 1024

SMALL = (("ada_b", NMOD * D), ("norm_mix_pre", D), ("norm_mix_post", D), ("norm_ffn_pre", D), ("norm_ffn_post", D),
         ("dn_a_log", DNH), ("dn_dt_bias", DNH), ("dn_norm_w", DND), ("attn_sinks", HQ), ("rel_bias", NBUCK * HQ))
SMALL_N = sum(n for _, n in SMALL)
SMALL_PAD = 10752


def _cp(sem):
    return pltpu.CompilerParams(dimension_semantics=sem, vmem_limit_bytes=VMEM_LIMIT)


def _pick(dim, target):
    if dim <= target:
        return dim
    best = None
    for d in range(LANE, target + 1, LANE):
        if dim % d == 0:
            best = d
    assert best is not None, (dim, target)
    return best


def _me():
    x, y, c = lax.axis_index("x"), lax.axis_index("y"), lax.axis_index("c")
    return x, y, c, 4 * x + 2 * y + c


def _peer(x, y, c, k):
    px = 1 - x if k & 4 else x
    py = 1 - y if k & 2 else y
    pc = 1 - c if k & 1 else c
    return (px, py, pc), 4 * px + 2 * py + pc


class _Comm:
    def __init__(self, arrs, two_level=False):
        self.arrs, self.n, self.two_level = list(arrs), len(arrs), two_level
        self.out_shape = [jax.ShapeDtypeStruct((NDEV,) + a.shape, a.dtype) for a in arrs]
        nsem = self.n * (NDEV - 1)
        self.scratch = [pltpu.SemaphoreType.DMA((nsem,)), pltpu.SemaphoreType.DMA((nsem,)), pltpu.SemaphoreType.DMA((self.n,))]
        self.specs = [pl.BlockSpec(memory_space=pl.ANY)] * self.n

    def phases(self, ins, out, send, recv, loc):
        x, y, c, me = _me()

        def remote(a, k, src, dst, to):
            s = a * (NDEV - 1) + k - 1
            return pltpu.make_async_remote_copy(src_ref=src, dst_ref=dst, send_sem=send.at[s], recv_sem=recv.at[s],
                                                device_id=to, device_id_type=MESH)

        def local(a):
            return pltpu.make_async_copy(ins[a], out[a].at[me], loc.at[a])

        if not self.two_level:
            def mine(a, k):
                peer, pid = _peer(x, y, c, k)
                return remote(a, k, ins[a], out[a].at[me], peer)

            def theirs(a, k):
                peer, pid = _peer(x, y, c, k)
                return remote(a, k, ins[a], out[a].at[pid], peer)

            def start():
                for a in range(self.n):
                    local(a).start()
                    for k in range(1, NDEV):
                        mine(a, k).start()

            def forward():
                pass

            def finish():
                for a in range(self.n):
                    for k in range(1, NDEV):
                        mine(a, k).wait_send()
                    for k in range(1, NDEV):
                        theirs(a, k).wait_recv()
                    local(a).wait()

            return start, forward, finish

        sibling = (x, y, 1 - c)
        chips = [(1 - x, y), (x, 1 - y), (1 - x, 1 - y)]
        slot = lambda px, py, pc: 4 * px + 2 * py + pc

        def own(a, k, to):
            return remote(a, k, ins[a], out[a].at[me], to)

        def landed(a, k, frm):
            return remote(a, k, ins[a], out[a].at[slot(*frm)], frm)

        def passed(a, j):
            rows = out[a].at[slot(*chips[j], c)]
            return remote(a, 5 + j, rows, rows, sibling)

        def start():
            for a in range(self.n):
                local(a).start()
                own(a, 1, sibling).start()
                for j, chip in enumerate(chips):
                    own(a, 2 + j, (*chip, c)).start()

        def forward():
            for a in range(self.n):
                for j, chip in enumerate(chips):
                    landed(a, 2 + j, (*chip, c)).wait_recv()
                    passed(a, j).start()

        def finish():
            for a in range(self.n):
                landed(a, 1, sibling).wait_recv()
                for j, chip in enumerate(chips):
                    remote(a, 5 + j, ins[a], out[a].at[slot(*chip, 1 - c)], sibling).wait_recv()
                own(a, 1, sibling).wait_send()
                for j, chip in enumerate(chips):
                    own(a, 2 + j, (*chip, c)).wait_send()
                    passed(a, j).wait_send()
                local(a).wait()

        return start, forward, finish


def _copy_start(arrs, name, gather=False, after=None):
    n = len(arrs)
    order = [] if after is None else [after]
    n_in = 2 * n + len(order)
    block = (lambda ref, j: ref) if gather else (lambda ref, j: ref.at[j])

    def body(*refs):
        ins, lands, send, recv, own, token = refs[:n], refs[n:2 * n], refs[n_in], refs[n_in + 1], refs[n_in + 2], refs[-1]
        x, y, c, me = _me()
        for a in range(n):
            pltpu.make_async_copy(block(ins[a], me), lands[a].at[me], own.at[a]).start()
            for k in range(1, NDEV):
                peer, pid = _peer(x, y, c, k)
                s = a * (NDEV - 1) + k - 1
                pltpu.make_async_remote_copy(src_ref=block(ins[a], pid), dst_ref=lands[a].at[me], send_sem=send.at[s],
                                             recv_sem=recv.at[s], device_id=peer, device_id_type=MESH).start()
        token[...] = jnp.zeros(token.shape, token.dtype)

    hbm, sem = pl.BlockSpec(memory_space=pltpu.HBM), pl.BlockSpec(memory_space=pltpu.SEMAPHORE)
    nsem = n * (NDEV - 1)
    land_shapes = [((NDEV,) + a.shape if gather else a.shape) for a in arrs]
    thru = [pltpu.HBM(a.shape, a.dtype) for a in arrs] + [pltpu.HBM(shp, a.dtype) for shp, a in zip(land_shapes, arrs)]
    return pl.pallas_call(
        body, name=name, in_specs=[hbm] * (2 * n) + [pl.BlockSpec(memory_space=pl.ANY)] * len(order),
        out_shape=(pltpu.SemaphoreType.DMA((nsem,)), pltpu.SemaphoreType.DMA((nsem,)), pltpu.SemaphoreType.DMA((n,)), *thru,
                   jax.ShapeDtypeStruct((8, LANE), f32)),
        out_specs=(sem, sem, sem, *[hbm] * (2 * n), pl.BlockSpec(memory_space=pltpu.VMEM)),
        input_output_aliases={i: 3 + i for i in range(2 * n)},
        compiler_params=pltpu.CompilerParams(has_side_effects=pltpu.SideEffectType.DATAFLOW_SIDE_EFFECTING),
    )(*[pltpu.with_memory_space_constraint(a, pltpu.HBM) for a in arrs],
      *[pltpu.with_memory_space_constraint(lax.empty(shp, a.dtype), pltpu.HBM) for shp, a in zip(land_shapes, arrs)], *order)


def _copy_finish(started, n, after, name, gather=False):
    send, recv, own, *rest = started
    srcs, lands = rest[:n], rest[n:2 * n]
    block = (lambda ref, j: ref) if gather else (lambda ref, j: ref.at[j])

    def body(*refs):
        ins, lnd, send_ref, recv_ref, own_ref = refs[:n], refs[n:2 * n], refs[2 * n], refs[2 * n + 1], refs[2 * n + 2]
        x, y, c, me = _me()
        for a in range(n):
            pltpu.make_async_copy(block(ins[a], me), lnd[a].at[me], own_ref.at[a]).wait()
            for k in range(1, NDEV):
                peer, pid = _peer(x, y, c, k)
                s = a * (NDEV - 1) + k - 1
                cp = pltpu.make_async_remote_copy(src_ref=block(ins[a], pid), dst_ref=lnd[a].at[pid], send_sem=send_ref.at[s],
                                                  recv_sem=recv_ref.at[s], device_id=peer, device_id_type=MESH)
                cp.wait_send()
                cp.wait_recv()

    hbm, sem = pl.BlockSpec(memory_space=pltpu.HBM), pl.BlockSpec(memory_space=pltpu.SEMAPHORE)
    thru = [pltpu.HBM(a.shape, a.dtype) for a in srcs] + [pltpu.HBM(a.shape, a.dtype) for a in lands]
    out = pl.pallas_call(
        body, name=name, in_specs=[hbm] * (2 * n) + [sem, sem, sem, pl.BlockSpec(memory_space=pl.ANY)],
        out_shape=tuple(thru), out_specs=tuple([hbm] * (2 * n)), input_output_aliases={i: i for i in range(2 * n)},
        compiler_params=pltpu.CompilerParams(has_side_effects=pltpu.SideEffectType.DATAFLOW_SIDE_EFFECTING),
    )(*srcs, *lands, send, recv, own, after)
    return list(out[n:])


def _exchange(arrs, name, two_level=False):
    comm = _Comm(arrs, two_level)

    def body(*refs):
        start, forward, finish = comm.phases(refs[:comm.n], refs[comm.n:2 * comm.n], *refs[2 * comm.n:])
        start()
        forward()
        finish()

    return pl.pallas_call(body, name=name, out_shape=comm.out_shape, in_specs=comm.specs, out_specs=comm.specs,
                          scratch_shapes=comm.scratch, compiler_params=pltpu.CompilerParams(has_side_effects=True))(*arrs)


def mm(a, b, mode, out_dtype, name, tm=1024, tn=1024, tk=1024, precision=None, b_cols=None):
    a_parts = a.shape[0] if a.ndim == 3 else 1
    b_parts = b.shape[0] if b.ndim == 3 else 1
    assert b_parts == 1 or mode == "tn"
    ash, bsh = (a.shape[-2], a.shape[-1] * a_parts), b.shape[-2:]
    if mode == "nn":
        (M, K), (K2, N) = ash, bsh
    elif mode == "nt":
        (M, K), (N, K2) = ash, bsh
    else:
        (K, M), (K2, N) = ash, (bsh[0], bsh[1] * b_parts)
    assert K == K2, (name, a.shape, b.shape)
    col0 = 0
    if b_cols is not None:
        assert mode in ("nn", "nt") and tn % LANE == 0
        col0, N = b_cols[0], b_cols[1] * tn
    if mode == "tn":
        tm, tn, tk = _pick(M // a_parts, tm), _pick(N // b_parts, tn), _pick(K, tk)
    else:
        tm, tn, tk = _pick(M, tm), _pick(N // b_parts, tn), _pick(K // a_parts, tk)
    nk = K // tk
    if mode == "tn" and a_parts > 1:
        per = M // tm // a_parts
        a_spec = pl.BlockSpec((None, tk, tm), lambda i, j, k: (i // per, k, i % per))
    elif mode == "tn":
        a_spec = pl.BlockSpec((tk, tm), lambda i, j, k: (k, i))
    elif a_parts > 1:
        per = nk // a_parts
        a_spec = pl.BlockSpec((None, tm, tk), lambda i, j, k: (k // per, i, k % per))
    else:
        a_spec = pl.BlockSpec((tm, tk), lambda i, j, k: (i, k))
    if mode == "nt":
        b_spec = pl.BlockSpec((tn, tk), lambda i, j, k: (col0 + j, k))
    elif b_parts > 1:
        per = N // tn // b_parts
        b_spec = pl.BlockSpec((None, tk, tn), lambda i, j, k: (j // per, k, j % per))
    else:
        b_spec = pl.BlockSpec((tk, tn), lambda i, j, k: (k, col0 + j))
    dims = {"nn": ((1,), (0,)), "nt": ((1,), (1,)), "tn": ((0,), (0,))}[mode]

    def body(a_ref, b_ref, o_ref, *scr):
        p = lax.dot_general(a_ref[...], b_ref[...], (dims, ((), ())), preferred_element_type=f32, precision=precision)
        if nk == 1:
            o_ref[...] = p.astype(o_ref.dtype)
        else:
            acc = scr[0]
            k = pl.program_id(2)

            @pl.when(k == 0)
            def _():
                acc[...] = p

            @pl.when(k > 0)
            def _():
                acc[...] += p

            @pl.when(k == nk - 1)
            def _():
                o_ref[...] = acc[...].astype(o_ref.dtype)

    return pl.pallas_call(
        body, name=name, grid=(M // tm, N // tn, nk), in_specs=[a_spec, b_spec],
        out_specs=pl.BlockSpec((tm, tn), lambda i, j, k: (i, j)), out_shape=jax.ShapeDtypeStruct((M, N), out_dtype),
        scratch_shapes=[pltpu.VMEM((tm, tn), f32)] if nk > 1 else [],
        compiler_params=_cp(("parallel", "parallel", "arbitrary")),
    )(a, b)


ROW_TILE = 512


def rowcall(name, fn, tok, bat, con, tok_out, acc_out, ts=ROW_TILE, into=None):
    B, S = tok[0][0].shape[:2]
    ts = min(ts, S)
    nt, nb, nc, no, na = len(tok), len(bat), len(con), len(tok_out), len(acc_out)
    nin = nt + nb + nc + (1 if into is not None else 0)

    def body(*refs):
        tr, br, cr = refs[:nt], refs[nt:nt + nb], refs[nt + nb:nt + nb + nc]
        orf, arf = refs[nin:nin + no], refs[nin + no:]
        touts, aouts = fn([r[0] for r in tr], [r[0] for r in br], [r[...] for r in cr])
        for r, v in zip(orf, touts):
            r[0] = v.astype(r.dtype)
        s = pl.program_id(1)
        for r, v in zip(arf, aouts):
            @pl.when(s == 0)
            def _(r=r):
                r[...] = jnp.zeros(r.shape, r.dtype)
            r[0] += v.astype(f32)

    in_specs = [pl.BlockSpec((1, ts, w), lambda b, s, cb=cb: (b, s, cb)) for (_, w, cb) in tok]
    in_specs += [pl.BlockSpec((1,) + a.shape[1:], lambda b, s: (b, 0, 0)) for a in bat]
    in_specs += [pl.BlockSpec(a.shape, lambda b, s, nd=a.ndim: (0,) * nd) for a in con]
    out_specs = [pl.BlockSpec((1, ts, w), lambda b, s: (b, s, 0)) for (w, _) in tok_out]
    out_specs += [pl.BlockSpec((1,) + shp, lambda b, s, nd=len(shp): (b,) + (0,) * nd) for shp in acc_out]
    out_shape = [jax.ShapeDtypeStruct((B, S, w), dt) for (w, dt) in tok_out]
    out_shape += [jax.ShapeDtypeStruct((B,) + shp, f32) for shp in acc_out]
    extra, aliases = [], {}
    if into is not None:
        buf, cb = into
        assert buf.dtype == tok_out[0][1]
        in_specs.append(pl.BlockSpec(memory_space=pl.ANY))
        out_specs[0] = pl.BlockSpec((1, ts, tok_out[0][0]), lambda b, s: (b, s, cb))
        out_shape[0] = jax.ShapeDtypeStruct(buf.shape, buf.dtype)
        extra, aliases = [buf], {nin - 1: 0}
    return pl.pallas_call(
        body, name=name, grid=(B, S // ts), in_specs=in_specs, out_specs=out_specs, out_shape=out_shape,
        input_output_aliases=aliases, compiler_params=_cp(("parallel", "arbitrary")),
    )(*[t[0] for t in tok], *bat, *con, *extra)


def rowcall_fwd(name, f, tok, bat, con, tok_out, ts=2 * ROW_TILE):
    def fn(t, b, c):
        return f([v.astype(f32) for v in t], b, c), []
    return rowcall(name, fn, tok, bat, con, tok_out, [], ts)


def rowcall_bwd(name, f, tok, bat, con, cts, tok_grads, add=None, ts=ROW_TILE, join_first=1, into=None):
    nt, ncts = len(tok), len(cts)

    def fn(t, b, c):
        prim = [v.astype(f32) for v in t[:nt]]
        ct = [v.astype(f32) for v in t[nt:nt + ncts]]
        _, vjp = jax.vjp(lambda tt, bb, cc: f(tt, bb, cc), prim, b, c)
        dt, db, dc = vjp(ct)
        touts = [dt[i] for i, _ in tok_grads]
        if add is not None:
            touts[0] = touts[0] + t[nt + ncts].astype(f32)
        if join_first > 1:
            touts = [jnp.concatenate(touts[:join_first], axis=1)] + touts[join_first:]
        return touts, list(db) + list(dc)

    all_tok = list(tok) + list(cts) + ([add] if add is not None else [])
    tok_out = [(tok[i][1], dt) for i, dt in tok_grads]
    if join_first > 1:
        tok_out = [(sum(w for w, _ in tok_out[:join_first]), tok_out[0][1])] + tok_out[join_first:]
    acc_out = [tuple(a.shape[1:]) for a in bat] + [tuple(a.shape) for a in con]
    return rowcall(name, fn, all_tok, bat, con, tok_out, acc_out, ts, into)


def _rms(y, w):
    return y * lax.rsqrt(jnp.mean(y * y, axis=-1, keepdims=True) + RMS_EPS) * w


@jax.custom_vjp
def _rms_vjp(y, w):
    return _rms(y, w)


def _rms_vjp_fwd(y, w):
    r = lax.rsqrt(jnp.mean(y * y, axis=-1, keepdims=True) + RMS_EPS)
    yhat = y * r
    return yhat * w, (yhat, r, w)


def _rms_vjp_bwd(res, g):
    yhat, r, w = res
    gw = g * w
    return r * (gw - yhat * jnp.mean(gw * yhat, axis=-1, keepdims=True)), jnp.sum(g * yhat, axis=0, keepdims=True)


_rms_vjp.defvjp(_rms_vjp_fwd, _rms_vjp_bwd)


def f_rms_mod(t, b, c, rms=_rms):
    return [rms(t[0], c[0]) * (1.0 + b[0]) + b[1]]


def f_post_pre(t, b, c, rms=_rms):
    h1 = t[0] + b[0] * rms(t[1], c[0])
    return [h1, rms(h1, c[1]) * (1.0 + b[1]) + b[2]]


def f_merge(t, b, c):
    ga, gd, ya, yd = t
    return [jax.nn.sigmoid(ga) * ya + jax.nn.sigmoid(gd) * yd]


def f_dnout(t, b, c, rms=_rms):
    o, z = t
    outs = []
    for h in range(DNH):
        sl = slice(h * DND, (h + 1) * DND)
        zh = z[:, sl]
        outs.append(rms(o[:, sl], c[0]) * (zh * jax.nn.sigmoid(zh)))
    return [jnp.concatenate(outs, axis=1)]


def _softplus(x):
    return jnp.maximum(x, 0.0) + jnp.log(1.0 + jnp.exp(-jnp.abs(x)))


def f_gate(t, b, c):
    ba = t[0]
    a_log, dt_bias = c
    lane = lax.broadcasted_iota(jnp.int32, ba.shape, 1)
    beta = jax.nn.sigmoid(ba)
    g = -jnp.exp(a_log) * _softplus(ba + dt_bias)
    return [jnp.where(lane < DNH, beta, jnp.where(lane < 2 * DNH, g, 0.0))]


def _bucket_table():
    qi = np.arange(WIN)[:, None]
    kj = np.arange(2 * WIN)[None, :]
    dist = np.maximum(WIN + qi - kj, 0)
    max_exact = NBUCK // 2
    scaled = np.log(np.maximum(dist, 1).astype(np.float64) / max_exact) / math.log(MAXDIST / max_exact)
    large = np.minimum(max_exact + (scaled * (NBUCK - max_exact)).astype(np.int32), NBUCK - 1)
    return np.where(dist < max_exact, dist, large).astype(np.int32)


def _attn_mask(n):
    qi = lax.broadcasted_iota(jnp.int32, (WIN, 2 * WIN), 0)
    kj = lax.broadcasted_iota(jnp.int32, (WIN, 2 * WIN), 1)
    dist = WIN + qi - kj
    return (dist >= 0) & (dist < WIN) & ((kj >= WIN) | (n > 0))


def _swap_halves(x):
    return pltpu.roll(x, HD, axis=x.ndim - 1)


@jax.custom_vjp
def _swap_halves_vjp(x):
    return _swap_halves(x)


_swap_halves_vjp.defvjp(lambda x: (_swap_halves(x), None), lambda _, g: (_swap_halves(g),))


def _attn_block(q, kp, kc, vp, vc, bias, sinks, mask, differentiated):
    dot = _bdot_bf16_vjp if differentiated else _bdot_bf16
    swap = _swap_halves_vjp if differentiated else _swap_halves
    B, grp = q.shape[0], HQ // HKV
    upper = lax.broadcasted_iota(jnp.int32, (2 * WIN, LANE), 1) >= HD

    def placed(natural, swapped, j, half):
        keep = upper if half == 1 else ~upper
        return jnp.where(keep, natural if j == half else swapped, 0.0)

    qh, ks, vs = [], [], []
    for b in range(B):
        kb, vb = jnp.concatenate([kp[b], kc[b]], axis=0), jnp.concatenate([vp[b], vc[b]], axis=0)
        kb_sw, vb_sw = swap(kb), swap(vb)
        for h in range(HQ):
            qh.append(q[b, :, (h // 2) * LANE:(h // 2 + 1) * LANE])
            ks.append(placed(kb, kb_sw, h // grp, h % 2))
            vs.append(placed(vb, vb_sw, h // grp, h % 2))
    s = dot(_stack(qh), _stack(ks), 2, 2).reshape(B, HQ, WIN, 2 * WIN) * (HD ** -0.5)
    s = jnp.where(mask, s + bias, NEG_INF)
    m = jnp.maximum(jnp.max(s, axis=-1, keepdims=True), sinks)
    p = jnp.exp(s - m)
    probs = p / (jnp.sum(p, axis=-1, keepdims=True) + jnp.exp(sinks - m))
    o = dot(probs.reshape(B * HQ, WIN, 2 * WIN), _stack(vs), 2, 1)
    return _stack([jnp.concatenate([o[b * HQ + 2 * i] + o[b * HQ + 2 * i + 1] for i in range(HQ // 2)], axis=1) for b in range(B)])


def _attn_specs(B, NB):
    last = NB - 1
    return [
        pl.BlockSpec((B, WIN, HQ * HD), lambda n: (0, jnp.minimum(n, last), CB_AQ // 4)),
        pl.BlockSpec((B, WIN, LANE), lambda n: (0, jnp.clip(n - 1, 0, last), CB_AK)),
        pl.BlockSpec((B, WIN, LANE), lambda n: (0, jnp.minimum(n, last), CB_AK)),
        pl.BlockSpec((B, WIN, LANE), lambda n: (0, jnp.clip(n - 1, 0, last), CB_AV)),
        pl.BlockSpec((B, WIN, LANE), lambda n: (0, jnp.minimum(n, last), CB_AV)),
        pl.BlockSpec((HQ, WIN, 2 * WIN), lambda n: (0, 0, 0)),
        pl.BlockSpec((HQ, 1, 1), lambda n: (0, 0, 0)),
    ]


def attn_fwd(proj, bias, sinks):
    B, S, _ = proj.shape
    NB = S // WIN

    def body(q, kp, kc, vp, vc, bias_ref, sink_ref, o_ref):
        mask = _attn_mask(pl.program_id(0))
        o = _attn_block(*[r[...].astype(f32) for r in (q, kp, kc, vp, vc)], bias_ref[...], sink_ref[...], mask, False)
        o_ref[...] = o.astype(o_ref.dtype)

    return pl.pallas_call(
        body, name="attn_fwd", grid=(NB,), in_specs=_attn_specs(B, NB),
        out_specs=pl.BlockSpec((B, WIN, HQ * HD), lambda n: (0, n, 0)), out_shape=jax.ShapeDtypeStruct((B, S, HQ * HD), bf16),
        compiler_params=_cp(("parallel",)),
    )(proj, proj, proj, proj, proj, bias, sinks)


def attn_bwd(proj, bias, sinks, dy, dproj):
    B, S, _ = proj.shape
    NB = S // WIN
    last = NB - 1

    def body(q, kp, kc, vp, vc, bias_ref, sink_ref, dy_ref, _, dq_ref, dk_ref, dv_ref, dbias_ref, dsink_ref, kcar, vcar):
        n = pl.program_id(0)

        @pl.when(n == 0)
        def _():
            dbias_ref[...] = jnp.zeros(dbias_ref.shape, f32)
            dsink_ref[...] = jnp.zeros(dsink_ref.shape, f32)
            kcar[...] = jnp.zeros(kcar.shape, f32)
            vcar[...] = jnp.zeros(vcar.shape, f32)

        @pl.when(n < NB)
        def _():
            mask = _attn_mask(n)
            _, vjp = jax.vjp(lambda *a: _attn_block(*a, mask, True), *[r[...].astype(f32) for r in (q, kp, kc, vp, vc)],
                             bias_ref[...], sink_ref[...])
            dq, dkp, dkc, dvp, dvc, dbias, dsink = vjp(dy_ref[...].astype(f32))
            dq_ref[...] = dq.astype(dq_ref.dtype)
            dbias_ref[...] += dbias
            dsink_ref[...] += dsink
            dk_ref[...] = (kcar[...] + dkp).astype(dk_ref.dtype)
            dv_ref[...] = (vcar[...] + dvp).astype(dv_ref.dtype)
            kcar[...] = dkc
            vcar[...] = dvc

        @pl.when(n == NB)
        def _():
            dk_ref[...] = kcar[...].astype(dk_ref.dtype)
            dv_ref[...] = vcar[...].astype(dv_ref.dtype)

    in_specs = _attn_specs(B, NB) + [pl.BlockSpec((B, WIN, HQ * HD), lambda n: (0, jnp.minimum(n, last), 0)),
                                     pl.BlockSpec(memory_space=pl.ANY)]
    kv_out = pl.BlockSpec((B, WIN, LANE), lambda n: (0, jnp.maximum(n - 1, 0), 0))
    return pl.pallas_call(
        body, name="attn_bwd", grid=(NB + 1,), in_specs=in_specs, input_output_aliases={8: 0},
        out_specs=[pl.BlockSpec((B, WIN, HQ * HD), lambda n: (0, jnp.minimum(n, last), CB_AQ // 4)), kv_out, kv_out,
                   pl.BlockSpec((HQ, WIN, 2 * WIN), lambda n: (0, 0, 0)), pl.BlockSpec((HQ, 1, 1), lambda n: (0, 0, 0))],
        out_shape=[jax.ShapeDtypeStruct(dproj.shape, dproj.dtype), jax.ShapeDtypeStruct((B, S, LANE), bf16),
                   jax.ShapeDtypeStruct((B, S, LANE), bf16), jax.ShapeDtypeStruct((HQ, WIN, 2 * WIN), f32),
                   jax.ShapeDtypeStruct((HQ, 1, 1), f32)],
        scratch_shapes=[pltpu.VMEM((B, WIN, LANE), f32), pltpu.VMEM((B, WIN, LANE), f32)],
        compiler_params=_cp(("arbitrary",)),
    )(proj, proj, proj, proj, proj, bias, sinks, dy, dproj)


DN_ROWS, FFN_ROWS = 256, 32


def _stage_rows(dst, value):
    dst[0:8] = jnp.zeros((8, LANE), f32)
    dst[8:8 + value.shape[0]] = value


def _conv_rows(xs, w, width, r, rows):
    wins = [xs[pl.ds(r + 8 - (width - 1) + j, rows), :] for j in range(width)]
    out = w[0:1] * wins[0]
    for j in range(1, width):
        out = out + w[j:j + 1] * wins[j]
    return out, wins


def _fold8(v):
    return jnp.sum(v.reshape(v.shape[0] // 8, 8, LANE), axis=0)


def _conv_rows_t(ds, w, width, r, rows):
    out = w[0:1] * ds[pl.ds(r + width - 1, rows), :]
    for j in range(1, width):
        out = out + w[j:j + 1] * ds[pl.ds(r + width - 1 - j, rows), :]
    return out


def _dn_outblk(i):
    return (i % DNH) * 3 + i // DNH


def _dn_act(c, isqk):
    sg = jax.nn.sigmoid(c)
    y = c * sg
    n = lax.rsqrt(jnp.sum(y * y, axis=-1, keepdims=True) + L2_EPS)
    return jnp.where(isqk, y * n, y), sg, n


def dnconv_fwd(proj, conv_w):
    B, S, _ = proj.shape
    rows = min(DN_ROWS, S)

    def body(x_ref, w_ref, o_ref, xs):
        isqk = pl.program_id(0) < 2 * DNH
        _stage_rows(xs, x_ref[0].astype(f32))
        w = w_ref[...]
        for r in range(0, S, rows):
            c, _ = _conv_rows(xs, w, DNK, r, rows)
            o_ref[0, pl.ds(r, rows), :] = _dn_act(c, isqk)[0]

    return pl.pallas_call(
        body, name="dnconv_fwd", grid=(3 * DNH, B),
        in_specs=[pl.BlockSpec((1, S, LANE), lambda i, b: (b, 0, CB_DQKV + i)), pl.BlockSpec((DNK, LANE), lambda i, b: (0, i))],
        out_specs=pl.BlockSpec((1, S, LANE), lambda i, b: (b, 0, _dn_outblk(i))),
        out_shape=jax.ShapeDtypeStruct((B, S, 3 * DNH * DND), f32), scratch_shapes=[pltpu.VMEM((S + 8, LANE), f32)],
        compiler_params=_cp(("parallel", "parallel")),
    )(proj, conv_w)


def dnconv_bwd(proj, conv_w, dqkvn, dproj):
    B, S, _ = proj.shape
    rows = min(DN_ROWS, S)

    def body(x_ref, w_ref, dy_ref, _, dx_ref, dw_ref, xs, ds):
        isqk = pl.program_id(0) < 2 * DNH
        _stage_rows(xs, x_ref[0].astype(f32))
        w = w_ref[...]
        dw = [jnp.zeros((8, LANE), f32) for _ in range(DNK)]
        for r in range(0, S, rows):
            c, wins = _conv_rows(xs, w, DNK, r, rows)
            out, sg, n = _dn_act(c, isqk)
            dout = dy_ref[0, pl.ds(r, rows), :]
            dy = jnp.where(isqk, n * (dout - out * jnp.sum(dout * out, axis=-1, keepdims=True)), dout)
            dc = dy * (sg * (1.0 + c * (1.0 - sg)))
            ds[pl.ds(r, rows), :] = dc
            for j in range(DNK):
                dw[j] = dw[j] + _fold8(dc * wins[j])
        ds[S:S + 8] = jnp.zeros((8, LANE), f32)
        for r in range(0, S, rows):
            dx_ref[0, pl.ds(r, rows), :] = _conv_rows_t(ds, w, DNK, r, rows).astype(dx_ref.dtype)

        @pl.when(pl.program_id(1) == 0)
        def _():
            dw_ref[...] = jnp.zeros(dw_ref.shape, f32)
        dw_ref[...] += jnp.concatenate([jnp.sum(d, axis=0, keepdims=True) for d in dw], axis=0)

    return pl.pallas_call(
        body, name="dnconv_bwd", grid=(3 * DNH, B),
        in_specs=[pl.BlockSpec((1, S, LANE), lambda i, b: (b, 0, CB_DQKV + i)), pl.BlockSpec((DNK, LANE), lambda i, b: (0, i)),
                  pl.BlockSpec((1, S, LANE), lambda i, b: (b, 0, _dn_outblk(i))), pl.BlockSpec(memory_space=pl.ANY)],
        out_specs=[pl.BlockSpec((1, S, LANE), lambda i, b: (b, 0, CB_DQKV + i)), pl.BlockSpec((DNK, LANE), lambda i, b: (0, i))],
        out_shape=[jax.ShapeDtypeStruct(dproj.shape, dproj.dtype), jax.ShapeDtypeStruct((DNK, 3 * DNH * DND), f32)],
        scratch_shapes=[pltpu.VMEM((S + 8, LANE), f32), pltpu.VMEM((S + 8, LANE), f32)],
        input_output_aliases={3: 0}, compiler_params=_cp(("parallel", "arbitrary")),
    )(proj, conv_w, dqkvn, dproj)


def _bdot(a, b, ca, cb, precision=HI):
    return lax.dot_general(a, b, (((ca,), (cb,)), ((0,), (0,))), preferred_element_type=f32, precision=precision)


def _bdot_bf16(a, b, ca, cb):
    return _bdot(a.astype(bf16), b.astype(bf16), ca, cb, None)


@functools.partial(jax.custom_vjp, nondiff_argnums=(2, 3))
def _bdot_bf16_vjp(a, b, ca, cb):
    return _bdot_bf16(a, b, ca, cb)


def _bdot_bf16_fwd(a, b, ca, cb):
    return _bdot_bf16(a, b, ca, cb), (a, b)


def _bdot_bf16_bwd(ca, cb, res, g):
    a, b = res
    fa, fb = 3 - ca, 3 - cb
    da = _bdot_bf16(g, b, 2, fb) if ca == 2 else _bdot_bf16(b, g, fb, 2)
    db = _bdot_bf16(a, g, fa, 1) if cb == 1 else _bdot_bf16(g, a, 1, fa)
    return da, db


_bdot_bf16_vjp.defvjp(_bdot_bf16_fwd, _bdot_bf16_bwd)


def _neumann_inverse(low):
    n = low.shape[-1]
    eye = (lax.broadcasted_iota(jnp.int32, (n, n), 0) == lax.broadcasted_iota(jnp.int32, (n, n), 1)).astype(f32)
    p = -low
    x = eye[None] + p
    for _ in range(5):
        p = _bdot_bf16(p, p, 2, 1)
        x = x + _bdot_bf16(x, p, 2, 1)
    return x


@jax.custom_vjp
def _unit_lower_inverse(low):
    return _neumann_inverse(low)


def _uli_fwd(low):
    t = _neumann_inverse(low)
    return t, t


def _uli_bwd(t, dt):
    return (-_bdot_bf16(_bdot_bf16(t, dt, 1, 1), t, 2, 2),)


_unit_lower_inverse.defvjp(_uli_fwd, _uli_bwd)


def _stack(xs):
    return jnp.concatenate([x[None] for x in xs], axis=0)


DELTA_CHUNKS = 4


def _delta_chunks(qkv, bg, state, differentiated):
    inverse = _unit_lower_inverse if differentiated else _neumann_inverse
    lo = _bdot_bf16_vjp if differentiated else _bdot_bf16
    B, n = qkv.shape[0], qkv.shape[1] // CH
    G = B * DNH
    N = n * G
    triples = [(i, b, h) for i in range(n) for b in range(B) for h in range(DNH)]
    col = lambda i, b, h, kind: qkv[b, i * CH:(i + 1) * CH, (3 * h + kind) * DND:(3 * h + kind + 1) * DND]
    q, k, v = [_stack([col(i, b, h, kind) for i, b, h in triples]) for kind in range(3)]
    lane = lax.broadcasted_iota(jnp.int32, (CH, LANE), 1)
    pick = lambda i, b, l: jnp.sum(jnp.where(lane == l, bg[b, i * CH:(i + 1) * CH], 0.0), axis=1, keepdims=True)
    beta = _stack([pick(i, b, h) for i, b, h in triples])
    g = _stack([pick(i, b, h + DNH) for i, b, h in triples])
    ri = lax.broadcasted_iota(jnp.int32, (CH, CH), 0)
    ci = lax.broadcasted_iota(jnp.int32, (CH, CH), 1)
    incl, strict = (ri >= ci)[None], (ri > ci)[None]
    gc = _bdot(jnp.broadcast_to(incl.astype(f32), (N, CH, CH)), jnp.broadcast_to(g, (N, CH, LANE)), 2, 1, MID)
    e0 = jnp.broadcast_to((lane == 0).astype(f32)[None], (N, CH, LANE))
    gc_row = _bdot(e0, gc, 2, 2, MID)
    diff = gc[:, :, :CH] - gc_row
    decay = jnp.where(incl, jnp.exp(jnp.where(incl, diff, 0.0)), 0.0)
    qs = q * (DND ** -0.5)
    kb, vb = k * beta, v * beta
    eg = jnp.exp(gc)
    with_k = lo(jnp.concatenate([kb, qs], axis=1), k, 2, 2)
    low = jnp.where(strict, with_k[:, :CH] * decay, 0.0)
    intra = jnp.where(incl, with_k[:, CH:] * decay, 0.0)
    tinv = inverse(low)
    solved = lo(tinv, jnp.concatenate([vb, kb * eg], axis=2), 2, 1)
    gl = gc[:, CH - 1:CH, :]
    k_tail = k * jnp.exp(gl - gc)
    to_state = jnp.concatenate([solved[:, :, DND:], qs * eg], axis=1)
    decay_all = jnp.exp(gl)
    outs = []
    for i in range(n):
        sl = slice(i * G, (i + 1) * G)
        with_state = lo(to_state[sl], state, 2, 1)
        v_new = solved[sl, :, :DND] - with_state[:, :CH]
        outs.append(with_state[:, CH:] + lo(intra[sl], v_new, 2, 1))
        state = state * decay_all[sl] + lo(k_tail[sl], v_new, 1, 1)
    return outs, state


def delta_fwd(qkvn, bg):
    B, S, _ = qkvn.shape
    n = DELTA_CHUNKS if (S // CH) % DELTA_CHUNKS == 0 else 1
    steps, G, rows = S // (n * CH), B * DNH, n * CH

    def body(qkv_ref, bg_ref, o_ref, st_ref, state):
        @pl.when(pl.program_id(0) == 0)
        def _():
            state[...] = jnp.zeros(state.shape, f32)
        s0 = state[...]
        st_ref[0] = s0
        outs, s1 = _delta_chunks(qkv_ref[...], bg_ref[...], s0, False)
        for i, o in enumerate(outs):
            for b in range(B):
                for h in range(DNH):
                    o_ref[b, i * CH:(i + 1) * CH, h * DND:(h + 1) * DND] = o[b * DNH + h]
        state[...] = s1

    return pl.pallas_call(
        body, name="delta_fwd", grid=(steps,),
        in_specs=[pl.BlockSpec((B, rows, 3 * DNH * DND), lambda c: (0, c, 0)), pl.BlockSpec((B, rows, LANE), lambda c: (0, c, 0))],
        out_specs=[pl.BlockSpec((B, rows, DNH * DND), lambda c: (0, c, 0)), pl.BlockSpec((1, G, DND, DND), lambda c: (c, 0, 0, 0))],
        out_shape=[jax.ShapeDtypeStruct((B, S, DNH * DND), f32), jax.ShapeDtypeStruct((steps, G, DND, DND), f32)],
        scratch_shapes=[pltpu.VMEM((G, DND, DND), f32)], compiler_params=_cp(("arbitrary",)),
    )(qkvn, bg)


def delta_bwd(qkvn, bg, states, do):
    B, S, _ = qkvn.shape
    steps, G = states.shape[0], B * DNH
    rows = S // steps
    n = rows // CH

    def body(qkv_ref, bg_ref, st_ref, do_ref, dqkv_ref, dbg_ref, dstate):
        @pl.when(pl.program_id(0) == 0)
        def _():
            dstate[...] = jnp.zeros(dstate.shape, f32)
        _, vjp = jax.vjp(lambda a, g, s: _delta_chunks(a, g, s, True), qkv_ref[...], bg_ref[...], st_ref[0])
        do = [_stack([do_ref[b, i * CH:(i + 1) * CH, h * DND:(h + 1) * DND] for b in range(B) for h in range(DNH)]) for i in range(n)]
        dqkv, dbg, ds = vjp((do, dstate[...]))
        dqkv_ref[...] = dqkv
        dbg_ref[...] = dbg
        dstate[...] = ds

    rev = lambda c: steps - 1 - c
    return pl.pallas_call(
        body, name="delta_bwd", grid=(steps,),
        in_specs=[pl.BlockSpec((B, rows, 3 * DNH * DND), lambda c: (0, rev(c), 0)), pl.BlockSpec((B, rows, LANE), lambda c: (0, rev(c), 0)),
                  pl.BlockSpec((1, G, DND, DND), lambda c: (rev(c), 0, 0, 0)),
                  pl.BlockSpec((B, rows, DNH * DND), lambda c: (0, rev(c), 0))],
        out_specs=[pl.BlockSpec((B, rows, 3 * DNH * DND), lambda c: (0, rev(c), 0)), pl.BlockSpec((B, rows, LANE), lambda c: (0, rev(c), 0))],
        out_shape=[jax.ShapeDtypeStruct((B, S, 3 * DNH * DND), f32), jax.ShapeDtypeStruct((B, S, LANE), f32)],
        scratch_shapes=[pltpu.VMEM((G, DND, DND), f32)], compiler_params=_cp(("arbitrary",)),
    )(qkvn, bg, states, do)


GELU_C0, GELU_C1 = math.sqrt(2.0 / math.pi), 0.044715


def _ffn_specs(S):
    nblk = DFF // LANE
    return [pl.BlockSpec((1, S, LANE), lambda i, b: (b, 0, i)), pl.BlockSpec((1, S, LANE), lambda i, b: (b, 0, nblk + i)),
            pl.BlockSpec((FK, LANE), lambda i, b: (0, i)), pl.BlockSpec((FK, LANE), lambda i, b: (0, nblk + i))]


def ffnconv_fwd(up, conv_w):
    B, S, _ = up.shape
    rows = min(FFN_ROWS, S)

    def body(g_ref, v_ref, gw_ref, vw_ref, o_ref, xg, xv):
        _stage_rows(xg, g_ref[0].astype(f32))
        _stage_rows(xv, v_ref[0].astype(f32))
        gw, vw = gw_ref[...], vw_ref[...]
        for r in range(0, S, rows):
            g, _ = _conv_rows(xg, gw, FK, r, rows)
            v, _ = _conv_rows(xv, vw, FK, r, rows)
            t = jnp.tanh(GELU_C0 * (g * (1.0 + GELU_C1 * (g * g))))
            o_ref[0, pl.ds(r, rows), :] = (0.5 * g * (1.0 + t) * v).astype(o_ref.dtype)

    return pl.pallas_call(
        body, name="ffnconv_fwd", grid=(DFF // LANE, B), in_specs=_ffn_specs(S),
        out_specs=pl.BlockSpec((1, S, LANE), lambda i, b: (b, 0, i)), out_shape=jax.ShapeDtypeStruct((B, S, DFF), bf16),
        scratch_shapes=[pltpu.VMEM((S + 8, LANE), f32)] * 2, compiler_params=_cp(("parallel", "parallel")),
    )(up, up, conv_w, conv_w)


def ffnconv_bwd(up, conv_w, dact):
    B, S, _ = up.shape
    rows = min(FFN_ROWS, S)

    def body(g_ref, v_ref, gw_ref, vw_ref, dy_ref, dx_ref, dw_ref, xg, xv, dg, dv):
        _stage_rows(xg, g_ref[0].astype(f32))
        _stage_rows(xv, v_ref[0].astype(f32))
        gw, vw = gw_ref[...], vw_ref[...]
        dgw = [jnp.zeros((8, LANE), f32) for _ in range(FK)]
        dvw = [jnp.zeros((8, LANE), f32) for _ in range(FK)]
        for r in range(0, S, rows):
            g, gwins = _conv_rows(xg, gw, FK, r, rows)
            v, vwins = _conv_rows(xv, vw, FK, r, rows)
            g2 = g * g
            t = jnp.tanh(GELU_C0 * (g * (1.0 + GELU_C1 * g2)))
            half = 0.5 * (1.0 + t)
            dgelu = half + (0.5 * GELU_C0) * g * (1.0 - t * t) * (1.0 + (3.0 * GELU_C1) * g2)
            dy = dy_ref[0, pl.ds(r, rows), :].astype(f32)
            dvc = dy * (g * half)
            dgc = dy * v * dgelu
            dg[pl.ds(r, rows), :] = dgc
            dv[pl.ds(r, rows), :] = dvc
            for j in range(FK):
                dgw[j] = dgw[j] + _fold8(dgc * gwins[j])
                dvw[j] = dvw[j] + _fold8(dvc * vwins[j])
        dg[S:S + 8] = jnp.zeros((8, LANE), f32)
        dv[S:S + 8] = jnp.zeros((8, LANE), f32)
        for r in range(0, S, rows):
            dx_ref[0, 0, pl.ds(r, rows), :] = _conv_rows_t(dg, gw, FK, r, rows).astype(dx_ref.dtype)
            dx_ref[1, 0, pl.ds(r, rows), :] = _conv_rows_t(dv, vw, FK, r, rows).astype(dx_ref.dtype)

        @pl.when(pl.program_id(1) == 0)
        def _():
            dw_ref[...] = jnp.zeros(dw_ref.shape, f32)
        dw_ref[0] += jnp.concatenate([jnp.sum(d, axis=0, keepdims=True) for d in dgw], axis=0)
        dw_ref[1] += jnp.concatenate([jnp.sum(d, axis=0, keepdims=True) for d in dvw], axis=0)

    return pl.pallas_call(
        body, name="ffnconv_bwd", grid=(DFF // LANE, B),
        in_specs=_ffn_specs(S) + [pl.BlockSpec((1, S, LANE), lambda i, b: (b, 0, i))],
        out_specs=[pl.BlockSpec((2, 1, S, LANE), lambda i, b: (0, b, 0, i)), pl.BlockSpec((2, FK, LANE), lambda i, b: (0, 0, i))],
        out_shape=[jax.ShapeDtypeStruct((2, B, S, DFF), bf16), jax.ShapeDtypeStruct((2, FK, DFF), f32)],
        scratch_shapes=[pltpu.VMEM((S + 8, LANE), f32)] * 4, compiler_params=_cp(("parallel", "arbitrary")),
    )(up, up, conv_w, conv_w, dact)


def ada_fwd(c_all, ada_w, ada_b):
    def body(c_ref, w_ref, b_ref, o_ref):
        c = c_ref[...]
        act = (c * jax.nn.sigmoid(c)).astype(bf16)
        o_ref[...] = jnp.dot(act, w_ref[...].astype(bf16), preferred_element_type=f32) + b_ref[...]

    return pl.pallas_call(body, name="ada_fwd", out_shape=jax.ShapeDtypeStruct((c_all.shape[0], ada_w.shape[1]), f32),
                          compiler_params=pltpu.CompilerParams(vmem_limit_bytes=VMEM_LIMIT))(c_all, ada_w, ada_b)


def ada_bwd(c_all, dmod):
    def body(c_ref, d_ref, o_ref):
        c = c_ref[...]
        act = (c * jax.nn.sigmoid(c)).astype(bf16)
        o_ref[...] = lax.dot_general(act, d_ref[...].astype(bf16), (((0,), (0,)), ((), ())), preferred_element_type=f32)

    return pl.pallas_call(body, name="ada_bwd", out_shape=jax.ShapeDtypeStruct((c_all.shape[1], dmod.shape[1]), f32),
                          compiler_params=pltpu.CompilerParams(vmem_limit_bytes=VMEM_LIMIT))(c_all, dmod)


def loss_head(h1, y2, target, g2, w):
    def fn(t, b, c):
        h, y, tg = [v.astype(f32) for v in t]

        def loss_fn(h, y, g, w):
            e = h + g * _rms_vjp(y, w) - tg
            return 0.5 * jnp.sum(jnp.mean(e * e, axis=-1))

        loss, grads = jax.value_and_grad(loss_fn, argnums=(0, 1, 2, 3))(h, y, b[0], c[0])
        return [grads[0], grads[1]], [grads[2], grads[3], jnp.full((1, LANE), loss, f32)]

    return rowcall("loss_head", fn, [(h1, D, 0), (y2, D, 0), (target, D, 0)], [g2], [w], [(D, f32), (D, bf16)],
                   [(1, D), (1, D), (1, LANE)])


def adamw(w, gparts, m, v, name):
    R, C = w.shape
    P = gparts.shape[0]
    budget = 2 * 1024 * 1024
    tr, tc = R, C
    if R * C * 4 > budget and R % 8 == 0:
        tr = max(t for t in range(8, R + 1, 8) if R % t == 0 and t * C * 4 <= budget)
    elif R * C * 4 > budget:
        tc = max(t for t in range(LANE, C + 1, LANE) if C % t == 0 and R * t * 4 <= budget)

    def body(w_ref, g_ref, m_ref, v_ref, go, do, mo, vo):
        g = g_ref[0].astype(f32)
        for p in range(1, P):
            g = g + g_ref[p].astype(f32)
        m2 = B1 * m_ref[...] + (1.0 - B1) * g
        v2 = B2 * v_ref[...] + (1.0 - B2) * jnp.square(g)
        m_hat = m2 * (1.0 / (1.0 - B1 ** STEP))
        v_hat = v2 * (1.0 / (1.0 - B2 ** STEP))
        go[...] = g
        do[...] = -LR * (m_hat / (jnp.sqrt(v_hat) + EPS) + WD * w_ref[...])
        mo[...] = m2
        vo[...] = v2

    blk = pl.BlockSpec((tr, tc), lambda i, j: (i, j))
    return pl.pallas_call(
        body, name=name, grid=(R // tr, C // tc), in_specs=[blk, pl.BlockSpec((P, tr, tc), lambda i, j: (0, i, j)), blk, blk],
        out_specs=[blk] * 4, out_shape=[jax.ShapeDtypeStruct((R, C), f32)] * 4, compiler_params=_cp(("parallel", "parallel")),
    )(w, gparts, m, v)


def _pack_w_in(wt):
    aq, ak, av, dqkv, dz, dbeta, da, ga, gd = jnp.split(wt, np.cumsum(IN_SPLITS)[:-1].tolist(), axis=0)
    ba = jnp.pad(jnp.concatenate([dbeta, da], axis=0), ((0, LANE - 2 * DNH), (0, 0)))
    return jnp.concatenate([ga, gd, aq, dqkv, dz, ak, av, ba], axis=0)


def _unpack_w_in(p):
    row = lambda cb, n: p[cb * LANE: cb * LANE + n]
    ba = row(CB_BA, 2 * DNH)
    return jnp.concatenate([row(CB_AQ, HQ * HD), row(CB_AK, HKV * HD), row(CB_AV, HKV * HD), row(CB_DQKV, 3 * DNH * DND),
                            row(CB_DZ, DNH * DND), ba[:DNH], ba[DNH:], row(CB_GA, D), row(CB_GD, D)], axis=0)


def _cols_gathered(g):
    return g.transpose(1, 0, 2).reshape(g.shape[1], NDEV * g.shape[2])


def _cols_split(w):
    r = w.shape[0]
    return w.reshape(r, NDEV, w.shape[1] // NDEV).transpose(1, 0, 2)


def kernel(x, c, ada_w, ada_b, norm_mix_pre, norm_mix_post, norm_ffn_pre, norm_ffn_post, w_in, dn_conv_w, dn_a_log, dn_dt_bias, dn_norm_w, attn_sinks, rel_bias, w_attn_branch, w_dn_branch, w_out, ffn_w_up, ffn_conv_w, ffn_w_down, loss_target, m_ada_w, m_ada_b, m_norm_mix_pre, m_norm_mix_post, m_norm_ffn_pre, m_norm_ffn_post, m_w_in, m_dn_conv_w, m_dn_a_log, m_dn_dt_bias, m_dn_norm_w, m_attn_sinks, m_rel_bias, m_w_attn_branch, m_w_dn_branch, m_w_out, m_ffn_w_up, m_ffn_conv_w, m_ffn_w_down, v_ada_w, v_ada_b, v_norm_mix_pre, v_norm_mix_post, v_norm_ffn_pre, v_norm_ffn_post, v_w_in, v_dn_conv_w, v_dn_a_log, v_dn_dt_bias, v_dn_norm_w, v_attn_sinks, v_rel_bias, v_w_attn_branch, v_w_dn_branch, v_w_out, v_ffn_w_up, v_ffn_conv_w, v_ffn_w_down):
    B, S, _ = x.shape
    T = B * S
    me = 4 * lax.axis_index("x") + 2 * lax.axis_index("y") + lax.axis_index("c")
    big = dict(w_in=w_in, dn_conv_w=dn_conv_w, w_attn_branch=w_attn_branch, w_dn_branch=w_dn_branch, w_out=w_out,
               ffn_w_up=ffn_w_up, ffn_conv_w=ffn_conv_w, ffn_w_down=ffn_w_down)
    big_names = list(big)

    first, mid, late = ["w_in", "dn_conv_w"], ["w_attn_branch", "w_dn_branch", "w_out"], ["ffn_w_up", "ffn_conv_w", "ffn_w_down"]
    transposed = ("w_in", "ffn_w_up")
    local = lambda n, a: a[0].T if n in transposed else a[0]
    shard = lambda names: [local(n, big[n]).astype(bf16) for n in names]
    *got, c_all = _exchange(shard(first) + [c], "gather_w_in", two_level=True)
    gw = dict(zip(first, got))
    c_all = c_all.reshape(NDEV * B, D)

    wp = _pack_w_in(gw["w_in"].reshape(IN_DIM, D))
    conv_dn = _cols_gathered(gw["dn_conv_w"]).astype(f32)

    ncol = ada_w.shape[2]
    ada_b_mine = lax.dynamic_slice_in_dim(ada_b, me * ncol, ncol, axis=1)
    mod_cols = ada_fwd(c_all, ada_w[0], ada_b_mine)
    (mod_g,) = _exchange([mod_cols], "gather_mod")
    gathering_mid = _copy_start(shard(mid), "gather_branches_start", gather=True, after=mod_g)
    gathering_ffn = _copy_start(shard(late), "gather_ffn_start", gather=True, after=gathering_mid[-1])
    mod_g = mod_g + gathering_ffn[-1][0, 0]
    mod = lax.dynamic_slice_in_dim(mod_g, me * B, B, axis=1).transpose(1, 0, 2).reshape(B, NMOD * D)
    sh1, sc1, g1, sh2, sc2, g2 = [mod[:, i * D:(i + 1) * D].reshape(B, 1, D) for i in range(NMOD)]

    onehot = (jnp.asarray(_bucket_table()).reshape(1, -1) == jnp.arange(NBUCK, dtype=jnp.int32)[:, None]).astype(f32)
    bias = mm(rel_bias.T, onehot, "nn", f32, "bias_table", tn=8192, precision=HI).reshape(HQ, WIN, 2 * WIN)
    sinks = attn_sinks.reshape(HQ, 1, 1)
    a_log_pad = jnp.pad(dn_a_log, ((0, 0), (DNH, LANE - 2 * DNH)))
    dt_bias_pad = jnp.pad(dn_dt_bias, ((0, 0), (DNH, LANE - 2 * DNH)))

    (u1,) = rowcall_fwd("mix_pre", f_rms_mod, [(x, D, 0)], [sc1, sh1], [norm_mix_pre], [(D, bf16)])
    proj = mm(u1.reshape(T, D), wp, "nt", bf16, "proj", tm=512, tn=CB_BA * LANE, b_cols=(0, 1)).reshape(B, S, CB_BA * LANE)
    ba = mm(u1.reshape(T, D), wp, "nt", f32, "proj_ba", tn=LANE, b_cols=(CB_BA, 1)).reshape(B, S, LANE)
    ya = attn_fwd(proj, bias, sinks)
    qkvn = dnconv_fwd(proj, conv_dn)
    (bg,) = rowcall_fwd("dn_gate", f_gate, [(ba, LANE, 0)], [], [a_log_pad, dt_bias_pad], [(LANE, f32)])
    o_dn, states = delta_fwd(qkvn, bg)
    gw.update(zip(mid, _copy_finish(gathering_mid, len(mid), o_dn, "gather_branches_finish", gather=True)))
    wa = _cols_gathered(gw["w_attn_branch"])
    wd = _cols_gathered(gw["w_dn_branch"])
    wo = gw["w_out"].reshape(D, D)
    (yd,) = rowcall_fwd("dn_out", f_dnout, [(o_dn, DNH * DND, 0), (proj, DNH * DND, CB_DZ // 4)], [], [dn_norm_w], [(DNH * DND, bf16)])
    pa = mm(ya.reshape(T, HQ * HD), wa, "nn", bf16, "attn_branch").reshape(B, S, D)
    pd = mm(yd.reshape(T, DNH * DND), wd, "nn", bf16, "dn_branch").reshape(B, S, D)
    merge_tok = [(proj, D, CB_GA // 8), (proj, D, CB_GD // 8), (pa, D, 0), (pd, D, 0)]
    (merged,) = rowcall_fwd("merge", f_merge, merge_tok, [], [], [(D, bf16)])
    y1 = mm(merged.reshape(T, D), wo, "nn", bf16, "mix_out").reshape(B, S, D)
    post_pre = ([(x, D, 0), (y1, D, 0)], [g1, sc2, sh2], [norm_mix_post, norm_ffn_pre])
    h1, u2 = rowcall_fwd("mix_post_ffn_pre", f_post_pre, *post_pre, [(D, f32), (D, bf16)])
    gw.update(zip(late, _copy_finish(gathering_ffn, len(late), h1, "gather_ffn_finish", gather=True)))
    wup = gw["ffn_w_up"].reshape(2 * DFF, D)
    conv_ffn = _cols_gathered(gw["ffn_conv_w"]).astype(f32)
    wdown = gw["ffn_w_down"].reshape(DFF, D)
    up = mm(u2.reshape(T, D), wup, "nt", bf16, "ffn_up", tn=2816).reshape(B, S, 2 * DFF)
    act = ffnconv_fwd(up, conv_ffn)
    y2 = mm(act.reshape(T, DFF), wdown, "nn", bf16, "ffn_down", tk=2816).reshape(B, S, D)

    dh1_a, dy2, dg2, dw_ffn_post, loss_b = loss_head(h1, y2, loss_target, g2, norm_ffn_post)
    dy2f = dy2.reshape(T, D)
    dact = mm(dy2f, wdown, "nt", bf16, "ffn_down_dx", tn=2816).reshape(B, S, DFF)
    g_wdown = mm(act.reshape(T, DFF), dy2f, "tn", bf16, "ffn_down_dw", tm=1408, tk=2048)
    in_flight = []

    def send_off(d, tag):
        in_flight.append((d, _copy_start([a.astype(bf16) for a in d.values()], "scatter_" + tag + "_start")))
        return in_flight[-1][1][-1][0, 0]

    started = send_off(dict(ffn_w_down=g_wdown.reshape(NDEV, DFF // NDEV, D)), "ffn_down")
    dup, g_conv_ffn = ffnconv_bwd(up, conv_ffn + started, dact)
    dupf = dup.reshape(2, T, DFF)
    g_conv_ffn = g_conv_ffn.transpose(1, 0, 2).reshape(FK, 2 * DFF)
    du2 = mm(dupf, wup, "nn", bf16, "ffn_up_dx", tk=2816).reshape(B, S, D)
    g_wup = mm(dupf, u2.reshape(T, D), "tn", bf16, "ffn_up_dw", tm=1408, tk=2048)
    started = send_off(dict(ffn_w_up=g_wup.reshape(NDEV, 2 * DFF // NDEV, D), ffn_conv_w=_cols_split(g_conv_ffn)), "ffn_up")
    post_pre = (post_pre[0], [g1 + started, sc2, sh2], post_pre[2])
    dh1, dy1, dg1, dsc2, dsh2, dw_mix_post, dw_ffn_pre = rowcall_bwd(
        "mix_post_ffn_pre_bwd", functools.partial(f_post_pre, rms=_rms_vjp), *post_pre, [(dh1_a, D, 0), (du2, D, 0)], [(0, f32), (1, bf16)])
    dy1f = dy1.reshape(T, D)
    dmerged = mm(dy1f, wo, "nt", bf16, "mix_out_dx").reshape(B, S, D)
    g_wo = mm(merged.reshape(T, D), dy1f, "tn", bf16, "mix_out_dw", tk=2048)
    dproj = lax.empty((B, S, NP), bf16)
    dproj, dpa, dpd = rowcall_bwd("merge_bwd", f_merge, merge_tok, [], [], [(dmerged, D, 0)],
                                  [(0, bf16), (1, bf16), (2, bf16), (3, bf16)], join_first=2, into=(dproj, CB_GA // 16))
    dpaf, dpdf = dpa.reshape(T, D), dpd.reshape(T, D)
    dya = mm(dpaf, wa, "nt", bf16, "attn_branch_dx").reshape(B, S, HQ * HD)
    g_wa = mm(ya.reshape(T, HQ * HD), dpaf, "tn", bf16, "attn_branch_dw", tk=2048)
    dyd = mm(dpdf, wd, "nt", bf16, "dn_branch_dx").reshape(B, S, DNH * DND)
    g_wd = mm(yd.reshape(T, DNH * DND), dpdf, "tn", bf16, "dn_branch_dw", tk=2048)
    dproj, do_dn, dw_dn_norm = rowcall_bwd("dn_out_bwd", functools.partial(f_dnout, rms=_rms_vjp), [(o_dn, DNH * DND, 0), (proj, DNH * DND, CB_DZ // 4)], [], [dn_norm_w],
                                           [(dyd, DNH * DND, 0)], [(1, bf16), (0, f32)], into=(dproj, CB_DZ // 4))
    started = send_off(dict(w_attn_branch=_cols_split(g_wa), w_dn_branch=_cols_split(g_wd), w_out=g_wo.reshape(NDEV, D // NDEV, D)), "branches")
    dqkvn, dbg = delta_bwd(qkvn, bg + started, states, do_dn)
    dproj, da_log_pad, ddt_bias_pad = rowcall_bwd("dn_gate_bwd", f_gate, [(ba, LANE, 0)], [], [a_log_pad, dt_bias_pad],
                                                  [(dbg, LANE, 0)], [(0, bf16)], into=(dproj, CB_BA))
    dproj, g_conv_dn = dnconv_bwd(proj, conv_dn, dqkvn, dproj)
    dproj, dk, dv, dbias, dsinks = attn_bwd(proj, bias, sinks, dya, dproj)
    dproj = lax.dynamic_update_slice(dproj, jnp.concatenate([dk, dv], axis=2), (0, 0, CB_AK * LANE)).reshape(T, NP)
    g_wp = mm(dproj, u1.reshape(T, D), "tn", bf16, "proj_dw", tm=1664, tk=1024)
    started = send_off(dict(w_in=_unpack_w_in(g_wp).reshape(NDEV, IN_DIM // NDEV, D), dn_conv_w=_cols_split(g_conv_dn)), "w_in")
    du1 = mm(dproj, wp, "nn", bf16, "proj_dx", tm=512, tk=NP).reshape(B, S, D)
    grad_x, dsc1, dsh1, dw_mix_pre = rowcall_bwd("mix_pre_bwd", functools.partial(f_rms_mod, rms=_rms_vjp), [(x, D, 0)], [sc1 + started, sh1], [norm_mix_pre],
                                                 [(du1, D, 0)], [(0, f32)], add=(dh1, D, 0))
    g_rel = mm(dbias.reshape(HQ, WIN * 2 * WIN), onehot, "nt", f32, "rel_bias_dw", tk=8192, precision=HI)

    dmod = jnp.concatenate([dsh1, dsc1, dg1, dsh2, dsc2, dg2], axis=2).reshape(B, NMOD * D)

    zrow = lambda a: jnp.concatenate([a.reshape(1, -1), jnp.zeros((B - 1, a.size), f32)], axis=0)
    small_g = jnp.concatenate([
        dmod, dw_mix_pre.reshape(B, D), dw_mix_post.reshape(B, D), dw_ffn_pre.reshape(B, D), dw_ffn_post.reshape(B, D),
        da_log_pad.reshape(B, LANE)[:, DNH:2 * DNH], ddt_bias_pad.reshape(B, LANE)[:, DNH:2 * DNH], dw_dn_norm.reshape(B, DND),
        zrow(dsinks), zrow(g_rel.T), loss_b.reshape(B, LANE)[:, :1], jnp.zeros((B, SMALL_PAD - SMALL_N - 1), f32)], axis=1)
    (small_all,) = _exchange([small_g], "gather_small")
    dmod_cols = lax.dynamic_slice_in_dim(small_all.reshape(NDEV * B, SMALL_PAD), me * ncol, ncol, axis=1)
    g_ada_w = ada_bwd(c_all, dmod_cols)
    parts = {}
    for i, (d, started) in enumerate(in_flight):
        parts.update(zip(d, _copy_finish(started, len(d), g_ada_w, "scatter_finish_%d" % i)))
    small_w = dict(ada_b=(ada_b, m_ada_b, v_ada_b), norm_mix_pre=(norm_mix_pre, m_norm_mix_pre, v_norm_mix_pre),
                   norm_mix_post=(norm_mix_post, m_norm_mix_post, v_norm_mix_post), norm_ffn_pre=(norm_ffn_pre, m_norm_ffn_pre, v_norm_ffn_pre),
                   norm_ffn_post=(norm_ffn_post, m_norm_ffn_post, v_norm_ffn_post), dn_a_log=(dn_a_log, m_dn_a_log, v_dn_a_log),
                   dn_dt_bias=(dn_dt_bias, m_dn_dt_bias, v_dn_dt_bias), dn_norm_w=(dn_norm_w, m_dn_norm_w, v_dn_norm_w),
                   attn_sinks=(attn_sinks, m_attn_sinks, v_attn_sinks), rel_bias=(rel_bias, m_rel_bias, v_rel_bias))

    def pack(i, fill):
        row = jnp.concatenate([small_w[n][i].reshape(1, -1) for n, _ in SMALL], axis=1)
        return jnp.pad(row, ((0, 0), (0, SMALL_PAD - SMALL_N)), constant_values=fill)

    small_out = adamw(pack(0, 0.0), small_all.reshape(NDEV * B, 1, SMALL_PAD), pack(1, 0.0), pack(2, 1.0), "adamw_small")
    loss = small_out[0][0, SMALL_N]

    res = {}
    off = 0
    for n, size in SMALL:
        shp = small_w[n][0].shape
        res[n] = [o[:, off:off + size].reshape(shp) for o in small_out]
        off += size
    res["ada_w"] = [o[None] for o in adamw(ada_w[0], g_ada_w[None], m_ada_w[0], v_ada_w[0], "adamw_ada_w")]
    moments = dict(w_in=(m_w_in, v_w_in), dn_conv_w=(m_dn_conv_w, v_dn_conv_w), w_attn_branch=(m_w_attn_branch, v_w_attn_branch),
                   w_dn_branch=(m_w_dn_branch, v_w_dn_branch), w_out=(m_w_out, v_w_out), ffn_w_up=(m_ffn_w_up, v_ffn_w_up),
                   ffn_conv_w=(m_ffn_conv_w, v_ffn_conv_w), ffn_w_down=(m_ffn_w_down, v_ffn_w_down))
    for n in big_names:
        outs = adamw(local(n, big[n]), parts[n], local(n, moments[n][0]), local(n, moments[n][1]), "adamw_" + n)
        res[n] = [(o.T if n in transposed else o)[None] for o in outs]

    order = ["ada_w", "ada_b", "norm_mix_pre", "norm_mix_post", "norm_ffn_pre", "norm_ffn_post", "w_in", "dn_conv_w", "dn_a_log",
             "dn_dt_bias", "dn_norm_w", "attn_sinks", "rel_bias", "w_attn_branch", "w_dn_branch", "w_out", "ffn_w_up", "ffn_conv_w",
             "ffn_w_down"]
    return (loss, grad_x, *[res[n][0] for n in order], *[res[n][1] for n in order], *[res[n][2] for n in order],
            *[res[n][3] for n in order])
```

```python
import functools
import math

import numpy as np
import jax
import jax.numpy as jnp
from jax import lax
from jax.experimental import pallas as pl
from jax.experimental.pallas import tpu as pltpu

f32 = jnp.float32
bf16 = jnp.bfloat16
HI = lax.Precision.HIGHEST
MID = lax.Precision.HIGH
MESH = pl.DeviceIdType.MESH

NDEV = 8
D = 1024
HQ, HKV, HD, WIN, NBUCK, MAXDIST = 8, 2, 64, 128, 32, 128
DNH, DND, DNK, CH = 4, 128, 4, 64
DFF, FK = 2816, 3
NMOD = 6
RMS_EPS = 1e-6
L2_EPS = 1e-6
NEG_INF = -1e30
LR, B1, B2, EPS, WD, STEP = 0.001, 0.9, 0.999, 1e-08, 0.01, 10

LANE = 128
CB_GA, CB_GD, CB_AQ, CB_DQKV, CB_DZ, CB_AK, CB_AV, CB_BA, NPB = 0, 8, 16, 20, 32, 36, 37, 38, 39
NP = NPB * LANE
IN_SPLITS = (HQ * HD, HKV * HD, HKV * HD, 3 * DNH * DND, DNH * DND, DNH, DNH, D, D)
IN_DIM = sum(IN_SPLITS)
VMEM_LIMIT = 56 * 1024 * 1024

SMALL = (("ada_b", NMOD * D), ("norm_mix_pre", D), ("norm_mix_post", D), ("norm_ffn_pre", D), ("norm_ffn_post", D),
         ("dn_a_log", DNH), ("dn_dt_bias", DNH), ("dn_norm_w", DND), ("attn_sinks", HQ), ("rel_bias", NBUCK * HQ))
SMALL_N = sum(n for _, n in SMALL)
SMALL_PAD = 10752


def _cp(sem):
    return pltpu.CompilerParams(dimension_semantics=sem, vmem_limit_bytes=VMEM_LIMIT)


def _pick(dim, target):
    if dim <= target:
        return dim
    best = None
    for d in range(LANE, target + 1, LANE):
        if dim % d == 0:
            best = d
    assert best is not None, (dim, target)
    return best


def _me():
    x, y, c = lax.axis_index("x"), lax.axis_index("y"), lax.axis_index("c")
    return x, y, c, 4 * x + 2 * y + c


def _peer(x, y, c, k):
    px = 1 - x if k & 4 else x
    py = 1 - y if k & 2 else y
    pc = 1 - c if k & 1 else c
    return (px, py, pc), 4 * px + 2 * py + pc


class _Comm:
    def __init__(self, arrs, two_level=False):
        self.arrs, self.n, self.two_level = list(arrs), len(arrs), two_level
        self.out_shape = [jax.ShapeDtypeStruct((NDEV,) + a.shape, a.dtype) for a in arrs]
        nsem = self.n * (NDEV - 1)
        self.scratch = [pltpu.SemaphoreType.DMA((nsem,)), pltpu.SemaphoreType.DMA((nsem,)), pltpu.SemaphoreType.DMA((self.n,))]
        self.specs = [pl.BlockSpec(memory_space=pl.ANY)] * self.n

    def phases(self, ins, out, send, recv, loc):
        x, y, c, me = _me()

        def remote(a, k, src, dst, to):
            s = a * (NDEV - 1) + k - 1
            return pltpu.make_async_remote_copy(src_ref=src, dst_ref=dst, send_sem=send.at[s], recv_sem=recv.at[s],
                                                device_id=to, device_id_type=MESH)

        def local(a):
            return pltpu.make_async_copy(ins[a], out[a].at[me], loc.at[a])

        if not self.two_level:
            def mine(a, k):
                peer, pid = _peer(x, y, c, k)
                return remote(a, k, ins[a], out[a].at[me], peer)

            def theirs(a, k):
                peer, pid = _peer(x, y, c, k)
                return remote(a, k, ins[a], out[a].at[pid], peer)

            def start():
                for a in range(self.n):
                    local(a).start()
                    for k in range(1, NDEV):
                        mine(a, k).start()

            def forward():
                pass

            def finish():
                for a in range(self.n):
                    for k in range(1, NDEV):
                        mine(a, k).wait_send()
                    for k in range(1, NDEV):
                        theirs(a, k).wait_recv()
                    local(a).wait()

            return start, forward, finish

        sibling = (x, y, 1 - c)
        chips = [(1 - x, y), (x, 1 - y), (1 - x, 1 - y)]
        slot = lambda px, py, pc: 4 * px + 2 * py + pc

        def own(a, k, to):
            return remote(a, k, ins[a], out[a].at[me], to)

        def landed(a, k, frm):
            return remote(a, k, ins[a], out[a].at[slot(*frm)], frm)

        def passed(a, j):
            rows = out[a].at[slot(*chips[j], c)]
            return remote(a, 5 + j, rows, rows, sibling)

        def start():
            for a in range(self.n):
                local(a).start()
                own(a, 1, sibling).start()
                for j, chip in enumerate(chips):
                    own(a, 2 + j, (*chip, c)).start()

        def forward():
            for a in range(self.n):
                for j, chip in enumerate(chips):
                    landed(a, 2 + j, (*chip, c)).wait_recv()
                    passed(a, j).start()

        def finish():
            for a in range(self.n):
                landed(a, 1, sibling).wait_recv()
                for j, chip in enumerate(chips):
                    remote(a, 5 + j, ins[a], out[a].at[slot(*chip, 1 - c)], sibling).wait_recv()
                own(a, 1, sibling).wait_send()
                for j, chip in enumerate(chips):
                    own(a, 2 + j, (*chip, c)).wait_send()
                    passed(a, j).wait_send()
                local(a).wait()

        return start, forward, finish


def _copy_start(arrs, name, gather=False, after=None):
    n = len(arrs)
    order = [] if after is None else [after]
    n_in = 2 * n + len(order)
    block = (lambda ref, j: ref) if gather else (lambda ref, j: ref.at[j])

    def body(*refs):
        ins, lands, send, recv, own, token = refs[:n], refs[n:2 * n], refs[n_in], refs[n_in + 1], refs[n_in + 2], refs[-1]
        x, y, c, me = _me()
        for a in range(n):
            pltpu.make_async_copy(block(ins[a], me), lands[a].at[me], own.at[a]).start()
            for k in range(1, NDEV):
                peer, pid = _peer(x, y, c, k)
                s = a * (NDEV - 1) + k - 1
                pltpu.make_async_remote_copy(src_ref=block(ins[a], pid), dst_ref=lands[a].at[me], send_sem=send.at[s],
                                             recv_sem=recv.at[s], device_id=peer, device_id_type=MESH).start()
        token[...] = jnp.zeros(token.shape, token.dtype)

    hbm, sem = pl.BlockSpec(memory_space=pltpu.HBM), pl.BlockSpec(memory_space=pltpu.SEMAPHORE)
    nsem = n * (NDEV - 1)
    land_shapes = [((NDEV,) + a.shape if gather else a.shape) for a in arrs]
    thru = [pltpu.HBM(a.shape, a.dtype) for a in arrs] + [pltpu.HBM(shp, a.dtype) for shp, a in zip(land_shapes, arrs)]
    return pl.pallas_call(
        body, name=name, in_specs=[hbm] * (2 * n) + [pl.BlockSpec(memory_space=pl.ANY)] * len(order),
        out_shape=(pltpu.SemaphoreType.DMA((nsem,)), pltpu.SemaphoreType.DMA((nsem,)), pltpu.SemaphoreType.DMA((n,)), *thru,
                   jax.ShapeDtypeStruct((8, LANE), f32)),
        out_specs=(sem, sem, sem, *[hbm] * (2 * n), pl.BlockSpec(memory_space=pltpu.VMEM)),
        input_output_aliases={i: 3 + i for i in range(2 * n)},
        compiler_params=pltpu.CompilerParams(has_side_effects=pltpu.SideEffectType.DATAFLOW_SIDE_EFFECTING),
    )(*[pltpu.with_memory_space_constraint(a, pltpu.HBM) for a in arrs],
      *[pltpu.with_memory_space_constraint(lax.empty(shp, a.dtype), pltpu.HBM) for shp, a in zip(land_shapes, arrs)], *order)


def _copy_finish(started, n, after, name, gather=False):
    send, recv, own, *rest = started
    srcs, lands = rest[:n], rest[n:2 * n]
    block = (lambda ref, j: ref) if gather else (lambda ref, j: ref.at[j])

    def body(*refs):
        ins, lnd, send_ref, recv_ref, own_ref = refs[:n], refs[n:2 * n], refs[2 * n], refs[2 * n + 1], refs[2 * n + 2]
        x, y, c, me = _me()
        for a in range(n):
            pltpu.make_async_copy(block(ins[a], me), lnd[a].at[me], own_ref.at[a]).wait()
            for k in range(1, NDEV):
                peer, pid = _peer(x, y, c, k)
                s = a * (NDEV - 1) + k - 1
                cp = pltpu.make_async_remote_copy(src_ref=block(ins[a], pid), dst_ref=lnd[a].at[pid], send_sem=send_ref.at[s],
                                                  recv_sem=recv_ref.at[s], device_id=peer, device_id_type=MESH)
                cp.wait_send()
                cp.wait_recv()

    hbm, sem = pl.BlockSpec(memory_space=pltpu.HBM), pl.BlockSpec(memory_space=pltpu.SEMAPHORE)
    thru = [pltpu.HBM(a.shape, a.dtype) for a in srcs] + [pltpu.HBM(a.shape, a.dtype) for a in lands]
    out = pl.pallas_call(
        body, name=name, in_specs=[hbm] * (2 * n) + [sem, sem, sem, pl.BlockSpec(memory_space=pl.ANY)],
        out_shape=tuple(thru), out_specs=tuple([hbm] * (2 * n)), input_output_aliases={i: i for i in range(2 * n)},
        compiler_params=pltpu.CompilerParams(has_side_effects=pltpu.SideEffectType.DATAFLOW_SIDE_EFFECTING),
    )(*srcs, *lands, send, recv, own, after)
    return list(out[n:])


def _exchange(arrs, name, two_level=False):
    comm = _Comm(arrs, two_level)

    def body(*refs):
        start, forward, finish = comm.phases(refs[:comm.n], refs[comm.n:2 * comm.n], *refs[2 * comm.n:])
        start()
        forward()
        finish()

    return pl.pallas_call(body, name=name, out_shape=comm.out_shape, in_specs=comm.specs, out_specs=comm.specs,
                          scratch_shapes=comm.scratch, compiler_params=pltpu.CompilerParams(has_side_effects=True))(*arrs)


def mm(a, b, mode, out_dtype, name, tm=1024, tn=1024, tk=1024, precision=None, b_cols=None):
    a_parts = a.shape[0] if a.ndim == 3 else 1
    b_parts = b.shape[0] if b.ndim == 3 else 1
    assert b_parts == 1 or mode == "tn"
    ash, bsh = (a.shape[-2], a.shape[-1] * a_parts), b.shape[-2:]
    if mode == "nn":
        (M, K), (K2, N) = ash, bsh
    elif mode == "nt":
        (M, K), (N, K2) = ash, bsh
    else:
        (K, M), (K2, N) = ash, (bsh[0], bsh[1] * b_parts)
    assert K == K2, (name, a.shape, b.shape)
    col0 = 0
    if b_cols is not None:
        assert mode in ("nn", "nt") and tn % LANE == 0
        col0, N = b_cols[0], b_cols[1] * tn
    if mode == "tn":
        tm, tn, tk = _pick(M // a_parts, tm), _pick(N // b_parts, tn), _pick(K, tk)
    else:
        tm, tn, tk = _pick(M, tm), _pick(N // b_parts, tn), _pick(K // a_parts, tk)
    nk = K // tk
    if mode == "tn" and a_parts > 1:
        per = M // tm // a_parts
        a_spec = pl.BlockSpec((None, tk, tm), lambda i, j, k: (i // per, k, i % per))
    elif mode == "tn":
        a_spec = pl.BlockSpec((tk, tm), lambda i, j, k: (k, i))
    elif a_parts > 1:
        per = nk // a_parts
        a_spec = pl.BlockSpec((None, tm, tk), lambda i, j, k: (k // per, i, k % per))
    else:
        a_spec = pl.BlockSpec((tm, tk), lambda i, j, k: (i, k))
    if mode == "nt":
        b_spec = pl.BlockSpec((tn, tk), lambda i, j, k: (col0 + j, k))
    elif b_parts > 1:
        per = N // tn // b_parts
        b_spec = pl.BlockSpec((None, tk, tn), lambda i, j, k: (j // per, k, j % per))
    else:
        b_spec = pl.BlockSpec((tk, tn), lambda i, j, k: (k, col0 + j))
    dims = {"nn": ((1,), (0,)), "nt": ((1,), (1,)), "tn": ((0,), (0,))}[mode]

    def body(a_ref, b_ref, o_ref, *scr):
        p = lax.dot_general(a_ref[...], b_ref[...], (dims, ((), ())), preferred_element_type=f32, precision=precision)
        if nk == 1:
            o_ref[...] = p.astype(o_ref.dtype)
        else:
            acc = scr[0]
            k = pl.program_id(2)

            @pl.when(k == 0)
            def _():
                acc[...] = p

            @pl.when(k > 0)
            def _():
                acc[...] += p

            @pl.when(k == nk - 1)
            def _():
                o_ref[...] = acc[...].astype(o_ref.dtype)

    return pl.pallas_call(
        body, name=name, grid=(M // tm, N // tn, nk), in_specs=[a_spec, b_spec],
        out_specs=pl.BlockSpec((tm, tn), lambda i, j, k: (i, j)), out_shape=jax.ShapeDtypeStruct((M, N), out_dtype),
        scratch_shapes=[pltpu.VMEM((tm, tn), f32)] if nk > 1 else [],
        compiler_params=_cp(("parallel", "parallel", "arbitrary")),
    )(a, b)


ROW_TILE = 512


def rowcall(name, fn, tok, bat, con, tok_out, acc_out, ts=ROW_TILE, into=None):
    B, S = tok[0][0].shape[:2]
    ts = min(ts, S)
    nt, nb, nc, no, na = len(tok), len(bat), len(con), len(tok_out), len(acc_out)
    nin = nt + nb + nc + (1 if into is not None else 0)

    def body(*refs):
        tr, br, cr = refs[:nt], refs[nt:nt + nb], refs[nt + nb:nt + nb + nc]
        orf, arf = refs[nin:nin + no], refs[nin + no:]
        touts, aouts = fn([r[0] for r in tr], [r[0] for r in br], [r[...] for r in cr])
        for r, v in zip(orf, touts):
            r[0] = v.astype(r.dtype)
        s = pl.program_id(1)
        for r, v in zip(arf, aouts):
            @pl.when(s == 0)
            def _(r=r):
                r[...] = jnp.zeros(r.shape, r.dtype)
            r[0] += v.astype(f32)

    in_specs = [pl.BlockSpec((1, ts, w), lambda b, s, cb=cb: (b, s, cb)) for (_, w, cb) in tok]
    in_specs += [pl.BlockSpec((1,) + a.shape[1:], lambda b, s: (b, 0, 0)) for a in bat]
    in_specs += [pl.BlockSpec(a.shape, lambda b, s, nd=a.ndim: (0,) * nd) for a in con]
    out_specs = [pl.BlockSpec((1, ts, w), lambda b, s: (b, s, 0)) for (w, _) in tok_out]
    out_specs += [pl.BlockSpec((1,) + shp, lambda b, s, nd=len(shp): (b,) + (0,) * nd) for shp in acc_out]
    out_shape = [jax.ShapeDtypeStruct((B, S, w), dt) for (w, dt) in tok_out]
    out_shape += [jax.ShapeDtypeStruct((B,) + shp, f32) for shp in acc_out]
    extra, aliases = [], {}
    if into is not None:
        buf, cb = into
        assert buf.dtype == tok_out[0][1]
        in_specs.append(pl.BlockSpec(memory_space=pl.ANY))
        out_specs[0] = pl.BlockSpec((1, ts, tok_out[0][0]), lambda b, s: (b, s, cb))
        out_shape[0] = jax.ShapeDtypeStruct(buf.shape, buf.dtype)
        extra, aliases = [buf], {nin - 1: 0}
    return pl.pallas_call(
        body, name=name, grid=(B, S // ts), in_specs=in_specs, out_specs=out_specs, out_shape=out_shape,
        input_output_aliases=aliases, compiler_params=_cp(("parallel", "arbitrary")),
    )(*[t[0] for t in tok], *bat, *con, *extra)


def rowcall_fwd(name, f, tok, bat, con, tok_out, ts=2 * ROW_TILE):
    def fn(t, b, c):
        return f([v.astype(f32) for v in t], b, c), []
    return rowcall(name, fn, tok, bat, con, tok_out, [], ts)


def rowcall_bwd(name, f, tok, bat, con, cts, tok_grads, add=None, ts=ROW_TILE, join_first=1, into=None):
    nt, ncts = len(tok), len(cts)

    def fn(t, b, c):
        prim = [v.astype(f32) for v in t[:nt]]
        ct = [v.astype(f32) for v in t[nt:nt + ncts]]
        _, vjp = jax.vjp(lambda tt, bb, cc: f(tt, bb, cc), prim, b, c)
        dt, db, dc = vjp(ct)
        touts = [dt[i] for i, _ in tok_grads]
        if add is not None:
            touts[0] = touts[0] + t[nt + ncts].astype(f32)
        if join_first > 1:
            touts = [jnp.concatenate(touts[:join_first], axis=1)] + touts[join_first:]
        return touts, list(db) + list(dc)

    all_tok = list(tok) + list(cts) + ([add] if add is not None else [])
    tok_out = [(tok[i][1], dt) for i, dt in tok_grads]
    if join_first > 1:
        tok_out = [(sum(w for w, _ in tok_out[:join_first]), tok_out[0][1])] + tok_out[join_first:]
    acc_out = [tuple(a.shape[1:]) for a in bat] + [tuple(a.shape) for a in con]
    return rowcall(name, fn, all_tok, bat, con, tok_out, acc_out, ts, into)


def _rms(y, w):
    return y * lax.rsqrt(jnp.mean(y * y, axis=-1, keepdims=True) + RMS_EPS) * w


@jax.custom_vjp
def _rms_vjp(y, w):
    return _rms(y, w)


def _rms_vjp_fwd(y, w):
    r = lax.rsqrt(jnp.mean(y * y, axis=-1, keepdims=True) + RMS_EPS)
    yhat = y * r
    return yhat * w, (yhat, r, w)


def _rms_vjp_bwd(res, g):
    yhat, r, w = res
    gw = g * w
    return r * (gw - yhat * jnp.mean(gw * yhat, axis=-1, keepdims=True)), jnp.sum(g * yhat, axis=0, keepdims=True)


_rms_vjp.defvjp(_rms_vjp_fwd, _rms_vjp_bwd)


def f_rms_mod(t, b, c, rms=_rms):
    return [rms(t[0], c[0]) * (1.0 + b[0]) + b[1]]


def f_post_pre(t, b, c, rms=_rms):
    h1 = t[0] + b[0] * rms(t[1], c[0])
    return [h1, rms(h1, c[1]) * (1.0 + b[1]) + b[2]]


def f_merge(t, b, c):
    ga, gd, ya, yd = t
    return [jax.nn.sigmoid(ga) * ya + jax.nn.sigmoid(gd) * yd]


def f_dnout(t, b, c, rms=_rms):
    o, z = t
    outs = []
    for h in range(DNH):
        sl = slice(h * DND, (h + 1) * DND)
        zh = z[:, sl]
        outs.append(rms(o[:, sl], c[0]) * (zh * jax.nn.sigmoid(zh)))
    return [jnp.concatenate(outs, axis=1)]


def _softplus(x):
    return jnp.maximum(x, 0.0) + jnp.log(1.0 + jnp.exp(-jnp.abs(x)))


def f_gate(t, b, c):
    ba = t[0]
    a_log, dt_bias = c
    lane = lax.broadcasted_iota(jnp.int32, ba.shape, 1)
    beta = jax.nn.sigmoid(ba)
    g = -jnp.exp(a_log) * _softplus(ba + dt_bias)
    return [jnp.where(lane < DNH, beta, jnp.where(lane < 2 * DNH, g, 0.0))]


def _bucket_table():
    qi = np.arange(WIN)[:, None]
    kj = np.arange(2 * WIN)[None, :]
    dist = np.maximum(WIN + qi - kj, 0)
    max_exact = NBUCK // 2
    scaled = np.log(np.maximum(dist, 1).astype(np.float64) / max_exact) / math.log(MAXDIST / max_exact)
    large = np.minimum(max_exact + (scaled * (NBUCK - max_exact)).astype(np.int32), NBUCK - 1)
    return np.where(dist < max_exact, dist, large).astype(np.int32)


def _attn_mask(n):
    qi = lax.broadcasted_iota(jnp.int32, (WIN, 2 * WIN), 0)
    kj = lax.broadcasted_iota(jnp.int32, (WIN, 2 * WIN), 1)
    dist = WIN + qi - kj
    return (dist >= 0) & (dist < WIN) & ((kj >= WIN) | (n > 0))


def _swap_halves(x):
    return pltpu.roll(x, HD, axis=x.ndim - 1)


@jax.custom_vjp
def _swap_halves_vjp(x):
    return _swap_halves(x)


_swap_halves_vjp.defvjp(lambda x: (_swap_halves(x), None), lambda _, g: (_swap_halves(g),))


def _sink_softmax(s, sinks):
    m = jnp.maximum(jnp.max(s, axis=-1, keepdims=True), sinks)
    p = jnp.exp(s - m)
    return p / (jnp.sum(p, axis=-1, keepdims=True) + jnp.exp(sinks - m))


@jax.custom_vjp
def _sink_softmax_vjp(s, sinks):
    return _sink_softmax(s, sinks)


def _sink_softmax_fwd(s, sinks):
    m = jnp.maximum(jnp.max(s, axis=-1, keepdims=True), sinks)
    p = jnp.exp(s - m)
    sink = jnp.exp(sinks - m)
    inv = 1.0 / (jnp.sum(p, axis=-1, keepdims=True) + sink)
    return p * inv, (p * inv, sink * inv)


def _sink_softmax_bwd(res, g):
    probs, sink_prob = res
    d = jnp.sum(g * probs, axis=-1, keepdims=True)
    return probs * (g - d), -jnp.sum(sink_prob * d, axis=(0, 2)).reshape(HQ, 1, 1)


_sink_softmax_vjp.defvjp(_sink_softmax_fwd, _sink_softmax_bwd)


def _attn_block(q, kp, kc, vp, vc, bias, sinks, mask, differentiated):
    dot = _bdot_bf16_vjp if differentiated else _bdot_bf16
    swap = _swap_halves_vjp if differentiated else _swap_halves
    B, grp = q.shape[0], HQ // HKV
    upper = lax.broadcasted_iota(jnp.int32, (2 * WIN, LANE), 1) >= HD

    def placed(natural, swapped, j, half):
        keep = upper if half == 1 else ~upper
        return jnp.where(keep, natural if j == half else swapped, 0.0)

    qh, ks, vs = [], [], []
    for b in range(B):
        kb, vb = jnp.concatenate([kp[b], kc[b]], axis=0), jnp.concatenate([vp[b], vc[b]], axis=0)
        kb_sw, vb_sw = swap(kb), swap(vb)
        for h in range(HQ):
            qh.append(q[b, :, (h // 2) * LANE:(h // 2 + 1) * LANE])
            ks.append(placed(kb, kb_sw, h // grp, h % 2))
            vs.append(placed(vb, vb_sw, h // grp, h % 2))
    s = dot(_stack(qh), _stack(ks), 2, 2).reshape(B, HQ, WIN, 2 * WIN) * (HD ** -0.5)
    probs = (_sink_softmax_vjp if differentiated else _sink_softmax)(jnp.where(mask, s + bias, NEG_INF), sinks)
    o = dot(probs.reshape(B * HQ, WIN, 2 * WIN), _stack(vs), 2, 1)
    return _stack([jnp.concatenate([o[b * HQ + 2 * i] + o[b * HQ + 2 * i + 1] for i in range(HQ // 2)], axis=1) for b in range(B)])


def _attn_specs(B, NB):
    last = NB - 1
    return [
        pl.BlockSpec((B, WIN, HQ * HD), lambda n: (0, jnp.minimum(n, last), CB_AQ // 4)),
        pl.BlockSpec((B, WIN, LANE), lambda n: (0, jnp.clip(n - 1, 0, last), CB_AK)),
        pl.BlockSpec((B, WIN, LANE), lambda n: (0, jnp.minimum(n, last), CB_AK)),
        pl.BlockSpec((B, WIN, LANE), lambda n: (0, jnp.clip(n - 1, 0, last), CB_AV)),
        pl.BlockSpec((B, WIN, LANE), lambda n: (0, jnp.minimum(n, last), CB_AV)),
        pl.BlockSpec((HQ, WIN, 2 * WIN), lambda n: (0, 0, 0)),
        pl.BlockSpec((HQ, 1, 1), lambda n: (0, 0, 0)),
    ]


def attn_fwd(proj, bias, sinks):
    B, S, _ = proj.shape
    NB = S // WIN

    def body(q, kp, kc, vp, vc, bias_ref, sink_ref, o_ref):
        mask = _attn_mask(pl.program_id(0))
        o = _attn_block(*[r[...].astype(f32) for r in (q, kp, kc, vp, vc)], bias_ref[...], sink_ref[...], mask, False)
        o_ref[...] = o.astype(o_ref.dtype)

    return pl.pallas_call(
        body, name="attn_fwd", grid=(NB,), in_specs=_attn_specs(B, NB),
        out_specs=pl.BlockSpec((B, WIN, HQ * HD), lambda n: (0, n, 0)), out_shape=jax.ShapeDtypeStruct((B, S, HQ * HD), bf16),
        compiler_params=_cp(("parallel",)),
    )(proj, proj, proj, proj, proj, bias, sinks)


def attn_bwd(proj, bias, sinks, dy, dproj):
    B, S, _ = proj.shape
    NB = S // WIN
    last = NB - 1

    def body(q, kp, kc, vp, vc, bias_ref, sink_ref, dy_ref, _, dq_ref, dk_ref, dv_ref, dbias_ref, dsink_ref, kcar, vcar):
        n = pl.program_id(0)

        @pl.when(n == 0)
        def _():
            dbias_ref[...] = jnp.zeros(dbias_ref.shape, f32)
            dsink_ref[...] = jnp.zeros(dsink_ref.shape, f32)
            kcar[...] = jnp.zeros(kcar.shape, f32)
            vcar[...] = jnp.zeros(vcar.shape, f32)

        @pl.when(n < NB)
        def _():
            mask = _attn_mask(n)
            _, vjp = jax.vjp(lambda *a: _attn_block(*a, mask, True), *[r[...].astype(f32) for r in (q, kp, kc, vp, vc)],
                             bias_ref[...], sink_ref[...])
            dq, dkp, dkc, dvp, dvc, dbias, dsink = vjp(dy_ref[...].astype(f32))
            dq_ref[...] = dq.astype(dq_ref.dtype)
            dbias_ref[...] += dbias
            dsink_ref[...] += dsink
            dk_ref[...] = (kcar[...] + dkp).astype(dk_ref.dtype)
            dv_ref[...] = (vcar[...] + dvp).astype(dv_ref.dtype)
            kcar[...] = dkc
            vcar[...] = dvc

        @pl.when(n == NB)
        def _():
            dk_ref[...] = kcar[...].astype(dk_ref.dtype)
            dv_ref[...] = vcar[...].astype(dv_ref.dtype)

    in_specs = _attn_specs(B, NB) + [pl.BlockSpec((B, WIN, HQ * HD), lambda n: (0, jnp.minimum(n, last), 0)),
                                     pl.BlockSpec(memory_space=pl.ANY)]
    kv_out = pl.BlockSpec((B, WIN, LANE), lambda n: (0, jnp.maximum(n - 1, 0), 0))
    return pl.pallas_call(
        body, name="attn_bwd", grid=(NB + 1,), in_specs=in_specs, input_output_aliases={8: 0},
        out_specs=[pl.BlockSpec((B, WIN, HQ * HD), lambda n: (0, jnp.minimum(n, last), CB_AQ // 4)), kv_out, kv_out,
                   pl.BlockSpec((HQ, WIN, 2 * WIN), lambda n: (0, 0, 0)), pl.BlockSpec((HQ, 1, 1), lambda n: (0, 0, 0))],
        out_shape=[jax.ShapeDtypeStruct(dproj.shape, dproj.dtype), jax.ShapeDtypeStruct((B, S, LANE), bf16),
                   jax.ShapeDtypeStruct((B, S, LANE), bf16), jax.ShapeDtypeStruct((HQ, WIN, 2 * WIN), f32),
                   jax.ShapeDtypeStruct((HQ, 1, 1), f32)],
        scratch_shapes=[pltpu.VMEM((B, WIN, LANE), f32), pltpu.VMEM((B, WIN, LANE), f32)],
        compiler_params=_cp(("arbitrary",)),
    )(proj, proj, proj, proj, proj, bias, sinks, dy, dproj)


DN_ROWS, FFN_ROWS = 256, 32


def _stage_rows(dst, value):
    dst[0:8] = jnp.zeros((8, LANE), f32)
    dst[8:8 + value.shape[0]] = value


def _conv_rows(xs, w, width, r, rows):
    wins = [xs[pl.ds(r + 8 - (width - 1) + j, rows), :] for j in range(width)]
    out = w[0:1] * wins[0]
    for j in range(1, width):
        out = out + w[j:j + 1] * wins[j]
    return out, wins


def _fold8(v):
    return jnp.sum(v.reshape(v.shape[0] // 8, 8, LANE), axis=0)


def _conv_rows_t(ds, w, width, r, rows):
    out = w[0:1] * ds[pl.ds(r + width - 1, rows), :]
    for j in range(1, width):
        out = out + w[j:j + 1] * ds[pl.ds(r + width - 1 - j, rows), :]
    return out


def _dn_outblk(i):
    return (i % DNH) * 3 + i // DNH


def _dn_act(c, isqk):
    sg = jax.nn.sigmoid(c)
    y = c * sg
    n = lax.rsqrt(jnp.sum(y * y, axis=-1, keepdims=True) + L2_EPS)
    return jnp.where(isqk, y * n, y), sg, n


def dnconv_fwd(proj, conv_w):
    B, S, _ = proj.shape
    rows = min(DN_ROWS, S)

    def body(x_ref, w_ref, o_ref, xs):
        isqk = pl.program_id(0) < 2 * DNH
        _stage_rows(xs, x_ref[0].astype(f32))
        w = w_ref[...]
        for r in range(0, S, rows):
            c, _ = _conv_rows(xs, w, DNK, r, rows)
            o_ref[0, pl.ds(r, rows), :] = _dn_act(c, isqk)[0]

    return pl.pallas_call(
        body, name="dnconv_fwd", grid=(3 * DNH, B),
        in_specs=[pl.BlockSpec((1, S, LANE), lambda i, b: (b, 0, CB_DQKV + i)), pl.BlockSpec((DNK, LANE), lambda i, b: (0, i))],
        out_specs=pl.BlockSpec((1, S, LANE), lambda i, b: (b, 0, _dn_outblk(i))),
        out_shape=jax.ShapeDtypeStruct((B, S, 3 * DNH * DND), f32), scratch_shapes=[pltpu.VMEM((S + 8, LANE), f32)],
        compiler_params=_cp(("parallel", "parallel")),
    )(proj, conv_w)


def dnconv_bwd(proj, conv_w, dqkvn, dproj):
    B, S, _ = proj.shape
    rows = min(DN_ROWS, S)

    def body(x_ref, w_ref, dy_ref, _, dx_ref, dw_ref, xs, ds):
        isqk = pl.program_id(0) < 2 * DNH
        _stage_rows(xs, x_ref[0].astype(f32))
        w = w_ref[...]
        dw = [jnp.zeros((8, LANE), f32) for _ in range(DNK)]
        for r in range(0, S, rows):
            c, wins = _conv_rows(xs, w, DNK, r, rows)
            out, sg, n = _dn_act(c, isqk)
            dout = dy_ref[0, pl.ds(r, rows), :]
            dy = jnp.where(isqk, n * (dout - out * jnp.sum(dout * out, axis=-1, keepdims=True)), dout)
            dc = dy * (sg * (1.0 + c * (1.0 - sg)))
            ds[pl.ds(r, rows), :] = dc
            for j in range(DNK):
                dw[j] = dw[j] + _fold8(dc * wins[j])
        ds[S:S + 8] = jnp.zeros((8, LANE), f32)
        for r in range(0, S, rows):
            dx_ref[0, pl.ds(r, rows), :] = _conv_rows_t(ds, w, DNK, r, rows).astype(dx_ref.dtype)

        @pl.when(pl.program_id(1) == 0)
        def _():
            dw_ref[...] = jnp.zeros(dw_ref.shape, f32)
        dw_ref[...] += jnp.concatenate([jnp.sum(d, axis=0, keepdims=True) for d in dw], axis=0)

    return pl.pallas_call(
        body, name="dnconv_bwd", grid=(3 * DNH, B),
        in_specs=[pl.BlockSpec((1, S, LANE), lambda i, b: (b, 0, CB_DQKV + i)), pl.BlockSpec((DNK, LANE), lambda i, b: (0, i)),
                  pl.BlockSpec((1, S, LANE), lambda i, b: (b, 0, _dn_outblk(i))), pl.BlockSpec(memory_space=pl.ANY)],
        out_specs=[pl.BlockSpec((1, S, LANE), lambda i, b: (b, 0, CB_DQKV + i)), pl.BlockSpec((DNK, LANE), lambda i, b: (0, i))],
        out_shape=[jax.ShapeDtypeStruct(dproj.shape, dproj.dtype), jax.ShapeDtypeStruct((DNK, 3 * DNH * DND), f32)],
        scratch_shapes=[pltpu.VMEM((S + 8, LANE), f32), pltpu.VMEM((S + 8, LANE), f32)],
        input_output_aliases={3: 0}, compiler_params=_cp(("parallel", "arbitrary")),
    )(proj, conv_w, dqkvn, dproj)


def _bdot(a, b, ca, cb, precision=HI):
    return lax.dot_general(a, b, (((ca,), (cb,)), ((0,), (0,))), preferred_element_type=f32, precision=precision)


def _bdot_bf16(a, b, ca, cb):
    return _bdot(a.astype(bf16), b.astype(bf16), ca, cb, None)


@functools.partial(jax.custom_vjp, nondiff_argnums=(2, 3))
def _bdot_bf16_vjp(a, b, ca, cb):
    return _bdot_bf16(a, b, ca, cb)


def _bdot_bf16_fwd(a, b, ca, cb):
    return _bdot_bf16(a, b, ca, cb), (a, b)


def _bdot_bf16_bwd(ca, cb, res, g):
    a, b = res
    fa, fb = 3 - ca, 3 - cb
    da = _bdot_bf16(g, b, 2, fb) if ca == 2 else _bdot_bf16(b, g, fb, 2)
    db = _bdot_bf16(a, g, fa, 1) if cb == 1 else _bdot_bf16(g, a, 1, fa)
    return da, db


_bdot_bf16_vjp.defvjp(_bdot_bf16_fwd, _bdot_bf16_bwd)


def _neumann_inverse(low):
    n = low.shape[-1]
    eye = (lax.broadcasted_iota(jnp.int32, (n, n), 0) == lax.broadcasted_iota(jnp.int32, (n, n), 1)).astype(f32)
    p = -low
    x = eye[None] + p
    for _ in range(5):
        p = _bdot_bf16(p, p, 2, 1)
        x = x + _bdot_bf16(x, p, 2, 1)
    return x


@jax.custom_vjp
def _unit_lower_inverse(low):
    return _neumann_inverse(low)


def _uli_fwd(low):
    t = _neumann_inverse(low)
    return t, t


def _uli_bwd(t, dt):
    return (-_bdot_bf16(_bdot_bf16(t, dt, 1, 1), t, 2, 2),)


_unit_lower_inverse.defvjp(_uli_fwd, _uli_bwd)


def _stack(xs):
    return jnp.concatenate([x[None] for x in xs], axis=0)


DELTA_CHUNKS = 4


def _delta_chunks(qkv, bg, state, differentiated):
    inverse = _unit_lower_inverse if differentiated else _neumann_inverse
    lo = _bdot_bf16_vjp if differentiated else _bdot_bf16
    B, n = qkv.shape[0], qkv.shape[1] // CH
    G = B * DNH
    N = n * G
    triples = [(i, b, h) for i in range(n) for b in range(B) for h in range(DNH)]
    col = lambda i, b, h, kind: qkv[b, i * CH:(i + 1) * CH, (3 * h + kind) * DND:(3 * h + kind + 1) * DND]
    q, k, v = [_stack([col(i, b, h, kind) for i, b, h in triples]) for kind in range(3)]
    lane = lax.broadcasted_iota(jnp.int32, (CH, LANE), 1)
    pick = lambda i, b, l: jnp.sum(jnp.where(lane == l, bg[b, i * CH:(i + 1) * CH], 0.0), axis=1, keepdims=True)
    beta = _stack([pick(i, b, h) for i, b, h in triples])
    g = _stack([pick(i, b, h + DNH) for i, b, h in triples])
    ri = lax.broadcasted_iota(jnp.int32, (CH, CH), 0)
    ci = lax.broadcasted_iota(jnp.int32, (CH, CH), 1)
    incl, strict = (ri >= ci)[None], (ri > ci)[None]
    gc = _bdot(jnp.broadcast_to(incl.astype(f32), (N, CH, CH)), jnp.broadcast_to(g, (N, CH, LANE)), 2, 1, MID)
    e0 = jnp.broadcast_to((lane == 0).astype(f32)[None], (N, CH, LANE))
    gc_row = _bdot(e0, gc, 2, 2, MID)
    diff = gc[:, :, :CH] - gc_row
    decay = jnp.where(incl, jnp.exp(jnp.where(incl, diff, 0.0)), 0.0)
    qs = q * (DND ** -0.5)
    kb, vb = k * beta, v * beta
    eg = jnp.exp(gc)
    with_k = lo(jnp.concatenate([kb, qs], axis=1), k, 2, 2)
    low = jnp.where(strict, with_k[:, :CH] * decay, 0.0)
    intra = jnp.where(incl, with_k[:, CH:] * decay, 0.0)
    tinv = inverse(low)
    solved = lo(tinv, jnp.concatenate([vb, kb * eg], axis=2), 2, 1)
    gl = gc[:, CH - 1:CH, :]
    k_tail = k * jnp.exp(gl - gc)
    to_state = jnp.concatenate([solved[:, :, DND:], qs * eg], axis=1)
    decay_all = jnp.exp(gl)
    outs = []
    for i in range(n):
        sl = slice(i * G, (i + 1) * G)
        with_state = lo(to_state[sl], state, 2, 1)
        v_new = solved[sl, :, :DND] - with_state[:, :CH]
        outs.append(with_state[:, CH:] + lo(intra[sl], v_new, 2, 1))
        state = state * decay_all[sl] + lo(k_tail[sl], v_new, 1, 1)
    return outs, state


def delta_fwd(qkvn, bg):
    B, S, _ = qkvn.shape
    n = DELTA_CHUNKS if (S // CH) % DELTA_CHUNKS == 0 else 1
    steps, G, rows = S // (n * CH), B * DNH, n * CH

    def body(qkv_ref, bg_ref, o_ref, st_ref, state):
        @pl.when(pl.program_id(0) == 0)
        def _():
            state[...] = jnp.zeros(state.shape, f32)
        s0 = state[...]
        st_ref[0] = s0
        outs, s1 = _delta_chunks(qkv_ref[...], bg_ref[...], s0, False)
        for i, o in enumerate(outs):
            for b in range(B):
                for h in range(DNH):
                    o_ref[b, i * CH:(i + 1) * CH, h * DND:(h + 1) * DND] = o[b * DNH + h]
        state[...] = s1

    return pl.pallas_call(
        body, name="delta_fwd", grid=(steps,),
        in_specs=[pl.BlockSpec((B, rows, 3 * DNH * DND), lambda c: (0, c, 0)), pl.BlockSpec((B, rows, LANE), lambda c: (0, c, 0))],
        out_specs=[pl.BlockSpec((B, rows, DNH * DND), lambda c: (0, c, 0)), pl.BlockSpec((1, G, DND, DND), lambda c: (c, 0, 0, 0))],
        out_shape=[jax.ShapeDtypeStruct((B, S, DNH * DND), f32), jax.ShapeDtypeStruct((steps, G, DND, DND), f32)],
        scratch_shapes=[pltpu.VMEM((G, DND, DND), f32)], compiler_params=_cp(("arbitrary",)),
    )(qkvn, bg)


def delta_bwd(qkvn, bg, states, do):
    B, S, _ = qkvn.shape
    steps, G = states.shape[0], B * DNH
    rows = S // steps
    n = rows // CH

    def body(qkv_ref, bg_ref, st_ref, do_ref, dqkv_ref, dbg_ref, dstate):
        @pl.when(pl.program_id(0) == 0)
        def _():
            dstate[...] = jnp.zeros(dstate.shape, f32)
        _, vjp = jax.vjp(lambda a, g, s: _delta_chunks(a, g, s, True), qkv_ref[...], bg_ref[...], st_ref[0])
        do = [_stack([do_ref[b, i * CH:(i + 1) * CH, h * DND:(h + 1) * DND] for b in range(B) for h in range(DNH)]) for i in range(n)]
        dqkv, dbg, ds = vjp((do, dstate[...]))
        dqkv_ref[...] = dqkv
        dbg_ref[...] = dbg
        dstate[...] = ds

    rev = lambda c: steps - 1 - c
    return pl.pallas_call(
        body, name="delta_bwd", grid=(steps,),
        in_specs=[pl.BlockSpec((B, rows, 3 * DNH * DND), lambda c: (0, rev(c), 0)), pl.BlockSpec((B, rows, LANE), lambda c: (0, rev(c), 0)),
                  pl.BlockSpec((1, G, DND, DND), lambda c: (rev(c), 0, 0, 0)),
                  pl.BlockSpec((B, rows, DNH * DND), lambda c: (0, rev(c), 0))],
        out_specs=[pl.BlockSpec((B, rows, 3 * DNH * DND), lambda c: (0, rev(c), 0)), pl.BlockSpec((B, rows, LANE), lambda c: (0, rev(c), 0))],
        out_shape=[jax.ShapeDtypeStruct((B, S, 3 * DNH * DND), f32), jax.ShapeDtypeStruct((B, S, LANE), f32)],
        scratch_shapes=[pltpu.VMEM((G, DND, DND), f32)], compiler_params=_cp(("arbitrary",)),
    )(qkvn, bg, states, do)


GELU_C0, GELU_C1 = math.sqrt(2.0 / math.pi), 0.044715


def _ffn_specs(S):
    nblk = DFF // LANE
    return [pl.BlockSpec((1, S, LANE), lambda i, b: (b, 0, i)), pl.BlockSpec((1, S, LANE), lambda i, b: (b, 0, nblk + i)),
            pl.BlockSpec((FK, LANE), lambda i, b: (0, i)), pl.BlockSpec((FK, LANE), lambda i, b: (0, nblk + i))]


def ffnconv_fwd(up, conv_w):
    B, S, _ = up.shape
    rows = min(FFN_ROWS, S)

    def body(g_ref, v_ref, gw_ref, vw_ref, o_ref, xg, xv):
        _stage_rows(xg, g_ref[0].astype(f32))
        _stage_rows(xv, v_ref[0].astype(f32))
        gw, vw = gw_ref[...], vw_ref[...]
        for r in range(0, S, rows):
            g, _ = _conv_rows(xg, gw, FK, r, rows)
            v, _ = _conv_rows(xv, vw, FK, r, rows)
            t = jnp.tanh(GELU_C0 * (g * (1.0 + GELU_C1 * (g * g))))
            o_ref[0, pl.ds(r, rows), :] = (0.5 * g * (1.0 + t) * v).astype(o_ref.dtype)

    return pl.pallas_call(
        body, name="ffnconv_fwd", grid=(DFF // LANE, B), in_specs=_ffn_specs(S),
        out_specs=pl.BlockSpec((1, S, LANE), lambda i, b: (b, 0, i)), out_shape=jax.ShapeDtypeStruct((B, S, DFF), bf16),
        scratch_shapes=[pltpu.VMEM((S + 8, LANE), f32)] * 2, compiler_params=_cp(("parallel", "parallel")),
    )(up, up, conv_w, conv_w)


def ffnconv_bwd(up, conv_w, dact):
    B, S, _ = up.shape
    rows = min(FFN_ROWS, S)

    def body(g_ref, v_ref, gw_ref, vw_ref, dy_ref, dx_ref, dw_ref, xg, xv, dg, dv):
        _stage_rows(xg, g_ref[0].astype(f32))
        _stage_rows(xv, v_ref[0].astype(f32))
        gw, vw = gw_ref[...], vw_ref[...]
        dgw = [jnp.zeros((8, LANE), f32) for _ in range(FK)]
        dvw = [jnp.zeros((8, LANE), f32) for _ in range(FK)]
        for r in range(0, S, rows):
            g, gwins = _conv_rows(xg, gw, FK, r, rows)
            v, vwins = _conv_rows(xv, vw, FK, r, rows)
            g2 = g * g
            t = jnp.tanh(GELU_C0 * (g * (1.0 + GELU_C1 * g2)))
            half = 0.5 * (1.0 + t)
            dgelu = half + (0.5 * GELU_C0) * g * (1.0 - t * t) * (1.0 + (3.0 * GELU_C1) * g2)
            dy = dy_ref[0, pl.ds(r, rows), :].astype(f32)
            dvc = dy * (g * half)
            dgc = dy * v * dgelu
            dg[pl.ds(r, rows), :] = dgc
            dv[pl.ds(r, rows), :] = dvc
            for j in range(FK):
                dgw[j] = dgw[j] + _fold8(dgc * gwins[j])
                dvw[j] = dvw[j] + _fold8(dvc * vwins[j])
        dg[S:S + 8] = jnp.zeros((8, LANE), f32)
        dv[S:S + 8] = jnp.zeros((8, LANE), f32)
        for r in range(0, S, rows):
            dx_ref[0, 0, pl.ds(r, rows), :] = _conv_rows_t(dg, gw, FK, r, rows).astype(dx_ref.dtype)
            dx_ref[1, 0, pl.ds(r, rows), :] = _conv_rows_t(dv, vw, FK, r, rows).astype(dx_ref.dtype)

        @pl.when(pl.program_id(1) == 0)
        def _():
            dw_ref[...] = jnp.zeros(dw_ref.shape, f32)
        dw_ref[0] += jnp.concatenate([jnp.sum(d, axis=0, keepdims=True) for d in dgw], axis=0)
        dw_ref[1] += jnp.concatenate([jnp.sum(d, axis=0, keepdims=True) for d in dvw], axis=0)

    return pl.pallas_call(
        body, name="ffnconv_bwd", grid=(DFF // LANE, B),
        in_specs=_ffn_specs(S) + [pl.BlockSpec((1, S, LANE), lambda i, b: (b, 0, i))],
        out_specs=[pl.BlockSpec((2, 1, S, LANE), lambda i, b: (0, b, 0, i)), pl.BlockSpec((2, FK, LANE), lambda i, b: (0, 0, i))],
        out_shape=[jax.ShapeDtypeStruct((2, B, S, DFF), bf16), jax.ShapeDtypeStruct((2, FK, DFF), f32)],
        scratch_shapes=[pltpu.VMEM((S + 8, LANE), f32)] * 4, compiler_params=_cp(("parallel", "arbitrary")),
    )(up, up, conv_w, conv_w, dact)


def ada_fwd(c_all, ada_w, ada_b):
    def body(c_ref, w_ref, b_ref, o_ref):
        c = c_ref[...]
        act = (c * jax.nn.sigmoid(c)).astype(bf16)
        o_ref[...] = jnp.dot(act, w_ref[...].astype(bf16), preferred_element_type=f32) + b_ref[...]

    return pl.pallas_call(body, name="ada_fwd", out_shape=jax.ShapeDtypeStruct((c_all.shape[0], ada_w.shape[1]), f32),
                          compiler_params=pltpu.CompilerParams(vmem_limit_bytes=VMEM_LIMIT))(c_all, ada_w, ada_b)


def ada_bwd(c_all, dmod):
    def body(c_ref, d_ref, o_ref):
        c = c_ref[...]
        act = (c * jax.nn.sigmoid(c)).astype(bf16)
        o_ref[...] = lax.dot_general(act, d_ref[...].astype(bf16), (((0,), (0,)), ((), ())), preferred_element_type=f32)

    return pl.pallas_call(body, name="ada_bwd", out_shape=jax.ShapeDtypeStruct((c_all.shape[1], dmod.shape[1]), f32),
                          compiler_params=pltpu.CompilerParams(vmem_limit_bytes=VMEM_LIMIT))(c_all, dmod)


def loss_head(h1, y2, target, g2, w):
    def fn(t, b, c):
        h, y, tg = [v.astype(f32) for v in t]

        def loss_fn(h, y, g, w):
            e = h + g * _rms_vjp(y, w) - tg
            return 0.5 * jnp.sum(jnp.mean(e * e, axis=-1))

        loss, grads = jax.value_and_grad(loss_fn, argnums=(0, 1, 2, 3))(h, y, b[0], c[0])
        return [grads[0], grads[1]], [grads[2], grads[3], jnp.full((1, LANE), loss, f32)]

    return rowcall("loss_head", fn, [(h1, D, 0), (y2, D, 0), (target, D, 0)], [g2], [w], [(D, f32), (D, bf16)],
                   [(1, D), (1, D), (1, LANE)])


def adamw(w, gparts, m, v, name):
    R, C = w.shape
    P = gparts.shape[0]
    budget = 2 * 1024 * 1024
    tr, tc = R, C
    if R * C * 4 > budget and R % 8 == 0:
        tr = max(t for t in range(8, R + 1, 8) if R % t == 0 and t * C * 4 <= budget)
    elif R * C * 4 > budget:
        tc = max(t for t in range(LANE, C + 1, LANE) if C % t == 0 and R * t * 4 <= budget)

    def body(w_ref, g_ref, m_ref, v_ref, go, do, mo, vo):
        g = g_ref[0].astype(f32)
        for p in range(1, P):
            g = g + g_ref[p].astype(f32)
        m2 = B1 * m_ref[...] + (1.0 - B1) * g
        v2 = B2 * v_ref[...] + (1.0 - B2) * jnp.square(g)
        m_hat = m2 * (1.0 / (1.0 - B1 ** STEP))
        v_hat = v2 * (1.0 / (1.0 - B2 ** STEP))
        go[...] = g
        do[...] = -LR * (m_hat / (jnp.sqrt(v_hat) + EPS) + WD * w_ref[...])
        mo[...] = m2
        vo[...] = v2

    blk = pl.BlockSpec((tr, tc), lambda i, j: (i, j))
    return pl.pallas_call(
        body, name=name, grid=(R // tr, C // tc), in_specs=[blk, pl.BlockSpec((P, tr, tc), lambda i, j: (0, i, j)), blk, blk],
        out_specs=[blk] * 4, out_shape=[jax.ShapeDtypeStruct((R, C), f32)] * 4, compiler_params=_cp(("parallel", "parallel")),
    )(w, gparts, m, v)


def _pack_w_in(wt):
    aq, ak, av, dqkv, dz, dbeta, da, ga, gd = jnp.split(wt, np.cumsum(IN_SPLITS)[:-1].tolist(), axis=0)
    ba = jnp.pad(jnp.concatenate([dbeta, da], axis=0), ((0, LANE - 2 * DNH), (0, 0)))
    return jnp.concatenate([ga, gd, aq, dqkv, dz, ak, av, ba], axis=0)


def _unpack_w_in(p):
    row = lambda cb, n: p[cb * LANE: cb * LANE + n]
    ba = row(CB_BA, 2 * DNH)
    return jnp.concatenate([row(CB_AQ, HQ * HD), row(CB_AK, HKV * HD), row(CB_AV, HKV * HD), row(CB_DQKV, 3 * DNH * DND),
                            row(CB_DZ, DNH * DND), ba[:DNH], ba[DNH:], row(CB_GA, D), row(CB_GD, D)], axis=0)


def _cols_gathered(g):
    return g.transpose(1, 0, 2).reshape(g.shape[1], NDEV * g.shape[2])


def _cols_split(w):
    r = w.shape[0]
    return w.reshape(r, NDEV, w.shape[1] // NDEV).transpose(1, 0, 2)


def kernel(x, c, ada_w, ada_b, norm_mix_pre, norm_mix_post, norm_ffn_pre, norm_ffn_post, w_in, dn_conv_w, dn_a_log, dn_dt_bias, dn_norm_w, attn_sinks, rel_bias, w_attn_branch, w_dn_branch, w_out, ffn_w_up, ffn_conv_w, ffn_w_down, loss_target, m_ada_w, m_ada_b, m_norm_mix_pre, m_norm_mix_post, m_norm_ffn_pre, m_norm_ffn_post, m_w_in, m_dn_conv_w, m_dn_a_log, m_dn_dt_bias, m_dn_norm_w, m_attn_sinks, m_rel_bias, m_w_attn_branch, m_w_dn_branch, m_w_out, m_ffn_w_up, m_ffn_conv_w, m_ffn_w_down, v_ada_w, v_ada_b, v_norm_mix_pre, v_norm_mix_post, v_norm_ffn_pre, v_norm_ffn_post, v_w_in, v_dn_conv_w, v_dn_a_log, v_dn_dt_bias, v_dn_norm_w, v_attn_sinks, v_rel_bias, v_w_attn_branch, v_w_dn_branch, v_w_out, v_ffn_w_up, v_ffn_conv_w, v_ffn_w_down):
    B, S, _ = x.shape
    T = B * S
    me = 4 * lax.axis_index("x") + 2 * lax.axis_index("y") + lax.axis_index("c")
    big = dict(w_in=w_in, dn_conv_w=dn_conv_w, w_attn_branch=w_attn_branch, w_dn_branch=w_dn_branch, w_out=w_out,
               ffn_w_up=ffn_w_up, ffn_conv_w=ffn_conv_w, ffn_w_down=ffn_w_down)
    big_names = list(big)

    first, mid, late = ["w_in", "dn_conv_w"], ["w_attn_branch", "w_dn_branch", "w_out"], ["ffn_w_up", "ffn_conv_w", "ffn_w_down"]
    transposed = ("w_in", "ffn_w_up")
    local = lambda n, a: a[0].T if n in transposed else a[0]
    shard = lambda names: [local(n, big[n]).astype(bf16) for n in names]
    *got, c_all = _exchange(shard(first) + [c], "gather_w_in", two_level=True)
    gw = dict(zip(first, got))
    c_all = c_all.reshape(NDEV * B, D)

    wp = _pack_w_in(gw["w_in"].reshape(IN_DIM, D))
    conv_dn = _cols_gathered(gw["dn_conv_w"]).astype(f32)

    ncol = ada_w.shape[2]
    ada_b_mine = lax.dynamic_slice_in_dim(ada_b, me * ncol, ncol, axis=1)
    mod_cols = ada_fwd(c_all, ada_w[0], ada_b_mine)
    (mod_g,) = _exchange([mod_cols], "gather_mod")
    gathering_mid = _copy_start(shard(mid), "gather_branches_start", gather=True, after=mod_g)
    gathering_ffn = _copy_start(shard(late), "gather_ffn_start", gather=True, after=gathering_mid[-1])
    mod_g = mod_g + gathering_ffn[-1][0, 0]
    mod = lax.dynamic_slice_in_dim(mod_g, me * B, B, axis=1).transpose(1, 0, 2).reshape(B, NMOD * D)
    sh1, sc1, g1, sh2, sc2, g2 = [mod[:, i * D:(i + 1) * D].reshape(B, 1, D) for i in range(NMOD)]

    onehot = (jnp.asarray(_bucket_table()).reshape(1, -1) == jnp.arange(NBUCK, dtype=jnp.int32)[:, None]).astype(f32)
    bias = mm(rel_bias.T, onehot, "nn", f32, "bias_table", tn=8192, precision=HI).reshape(HQ, WIN, 2 * WIN)
    sinks = attn_sinks.reshape(HQ, 1, 1)
    a_log_pad = jnp.pad(dn_a_log, ((0, 0), (DNH, LANE - 2 * DNH)))
    dt_bias_pad = jnp.pad(dn_dt_bias, ((0, 0), (DNH, LANE - 2 * DNH)))

    (u1,) = rowcall_fwd("mix_pre", f_rms_mod, [(x, D, 0)], [sc1, sh1], [norm_mix_pre], [(D, bf16)])
    proj = mm(u1.reshape(T, D), wp, "nt", bf16, "proj", tm=512, tn=CB_BA * LANE, b_cols=(0, 1)).reshape(B, S, CB_BA * LANE)
    ba = mm(u1.reshape(T, D), wp, "nt", f32, "proj_ba", tn=LANE, b_cols=(CB_BA, 1)).reshape(B, S, LANE)
    ya = attn_fwd(proj, bias, sinks)
    qkvn = dnconv_fwd(proj, conv_dn)
    (bg,) = rowcall_fwd("dn_gate", f_gate, [(ba, LANE, 0)], [], [a_log_pad, dt_bias_pad], [(LANE, f32)])
    o_dn, states = delta_fwd(qkvn, bg)
    gw.update(zip(mid, _copy_finish(gathering_mid, len(mid), o_dn, "gather_branches_finish", gather=True)))
    wa = _cols_gathered(gw["w_attn_branch"])
    wd = _cols_gathered(gw["w_dn_branch"])
    wo = gw["w_out"].reshape(D, D)
    (yd,) = rowcall_fwd("dn_out", f_dnout, [(o_dn, DNH * DND, 0), (proj, DNH * DND, CB_DZ // 4)], [], [dn_norm_w], [(DNH * DND, bf16)])
    pa = mm(ya.reshape(T, HQ * HD), wa, "nn", bf16, "attn_branch").reshape(B, S, D)
    pd = mm(yd.reshape(T, DNH * DND), wd, "nn", bf16, "dn_branch").reshape(B, S, D)
    merge_tok = [(proj, D, CB_GA // 8), (proj, D, CB_GD // 8), (pa, D, 0), (pd, D, 0)]
    (merged,) = rowcall_fwd("merge", f_merge, merge_tok, [], [], [(D, bf16)])
    y1 = mm(merged.reshape(T, D), wo, "nn", bf16, "mix_out").reshape(B, S, D)
    post_pre = ([(x, D, 0), (y1, D, 0)], [g1, sc2, sh2], [norm_mix_post, norm_ffn_pre])
    h1, u2 = rowcall_fwd("mix_post_ffn_pre", f_post_pre, *post_pre, [(D, f32), (D, bf16)])
    gw.update(zip(late, _copy_finish(gathering_ffn, len(late), h1, "gather_ffn_finish", gather=True)))
    wup = gw["ffn_w_up"].reshape(2 * DFF, D)
    conv_ffn = _cols_gathered(gw["ffn_conv_w"]).astype(f32)
    wdown = gw["ffn_w_down"].reshape(DFF, D)
    up = mm(u2.reshape(T, D), wup, "nt", bf16, "ffn_up", tn=2816).reshape(B, S, 2 * DFF)
    act = ffnconv_fwd(up, conv_ffn)
    y2 = mm(act.reshape(T, DFF), wdown, "nn", bf16, "ffn_down", tk=2816).reshape(B, S, D)

    dh1_a, dy2, dg2, dw_ffn_post, loss_b = loss_head(h1, y2, loss_target, g2, norm_ffn_post)
    dy2f = dy2.reshape(T, D)
    dact = mm(dy2f, wdown, "nt", bf16, "ffn_down_dx", tn=2816).reshape(B, S, DFF)
    g_wdown = mm(act.reshape(T, DFF), dy2f, "tn", bf16, "ffn_down_dw", tm=1408, tk=2048)
    in_flight = []

    def send_off(d, tag):
        in_flight.append((d, _copy_start([a.astype(bf16) for a in d.values()], "scatter_" + tag + "_start")))
        return in_flight[-1][1][-1][0, 0]

    started = send_off(dict(ffn_w_down=g_wdown.reshape(NDEV, DFF // NDEV, D)), "ffn_down")
    dup, g_conv_ffn = ffnconv_bwd(up, conv_ffn + started, dact)
    dupf = dup.reshape(2, T, DFF)
    g_conv_ffn = g_conv_ffn.transpose(1, 0, 2).reshape(FK, 2 * DFF)
    du2 = mm(dupf, wup, "nn", bf16, "ffn_up_dx", tk=2816).reshape(B, S, D)
    g_wup = mm(dupf, u2.reshape(T, D), "tn", bf16, "ffn_up_dw", tm=1408, tk=2048)
    started = send_off(dict(ffn_w_up=g_wup.reshape(NDEV, 2 * DFF // NDEV, D), ffn_conv_w=_cols_split(g_conv_ffn)), "ffn_up")
    post_pre = (post_pre[0], [g1 + started, sc2, sh2], post_pre[2])
    dh1, dy1, dg1, dsc2, dsh2, dw_mix_post, dw_ffn_pre = rowcall_bwd(
        "mix_post_ffn_pre_bwd", functools.partial(f_post_pre, rms=_rms_vjp), *post_pre, [(dh1_a, D, 0), (du2, D, 0)], [(0, f32), (1, bf16)])
    dy1f = dy1.reshape(T, D)
    dmerged = mm(dy1f, wo, "nt", bf16, "mix_out_dx").reshape(B, S, D)
    g_wo = mm(merged.reshape(T, D), dy1f, "tn", bf16, "mix_out_dw", tk=2048)
    dproj = lax.empty((B, S, NP), bf16)
    dproj, dpa, dpd = rowcall_bwd("merge_bwd", f_merge, merge_tok, [], [], [(dmerged, D, 0)],
                                  [(0, bf16), (1, bf16), (2, bf16), (3, bf16)], join_first=2, into=(dproj, CB_GA // 16))
    dpaf, dpdf = dpa.reshape(T, D), dpd.reshape(T, D)
    dya = mm(dpaf, wa, "nt", bf16, "attn_branch_dx").reshape(B, S, HQ * HD)
    g_wa = mm(ya.reshape(T, HQ * HD), dpaf, "tn", bf16, "attn_branch_dw", tk=2048)
    dyd = mm(dpdf, wd, "nt", bf16, "dn_branch_dx").reshape(B, S, DNH * DND)
    g_wd = mm(yd.reshape(T, DNH * DND), dpdf, "tn", bf16, "dn_branch_dw", tk=2048)
    dproj, do_dn, dw_dn_norm = rowcall_bwd("dn_out_bwd", functools.partial(f_dnout, rms=_rms_vjp), [(o_dn, DNH * DND, 0), (proj, DNH * DND, CB_DZ // 4)], [], [dn_norm_w],
                                           [(dyd, DNH * DND, 0)], [(1, bf16), (0, f32)], into=(dproj, CB_DZ // 4))
    started = send_off(dict(w_attn_branch=_cols_split(g_wa), w_dn_branch=_cols_split(g_wd), w_out=g_wo.reshape(NDEV, D // NDEV, D)), "branches")
    dqkvn, dbg = delta_bwd(qkvn, bg + started, states, do_dn)
    dproj, da_log_pad, ddt_bias_pad = rowcall_bwd("dn_gate_bwd", f_gate, [(ba, LANE, 0)], [], [a_log_pad, dt_bias_pad],
                                                  [(dbg, LANE, 0)], [(0, bf16)], into=(dproj, CB_BA))
    dproj, g_conv_dn = dnconv_bwd(proj, conv_dn, dqkvn, dproj)
    dproj, dk, dv, dbias, dsinks = attn_bwd(proj, bias, sinks, dya, dproj)
    dproj = lax.dynamic_update_slice(dproj, jnp.concatenate([dk, dv], axis=2), (0, 0, CB_AK * LANE)).reshape(T, NP)
    g_wp = mm(dproj, u1.reshape(T, D), "tn", bf16, "proj_dw", tm=1664, tk=1024)
    started = send_off(dict(w_in=_unpack_w_in(g_wp).reshape(NDEV, IN_DIM // NDEV, D), dn_conv_w=_cols_split(g_conv_dn)), "w_in")
    du1 = mm(dproj, wp, "nn", bf16, "proj_dx", tm=512, tk=NP).reshape(B, S, D)
    grad_x, dsc1, dsh1, dw_mix_pre = rowcall_bwd("mix_pre_bwd", functools.partial(f_rms_mod, rms=_rms_vjp), [(x, D, 0)], [sc1 + started, sh1], [norm_mix_pre],
                                                 [(du1, D, 0)], [(0, f32)], add=(dh1, D, 0))
    g_rel = mm(dbias.reshape(HQ, WIN * 2 * WIN), onehot, "nt", f32, "rel_bias_dw", tk=8192, precision=HI)

    dmod = jnp.concatenate([dsh1, dsc1, dg1, dsh2, dsc2, dg2], axis=2).reshape(B, NMOD * D)

    zrow = lambda a: jnp.concatenate([a.reshape(1, -1), jnp.zeros((B - 1, a.size), f32)], axis=0)
    small_g = jnp.concatenate([
        dmod, dw_mix_pre.reshape(B, D), dw_mix_post.reshape(B, D), dw_ffn_pre.reshape(B, D), dw_ffn_post.reshape(B, D),
        da_log_pad.reshape(B, LANE)[:, DNH:2 * DNH], ddt_bias_pad.reshape(B, LANE)[:, DNH:2 * DNH], dw_dn_norm.reshape(B, DND),
        zrow(dsinks), zrow(g_rel.T), loss_b.reshape(B, LANE)[:, :1], jnp.zeros((B, SMALL_PAD - SMALL_N - 1), f32)], axis=1)
    (small_all,) = _exchange([small_g], "gather_small")
    dmod_cols = lax.dynamic_slice_in_dim(small_all.reshape(NDEV * B, SMALL_PAD), me * ncol, ncol, axis=1)
    g_ada_w = ada_bwd(c_all, dmod_cols)
    parts = {}
    for i, (d, started) in enumerate(in_flight):
        parts.update(zip(d, _copy_finish(started, len(d), g_ada_w, "scatter_finish_%d" % i)))
    small_w = dict(ada_b=(ada_b, m_ada_b, v_ada_b), norm_mix_pre=(norm_mix_pre, m_norm_mix_pre, v_norm_mix_pre),
                   norm_mix_post=(norm_mix_post, m_norm_mix_post, v_norm_mix_post), norm_ffn_pre=(norm_ffn_pre, m_norm_ffn_pre, v_norm_ffn_pre),
                   norm_ffn_post=(norm_ffn_post, m_norm_ffn_post, v_norm_ffn_post), dn_a_log=(dn_a_log, m_dn_a_log, v_dn_a_log),
                   dn_dt_bias=(dn_dt_bias, m_dn_dt_bias, v_dn_dt_bias), dn_norm_w=(dn_norm_w, m_dn_norm_w, v_dn_norm_w),
                   attn_sinks=(attn_sinks, m_attn_sinks, v_attn_sinks), rel_bias=(rel_bias, m_rel_bias, v_rel_bias))

    def pack(i, fill):
        row = jnp.concatenate([small_w[n][i].reshape(1, -1) for n, _ in SMALL], axis=1)
        return jnp.pad(row, ((0, 0), (0, SMALL_PAD - SMALL_N)), constant_values=fill)

    small_out = adamw(pack(0, 0.0), small_all.reshape(NDEV * B, 1, SMALL_PAD), pack(1, 0.0), pack(2, 1.0), "adamw_small")
    loss = small_out[0][0, SMALL_N]

    res = {}
    off = 0
    for n, size in SMALL:
        shp = small_w[n][0].shape
        res[n] = [o[:, off:off + size].reshape(shp) for o in small_out]
        off += size
    res["ada_w"] = [o[None] for o in adamw(ada_w[0], g_ada_w[None], m_ada_w[0], v_ada_w[0], "adamw_ada_w")]
    moments = dict(w_in=(m_w_in, v_w_in), dn_conv_w=(m_dn_conv_w, v_dn_conv_w), w_attn_branch=(m_w_attn_branch, v_w_attn_branch),
                   w_dn_branch=(m_w_dn_branch, v_w_dn_branch), w_out=(m_w_out, v_w_out), ffn_w_up=(m_ffn_w_up, v_ffn_w_up),
                   ffn_conv_w=(m_ffn_conv_w, v_ffn_conv_w), ffn_w_down=(m_ffn_w_down, v_ffn_w_down))
    for n in big_names:
        outs = adamw(local(n, big[n]), parts[n], local(n, moments[n][0]), local(n, moments[n][1]), "adamw_" + n)
        res[n] = [(o.T if n in transposed else o)[None] for o in outs]

    order = ["ada_w", "ada_b", "norm_mix_pre", "norm_mix_post", "norm_ffn_pre", "norm_ffn_post", "w_in", "dn_conv_w", "dn_a_log",
             "dn_dt_bias", "dn_norm_w", "attn_sinks", "rel_bias", "w_attn_branch", "w_dn_branch", "w_out", "ffn_w_up", "ffn_conv_w",
             "ffn_w_down"]
    return (loss, grad_x, *[res[n][0] for n in order], *[res[n][1] for n in order], *[res[n][2] for n in order],
            *[res[n][3] for n in order])
```

```python
import functools
import math

import numpy as np
import jax
import jax.numpy as jnp
from jax import lax
from jax.experimental import pallas as pl
from jax.experimental.pallas import tpu as pltpu

f32 = jnp.float32
bf16 = jnp.bfloat16
HI = lax.Precision.HIGHEST
MID = lax.Precision.HIGH
MESH = pl.DeviceIdType.MESH

NDEV = 8
D = 1024
HQ, HKV, HD, WIN, NBUCK, MAXDIST = 8, 2, 64, 128, 32, 128
DNH, DND, DNK, CH = 4, 128, 4, 64
DFF, FK = 2816, 3
NMOD = 6
RMS_EPS = 1e-6
L2_EPS = 1e-6
NEG_INF = -1e30
LR, B1, B2, EPS, WD, STEP = 0.001, 0.9, 0.999, 1e-08, 0.01, 10

LANE = 128
CB_GA, CB_GD, CB_AQ, CB_DQKV, CB_DZ, CB_AK, CB_AV, CB_BA, NPB = 0, 8, 16, 20, 32, 36, 37, 38, 39
NP = NPB * LANE
IN_SPLITS = (HQ * HD, HKV * HD, HKV * HD, 3 * DNH * DND, DNH * DND, DNH, DNH, D, D)
IN_DIM = sum(IN_SPLITS)
VMEM_LIMIT = 56 * 1024 * 1024

SMALL = (("ada_b", NMOD * D), ("norm_mix_pre", D), ("norm_mix_post", D), ("norm_ffn_pre", D), ("norm_ffn_post", D),
         ("dn_a_log", DNH), ("dn_dt_bias", DNH), ("dn_norm_w", DND), ("attn_sinks", HQ), ("rel_bias", NBUCK * HQ))
SMALL_N = sum(n for _, n in SMALL)
SMALL_PAD = 10752


def _cp(sem):
    return pltpu.CompilerParams(dimension_semantics=sem, vmem_limit_bytes=VMEM_LIMIT)


def _pick(dim, target):
    if dim <= target:
        return dim
    best = None
    for d in range(LANE, target + 1, LANE):
        if dim % d == 0:
            best = d
    assert best is not None, (dim, target)
    return best


def _me():
    x, y, c = lax.axis_index("x"), lax.axis_index("y"), lax.axis_index("c")
    return x, y, c, 4 * x + 2 * y + c


def _peer(x, y, c, k):
    px = 1 - x if k & 4 else x
    py = 1 - y if k & 2 else y
    pc = 1 - c if k & 1 else c
    return (px, py, pc), 4 * px + 2 * py + pc


class _Comm:
    def __init__(self, arrs, two_level=False):
        self.arrs, self.n, self.two_level = list(arrs), len(arrs), two_level
        self.out_shape = [jax.ShapeDtypeStruct((NDEV,) + a.shape, a.dtype) for a in arrs]
        nsem = self.n * (NDEV - 1)
        self.scratch = [pltpu.SemaphoreType.DMA((nsem,)), pltpu.SemaphoreType.DMA((nsem,)), pltpu.SemaphoreType.DMA((self.n,))]
        self.specs = [pl.BlockSpec(memory_space=pl.ANY)] * self.n

    def phases(self, ins, out, send, recv, loc):
        x, y, c, me = _me()

        def remote(a, k, src, dst, to):
            s = a * (NDEV - 1) + k - 1
            return pltpu.make_async_remote_copy(src_ref=src, dst_ref=dst, send_sem=send.at[s], recv_sem=recv.at[s],
                                                device_id=to, device_id_type=MESH)

        def local(a):
            return pltpu.make_async_copy(ins[a], out[a].at[me], loc.at[a])

        if not self.two_level:
            def mine(a, k):
                peer, pid = _peer(x, y, c, k)
                return remote(a, k, ins[a], out[a].at[me], peer)

            def theirs(a, k):
                peer, pid = _peer(x, y, c, k)
                return remote(a, k, ins[a], out[a].at[pid], peer)

            def start():
                for a in range(self.n):
                    local(a).start()
                    for k in range(1, NDEV):
                        mine(a, k).start()

            def forward():
                pass

            def finish():
                for a in range(self.n):
                    for k in range(1, NDEV):
                        mine(a, k).wait_send()
                    for k in range(1, NDEV):
                        theirs(a, k).wait_recv()
                    local(a).wait()

            return start, forward, finish

        sibling = (x, y, 1 - c)
        chips = [(1 - x, y), (x, 1 - y), (1 - x, 1 - y)]
        slot = lambda px, py, pc: 4 * px + 2 * py + pc

        def own(a, k, to):
            return remote(a, k, ins[a], out[a].at[me], to)

        def landed(a, k, frm):
            return remote(a, k, ins[a], out[a].at[slot(*frm)], frm)

        def passed(a, j):
            rows = out[a].at[slot(*chips[j], c)]
            return remote(a, 5 + j, rows, rows, sibling)

        def start():
            for a in range(self.n):
                local(a).start()
                own(a, 1, sibling).start()
                for j, chip in enumerate(chips):
                    own(a, 2 + j, (*chip, c)).start()

        def forward():
            for a in range(self.n):
                for j, chip in enumerate(chips):
                    landed(a, 2 + j, (*chip, c)).wait_recv()
                    passed(a, j).start()

        def finish():
            for a in range(self.n):
                landed(a, 1, sibling).wait_recv()
                for j, chip in enumerate(chips):
                    remote(a, 5 + j, ins[a], out[a].at[slot(*chip, 1 - c)], sibling).wait_recv()
                own(a, 1, sibling).wait_send()
                for j, chip in enumerate(chips):
                    own(a, 2 + j, (*chip, c)).wait_send()
                    passed(a, j).wait_send()
                local(a).wait()

        return start, forward, finish


def _copy_start(arrs, name, gather=False, after=None):
    n = len(arrs)
    order = [] if after is None else [after]
    n_in = 2 * n + len(order)
    block = (lambda ref, j: ref) if gather else (lambda ref, j: ref.at[j])

    def body(*refs):
        ins, lands, send, recv, own, token = refs[:n], refs[n:2 * n], refs[n_in], refs[n_in + 1], refs[n_in + 2], refs[-1]
        x, y, c, me = _me()
        for a in range(n):
            pltpu.make_async_copy(block(ins[a], me), lands[a].at[me], own.at[a]).start()
            for k in range(1, NDEV):
                peer, pid = _peer(x, y, c, k)
                s = a * (NDEV - 1) + k - 1
                pltpu.make_async_remote_copy(src_ref=block(ins[a], pid), dst_ref=lands[a].at[me], send_sem=send.at[s],
                                             recv_sem=recv.at[s], device_id=peer, device_id_type=MESH).start()
        token[...] = jnp.zeros(token.shape, token.dtype)

    hbm, sem = pl.BlockSpec(memory_space=pltpu.HBM), pl.BlockSpec(memory_space=pltpu.SEMAPHORE)
    nsem = n * (NDEV - 1)
    land_shapes = [((NDEV,) + a.shape if gather else a.shape) for a in arrs]
    thru = [pltpu.HBM(a.shape, a.dtype) for a in arrs] + [pltpu.HBM(shp, a.dtype) for shp, a in zip(land_shapes, arrs)]
    return pl.pallas_call(
        body, name=name, in_specs=[hbm] * (2 * n) + [pl.BlockSpec(memory_space=pl.ANY)] * len(order),
        out_shape=(pltpu.SemaphoreType.DMA((nsem,)), pltpu.SemaphoreType.DMA((nsem,)), pltpu.SemaphoreType.DMA((n,)), *thru,
                   jax.ShapeDtypeStruct((8, LANE), f32)),
        out_specs=(sem, sem, sem, *[hbm] * (2 * n), pl.BlockSpec(memory_space=pltpu.VMEM)),
        input_output_aliases={i: 3 + i for i in range(2 * n)},
        compiler_params=pltpu.CompilerParams(has_side_effects=pltpu.SideEffectType.DATAFLOW_SIDE_EFFECTING),
    )(*[pltpu.with_memory_space_constraint(a, pltpu.HBM) for a in arrs],
      *[pltpu.with_memory_space_constraint(lax.empty(shp, a.dtype), pltpu.HBM) for shp, a in zip(land_shapes, arrs)], *order)


def _copy_finish(started, n, after, name, gather=False):
    send, recv, own, *rest = started
    srcs, lands = rest[:n], rest[n:2 * n]
    block = (lambda ref, j: ref) if gather else (lambda ref, j: ref.at[j])

    def body(*refs):
        ins, lnd, send_ref, recv_ref, own_ref = refs[:n], refs[n:2 * n], refs[2 * n], refs[2 * n + 1], refs[2 * n + 2]
        x, y, c, me = _me()
        for a in range(n):
            pltpu.make_async_copy(block(ins[a], me), lnd[a].at[me], own_ref.at[a]).wait()
            for k in range(1, NDEV):
                peer, pid = _peer(x, y, c, k)
                s = a * (NDEV - 1) + k - 1
                cp = pltpu.make_async_remote_copy(src_ref=block(ins[a], pid), dst_ref=lnd[a].at[pid], send_sem=send_ref.at[s],
                                                  recv_sem=recv_ref.at[s], device_id=peer, device_id_type=MESH)
                cp.wait_send()
                cp.wait_recv()

    hbm, sem = pl.BlockSpec(memory_space=pltpu.HBM), pl.BlockSpec(memory_space=pltpu.SEMAPHORE)
    thru = [pltpu.HBM(a.shape, a.dtype) for a in srcs] + [pltpu.HBM(a.shape, a.dtype) for a in lands]
    out = pl.pallas_call(
        body, name=name, in_specs=[hbm] * (2 * n) + [sem, sem, sem, pl.BlockSpec(memory_space=pl.ANY)],
        out_shape=tuple(thru), out_specs=tuple([hbm] * (2 * n)), input_output_aliases={i: i for i in range(2 * n)},
        compiler_params=pltpu.CompilerParams(has_side_effects=pltpu.SideEffectType.DATAFLOW_SIDE_EFFECTING),
    )(*srcs, *lands, send, recv, own, after)
    return list(out[n:])


def _exchange(arrs, name, two_level=False):
    comm = _Comm(arrs, two_level)

    def body(*refs):
        start, forward, finish = comm.phases(refs[:comm.n], refs[comm.n:2 * comm.n], *refs[2 * comm.n:])
        start()
        forward()
        finish()

    return pl.pallas_call(body, name=name, out_shape=comm.out_shape, in_specs=comm.specs, out_specs=comm.specs,
                          scratch_shapes=comm.scratch, compiler_params=pltpu.CompilerParams(has_side_effects=True))(*arrs)


def mm(a, b, mode, out_dtype, name, tm=1024, tn=1024, tk=1024, precision=None, b_cols=None):
    a_parts = a.shape[0] if a.ndim == 3 else 1
    b_parts = b.shape[0] if b.ndim == 3 else 1
    assert b_parts == 1 or mode == "tn"
    ash, bsh = (a.shape[-2], a.shape[-1] * a_parts), b.shape[-2:]
    if mode == "nn":
        (M, K), (K2, N) = ash, bsh
    elif mode == "nt":
        (M, K), (N, K2) = ash, bsh
    else:
        (K, M), (K2, N) = ash, (bsh[0], bsh[1] * b_parts)
    assert K == K2, (name, a.shape, b.shape)
    col0 = 0
    if b_cols is not None:
        assert mode in ("nn", "nt") and tn % LANE == 0
        col0, N = b_cols[0], b_cols[1] * tn
    if mode == "tn":
        tm, tn, tk = _pick(M // a_parts, tm), _pick(N // b_parts, tn), _pick(K, tk)
    else:
        tm, tn, tk = _pick(M, tm), _pick(N // b_parts, tn), _pick(K // a_parts, tk)
    nk = K // tk
    if mode == "tn" and a_parts > 1:
        per = M // tm // a_parts
        a_spec = pl.BlockSpec((None, tk, tm), lambda i, j, k: (i // per, k, i % per))
    elif mode == "tn":
        a_spec = pl.BlockSpec((tk, tm), lambda i, j, k: (k, i))
    elif a_parts > 1:
        per = nk // a_parts
        a_spec = pl.BlockSpec((None, tm, tk), lambda i, j, k: (k // per, i, k % per))
    else:
        a_spec = pl.BlockSpec((tm, tk), lambda i, j, k: (i, k))
    if mode == "nt":
        b_spec = pl.BlockSpec((tn, tk), lambda i, j, k: (col0 + j, k))
    elif b_parts > 1:
        per = N // tn // b_parts
        b_spec = pl.BlockSpec((None, tk, tn), lambda i, j, k: (j // per, k, j % per))
    else:
        b_spec = pl.BlockSpec((tk, tn), lambda i, j, k: (k, col0 + j))
    dims = {"nn": ((1,), (0,)), "nt": ((1,), (1,)), "tn": ((0,), (0,))}[mode]

    def body(a_ref, b_ref, o_ref, *scr):
        p = lax.dot_general(a_ref[...], b_ref[...], (dims, ((), ())), preferred_element_type=f32, precision=precision)
        if nk == 1:
            o_ref[...] = p.astype(o_ref.dtype)
        else:
            acc = scr[0]
            k = pl.program_id(2)

            @pl.when(k == 0)
            def _():
                acc[...] = p

            @pl.when(k > 0)
            def _():
                acc[...] += p

            @pl.when(k == nk - 1)
            def _():
                o_ref[...] = acc[...].astype(o_ref.dtype)

    return pl.pallas_call(
        body, name=name, grid=(M // tm, N // tn, nk), in_specs=[a_spec, b_spec],
        out_specs=pl.BlockSpec((tm, tn), lambda i, j, k: (i, j)), out_shape=jax.ShapeDtypeStruct((M, N), out_dtype),
        scratch_shapes=[pltpu.VMEM((tm, tn), f32)] if nk > 1 else [],
        compiler_params=_cp(("parallel", "parallel", "arbitrary")),
    )(a, b)


ROW_TILE = 512


def rowcall(name, fn, tok, bat, con, tok_out, acc_out, ts=ROW_TILE, into=None):
    B, S = tok[0][0].shape[:2]
    ts = min(ts, S)
    nt, nb, nc, no, na = len(tok), len(bat), len(con), len(tok_out), len(acc_out)
    nin = nt + nb + nc + (1 if into is not None else 0)

    ns = S // ts
    steps = B * ns

    def body(*refs):
        tr, br, cr = refs[:nt], refs[nt:nt + nb], refs[nt + nb:nt + nb + nc]
        orf, arf = refs[nin:nin + no], refs[nin + no:nin + no + na]
        bufs, sem = refs[nin + no + na:-1], refs[-1]
        s = pl.program_id(1)
        step = pl.program_id(0) * ns + s

        def tile_copy(i, k):
            w, cb = tok[i][1], tok[i][2]
            src = tr[i].at[k // ns, pl.ds((k % ns) * ts, ts), pl.ds(cb * w, w)]
            return pltpu.make_async_copy(src, bufs[i].at[k % RING], sem.at[i, k % RING])

        @pl.when(step == 0)
        def _():
            for k in range(min(RING - 1, steps)):
                for i in range(nt):
                    tile_copy(i, k).start()

        @pl.when(step + (RING - 1) < steps)
        def _():
            for i in range(nt):
                tile_copy(i, step + (RING - 1)).start()

        for i in range(nt):
            tile_copy(i, step).wait()
        touts, aouts = fn([bufs[i][step % RING] for i in range(nt)], [r[0] for r in br], [r[...] for r in cr])
        for r, v in zip(orf, touts):
            r[0] = v.astype(r.dtype)
        for r, v in zip(arf, aouts):
            @pl.when(s == 0)
            def _(r=r):
                r[...] = jnp.zeros(r.shape, r.dtype)
            r[0] += v.astype(f32)

    in_specs = [pl.BlockSpec(memory_space=pl.ANY) for _ in tok]
    in_specs += [pl.BlockSpec((1,) + a.shape[1:], lambda b, s: (b, 0, 0)) for a in bat]
    in_specs += [pl.BlockSpec(a.shape, lambda b, s, nd=a.ndim: (0,) * nd) for a in con]
    out_specs = [pl.BlockSpec((1, ts, w), lambda b, s: (b, s, 0)) for (w, _) in tok_out]
    out_specs += [pl.BlockSpec((1,) + shp, lambda b, s, nd=len(shp): (b,) + (0,) * nd) for shp in acc_out]
    out_shape = [jax.ShapeDtypeStruct((B, S, w), dt) for (w, dt) in tok_out]
    out_shape += [jax.ShapeDtypeStruct((B,) + shp, f32) for shp in acc_out]
    extra, aliases = [], {}
    if into is not None:
        buf, cb = into
        assert buf.dtype == tok_out[0][1]
        in_specs.append(pl.BlockSpec(memory_space=pl.ANY))
        out_specs[0] = pl.BlockSpec((1, ts, tok_out[0][0]), lambda b, s: (b, s, cb))
        out_shape[0] = jax.ShapeDtypeStruct(buf.shape, buf.dtype)
        extra, aliases = [buf], {nin - 1: 0}
    return pl.pallas_call(
        body, name=name, grid=(B, S // ts), in_specs=in_specs, out_specs=out_specs, out_shape=out_shape,
        input_output_aliases=aliases, compiler_params=_cp(("arbitrary", "arbitrary")),
        scratch_shapes=[pltpu.VMEM((RING, ts, w), a.dtype) for (a, w, _) in tok] + [pltpu.SemaphoreType.DMA((nt, RING))],
    )(*[t[0] for t in tok], *bat, *con, *extra)


def rowcall_fwd(name, f, tok, bat, con, tok_out, ts=2 * ROW_TILE):
    def fn(t, b, c):
        return f([v.astype(f32) for v in t], b, c), []
    return rowcall(name, fn, tok, bat, con, tok_out, [], ts)


def rowcall_bwd(name, f, tok, bat, con, cts, tok_grads, add=None, ts=ROW_TILE, join_first=1, into=None):
    nt, ncts = len(tok), len(cts)

    def fn(t, b, c):
        prim = [v.astype(f32) for v in t[:nt]]
        ct = [v.astype(f32) for v in t[nt:nt + ncts]]
        _, vjp = jax.vjp(lambda tt, bb, cc: f(tt, bb, cc), prim, b, c)
        dt, db, dc = vjp(ct)
        touts = [dt[i] for i, _ in tok_grads]
        if add is not None:
            touts[0] = touts[0] + t[nt + ncts].astype(f32)
        if join_first > 1:
            touts = [jnp.concatenate(touts[:join_first], axis=1)] + touts[join_first:]
        return touts, list(db) + list(dc)

    all_tok = list(tok) + list(cts) + ([add] if add is not None else [])
    tok_out = [(tok[i][1], dt) for i, dt in tok_grads]
    if join_first > 1:
        tok_out = [(sum(w for w, _ in tok_out[:join_first]), tok_out[0][1])] + tok_out[join_first:]
    acc_out = [tuple(a.shape[1:]) for a in bat] + [tuple(a.shape) for a in con]
    return rowcall(name, fn, all_tok, bat, con, tok_out, acc_out, ts, into)


def _rms(y, w):
    return y * lax.rsqrt(jnp.mean(y * y, axis=-1, keepdims=True) + RMS_EPS) * w


@jax.custom_vjp
def _rms_vjp(y, w):
    return _rms(y, w)


def _rms_vjp_fwd(y, w):
    r = lax.rsqrt(jnp.mean(y * y, axis=-1, keepdims=True) + RMS_EPS)
    yhat = y * r
    return yhat * w, (yhat, r, w)


def _rms_vjp_bwd(res, g):
    yhat, r, w = res
    gw = g * w
    return r * (gw - yhat * jnp.mean(gw * yhat, axis=-1, keepdims=True)), jnp.sum(g * yhat, axis=0, keepdims=True)


_rms_vjp.defvjp(_rms_vjp_fwd, _rms_vjp_bwd)


def f_rms_mod(t, b, c, rms=_rms):
    return [rms(t[0], c[0]) * (1.0 + b[0]) + b[1]]


def f_post_pre(t, b, c, rms=_rms):
    h1 = t[0] + b[0] * rms(t[1], c[0])
    return [h1, rms(h1, c[1]) * (1.0 + b[1]) + b[2]]


def f_merge(t, b, c):
    ga, gd, ya, yd = t
    return [jax.nn.sigmoid(ga) * ya + jax.nn.sigmoid(gd) * yd]


def f_dnout(t, b, c, rms=_rms):
    o, z = t
    outs = []
    for h in range(DNH):
        sl = slice(h * DND, (h + 1) * DND)
        zh = z[:, sl]
        outs.append(rms(o[:, sl], c[0]) * (zh * jax.nn.sigmoid(zh)))
    return [jnp.concatenate(outs, axis=1)]


def _softplus(x):
    return jnp.maximum(x, 0.0) + jnp.log(1.0 + jnp.exp(-jnp.abs(x)))


def f_gate(t, b, c):
    ba = t[0]
    a_log, dt_bias = c
    lane = lax.broadcasted_iota(jnp.int32, ba.shape, 1)
    beta = jax.nn.sigmoid(ba)
    g = -jnp.exp(a_log) * _softplus(ba + dt_bias)
    return [jnp.where(lane < DNH, beta, jnp.where(lane < 2 * DNH, g, 0.0))]


def _bucket_table():
    qi = np.arange(WIN)[:, None]
    kj = np.arange(2 * WIN)[None, :]
    dist = np.maximum(WIN + qi - kj, 0)
    max_exact = NBUCK // 2
    scaled = np.log(np.maximum(dist, 1).astype(np.float64) / max_exact) / math.log(MAXDIST / max_exact)
    large = np.minimum(max_exact + (scaled * (NBUCK - max_exact)).astype(np.int32), NBUCK - 1)
    return np.where(dist < max_exact, dist, large).astype(np.int32)


def _attn_mask(n):
    qi = lax.broadcasted_iota(jnp.int32, (WIN, 2 * WIN), 0)
    kj = lax.broadcasted_iota(jnp.int32, (WIN, 2 * WIN), 1)
    dist = WIN + qi - kj
    return (dist >= 0) & (dist < WIN) & ((kj >= WIN) | (n > 0))


def _swap_halves(x):
    return pltpu.roll(x, HD, axis=x.ndim - 1)


@jax.custom_vjp
def _swap_halves_vjp(x):
    return _swap_halves(x)


_swap_halves_vjp.defvjp(lambda x: (_swap_halves(x), None), lambda _, g: (_swap_halves(g),))


def _sink_softmax(s, sinks):
    m = jnp.maximum(jnp.max(s, axis=-1, keepdims=True), sinks)
    p = jnp.exp(s - m)
    return p / (jnp.sum(p, axis=-1, keepdims=True) + jnp.exp(sinks - m))


@jax.custom_vjp
def _sink_softmax_vjp(s, sinks):
    return _sink_softmax(s, sinks)


def _sink_softmax_fwd(s, sinks):
    m = jnp.maximum(jnp.max(s, axis=-1, keepdims=True), sinks)
    p = jnp.exp(s - m)
    sink = jnp.exp(sinks - m)
    inv = 1.0 / (jnp.sum(p, axis=-1, keepdims=True) + sink)
    return p * inv, (p * inv, sink * inv)


def _sink_softmax_bwd(res, g):
    probs, sink_prob = res
    d = jnp.sum(g * probs, axis=-1, keepdims=True)
    return probs * (g - d), -jnp.sum(sink_prob * d, axis=(0, 2)).reshape(HQ, 1, 1)


_sink_softmax_vjp.defvjp(_sink_softmax_fwd, _sink_softmax_bwd)


def _attn_block(q, kp, kc, vp, vc, bias, sinks, mask, differentiated):
    dot = _bdot_bf16_vjp if differentiated else _bdot_bf16
    swap = _swap_halves_vjp if differentiated else _swap_halves
    B, grp = q.shape[0], HQ // HKV
    upper = lax.broadcasted_iota(jnp.int32, (2 * WIN, LANE), 1) >= HD

    def placed(natural, swapped, j, half):
        keep = upper if half == 1 else ~upper
        return jnp.where(keep, natural if j == half else swapped, 0.0)

    qh, ks, vs = [], [], []
    for b in range(B):
        kb, vb = jnp.concatenate([kp[b], kc[b]], axis=0), jnp.concatenate([vp[b], vc[b]], axis=0)
        kb_sw, vb_sw = swap(kb), swap(vb)
        for h in range(HQ):
            qh.append(q[b, :, (h // 2) * LANE:(h // 2 + 1) * LANE])
            ks.append(placed(kb, kb_sw, h // grp, h % 2))
            vs.append(placed(vb, vb_sw, h // grp, h % 2))
    s = dot(_stack(qh), _stack(ks), 2, 2).reshape(B, HQ, WIN, 2 * WIN) * (HD ** -0.5)
    probs = (_sink_softmax_vjp if differentiated else _sink_softmax)(jnp.where(mask, s + bias, NEG_INF), sinks)
    o = dot(probs.reshape(B * HQ, WIN, 2 * WIN), _stack(vs), 2, 1)
    return _stack([jnp.concatenate([o[b * HQ + 2 * i] + o[b * HQ + 2 * i + 1] for i in range(HQ // 2)], axis=1) for b in range(B)])


def _attn_specs(B, NB):
    last = NB - 1
    return [
        pl.BlockSpec((B, WIN, HQ * HD), lambda n: (0, jnp.minimum(n, last), CB_AQ // 4)),
        pl.BlockSpec((B, WIN, LANE), lambda n: (0, jnp.clip(n - 1, 0, last), CB_AK)),
        pl.BlockSpec((B, WIN, LANE), lambda n: (0, jnp.minimum(n, last), CB_AK)),
        pl.BlockSpec((B, WIN, LANE), lambda n: (0, jnp.clip(n - 1, 0, last), CB_AV)),
        pl.BlockSpec((B, WIN, LANE), lambda n: (0, jnp.minimum(n, last), CB_AV)),
        pl.BlockSpec((HQ, WIN, 2 * WIN), lambda n: (0, 0, 0)),
        pl.BlockSpec((HQ, 1, 1), lambda n: (0, 0, 0)),
    ]


def attn_fwd(proj, bias, sinks):
    B, S, _ = proj.shape
    NB = S // WIN

    def body(q, kp, kc, vp, vc, bias_ref, sink_ref, o_ref):
        mask = _attn_mask(pl.program_id(0))
        o = _attn_block(*[r[...].astype(f32) for r in (q, kp, kc, vp, vc)], bias_ref[...], sink_ref[...], mask, False)
        o_ref[...] = o.astype(o_ref.dtype)

    return pl.pallas_call(
        body, name="attn_fwd", grid=(NB,), in_specs=_attn_specs(B, NB),
        out_specs=pl.BlockSpec((B, WIN, HQ * HD), lambda n: (0, n, 0)), out_shape=jax.ShapeDtypeStruct((B, S, HQ * HD), bf16),
        compiler_params=_cp(("parallel",)),
    )(proj, proj, proj, proj, proj, bias, sinks)


def attn_bwd(proj, bias, sinks, dy, dproj):
    B, S, _ = proj.shape
    NB = S // WIN
    last = NB - 1

    def body(q, kp, kc, vp, vc, bias_ref, sink_ref, dy_ref, _, dq_ref, dk_ref, dv_ref, dbias_ref, dsink_ref, kcar, vcar):
        n = pl.program_id(0)

        @pl.when(n == 0)
        def _():
            dbias_ref[...] = jnp.zeros(dbias_ref.shape, f32)
            dsink_ref[...] = jnp.zeros(dsink_ref.shape, f32)
            kcar[...] = jnp.zeros(kcar.shape, f32)
            vcar[...] = jnp.zeros(vcar.shape, f32)

        @pl.when(n < NB)
        def _():
            mask = _attn_mask(n)
            _, vjp = jax.vjp(lambda *a: _attn_block(*a, mask, True), *[r[...].astype(f32) for r in (q, kp, kc, vp, vc)],
                             bias_ref[...], sink_ref[...])
            dq, dkp, dkc, dvp, dvc, dbias, dsink = vjp(dy_ref[...].astype(f32))
            dq_ref[...] = dq.astype(dq_ref.dtype)
            dbias_ref[...] += dbias
            dsink_ref[...] += dsink
            dk_ref[...] = (kcar[...] + dkp).astype(dk_ref.dtype)
            dv_ref[...] = (vcar[...] + dvp).astype(dv_ref.dtype)
            kcar[...] = dkc
            vcar[...] = dvc

        @pl.when(n == NB)
        def _():
            dk_ref[...] = kcar[...].astype(dk_ref.dtype)
            dv_ref[...] = vcar[...].astype(dv_ref.dtype)

    in_specs = _attn_specs(B, NB) + [pl.BlockSpec((B, WIN, HQ * HD), lambda n: (0, jnp.minimum(n, last), 0)),
                                     pl.BlockSpec(memory_space=pl.ANY)]
    kv_out = pl.BlockSpec((B, WIN, LANE), lambda n: (0, jnp.maximum(n - 1, 0), 0))
    return pl.pallas_call(
        body, name="attn_bwd", grid=(NB + 1,), in_specs=in_specs, input_output_aliases={8: 0},
        out_specs=[pl.BlockSpec((B, WIN, HQ * HD), lambda n: (0, jnp.minimum(n, last), CB_AQ // 4)), kv_out, kv_out,
                   pl.BlockSpec((HQ, WIN, 2 * WIN), lambda n: (0, 0, 0)), pl.BlockSpec((HQ, 1, 1), lambda n: (0, 0, 0))],
        out_shape=[jax.ShapeDtypeStruct(dproj.shape, dproj.dtype), jax.ShapeDtypeStruct((B, S, LANE), bf16),
                   jax.ShapeDtypeStruct((B, S, LANE), bf16), jax.ShapeDtypeStruct((HQ, WIN, 2 * WIN), f32),
                   jax.ShapeDtypeStruct((HQ, 1, 1), f32)],
        scratch_shapes=[pltpu.VMEM((B, WIN, LANE), f32), pltpu.VMEM((B, WIN, LANE), f32)],
        compiler_params=_cp(("arbitrary",)),
    )(proj, proj, proj, proj, proj, bias, sinks, dy, dproj)


DN_ROWS, FFN_ROWS = 256, 32
RING = 3


def _stage_rows(dst, value):
    dst[0:8] = jnp.zeros((8, LANE), f32)
    dst[8:8 + value.shape[0]] = value


def _conv_rows(xs, w, width, r, rows):
    wins = [xs[pl.ds(r + 8 - (width - 1) + j, rows), :] for j in range(width)]
    out = w[0:1] * wins[0]
    for j in range(1, width):
        out = out + w[j:j + 1] * wins[j]
    return out, wins


def _fold8(v):
    return jnp.sum(v.reshape(v.shape[0] // 8, 8, LANE), axis=0)


def _conv_rows_t(ds, w, width, r, rows):
    out = w[0:1] * ds[pl.ds(r + width - 1, rows), :]
    for j in range(1, width):
        out = out + w[j:j + 1] * ds[pl.ds(r + width - 1 - j, rows), :]
    return out


def _dn_outblk(i):
    return (i % DNH) * 3 + i // DNH


def _dn_act(c, isqk):
    sg = jax.nn.sigmoid(c)
    y = c * sg
    n = lax.rsqrt(jnp.sum(y * y, axis=-1, keepdims=True) + L2_EPS)
    return jnp.where(isqk, y * n, y), sg, n


def dnconv_fwd(proj, conv_w):
    B, S, _ = proj.shape
    rows = min(DN_ROWS, S)

    def body(x_ref, w_ref, o_ref, xs):
        isqk = pl.program_id(0) < 2 * DNH
        _stage_rows(xs, x_ref[0].astype(f32))
        w = w_ref[...]
        for r in range(0, S, rows):
            c, _ = _conv_rows(xs, w, DNK, r, rows)
            o_ref[0, pl.ds(r, rows), :] = _dn_act(c, isqk)[0]

    return pl.pallas_call(
        body, name="dnconv_fwd", grid=(3 * DNH, B),
        in_specs=[pl.BlockSpec((1, S, LANE), lambda i, b: (b, 0, CB_DQKV + i)), pl.BlockSpec((DNK, LANE), lambda i, b: (0, i))],
        out_specs=pl.BlockSpec((1, S, LANE), lambda i, b: (b, 0, _dn_outblk(i))),
        out_shape=jax.ShapeDtypeStruct((B, S, 3 * DNH * DND), f32), scratch_shapes=[pltpu.VMEM((S + 8, LANE), f32)],
        compiler_params=_cp(("parallel", "parallel")),
    )(proj, conv_w)


def dnconv_bwd(proj, conv_w, dqkvn, dproj):
    B, S, _ = proj.shape
    rows = min(DN_ROWS, S)

    def body(x_ref, w_ref, dy_ref, _, dx_ref, dw_ref, xs, ds):
        isqk = pl.program_id(0) < 2 * DNH
        _stage_rows(xs, x_ref[0].astype(f32))
        w = w_ref[...]
        dw = [jnp.zeros((8, LANE), f32) for _ in range(DNK)]
        for r in range(0, S, rows):
            c, wins = _conv_rows(xs, w, DNK, r, rows)
            out, sg, n = _dn_act(c, isqk)
            dout = dy_ref[0, pl.ds(r, rows), :]
            dy = jnp.where(isqk, n * (dout - out * jnp.sum(dout * out, axis=-1, keepdims=True)), dout)
            dc = dy * (sg * (1.0 + c * (1.0 - sg)))
            ds[pl.ds(r, rows), :] = dc
            for j in range(DNK):
                dw[j] = dw[j] + _fold8(dc * wins[j])
        ds[S:S + 8] = jnp.zeros((8, LANE), f32)
        for r in range(0, S, rows):
            dx_ref[0, pl.ds(r, rows), :] = _conv_rows_t(ds, w, DNK, r, rows).astype(dx_ref.dtype)

        @pl.when(pl.program_id(1) == 0)
        def _():
            dw_ref[...] = jnp.zeros(dw_ref.shape, f32)
        dw_ref[...] += jnp.concatenate([jnp.sum(d, axis=0, keepdims=True) for d in dw], axis=0)

    return pl.pallas_call(
        body, name="dnconv_bwd", grid=(3 * DNH, B),
        in_specs=[pl.BlockSpec((1, S, LANE), lambda i, b: (b, 0, CB_DQKV + i)), pl.BlockSpec((DNK, LANE), lambda i, b: (0, i)),
                  pl.BlockSpec((1, S, LANE), lambda i, b: (b, 0, _dn_outblk(i))), pl.BlockSpec(memory_space=pl.ANY)],
        out_specs=[pl.BlockSpec((1, S, LANE), lambda i, b: (b, 0, CB_DQKV + i)), pl.BlockSpec((DNK, LANE), lambda i, b: (0, i))],
        out_shape=[jax.ShapeDtypeStruct(dproj.shape, dproj.dtype), jax.ShapeDtypeStruct((DNK, 3 * DNH * DND), f32)],
        scratch_shapes=[pltpu.VMEM((S + 8, LANE), f32), pltpu.VMEM((S + 8, LANE), f32)],
        input_output_aliases={3: 0}, compiler_params=_cp(("parallel", "arbitrary")),
    )(proj, conv_w, dqkvn, dproj)


def _bdot(a, b, ca, cb, precision=HI):
    return lax.dot_general(a, b, (((ca,), (cb,)), ((0,), (0,))), preferred_element_type=f32, precision=precision)


def _bdot_bf16(a, b, ca, cb):
    return _bdot(a.astype(bf16), b.astype(bf16), ca, cb, None)


@functools.partial(jax.custom_vjp, nondiff_argnums=(2, 3))
def _bdot_bf16_vjp(a, b, ca, cb):
    return _bdot_bf16(a, b, ca, cb)


def _bdot_bf16_fwd(a, b, ca, cb):
    return _bdot_bf16(a, b, ca, cb), (a, b)


def _bdot_bf16_bwd(ca, cb, res, g):
    a, b = res
    fa, fb = 3 - ca, 3 - cb
    da = _bdot_bf16(g, b, 2, fb) if ca == 2 else _bdot_bf16(b, g, fb, 2)
    db = _bdot_bf16(a, g, fa, 1) if cb == 1 else _bdot_bf16(g, a, 1, fa)
    return da, db


_bdot_bf16_vjp.defvjp(_bdot_bf16_fwd, _bdot_bf16_bwd)


def _neumann_inverse(low):
    n = low.shape[-1]
    eye = (lax.broadcasted_iota(jnp.int32, (n, n), 0) == lax.broadcasted_iota(jnp.int32, (n, n), 1)).astype(f32)
    p = -low
    x = eye[None] + p
    for _ in range(5):
        p = _bdot_bf16(p, p, 2, 1)
        x = x + _bdot_bf16(x, p, 2, 1)
    return x


@jax.custom_vjp
def _unit_lower_inverse(low):
    return _neumann_inverse(low)


def _uli_fwd(low):
    t = _neumann_inverse(low)
    return t, t


def _uli_bwd(t, dt):
    return (-_bdot_bf16(_bdot_bf16(t, dt, 1, 1), t, 2, 2),)


_unit_lower_inverse.defvjp(_uli_fwd, _uli_bwd)


def _stack(xs):
    return jnp.concatenate([x[None] for x in xs], axis=0)


DELTA_CHUNKS = 4


def _delta_chunks(qkv, bg, state, differentiated):
    inverse = _unit_lower_inverse if differentiated else _neumann_inverse
    lo = _bdot_bf16_vjp if differentiated else _bdot_bf16
    B, n = qkv.shape[0], qkv.shape[1] // CH
    G = B * DNH
    N = n * G
    triples = [(i, b, h) for i in range(n) for b in range(B) for h in range(DNH)]
    col = lambda i, b, h, kind: qkv[b, i * CH:(i + 1) * CH, (3 * h + kind) * DND:(3 * h + kind + 1) * DND]
    q, k, v = [_stack([col(i, b, h, kind) for i, b, h in triples]) for kind in range(3)]
    lane = lax.broadcasted_iota(jnp.int32, (CH, LANE), 1)
    pick = lambda i, b, l: jnp.sum(jnp.where(lane == l, bg[b, i * CH:(i + 1) * CH], 0.0), axis=1, keepdims=True)
    beta = _stack([pick(i, b, h) for i, b, h in triples])
    g = _stack([pick(i, b, h + DNH) for i, b, h in triples])
    ri = lax.broadcasted_iota(jnp.int32, (CH, CH), 0)
    ci = lax.broadcasted_iota(jnp.int32, (CH, CH), 1)
    incl, strict = (ri >= ci)[None], (ri > ci)[None]
    gc = _bdot(jnp.broadcast_to(incl.astype(f32), (N, CH, CH)), jnp.broadcast_to(g, (N, CH, LANE)), 2, 1, MID)
    e0 = jnp.broadcast_to((lane == 0).astype(f32)[None], (N, CH, LANE))
    gc_row = _bdot(e0, gc, 2, 2, MID)
    diff = gc[:, :, :CH] - gc_row
    decay = jnp.where(incl, jnp.exp(jnp.where(incl, diff, 0.0)), 0.0)
    qs = q * (DND ** -0.5)
    kb, vb = k * beta, v * beta
    eg = jnp.exp(gc)
    with_k = lo(jnp.concatenate([kb, qs], axis=1), k, 2, 2)
    low = jnp.where(strict, with_k[:, :CH] * decay, 0.0)
    intra = jnp.where(incl, with_k[:, CH:] * decay, 0.0)
    tinv = inverse(low)
    solved = lo(tinv, jnp.concatenate([vb, kb * eg], axis=2), 2, 1)
    gl = gc[:, CH - 1:CH, :]
    k_tail = k * jnp.exp(gl - gc)
    to_state = jnp.concatenate([solved[:, :, DND:], qs * eg], axis=1)
    decay_all = jnp.exp(gl)
    outs = []
    for i in range(n):
        sl = slice(i * G, (i + 1) * G)
        with_state = lo(to_state[sl], state, 2, 1)
        v_new = solved[sl, :, :DND] - with_state[:, :CH]
        outs.append(with_state[:, CH:] + lo(intra[sl], v_new, 2, 1))
        state = state * decay_all[sl] + lo(k_tail[sl], v_new, 1, 1)
    return outs, state


def delta_fwd(qkvn, bg):
    B, S, _ = qkvn.shape
    n = DELTA_CHUNKS if (S // CH) % DELTA_CHUNKS == 0 else 1
    steps, G, rows = S // (n * CH), B * DNH, n * CH

    def body(qkv_ref, bg_ref, o_ref, st_ref, state):
        @pl.when(pl.program_id(0) == 0)
        def _():
            state[...] = jnp.zeros(state.shape, f32)
        s0 = state[...]
        st_ref[0] = s0
        outs, s1 = _delta_chunks(qkv_ref[...], bg_ref[...], s0, False)
        for i, o in enumerate(outs):
            for b in range(B):
                for h in range(DNH):
                    o_ref[b, i * CH:(i + 1) * CH, h * DND:(h + 1) * DND] = o[b * DNH + h]
        state[...] = s1

    return pl.pallas_call(
        body, name="delta_fwd", grid=(steps,),
        in_specs=[pl.BlockSpec((B, rows, 3 * DNH * DND), lambda c: (0, c, 0)), pl.BlockSpec((B, rows, LANE), lambda c: (0, c, 0))],
        out_specs=[pl.BlockSpec((B, rows, DNH * DND), lambda c: (0, c, 0)), pl.BlockSpec((1, G, DND, DND), lambda c: (c, 0, 0, 0))],
        out_shape=[jax.ShapeDtypeStruct((B, S, DNH * DND), f32), jax.ShapeDtypeStruct((steps, G, DND, DND), f32)],
        scratch_shapes=[pltpu.VMEM((G, DND, DND), f32)], compiler_params=_cp(("arbitrary",)),
    )(qkvn, bg)


def delta_bwd(qkvn, bg, states, do):
    B, S, _ = qkvn.shape
    steps, G = states.shape[0], B * DNH
    rows = S // steps
    n = rows // CH

    def body(qkv_ref, bg_ref, st_ref, do_ref, dqkv_ref, dbg_ref, dstate):
        @pl.when(pl.program_id(0) == 0)
        def _():
            dstate[...] = jnp.zeros(dstate.shape, f32)
        _, vjp = jax.vjp(lambda a, g, s: _delta_chunks(a, g, s, True), qkv_ref[...], bg_ref[...], st_ref[0])
        do = [_stack([do_ref[b, i * CH:(i + 1) * CH, h * DND:(h + 1) * DND] for b in range(B) for h in range(DNH)]) for i in range(n)]
        dqkv, dbg, ds = vjp((do, dstate[...]))
        dqkv_ref[...] = dqkv
        dbg_ref[...] = dbg
        dstate[...] = ds

    rev = lambda c: steps - 1 - c
    return pl.pallas_call(
        body, name="delta_bwd", grid=(steps,),
        in_specs=[pl.BlockSpec((B, rows, 3 * DNH * DND), lambda c: (0, rev(c), 0)), pl.BlockSpec((B, rows, LANE), lambda c: (0, rev(c), 0)),
                  pl.BlockSpec((1, G, DND, DND), lambda c: (rev(c), 0, 0, 0)),
                  pl.BlockSpec((B, rows, DNH * DND), lambda c: (0, rev(c), 0))],
        out_specs=[pl.BlockSpec((B, rows, 3 * DNH * DND), lambda c: (0, rev(c), 0)), pl.BlockSpec((B, rows, LANE), lambda c: (0, rev(c), 0))],
        out_shape=[jax.ShapeDtypeStruct((B, S, 3 * DNH * DND), f32), jax.ShapeDtypeStruct((B, S, LANE), f32)],
        scratch_shapes=[pltpu.VMEM((G, DND, DND), f32)], compiler_params=_cp(("arbitrary",)),
    )(qkvn, bg, states, do)


GELU_C0, GELU_C1 = math.sqrt(2.0 / math.pi), 0.044715


def _ffn_specs(S):
    nblk = DFF // LANE
    return [pl.BlockSpec((1, S, LANE), lambda i, b: (b, 0, i)), pl.BlockSpec((1, S, LANE), lambda i, b: (b, 0, nblk + i)),
            pl.BlockSpec((FK, LANE), lambda i, b: (0, i)), pl.BlockSpec((FK, LANE), lambda i, b: (0, nblk + i))]


def ffnconv_fwd(up, conv_w):
    B, S, _ = up.shape
    rows = min(FFN_ROWS, S)

    def body(g_ref, v_ref, gw_ref, vw_ref, o_ref, xg, xv):
        _stage_rows(xg, g_ref[0].astype(f32))
        _stage_rows(xv, v_ref[0].astype(f32))
        gw, vw = gw_ref[...], vw_ref[...]
        for r in range(0, S, rows):
            g, _ = _conv_rows(xg, gw, FK, r, rows)
            v, _ = _conv_rows(xv, vw, FK, r, rows)
            t = jnp.tanh(GELU_C0 * (g * (1.0 + GELU_C1 * (g * g))))
            o_ref[0, pl.ds(r, rows), :] = (0.5 * g * (1.0 + t) * v).astype(o_ref.dtype)

    return pl.pallas_call(
        body, name="ffnconv_fwd", grid=(DFF // LANE, B), in_specs=_ffn_specs(S),
        out_specs=pl.BlockSpec((1, S, LANE), lambda i, b: (b, 0, i)), out_shape=jax.ShapeDtypeStruct((B, S, DFF), bf16),
        scratch_shapes=[pltpu.VMEM((S + 8, LANE), f32)] * 2, compiler_params=_cp(("parallel", "parallel")),
    )(up, up, conv_w, conv_w)


def ffnconv_bwd(up, conv_w, dact):
    B, S, _ = up.shape
    rows = min(FFN_ROWS, S)

    def body(g_ref, v_ref, gw_ref, vw_ref, dy_ref, dx_ref, dw_ref, xg, xv, dg, dv):
        _stage_rows(xg, g_ref[0].astype(f32))
        _stage_rows(xv, v_ref[0].astype(f32))
        gw, vw = gw_ref[...], vw_ref[...]
        dgw = [jnp.zeros((8, LANE), f32) for _ in range(FK)]
        dvw = [jnp.zeros((8, LANE), f32) for _ in range(FK)]
        for r in range(0, S, rows):
            g, gwins = _conv_rows(xg, gw, FK, r, rows)
            v, vwins = _conv_rows(xv, vw, FK, r, rows)
            g2 = g * g
            t = jnp.tanh(GELU_C0 * (g * (1.0 + GELU_C1 * g2)))
            half = 0.5 * (1.0 + t)
            dgelu = half + (0.5 * GELU_C0) * g * (1.0 - t * t) * (1.0 + (3.0 * GELU_C1) * g2)
            dy = dy_ref[0, pl.ds(r, rows), :].astype(f32)
            dvc = dy * (g * half)
            dgc = dy * v * dgelu
            dg[pl.ds(r, rows), :] = dgc
            dv[pl.ds(r, rows), :] = dvc
            for j in range(FK):
                dgw[j] = dgw[j] + _fold8(dgc * gwins[j])
                dvw[j] = dvw[j] + _fold8(dvc * vwins[j])
        dg[S:S + 8] = jnp.zeros((8, LANE), f32)
        dv[S:S + 8] = jnp.zeros((8, LANE), f32)
        for r in range(0, S, rows):
            dx_ref[0, 0, pl.ds(r, rows), :] = _conv_rows_t(dg, gw, FK, r, rows).astype(dx_ref.dtype)
            dx_ref[1, 0, pl.ds(r, rows), :] = _conv_rows_t(dv, vw, FK, r, rows).astype(dx_ref.dtype)

        @pl.when(pl.program_id(1) == 0)
        def _():
            dw_ref[...] = jnp.zeros(dw_ref.shape, f32)
        dw_ref[0] += jnp.concatenate([jnp.sum(d, axis=0, keepdims=True) for d in dgw], axis=0)
        dw_ref[1] += jnp.concatenate([jnp.sum(d, axis=0, keepdims=True) for d in dvw], axis=0)

    return pl.pallas_call(
        body, name="ffnconv_bwd", grid=(DFF // LANE, B),
        in_specs=_ffn_specs(S) + [pl.BlockSpec((1, S, LANE), lambda i, b: (b, 0, i))],
        out_specs=[pl.BlockSpec((2, 1, S, LANE), lambda i, b: (0, b, 0, i)), pl.BlockSpec((2, FK, LANE), lambda i, b: (0, 0, i))],
        out_shape=[jax.ShapeDtypeStruct((2, B, S, DFF), bf16), jax.ShapeDtypeStruct((2, FK, DFF), f32)],
        scratch_shapes=[pltpu.VMEM((S + 8, LANE), f32)] * 4, compiler_params=_cp(("parallel", "arbitrary")),
    )(up, up, conv_w, conv_w, dact)


def ada_fwd(c_all, ada_w, ada_b):
    def body(c_ref, w_ref, b_ref, o_ref):
        c = c_ref[...]
        act = (c * jax.nn.sigmoid(c)).astype(bf16)
        o_ref[...] = jnp.dot(act, w_ref[...].astype(bf16), preferred_element_type=f32) + b_ref[...]

    return pl.pallas_call(body, name="ada_fwd", out_shape=jax.ShapeDtypeStruct((c_all.shape[0], ada_w.shape[1]), f32),
                          compiler_params=pltpu.CompilerParams(vmem_limit_bytes=VMEM_LIMIT))(c_all, ada_w, ada_b)


def ada_bwd(c_all, dmod):
    def body(c_ref, d_ref, o_ref):
        c = c_ref[...]
        act = (c * jax.nn.sigmoid(c)).astype(bf16)
        o_ref[...] = lax.dot_general(act, d_ref[...].astype(bf16), (((0,), (0,)), ((), ())), preferred_element_type=f32)

    return pl.pallas_call(body, name="ada_bwd", out_shape=jax.ShapeDtypeStruct((c_all.shape[1], dmod.shape[1]), f32),
                          compiler_params=pltpu.CompilerParams(vmem_limit_bytes=VMEM_LIMIT))(c_all, dmod)


def loss_head(h1, y2, target, g2, w):
    def fn(t, b, c):
        h, y, tg = [v.astype(f32) for v in t]

        def loss_fn(h, y, g, w):
            e = h + g * _rms_vjp(y, w) - tg
            return 0.5 * jnp.sum(jnp.mean(e * e, axis=-1))

        loss, grads = jax.value_and_grad(loss_fn, argnums=(0, 1, 2, 3))(h, y, b[0], c[0])
        return [grads[0], grads[1]], [grads[2], grads[3], jnp.full((1, LANE), loss, f32)]

    return rowcall("loss_head", fn, [(h1, D, 0), (y2, D, 0), (target, D, 0)], [g2], [w], [(D, f32), (D, bf16)],
                   [(1, D), (1, D), (1, LANE)])


def adamw(w, gparts, m, v, name):
    R, C = w.shape
    P = gparts.shape[0]
    budget = 2 * 1024 * 1024
    tr, tc = R, C
    if R * C * 4 > budget and R % 8 == 0:
        tr = max(t for t in range(8, R + 1, 8) if R % t == 0 and t * C * 4 <= budget)
    elif R * C * 4 > budget:
        tc = max(t for t in range(LANE, C + 1, LANE) if C % t == 0 and R * t * 4 <= budget)

    def body(w_ref, g_ref, m_ref, v_ref, go, do, mo, vo):
        g = g_ref[0].astype(f32)
        for p in range(1, P):
            g = g + g_ref[p].astype(f32)
        m2 = B1 * m_ref[...] + (1.0 - B1) * g
        v2 = B2 * v_ref[...] + (1.0 - B2) * jnp.square(g)
        m_hat = m2 * (1.0 / (1.0 - B1 ** STEP))
        v_hat = v2 * (1.0 / (1.0 - B2 ** STEP))
        go[...] = g
        do[...] = -LR * (m_hat / (jnp.sqrt(v_hat) + EPS) + WD * w_ref[...])
        mo[...] = m2
        vo[...] = v2

    blk = pl.BlockSpec((tr, tc), lambda i, j: (i, j))
    return pl.pallas_call(
        body, name=name, grid=(R // tr, C // tc), in_specs=[blk, pl.BlockSpec((P, tr, tc), lambda i, j: (0, i, j)), blk, blk],
        out_specs=[blk] * 4, out_shape=[jax.ShapeDtypeStruct((R, C), f32)] * 4, compiler_params=_cp(("parallel", "parallel")),
    )(w, gparts, m, v)


def _pack_w_in(wt):
    aq, ak, av, dqkv, dz, dbeta, da, ga, gd = jnp.split(wt, np.cumsum(IN_SPLITS)[:-1].tolist(), axis=0)
    ba = jnp.pad(jnp.concatenate([dbeta, da], axis=0), ((0, LANE - 2 * DNH), (0, 0)))
    return jnp.concatenate([ga, gd, aq, dqkv, dz, ak, av, ba], axis=0)


def _unpack_w_in(p):
    row = lambda cb, n: p[cb * LANE: cb * LANE + n]
    ba = row(CB_BA, 2 * DNH)
    return jnp.concatenate([row(CB_AQ, HQ * HD), row(CB_AK, HKV * HD), row(CB_AV, HKV * HD), row(CB_DQKV, 3 * DNH * DND),
                            row(CB_DZ, DNH * DND), ba[:DNH], ba[DNH:], row(CB_GA, D), row(CB_GD, D)], axis=0)


def _cols_gathered(g):
    return g.transpose(1, 0, 2).reshape(g.shape[1], NDEV * g.shape[2])


def _cols_split(w):
    r = w.shape[0]
    return w.reshape(r, NDEV, w.shape[1] // NDEV).transpose(1, 0, 2)


def kernel(x, c, ada_w, ada_b, norm_mix_pre, norm_mix_post, norm_ffn_pre, norm_ffn_post, w_in, dn_conv_w, dn_a_log, dn_dt_bias, dn_norm_w, attn_sinks, rel_bias, w_attn_branch, w_dn_branch, w_out, ffn_w_up, ffn_conv_w, ffn_w_down, loss_target, m_ada_w, m_ada_b, m_norm_mix_pre, m_norm_mix_post, m_norm_ffn_pre, m_norm_ffn_post, m_w_in, m_dn_conv_w, m_dn_a_log, m_dn_dt_bias, m_dn_norm_w, m_attn_sinks, m_rel_bias, m_w_attn_branch, m_w_dn_branch, m_w_out, m_ffn_w_up, m_ffn_conv_w, m_ffn_w_down, v_ada_w, v_ada_b, v_norm_mix_pre, v_norm_mix_post, v_norm_ffn_pre, v_norm_ffn_post, v_w_in, v_dn_conv_w, v_dn_a_log, v_dn_dt_bias, v_dn_norm_w, v_attn_sinks, v_rel_bias, v_w_attn_branch, v_w_dn_branch, v_w_out, v_ffn_w_up, v_ffn_conv_w, v_ffn_w_down):
    B, S, _ = x.shape
    T = B * S
    me = 4 * lax.axis_index("x") + 2 * lax.axis_index("y") + lax.axis_index("c")
    big = dict(w_in=w_in, dn_conv_w=dn_conv_w, w_attn_branch=w_attn_branch, w_dn_branch=w_dn_branch, w_out=w_out,
               ffn_w_up=ffn_w_up, ffn_conv_w=ffn_conv_w, ffn_w_down=ffn_w_down)
    big_names = list(big)

    first, mid, late = ["w_in", "dn_conv_w"], ["w_attn_branch", "w_dn_branch", "w_out"], ["ffn_w_up", "ffn_conv_w", "ffn_w_down"]
    transposed = ("w_in", "ffn_w_up")
    local = lambda n, a: a[0].T if n in transposed else a[0]
    shard = lambda names: [local(n, big[n]).astype(bf16) for n in names]
    *got, c_all = _exchange(shard(first) + [c], "gather_w_in", two_level=True)
    gw = dict(zip(first, got))
    c_all = c_all.reshape(NDEV * B, D)

    wp = _pack_w_in(gw["w_in"].reshape(IN_DIM, D))
    conv_dn = _cols_gathered(gw["dn_conv_w"]).astype(f32)

    ncol = ada_w.shape[2]
    ada_b_mine = lax.dynamic_slice_in_dim(ada_b, me * ncol, ncol, axis=1)
    mod_cols = ada_fwd(c_all, ada_w[0], ada_b_mine)
    (mod_g,) = _exchange([mod_cols], "gather_mod")
    gathering_mid = _copy_start(shard(mid), "gather_branches_start", gather=True, after=mod_g)
    gathering_ffn = _copy_start(shard(late), "gather_ffn_start", gather=True, after=gathering_mid[-1])
    mod_g = mod_g + gathering_ffn[-1][0, 0]
    mod = lax.dynamic_slice_in_dim(mod_g, me * B, B, axis=1).transpose(1, 0, 2).reshape(B, NMOD * D)
    sh1, sc1, g1, sh2, sc2, g2 = [mod[:, i * D:(i + 1) * D].reshape(B, 1, D) for i in range(NMOD)]

    onehot = (jnp.asarray(_bucket_table()).reshape(1, -1) == jnp.arange(NBUCK, dtype=jnp.int32)[:, None]).astype(f32)
    bias = mm(rel_bias.T, onehot, "nn", f32, "bias_table", tn=8192, precision=HI).reshape(HQ, WIN, 2 * WIN)
    sinks = attn_sinks.reshape(HQ, 1, 1)
    a_log_pad = jnp.pad(dn_a_log, ((0, 0), (DNH, LANE - 2 * DNH)))
    dt_bias_pad = jnp.pad(dn_dt_bias, ((0, 0), (DNH, LANE - 2 * DNH)))

    (u1,) = rowcall_fwd("mix_pre", f_rms_mod, [(x, D, 0)], [sc1, sh1], [norm_mix_pre], [(D, bf16)])
    proj = mm(u1.reshape(T, D), wp, "nt", bf16, "proj", tm=512, tn=CB_BA * LANE, b_cols=(0, 1)).reshape(B, S, CB_BA * LANE)
    ba = mm(u1.reshape(T, D), wp, "nt", f32, "proj_ba", tn=LANE, b_cols=(CB_BA, 1)).reshape(B, S, LANE)
    ya = attn_fwd(proj, bias, sinks)
    qkvn = dnconv_fwd(proj, conv_dn)
    (bg,) = rowcall_fwd("dn_gate", f_gate, [(ba, LANE, 0)], [], [a_log_pad, dt_bias_pad], [(LANE, f32)])
    o_dn, states = delta_fwd(qkvn, bg)
    gw.update(zip(mid, _copy_finish(gathering_mid, len(mid), o_dn, "gather_branches_finish", gather=True)))
    wa = _cols_gathered(gw["w_attn_branch"])
    wd = _cols_gathered(gw["w_dn_branch"])
    wo = gw["w_out"].reshape(D, D)
    (yd,) = rowcall_fwd("dn_out", f_dnout, [(o_dn, DNH * DND, 0), (proj, DNH * DND, CB_DZ // 4)], [], [dn_norm_w], [(DNH * DND, bf16)])
    pa = mm(ya.reshape(T, HQ * HD), wa, "nn", bf16, "attn_branch").reshape(B, S, D)
    pd = mm(yd.reshape(T, DNH * DND), wd, "nn", bf16, "dn_branch").reshape(B, S, D)
    merge_tok = [(proj, D, CB_GA // 8), (proj, D, CB_GD // 8), (pa, D, 0), (pd, D, 0)]
    (merged,) = rowcall_fwd("merge", f_merge, merge_tok, [], [], [(D, bf16)])
    y1 = mm(merged.reshape(T, D), wo, "nn", bf16, "mix_out").reshape(B, S, D)
    post_pre = ([(x, D, 0), (y1, D, 0)], [g1, sc2, sh2], [norm_mix_post, norm_ffn_pre])
    h1, u2 = rowcall_fwd("mix_post_ffn_pre", f_post_pre, *post_pre, [(D, f32), (D, bf16)])
    gw.update(zip(late, _copy_finish(gathering_ffn, len(late), h1, "gather_ffn_finish", gather=True)))
    wup = gw["ffn_w_up"].reshape(2 * DFF, D)
    conv_ffn = _cols_gathered(gw["ffn_conv_w"]).astype(f32)
    wdown = gw["ffn_w_down"].reshape(DFF, D)
    up = mm(u2.reshape(T, D), wup, "nt", bf16, "ffn_up", tn=2816).reshape(B, S, 2 * DFF)
    act = ffnconv_fwd(up, conv_ffn)
    y2 = mm(act.reshape(T, DFF), wdown, "nn", bf16, "ffn_down", tk=2816).reshape(B, S, D)

    dh1_a, dy2, dg2, dw_ffn_post, loss_b = loss_head(h1, y2, loss_target, g2, norm_ffn_post)
    dy2f = dy2.reshape(T, D)
    dact = mm(dy2f, wdown, "nt", bf16, "ffn_down_dx", tn=2816).reshape(B, S, DFF)
    g_wdown = mm(act.reshape(T, DFF), dy2f, "tn", bf16, "ffn_down_dw", tm=1408, tk=2048)
    in_flight = []

    def send_off(d, tag):
        in_flight.append((d, _copy_start([a.astype(bf16) for a in d.values()], "scatter_" + tag + "_start")))
        return in_flight[-1][1][-1][0, 0]

    started = send_off(dict(ffn_w_down=g_wdown.reshape(NDEV, DFF // NDEV, D)), "ffn_down")
    dup, g_conv_ffn = ffnconv_bwd(up, conv_ffn + started, dact)
    dupf = dup.reshape(2, T, DFF)
    g_conv_ffn = g_conv_ffn.transpose(1, 0, 2).reshape(FK, 2 * DFF)
    du2 = mm(dupf, wup, "nn", bf16, "ffn_up_dx", tk=2816).reshape(B, S, D)
    g_wup = mm(dupf, u2.reshape(T, D), "tn", bf16, "ffn_up_dw", tm=1408, tk=2048)
    started = send_off(dict(ffn_w_up=g_wup.reshape(NDEV, 2 * DFF // NDEV, D), ffn_conv_w=_cols_split(g_conv_ffn)), "ffn_up")
    post_pre = (post_pre[0], [g1 + started, sc2, sh2], post_pre[2])
    dh1, dy1, dg1, dsc2, dsh2, dw_mix_post, dw_ffn_pre = rowcall_bwd(
        "mix_post_ffn_pre_bwd", functools.partial(f_post_pre, rms=_rms_vjp), *post_pre, [(dh1_a, D, 0), (du2, D, 0)], [(0, f32), (1, bf16)])
    dy1f = dy1.reshape(T, D)
    dmerged = mm(dy1f, wo, "nt", bf16, "mix_out_dx").reshape(B, S, D)
    g_wo = mm(merged.reshape(T, D), dy1f, "tn", bf16, "mix_out_dw", tk=2048)
    dproj = lax.empty((B, S, NP), bf16)
    dproj, dpa, dpd = rowcall_bwd("merge_bwd", f_merge, merge_tok, [], [], [(dmerged, D, 0)],
                                  [(0, bf16), (1, bf16), (2, bf16), (3, bf16)], join_first=2, into=(dproj, CB_GA // 16))
    dpaf, dpdf = dpa.reshape(T, D), dpd.reshape(T, D)
    dya = mm(dpaf, wa, "nt", bf16, "attn_branch_dx").reshape(B, S, HQ * HD)
    g_wa = mm(ya.reshape(T, HQ * HD), dpaf, "tn", bf16, "attn_branch_dw", tk=2048)
    dyd = mm(dpdf, wd, "nt", bf16, "dn_branch_dx").reshape(B, S, DNH * DND)
    g_wd = mm(yd.reshape(T, DNH * DND), dpdf, "tn", bf16, "dn_branch_dw", tk=2048)
    dproj, do_dn, dw_dn_norm = rowcall_bwd("dn_out_bwd", functools.partial(f_dnout, rms=_rms_vjp), [(o_dn, DNH * DND, 0), (proj, DNH * DND, CB_DZ // 4)], [], [dn_norm_w],
                                           [(dyd, DNH * DND, 0)], [(1, bf16), (0, f32)], into=(dproj, CB_DZ // 4))
    started = send_off(dict(w_attn_branch=_cols_split(g_wa), w_dn_branch=_cols_split(g_wd), w_out=g_wo.reshape(NDEV, D // NDEV, D)), "branches")
    dqkvn, dbg = delta_bwd(qkvn, bg + started, states, do_dn)
    dproj, da_log_pad, ddt_bias_pad = rowcall_bwd("dn_gate_bwd", f_gate, [(ba, LANE, 0)], [], [a_log_pad, dt_bias_pad],
                                                  [(dbg, LANE, 0)], [(0, bf16)], into=(dproj, CB_BA))
    dproj, g_conv_dn = dnconv_bwd(proj, conv_dn, dqkvn, dproj)
    dproj, dk, dv, dbias, dsinks = attn_bwd(proj, bias, sinks, dya, dproj)
    dproj = lax.dynamic_update_slice(dproj, jnp.concatenate([dk, dv], axis=2), (0, 0, CB_AK * LANE)).reshape(T, NP)
    g_wp = mm(dproj, u1.reshape(T, D), "tn", bf16, "proj_dw", tm=1664, tk=1024)
    started = send_off(dict(w_in=_unpack_w_in(g_wp).reshape(NDEV, IN_DIM // NDEV, D), dn_conv_w=_cols_split(g_conv_dn)), "w_in")
    du1 = mm(dproj, wp, "nn", bf16, "proj_dx", tm=512, tk=NP).reshape(B, S, D)
    grad_x, dsc1, dsh1, dw_mix_pre = rowcall_bwd("mix_pre_bwd", functools.partial(f_rms_mod, rms=_rms_vjp), [(x, D, 0)], [sc1 + started, sh1], [norm_mix_pre],
                                                 [(du1, D, 0)], [(0, f32)], add=(dh1, D, 0))
    g_rel = mm(dbias.reshape(HQ, WIN * 2 * WIN), onehot, "nt", f32, "rel_bias_dw", tk=8192, precision=HI)

    dmod = jnp.concatenate([dsh1, dsc1, dg1, dsh2, dsc2, dg2], axis=2).reshape(B, NMOD * D)

    zrow = lambda a: jnp.concatenate([a.reshape(1, -1), jnp.zeros((B - 1, a.size), f32)], axis=0)
    small_g = jnp.concatenate([
        dmod, dw_mix_pre.reshape(B, D), dw_mix_post.reshape(B, D), dw_ffn_pre.reshape(B, D), dw_ffn_post.reshape(B, D),
        da_log_pad.reshape(B, LANE)[:, DNH:2 * DNH], ddt_bias_pad.reshape(B, LANE)[:, DNH:2 * DNH], dw_dn_norm.reshape(B, DND),
        zrow(dsinks), zrow(g_rel.T), loss_b.reshape(B, LANE)[:, :1], jnp.zeros((B, SMALL_PAD - SMALL_N - 1), f32)], axis=1)
    (small_all,) = _exchange([small_g], "gather_small")
    dmod_cols = lax.dynamic_slice_in_dim(small_all.reshape(NDEV * B, SMALL_PAD), me * ncol, ncol, axis=1)
    g_ada_w = ada_bwd(c_all, dmod_cols)
    parts = {}
    for i, (d, started) in enumerate(in_flight):
        parts.update(zip(d, _copy_finish(started, len(d), g_ada_w, "scatter_finish_%d" % i)))
    small_w = dict(ada_b=(ada_b, m_ada_b, v_ada_b), norm_mix_pre=(norm_mix_pre, m_norm_mix_pre, v_norm_mix_pre),
                   norm_mix_post=(norm_mix_post, m_norm_mix_post, v_norm_mix_post), norm_ffn_pre=(norm_ffn_pre, m_norm_ffn_pre, v_norm_ffn_pre),
                   norm_ffn_post=(norm_ffn_post, m_norm_ffn_post, v_norm_ffn_post), dn_a_log=(dn_a_log, m_dn_a_log, v_dn_a_log),
                   dn_dt_bias=(dn_dt_bias, m_dn_dt_bias, v_dn_dt_bias), dn_norm_w=(dn_norm_w, m_dn_norm_w, v_dn_norm_w),
                   attn_sinks=(attn_sinks, m_attn_sinks, v_attn_sinks), rel_bias=(rel_bias, m_rel_bias, v_rel_bias))

    def pack(i, fill):
        row = jnp.concatenate([small_w[n][i].reshape(1, -1) for n, _ in SMALL], axis=1)
        return jnp.pad(row, ((0, 0), (0, SMALL_PAD - SMALL_N)), constant_values=fill)

    small_out = adamw(pack(0, 0.0), small_all.reshape(NDEV * B, 1, SMALL_PAD), pack(1, 0.0), pack(2, 1.0), "adamw_small")
    loss = small_out[0][0, SMALL_N]

    res = {}
    off = 0
    for n, size in SMALL:
        shp = small_w[n][0].shape
        res[n] = [o[:, off:off + size].reshape(shp) for o in small_out]
        off += size
    res["ada_w"] = [o[None] for o in adamw(ada_w[0], g_ada_w[None], m_ada_w[0], v_ada_w[0], "adamw_ada_w")]
    moments = dict(w_in=(m_w_in, v_w_in), dn_conv_w=(m_dn_conv_w, v_dn_conv_w), w_attn_branch=(m_w_attn_branch, v_w_attn_branch),
                   w_dn_branch=(m_w_dn_branch, v_w_dn_branch), w_out=(m_w_out, v_w_out), ffn_w_up=(m_ffn_w_up, v_ffn_w_up),
                   ffn_conv_w=(m_ffn_conv_w, v_ffn_conv_w), ffn_w_down=(m_ffn_w_down, v_ffn_w_down))
    for n in big_names:
        outs = adamw(local(n, big[n]), parts[n], local(n, moments[n][0]), local(n, moments[n][1]), "adamw_" + n)
        res[n] = [(o.T if n in transposed else o)[None] for o in outs]

    order = ["ada_w", "ada_b", "norm_mix_pre", "norm_mix_post", "norm_ffn_pre", "norm_ffn_post", "w_in", "dn_conv_w", "dn_a_log",
             "dn_dt_bias", "dn_norm_w", "attn_sinks", "rel_bias", "w_attn_branch", "w_dn_branch", "w_out", "ffn_w_up", "ffn_conv_w",
             "ffn_w_down"]
    return (loss, grad_x, *[res[n][0] for n in order], *[res[n][1] for n in order], *[res[n][2] for n in order],
            *[res[n][3] for n in order])
```

```python
import functools
import math

import numpy as np
import jax
import jax.numpy as jnp
from jax import lax
from jax.experimental import pallas as pl
from jax.experimental.pallas import tpu as pltpu

f32 = jnp.float32
bf16 = jnp.bfloat16
HI = lax.Precision.HIGHEST
MID = lax.Precision.HIGH
MESH = pl.DeviceIdType.MESH

NDEV = 8
D = 1024
HQ, HKV, HD, WIN, NBUCK, MAXDIST = 8, 2, 64, 128, 32, 128
DNH, DND, DNK, CH = 4, 128, 4, 64
DFF, FK = 2816, 3
NMOD = 6
RMS_EPS = 1e-6
L2_EPS = 1e-6
NEG_INF = -1e30
LR, B1, B2, EPS, WD, STEP = 0.001, 0.9, 0.999, 1e-08, 0.01, 10

LANE = 128
CB_GA, CB_GD, CB_AQ, CB_DQKV, CB_DZ, CB_AK, CB_AV, CB_BA, NPB = 0, 8, 16, 20, 32, 36, 37, 38, 39
NP = NPB * LANE
IN_SPLITS = (HQ * HD, HKV * HD, HKV * HD, 3 * DNH * DND, DNH * DND, DNH, DNH, D, D)
IN_DIM = sum(IN_SPLITS)
VMEM_LIMIT = 56 * 1024 * 1024

SMALL = (("ada_b", NMOD * D), ("norm_mix_pre", D), ("norm_mix_post", D), ("norm_ffn_pre", D), ("norm_ffn_post", D),
         ("dn_a_log", DNH), ("dn_dt_bias", DNH), ("dn_norm_w", DND), ("attn_sinks", HQ), ("rel_bias", NBUCK * HQ))
SMALL_N = sum(n for _, n in SMALL)
SMALL_PAD = 10752


def _cp(sem):
    return pltpu.CompilerParams(dimension_semantics=sem, vmem_limit_bytes=VMEM_LIMIT)


def _pick(dim, target):
    if dim <= target:
        return dim
    best = None
    for d in range(LANE, target + 1, LANE):
        if dim % d == 0:
            best = d
    assert best is not None, (dim, target)
    return best


def _me():
    x, y, c = lax.axis_index("x"), lax.axis_index("y"), lax.axis_index("c")
    return x, y, c, 4 * x + 2 * y + c


def _peer(x, y, c, k):
    px = 1 - x if k & 4 else x
    py = 1 - y if k & 2 else y
    pc = 1 - c if k & 1 else c
    return (px, py, pc), 4 * px + 2 * py + pc


class _Comm:
    def __init__(self, arrs, two_level=False):
        self.arrs, self.n, self.two_level = list(arrs), len(arrs), two_level
        self.out_shape = [jax.ShapeDtypeStruct((NDEV,) + a.shape, a.dtype) for a in arrs]
        nsem = self.n * (NDEV - 1)
        self.scratch = [pltpu.SemaphoreType.DMA((nsem,)), pltpu.SemaphoreType.DMA((nsem,)), pltpu.SemaphoreType.DMA((self.n,))]
        self.specs = [pl.BlockSpec(memory_space=pl.ANY)] * self.n

    def phases(self, ins, out, send, recv, loc):
        x, y, c, me = _me()

        def remote(a, k, src, dst, to):
            s = a * (NDEV - 1) + k - 1
            return pltpu.make_async_remote_copy(src_ref=src, dst_ref=dst, send_sem=send.at[s], recv_sem=recv.at[s],
                                                device_id=to, device_id_type=MESH)

        def local(a):
            return pltpu.make_async_copy(ins[a], out[a].at[me], loc.at[a])

        if not self.two_level:
            def mine(a, k):
                peer, pid = _peer(x, y, c, k)
                return remote(a, k, ins[a], out[a].at[me], peer)

            def theirs(a, k):
                peer, pid = _peer(x, y, c, k)
                return remote(a, k, ins[a], out[a].at[pid], peer)

            def start():
                for a in range(self.n):
                    local(a).start()
                    for k in range(1, NDEV):
                        mine(a, k).start()

            def forward():
                pass

            def finish():
                for a in range(self.n):
                    for k in range(1, NDEV):
                        mine(a, k).wait_send()
                    for k in range(1, NDEV):
                        theirs(a, k).wait_recv()
                    local(a).wait()

            return start, forward, finish

        sibling = (x, y, 1 - c)
        chips = [(1 - x, y), (x, 1 - y), (1 - x, 1 - y)]
        slot = lambda px, py, pc: 4 * px + 2 * py + pc

        def own(a, k, to):
            return remote(a, k, ins[a], out[a].at[me], to)

        def landed(a, k, frm):
            return remote(a, k, ins[a], out[a].at[slot(*frm)], frm)

        def passed(a, j):
            rows = out[a].at[slot(*chips[j], c)]
            return remote(a, 5 + j, rows, rows, sibling)

        def start():
            for a in range(self.n):
                local(a).start()
                own(a, 1, sibling).start()
                for j, chip in enumerate(chips):
                    own(a, 2 + j, (*chip, c)).start()

        def forward():
            for a in range(self.n):
                for j, chip in enumerate(chips):
                    landed(a, 2 + j, (*chip, c)).wait_recv()
                    passed(a, j).start()

        def finish():
            for a in range(self.n):
                landed(a, 1, sibling).wait_recv()
                for j, chip in enumerate(chips):
                    remote(a, 5 + j, ins[a], out[a].at[slot(*chip, 1 - c)], sibling).wait_recv()
                own(a, 1, sibling).wait_send()
                for j, chip in enumerate(chips):
                    own(a, 2 + j, (*chip, c)).wait_send()
                    passed(a, j).wait_send()
                local(a).wait()

        return start, forward, finish


def _copy_start(arrs, name, gather=False, after=None):
    n = len(arrs)
    order = [] if after is None else [after]
    n_in = 2 * n + len(order)
    block = (lambda ref, j: ref) if gather else (lambda ref, j: ref.at[j])

    def body(*refs):
        ins, lands, send, recv, own, token = refs[:n], refs[n:2 * n], refs[n_in], refs[n_in + 1], refs[n_in + 2], refs[-1]
        x, y, c, me = _me()
        for a in range(n):
            pltpu.make_async_copy(block(ins[a], me), lands[a].at[me], own.at[a]).start()
            for k in range(1, NDEV):
                peer, pid = _peer(x, y, c, k)
                s = a * (NDEV - 1) + k - 1
                pltpu.make_async_remote_copy(src_ref=block(ins[a], pid), dst_ref=lands[a].at[me], send_sem=send.at[s],
                                             recv_sem=recv.at[s], device_id=peer, device_id_type=MESH).start()
        token[...] = jnp.zeros(token.shape, token.dtype)

    hbm, sem = pl.BlockSpec(memory_space=pltpu.HBM), pl.BlockSpec(memory_space=pltpu.SEMAPHORE)
    nsem = n * (NDEV - 1)
    land_shapes = [((NDEV,) + a.shape if gather else a.shape) for a in arrs]
    thru = [pltpu.HBM(a.shape, a.dtype) for a in arrs] + [pltpu.HBM(shp, a.dtype) for shp, a in zip(land_shapes, arrs)]
    return pl.pallas_call(
        body, name=name, in_specs=[hbm] * (2 * n) + [pl.BlockSpec(memory_space=pl.ANY)] * len(order),
        out_shape=(pltpu.SemaphoreType.DMA((nsem,)), pltpu.SemaphoreType.DMA((nsem,)), pltpu.SemaphoreType.DMA((n,)), *thru,
                   jax.ShapeDtypeStruct((8, LANE), f32)),
        out_specs=(sem, sem, sem, *[hbm] * (2 * n), pl.BlockSpec(memory_space=pltpu.VMEM)),
        input_output_aliases={i: 3 + i for i in range(2 * n)},
        compiler_params=pltpu.CompilerParams(has_side_effects=pltpu.SideEffectType.DATAFLOW_SIDE_EFFECTING),
    )(*[pltpu.with_memory_space_constraint(a, pltpu.HBM) for a in arrs],
      *[pltpu.with_memory_space_constraint(lax.empty(shp, a.dtype), pltpu.HBM) for shp, a in zip(land_shapes, arrs)], *order)


def _copy_finish(started, n, after, name, gather=False):
    send, recv, own, *rest = started
    srcs, lands = rest[:n], rest[n:2 * n]
    block = (lambda ref, j: ref) if gather else (lambda ref, j: ref.at[j])

    def body(*refs):
        ins, lnd, send_ref, recv_ref, own_ref = refs[:n], refs[n:2 * n], refs[2 * n], refs[2 * n + 1], refs[2 * n + 2]
        x, y, c, me = _me()
        for a in range(n):
            pltpu.make_async_copy(block(ins[a], me), lnd[a].at[me], own_ref.at[a]).wait()
            for k in range(1, NDEV):
                peer, pid = _peer(x, y, c, k)
                s = a * (NDEV - 1) + k - 1
                cp = pltpu.make_async_remote_copy(src_ref=block(ins[a], pid), dst_ref=lnd[a].at[pid], send_sem=send_ref.at[s],
                                                  recv_sem=recv_ref.at[s], device_id=peer, device_id_type=MESH)
                cp.wait_send()
                cp.wait_recv()

    hbm, sem = pl.BlockSpec(memory_space=pltpu.HBM), pl.BlockSpec(memory_space=pltpu.SEMAPHORE)
    thru = [pltpu.HBM(a.shape, a.dtype) for a in srcs] + [pltpu.HBM(a.shape, a.dtype) for a in lands]
    out = pl.pallas_call(
        body, name=name, in_specs=[hbm] * (2 * n) + [sem, sem, sem, pl.BlockSpec(memory_space=pl.ANY)],
        out_shape=tuple(thru), out_specs=tuple([hbm] * (2 * n)), input_output_aliases={i: i for i in range(2 * n)},
        compiler_params=pltpu.CompilerParams(has_side_effects=pltpu.SideEffectType.DATAFLOW_SIDE_EFFECTING),
    )(*srcs, *lands, send, recv, own, after)
    return list(out[n:])


def _exchange(arrs, name, two_level=False):
    comm = _Comm(arrs, two_level)

    def body(*refs):
        start, forward, finish = comm.phases(refs[:comm.n], refs[comm.n:2 * comm.n], *refs[2 * comm.n:])
        start()
        forward()
        finish()

    return pl.pallas_call(body, name=name, out_shape=comm.out_shape, in_specs=comm.specs, out_specs=comm.specs,
                          scratch_shapes=comm.scratch, compiler_params=pltpu.CompilerParams(has_side_effects=True))(*arrs)


def mm(a, b, mode, out_dtype, name, tm=1024, tn=1024, tk=1024, precision=None, b_cols=None):
    a_parts = a.shape[0] if a.ndim == 3 else 1
    b_parts = b.shape[0] if b.ndim == 3 else 1
    assert b_parts == 1 or mode == "tn"
    ash, bsh = (a.shape[-2], a.shape[-1] * a_parts), b.shape[-2:]
    if mode == "nn":
        (M, K), (K2, N) = ash, bsh
    elif mode == "nt":
        (M, K), (N, K2) = ash, bsh
    else:
        (K, M), (K2, N) = ash, (bsh[0], bsh[1] * b_parts)
    assert K == K2, (name, a.shape, b.shape)
    col0 = 0
    if b_cols is not None:
        assert mode in ("nn", "nt") and tn % LANE == 0
        col0, N = b_cols[0], b_cols[1] * tn
    if mode == "tn":
        tm, tn, tk = _pick(M // a_parts, tm), _pick(N // b_parts, tn), _pick(K, tk)
    else:
        tm, tn, tk = _pick(M, tm), _pick(N // b_parts, tn), _pick(K // a_parts, tk)
    nk = K // tk
    if mode == "tn" and a_parts > 1:
        per = M // tm // a_parts
        a_spec = pl.BlockSpec((None, tk, tm), lambda i, j, k: (i // per, k, i % per))
    elif mode == "tn":
        a_spec = pl.BlockSpec((tk, tm), lambda i, j, k: (k, i))
    elif a_parts > 1:
        per = nk // a_parts
        a_spec = pl.BlockSpec((None, tm, tk), lambda i, j, k: (k // per, i, k % per))
    else:
        a_spec = pl.BlockSpec((tm, tk), lambda i, j, k: (i, k))
    if mode == "nt":
        b_spec = pl.BlockSpec((tn, tk), lambda i, j, k: (col0 + j, k))
    elif b_parts > 1:
        per = N // tn // b_parts
        b_spec = pl.BlockSpec((None, tk, tn), lambda i, j, k: (j // per, k, j % per))
    else:
        b_spec = pl.BlockSpec((tk, tn), lambda i, j, k: (k, col0 + j))
    dims = {"nn": ((1,), (0,)), "nt": ((1,), (1,)), "tn": ((0,), (0,))}[mode]

    def body(a_ref, b_ref, o_ref, *scr):
        p = lax.dot_general(a_ref[...], b_ref[...], (dims, ((), ())), preferred_element_type=f32, precision=precision)
        if nk == 1:
            o_ref[...] = p.astype(o_ref.dtype)
        else:
            acc = scr[0]
            k = pl.program_id(2)

            @pl.when(k == 0)
            def _():
                acc[...] = p

            @pl.when(k > 0)
            def _():
                acc[...] += p

            @pl.when(k == nk - 1)
            def _():
                o_ref[...] = acc[...].astype(o_ref.dtype)

    return pl.pallas_call(
        body, name=name, grid=(M // tm, N // tn, nk), in_specs=[a_spec, b_spec],
        out_specs=pl.BlockSpec((tm, tn), lambda i, j, k: (i, j)), out_shape=jax.ShapeDtypeStruct((M, N), out_dtype),
        scratch_shapes=[pltpu.VMEM((tm, tn), f32)] if nk > 1 else [],
        compiler_params=_cp(("parallel", "parallel", "arbitrary")),
    )(a, b)


ROW_TILE = 512


def rowcall(name, fn, tok, bat, con, tok_out, acc_out, ts=ROW_TILE, into=None):
    B, S = tok[0][0].shape[:2]
    ts = min(ts, S)
    nt, nb, nc, no, na = len(tok), len(bat), len(con), len(tok_out), len(acc_out)
    nin = nt + nb + nc + (1 if into is not None else 0)

    ns = S // ts
    steps = B * ns

    def body(*refs):
        tr, br, cr = refs[:nt], refs[nt:nt + nb], refs[nt + nb:nt + nb + nc]
        orf, arf = refs[nin:nin + no], refs[nin + no:nin + no + na]
        bufs, sem = refs[nin + no + na:-1], refs[-1]
        s = pl.program_id(1)
        step = pl.program_id(0) * ns + s

        def tile_copy(i, k):
            w, cb = tok[i][1], tok[i][2]
            src = tr[i].at[k // ns, pl.ds((k % ns) * ts, ts), pl.ds(cb * w, w)]
            return pltpu.make_async_copy(src, bufs[i].at[k % RING], sem.at[i, k % RING])

        @pl.when(step == 0)
        def _():
            for k in range(min(RING - 1, steps)):
                for i in range(nt):
                    tile_copy(i, k).start()

        @pl.when(step + (RING - 1) < steps)
        def _():
            for i in range(nt):
                tile_copy(i, step + (RING - 1)).start()

        for i in range(nt):
            tile_copy(i, step).wait()
        touts, aouts = fn([bufs[i][step % RING] for i in range(nt)], [r[0] for r in br], [r[...] for r in cr])
        for r, v in zip(orf, touts):
            r[0] = v.astype(r.dtype)
        for r, v in zip(arf, aouts):
            @pl.when(s == 0)
            def _(r=r):
                r[...] = jnp.zeros(r.shape, r.dtype)
            r[0] += v.astype(f32)

    in_specs = [pl.BlockSpec(memory_space=pl.ANY) for _ in tok]
    in_specs += [pl.BlockSpec((1,) + a.shape[1:], lambda b, s: (b, 0, 0)) for a in bat]
    in_specs += [pl.BlockSpec(a.shape, lambda b, s, nd=a.ndim: (0,) * nd) for a in con]
    out_specs = [pl.BlockSpec((1, ts, w), lambda b, s: (b, s, 0)) for (w, _) in tok_out]
    out_specs += [pl.BlockSpec((1,) + shp, lambda b, s, nd=len(shp): (b,) + (0,) * nd) for shp in acc_out]
    out_shape = [jax.ShapeDtypeStruct((B, S, w), dt) for (w, dt) in tok_out]
    out_shape += [jax.ShapeDtypeStruct((B,) + shp, f32) for shp in acc_out]
    extra, aliases = [], {}
    if into is not None:
        buf, cb = into
        assert buf.dtype == tok_out[0][1]
        in_specs.append(pl.BlockSpec(memory_space=pl.ANY))
        out_specs[0] = pl.BlockSpec((1, ts, tok_out[0][0]), lambda b, s: (b, s, cb))
        out_shape[0] = jax.ShapeDtypeStruct(buf.shape, buf.dtype)
        extra, aliases = [buf], {nin - 1: 0}
    return pl.pallas_call(
        body, name=name, grid=(B, S // ts), in_specs=in_specs, out_specs=out_specs, out_shape=out_shape,
        input_output_aliases=aliases, compiler_params=_cp(("arbitrary", "arbitrary")),
        scratch_shapes=[pltpu.VMEM((RING, ts, w), a.dtype) for (a, w, _) in tok] + [pltpu.SemaphoreType.DMA((nt, RING))],
    )(*[t[0] for t in tok], *bat, *con, *extra)


def rowcall_fwd(name, f, tok, bat, con, tok_out, ts=2 * ROW_TILE):
    def fn(t, b, c):
        return f([v.astype(f32) for v in t], b, c), []
    return rowcall(name, fn, tok, bat, con, tok_out, [], ts)


def rowcall_bwd(name, f, tok, bat, con, cts, tok_grads, add=None, ts=ROW_TILE, join_first=1, into=None):
    nt, ncts = len(tok), len(cts)

    def fn(t, b, c):
        prim = [v.astype(f32) for v in t[:nt]]
        ct = [v.astype(f32) for v in t[nt:nt + ncts]]
        _, vjp = jax.vjp(lambda tt, bb, cc: f(tt, bb, cc), prim, b, c)
        dt, db, dc = vjp(ct)
        touts = [dt[i] for i, _ in tok_grads]
        if add is not None:
            touts[0] = touts[0] + t[nt + ncts].astype(f32)
        if join_first > 1:
            touts = [jnp.concatenate(touts[:join_first], axis=1)] + touts[join_first:]
        return touts, list(db) + list(dc)

    all_tok = list(tok) + list(cts) + ([add] if add is not None else [])
    tok_out = [(tok[i][1], dt) for i, dt in tok_grads]
    if join_first > 1:
        tok_out = [(sum(w for w, _ in tok_out[:join_first]), tok_out[0][1])] + tok_out[join_first:]
    acc_out = [tuple(a.shape[1:]) for a in bat] + [tuple(a.shape) for a in con]
    return rowcall(name, fn, all_tok, bat, con, tok_out, acc_out, ts, into)


def _rms(y, w):
    return y * lax.rsqrt(jnp.mean(y * y, axis=-1, keepdims=True) + RMS_EPS) * w


@jax.custom_vjp
def _rms_vjp(y, w):
    return _rms(y, w)


def _rms_vjp_fwd(y, w):
    r = lax.rsqrt(jnp.mean(y * y, axis=-1, keepdims=True) + RMS_EPS)
    yhat = y * r
    return yhat * w, (yhat, r, w)


def _rms_vjp_bwd(res, g):
    yhat, r, w = res
    gw = g * w
    return r * (gw - yhat * jnp.mean(gw * yhat, axis=-1, keepdims=True)), jnp.sum(g * yhat, axis=0, keepdims=True)


_rms_vjp.defvjp(_rms_vjp_fwd, _rms_vjp_bwd)


def f_rms_mod(t, b, c, rms=_rms):
    return [rms(t[0], c[0]) * (1.0 + b[0]) + b[1]]


def f_post_pre(t, b, c, rms=_rms):
    h1 = t[0] + b[0] * rms(t[1], c[0])
    return [h1, rms(h1, c[1]) * (1.0 + b[1]) + b[2]]


def f_merge(t, b, c):
    ga, gd, ya, yd = t
    return [jax.nn.sigmoid(ga) * ya + jax.nn.sigmoid(gd) * yd]


def f_dnout(t, b, c, rms=_rms):
    o, z = t
    outs = []
    for h in range(DNH):
        sl = slice(h * DND, (h + 1) * DND)
        zh = z[:, sl]
        outs.append(rms(o[:, sl], c[0]) * (zh * jax.nn.sigmoid(zh)))
    return [jnp.concatenate(outs, axis=1)]


def _softplus(x):
    return jnp.maximum(x, 0.0) + jnp.log(1.0 + jnp.exp(-jnp.abs(x)))


def f_gate(t, b, c):
    ba = t[0]
    a_log, dt_bias = c
    lane = lax.broadcasted_iota(jnp.int32, ba.shape, 1)
    beta = jax.nn.sigmoid(ba)
    g = -jnp.exp(a_log) * _softplus(ba + dt_bias)
    return [jnp.where(lane < DNH, beta, jnp.where(lane < 2 * DNH, g, 0.0))]


def _bucket_table():
    qi = np.arange(WIN)[:, None]
    kj = np.arange(2 * WIN)[None, :]
    dist = np.maximum(WIN + qi - kj, 0)
    max_exact = NBUCK // 2
    scaled = np.log(np.maximum(dist, 1).astype(np.float64) / max_exact) / math.log(MAXDIST / max_exact)
    large = np.minimum(max_exact + (scaled * (NBUCK - max_exact)).astype(np.int32), NBUCK - 1)
    return np.where(dist < max_exact, dist, large).astype(np.int32)


def _attn_mask(n):
    qi = lax.broadcasted_iota(jnp.int32, (WIN, 2 * WIN), 0)
    kj = lax.broadcasted_iota(jnp.int32, (WIN, 2 * WIN), 1)
    dist = WIN + qi - kj
    return (dist >= 0) & (dist < WIN) & ((kj >= WIN) | (n > 0))


def _swap_halves(x):
    return pltpu.roll(x, HD, axis=x.ndim - 1)


@jax.custom_vjp
def _swap_halves_vjp(x):
    return _swap_halves(x)


_swap_halves_vjp.defvjp(lambda x: (_swap_halves(x), None), lambda _, g: (_swap_halves(g),))


def _sink_softmax(s, sinks):
    m = jnp.maximum(jnp.max(s, axis=-1, keepdims=True), sinks)
    p = jnp.exp(s - m)
    return p / (jnp.sum(p, axis=-1, keepdims=True) + jnp.exp(sinks - m))


@jax.custom_vjp
def _sink_softmax_vjp(s, sinks):
    return _sink_softmax(s, sinks)


def _sink_softmax_fwd(s, sinks):
    m = jnp.maximum(jnp.max(s, axis=-1, keepdims=True), sinks)
    p = jnp.exp(s - m)
    sink = jnp.exp(sinks - m)
    inv = 1.0 / (jnp.sum(p, axis=-1, keepdims=True) + sink)
    return p * inv, (p * inv, sink * inv)


def _sink_softmax_bwd(res, g):
    probs, sink_prob = res
    d = jnp.sum(g * probs, axis=-1, keepdims=True)
    return probs * (g - d), -jnp.sum(sink_prob * d, axis=(0, 2)).reshape(HQ, 1, 1)


_sink_softmax_vjp.defvjp(_sink_softmax_fwd, _sink_softmax_bwd)


def _attn_block(q, kp, kc, vp, vc, bias, sinks, mask, differentiated):
    dot = _bdot_bf16_vjp if differentiated else _bdot_bf16
    swap = _swap_halves_vjp if differentiated else _swap_halves
    B, grp = q.shape[0], HQ // HKV
    upper = lax.broadcasted_iota(jnp.int32, (2 * WIN, LANE), 1) >= HD

    def placed(natural, swapped, j, half):
        keep = upper if half == 1 else ~upper
        return jnp.where(keep, natural if j == half else swapped, 0.0)

    qh, ks, vs = [], [], []
    for b in range(B):
        kb, vb = jnp.concatenate([kp[b], kc[b]], axis=0), jnp.concatenate([vp[b], vc[b]], axis=0)
        kb_sw, vb_sw = swap(kb), swap(vb)
        for h in range(HQ):
            qh.append(q[b, :, (h // 2) * LANE:(h // 2 + 1) * LANE])
            ks.append(placed(kb, kb_sw, h // grp, h % 2))
            vs.append(placed(vb, vb_sw, h // grp, h % 2))
    s = dot(_stack(qh), _stack(ks), 2, 2).reshape(B, HQ, WIN, 2 * WIN) * (HD ** -0.5)
    probs = (_sink_softmax_vjp if differentiated else _sink_softmax)(jnp.where(mask, s + bias, NEG_INF), sinks)
    o = dot(probs.reshape(B * HQ, WIN, 2 * WIN), _stack(vs), 2, 1)
    return _stack([jnp.concatenate([o[b * HQ + 2 * i] + o[b * HQ + 2 * i + 1] for i in range(HQ // 2)], axis=1) for b in range(B)])


def _attn_specs(B, NB):
    last = NB - 1
    return [
        pl.BlockSpec((B, WIN, HQ * HD), lambda n: (0, jnp.minimum(n, last), CB_AQ // 4)),
        pl.BlockSpec((B, WIN, LANE), lambda n: (0, jnp.clip(n - 1, 0, last), CB_AK)),
        pl.BlockSpec((B, WIN, LANE), lambda n: (0, jnp.minimum(n, last), CB_AK)),
        pl.BlockSpec((B, WIN, LANE), lambda n: (0, jnp.clip(n - 1, 0, last), CB_AV)),
        pl.BlockSpec((B, WIN, LANE), lambda n: (0, jnp.minimum(n, last), CB_AV)),
        pl.BlockSpec((HQ, WIN, 2 * WIN), lambda n: (0, 0, 0)),
        pl.BlockSpec((HQ, 1, 1), lambda n: (0, 0, 0)),
    ]


def attn_fwd(proj, bias, sinks):
    B, S, _ = proj.shape
    NB = S // WIN

    def body(q, kp, kc, vp, vc, bias_ref, sink_ref, o_ref):
        mask = _attn_mask(pl.program_id(0))
        o = _attn_block(*[r[...].astype(f32) for r in (q, kp, kc, vp, vc)], bias_ref[...], sink_ref[...], mask, False)
        o_ref[...] = o.astype(o_ref.dtype)

    return pl.pallas_call(
        body, name="attn_fwd", grid=(NB,), in_specs=_attn_specs(B, NB),
        out_specs=pl.BlockSpec((B, WIN, HQ * HD), lambda n: (0, n, 0)), out_shape=jax.ShapeDtypeStruct((B, S, HQ * HD), bf16),
        compiler_params=_cp(("parallel",)),
    )(proj, proj, proj, proj, proj, bias, sinks)


def attn_bwd(proj, bias, sinks, dy, dproj):
    B, S, _ = proj.shape
    NB = S // WIN
    last = NB - 1

    def body(q, kp, kc, vp, vc, bias_ref, sink_ref, dy_ref, _, dq_ref, dk_ref, dv_ref, dbias_ref, dsink_ref, kcar, vcar):
        n = pl.program_id(0)

        @pl.when(n == 0)
        def _():
            dbias_ref[...] = jnp.zeros(dbias_ref.shape, f32)
            dsink_ref[...] = jnp.zeros(dsink_ref.shape, f32)
            kcar[...] = jnp.zeros(kcar.shape, f32)
            vcar[...] = jnp.zeros(vcar.shape, f32)

        @pl.when(n < NB)
        def _():
            mask = _attn_mask(n)
            _, vjp = jax.vjp(lambda *a: _attn_block(*a, mask, True), *[r[...].astype(f32) for r in (q, kp, kc, vp, vc)],
                             bias_ref[...], sink_ref[...])
            dq, dkp, dkc, dvp, dvc, dbias, dsink = vjp(dy_ref[...].astype(f32))
            dq_ref[...] = dq.astype(dq_ref.dtype)
            dbias_ref[...] += dbias
            dsink_ref[...] += dsink
            dk_ref[...] = (kcar[...] + dkp).astype(dk_ref.dtype)
            dv_ref[...] = (vcar[...] + dvp).astype(dv_ref.dtype)
            kcar[...] = dkc
            vcar[...] = dvc

        @pl.when(n == NB)
        def _():
            dk_ref[...] = kcar[...].astype(dk_ref.dtype)
            dv_ref[...] = vcar[...].astype(dv_ref.dtype)

    in_specs = _attn_specs(B, NB) + [pl.BlockSpec((B, WIN, HQ * HD), lambda n: (0, jnp.minimum(n, last), 0)),
                                     pl.BlockSpec(memory_space=pl.ANY)]
    kv_out = pl.BlockSpec((B, WIN, LANE), lambda n: (0, jnp.maximum(n - 1, 0), 0))
    return pl.pallas_call(
        body, name="attn_bwd", grid=(NB + 1,), in_specs=in_specs, input_output_aliases={8: 0},
        out_specs=[pl.BlockSpec((B, WIN, HQ * HD), lambda n: (0, jnp.minimum(n, last), CB_AQ // 4)), kv_out, kv_out,
                   pl.BlockSpec((HQ, WIN, 2 * WIN), lambda n: (0, 0, 0)), pl.BlockSpec((HQ, 1, 1), lambda n: (0, 0, 0))],
        out_shape=[jax.ShapeDtypeStruct(dproj.shape, dproj.dtype), jax.ShapeDtypeStruct((B, S, LANE), bf16),
                   jax.ShapeDtypeStruct((B, S, LANE), bf16), jax.ShapeDtypeStruct((HQ, WIN, 2 * WIN), f32),
                   jax.ShapeDtypeStruct((HQ, 1, 1), f32)],
        scratch_shapes=[pltpu.VMEM((B, WIN, LANE), f32), pltpu.VMEM((B, WIN, LANE), f32)],
        compiler_params=_cp(("arbitrary",)),
    )(proj, proj, proj, proj, proj, bias, sinks, dy, dproj)


DN_ROWS, FFN_ROWS = 256, 32
RING = 3


def _stage_rows(dst, value):
    dst[0:8] = jnp.zeros((8, LANE), f32)
    dst[8:8 + value.shape[0]] = value


def _conv_rows(xs, w, width, r, rows):
    wins = [xs[pl.ds(r + 8 - (width - 1) + j, rows), :] for j in range(width)]
    out = w[0:1] * wins[0]
    for j in range(1, width):
        out = out + w[j:j + 1] * wins[j]
    return out, wins


def _fold8(v):
    return jnp.sum(v.reshape(v.shape[0] // 8, 8, LANE), axis=0)


def _conv_rows_t(ds, w, width, r, rows):
    out = w[0:1] * ds[pl.ds(r + width - 1, rows), :]
    for j in range(1, width):
        out = out + w[j:j + 1] * ds[pl.ds(r + width - 1 - j, rows), :]
    return out


def _dn_outblk(i):
    return (i % DNH) * 3 + i // DNH


def _dn_act(c, isqk):
    sg = jax.nn.sigmoid(c)
    y = c * sg
    n = lax.rsqrt(jnp.sum(y * y, axis=-1, keepdims=True) + L2_EPS)
    return jnp.where(isqk, y * n, y), sg, n


def dnconv_fwd(proj, conv_w):
    B, S, _ = proj.shape
    rows = min(DN_ROWS, S)

    def body(x_ref, w_ref, o_ref, xs):
        isqk = pl.program_id(0) < 2 * DNH
        _stage_rows(xs, x_ref[0].astype(f32))
        w = w_ref[...]
        for r in range(0, S, rows):
            c, _ = _conv_rows(xs, w, DNK, r, rows)
            o_ref[0, pl.ds(r, rows), :] = _dn_act(c, isqk)[0]

    return pl.pallas_call(
        body, name="dnconv_fwd", grid=(3 * DNH, B),
        in_specs=[pl.BlockSpec((1, S, LANE), lambda i, b: (b, 0, CB_DQKV + i)), pl.BlockSpec((DNK, LANE), lambda i, b: (0, i))],
        out_specs=pl.BlockSpec((1, S, LANE), lambda i, b: (b, 0, _dn_outblk(i))),
        out_shape=jax.ShapeDtypeStruct((B, S, 3 * DNH * DND), f32), scratch_shapes=[pltpu.VMEM((S + 8, LANE), f32)],
        compiler_params=_cp(("parallel", "parallel")),
    )(proj, conv_w)


def dnconv_bwd(proj, conv_w, dqkvn, dproj):
    B, S, _ = proj.shape
    rows = min(DN_ROWS, S)

    def body(x_ref, w_ref, dy_ref, _, dx_ref, dw_ref, xs, ds):
        isqk = pl.program_id(0) < 2 * DNH
        _stage_rows(xs, x_ref[0].astype(f32))
        w = w_ref[...]
        dw = [jnp.zeros((8, LANE), f32) for _ in range(DNK)]
        for r in range(0, S, rows):
            c, wins = _conv_rows(xs, w, DNK, r, rows)
            out, sg, n = _dn_act(c, isqk)
            dout = dy_ref[0, pl.ds(r, rows), :]
            dy = jnp.where(isqk, n * (dout - out * jnp.sum(dout * out, axis=-1, keepdims=True)), dout)
            dc = dy * (sg * (1.0 + c * (1.0 - sg)))
            ds[pl.ds(r, rows), :] = dc
            for j in range(DNK):
                dw[j] = dw[j] + _fold8(dc * wins[j])
        ds[S:S + 8] = jnp.zeros((8, LANE), f32)
        for r in range(0, S, rows):
            dx_ref[0, pl.ds(r, rows), :] = _conv_rows_t(ds, w, DNK, r, rows).astype(dx_ref.dtype)

        @pl.when(pl.program_id(1) == 0)
        def _():
            dw_ref[...] = jnp.zeros(dw_ref.shape, f32)
        dw_ref[...] += jnp.concatenate([jnp.sum(d, axis=0, keepdims=True) for d in dw], axis=0)

    return pl.pallas_call(
        body, name="dnconv_bwd", grid=(3 * DNH, B),
        in_specs=[pl.BlockSpec((1, S, LANE), lambda i, b: (b, 0, CB_DQKV + i)), pl.BlockSpec((DNK, LANE), lambda i, b: (0, i)),
                  pl.BlockSpec((1, S, LANE), lambda i, b: (b, 0, _dn_outblk(i))), pl.BlockSpec(memory_space=pl.ANY)],
        out_specs=[pl.BlockSpec((1, S, LANE), lambda i, b: (b, 0, CB_DQKV + i)), pl.BlockSpec((DNK, LANE), lambda i, b: (0, i))],
        out_shape=[jax.ShapeDtypeStruct(dproj.shape, dproj.dtype), jax.ShapeDtypeStruct((DNK, 3 * DNH * DND), f32)],
        scratch_shapes=[pltpu.VMEM((S + 8, LANE), f32), pltpu.VMEM((S + 8, LANE), f32)],
        input_output_aliases={3: 0}, compiler_params=_cp(("parallel", "arbitrary")),
    )(proj, conv_w, dqkvn, dproj)


def _bdot(a, b, ca, cb, precision=HI):
    return lax.dot_general(a, b, (((ca,), (cb,)), ((0,), (0,))), preferred_element_type=f32, precision=precision)


def _bdot_bf16(a, b, ca, cb):
    return _bdot(a.astype(bf16), b.astype(bf16), ca, cb, None)


@functools.partial(jax.custom_vjp, nondiff_argnums=(2, 3))
def _bdot_bf16_vjp(a, b, ca, cb):
    return _bdot_bf16(a, b, ca, cb)


def _bdot_bf16_fwd(a, b, ca, cb):
    return _bdot_bf16(a, b, ca, cb), (a, b)


def _bdot_bf16_bwd(ca, cb, res, g):
    a, b = res
    fa, fb = 3 - ca, 3 - cb
    da = _bdot_bf16(g, b, 2, fb) if ca == 2 else _bdot_bf16(b, g, fb, 2)
    db = _bdot_bf16(a, g, fa, 1) if cb == 1 else _bdot_bf16(g, a, 1, fa)
    return da, db


_bdot_bf16_vjp.defvjp(_bdot_bf16_fwd, _bdot_bf16_bwd)


def _neumann_inverse(low):
    n = low.shape[-1]
    eye = (lax.broadcasted_iota(jnp.int32, (n, n), 0) == lax.broadcasted_iota(jnp.int32, (n, n), 1)).astype(f32)
    p = -low
    x = eye[None] + p
    for _ in range(5):
        p = _bdot_bf16(p, p, 2, 1)
        x = x + _bdot_bf16(x, p, 2, 1)
    return x


@jax.custom_vjp
def _unit_lower_inverse(low):
    return _neumann_inverse(low)


def _uli_fwd(low):
    t = _neumann_inverse(low)
    return t, t


def _uli_bwd(t, dt):
    return (-_bdot_bf16(_bdot_bf16(t, dt, 1, 1), t, 2, 2),)


_unit_lower_inverse.defvjp(_uli_fwd, _uli_bwd)


def _stack(xs):
    return jnp.concatenate([x[None] for x in xs], axis=0)


DELTA_CHUNKS = 4


def _delta_chunks(qkv, bg, state, differentiated):
    inverse = _unit_lower_inverse if differentiated else _neumann_inverse
    lo = _bdot_bf16_vjp if differentiated else _bdot_bf16
    B, n = qkv.shape[0], qkv.shape[1] // CH
    G = B * DNH
    N = n * G
    triples = [(i, b, h) for i in range(n) for b in range(B) for h in range(DNH)]
    col = lambda i, b, h, kind: qkv[b, i * CH:(i + 1) * CH, (3 * h + kind) * DND:(3 * h + kind + 1) * DND]
    q, k, v = [_stack([col(i, b, h, kind) for i, b, h in triples]) for kind in range(3)]
    lane = lax.broadcasted_iota(jnp.int32, (CH, LANE), 1)
    pick = lambda i, b, l: jnp.sum(jnp.where(lane == l, bg[b, i * CH:(i + 1) * CH], 0.0), axis=1, keepdims=True)
    beta = _stack([pick(i, b, h) for i, b, h in triples])
    g = _stack([pick(i, b, h + DNH) for i, b, h in triples])
    ri = lax.broadcasted_iota(jnp.int32, (CH, CH), 0)
    ci = lax.broadcasted_iota(jnp.int32, (CH, CH), 1)
    incl, strict = (ri >= ci)[None], (ri > ci)[None]
    gc = _bdot(jnp.broadcast_to(incl.astype(f32), (N, CH, CH)), jnp.broadcast_to(g, (N, CH, LANE)), 2, 1, MID)
    e0 = jnp.broadcast_to((lane == 0).astype(f32)[None], (N, CH, LANE))
    gc_row = _bdot(e0, gc, 2, 2, MID)
    diff = gc[:, :, :CH] - gc_row
    decay = jnp.where(incl, jnp.exp(jnp.where(incl, diff, 0.0)), 0.0)
    qs = q * (DND ** -0.5)
    kb, vb = k * beta, v * beta
    eg = jnp.exp(gc)
    with_k = lo(jnp.concatenate([kb, qs], axis=1), k, 2, 2)
    low = jnp.where(strict, with_k[:, :CH] * decay, 0.0)
    intra = jnp.where(incl, with_k[:, CH:] * decay, 0.0)
    tinv = inverse(low)
    solved = lo(tinv, jnp.concatenate([vb, kb * eg], axis=2), 2, 1)
    gl = gc[:, CH - 1:CH, :]
    k_tail = k * jnp.exp(gl - gc)
    to_state = jnp.concatenate([solved[:, :, DND:], qs * eg], axis=1)
    decay_all = jnp.exp(gl)
    outs = []
    for i in range(n):
        sl = slice(i * G, (i + 1) * G)
        with_state = lo(to_state[sl], state, 2, 1)
        v_new = solved[sl, :, :DND] - with_state[:, :CH]
        outs.append(with_state[:, CH:] + lo(intra[sl], v_new, 2, 1))
        state = state * decay_all[sl] + lo(k_tail[sl], v_new, 1, 1)
    return outs, state


def delta_fwd(qkvn, bg):
    B, S, _ = qkvn.shape
    n = DELTA_CHUNKS if (S // CH) % DELTA_CHUNKS == 0 else 1
    steps, G, rows = S // (n * CH), B * DNH, n * CH

    def body(qkv_ref, bg_ref, o_ref, st_ref, state):
        @pl.when(pl.program_id(0) == 0)
        def _():
            state[...] = jnp.zeros(state.shape, f32)
        s0 = state[...]
        st_ref[0] = s0
        outs, s1 = _delta_chunks(qkv_ref[...], bg_ref[...], s0, False)
        for i, o in enumerate(outs):
            for b in range(B):
                for h in range(DNH):
                    o_ref[b, i * CH:(i + 1) * CH, h * DND:(h + 1) * DND] = o[b * DNH + h]
        state[...] = s1

    return pl.pallas_call(
        body, name="delta_fwd", grid=(steps,),
        in_specs=[pl.BlockSpec((B, rows, 3 * DNH * DND), lambda c: (0, c, 0)), pl.BlockSpec((B, rows, LANE), lambda c: (0, c, 0))],
        out_specs=[pl.BlockSpec((B, rows, DNH * DND), lambda c: (0, c, 0)), pl.BlockSpec((1, G, DND, DND), lambda c: (c, 0, 0, 0))],
        out_shape=[jax.ShapeDtypeStruct((B, S, DNH * DND), f32), jax.ShapeDtypeStruct((steps, G, DND, DND), f32)],
        scratch_shapes=[pltpu.VMEM((G, DND, DND), f32)], compiler_params=_cp(("arbitrary",)),
    )(qkvn, bg)


def delta_bwd(qkvn, bg, states, do):
    B, S, _ = qkvn.shape
    steps, G = states.shape[0], B * DNH
    rows = S // steps
    n = rows // CH

    def body(qkv_ref, bg_ref, st_ref, do_ref, dqkv_ref, dbg_ref, dstate):
        @pl.when(pl.program_id(0) == 0)
        def _():
            dstate[...] = jnp.zeros(dstate.shape, f32)
        _, vjp = jax.vjp(lambda a, g, s: _delta_chunks(a, g, s, True), qkv_ref[...], bg_ref[...], st_ref[0])
        do = [_stack([do_ref[b, i * CH:(i + 1) * CH, h * DND:(h + 1) * DND] for b in range(B) for h in range(DNH)]) for i in range(n)]
        dqkv, dbg, ds = vjp((do, dstate[...]))
        dqkv_ref[...] = dqkv
        dbg_ref[...] = dbg
        dstate[...] = ds

    rev = lambda c: steps - 1 - c
    return pl.pallas_call(
        body, name="delta_bwd", grid=(steps,),
        in_specs=[pl.BlockSpec((B, rows, 3 * DNH * DND), lambda c: (0, rev(c), 0)), pl.BlockSpec((B, rows, LANE), lambda c: (0, rev(c), 0)),
                  pl.BlockSpec((1, G, DND, DND), lambda c: (rev(c), 0, 0, 0)),
                  pl.BlockSpec((B, rows, DNH * DND), lambda c: (0, rev(c), 0))],
        out_specs=[pl.BlockSpec((B, rows, 3 * DNH * DND), lambda c: (0, rev(c), 0)), pl.BlockSpec((B, rows, LANE), lambda c: (0, rev(c), 0))],
        out_shape=[jax.ShapeDtypeStruct((B, S, 3 * DNH * DND), f32), jax.ShapeDtypeStruct((B, S, LANE), f32)],
        scratch_shapes=[pltpu.VMEM((G, DND, DND), f32)], compiler_params=_cp(("arbitrary",)),
    )(qkvn, bg, states, do)


GELU_C0, GELU_C1 = math.sqrt(2.0 / math.pi), 0.044715


def _ffn_specs(S):
    nblk = DFF // LANE
    return [pl.BlockSpec((1, S, LANE), lambda i, b: (b, 0, i)), pl.BlockSpec((1, S, LANE), lambda i, b: (b, 0, nblk + i)),
            pl.BlockSpec((FK, LANE), lambda i, b: (0, i)), pl.BlockSpec((FK, LANE), lambda i, b: (0, nblk + i))]


def ffnconv_fwd(up, conv_w):
    B, S, _ = up.shape
    rows = min(FFN_ROWS, S)

    def body(g_ref, v_ref, gw_ref, vw_ref, o_ref, xg, xv):
        _stage_rows(xg, g_ref[0].astype(f32))
        _stage_rows(xv, v_ref[0].astype(f32))
        gw, vw = gw_ref[...], vw_ref[...]
        for r in range(0, S, rows):
            g, _ = _conv_rows(xg, gw, FK, r, rows)
            v, _ = _conv_rows(xv, vw, FK, r, rows)
            t = jnp.tanh(GELU_C0 * (g * (1.0 + GELU_C1 * (g * g))))
            o_ref[0, pl.ds(r, rows), :] = (0.5 * g * (1.0 + t) * v).astype(o_ref.dtype)

    return pl.pallas_call(
        body, name="ffnconv_fwd", grid=(DFF // LANE, B), in_specs=_ffn_specs(S),
        out_specs=pl.BlockSpec((1, S, LANE), lambda i, b: (b, 0, i)), out_shape=jax.ShapeDtypeStruct((B, S, DFF), bf16),
        scratch_shapes=[pltpu.VMEM((S + 8, LANE), f32)] * 2, compiler_params=_cp(("parallel", "parallel")),
    )(up, up, conv_w, conv_w)


def ffnconv_bwd(up, conv_w, dact):
    B, S, _ = up.shape
    rows = min(FFN_ROWS, S)

    def body(g_ref, v_ref, gw_ref, vw_ref, dy_ref, dx_ref, dw_ref, xg, xv, dg, dv):
        _stage_rows(xg, g_ref[0].astype(f32))
        _stage_rows(xv, v_ref[0].astype(f32))
        gw, vw = gw_ref[...], vw_ref[...]
        dgw = [jnp.zeros((8, LANE), f32) for _ in range(FK)]
        dvw = [jnp.zeros((8, LANE), f32) for _ in range(FK)]
        for r in range(0, S, rows):
            g, gwins = _conv_rows(xg, gw, FK, r, rows)
            v, vwins = _conv_rows(xv, vw, FK, r, rows)
            g2 = g * g
            t = jnp.tanh(GELU_C0 * (g * (1.0 + GELU_C1 * g2)))
            half = 0.5 * (1.0 + t)
            dgelu = half + (0.5 * GELU_C0) * g * (1.0 - t * t) * (1.0 + (3.0 * GELU_C1) * g2)
            dy = dy_ref[0, pl.ds(r, rows), :].astype(f32)
            dvc = dy * (g * half)
            dgc = dy * v * dgelu
            dg[pl.ds(r, rows), :] = dgc
            dv[pl.ds(r, rows), :] = dvc
            for j in range(FK):
                dgw[j] = dgw[j] + _fold8(dgc * gwins[j])
                dvw[j] = dvw[j] + _fold8(dvc * vwins[j])
        dg[S:S + 8] = jnp.zeros((8, LANE), f32)
        dv[S:S + 8] = jnp.zeros((8, LANE), f32)
        for r in range(0, S, rows):
            dx_ref[0, 0, pl.ds(r, rows), :] = _conv_rows_t(dg, gw, FK, r, rows).astype(dx_ref.dtype)
            dx_ref[1, 0, pl.ds(r, rows), :] = _conv_rows_t(dv, vw, FK, r, rows).astype(dx_ref.dtype)

        @pl.when(pl.program_id(1) == 0)
        def _():
            dw_ref[...] = jnp.zeros(dw_ref.shape, f32)
        dw_ref[0] += jnp.concatenate([jnp.sum(d, axis=0, keepdims=True) for d in dgw], axis=0)
        dw_ref[1] += jnp.concatenate([jnp.sum(d, axis=0, keepdims=True) for d in dvw], axis=0)

    return pl.pallas_call(
        body, name="ffnconv_bwd", grid=(DFF // LANE, B),
        in_specs=_ffn_specs(S) + [pl.BlockSpec((1, S, LANE), lambda i, b: (b, 0, i))],
        out_specs=[pl.BlockSpec((2, 1, S, LANE), lambda i, b: (0, b, 0, i)), pl.BlockSpec((2, FK, LANE), lambda i, b: (0, 0, i))],
        out_shape=[jax.ShapeDtypeStruct((2, B, S, DFF), bf16), jax.ShapeDtypeStruct((2, FK, DFF), f32)],
        scratch_shapes=[pltpu.VMEM((S + 8, LANE), f32)] * 4, compiler_params=_cp(("parallel", "arbitrary")),
    )(up, up, conv_w, conv_w, dact)


def ada_fwd(c_all, ada_w, ada_b):
    def body(c_ref, w_ref, b_ref, o_ref):
        c = c_ref[...]
        act = (c * jax.nn.sigmoid(c)).astype(bf16)
        o_ref[...] = jnp.dot(act, w_ref[...].astype(bf16), preferred_element_type=f32) + b_ref[...]

    return pl.pallas_call(body, name="ada_fwd", out_shape=jax.ShapeDtypeStruct((c_all.shape[0], ada_w.shape[1]), f32),
                          compiler_params=pltpu.CompilerParams(vmem_limit_bytes=VMEM_LIMIT))(c_all, ada_w, ada_b)


def ada_bwd(c_all, dmod):
    def body(c_ref, d_ref, o_ref):
        c = c_ref[...]
        act = (c * jax.nn.sigmoid(c)).astype(bf16)
        o_ref[...] = lax.dot_general(act, d_ref[...].astype(bf16), (((0,), (0,)), ((), ())), preferred_element_type=f32)

    return pl.pallas_call(body, name="ada_bwd", out_shape=jax.ShapeDtypeStruct((c_all.shape[1], dmod.shape[1]), f32),
                          compiler_params=pltpu.CompilerParams(vmem_limit_bytes=VMEM_LIMIT))(c_all, dmod)


def loss_head(h1, y2, target, g2, w):
    def fn(t, b, c):
        h, y, tg = [v.astype(f32) for v in t]

        def loss_fn(h, y, g, w):
            e = h + g * _rms_vjp(y, w) - tg
            return 0.5 * jnp.sum(jnp.mean(e * e, axis=-1))

        loss, grads = jax.value_and_grad(loss_fn, argnums=(0, 1, 2, 3))(h, y, b[0], c[0])
        return [grads[0], grads[1]], [grads[2], grads[3], jnp.full((1, LANE), loss, f32)]

    return rowcall("loss_head", fn, [(h1, D, 0), (y2, D, 0), (target, D, 0)], [g2], [w], [(D, f32), (D, bf16)],
                   [(1, D), (1, D), (1, LANE)])


def adamw(w, gparts, m, v, name):
    R, C = w.shape
    P = gparts.shape[0]
    budget = 2 * 1024 * 1024
    tr, tc = R, C
    if R * C * 4 > budget and R % 8 == 0:
        tr = max(t for t in range(8, R + 1, 8) if R % t == 0 and t * C * 4 <= budget)
    elif R * C * 4 > budget:
        tc = max(t for t in range(LANE, C + 1, LANE) if C % t == 0 and R * t * 4 <= budget)

    nj = C // tc
    steps = (R // tr) * nj

    def body(w_ref, g_hbm, m_ref, v_ref, go, do, mo, vo, gbuf, sem):
        step = pl.program_id(0) * nj + pl.program_id(1)

        def parts_copy(k):
            row = 0 if tr == R else pl.multiple_of((k // nj) * tr, 8)
            col = 0 if tc == C else pl.multiple_of((k % nj) * tc, LANE)
            src = g_hbm.at[:, pl.ds(row, tr), pl.ds(col, tc)]
            return pltpu.make_async_copy(src, gbuf.at[k % RING], sem.at[k % RING])

        @pl.when(step == 0)
        def _():
            for k in range(min(RING - 1, steps)):
                parts_copy(k).start()

        @pl.when(step + (RING - 1) < steps)
        def _():
            parts_copy(step + (RING - 1)).start()

        parts_copy(step).wait()
        g_ref = gbuf.at[step % RING]
        g = g_ref[0].astype(f32)
        for p in range(1, P):
            g = g + g_ref[p].astype(f32)
        m2 = B1 * m_ref[...] + (1.0 - B1) * g
        v2 = B2 * v_ref[...] + (1.0 - B2) * jnp.square(g)
        m_hat = m2 * (1.0 / (1.0 - B1 ** STEP))
        v_hat = v2 * (1.0 / (1.0 - B2 ** STEP))
        go[...] = g
        do[...] = -LR * (m_hat / (jnp.sqrt(v_hat) + EPS) + WD * w_ref[...])
        mo[...] = m2
        vo[...] = v2

    blk = pl.BlockSpec((tr, tc), lambda i, j: (i, j))
    return pl.pallas_call(
        body, name=name, grid=(R // tr, nj), in_specs=[blk, pl.BlockSpec(memory_space=pl.ANY), blk, blk],
        out_specs=[blk] * 4, out_shape=[jax.ShapeDtypeStruct((R, C), f32)] * 4, compiler_params=_cp(("arbitrary", "arbitrary")),
        scratch_shapes=[pltpu.VMEM((RING, P, tr, tc), gparts.dtype), pltpu.SemaphoreType.DMA((RING,))],
    )(w, gparts, m, v)


def _pack_w_in(wt):
    aq, ak, av, dqkv, dz, dbeta, da, ga, gd = jnp.split(wt, np.cumsum(IN_SPLITS)[:-1].tolist(), axis=0)
    ba = jnp.pad(jnp.concatenate([dbeta, da], axis=0), ((0, LANE - 2 * DNH), (0, 0)))
    return jnp.concatenate([ga, gd, aq, dqkv, dz, ak, av, ba], axis=0)


def _unpack_w_in(p):
    row = lambda cb, n: p[cb * LANE: cb * LANE + n]
    ba = row(CB_BA, 2 * DNH)
    return jnp.concatenate([row(CB_AQ, HQ * HD), row(CB_AK, HKV * HD), row(CB_AV, HKV * HD), row(CB_DQKV, 3 * DNH * DND),
                            row(CB_DZ, DNH * DND), ba[:DNH], ba[DNH:], row(CB_GA, D), row(CB_GD, D)], axis=0)


def _cols_gathered(g):
    return g.transpose(1, 0, 2).reshape(g.shape[1], NDEV * g.shape[2])


def _cols_split(w):
    r = w.shape[0]
    return w.reshape(r, NDEV, w.shape[1] // NDEV).transpose(1, 0, 2)


def kernel(x, c, ada_w, ada_b, norm_mix_pre, norm_mix_post, norm_ffn_pre, norm_ffn_post, w_in, dn_conv_w, dn_a_log, dn_dt_bias, dn_norm_w, attn_sinks, rel_bias, w_attn_branch, w_dn_branch, w_out, ffn_w_up, ffn_conv_w, ffn_w_down, loss_target, m_ada_w, m_ada_b, m_norm_mix_pre, m_norm_mix_post, m_norm_ffn_pre, m_norm_ffn_post, m_w_in, m_dn_conv_w, m_dn_a_log, m_dn_dt_bias, m_dn_norm_w, m_attn_sinks, m_rel_bias, m_w_attn_branch, m_w_dn_branch, m_w_out, m_ffn_w_up, m_ffn_conv_w, m_ffn_w_down, v_ada_w, v_ada_b, v_norm_mix_pre, v_norm_mix_post, v_norm_ffn_pre, v_norm_ffn_post, v_w_in, v_dn_conv_w, v_dn_a_log, v_dn_dt_bias, v_dn_norm_w, v_attn_sinks, v_rel_bias, v_w_attn_branch, v_w_dn_branch, v_w_out, v_ffn_w_up, v_ffn_conv_w, v_ffn_w_down):
    B, S, _ = x.shape
    T = B * S
    me = 4 * lax.axis_index("x") + 2 * lax.axis_index("y") + lax.axis_index("c")
    big = dict(w_in=w_in, dn_conv_w=dn_conv_w, w_attn_branch=w_attn_branch, w_dn_branch=w_dn_branch, w_out=w_out,
               ffn_w_up=ffn_w_up, ffn_conv_w=ffn_conv_w, ffn_w_down=ffn_w_down)
    big_names = list(big)

    first, mid, late = ["w_in", "dn_conv_w"], ["w_attn_branch", "w_dn_branch", "w_out"], ["ffn_w_up", "ffn_conv_w", "ffn_w_down"]
    transposed = ("w_in", "ffn_w_up")
    local = lambda n, a: a[0].T if n in transposed else a[0]
    shard = lambda names: [local(n, big[n]).astype(bf16) for n in names]
    *got, c_all = _exchange(shard(first) + [c], "gather_w_in", two_level=True)
    gw = dict(zip(first, got))
    c_all = c_all.reshape(NDEV * B, D)

    wp = _pack_w_in(gw["w_in"].reshape(IN_DIM, D))
    conv_dn = _cols_gathered(gw["dn_conv_w"]).astype(f32)

    ncol = ada_w.shape[2]
    ada_b_mine = lax.dynamic_slice_in_dim(ada_b, me * ncol, ncol, axis=1)
    mod_cols = ada_fwd(c_all, ada_w[0], ada_b_mine)
    (mod_g,) = _exchange([mod_cols], "gather_mod")
    gathering_mid = _copy_start(shard(mid), "gather_branches_start", gather=True, after=mod_g)
    gathering_ffn = _copy_start(shard(late), "gather_ffn_start", gather=True, after=gathering_mid[-1])
    mod_g = mod_g + gathering_ffn[-1][0, 0]
    mod = lax.dynamic_slice_in_dim(mod_g, me * B, B, axis=1).transpose(1, 0, 2).reshape(B, NMOD * D)
    sh1, sc1, g1, sh2, sc2, g2 = [mod[:, i * D:(i + 1) * D].reshape(B, 1, D) for i in range(NMOD)]

    onehot = (jnp.asarray(_bucket_table()).reshape(1, -1) == jnp.arange(NBUCK, dtype=jnp.int32)[:, None]).astype(f32)
    bias = mm(rel_bias.T, onehot, "nn", f32, "bias_table", tn=8192, precision=HI).reshape(HQ, WIN, 2 * WIN)
    sinks = attn_sinks.reshape(HQ, 1, 1)
    a_log_pad = jnp.pad(dn_a_log, ((0, 0), (DNH, LANE - 2 * DNH)))
    dt_bias_pad = jnp.pad(dn_dt_bias, ((0, 0), (DNH, LANE - 2 * DNH)))

    (u1,) = rowcall_fwd("mix_pre", f_rms_mod, [(x, D, 0)], [sc1, sh1], [norm_mix_pre], [(D, bf16)])
    proj = mm(u1.reshape(T, D), wp, "nt", bf16, "proj", tm=512, tn=CB_BA * LANE, b_cols=(0, 1)).reshape(B, S, CB_BA * LANE)
    ba = mm(u1.reshape(T, D), wp, "nt", f32, "proj_ba", tn=LANE, b_cols=(CB_BA, 1)).reshape(B, S, LANE)
    ya = attn_fwd(proj, bias, sinks)
    qkvn = dnconv_fwd(proj, conv_dn)
    (bg,) = rowcall_fwd("dn_gate", f_gate, [(ba, LANE, 0)], [], [a_log_pad, dt_bias_pad], [(LANE, f32)])
    o_dn, states = delta_fwd(qkvn, bg)
    gw.update(zip(mid, _copy_finish(gathering_mid, len(mid), o_dn, "gather_branches_finish", gather=True)))
    wa = _cols_gathered(gw["w_attn_branch"])
    wd = _cols_gathered(gw["w_dn_branch"])
    wo = gw["w_out"].reshape(D, D)
    (yd,) = rowcall_fwd("dn_out", f_dnout, [(o_dn, DNH * DND, 0), (proj, DNH * DND, CB_DZ // 4)], [], [dn_norm_w], [(DNH * DND, bf16)])
    pa = mm(ya.reshape(T, HQ * HD), wa, "nn", bf16, "attn_branch").reshape(B, S, D)
    pd = mm(yd.reshape(T, DNH * DND), wd, "nn", bf16, "dn_branch").reshape(B, S, D)
    merge_tok = [(proj, D, CB_GA // 8), (proj, D, CB_GD // 8), (pa, D, 0), (pd, D, 0)]
    (merged,) = rowcall_fwd("merge", f_merge, merge_tok, [], [], [(D, bf16)])
    y1 = mm(merged.reshape(T, D), wo, "nn", bf16, "mix_out").reshape(B, S, D)
    post_pre = ([(x, D, 0), (y1, D, 0)], [g1, sc2, sh2], [norm_mix_post, norm_ffn_pre])
    h1, u2 = rowcall_fwd("mix_post_ffn_pre", f_post_pre, *post_pre, [(D, f32), (D, bf16)])
    gw.update(zip(late, _copy_finish(gathering_ffn, len(late), h1, "gather_ffn_finish", gather=True)))
    wup = gw["ffn_w_up"].reshape(2 * DFF, D)
    conv_ffn = _cols_gathered(gw["ffn_conv_w"]).astype(f32)
    wdown = gw["ffn_w_down"].reshape(DFF, D)
    up = mm(u2.reshape(T, D), wup, "nt", bf16, "ffn_up", tn=2816).reshape(B, S, 2 * DFF)
    act = ffnconv_fwd(up, conv_ffn)
    y2 = mm(act.reshape(T, DFF), wdown, "nn", bf16, "ffn_down", tk=2816).reshape(B, S, D)

    dh1_a, dy2, dg2, dw_ffn_post, loss_b = loss_head(h1, y2, loss_target, g2, norm_ffn_post)
    dy2f = dy2.reshape(T, D)
    dact = mm(dy2f, wdown, "nt", bf16, "ffn_down_dx", tn=2816).reshape(B, S, DFF)
    g_wdown = mm(act.reshape(T, DFF), dy2f, "tn", bf16, "ffn_down_dw", tm=1408, tk=2048)
    in_flight = []

    def send_off(d, tag):
        in_flight.append((d, _copy_start([a.astype(bf16) for a in d.values()], "scatter_" + tag + "_start")))
        return in_flight[-1][1][-1][0, 0]

    started = send_off(dict(ffn_w_down=g_wdown.reshape(NDEV, DFF // NDEV, D)), "ffn_down")
    dup, g_conv_ffn = ffnconv_bwd(up, conv_ffn + started, dact)
    dupf = dup.reshape(2, T, DFF)
    g_conv_ffn = g_conv_ffn.transpose(1, 0, 2).reshape(FK, 2 * DFF)
    du2 = mm(dupf, wup, "nn", bf16, "ffn_up_dx", tk=2816).reshape(B, S, D)
    g_wup = mm(dupf, u2.reshape(T, D), "tn", bf16, "ffn_up_dw", tm=1408, tk=2048)
    started = send_off(dict(ffn_w_up=g_wup.reshape(NDEV, 2 * DFF // NDEV, D), ffn_conv_w=_cols_split(g_conv_ffn)), "ffn_up")
    post_pre = (post_pre[0], [g1 + started, sc2, sh2], post_pre[2])
    dh1, dy1, dg1, dsc2, dsh2, dw_mix_post, dw_ffn_pre = rowcall_bwd(
        "mix_post_ffn_pre_bwd", functools.partial(f_post_pre, rms=_rms_vjp), *post_pre, [(dh1_a, D, 0), (du2, D, 0)], [(0, f32), (1, bf16)])
    dy1f = dy1.reshape(T, D)
    dmerged = mm(dy1f, wo, "nt", bf16, "mix_out_dx").reshape(B, S, D)
    g_wo = mm(merged.reshape(T, D), dy1f, "tn", bf16, "mix_out_dw", tk=2048)
    dproj = lax.empty((B, S, NP), bf16)
    dproj, dpa, dpd = rowcall_bwd("merge_bwd", f_merge, merge_tok, [], [], [(dmerged, D, 0)],
                                  [(0, bf16), (1, bf16), (2, bf16), (3, bf16)], join_first=2, into=(dproj, CB_GA // 16))
    dpaf, dpdf = dpa.reshape(T, D), dpd.reshape(T, D)
    dya = mm(dpaf, wa, "nt", bf16, "attn_branch_dx").reshape(B, S, HQ * HD)
    g_wa = mm(ya.reshape(T, HQ * HD), dpaf, "tn", bf16, "attn_branch_dw", tk=2048)
    dyd = mm(dpdf, wd, "nt", bf16, "dn_branch_dx").reshape(B, S, DNH * DND)
    g_wd = mm(yd.reshape(T, DNH * DND), dpdf, "tn", bf16, "dn_branch_dw", tk=2048)
    dproj, do_dn, dw_dn_norm = rowcall_bwd("dn_out_bwd", functools.partial(f_dnout, rms=_rms_vjp), [(o_dn, DNH * DND, 0), (proj, DNH * DND, CB_DZ // 4)], [], [dn_norm_w],
                                           [(dyd, DNH * DND, 0)], [(1, bf16), (0, f32)], into=(dproj, CB_DZ // 4))
    started = send_off(dict(w_attn_branch=_cols_split(g_wa), w_dn_branch=_cols_split(g_wd), w_out=g_wo.reshape(NDEV, D // NDEV, D)), "branches")
    dqkvn, dbg = delta_bwd(qkvn, bg + started, states, do_dn)
    dproj, da_log_pad, ddt_bias_pad = rowcall_bwd("dn_gate_bwd", f_gate, [(ba, LANE, 0)], [], [a_log_pad, dt_bias_pad],
                                                  [(dbg, LANE, 0)], [(0, bf16)], into=(dproj, CB_BA))
    dproj, g_conv_dn = dnconv_bwd(proj, conv_dn, dqkvn, dproj)
    dproj, dk, dv, dbias, dsinks = attn_bwd(proj, bias, sinks, dya, dproj)
    dproj = lax.dynamic_update_slice(dproj, jnp.concatenate([dk, dv], axis=2), (0, 0, CB_AK * LANE)).reshape(T, NP)
    g_wp = mm(dproj, u1.reshape(T, D), "tn", bf16, "proj_dw", tm=1664, tk=1024)
    started = send_off(dict(w_in=_unpack_w_in(g_wp).reshape(NDEV, IN_DIM // NDEV, D), dn_conv_w=_cols_split(g_conv_dn)), "w_in")
    du1 = mm(dproj, wp, "nn", bf16, "proj_dx", tm=512, tk=NP).reshape(B, S, D)
    grad_x, dsc1, dsh1, dw_mix_pre = rowcall_bwd("mix_pre_bwd", functools.partial(f_rms_mod, rms=_rms_vjp), [(x, D, 0)], [sc1 + started, sh1], [norm_mix_pre],
                                                 [(du1, D, 0)], [(0, f32)], add=(dh1, D, 0))
    g_rel = mm(dbias.reshape(HQ, WIN * 2 * WIN), onehot, "nt", f32, "rel_bias_dw", tk=8192, precision=HI)

    dmod = jnp.concatenate([dsh1, dsc1, dg1, dsh2, dsc2, dg2], axis=2).reshape(B, NMOD * D)

    zrow = lambda a: jnp.concatenate([a.reshape(1, -1), jnp.zeros((B - 1, a.size), f32)], axis=0)
    small_g = jnp.concatenate([
        dmod, dw_mix_pre.reshape(B, D), dw_mix_post.reshape(B, D), dw_ffn_pre.reshape(B, D), dw_ffn_post.reshape(B, D),
        da_log_pad.reshape(B, LANE)[:, DNH:2 * DNH], ddt_bias_pad.reshape(B, LANE)[:, DNH:2 * DNH], dw_dn_norm.reshape(B, DND),
        zrow(dsinks), zrow(g_rel.T), loss_b.reshape(B, LANE)[:, :1], jnp.zeros((B, SMALL_PAD - SMALL_N - 1), f32)], axis=1)
    (small_all,) = _exchange([small_g], "gather_small")
    dmod_cols = lax.dynamic_slice_in_dim(small_all.reshape(NDEV * B, SMALL_PAD), me * ncol, ncol, axis=1)
    g_ada_w = ada_bwd(c_all, dmod_cols)
    parts = {}
    for i, (d, started) in enumerate(in_flight):
        parts.update(zip(d, _copy_finish(started, len(d), g_ada_w, "scatter_finish_%d" % i)))
    small_w = dict(ada_b=(ada_b, m_ada_b, v_ada_b), norm_mix_pre=(norm_mix_pre, m_norm_mix_pre, v_norm_mix_pre),
                   norm_mix_post=(norm_mix_post, m_norm_mix_post, v_norm_mix_post), norm_ffn_pre=(norm_ffn_pre, m_norm_ffn_pre, v_norm_ffn_pre),
                   norm_ffn_post=(norm_ffn_post, m_norm_ffn_post, v_norm_ffn_post), dn_a_log=(dn_a_log, m_dn_a_log, v_dn_a_log),
                   dn_dt_bias=(dn_dt_bias, m_dn_dt_bias, v_dn_dt_bias), dn_norm_w=(dn_norm_w, m_dn_norm_w, v_dn_norm_w),
                   attn_sinks=(attn_sinks, m_attn_sinks, v_attn_sinks), rel_bias=(rel_bias, m_rel_bias, v_rel_bias))

    def pack(i, fill):
        row = jnp.concatenate([small_w[n][i].reshape(1, -1) for n, _ in SMALL], axis=1)
        return jnp.pad(row, ((0, 0), (0, SMALL_PAD - SMALL_N)), constant_values=fill)

    small_out = adamw(pack(0, 0.0), small_all.reshape(NDEV * B, 1, SMALL_PAD), pack(1, 0.0), pack(2, 1.0), "adamw_small")
    loss = small_out[0][0, SMALL_N]

    res = {}
    off = 0
    for n, size in SMALL:
        shp = small_w[n][0].shape
        res[n] = [o[:, off:off + size].reshape(shp) for o in small_out]
        off += size
    res["ada_w"] = [o[None] for o in adamw(ada_w[0], g_ada_w[None], m_ada_w[0], v_ada_w[0], "adamw_ada_w")]
    moments = dict(w_in=(m_w_in, v_w_in), dn_conv_w=(m_dn_conv_w, v_dn_conv_w), w_attn_branch=(m_w_attn_branch, v_w_attn_branch),
                   w_dn_branch=(m_w_dn_branch, v_w_dn_branch), w_out=(m_w_out, v_w_out), ffn_w_up=(m_ffn_w_up, v_ffn_w_up),
                   ffn_conv_w=(m_ffn_conv_w, v_ffn_conv_w), ffn_w_down=(m_ffn_w_down, v_ffn_w_down))
    for n in big_names:
        outs = adamw(local(n, big[n]), parts[n], local(n, moments[n][0]), local(n, moments[n][1]), "adamw_" + n)
        res[n] = [(o.T if n in transposed else o)[None] for o in outs]

    order = ["ada_w", "ada_b", "norm_mix_pre", "norm_mix_post", "norm_ffn_pre", "norm_ffn_post", "w_in", "dn_conv_w", "dn_a_log",
             "dn_dt_bias", "dn_norm_w", "attn_sinks", "rel_bias", "w_attn_branch", "w_dn_branch", "w_out", "ffn_w_up", "ffn_conv_w",
             "ffn_w_down"]
    return (loss, grad_x, *[res[n][0] for n in order], *[res[n][1] for n in order], *[res[n][2] for n in order],
            *[res[n][3] for n in order])
```

```python
import functools
import math

import numpy as np
import jax
import jax.numpy as jnp
from jax import lax
from jax.experimental import pallas as pl
from jax.experimental.pallas import tpu as pltpu

f32 = jnp.float32
bf16 = jnp.bfloat16
HI = lax.Precision.HIGHEST
MID = lax.Precision.HIGH
MESH = pl.DeviceIdType.MESH

NDEV = 8
D = 1024
HQ, HKV, HD, WIN, NBUCK, MAXDIST = 8, 2, 64, 128, 32, 128
DNH, DND, DNK, CH = 4, 128, 4, 64
DFF, FK = 2816, 3
NMOD = 6
RMS_EPS = 1e-6
L2_EPS = 1e-6
NEG_INF = -1e30
LR, B1, B2, EPS, WD, STEP = 0.001, 0.9, 0.999, 1e-08, 0.01, 10

LANE = 128
CB_GA, CB_GD, CB_AQ, CB_DQKV, CB_DZ, CB_AK, CB_AV, CB_BA, NPB = 0, 8, 16, 20, 32, 36, 37, 38, 39
NP = NPB * LANE
IN_SPLITS = (HQ * HD, HKV * HD, HKV * HD, 3 * DNH * DND, DNH * DND, DNH, DNH, D, D)
IN_DIM = sum(IN_SPLITS)
VMEM_LIMIT = 56 * 1024 * 1024

SMALL = (("ada_b", NMOD * D), ("norm_mix_pre", D), ("norm_mix_post", D), ("norm_ffn_pre", D), ("norm_ffn_post", D),
         ("dn_a_log", DNH), ("dn_dt_bias", DNH), ("dn_norm_w", DND), ("attn_sinks", HQ), ("rel_bias", NBUCK * HQ))
SMALL_N = sum(n for _, n in SMALL)
SMALL_PAD = 10752


def _cp(sem):
    return pltpu.CompilerParams(dimension_semantics=sem, vmem_limit_bytes=VMEM_LIMIT)


def _pick(dim, target):
    if dim <= target:
        return dim
    best = None
    for d in range(LANE, target + 1, LANE):
        if dim % d == 0:
            best = d
    assert best is not None, (dim, target)
    return best


def _me():
    x, y, c = lax.axis_index("x"), lax.axis_index("y"), lax.axis_index("c")
    return x, y, c, 4 * x + 2 * y + c


def _peer(x, y, c, k):
    px = 1 - x if k & 4 else x
    py = 1 - y if k & 2 else y
    pc = 1 - c if k & 1 else c
    return (px, py, pc), 4 * px + 2 * py + pc


class _Comm:
    def __init__(self, arrs, two_level=False):
        self.arrs, self.n, self.two_level = list(arrs), len(arrs), two_level
        self.out_shape = [jax.ShapeDtypeStruct((NDEV,) + a.shape, a.dtype) for a in arrs]
        nsem = self.n * (NDEV - 1)
        self.scratch = [pltpu.SemaphoreType.DMA((nsem,)), pltpu.SemaphoreType.DMA((nsem,)), pltpu.SemaphoreType.DMA((self.n,))]
        self.specs = [pl.BlockSpec(memory_space=pl.ANY)] * self.n

    def phases(self, ins, out, send, recv, loc):
        x, y, c, me = _me()

        def remote(a, k, src, dst, to):
            s = a * (NDEV - 1) + k - 1
            return pltpu.make_async_remote_copy(src_ref=src, dst_ref=dst, send_sem=send.at[s], recv_sem=recv.at[s],
                                                device_id=to, device_id_type=MESH)

        def local(a):
            return pltpu.make_async_copy(ins[a], out[a].at[me], loc.at[a])

        if not self.two_level:
            def mine(a, k):
                peer, pid = _peer(x, y, c, k)
                return remote(a, k, ins[a], out[a].at[me], peer)

            def theirs(a, k):
                peer, pid = _peer(x, y, c, k)
                return remote(a, k, ins[a], out[a].at[pid], peer)

            def start():
                for a in range(self.n):
                    local(a).start()
                    for k in range(1, NDEV):
                        mine(a, k).start()

            def forward():
                pass

            def finish():
                for a in range(self.n):
                    for k in range(1, NDEV):
                        mine(a, k).wait_send()
                    for k in range(1, NDEV):
                        theirs(a, k).wait_recv()
                    local(a).wait()

            return start, forward, finish

        sibling = (x, y, 1 - c)
        chips = [(1 - x, y), (x, 1 - y), (1 - x, 1 - y)]
        slot = lambda px, py, pc: 4 * px + 2 * py + pc

        def own(a, k, to):
            return remote(a, k, ins[a], out[a].at[me], to)

        def landed(a, k, frm):
            return remote(a, k, ins[a], out[a].at[slot(*frm)], frm)

        def passed(a, j):
            rows = out[a].at[slot(*chips[j], c)]
            return remote(a, 5 + j, rows, rows, sibling)

        def start():
            for a in range(self.n):
                local(a).start()
                own(a, 1, sibling).start()
                for j, chip in enumerate(chips):
                    own(a, 2 + j, (*chip, c)).start()

        def forward():
            for a in range(self.n):
                for j, chip in enumerate(chips):
                    landed(a, 2 + j, (*chip, c)).wait_recv()
                    passed(a, j).start()

        def finish():
            for a in range(self.n):
                landed(a, 1, sibling).wait_recv()
                for j, chip in enumerate(chips):
                    remote(a, 5 + j, ins[a], out[a].at[slot(*chip, 1 - c)], sibling).wait_recv()
                own(a, 1, sibling).wait_send()
                for j, chip in enumerate(chips):
                    own(a, 2 + j, (*chip, c)).wait_send()
                    passed(a, j).wait_send()
                local(a).wait()

        return start, forward, finish


def _copy_start(arrs, name, gather=False, after=None):
    n = len(arrs)
    order = [] if after is None else [after]
    n_in = 2 * n + len(order)
    block = (lambda ref, j: ref) if gather else (lambda ref, j: ref.at[j])

    def body(*refs):
        ins, lands, send, recv, own, token = refs[:n], refs[n:2 * n], refs[n_in], refs[n_in + 1], refs[n_in + 2], refs[-1]
        x, y, c, me = _me()
        for a in range(n):
            pltpu.make_async_copy(block(ins[a], me), lands[a].at[me], own.at[a]).start()
            for k in range(1, NDEV):
                peer, pid = _peer(x, y, c, k)
                s = a * (NDEV - 1) + k - 1
                pltpu.make_async_remote_copy(src_ref=block(ins[a], pid), dst_ref=lands[a].at[me], send_sem=send.at[s],
                                             recv_sem=recv.at[s], device_id=peer, device_id_type=MESH).start()
        token[...] = jnp.zeros(token.shape, token.dtype)

    hbm, sem = pl.BlockSpec(memory_space=pltpu.HBM), pl.BlockSpec(memory_space=pltpu.SEMAPHORE)
    nsem = n * (NDEV - 1)
    land_shapes = [((NDEV,) + a.shape if gather else a.shape) for a in arrs]
    thru = [pltpu.HBM(a.shape, a.dtype) for a in arrs] + [pltpu.HBM(shp, a.dtype) for shp, a in zip(land_shapes, arrs)]
    return pl.pallas_call(
        body, name=name, in_specs=[hbm] * (2 * n) + [pl.BlockSpec(memory_space=pl.ANY)] * len(order),
        out_shape=(pltpu.SemaphoreType.DMA((nsem,)), pltpu.SemaphoreType.DMA((nsem,)), pltpu.SemaphoreType.DMA((n,)), *thru,
                   jax.ShapeDtypeStruct((8, LANE), f32)),
        out_specs=(sem, sem, sem, *[hbm] * (2 * n), pl.BlockSpec(memory_space=pltpu.VMEM)),
        input_output_aliases={i: 3 + i for i in range(2 * n)},
        compiler_params=pltpu.CompilerParams(has_side_effects=pltpu.SideEffectType.DATAFLOW_SIDE_EFFECTING),
    )(*[pltpu.with_memory_space_constraint(a, pltpu.HBM) for a in arrs],
      *[pltpu.with_memory_space_constraint(lax.empty(shp, a.dtype), pltpu.HBM) for shp, a in zip(land_shapes, arrs)], *order)


def _copy_finish(started, n, after, name, gather=False):
    send, recv, own, *rest = started
    srcs, lands = rest[:n], rest[n:2 * n]
    block = (lambda ref, j: ref) if gather else (lambda ref, j: ref.at[j])

    def body(*refs):
        ins, lnd, send_ref, recv_ref, own_ref = refs[:n], refs[n:2 * n], refs[2 * n], refs[2 * n + 1], refs[2 * n + 2]
        x, y, c, me = _me()
        for a in range(n):
            pltpu.make_async_copy(block(ins[a], me), lnd[a].at[me], own_ref.at[a]).wait()
            for k in range(1, NDEV):
                peer, pid = _peer(x, y, c, k)
                s = a * (NDEV - 1) + k - 1
                cp = pltpu.make_async_remote_copy(src_ref=block(ins[a], pid), dst_ref=lnd[a].at[pid], send_sem=send_ref.at[s],
                                                  recv_sem=recv_ref.at[s], device_id=peer, device_id_type=MESH)
                cp.wait_send()
                cp.wait_recv()

    hbm, sem = pl.BlockSpec(memory_space=pltpu.HBM), pl.BlockSpec(memory_space=pltpu.SEMAPHORE)
    thru = [pltpu.HBM(a.shape, a.dtype) for a in srcs] + [pltpu.HBM(a.shape, a.dtype) for a in lands]
    out = pl.pallas_call(
        body, name=name, in_specs=[hbm] * (2 * n) + [sem, sem, sem, pl.BlockSpec(memory_space=pl.ANY)],
        out_shape=tuple(thru), out_specs=tuple([hbm] * (2 * n)), input_output_aliases={i: i for i in range(2 * n)},
        compiler_params=pltpu.CompilerParams(has_side_effects=pltpu.SideEffectType.DATAFLOW_SIDE_EFFECTING),
    )(*srcs, *lands, send, recv, own, after)
    return list(out[n:])


def _exchange(arrs, name, two_level=False):
    comm = _Comm(arrs, two_level)

    def body(*refs):
        start, forward, finish = comm.phases(refs[:comm.n], refs[comm.n:2 * comm.n], *refs[2 * comm.n:])
        start()
        forward()
        finish()

    return pl.pallas_call(body, name=name, out_shape=comm.out_shape, in_specs=comm.specs, out_specs=comm.specs,
                          scratch_shapes=comm.scratch, compiler_params=pltpu.CompilerParams(has_side_effects=True))(*arrs)


def mm(a, b, mode, out_dtype, name, tm=1024, tn=1024, tk=1024, precision=None, b_cols=None):
    a_parts = a.shape[0] if a.ndim == 3 else 1
    b_parts = b.shape[0] if b.ndim == 3 else 1
    assert b_parts == 1 or mode == "tn"
    ash, bsh = (a.shape[-2], a.shape[-1] * a_parts), b.shape[-2:]
    if mode == "nn":
        (M, K), (K2, N) = ash, bsh
    elif mode == "nt":
        (M, K), (N, K2) = ash, bsh
    else:
        (K, M), (K2, N) = ash, (bsh[0], bsh[1] * b_parts)
    assert K == K2, (name, a.shape, b.shape)
    col0 = 0
    if b_cols is not None:
        assert mode in ("nn", "nt") and tn % LANE == 0
        col0, N = b_cols[0], b_cols[1] * tn
    if mode == "tn":
        tm, tn, tk = _pick(M // a_parts, tm), _pick(N // b_parts, tn), _pick(K, tk)
    else:
        tm, tn, tk = _pick(M, tm), _pick(N // b_parts, tn), _pick(K // a_parts, tk)
    nk = K // tk
    if mode == "tn" and a_parts > 1:
        per = M // tm // a_parts
        a_spec = pl.BlockSpec((None, tk, tm), lambda i, j, k: (i // per, k, i % per))
    elif mode == "tn":
        a_spec = pl.BlockSpec((tk, tm), lambda i, j, k: (k, i))
    elif a_parts > 1:
        per = nk // a_parts
        a_spec = pl.BlockSpec((None, tm, tk), lambda i, j, k: (k // per, i, k % per))
    else:
        a_spec = pl.BlockSpec((tm, tk), lambda i, j, k: (i, k))
    if mode == "nt":
        b_spec = pl.BlockSpec((tn, tk), lambda i, j, k: (col0 + j, k))
    elif b_parts > 1:
        per = N // tn // b_parts
        b_spec = pl.BlockSpec((None, tk, tn), lambda i, j, k: (j // per, k, j % per))
    else:
        b_spec = pl.BlockSpec((tk, tn), lambda i, j, k: (k, col0 + j))
    dims = {"nn": ((1,), (0,)), "nt": ((1,), (1,)), "tn": ((0,), (0,))}[mode]

    def body(a_ref, b_ref, o_ref, *scr):
        p = lax.dot_general(a_ref[...], b_ref[...], (dims, ((), ())), preferred_element_type=f32, precision=precision)
        if nk == 1:
            o_ref[...] = p.astype(o_ref.dtype)
        else:
            acc = scr[0]
            k = pl.program_id(2)

            @pl.when(k == 0)
            def _():
                acc[...] = p

            @pl.when(k > 0)
            def _():
                acc[...] += p

            @pl.when(k == nk - 1)
            def _():
                o_ref[...] = acc[...].astype(o_ref.dtype)

    return pl.pallas_call(
        body, name=name, grid=(M // tm, N // tn, nk), in_specs=[a_spec, b_spec],
        out_specs=pl.BlockSpec((tm, tn), lambda i, j, k: (i, j)), out_shape=jax.ShapeDtypeStruct((M, N), out_dtype),
        scratch_shapes=[pltpu.VMEM((tm, tn), f32)] if nk > 1 else [],
        compiler_params=_cp(("parallel", "parallel", "arbitrary")),
    )(a, b)


ROW_TILE = 512


def rowcall(name, fn, tok, bat, con, tok_out, acc_out, ts=ROW_TILE, into=None):
    B, S = tok[0][0].shape[:2]
    ts = min(ts, S)
    nt, nb, nc, no, na = len(tok), len(bat), len(con), len(tok_out), len(acc_out)
    nin = nt + nb + nc + (1 if into is not None else 0)

    ns = S // ts
    steps = B * ns

    def body(*refs):
        tr, br, cr = refs[:nt], refs[nt:nt + nb], refs[nt + nb:nt + nb + nc]
        orf, arf = refs[nin:nin + no], refs[nin + no:nin + no + na]
        bufs, sem = refs[nin + no + na:-1], refs[-1]
        s = pl.program_id(1)
        step = pl.program_id(0) * ns + s

        def tile_copy(i, k):
            w, cb = tok[i][1], tok[i][2]
            src = tr[i].at[k // ns, pl.ds((k % ns) * ts, ts), pl.ds(cb * w, w)]
            return pltpu.make_async_copy(src, bufs[i].at[k % RING], sem.at[i, k % RING])

        @pl.when(step == 0)
        def _():
            for k in range(min(RING - 1, steps)):
                for i in range(nt):
                    tile_copy(i, k).start(priority=i % 2)

        @pl.when(step + (RING - 1) < steps)
        def _():
            for i in range(nt):
                tile_copy(i, step + (RING - 1)).start(priority=i % 2)

        for i in range(nt):
            tile_copy(i, step).wait()
        touts, aouts = fn([bufs[i][step % RING] for i in range(nt)], [r[0] for r in br], [r[...] for r in cr])
        for r, v in zip(orf, touts):
            r[0] = v.astype(r.dtype)
        for r, v in zip(arf, aouts):
            @pl.when(s == 0)
            def _(r=r):
                r[...] = jnp.zeros(r.shape, r.dtype)
            r[0] += v.astype(f32)

    in_specs = [pl.BlockSpec(memory_space=pl.ANY) for _ in tok]
    in_specs += [pl.BlockSpec((1,) + a.shape[1:], lambda b, s: (b, 0, 0)) for a in bat]
    in_specs += [pl.BlockSpec(a.shape, lambda b, s, nd=a.ndim: (0,) * nd) for a in con]
    out_specs = [pl.BlockSpec((1, ts, w), lambda b, s: (b, s, 0)) for (w, _) in tok_out]
    out_specs += [pl.BlockSpec((1,) + shp, lambda b, s, nd=len(shp): (b,) + (0,) * nd) for shp in acc_out]
    out_shape = [jax.ShapeDtypeStruct((B, S, w), dt) for (w, dt) in tok_out]
    out_shape += [jax.ShapeDtypeStruct((B,) + shp, f32) for shp in acc_out]
    extra, aliases = [], {}
    if into is not None:
        buf, cb = into
        assert buf.dtype == tok_out[0][1]
        in_specs.append(pl.BlockSpec(memory_space=pl.ANY))
        out_specs[0] = pl.BlockSpec((1, ts, tok_out[0][0]), lambda b, s: (b, s, cb))
        out_shape[0] = jax.ShapeDtypeStruct(buf.shape, buf.dtype)
        extra, aliases = [buf], {nin - 1: 0}
    return pl.pallas_call(
        body, name=name, grid=(B, S // ts), in_specs=in_specs, out_specs=out_specs, out_shape=out_shape,
        input_output_aliases=aliases, compiler_params=_cp(("arbitrary", "arbitrary")),
        scratch_shapes=[pltpu.VMEM((RING, ts, w), a.dtype) for (a, w, _) in tok] + [pltpu.SemaphoreType.DMA((nt, RING))],
    )(*[t[0] for t in tok], *bat, *con, *extra)


def rowcall_fwd(name, f, tok, bat, con, tok_out, ts=2 * ROW_TILE):
    def fn(t, b, c):
        return f([v.astype(f32) for v in t], b, c), []
    return rowcall(name, fn, tok, bat, con, tok_out, [], ts)


def rowcall_bwd(name, f, tok, bat, con, cts, tok_grads, add=None, ts=ROW_TILE, join_first=1, into=None):
    nt, ncts = len(tok), len(cts)

    def fn(t, b, c):
        prim = [v.astype(f32) for v in t[:nt]]
        ct = [v.astype(f32) for v in t[nt:nt + ncts]]
        _, vjp = jax.vjp(lambda tt, bb, cc: f(tt, bb, cc), prim, b, c)
        dt, db, dc = vjp(ct)
        touts = [dt[i] for i, _ in tok_grads]
        if add is not None:
            touts[0] = touts[0] + t[nt + ncts].astype(f32)
        if join_first > 1:
            touts = [jnp.concatenate(touts[:join_first], axis=1)] + touts[join_first:]
        return touts, list(db) + list(dc)

    all_tok = list(tok) + list(cts) + ([add] if add is not None else [])
    tok_out = [(tok[i][1], dt) for i, dt in tok_grads]
    if join_first > 1:
        tok_out = [(sum(w for w, _ in tok_out[:join_first]), tok_out[0][1])] + tok_out[join_first:]
    acc_out = [tuple(a.shape[1:]) for a in bat] + [tuple(a.shape) for a in con]
    return rowcall(name, fn, all_tok, bat, con, tok_out, acc_out, ts, into)


def _rms(y, w):
    return y * lax.rsqrt(jnp.mean(y * y, axis=-1, keepdims=True) + RMS_EPS) * w


@jax.custom_vjp
def _rms_vjp(y, w):
    return _rms(y, w)


def _rms_vjp_fwd(y, w):
    r = lax.rsqrt(jnp.mean(y * y, axis=-1, keepdims=True) + RMS_EPS)
    yhat = y * r
    return yhat * w, (yhat, r, w)


def _rms_vjp_bwd(res, g):
    yhat, r, w = res
    gw = g * w
    return r * (gw - yhat * jnp.mean(gw * yhat, axis=-1, keepdims=True)), jnp.sum(g * yhat, axis=0, keepdims=True)


_rms_vjp.defvjp(_rms_vjp_fwd, _rms_vjp_bwd)


def f_rms_mod(t, b, c, rms=_rms):
    return [rms(t[0], c[0]) * (1.0 + b[0]) + b[1]]


def f_post_pre(t, b, c, rms=_rms):
    h1 = t[0] + b[0] * rms(t[1], c[0])
    return [h1, rms(h1, c[1]) * (1.0 + b[1]) + b[2]]


def f_merge(t, b, c):
    ga, gd, ya, yd = t
    return [jax.nn.sigmoid(ga) * ya + jax.nn.sigmoid(gd) * yd]


def f_dnout(t, b, c, rms=_rms):
    o, z = t
    outs = []
    for h in range(DNH):
        sl = slice(h * DND, (h + 1) * DND)
        zh = z[:, sl]
        outs.append(rms(o[:, sl], c[0]) * (zh * jax.nn.sigmoid(zh)))
    return [jnp.concatenate(outs, axis=1)]


def _softplus(x):
    return jnp.maximum(x, 0.0) + jnp.log(1.0 + jnp.exp(-jnp.abs(x)))


def f_gate(t, b, c):
    ba = t[0]
    a_log, dt_bias = c
    lane = lax.broadcasted_iota(jnp.int32, ba.shape, 1)
    beta = jax.nn.sigmoid(ba)
    g = -jnp.exp(a_log) * _softplus(ba + dt_bias)
    return [jnp.where(lane < DNH, beta, jnp.where(lane < 2 * DNH, g, 0.0))]


def _bucket_table():
    qi = np.arange(WIN)[:, None]
    kj = np.arange(2 * WIN)[None, :]
    dist = np.maximum(WIN + qi - kj, 0)
    max_exact = NBUCK // 2
    scaled = np.log(np.maximum(dist, 1).astype(np.float64) / max_exact) / math.log(MAXDIST / max_exact)
    large = np.minimum(max_exact + (scaled * (NBUCK - max_exact)).astype(np.int32), NBUCK - 1)
    return np.where(dist < max_exact, dist, large).astype(np.int32)


def _attn_mask(n):
    qi = lax.broadcasted_iota(jnp.int32, (WIN, 2 * WIN), 0)
    kj = lax.broadcasted_iota(jnp.int32, (WIN, 2 * WIN), 1)
    dist = WIN + qi - kj
    return (dist >= 0) & (dist < WIN) & ((kj >= WIN) | (n > 0))


def _swap_halves(x):
    return pltpu.roll(x, HD, axis=x.ndim - 1)


@jax.custom_vjp
def _swap_halves_vjp(x):
    return _swap_halves(x)


_swap_halves_vjp.defvjp(lambda x: (_swap_halves(x), None), lambda _, g: (_swap_halves(g),))


def _sink_softmax(s, sinks):
    m = jnp.maximum(jnp.max(s, axis=-1, keepdims=True), sinks)
    p = jnp.exp(s - m)
    return p / (jnp.sum(p, axis=-1, keepdims=True) + jnp.exp(sinks - m))


@jax.custom_vjp
def _sink_softmax_vjp(s, sinks):
    return _sink_softmax(s, sinks)


def _sink_softmax_fwd(s, sinks):
    m = jnp.maximum(jnp.max(s, axis=-1, keepdims=True), sinks)
    p = jnp.exp(s - m)
    sink = jnp.exp(sinks - m)
    inv = 1.0 / (jnp.sum(p, axis=-1, keepdims=True) + sink)
    return p * inv, (p * inv, sink * inv)


def _sink_softmax_bwd(res, g):
    probs, sink_prob = res
    d = jnp.sum(g * probs, axis=-1, keepdims=True)
    return probs * (g - d), -jnp.sum(sink_prob * d, axis=(0, 2)).reshape(HQ, 1, 1)


_sink_softmax_vjp.defvjp(_sink_softmax_fwd, _sink_softmax_bwd)


def _attn_block(q, kp, kc, vp, vc, bias, sinks, mask, differentiated):
    dot = _bdot_bf16_vjp if differentiated else _bdot_bf16
    swap = _swap_halves_vjp if differentiated else _swap_halves
    B, grp = q.shape[0], HQ // HKV
    upper = lax.broadcasted_iota(jnp.int32, (2 * WIN, LANE), 1) >= HD

    def placed(natural, swapped, j, half):
        keep = upper if half == 1 else ~upper
        return jnp.where(keep, natural if j == half else swapped, 0.0)

    qh, ks, vs = [], [], []
    for b in range(B):
        kb, vb = jnp.concatenate([kp[b], kc[b]], axis=0), jnp.concatenate([vp[b], vc[b]], axis=0)
        kb_sw, vb_sw = swap(kb), swap(vb)
        for h in range(HQ):
            qh.append(q[b, :, (h // 2) * LANE:(h // 2 + 1) * LANE])
            ks.append(placed(kb, kb_sw, h // grp, h % 2))
            vs.append(placed(vb, vb_sw, h // grp, h % 2))
    s = dot(_stack(qh), _stack(ks), 2, 2).reshape(B, HQ, WIN, 2 * WIN) * (HD ** -0.5)
    probs = (_sink_softmax_vjp if differentiated else _sink_softmax)(jnp.where(mask, s + bias, NEG_INF), sinks)
    o = dot(probs.reshape(B * HQ, WIN, 2 * WIN), _stack(vs), 2, 1)
    return _stack([jnp.concatenate([o[b * HQ + 2 * i] + o[b * HQ + 2 * i + 1] for i in range(HQ // 2)], axis=1) for b in range(B)])


def _attn_specs(B, NB):
    last = NB - 1
    return [
        pl.BlockSpec((B, WIN, HQ * HD), lambda n: (0, jnp.minimum(n, last), CB_AQ // 4)),
        pl.BlockSpec((B, WIN, LANE), lambda n: (0, jnp.clip(n - 1, 0, last), CB_AK)),
        pl.BlockSpec((B, WIN, LANE), lambda n: (0, jnp.minimum(n, last), CB_AK)),
        pl.BlockSpec((B, WIN, LANE), lambda n: (0, jnp.clip(n - 1, 0, last), CB_AV)),
        pl.BlockSpec((B, WIN, LANE), lambda n: (0, jnp.minimum(n, last), CB_AV)),
        pl.BlockSpec((HQ, WIN, 2 * WIN), lambda n: (0, 0, 0)),
        pl.BlockSpec((HQ, 1, 1), lambda n: (0, 0, 0)),
    ]


def attn_fwd(proj, bias, sinks):
    B, S, _ = proj.shape
    NB = S // WIN

    def body(q, kp, kc, vp, vc, bias_ref, sink_ref, o_ref):
        mask = _attn_mask(pl.program_id(0))
        o = _attn_block(*[r[...].astype(f32) for r in (q, kp, kc, vp, vc)], bias_ref[...], sink_ref[...], mask, False)
        o_ref[...] = o.astype(o_ref.dtype)

    return pl.pallas_call(
        body, name="attn_fwd", grid=(NB,), in_specs=_attn_specs(B, NB),
        out_specs=pl.BlockSpec((B, WIN, HQ * HD), lambda n: (0, n, 0)), out_shape=jax.ShapeDtypeStruct((B, S, HQ * HD), bf16),
        compiler_params=_cp(("parallel",)),
    )(proj, proj, proj, proj, proj, bias, sinks)


def attn_bwd(proj, bias, sinks, dy, dproj):
    B, S, _ = proj.shape
    NB = S // WIN
    last = NB - 1

    def body(q, kp, kc, vp, vc, bias_ref, sink_ref, dy_ref, _, dq_ref, dk_ref, dv_ref, dbias_ref, dsink_ref, kcar, vcar):
        n = pl.program_id(0)

        @pl.when(n == 0)
        def _():
            dbias_ref[...] = jnp.zeros(dbias_ref.shape, f32)
            dsink_ref[...] = jnp.zeros(dsink_ref.shape, f32)
            kcar[...] = jnp.zeros(kcar.shape, f32)
            vcar[...] = jnp.zeros(vcar.shape, f32)

        @pl.when(n < NB)
        def _():
            mask = _attn_mask(n)
            _, vjp = jax.vjp(lambda *a: _attn_block(*a, mask, True), *[r[...].astype(f32) for r in (q, kp, kc, vp, vc)],
                             bias_ref[...], sink_ref[...])
            dq, dkp, dkc, dvp, dvc, dbias, dsink = vjp(dy_ref[...].astype(f32))
            dq_ref[...] = dq.astype(dq_ref.dtype)
            dbias_ref[...] += dbias
            dsink_ref[...] += dsink
            dk_ref[...] = (kcar[...] + dkp).astype(dk_ref.dtype)
            dv_ref[...] = (vcar[...] + dvp).astype(dv_ref.dtype)
            kcar[...] = dkc
            vcar[...] = dvc

        @pl.when(n == NB)
        def _():
            dk_ref[...] = kcar[...].astype(dk_ref.dtype)
            dv_ref[...] = vcar[...].astype(dv_ref.dtype)

    in_specs = _attn_specs(B, NB) + [pl.BlockSpec((B, WIN, HQ * HD), lambda n: (0, jnp.minimum(n, last), 0)),
                                     pl.BlockSpec(memory_space=pl.ANY)]
    kv_out = pl.BlockSpec((B, WIN, LANE), lambda n: (0, jnp.maximum(n - 1, 0), 0))
    return pl.pallas_call(
        body, name="attn_bwd", grid=(NB + 1,), in_specs=in_specs, input_output_aliases={8: 0},
        out_specs=[pl.BlockSpec((B, WIN, HQ * HD), lambda n: (0, jnp.minimum(n, last), CB_AQ // 4)), kv_out, kv_out,
                   pl.BlockSpec((HQ, WIN, 2 * WIN), lambda n: (0, 0, 0)), pl.BlockSpec((HQ, 1, 1), lambda n: (0, 0, 0))],
        out_shape=[jax.ShapeDtypeStruct(dproj.shape, dproj.dtype), jax.ShapeDtypeStruct((B, S, LANE), bf16),
                   jax.ShapeDtypeStruct((B, S, LANE), bf16), jax.ShapeDtypeStruct((HQ, WIN, 2 * WIN), f32),
                   jax.ShapeDtypeStruct((HQ, 1, 1), f32)],
        scratch_shapes=[pltpu.VMEM((B, WIN, LANE), f32), pltpu.VMEM((B, WIN, LANE), f32)],
        compiler_params=_cp(("arbitrary",)),
    )(proj, proj, proj, proj, proj, bias, sinks, dy, dproj)


DN_ROWS, FFN_ROWS = 256, 32
RING = 3


def _stage_rows(dst, value):
    dst[0:8] = jnp.zeros((8, LANE), f32)
    dst[8:8 + value.shape[0]] = value


def _conv_rows(xs, w, width, r, rows):
    wins = [xs[pl.ds(r + 8 - (width - 1) + j, rows), :] for j in range(width)]
    out = w[0:1] * wins[0]
    for j in range(1, width):
        out = out + w[j:j + 1] * wins[j]
    return out, wins


def _fold8(v):
    return jnp.sum(v.reshape(v.shape[0] // 8, 8, LANE), axis=0)


def _conv_rows_t(ds, w, width, r, rows):
    out = w[0:1] * ds[pl.ds(r + width - 1, rows), :]
    for j in range(1, width):
        out = out + w[j:j + 1] * ds[pl.ds(r + width - 1 - j, rows), :]
    return out


def _dn_outblk(i):
    return (i % DNH) * 3 + i // DNH


def _dn_act(c, isqk):
    sg = jax.nn.sigmoid(c)
    y = c * sg
    n = lax.rsqrt(jnp.sum(y * y, axis=-1, keepdims=True) + L2_EPS)
    return jnp.where(isqk, y * n, y), sg, n


def dnconv_fwd(proj, conv_w):
    B, S, _ = proj.shape
    rows = min(DN_ROWS, S)

    def body(x_ref, w_ref, o_ref, xs):
        isqk = pl.program_id(0) < 2 * DNH
        _stage_rows(xs, x_ref[0].astype(f32))
        w = w_ref[...]
        for r in range(0, S, rows):
            c, _ = _conv_rows(xs, w, DNK, r, rows)
            o_ref[0, pl.ds(r, rows), :] = _dn_act(c, isqk)[0]

    return pl.pallas_call(
        body, name="dnconv_fwd", grid=(3 * DNH, B),
        in_specs=[pl.BlockSpec((1, S, LANE), lambda i, b: (b, 0, CB_DQKV + i)), pl.BlockSpec((DNK, LANE), lambda i, b: (0, i))],
        out_specs=pl.BlockSpec((1, S, LANE), lambda i, b: (b, 0, _dn_outblk(i))),
        out_shape=jax.ShapeDtypeStruct((B, S, 3 * DNH * DND), f32), scratch_shapes=[pltpu.VMEM((S + 8, LANE), f32)],
        compiler_params=_cp(("parallel", "parallel")),
    )(proj, conv_w)


def dnconv_bwd(proj, conv_w, dqkvn, dproj):
    B, S, _ = proj.shape
    rows = min(DN_ROWS, S)

    def body(x_ref, w_ref, dy_ref, _, dx_ref, dw_ref, xs, ds):
        isqk = pl.program_id(0) < 2 * DNH
        _stage_rows(xs, x_ref[0].astype(f32))
        w = w_ref[...]
        dw = [jnp.zeros((8, LANE), f32) for _ in range(DNK)]
        for r in range(0, S, rows):
            c, wins = _conv_rows(xs, w, DNK, r, rows)
            out, sg, n = _dn_act(c, isqk)
            dout = dy_ref[0, pl.ds(r, rows), :]
            dy = jnp.where(isqk, n * (dout - out * jnp.sum(dout * out, axis=-1, keepdims=True)), dout)
            dc = dy * (sg * (1.0 + c * (1.0 - sg)))
            ds[pl.ds(r, rows), :] = dc
            for j in range(DNK):
                dw[j] = dw[j] + _fold8(dc * wins[j])
        ds[S:S + 8] = jnp.zeros((8, LANE), f32)
        for r in range(0, S, rows):
            dx_ref[0, pl.ds(r, rows), :] = _conv_rows_t(ds, w, DNK, r, rows).astype(dx_ref.dtype)

        @pl.when(pl.program_id(1) == 0)
        def _():
            dw_ref[...] = jnp.zeros(dw_ref.shape, f32)
        dw_ref[...] += jnp.concatenate([jnp.sum(d, axis=0, keepdims=True) for d in dw], axis=0)

    return pl.pallas_call(
        body, name="dnconv_bwd", grid=(3 * DNH, B),
        in_specs=[pl.BlockSpec((1, S, LANE), lambda i, b: (b, 0, CB_DQKV + i)), pl.BlockSpec((DNK, LANE), lambda i, b: (0, i)),
                  pl.BlockSpec((1, S, LANE), lambda i, b: (b, 0, _dn_outblk(i))), pl.BlockSpec(memory_space=pl.ANY)],
        out_specs=[pl.BlockSpec((1, S, LANE), lambda i, b: (b, 0, CB_DQKV + i)), pl.BlockSpec((DNK, LANE), lambda i, b: (0, i))],
        out_shape=[jax.ShapeDtypeStruct(dproj.shape, dproj.dtype), jax.ShapeDtypeStruct((DNK, 3 * DNH * DND), f32)],
        scratch_shapes=[pltpu.VMEM((S + 8, LANE), f32), pltpu.VMEM((S + 8, LANE), f32)],
        input_output_aliases={3: 0}, compiler_params=_cp(("parallel", "arbitrary")),
    )(proj, conv_w, dqkvn, dproj)


def _bdot(a, b, ca, cb, precision=HI):
    return lax.dot_general(a, b, (((ca,), (cb,)), ((0,), (0,))), preferred_element_type=f32, precision=precision)


def _bdot_bf16(a, b, ca, cb):
    return _bdot(a.astype(bf16), b.astype(bf16), ca, cb, None)


@functools.partial(jax.custom_vjp, nondiff_argnums=(2, 3))
def _bdot_bf16_vjp(a, b, ca, cb):
    return _bdot_bf16(a, b, ca, cb)


def _bdot_bf16_fwd(a, b, ca, cb):
    return _bdot_bf16(a, b, ca, cb), (a, b)


def _bdot_bf16_bwd(ca, cb, res, g):
    a, b = res
    fa, fb = 3 - ca, 3 - cb
    da = _bdot_bf16(g, b, 2, fb) if ca == 2 else _bdot_bf16(b, g, fb, 2)
    db = _bdot_bf16(a, g, fa, 1) if cb == 1 else _bdot_bf16(g, a, 1, fa)
    return da, db


_bdot_bf16_vjp.defvjp(_bdot_bf16_fwd, _bdot_bf16_bwd)


def _neumann_inverse(low):
    n = low.shape[-1]
    eye = (lax.broadcasted_iota(jnp.int32, (n, n), 0) == lax.broadcasted_iota(jnp.int32, (n, n), 1)).astype(f32)
    p = -low
    x = eye[None] + p
    for _ in range(5):
        p = _bdot_bf16(p, p, 2, 1)
        x = x + _bdot_bf16(x, p, 2, 1)
    return x


@jax.custom_vjp
def _unit_lower_inverse(low):
    return _neumann_inverse(low)


def _uli_fwd(low):
    t = _neumann_inverse(low)
    return t, t


def _uli_bwd(t, dt):
    return (-_bdot_bf16(_bdot_bf16(t, dt, 1, 1), t, 2, 2),)


_unit_lower_inverse.defvjp(_uli_fwd, _uli_bwd)


def _stack(xs):
    return jnp.concatenate([x[None] for x in xs], axis=0)


DELTA_CHUNKS = 4


def _delta_chunks(qkv, bg, state, differentiated):
    inverse = _unit_lower_inverse if differentiated else _neumann_inverse
    lo = _bdot_bf16_vjp if differentiated else _bdot_bf16
    B, n = qkv.shape[0], qkv.shape[1] // CH
    G = B * DNH
    N = n * G
    triples = [(i, b, h) for i in range(n) for b in range(B) for h in range(DNH)]
    col = lambda i, b, h, kind: qkv[b, i * CH:(i + 1) * CH, (3 * h + kind) * DND:(3 * h + kind + 1) * DND]
    q, k, v = [_stack([col(i, b, h, kind) for i, b, h in triples]) for kind in range(3)]
    lane = lax.broadcasted_iota(jnp.int32, (CH, LANE), 1)
    pick = lambda i, b, l: jnp.sum(jnp.where(lane == l, bg[b, i * CH:(i + 1) * CH], 0.0), axis=1, keepdims=True)
    beta = _stack([pick(i, b, h) for i, b, h in triples])
    g = _stack([pick(i, b, h + DNH) for i, b, h in triples])
    ri = lax.broadcasted_iota(jnp.int32, (CH, CH), 0)
    ci = lax.broadcasted_iota(jnp.int32, (CH, CH), 1)
    incl, strict = (ri >= ci)[None], (ri > ci)[None]
    gc = _bdot(jnp.broadcast_to(incl.astype(f32), (N, CH, CH)), jnp.broadcast_to(g, (N, CH, LANE)), 2, 1, MID)
    e0 = jnp.broadcast_to((lane == 0).astype(f32)[None], (N, CH, LANE))
    gc_row = _bdot(e0, gc, 2, 2, MID)
    diff = gc[:, :, :CH] - gc_row
    decay = jnp.where(incl, jnp.exp(jnp.where(incl, diff, 0.0)), 0.0)
    qs = q * (DND ** -0.5)
    kb, vb = k * beta, v * beta
    eg = jnp.exp(gc)
    with_k = lo(jnp.concatenate([kb, qs], axis=1), k, 2, 2)
    low = jnp.where(strict, with_k[:, :CH] * decay, 0.0)
    intra = jnp.where(incl, with_k[:, CH:] * decay, 0.0)
    tinv = inverse(low)
    solved = lo(tinv, jnp.concatenate([vb, kb * eg], axis=2), 2, 1)
    gl = gc[:, CH - 1:CH, :]
    k_tail = k * jnp.exp(gl - gc)
    to_state = jnp.concatenate([solved[:, :, DND:], qs * eg], axis=1)
    decay_all = jnp.exp(gl)
    outs = []
    for i in range(n):
        sl = slice(i * G, (i + 1) * G)
        with_state = lo(to_state[sl], state, 2, 1)
        v_new = solved[sl, :, :DND] - with_state[:, :CH]
        outs.append(with_state[:, CH:] + lo(intra[sl], v_new, 2, 1))
        state = state * decay_all[sl] + lo(k_tail[sl], v_new, 1, 1)
    return outs, state


def delta_fwd(qkvn, bg):
    B, S, _ = qkvn.shape
    n = DELTA_CHUNKS if (S // CH) % DELTA_CHUNKS == 0 else 1
    steps, G, rows = S // (n * CH), B * DNH, n * CH

    def body(qkv_ref, bg_ref, o_ref, st_ref, state):
        @pl.when(pl.program_id(0) == 0)
        def _():
            state[...] = jnp.zeros(state.shape, f32)
        s0 = state[...]
        st_ref[0] = s0
        outs, s1 = _delta_chunks(qkv_ref[...], bg_ref[...], s0, False)
        for i, o in enumerate(outs):
            for b in range(B):
                for h in range(DNH):
                    o_ref[b, i * CH:(i + 1) * CH, h * DND:(h + 1) * DND] = o[b * DNH + h]
        state[...] = s1

    return pl.pallas_call(
        body, name="delta_fwd", grid=(steps,),
        in_specs=[pl.BlockSpec((B, rows, 3 * DNH * DND), lambda c: (0, c, 0)), pl.BlockSpec((B, rows, LANE), lambda c: (0, c, 0))],
        out_specs=[pl.BlockSpec((B, rows, DNH * DND), lambda c: (0, c, 0)), pl.BlockSpec((1, G, DND, DND), lambda c: (c, 0, 0, 0))],
        out_shape=[jax.ShapeDtypeStruct((B, S, DNH * DND), f32), jax.ShapeDtypeStruct((steps, G, DND, DND), f32)],
        scratch_shapes=[pltpu.VMEM((G, DND, DND), f32)], compiler_params=_cp(("arbitrary",)),
    )(qkvn, bg)


def delta_bwd(qkvn, bg, states, do):
    B, S, _ = qkvn.shape
    steps, G = states.shape[0], B * DNH
    rows = S // steps
    n = rows // CH

    def body(qkv_ref, bg_ref, st_ref, do_ref, dqkv_ref, dbg_ref, dstate):
        @pl.when(pl.program_id(0) == 0)
        def _():
            dstate[...] = jnp.zeros(dstate.shape, f32)
        _, vjp = jax.vjp(lambda a, g, s: _delta_chunks(a, g, s, True), qkv_ref[...], bg_ref[...], st_ref[0])
        do = [_stack([do_ref[b, i * CH:(i + 1) * CH, h * DND:(h + 1) * DND] for b in range(B) for h in range(DNH)]) for i in range(n)]
        dqkv, dbg, ds = vjp((do, dstate[...]))
        dqkv_ref[...] = dqkv
        dbg_ref[...] = dbg
        dstate[...] = ds

    rev = lambda c: steps - 1 - c
    return pl.pallas_call(
        body, name="delta_bwd", grid=(steps,),
        in_specs=[pl.BlockSpec((B, rows, 3 * DNH * DND), lambda c: (0, rev(c), 0)), pl.BlockSpec((B, rows, LANE), lambda c: (0, rev(c), 0)),
                  pl.BlockSpec((1, G, DND, DND), lambda c: (rev(c), 0, 0, 0)),
                  pl.BlockSpec((B, rows, DNH * DND), lambda c: (0, rev(c), 0))],
        out_specs=[pl.BlockSpec((B, rows, 3 * DNH * DND), lambda c: (0, rev(c), 0)), pl.BlockSpec((B, rows, LANE), lambda c: (0, rev(c), 0))],
        out_shape=[jax.ShapeDtypeStruct((B, S, 3 * DNH * DND), f32), jax.ShapeDtypeStruct((B, S, LANE), f32)],
        scratch_shapes=[pltpu.VMEM((G, DND, DND), f32)], compiler_params=_cp(("arbitrary",)),
    )(qkvn, bg, states, do)


GELU_C0, GELU_C1 = math.sqrt(2.0 / math.pi), 0.044715


def _ffn_specs(S):
    nblk = DFF // LANE
    return [pl.BlockSpec((1, S, LANE), lambda i, b: (b, 0, i)), pl.BlockSpec((1, S, LANE), lambda i, b: (b, 0, nblk + i)),
            pl.BlockSpec((FK, LANE), lambda i, b: (0, i)), pl.BlockSpec((FK, LANE), lambda i, b: (0, nblk + i))]


def ffnconv_fwd(up, conv_w):
    B, S, _ = up.shape
    rows = min(FFN_ROWS, S)

    def body(g_ref, v_ref, gw_ref, vw_ref, o_ref, xg, xv):
        _stage_rows(xg, g_ref[0].astype(f32))
        _stage_rows(xv, v_ref[0].astype(f32))
        gw, vw = gw_ref[...], vw_ref[...]
        for r in range(0, S, rows):
            g, _ = _conv_rows(xg, gw, FK, r, rows)
            v, _ = _conv_rows(xv, vw, FK, r, rows)
            t = jnp.tanh(GELU_C0 * (g * (1.0 + GELU_C1 * (g * g))))
            o_ref[0, pl.ds(r, rows), :] = (0.5 * g * (1.0 + t) * v).astype(o_ref.dtype)

    return pl.pallas_call(
        body, name="ffnconv_fwd", grid=(DFF // LANE, B), in_specs=_ffn_specs(S),
        out_specs=pl.BlockSpec((1, S, LANE), lambda i, b: (b, 0, i)), out_shape=jax.ShapeDtypeStruct((B, S, DFF), bf16),
        scratch_shapes=[pltpu.VMEM((S + 8, LANE), f32)] * 2, compiler_params=_cp(("parallel", "parallel")),
    )(up, up, conv_w, conv_w)


def ffnconv_bwd(up, conv_w, dact):
    B, S, _ = up.shape
    rows = min(FFN_ROWS, S)

    def body(g_ref, v_ref, gw_ref, vw_ref, dy_ref, dx_ref, dw_ref, xg, xv, dg, dv):
        _stage_rows(xg, g_ref[0].astype(f32))
        _stage_rows(xv, v_ref[0].astype(f32))
        gw, vw = gw_ref[...], vw_ref[...]
        dgw = [jnp.zeros((8, LANE), f32) for _ in range(FK)]
        dvw = [jnp.zeros((8, LANE), f32) for _ in range(FK)]
        for r in range(0, S, rows):
            g, gwins = _conv_rows(xg, gw, FK, r, rows)
            v, vwins = _conv_rows(xv, vw, FK, r, rows)
            g2 = g * g
            t = jnp.tanh(GELU_C0 * (g * (1.0 + GELU_C1 * g2)))
            half = 0.5 * (1.0 + t)
            dgelu = half + (0.5 * GELU_C0) * g * (1.0 - t * t) * (1.0 + (3.0 * GELU_C1) * g2)
            dy = dy_ref[0, pl.ds(r, rows), :].astype(f32)
            dvc = dy * (g * half)
            dgc = dy * v * dgelu
            dg[pl.ds(r, rows), :] = dgc
            dv[pl.ds(r, rows), :] = dvc
            for j in range(FK):
                dgw[j] = dgw[j] + _fold8(dgc * gwins[j])
                dvw[j] = dvw[j] + _fold8(dvc * vwins[j])
        dg[S:S + 8] = jnp.zeros((8, LANE), f32)
        dv[S:S + 8] = jnp.zeros((8, LANE), f32)
        for r in range(0, S, rows):
            dx_ref[0, 0, pl.ds(r, rows), :] = _conv_rows_t(dg, gw, FK, r, rows).astype(dx_ref.dtype)
            dx_ref[1, 0, pl.ds(r, rows), :] = _conv_rows_t(dv, vw, FK, r, rows).astype(dx_ref.dtype)

        @pl.when(pl.program_id(1) == 0)
        def _():
            dw_ref[...] = jnp.zeros(dw_ref.shape, f32)
        dw_ref[0] += jnp.concatenate([jnp.sum(d, axis=0, keepdims=True) for d in dgw], axis=0)
        dw_ref[1] += jnp.concatenate([jnp.sum(d, axis=0, keepdims=True) for d in dvw], axis=0)

    return pl.pallas_call(
        body, name="ffnconv_bwd", grid=(DFF // LANE, B),
        in_specs=_ffn_specs(S) + [pl.BlockSpec((1, S, LANE), lambda i, b: (b, 0, i))],
        out_specs=[pl.BlockSpec((2, 1, S, LANE), lambda i, b: (0, b, 0, i)), pl.BlockSpec((2, FK, LANE), lambda i, b: (0, 0, i))],
        out_shape=[jax.ShapeDtypeStruct((2, B, S, DFF), bf16), jax.ShapeDtypeStruct((2, FK, DFF), f32)],
        scratch_shapes=[pltpu.VMEM((S + 8, LANE), f32)] * 4, compiler_params=_cp(("parallel", "arbitrary")),
    )(up, up, conv_w, conv_w, dact)


def ada_fwd(c_all, ada_w, ada_b):
    def body(c_ref, w_ref, b_ref, o_ref):
        c = c_ref[...]
        act = (c * jax.nn.sigmoid(c)).astype(bf16)
        o_ref[...] = jnp.dot(act, w_ref[...].astype(bf16), preferred_element_type=f32) + b_ref[...]

    return pl.pallas_call(body, name="ada_fwd", out_shape=jax.ShapeDtypeStruct((c_all.shape[0], ada_w.shape[1]), f32),
                          compiler_params=pltpu.CompilerParams(vmem_limit_bytes=VMEM_LIMIT))(c_all, ada_w, ada_b)


def ada_bwd(c_all, dmod):
    def body(c_ref, d_ref, o_ref):
        c = c_ref[...]
        act = (c * jax.nn.sigmoid(c)).astype(bf16)
        o_ref[...] = lax.dot_general(act, d_ref[...].astype(bf16), (((0,), (0,)), ((), ())), preferred_element_type=f32)

    return pl.pallas_call(body, name="ada_bwd", out_shape=jax.ShapeDtypeStruct((c_all.shape[1], dmod.shape[1]), f32),
                          compiler_params=pltpu.CompilerParams(vmem_limit_bytes=VMEM_LIMIT))(c_all, dmod)


def loss_head(h1, y2, target, g2, w):
    def fn(t, b, c):
        h, y, tg = [v.astype(f32) for v in t]

        def loss_fn(h, y, g, w):
            e = h + g * _rms_vjp(y, w) - tg
            return 0.5 * jnp.sum(jnp.mean(e * e, axis=-1))

        loss, grads = jax.value_and_grad(loss_fn, argnums=(0, 1, 2, 3))(h, y, b[0], c[0])
        return [grads[0], grads[1]], [grads[2], grads[3], jnp.full((1, LANE), loss, f32)]

    return rowcall("loss_head", fn, [(h1, D, 0), (y2, D, 0), (target, D, 0)], [g2], [w], [(D, f32), (D, bf16)],
                   [(1, D), (1, D), (1, LANE)])


def adamw(w, gparts, m, v, name):
    R, C = w.shape
    P = gparts.shape[0]
    budget = 2 * 1024 * 1024
    tr, tc = R, C
    if R * C * 4 > budget and R % 8 == 0:
        tr = max(t for t in range(8, R + 1, 8) if R % t == 0 and t * C * 4 <= budget)
    elif R * C * 4 > budget:
        tc = max(t for t in range(LANE, C + 1, LANE) if C % t == 0 and R * t * 4 <= budget)

    def body(w_ref, g_ref, m_ref, v_ref, go, do, mo, vo):
        g = g_ref[0].astype(f32)
        for p in range(1, P):
            g = g + g_ref[p].astype(f32)
        m2 = B1 * m_ref[...] + (1.0 - B1) * g
        v2 = B2 * v_ref[...] + (1.0 - B2) * jnp.square(g)
        m_hat = m2 * (1.0 / (1.0 - B1 ** STEP))
        v_hat = v2 * (1.0 / (1.0 - B2 ** STEP))
        go[...] = g
        do[...] = -LR * (m_hat / (jnp.sqrt(v_hat) + EPS) + WD * w_ref[...])
        mo[...] = m2
        vo[...] = v2

    blk = pl.BlockSpec((tr, tc), lambda i, j: (i, j))
    return pl.pallas_call(
        body, name=name, grid=(R // tr, C // tc), in_specs=[blk, pl.BlockSpec((P, tr, tc), lambda i, j: (0, i, j)), blk, blk],
        out_specs=[blk] * 4, out_shape=[jax.ShapeDtypeStruct((R, C), f32)] * 4, compiler_params=_cp(("parallel", "parallel")),
    )(w, gparts, m, v)


def _pack_w_in(wt):
    aq, ak, av, dqkv, dz, dbeta, da, ga, gd = jnp.split(wt, np.cumsum(IN_SPLITS)[:-1].tolist(), axis=0)
    ba = jnp.pad(jnp.concatenate([dbeta, da], axis=0), ((0, LANE - 2 * DNH), (0, 0)))
    return jnp.concatenate([ga, gd, aq, dqkv, dz, ak, av, ba], axis=0)


def _unpack_w_in(p):
    row = lambda cb, n: p[cb * LANE: cb * LANE + n]
    ba = row(CB_BA, 2 * DNH)
    return jnp.concatenate([row(CB_AQ, HQ * HD), row(CB_AK, HKV * HD), row(CB_AV, HKV * HD), row(CB_DQKV, 3 * DNH * DND),
                            row(CB_DZ, DNH * DND), ba[:DNH], ba[DNH:], row(CB_GA, D), row(CB_GD, D)], axis=0)


def _cols_gathered(g):
    return g.transpose(1, 0, 2).reshape(g.shape[1], NDEV * g.shape[2])


def _cols_split(w):
    r = w.shape[0]
    return w.reshape(r, NDEV, w.shape[1] // NDEV).transpose(1, 0, 2)


def kernel(x, c, ada_w, ada_b, norm_mix_pre, norm_mix_post, norm_ffn_pre, norm_ffn_post, w_in, dn_conv_w, dn_a_log, dn_dt_bias, dn_norm_w, attn_sinks, rel_bias, w_attn_branch, w_dn_branch, w_out, ffn_w_up, ffn_conv_w, ffn_w_down, loss_target, m_ada_w, m_ada_b, m_norm_mix_pre, m_norm_mix_post, m_norm_ffn_pre, m_norm_ffn_post, m_w_in, m_dn_conv_w, m_dn_a_log, m_dn_dt_bias, m_dn_norm_w, m_attn_sinks, m_rel_bias, m_w_attn_branch, m_w_dn_branch, m_w_out, m_ffn_w_up, m_ffn_conv_w, m_ffn_w_down, v_ada_w, v_ada_b, v_norm_mix_pre, v_norm_mix_post, v_norm_ffn_pre, v_norm_ffn_post, v_w_in, v_dn_conv_w, v_dn_a_log, v_dn_dt_bias, v_dn_norm_w, v_attn_sinks, v_rel_bias, v_w_attn_branch, v_w_dn_branch, v_w_out, v_ffn_w_up, v_ffn_conv_w, v_ffn_w_down):
    B, S, _ = x.shape
    T = B * S
    me = 4 * lax.axis_index("x") + 2 * lax.axis_index("y") + lax.axis_index("c")
    big = dict(w_in=w_in, dn_conv_w=dn_conv_w, w_attn_branch=w_attn_branch, w_dn_branch=w_dn_branch, w_out=w_out,
               ffn_w_up=ffn_w_up, ffn_conv_w=ffn_conv_w, ffn_w_down=ffn_w_down)
    big_names = list(big)

    first, mid, late = ["w_in", "dn_conv_w"], ["w_attn_branch", "w_dn_branch", "w_out"], ["ffn_w_up", "ffn_conv_w", "ffn_w_down"]
    transposed = ("w_in", "ffn_w_up")
    local = lambda n, a: a[0].T if n in transposed else a[0]
    shard = lambda names: [local(n, big[n]).astype(bf16) for n in names]
    *got, c_all = _exchange(shard(first) + [c], "gather_w_in", two_level=True)
    gw = dict(zip(first, got))
    c_all = c_all.reshape(NDEV * B, D)

    wp = _pack_w_in(gw["w_in"].reshape(IN_DIM, D))
    conv_dn = _cols_gathered(gw["dn_conv_w"]).astype(f32)

    ncol = ada_w.shape[2]
    ada_b_mine = lax.dynamic_slice_in_dim(ada_b, me * ncol, ncol, axis=1)
    mod_cols = ada_fwd(c_all, ada_w[0], ada_b_mine)
    (mod_g,) = _exchange([mod_cols], "gather_mod")
    gathering_mid = _copy_start(shard(mid), "gather_branches_start", gather=True, after=mod_g)
    gathering_ffn = _copy_start(shard(late), "gather_ffn_start", gather=True, after=gathering_mid[-1])
    mod_g = mod_g + gathering_ffn[-1][0, 0]
    mod = lax.dynamic_slice_in_dim(mod_g, me * B, B, axis=1).transpose(1, 0, 2).reshape(B, NMOD * D)
    sh1, sc1, g1, sh2, sc2, g2 = [mod[:, i * D:(i + 1) * D].reshape(B, 1, D) for i in range(NMOD)]

    onehot = (jnp.asarray(_bucket_table()).reshape(1, -1) == jnp.arange(NBUCK, dtype=jnp.int32)[:, None]).astype(f32)
    bias = mm(rel_bias.T, onehot, "nn", f32, "bias_table", tn=8192, precision=HI).reshape(HQ, WIN, 2 * WIN)
    sinks = attn_sinks.reshape(HQ, 1, 1)
    a_log_pad = jnp.pad(dn_a_log, ((0, 0), (DNH, LANE - 2 * DNH)))
    dt_bias_pad = jnp.pad(dn_dt_bias, ((0, 0), (DNH, LANE - 2 * DNH)))

    (u1,) = rowcall_fwd("mix_pre", f_rms_mod, [(x, D, 0)], [sc1, sh1], [norm_mix_pre], [(D, bf16)])
    proj = mm(u1.reshape(T, D), wp, "nt", bf16, "proj", tm=512, tn=CB_BA * LANE, b_cols=(0, 1)).reshape(B, S, CB_BA * LANE)
    ba = mm(u1.reshape(T, D), wp, "nt", f32, "proj_ba", tn=LANE, b_cols=(CB_BA, 1)).reshape(B, S, LANE)
    ya = attn_fwd(proj, bias, sinks)
    qkvn = dnconv_fwd(proj, conv_dn)
    (bg,) = rowcall_fwd("dn_gate", f_gate, [(ba, LANE, 0)], [], [a_log_pad, dt_bias_pad], [(LANE, f32)])
    o_dn, states = delta_fwd(qkvn, bg)
    gw.update(zip(mid, _copy_finish(gathering_mid, len(mid), o_dn, "gather_branches_finish", gather=True)))
    wa = _cols_gathered(gw["w_attn_branch"])
    wd = _cols_gathered(gw["w_dn_branch"])
    wo = gw["w_out"].reshape(D, D)
    (yd,) = rowcall_fwd("dn_out", f_dnout, [(o_dn, DNH * DND, 0), (proj, DNH * DND, CB_DZ // 4)], [], [dn_norm_w], [(DNH * DND, bf16)])
    pa = mm(ya.reshape(T, HQ * HD), wa, "nn", bf16, "attn_branch").reshape(B, S, D)
    pd = mm(yd.reshape(T, DNH * DND), wd, "nn", bf16, "dn_branch").reshape(B, S, D)
    merge_tok = [(proj, D, CB_GA // 8), (proj, D, CB_GD // 8), (pa, D, 0), (pd, D, 0)]
    (merged,) = rowcall_fwd("merge", f_merge, merge_tok, [], [], [(D, bf16)])
    y1 = mm(merged.reshape(T, D), wo, "nn", bf16, "mix_out").reshape(B, S, D)
    post_pre = ([(x, D, 0), (y1, D, 0)], [g1, sc2, sh2], [norm_mix_post, norm_ffn_pre])
    h1, u2 = rowcall_fwd("mix_post_ffn_pre", f_post_pre, *post_pre, [(D, f32), (D, bf16)])
    gw.update(zip(late, _copy_finish(gathering_ffn, len(late), h1, "gather_ffn_finish", gather=True)))
    wup = gw["ffn_w_up"].reshape(2 * DFF, D)
    conv_ffn = _cols_gathered(gw["ffn_conv_w"]).astype(f32)
    wdown = gw["ffn_w_down"].reshape(DFF, D)
    up = mm(u2.reshape(T, D), wup, "nt", bf16, "ffn_up", tn=2816).reshape(B, S, 2 * DFF)
    act = ffnconv_fwd(up, conv_ffn)
    y2 = mm(act.reshape(T, DFF), wdown, "nn", bf16, "ffn_down", tk=2816).reshape(B, S, D)

    dh1_a, dy2, dg2, dw_ffn_post, loss_b = loss_head(h1, y2, loss_target, g2, norm_ffn_post)
    dy2f = dy2.reshape(T, D)
    dact = mm(dy2f, wdown, "nt", bf16, "ffn_down_dx", tn=2816).reshape(B, S, DFF)
    g_wdown = mm(act.reshape(T, DFF), dy2f, "tn", bf16, "ffn_down_dw", tm=1408, tk=2048)
    in_flight = []

    def send_off(d, tag):
        in_flight.append((d, _copy_start([a.astype(bf16) for a in d.values()], "scatter_" + tag + "_start")))
        return in_flight[-1][1][-1][0, 0]

    started = send_off(dict(ffn_w_down=g_wdown.reshape(NDEV, DFF // NDEV, D)), "ffn_down")
    dup, g_conv_ffn = ffnconv_bwd(up, conv_ffn + started, dact)
    dupf = dup.reshape(2, T, DFF)
    g_conv_ffn = g_conv_ffn.transpose(1, 0, 2).reshape(FK, 2 * DFF)
    du2 = mm(dupf, wup, "nn", bf16, "ffn_up_dx", tk=2816).reshape(B, S, D)
    g_wup = mm(dupf, u2.reshape(T, D), "tn", bf16, "ffn_up_dw", tm=1408, tk=2048)
    started = send_off(dict(ffn_w_up=g_wup.reshape(NDEV, 2 * DFF // NDEV, D), ffn_conv_w=_cols_split(g_conv_ffn)), "ffn_up")
    post_pre = (post_pre[0], [g1 + started, sc2, sh2], post_pre[2])
    dh1, dy1, dg1, dsc2, dsh2, dw_mix_post, dw_ffn_pre = rowcall_bwd(
        "mix_post_ffn_pre_bwd", functools.partial(f_post_pre, rms=_rms_vjp), *post_pre, [(dh1_a, D, 0), (du2, D, 0)], [(0, f32), (1, bf16)])
    dy1f = dy1.reshape(T, D)
    dmerged = mm(dy1f, wo, "nt", bf16, "mix_out_dx").reshape(B, S, D)
    g_wo = mm(merged.reshape(T, D), dy1f, "tn", bf16, "mix_out_dw", tk=2048)
    dproj = lax.empty((B, S, NP), bf16)
    dproj, dpa, dpd = rowcall_bwd("merge_bwd", f_merge, merge_tok, [], [], [(dmerged, D, 0)],
                                  [(0, bf16), (1, bf16), (2, bf16), (3, bf16)], join_first=2, into=(dproj, CB_GA // 16))
    dpaf, dpdf = dpa.reshape(T, D), dpd.reshape(T, D)
    dya = mm(dpaf, wa, "nt", bf16, "attn_branch_dx").reshape(B, S, HQ * HD)
    g_wa = mm(ya.reshape(T, HQ * HD), dpaf, "tn", bf16, "attn_branch_dw", tk=2048)
    dyd = mm(dpdf, wd, "nt", bf16, "dn_branch_dx").reshape(B, S, DNH * DND)
    g_wd = mm(yd.reshape(T, DNH * DND), dpdf, "tn", bf16, "dn_branch_dw", tk=2048)
    dproj, do_dn, dw_dn_norm = rowcall_bwd("dn_out_bwd", functools.partial(f_dnout, rms=_rms_vjp), [(o_dn, DNH * DND, 0), (proj, DNH * DND, CB_DZ // 4)], [], [dn_norm_w],
                                           [(dyd, DNH * DND, 0)], [(1, bf16), (0, f32)], into=(dproj, CB_DZ // 4))
    started = send_off(dict(w_attn_branch=_cols_split(g_wa), w_dn_branch=_cols_split(g_wd), w_out=g_wo.reshape(NDEV, D // NDEV, D)), "branches")
    dqkvn, dbg = delta_bwd(qkvn, bg + started, states, do_dn)
    dproj, da_log_pad, ddt_bias_pad = rowcall_bwd("dn_gate_bwd", f_gate, [(ba, LANE, 0)], [], [a_log_pad, dt_bias_pad],
                                                  [(dbg, LANE, 0)], [(0, bf16)], into=(dproj, CB_BA))
    dproj, g_conv_dn = dnconv_bwd(proj, conv_dn, dqkvn, dproj)
    dproj, dk, dv, dbias, dsinks = attn_bwd(proj, bias, sinks, dya, dproj)
    dproj = lax.dynamic_update_slice(dproj, jnp.concatenate([dk, dv], axis=2), (0, 0, CB_AK * LANE)).reshape(T, NP)
    g_wp = mm(dproj, u1.reshape(T, D), "tn", bf16, "proj_dw", tm=1664, tk=1024)
    started = send_off(dict(w_in=_unpack_w_in(g_wp).reshape(NDEV, IN_DIM // NDEV, D), dn_conv_w=_cols_split(g_conv_dn)), "w_in")
    du1 = mm(dproj, wp, "nn", bf16, "proj_dx", tm=512, tk=NP).reshape(B, S, D)
    grad_x, dsc1, dsh1, dw_mix_pre = rowcall_bwd("mix_pre_bwd", functools.partial(f_rms_mod, rms=_rms_vjp), [(x, D, 0)], [sc1 + started, sh1], [norm_mix_pre],
                                                 [(du1, D, 0)], [(0, f32)], add=(dh1, D, 0))
    g_rel = mm(dbias.reshape(HQ, WIN * 2 * WIN), onehot, "nt", f32, "rel_bias_dw", tk=8192, precision=HI)

    dmod = jnp.concatenate([dsh1, dsc1, dg1, dsh2, dsc2, dg2], axis=2).reshape(B, NMOD * D)

    zrow = lambda a: jnp.concatenate([a.reshape(1, -1), jnp.zeros((B - 1, a.size), f32)], axis=0)
    small_g = jnp.concatenate([
        dmod, dw_mix_pre.reshape(B, D), dw_mix_post.reshape(B, D), dw_ffn_pre.reshape(B, D), dw_ffn_post.reshape(B, D),
        da_log_pad.reshape(B, LANE)[:, DNH:2 * DNH], ddt_bias_pad.reshape(B, LANE)[:, DNH:2 * DNH], dw_dn_norm.reshape(B, DND),
        zrow(dsinks), zrow(g_rel.T), loss_b.reshape(B, LANE)[:, :1], jnp.zeros((B, SMALL_PAD - SMALL_N - 1), f32)], axis=1)
    (small_all,) = _exchange([small_g], "gather_small")
    dmod_cols = lax.dynamic_slice_in_dim(small_all.reshape(NDEV * B, SMALL_PAD), me * ncol, ncol, axis=1)
    g_ada_w = ada_bwd(c_all, dmod_cols)
    parts = {}
    for i, (d, started) in enumerate(in_flight):
        parts.update(zip(d, _copy_finish(started, len(d), g_ada_w, "scatter_finish_%d" % i)))
    small_w = dict(ada_b=(ada_b, m_ada_b, v_ada_b), norm_mix_pre=(norm_mix_pre, m_norm_mix_pre, v_norm_mix_pre),
                   norm_mix_post=(norm_mix_post, m_norm_mix_post, v_norm_mix_post), norm_ffn_pre=(norm_ffn_pre, m_norm_ffn_pre, v_norm_ffn_pre),
                   norm_ffn_post=(norm_ffn_post, m_norm_ffn_post, v_norm_ffn_post), dn_a_log=(dn_a_log, m_dn_a_log, v_dn_a_log),
                   dn_dt_bias=(dn_dt_bias, m_dn_dt_bias, v_dn_dt_bias), dn_norm_w=(dn_norm_w, m_dn_norm_w, v_dn_norm_w),
                   attn_sinks=(attn_sinks, m_attn_sinks, v_attn_sinks), rel_bias=(rel_bias, m_rel_bias, v_rel_bias))

    def pack(i, fill):
        row = jnp.concatenate([small_w[n][i].reshape(1, -1) for n, _ in SMALL], axis=1)
        return jnp.pad(row, ((0, 0), (0, SMALL_PAD - SMALL_N)), constant_values=fill)

    small_out = adamw(pack(0, 0.0), small_all.reshape(NDEV * B, 1, SMALL_PAD), pack(1, 0.0), pack(2, 1.0), "adamw_small")
    loss = small_out[0][0, SMALL_N]

    res = {}
    off = 0
    for n, size in SMALL:
        shp = small_w[n][0].shape
        res[n] = [o[:, off:off + size].reshape(shp) for o in small_out]
        off += size
    res["ada_w"] = [o[None] for o in adamw(ada_w[0], g_ada_w[None], m_ada_w[0], v_ada_w[0], "adamw_ada_w")]
    moments = dict(w_in=(m_w_in, v_w_in), dn_conv_w=(m_dn_conv_w, v_dn_conv_w), w_attn_branch=(m_w_attn_branch, v_w_attn_branch),
                   w_dn_branch=(m_w_dn_branch, v_w_dn_branch), w_out=(m_w_out, v_w_out), ffn_w_up=(m_ffn_w_up, v_ffn_w_up),
                   ffn_conv_w=(m_ffn_conv_w, v_ffn_conv_w), ffn_w_down=(m_ffn_w_down, v_ffn_w_down))
    for n in big_names:
        outs = adamw(local(n, big[n]), parts[n], local(n, moments[n][0]), local(n, moments[n][1]), "adamw_" + n)
        res[n] = [(o.T if n in transposed else o)[None] for o in outs]

    order = ["ada_w", "ada_b", "norm_mix_pre", "norm_mix_post", "norm_ffn_pre", "norm_ffn_post", "w_in", "dn_conv_w", "dn_a_log",
             "dn_dt_bias", "dn_norm_w", "attn_sinks", "rel_bias", "w_attn_branch", "w_dn_branch", "w_out", "ffn_w_up", "ffn_conv_w",
             "ffn_w_down"]
    return (loss, grad_x, *[res[n][0] for n in order], *[res[n][1] for n in order], *[res[n][2] for n in order],
            *[res[n][3] for n in order])
```

```python
import functools
import math

import numpy as np
import jax
import jax.numpy as jnp
from jax import lax
from jax.experimental import pallas as pl
from jax.experimental.pallas import tpu as pltpu

f32 = jnp.float32
bf16 = jnp.bfloat16
HI = lax.Precision.HIGHEST
MID = lax.Precision.HIGH
MESH = pl.DeviceIdType.MESH

NDEV = 8
D = 1024
HQ, HKV, HD, WIN, NBUCK, MAXDIST = 8, 2, 64, 128, 32, 128
DNH, DND, DNK, CH = 4, 128, 4, 64
DFF, FK = 2816, 3
NMOD = 6
RMS_EPS = 1e-6
L2_EPS = 1e-6
NEG_INF = -1e30
LR, B1, B2, EPS, WD, STEP = 0.001, 0.9, 0.999, 1e-08, 0.01, 10

LANE = 128
CB_GA, CB_GD, CB_AQ, CB_DQKV, CB_DZ, CB_AK, CB_AV, CB_BA, NPB = 0, 8, 16, 20, 32, 36, 37, 38, 39
NP = NPB * LANE
IN_SPLITS = (HQ * HD, HKV * HD, HKV * HD, 3 * DNH * DND, DNH * DND, DNH, DNH, D, D)
IN_DIM = sum(IN_SPLITS)
VMEM_LIMIT = 56 * 1024 * 1024

SMALL = (("ada_b", NMOD * D), ("norm_mix_pre", D), ("norm_mix_post", D), ("norm_ffn_pre", D), ("norm_ffn_post", D),
         ("dn_a_log", DNH), ("dn_dt_bias", DNH), ("dn_norm_w", DND), ("attn_sinks", HQ), ("rel_bias", NBUCK * HQ))
SMALL_N = sum(n for _, n in SMALL)
SMALL_PAD = 10752


def _cp(sem):
    return pltpu.CompilerParams(dimension_semantics=sem, vmem_limit_bytes=VMEM_LIMIT)


def _pick(dim, target):
    if dim <= target:
        return dim
    best = None
    for d in range(LANE, target + 1, LANE):
        if dim % d == 0:
            best = d
    assert best is not None, (dim, target)
    return best


def _me():
    x, y, c = lax.axis_index("x"), lax.axis_index("y"), lax.axis_index("c")
    return x, y, c, 4 * x + 2 * y + c


def _peer(x, y, c, k):
    px = 1 - x if k & 4 else x
    py = 1 - y if k & 2 else y
    pc = 1 - c if k & 1 else c
    return (px, py, pc), 4 * px + 2 * py + pc


class _Comm:
    def __init__(self, arrs, two_level=False):
        self.arrs, self.n, self.two_level = list(arrs), len(arrs), two_level
        self.out_shape = [jax.ShapeDtypeStruct((NDEV,) + a.shape, a.dtype) for a in arrs]
        nsem = self.n * (NDEV - 1)
        self.scratch = [pltpu.SemaphoreType.DMA((nsem,)), pltpu.SemaphoreType.DMA((nsem,)), pltpu.SemaphoreType.DMA((self.n,))]
        self.specs = [pl.BlockSpec(memory_space=pl.ANY)] * self.n

    def phases(self, ins, out, send, recv, loc):
        x, y, c, me = _me()

        def remote(a, k, src, dst, to):
            s = a * (NDEV - 1) + k - 1
            return pltpu.make_async_remote_copy(src_ref=src, dst_ref=dst, send_sem=send.at[s], recv_sem=recv.at[s],
                                                device_id=to, device_id_type=MESH)

        def local(a):
            return pltpu.make_async_copy(ins[a], out[a].at[me], loc.at[a])

        if not self.two_level:
            def mine(a, k):
                peer, pid = _peer(x, y, c, k)
                return remote(a, k, ins[a], out[a].at[me], peer)

            def theirs(a, k):
                peer, pid = _peer(x, y, c, k)
                return remote(a, k, ins[a], out[a].at[pid], peer)

            def start():
                for a in range(self.n):
                    local(a).start()
                    for k in range(1, NDEV):
                        mine(a, k).start()

            def forward():
                pass

            def finish():
                for a in range(self.n):
                    for k in range(1, NDEV):
                        mine(a, k).wait_send()
                    for k in range(1, NDEV):
                        theirs(a, k).wait_recv()
                    local(a).wait()

            return start, forward, finish

        sibling = (x, y, 1 - c)
        chips = [(1 - x, y), (x, 1 - y), (1 - x, 1 - y)]
        slot = lambda px, py, pc: 4 * px + 2 * py + pc

        def own(a, k, to):
            return remote(a, k, ins[a], out[a].at[me], to)

        def landed(a, k, frm):
            return remote(a, k, ins[a], out[a].at[slot(*frm)], frm)

        def passed(a, j):
            rows = out[a].at[slot(*chips[j], c)]
            return remote(a, 5 + j, rows, rows, sibling)

        def start():
            for a in range(self.n):
                local(a).start()
                own(a, 1, sibling).start()
                for j, chip in enumerate(chips):
                    own(a, 2 + j, (*chip, c)).start()

        def forward():
            for a in range(self.n):
                for j, chip in enumerate(chips):
                    landed(a, 2 + j, (*chip, c)).wait_recv()
                    passed(a, j).start()

        def finish():
            for a in range(self.n):
                landed(a, 1, sibling).wait_recv()
                for j, chip in enumerate(chips):
                    remote(a, 5 + j, ins[a], out[a].at[slot(*chip, 1 - c)], sibling).wait_recv()
                own(a, 1, sibling).wait_send()
                for j, chip in enumerate(chips):
                    own(a, 2 + j, (*chip, c)).wait_send()
                    passed(a, j).wait_send()
                local(a).wait()

        return start, forward, finish


def _copy_start(arrs, name, gather=False, after=None):
    n = len(arrs)
    order = [] if after is None else [after]
    n_in = 2 * n + len(order)
    block = (lambda ref, j: ref) if gather else (lambda ref, j: ref.at[j])

    def body(*refs):
        ins, lands, send, recv, own, token = refs[:n], refs[n:2 * n], refs[n_in], refs[n_in + 1], refs[n_in + 2], refs[-1]
        x, y, c, me = _me()
        for a in range(n):
            pltpu.make_async_copy(block(ins[a], me), lands[a].at[me], own.at[a]).start()
            for k in range(1, NDEV):
                peer, pid = _peer(x, y, c, k)
                s = a * (NDEV - 1) + k - 1
                pltpu.make_async_remote_copy(src_ref=block(ins[a], pid), dst_ref=lands[a].at[me], send_sem=send.at[s],
                                             recv_sem=recv.at[s], device_id=peer, device_id_type=MESH).start()
        token[...] = jnp.zeros(token.shape, token.dtype)

    hbm, sem = pl.BlockSpec(memory_space=pltpu.HBM), pl.BlockSpec(memory_space=pltpu.SEMAPHORE)
    nsem = n * (NDEV - 1)
    land_shapes = [((NDEV,) + a.shape if gather else a.shape) for a in arrs]
    thru = [pltpu.HBM(a.shape, a.dtype) for a in arrs] + [pltpu.HBM(shp, a.dtype) for shp, a in zip(land_shapes, arrs)]
    return pl.pallas_call(
        body, name=name, in_specs=[hbm] * (2 * n) + [pl.BlockSpec(memory_space=pl.ANY)] * len(order),
        out_shape=(pltpu.SemaphoreType.DMA((nsem,)), pltpu.SemaphoreType.DMA((nsem,)), pltpu.SemaphoreType.DMA((n,)), *thru,
                   jax.ShapeDtypeStruct((8, LANE), f32)),
        out_specs=(sem, sem, sem, *[hbm] * (2 * n), pl.BlockSpec(memory_space=pltpu.VMEM)),
        input_output_aliases={i: 3 + i for i in range(2 * n)},
        compiler_params=pltpu.CompilerParams(has_side_effects=pltpu.SideEffectType.DATAFLOW_SIDE_EFFECTING),
    )(*[pltpu.with_memory_space_constraint(a, pltpu.HBM) for a in arrs],
      *[pltpu.with_memory_space_constraint(lax.empty(shp, a.dtype), pltpu.HBM) for shp, a in zip(land_shapes, arrs)], *order)


def _copy_finish(started, n, after, name, gather=False):
    send, recv, own, *rest = started
    srcs, lands = rest[:n], rest[n:2 * n]
    block = (lambda ref, j: ref) if gather else (lambda ref, j: ref.at[j])

    def body(*refs):
        ins, lnd, send_ref, recv_ref, own_ref = refs[:n], refs[n:2 * n], refs[2 * n], refs[2 * n + 1], refs[2 * n + 2]
        x, y, c, me = _me()
        for a in range(n):
            pltpu.make_async_copy(block(ins[a], me), lnd[a].at[me], own_ref.at[a]).wait()
            for k in range(1, NDEV):
                peer, pid = _peer(x, y, c, k)
                s = a * (NDEV - 1) + k - 1
                cp = pltpu.make_async_remote_copy(src_ref=block(ins[a], pid), dst_ref=lnd[a].at[pid], send_sem=send_ref.at[s],
                                                  recv_sem=recv_ref.at[s], device_id=peer, device_id_type=MESH)
                cp.wait_send()
                cp.wait_recv()

    hbm, sem = pl.BlockSpec(memory_space=pltpu.HBM), pl.BlockSpec(memory_space=pltpu.SEMAPHORE)
    thru = [pltpu.HBM(a.shape, a.dtype) for a in srcs] + [pltpu.HBM(a.shape, a.dtype) for a in lands]
    out = pl.pallas_call(
        body, name=name, in_specs=[hbm] * (2 * n) + [sem, sem, sem, pl.BlockSpec(memory_space=pl.ANY)],
        out_shape=tuple(thru), out_specs=tuple([hbm] * (2 * n)), input_output_aliases={i: i for i in range(2 * n)},
        compiler_params=pltpu.CompilerParams(has_side_effects=pltpu.SideEffectType.DATAFLOW_SIDE_EFFECTING),
    )(*srcs, *lands, send, recv, own, after)
    return list(out[n:])


def _exchange(arrs, name, two_level=False):
    comm = _Comm(arrs, two_level)

    def body(*refs):
        start, forward, finish = comm.phases(refs[:comm.n], refs[comm.n:2 * comm.n], *refs[2 * comm.n:])
        start()
        forward()
        finish()

    return pl.pallas_call(body, name=name, out_shape=comm.out_shape, in_specs=comm.specs, out_specs=comm.specs,
                          scratch_shapes=comm.scratch, compiler_params=pltpu.CompilerParams(has_side_effects=True))(*arrs)


def mm(a, b, mode, out_dtype, name, tm=1024, tn=1024, tk=1024, precision=None, b_cols=None):
    a_parts = a.shape[0] if a.ndim == 3 else 1
    b_parts = b.shape[0] if b.ndim == 3 else 1
    assert b_parts == 1 or mode == "tn"
    ash, bsh = (a.shape[-2], a.shape[-1] * a_parts), b.shape[-2:]
    if mode == "nn":
        (M, K), (K2, N) = ash, bsh
    elif mode == "nt":
        (M, K), (N, K2) = ash, bsh
    else:
        (K, M), (K2, N) = ash, (bsh[0], bsh[1] * b_parts)
    assert K == K2, (name, a.shape, b.shape)
    col0 = 0
    if b_cols is not None:
        assert mode in ("nn", "nt") and tn % LANE == 0
        col0, N = b_cols[0], b_cols[1] * tn
    if mode == "tn":
        tm, tn, tk = _pick(M // a_parts, tm), _pick(N // b_parts, tn), _pick(K, tk)
    else:
        tm, tn, tk = _pick(M, tm), _pick(N // b_parts, tn), _pick(K // a_parts, tk)
    nk = K // tk
    if mode == "tn" and a_parts > 1:
        per = M // tm // a_parts
        a_spec = pl.BlockSpec((None, tk, tm), lambda i, j, k: (i // per, k, i % per))
    elif mode == "tn":
        a_spec = pl.BlockSpec((tk, tm), lambda i, j, k: (k, i))
    elif a_parts > 1:
        per = nk // a_parts
        a_spec = pl.BlockSpec((None, tm, tk), lambda i, j, k: (k // per, i, k % per))
    else:
        a_spec = pl.BlockSpec((tm, tk), lambda i, j, k: (i, k))
    if mode == "nt":
        b_spec = pl.BlockSpec((tn, tk), lambda i, j, k: (col0 + j, k))
    elif b_parts > 1:
        per = N // tn // b_parts
        b_spec = pl.BlockSpec((None, tk, tn), lambda i, j, k: (j // per, k, j % per))
    else:
        b_spec = pl.BlockSpec((tk, tn), lambda i, j, k: (k, col0 + j))
    dims = {"nn": ((1,), (0,)), "nt": ((1,), (1,)), "tn": ((0,), (0,))}[mode]

    def body(a_ref, b_ref, o_ref, *scr):
        p = lax.dot_general(a_ref[...], b_ref[...], (dims, ((), ())), preferred_element_type=f32, precision=precision)
        if nk == 1:
            o_ref[...] = p.astype(o_ref.dtype)
        else:
            acc = scr[0]
            k = pl.program_id(2)

            @pl.when(k == 0)
            def _():
                acc[...] = p

            @pl.when(k > 0)
            def _():
                acc[...] += p

            @pl.when(k == nk - 1)
            def _():
                o_ref[...] = acc[...].astype(o_ref.dtype)

    return pl.pallas_call(
        body, name=name, grid=(M // tm, N // tn, nk), in_specs=[a_spec, b_spec],
        out_specs=pl.BlockSpec((tm, tn), lambda i, j, k: (i, j)), out_shape=jax.ShapeDtypeStruct((M, N), out_dtype),
        scratch_shapes=[pltpu.VMEM((tm, tn), f32)] if nk > 1 else [],
        compiler_params=_cp(("parallel", "parallel", "arbitrary")),
    )(a, b)


ROW_TILE = 512


def rowcall(name, fn, tok, bat, con, tok_out, acc_out, ts=ROW_TILE, into=None):
    B, S = tok[0][0].shape[:2]
    widest = max([w for _, w, _ in tok] + [w for w, _ in tok_out])
    ts = min(ts * max(1, D // widest), S)
    nt, nb, nc, no, na = len(tok), len(bat), len(con), len(tok_out), len(acc_out)
    nin = nt + nb + nc + (1 if into is not None else 0)

    ns = S // ts
    steps = B * ns

    def body(*refs):
        tr, br, cr = refs[:nt], refs[nt:nt + nb], refs[nt + nb:nt + nb + nc]
        orf, arf = refs[nin:nin + no], refs[nin + no:nin + no + na]
        bufs, sem = refs[nin + no + na:-1], refs[-1]
        s = pl.program_id(1)
        step = pl.program_id(0) * ns + s

        def tile_copy(i, k):
            w, cb = tok[i][1], tok[i][2]
            src = tr[i].at[k // ns, pl.ds((k % ns) * ts, ts), pl.ds(cb * w, w)]
            return pltpu.make_async_copy(src, bufs[i].at[k % RING], sem.at[i, k % RING])

        @pl.when(step == 0)
        def _():
            for k in range(min(RING - 1, steps)):
                for i in range(nt):
                    tile_copy(i, k).start()

        @pl.when(step + (RING - 1) < steps)
        def _():
            for i in range(nt):
                tile_copy(i, step + (RING - 1)).start()

        for i in range(nt):
            tile_copy(i, step).wait()
        touts, aouts = fn([bufs[i][step % RING] for i in range(nt)], [r[0] for r in br], [r[...] for r in cr])
        for r, v in zip(orf, touts):
            r[0] = v.astype(r.dtype)
        for r, v in zip(arf, aouts):
            @pl.when(s == 0)
            def _(r=r):
                r[...] = jnp.zeros(r.shape, r.dtype)
            r[0] += v.astype(f32)

    in_specs = [pl.BlockSpec(memory_space=pl.ANY) for _ in tok]
    in_specs += [pl.BlockSpec((1,) + a.shape[1:], lambda b, s: (b, 0, 0)) for a in bat]
    in_specs += [pl.BlockSpec(a.shape, lambda b, s, nd=a.ndim: (0,) * nd) for a in con]
    out_specs = [pl.BlockSpec((1, ts, w), lambda b, s: (b, s, 0)) for (w, _) in tok_out]
    out_specs += [pl.BlockSpec((1,) + shp, lambda b, s, nd=len(shp): (b,) + (0,) * nd) for shp in acc_out]
    out_shape = [jax.ShapeDtypeStruct((B, S, w), dt) for (w, dt) in tok_out]
    out_shape += [jax.ShapeDtypeStruct((B,) + shp, f32) for shp in acc_out]
    extra, aliases = [], {}
    if into is not None:
        buf, cb = into
        assert buf.dtype == tok_out[0][1]
        in_specs.append(pl.BlockSpec(memory_space=pl.ANY))
        out_specs[0] = pl.BlockSpec((1, ts, tok_out[0][0]), lambda b, s: (b, s, cb))
        out_shape[0] = jax.ShapeDtypeStruct(buf.shape, buf.dtype)
        extra, aliases = [buf], {nin - 1: 0}
    return pl.pallas_call(
        body, name=name, grid=(B, S // ts), in_specs=in_specs, out_specs=out_specs, out_shape=out_shape,
        input_output_aliases=aliases, compiler_params=_cp(("arbitrary", "arbitrary")),
        scratch_shapes=[pltpu.VMEM((RING, ts, w), a.dtype) for (a, w, _) in tok] + [pltpu.SemaphoreType.DMA((nt, RING))],
    )(*[t[0] for t in tok], *bat, *con, *extra)


def rowcall_fwd(name, f, tok, bat, con, tok_out, ts=2 * ROW_TILE):
    def fn(t, b, c):
        return f([v.astype(f32) for v in t], b, c), []
    return rowcall(name, fn, tok, bat, con, tok_out, [], ts)


def rowcall_bwd(name, f, tok, bat, con, cts, tok_grads, add=None, ts=ROW_TILE, join_first=1, into=None):
    nt, ncts = len(tok), len(cts)

    def fn(t, b, c):
        prim = [v.astype(f32) for v in t[:nt]]
        ct = [v.astype(f32) for v in t[nt:nt + ncts]]
        _, vjp = jax.vjp(lambda tt, bb, cc: f(tt, bb, cc), prim, b, c)
        dt, db, dc = vjp(ct)
        touts = [dt[i] for i, _ in tok_grads]
        if add is not None:
            touts[0] = touts[0] + t[nt + ncts].astype(f32)
        if join_first > 1:
            touts = [jnp.concatenate(touts[:join_first], axis=1)] + touts[join_first:]
        return touts, list(db) + list(dc)

    all_tok = list(tok) + list(cts) + ([add] if add is not None else [])
    tok_out = [(tok[i][1], dt) for i, dt in tok_grads]
    if join_first > 1:
        tok_out = [(sum(w for w, _ in tok_out[:join_first]), tok_out[0][1])] + tok_out[join_first:]
    acc_out = [tuple(a.shape[1:]) for a in bat] + [tuple(a.shape) for a in con]
    return rowcall(name, fn, all_tok, bat, con, tok_out, acc_out, ts, into)


def _rms(y, w):
    return y * lax.rsqrt(jnp.mean(y * y, axis=-1, keepdims=True) + RMS_EPS) * w


@jax.custom_vjp
def _rms_vjp(y, w):
    return _rms(y, w)


def _rms_vjp_fwd(y, w):
    r = lax.rsqrt(jnp.mean(y * y, axis=-1, keepdims=True) + RMS_EPS)
    yhat = y * r
    return yhat * w, (yhat, r, w)


def _rms_vjp_bwd(res, g):
    yhat, r, w = res
    gw = g * w
    return r * (gw - yhat * jnp.mean(gw * yhat, axis=-1, keepdims=True)), jnp.sum(g * yhat, axis=0, keepdims=True)


_rms_vjp.defvjp(_rms_vjp_fwd, _rms_vjp_bwd)


def f_rms_mod(t, b, c, rms=_rms):
    return [rms(t[0], c[0]) * (1.0 + b[0]) + b[1]]


def f_post_pre(t, b, c, rms=_rms):
    h1 = t[0] + b[0] * rms(t[1], c[0])
    return [h1, rms(h1, c[1]) * (1.0 + b[1]) + b[2]]


def f_merge(t, b, c):
    ga, gd, ya, yd = t
    return [jax.nn.sigmoid(ga) * ya + jax.nn.sigmoid(gd) * yd]


def f_dnout(t, b, c, rms=_rms):
    o, z = t
    outs = []
    for h in range(DNH):
        sl = slice(h * DND, (h + 1) * DND)
        zh = z[:, sl]
        outs.append(rms(o[:, sl], c[0]) * (zh * jax.nn.sigmoid(zh)))
    return [jnp.concatenate(outs, axis=1)]


def _softplus(x):
    return jnp.maximum(x, 0.0) + jnp.log(1.0 + jnp.exp(-jnp.abs(x)))


def f_gate(t, b, c):
    ba = t[0]
    a_log, dt_bias = c
    lane = lax.broadcasted_iota(jnp.int32, ba.shape, 1)
    beta = jax.nn.sigmoid(ba)
    g = -jnp.exp(a_log) * _softplus(ba + dt_bias)
    return [jnp.where(lane < DNH, beta, jnp.where(lane < 2 * DNH, g, 0.0))]


def _bucket_table():
    qi = np.arange(WIN)[:, None]
    kj = np.arange(2 * WIN)[None, :]
    dist = np.maximum(WIN + qi - kj, 0)
    max_exact = NBUCK // 2
    scaled = np.log(np.maximum(dist, 1).astype(np.float64) / max_exact) / math.log(MAXDIST / max_exact)
    large = np.minimum(max_exact + (scaled * (NBUCK - max_exact)).astype(np.int32), NBUCK - 1)
    return np.where(dist < max_exact, dist, large).astype(np.int32)


def _attn_mask(n):
    qi = lax.broadcasted_iota(jnp.int32, (WIN, 2 * WIN), 0)
    kj = lax.broadcasted_iota(jnp.int32, (WIN, 2 * WIN), 1)
    dist = WIN + qi - kj
    return (dist >= 0) & (dist < WIN) & ((kj >= WIN) | (n > 0))


def _swap_halves(x):
    return pltpu.roll(x, HD, axis=x.ndim - 1)


@jax.custom_vjp
def _swap_halves_vjp(x):
    return _swap_halves(x)


_swap_halves_vjp.defvjp(lambda x: (_swap_halves(x), None), lambda _, g: (_swap_halves(g),))


def _sink_softmax(s, sinks):
    m = jnp.maximum(jnp.max(s, axis=-1, keepdims=True), sinks)
    p = jnp.exp(s - m)
    return p / (jnp.sum(p, axis=-1, keepdims=True) + jnp.exp(sinks - m))


@jax.custom_vjp
def _sink_softmax_vjp(s, sinks):
    return _sink_softmax(s, sinks)


def _sink_softmax_fwd(s, sinks):
    m = jnp.maximum(jnp.max(s, axis=-1, keepdims=True), sinks)
    p = jnp.exp(s - m)
    sink = jnp.exp(sinks - m)
    inv = 1.0 / (jnp.sum(p, axis=-1, keepdims=True) + sink)
    return p * inv, (p * inv, sink * inv)


def _sink_softmax_bwd(res, g):
    probs, sink_prob = res
    d = jnp.sum(g * probs, axis=-1, keepdims=True)
    return probs * (g - d), -jnp.sum(sink_prob * d, axis=(0, 2)).reshape(HQ, 1, 1)


_sink_softmax_vjp.defvjp(_sink_softmax_fwd, _sink_softmax_bwd)


def _attn_block(q, kp, kc, vp, vc, bias, sinks, mask, differentiated):
    dot = _bdot_bf16_vjp if differentiated else _bdot_bf16
    swap = _swap_halves_vjp if differentiated else _swap_halves
    B, grp = q.shape[0], HQ // HKV
    upper = lax.broadcasted_iota(jnp.int32, (2 * WIN, LANE), 1) >= HD

    def placed(natural, swapped, j, half):
        keep = upper if half == 1 else ~upper
        return jnp.where(keep, natural if j == half else swapped, 0.0)

    qh, ks, vs = [], [], []
    for b in range(B):
        kb, vb = jnp.concatenate([kp[b], kc[b]], axis=0), jnp.concatenate([vp[b], vc[b]], axis=0)
        kb_sw, vb_sw = swap(kb), swap(vb)
        for h in range(HQ):
            qh.append(q[b, :, (h // 2) * LANE:(h // 2 + 1) * LANE])
            ks.append(placed(kb, kb_sw, h // grp, h % 2))
            vs.append(placed(vb, vb_sw, h // grp, h % 2))
    s = dot(_stack(qh), _stack(ks), 2, 2).reshape(B, HQ, WIN, 2 * WIN) * (HD ** -0.5)
    probs = (_sink_softmax_vjp if differentiated else _sink_softmax)(jnp.where(mask, s + bias, NEG_INF), sinks)
    o = dot(probs.reshape(B * HQ, WIN, 2 * WIN), _stack(vs), 2, 1)
    return _stack([jnp.concatenate([o[b * HQ + 2 * i] + o[b * HQ + 2 * i + 1] for i in range(HQ // 2)], axis=1) for b in range(B)])


def _attn_specs(B, NB):
    last = NB - 1
    return [
        pl.BlockSpec((B, WIN, HQ * HD), lambda n: (0, jnp.minimum(n, last), CB_AQ // 4)),
        pl.BlockSpec((B, WIN, LANE), lambda n: (0, jnp.clip(n - 1, 0, last), CB_AK)),
        pl.BlockSpec((B, WIN, LANE), lambda n: (0, jnp.minimum(n, last), CB_AK)),
        pl.BlockSpec((B, WIN, LANE), lambda n: (0, jnp.clip(n - 1, 0, last), CB_AV)),
        pl.BlockSpec((B, WIN, LANE), lambda n: (0, jnp.minimum(n, last), CB_AV)),
        pl.BlockSpec((HQ, WIN, 2 * WIN), lambda n: (0, 0, 0)),
        pl.BlockSpec((HQ, 1, 1), lambda n: (0, 0, 0)),
    ]


def attn_fwd(proj, bias, sinks):
    B, S, _ = proj.shape
    NB = S // WIN

    def body(q, kp, kc, vp, vc, bias_ref, sink_ref, o_ref):
        mask = _attn_mask(pl.program_id(0))
        o = _attn_block(*[r[...].astype(f32) for r in (q, kp, kc, vp, vc)], bias_ref[...], sink_ref[...], mask, False)
        o_ref[...] = o.astype(o_ref.dtype)

    return pl.pallas_call(
        body, name="attn_fwd", grid=(NB,), in_specs=_attn_specs(B, NB),
        out_specs=pl.BlockSpec((B, WIN, HQ * HD), lambda n: (0, n, 0)), out_shape=jax.ShapeDtypeStruct((B, S, HQ * HD), bf16),
        compiler_params=_cp(("parallel",)),
    )(proj, proj, proj, proj, proj, bias, sinks)


def attn_bwd(proj, bias, sinks, dy, dproj):
    B, S, _ = proj.shape
    NB = S // WIN
    last = NB - 1

    def body(q, kp, kc, vp, vc, bias_ref, sink_ref, dy_ref, _, dq_ref, dk_ref, dv_ref, dbias_ref, dsink_ref, kcar, vcar):
        n = pl.program_id(0)

        @pl.when(n == 0)
        def _():
            dbias_ref[...] = jnp.zeros(dbias_ref.shape, f32)
            dsink_ref[...] = jnp.zeros(dsink_ref.shape, f32)
            kcar[...] = jnp.zeros(kcar.shape, f32)
            vcar[...] = jnp.zeros(vcar.shape, f32)

        @pl.when(n < NB)
        def _():
            mask = _attn_mask(n)
            _, vjp = jax.vjp(lambda *a: _attn_block(*a, mask, True), *[r[...].astype(f32) for r in (q, kp, kc, vp, vc)],
                             bias_ref[...], sink_ref[...])
            dq, dkp, dkc, dvp, dvc, dbias, dsink = vjp(dy_ref[...].astype(f32))
            dq_ref[...] = dq.astype(dq_ref.dtype)
            dbias_ref[...] += dbias
            dsink_ref[...] += dsink
            dk_ref[...] = (kcar[...] + dkp).astype(dk_ref.dtype)
            dv_ref[...] = (vcar[...] + dvp).astype(dv_ref.dtype)
            kcar[...] = dkc
            vcar[...] = dvc

        @pl.when(n == NB)
        def _():
            dk_ref[...] = kcar[...].astype(dk_ref.dtype)
            dv_ref[...] = vcar[...].astype(dv_ref.dtype)

    in_specs = _attn_specs(B, NB) + [pl.BlockSpec((B, WIN, HQ * HD), lambda n: (0, jnp.minimum(n, last), 0)),
                                     pl.BlockSpec(memory_space=pl.ANY)]
    kv_out = pl.BlockSpec((B, WIN, LANE), lambda n: (0, jnp.maximum(n - 1, 0), 0))
    return pl.pallas_call(
        body, name="attn_bwd", grid=(NB + 1,), in_specs=in_specs, input_output_aliases={8: 0},
        out_specs=[pl.BlockSpec((B, WIN, HQ * HD), lambda n: (0, jnp.minimum(n, last), CB_AQ // 4)), kv_out, kv_out,
                   pl.BlockSpec((HQ, WIN, 2 * WIN), lambda n: (0, 0, 0)), pl.BlockSpec((HQ, 1, 1), lambda n: (0, 0, 0))],
        out_shape=[jax.ShapeDtypeStruct(dproj.shape, dproj.dtype), jax.ShapeDtypeStruct((B, S, LANE), bf16),
                   jax.ShapeDtypeStruct((B, S, LANE), bf16), jax.ShapeDtypeStruct((HQ, WIN, 2 * WIN), f32),
                   jax.ShapeDtypeStruct((HQ, 1, 1), f32)],
        scratch_shapes=[pltpu.VMEM((B, WIN, LANE), f32), pltpu.VMEM((B, WIN, LANE), f32)],
        compiler_params=_cp(("arbitrary",)),
    )(proj, proj, proj, proj, proj, bias, sinks, dy, dproj)


DN_ROWS, FFN_ROWS = 256, 32
RING = 3


def _stage_rows(dst, value):
    dst[0:8] = jnp.zeros((8, LANE), f32)
    dst[8:8 + value.shape[0]] = value


def _conv_rows(xs, w, width, r, rows):
    wins = [xs[pl.ds(r + 8 - (width - 1) + j, rows), :] for j in range(width)]
    out = w[0:1] * wins[0]
    for j in range(1, width):
        out = out + w[j:j + 1] * wins[j]
    return out, wins


def _fold8(v):
    return jnp.sum(v.reshape(v.shape[0] // 8, 8, LANE), axis=0)


def _conv_rows_t(ds, w, width, r, rows):
    out = w[0:1] * ds[pl.ds(r + width - 1, rows), :]
    for j in range(1, width):
        out = out + w[j:j + 1] * ds[pl.ds(r + width - 1 - j, rows), :]
    return out


def _dn_outblk(i):
    return (i % DNH) * 3 + i // DNH


def _dn_act(c, isqk):
    sg = jax.nn.sigmoid(c)
    y = c * sg
    n = lax.rsqrt(jnp.sum(y * y, axis=-1, keepdims=True) + L2_EPS)
    return jnp.where(isqk, y * n, y), sg, n


def dnconv_fwd(proj, conv_w):
    B, S, _ = proj.shape
    rows = min(DN_ROWS, S)

    def body(x_ref, w_ref, o_ref, xs):
        isqk = pl.program_id(0) < 2 * DNH
        _stage_rows(xs, x_ref[0].astype(f32))
        w = w_ref[...]
        for r in range(0, S, rows):
            c, _ = _conv_rows(xs, w, DNK, r, rows)
            o_ref[0, pl.ds(r, rows), :] = _dn_act(c, isqk)[0]

    return pl.pallas_call(
        body, name="dnconv_fwd", grid=(3 * DNH, B),
        in_specs=[pl.BlockSpec((1, S, LANE), lambda i, b: (b, 0, CB_DQKV + i)), pl.BlockSpec((DNK, LANE), lambda i, b: (0, i))],
        out_specs=pl.BlockSpec((1, S, LANE), lambda i, b: (b, 0, _dn_outblk(i))),
        out_shape=jax.ShapeDtypeStruct((B, S, 3 * DNH * DND), f32), scratch_shapes=[pltpu.VMEM((S + 8, LANE), f32)],
        compiler_params=_cp(("parallel", "parallel")),
    )(proj, conv_w)


def dnconv_bwd(proj, conv_w, dqkvn, dproj):
    B, S, _ = proj.shape
    rows = min(DN_ROWS, S)

    def body(x_ref, w_ref, dy_ref, _, dx_ref, dw_ref, xs, ds):
        isqk = pl.program_id(0) < 2 * DNH
        _stage_rows(xs, x_ref[0].astype(f32))
        w = w_ref[...]
        dw = [jnp.zeros((8, LANE), f32) for _ in range(DNK)]
        for r in range(0, S, rows):
            c, wins = _conv_rows(xs, w, DNK, r, rows)
            out, sg, n = _dn_act(c, isqk)
            dout = dy_ref[0, pl.ds(r, rows), :]
            dy = jnp.where(isqk, n * (dout - out * jnp.sum(dout * out, axis=-1, keepdims=True)), dout)
            dc = dy * (sg * (1.0 + c * (1.0 - sg)))
            ds[pl.ds(r, rows), :] = dc
            for j in range(DNK):
                dw[j] = dw[j] + _fold8(dc * wins[j])
        ds[S:S + 8] = jnp.zeros((8, LANE), f32)
        for r in range(0, S, rows):
            dx_ref[0, pl.ds(r, rows), :] = _conv_rows_t(ds, w, DNK, r, rows).astype(dx_ref.dtype)

        @pl.when(pl.program_id(1) == 0)
        def _():
            dw_ref[...] = jnp.zeros(dw_ref.shape, f32)
        dw_ref[...] += jnp.concatenate([jnp.sum(d, axis=0, keepdims=True) for d in dw], axis=0)

    return pl.pallas_call(
        body, name="dnconv_bwd", grid=(3 * DNH, B),
        in_specs=[pl.BlockSpec((1, S, LANE), lambda i, b: (b, 0, CB_DQKV + i)), pl.BlockSpec((DNK, LANE), lambda i, b: (0, i)),
                  pl.BlockSpec((1, S, LANE), lambda i, b: (b, 0, _dn_outblk(i))), pl.BlockSpec(memory_space=pl.ANY)],
        out_specs=[pl.BlockSpec((1, S, LANE), lambda i, b: (b, 0, CB_DQKV + i)), pl.BlockSpec((DNK, LANE), lambda i, b: (0, i))],
        out_shape=[jax.ShapeDtypeStruct(dproj.shape, dproj.dtype), jax.ShapeDtypeStruct((DNK, 3 * DNH * DND), f32)],
        scratch_shapes=[pltpu.VMEM((S + 8, LANE), f32), pltpu.VMEM((S + 8, LANE), f32)],
        input_output_aliases={3: 0}, compiler_params=_cp(("parallel", "arbitrary")),
    )(proj, conv_w, dqkvn, dproj)


def _bdot(a, b, ca, cb, precision=HI):
    return lax.dot_general(a, b, (((ca,), (cb,)), ((0,), (0,))), preferred_element_type=f32, precision=precision)


def _bdot_bf16(a, b, ca, cb):
    return _bdot(a.astype(bf16), b.astype(bf16), ca, cb, None)


@functools.partial(jax.custom_vjp, nondiff_argnums=(2, 3))
def _bdot_bf16_vjp(a, b, ca, cb):
    return _bdot_bf16(a, b, ca, cb)


def _bdot_bf16_fwd(a, b, ca, cb):
    return _bdot_bf16(a, b, ca, cb), (a, b)


def _bdot_bf16_bwd(ca, cb, res, g):
    a, b = res
    fa, fb = 3 - ca, 3 - cb
    da = _bdot_bf16(g, b, 2, fb) if ca == 2 else _bdot_bf16(b, g, fb, 2)
    db = _bdot_bf16(a, g, fa, 1) if cb == 1 else _bdot_bf16(g, a, 1, fa)
    return da, db


_bdot_bf16_vjp.defvjp(_bdot_bf16_fwd, _bdot_bf16_bwd)


def _neumann_inverse(low):
    n = low.shape[-1]
    eye = (lax.broadcasted_iota(jnp.int32, (n, n), 0) == lax.broadcasted_iota(jnp.int32, (n, n), 1)).astype(f32)
    p = -low
    x = eye[None] + p
    for _ in range(5):
        p = _bdot_bf16(p, p, 2, 1)
        x = x + _bdot_bf16(x, p, 2, 1)
    return x


@jax.custom_vjp
def _unit_lower_inverse(low):
    return _neumann_inverse(low)


def _uli_fwd(low):
    t = _neumann_inverse(low)
    return t, t


def _uli_bwd(t, dt):
    return (-_bdot_bf16(_bdot_bf16(t, dt, 1, 1), t, 2, 2),)


_unit_lower_inverse.defvjp(_uli_fwd, _uli_bwd)


def _stack(xs):
    return jnp.concatenate([x[None] for x in xs], axis=0)


DELTA_CHUNKS = 4


def _delta_chunks(qkv, bg, state, differentiated):
    inverse = _unit_lower_inverse if differentiated else _neumann_inverse
    lo = _bdot_bf16_vjp if differentiated else _bdot_bf16
    B, n = qkv.shape[0], qkv.shape[1] // CH
    G = B * DNH
    N = n * G
    triples = [(i, b, h) for i in range(n) for b in range(B) for h in range(DNH)]
    col = lambda i, b, h, kind: qkv[b, i * CH:(i + 1) * CH, (3 * h + kind) * DND:(3 * h + kind + 1) * DND]
    q, k, v = [_stack([col(i, b, h, kind) for i, b, h in triples]) for kind in range(3)]
    lane = lax.broadcasted_iota(jnp.int32, (CH, LANE), 1)
    pick = lambda i, b, l: jnp.sum(jnp.where(lane == l, bg[b, i * CH:(i + 1) * CH], 0.0), axis=1, keepdims=True)
    beta = _stack([pick(i, b, h) for i, b, h in triples])
    g = _stack([pick(i, b, h + DNH) for i, b, h in triples])
    ri = lax.broadcasted_iota(jnp.int32, (CH, CH), 0)
    ci = lax.broadcasted_iota(jnp.int32, (CH, CH), 1)
    incl, strict = (ri >= ci)[None], (ri > ci)[None]
    gc = _bdot(jnp.broadcast_to(incl.astype(f32), (N, CH, CH)), jnp.broadcast_to(g, (N, CH, LANE)), 2, 1, MID)
    e0 = jnp.broadcast_to((lane == 0).astype(f32)[None], (N, CH, LANE))
    gc_row = _bdot(e0, gc, 2, 2, MID)
    diff = gc[:, :, :CH] - gc_row
    decay = jnp.where(incl, jnp.exp(jnp.where(incl, diff, 0.0)), 0.0)
    qs = q * (DND ** -0.5)
    kb, vb = k * beta, v * beta
    eg = jnp.exp(gc)
    with_k = lo(jnp.concatenate([kb, qs], axis=1), k, 2, 2)
    low = jnp.where(strict, with_k[:, :CH] * decay, 0.0)
    intra = jnp.where(incl, with_k[:, CH:] * decay, 0.0)
    tinv = inverse(low)
    solved = lo(tinv, jnp.concatenate([vb, kb * eg], axis=2), 2, 1)
    gl = gc[:, CH - 1:CH, :]
    k_tail = k * jnp.exp(gl - gc)
    to_state = jnp.concatenate([solved[:, :, DND:], qs * eg], axis=1)
    decay_all = jnp.exp(gl)
    outs = []
    for i in range(n):
        sl = slice(i * G, (i + 1) * G)
        with_state = lo(to_state[sl], state, 2, 1)
        v_new = solved[sl, :, :DND] - with_state[:, :CH]
        outs.append(with_state[:, CH:] + lo(intra[sl], v_new, 2, 1))
        state = state * decay_all[sl] + lo(k_tail[sl], v_new, 1, 1)
    return outs, state


def delta_fwd(qkvn, bg):
    B, S, _ = qkvn.shape
    n = DELTA_CHUNKS if (S // CH) % DELTA_CHUNKS == 0 else 1
    steps, G, rows = S // (n * CH), B * DNH, n * CH

    def body(qkv_ref, bg_ref, o_ref, st_ref, state):
        @pl.when(pl.program_id(0) == 0)
        def _():
            state[...] = jnp.zeros(state.shape, f32)
        s0 = state[...]
        st_ref[0] = s0
        outs, s1 = _delta_chunks(qkv_ref[...], bg_ref[...], s0, False)
        for i, o in enumerate(outs):
            for b in range(B):
                for h in range(DNH):
                    o_ref[b, i * CH:(i + 1) * CH, h * DND:(h + 1) * DND] = o[b * DNH + h]
        state[...] = s1

    return pl.pallas_call(
        body, name="delta_fwd", grid=(steps,),
        in_specs=[pl.BlockSpec((B, rows, 3 * DNH * DND), lambda c: (0, c, 0)), pl.BlockSpec((B, rows, LANE), lambda c: (0, c, 0))],
        out_specs=[pl.BlockSpec((B, rows, DNH * DND), lambda c: (0, c, 0)), pl.BlockSpec((1, G, DND, DND), lambda c: (c, 0, 0, 0))],
        out_shape=[jax.ShapeDtypeStruct((B, S, DNH * DND), f32), jax.ShapeDtypeStruct((steps, G, DND, DND), f32)],
        scratch_shapes=[pltpu.VMEM((G, DND, DND), f32)], compiler_params=_cp(("arbitrary",)),
    )(qkvn, bg)


def delta_bwd(qkvn, bg, states, do):
    B, S, _ = qkvn.shape
    steps, G = states.shape[0], B * DNH
    rows = S // steps
    n = rows // CH

    def body(qkv_ref, bg_ref, st_ref, do_ref, dqkv_ref, dbg_ref, dstate):
        @pl.when(pl.program_id(0) == 0)
        def _():
            dstate[...] = jnp.zeros(dstate.shape, f32)
        _, vjp = jax.vjp(lambda a, g, s: _delta_chunks(a, g, s, True), qkv_ref[...], bg_ref[...], st_ref[0])
        do = [_stack([do_ref[b, i * CH:(i + 1) * CH, h * DND:(h + 1) * DND] for b in range(B) for h in range(DNH)]) for i in range(n)]
        dqkv, dbg, ds = vjp((do, dstate[...]))
        dqkv_ref[...] = dqkv
        dbg_ref[...] = dbg
        dstate[...] = ds

    rev = lambda c: steps - 1 - c
    return pl.pallas_call(
        body, name="delta_bwd", grid=(steps,),
        in_specs=[pl.BlockSpec((B, rows, 3 * DNH * DND), lambda c: (0, rev(c), 0)), pl.BlockSpec((B, rows, LANE), lambda c: (0, rev(c), 0)),
                  pl.BlockSpec((1, G, DND, DND), lambda c: (rev(c), 0, 0, 0)),
                  pl.BlockSpec((B, rows, DNH * DND), lambda c: (0, rev(c), 0))],
        out_specs=[pl.BlockSpec((B, rows, 3 * DNH * DND), lambda c: (0, rev(c), 0)), pl.BlockSpec((B, rows, LANE), lambda c: (0, rev(c), 0))],
        out_shape=[jax.ShapeDtypeStruct((B, S, 3 * DNH * DND), f32), jax.ShapeDtypeStruct((B, S, LANE), f32)],
        scratch_shapes=[pltpu.VMEM((G, DND, DND), f32)], compiler_params=_cp(("arbitrary",)),
    )(qkvn, bg, states, do)


GELU_C0, GELU_C1 = math.sqrt(2.0 / math.pi), 0.044715


def _ffn_specs(S):
    nblk = DFF // LANE
    return [pl.BlockSpec((1, S, LANE), lambda i, b: (b, 0, i)), pl.BlockSpec((1, S, LANE), lambda i, b: (b, 0, nblk + i)),
            pl.BlockSpec((FK, LANE), lambda i, b: (0, i)), pl.BlockSpec((FK, LANE), lambda i, b: (0, nblk + i))]


def ffnconv_fwd(up, conv_w):
    B, S, _ = up.shape
    rows = min(FFN_ROWS, S)

    def body(g_ref, v_ref, gw_ref, vw_ref, o_ref, xg, xv):
        _stage_rows(xg, g_ref[0].astype(f32))
        _stage_rows(xv, v_ref[0].astype(f32))
        gw, vw = gw_ref[...], vw_ref[...]
        for r in range(0, S, rows):
            g, _ = _conv_rows(xg, gw, FK, r, rows)
            v, _ = _conv_rows(xv, vw, FK, r, rows)
            t = jnp.tanh(GELU_C0 * (g * (1.0 + GELU_C1 * (g * g))))
            o_ref[0, pl.ds(r, rows), :] = (0.5 * g * (1.0 + t) * v).astype(o_ref.dtype)

    return pl.pallas_call(
        body, name="ffnconv_fwd", grid=(DFF // LANE, B), in_specs=_ffn_specs(S),
        out_specs=pl.BlockSpec((1, S, LANE), lambda i, b: (b, 0, i)), out_shape=jax.ShapeDtypeStruct((B, S, DFF), bf16),
        scratch_shapes=[pltpu.VMEM((S + 8, LANE), f32)] * 2, compiler_params=_cp(("parallel", "parallel")),
    )(up, up, conv_w, conv_w)


def ffnconv_bwd(up, conv_w, dact):
    B, S, _ = up.shape
    rows = min(FFN_ROWS, S)

    def body(g_ref, v_ref, gw_ref, vw_ref, dy_ref, dx_ref, dw_ref, xg, xv, dg, dv):
        _stage_rows(xg, g_ref[0].astype(f32))
        _stage_rows(xv, v_ref[0].astype(f32))
        gw, vw = gw_ref[...], vw_ref[...]
        dgw = [jnp.zeros((8, LANE), f32) for _ in range(FK)]
        dvw = [jnp.zeros((8, LANE), f32) for _ in range(FK)]
        for r in range(0, S, rows):
            g, gwins = _conv_rows(xg, gw, FK, r, rows)
            v, vwins = _conv_rows(xv, vw, FK, r, rows)
            g2 = g * g
            t = jnp.tanh(GELU_C0 * (g * (1.0 + GELU_C1 * g2)))
            half = 0.5 * (1.0 + t)
            dgelu = half + (0.5 * GELU_C0) * g * (1.0 - t * t) * (1.0 + (3.0 * GELU_C1) * g2)
            dy = dy_ref[0, pl.ds(r, rows), :].astype(f32)
            dvc = dy * (g * half)
            dgc = dy * v * dgelu
            dg[pl.ds(r, rows), :] = dgc
            dv[pl.ds(r, rows), :] = dvc
            for j in range(FK):
                dgw[j] = dgw[j] + _fold8(dgc * gwins[j])
                dvw[j] = dvw[j] + _fold8(dvc * vwins[j])
        dg[S:S + 8] = jnp.zeros((8, LANE), f32)
        dv[S:S + 8] = jnp.zeros((8, LANE), f32)
        for r in range(0, S, rows):
            dx_ref[0, 0, pl.ds(r, rows), :] = _conv_rows_t(dg, gw, FK, r, rows).astype(dx_ref.dtype)
            dx_ref[1, 0, pl.ds(r, rows), :] = _conv_rows_t(dv, vw, FK, r, rows).astype(dx_ref.dtype)

        @pl.when(pl.program_id(1) == 0)
        def _():
            dw_ref[...] = jnp.zeros(dw_ref.shape, f32)
        dw_ref[0] += jnp.concatenate([jnp.sum(d, axis=0, keepdims=True) for d in dgw], axis=0)
        dw_ref[1] += jnp.concatenate([jnp.sum(d, axis=0, keepdims=True) for d in dvw], axis=0)

    return pl.pallas_call(
        body, name="ffnconv_bwd", grid=(DFF // LANE, B),
        in_specs=_ffn_specs(S) + [pl.BlockSpec((1, S, LANE), lambda i, b: (b, 0, i))],
        out_specs=[pl.BlockSpec((2, 1, S, LANE), lambda i, b: (0, b, 0, i)), pl.BlockSpec((2, FK, LANE), lambda i, b: (0, 0, i))],
        out_shape=[jax.ShapeDtypeStruct((2, B, S, DFF), bf16), jax.ShapeDtypeStruct((2, FK, DFF), f32)],
        scratch_shapes=[pltpu.VMEM((S + 8, LANE), f32)] * 4, compiler_params=_cp(("parallel", "arbitrary")),
    )(up, up, conv_w, conv_w, dact)


def ada_fwd(c_all, ada_w, ada_b):
    def body(c_ref, w_ref, b_ref, o_ref):
        c = c_ref[...]
        act = (c * jax.nn.sigmoid(c)).astype(bf16)
        o_ref[...] = jnp.dot(act, w_ref[...].astype(bf16), preferred_element_type=f32) + b_ref[...]

    return pl.pallas_call(body, name="ada_fwd", out_shape=jax.ShapeDtypeStruct((c_all.shape[0], ada_w.shape[1]), f32),
                          compiler_params=pltpu.CompilerParams(vmem_limit_bytes=VMEM_LIMIT))(c_all, ada_w, ada_b)


def ada_bwd(c_all, dmod):
    def body(c_ref, d_ref, o_ref):
        c = c_ref[...]
        act = (c * jax.nn.sigmoid(c)).astype(bf16)
        o_ref[...] = lax.dot_general(act, d_ref[...].astype(bf16), (((0,), (0,)), ((), ())), preferred_element_type=f32)

    return pl.pallas_call(body, name="ada_bwd", out_shape=jax.ShapeDtypeStruct((c_all.shape[1], dmod.shape[1]), f32),
                          compiler_params=pltpu.CompilerParams(vmem_limit_bytes=VMEM_LIMIT))(c_all, dmod)


def loss_head(h1, y2, target, g2, w):
    def fn(t, b, c):
        h, y, tg = [v.astype(f32) for v in t]

        def loss_fn(h, y, g, w):
            e = h + g * _rms_vjp(y, w) - tg
            return 0.5 * jnp.sum(jnp.mean(e * e, axis=-1))

        loss, grads = jax.value_and_grad(loss_fn, argnums=(0, 1, 2, 3))(h, y, b[0], c[0])
        return [grads[0], grads[1]], [grads[2], grads[3], jnp.full((1, LANE), loss, f32)]

    return rowcall("loss_head", fn, [(h1, D, 0), (y2, D, 0), (target, D, 0)], [g2], [w], [(D, f32), (D, bf16)],
                   [(1, D), (1, D), (1, LANE)])


def adamw(w, gparts, m, v, name):
    R, C = w.shape
    P = gparts.shape[0]
    budget = 2 * 1024 * 1024
    tr, tc = R, C
    if R * C * 4 > budget and R % 8 == 0:
        tr = max(t for t in range(8, R + 1, 8) if R % t == 0 and t * C * 4 <= budget)
    elif R * C * 4 > budget:
        tc = max(t for t in range(LANE, C + 1, LANE) if C % t == 0 and R * t * 4 <= budget)

    def body(w_ref, g_ref, m_ref, v_ref, go, do, mo, vo):
        g = g_ref[0].astype(f32)
        for p in range(1, P):
            g = g + g_ref[p].astype(f32)
        m2 = B1 * m_ref[...] + (1.0 - B1) * g
        v2 = B2 * v_ref[...] + (1.0 - B2) * jnp.square(g)
        m_hat = m2 * (1.0 / (1.0 - B1 ** STEP))
        v_hat = v2 * (1.0 / (1.0 - B2 ** STEP))
        go[...] = g
        do[...] = -LR * (m_hat / (jnp.sqrt(v_hat) + EPS) + WD * w_ref[...])
        mo[...] = m2
        vo[...] = v2

    blk = pl.BlockSpec((tr, tc), lambda i, j: (i, j))
    return pl.pallas_call(
        body, name=name, grid=(R // tr, C // tc), in_specs=[blk, pl.BlockSpec((P, tr, tc), lambda i, j: (0, i, j)), blk, blk],
        out_specs=[blk] * 4, out_shape=[jax.ShapeDtypeStruct((R, C), f32)] * 4, compiler_params=_cp(("parallel", "parallel")),
    )(w, gparts, m, v)


def _pack_w_in(wt):
    aq, ak, av, dqkv, dz, dbeta, da, ga, gd = jnp.split(wt, np.cumsum(IN_SPLITS)[:-1].tolist(), axis=0)
    ba = jnp.pad(jnp.concatenate([dbeta, da], axis=0), ((0, LANE - 2 * DNH), (0, 0)))
    return jnp.concatenate([ga, gd, aq, dqkv, dz, ak, av, ba], axis=0)


def _unpack_w_in(p):
    row = lambda cb, n: p[cb * LANE: cb * LANE + n]
    ba = row(CB_BA, 2 * DNH)
    return jnp.concatenate([row(CB_AQ, HQ * HD), row(CB_AK, HKV * HD), row(CB_AV, HKV * HD), row(CB_DQKV, 3 * DNH * DND),
                            row(CB_DZ, DNH * DND), ba[:DNH], ba[DNH:], row(CB_GA, D), row(CB_GD, D)], axis=0)


def _cols_gathered(g):
    return g.transpose(1, 0, 2).reshape(g.shape[1], NDEV * g.shape[2])


def _cols_split(w):
    r = w.shape[0]
    return w.reshape(r, NDEV, w.shape[1] // NDEV).transpose(1, 0, 2)


def kernel(x, c, ada_w, ada_b, norm_mix_pre, norm_mix_post, norm_ffn_pre, norm_ffn_post, w_in, dn_conv_w, dn_a_log, dn_dt_bias, dn_norm_w, attn_sinks, rel_bias, w_attn_branch, w_dn_branch, w_out, ffn_w_up, ffn_conv_w, ffn_w_down, loss_target, m_ada_w, m_ada_b, m_norm_mix_pre, m_norm_mix_post, m_norm_ffn_pre, m_norm_ffn_post, m_w_in, m_dn_conv_w, m_dn_a_log, m_dn_dt_bias, m_dn_norm_w, m_attn_sinks, m_rel_bias, m_w_attn_branch, m_w_dn_branch, m_w_out, m_ffn_w_up, m_ffn_conv_w, m_ffn_w_down, v_ada_w, v_ada_b, v_norm_mix_pre, v_norm_mix_post, v_norm_ffn_pre, v_norm_ffn_post, v_w_in, v_dn_conv_w, v_dn_a_log, v_dn_dt_bias, v_dn_norm_w, v_attn_sinks, v_rel_bias, v_w_attn_branch, v_w_dn_branch, v_w_out, v_ffn_w_up, v_ffn_conv_w, v_ffn_w_down):
    B, S, _ = x.shape
    T = B * S
    me = 4 * lax.axis_index("x") + 2 * lax.axis_index("y") + lax.axis_index("c")
    big = dict(w_in=w_in, dn_conv_w=dn_conv_w, w_attn_branch=w_attn_branch, w_dn_branch=w_dn_branch, w_out=w_out,
               ffn_w_up=ffn_w_up, ffn_conv_w=ffn_conv_w, ffn_w_down=ffn_w_down)
    big_names = list(big)

    first, mid, late = ["w_in", "dn_conv_w"], ["w_attn_branch", "w_dn_branch", "w_out"], ["ffn_w_up", "ffn_conv_w", "ffn_w_down"]
    transposed = ("w_in", "ffn_w_up")
    local = lambda n, a: a[0].T if n in transposed else a[0]
    shard = lambda names: [local(n, big[n]).astype(bf16) for n in names]
    *got, c_all = _exchange(shard(first) + [c], "gather_w_in", two_level=True)
    gw = dict(zip(first, got))
    c_all = c_all.reshape(NDEV * B, D)

    wp = _pack_w_in(gw["w_in"].reshape(IN_DIM, D))
    conv_dn = _cols_gathered(gw["dn_conv_w"]).astype(f32)

    ncol = ada_w.shape[2]
    ada_b_mine = lax.dynamic_slice_in_dim(ada_b, me * ncol, ncol, axis=1)
    mod_cols = ada_fwd(c_all, ada_w[0], ada_b_mine)
    (mod_g,) = _exchange([mod_cols], "gather_mod")
    gathering_mid = _copy_start(shard(mid), "gather_branches_start", gather=True, after=mod_g)
    gathering_ffn = _copy_start(shard(late), "gather_ffn_start", gather=True, after=gathering_mid[-1])
    mod_g = mod_g + gathering_ffn[-1][0, 0]
    mod = lax.dynamic_slice_in_dim(mod_g, me * B, B, axis=1).transpose(1, 0, 2).reshape(B, NMOD * D)
    sh1, sc1, g1, sh2, sc2, g2 = [mod[:, i * D:(i + 1) * D].reshape(B, 1, D) for i in range(NMOD)]

    onehot = (jnp.asarray(_bucket_table()).reshape(1, -1) == jnp.arange(NBUCK, dtype=jnp.int32)[:, None]).astype(f32)
    bias = mm(rel_bias.T, onehot, "nn", f32, "bias_table", tn=8192, precision=HI).reshape(HQ, WIN, 2 * WIN)
    sinks = attn_sinks.reshape(HQ, 1, 1)
    a_log_pad = jnp.pad(dn_a_log, ((0, 0), (DNH, LANE - 2 * DNH)))
    dt_bias_pad = jnp.pad(dn_dt_bias, ((0, 0), (DNH, LANE - 2 * DNH)))

    (u1,) = rowcall_fwd("mix_pre", f_rms_mod, [(x, D, 0)], [sc1, sh1], [norm_mix_pre], [(D, bf16)])
    proj = mm(u1.reshape(T, D), wp, "nt", bf16, "proj", tm=512, tn=CB_BA * LANE, b_cols=(0, 1)).reshape(B, S, CB_BA * LANE)
    ba = mm(u1.reshape(T, D), wp, "nt", f32, "proj_ba", tn=LANE, b_cols=(CB_BA, 1)).reshape(B, S, LANE)
    ya = attn_fwd(proj, bias, sinks)
    qkvn = dnconv_fwd(proj, conv_dn)
    (bg,) = rowcall_fwd("dn_gate", f_gate, [(ba, LANE, 0)], [], [a_log_pad, dt_bias_pad], [(LANE, f32)])
    o_dn, states = delta_fwd(qkvn, bg)
    gw.update(zip(mid, _copy_finish(gathering_mid, len(mid), o_dn, "gather_branches_finish", gather=True)))
    wa = _cols_gathered(gw["w_attn_branch"])
    wd = _cols_gathered(gw["w_dn_branch"])
    wo = gw["w_out"].reshape(D, D)
    (yd,) = rowcall_fwd("dn_out", f_dnout, [(o_dn, DNH * DND, 0), (proj, DNH * DND, CB_DZ // 4)], [], [dn_norm_w], [(DNH * DND, bf16)])
    pa = mm(ya.reshape(T, HQ * HD), wa, "nn", bf16, "attn_branch").reshape(B, S, D)
    pd = mm(yd.reshape(T, DNH * DND), wd, "nn", bf16, "dn_branch").reshape(B, S, D)
    merge_tok = [(proj, D, CB_GA // 8), (proj, D, CB_GD // 8), (pa, D, 0), (pd, D, 0)]
    (merged,) = rowcall_fwd("merge", f_merge, merge_tok, [], [], [(D, bf16)])
    y1 = mm(merged.reshape(T, D), wo, "nn", bf16, "mix_out").reshape(B, S, D)
    post_pre = ([(x, D, 0), (y1, D, 0)], [g1, sc2, sh2], [norm_mix_post, norm_ffn_pre])
    h1, u2 = rowcall_fwd("mix_post_ffn_pre", f_post_pre, *post_pre, [(D, f32), (D, bf16)])
    gw.update(zip(late, _copy_finish(gathering_ffn, len(late), h1, "gather_ffn_finish", gather=True)))
    wup = gw["ffn_w_up"].reshape(2 * DFF, D)
    conv_ffn = _cols_gathered(gw["ffn_conv_w"]).astype(f32)
    wdown = gw["ffn_w_down"].reshape(DFF, D)
    up = mm(u2.reshape(T, D), wup, "nt", bf16, "ffn_up", tn=2816).reshape(B, S, 2 * DFF)
    act = ffnconv_fwd(up, conv_ffn)
    y2 = mm(act.reshape(T, DFF), wdown, "nn", bf16, "ffn_down", tk=2816).reshape(B, S, D)

    dh1_a, dy2, dg2, dw_ffn_post, loss_b = loss_head(h1, y2, loss_target, g2, norm_ffn_post)
    dy2f = dy2.reshape(T, D)
    dact = mm(dy2f, wdown, "nt", bf16, "ffn_down_dx", tn=2816).reshape(B, S, DFF)
    g_wdown = mm(act.reshape(T, DFF), dy2f, "tn", bf16, "ffn_down_dw", tm=1408, tk=2048)
    in_flight = []

    def send_off(d, tag):
        in_flight.append((d, _copy_start([a.astype(bf16) for a in d.values()], "scatter_" + tag + "_start")))
        return in_flight[-1][1][-1][0, 0]

    started = send_off(dict(ffn_w_down=g_wdown.reshape(NDEV, DFF // NDEV, D)), "ffn_down")
    dup, g_conv_ffn = ffnconv_bwd(up, conv_ffn + started, dact)
    dupf = dup.reshape(2, T, DFF)
    g_conv_ffn = g_conv_ffn.transpose(1, 0, 2).reshape(FK, 2 * DFF)
    du2 = mm(dupf, wup, "nn", bf16, "ffn_up_dx", tk=2816).reshape(B, S, D)
    g_wup = mm(dupf, u2.reshape(T, D), "tn", bf16, "ffn_up_dw", tm=1408, tk=2048)
    started = send_off(dict(ffn_w_up=g_wup.reshape(NDEV, 2 * DFF // NDEV, D), ffn_conv_w=_cols_split(g_conv_ffn)), "ffn_up")
    post_pre = (post_pre[0], [g1 + started, sc2, sh2], post_pre[2])
    dh1, dy1, dg1, dsc2, dsh2, dw_mix_post, dw_ffn_pre = rowcall_bwd(
        "mix_post_ffn_pre_bwd", functools.partial(f_post_pre, rms=_rms_vjp), *post_pre, [(dh1_a, D, 0), (du2, D, 0)], [(0, f32), (1, bf16)])
    dy1f = dy1.reshape(T, D)
    dmerged = mm(dy1f, wo, "nt", bf16, "mix_out_dx").reshape(B, S, D)
    g_wo = mm(merged.reshape(T, D), dy1f, "tn", bf16, "mix_out_dw", tk=2048)
    dproj = lax.empty((B, S, NP), bf16)
    dproj, dpa, dpd = rowcall_bwd("merge_bwd", f_merge, merge_tok, [], [], [(dmerged, D, 0)],
                                  [(0, bf16), (1, bf16), (2, bf16), (3, bf16)], join_first=2, into=(dproj, CB_GA // 16))
    dpaf, dpdf = dpa.reshape(T, D), dpd.reshape(T, D)
    dya = mm(dpaf, wa, "nt", bf16, "attn_branch_dx").reshape(B, S, HQ * HD)
    g_wa = mm(ya.reshape(T, HQ * HD), dpaf, "tn", bf16, "attn_branch_dw", tk=2048)
    dyd = mm(dpdf, wd, "nt", bf16, "dn_branch_dx").reshape(B, S, DNH * DND)
    g_wd = mm(yd.reshape(T, DNH * DND), dpdf, "tn", bf16, "dn_branch_dw", tk=2048)
    dproj, do_dn, dw_dn_norm = rowcall_bwd("dn_out_bwd", functools.partial(f_dnout, rms=_rms_vjp), [(o_dn, DNH * DND, 0), (proj, DNH * DND, CB_DZ // 4)], [], [dn_norm_w],
                                           [(dyd, DNH * DND, 0)], [(1, bf16), (0, f32)], into=(dproj, CB_DZ // 4))
    started = send_off(dict(w_attn_branch=_cols_split(g_wa), w_dn_branch=_cols_split(g_wd), w_out=g_wo.reshape(NDEV, D // NDEV, D)), "branches")
    dqkvn, dbg = delta_bwd(qkvn, bg + started, states, do_dn)
    dproj, da_log_pad, ddt_bias_pad = rowcall_bwd("dn_gate_bwd", f_gate, [(ba, LANE, 0)], [], [a_log_pad, dt_bias_pad],
                                                  [(dbg, LANE, 0)], [(0, bf16)], into=(dproj, CB_BA))
    dproj, g_conv_dn = dnconv_bwd(proj, conv_dn, dqkvn, dproj)
    dproj, dk, dv, dbias, dsinks = attn_bwd(proj, bias, sinks, dya, dproj)
    dproj = lax.dynamic_update_slice(dproj, jnp.concatenate([dk, dv], axis=2), (0, 0, CB_AK * LANE)).reshape(T, NP)
    g_wp = mm(dproj, u1.reshape(T, D), "tn", bf16, "proj_dw", tm=1664, tk=1024)
    started = send_off(dict(w_in=_unpack_w_in(g_wp).reshape(NDEV, IN_DIM // NDEV, D), dn_conv_w=_cols_split(g_conv_dn)), "w_in")
    du1 = mm(dproj, wp, "nn", bf16, "proj_dx", tm=512, tk=NP).reshape(B, S, D)
    grad_x, dsc1, dsh1, dw_mix_pre = rowcall_bwd("mix_pre_bwd", functools.partial(f_rms_mod, rms=_rms_vjp), [(x, D, 0)], [sc1 + started, sh1], [norm_mix_pre],
                                                 [(du1, D, 0)], [(0, f32)], add=(dh1, D, 0))
    g_rel = mm(dbias.reshape(HQ, WIN * 2 * WIN), onehot, "nt", f32, "rel_bias_dw", tk=8192, precision=HI)

    dmod = jnp.concatenate([dsh1, dsc1, dg1, dsh2, dsc2, dg2], axis=2).reshape(B, NMOD * D)

    zrow = lambda a: jnp.concatenate([a.reshape(1, -1), jnp.zeros((B - 1, a.size), f32)], axis=0)
    small_g = jnp.concatenate([
        dmod, dw_mix_pre.reshape(B, D), dw_mix_post.reshape(B, D), dw_ffn_pre.reshape(B, D), dw_ffn_post.reshape(B, D),
        da_log_pad.reshape(B, LANE)[:, DNH:2 * DNH], ddt_bias_pad.reshape(B, LANE)[:, DNH:2 * DNH], dw_dn_norm.reshape(B, DND),
        zrow(dsinks), zrow(g_rel.T), loss_b.reshape(B, LANE)[:, :1], jnp.zeros((B, SMALL_PAD - SMALL_N - 1), f32)], axis=1)
    (small_all,) = _exchange([small_g], "gather_small")
    dmod_cols = lax.dynamic_slice_in_dim(small_all.reshape(NDEV * B, SMALL_PAD), me * ncol, ncol, axis=1)
    g_ada_w = ada_bwd(c_all, dmod_cols)
    parts = {}
    for i, (d, started) in enumerate(in_flight):
        parts.update(zip(d, _copy_finish(started, len(d), g_ada_w, "scatter_finish_%d" % i)))
    small_w = dict(ada_b=(ada_b, m_ada_b, v_ada_b), norm_mix_pre=(norm_mix_pre, m_norm_mix_pre, v_norm_mix_pre),
                   norm_mix_post=(norm_mix_post, m_norm_mix_post, v_norm_mix_post), norm_ffn_pre=(norm_ffn_pre, m_norm_ffn_pre, v_norm_ffn_pre),
                   norm_ffn_post=(norm_ffn_post, m_norm_ffn_post, v_norm_ffn_post), dn_a_log=(dn_a_log, m_dn_a_log, v_dn_a_log),
                   dn_dt_bias=(dn_dt_bias, m_dn_dt_bias, v_dn_dt_bias), dn_norm_w=(dn_norm_w, m_dn_norm_w, v_dn_norm_w),
                   attn_sinks=(attn_sinks, m_attn_sinks, v_attn_sinks), rel_bias=(rel_bias, m_rel_bias, v_rel_bias))

    def pack(i, fill):
        row = jnp.concatenate([small_w[n][i].reshape(1, -1) for n, _ in SMALL], axis=1)
        return jnp.pad(row, ((0, 0), (0, SMALL_PAD - SMALL_N)), constant_values=fill)

    small_out = adamw(pack(0, 0.0), small_all.reshape(NDEV * B, 1, SMALL_PAD), pack(1, 0.0), pack(2, 1.0), "adamw_small")
    loss = small_out[0][0, SMALL_N]

    res = {}
    off = 0
    for n, size in SMALL:
        shp = small_w[n][0].shape
        res[n] = [o[:, off:off + size].reshape(shp) for o in small_out]
        off += size
    res["ada_w"] = [o[None] for o in adamw(ada_w[0], g_ada_w[None], m_ada_w[0], v_ada_w[0], "adamw_ada_w")]
    moments = dict(w_in=(m_w_in, v_w_in), dn_conv_w=(m_dn_conv_w, v_dn_conv_w), w_attn_branch=(m_w_attn_branch, v_w_attn_branch),
                   w_dn_branch=(m_w_dn_branch, v_w_dn_branch), w_out=(m_w_out, v_w_out), ffn_w_up=(m_ffn_w_up, v_ffn_w_up),
                   ffn_conv_w=(m_ffn_conv_w, v_ffn_conv_w), ffn_w_down=(m_ffn_w_down, v_ffn_w_down))
    for n in big_names:
        outs = adamw(local(n, big[n]), parts[n], local(n, moments[n][0]), local(n, moments[n][1]), "adamw_" + n)
        res[n] = [(o.T if n in transposed else o)[None] for o in outs]

    order = ["ada_w", "ada_b", "norm_mix_pre", "norm_mix_post", "norm_ffn_pre", "norm_ffn_post", "w_in", "dn_conv_w", "dn_a_log",
             "dn_dt_bias", "dn_norm_w", "attn_sinks", "rel_bias", "w_attn_branch", "w_dn_branch", "w_out", "ffn_w_up", "ffn_conv_w",
             "ffn_w_down"]
    return (loss, grad_x, *[res[n][0] for n in order], *[res[n][1] for n in order], *[res[n][2] for n in order],
            *[res[n][3] for n in order])
```
